```python
import jax, jax.numpy as jnp
from jax import lax
import numpy as np

D_MODEL = 1024
BATCH = 8
SEQ = 2048
DEPTH = 2

N_META = 16
BLOCK = 128
SSD_HEADS = 8
SSD_HEAD_DIM = 64
SSD_D = SSD_HEADS * SSD_HEAD_DIM
SSD_GROUPS = 2
SSD_STATE = 64
SSD_CONV = 4
SSD_CONV_DIM = SSD_D + 2 * SSD_GROUPS * SSD_STATE
FOX_HEADS = 4
FOX_HEAD_DIM = 64
FOX_D = FOX_HEADS * FOX_HEAD_DIM
MLA_HEADS = 4
MLA_Q_LORA = 256
MLA_KV_LORA = 128
MLA_NOPE = 64
MLA_ROPE = 32
MLA_V = 64
MLA_D = MLA_HEADS * MLA_V
ROPE_THETA = 10000.0
D_MIX = SSD_D + FOX_D + MLA_D
IN_SIZES = [SSD_D, SSD_CONV_DIM, SSD_HEADS,
            FOX_D, FOX_D, FOX_D, FOX_HEADS,
            MLA_Q_LORA, MLA_KV_LORA, MLA_ROPE]
N_IN = sum(IN_SIZES)
IN_SPLITS = [int(s) for s in np.cumsum(IN_SIZES)[:-1]]
D_FF = 2816
ALPHA = (2 * DEPTH) ** 0.25
BETA = (8 * DEPTH) ** -0.25
EPS = 1e-5

kernel_name = "hybrid_ssd_fox_mla_macaron_deepnorm"


def layer_norm(x, g, b):
    xf = x.astype(jnp.float32)
    mu = jnp.mean(xf, -1, keepdims=True)
    var = jnp.mean(jnp.square(xf - mu), -1, keepdims=True)
    return ((xf - mu) * lax.rsqrt(var + EPS) * g + b).astype(x.dtype)


def rms_norm(x, g):
    xf = x.astype(jnp.float32)
    y = xf * lax.rsqrt(jnp.mean(jnp.square(xf), -1, keepdims=True) + EPS)
    return (y * g).astype(x.dtype)


def swiglu(x, w_gate, w_up, w_down):
    return (jax.nn.silu(x @ w_gate) * (x @ w_up)) @ w_down


def rope(x, cos, sin):
    x1, x2 = jnp.split(x.astype(jnp.float32), 2, axis=-1)
    return jnp.concatenate([x1 * cos - x2 * sin, x2 * cos + x1 * sin], -1).astype(x.dtype)


def block_edges(total):
    return sorted(set([0] + list(range(N_META, total, BLOCK)) + [total]))


def blocked_causal_attention(logits_fn, v):
    total = v.shape[1]
    outs = []
    edges = block_edges(total)
    for q0, q1 in zip(edges[:-1], edges[1:]):
        s = logits_fn(q0, q1).astype(jnp.float32)
        causal = jnp.arange(q1)[None, :] <= jnp.arange(q0, q1)[:, None]
        s = jnp.where(causal, s, -jnp.inf)
        p = jax.nn.softmax(s, axis=-1).astype(v.dtype)
        outs.append(jnp.einsum('bhqk,bkhd->bqhd', p, v[:, :q1]))
    return jnp.concatenate(outs, axis=1)


def causal_depthwise_conv(x, w, bias):
    out = lax.conv_general_dilated(
        x, w[:, None, :], window_strides=(1,), padding=[(SSD_CONV - 1, 0)],
        dimension_numbers=('NWC', 'WIO', 'NWC'), feature_group_count=x.shape[-1])
    return out + bias


def ssd_chunked(x, dt, A, Bm, Cm):
    b, l, h, p = x.shape
    n = Bm.shape[-1]
    nc = l // BLOCK
    x = x.reshape(b, nc, BLOCK, h, p)
    dt = dt.reshape(b, nc, BLOCK, h)
    Bm = Bm.reshape(b, nc, BLOCK, h, n)
    Cm = Cm.reshape(b, nc, BLOCK, h, n)
    a = jnp.moveaxis(dt * A, -1, 1)
    a_cum = jnp.cumsum(a, axis=-1)
    xdt = x * dt[..., None]
    idx = jnp.arange(BLOCK)
    causal = idx[:, None] >= idx[None, :]
    seg = jnp.exp(jnp.where(causal, a_cum[..., :, None] - a_cum[..., None, :], -jnp.inf))
    cb = jnp.einsum('bclhn,bcshn->bhcls', Cm, Bm)
    y_diag = jnp.einsum('bhcls,bcshp->bclhp', cb * seg, xdt)
    decay_states = jnp.exp(a_cum[..., -1:] - a_cum)
    states = jnp.einsum('bclhn,bhcl,bclhp->bchpn', Bm, decay_states, xdt)
    chunk_decay = jnp.exp(a_cum[..., -1])

    def step(s, inp):
        st, dec = inp
        return s * dec[..., None, None] + st, s

    init = jnp.zeros((b, h, p, n), x.dtype)
    _, prev = lax.scan(step, init, (jnp.moveaxis(states, 1, 0), jnp.moveaxis(chunk_decay, 2, 0)))
    prev = jnp.moveaxis(prev, 0, 1)
    y_off = jnp.einsum('bclhn,bchpn,bhcl->bclhp', Cm, prev, jnp.exp(a_cum))
    return (y_diag + y_off).reshape(b, l, h, p)


def ssd_mixer(z, xbc, dt_raw, conv_w, conv_b, dt_bias, a_log, d_skip, norm_g):
    b, L, _ = xbc.shape
    f32 = jnp.float32
    xbc = jax.nn.silu(causal_depthwise_conv(xbc, conv_w, conv_b)).astype(f32)
    xs, Bm, Cm = jnp.split(xbc, [SSD_D, SSD_D + SSD_GROUPS * SSD_STATE], axis=-1)
    xs = xs.reshape(b, L, SSD_HEADS, SSD_HEAD_DIM)
    rep = SSD_HEADS // SSD_GROUPS
    Bm = jnp.repeat(Bm.reshape(b, L, SSD_GROUPS, SSD_STATE), rep, axis=2)
    Cm = jnp.repeat(Cm.reshape(b, L, SSD_GROUPS, SSD_STATE), rep, axis=2)
    dt = jax.nn.softplus(dt_raw.astype(f32) + dt_bias.astype(f32))
    A = -jnp.exp(a_log.astype(f32))
    pad = (-L) % BLOCK
    padf = lambda t: jnp.pad(t, ((0, 0), (pad, 0)) + ((0, 0),) * (t.ndim - 2))
    y = ssd_chunked(padf(xs), padf(dt), A, padf(Bm), padf(Cm))[:, pad:]
    y = y + d_skip.astype(f32)[:, None] * xs
    y = y.reshape(b, L, SSD_D) * jax.nn.silu(z.astype(f32))
    y = rms_norm(y.reshape(b, L, SSD_GROUPS, SSD_D // SSD_GROUPS), 1.0).reshape(b, L, SSD_D) * norm_g
    return y.astype(z.dtype)


def fox_mixer(q, k, v, f_raw, f_b):
    b, L, _ = q.shape
    q = q.reshape(b, L, FOX_HEADS, FOX_HEAD_DIM)
    k = k.reshape(b, L, FOX_HEADS, FOX_HEAD_DIM)
    v = v.reshape(b, L, FOX_HEADS, FOX_HEAD_DIM)
    log_f = jax.nn.log_sigmoid(f_raw.astype(jnp.float32) + f_b.astype(jnp.float32))
    c = jnp.cumsum(log_f, axis=1).transpose(0, 2, 1)
    scale = FOX_HEAD_DIM ** -0.5

    def logits(q0, q1):
        s = jnp.einsum('bqhd,bkhd->bhqk', q[:, q0:q1], k[:, :q1]).astype(jnp.float32) * scale
        return s + (c[:, :, q0:q1, None] - c[:, :, None, :q1])

    return blocked_causal_attention(logits, v).reshape(b, L, FOX_D)


def mla_mixer(cq, ckv, k_rope, q_norm_g, w_uq, kv_norm_g, w_ukv, cos, sin):
    b, L, _ = cq.shape
    qh = (rms_norm(cq, q_norm_g) @ w_uq).reshape(b, L, MLA_HEADS, MLA_NOPE + MLA_ROPE)
    q_nope, q_rope = qh[..., :MLA_NOPE], qh[..., MLA_NOPE:]
    q_rope = rope(q_rope, cos[None, :, None, :], sin[None, :, None, :])
    kv = (rms_norm(ckv, kv_norm_g) @ w_ukv).reshape(b, L, MLA_HEADS, MLA_NOPE + MLA_V)
    k_nope, v = kv[..., :MLA_NOPE], kv[..., MLA_NOPE:]
    k_rope = rope(k_rope, cos[None], sin[None])
    scale = (MLA_NOPE + MLA_ROPE) ** -0.5

    def logits(q0, q1):
        s = jnp.einsum('bqhd,bkhd->bhqk', q_nope[:, q0:q1], k_nope[:, :q1])
        s = s + jnp.einsum('bqhr,bkr->bhqk', q_rope[:, q0:q1], k_rope[:, :q1])
        return s * scale

    return blocked_causal_attention(logits, v).reshape(b, L, MLA_D)


def _fwd_setup_inputs(seed: int = 0) -> dict:
    key = jax.random.key(seed)
    ks = iter(jax.random.split(key, 48))
    f32 = jnp.float32
    Dm, F, NL = D_MODEL, D_FF, DEPTH

    def nrm(shape, scale):
        return jax.random.normal(next(ks), shape, f32) * scale

    def gain(shape):
        return 1.0 + nrm(shape, 0.02)

    u = jax.random.uniform(next(ks), (NL, SSD_HEADS), f32)
    dt0 = jnp.exp(u * (np.log(0.1) - np.log(0.001)) + np.log(0.001))
    dt_bias = dt0 + jnp.log(-jnp.expm1(-dt0))
    a_log = jnp.log(jax.random.uniform(next(ks), (NL, SSD_HEADS), f32, 1.0, 16.0))
    return {
        "x": nrm((BATCH, SEQ, Dm), 1.0),
        "meta": nrm((N_META, Dm), 1.0),
        "ffn1_w_gate": nrm((NL, Dm, F), Dm ** -0.5),
        "ffn1_w_up": nrm((NL, Dm, F), Dm ** -0.5),
        "ffn1_w_down": nrm((NL, F, Dm), F ** -0.5 * BETA),
        "ln1_g": gain((NL, Dm)),
        "ln1_b": nrm((NL, Dm), 0.02),
        "w_in": nrm((NL, Dm, N_IN), Dm ** -0.5),
        "conv_w": nrm((NL, SSD_CONV, SSD_CONV_DIM), SSD_CONV ** -0.5),
        "conv_b": nrm((NL, SSD_CONV_DIM), 0.02),
        "dt_bias": dt_bias,
        "a_log": a_log,
        "d_skip": gain((NL, SSD_HEADS)),
        "ssd_norm_g": gain((NL, SSD_D)),
        "fox_f_b": 3.0 + nrm((NL, FOX_HEADS), 0.5),
        "mla_q_norm_g": gain((NL, MLA_Q_LORA)),
        "mla_w_uq": nrm((NL, MLA_Q_LORA, MLA_HEADS * (MLA_NOPE + MLA_ROPE)), MLA_Q_LORA ** -0.5),
        "mla_kv_norm_g": gain((NL, MLA_KV_LORA)),
        "mla_w_ukv": nrm((NL, MLA_KV_LORA, MLA_HEADS * (MLA_NOPE + MLA_V)), MLA_KV_LORA ** -0.5),
        "w_out": nrm((NL, D_MIX, Dm), D_MIX ** -0.5 * BETA),
        "ln2_g": gain((NL, Dm)),
        "ln2_b": nrm((NL, Dm), 0.02),
        "ffn2_w_gate": nrm((NL, Dm, F), Dm ** -0.5),
        "ffn2_w_up": nrm((NL, Dm, F), Dm ** -0.5),
        "ffn2_w_down": nrm((NL, F, Dm), F ** -0.5 * BETA),
        "ln3_g": gain((NL, Dm)),
        "ln3_b": nrm((NL, Dm), 0.02),
    }


def _fwd_reference(x, meta, ffn1_w_gate, ffn1_w_up, ffn1_w_down, ln1_g, ln1_b, w_in,
              conv_w, conv_b, dt_bias, a_log, d_skip, ssd_norm_g, fox_f_b,
              mla_q_norm_g, mla_w_uq, mla_kv_norm_g, mla_w_ukv, w_out, ln2_g, ln2_b,
              ffn2_w_gate, ffn2_w_up, ffn2_w_down, ln3_g, ln3_b):
    b = x.shape[0]
    h = jnp.concatenate([jnp.broadcast_to(meta[None].astype(x.dtype), (b, N_META, D_MODEL)), x], axis=1)
    total = h.shape[1]
    pos = jnp.arange(total, dtype=jnp.float32)
    inv_freq = 1.0 / (ROPE_THETA ** (jnp.arange(0, MLA_ROPE, 2, dtype=jnp.float32) / MLA_ROPE))
    ang = pos[:, None] * inv_freq[None, :]
    cos, sin = jnp.cos(ang), jnp.sin(ang)

    for l in range(DEPTH):
        h = layer_norm(ALPHA * h + 0.5 * swiglu(h, ffn1_w_gate[l], ffn1_w_up[l], ffn1_w_down[l]),
                       ln1_g[l], ln1_b[l])
        proj = h @ w_in[l]
        (z, xbc, dt_raw, fq, fk, fv, f_raw, cq, ckv, k_rope) = jnp.split(proj, IN_SPLITS, axis=-1)
        y_ssd = ssd_mixer(z, xbc, dt_raw, conv_w[l], conv_b[l], dt_bias[l], a_log[l],
                          d_skip[l], ssd_norm_g[l])
        y_fox = fox_mixer(fq, fk, fv, f_raw, fox_f_b[l])
        y_mla = mla_mixer(cq, ckv, k_rope, mla_q_norm_g[l], mla_w_uq[l], mla_kv_norm_g[l],
                          mla_w_ukv[l], cos, sin)
        mix = jnp.concatenate([y_ssd, y_fox.astype(h.dtype), y_mla.astype(h.dtype)], axis=-1) @ w_out[l]
        h = layer_norm(ALPHA * h + mix, ln2_g[l], ln2_b[l])
        h = layer_norm(ALPHA * h + 0.5 * swiglu(h, ffn2_w_gate[l], ffn2_w_up[l], ffn2_w_down[l]),
                       ln3_g[l], ln3_b[l])
    return h[:, N_META:]


import jax as _jax
import jax.numpy as _jnp

TWIN_FORMAT = 'train_step'
FWD_PARAMS = ['x', 'meta', 'ffn1_w_gate', 'ffn1_w_up', 'ffn1_w_down', 'ln1_g', 'ln1_b', 'w_in', 'conv_w', 'conv_b', 'dt_bias', 'a_log', 'd_skip', 'ssd_norm_g', 'fox_f_b', 'mla_q_norm_g', 'mla_w_uq', 'mla_kv_norm_g', 'mla_w_ukv', 'w_out', 'ln2_g', 'ln2_b', 'ffn2_w_gate', 'ffn2_w_up', 'ffn2_w_down', 'ln3_g', 'ln3_b']
TWIN_WEIGHTS = ['meta', 'ffn1_w_gate', 'ffn1_w_up', 'ffn1_w_down', 'ln1_g', 'ln1_b', 'w_in', 'conv_w', 'conv_b', 'dt_bias', 'a_log', 'd_skip', 'ssd_norm_g', 'fox_f_b', 'mla_q_norm_g', 'mla_w_uq', 'mla_kv_norm_g', 'mla_w_ukv', 'w_out', 'ln2_g', 'ln2_b', 'ffn2_w_gate', 'ffn2_w_up', 'ffn2_w_down', 'ln3_g', 'ln3_b']
TWIN_DIFF_INPUT = 'x'
TWIN_INPUTS = ['x', 'meta', 'ffn1_w_gate', 'ffn1_w_up', 'ffn1_w_down', 'ln1_g', 'ln1_b', 'w_in', 'conv_w', 'conv_b', 'dt_bias', 'a_log', 'd_skip', 'ssd_norm_g', 'fox_f_b', 'mla_q_norm_g', 'mla_w_uq', 'mla_kv_norm_g', 'mla_w_ukv', 'w_out', 'ln2_g', 'ln2_b', 'ffn2_w_gate', 'ffn2_w_up', 'ffn2_w_down', 'ln3_g', 'ln3_b', 'loss_target', 'm_meta', 'm_ffn1_w_gate', 'm_ffn1_w_up', 'm_ffn1_w_down', 'm_ln1_g', 'm_ln1_b', 'm_w_in', 'm_conv_w', 'm_conv_b', 'm_dt_bias', 'm_a_log', 'm_d_skip', 'm_ssd_norm_g', 'm_fox_f_b', 'm_mla_q_norm_g', 'm_mla_w_uq', 'm_mla_kv_norm_g', 'm_mla_w_ukv', 'm_w_out', 'm_ln2_g', 'm_ln2_b', 'm_ffn2_w_gate', 'm_ffn2_w_up', 'm_ffn2_w_down', 'm_ln3_g', 'm_ln3_b', 'v_meta', 'v_ffn1_w_gate', 'v_ffn1_w_up', 'v_ffn1_w_down', 'v_ln1_g', 'v_ln1_b', 'v_w_in', 'v_conv_w', 'v_conv_b', 'v_dt_bias', 'v_a_log', 'v_d_skip', 'v_ssd_norm_g', 'v_fox_f_b', 'v_mla_q_norm_g', 'v_mla_w_uq', 'v_mla_kv_norm_g', 'v_mla_w_ukv', 'v_w_out', 'v_ln2_g', 'v_ln2_b', 'v_ffn2_w_gate', 'v_ffn2_w_up', 'v_ffn2_w_down', 'v_ln3_g', 'v_ln3_b']
TWIN_OUTPUTS = ['loss', 'grad_x', 'grad_meta', 'grad_ffn1_w_gate', 'grad_ffn1_w_up', 'grad_ffn1_w_down', 'grad_ln1_g', 'grad_ln1_b', 'grad_w_in', 'grad_conv_w', 'grad_conv_b', 'grad_dt_bias', 'grad_a_log', 'grad_d_skip', 'grad_ssd_norm_g', 'grad_fox_f_b', 'grad_mla_q_norm_g', 'grad_mla_w_uq', 'grad_mla_kv_norm_g', 'grad_mla_w_ukv', 'grad_w_out', 'grad_ln2_g', 'grad_ln2_b', 'grad_ffn2_w_gate', 'grad_ffn2_w_up', 'grad_ffn2_w_down', 'grad_ln3_g', 'grad_ln3_b', 'delta_meta', 'delta_ffn1_w_gate', 'delta_ffn1_w_up', 'delta_ffn1_w_down', 'delta_ln1_g', 'delta_ln1_b', 'delta_w_in', 'delta_conv_w', 'delta_conv_b', 'delta_dt_bias', 'delta_a_log', 'delta_d_skip', 'delta_ssd_norm_g', 'delta_fox_f_b', 'delta_mla_q_norm_g', 'delta_mla_w_uq', 'delta_mla_kv_norm_g', 'delta_mla_w_ukv', 'delta_w_out', 'delta_ln2_g', 'delta_ln2_b', 'delta_ffn2_w_gate', 'delta_ffn2_w_up', 'delta_ffn2_w_down', 'delta_ln3_g', 'delta_ln3_b', 'new_m_meta', 'new_m_ffn1_w_gate', 'new_m_ffn1_w_up', 'new_m_ffn1_w_down', 'new_m_ln1_g', 'new_m_ln1_b', 'new_m_w_in', 'new_m_conv_w', 'new_m_conv_b', 'new_m_dt_bias', 'new_m_a_log', 'new_m_d_skip', 'new_m_ssd_norm_g', 'new_m_fox_f_b', 'new_m_mla_q_norm_g', 'new_m_mla_w_uq', 'new_m_mla_kv_norm_g', 'new_m_mla_w_ukv', 'new_m_w_out', 'new_m_ln2_g', 'new_m_ln2_b', 'new_m_ffn2_w_gate', 'new_m_ffn2_w_up', 'new_m_ffn2_w_down', 'new_m_ln3_g', 'new_m_ln3_b', 'new_v_meta', 'new_v_ffn1_w_gate', 'new_v_ffn1_w_up', 'new_v_ffn1_w_down', 'new_v_ln1_g', 'new_v_ln1_b', 'new_v_w_in', 'new_v_conv_w', 'new_v_conv_b', 'new_v_dt_bias', 'new_v_a_log', 'new_v_d_skip', 'new_v_ssd_norm_g', 'new_v_fox_f_b', 'new_v_mla_q_norm_g', 'new_v_mla_w_uq', 'new_v_mla_kv_norm_g', 'new_v_mla_w_ukv', 'new_v_w_out', 'new_v_ln2_g', 'new_v_ln2_b', 'new_v_ffn2_w_gate', 'new_v_ffn2_w_up', 'new_v_ffn2_w_down', 'new_v_ln3_g', 'new_v_ln3_b']
TWIN_LEAF_KINDS = {'loss': 'loss', 'grad_x': 'grad_x', 'grad_meta': 'grad_w', 'grad_ffn1_w_gate': 'grad_w', 'grad_ffn1_w_up': 'grad_w', 'grad_ffn1_w_down': 'grad_w', 'grad_ln1_g': 'grad_w', 'grad_ln1_b': 'grad_w', 'grad_w_in': 'grad_w', 'grad_conv_w': 'grad_w', 'grad_conv_b': 'grad_w', 'grad_dt_bias': 'grad_w', 'grad_a_log': 'grad_w', 'grad_d_skip': 'grad_w', 'grad_ssd_norm_g': 'grad_w', 'grad_fox_f_b': 'grad_w', 'grad_mla_q_norm_g': 'grad_w', 'grad_mla_w_uq': 'grad_w', 'grad_mla_kv_norm_g': 'grad_w', 'grad_mla_w_ukv': 'grad_w', 'grad_w_out': 'grad_w', 'grad_ln2_g': 'grad_w', 'grad_ln2_b': 'grad_w', 'grad_ffn2_w_gate': 'grad_w', 'grad_ffn2_w_up': 'grad_w', 'grad_ffn2_w_down': 'grad_w', 'grad_ln3_g': 'grad_w', 'grad_ln3_b': 'grad_w', 'delta_meta': 'delta_w', 'delta_ffn1_w_gate': 'delta_w', 'delta_ffn1_w_up': 'delta_w', 'delta_ffn1_w_down': 'delta_w', 'delta_ln1_g': 'delta_w', 'delta_ln1_b': 'delta_w', 'delta_w_in': 'delta_w', 'delta_conv_w': 'delta_w', 'delta_conv_b': 'delta_w', 'delta_dt_bias': 'delta_w', 'delta_a_log': 'delta_w', 'delta_d_skip': 'delta_w', 'delta_ssd_norm_g': 'delta_w', 'delta_fox_f_b': 'delta_w', 'delta_mla_q_norm_g': 'delta_w', 'delta_mla_w_uq': 'delta_w', 'delta_mla_kv_norm_g': 'delta_w', 'delta_mla_w_ukv': 'delta_w', 'delta_w_out': 'delta_w', 'delta_ln2_g': 'delta_w', 'delta_ln2_b': 'delta_w', 'delta_ffn2_w_gate': 'delta_w', 'delta_ffn2_w_up': 'delta_w', 'delta_ffn2_w_down': 'delta_w', 'delta_ln3_g': 'delta_w', 'delta_ln3_b': 'delta_w', 'new_m_meta': 'new_m', 'new_m_ffn1_w_gate': 'new_m', 'new_m_ffn1_w_up': 'new_m', 'new_m_ffn1_w_down': 'new_m', 'new_m_ln1_g': 'new_m', 'new_m_ln1_b': 'new_m', 'new_m_w_in': 'new_m', 'new_m_conv_w': 'new_m', 'new_m_conv_b': 'new_m', 'new_m_dt_bias': 'new_m', 'new_m_a_log': 'new_m', 'new_m_d_skip': 'new_m', 'new_m_ssd_norm_g': 'new_m', 'new_m_fox_f_b': 'new_m', 'new_m_mla_q_norm_g': 'new_m', 'new_m_mla_w_uq': 'new_m', 'new_m_mla_kv_norm_g': 'new_m', 'new_m_mla_w_ukv': 'new_m', 'new_m_w_out': 'new_m', 'new_m_ln2_g': 'new_m', 'new_m_ln2_b': 'new_m', 'new_m_ffn2_w_gate': 'new_m', 'new_m_ffn2_w_up': 'new_m', 'new_m_ffn2_w_down': 'new_m', 'new_m_ln3_g': 'new_m', 'new_m_ln3_b': 'new_m', 'new_v_meta': 'new_v', 'new_v_ffn1_w_gate': 'new_v', 'new_v_ffn1_w_up': 'new_v', 'new_v_ffn1_w_down': 'new_v', 'new_v_ln1_g': 'new_v', 'new_v_ln1_b': 'new_v', 'new_v_w_in': 'new_v', 'new_v_conv_w': 'new_v', 'new_v_conv_b': 'new_v', 'new_v_dt_bias': 'new_v', 'new_v_a_log': 'new_v', 'new_v_d_skip': 'new_v', 'new_v_ssd_norm_g': 'new_v', 'new_v_fox_f_b': 'new_v', 'new_v_mla_q_norm_g': 'new_v', 'new_v_mla_w_uq': 'new_v', 'new_v_mla_kv_norm_g': 'new_v', 'new_v_mla_w_ukv': 'new_v', 'new_v_w_out': 'new_v', 'new_v_ln2_g': 'new_v', 'new_v_ln2_b': 'new_v', 'new_v_ffn2_w_gate': 'new_v', 'new_v_ffn2_w_up': 'new_v', 'new_v_ffn2_w_down': 'new_v', 'new_v_ln3_g': 'new_v', 'new_v_ln3_b': 'new_v'}


def _forward(args):
    return _fwd_reference(*[args[k] for k in FWD_PARAMS])


def _output_shape():
    out = _jax.eval_shape(lambda: _forward(_fwd_setup_inputs(0)))
    return out.shape, out.dtype

N_MICROBATCH = 1
ADAM_LR = 0.001
ADAM_B1 = 0.9
ADAM_B2 = 0.999
ADAM_EPS = 1e-08
ADAM_WD = 0.01
ADAM_STEP = 10
PER_EXAMPLE_BATCH_AXIS = {'x': 0, 'loss_target': 0}
SHARED_INPUTS = []
_WEIGHT_DTYPES = {'meta': _jnp.float32, 'ffn1_w_gate': _jnp.float32, 'ffn1_w_up': _jnp.float32, 'ffn1_w_down': _jnp.float32, 'ln1_g': _jnp.float32, 'ln1_b': _jnp.float32, 'w_in': _jnp.float32, 'conv_w': _jnp.float32, 'conv_b': _jnp.float32, 'dt_bias': _jnp.float32, 'a_log': _jnp.float32, 'd_skip': _jnp.float32, 'ssd_norm_g': _jnp.float32, 'fox_f_b': _jnp.float32, 'mla_q_norm_g': _jnp.float32, 'mla_w_uq': _jnp.float32, 'mla_kv_norm_g': _jnp.float32, 'mla_w_ukv': _jnp.float32, 'w_out': _jnp.float32, 'ln2_g': _jnp.float32, 'ln2_b': _jnp.float32, 'ffn2_w_gate': _jnp.float32, 'ffn2_w_up': _jnp.float32, 'ffn2_w_down': _jnp.float32, 'ln3_g': _jnp.float32, 'ln3_b': _jnp.float32}
MOMENT_SCALE = {'meta': 1.723207e-03, 'ffn1_w_gate': 8.769811e-03, 'ffn1_w_up': 8.504550e-03, 'ffn1_w_down': 2.819349e-02, 'ln1_g': 5.115115e-01, 'ln1_b': 2.709041e-01, 'w_in': 3.174304e-02, 'conv_w': 3.921304e-02, 'conv_b': 5.492348e-02, 'dt_bias': 1.557058e-01, 'a_log': 1.598184e-01, 'd_skip': 2.677369e-01, 'ssd_norm_g': 4.543175e-02, 'fox_f_b': 6.549377e-02, 'mla_q_norm_g': 7.358380e-03, 'mla_w_uq': 6.011945e-03, 'mla_kv_norm_g': 1.739183e-02, 'mla_w_ukv': 7.616986e-03, 'w_out': 6.679409e-02, 'ln2_g': 5.526728e-01, 'ln2_b': 2.688332e-01, 'ffn2_w_gate': 8.396648e-03, 'ffn2_w_up': 8.167986e-03, 'ffn2_w_down': 2.702948e-02, 'ln3_g': 1.133461e+01, 'ln3_b': 8.461342e-01}


def _to_microbatches(a, axis):
    t = _jnp.moveaxis(a, axis, 0)
    t = t.reshape((N_MICROBATCH, t.shape[0] // N_MICROBATCH) + t.shape[1:])
    return _jnp.moveaxis(t, 1, axis + 1)


def setup_inputs(seed: int = 0) -> dict:
    inp = _fwd_setup_inputs(seed)
    key = _jax.random.fold_in(_jax.random.key(seed), 7919)
    shape, _ = _output_shape()
    out = dict(inp)
    out["loss_target"] = _jax.random.normal(_jax.random.fold_in(key, 0), shape, _jnp.float32)
    for i, name in enumerate(TWIN_WEIGHTS):
        w = inp[name].astype(_jnp.float32)
        if MOMENT_SCALE is None:
            s = _jnp.sqrt(_jnp.mean(_jnp.square(w)) + 1e-30)
        else:
            s = MOMENT_SCALE[name]
        km, kv = _jax.random.split(_jax.random.fold_in(key, i + 1))
        out[name] = w
        out["m_" + name] = s * _jax.random.normal(km, w.shape, _jnp.float32)
        out["v_" + name] = (s * s) * _jax.random.uniform(kv, w.shape, _jnp.float32, 0.5, 1.5)
    if N_MICROBATCH > 1:
        for name, axis in PER_EXAMPLE_BATCH_AXIS.items():
            out[name] = _to_microbatches(out[name], axis)
    return {'x': out['x'], 'meta': out['meta'], 'ffn1_w_gate': out['ffn1_w_gate'], 'ffn1_w_up': out['ffn1_w_up'], 'ffn1_w_down': out['ffn1_w_down'], 'ln1_g': out['ln1_g'], 'ln1_b': out['ln1_b'], 'w_in': out['w_in'], 'conv_w': out['conv_w'], 'conv_b': out['conv_b'], 'dt_bias': out['dt_bias'], 'a_log': out['a_log'], 'd_skip': out['d_skip'], 'ssd_norm_g': out['ssd_norm_g'], 'fox_f_b': out['fox_f_b'], 'mla_q_norm_g': out['mla_q_norm_g'], 'mla_w_uq': out['mla_w_uq'], 'mla_kv_norm_g': out['mla_kv_norm_g'], 'mla_w_ukv': out['mla_w_ukv'], 'w_out': out['w_out'], 'ln2_g': out['ln2_g'], 'ln2_b': out['ln2_b'], 'ffn2_w_gate': out['ffn2_w_gate'], 'ffn2_w_up': out['ffn2_w_up'], 'ffn2_w_down': out['ffn2_w_down'], 'ln3_g': out['ln3_g'], 'ln3_b': out['ln3_b'], 'loss_target': out['loss_target'], 'm_meta': out['m_meta'], 'm_ffn1_w_gate': out['m_ffn1_w_gate'], 'm_ffn1_w_up': out['m_ffn1_w_up'], 'm_ffn1_w_down': out['m_ffn1_w_down'], 'm_ln1_g': out['m_ln1_g'], 'm_ln1_b': out['m_ln1_b'], 'm_w_in': out['m_w_in'], 'm_conv_w': out['m_conv_w'], 'm_conv_b': out['m_conv_b'], 'm_dt_bias': out['m_dt_bias'], 'm_a_log': out['m_a_log'], 'm_d_skip': out['m_d_skip'], 'm_ssd_norm_g': out['m_ssd_norm_g'], 'm_fox_f_b': out['m_fox_f_b'], 'm_mla_q_norm_g': out['m_mla_q_norm_g'], 'm_mla_w_uq': out['m_mla_w_uq'], 'm_mla_kv_norm_g': out['m_mla_kv_norm_g'], 'm_mla_w_ukv': out['m_mla_w_ukv'], 'm_w_out': out['m_w_out'], 'm_ln2_g': out['m_ln2_g'], 'm_ln2_b': out['m_ln2_b'], 'm_ffn2_w_gate': out['m_ffn2_w_gate'], 'm_ffn2_w_up': out['m_ffn2_w_up'], 'm_ffn2_w_down': out['m_ffn2_w_down'], 'm_ln3_g': out['m_ln3_g'], 'm_ln3_b': out['m_ln3_b'], 'v_meta': out['v_meta'], 'v_ffn1_w_gate': out['v_ffn1_w_gate'], 'v_ffn1_w_up': out['v_ffn1_w_up'], 'v_ffn1_w_down': out['v_ffn1_w_down'], 'v_ln1_g': out['v_ln1_g'], 'v_ln1_b': out['v_ln1_b'], 'v_w_in': out['v_w_in'], 'v_conv_w': out['v_conv_w'], 'v_conv_b': out['v_conv_b'], 'v_dt_bias': out['v_dt_bias'], 'v_a_log': out['v_a_log'], 'v_d_skip': out['v_d_skip'], 'v_ssd_norm_g': out['v_ssd_norm_g'], 'v_fox_f_b': out['v_fox_f_b'], 'v_mla_q_norm_g': out['v_mla_q_norm_g'], 'v_mla_w_uq': out['v_mla_w_uq'], 'v_mla_kv_norm_g': out['v_mla_kv_norm_g'], 'v_mla_w_ukv': out['v_mla_w_ukv'], 'v_w_out': out['v_w_out'], 'v_ln2_g': out['v_ln2_g'], 'v_ln2_b': out['v_ln2_b'], 'v_ffn2_w_gate': out['v_ffn2_w_gate'], 'v_ffn2_w_up': out['v_ffn2_w_up'], 'v_ffn2_w_down': out['v_ffn2_w_down'], 'v_ln3_g': out['v_ln3_g'], 'v_ln3_b': out['v_ln3_b']}


def _loss(weights, diff, rest, loss_target):
    with _jax.named_scope("forward"):
        args = {**rest, TWIN_DIFF_INPUT: diff, **{k: w.astype(_WEIGHT_DTYPES[k]) for k, w in weights.items()}}
        y = _forward(args)
    with _jax.named_scope("loss_head"):
        err = _jnp.square(y.astype(_jnp.float32) - loss_target)
        return 0.5 * _jnp.sum(_jnp.mean(err, axis=-1)) if err.ndim else 0.5 * err


def _adamw(w, g, m, v):
    m = ADAM_B1 * m + (1.0 - ADAM_B1) * g
    v = ADAM_B2 * v + (1.0 - ADAM_B2) * _jnp.square(g)
    m_hat = m / (1.0 - ADAM_B1 ** ADAM_STEP)
    v_hat = v / (1.0 - ADAM_B2 ** ADAM_STEP)
    delta = -ADAM_LR * (m_hat / (_jnp.sqrt(v_hat) + ADAM_EPS) + ADAM_WD * w)
    return delta, m, v


def reference(x, meta, ffn1_w_gate, ffn1_w_up, ffn1_w_down, ln1_g, ln1_b, w_in, conv_w, conv_b, dt_bias, a_log, d_skip, ssd_norm_g, fox_f_b, mla_q_norm_g, mla_w_uq, mla_kv_norm_g, mla_w_ukv, w_out, ln2_g, ln2_b, ffn2_w_gate, ffn2_w_up, ffn2_w_down, ln3_g, ln3_b, loss_target, m_meta, m_ffn1_w_gate, m_ffn1_w_up, m_ffn1_w_down, m_ln1_g, m_ln1_b, m_w_in, m_conv_w, m_conv_b, m_dt_bias, m_a_log, m_d_skip, m_ssd_norm_g, m_fox_f_b, m_mla_q_norm_g, m_mla_w_uq, m_mla_kv_norm_g, m_mla_w_ukv, m_w_out, m_ln2_g, m_ln2_b, m_ffn2_w_gate, m_ffn2_w_up, m_ffn2_w_down, m_ln3_g, m_ln3_b, v_meta, v_ffn1_w_gate, v_ffn1_w_up, v_ffn1_w_down, v_ln1_g, v_ln1_b, v_w_in, v_conv_w, v_conv_b, v_dt_bias, v_a_log, v_d_skip, v_ssd_norm_g, v_fox_f_b, v_mla_q_norm_g, v_mla_w_uq, v_mla_kv_norm_g, v_mla_w_ukv, v_w_out, v_ln2_g, v_ln2_b, v_ffn2_w_gate, v_ffn2_w_up, v_ffn2_w_down, v_ln3_g, v_ln3_b):
    given = dict(x=x, meta=meta, ffn1_w_gate=ffn1_w_gate, ffn1_w_up=ffn1_w_up, ffn1_w_down=ffn1_w_down, ln1_g=ln1_g, ln1_b=ln1_b, w_in=w_in, conv_w=conv_w, conv_b=conv_b, dt_bias=dt_bias, a_log=a_log, d_skip=d_skip, ssd_norm_g=ssd_norm_g, fox_f_b=fox_f_b, mla_q_norm_g=mla_q_norm_g, mla_w_uq=mla_w_uq, mla_kv_norm_g=mla_kv_norm_g, mla_w_ukv=mla_w_ukv, w_out=w_out, ln2_g=ln2_g, ln2_b=ln2_b, ffn2_w_gate=ffn2_w_gate, ffn2_w_up=ffn2_w_up, ffn2_w_down=ffn2_w_down, ln3_g=ln3_g, ln3_b=ln3_b, loss_target=loss_target, m_meta=m_meta, m_ffn1_w_gate=m_ffn1_w_gate, m_ffn1_w_up=m_ffn1_w_up, m_ffn1_w_down=m_ffn1_w_down, m_ln1_g=m_ln1_g, m_ln1_b=m_ln1_b, m_w_in=m_w_in, m_conv_w=m_conv_w, m_conv_b=m_conv_b, m_dt_bias=m_dt_bias, m_a_log=m_a_log, m_d_skip=m_d_skip, m_ssd_norm_g=m_ssd_norm_g, m_fox_f_b=m_fox_f_b, m_mla_q_norm_g=m_mla_q_norm_g, m_mla_w_uq=m_mla_w_uq, m_mla_kv_norm_g=m_mla_kv_norm_g, m_mla_w_ukv=m_mla_w_ukv, m_w_out=m_w_out, m_ln2_g=m_ln2_g, m_ln2_b=m_ln2_b, m_ffn2_w_gate=m_ffn2_w_gate, m_ffn2_w_up=m_ffn2_w_up, m_ffn2_w_down=m_ffn2_w_down, m_ln3_g=m_ln3_g, m_ln3_b=m_ln3_b, v_meta=v_meta, v_ffn1_w_gate=v_ffn1_w_gate, v_ffn1_w_up=v_ffn1_w_up, v_ffn1_w_down=v_ffn1_w_down, v_ln1_g=v_ln1_g, v_ln1_b=v_ln1_b, v_w_in=v_w_in, v_conv_w=v_conv_w, v_conv_b=v_conv_b, v_dt_bias=v_dt_bias, v_a_log=v_a_log, v_d_skip=v_d_skip, v_ssd_norm_g=v_ssd_norm_g, v_fox_f_b=v_fox_f_b, v_mla_q_norm_g=v_mla_q_norm_g, v_mla_w_uq=v_mla_w_uq, v_mla_kv_norm_g=v_mla_kv_norm_g, v_mla_w_ukv=v_mla_w_ukv, v_w_out=v_w_out, v_ln2_g=v_ln2_g, v_ln2_b=v_ln2_b, v_ffn2_w_gate=v_ffn2_w_gate, v_ffn2_w_up=v_ffn2_w_up, v_ffn2_w_down=v_ffn2_w_down, v_ln3_g=v_ln3_g, v_ln3_b=v_ln3_b)
    weights = {n: given[n] for n in TWIN_WEIGHTS}
    shared = {n: given[n] for n in SHARED_INPUTS}
    per_example = {n: given[n] for n in ['x']}
    grad_fn = _jax.value_and_grad(_loss, argnums=(0, 1))

    def one_microbatch(ex, loss_target):
        ex = dict(ex)
        diff = ex.pop(TWIN_DIFF_INPUT)
        return grad_fn(weights, diff, {**shared, **ex}, loss_target)

    if N_MICROBATCH == 1:
        loss, (grad_w, grad_x) = one_microbatch(per_example, given["loss_target"])
    else:
        def body(carry, xs):
            loss_sum, grad_sum = carry
            l_k, (gw_k, gx_k) = one_microbatch(xs[0], xs[1])
            with _jax.named_scope("update"):
                return (loss_sum + l_k, _jax.tree.map(_jnp.add, grad_sum, gw_k)), gx_k

        init = (_jnp.zeros((), _jnp.float32), _jax.tree.map(_jnp.zeros_like, weights))
        (loss, grad_w), grad_x = _jax.lax.scan(body, init, (per_example, given["loss_target"]))
    with _jax.named_scope("update"):
        delta_w, new_m, new_v = {}, {}, {}
        for n in TWIN_WEIGHTS:
            delta_w[n], new_m[n], new_v[n] = _adamw(weights[n], grad_w[n], given["m_" + n], given["v_" + n])
    return (loss, grad_x, *[grad_w[n] for n in TWIN_WEIGHTS], *[delta_w[n] for n in TWIN_WEIGHTS],
            *[new_m[n] for n in TWIN_WEIGHTS], *[new_v[n] for n in TWIN_WEIGHTS])
```

```python
import functools

import jax
import jax.numpy as jnp
from jax import lax
from jax.experimental import pallas as pl
from jax.experimental.pallas import tpu as pltpu

F32, BF16 = jnp.float32, jnp.bfloat16
HI = lax.Precision.HIGHEST

N_DEV = 8
D = 1024
NL = 2
N_META = 16
BLK = 128
PAD = BLK - N_META
D_FF = 2816
HS = D_FF // N_DEV
SSD_H, SSD_P, SSD_N, SSD_G = 8, 64, 64, 2
SSD_D = SSD_H * SSD_P
CONV_K = 4
CONV_D = SSD_D + 2 * SSD_G * SSD_N
FOX_H, FOX_DH = 4, 64
MLA_H, MLA_QL, MLA_KVL, MLA_NOPE, MLA_ROPE, MLA_V = 4, 256, 128, 64, 32, 64
N_IN = 2476
C_Z, C_XBC, C_FQ, C_FK, C_FV, C_CQ, C_CKV, C_SM, N_INP = 0, 512, 1280, 1536, 1792, 2048, 2304, 2432, 2560
SM_DT, SM_F, SM_KR = 0, 8, 64
ALPHA = (2 * NL) ** 0.25
EPS = 1e-5
NEG = -1e30
LR, B1, B2, AEPS, WD, STEP = 0.001, 0.9, 0.999, 1e-08, 0.01, 10
VMEM_MB = 56


def _cp(*sem):
    return pltpu.CompilerParams(dimension_semantics=sem, vmem_limit_bytes=VMEM_MB << 20)


def _nn(a, b):
    return lax.dot_general(a, b, (((1,), (0,)), ((), ())), preferred_element_type=F32)


def _nt(a, b):
    return lax.dot_general(a, b, (((1,), (1,)), ((), ())), preferred_element_type=F32)


def _tn(a, b):
    return lax.dot_general(a, b, (((0,), (0,)), ((), ())), preferred_element_type=F32)


def _nn_hi(a, b):
    return lax.dot_general(a, b, (((1,), (0,)), ((), ())), precision=HI, preferred_element_type=F32)


def _row_tile(t):
    for d in range(640, 15, -16):
        if t % d == 0:
            return d
    raise ValueError(t)


def _sig(x):
    return 1.0 / (1.0 + jnp.exp(-x))


def _tri(lower=True):
    r = lax.broadcasted_iota(jnp.int32, (BLK, BLK), 0)
    c = lax.broadcasted_iota(jnp.int32, (BLK, BLK), 1)
    return (r >= c) if lower else (r <= c)


def build_h0(meta_full, x):
    s = x.shape[0]
    nb = s // BLK + 1

    def body(m_ref, x_ref, h_ref, hb_ref):
        i = pl.program_id(0)

        @pl.when(i == 0)
        def _():
            h = jnp.concatenate([jnp.zeros((PAD, D), F32), m_ref[...]], axis=0)
            h_ref[...] = h
            hb_ref[...] = h.astype(BF16)

        @pl.when(i > 0)
        def _():
            h_ref[...] = x_ref[...]
            hb_ref[...] = x_ref[...].astype(BF16)

    return pl.pallas_call(
        body, name="build_h0", grid=(nb,),
        in_specs=[pl.BlockSpec((N_META, D), lambda i: (0, 0)),
                  pl.BlockSpec((BLK, D), lambda i: (jnp.maximum(i - 1, 0), 0))],
        out_specs=[pl.BlockSpec((BLK, D), lambda i: (i, 0))] * 2,
        out_shape=[jax.ShapeDtypeStruct((nb * BLK, D), F32), jax.ShapeDtypeStruct((nb * BLK, D), BF16)],
        compiler_params=_cp("arbitrary"),
    )(meta_full, x)


def ffn_up(hb, wg, wu):
    t = hb.shape[0]
    g, _, hs = wg.shape
    tm = _row_tile(t)

    def body(h_ref, wg_ref, wu_ref, u_ref, v_ref, a_ref):
        h = h_ref[...]
        u = _nn(h, wg_ref[...])
        v = _nn(h, wu_ref[...])
        u_ref[...] = u.astype(BF16)
        v_ref[...] = v.astype(BF16)
        a_ref[...] = (u * _sig(u) * v).astype(BF16)

    w_spec = pl.BlockSpec((None, D, hs), lambda gi, i: (gi, 0, 0))
    o_spec = pl.BlockSpec((None, tm, hs), lambda gi, i: (gi, i, 0))
    return pl.pallas_call(
        body, name="ffn_up", grid=(g, t // tm),
        in_specs=[pl.BlockSpec((tm, D), lambda gi, i: (i, 0)), w_spec, w_spec],
        out_specs=[o_spec] * 3,
        out_shape=[jax.ShapeDtypeStruct((g, t, hs), BF16)] * 3,
        compiler_params=_cp("arbitrary", "arbitrary"),
    )(hb, wg, wu)


def mm_res_ln(a, b, res, alpha, scale, gamma, beta):
    g, t, k = a.shape
    tm = _row_tile(t)

    def body(a_ref, b_ref, res_ref, g_ref, be_ref, r_ref, y_ref, yb_ref, acc):
        gi = pl.program_id(1)

        @pl.when(gi == 0)
        def _():
            acc[...] = jnp.zeros_like(acc)

        acc[...] += _nn(a_ref[...], b_ref[...])

        @pl.when(gi == g - 1)
        def _():
            r = alpha * res_ref[...] + scale * acc[...]
            mu = jnp.mean(r, axis=1, keepdims=True)
            xc = r - mu
            var = jnp.mean(xc * xc, axis=1, keepdims=True)
            y = xc * lax.rsqrt(var + EPS) * g_ref[...] + be_ref[...]
            r_ref[...] = r
            y_ref[...] = y
            yb_ref[...] = y.astype(BF16)

    row = pl.BlockSpec((tm, D), lambda i, gi: (i, 0))
    vec = pl.BlockSpec((1, D), lambda i, gi: (0, 0))
    return pl.pallas_call(
        body, name="mm_res_ln", grid=(t // tm, g),
        in_specs=[pl.BlockSpec((None, tm, k), lambda i, gi: (gi, i, 0)),
                  pl.BlockSpec((None, k, D), lambda i, gi: (gi, 0, 0)), row, vec, vec],
        out_specs=[row] * 3,
        out_shape=[jax.ShapeDtypeStruct((t, D), F32), jax.ShapeDtypeStruct((t, D), F32),
                   jax.ShapeDtypeStruct((t, D), BF16)],
        scratch_shapes=[pltpu.VMEM((tm, D), F32)],
        compiler_params=_cp("arbitrary", "arbitrary"),
    )(a, b, res, gamma, beta)


def mm_nn(a, b, tn=512):
    t, k = a.shape
    n = b.shape[1]
    tm = _row_tile(t)

    def body(a_ref, b_ref, o_ref):
        o_ref[...] = _nn(a_ref[...], b_ref[...])

    return pl.pallas_call(
        body, name="mm_nn", grid=(n // tn, t // tm),
        in_specs=[pl.BlockSpec((tm, k), lambda j, i: (i, 0)), pl.BlockSpec((k, tn), lambda j, i: (0, j))],
        out_specs=pl.BlockSpec((tm, tn), lambda j, i: (i, j)),
        out_shape=jax.ShapeDtypeStruct((t, n), F32),
        compiler_params=_cp("arbitrary", "arbitrary"),
    )(a, b)


def ffn_dact(dfb, wd, u, v):
    g, t, hs = u.shape
    tm = _row_tile(t)

    def body(df_ref, wd_ref, u_ref, v_ref, du_ref, dv_ref):
        da = _nt(df_ref[...], wd_ref[...])
        uu = u_ref[...].astype(F32)
        sg = _sig(uu)
        du_ref[...] = (da * v_ref[...].astype(F32) * (sg * (1.0 + uu * (1.0 - sg)))).astype(BF16)
        dv_ref[...] = (da * uu * sg).astype(BF16)

    act = pl.BlockSpec((None, tm, hs), lambda gi, i: (gi, i, 0))
    return pl.pallas_call(
        body, name="ffn_dact", grid=(g, t // tm),
        in_specs=[pl.BlockSpec((tm, D), lambda gi, i: (i, 0)),
                  pl.BlockSpec((None, hs, D), lambda gi, i: (gi, 0, 0)), act, act],
        out_specs=[act] * 2,
        out_shape=[jax.ShapeDtypeStruct((g, t, hs), BF16)] * 2,
        compiler_params=_cp("arbitrary", "arbitrary"),
    )(dfb, wd, u, v)


def mm_nt_reduce(pairs, n):
    g, t, _ = pairs[0][0].shape
    tm = _row_tile(t)
    npair = len(pairs)

    def body(*refs):
        o_ref = refs[-1]
        gi = pl.program_id(1)
        tot = _nt(refs[0][...], refs[1][...])
        for p in range(1, npair):
            tot += _nt(refs[2 * p][...], refs[2 * p + 1][...])

        @pl.when(gi == 0)
        def _():
            o_ref[...] = tot

        @pl.when(gi > 0)
        def _():
            o_ref[...] += tot

    in_specs, args = [], []
    for x, w in pairs:
        k = x.shape[2]
        in_specs += [pl.BlockSpec((None, tm, k), lambda i, gi: (gi, i, 0)),
                     pl.BlockSpec((None, n, k), lambda i, gi: (gi, 0, 0))]
        args += [x, w]
    return pl.pallas_call(
        body, name="mm_nt_reduce", grid=(t // tm, g),
        in_specs=in_specs, out_specs=pl.BlockSpec((tm, n), lambda i, gi: (i, 0)),
        out_shape=jax.ShapeDtypeStruct((t, n), F32),
        compiler_params=_cp("arbitrary", "arbitrary"),
    )(*args)


def mm_tn(x, y, out_dtype=BF16):
    gx, t, k = x.shape
    gy, _, n = y.shape
    g = max(gx, gy)
    tm = _row_tile(t)
    nt = t // tm

    def body(x_ref, y_ref, o_ref, acc):
        i = pl.program_id(1)

        @pl.when(i == 0)
        def _():
            acc[...] = jnp.zeros_like(acc)

        acc[...] += _tn(x_ref[...], y_ref[...])

        @pl.when(i == nt - 1)
        def _():
            o_ref[...] = acc[...].astype(out_dtype)

    return pl.pallas_call(
        body, name="mm_tn", grid=(g, nt),
        in_specs=[pl.BlockSpec((None, tm, k), (lambda gi, i: (gi, i, 0)) if gx > 1 else (lambda gi, i: (0, i, 0))),
                  pl.BlockSpec((None, tm, n), (lambda gi, i: (gi, i, 0)) if gy > 1 else (lambda gi, i: (0, i, 0)))],
        out_specs=pl.BlockSpec((None, k, n), lambda gi, i: (gi, 0, 0)),
        out_shape=jax.ShapeDtypeStruct((g, k, n), out_dtype),
        scratch_shapes=[pltpu.VMEM((k, n), F32)],
        compiler_params=_cp("arbitrary", "arbitrary"),
    )(x, y)


def ln_bwd(parts, r, gamma, out_scale):
    t = r.shape[0]
    tm = _row_tile(t)
    scales = [s for _, s in parts]
    npart = len(parts)

    def body(*refs):
        r_ref, g_ref = refs[npart], refs[npart + 1]
        dr_ref, drb_ref, dg_ref, db_ref = refs[npart + 2:]
        i = pl.program_id(0)
        dy = scales[0] * refs[0][...]
        for p in range(1, npart):
            dy += scales[p] * refs[p][...]
        rr = r_ref[...]
        mu = jnp.mean(rr, axis=1, keepdims=True)
        xc = rr - mu
        rstd = lax.rsqrt(jnp.mean(xc * xc, axis=1, keepdims=True) + EPS)
        xh = xc * rstd
        dxh = dy * g_ref[...]
        m1 = jnp.mean(dxh, axis=1, keepdims=True)
        m2 = jnp.mean(dxh * xh, axis=1, keepdims=True)
        dr = rstd * (dxh - m1 - xh * m2)
        dr_ref[...] = dr
        drb_ref[...] = (out_scale * dr).astype(BF16)
        dg = jnp.sum(dy * xh, axis=0, keepdims=True)
        db = jnp.sum(dy, axis=0, keepdims=True)

        @pl.when(i == 0)
        def _():
            dg_ref[...] = dg
            db_ref[...] = db

        @pl.when(i > 0)
        def _():
            dg_ref[...] += dg
            db_ref[...] += db

    row = pl.BlockSpec((tm, D), lambda i: (i, 0))
    vec = pl.BlockSpec((1, D), lambda i: (0, 0))
    return pl.pallas_call(
        body, name="ln_bwd", grid=(t // tm,),
        in_specs=[row] * (npart + 1) + [vec],
        out_specs=[row, row, vec, vec],
        out_shape=[jax.ShapeDtypeStruct((t, D), F32), jax.ShapeDtypeStruct((t, D), BF16),
                   jax.ShapeDtypeStruct((1, D), F32), jax.ShapeDtypeStruct((1, D), F32)],
        compiler_params=_cp("arbitrary"),
    )(*[p for p, _ in parts], r, gamma)


def loss_head(h, target):
    t = h.shape[0]
    nb = t // BLK

    def body(h_ref, t_ref, dy_ref, l_ref):
        i = pl.program_id(0)

        @pl.when(i == 0)
        def _():
            dy_ref[...] = jnp.zeros_like(dy_ref)
            l_ref[...] = jnp.zeros_like(l_ref)

        @pl.when(i > 0)
        def _():
            err = h_ref[...] - t_ref[...]
            dy_ref[...] = err * (1.0 / D)
            l_ref[...] += (0.5 / D) * jnp.sum(err * err, keepdims=True)

    return pl.pallas_call(
        body, name="loss_head", grid=(nb,),
        in_specs=[pl.BlockSpec((BLK, D), lambda i: (i, 0)),
                  pl.BlockSpec((BLK, D), lambda i: (jnp.maximum(i - 1, 0), 0))],
        out_specs=[pl.BlockSpec((BLK, D), lambda i: (i, 0)), pl.BlockSpec((1, 1), lambda i: (0, 0))],
        out_shape=[jax.ShapeDtypeStruct((t, D), F32), jax.ShapeDtypeStruct((1, 1), F32)],
        compiler_params=_cp("arbitrary"),
    )(h, target)


def final_add(dr, dh):
    t = dr.shape[0]
    nb = t // BLK

    def body(a_ref, b_ref, gx_ref, gm_ref):
        i = pl.program_id(0)
        tot = ALPHA * a_ref[...] + b_ref[...]

        @pl.when(i == 0)
        def _():
            gm_ref[...] = tot[PAD:, :]

        @pl.when(i > 0)
        def _():
            gx_ref[...] = tot

    blk = pl.BlockSpec((BLK, D), lambda i: (i, 0))
    return pl.pallas_call(
        body, name="final_add", grid=(nb,),
        in_specs=[blk, blk],
        out_specs=[pl.BlockSpec((BLK, D), lambda i: (jnp.maximum(i - 1, 0), 0)),
                   pl.BlockSpec((N_META, D), lambda i: (0, 0))],
        out_shape=[jax.ShapeDtypeStruct((t - BLK, D), F32), jax.ShapeDtypeStruct((N_META, D), F32)],
        compiler_params=_cp("arbitrary"),
    )(dr, dh)


def _valid_rows(nrows, first_row):
    return (first_row + lax.broadcasted_iota(jnp.int32, (nrows, 1), 0)) >= PAD


def conv_fwd(proj, conv_w, conv_b):
    t = proj.shape[0]
    c0 = C_XBC // BLK

    def body(x_ref, w_ref, b_ref, o_ref):
        ok = _valid_rows(t, 0)
        x = jnp.where(ok, x_ref[...], 0.0)
        w = w_ref[...]
        acc = b_ref[...] + w[CONV_K - 1:CONV_K, :] * x
        for s in range(1, CONV_K):
            acc += w[CONV_K - 1 - s:CONV_K - s, :] * pltpu.roll(x, s, 0)
        o_ref[...] = jnp.where(ok, acc * _sig(acc), 0.0)

    return pl.pallas_call(
        body, name="conv_fwd", grid=(CONV_D // BLK,),
        in_specs=[pl.BlockSpec((t, BLK), lambda j: (0, c0 + j)),
                  pl.BlockSpec((CONV_K, BLK), lambda j: (0, j)), pl.BlockSpec((1, BLK), lambda j: (0, j))],
        out_specs=pl.BlockSpec((t, BLK), lambda j: (0, j)),
        out_shape=jax.ShapeDtypeStruct((t, CONV_D), F32),
        compiler_params=_cp("arbitrary"),
    )(proj, conv_w, conv_b)


def conv_bwd(dxa, proj, conv_w, conv_b):
    t = proj.shape[0]
    c0 = C_XBC // BLK

    def body(d_ref, x_ref, w_ref, b_ref, dx_ref, dw_ref, db_ref):
        ok = _valid_rows(t, 0)
        x = jnp.where(ok, x_ref[...], 0.0)
        w = w_ref[...]
        xs = [x] + [pltpu.roll(x, s, 0) for s in range(1, CONV_K)]
        acc = b_ref[...] + w[CONV_K - 1:CONV_K, :] * x
        for s in range(1, CONV_K):
            acc += w[CONV_K - 1 - s:CONV_K - s, :] * xs[s]
        sg = _sig(acc)
        dxc = jnp.where(ok, d_ref[...] * (sg * (1.0 + acc * (1.0 - sg))), 0.0)
        db_ref[...] = jnp.sum(dxc, axis=0, keepdims=True)
        dw_ref[...] = jnp.concatenate(
            [jnp.sum(dxc * xs[CONV_K - 1 - k], axis=0, keepdims=True) for k in range(CONV_K)], axis=0)
        dx = w[CONV_K - 1:CONV_K, :] * dxc
        for s in range(1, CONV_K):
            dx += w[CONV_K - 1 - s:CONV_K - s, :] * pltpu.roll(dxc, t - s, 0)
        dx_ref[...] = jnp.where(ok, dx, 0.0)

    col = pl.BlockSpec((t, BLK), lambda j: (0, j))
    return pl.pallas_call(
        body, name="conv_bwd", grid=(CONV_D // BLK,),
        in_specs=[col, pl.BlockSpec((t, BLK), lambda j: (0, c0 + j)),
                  pl.BlockSpec((CONV_K, BLK), lambda j: (0, j)), pl.BlockSpec((1, BLK), lambda j: (0, j))],
        out_specs=[col, pl.BlockSpec((CONV_K, BLK), lambda j: (0, j)), pl.BlockSpec((1, BLK), lambda j: (0, j))],
        out_shape=[jax.ShapeDtypeStruct((t, CONV_D), F32), jax.ShapeDtypeStruct((CONV_K, CONV_D), F32),
                   jax.ShapeDtypeStruct((1, CONV_D), F32)],
        compiler_params=_cp("arbitrary"),
    )(dxa, proj, conv_w, conv_b)


def _softplus(x):
    return jnp.maximum(x, 0.0) + jnp.log(1.0 + jnp.exp(-jnp.abs(x)))


def _ssd_chunk(xa, sm, dtb, alog, ok):
    dt = jnp.where(ok, _softplus(sm + dtb), 0.0)
    amat = -jnp.exp(alog)
    a = dt * amat
    ac = _nn_hi(_tri().astype(F32), a)
    act = ac.T
    return dt, amat, ac, act


def _ssd_head(xa, dt, ac, act, h, cb, sp):
    g = h // (SSD_H // SSD_G)
    xs = xa[:, SSD_P * h:SSD_P * (h + 1)]
    bg = xa[:, SSD_D + SSD_N * g:SSD_D + SSD_N * (g + 1)]
    cg = xa[:, SSD_D + SSD_G * SSD_N + SSD_N * g:SSD_D + SSD_G * SSD_N + SSD_N * (g + 1)]
    dth = dt[:, h:h + 1]
    ach = ac[:, h:h + 1]
    acth = act[h:h + 1, :]
    xdt = xs * dth
    seg = jnp.where(_tri(), jnp.exp(jnp.minimum(ach - acth, 0.0)), 0.0)
    m = cb * seg
    yd = _nn(m.astype(BF16), xdt.astype(BF16))
    last = ac[BLK - 1:BLK, h:h + 1]
    dec = jnp.exp(last - ach)
    e = jnp.exp(ach)
    yo = _nn(cg.astype(BF16), sp.astype(BF16)) * e
    return xs, bg, cg, dth, ach, xdt, seg, m, yd, last, dec, e, yo


def ssd_fwd(xa, proj, dtb, alog, dskip, normg):
    t = xa.shape[0]
    nb = t // BLK
    gw = SSD_D // SSD_G

    def body(xa_ref, z_ref, sm_ref, dtb_ref, al_ref, ds_ref, ng_ref, y_ref, sp_ref, st):
        c = pl.program_id(0)

        @pl.when(c == 0)
        def _():
            st[...] = jnp.zeros_like(st)

        ok = _valid_rows(BLK, c * BLK)
        xa = xa_ref[...]
        dt, _, ac, act = _ssd_chunk(xa, sm_ref[...], dtb_ref[...], al_ref[...], ok)
        sp_ref[...] = st[...]
        ys = []
        cbs = {}
        for h in range(SSD_H):
            g = h // (SSD_H // SSD_G)
            if g not in cbs:
                bg = xa[:, SSD_D + SSD_N * g:SSD_D + SSD_N * (g + 1)]
                cg = xa[:, SSD_D + SSD_G * SSD_N + SSD_N * g:SSD_D + SSD_G * SSD_N + SSD_N * (g + 1)]
                cbs[g] = _nt(cg.astype(BF16), bg.astype(BF16))
            sp = st[:, SSD_P * h:SSD_P * (h + 1)]
            xs, bg, cg, dth, ach, xdt, seg, m, yd, last, dec, e, yo = _ssd_head(xa, dt, ac, act, h, cbs[g], sp)
            sloc = _tn((bg * dec).astype(BF16), xdt.astype(BF16))
            st[:, SSD_P * h:SSD_P * (h + 1)] = jnp.exp(last) * sp + sloc
            ys.append(yd + yo + ds_ref[:, h:h + 1] * xs)
        y = jnp.concatenate(ys, axis=1)
        z = z_ref[...]
        yg = y * (z * _sig(z))
        outs = []
        for g in range(SSD_G):
            v = yg[:, gw * g:gw * (g + 1)]
            outs.append(v * lax.rsqrt(jnp.mean(v * v, axis=1, keepdims=True) + EPS))
        y_ref[...] = (jnp.concatenate(outs, axis=1) * ng_ref[...]).astype(BF16)

    vec = pl.BlockSpec((1, BLK), lambda c: (0, 0))
    return pl.pallas_call(
        body, name="ssd_fwd", grid=(nb,),
        in_specs=[pl.BlockSpec((BLK, CONV_D), lambda c: (c, 0)),
                  pl.BlockSpec((BLK, SSD_D), lambda c: (c, C_Z // SSD_D)),
                  pl.BlockSpec((BLK, BLK), lambda c: (c, C_SM // BLK)),
                  vec, vec, vec, pl.BlockSpec((1, SSD_D), lambda c: (0, 0))],
        out_specs=[pl.BlockSpec((BLK, SSD_D), lambda c: (c, 0)),
                   pl.BlockSpec((None, SSD_N, SSD_D), lambda c: (c, 0, 0))],
        out_shape=[jax.ShapeDtypeStruct((t, SSD_D), BF16), jax.ShapeDtypeStruct((nb, SSD_N, SSD_D), F32)],
        scratch_shapes=[pltpu.VMEM((SSD_N, SSD_D), F32)],
        compiler_params=_cp("arbitrary"),
    )(xa, proj, proj, dtb, alog, dskip, normg)


def _lane_put(col, lane):
    li = lax.broadcasted_iota(jnp.int32, (col.shape[0], BLK), 1)
    return jnp.where(li == lane, col, 0.0)


def ssd_bwd(dmix, xa, proj, sprev, dtb, alog, dskip, normg):
    t = xa.shape[0]
    nb = t // BLK
    gw = SSD_D // SSD_G
    rev = lambda c: nb - 1 - c

    def body(dy_ref, xa_ref, z_ref, sm_ref, sp_ref, dtb_ref, al_ref, ds_ref, ng_ref,
             dxa_ref, dz_ref, dsm_ref, dng_ref, dds_ref, dal_ref, ddtb_ref, dst):
        c = pl.program_id(0)

        @pl.when(c == 0)
        def _():
            dst[...] = jnp.zeros_like(dst)
            dng_ref[...] = jnp.zeros_like(dng_ref)
            dds_ref[...] = jnp.zeros_like(dds_ref)
            dal_ref[...] = jnp.zeros_like(dal_ref)
            ddtb_ref[...] = jnp.zeros_like(ddtb_ref)

        ok = _valid_rows(BLK, rev(c) * BLK)
        xa = xa_ref[...]
        sm = sm_ref[...]
        dt, amat, ac, act = _ssd_chunk(xa, sm, dtb_ref[...], al_ref[...], ok)
        tri = _tri()
        rowi = lax.broadcasted_iota(jnp.int32, (BLK, 1), 0)
        cbs, heads, ys = {}, [], []
        for h in range(SSD_H):
            g = h // (SSD_H // SSD_G)
            if g not in cbs:
                bg = xa[:, SSD_D + SSD_N * g:SSD_D + SSD_N * (g + 1)]
                cg = xa[:, SSD_D + SSD_G * SSD_N + SSD_N * g:SSD_D + SSD_G * SSD_N + SSD_N * (g + 1)]
                cbs[g] = _nt(cg.astype(BF16), bg.astype(BF16))
            sp = sp_ref[:, SSD_P * h:SSD_P * (h + 1)]
            hd = _ssd_head(xa, dt, ac, act, h, cbs[g], sp)
            heads.append(hd)
            ys.append(hd[8] + hd[12] + ds_ref[:, h:h + 1] * hd[0])
        y = jnp.concatenate(ys, axis=1)
        z = z_ref[...]
        sgz = _sig(z)
        siluz = z * sgz
        yg = y * siluz
        dout = dy_ref[...]
        ng = ng_ref[...]
        dygs, xhs = [], []
        for g in range(SSD_G):
            v = yg[:, gw * g:gw * (g + 1)]
            rr = lax.rsqrt(jnp.mean(v * v, axis=1, keepdims=True) + EPS)
            xh = v * rr
            dxh = dout[:, gw * g:gw * (g + 1)] * ng[:, gw * g:gw * (g + 1)]
            dygs.append(rr * (dxh - xh * jnp.mean(dxh * xh, axis=1, keepdims=True)))
            xhs.append(xh)
        dyg = jnp.concatenate(dygs, axis=1)
        dng_ref[...] += jnp.sum(dout * jnp.concatenate(xhs, axis=1), axis=0, keepdims=True)
        dy = dyg * siluz
        dz_ref[...] = dyg * y * (sgz * (1.0 + z * (1.0 - sgz)))

        dxs_l = []
        db_g = [jnp.zeros((BLK, SSD_N), F32) for _ in range(SSD_G)]
        dc_g = [jnp.zeros((BLK, SSD_N), F32) for _ in range(SSD_G)]
        dac_all = jnp.zeros((BLK, BLK), F32)
        ddt_all = jnp.zeros((BLK, BLK), F32)
        dds_row = jnp.zeros((1, BLK), F32)
        lane1 = lax.broadcasted_iota(jnp.int32, (1, BLK), 1)
        for h in range(SSD_H):
            g = h // (SSD_H // SSD_G)
            xs, bg, cg, dth, ach, xdt, seg, m, yd, last, dec, e, yo = heads[h]
            sp = sp_ref[:, SSD_P * h:SSD_P * (h + 1)]
            dyh = dy[:, SSD_P * h:SSD_P * (h + 1)]
            dyb = dyh.astype(BF16)
            xdtb = xdt.astype(BF16)
            dds_row += jnp.where(lane1 == h, jnp.sum(dyh * xs, keepdims=True), 0.0)
            dxs = ds_ref[:, h:h + 1] * dyh
            dyo = (dyh * e).astype(BF16)
            dc_g[g] += _nt(dyo, sp.astype(BF16))
            dsp = _tn(cg.astype(BF16), dyo)
            dac = jnp.sum(dyh * yo, axis=1, keepdims=True)
            dsn = dst[:, SSD_P * h:SSD_P * (h + 1)]
            gl = jnp.exp(last)
            dst[:, SSD_P * h:SSD_P * (h + 1)] = dsp + gl * dsn
            dlast = jnp.sum(dsn * sp, keepdims=True) * gl
            dsnb = dsn.astype(BF16)
            dbd = _nt(xdtb, dsnb)
            db_g[g] += dbd * dec
            tdec = jnp.sum(dbd * bg, axis=1, keepdims=True) * dec
            dxdt = _nn((bg * dec).astype(BF16), dsnb)
            dlast += jnp.sum(tdec, keepdims=True)
            dac -= tdec
            dm = _nt(dyb, xdtb)
            dxdt += _tn(m.astype(BF16), dyb)
            dcb = (dm * seg).astype(BF16)
            dc_g[g] += _nn(dcb, bg.astype(BF16))
            db_g[g] += _tn(dcb, cg.astype(BF16))
            w = dm * m
            dac += jnp.sum(w, axis=1, keepdims=True) - jnp.sum(w.T, axis=1, keepdims=True)
            dac += jnp.where(rowi == BLK - 1, dlast, 0.0)
            dxs_l.append(dxs + dxdt * dth)
            ddt_all += _lane_put(jnp.sum(dxdt * xs, axis=1, keepdims=True), h)
            dac_all += _lane_put(dac, h)
        da = _nn_hi(_tri(lower=False).astype(F32), dac_all)
        ddt = ddt_all + da * amat
        dal_ref[...] += jnp.sum(da * dt, axis=0, keepdims=True) * amat
        ddtr = jnp.where(ok, ddt * _sig(sm + dtb_ref[...]), 0.0)
        ddtb_ref[...] += jnp.sum(ddtr, axis=0, keepdims=True)
        dds_ref[...] += dds_row
        dsm_ref[...] = ddtr
        dxa_ref[...] = jnp.where(ok, jnp.concatenate(dxs_l + db_g + dc_g, axis=1), 0.0)

    vec = pl.BlockSpec((1, BLK), lambda c: (0, 0))
    nvec = pl.BlockSpec((1, SSD_D), lambda c: (0, 0))
    return pl.pallas_call(
        body, name="ssd_bwd", grid=(nb,),
        in_specs=[pl.BlockSpec((BLK, SSD_D), lambda c: (rev(c), 0)),
                  pl.BlockSpec((BLK, CONV_D), lambda c: (rev(c), 0)),
                  pl.BlockSpec((BLK, SSD_D), lambda c: (rev(c), C_Z // SSD_D)),
                  pl.BlockSpec((BLK, BLK), lambda c: (rev(c), C_SM // BLK)),
                  pl.BlockSpec((None, SSD_N, SSD_D), lambda c: (rev(c), 0, 0)),
                  vec, vec, vec, nvec],
        out_specs=[pl.BlockSpec((BLK, CONV_D), lambda c: (rev(c), 0)),
                   pl.BlockSpec((BLK, SSD_D), lambda c: (rev(c), 0)),
                   pl.BlockSpec((BLK, BLK), lambda c: (rev(c), 0)),
                   nvec, vec, vec, vec],
        out_shape=[jax.ShapeDtypeStruct((t, CONV_D), F32), jax.ShapeDtypeStruct((t, SSD_D), F32),
                   jax.ShapeDtypeStruct((t, BLK), F32), jax.ShapeDtypeStruct((1, SSD_D), F32),
                   jax.ShapeDtypeStruct((1, BLK), F32), jax.ShapeDtypeStruct((1, BLK), F32),
                   jax.ShapeDtypeStruct((1, BLK), F32)],
        scratch_shapes=[pltpu.VMEM((SSD_N, SSD_D), F32)],
        compiler_params=_cp("arbitrary"),
    )(dmix, xa, proj, proj, sprev, dtb, alog, dskip, normg)


def _attn_scores(q_ref, k_ref, h, dq, scale, mask, bias):
    qh = q_ref[:, dq * h:dq * (h + 1)].astype(BF16)
    kh = k_ref[:, dq * h:dq * (h + 1)].astype(BF16)
    s = _nt(qh, kh) * scale
    if bias is not None:
        s = s + bias
    return qh, kh, jnp.where(mask, s, NEG)


def attn_fwd(q, k, v, qcol, kcol, vcol, nh, dq, dv, scale, c_col=None, c_row=None, lane0=0):
    t = q.shape[0]
    tq = BLK
    use_bias = c_col is not None

    def body(*refs):
        if use_bias:
            q_ref, k_ref, v_ref, cc_ref, cr_ref, o_ref, l_ref = refs
        else:
            q_ref, k_ref, v_ref, o_ref, l_ref = refs
        i = pl.program_id(0)
        rowg = i * tq + lax.broadcasted_iota(jnp.int32, (tq, 1), 0)
        col = lax.broadcasted_iota(jnp.int32, (1, t), 1)
        mask = (col <= rowg) & (col >= PAD)
        outs = []
        lse = jnp.zeros((tq, BLK), F32)
        for h in range(nh):
            bias = (cc_ref[:, lane0 + h:lane0 + h + 1] - cr_ref[h:h + 1, :]) if use_bias else None
            _, _, s = _attn_scores(q_ref, k_ref, h, dq, scale, mask, bias)
            m = jnp.max(s, axis=1, keepdims=True)
            p = jnp.exp(s - m)
            l = jnp.sum(p, axis=1, keepdims=True)
            vh = v_ref[:, dv * h:dv * (h + 1)].astype(BF16)
            outs.append(_nn(p.astype(BF16), vh) / l)
            lse += _lane_put(m + jnp.log(l), h)
        o_ref[...] = jnp.concatenate(outs, axis=1).astype(BF16)
        l_ref[...] = lse

    in_specs = [pl.BlockSpec((tq, nh * dq), lambda i: (i, qcol)),
                pl.BlockSpec((t, nh * dq), lambda i: (0, kcol)),
                pl.BlockSpec((t, nh * dv), lambda i: (0, vcol))]
    args = [q, k, v]
    if use_bias:
        in_specs += [pl.BlockSpec((tq, BLK), lambda i: (i, 0)), pl.BlockSpec((8, t), lambda i: (0, 0))]
        args += [c_col, c_row]
    return pl.pallas_call(
        body, name="attn_fwd", grid=(t // tq,),
        in_specs=in_specs,
        out_specs=[pl.BlockSpec((tq, nh * dv), lambda i: (i, 0)), pl.BlockSpec((tq, BLK), lambda i: (i, 0))],
        out_shape=[jax.ShapeDtypeStruct((t, nh * dv), BF16), jax.ShapeDtypeStruct((t, BLK), F32)],
        compiler_params=_cp("arbitrary"),
    )(*args)


def attn_bwd(q, k, v, do, lse, qcol, kcol, vcol, docol, nh, dq, dv, scale, c_col=None, c_row=None, lane0=0):
    t = q.shape[0]
    tq = BLK
    use_bias = c_col is not None

    def body(*refs):
        if use_bias:
            q_ref, k_ref, v_ref, do_ref, l_ref, cc_ref, cr_ref, dq_ref, dk_ref, dv_ref, dcq_ref, dck_ref = refs
        else:
            q_ref, k_ref, v_ref, do_ref, l_ref, dq_ref, dk_ref, dv_ref = refs
        i = pl.program_id(0)

        @pl.when(i == 0)
        def _():
            dk_ref[...] = jnp.zeros_like(dk_ref)
            dv_ref[...] = jnp.zeros_like(dv_ref)
            if use_bias:
                dck_ref[...] = jnp.zeros_like(dck_ref)

        rowg = i * tq + lax.broadcasted_iota(jnp.int32, (tq, 1), 0)
        col = lax.broadcasted_iota(jnp.int32, (1, t), 1)
        mask = (col <= rowg) & (col >= PAD)
        dqs = []
        dcq = jnp.zeros((tq, BLK), F32)
        for h in range(nh):
            bias = (cc_ref[:, lane0 + h:lane0 + h + 1] - cr_ref[h:h + 1, :]) if use_bias else None
            qh, kh, s = _attn_scores(q_ref, k_ref, h, dq, scale, mask, bias)
            p = jnp.where(mask, jnp.exp(s - l_ref[:, h:h + 1]), 0.0)
            vh = v_ref[:, dv * h:dv * (h + 1)].astype(BF16)
            doh = do_ref[:, dv * h:dv * (h + 1)].astype(BF16)
            dp = _nt(doh, vh)
            delta = jnp.sum(p * dp, axis=1, keepdims=True)
            ds = p * (dp - delta)
            dsb = ds.astype(BF16)
            dqs.append(_nn(dsb, kh) * scale)
            dk_ref[:, dq * h:dq * (h + 1)] += _tn(dsb, qh) * scale
            dv_ref[:, dv * h:dv * (h + 1)] += _tn(p.astype(BF16), doh)
            if use_bias:
                dcq += _lane_put(jnp.sum(ds, axis=1, keepdims=True), lane0 + h)
                dck_ref[h:h + 1, :] += jnp.sum(ds, axis=0, keepdims=True)
        dq_ref[...] = jnp.concatenate(dqs, axis=1)
        if use_bias:
            dcq_ref[...] = dcq

    in_specs = [pl.BlockSpec((tq, nh * dq), lambda i: (i, qcol)),
                pl.BlockSpec((t, nh * dq), lambda i: (0, kcol)),
                pl.BlockSpec((t, nh * dv), lambda i: (0, vcol)),
                pl.BlockSpec((tq, nh * dv), lambda i: (i, docol)),
                pl.BlockSpec((tq, BLK), lambda i: (i, 0))]
    args = [q, k, v, do, lse]
    out_specs = [pl.BlockSpec((tq, nh * dq), lambda i: (i, 0)), pl.BlockSpec((t, nh * dq), lambda i: (0, 0)),
                 pl.BlockSpec((t, nh * dv), lambda i: (0, 0))]
    out_shape = [jax.ShapeDtypeStruct((t, nh * dq), F32), jax.ShapeDtypeStruct((t, nh * dq), F32),
                 jax.ShapeDtypeStruct((t, nh * dv), F32)]
    if use_bias:
        in_specs += [pl.BlockSpec((tq, BLK), lambda i: (i, 0)), pl.BlockSpec((8, t), lambda i: (0, 0))]
        args += [c_col, c_row]
        out_specs += [pl.BlockSpec((tq, BLK), lambda i: (i, 0)), pl.BlockSpec((8, t), lambda i: (0, 0))]
        out_shape += [jax.ShapeDtypeStruct((t, BLK), F32), jax.ShapeDtypeStruct((8, t), F32)]
    return pl.pallas_call(
        body, name="attn_bwd", grid=(t // tq,),
        in_specs=in_specs, out_specs=out_specs, out_shape=out_shape,
        compiler_params=_cp("arbitrary"),
    )(*args)


def fox_pre(proj, fb):
    t = proj.shape[0]
    nb = t // BLK

    def body(sm_ref, fb_ref, c_ref, cr_ref):
        x = sm_ref[...] + fb_ref[...]
        lane = lax.broadcasted_iota(jnp.int32, (1, BLK), 1)
        keep = _valid_rows(t, 0) & (lane >= SM_F) & (lane < SM_F + FOX_H)
        logf = jnp.where(keep, jnp.minimum(x, 0.0) - jnp.log(1.0 + jnp.exp(-jnp.abs(x))), 0.0)
        tri = _tri().astype(F32)
        carry = jnp.zeros((1, BLK), F32)
        for b in range(nb):
            cb = _nn_hi(tri, logf[b * BLK:(b + 1) * BLK, :]) + carry
            c_ref[b * BLK:(b + 1) * BLK, :] = cb
            carry = cb[BLK - 1:BLK, :]
        cr_ref[...] = c_ref[...].T[SM_F:SM_F + 8, :]

    return pl.pallas_call(
        body, name="fox_pre", grid=(1,),
        in_specs=[pl.BlockSpec((t, BLK), lambda i: (0, C_SM // BLK)), pl.BlockSpec((1, BLK), lambda i: (0, 0))],
        out_specs=[pl.BlockSpec((t, BLK), lambda i: (0, 0)), pl.BlockSpec((8, t), lambda i: (0, 0))],
        out_shape=[jax.ShapeDtypeStruct((t, BLK), F32), jax.ShapeDtypeStruct((8, t), F32)],
        compiler_params=_cp("arbitrary"),
    )(proj, fb)


def fox_pre_bwd(dcq, dck, proj, fb, dsm_in):
    t = proj.shape[0]
    nb = t // BLK

    def body(dcq_ref, dck_ref, sm_ref, fb_ref, din_ref, dsm_ref, dfb_ref, scr):
        triu = _tri(lower=False).astype(F32)
        carry = jnp.zeros((1, BLK), F32)
        scr[...] = jnp.concatenate([jnp.zeros((SM_F, t), F32), dck_ref[...], jnp.zeros((BLK - SM_F - 8, t), F32)], axis=0).T
        for b in range(nb - 1, -1, -1):
            blk = dcq_ref[b * BLK:(b + 1) * BLK, :] - scr[b * BLK:(b + 1) * BLK, :]
            cb = _nn_hi(triu, blk) + carry
            scr[b * BLK:(b + 1) * BLK, :] = cb
            carry = cb[0:1, :]
        x = sm_ref[...] + fb_ref[...]
        lane = lax.broadcasted_iota(jnp.int32, (1, BLK), 1)
        keep = _valid_rows(t, 0) & (lane >= SM_F) & (lane < SM_F + FOX_H)
        df = jnp.where(keep, scr[...] * _sig(-x), 0.0)
        dfb_ref[...] = jnp.sum(df, axis=0, keepdims=True)
        dsm_ref[...] = din_ref[...] + df

    full = pl.BlockSpec((t, BLK), lambda i: (0, 0))
    return pl.pallas_call(
        body, name="fox_pre_bwd", grid=(1,),
        in_specs=[full, pl.BlockSpec((8, t), lambda i: (0, 0)), pl.BlockSpec((t, BLK), lambda i: (0, C_SM // BLK)),
                  pl.BlockSpec((1, BLK), lambda i: (0, 0)), full],
        out_specs=[full, pl.BlockSpec((1, BLK), lambda i: (0, 0))],
        out_shape=[jax.ShapeDtypeStruct((t, BLK), F32), jax.ShapeDtypeStruct((1, BLK), F32)],
        scratch_shapes=[pltpu.VMEM((t, BLK), F32)],
        compiler_params=_cp("arbitrary"),
    )(dcq, dck, proj, fb, dsm_in)


def _swap_rope(x):
    lane = lax.broadcasted_iota(jnp.int32, (1, BLK), 1)
    return jnp.where((lane >= SM_KR) & (lane < SM_KR + 16), pltpu.roll(x, BLK - 16, 1),
                     jnp.where((lane >= SM_KR + 16) & (lane < SM_KR + 32), pltpu.roll(x, 16, 1), 0.0))


def _rms(x, g):
    r = lax.rsqrt(jnp.mean(x * x, axis=1, keepdims=True) + EPS)
    return r, x * r


def mla_pre(proj, qg, kvg, wq, wk, wv, cosq, sinq):
    t = proj.shape[0]
    tm = _row_tile(t)

    def body(cq_ref, ckv_ref, sm_ref, qg_ref, kvg_ref, wq_ref, wk_ref, wv_ref, cos_ref, sin_ref,
             q_ref, k_ref, v_ref, cqn_ref, ckvn_ref):
        cs, sn = cos_ref[...], sin_ref[...]
        _, xh = _rms(cq_ref[...], None)
        cqn = (xh * qg_ref[...]).astype(BF16)
        cqn_ref[...] = cqn
        qraw = _nn(cqn, wq_ref[...])
        qs = []
        for h in range(MLA_H):
            hb = qraw[:, BLK * h:BLK * (h + 1)]
            qs.append(hb * cs + _swap_rope(hb) * sn)
        q_ref[...] = jnp.concatenate(qs, axis=1).astype(BF16)
        _, kh = _rms(ckv_ref[...], None)
        ckvn = (kh * kvg_ref[...]).astype(BF16)
        ckvn_ref[...] = ckvn
        kraw = _nn(ckvn, wk_ref[...])
        v_ref[...] = _nn(ckvn, wv_ref[...]).astype(BF16)
        lane = lax.broadcasted_iota(jnp.int32, (1, BLK), 1)
        kr = sm_ref[...]
        krr = jnp.where((lane >= SM_KR) & (lane < SM_KR + MLA_ROPE), kr * cs + _swap_rope(kr) * sn, 0.0)
        k_ref[...] = jnp.concatenate([kraw[:, BLK * h:BLK * (h + 1)] + krr for h in range(MLA_H)], axis=1).astype(BF16)

    def rows(w, cb):
        return pl.BlockSpec((tm, w), lambda i: (i, cb))

    def whole(a):
        return pl.BlockSpec(a.shape, lambda i: (0, 0))

    return pl.pallas_call(
        body, name="mla_pre", grid=(t // tm,),
        in_specs=[rows(MLA_QL, C_CQ // MLA_QL), rows(MLA_KVL, C_CKV // MLA_KVL), rows(BLK, C_SM // BLK),
                  whole(qg), whole(kvg), whole(wq), whole(wk), whole(wv), rows(BLK, 0), rows(BLK, 0)],
        out_specs=[rows(512, 0), rows(512, 0), rows(256, 0), rows(MLA_QL, 0), rows(MLA_KVL, 0)],
        out_shape=[jax.ShapeDtypeStruct((t, 512), BF16), jax.ShapeDtypeStruct((t, 512), BF16),
                   jax.ShapeDtypeStruct((t, 256), BF16), jax.ShapeDtypeStruct((t, MLA_QL), BF16),
                   jax.ShapeDtypeStruct((t, MLA_KVL), BF16)],
        compiler_params=_cp("arbitrary"),
    )(proj, proj, proj, qg, kvg, wq, wk, wv, cosq, sinq)


def mla_pre_bwd(dq, dk, dv, proj, cqn, ckvn, qg, kvg, wq, wk, wv, cosq, sinq, dsm_in):
    t = proj.shape[0]
    tm = _row_tile(t)

    def body(dq_ref, dk_ref, dv_ref, cq_ref, ckv_ref, cqn_ref, ckvn_ref, qg_ref, kvg_ref, wq_ref, wk_ref, wv_ref,
             cos_ref, sin_ref, din_ref, dcq_ref, dckv_ref, dsm_ref, dwq_ref, dwk_ref, dwv_ref, dqg_ref, dkvg_ref):
        i = pl.program_id(0)

        @pl.when(i == 0)
        def _():
            for r in (dwq_ref, dwk_ref, dwv_ref, dqg_ref, dkvg_ref):
                r[...] = jnp.zeros_like(r)

        cs, sn = cos_ref[...], sin_ref[...]
        lane = lax.broadcasted_iota(jnp.int32, (1, BLK), 1)

        def unrope(dy):
            return dy * cs + _swap_rope(dy * sn)

        dqp = jnp.concatenate([unrope(dq_ref[:, BLK * h:BLK * (h + 1)]) for h in range(MLA_H)], axis=1).astype(BF16)
        dwq_ref[...] += _tn(cqn_ref[...], dqp)
        dcqn = _nt(dqp, wq_ref[...])
        r, xh = _rms(cq_ref[...], None)
        dqg_ref[...] += jnp.sum(dcqn * xh, axis=0, keepdims=True)
        dxh = dcqn * qg_ref[...]
        dcq_ref[...] = r * (dxh - xh * jnp.mean(dxh * xh, axis=1, keepdims=True))

        dkn, dkr = [], jnp.zeros((tm, BLK), F32)
        for h in range(MLA_H):
            blk = dk_ref[:, BLK * h:BLK * (h + 1)]
            dkn.append(jnp.where(lane < MLA_NOPE, blk, 0.0))
            dkr += jnp.where((lane >= SM_KR) & (lane < SM_KR + MLA_ROPE), blk, 0.0)
        dknb = jnp.concatenate(dkn, axis=1).astype(BF16)
        dvb = dv_ref[...].astype(BF16)
        ckvn = ckvn_ref[...]
        dwk_ref[...] += _tn(ckvn, dknb)
        dwv_ref[...] += _tn(ckvn, dvb)
        dckvn = _nt(dknb, wk_ref[...]) + _nt(dvb, wv_ref[...])
        r2, kh = _rms(ckv_ref[...], None)
        dkvg_ref[...] += jnp.sum(dckvn * kh, axis=0, keepdims=True)
        dkh = dckvn * kvg_ref[...]
        dckv_ref[...] = r2 * (dkh - kh * jnp.mean(dkh * kh, axis=1, keepdims=True))
        dsm_ref[...] = din_ref[...] + jnp.where((lane >= SM_KR) & (lane < SM_KR + MLA_ROPE), unrope(dkr), 0.0)

    def rows(w, cb):
        return pl.BlockSpec((tm, w), lambda i: (i, cb))

    def whole(a):
        return pl.BlockSpec(a.shape, lambda i: (0, 0))

    def wshape(a):
        return jax.ShapeDtypeStruct(a.shape, F32)

    return pl.pallas_call(
        body, name="mla_pre_bwd", grid=(t // tm,),
        in_specs=[rows(512, 0), rows(512, 0), rows(256, 0), rows(MLA_QL, C_CQ // MLA_QL), rows(MLA_KVL, C_CKV // MLA_KVL),
                  rows(MLA_QL, 0), rows(MLA_KVL, 0), whole(qg), whole(kvg), whole(wq), whole(wk), whole(wv),
                  rows(BLK, 0), rows(BLK, 0), rows(BLK, 0)],
        out_specs=[rows(MLA_QL, 0), rows(MLA_KVL, 0), rows(BLK, 0), whole(wq), whole(wk), whole(wv), whole(qg), whole(kvg)],
        out_shape=[jax.ShapeDtypeStruct((t, MLA_QL), F32), jax.ShapeDtypeStruct((t, MLA_KVL), F32),
                   jax.ShapeDtypeStruct((t, BLK), F32), wshape(wq), wshape(wk), wshape(wv), wshape(qg), wshape(kvg)],
        compiler_params=_cp("arbitrary"),
    )(dq, dk, dv, proj, proj, cqn, ckvn, qg, kvg, wq, wk, wv, cosq, sinq, dsm_in)


def adamw(w, m, v, g=None, recv=None):
    shape = w.shape
    c = shape[-1]
    rws = w.size // c
    tr = rws
    for d in (1024, 512, 352, 256, 128, 64, 32, 16, 8):
        if rws % d == 0 and d * c * 4 <= (2 << 20):
            tr = d
            break
    from_recv = recv is not None
    w2, m2, v2 = (a.reshape(rws, c) for a in (w, m, v))
    gin = recv.reshape(N_DEV, rws, c) if from_recv else g.reshape(rws, c)

    def body(w_ref, m_ref, v_ref, g_ref, *outs):
        if from_recv:
            gg = g_ref[0].astype(F32)
            for s in range(1, N_DEV):
                gg = gg + g_ref[s].astype(F32)
            outs[0][...] = gg
            outs = outs[1:]
        else:
            gg = g_ref[...]
        d_ref, nm_ref, nv_ref = outs
        nm = B1 * m_ref[...] + (1.0 - B1) * gg
        nv = B2 * v_ref[...] + (1.0 - B2) * (gg * gg)
        mh = nm / (1.0 - B1 ** STEP)
        vh = nv / (1.0 - B2 ** STEP)
        d_ref[...] = -LR * (mh / (jnp.sqrt(vh) + AEPS) + WD * w_ref[...])
        nm_ref[...] = nm
        nv_ref[...] = nv

    row = pl.BlockSpec((tr, c), lambda i: (i, 0))
    gspec = pl.BlockSpec((N_DEV, tr, c), lambda i: (0, i, 0)) if from_recv else row
    nout = 4 if from_recv else 3
    outs = pl.pallas_call(
        body, name="adamw", grid=(rws // tr,),
        in_specs=[row, row, row, gspec], out_specs=[row] * nout,
        out_shape=[jax.ShapeDtypeStruct((rws, c), F32)] * nout,
        compiler_params=_cp("arbitrary"),
    )(w2, m2, v2, gin)
    return tuple(o.reshape(shape) for o in outs)


def sum_slots(recv):
    _, r, c = recv.shape

    def body(r_ref, o_ref):
        gg = r_ref[0].astype(F32)
        for s in range(1, N_DEV):
            gg = gg + r_ref[s].astype(F32)
        o_ref[...] = gg

    return pl.pallas_call(
        body, name="sum_slots", grid=(1,),
        in_specs=[pl.BlockSpec((N_DEV, r, c), lambda i: (0, 0, 0))],
        out_specs=pl.BlockSpec((r, c), lambda i: (0, 0)),
        out_shape=jax.ShapeDtypeStruct((r, c), F32),
        compiler_params=_cp("arbitrary"),
    )(recv)


_FLIPS = [(0, 0, 1), (0, 1, 0), (0, 1, 1), (1, 0, 0), (1, 0, 1), (1, 1, 0), (1, 1, 1)]
_ANY = pl.BlockSpec(memory_space=pl.ANY)


def _mesh_place():
    x, y, c = lax.axis_index("x"), lax.axis_index("y"), lax.axis_index("c")
    me = 4 * x + 2 * y + c
    peers = [((x + fx) % 2, (y + fy) % 2, (c + fc) % 2) for fx, fy, fc in _FLIPS]
    return me, peers


def exchange(jobs, name):
    n_in, plan, out_shape = 0, [], []
    args = []
    for arr, mode, nl in jobs:
        if mode == "gather":
            args.append(arr)
            for l in range(nl):
                plan.append((mode, n_in, l, len(out_shape)))
                out_shape.append(jax.ShapeDtypeStruct((N_DEV,) + arr.shape[1:], arr.dtype))
            n_in += 1
        else:
            for l in range(nl):
                args.append(arr[l])
                plan.append((mode, n_in, l, len(out_shape)))
                n_in += 1
            out_shape.append(jax.ShapeDtypeStruct((N_DEV, nl) + arr[0].shape[1:], arr[0].dtype))
    ncopy = len(plan)

    def body(*refs):
        ins, outs = refs[:n_in], refs[n_in:n_in + len(out_shape)]
        send_sems, recv_sems, loc_sems = refs[n_in + len(out_shape):]
        me, peers = _mesh_place()
        ids = [4 * p[0] + 2 * p[1] + p[2] for p in peers]

        def parts(j, slot_src, slot_dst):
            mode, ii, l, oi = plan[j]
            if mode == "gather":
                return ins[ii].at[l], outs[oi].at[slot_dst]
            return ins[ii].at[slot_src], outs[oi].at[slot_dst, l]

        started = []
        for j in range(ncopy):
            src, dst = parts(j, me, me)
            loc = pltpu.make_async_copy(src, dst, loc_sems.at[j])
            loc.start()
            started.append(loc)
            for k in range(N_DEV - 1):
                src, dst = parts(j, ids[k], me)
                cp = pltpu.make_async_remote_copy(src_ref=src, dst_ref=dst, send_sem=send_sems.at[j, k],
                                                  recv_sem=recv_sems.at[j, k], device_id=peers[k],
                                                  device_id_type=pl.DeviceIdType.MESH)
                cp.start()
        for j in range(ncopy):
            started[j].wait()
            for k in range(N_DEV - 1):
                src, dst = parts(j, ids[k], ids[k])
                cp = pltpu.make_async_remote_copy(src_ref=src, dst_ref=dst, send_sem=send_sems.at[j, k],
                                                  recv_sem=recv_sems.at[j, k], device_id=peers[k],
                                                  device_id_type=pl.DeviceIdType.MESH)
                cp.wait_send()
                cp.wait_recv()

    return pl.pallas_call(
        body, name=name,
        in_specs=[_ANY] * n_in, out_specs=[_ANY] * len(out_shape), out_shape=out_shape,
        scratch_shapes=[pltpu.SemaphoreType.DMA((ncopy, N_DEV - 1)), pltpu.SemaphoreType.DMA((ncopy, N_DEV - 1)),
                        pltpu.SemaphoreType.DMA((ncopy,))],
    )(*args)


def _pad_cols(a, n):
    return jnp.pad(a, ((0, 0),) * (a.ndim - 1) + ((0, n - a.shape[-1]),))


def w_in_to_padded(w):
    z = lambda n: jnp.zeros(w.shape[:-1] + (n,), w.dtype)
    return jnp.concatenate([
        w[..., 0:1280], w[..., 1288:2056], w[..., 2060:2316], w[..., 2316:2444],
        w[..., 1280:1288], w[..., 2056:2060], z(SM_KR - SM_F - FOX_H), w[..., 2444:2476], z(BLK - SM_KR - MLA_ROPE)], axis=-1)


def w_in_from_padded(g):
    s = C_SM
    return jnp.concatenate([
        g[..., 0:1280], g[..., s + SM_DT:s + SM_DT + 8], g[..., 1280:2048], g[..., s + SM_F:s + SM_F + 4],
        g[..., 2048:2304], g[..., 2304:2432], g[..., s + SM_KR:s + SM_KR + MLA_ROPE]], axis=-1)


def _unshard_cols(gth):
    n, r, c = gth.shape
    return jnp.transpose(gth, (1, 0, 2)).reshape(r, n * c)


def _shard_cols(full):
    r, nc = full.shape
    return jnp.transpose(full.reshape(r, N_DEV, nc // N_DEV), (1, 0, 2))


def mla_weights(uq_g, ukv_g):
    uq = _unshard_cols(uq_g)
    dqh = MLA_NOPE + MLA_ROPE
    wq = jnp.concatenate([_pad_cols(uq[:, dqh * h:dqh * (h + 1)], BLK) for h in range(MLA_H)], axis=1)
    wk = jnp.concatenate([_pad_cols(ukv_g[2 * h], BLK) for h in range(MLA_H)], axis=1)
    wv = jnp.concatenate([ukv_g[2 * h + 1] for h in range(MLA_H)], axis=1)
    return wq, wk, wv


def mla_weight_grads(dwq, dwk, dwv):
    dqh = MLA_NOPE + MLA_ROPE
    duq = _shard_cols(jnp.concatenate([dwq[:, BLK * h:BLK * h + dqh] for h in range(MLA_H)], axis=1))
    parts = []
    for h in range(MLA_H):
        parts += [dwk[:, BLK * h:BLK * h + MLA_NOPE], dwv[:, MLA_V * h:MLA_V * (h + 1)]]
    return duq, jnp.stack(parts, axis=0)


def rope_tables(t):
    pos = (jnp.arange(t, dtype=jnp.int32) - PAD).astype(F32)
    inv_freq = 1.0 / (10000.0 ** (jnp.arange(0, MLA_ROPE, 2, dtype=F32) / MLA_ROPE))
    ang = pos[:, None] * inv_freq[None, :]
    cos, sin = jnp.cos(ang), jnp.sin(ang)
    one, zero = jnp.ones((t, SM_KR), F32), jnp.zeros((t, SM_KR), F32)
    tail = BLK - SM_KR - MLA_ROPE
    cosq = jnp.concatenate([one, cos, cos, jnp.ones((t, tail), F32)], axis=1)
    sinq = jnp.concatenate([zero, -sin, sin, jnp.zeros((t, tail), F32)], axis=1)
    return cosq, sinq


def _lanes(v, off=0):
    return jnp.pad(v.astype(F32), (off, BLK - off - v.shape[0]))[None, :]


def layer_fwd(h, hb, W, tabs):
    sv = {"h0": h, "h0b": hb}
    u, v, a = ffn_up(hb, W["g1"], W["u1"])
    r1, h1, h1b = mm_res_ln(a, W["d1"], h, ALPHA, 0.5, W["ln1_g"], W["ln1_b"])
    sv.update(u1=u, v1=v, a1=a, r1=r1, h1=h1, h1b=h1b)
    proj = mm_nn(h1b, W["w_in"])
    xa = conv_fwd(proj, W["conv_w"], W["conv_b"])
    y_ssd, sprev = ssd_fwd(xa, proj, W["dtb"], W["alog"], W["dskip"], W["normg"])
    c_col, c_row = fox_pre(proj, W["fb"])
    y_fox, lse_f = attn_fwd(proj, proj, proj, C_FQ // 256, C_FK // 256, C_FV // 256, FOX_H, FOX_DH, FOX_DH,
                            FOX_DH ** -0.5, c_col, c_row, SM_F)
    q, k, vv, cqn, ckvn = mla_pre(proj, W["qg"], W["kvg"], W["wq"], W["wk"], W["wv"], *tabs)
    y_mla, lse_m = attn_fwd(q, k, vv, 0, 0, 0, MLA_H, BLK, MLA_V, (MLA_NOPE + MLA_ROPE) ** -0.5)
    mixcat = jnp.concatenate([y_ssd, y_fox, y_mla], axis=1)
    r2, h2, h2b = mm_res_ln(mixcat[None], W["w_out"][None], h1, ALPHA, 1.0, W["ln2_g"], W["ln2_b"])
    sv.update(proj=proj, xa=xa, sprev=sprev, c_col=c_col, c_row=c_row, lse_f=lse_f, q=q, k=k, v=vv, cqn=cqn, ckvn=ckvn,
              lse_m=lse_m, mixcat=mixcat, r2=r2, h2=h2, h2b=h2b)
    u, v, a = ffn_up(h2b, W["g2"], W["u2"])
    r3, h3, h3b = mm_res_ln(a, W["d2"], h2, ALPHA, 0.5, W["ln3_g"], W["ln3_b"])
    sv.update(u2=u, v2=v, a2=a, r3=r3)
    return h3, h3b, sv


def ffn_bwd(parts, r, gamma, hb_in, u, v, a, wg, wu, wd):
    dr, dfb, dg, db = ln_bwd(parts, r, gamma, 0.5)
    du, dv = ffn_dact(dfb, wd, u, v)
    dh = mm_nt_reduce([(du, wg), (dv, wu)], D)
    gr = dict(d=mm_tn(a, dfb[None]), g=mm_tn(hb_in[None], du), u=mm_tn(hb_in[None], dv), ln_g=dg, ln_b=db)
    return dr, dh, gr


def layer_bwd(parts, W, sv, tabs):
    G = {}
    dr3, dh2f, g2 = ffn_bwd(parts, sv["r3"], W["ln3_g"], sv["h2b"], sv["u2"], sv["v2"], sv["a2"], W["g2"], W["u2"], W["d2"])
    G.update(g2=g2["g"], u2=g2["u"], d2=g2["d"], ln3_g=g2["ln_g"], ln3_b=g2["ln_b"])
    dr2, dmixb, G["ln2_g"], G["ln2_b"] = ln_bwd([(dr3, ALPHA), (dh2f, 1.0)], sv["r2"], W["ln2_g"], 1.0)
    dmc = mm_nt_reduce([(dmixb[None], W["w_out"][None])], D)
    G["w_out"] = mm_tn(sv["mixcat"][None], dmixb[None])[0]
    proj = sv["proj"]
    dxa, dz, dsm, G["normg"], G["dskip"], G["alog"], G["dtb"] = ssd_bwd(
        dmc, sv["xa"], proj, sv["sprev"], W["dtb"], W["alog"], W["dskip"], W["normg"])
    dxbc, G["conv_w"], G["conv_b"] = conv_bwd(dxa, proj, W["conv_w"], W["conv_b"])
    dfq, dfk, dfv, dcq, dck = attn_bwd(proj, proj, proj, dmc, sv["lse_f"], C_FQ // 256, C_FK // 256, C_FV // 256, 2,
                                       FOX_H, FOX_DH, FOX_DH, FOX_DH ** -0.5, sv["c_col"], sv["c_row"], SM_F)
    dsm, G["fb"] = fox_pre_bwd(dcq, dck, proj, W["fb"], dsm)
    dq, dk, dv = attn_bwd(sv["q"], sv["k"], sv["v"], dmc, sv["lse_m"], 0, 0, 0, 3, MLA_H, BLK, MLA_V,
                          (MLA_NOPE + MLA_ROPE) ** -0.5)
    dcql, dckv, dsm, G["wq"], G["wk"], G["wv"], G["qg"], G["kvg"] = mla_pre_bwd(
        dq, dk, dv, proj, sv["cqn"], sv["ckvn"], W["qg"], W["kvg"], W["wq"], W["wk"], W["wv"], *tabs, dsm)
    dproj = jnp.concatenate([dz, dxbc, dfq, dfk, dfv, dcql, dckv, dsm], axis=1).astype(BF16)
    dh1p = mm_nt_reduce([(dproj[None], W["w_in"][None])], D)
    G["w_in"] = mm_tn(sv["h1b"][None], dproj[None])[0]
    dr1, dh0f, g1 = ffn_bwd([(dr2, ALPHA), (dh1p, 1.0)], sv["r1"], W["ln1_g"], sv["h0b"], sv["u1"], sv["v1"], sv["a1"],
                            W["g1"], W["u1"], W["d1"])
    G.update(g1=g1["g"], u1=g1["u"], d1=g1["d"], ln1_g=g1["ln_g"], ln1_b=g1["ln_b"])
    return [(dr1, ALPHA), (dh0f, 1.0)], G


def local_step(x, target, meta_full, Ws):
    t = x.shape[0] + BLK
    tabs = rope_tables(t)
    h, hb = build_h0(meta_full, x)
    saved = []
    for W in Ws:
        h, hb, sv = layer_fwd(h, hb, W, tabs)
        saved.append(sv)
    dy, loss = loss_head(h, target)
    parts = [(dy, 1.0)]
    grads = [None] * NL
    for l in range(NL - 1, -1, -1):
        parts, grads[l] = layer_bwd(parts, Ws[l], saved[l], tabs)
    gx, gmeta = final_add(parts[0][0], parts[1][0])
    return loss, gx, gmeta, grads


_SMALL = ["ln1_g", "ln1_b", "ln2_g", "ln2_b", "ln3_g", "ln3_b", "conv_b", "ssd_norm_g", "mla_q_norm_g",
          "mla_kv_norm_g", "dt_bias", "a_log", "d_skip", "fox_f_b"]
_SMALL_ROWS = 16
_BIG = ["ffn1_w_gate", "ffn1_w_up", "ffn1_w_down", "w_in", "conv_w", "mla_w_uq", "mla_w_ukv", "w_out",
        "ffn2_w_gate", "ffn2_w_up", "ffn2_w_down"]
_NAMES = ["meta", "ffn1_w_gate", "ffn1_w_up", "ffn1_w_down", "ln1_g", "ln1_b", "w_in", "conv_w", "conv_b", "dt_bias",
          "a_log", "d_skip", "ssd_norm_g", "fox_f_b", "mla_q_norm_g", "mla_w_uq", "mla_kv_norm_g", "mla_w_ukv", "w_out",
          "ln2_g", "ln2_b", "ffn2_w_gate", "ffn2_w_up", "ffn2_w_down", "ln3_g", "ln3_b"]


def pack_small(p):
    rows = []
    for l in range(NL):
        for n in _SMALL:
            rows.append(_pad_cols(p[n][l][None, :].astype(F32), D))
        rows.append(jnp.zeros((_SMALL_ROWS - len(_SMALL), D), F32))
    return jnp.concatenate(rows, axis=0)


def unpack_small(a, like):
    out = {}
    for i, n in enumerate(_SMALL):
        out[n] = jnp.stack([a[l * _SMALL_ROWS + i, :like[n].shape[1]] for l in range(NL)], axis=0)
    return out


def layer_weights(l, gath, rep):
    W = {}
    for k, n in (("g1", "ffn1_w_gate"), ("u1", "ffn1_w_up"), ("d1", "ffn1_w_down"),
                 ("g2", "ffn2_w_gate"), ("u2", "ffn2_w_up"), ("d2", "ffn2_w_down")):
        W[k] = gath[n][l]
    W["w_in"] = gath["w_in"][l].reshape(D, N_INP)
    W["w_out"] = gath["w_out"][l].reshape(D, D)
    W["wq"], W["wk"], W["wv"] = mla_weights(gath["mla_w_uq"][l], gath["mla_w_ukv"][l])
    W["conv_w"] = _unshard_cols(gath["conv_w"][l])
    for k in ("ln1_g", "ln1_b", "ln2_g", "ln2_b", "ln3_g", "ln3_b", "conv_b"):
        W[k] = rep[k][l][None, :]
    W["normg"] = rep["ssd_norm_g"][l][None, :]
    W["qg"] = rep["mla_q_norm_g"][l][None, :]
    W["kvg"] = rep["mla_kv_norm_g"][l][None, :]
    W["dtb"] = _lanes(rep["dt_bias"][l], SM_DT)
    W["alog"] = _lanes(rep["a_log"][l], SM_DT)
    W["dskip"] = _lanes(rep["d_skip"][l], SM_DT)
    W["fb"] = _lanes(rep["fox_f_b"][l], SM_F)
    return W


def small_grads(G):
    return {"ln1_g": G["ln1_g"][0], "ln1_b": G["ln1_b"][0], "ln2_g": G["ln2_g"][0], "ln2_b": G["ln2_b"][0],
            "ln3_g": G["ln3_g"][0], "ln3_b": G["ln3_b"][0], "conv_b": G["conv_b"][0], "ssd_norm_g": G["normg"][0],
            "mla_q_norm_g": G["qg"][0], "mla_kv_norm_g": G["kvg"][0], "dt_bias": G["dtb"][0, :SSD_H],
            "a_log": G["alog"][0, :SSD_H], "d_skip": G["dskip"][0, :SSD_H], "fox_f_b": G["fb"][0, SM_F:SM_F + FOX_H]}


def big_grads(G):
    duq, dukv = mla_weight_grads(G["wq"], G["wk"], G["wv"])
    return {"ffn1_w_gate": G["g1"], "ffn1_w_up": G["u1"], "ffn1_w_down": G["d1"],
            "ffn2_w_gate": G["g2"], "ffn2_w_up": G["u2"], "ffn2_w_down": G["d2"],
            "w_in": G["w_in"].reshape(N_DEV, D // N_DEV, N_INP), "w_out": G["w_out"].reshape(N_DEV, D // N_DEV, D),
            "mla_w_uq": duq, "mla_w_ukv": dukv, "conv_w": _shard_cols(G["conv_w"])}


def kernel(x, meta, ffn1_w_gate, ffn1_w_up, ffn1_w_down, ln1_g, ln1_b, w_in, conv_w, conv_b, dt_bias, a_log, d_skip, ssd_norm_g, fox_f_b, mla_q_norm_g, mla_w_uq, mla_kv_norm_g, mla_w_ukv, w_out, ln2_g, ln2_b, ffn2_w_gate, ffn2_w_up, ffn2_w_down, ln3_g, ln3_b, loss_target, m_meta, m_ffn1_w_gate, m_ffn1_w_up, m_ffn1_w_down, m_ln1_g, m_ln1_b, m_w_in, m_conv_w, m_conv_b, m_dt_bias, m_a_log, m_d_skip, m_ssd_norm_g, m_fox_f_b, m_mla_q_norm_g, m_mla_w_uq, m_mla_kv_norm_g, m_mla_w_ukv, m_w_out, m_ln2_g, m_ln2_b, m_ffn2_w_gate, m_ffn2_w_up, m_ffn2_w_down, m_ln3_g, m_ln3_b, v_meta, v_ffn1_w_gate, v_ffn1_w_up, v_ffn1_w_down, v_ln1_g, v_ln1_b, v_w_in, v_conv_w, v_conv_b, v_dt_bias, v_a_log, v_d_skip, v_ssd_norm_g, v_fox_f_b, v_mla_q_norm_g, v_mla_w_uq, v_mla_kv_norm_g, v_mla_w_ukv, v_w_out, v_ln2_g, v_ln2_b, v_ffn2_w_gate, v_ffn2_w_up, v_ffn2_w_down, v_ln3_g, v_ln3_b):
    vals = (meta, ffn1_w_gate, ffn1_w_up, ffn1_w_down, ln1_g, ln1_b, w_in, conv_w, conv_b, dt_bias, a_log, d_skip, ssd_norm_g, fox_f_b, mla_q_norm_g, mla_w_uq, mla_kv_norm_g, mla_w_ukv, w_out, ln2_g, ln2_b, ffn2_w_gate, ffn2_w_up, ffn2_w_down, ln3_g, ln3_b)
    moms = (m_meta, m_ffn1_w_gate, m_ffn1_w_up, m_ffn1_w_down, m_ln1_g, m_ln1_b, m_w_in, m_conv_w, m_conv_b, m_dt_bias, m_a_log, m_d_skip, m_ssd_norm_g, m_fox_f_b, m_mla_q_norm_g, m_mla_w_uq, m_mla_kv_norm_g, m_mla_w_ukv, m_w_out, m_ln2_g, m_ln2_b, m_ffn2_w_gate, m_ffn2_w_up, m_ffn2_w_down, m_ln3_g, m_ln3_b)
    vars_ = (v_meta, v_ffn1_w_gate, v_ffn1_w_up, v_ffn1_w_down, v_ln1_g, v_ln1_b, v_w_in, v_conv_w, v_conv_b, v_dt_bias, v_a_log, v_d_skip, v_ssd_norm_g, v_fox_f_b, v_mla_q_norm_g, v_mla_w_uq, v_mla_kv_norm_g, v_mla_w_ukv, v_w_out, v_ln2_g, v_ln2_b, v_ffn2_w_gate, v_ffn2_w_up, v_ffn2_w_down, v_ln3_g, v_ln3_b)
    P = dict(zip(_NAMES, vals))
    M = dict(zip(_NAMES, moms))
    V = dict(zip(_NAMES, vars_))
    me = 4 * lax.axis_index("x") + 2 * lax.axis_index("y") + lax.axis_index("c")

    send = {n: P[n].astype(BF16) for n in _BIG if n not in ("w_in", "conv_w")}
    send["w_in"] = w_in_to_padded(P["w_in"]).astype(BF16)
    send["conv_w"] = P["conv_w"]
    jobs = [(send[n], "gather", NL) for n in _BIG] + [(P["meta"][None], "gather", 1)]
    res = exchange(jobs, "gather_weights")
    gath = {n: res[NL * i:NL * (i + 1)] for i, n in enumerate(_BIG)}
    meta_full = _unshard_cols(res[NL * len(_BIG)])
    Ws = [layer_weights(l, gath, P) for l in range(NL)]

    loss, gx, gmeta, grads = local_step(x[0], loss_target[0], meta_full, Ws)

    bg = [big_grads(g) for g in grads]
    small = jnp.concatenate([pack_small({n: jnp.stack([small_grads(g)[n] for g in grads]) for n in _SMALL}), gmeta], axis=0)
    jobs = [([bg[l][n] for l in range(NL)], "scatter", NL) for n in _BIG] + [(small[None], "gather", 1)]
    res = exchange(jobs, "scatter_grads")
    recv = dict(zip(_BIG, res[:len(_BIG)]))
    gsmall = sum_slots(res[len(_BIG)].reshape(N_DEV, -1, D))

    out = {}
    for n in _BIG:
        if n == "w_in":
            gp = sum_slots(recv[n].reshape(N_DEV, NL * (D // N_DEV), N_INP))
            g = w_in_from_padded(gp).reshape(P[n].shape)
            out[n] = (g,) + adamw(P[n], M[n], V[n], g=g)
        else:
            out[n] = adamw(P[n], M[n], V[n], recv=recv[n])
    gm = lax.dynamic_slice(gsmall[NL * _SMALL_ROWS:], (0, me * (D // N_DEV)), (N_META, D // N_DEV))
    out["meta"] = (gm,) + adamw(P["meta"], M["meta"], V["meta"], g=gm)
    gs = gsmall[:NL * _SMALL_ROWS]
    sd, sm_, sv_ = adamw(pack_small(P), pack_small(M), pack_small(V), g=gs)
    ups = [unpack_small(a, P) for a in (gs, sd, sm_, sv_)]
    for n in _SMALL:
        out[n] = tuple(u[n] for u in ups)

    loss_all = lax.psum(loss[0, 0], ("x", "y", "c"))
    flat = [loss_all, gx[None]]
    for k in range(4):
        flat += [out[n][k] for n in _NAMES]
    return tuple(flat)
```

```python
import functools

import jax
import jax.numpy as jnp
from jax import lax
from jax.experimental import pallas as pl
from jax.experimental.pallas import tpu as pltpu

F32, BF16 = jnp.float32, jnp.bfloat16
HI = lax.Precision.HIGHEST

N_DEV = 8
D = 1024
NL = 2
N_META = 16
BLK = 128
PAD = BLK - N_META
D_FF = 2816
HS = D_FF // N_DEV
SSD_H, SSD_P, SSD_N, SSD_G = 8, 64, 64, 2
SSD_D = SSD_H * SSD_P
CONV_K = 4
CONV_D = SSD_D + 2 * SSD_G * SSD_N
FOX_H, FOX_DH = 4, 64
MLA_H, MLA_QL, MLA_KVL, MLA_NOPE, MLA_ROPE, MLA_V = 4, 256, 128, 64, 32, 64
N_IN = 2476
C_Z, C_XBC, C_FQ, C_FK, C_FV, C_CQ, C_CKV, C_SM, N_INP = 0, 512, 1280, 1536, 1792, 2048, 2304, 2432, 2560
SM_DT, SM_F, SM_KR = 0, 8, 64
ALPHA = (2 * NL) ** 0.25
EPS = 1e-5
NEG = -1e30
LR, B1, B2, AEPS, WD, STEP = 0.001, 0.9, 0.999, 1e-08, 0.01, 10
VMEM_MB = 56


def _cp(*sem):
    return pltpu.CompilerParams(dimension_semantics=sem, vmem_limit_bytes=VMEM_MB << 20)


def _nn(a, b):
    return lax.dot_general(a, b, (((1,), (0,)), ((), ())), preferred_element_type=F32)


def _nt(a, b):
    return lax.dot_general(a, b, (((1,), (1,)), ((), ())), preferred_element_type=F32)


def _tn(a, b):
    return lax.dot_general(a, b, (((0,), (0,)), ((), ())), preferred_element_type=F32)


def _nn_hi(a, b):
    return lax.dot_general(a, b, (((1,), (0,)), ((), ())), precision=HI, preferred_element_type=F32)


def _row_tile(t):
    for d in range(640, 15, -16):
        if t % d == 0:
            return d
    raise ValueError(t)


def _sig(x):
    return 1.0 / (1.0 + jnp.exp(-x))


def _tri(lower=True):
    r = lax.broadcasted_iota(jnp.int32, (BLK, BLK), 0)
    c = lax.broadcasted_iota(jnp.int32, (BLK, BLK), 1)
    return (r >= c) if lower else (r <= c)


def build_h0(meta_full, x):
    s = x.shape[0]
    nb = s // BLK + 1

    def body(m_ref, x_ref, h_ref, hb_ref):
        i = pl.program_id(0)

        @pl.when(i == 0)
        def _():
            h = jnp.concatenate([jnp.zeros((PAD, D), F32), m_ref[...]], axis=0)
            h_ref[...] = h
            hb_ref[...] = h.astype(BF16)

        @pl.when(i > 0)
        def _():
            h_ref[...] = x_ref[...]
            hb_ref[...] = x_ref[...].astype(BF16)

    return pl.pallas_call(
        body, name="build_h0", grid=(nb,),
        in_specs=[pl.BlockSpec((N_META, D), lambda i: (0, 0)),
                  pl.BlockSpec((BLK, D), lambda i: (jnp.maximum(i - 1, 0), 0))],
        out_specs=[pl.BlockSpec((BLK, D), lambda i: (i, 0))] * 2,
        out_shape=[jax.ShapeDtypeStruct((nb * BLK, D), F32), jax.ShapeDtypeStruct((nb * BLK, D), BF16)],
        compiler_params=_cp("arbitrary"),
    )(meta_full, x)


def ffn_up(hb, wg, wu):
    t = hb.shape[0]
    g, _, hs = wg.shape
    tm = _row_tile(t)

    def body(h_ref, wg_ref, wu_ref, u_ref, v_ref, a_ref):
        h = h_ref[...]
        u = _nn(h, wg_ref[...])
        v = _nn(h, wu_ref[...])
        u_ref[...] = u.astype(BF16)
        v_ref[...] = v.astype(BF16)
        a_ref[...] = (u * _sig(u) * v).astype(BF16)

    w_spec = pl.BlockSpec((None, D, hs), lambda gi, i: (gi, 0, 0))
    o_spec = pl.BlockSpec((None, tm, hs), lambda gi, i: (gi, i, 0))
    return pl.pallas_call(
        body, name="ffn_up", grid=(g, t // tm),
        in_specs=[pl.BlockSpec((tm, D), lambda gi, i: (i, 0)), w_spec, w_spec],
        out_specs=[o_spec] * 3,
        out_shape=[jax.ShapeDtypeStruct((g, t, hs), BF16)] * 3,
        compiler_params=_cp("arbitrary", "arbitrary"),
    )(hb, wg, wu)


def mm_res_ln(a, b, res, alpha, scale, gamma, beta):
    g, t, k = a.shape
    tm = _row_tile(t)

    def body(a_ref, b_ref, res_ref, g_ref, be_ref, r_ref, y_ref, yb_ref, acc):
        gi = pl.program_id(1)

        @pl.when(gi == 0)
        def _():
            acc[...] = jnp.zeros_like(acc)

        acc[...] += _nn(a_ref[...], b_ref[...])

        @pl.when(gi == g - 1)
        def _():
            r = alpha * res_ref[...] + scale * acc[...]
            mu = jnp.mean(r, axis=1, keepdims=True)
            xc = r - mu
            var = jnp.mean(xc * xc, axis=1, keepdims=True)
            y = xc * lax.rsqrt(var + EPS) * g_ref[...] + be_ref[...]
            r_ref[...] = r
            y_ref[...] = y
            yb_ref[...] = y.astype(BF16)

    row = pl.BlockSpec((tm, D), lambda i, gi: (i, 0))
    vec = pl.BlockSpec((1, D), lambda i, gi: (0, 0))
    return pl.pallas_call(
        body, name="mm_res_ln", grid=(t // tm, g),
        in_specs=[pl.BlockSpec((None, tm, k), lambda i, gi: (gi, i, 0)),
                  pl.BlockSpec((None, k, D), lambda i, gi: (gi, 0, 0)), row, vec, vec],
        out_specs=[row] * 3,
        out_shape=[jax.ShapeDtypeStruct((t, D), F32), jax.ShapeDtypeStruct((t, D), F32),
                   jax.ShapeDtypeStruct((t, D), BF16)],
        scratch_shapes=[pltpu.VMEM((tm, D), F32)],
        compiler_params=_cp("arbitrary", "arbitrary"),
    )(a, b, res, gamma, beta)


def mm_nn(a, b, tn=512):
    t, k = a.shape
    n = b.shape[1]
    tm = _row_tile(t)

    def body(a_ref, b_ref, o_ref):
        o_ref[...] = _nn(a_ref[...], b_ref[...])

    return pl.pallas_call(
        body, name="mm_nn", grid=(n // tn, t // tm),
        in_specs=[pl.BlockSpec((tm, k), lambda j, i: (i, 0)), pl.BlockSpec((k, tn), lambda j, i: (0, j))],
        out_specs=pl.BlockSpec((tm, tn), lambda j, i: (i, j)),
        out_shape=jax.ShapeDtypeStruct((t, n), F32),
        compiler_params=_cp("arbitrary", "arbitrary"),
    )(a, b)


def ffn_dact(dfb, wd, u, v):
    g, t, hs = u.shape
    tm = _row_tile(t)

    def body(df_ref, wd_ref, u_ref, v_ref, du_ref, dv_ref):
        da = _nt(df_ref[...], wd_ref[...])
        uu = u_ref[...].astype(F32)
        sg = _sig(uu)
        du_ref[...] = (da * v_ref[...].astype(F32) * (sg * (1.0 + uu * (1.0 - sg)))).astype(BF16)
        dv_ref[...] = (da * uu * sg).astype(BF16)

    act = pl.BlockSpec((None, tm, hs), lambda gi, i: (gi, i, 0))
    return pl.pallas_call(
        body, name="ffn_dact", grid=(g, t // tm),
        in_specs=[pl.BlockSpec((tm, D), lambda gi, i: (i, 0)),
                  pl.BlockSpec((None, hs, D), lambda gi, i: (gi, 0, 0)), act, act],
        out_specs=[act] * 2,
        out_shape=[jax.ShapeDtypeStruct((g, t, hs), BF16)] * 2,
        compiler_params=_cp("arbitrary", "arbitrary"),
    )(dfb, wd, u, v)


def mm_nt_reduce(pairs, n):
    g, t, _ = pairs[0][0].shape
    tm = _row_tile(t)
    npair = len(pairs)

    def body(*refs):
        o_ref = refs[-1]
        gi = pl.program_id(1)
        tot = _nt(refs[0][...], refs[1][...])
        for p in range(1, npair):
            tot += _nt(refs[2 * p][...], refs[2 * p + 1][...])

        @pl.when(gi == 0)
        def _():
            o_ref[...] = tot

        @pl.when(gi > 0)
        def _():
            o_ref[...] += tot

    in_specs, args = [], []
    for x, w in pairs:
        k = x.shape[2]
        in_specs += [pl.BlockSpec((None, tm, k), lambda i, gi: (gi, i, 0)),
                     pl.BlockSpec((None, n, k), lambda i, gi: (gi, 0, 0))]
        args += [x, w]
    return pl.pallas_call(
        body, name="mm_nt_reduce", grid=(t // tm, g),
        in_specs=in_specs, out_specs=pl.BlockSpec((tm, n), lambda i, gi: (i, 0)),
        out_shape=jax.ShapeDtypeStruct((t, n), F32),
        compiler_params=_cp("arbitrary", "arbitrary"),
    )(*args)


def mm_tn(x, y, out_dtype=BF16):
    gx, t, k = x.shape
    gy, _, n = y.shape
    g = max(gx, gy)
    tm = _row_tile(t)
    nt = t // tm

    def body(x_ref, y_ref, o_ref, acc):
        i = pl.program_id(1)

        @pl.when(i == 0)
        def _():
            acc[...] = jnp.zeros_like(acc)

        acc[...] += _tn(x_ref[...], y_ref[...])

        @pl.when(i == nt - 1)
        def _():
            o_ref[...] = acc[...].astype(out_dtype)

    return pl.pallas_call(
        body, name="mm_tn", grid=(g, nt),
        in_specs=[pl.BlockSpec((None, tm, k), (lambda gi, i: (gi, i, 0)) if gx > 1 else (lambda gi, i: (0, i, 0))),
                  pl.BlockSpec((None, tm, n), (lambda gi, i: (gi, i, 0)) if gy > 1 else (lambda gi, i: (0, i, 0)))],
        out_specs=pl.BlockSpec((None, k, n), lambda gi, i: (gi, 0, 0)),
        out_shape=jax.ShapeDtypeStruct((g, k, n), out_dtype),
        scratch_shapes=[pltpu.VMEM((k, n), F32)],
        compiler_params=_cp("arbitrary", "arbitrary"),
    )(x, y)


def ln_bwd(parts, r, gamma, out_scale, after=None):
    t = r.shape[0]
    tm = _row_tile(t)
    scales = [s for _, s in parts]
    npart = len(parts)
    extra = [] if after is None else [after]

    def body(*refs):
        refs = refs[len(extra):]
        r_ref, g_ref = refs[npart], refs[npart + 1]
        dr_ref, drb_ref, dg_ref, db_ref = refs[npart + 2:]
        i = pl.program_id(0)
        dy = scales[0] * refs[0][...]
        for p in range(1, npart):
            dy += scales[p] * refs[p][...]
        rr = r_ref[...]
        mu = jnp.mean(rr, axis=1, keepdims=True)
        xc = rr - mu
        rstd = lax.rsqrt(jnp.mean(xc * xc, axis=1, keepdims=True) + EPS)
        xh = xc * rstd
        dxh = dy * g_ref[...]
        m1 = jnp.mean(dxh, axis=1, keepdims=True)
        m2 = jnp.mean(dxh * xh, axis=1, keepdims=True)
        dr = rstd * (dxh - m1 - xh * m2)
        dr_ref[...] = dr
        drb_ref[...] = (out_scale * dr).astype(BF16)
        dg = jnp.sum(dy * xh, axis=0, keepdims=True)
        db = jnp.sum(dy, axis=0, keepdims=True)

        @pl.when(i == 0)
        def _():
            dg_ref[...] = dg
            db_ref[...] = db

        @pl.when(i > 0)
        def _():
            dg_ref[...] += dg
            db_ref[...] += db

    row = pl.BlockSpec((tm, D), lambda i: (i, 0))
    vec = pl.BlockSpec((1, D), lambda i: (0, 0))
    return pl.pallas_call(
        body, name="ln_bwd", grid=(t // tm,),
        in_specs=[_ANY] * len(extra) + [row] * (npart + 1) + [vec],
        out_specs=[row, row, vec, vec],
        out_shape=[jax.ShapeDtypeStruct((t, D), F32), jax.ShapeDtypeStruct((t, D), BF16),
                   jax.ShapeDtypeStruct((1, D), F32), jax.ShapeDtypeStruct((1, D), F32)],
        compiler_params=_cp("arbitrary"),
    )(*extra, *[p for p, _ in parts], r, gamma)


def loss_head(h, target):
    t = h.shape[0]
    nb = t // BLK

    def body(h_ref, t_ref, dy_ref, l_ref):
        i = pl.program_id(0)

        @pl.when(i == 0)
        def _():
            dy_ref[...] = jnp.zeros_like(dy_ref)
            l_ref[...] = jnp.zeros_like(l_ref)

        @pl.when(i > 0)
        def _():
            err = h_ref[...] - t_ref[...]
            dy_ref[...] = err * (1.0 / D)
            l_ref[...] += (0.5 / D) * jnp.sum(err * err, keepdims=True)

    return pl.pallas_call(
        body, name="loss_head", grid=(nb,),
        in_specs=[pl.BlockSpec((BLK, D), lambda i: (i, 0)),
                  pl.BlockSpec((BLK, D), lambda i: (jnp.maximum(i - 1, 0), 0))],
        out_specs=[pl.BlockSpec((BLK, D), lambda i: (i, 0)), pl.BlockSpec((1, 1), lambda i: (0, 0))],
        out_shape=[jax.ShapeDtypeStruct((t, D), F32), jax.ShapeDtypeStruct((1, 1), F32)],
        compiler_params=_cp("arbitrary"),
    )(h, target)


def final_add(dr, dh, after=None):
    t = dr.shape[0]
    nb = t // BLK
    extra = [] if after is None else [after]

    def body(*refs):
        a_ref, b_ref, gx_ref, gm_ref = refs[len(extra):]
        i = pl.program_id(0)
        tot = ALPHA * a_ref[...] + b_ref[...]

        @pl.when(i == 0)
        def _():
            gm_ref[...] = tot[PAD:, :]

        @pl.when(i > 0)
        def _():
            gx_ref[...] = tot

    blk = pl.BlockSpec((BLK, D), lambda i: (i, 0))
    return pl.pallas_call(
        body, name="final_add", grid=(nb,),
        in_specs=[_ANY] * len(extra) + [blk, blk],
        out_specs=[pl.BlockSpec((BLK, D), lambda i: (jnp.maximum(i - 1, 0), 0)),
                   pl.BlockSpec((N_META, D), lambda i: (0, 0))],
        out_shape=[jax.ShapeDtypeStruct((t - BLK, D), F32), jax.ShapeDtypeStruct((N_META, D), F32)],
        compiler_params=_cp("arbitrary"),
    )(*extra, dr, dh)


def _valid_rows(nrows, first_row):
    return (first_row + lax.broadcasted_iota(jnp.int32, (nrows, 1), 0)) >= PAD


def conv_fwd(proj, conv_w, conv_b):
    t = proj.shape[0]
    c0 = C_XBC // BLK

    def body(x_ref, w_ref, b_ref, o_ref):
        ok = _valid_rows(t, 0)
        x = jnp.where(ok, x_ref[...], 0.0)
        w = w_ref[...]
        acc = b_ref[...] + w[CONV_K - 1:CONV_K, :] * x
        for s in range(1, CONV_K):
            acc += w[CONV_K - 1 - s:CONV_K - s, :] * pltpu.roll(x, s, 0)
        o_ref[...] = jnp.where(ok, acc * _sig(acc), 0.0)

    return pl.pallas_call(
        body, name="conv_fwd", grid=(CONV_D // BLK,),
        in_specs=[pl.BlockSpec((t, BLK), lambda j: (0, c0 + j)),
                  pl.BlockSpec((CONV_K, BLK), lambda j: (0, j)), pl.BlockSpec((1, BLK), lambda j: (0, j))],
        out_specs=pl.BlockSpec((t, BLK), lambda j: (0, j)),
        out_shape=jax.ShapeDtypeStruct((t, CONV_D), F32),
        compiler_params=_cp("arbitrary"),
    )(proj, conv_w, conv_b)


def conv_bwd(dxa, proj, conv_w, conv_b):
    t = proj.shape[0]
    c0 = C_XBC // BLK

    def body(d_ref, x_ref, w_ref, b_ref, dx_ref, dw_ref, db_ref):
        ok = _valid_rows(t, 0)
        x = jnp.where(ok, x_ref[...], 0.0)
        w = w_ref[...]
        xs = [x] + [pltpu.roll(x, s, 0) for s in range(1, CONV_K)]
        acc = b_ref[...] + w[CONV_K - 1:CONV_K, :] * x
        for s in range(1, CONV_K):
            acc += w[CONV_K - 1 - s:CONV_K - s, :] * xs[s]
        sg = _sig(acc)
        dxc = jnp.where(ok, d_ref[...] * (sg * (1.0 + acc * (1.0 - sg))), 0.0)
        db_ref[...] = jnp.sum(dxc, axis=0, keepdims=True)
        dw_ref[...] = jnp.concatenate(
            [jnp.sum(dxc * xs[CONV_K - 1 - k], axis=0, keepdims=True) for k in range(CONV_K)], axis=0)
        dx = w[CONV_K - 1:CONV_K, :] * dxc
        for s in range(1, CONV_K):
            dx += w[CONV_K - 1 - s:CONV_K - s, :] * pltpu.roll(dxc, t - s, 0)
        dx_ref[...] = jnp.where(ok, dx, 0.0)

    col = pl.BlockSpec((t, BLK), lambda j: (0, j))
    return pl.pallas_call(
        body, name="conv_bwd", grid=(CONV_D // BLK,),
        in_specs=[col, pl.BlockSpec((t, BLK), lambda j: (0, c0 + j)),
                  pl.BlockSpec((CONV_K, BLK), lambda j: (0, j)), pl.BlockSpec((1, BLK), lambda j: (0, j))],
        out_specs=[col, pl.BlockSpec((CONV_K, BLK), lambda j: (0, j)), pl.BlockSpec((1, BLK), lambda j: (0, j))],
        out_shape=[jax.ShapeDtypeStruct((t, CONV_D), F32), jax.ShapeDtypeStruct((CONV_K, CONV_D), F32),
                   jax.ShapeDtypeStruct((1, CONV_D), F32)],
        compiler_params=_cp("arbitrary"),
    )(dxa, proj, conv_w, conv_b)


def _softplus(x):
    return jnp.maximum(x, 0.0) + jnp.log(1.0 + jnp.exp(-jnp.abs(x)))


def _ssd_chunk(xa, sm, dtb, alog, ok):
    dt = jnp.where(ok, _softplus(sm + dtb), 0.0)
    amat = -jnp.exp(alog)
    a = dt * amat
    ac = _nn_hi(_tri().astype(F32), a)
    act = ac.T
    return dt, amat, ac, act


def _ssd_head(xa, dt, ac, act, h, cb, sp):
    g = h // (SSD_H // SSD_G)
    xs = xa[:, SSD_P * h:SSD_P * (h + 1)]
    bg = xa[:, SSD_D + SSD_N * g:SSD_D + SSD_N * (g + 1)]
    cg = xa[:, SSD_D + SSD_G * SSD_N + SSD_N * g:SSD_D + SSD_G * SSD_N + SSD_N * (g + 1)]
    dth = dt[:, h:h + 1]
    ach = ac[:, h:h + 1]
    acth = act[h:h + 1, :]
    xdt = xs * dth
    seg = jnp.where(_tri(), jnp.exp(jnp.minimum(ach - acth, 0.0)), 0.0)
    m = cb * seg
    yd = _nn(m.astype(BF16), xdt.astype(BF16))
    last = ac[BLK - 1:BLK, h:h + 1]
    dec = jnp.exp(last - ach)
    e = jnp.exp(ach)
    yo = _nn(cg.astype(BF16), sp.astype(BF16)) * e
    return xs, bg, cg, dth, ach, xdt, seg, m, yd, last, dec, e, yo


def ssd_fwd(xa, proj, dtb, alog, dskip, normg):
    t = xa.shape[0]
    nb = t // BLK
    gw = SSD_D // SSD_G

    def body(xa_ref, z_ref, sm_ref, dtb_ref, al_ref, ds_ref, ng_ref, y_ref, sp_ref, st):
        c = pl.program_id(0)

        @pl.when(c == 0)
        def _():
            st[...] = jnp.zeros_like(st)

        ok = _valid_rows(BLK, c * BLK)
        xa = xa_ref[...]
        dt, _, ac, act = _ssd_chunk(xa, sm_ref[...], dtb_ref[...], al_ref[...], ok)
        sp_ref[...] = st[...]
        ys = []
        cbs = {}
        for h in range(SSD_H):
            g = h // (SSD_H // SSD_G)
            if g not in cbs:
                bg = xa[:, SSD_D + SSD_N * g:SSD_D + SSD_N * (g + 1)]
                cg = xa[:, SSD_D + SSD_G * SSD_N + SSD_N * g:SSD_D + SSD_G * SSD_N + SSD_N * (g + 1)]
                cbs[g] = _nt(cg.astype(BF16), bg.astype(BF16))
            sp = st[:, SSD_P * h:SSD_P * (h + 1)]
            xs, bg, cg, dth, ach, xdt, seg, m, yd, last, dec, e, yo = _ssd_head(xa, dt, ac, act, h, cbs[g], sp)
            sloc = _tn((bg * dec).astype(BF16), xdt.astype(BF16))
            st[:, SSD_P * h:SSD_P * (h + 1)] = jnp.exp(last) * sp + sloc
            ys.append(yd + yo + ds_ref[:, h:h + 1] * xs)
        y = jnp.concatenate(ys, axis=1)
        z = z_ref[...]
        yg = y * (z * _sig(z))
        outs = []
        for g in range(SSD_G):
            v = yg[:, gw * g:gw * (g + 1)]
            outs.append(v * lax.rsqrt(jnp.mean(v * v, axis=1, keepdims=True) + EPS))
        y_ref[...] = (jnp.concatenate(outs, axis=1) * ng_ref[...]).astype(BF16)

    vec = pl.BlockSpec((1, BLK), lambda c: (0, 0))
    return pl.pallas_call(
        body, name="ssd_fwd", grid=(nb,),
        in_specs=[pl.BlockSpec((BLK, CONV_D), lambda c: (c, 0)),
                  pl.BlockSpec((BLK, SSD_D), lambda c: (c, C_Z // SSD_D)),
                  pl.BlockSpec((BLK, BLK), lambda c: (c, C_SM // BLK)),
                  vec, vec, vec, pl.BlockSpec((1, SSD_D), lambda c: (0, 0))],
        out_specs=[pl.BlockSpec((BLK, SSD_D), lambda c: (c, 0)),
                   pl.BlockSpec((None, SSD_N, SSD_D), lambda c: (c, 0, 0))],
        out_shape=[jax.ShapeDtypeStruct((t, SSD_D), BF16), jax.ShapeDtypeStruct((nb, SSD_N, SSD_D), F32)],
        scratch_shapes=[pltpu.VMEM((SSD_N, SSD_D), F32)],
        compiler_params=_cp("arbitrary"),
    )(xa, proj, proj, dtb, alog, dskip, normg)


def _lane_put(col, lane):
    li = lax.broadcasted_iota(jnp.int32, (col.shape[0], BLK), 1)
    return jnp.where(li == lane, col, 0.0)


def ssd_bwd(dmix, xa, proj, sprev, dtb, alog, dskip, normg):
    t = xa.shape[0]
    nb = t // BLK
    gw = SSD_D // SSD_G
    rev = lambda c: nb - 1 - c

    def body(dy_ref, xa_ref, z_ref, sm_ref, sp_ref, dtb_ref, al_ref, ds_ref, ng_ref,
             dxa_ref, dz_ref, dsm_ref, dng_ref, dds_ref, dal_ref, ddtb_ref, dst):
        c = pl.program_id(0)

        @pl.when(c == 0)
        def _():
            dst[...] = jnp.zeros_like(dst)
            dng_ref[...] = jnp.zeros_like(dng_ref)
            dds_ref[...] = jnp.zeros_like(dds_ref)
            dal_ref[...] = jnp.zeros_like(dal_ref)
            ddtb_ref[...] = jnp.zeros_like(ddtb_ref)

        ok = _valid_rows(BLK, rev(c) * BLK)
        xa = xa_ref[...]
        sm = sm_ref[...]
        dt, amat, ac, act = _ssd_chunk(xa, sm, dtb_ref[...], al_ref[...], ok)
        tri = _tri()
        rowi = lax.broadcasted_iota(jnp.int32, (BLK, 1), 0)
        cbs, heads, ys = {}, [], []
        for h in range(SSD_H):
            g = h // (SSD_H // SSD_G)
            if g not in cbs:
                bg = xa[:, SSD_D + SSD_N * g:SSD_D + SSD_N * (g + 1)]
                cg = xa[:, SSD_D + SSD_G * SSD_N + SSD_N * g:SSD_D + SSD_G * SSD_N + SSD_N * (g + 1)]
                cbs[g] = _nt(cg.astype(BF16), bg.astype(BF16))
            sp = sp_ref[:, SSD_P * h:SSD_P * (h + 1)]
            hd = _ssd_head(xa, dt, ac, act, h, cbs[g], sp)
            heads.append(hd)
            ys.append(hd[8] + hd[12] + ds_ref[:, h:h + 1] * hd[0])
        y = jnp.concatenate(ys, axis=1)
        z = z_ref[...]
        sgz = _sig(z)
        siluz = z * sgz
        yg = y * siluz
        dout = dy_ref[...]
        ng = ng_ref[...]
        dygs, xhs = [], []
        for g in range(SSD_G):
            v = yg[:, gw * g:gw * (g + 1)]
            rr = lax.rsqrt(jnp.mean(v * v, axis=1, keepdims=True) + EPS)
            xh = v * rr
            dxh = dout[:, gw * g:gw * (g + 1)] * ng[:, gw * g:gw * (g + 1)]
            dygs.append(rr * (dxh - xh * jnp.mean(dxh * xh, axis=1, keepdims=True)))
            xhs.append(xh)
        dyg = jnp.concatenate(dygs, axis=1)
        dng_ref[...] += jnp.sum(dout * jnp.concatenate(xhs, axis=1), axis=0, keepdims=True)
        dy = dyg * siluz
        dz_ref[...] = dyg * y * (sgz * (1.0 + z * (1.0 - sgz)))

        dxs_l = []
        db_g = [jnp.zeros((BLK, SSD_N), F32) for _ in range(SSD_G)]
        dc_g = [jnp.zeros((BLK, SSD_N), F32) for _ in range(SSD_G)]
        dac_all = jnp.zeros((BLK, BLK), F32)
        ddt_all = jnp.zeros((BLK, BLK), F32)
        dds_row = jnp.zeros((1, BLK), F32)
        lane1 = lax.broadcasted_iota(jnp.int32, (1, BLK), 1)
        for h in range(SSD_H):
            g = h // (SSD_H // SSD_G)
            xs, bg, cg, dth, ach, xdt, seg, m, yd, last, dec, e, yo = heads[h]
            sp = sp_ref[:, SSD_P * h:SSD_P * (h + 1)]
            dyh = dy[:, SSD_P * h:SSD_P * (h + 1)]
            dyb = dyh.astype(BF16)
            xdtb = xdt.astype(BF16)
            dds_row += jnp.where(lane1 == h, jnp.sum(dyh * xs, keepdims=True), 0.0)
            dxs = ds_ref[:, h:h + 1] * dyh
            dyo = (dyh * e).astype(BF16)
            dc_g[g] += _nt(dyo, sp.astype(BF16))
            dsp = _tn(cg.astype(BF16), dyo)
            dac = jnp.sum(dyh * yo, axis=1, keepdims=True)
            dsn = dst[:, SSD_P * h:SSD_P * (h + 1)]
            gl = jnp.exp(last)
            dst[:, SSD_P * h:SSD_P * (h + 1)] = dsp + gl * dsn
            dlast = jnp.sum(dsn * sp, keepdims=True) * gl
            dsnb = dsn.astype(BF16)
            dbd = _nt(xdtb, dsnb)
            db_g[g] += dbd * dec
            tdec = jnp.sum(dbd * bg, axis=1, keepdims=True) * dec
            dxdt = _nn((bg * dec).astype(BF16), dsnb)
            dlast += jnp.sum(tdec, keepdims=True)
            dac -= tdec
            dm = _nt(dyb, xdtb)
            dxdt += _tn(m.astype(BF16), dyb)
            dcb = (dm * seg).astype(BF16)
            dc_g[g] += _nn(dcb, bg.astype(BF16))
            db_g[g] += _tn(dcb, cg.astype(BF16))
            w = dm * m
            dac += jnp.sum(w, axis=1, keepdims=True) - jnp.sum(w.T, axis=1, keepdims=True)
            dac += jnp.where(rowi == BLK - 1, dlast, 0.0)
            dxs_l.append(dxs + dxdt * dth)
            ddt_all += _lane_put(jnp.sum(dxdt * xs, axis=1, keepdims=True), h)
            dac_all += _lane_put(dac, h)
        da = _nn_hi(_tri(lower=False).astype(F32), dac_all)
        ddt = ddt_all + da * amat
        dal_ref[...] += jnp.sum(da * dt, axis=0, keepdims=True) * amat
        ddtr = jnp.where(ok, ddt * _sig(sm + dtb_ref[...]), 0.0)
        ddtb_ref[...] += jnp.sum(ddtr, axis=0, keepdims=True)
        dds_ref[...] += dds_row
        dsm_ref[...] = ddtr
        dxa_ref[...] = jnp.where(ok, jnp.concatenate(dxs_l + db_g + dc_g, axis=1), 0.0)

    vec = pl.BlockSpec((1, BLK), lambda c: (0, 0))
    nvec = pl.BlockSpec((1, SSD_D), lambda c: (0, 0))
    return pl.pallas_call(
        body, name="ssd_bwd", grid=(nb,),
        in_specs=[pl.BlockSpec((BLK, SSD_D), lambda c: (rev(c), 0)),
                  pl.BlockSpec((BLK, CONV_D), lambda c: (rev(c), 0)),
                  pl.BlockSpec((BLK, SSD_D), lambda c: (rev(c), C_Z // SSD_D)),
                  pl.BlockSpec((BLK, BLK), lambda c: (rev(c), C_SM // BLK)),
                  pl.BlockSpec((None, SSD_N, SSD_D), lambda c: (rev(c), 0, 0)),
                  vec, vec, vec, nvec],
        out_specs=[pl.BlockSpec((BLK, CONV_D), lambda c: (rev(c), 0)),
                   pl.BlockSpec((BLK, SSD_D), lambda c: (rev(c), 0)),
                   pl.BlockSpec((BLK, BLK), lambda c: (rev(c), 0)),
                   nvec, vec, vec, vec],
        out_shape=[jax.ShapeDtypeStruct((t, CONV_D), F32), jax.ShapeDtypeStruct((t, SSD_D), F32),
                   jax.ShapeDtypeStruct((t, BLK), F32), jax.ShapeDtypeStruct((1, SSD_D), F32),
                   jax.ShapeDtypeStruct((1, BLK), F32), jax.ShapeDtypeStruct((1, BLK), F32),
                   jax.ShapeDtypeStruct((1, BLK), F32)],
        scratch_shapes=[pltpu.VMEM((SSD_N, SSD_D), F32)],
        compiler_params=_cp("arbitrary"),
    )(dmix, xa, proj, proj, sprev, dtb, alog, dskip, normg)


def _attn_scores(q_ref, k_ref, h, dq, scale, mask, bias):
    qh = q_ref[:, dq * h:dq * (h + 1)].astype(BF16)
    kh = k_ref[:, dq * h:dq * (h + 1)].astype(BF16)
    s = _nt(qh, kh) * scale
    if bias is not None:
        s = s + bias
    return qh, kh, jnp.where(mask, s, NEG)


def attn_fwd(q, k, v, qcol, kcol, vcol, nh, dq, dv, scale, c_col=None, c_row=None, lane0=0):
    t = q.shape[0]
    tq = BLK
    use_bias = c_col is not None

    def body(*refs):
        if use_bias:
            q_ref, k_ref, v_ref, cc_ref, cr_ref, o_ref, l_ref = refs
        else:
            q_ref, k_ref, v_ref, o_ref, l_ref = refs
        i = pl.program_id(0)
        rowg = i * tq + lax.broadcasted_iota(jnp.int32, (tq, 1), 0)
        col = lax.broadcasted_iota(jnp.int32, (1, t), 1)
        mask = (col <= rowg) & (col >= PAD)
        outs = []
        lse = jnp.zeros((tq, BLK), F32)
        for h in range(nh):
            bias = (cc_ref[:, lane0 + h:lane0 + h + 1] - cr_ref[h:h + 1, :]) if use_bias else None
            _, _, s = _attn_scores(q_ref, k_ref, h, dq, scale, mask, bias)
            m = jnp.max(s, axis=1, keepdims=True)
            p = jnp.exp(s - m)
            l = jnp.sum(p, axis=1, keepdims=True)
            vh = v_ref[:, dv * h:dv * (h + 1)].astype(BF16)
            outs.append(_nn(p.astype(BF16), vh) / l)
            lse += _lane_put(m + jnp.log(l), h)
        o_ref[...] = jnp.concatenate(outs, axis=1).astype(BF16)
        l_ref[...] = lse

    in_specs = [pl.BlockSpec((tq, nh * dq), lambda i: (i, qcol)),
                pl.BlockSpec((t, nh * dq), lambda i: (0, kcol)),
                pl.BlockSpec((t, nh * dv), lambda i: (0, vcol))]
    args = [q, k, v]
    if use_bias:
        in_specs += [pl.BlockSpec((tq, BLK), lambda i: (i, 0)), pl.BlockSpec((8, t), lambda i: (0, 0))]
        args += [c_col, c_row]
    return pl.pallas_call(
        body, name="attn_fwd", grid=(t // tq,),
        in_specs=in_specs,
        out_specs=[pl.BlockSpec((tq, nh * dv), lambda i: (i, 0)), pl.BlockSpec((tq, BLK), lambda i: (i, 0))],
        out_shape=[jax.ShapeDtypeStruct((t, nh * dv), BF16), jax.ShapeDtypeStruct((t, BLK), F32)],
        compiler_params=_cp("arbitrary"),
    )(*args)


def attn_bwd(q, k, v, do, lse, qcol, kcol, vcol, docol, nh, dq, dv, scale, c_col=None, c_row=None, lane0=0):
    t = q.shape[0]
    tq = BLK
    use_bias = c_col is not None

    def body(*refs):
        if use_bias:
            q_ref, k_ref, v_ref, do_ref, l_ref, cc_ref, cr_ref, dq_ref, dk_ref, dv_ref, dcq_ref, dck_ref = refs
        else:
            q_ref, k_ref, v_ref, do_ref, l_ref, dq_ref, dk_ref, dv_ref = refs
        i = pl.program_id(0)

        @pl.when(i == 0)
        def _():
            dk_ref[...] = jnp.zeros_like(dk_ref)
            dv_ref[...] = jnp.zeros_like(dv_ref)
            if use_bias:
                dck_ref[...] = jnp.zeros_like(dck_ref)

        rowg = i * tq + lax.broadcasted_iota(jnp.int32, (tq, 1), 0)
        col = lax.broadcasted_iota(jnp.int32, (1, t), 1)
        mask = (col <= rowg) & (col >= PAD)
        dqs = []
        dcq = jnp.zeros((tq, BLK), F32)
        for h in range(nh):
            bias = (cc_ref[:, lane0 + h:lane0 + h + 1] - cr_ref[h:h + 1, :]) if use_bias else None
            qh, kh, s = _attn_scores(q_ref, k_ref, h, dq, scale, mask, bias)
            p = jnp.where(mask, jnp.exp(s - l_ref[:, h:h + 1]), 0.0)
            vh = v_ref[:, dv * h:dv * (h + 1)].astype(BF16)
            doh = do_ref[:, dv * h:dv * (h + 1)].astype(BF16)
            dp = _nt(doh, vh)
            delta = jnp.sum(p * dp, axis=1, keepdims=True)
            ds = p * (dp - delta)
            dsb = ds.astype(BF16)
            dqs.append(_nn(dsb, kh) * scale)
            dk_ref[:, dq * h:dq * (h + 1)] += _tn(dsb, qh) * scale
            dv_ref[:, dv * h:dv * (h + 1)] += _tn(p.astype(BF16), doh)
            if use_bias:
                dcq += _lane_put(jnp.sum(ds, axis=1, keepdims=True), lane0 + h)
                dck_ref[h:h + 1, :] += jnp.sum(ds, axis=0, keepdims=True)
        dq_ref[...] = jnp.concatenate(dqs, axis=1)
        if use_bias:
            dcq_ref[...] = dcq

    in_specs = [pl.BlockSpec((tq, nh * dq), lambda i: (i, qcol)),
                pl.BlockSpec((t, nh * dq), lambda i: (0, kcol)),
                pl.BlockSpec((t, nh * dv), lambda i: (0, vcol)),
                pl.BlockSpec((tq, nh * dv), lambda i: (i, docol)),
                pl.BlockSpec((tq, BLK), lambda i: (i, 0))]
    args = [q, k, v, do, lse]
    out_specs = [pl.BlockSpec((tq, nh * dq), lambda i: (i, 0)), pl.BlockSpec((t, nh * dq), lambda i: (0, 0)),
                 pl.BlockSpec((t, nh * dv), lambda i: (0, 0))]
    out_shape = [jax.ShapeDtypeStruct((t, nh * dq), F32), jax.ShapeDtypeStruct((t, nh * dq), F32),
                 jax.ShapeDtypeStruct((t, nh * dv), F32)]
    if use_bias:
        in_specs += [pl.BlockSpec((tq, BLK), lambda i: (i, 0)), pl.BlockSpec((8, t), lambda i: (0, 0))]
        args += [c_col, c_row]
        out_specs += [pl.BlockSpec((tq, BLK), lambda i: (i, 0)), pl.BlockSpec((8, t), lambda i: (0, 0))]
        out_shape += [jax.ShapeDtypeStruct((t, BLK), F32), jax.ShapeDtypeStruct((8, t), F32)]
    return pl.pallas_call(
        body, name="attn_bwd", grid=(t // tq,),
        in_specs=in_specs, out_specs=out_specs, out_shape=out_shape,
        compiler_params=_cp("arbitrary"),
    )(*args)


def fox_pre(proj, fb):
    t = proj.shape[0]
    nb = t // BLK

    def body(sm_ref, fb_ref, c_ref, cr_ref):
        x = sm_ref[...] + fb_ref[...]
        lane = lax.broadcasted_iota(jnp.int32, (1, BLK), 1)
        keep = _valid_rows(t, 0) & (lane >= SM_F) & (lane < SM_F + FOX_H)
        logf = jnp.where(keep, jnp.minimum(x, 0.0) - jnp.log(1.0 + jnp.exp(-jnp.abs(x))), 0.0)
        tri = _tri().astype(F32)
        carry = jnp.zeros((1, BLK), F32)
        for b in range(nb):
            cb = _nn_hi(tri, logf[b * BLK:(b + 1) * BLK, :]) + carry
            c_ref[b * BLK:(b + 1) * BLK, :] = cb
            carry = cb[BLK - 1:BLK, :]
        cr_ref[...] = c_ref[...].T[SM_F:SM_F + 8, :]

    return pl.pallas_call(
        body, name="fox_pre", grid=(1,),
        in_specs=[pl.BlockSpec((t, BLK), lambda i: (0, C_SM // BLK)), pl.BlockSpec((1, BLK), lambda i: (0, 0))],
        out_specs=[pl.BlockSpec((t, BLK), lambda i: (0, 0)), pl.BlockSpec((8, t), lambda i: (0, 0))],
        out_shape=[jax.ShapeDtypeStruct((t, BLK), F32), jax.ShapeDtypeStruct((8, t), F32)],
        compiler_params=_cp("arbitrary"),
    )(proj, fb)


def fox_pre_bwd(dcq, dck, proj, fb, dsm_in):
    t = proj.shape[0]
    nb = t // BLK

    def body(dcq_ref, dck_ref, sm_ref, fb_ref, din_ref, dsm_ref, dfb_ref, scr):
        triu = _tri(lower=False).astype(F32)
        carry = jnp.zeros((1, BLK), F32)
        scr[...] = jnp.concatenate([jnp.zeros((SM_F, t), F32), dck_ref[...], jnp.zeros((BLK - SM_F - 8, t), F32)], axis=0).T
        for b in range(nb - 1, -1, -1):
            blk = dcq_ref[b * BLK:(b + 1) * BLK, :] - scr[b * BLK:(b + 1) * BLK, :]
            cb = _nn_hi(triu, blk) + carry
            scr[b * BLK:(b + 1) * BLK, :] = cb
            carry = cb[0:1, :]
        x = sm_ref[...] + fb_ref[...]
        lane = lax.broadcasted_iota(jnp.int32, (1, BLK), 1)
        keep = _valid_rows(t, 0) & (lane >= SM_F) & (lane < SM_F + FOX_H)
        df = jnp.where(keep, scr[...] * _sig(-x), 0.0)
        dfb_ref[...] = jnp.sum(df, axis=0, keepdims=True)
        dsm_ref[...] = din_ref[...] + df

    full = pl.BlockSpec((t, BLK), lambda i: (0, 0))
    return pl.pallas_call(
        body, name="fox_pre_bwd", grid=(1,),
        in_specs=[full, pl.BlockSpec((8, t), lambda i: (0, 0)), pl.BlockSpec((t, BLK), lambda i: (0, C_SM // BLK)),
                  pl.BlockSpec((1, BLK), lambda i: (0, 0)), full],
        out_specs=[full, pl.BlockSpec((1, BLK), lambda i: (0, 0))],
        out_shape=[jax.ShapeDtypeStruct((t, BLK), F32), jax.ShapeDtypeStruct((1, BLK), F32)],
        scratch_shapes=[pltpu.VMEM((t, BLK), F32)],
        compiler_params=_cp("arbitrary"),
    )(dcq, dck, proj, fb, dsm_in)


def _swap_rope(x):
    lane = lax.broadcasted_iota(jnp.int32, (1, BLK), 1)
    return jnp.where((lane >= SM_KR) & (lane < SM_KR + 16), pltpu.roll(x, BLK - 16, 1),
                     jnp.where((lane >= SM_KR + 16) & (lane < SM_KR + 32), pltpu.roll(x, 16, 1), 0.0))


def _rms(x, g):
    r = lax.rsqrt(jnp.mean(x * x, axis=1, keepdims=True) + EPS)
    return r, x * r


def mla_pre(proj, qg, kvg, wq, wk, wv, cosq, sinq):
    t = proj.shape[0]
    tm = _row_tile(t)

    def body(cq_ref, ckv_ref, sm_ref, qg_ref, kvg_ref, wq_ref, wk_ref, wv_ref, cos_ref, sin_ref,
             q_ref, k_ref, v_ref, cqn_ref, ckvn_ref):
        cs, sn = cos_ref[...], sin_ref[...]
        _, xh = _rms(cq_ref[...], None)
        cqn = (xh * qg_ref[...]).astype(BF16)
        cqn_ref[...] = cqn
        qraw = _nn(cqn, wq_ref[...])
        qs = []
        for h in range(MLA_H):
            hb = qraw[:, BLK * h:BLK * (h + 1)]
            qs.append(hb * cs + _swap_rope(hb) * sn)
        q_ref[...] = jnp.concatenate(qs, axis=1).astype(BF16)
        _, kh = _rms(ckv_ref[...], None)
        ckvn = (kh * kvg_ref[...]).astype(BF16)
        ckvn_ref[...] = ckvn
        kraw = _nn(ckvn, wk_ref[...])
        v_ref[...] = _nn(ckvn, wv_ref[...]).astype(BF16)
        lane = lax.broadcasted_iota(jnp.int32, (1, BLK), 1)
        kr = sm_ref[...]
        krr = jnp.where((lane >= SM_KR) & (lane < SM_KR + MLA_ROPE), kr * cs + _swap_rope(kr) * sn, 0.0)
        k_ref[...] = jnp.concatenate([kraw[:, BLK * h:BLK * (h + 1)] + krr for h in range(MLA_H)], axis=1).astype(BF16)

    def rows(w, cb):
        return pl.BlockSpec((tm, w), lambda i: (i, cb))

    def whole(a):
        return pl.BlockSpec(a.shape, lambda i: (0, 0))

    return pl.pallas_call(
        body, name="mla_pre", grid=(t // tm,),
        in_specs=[rows(MLA_QL, C_CQ // MLA_QL), rows(MLA_KVL, C_CKV // MLA_KVL), rows(BLK, C_SM // BLK),
                  whole(qg), whole(kvg), whole(wq), whole(wk), whole(wv), rows(BLK, 0), rows(BLK, 0)],
        out_specs=[rows(512, 0), rows(512, 0), rows(256, 0), rows(MLA_QL, 0), rows(MLA_KVL, 0)],
        out_shape=[jax.ShapeDtypeStruct((t, 512), BF16), jax.ShapeDtypeStruct((t, 512), BF16),
                   jax.ShapeDtypeStruct((t, 256), BF16), jax.ShapeDtypeStruct((t, MLA_QL), BF16),
                   jax.ShapeDtypeStruct((t, MLA_KVL), BF16)],
        compiler_params=_cp("arbitrary"),
    )(proj, proj, proj, qg, kvg, wq, wk, wv, cosq, sinq)


def mla_pre_bwd(dq, dk, dv, proj, cqn, ckvn, qg, kvg, wq, wk, wv, cosq, sinq, dsm_in):
    t = proj.shape[0]
    tm = _row_tile(t)

    def body(dq_ref, dk_ref, dv_ref, cq_ref, ckv_ref, cqn_ref, ckvn_ref, qg_ref, kvg_ref, wq_ref, wk_ref, wv_ref,
             cos_ref, sin_ref, din_ref, dcq_ref, dckv_ref, dsm_ref, dwq_ref, dwk_ref, dwv_ref, dqg_ref, dkvg_ref):
        i = pl.program_id(0)

        @pl.when(i == 0)
        def _():
            for r in (dwq_ref, dwk_ref, dwv_ref, dqg_ref, dkvg_ref):
                r[...] = jnp.zeros_like(r)

        cs, sn = cos_ref[...], sin_ref[...]
        lane = lax.broadcasted_iota(jnp.int32, (1, BLK), 1)

        def unrope(dy):
            return dy * cs + _swap_rope(dy * sn)

        dqp = jnp.concatenate([unrope(dq_ref[:, BLK * h:BLK * (h + 1)]) for h in range(MLA_H)], axis=1).astype(BF16)
        dwq_ref[...] += _tn(cqn_ref[...], dqp)
        dcqn = _nt(dqp, wq_ref[...])
        r, xh = _rms(cq_ref[...], None)
        dqg_ref[...] += jnp.sum(dcqn * xh, axis=0, keepdims=True)
        dxh = dcqn * qg_ref[...]
        dcq_ref[...] = r * (dxh - xh * jnp.mean(dxh * xh, axis=1, keepdims=True))

        dkn, dkr = [], jnp.zeros((tm, BLK), F32)
        for h in range(MLA_H):
            blk = dk_ref[:, BLK * h:BLK * (h + 1)]
            dkn.append(jnp.where(lane < MLA_NOPE, blk, 0.0))
            dkr += jnp.where((lane >= SM_KR) & (lane < SM_KR + MLA_ROPE), blk, 0.0)
        dknb = jnp.concatenate(dkn, axis=1).astype(BF16)
        dvb = dv_ref[...].astype(BF16)
        ckvn = ckvn_ref[...]
        dwk_ref[...] += _tn(ckvn, dknb)
        dwv_ref[...] += _tn(ckvn, dvb)
        dckvn = _nt(dknb, wk_ref[...]) + _nt(dvb, wv_ref[...])
        r2, kh = _rms(ckv_ref[...], None)
        dkvg_ref[...] += jnp.sum(dckvn * kh, axis=0, keepdims=True)
        dkh = dckvn * kvg_ref[...]
        dckv_ref[...] = r2 * (dkh - kh * jnp.mean(dkh * kh, axis=1, keepdims=True))
        dsm_ref[...] = din_ref[...] + jnp.where((lane >= SM_KR) & (lane < SM_KR + MLA_ROPE), unrope(dkr), 0.0)

    def rows(w, cb):
        return pl.BlockSpec((tm, w), lambda i: (i, cb))

    def whole(a):
        return pl.BlockSpec(a.shape, lambda i: (0, 0))

    def wshape(a):
        return jax.ShapeDtypeStruct(a.shape, F32)

    return pl.pallas_call(
        body, name="mla_pre_bwd", grid=(t // tm,),
        in_specs=[rows(512, 0), rows(512, 0), rows(256, 0), rows(MLA_QL, C_CQ // MLA_QL), rows(MLA_KVL, C_CKV // MLA_KVL),
                  rows(MLA_QL, 0), rows(MLA_KVL, 0), whole(qg), whole(kvg), whole(wq), whole(wk), whole(wv),
                  rows(BLK, 0), rows(BLK, 0), rows(BLK, 0)],
        out_specs=[rows(MLA_QL, 0), rows(MLA_KVL, 0), rows(BLK, 0), whole(wq), whole(wk), whole(wv), whole(qg), whole(kvg)],
        out_shape=[jax.ShapeDtypeStruct((t, MLA_QL), F32), jax.ShapeDtypeStruct((t, MLA_KVL), F32),
                   jax.ShapeDtypeStruct((t, BLK), F32), wshape(wq), wshape(wk), wshape(wv), wshape(qg), wshape(kvg)],
        compiler_params=_cp("arbitrary"),
    )(dq, dk, dv, proj, proj, cqn, ckvn, qg, kvg, wq, wk, wv, cosq, sinq, dsm_in)


def adamw(w, m, v, g=None, recv=None):
    shape = w.shape
    c = shape[-1]
    from_recv = recv is not None
    nl = len(recv) if from_recv else 1
    rws = w.size // c // nl
    tr = rws
    for d in (1024, 512, 352, 256, 128, 64, 32, 16, 8):
        if rws % d == 0 and d * c * 4 <= (2 << 20):
            tr = d
            break
    nt = rws // tr
    w2, m2, v2 = (a.reshape(nl, rws, c) for a in (w, m, v))
    gin = [r.reshape(N_DEV, rws, c) for r in recv] if from_recv else [g.reshape(1, rws, c)]

    def body(w_ref, m_ref, v_ref, *rest):
        g_refs, outs = rest[:len(gin)], rest[len(gin):]
        if from_recv:
            g_out, outs = outs[0], outs[1:]
            for li in range(nl):
                @pl.when(pl.program_id(0) == li)
                def _(li=li):
                    gg = g_refs[li][0].astype(F32)
                    for s in range(1, N_DEV):
                        gg = gg + g_refs[li][s].astype(F32)
                    g_out[...] = gg
            gg = g_out[...]
        else:
            gg = g_refs[0][...]
        d_ref, nm_ref, nv_ref = outs
        nm = B1 * m_ref[...] + (1.0 - B1) * gg
        nv = B2 * v_ref[...] + (1.0 - B2) * (gg * gg)
        mh = nm / (1.0 - B1 ** STEP)
        vh = nv / (1.0 - B2 ** STEP)
        d_ref[...] = -LR * (mh / (jnp.sqrt(vh) + AEPS) + WD * w_ref[...])
        nm_ref[...] = nm
        nv_ref[...] = nv

    row = pl.BlockSpec((None, tr, c), lambda l, i: (l, i, 0))
    if from_recv:
        gspecs = [pl.BlockSpec((N_DEV, tr, c), lambda l, i, li=li: (0, jnp.where(l == li, i, 0), 0)) for li in range(nl)]
    else:
        gspecs = [row]
    nout = 4 if from_recv else 3
    outs = pl.pallas_call(
        body, name="adamw", grid=(nl, nt),
        in_specs=[row, row, row] + gspecs, out_specs=[row] * nout,
        out_shape=[jax.ShapeDtypeStruct((nl, rws, c), F32)] * nout,
        compiler_params=_cp("arbitrary", "arbitrary"),
    )(w2, m2, v2, *gin)
    return tuple(o.reshape(shape) for o in outs)


def sum_slots(recv):
    _, r, c = recv.shape

    def body(r_ref, o_ref):
        gg = r_ref[0].astype(F32)
        for s in range(1, N_DEV):
            gg = gg + r_ref[s].astype(F32)
        o_ref[...] = gg

    return pl.pallas_call(
        body, name="sum_slots", grid=(1,),
        in_specs=[pl.BlockSpec((N_DEV, r, c), lambda i: (0, 0, 0))],
        out_specs=pl.BlockSpec((r, c), lambda i: (0, 0)),
        out_shape=jax.ShapeDtypeStruct((r, c), F32),
        compiler_params=_cp("arbitrary"),
    )(recv)


_FLIPS = [(0, 0, 1), (0, 1, 0), (0, 1, 1), (1, 0, 0), (1, 0, 1), (1, 1, 0), (1, 1, 1)]
_ANY = pl.BlockSpec(memory_space=pl.ANY)


def _mesh_place():
    x, y, c = lax.axis_index("x"), lax.axis_index("y"), lax.axis_index("c")
    me = 4 * x + 2 * y + c
    peers = [((x + fx) % 2, (y + fy) % 2, (c + fc) % 2) for fx, fy, fc in _FLIPS]
    return me, peers


def exchange(jobs, name):
    n_in, plan, out_shape = 0, [], []
    args = []
    for arr, mode, nl in jobs:
        if mode == "gather":
            args.append(arr)
            for l in range(nl):
                plan.append((mode, n_in, l, len(out_shape)))
                out_shape.append(jax.ShapeDtypeStruct((N_DEV,) + arr.shape[1:], arr.dtype))
            n_in += 1
        else:
            for l in range(nl):
                args.append(arr[l])
                plan.append((mode, n_in, l, len(out_shape)))
                n_in += 1
            out_shape.append(jax.ShapeDtypeStruct((N_DEV, nl) + arr[0].shape[1:], arr[0].dtype))
    ncopy = len(plan)

    def body(*refs):
        ins, outs = refs[:n_in], refs[n_in:n_in + len(out_shape)]
        send_sems, recv_sems, loc_sems = refs[n_in + len(out_shape):]
        me, peers = _mesh_place()
        ids = [4 * p[0] + 2 * p[1] + p[2] for p in peers]

        def parts(j, slot_src, slot_dst):
            mode, ii, l, oi = plan[j]
            if mode == "gather":
                return ins[ii].at[l], outs[oi].at[slot_dst]
            return ins[ii].at[slot_src], outs[oi].at[slot_dst, l]

        started = []
        for j in range(ncopy):
            src, dst = parts(j, me, me)
            loc = pltpu.make_async_copy(src, dst, loc_sems.at[j])
            loc.start()
            started.append(loc)
            for k in range(N_DEV - 1):
                src, dst = parts(j, ids[k], me)
                cp = pltpu.make_async_remote_copy(src_ref=src, dst_ref=dst, send_sem=send_sems.at[j, k],
                                                  recv_sem=recv_sems.at[j, k], device_id=peers[k],
                                                  device_id_type=pl.DeviceIdType.MESH)
                cp.start()
        for j in range(ncopy):
            started[j].wait()
            for k in range(N_DEV - 1):
                src, dst = parts(j, ids[k], ids[k])
                cp = pltpu.make_async_remote_copy(src_ref=src, dst_ref=dst, send_sem=send_sems.at[j, k],
                                                  recv_sem=recv_sems.at[j, k], device_id=peers[k],
                                                  device_id_type=pl.DeviceIdType.MESH)
                cp.wait_send()
                cp.wait_recv()

    return pl.pallas_call(
        body, name=name,
        in_specs=[_ANY] * n_in, out_specs=[_ANY] * len(out_shape), out_shape=out_shape,
        scratch_shapes=[pltpu.SemaphoreType.DMA((ncopy, N_DEV - 1)), pltpu.SemaphoreType.DMA((ncopy, N_DEV - 1)),
                        pltpu.SemaphoreType.DMA((ncopy,))],
    )(*args)


_HBM = pl.BlockSpec(memory_space=pltpu.HBM)
_SEMS = pl.BlockSpec(memory_space=pltpu.SEMAPHORE)
_EFFECT = pltpu.SideEffectType.DATAFLOW_SIDE_EFFECTING


def _plan_jobs(jobs):
    srcs, plan, lands = [], [], []
    for arr, mode, nl, key in jobs:
        if mode == "gather":
            for l in range(nl):
                plan.append((key(l), mode, len(srcs), l, len(lands)))
                lands.append(jax.ShapeDtypeStruct((N_DEV,) + arr.shape[1:], arr.dtype))
            srcs.append(arr)
        else:
            for l in range(nl):
                plan.append((key(l), mode, len(srcs), l, len(lands)))
                srcs.append(arr[l])
            lands.append(jax.ShapeDtypeStruct((N_DEV, nl) + arr[0].shape[1:], arr[0].dtype))
    order = sorted(range(len(plan)), key=lambda j: plan[j][0])
    return srcs, [plan[j][1:] for j in order], lands


def _copy_ends(plan, srcs, lands, j, slot_src, slot_dst):
    mode, si, l, li = plan[j]
    if mode == "gather":
        return srcs[si].at[l], lands[li].at[slot_dst]
    return srcs[si].at[slot_src], lands[li].at[slot_dst, l]


def exchange_start(jobs, name):
    srcs, plan, lands = _plan_jobs(jobs)
    ns, nd, ncopy = len(srcs), len(lands), len(plan)

    def fill_body(*refs):
        ins, outs, sems = refs[:ns], refs[ns:ns + nd], refs[ns + nd]
        me, _ = _mesh_place()
        cps = []
        for j in range(ncopy):
            src, dst = _copy_ends(plan, ins, outs, j, me, me)
            cps.append(pltpu.make_async_copy(src, dst, sems.at[j]))
            cps[-1].start()
        for cp in cps:
            cp.wait()

    filled = pl.pallas_call(
        fill_body, name=name + "_own", in_specs=[_ANY] * ns, out_specs=[_ANY] * nd, out_shape=lands,
        scratch_shapes=[pltpu.SemaphoreType.DMA((ncopy,))],
    )(*srcs)

    def body(*refs):
        ins, outs = refs[:ns], refs[ns:ns + nd]
        send_sems, recv_sems = refs[ns + nd], refs[ns + nd + 1]
        token = refs[-1]
        me, peers = _mesh_place()
        ids = [4 * p[0] + 2 * p[1] + p[2] for p in peers]
        for j in range(ncopy):
            for k in range(N_DEV - 1):
                src, dst = _copy_ends(plan, ins, outs, j, ids[k], me)
                pltpu.make_async_remote_copy(src_ref=src, dst_ref=dst, send_sem=send_sems.at[j * (N_DEV - 1) + k],
                                             recv_sem=recv_sems.at[j * (N_DEV - 1) + k], device_id=peers[k],
                                             device_id_type=pl.DeviceIdType.MESH).start()
        token[...] = jnp.zeros_like(token)

    thru = [pltpu.HBM(a.shape, a.dtype) for a in srcs] + [pltpu.HBM(s.shape, s.dtype) for s in lands]
    res = pl.pallas_call(
        body, name=name,
        in_specs=[_HBM] * (ns + nd),
        out_specs=(_SEMS, _SEMS, *[_HBM] * (ns + nd), pl.BlockSpec(memory_space=pltpu.VMEM)),
        out_shape=(pltpu.SemaphoreType.DMA((ncopy * (N_DEV - 1),)), pltpu.SemaphoreType.DMA((ncopy * (N_DEV - 1),)), *thru,
                   jax.ShapeDtypeStruct((8, BLK), F32)),
        input_output_aliases={i: 2 + i for i in range(ns + nd)},
        compiler_params=pltpu.CompilerParams(has_side_effects=_EFFECT),
    )(*[pltpu.with_memory_space_constraint(a, pltpu.HBM) for a in list(srcs) + list(filled)])
    return dict(send=res[0], recv=res[1], srcs=list(res[2:2 + ns]), lands=list(res[2 + ns:2 + ns + nd]),
                token=res[-1], plan=plan)


def exchange_wait(hd, land_ids, name, after):
    plan = hd["plan"]
    js = [j for j in range(len(plan)) if plan[j][3] in land_ids]
    src_ids = sorted({plan[j][1] for j in js})
    ns, nd = len(src_ids), len(land_ids)

    def body(*refs):
        s_refs = {si: refs[p] for p, si in enumerate(src_ids)}
        l_refs = {li: refs[ns + p] for p, li in enumerate(land_ids)}
        send_sems, recv_sems = refs[ns + nd], refs[ns + nd + 1]
        _, peers = _mesh_place()
        ids = [4 * p[0] + 2 * p[1] + p[2] for p in peers]
        for j in js:
            for k in range(N_DEV - 1):
                src, dst = _copy_ends(plan, s_refs, l_refs, j, ids[k], ids[k])
                cp = pltpu.make_async_remote_copy(src_ref=src, dst_ref=dst, send_sem=send_sems.at[j * (N_DEV - 1) + k],
                                                  recv_sem=recv_sems.at[j * (N_DEV - 1) + k], device_id=peers[k],
                                                  device_id_type=pl.DeviceIdType.MESH)
                cp.wait_send()
                cp.wait_recv()

    ops = [hd["srcs"][si] for si in src_ids] + [hd["lands"][li] for li in land_ids]
    res = pl.pallas_call(
        body, name=name,
        in_specs=[_HBM] * (ns + nd) + [_SEMS, _SEMS, _ANY],
        out_specs=[_HBM] * (ns + nd),
        out_shape=[pltpu.HBM(a.shape, a.dtype) for a in ops],
        input_output_aliases={i: i for i in range(ns + nd)},
        compiler_params=pltpu.CompilerParams(has_side_effects=_EFFECT),
    )(*ops, hd["send"], hd["recv"], after)
    for p, si in enumerate(src_ids):
        hd["srcs"][si] = res[p]
    return list(res[ns:])


def _pad_cols(a, n):
    return jnp.pad(a, ((0, 0),) * (a.ndim - 1) + ((0, n - a.shape[-1]),))


def w_in_to_padded(w):
    z = lambda n: jnp.zeros(w.shape[:-1] + (n,), w.dtype)
    return jnp.concatenate([
        w[..., 0:1280], w[..., 1288:2056], w[..., 2060:2316], w[..., 2316:2444],
        w[..., 1280:1288], w[..., 2056:2060], z(SM_KR - SM_F - FOX_H), w[..., 2444:2476], z(BLK - SM_KR - MLA_ROPE)], axis=-1)


def w_in_from_padded(g):
    s = C_SM
    return jnp.concatenate([
        g[..., 0:1280], g[..., s + SM_DT:s + SM_DT + 8], g[..., 1280:2048], g[..., s + SM_F:s + SM_F + 4],
        g[..., 2048:2304], g[..., 2304:2432], g[..., s + SM_KR:s + SM_KR + MLA_ROPE]], axis=-1)


def _unshard_cols(gth):
    n, r, c = gth.shape
    return jnp.transpose(gth, (1, 0, 2)).reshape(r, n * c)


def _shard_cols(full):
    r, nc = full.shape
    return jnp.transpose(full.reshape(r, N_DEV, nc // N_DEV), (1, 0, 2))


def mla_weights(uq_g, ukv_g):
    uq = _unshard_cols(uq_g)
    dqh = MLA_NOPE + MLA_ROPE
    wq = jnp.concatenate([_pad_cols(uq[:, dqh * h:dqh * (h + 1)], BLK) for h in range(MLA_H)], axis=1)
    wk = jnp.concatenate([_pad_cols(ukv_g[2 * h], BLK) for h in range(MLA_H)], axis=1)
    wv = jnp.concatenate([ukv_g[2 * h + 1] for h in range(MLA_H)], axis=1)
    return wq, wk, wv


def mla_weight_grads(dwq, dwk, dwv):
    dqh = MLA_NOPE + MLA_ROPE
    duq = _shard_cols(jnp.concatenate([dwq[:, BLK * h:BLK * h + dqh] for h in range(MLA_H)], axis=1))
    parts = []
    for h in range(MLA_H):
        parts += [dwk[:, BLK * h:BLK * h + MLA_NOPE], dwv[:, MLA_V * h:MLA_V * (h + 1)]]
    return duq, jnp.stack(parts, axis=0)


def rope_tables(t):
    pos = (jnp.arange(t, dtype=jnp.int32) - PAD).astype(F32)
    inv_freq = 1.0 / (10000.0 ** (jnp.arange(0, MLA_ROPE, 2, dtype=F32) / MLA_ROPE))
    ang = pos[:, None] * inv_freq[None, :]
    cos, sin = jnp.cos(ang), jnp.sin(ang)
    one, zero = jnp.ones((t, SM_KR), F32), jnp.zeros((t, SM_KR), F32)
    tail = BLK - SM_KR - MLA_ROPE
    cosq = jnp.concatenate([one, cos, cos, jnp.ones((t, tail), F32)], axis=1)
    sinq = jnp.concatenate([zero, -sin, sin, jnp.zeros((t, tail), F32)], axis=1)
    return cosq, sinq


def _lanes(v, off=0):
    return jnp.pad(v.astype(F32), (off, BLK - off - v.shape[0]))[None, :]


def layer_fwd(h, hb, getw, tabs):
    sv = {"h0": h, "h0b": hb}
    W = dict(getw("ffn1", hb))
    u, v, a = ffn_up(hb, W["g1"], W["u1"])
    r1, h1, h1b = mm_res_ln(a, W["d1"], h, ALPHA, 0.5, W["ln1_g"], W["ln1_b"])
    sv.update(u1=u, v1=v, a1=a, r1=r1, h1=h1, h1b=h1b)
    W.update(getw("mix", h1b))
    proj = mm_nn(h1b, W["w_in"])
    xa = conv_fwd(proj, W["conv_w"], W["conv_b"])
    y_ssd, sprev = ssd_fwd(xa, proj, W["dtb"], W["alog"], W["dskip"], W["normg"])
    c_col, c_row = fox_pre(proj, W["fb"])
    y_fox, lse_f = attn_fwd(proj, proj, proj, C_FQ // 256, C_FK // 256, C_FV // 256, FOX_H, FOX_DH, FOX_DH,
                            FOX_DH ** -0.5, c_col, c_row, SM_F)
    q, k, vv, cqn, ckvn = mla_pre(proj, W["qg"], W["kvg"], W["wq"], W["wk"], W["wv"], *tabs)
    y_mla, lse_m = attn_fwd(q, k, vv, 0, 0, 0, MLA_H, BLK, MLA_V, (MLA_NOPE + MLA_ROPE) ** -0.5)
    mixcat = jnp.concatenate([y_ssd, y_fox, y_mla], axis=1)
    r2, h2, h2b = mm_res_ln(mixcat[None], W["w_out"][None], h1, ALPHA, 1.0, W["ln2_g"], W["ln2_b"])
    sv.update(proj=proj, xa=xa, sprev=sprev, c_col=c_col, c_row=c_row, lse_f=lse_f, q=q, k=k, v=vv, cqn=cqn, ckvn=ckvn,
              lse_m=lse_m, mixcat=mixcat, r2=r2, h2=h2, h2b=h2b)
    W.update(getw("ffn2", h2b))
    u, v, a = ffn_up(h2b, W["g2"], W["u2"])
    r3, h3, h3b = mm_res_ln(a, W["d2"], h2, ALPHA, 0.5, W["ln3_g"], W["ln3_b"])
    sv.update(u2=u, v2=v, a2=a, r3=r3, W=W)
    return h3, h3b, sv


def ffn_bwd(parts, r, gamma, hb_in, u, v, a, wg, wu, wd, after=None):
    dr, dfb, dg, db = ln_bwd(parts, r, gamma, 0.5, after)
    du, dv = ffn_dact(dfb, wd, u, v)
    dh = mm_nt_reduce([(du, wg), (dv, wu)], D)
    gr = dict(d=mm_tn(a, dfb[None]), g=mm_tn(hb_in[None], du), u=mm_tn(hb_in[None], dv), ln_g=dg, ln_b=db)
    return dr, dh, gr


def layer_bwd(parts, sv, emit, tabs, after):
    G = {}
    W = sv["W"]
    dr3, dh2f, g2 = ffn_bwd(parts, sv["r3"], W["ln3_g"], sv["h2b"], sv["u2"], sv["v2"], sv["a2"], W["g2"], W["u2"], W["d2"],
                            after)
    G.update(g2=g2["g"], u2=g2["u"], d2=g2["d"], ln3_g=g2["ln_g"], ln3_b=g2["ln_b"])
    tok = emit("ffn2", G)
    dr2, dmixb, G["ln2_g"], G["ln2_b"] = ln_bwd([(dr3, ALPHA), (dh2f, 1.0)], sv["r2"], W["ln2_g"], 1.0, tok)
    dmc = mm_nt_reduce([(dmixb[None], W["w_out"][None])], D)
    G["w_out"] = mm_tn(sv["mixcat"][None], dmixb[None])[0]
    proj = sv["proj"]
    dxa, dz, dsm, G["normg"], G["dskip"], G["alog"], G["dtb"] = ssd_bwd(
        dmc, sv["xa"], proj, sv["sprev"], W["dtb"], W["alog"], W["dskip"], W["normg"])
    dxbc, G["conv_w"], G["conv_b"] = conv_bwd(dxa, proj, W["conv_w"], W["conv_b"])
    dfq, dfk, dfv, dcq, dck = attn_bwd(proj, proj, proj, dmc, sv["lse_f"], C_FQ // 256, C_FK // 256, C_FV // 256, 2,
                                       FOX_H, FOX_DH, FOX_DH, FOX_DH ** -0.5, sv["c_col"], sv["c_row"], SM_F)
    dsm, G["fb"] = fox_pre_bwd(dcq, dck, proj, W["fb"], dsm)
    dq, dk, dv = attn_bwd(sv["q"], sv["k"], sv["v"], dmc, sv["lse_m"], 0, 0, 0, 3, MLA_H, BLK, MLA_V,
                          (MLA_NOPE + MLA_ROPE) ** -0.5)
    dcql, dckv, dsm, G["wq"], G["wk"], G["wv"], G["qg"], G["kvg"] = mla_pre_bwd(
        dq, dk, dv, proj, sv["cqn"], sv["ckvn"], W["qg"], W["kvg"], W["wq"], W["wk"], W["wv"], *tabs, dsm)
    dproj = jnp.concatenate([dz, dxbc, dfq, dfk, dfv, dcql, dckv, dsm], axis=1).astype(BF16)
    dh1p = mm_nt_reduce([(dproj[None], W["w_in"][None])], D)
    G["w_in"] = mm_tn(sv["h1b"][None], dproj[None])[0]
    tok = emit("mix", G)
    dr1, dh0f, g1 = ffn_bwd([(dr2, ALPHA), (dh1p, 1.0)], sv["r1"], W["ln1_g"], sv["h0b"], sv["u1"], sv["v1"], sv["a1"],
                            W["g1"], W["u1"], W["d1"], tok)
    G.update(g1=g1["g"], u1=g1["u"], d1=g1["d"], ln1_g=g1["ln_g"], ln1_b=g1["ln_b"])
    tok = emit("ffn1", G)
    return [(dr1, ALPHA), (dh0f, 1.0)], G, tok


def local_step(x, target, meta_full, getw, emit):
    t = x.shape[0] + BLK
    tabs = rope_tables(t)
    h, hb = build_h0(meta_full, x)
    saved = []
    for l in range(NL):
        h, hb, sv = layer_fwd(h, hb, functools.partial(getw, l), tabs)
        saved.append(sv)
    dy, loss = loss_head(h, target)
    parts = [(dy, 1.0)]
    grads = [None] * NL
    tok = None
    for l in range(NL - 1, -1, -1):
        parts, grads[l], tok = layer_bwd(parts, saved[l], functools.partial(emit, l), tabs, tok)
    gx, gmeta = final_add(parts[0][0], parts[1][0], tok)
    return loss, gx, gmeta, grads


_SMALL = ["ln1_g", "ln1_b", "ln2_g", "ln2_b", "ln3_g", "ln3_b", "conv_b", "ssd_norm_g", "mla_q_norm_g",
          "mla_kv_norm_g", "dt_bias", "a_log", "d_skip", "fox_f_b"]
_SMALL_ROWS = 16
_BIG = ["ffn1_w_gate", "ffn1_w_up", "ffn1_w_down", "w_in", "conv_w", "mla_w_uq", "mla_w_ukv", "w_out",
        "ffn2_w_gate", "ffn2_w_up", "ffn2_w_down"]
_NAMES = ["meta", "ffn1_w_gate", "ffn1_w_up", "ffn1_w_down", "ln1_g", "ln1_b", "w_in", "conv_w", "conv_b", "dt_bias",
          "a_log", "d_skip", "ssd_norm_g", "fox_f_b", "mla_q_norm_g", "mla_w_uq", "mla_kv_norm_g", "mla_w_ukv", "w_out",
          "ln2_g", "ln2_b", "ffn2_w_gate", "ffn2_w_up", "ffn2_w_down", "ln3_g", "ln3_b"]


def pack_small(p):
    rows = []
    for l in range(NL):
        for n in _SMALL:
            rows.append(_pad_cols(p[n][l][None, :].astype(F32), D))
        rows.append(jnp.zeros((_SMALL_ROWS - len(_SMALL), D), F32))
    return jnp.concatenate(rows, axis=0)


def unpack_small(a, like):
    out = {}
    for i, n in enumerate(_SMALL):
        out[n] = jnp.stack([a[l * _SMALL_ROWS + i, :like[n].shape[1]] for l in range(NL)], axis=0)
    return out


_STAGES = {"ffn1": ["ffn1_w_gate", "ffn1_w_up", "ffn1_w_down"],
           "mix": ["w_in", "conv_w", "mla_w_uq", "mla_w_ukv", "w_out"],
           "ffn2": ["ffn2_w_gate", "ffn2_w_up", "ffn2_w_down"]}


def stage_weights(l, stage, g, rep):
    if stage != "mix":
        i = stage[-1]
        return {"g" + i: g[f"ffn{i}_w_gate"], "u" + i: g[f"ffn{i}_w_up"], "d" + i: g[f"ffn{i}_w_down"],
                "ln1_g" if i == "1" else "ln3_g": rep["ln1_g" if i == "1" else "ln3_g"][l][None, :],
                "ln1_b" if i == "1" else "ln3_b": rep["ln1_b" if i == "1" else "ln3_b"][l][None, :]}
    W = {}
    W["w_in"] = g["w_in"].reshape(D, N_INP)
    W["w_out"] = g["w_out"].reshape(D, D)
    W["wq"], W["wk"], W["wv"] = mla_weights(g["mla_w_uq"], g["mla_w_ukv"])
    W["conv_w"] = _unshard_cols(g["conv_w"])
    for k in ("ln2_g", "ln2_b", "conv_b"):
        W[k] = rep[k][l][None, :]
    W["normg"] = rep["ssd_norm_g"][l][None, :]
    W["qg"] = rep["mla_q_norm_g"][l][None, :]
    W["kvg"] = rep["mla_kv_norm_g"][l][None, :]
    W["dtb"] = _lanes(rep["dt_bias"][l], SM_DT)
    W["alog"] = _lanes(rep["a_log"][l], SM_DT)
    W["dskip"] = _lanes(rep["d_skip"][l], SM_DT)
    W["fb"] = _lanes(rep["fox_f_b"][l], SM_F)
    return W


def small_grads(G):
    return {"ln1_g": G["ln1_g"][0], "ln1_b": G["ln1_b"][0], "ln2_g": G["ln2_g"][0], "ln2_b": G["ln2_b"][0],
            "ln3_g": G["ln3_g"][0], "ln3_b": G["ln3_b"][0], "conv_b": G["conv_b"][0], "ssd_norm_g": G["normg"][0],
            "mla_q_norm_g": G["qg"][0], "mla_kv_norm_g": G["kvg"][0], "dt_bias": G["dtb"][0, :SSD_H],
            "a_log": G["alog"][0, :SSD_H], "d_skip": G["dskip"][0, :SSD_H], "fox_f_b": G["fb"][0, SM_F:SM_F + FOX_H]}


def big_grads(G, stage):
    if stage != "mix":
        i = stage[-1]
        return {f"ffn{i}_w_gate": G["g" + i], f"ffn{i}_w_up": G["u" + i], f"ffn{i}_w_down": G["d" + i]}
    duq, dukv = mla_weight_grads(G["wq"], G["wk"], G["wv"])
    return {"w_in": G["w_in"].reshape(N_DEV, D // N_DEV, N_INP), "w_out": G["w_out"].reshape(N_DEV, D // N_DEV, D),
            "mla_w_uq": duq, "mla_w_ukv": dukv, "conv_w": _shard_cols(G["conv_w"])}


def kernel(x, meta, ffn1_w_gate, ffn1_w_up, ffn1_w_down, ln1_g, ln1_b, w_in, conv_w, conv_b, dt_bias, a_log, d_skip, ssd_norm_g, fox_f_b, mla_q_norm_g, mla_w_uq, mla_kv_norm_g, mla_w_ukv, w_out, ln2_g, ln2_b, ffn2_w_gate, ffn2_w_up, ffn2_w_down, ln3_g, ln3_b, loss_target, m_meta, m_ffn1_w_gate, m_ffn1_w_up, m_ffn1_w_down, m_ln1_g, m_ln1_b, m_w_in, m_conv_w, m_conv_b, m_dt_bias, m_a_log, m_d_skip, m_ssd_norm_g, m_fox_f_b, m_mla_q_norm_g, m_mla_w_uq, m_mla_kv_norm_g, m_mla_w_ukv, m_w_out, m_ln2_g, m_ln2_b, m_ffn2_w_gate, m_ffn2_w_up, m_ffn2_w_down, m_ln3_g, m_ln3_b, v_meta, v_ffn1_w_gate, v_ffn1_w_up, v_ffn1_w_down, v_ln1_g, v_ln1_b, v_w_in, v_conv_w, v_conv_b, v_dt_bias, v_a_log, v_d_skip, v_ssd_norm_g, v_fox_f_b, v_mla_q_norm_g, v_mla_w_uq, v_mla_kv_norm_g, v_mla_w_ukv, v_w_out, v_ln2_g, v_ln2_b, v_ffn2_w_gate, v_ffn2_w_up, v_ffn2_w_down, v_ln3_g, v_ln3_b):
    vals = (meta, ffn1_w_gate, ffn1_w_up, ffn1_w_down, ln1_g, ln1_b, w_in, conv_w, conv_b, dt_bias, a_log, d_skip, ssd_norm_g, fox_f_b, mla_q_norm_g, mla_w_uq, mla_kv_norm_g, mla_w_ukv, w_out, ln2_g, ln2_b, ffn2_w_gate, ffn2_w_up, ffn2_w_down, ln3_g, ln3_b)
    moms = (m_meta, m_ffn1_w_gate, m_ffn1_w_up, m_ffn1_w_down, m_ln1_g, m_ln1_b, m_w_in, m_conv_w, m_conv_b, m_dt_bias, m_a_log, m_d_skip, m_ssd_norm_g, m_fox_f_b, m_mla_q_norm_g, m_mla_w_uq, m_mla_kv_norm_g, m_mla_w_ukv, m_w_out, m_ln2_g, m_ln2_b, m_ffn2_w_gate, m_ffn2_w_up, m_ffn2_w_down, m_ln3_g, m_ln3_b)
    vars_ = (v_meta, v_ffn1_w_gate, v_ffn1_w_up, v_ffn1_w_down, v_ln1_g, v_ln1_b, v_w_in, v_conv_w, v_conv_b, v_dt_bias, v_a_log, v_d_skip, v_ssd_norm_g, v_fox_f_b, v_mla_q_norm_g, v_mla_w_uq, v_mla_kv_norm_g, v_mla_w_ukv, v_w_out, v_ln2_g, v_ln2_b, v_ffn2_w_gate, v_ffn2_w_up, v_ffn2_w_down, v_ln3_g, v_ln3_b)
    P = dict(zip(_NAMES, vals))
    M = dict(zip(_NAMES, moms))
    V = dict(zip(_NAMES, vars_))
    me = 4 * lax.axis_index("x") + 2 * lax.axis_index("y") + lax.axis_index("c")

    send = {n: P[n].astype(BF16) for n in _BIG if n not in ("w_in", "conv_w")}
    send["w_in"] = w_in_to_padded(P["w_in"]).astype(BF16)
    send["conv_w"] = P["conv_w"]
    stage_of = {n: s for s, names in _STAGES.items() for n in names}
    rank = {s: i for i, s in enumerate(_STAGES)}
    jobs = [(P["meta"][None], "gather", 1, lambda l: -1)]
    jobs += [(send[n], "gather", NL, lambda l, n=n: len(rank) * l + rank[stage_of[n]]) for n in _BIG]
    hg = exchange_start(jobs, "gather_start")
    land_of = {(n, l): 1 + NL * i + l for i, n in enumerate(_BIG) for l in range(NL)}
    meta_full = _unshard_cols(exchange_wait(hg, [0], "gather_wait_meta", hg["token"])[0])

    def getw(l, stage, after):
        names = _STAGES[stage]
        lands = exchange_wait(hg, [land_of[(n, l)] for n in names], f"gather_wait_{l}_{stage}", after)
        return stage_weights(l, stage, dict(zip(names, lands)), P)

    sent = {}

    def emit(l, stage, G):
        bg = big_grads(G, stage)
        sent[(l, stage)] = exchange_start([([bg[n]], "scatter", 1, lambda _: 0) for n in _STAGES[stage]],
                                          f"scatter_start_{l}_{stage}")
        return sent[(l, stage)]["token"]

    loss, gx, gmeta, grads = local_step(x[0], loss_target[0], meta_full, getw, emit)

    small = jnp.concatenate([pack_small({n: jnp.stack([small_grads(g)[n] for g in grads]) for n in _SMALL}), gmeta], axis=0)
    gsmall = sum_slots(exchange([(small[None], "gather", 1)], "gather_small_grads")[0].reshape(N_DEV, -1, D))

    out = {}
    after = gsmall
    for stage in ("ffn2", "mix", "ffn1"):
        names = _STAGES[stage]
        recv = [exchange_wait(sent[(l, stage)], list(range(len(names))), f"scatter_wait_{l}_{stage}", after)
                for l in range(NL - 1, -1, -1)][::-1]
        for i, n in enumerate(names):
            if n == "w_in":
                g = jnp.stack([w_in_from_padded(sum_slots(recv[l][i].reshape(N_DEV, D // N_DEV, N_INP))) for l in range(NL)])
                out[n] = (g,) + adamw(P[n], M[n], V[n], g=g)
            else:
                out[n] = adamw(P[n], M[n], V[n], recv=[recv[l][i] for l in range(NL)])
        after = out[names[-1]][1]
    gm = lax.dynamic_slice(gsmall[NL * _SMALL_ROWS:], (0, me * (D // N_DEV)), (N_META, D // N_DEV))
    out["meta"] = (gm,) + adamw(P["meta"], M["meta"], V["meta"], g=gm)
    gs = gsmall[:NL * _SMALL_ROWS]
    sd, sm_, sv_ = adamw(pack_small(P), pack_small(M), pack_small(V), g=gs)
    ups = [unpack_small(a, P) for a in (gs, sd, sm_, sv_)]
    for n in _SMALL:
        out[n] = tuple(u[n] for u in ups)

    loss_all = lax.psum(loss[0, 0], ("x", "y", "c"))
    flat = [loss_all, gx[None]]
    for k in range(4):
        flat += [out[n][k] for n in _NAMES]
    return tuple(flat)
```

```python
import functools

import jax
import jax.numpy as jnp
from jax import lax
from jax.experimental import pallas as pl
from jax.experimental.pallas import tpu as pltpu

F32, BF16 = jnp.float32, jnp.bfloat16
HI = lax.Precision.HIGHEST

N_DEV = 8
D = 1024
NL = 2
N_META = 16
BLK = 128
PAD = BLK - N_META
D_FF = 2816
HS = D_FF // N_DEV
SSD_H, SSD_P, SSD_N, SSD_G = 8, 64, 64, 2
SSD_D = SSD_H * SSD_P
CONV_K = 4
CONV_D = SSD_D + 2 * SSD_G * SSD_N
FOX_H, FOX_DH = 4, 64
MLA_H, MLA_QL, MLA_KVL, MLA_NOPE, MLA_ROPE, MLA_V = 4, 256, 128, 64, 32, 64
N_IN = 2476
C_Z, C_XBC, C_FQ, C_FK, C_FV, C_CQ, C_CKV, C_SM, N_INP = 0, 512, 1280, 1536, 1792, 2048, 2304, 2432, 2560
SM_DT, SM_F, SM_KR = 0, 8, 64
ALPHA = (2 * NL) ** 0.25
EPS = 1e-5
NEG = -1e30
LR, B1, B2, AEPS, WD, STEP = 0.001, 0.9, 0.999, 1e-08, 0.01, 10
VMEM_MB = 56


def _cp(*sem):
    return pltpu.CompilerParams(dimension_semantics=sem, vmem_limit_bytes=VMEM_MB << 20)


def _nn(a, b):
    return lax.dot_general(a, b, (((1,), (0,)), ((), ())), preferred_element_type=F32)


def _nt(a, b):
    return lax.dot_general(a, b, (((1,), (1,)), ((), ())), preferred_element_type=F32)


def _tn(a, b):
    return lax.dot_general(a, b, (((0,), (0,)), ((), ())), preferred_element_type=F32)


def _nn_hi(a, b):
    return lax.dot_general(a, b, (((1,), (0,)), ((), ())), precision=HI, preferred_element_type=F32)


def _row_tile(t):
    for d in range(640, 15, -16):
        if t % d == 0:
            return d
    raise ValueError(t)


def _sig(x):
    return 1.0 / (1.0 + jnp.exp(-x))


def _tri(lower=True):
    r = lax.broadcasted_iota(jnp.int32, (BLK, BLK), 0)
    c = lax.broadcasted_iota(jnp.int32, (BLK, BLK), 1)
    return (r >= c) if lower else (r <= c)


def build_h0(meta_full, x):
    s = x.shape[0]
    nb = s // BLK + 1

    def body(m_ref, x_ref, h_ref, hb_ref):
        i = pl.program_id(0)

        @pl.when(i == 0)
        def _():
            h = jnp.concatenate([jnp.zeros((PAD, D), F32), m_ref[...]], axis=0)
            h_ref[...] = h
            hb_ref[...] = h.astype(BF16)

        @pl.when(i > 0)
        def _():
            h_ref[...] = x_ref[...]
            hb_ref[...] = x_ref[...].astype(BF16)

    return pl.pallas_call(
        body, name="build_h0", grid=(nb,),
        in_specs=[pl.BlockSpec((N_META, D), lambda i: (0, 0)),
                  pl.BlockSpec((BLK, D), lambda i: (jnp.maximum(i - 1, 0), 0))],
        out_specs=[pl.BlockSpec((BLK, D), lambda i: (i, 0))] * 2,
        out_shape=[jax.ShapeDtypeStruct((nb * BLK, D), F32), jax.ShapeDtypeStruct((nb * BLK, D), BF16)],
        compiler_params=_cp("arbitrary"),
    )(meta_full, x)


def ffn_up(hb, wg, wu):
    t = hb.shape[0]
    g, _, hs = wg.shape
    tm = _row_tile(t)

    def body(h_ref, wg_ref, wu_ref, u_ref, v_ref, a_ref):
        h = h_ref[...]
        u = _nn(h, wg_ref[...])
        v = _nn(h, wu_ref[...])
        u_ref[...] = u.astype(BF16)
        v_ref[...] = v.astype(BF16)
        a_ref[...] = (u * _sig(u) * v).astype(BF16)

    w_spec = pl.BlockSpec((None, D, hs), lambda gi, i: (gi, 0, 0))
    o_spec = pl.BlockSpec((None, tm, hs), lambda gi, i: (gi, i, 0))
    return pl.pallas_call(
        body, name="ffn_up", grid=(g, t // tm),
        in_specs=[pl.BlockSpec((tm, D), lambda gi, i: (i, 0)), w_spec, w_spec],
        out_specs=[o_spec] * 3,
        out_shape=[jax.ShapeDtypeStruct((g, t, hs), BF16)] * 3,
        compiler_params=_cp("arbitrary", "arbitrary"),
    )(hb, wg, wu)


def mm_res_ln(a, b, res, alpha, scale, gamma, beta):
    g, t, k = a.shape
    tm = _row_tile(t)

    def body(a_ref, b_ref, res_ref, g_ref, be_ref, r_ref, y_ref, yb_ref, acc):
        gi = pl.program_id(1)

        @pl.when(gi == 0)
        def _():
            acc[...] = jnp.zeros_like(acc)

        acc[...] += _nn(a_ref[...], b_ref[...])

        @pl.when(gi == g - 1)
        def _():
            r = alpha * res_ref[...] + scale * acc[...]
            mu = jnp.mean(r, axis=1, keepdims=True)
            xc = r - mu
            var = jnp.mean(xc * xc, axis=1, keepdims=True)
            y = xc * lax.rsqrt(var + EPS) * g_ref[...] + be_ref[...]
            r_ref[...] = r
            y_ref[...] = y
            yb_ref[...] = y.astype(BF16)

    row = pl.BlockSpec((tm, D), lambda i, gi: (i, 0))
    vec = pl.BlockSpec((1, D), lambda i, gi: (0, 0))
    return pl.pallas_call(
        body, name="mm_res_ln", grid=(t // tm, g),
        in_specs=[pl.BlockSpec((None, tm, k), lambda i, gi: (gi, i, 0)),
                  pl.BlockSpec((None, k, D), lambda i, gi: (gi, 0, 0)), row, vec, vec],
        out_specs=[row] * 3,
        out_shape=[jax.ShapeDtypeStruct((t, D), F32), jax.ShapeDtypeStruct((t, D), F32),
                   jax.ShapeDtypeStruct((t, D), BF16)],
        scratch_shapes=[pltpu.VMEM((tm, D), F32)],
        compiler_params=_cp("arbitrary", "arbitrary"),
    )(a, b, res, gamma, beta)


def mm_nn(a, b, tn=512):
    t, k = a.shape
    n = b.shape[1]
    tm = _row_tile(t)

    def body(a_ref, b_ref, o_ref):
        o_ref[...] = _nn(a_ref[...], b_ref[...])

    return pl.pallas_call(
        body, name="mm_nn", grid=(n // tn, t // tm),
        in_specs=[pl.BlockSpec((tm, k), lambda j, i: (i, 0)), pl.BlockSpec((k, tn), lambda j, i: (0, j))],
        out_specs=pl.BlockSpec((tm, tn), lambda j, i: (i, j)),
        out_shape=jax.ShapeDtypeStruct((t, n), F32),
        compiler_params=_cp("arbitrary", "arbitrary"),
    )(a, b)


def ffn_dact(dfb, wd, u, v):
    g, t, hs = u.shape
    tm = _row_tile(t)

    def body(df_ref, wd_ref, u_ref, v_ref, du_ref, dv_ref):
        da = _nt(df_ref[...], wd_ref[...])
        uu = u_ref[...].astype(F32)
        sg = _sig(uu)
        du_ref[...] = (da * v_ref[...].astype(F32) * (sg * (1.0 + uu * (1.0 - sg)))).astype(BF16)
        dv_ref[...] = (da * uu * sg).astype(BF16)

    act = pl.BlockSpec((None, tm, hs), lambda gi, i: (gi, i, 0))
    return pl.pallas_call(
        body, name="ffn_dact", grid=(g, t // tm),
        in_specs=[pl.BlockSpec((tm, D), lambda gi, i: (i, 0)),
                  pl.BlockSpec((None, hs, D), lambda gi, i: (gi, 0, 0)), act, act],
        out_specs=[act] * 2,
        out_shape=[jax.ShapeDtypeStruct((g, t, hs), BF16)] * 2,
        compiler_params=_cp("arbitrary", "arbitrary"),
    )(dfb, wd, u, v)


def mm_nt_reduce(pairs, n):
    g, t, _ = pairs[0][0].shape
    tm = _row_tile(t)
    npair = len(pairs)

    def body(*refs):
        o_ref = refs[-1]
        gi = pl.program_id(1)
        tot = _nt(refs[0][...], refs[1][...])
        for p in range(1, npair):
            tot += _nt(refs[2 * p][...], refs[2 * p + 1][...])

        @pl.when(gi == 0)
        def _():
            o_ref[...] = tot

        @pl.when(gi > 0)
        def _():
            o_ref[...] += tot

    in_specs, args = [], []
    for x, w in pairs:
        k = x.shape[2]
        in_specs += [pl.BlockSpec((None, tm, k), lambda i, gi: (gi, i, 0)),
                     pl.BlockSpec((None, n, k), lambda i, gi: (gi, 0, 0))]
        args += [x, w]
    return pl.pallas_call(
        body, name="mm_nt_reduce", grid=(t // tm, g),
        in_specs=in_specs, out_specs=pl.BlockSpec((tm, n), lambda i, gi: (i, 0)),
        out_shape=jax.ShapeDtypeStruct((t, n), F32),
        compiler_params=_cp("arbitrary", "arbitrary"),
    )(*args)


def mm_tn(x, y, out_dtype=BF16):
    gx, t, k = x.shape
    gy, _, n = y.shape
    g = max(gx, gy)
    tm = _row_tile(t)
    nt = t // tm

    def body(x_ref, y_ref, o_ref, acc):
        i = pl.program_id(1)

        @pl.when(i == 0)
        def _():
            acc[...] = jnp.zeros_like(acc)

        acc[...] += _tn(x_ref[...], y_ref[...])

        @pl.when(i == nt - 1)
        def _():
            o_ref[...] = acc[...].astype(out_dtype)

    return pl.pallas_call(
        body, name="mm_tn", grid=(g, nt),
        in_specs=[pl.BlockSpec((None, tm, k), (lambda gi, i: (gi, i, 0)) if gx > 1 else (lambda gi, i: (0, i, 0))),
                  pl.BlockSpec((None, tm, n), (lambda gi, i: (gi, i, 0)) if gy > 1 else (lambda gi, i: (0, i, 0)))],
        out_specs=pl.BlockSpec((None, k, n), lambda gi, i: (gi, 0, 0)),
        out_shape=jax.ShapeDtypeStruct((g, k, n), out_dtype),
        scratch_shapes=[pltpu.VMEM((k, n), F32)],
        compiler_params=_cp("arbitrary", "arbitrary"),
    )(x, y)


def ln_bwd(parts, r, gamma, out_scale, after=None):
    t = r.shape[0]
    tm = _row_tile(t)
    scales = [s for _, s in parts]
    npart = len(parts)
    extra = [] if after is None else [after]

    def body(*refs):
        refs = refs[len(extra):]
        r_ref, g_ref = refs[npart], refs[npart + 1]
        dr_ref, drb_ref, dg_ref, db_ref = refs[npart + 2:]
        i = pl.program_id(0)
        dy = scales[0] * refs[0][...]
        for p in range(1, npart):
            dy += scales[p] * refs[p][...]
        rr = r_ref[...]
        mu = jnp.mean(rr, axis=1, keepdims=True)
        xc = rr - mu
        rstd = lax.rsqrt(jnp.mean(xc * xc, axis=1, keepdims=True) + EPS)
        xh = xc * rstd
        dxh = dy * g_ref[...]
        m1 = jnp.mean(dxh, axis=1, keepdims=True)
        m2 = jnp.mean(dxh * xh, axis=1, keepdims=True)
        dr = rstd * (dxh - m1 - xh * m2)
        dr_ref[...] = dr
        drb_ref[...] = (out_scale * dr).astype(BF16)
        dg = jnp.sum(dy * xh, axis=0, keepdims=True)
        db = jnp.sum(dy, axis=0, keepdims=True)

        @pl.when(i == 0)
        def _():
            dg_ref[...] = dg
            db_ref[...] = db

        @pl.when(i > 0)
        def _():
            dg_ref[...] += dg
            db_ref[...] += db

    row = pl.BlockSpec((tm, D), lambda i: (i, 0))
    vec = pl.BlockSpec((1, D), lambda i: (0, 0))
    return pl.pallas_call(
        body, name="ln_bwd", grid=(t // tm,),
        in_specs=[_ANY] * len(extra) + [row] * (npart + 1) + [vec],
        out_specs=[row, row, vec, vec],
        out_shape=[jax.ShapeDtypeStruct((t, D), F32), jax.ShapeDtypeStruct((t, D), BF16),
                   jax.ShapeDtypeStruct((1, D), F32), jax.ShapeDtypeStruct((1, D), F32)],
        compiler_params=_cp("arbitrary"),
    )(*extra, *[p for p, _ in parts], r, gamma)


def loss_head(h, target):
    t = h.shape[0]
    nb = t // BLK

    def body(h_ref, t_ref, dy_ref, l_ref):
        i = pl.program_id(0)

        @pl.when(i == 0)
        def _():
            dy_ref[...] = jnp.zeros_like(dy_ref)
            l_ref[...] = jnp.zeros_like(l_ref)

        @pl.when(i > 0)
        def _():
            err = h_ref[...] - t_ref[...]
            dy_ref[...] = err * (1.0 / D)
            l_ref[...] += (0.5 / D) * jnp.sum(err * err, keepdims=True)

    return pl.pallas_call(
        body, name="loss_head", grid=(nb,),
        in_specs=[pl.BlockSpec((BLK, D), lambda i: (i, 0)),
                  pl.BlockSpec((BLK, D), lambda i: (jnp.maximum(i - 1, 0), 0))],
        out_specs=[pl.BlockSpec((BLK, D), lambda i: (i, 0)), pl.BlockSpec((1, 1), lambda i: (0, 0))],
        out_shape=[jax.ShapeDtypeStruct((t, D), F32), jax.ShapeDtypeStruct((1, 1), F32)],
        compiler_params=_cp("arbitrary"),
    )(h, target)


def final_add(dr, dh, after=None):
    t = dr.shape[0]
    nb = t // BLK
    extra = [] if after is None else [after]

    def body(*refs):
        a_ref, b_ref, gx_ref, gm_ref = refs[len(extra):]
        i = pl.program_id(0)
        tot = ALPHA * a_ref[...] + b_ref[...]

        @pl.when(i == 0)
        def _():
            gm_ref[...] = tot[PAD:, :]

        @pl.when(i > 0)
        def _():
            gx_ref[...] = tot

    blk = pl.BlockSpec((BLK, D), lambda i: (i, 0))
    return pl.pallas_call(
        body, name="final_add", grid=(nb,),
        in_specs=[_ANY] * len(extra) + [blk, blk],
        out_specs=[pl.BlockSpec((BLK, D), lambda i: (jnp.maximum(i - 1, 0), 0)),
                   pl.BlockSpec((N_META, D), lambda i: (0, 0))],
        out_shape=[jax.ShapeDtypeStruct((t - BLK, D), F32), jax.ShapeDtypeStruct((N_META, D), F32)],
        compiler_params=_cp("arbitrary"),
    )(*extra, dr, dh)


def _valid_rows(nrows, first_row):
    return (first_row + lax.broadcasted_iota(jnp.int32, (nrows, 1), 0)) >= PAD


def conv_fwd(proj, conv_w, conv_b):
    t = proj.shape[0]
    c0 = C_XBC // BLK

    def body(x_ref, w_ref, b_ref, o_ref):
        ok = _valid_rows(t, 0)
        x = jnp.where(ok, x_ref[...], 0.0)
        w = w_ref[...]
        acc = b_ref[...] + w[CONV_K - 1:CONV_K, :] * x
        for s in range(1, CONV_K):
            acc += w[CONV_K - 1 - s:CONV_K - s, :] * pltpu.roll(x, s, 0)
        o_ref[...] = jnp.where(ok, acc * _sig(acc), 0.0)

    return pl.pallas_call(
        body, name="conv_fwd", grid=(CONV_D // BLK,),
        in_specs=[pl.BlockSpec((t, BLK), lambda j: (0, c0 + j)),
                  pl.BlockSpec((CONV_K, BLK), lambda j: (0, j)), pl.BlockSpec((1, BLK), lambda j: (0, j))],
        out_specs=pl.BlockSpec((t, BLK), lambda j: (0, j)),
        out_shape=jax.ShapeDtypeStruct((t, CONV_D), F32),
        compiler_params=_cp("arbitrary"),
    )(proj, conv_w, conv_b)


def conv_bwd(dxa, proj, conv_w, conv_b):
    t = proj.shape[0]
    c0 = C_XBC // BLK

    def body(d_ref, x_ref, w_ref, b_ref, dx_ref, dw_ref, db_ref):
        ok = _valid_rows(t, 0)
        x = jnp.where(ok, x_ref[...], 0.0)
        w = w_ref[...]
        xs = [x] + [pltpu.roll(x, s, 0) for s in range(1, CONV_K)]
        acc = b_ref[...] + w[CONV_K - 1:CONV_K, :] * x
        for s in range(1, CONV_K):
            acc += w[CONV_K - 1 - s:CONV_K - s, :] * xs[s]
        sg = _sig(acc)
        dxc = jnp.where(ok, d_ref[...] * (sg * (1.0 + acc * (1.0 - sg))), 0.0)
        db_ref[...] = jnp.sum(dxc, axis=0, keepdims=True)
        dw_ref[...] = jnp.concatenate(
            [jnp.sum(dxc * xs[CONV_K - 1 - k], axis=0, keepdims=True) for k in range(CONV_K)], axis=0)
        dx = w[CONV_K - 1:CONV_K, :] * dxc
        for s in range(1, CONV_K):
            dx += w[CONV_K - 1 - s:CONV_K - s, :] * pltpu.roll(dxc, t - s, 0)
        dx_ref[...] = jnp.where(ok, dx, 0.0)

    col = pl.BlockSpec((t, BLK), lambda j: (0, j))
    return pl.pallas_call(
        body, name="conv_bwd", grid=(CONV_D // BLK,),
        in_specs=[col, pl.BlockSpec((t, BLK), lambda j: (0, c0 + j)),
                  pl.BlockSpec((CONV_K, BLK), lambda j: (0, j)), pl.BlockSpec((1, BLK), lambda j: (0, j))],
        out_specs=[col, pl.BlockSpec((CONV_K, BLK), lambda j: (0, j)), pl.BlockSpec((1, BLK), lambda j: (0, j))],
        out_shape=[jax.ShapeDtypeStruct((t, CONV_D), F32), jax.ShapeDtypeStruct((CONV_K, CONV_D), F32),
                   jax.ShapeDtypeStruct((1, CONV_D), F32)],
        compiler_params=_cp("arbitrary"),
    )(dxa, proj, conv_w, conv_b)


def _softplus(x):
    return jnp.maximum(x, 0.0) + jnp.log(1.0 + jnp.exp(-jnp.abs(x)))


def _ssd_chunk(xa, sm, dtb, alog, ok):
    dt = jnp.where(ok, _softplus(sm + dtb), 0.0)
    amat = -jnp.exp(alog)
    a = dt * amat
    ac = _nn_hi(_tri().astype(F32), a)
    act = ac.T
    return dt, amat, ac, act


def _ssd_head(xa, dt, ac, act, h, cb, sp):
    g = h // (SSD_H // SSD_G)
    xs = xa[:, SSD_P * h:SSD_P * (h + 1)]
    bg = xa[:, SSD_D + SSD_N * g:SSD_D + SSD_N * (g + 1)]
    cg = xa[:, SSD_D + SSD_G * SSD_N + SSD_N * g:SSD_D + SSD_G * SSD_N + SSD_N * (g + 1)]
    dth = dt[:, h:h + 1]
    ach = ac[:, h:h + 1]
    acth = act[h:h + 1, :]
    xdt = xs * dth
    seg = jnp.where(_tri(), jnp.exp(jnp.minimum(ach - acth, 0.0)), 0.0)
    m = cb * seg
    yd = _nn(m.astype(BF16), xdt.astype(BF16))
    last = ac[BLK - 1:BLK, h:h + 1]
    dec = jnp.exp(last - ach)
    e = jnp.exp(ach)
    yo = _nn(cg.astype(BF16), sp.astype(BF16)) * e
    return xs, bg, cg, dth, ach, xdt, seg, m, yd, last, dec, e, yo


def ssd_fwd(xa, proj, dtb, alog, dskip, normg):
    t = xa.shape[0]
    nb = t // BLK
    gw = SSD_D // SSD_G

    def body(xa_ref, z_ref, sm_ref, dtb_ref, al_ref, ds_ref, ng_ref, y_ref, sp_ref, st):
        c = pl.program_id(0)

        @pl.when(c == 0)
        def _():
            st[...] = jnp.zeros_like(st)

        ok = _valid_rows(BLK, c * BLK)
        xa = xa_ref[...]
        dt, _, ac, act = _ssd_chunk(xa, sm_ref[...], dtb_ref[...], al_ref[...], ok)
        sp_ref[...] = st[...]
        ys = []
        cbs = {}
        for h in range(SSD_H):
            g = h // (SSD_H // SSD_G)
            if g not in cbs:
                bg = xa[:, SSD_D + SSD_N * g:SSD_D + SSD_N * (g + 1)]
                cg = xa[:, SSD_D + SSD_G * SSD_N + SSD_N * g:SSD_D + SSD_G * SSD_N + SSD_N * (g + 1)]
                cbs[g] = _nt(cg.astype(BF16), bg.astype(BF16))
            sp = st[:, SSD_P * h:SSD_P * (h + 1)]
            xs, bg, cg, dth, ach, xdt, seg, m, yd, last, dec, e, yo = _ssd_head(xa, dt, ac, act, h, cbs[g], sp)
            sloc = _tn((bg * dec).astype(BF16), xdt.astype(BF16))
            st[:, SSD_P * h:SSD_P * (h + 1)] = jnp.exp(last) * sp + sloc
            ys.append(yd + yo + ds_ref[:, h:h + 1] * xs)
        y = jnp.concatenate(ys, axis=1)
        z = z_ref[...]
        yg = y * (z * _sig(z))
        outs = []
        for g in range(SSD_G):
            v = yg[:, gw * g:gw * (g + 1)]
            outs.append(v * lax.rsqrt(jnp.mean(v * v, axis=1, keepdims=True) + EPS))
        y_ref[...] = (jnp.concatenate(outs, axis=1) * ng_ref[...]).astype(BF16)

    vec = pl.BlockSpec((1, BLK), lambda c: (0, 0))
    return pl.pallas_call(
        body, name="ssd_fwd", grid=(nb,),
        in_specs=[pl.BlockSpec((BLK, CONV_D), lambda c: (c, 0)),
                  pl.BlockSpec((BLK, SSD_D), lambda c: (c, C_Z // SSD_D)),
                  pl.BlockSpec((BLK, BLK), lambda c: (c, C_SM // BLK)),
                  vec, vec, vec, pl.BlockSpec((1, SSD_D), lambda c: (0, 0))],
        out_specs=[pl.BlockSpec((BLK, SSD_D), lambda c: (c, 0)),
                   pl.BlockSpec((None, SSD_N, SSD_D), lambda c: (c, 0, 0))],
        out_shape=[jax.ShapeDtypeStruct((t, SSD_D), BF16), jax.ShapeDtypeStruct((nb, SSD_N, SSD_D), F32)],
        scratch_shapes=[pltpu.VMEM((SSD_N, SSD_D), F32)],
        compiler_params=_cp("arbitrary"),
    )(xa, proj, proj, dtb, alog, dskip, normg)


def _lane_put(col, lane):
    li = lax.broadcasted_iota(jnp.int32, (col.shape[0], BLK), 1)
    return jnp.where(li == lane, col, 0.0)


def ssd_bwd(dmix, xa, proj, sprev, dtb, alog, dskip, normg):
    t = xa.shape[0]
    nb = t // BLK
    gw = SSD_D // SSD_G
    rev = lambda c: nb - 1 - c

    def body(dy_ref, xa_ref, z_ref, sm_ref, sp_ref, dtb_ref, al_ref, ds_ref, ng_ref,
             dxa_ref, dz_ref, dsm_ref, dng_ref, dds_ref, dal_ref, ddtb_ref, dst):
        c = pl.program_id(0)

        @pl.when(c == 0)
        def _():
            dst[...] = jnp.zeros_like(dst)
            dng_ref[...] = jnp.zeros_like(dng_ref)
            dds_ref[...] = jnp.zeros_like(dds_ref)
            dal_ref[...] = jnp.zeros_like(dal_ref)
            ddtb_ref[...] = jnp.zeros_like(ddtb_ref)

        ok = _valid_rows(BLK, rev(c) * BLK)
        xa = xa_ref[...]
        sm = sm_ref[...]
        dt, amat, ac, act = _ssd_chunk(xa, sm, dtb_ref[...], al_ref[...], ok)
        tri = _tri()
        rowi = lax.broadcasted_iota(jnp.int32, (BLK, 1), 0)
        cbs, heads, ys = {}, [], []
        for h in range(SSD_H):
            g = h // (SSD_H // SSD_G)
            if g not in cbs:
                bg = xa[:, SSD_D + SSD_N * g:SSD_D + SSD_N * (g + 1)]
                cg = xa[:, SSD_D + SSD_G * SSD_N + SSD_N * g:SSD_D + SSD_G * SSD_N + SSD_N * (g + 1)]
                cbs[g] = _nt(cg.astype(BF16), bg.astype(BF16))
            sp = sp_ref[:, SSD_P * h:SSD_P * (h + 1)]
            hd = _ssd_head(xa, dt, ac, act, h, cbs[g], sp)
            heads.append(hd)
            ys.append(hd[8] + hd[12] + ds_ref[:, h:h + 1] * hd[0])
        y = jnp.concatenate(ys, axis=1)
        z = z_ref[...]
        sgz = _sig(z)
        siluz = z * sgz
        yg = y * siluz
        dout = dy_ref[...]
        ng = ng_ref[...]
        dygs, xhs = [], []
        for g in range(SSD_G):
            v = yg[:, gw * g:gw * (g + 1)]
            rr = lax.rsqrt(jnp.mean(v * v, axis=1, keepdims=True) + EPS)
            xh = v * rr
            dxh = dout[:, gw * g:gw * (g + 1)] * ng[:, gw * g:gw * (g + 1)]
            dygs.append(rr * (dxh - xh * jnp.mean(dxh * xh, axis=1, keepdims=True)))
            xhs.append(xh)
        dyg = jnp.concatenate(dygs, axis=1)
        dng_ref[...] += jnp.sum(dout * jnp.concatenate(xhs, axis=1), axis=0, keepdims=True)
        dy = dyg * siluz
        dz_ref[...] = dyg * y * (sgz * (1.0 + z * (1.0 - sgz)))

        dxs_l = []
        db_g = [jnp.zeros((BLK, SSD_N), F32) for _ in range(SSD_G)]
        dc_g = [jnp.zeros((BLK, SSD_N), F32) for _ in range(SSD_G)]
        dac_all = jnp.zeros((BLK, BLK), F32)
        ddt_all = jnp.zeros((BLK, BLK), F32)
        dds_row = jnp.zeros((1, BLK), F32)
        lane1 = lax.broadcasted_iota(jnp.int32, (1, BLK), 1)
        for h in range(SSD_H):
            g = h // (SSD_H // SSD_G)
            xs, bg, cg, dth, ach, xdt, seg, m, yd, last, dec, e, yo = heads[h]
            sp = sp_ref[:, SSD_P * h:SSD_P * (h + 1)]
            dyh = dy[:, SSD_P * h:SSD_P * (h + 1)]
            dyb = dyh.astype(BF16)
            xdtb = xdt.astype(BF16)
            dds_row += jnp.where(lane1 == h, jnp.sum(dyh * xs, keepdims=True), 0.0)
            dxs = ds_ref[:, h:h + 1] * dyh
            dyo = (dyh * e).astype(BF16)
            dc_g[g] += _nt(dyo, sp.astype(BF16))
            dsp = _tn(cg.astype(BF16), dyo)
            dac = jnp.sum(dyh * yo, axis=1, keepdims=True)
            dsn = dst[:, SSD_P * h:SSD_P * (h + 1)]
            gl = jnp.exp(last)
            dst[:, SSD_P * h:SSD_P * (h + 1)] = dsp + gl * dsn
            dlast = jnp.sum(dsn * sp, keepdims=True) * gl
            dsnb = dsn.astype(BF16)
            dbd = _nt(xdtb, dsnb)
            db_g[g] += dbd * dec
            tdec = jnp.sum(dbd * bg, axis=1, keepdims=True) * dec
            dxdt = _nn((bg * dec).astype(BF16), dsnb)
            dlast += jnp.sum(tdec, keepdims=True)
            dac -= tdec
            dm = _nt(dyb, xdtb)
            dxdt += _tn(m.astype(BF16), dyb)
            dcb = (dm * seg).astype(BF16)
            dc_g[g] += _nn(dcb, bg.astype(BF16))
            db_g[g] += _tn(dcb, cg.astype(BF16))
            w = dm * m
            dac += jnp.sum(w, axis=1, keepdims=True) - jnp.sum(w.T, axis=1, keepdims=True)
            dac += jnp.where(rowi == BLK - 1, dlast, 0.0)
            dxs_l.append(dxs + dxdt * dth)
            ddt_all += _lane_put(jnp.sum(dxdt * xs, axis=1, keepdims=True), h)
            dac_all += _lane_put(dac, h)
        da = _nn_hi(_tri(lower=False).astype(F32), dac_all)
        ddt = ddt_all + da * amat
        dal_ref[...] += jnp.sum(da * dt, axis=0, keepdims=True) * amat
        ddtr = jnp.where(ok, ddt * _sig(sm + dtb_ref[...]), 0.0)
        ddtb_ref[...] += jnp.sum(ddtr, axis=0, keepdims=True)
        dds_ref[...] += dds_row
        dsm_ref[...] = ddtr
        dxa_ref[...] = jnp.where(ok, jnp.concatenate(dxs_l + db_g + dc_g, axis=1), 0.0)

    vec = pl.BlockSpec((1, BLK), lambda c: (0, 0))
    nvec = pl.BlockSpec((1, SSD_D), lambda c: (0, 0))
    return pl.pallas_call(
        body, name="ssd_bwd", grid=(nb,),
        in_specs=[pl.BlockSpec((BLK, SSD_D), lambda c: (rev(c), 0)),
                  pl.BlockSpec((BLK, CONV_D), lambda c: (rev(c), 0)),
                  pl.BlockSpec((BLK, SSD_D), lambda c: (rev(c), C_Z // SSD_D)),
                  pl.BlockSpec((BLK, BLK), lambda c: (rev(c), C_SM // BLK)),
                  pl.BlockSpec((None, SSD_N, SSD_D), lambda c: (rev(c), 0, 0)),
                  vec, vec, vec, nvec],
        out_specs=[pl.BlockSpec((BLK, CONV_D), lambda c: (rev(c), 0)),
                   pl.BlockSpec((BLK, SSD_D), lambda c: (rev(c), 0)),
                   pl.BlockSpec((BLK, BLK), lambda c: (rev(c), 0)),
                   nvec, vec, vec, vec],
        out_shape=[jax.ShapeDtypeStruct((t, CONV_D), F32), jax.ShapeDtypeStruct((t, SSD_D), F32),
                   jax.ShapeDtypeStruct((t, BLK), F32), jax.ShapeDtypeStruct((1, SSD_D), F32),
                   jax.ShapeDtypeStruct((1, BLK), F32), jax.ShapeDtypeStruct((1, BLK), F32),
                   jax.ShapeDtypeStruct((1, BLK), F32)],
        scratch_shapes=[pltpu.VMEM((SSD_N, SSD_D), F32)],
        compiler_params=_cp("arbitrary"),
    )(dmix, xa, proj, proj, sprev, dtb, alog, dskip, normg)


def _attn_scores(q_ref, k_ref, h, dq, scale, mask, bias):
    qh = q_ref[:, dq * h:dq * (h + 1)].astype(BF16)
    kh = k_ref[:, dq * h:dq * (h + 1)].astype(BF16)
    s = _nt(qh, kh) * scale
    if bias is not None:
        s = s + bias
    return qh, kh, jnp.where(mask, s, NEG)


def attn_fwd(q, k, v, qcol, kcol, vcol, nh, dq, dv, scale, c_col=None, c_row=None, lane0=0):
    t = q.shape[0]
    tq = BLK
    use_bias = c_col is not None

    def body(*refs):
        if use_bias:
            q_ref, k_ref, v_ref, cc_ref, cr_ref, o_ref, l_ref = refs
        else:
            q_ref, k_ref, v_ref, o_ref, l_ref = refs
        i = pl.program_id(0)
        rowg = i * tq + lax.broadcasted_iota(jnp.int32, (tq, 1), 0)
        col = lax.broadcasted_iota(jnp.int32, (1, t), 1)
        mask = (col <= rowg) & (col >= PAD)
        outs = []
        lse = jnp.zeros((tq, BLK), F32)
        for h in range(nh):
            bias = (cc_ref[:, lane0 + h:lane0 + h + 1] - cr_ref[h:h + 1, :]) if use_bias else None
            _, _, s = _attn_scores(q_ref, k_ref, h, dq, scale, mask, bias)
            m = jnp.max(s, axis=1, keepdims=True)
            p = jnp.exp(s - m)
            l = jnp.sum(p, axis=1, keepdims=True)
            vh = v_ref[:, dv * h:dv * (h + 1)].astype(BF16)
            outs.append(_nn(p.astype(BF16), vh) / l)
            lse += _lane_put(m + jnp.log(l), h)
        o_ref[...] = jnp.concatenate(outs, axis=1).astype(BF16)
        l_ref[...] = lse

    in_specs = [pl.BlockSpec((tq, nh * dq), lambda i: (i, qcol)),
                pl.BlockSpec((t, nh * dq), lambda i: (0, kcol)),
                pl.BlockSpec((t, nh * dv), lambda i: (0, vcol))]
    args = [q, k, v]
    if use_bias:
        in_specs += [pl.BlockSpec((tq, BLK), lambda i: (i, 0)), pl.BlockSpec((8, t), lambda i: (0, 0))]
        args += [c_col, c_row]
    return pl.pallas_call(
        body, name="attn_fwd", grid=(t // tq,),
        in_specs=in_specs,
        out_specs=[pl.BlockSpec((tq, nh * dv), lambda i: (i, 0)), pl.BlockSpec((tq, BLK), lambda i: (i, 0))],
        out_shape=[jax.ShapeDtypeStruct((t, nh * dv), BF16), jax.ShapeDtypeStruct((t, BLK), F32)],
        compiler_params=_cp("arbitrary"),
    )(*args)


def attn_bwd(q, k, v, do, lse, qcol, kcol, vcol, docol, nh, dq, dv, scale, c_col=None, c_row=None, lane0=0):
    t = q.shape[0]
    tq = BLK
    use_bias = c_col is not None

    def body(*refs):
        if use_bias:
            q_ref, k_ref, v_ref, do_ref, l_ref, cc_ref, cr_ref, dq_ref, dk_ref, dv_ref, dcq_ref, dck_ref = refs
        else:
            q_ref, k_ref, v_ref, do_ref, l_ref, dq_ref, dk_ref, dv_ref = refs
        i = pl.program_id(0)

        @pl.when(i == 0)
        def _():
            dk_ref[...] = jnp.zeros_like(dk_ref)
            dv_ref[...] = jnp.zeros_like(dv_ref)
            if use_bias:
                dck_ref[...] = jnp.zeros_like(dck_ref)

        rowg = i * tq + lax.broadcasted_iota(jnp.int32, (tq, 1), 0)
        col = lax.broadcasted_iota(jnp.int32, (1, t), 1)
        mask = (col <= rowg) & (col >= PAD)
        dqs = []
        dcq = jnp.zeros((tq, BLK), F32)
        for h in range(nh):
            bias = (cc_ref[:, lane0 + h:lane0 + h + 1] - cr_ref[h:h + 1, :]) if use_bias else None
            qh, kh, s = _attn_scores(q_ref, k_ref, h, dq, scale, mask, bias)
            p = jnp.where(mask, jnp.exp(s - l_ref[:, h:h + 1]), 0.0)
            vh = v_ref[:, dv * h:dv * (h + 1)].astype(BF16)
            doh = do_ref[:, dv * h:dv * (h + 1)].astype(BF16)
            dp = _nt(doh, vh)
            delta = jnp.sum(p * dp, axis=1, keepdims=True)
            ds = p * (dp - delta)
            dsb = ds.astype(BF16)
            dqs.append(_nn(dsb, kh) * scale)
            dk_ref[:, dq * h:dq * (h + 1)] += _tn(dsb, qh) * scale
            dv_ref[:, dv * h:dv * (h + 1)] += _tn(p.astype(BF16), doh)
            if use_bias:
                dcq += _lane_put(jnp.sum(ds, axis=1, keepdims=True), lane0 + h)
                dck_ref[h:h + 1, :] += jnp.sum(ds, axis=0, keepdims=True)
        dq_ref[...] = jnp.concatenate(dqs, axis=1)
        if use_bias:
            dcq_ref[...] = dcq

    in_specs = [pl.BlockSpec((tq, nh * dq), lambda i: (i, qcol)),
                pl.BlockSpec((t, nh * dq), lambda i: (0, kcol)),
                pl.BlockSpec((t, nh * dv), lambda i: (0, vcol)),
                pl.BlockSpec((tq, nh * dv), lambda i: (i, docol)),
                pl.BlockSpec((tq, BLK), lambda i: (i, 0))]
    args = [q, k, v, do, lse]
    out_specs = [pl.BlockSpec((tq, nh * dq), lambda i: (i, 0)), pl.BlockSpec((t, nh * dq), lambda i: (0, 0)),
                 pl.BlockSpec((t, nh * dv), lambda i: (0, 0))]
    out_shape = [jax.ShapeDtypeStruct((t, nh * dq), F32), jax.ShapeDtypeStruct((t, nh * dq), F32),
                 jax.ShapeDtypeStruct((t, nh * dv), F32)]
    if use_bias:
        in_specs += [pl.BlockSpec((tq, BLK), lambda i: (i, 0)), pl.BlockSpec((8, t), lambda i: (0, 0))]
        args += [c_col, c_row]
        out_specs += [pl.BlockSpec((tq, BLK), lambda i: (i, 0)), pl.BlockSpec((8, t), lambda i: (0, 0))]
        out_shape += [jax.ShapeDtypeStruct((t, BLK), F32), jax.ShapeDtypeStruct((8, t), F32)]
    return pl.pallas_call(
        body, name="attn_bwd", grid=(t // tq,),
        in_specs=in_specs, out_specs=out_specs, out_shape=out_shape,
        compiler_params=_cp("arbitrary"),
    )(*args)


def fox_pre(proj, fb):
    t = proj.shape[0]
    nb = t // BLK

    def body(sm_ref, fb_ref, c_ref, cr_ref):
        x = sm_ref[...] + fb_ref[...]
        lane = lax.broadcasted_iota(jnp.int32, (1, BLK), 1)
        keep = _valid_rows(t, 0) & (lane >= SM_F) & (lane < SM_F + FOX_H)
        logf = jnp.where(keep, jnp.minimum(x, 0.0) - jnp.log(1.0 + jnp.exp(-jnp.abs(x))), 0.0)
        tri = _tri().astype(F32)
        carry = jnp.zeros((1, BLK), F32)
        for b in range(nb):
            cb = _nn_hi(tri, logf[b * BLK:(b + 1) * BLK, :]) + carry
            c_ref[b * BLK:(b + 1) * BLK, :] = cb
            carry = cb[BLK - 1:BLK, :]
        cr_ref[...] = c_ref[...].T[SM_F:SM_F + 8, :]

    return pl.pallas_call(
        body, name="fox_pre", grid=(1,),
        in_specs=[pl.BlockSpec((t, BLK), lambda i: (0, C_SM // BLK)), pl.BlockSpec((1, BLK), lambda i: (0, 0))],
        out_specs=[pl.BlockSpec((t, BLK), lambda i: (0, 0)), pl.BlockSpec((8, t), lambda i: (0, 0))],
        out_shape=[jax.ShapeDtypeStruct((t, BLK), F32), jax.ShapeDtypeStruct((8, t), F32)],
        compiler_params=_cp("arbitrary"),
    )(proj, fb)


def fox_pre_bwd(dcq, dck, proj, fb, dsm_in):
    t = proj.shape[0]
    nb = t // BLK

    def body(dcq_ref, dck_ref, sm_ref, fb_ref, din_ref, dsm_ref, dfb_ref, scr):
        triu = _tri(lower=False).astype(F32)
        carry = jnp.zeros((1, BLK), F32)
        scr[...] = jnp.concatenate([jnp.zeros((SM_F, t), F32), dck_ref[...], jnp.zeros((BLK - SM_F - 8, t), F32)], axis=0).T
        for b in range(nb - 1, -1, -1):
            blk = dcq_ref[b * BLK:(b + 1) * BLK, :] - scr[b * BLK:(b + 1) * BLK, :]
            cb = _nn_hi(triu, blk) + carry
            scr[b * BLK:(b + 1) * BLK, :] = cb
            carry = cb[0:1, :]
        x = sm_ref[...] + fb_ref[...]
        lane = lax.broadcasted_iota(jnp.int32, (1, BLK), 1)
        keep = _valid_rows(t, 0) & (lane >= SM_F) & (lane < SM_F + FOX_H)
        df = jnp.where(keep, scr[...] * _sig(-x), 0.0)
        dfb_ref[...] = jnp.sum(df, axis=0, keepdims=True)
        dsm_ref[...] = din_ref[...] + df

    full = pl.BlockSpec((t, BLK), lambda i: (0, 0))
    return pl.pallas_call(
        body, name="fox_pre_bwd", grid=(1,),
        in_specs=[full, pl.BlockSpec((8, t), lambda i: (0, 0)), pl.BlockSpec((t, BLK), lambda i: (0, C_SM // BLK)),
                  pl.BlockSpec((1, BLK), lambda i: (0, 0)), full],
        out_specs=[full, pl.BlockSpec((1, BLK), lambda i: (0, 0))],
        out_shape=[jax.ShapeDtypeStruct((t, BLK), F32), jax.ShapeDtypeStruct((1, BLK), F32)],
        scratch_shapes=[pltpu.VMEM((t, BLK), F32)],
        compiler_params=_cp("arbitrary"),
    )(dcq, dck, proj, fb, dsm_in)


def _swap_rope(x):
    lane = lax.broadcasted_iota(jnp.int32, (1, BLK), 1)
    return jnp.where((lane >= SM_KR) & (lane < SM_KR + 16), pltpu.roll(x, BLK - 16, 1),
                     jnp.where((lane >= SM_KR + 16) & (lane < SM_KR + 32), pltpu.roll(x, 16, 1), 0.0))


def _rms(x, g):
    r = lax.rsqrt(jnp.mean(x * x, axis=1, keepdims=True) + EPS)
    return r, x * r


def mla_pre(proj, qg, kvg, wq, wk, wv, cosq, sinq):
    t = proj.shape[0]
    tm = _row_tile(t)

    def body(cq_ref, ckv_ref, sm_ref, qg_ref, kvg_ref, wq_ref, wk_ref, wv_ref, cos_ref, sin_ref,
             q_ref, k_ref, v_ref, cqn_ref, ckvn_ref):
        cs, sn = cos_ref[...], sin_ref[...]
        _, xh = _rms(cq_ref[...], None)
        cqn = (xh * qg_ref[...]).astype(BF16)
        cqn_ref[...] = cqn
        qraw = _nn(cqn, wq_ref[...])
        qs = []
        for h in range(MLA_H):
            hb = qraw[:, BLK * h:BLK * (h + 1)]
            qs.append(hb * cs + _swap_rope(hb) * sn)
        q_ref[...] = jnp.concatenate(qs, axis=1).astype(BF16)
        _, kh = _rms(ckv_ref[...], None)
        ckvn = (kh * kvg_ref[...]).astype(BF16)
        ckvn_ref[...] = ckvn
        kraw = _nn(ckvn, wk_ref[...])
        v_ref[...] = _nn(ckvn, wv_ref[...]).astype(BF16)
        lane = lax.broadcasted_iota(jnp.int32, (1, BLK), 1)
        kr = sm_ref[...]
        krr = jnp.where((lane >= SM_KR) & (lane < SM_KR + MLA_ROPE), kr * cs + _swap_rope(kr) * sn, 0.0)
        k_ref[...] = jnp.concatenate([kraw[:, BLK * h:BLK * (h + 1)] + krr for h in range(MLA_H)], axis=1).astype(BF16)

    def rows(w, cb):
        return pl.BlockSpec((tm, w), lambda i: (i, cb))

    def whole(a):
        return pl.BlockSpec(a.shape, lambda i: (0, 0))

    return pl.pallas_call(
        body, name="mla_pre", grid=(t // tm,),
        in_specs=[rows(MLA_QL, C_CQ // MLA_QL), rows(MLA_KVL, C_CKV // MLA_KVL), rows(BLK, C_SM // BLK),
                  whole(qg), whole(kvg), whole(wq), whole(wk), whole(wv), rows(BLK, 0), rows(BLK, 0)],
        out_specs=[rows(512, 0), rows(512, 0), rows(256, 0), rows(MLA_QL, 0), rows(MLA_KVL, 0)],
        out_shape=[jax.ShapeDtypeStruct((t, 512), BF16), jax.ShapeDtypeStruct((t, 512), BF16),
                   jax.ShapeDtypeStruct((t, 256), BF16), jax.ShapeDtypeStruct((t, MLA_QL), BF16),
                   jax.ShapeDtypeStruct((t, MLA_KVL), BF16)],
        compiler_params=_cp("arbitrary"),
    )(proj, proj, proj, qg, kvg, wq, wk, wv, cosq, sinq)


def mla_pre_bwd(dq, dk, dv, proj, cqn, ckvn, qg, kvg, wq, wk, wv, cosq, sinq, dsm_in):
    t = proj.shape[0]
    tm = _row_tile(t)

    def body(dq_ref, dk_ref, dv_ref, cq_ref, ckv_ref, cqn_ref, ckvn_ref, qg_ref, kvg_ref, wq_ref, wk_ref, wv_ref,
             cos_ref, sin_ref, din_ref, dcq_ref, dckv_ref, dsm_ref, dwq_ref, dwk_ref, dwv_ref, dqg_ref, dkvg_ref):
        i = pl.program_id(0)

        @pl.when(i == 0)
        def _():
            for r in (dwq_ref, dwk_ref, dwv_ref, dqg_ref, dkvg_ref):
                r[...] = jnp.zeros_like(r)

        cs, sn = cos_ref[...], sin_ref[...]
        lane = lax.broadcasted_iota(jnp.int32, (1, BLK), 1)

        def unrope(dy):
            return dy * cs + _swap_rope(dy * sn)

        dqp = jnp.concatenate([unrope(dq_ref[:, BLK * h:BLK * (h + 1)]) for h in range(MLA_H)], axis=1).astype(BF16)
        dwq_ref[...] += _tn(cqn_ref[...], dqp)
        dcqn = _nt(dqp, wq_ref[...])
        r, xh = _rms(cq_ref[...], None)
        dqg_ref[...] += jnp.sum(dcqn * xh, axis=0, keepdims=True)
        dxh = dcqn * qg_ref[...]
        dcq_ref[...] = r * (dxh - xh * jnp.mean(dxh * xh, axis=1, keepdims=True))

        dkn, dkr = [], jnp.zeros((tm, BLK), F32)
        for h in range(MLA_H):
            blk = dk_ref[:, BLK * h:BLK * (h + 1)]
            dkn.append(jnp.where(lane < MLA_NOPE, blk, 0.0))
            dkr += jnp.where((lane >= SM_KR) & (lane < SM_KR + MLA_ROPE), blk, 0.0)
        dknb = jnp.concatenate(dkn, axis=1).astype(BF16)
        dvb = dv_ref[...].astype(BF16)
        ckvn = ckvn_ref[...]
        dwk_ref[...] += _tn(ckvn, dknb)
        dwv_ref[...] += _tn(ckvn, dvb)
        dckvn = _nt(dknb, wk_ref[...]) + _nt(dvb, wv_ref[...])
        r2, kh = _rms(ckv_ref[...], None)
        dkvg_ref[...] += jnp.sum(dckvn * kh, axis=0, keepdims=True)
        dkh = dckvn * kvg_ref[...]
        dckv_ref[...] = r2 * (dkh - kh * jnp.mean(dkh * kh, axis=1, keepdims=True))
        dsm_ref[...] = din_ref[...] + jnp.where((lane >= SM_KR) & (lane < SM_KR + MLA_ROPE), unrope(dkr), 0.0)

    def rows(w, cb):
        return pl.BlockSpec((tm, w), lambda i: (i, cb))

    def whole(a):
        return pl.BlockSpec(a.shape, lambda i: (0, 0))

    def wshape(a):
        return jax.ShapeDtypeStruct(a.shape, F32)

    return pl.pallas_call(
        body, name="mla_pre_bwd", grid=(t // tm,),
        in_specs=[rows(512, 0), rows(512, 0), rows(256, 0), rows(MLA_QL, C_CQ // MLA_QL), rows(MLA_KVL, C_CKV // MLA_KVL),
                  rows(MLA_QL, 0), rows(MLA_KVL, 0), whole(qg), whole(kvg), whole(wq), whole(wk), whole(wv),
                  rows(BLK, 0), rows(BLK, 0), rows(BLK, 0)],
        out_specs=[rows(MLA_QL, 0), rows(MLA_KVL, 0), rows(BLK, 0), whole(wq), whole(wk), whole(wv), whole(qg), whole(kvg)],
        out_shape=[jax.ShapeDtypeStruct((t, MLA_QL), F32), jax.ShapeDtypeStruct((t, MLA_KVL), F32),
                   jax.ShapeDtypeStruct((t, BLK), F32), wshape(wq), wshape(wk), wshape(wv), wshape(qg), wshape(kvg)],
        compiler_params=_cp("arbitrary"),
    )(dq, dk, dv, proj, proj, cqn, ckvn, qg, kvg, wq, wk, wv, cosq, sinq, dsm_in)


def _slot_sum(me, own, recv_ref):
    gg = own.astype(F32)
    for s in range(N_DEV):
        gg = gg + jnp.where(me == s, 0.0, recv_ref[s].astype(F32))
    return gg


def adamw(w, m, v, g=None, recv=None, own=None, me_arr=None):
    shape = w.shape
    c = shape[-1]
    from_recv = recv is not None
    if not from_recv:
        me_arr = jnp.zeros((1,), jnp.int32)
    nl = len(recv) if from_recv else 1
    rws = w.size // c // nl
    tr = rws
    for d in (1024, 512, 352, 256, 128, 64, 32, 16, 8):
        if rws % d == 0 and d * c * 4 <= (2 << 20):
            tr = d
            break
    nt = rws // tr
    w2, m2, v2 = (a.reshape(nl, rws, c) for a in (w, m, v))
    if from_recv:
        gin = [a.reshape(N_DEV, rws, c) for a in list(recv) + list(own)]
    else:
        gin = [g.reshape(1, rws, c)]

    def body(me_ref, w_ref, m_ref, v_ref, *rest):
        g_refs, outs = rest[:len(gin)], rest[len(gin):]
        if from_recv:
            g_out, outs = outs[0], outs[1:]
            for li in range(nl):
                @pl.when(pl.program_id(0) == li)
                def _(li=li):
                    g_out[...] = _slot_sum(me_ref[0], g_refs[nl + li][...], g_refs[li])
            gg = g_out[...]
        else:
            gg = g_refs[0][...]
        d_ref, nm_ref, nv_ref = outs
        nm = B1 * m_ref[...] + (1.0 - B1) * gg
        nv = B2 * v_ref[...] + (1.0 - B2) * (gg * gg)
        mh = nm / (1.0 - B1 ** STEP)
        vh = nv / (1.0 - B2 ** STEP)
        d_ref[...] = -LR * (mh / (jnp.sqrt(vh) + AEPS) + WD * w_ref[...])
        nm_ref[...] = nm
        nv_ref[...] = nv

    row = pl.BlockSpec((None, tr, c), lambda l, i, me: (l, i, 0))
    if from_recv:
        gspecs = [pl.BlockSpec((N_DEV, tr, c), lambda l, i, me, li=li: (0, jnp.where(l == li, i, 0), 0))
                  for li in range(nl)]
        gspecs += [pl.BlockSpec((None, tr, c), lambda l, i, me, li=li: (me[0], jnp.where(l == li, i, 0), 0))
                   for li in range(nl)]
    else:
        gspecs = [row]
    nout = 4 if from_recv else 3
    outs = pl.pallas_call(
        body, name="adamw",
        grid_spec=pltpu.PrefetchScalarGridSpec(num_scalar_prefetch=1, grid=(nl, nt), in_specs=[row, row, row] + gspecs,
                                               out_specs=[row] * nout),
        out_shape=[jax.ShapeDtypeStruct((nl, rws, c), F32)] * nout,
        compiler_params=_cp("arbitrary", "arbitrary"),
    )(me_arr, w2, m2, v2, *gin)
    return tuple(o.reshape(shape) for o in outs)


def sum_slots(recv, own=None, me_arr=None):
    _, r, c = recv.shape
    if own is None:
        own, me_arr = recv, jnp.zeros((1,), jnp.int32)
        plain = True
    else:
        plain = False

    def body(me_ref, r_ref, own_ref, o_ref):
        if plain:
            gg = r_ref[0].astype(F32)
            for s in range(1, N_DEV):
                gg = gg + r_ref[s].astype(F32)
            o_ref[...] = gg
        else:
            o_ref[...] = _slot_sum(me_ref[0], own_ref[...], r_ref)

    return pl.pallas_call(
        body, name="sum_slots",
        grid_spec=pltpu.PrefetchScalarGridSpec(
            num_scalar_prefetch=1, grid=(1,),
            in_specs=[pl.BlockSpec((N_DEV, r, c), lambda i, me: (0, 0, 0)),
                      pl.BlockSpec((None, r, c), lambda i, me: (me[0], 0, 0))],
            out_specs=pl.BlockSpec((r, c), lambda i, me: (0, 0))),
        out_shape=jax.ShapeDtypeStruct((r, c), F32),
        compiler_params=_cp("arbitrary"),
    )(me_arr, recv, own)


_FLIPS = [(0, 0, 1), (0, 1, 0), (0, 1, 1), (1, 0, 0), (1, 0, 1), (1, 1, 0), (1, 1, 1)]
_ANY = pl.BlockSpec(memory_space=pl.ANY)


def _mesh_place():
    x, y, c = lax.axis_index("x"), lax.axis_index("y"), lax.axis_index("c")
    me = 4 * x + 2 * y + c
    peers = [((x + fx) % 2, (y + fy) % 2, (c + fc) % 2) for fx, fy, fc in _FLIPS]
    return me, peers


def place_own(src, l, dtype, me_arr):
    _, r, c = src.shape
    tr = r
    for d in (512, 352, 256, 128, 64, 32, 16, 8):
        if r % d == 0 and d * c * 4 <= (2 << 20):
            tr = d
            break

    def body(me_ref, s_ref, o_ref):
        o_ref[...] = s_ref[...].astype(dtype)

    return pl.pallas_call(
        body, name="place_own",
        grid_spec=pltpu.PrefetchScalarGridSpec(
            num_scalar_prefetch=1, grid=(r // tr,),
            in_specs=[pl.BlockSpec((None, tr, c), lambda i, me: (l, i, 0))],
            out_specs=pl.BlockSpec((None, tr, c), lambda i, me: (me[0], i, 0))),
        out_shape=jax.ShapeDtypeStruct((N_DEV, r, c), dtype),
        compiler_params=_cp("arbitrary"),
    )(me_arr, src)


_HBM = pl.BlockSpec(memory_space=pltpu.HBM)
_SEMS = pl.BlockSpec(memory_space=pltpu.SEMAPHORE)
_EFFECT = pltpu.SideEffectType.DATAFLOW_SIDE_EFFECTING


def exchange_start(mode, arrays, name):
    n = len(arrays)
    gather = mode == "gather"
    ns = 0 if gather else n
    zones = list(arrays) if gather else [lax.empty(a.shape, a.dtype) for a in arrays]
    ops = ([] if gather else list(arrays)) + zones

    def body(*refs):
        srcs, lands = refs[:ns], refs[ns:ns + n]
        send_sems, recv_sems = refs[ns + n], refs[ns + n + 1]
        token = refs[-1]
        me, peers = _mesh_place()
        ids = [4 * p[0] + 2 * p[1] + p[2] for p in peers]
        for j in range(n):
            for k in range(N_DEV - 1):
                src = lands[j].at[me] if gather else srcs[j].at[ids[k]]
                pltpu.make_async_remote_copy(src_ref=src, dst_ref=lands[j].at[me],
                                             send_sem=send_sems.at[j * (N_DEV - 1) + k],
                                             recv_sem=recv_sems.at[j * (N_DEV - 1) + k], device_id=peers[k],
                                             device_id_type=pl.DeviceIdType.MESH).start()
        token[...] = jnp.zeros_like(token)

    nsem = n * (N_DEV - 1)
    res = pl.pallas_call(
        body, name=name,
        in_specs=[_HBM] * (ns + n),
        out_specs=(_SEMS, _SEMS, *[_HBM] * (ns + n), pl.BlockSpec(memory_space=pltpu.VMEM)),
        out_shape=(pltpu.SemaphoreType.DMA((nsem,)), pltpu.SemaphoreType.DMA((nsem,)),
                   *[pltpu.HBM(a.shape, a.dtype) for a in ops], jax.ShapeDtypeStruct((8, BLK), F32)),
        input_output_aliases={i: 2 + i for i in range(ns + n)},
        compiler_params=pltpu.CompilerParams(has_side_effects=_EFFECT),
    )(*[pltpu.with_memory_space_constraint(a, pltpu.HBM) for a in ops])
    return dict(gather=gather, send=res[0], recv=res[1], srcs=list(res[2:2 + ns]), lands=list(res[2 + ns:2 + ns + n]),
                token=res[-1])


def exchange_wait(hd, idxs, name, after):
    gather = hd["gather"]
    n = len(idxs)
    ns = 0 if gather else n
    ops = ([] if gather else [hd["srcs"][j] for j in idxs]) + [hd["lands"][j] for j in idxs]

    def body(*refs):
        srcs, lands = refs[:ns], refs[ns:ns + n]
        send_sems, recv_sems = refs[ns + n], refs[ns + n + 1]
        me, peers = _mesh_place()
        ids = [4 * p[0] + 2 * p[1] + p[2] for p in peers]
        for p, j in enumerate(idxs):
            for k in range(N_DEV - 1):
                src = lands[p].at[me] if gather else srcs[p].at[ids[k]]
                cp = pltpu.make_async_remote_copy(src_ref=src, dst_ref=lands[p].at[ids[k]],
                                                  send_sem=send_sems.at[j * (N_DEV - 1) + k],
                                                  recv_sem=recv_sems.at[j * (N_DEV - 1) + k], device_id=peers[k],
                                                  device_id_type=pl.DeviceIdType.MESH)
                cp.wait_send()
                cp.wait_recv()

    res = pl.pallas_call(
        body, name=name,
        in_specs=[_HBM] * (ns + n) + [_SEMS, _SEMS, _ANY],
        out_specs=[_HBM] * (ns + n),
        out_shape=[pltpu.HBM(a.shape, a.dtype) for a in ops],
        input_output_aliases={i: i for i in range(ns + n)},
        compiler_params=pltpu.CompilerParams(has_side_effects=_EFFECT),
    )(*ops, hd["send"], hd["recv"], after)
    return list(res[:ns]), list(res[ns:])


def _pad_cols(a, n):
    return jnp.pad(a, ((0, 0),) * (a.ndim - 1) + ((0, n - a.shape[-1]),))


def w_in_to_padded(w):
    z = lambda n: jnp.zeros(w.shape[:-1] + (n,), w.dtype)
    return jnp.concatenate([
        w[..., 0:1280], w[..., 1288:2056], w[..., 2060:2316], w[..., 2316:2444],
        w[..., 1280:1288], w[..., 2056:2060], z(SM_KR - SM_F - FOX_H), w[..., 2444:2476], z(BLK - SM_KR - MLA_ROPE)], axis=-1)


def w_in_from_padded(g):
    s = C_SM
    return jnp.concatenate([
        g[..., 0:1280], g[..., s + SM_DT:s + SM_DT + 8], g[..., 1280:2048], g[..., s + SM_F:s + SM_F + 4],
        g[..., 2048:2304], g[..., 2304:2432], g[..., s + SM_KR:s + SM_KR + MLA_ROPE]], axis=-1)


def _unshard_cols(gth):
    n, r, c = gth.shape
    return jnp.transpose(gth, (1, 0, 2)).reshape(r, n * c)


def _shard_cols(full):
    r, nc = full.shape
    return jnp.transpose(full.reshape(r, N_DEV, nc // N_DEV), (1, 0, 2))


def mla_weights(uq_g, ukv_g):
    uq = _unshard_cols(uq_g)
    dqh = MLA_NOPE + MLA_ROPE
    wq = jnp.concatenate([_pad_cols(uq[:, dqh * h:dqh * (h + 1)], BLK) for h in range(MLA_H)], axis=1)
    wk = jnp.concatenate([_pad_cols(ukv_g[2 * h], BLK) for h in range(MLA_H)], axis=1)
    wv = jnp.concatenate([ukv_g[2 * h + 1] for h in range(MLA_H)], axis=1)
    return wq, wk, wv


def mla_weight_grads(dwq, dwk, dwv):
    dqh = MLA_NOPE + MLA_ROPE
    duq = _shard_cols(jnp.concatenate([dwq[:, BLK * h:BLK * h + dqh] for h in range(MLA_H)], axis=1))
    parts = []
    for h in range(MLA_H):
        parts += [dwk[:, BLK * h:BLK * h + MLA_NOPE], dwv[:, MLA_V * h:MLA_V * (h + 1)]]
    return duq, jnp.stack(parts, axis=0)


def rope_tables(t):
    pos = (jnp.arange(t, dtype=jnp.int32) - PAD).astype(F32)
    inv_freq = 1.0 / (10000.0 ** (jnp.arange(0, MLA_ROPE, 2, dtype=F32) / MLA_ROPE))
    ang = pos[:, None] * inv_freq[None, :]
    cos, sin = jnp.cos(ang), jnp.sin(ang)
    one, zero = jnp.ones((t, SM_KR), F32), jnp.zeros((t, SM_KR), F32)
    tail = BLK - SM_KR - MLA_ROPE
    cosq = jnp.concatenate([one, cos, cos, jnp.ones((t, tail), F32)], axis=1)
    sinq = jnp.concatenate([zero, -sin, sin, jnp.zeros((t, tail), F32)], axis=1)
    return cosq, sinq


def _lanes(v, off=0):
    return jnp.pad(v.astype(F32), (off, BLK - off - v.shape[0]))[None, :]


def layer_fwd(h, hb, getw, tabs):
    sv = {"h0": h, "h0b": hb}
    W = dict(getw("ffn1", hb))
    u, v, a = ffn_up(hb, W["g1"], W["u1"])
    W.update(getw("ffn1d", a))
    r1, h1, h1b = mm_res_ln(a, W["d1"], h, ALPHA, 0.5, W["ln1_g"], W["ln1_b"])
    sv.update(u1=u, v1=v, a1=a, r1=r1, h1=h1, h1b=h1b)
    W.update(getw("mix", h1b))
    proj = mm_nn(h1b, W["w_in"])
    xa = conv_fwd(proj, W["conv_w"], W["conv_b"])
    y_ssd, sprev = ssd_fwd(xa, proj, W["dtb"], W["alog"], W["dskip"], W["normg"])
    c_col, c_row = fox_pre(proj, W["fb"])
    y_fox, lse_f = attn_fwd(proj, proj, proj, C_FQ // 256, C_FK // 256, C_FV // 256, FOX_H, FOX_DH, FOX_DH,
                            FOX_DH ** -0.5, c_col, c_row, SM_F)
    q, k, vv, cqn, ckvn = mla_pre(proj, W["qg"], W["kvg"], W["wq"], W["wk"], W["wv"], *tabs)
    y_mla, lse_m = attn_fwd(q, k, vv, 0, 0, 0, MLA_H, BLK, MLA_V, (MLA_NOPE + MLA_ROPE) ** -0.5)
    mixcat = jnp.concatenate([y_ssd, y_fox, y_mla], axis=1)
    r2, h2, h2b = mm_res_ln(mixcat[None], W["w_out"][None], h1, ALPHA, 1.0, W["ln2_g"], W["ln2_b"])
    sv.update(proj=proj, xa=xa, sprev=sprev, c_col=c_col, c_row=c_row, lse_f=lse_f, q=q, k=k, v=vv, cqn=cqn, ckvn=ckvn,
              lse_m=lse_m, mixcat=mixcat, r2=r2, h2=h2, h2b=h2b)
    W.update(getw("ffn2", h2b))
    u, v, a = ffn_up(h2b, W["g2"], W["u2"])
    W.update(getw("ffn2d", a))
    r3, h3, h3b = mm_res_ln(a, W["d2"], h2, ALPHA, 0.5, W["ln3_g"], W["ln3_b"])
    sv.update(u2=u, v2=v, a2=a, r3=r3, W=W)
    return h3, h3b, sv


def ffn_bwd(parts, r, gamma, hb_in, u, v, a, wg, wu, wd, after=None):
    dr, dfb, dg, db = ln_bwd(parts, r, gamma, 0.5, after)
    du, dv = ffn_dact(dfb, wd, u, v)
    dh = mm_nt_reduce([(du, wg), (dv, wu)], D)
    gr = dict(d=mm_tn(a, dfb[None]), g=mm_tn(hb_in[None], du), u=mm_tn(hb_in[None], dv), ln_g=dg, ln_b=db)
    return dr, dh, gr


def layer_bwd(parts, sv, emit, tabs, after):
    G = {}
    W = sv["W"]
    dr3, dh2f, g2 = ffn_bwd(parts, sv["r3"], W["ln3_g"], sv["h2b"], sv["u2"], sv["v2"], sv["a2"], W["g2"], W["u2"], W["d2"],
                            after)
    G.update(g2=g2["g"], u2=g2["u"], d2=g2["d"], ln3_g=g2["ln_g"], ln3_b=g2["ln_b"])
    tok = emit("ffn2", G)
    dr2, dmixb, G["ln2_g"], G["ln2_b"] = ln_bwd([(dr3, ALPHA), (dh2f, 1.0)], sv["r2"], W["ln2_g"], 1.0, tok)
    dmc = mm_nt_reduce([(dmixb[None], W["w_out"][None])], D)
    G["w_out"] = mm_tn(sv["mixcat"][None], dmixb[None])[0]
    proj = sv["proj"]
    dxa, dz, dsm, G["normg"], G["dskip"], G["alog"], G["dtb"] = ssd_bwd(
        dmc, sv["xa"], proj, sv["sprev"], W["dtb"], W["alog"], W["dskip"], W["normg"])
    dxbc, G["conv_w"], G["conv_b"] = conv_bwd(dxa, proj, W["conv_w"], W["conv_b"])
    dfq, dfk, dfv, dcq, dck = attn_bwd(proj, proj, proj, dmc, sv["lse_f"], C_FQ // 256, C_FK // 256, C_FV // 256, 2,
                                       FOX_H, FOX_DH, FOX_DH, FOX_DH ** -0.5, sv["c_col"], sv["c_row"], SM_F)
    dsm, G["fb"] = fox_pre_bwd(dcq, dck, proj, W["fb"], dsm)
    dq, dk, dv = attn_bwd(sv["q"], sv["k"], sv["v"], dmc, sv["lse_m"], 0, 0, 0, 3, MLA_H, BLK, MLA_V,
                          (MLA_NOPE + MLA_ROPE) ** -0.5)
    dcql, dckv, dsm, G["wq"], G["wk"], G["wv"], G["qg"], G["kvg"] = mla_pre_bwd(
        dq, dk, dv, proj, sv["cqn"], sv["ckvn"], W["qg"], W["kvg"], W["wq"], W["wk"], W["wv"], *tabs, dsm)
    dproj = jnp.concatenate([dz, dxbc, dfq, dfk, dfv, dcql, dckv, dsm], axis=1).astype(BF16)
    dh1p = mm_nt_reduce([(dproj[None], W["w_in"][None])], D)
    G["w_in"] = mm_tn(sv["h1b"][None], dproj[None])[0]
    tok = emit("mix", G)
    dr1, dh0f, g1 = ffn_bwd([(dr2, ALPHA), (dh1p, 1.0)], sv["r1"], W["ln1_g"], sv["h0b"], sv["u1"], sv["v1"], sv["a1"],
                            W["g1"], W["u1"], W["d1"], tok)
    G.update(g1=g1["g"], u1=g1["u"], d1=g1["d"], ln1_g=g1["ln_g"], ln1_b=g1["ln_b"])
    tok = emit("ffn1", G)
    return [(dr1, ALPHA), (dh0f, 1.0)], G, tok


def local_step(x, target, meta_full, getw, emit):
    t = x.shape[0] + BLK
    tabs = rope_tables(t)
    h, hb = build_h0(meta_full, x)
    saved = []
    for l in range(NL):
        h, hb, sv = layer_fwd(h, hb, functools.partial(getw, l), tabs)
        saved.append(sv)
    dy, loss = loss_head(h, target)
    parts = [(dy, 1.0)]
    grads = [None] * NL
    tok = None
    for l in range(NL - 1, -1, -1):
        parts, grads[l], tok = layer_bwd(parts, saved[l], functools.partial(emit, l), tabs, tok)
    gx, gmeta = final_add(parts[0][0], parts[1][0], tok)
    return loss, gx, gmeta, grads


_SMALL = ["ln1_g", "ln1_b", "ln2_g", "ln2_b", "ln3_g", "ln3_b", "conv_b", "ssd_norm_g", "mla_q_norm_g",
          "mla_kv_norm_g", "dt_bias", "a_log", "d_skip", "fox_f_b"]
_SMALL_ROWS = 16
_BIG = ["ffn1_w_gate", "ffn1_w_up", "ffn1_w_down", "w_in", "conv_w", "mla_w_uq", "mla_w_ukv", "w_out",
        "ffn2_w_gate", "ffn2_w_up", "ffn2_w_down"]
_NAMES = ["meta", "ffn1_w_gate", "ffn1_w_up", "ffn1_w_down", "ln1_g", "ln1_b", "w_in", "conv_w", "conv_b", "dt_bias",
          "a_log", "d_skip", "ssd_norm_g", "fox_f_b", "mla_q_norm_g", "mla_w_uq", "mla_kv_norm_g", "mla_w_ukv", "w_out",
          "ln2_g", "ln2_b", "ffn2_w_gate", "ffn2_w_up", "ffn2_w_down", "ln3_g", "ln3_b"]


def pack_small(p):
    rows = []
    for l in range(NL):
        for n in _SMALL:
            rows.append(_pad_cols(p[n][l][None, :].astype(F32), D))
        rows.append(jnp.zeros((_SMALL_ROWS - len(_SMALL), D), F32))
    return jnp.concatenate(rows, axis=0)


def unpack_small(a, like):
    out = {}
    for i, n in enumerate(_SMALL):
        out[n] = jnp.stack([a[l * _SMALL_ROWS + i, :like[n].shape[1]] for l in range(NL)], axis=0)
    return out


_STAGES = {"ffn1": ["ffn1_w_gate", "ffn1_w_up", "ffn1_w_down"],
           "mix": ["w_in", "conv_w", "mla_w_uq", "mla_w_ukv", "w_out"],
           "ffn2": ["ffn2_w_gate", "ffn2_w_up", "ffn2_w_down"]}


_GROUPS = {"ffn1": ["ffn1_w_gate", "ffn1_w_up"], "ffn1d": ["ffn1_w_down"],
           "mix": ["w_in", "conv_w", "mla_w_uq", "mla_w_ukv", "w_out"],
           "ffn2": ["ffn2_w_gate", "ffn2_w_up"], "ffn2d": ["ffn2_w_down"]}


def stage_weights(l, group, g, rep):
    if group != "mix":
        i = group[3]
        if group.endswith("d"):
            return {"d" + i: g[f"ffn{i}_w_down"]}
        return {"g" + i: g[f"ffn{i}_w_gate"], "u" + i: g[f"ffn{i}_w_up"],
                "ln1_g" if i == "1" else "ln3_g": rep["ln1_g" if i == "1" else "ln3_g"][l][None, :],
                "ln1_b" if i == "1" else "ln3_b": rep["ln1_b" if i == "1" else "ln3_b"][l][None, :]}
    W = {}
    W["w_in"] = g["w_in"].reshape(D, N_INP)
    W["w_out"] = g["w_out"].reshape(D, D)
    W["wq"], W["wk"], W["wv"] = mla_weights(g["mla_w_uq"], g["mla_w_ukv"])
    W["conv_w"] = _unshard_cols(g["conv_w"])
    for k in ("ln2_g", "ln2_b", "conv_b"):
        W[k] = rep[k][l][None, :]
    W["normg"] = rep["ssd_norm_g"][l][None, :]
    W["qg"] = rep["mla_q_norm_g"][l][None, :]
    W["kvg"] = rep["mla_kv_norm_g"][l][None, :]
    W["dtb"] = _lanes(rep["dt_bias"][l], SM_DT)
    W["alog"] = _lanes(rep["a_log"][l], SM_DT)
    W["dskip"] = _lanes(rep["d_skip"][l], SM_DT)
    W["fb"] = _lanes(rep["fox_f_b"][l], SM_F)
    return W


def small_grads(G):
    return {"ln1_g": G["ln1_g"][0], "ln1_b": G["ln1_b"][0], "ln2_g": G["ln2_g"][0], "ln2_b": G["ln2_b"][0],
            "ln3_g": G["ln3_g"][0], "ln3_b": G["ln3_b"][0], "conv_b": G["conv_b"][0], "ssd_norm_g": G["normg"][0],
            "mla_q_norm_g": G["qg"][0], "mla_kv_norm_g": G["kvg"][0], "dt_bias": G["dtb"][0, :SSD_H],
            "a_log": G["alog"][0, :SSD_H], "d_skip": G["dskip"][0, :SSD_H], "fox_f_b": G["fb"][0, SM_F:SM_F + FOX_H]}


def big_grads(G, stage):
    if stage != "mix":
        i = stage[-1]
        return {f"ffn{i}_w_gate": G["g" + i], f"ffn{i}_w_up": G["u" + i], f"ffn{i}_w_down": G["d" + i]}
    duq, dukv = mla_weight_grads(G["wq"], G["wk"], G["wv"])
    return {"w_in": G["w_in"].reshape(N_DEV, D // N_DEV, N_INP), "w_out": G["w_out"].reshape(N_DEV, D // N_DEV, D),
            "mla_w_uq": duq, "mla_w_ukv": dukv, "conv_w": _shard_cols(G["conv_w"])}


def kernel(x, meta, ffn1_w_gate, ffn1_w_up, ffn1_w_down, ln1_g, ln1_b, w_in, conv_w, conv_b, dt_bias, a_log, d_skip, ssd_norm_g, fox_f_b, mla_q_norm_g, mla_w_uq, mla_kv_norm_g, mla_w_ukv, w_out, ln2_g, ln2_b, ffn2_w_gate, ffn2_w_up, ffn2_w_down, ln3_g, ln3_b, loss_target, m_meta, m_ffn1_w_gate, m_ffn1_w_up, m_ffn1_w_down, m_ln1_g, m_ln1_b, m_w_in, m_conv_w, m_conv_b, m_dt_bias, m_a_log, m_d_skip, m_ssd_norm_g, m_fox_f_b, m_mla_q_norm_g, m_mla_w_uq, m_mla_kv_norm_g, m_mla_w_ukv, m_w_out, m_ln2_g, m_ln2_b, m_ffn2_w_gate, m_ffn2_w_up, m_ffn2_w_down, m_ln3_g, m_ln3_b, v_meta, v_ffn1_w_gate, v_ffn1_w_up, v_ffn1_w_down, v_ln1_g, v_ln1_b, v_w_in, v_conv_w, v_conv_b, v_dt_bias, v_a_log, v_d_skip, v_ssd_norm_g, v_fox_f_b, v_mla_q_norm_g, v_mla_w_uq, v_mla_kv_norm_g, v_mla_w_ukv, v_w_out, v_ln2_g, v_ln2_b, v_ffn2_w_gate, v_ffn2_w_up, v_ffn2_w_down, v_ln3_g, v_ln3_b):
    vals = (meta, ffn1_w_gate, ffn1_w_up, ffn1_w_down, ln1_g, ln1_b, w_in, conv_w, conv_b, dt_bias, a_log, d_skip, ssd_norm_g, fox_f_b, mla_q_norm_g, mla_w_uq, mla_kv_norm_g, mla_w_ukv, w_out, ln2_g, ln2_b, ffn2_w_gate, ffn2_w_up, ffn2_w_down, ln3_g, ln3_b)
    moms = (m_meta, m_ffn1_w_gate, m_ffn1_w_up, m_ffn1_w_down, m_ln1_g, m_ln1_b, m_w_in, m_conv_w, m_conv_b, m_dt_bias, m_a_log, m_d_skip, m_ssd_norm_g, m_fox_f_b, m_mla_q_norm_g, m_mla_w_uq, m_mla_kv_norm_g, m_mla_w_ukv, m_w_out, m_ln2_g, m_ln2_b, m_ffn2_w_gate, m_ffn2_w_up, m_ffn2_w_down, m_ln3_g, m_ln3_b)
    vars_ = (v_meta, v_ffn1_w_gate, v_ffn1_w_up, v_ffn1_w_down, v_ln1_g, v_ln1_b, v_w_in, v_conv_w, v_conv_b, v_dt_bias, v_a_log, v_d_skip, v_ssd_norm_g, v_fox_f_b, v_mla_q_norm_g, v_mla_w_uq, v_mla_kv_norm_g, v_mla_w_ukv, v_w_out, v_ln2_g, v_ln2_b, v_ffn2_w_gate, v_ffn2_w_up, v_ffn2_w_down, v_ln3_g, v_ln3_b)
    P = dict(zip(_NAMES, vals))
    M = dict(zip(_NAMES, moms))
    V = dict(zip(_NAMES, vars_))
    me = 4 * lax.axis_index("x") + 2 * lax.axis_index("y") + lax.axis_index("c")

    me_arr = me.astype(jnp.int32).reshape(1)
    src = dict(P)
    src["w_in"] = w_in_to_padded(P["w_in"])
    order = [("meta", 0)] + [(n, l) for l in range(NL) for names in _GROUPS.values() for n in names]
    zone_of = {nl_: i for i, nl_ in enumerate(order)}
    zones = [place_own(P["meta"][None], 0, F32, me_arr)]
    zones += [place_own(src[n], l, F32 if n == "conv_w" else BF16, me_arr) for n, l in order[1:]]
    hg = exchange_start("gather", zones, "gather_start")
    meta_full = _unshard_cols(exchange_wait(hg, [0], "gather_wait_meta", hg["token"])[1][0])

    def getw(l, group, after):
        names = _GROUPS[group]
        lands = exchange_wait(hg, [zone_of[(n, l)] for n in names], f"gather_wait_{l}_{group}", after)[1]
        return stage_weights(l, group, dict(zip(names, lands)), P)

    sent = {}

    def emit(l, stage, G):
        bg = big_grads(G, stage)
        sent[(l, stage)] = exchange_start("scatter", [bg[n] for n in _STAGES[stage]], f"scatter_start_{l}_{stage}")
        return sent[(l, stage)]["token"]

    loss, gx, gmeta, grads = local_step(x[0], loss_target[0], meta_full, getw, emit)

    small = jnp.concatenate([pack_small({n: jnp.stack([small_grads(g)[n] for g in grads]) for n in _SMALL}), gmeta], axis=0)
    hs = exchange_start("gather", [place_own(small[None], 0, F32, me_arr)], "small_start")

    out = {}
    after = hs["token"]
    for stage in ("ffn2", "mix", "ffn1"):
        names = _STAGES[stage]
        got = [exchange_wait(sent[(l, stage)], list(range(len(names))), f"scatter_wait_{l}_{stage}", after)
               for l in range(NL - 1, -1, -1)][::-1]
        for i, n in enumerate(names):
            own = [got[l][0][i] for l in range(NL)]
            recv = [got[l][1][i] for l in range(NL)]
            if n == "w_in":
                g = jnp.stack([w_in_from_padded(sum_slots(recv[l], own[l], me_arr)) for l in range(NL)])
                out[n] = (g,) + adamw(P[n], M[n], V[n], g=g)
            else:
                out[n] = adamw(P[n], M[n], V[n], recv=recv, own=own, me_arr=me_arr)
        after = out[names[-1]][1]
    gsmall = sum_slots(exchange_wait(hs, [0], "small_wait", after)[1][0])
    gm = lax.dynamic_slice(gsmall[NL * _SMALL_ROWS:], (0, me * (D // N_DEV)), (N_META, D // N_DEV))
    out["meta"] = (gm,) + adamw(P["meta"], M["meta"], V["meta"], g=gm)
    gs = gsmall[:NL * _SMALL_ROWS]
    sd, sm_, sv_ = adamw(pack_small(P), pack_small(M), pack_small(V), g=gs)
    ups = [unpack_small(a, P) for a in (gs, sd, sm_, sv_)]
    for n in _SMALL:
        out[n] = tuple(u[n] for u in ups)

    loss_all = lax.psum(loss[0, 0], ("x", "y", "c"))
    flat = [loss_all, gx[None]]
    for k in range(4):
        flat += [out[n][k] for n in _NAMES]
    return tuple(flat)
```

```python
import functools

import jax
import jax.numpy as jnp
from jax import lax
from jax.experimental import pallas as pl
from jax.experimental.pallas import tpu as pltpu

F32, BF16 = jnp.float32, jnp.bfloat16
HI = lax.Precision.HIGHEST

N_DEV = 8
D = 1024
NL = 2
N_META = 16
BLK = 128
PAD = BLK - N_META
D_FF = 2816
HS = D_FF // N_DEV
SSD_H, SSD_P, SSD_N, SSD_G = 8, 64, 64, 2
SSD_D = SSD_H * SSD_P
CONV_K = 4
CONV_D = SSD_D + 2 * SSD_G * SSD_N
FOX_H, FOX_DH = 4, 64
MLA_H, MLA_QL, MLA_KVL, MLA_NOPE, MLA_ROPE, MLA_V = 4, 256, 128, 64, 32, 64
N_IN = 2476
C_Z, C_XBC, C_FQ, C_FK, C_FV, C_CQ, C_CKV, C_SM, N_INP = 0, 512, 1280, 1536, 1792, 2048, 2304, 2432, 2560
SM_DT, SM_F, SM_KR = 0, 8, 64
ALPHA = (2 * NL) ** 0.25
EPS = 1e-5
NEG = -1e30
LR, B1, B2, AEPS, WD, STEP = 0.001, 0.9, 0.999, 1e-08, 0.01, 10
VMEM_MB = 56


def _cp(*sem):
    return pltpu.CompilerParams(dimension_semantics=sem, vmem_limit_bytes=VMEM_MB << 20)


def _nn(a, b):
    return lax.dot_general(a, b, (((1,), (0,)), ((), ())), preferred_element_type=F32)


def _nt(a, b):
    return lax.dot_general(a, b, (((1,), (1,)), ((), ())), preferred_element_type=F32)


def _tn(a, b):
    return lax.dot_general(a, b, (((0,), (0,)), ((), ())), preferred_element_type=F32)


def _nn_hi(a, b):
    return lax.dot_general(a, b, (((1,), (0,)), ((), ())), precision=HI, preferred_element_type=F32)


def _row_tile(t):
    for d in range(640, 15, -16):
        if t % d == 0:
            return d
    raise ValueError(t)


def _sig(x):
    return 1.0 / (1.0 + jnp.exp(-x))


def _tri(lower=True):
    r = lax.broadcasted_iota(jnp.int32, (BLK, BLK), 0)
    c = lax.broadcasted_iota(jnp.int32, (BLK, BLK), 1)
    return (r >= c) if lower else (r <= c)


def build_h0(meta_full, x):
    s = x.shape[0]
    nb = s // BLK + 1

    def body(m_ref, x_ref, h_ref, hb_ref):
        i = pl.program_id(0)

        @pl.when(i == 0)
        def _():
            h = jnp.concatenate([jnp.zeros((PAD, D), F32), m_ref[...]], axis=0)
            h_ref[...] = h
            hb_ref[...] = h.astype(BF16)

        @pl.when(i > 0)
        def _():
            h_ref[...] = x_ref[...]
            hb_ref[...] = x_ref[...].astype(BF16)

    return pl.pallas_call(
        body, name="build_h0", grid=(nb,),
        in_specs=[pl.BlockSpec((N_META, D), lambda i: (0, 0)),
                  pl.BlockSpec((BLK, D), lambda i: (jnp.maximum(i - 1, 0), 0))],
        out_specs=[pl.BlockSpec((BLK, D), lambda i: (i, 0))] * 2,
        out_shape=[jax.ShapeDtypeStruct((nb * BLK, D), F32), jax.ShapeDtypeStruct((nb * BLK, D), BF16)],
        compiler_params=_cp("arbitrary"),
    )(meta_full, x)


FT = 256


def _layer_norm(r, gamma, beta):
    mu = jnp.mean(r, axis=1, keepdims=True)
    xc = r - mu
    var = jnp.mean(xc * xc, axis=1, keepdims=True)
    return xc * lax.rsqrt(var + EPS) * gamma + beta


def ffn_fwd(hb, res, wg, wu, wd, gamma, beta):
    t = hb.shape[0]
    f = wg.shape[0]
    tm = _row_tile(t)
    nj = f // FT

    def body(h_ref, res_ref, wg_ref, wu_ref, wd_ref, g_ref, be_ref, u_ref, v_ref, r_ref, y_ref, yb_ref, acc):
        j = pl.program_id(1)

        @pl.when(j == 0)
        def _():
            acc[...] = jnp.zeros_like(acc)

        h = h_ref[...]
        u = _nt(h, wg_ref[...])
        v = _nt(h, wu_ref[...])
        u_ref[...] = u.astype(BF16)
        v_ref[...] = v.astype(BF16)
        acc[...] += _nn((u * _sig(u) * v).astype(BF16), wd_ref[...])

        @pl.when(j == nj - 1)
        def _():
            r = ALPHA * res_ref[...] + 0.5 * acc[...]
            y = _layer_norm(r, g_ref[...], be_ref[...])
            r_ref[...] = r
            y_ref[...] = y
            yb_ref[...] = y.astype(BF16)

    row = pl.BlockSpec((tm, D), lambda i, j: (i, 0))
    vec = pl.BlockSpec((1, D), lambda i, j: (0, 0))
    wsp = pl.BlockSpec((FT, D), lambda i, j: (j, 0))
    act = pl.BlockSpec((tm, FT), lambda i, j: (i, j))
    return pl.pallas_call(
        body, name="ffn_fwd", grid=(t // tm, nj),
        in_specs=[row, row, wsp, wsp, wsp, vec, vec],
        out_specs=[act, act, row, row, row],
        out_shape=[jax.ShapeDtypeStruct((t, f), BF16), jax.ShapeDtypeStruct((t, f), BF16),
                   jax.ShapeDtypeStruct((t, D), F32), jax.ShapeDtypeStruct((t, D), F32),
                   jax.ShapeDtypeStruct((t, D), BF16)],
        scratch_shapes=[pltpu.VMEM((tm, D), F32)],
        compiler_params=_cp("arbitrary", "arbitrary"),
    )(hb, res, wg, wu, wd, gamma, beta)


def ffn_bwd_act(dfb, u, v, wg, wu, wd):
    t, f = u.shape
    tm = _row_tile(t)

    def body(df_ref, u_ref, v_ref, wg_ref, wu_ref, wd_ref, du_ref, dv_ref, dh_ref):
        j = pl.program_id(1)
        da = _nt(df_ref[...], wd_ref[...])
        uu = u_ref[...].astype(F32)
        sg = _sig(uu)
        du = (da * v_ref[...].astype(F32) * (sg * (1.0 + uu * (1.0 - sg)))).astype(BF16)
        dv = (da * uu * sg).astype(BF16)
        du_ref[...] = du
        dv_ref[...] = dv
        tot = _nn(du, wg_ref[...]) + _nn(dv, wu_ref[...])

        @pl.when(j == 0)
        def _():
            dh_ref[...] = tot

        @pl.when(j > 0)
        def _():
            dh_ref[...] += tot

    row = pl.BlockSpec((tm, D), lambda i, j: (i, 0))
    wsp = pl.BlockSpec((FT, D), lambda i, j: (j, 0))
    act = pl.BlockSpec((tm, FT), lambda i, j: (i, j))
    return pl.pallas_call(
        body, name="ffn_bwd_act", grid=(t // tm, f // FT),
        in_specs=[row, act, act, wsp, wsp, wsp],
        out_specs=[act, act, row],
        out_shape=[jax.ShapeDtypeStruct((t, f), BF16), jax.ShapeDtypeStruct((t, f), BF16),
                   jax.ShapeDtypeStruct((t, D), F32)],
        compiler_params=_cp("arbitrary", "arbitrary"),
    )(dfb, u, v, wg, wu, wd)


def ffn_bwd_w(hb, dfb, u, v, du, dv):
    t, f = u.shape

    def body(h_ref, df_ref, u_ref, v_ref, du_ref, dv_ref, dwg_ref, dwu_ref, dwd_ref, ht, dft):
        @pl.when(pl.program_id(0) == 0)
        def _():
            ht[...] = h_ref[...].T
            dft[...] = df_ref[...].T

        uu = u_ref[...].astype(F32)
        a = (uu * _sig(uu) * v_ref[...].astype(F32)).astype(BF16)
        dwg_ref[...] = _nn(ht[...], du_ref[...]).T.astype(BF16)
        dwu_ref[...] = _nn(ht[...], dv_ref[...]).T.astype(BF16)
        dwd_ref[...] = _nn(dft[...], a).T.astype(BF16)

    full = pl.BlockSpec((t, D), lambda j: (0, 0))
    act = pl.BlockSpec((t, FT), lambda j: (0, j))
    wsp = pl.BlockSpec((FT, D), lambda j: (j, 0))
    return pl.pallas_call(
        body, name="ffn_bwd_w", grid=(f // FT,),
        in_specs=[full, full, act, act, act, act],
        out_specs=[wsp] * 3,
        out_shape=[jax.ShapeDtypeStruct((f, D), BF16)] * 3,
        scratch_shapes=[pltpu.VMEM((D, t), BF16), pltpu.VMEM((D, t), BF16)],
        compiler_params=_cp("arbitrary"),
    )(hb, dfb, u, v, du, dv)


def mm_res_ln(a, b, res, alpha, scale, gamma, beta):
    g, t, k = a.shape
    tm = _row_tile(t)

    def body(a_ref, b_ref, res_ref, g_ref, be_ref, r_ref, y_ref, yb_ref, acc):
        gi = pl.program_id(1)

        @pl.when(gi == 0)
        def _():
            acc[...] = jnp.zeros_like(acc)

        acc[...] += _nn(a_ref[...], b_ref[...])

        @pl.when(gi == g - 1)
        def _():
            r = alpha * res_ref[...] + scale * acc[...]
            mu = jnp.mean(r, axis=1, keepdims=True)
            xc = r - mu
            var = jnp.mean(xc * xc, axis=1, keepdims=True)
            y = xc * lax.rsqrt(var + EPS) * g_ref[...] + be_ref[...]
            r_ref[...] = r
            y_ref[...] = y
            yb_ref[...] = y.astype(BF16)

    row = pl.BlockSpec((tm, D), lambda i, gi: (i, 0))
    vec = pl.BlockSpec((1, D), lambda i, gi: (0, 0))
    return pl.pallas_call(
        body, name="mm_res_ln", grid=(t // tm, g),
        in_specs=[pl.BlockSpec((None, tm, k), lambda i, gi: (gi, i, 0)),
                  pl.BlockSpec((None, k, D), lambda i, gi: (gi, 0, 0)), row, vec, vec],
        out_specs=[row] * 3,
        out_shape=[jax.ShapeDtypeStruct((t, D), F32), jax.ShapeDtypeStruct((t, D), F32),
                   jax.ShapeDtypeStruct((t, D), BF16)],
        scratch_shapes=[pltpu.VMEM((tm, D), F32)],
        compiler_params=_cp("arbitrary", "arbitrary"),
    )(a, b, res, gamma, beta)


def mm_nn(a, b, tn=512):
    t, k = a.shape
    n = b.shape[1]
    tm = _row_tile(t)

    def body(a_ref, b_ref, o_ref):
        o_ref[...] = _nn(a_ref[...], b_ref[...])

    return pl.pallas_call(
        body, name="mm_nn", grid=(n // tn, t // tm),
        in_specs=[pl.BlockSpec((tm, k), lambda j, i: (i, 0)), pl.BlockSpec((k, tn), lambda j, i: (0, j))],
        out_specs=pl.BlockSpec((tm, tn), lambda j, i: (i, j)),
        out_shape=jax.ShapeDtypeStruct((t, n), F32),
        compiler_params=_cp("arbitrary", "arbitrary"),
    )(a, b)


def mm_nt_reduce(pairs, n):
    g, t, _ = pairs[0][0].shape
    tm = _row_tile(t)
    npair = len(pairs)

    def body(*refs):
        o_ref = refs[-1]
        gi = pl.program_id(1)
        tot = _nt(refs[0][...], refs[1][...])
        for p in range(1, npair):
            tot += _nt(refs[2 * p][...], refs[2 * p + 1][...])

        @pl.when(gi == 0)
        def _():
            o_ref[...] = tot

        @pl.when(gi > 0)
        def _():
            o_ref[...] += tot

    in_specs, args = [], []
    for x, w in pairs:
        k = x.shape[2]
        in_specs += [pl.BlockSpec((None, tm, k), lambda i, gi: (gi, i, 0)),
                     pl.BlockSpec((None, n, k), lambda i, gi: (gi, 0, 0))]
        args += [x, w]
    return pl.pallas_call(
        body, name="mm_nt_reduce", grid=(t // tm, g),
        in_specs=in_specs, out_specs=pl.BlockSpec((tm, n), lambda i, gi: (i, 0)),
        out_shape=jax.ShapeDtypeStruct((t, n), F32),
        compiler_params=_cp("arbitrary", "arbitrary"),
    )(*args)


def mm_tn(x, y, out_dtype=BF16):
    gx, t, k = x.shape
    gy, _, n = y.shape
    g = max(gx, gy)
    tm = _row_tile(t)
    nt = t // tm

    def body(x_ref, y_ref, o_ref, acc):
        i = pl.program_id(1)

        @pl.when(i == 0)
        def _():
            acc[...] = jnp.zeros_like(acc)

        acc[...] += _tn(x_ref[...], y_ref[...])

        @pl.when(i == nt - 1)
        def _():
            o_ref[...] = acc[...].astype(out_dtype)

    return pl.pallas_call(
        body, name="mm_tn", grid=(g, nt),
        in_specs=[pl.BlockSpec((None, tm, k), (lambda gi, i: (gi, i, 0)) if gx > 1 else (lambda gi, i: (0, i, 0))),
                  pl.BlockSpec((None, tm, n), (lambda gi, i: (gi, i, 0)) if gy > 1 else (lambda gi, i: (0, i, 0)))],
        out_specs=pl.BlockSpec((None, k, n), lambda gi, i: (gi, 0, 0)),
        out_shape=jax.ShapeDtypeStruct((g, k, n), out_dtype),
        scratch_shapes=[pltpu.VMEM((k, n), F32)],
        compiler_params=_cp("arbitrary", "arbitrary"),
    )(x, y)


def ln_bwd(parts, r, gamma, out_scale, after=None):
    t = r.shape[0]
    tm = _row_tile(t)
    scales = [s for _, s in parts]
    npart = len(parts)
    extra = [] if after is None else [after]

    def body(*refs):
        refs = refs[len(extra):]
        r_ref, g_ref = refs[npart], refs[npart + 1]
        dr_ref, drb_ref, dg_ref, db_ref = refs[npart + 2:]
        i = pl.program_id(0)
        dy = scales[0] * refs[0][...]
        for p in range(1, npart):
            dy += scales[p] * refs[p][...]
        rr = r_ref[...]
        mu = jnp.mean(rr, axis=1, keepdims=True)
        xc = rr - mu
        rstd = lax.rsqrt(jnp.mean(xc * xc, axis=1, keepdims=True) + EPS)
        xh = xc * rstd
        dxh = dy * g_ref[...]
        m1 = jnp.mean(dxh, axis=1, keepdims=True)
        m2 = jnp.mean(dxh * xh, axis=1, keepdims=True)
        dr = rstd * (dxh - m1 - xh * m2)
        dr_ref[...] = dr
        drb_ref[...] = (out_scale * dr).astype(BF16)
        dg = jnp.sum(dy * xh, axis=0, keepdims=True)
        db = jnp.sum(dy, axis=0, keepdims=True)

        @pl.when(i == 0)
        def _():
            dg_ref[...] = dg
            db_ref[...] = db

        @pl.when(i > 0)
        def _():
            dg_ref[...] += dg
            db_ref[...] += db

    row = pl.BlockSpec((tm, D), lambda i: (i, 0))
    vec = pl.BlockSpec((1, D), lambda i: (0, 0))
    return pl.pallas_call(
        body, name="ln_bwd", grid=(t // tm,),
        in_specs=[_ANY] * len(extra) + [row] * (npart + 1) + [vec],
        out_specs=[row, row, vec, vec],
        out_shape=[jax.ShapeDtypeStruct((t, D), F32), jax.ShapeDtypeStruct((t, D), BF16),
                   jax.ShapeDtypeStruct((1, D), F32), jax.ShapeDtypeStruct((1, D), F32)],
        compiler_params=_cp("arbitrary"),
    )(*extra, *[p for p, _ in parts], r, gamma)


def loss_head(h, target):
    t = h.shape[0]
    nb = t // BLK

    def body(h_ref, t_ref, dy_ref, l_ref):
        i = pl.program_id(0)

        @pl.when(i == 0)
        def _():
            dy_ref[...] = jnp.zeros_like(dy_ref)
            l_ref[...] = jnp.zeros_like(l_ref)

        @pl.when(i > 0)
        def _():
            err = h_ref[...] - t_ref[...]
            dy_ref[...] = err * (1.0 / D)
            l_ref[...] += (0.5 / D) * jnp.sum(err * err, keepdims=True)

    return pl.pallas_call(
        body, name="loss_head", grid=(nb,),
        in_specs=[pl.BlockSpec((BLK, D), lambda i: (i, 0)),
                  pl.BlockSpec((BLK, D), lambda i: (jnp.maximum(i - 1, 0), 0))],
        out_specs=[pl.BlockSpec((BLK, D), lambda i: (i, 0)), pl.BlockSpec((1, 1), lambda i: (0, 0))],
        out_shape=[jax.ShapeDtypeStruct((t, D), F32), jax.ShapeDtypeStruct((1, 1), F32)],
        compiler_params=_cp("arbitrary"),
    )(h, target)


def final_add(dr, dh, after=None):
    t = dr.shape[0]
    nb = t // BLK
    extra = [] if after is None else [after]

    def body(*refs):
        a_ref, b_ref, gx_ref, gm_ref = refs[len(extra):]
        i = pl.program_id(0)
        tot = ALPHA * a_ref[...] + b_ref[...]

        @pl.when(i == 0)
        def _():
            gm_ref[...] = tot[PAD:, :]

        @pl.when(i > 0)
        def _():
            gx_ref[...] = tot

    blk = pl.BlockSpec((BLK, D), lambda i: (i, 0))
    return pl.pallas_call(
        body, name="final_add", grid=(nb,),
        in_specs=[_ANY] * len(extra) + [blk, blk],
        out_specs=[pl.BlockSpec((BLK, D), lambda i: (jnp.maximum(i - 1, 0), 0)),
                   pl.BlockSpec((N_META, D), lambda i: (0, 0))],
        out_shape=[jax.ShapeDtypeStruct((t - BLK, D), F32), jax.ShapeDtypeStruct((N_META, D), F32)],
        compiler_params=_cp("arbitrary"),
    )(*extra, dr, dh)


def _valid_rows(nrows, first_row):
    return (first_row + lax.broadcasted_iota(jnp.int32, (nrows, 1), 0)) >= PAD


def conv_fwd(proj, conv_w, conv_b):
    t = proj.shape[0]
    c0 = C_XBC // BLK

    def body(x_ref, w_ref, b_ref, o_ref):
        ok = _valid_rows(t, 0)
        x = jnp.where(ok, x_ref[...], 0.0)
        w = w_ref[...]
        acc = b_ref[...] + w[CONV_K - 1:CONV_K, :] * x
        for s in range(1, CONV_K):
            acc += w[CONV_K - 1 - s:CONV_K - s, :] * pltpu.roll(x, s, 0)
        o_ref[...] = jnp.where(ok, acc * _sig(acc), 0.0)

    return pl.pallas_call(
        body, name="conv_fwd", grid=(CONV_D // BLK,),
        in_specs=[pl.BlockSpec((t, BLK), lambda j: (0, c0 + j)),
                  pl.BlockSpec((CONV_K, BLK), lambda j: (0, j)), pl.BlockSpec((1, BLK), lambda j: (0, j))],
        out_specs=pl.BlockSpec((t, BLK), lambda j: (0, j)),
        out_shape=jax.ShapeDtypeStruct((t, CONV_D), F32),
        compiler_params=_cp("arbitrary"),
    )(proj, conv_w, conv_b)


def conv_bwd(dxa, proj, conv_w, conv_b):
    t = proj.shape[0]
    c0 = C_XBC // BLK

    def body(d_ref, x_ref, w_ref, b_ref, dx_ref, dw_ref, db_ref):
        ok = _valid_rows(t, 0)
        x = jnp.where(ok, x_ref[...], 0.0)
        w = w_ref[...]
        xs = [x] + [pltpu.roll(x, s, 0) for s in range(1, CONV_K)]
        acc = b_ref[...] + w[CONV_K - 1:CONV_K, :] * x
        for s in range(1, CONV_K):
            acc += w[CONV_K - 1 - s:CONV_K - s, :] * xs[s]
        sg = _sig(acc)
        dxc = jnp.where(ok, d_ref[...] * (sg * (1.0 + acc * (1.0 - sg))), 0.0)
        db_ref[...] = jnp.sum(dxc, axis=0, keepdims=True)
        dw_ref[...] = jnp.concatenate(
            [jnp.sum(dxc * xs[CONV_K - 1 - k], axis=0, keepdims=True) for k in range(CONV_K)], axis=0)
        dx = w[CONV_K - 1:CONV_K, :] * dxc
        for s in range(1, CONV_K):
            dx += w[CONV_K - 1 - s:CONV_K - s, :] * pltpu.roll(dxc, t - s, 0)
        dx_ref[...] = jnp.where(ok, dx, 0.0)

    col = pl.BlockSpec((t, BLK), lambda j: (0, j))
    return pl.pallas_call(
        body, name="conv_bwd", grid=(CONV_D // BLK,),
        in_specs=[col, pl.BlockSpec((t, BLK), lambda j: (0, c0 + j)),
                  pl.BlockSpec((CONV_K, BLK), lambda j: (0, j)), pl.BlockSpec((1, BLK), lambda j: (0, j))],
        out_specs=[col, pl.BlockSpec((CONV_K, BLK), lambda j: (0, j)), pl.BlockSpec((1, BLK), lambda j: (0, j))],
        out_shape=[jax.ShapeDtypeStruct((t, CONV_D), F32), jax.ShapeDtypeStruct((CONV_K, CONV_D), F32),
                   jax.ShapeDtypeStruct((1, CONV_D), F32)],
        compiler_params=_cp("arbitrary"),
    )(dxa, proj, conv_w, conv_b)


def _softplus(x):
    return jnp.maximum(x, 0.0) + jnp.log(1.0 + jnp.exp(-jnp.abs(x)))


def _ssd_chunk(xa, sm, dtb, alog, ok):
    dt = jnp.where(ok, _softplus(sm + dtb), 0.0)
    amat = -jnp.exp(alog)
    a = dt * amat
    ac = _nn_hi(_tri().astype(F32), a)
    act = ac.T
    return dt, amat, ac, act


def _ssd_head(xa, dt, ac, act, h, cb, sp):
    g = h // (SSD_H // SSD_G)
    xs = xa[:, SSD_P * h:SSD_P * (h + 1)]
    bg = xa[:, SSD_D + SSD_N * g:SSD_D + SSD_N * (g + 1)]
    cg = xa[:, SSD_D + SSD_G * SSD_N + SSD_N * g:SSD_D + SSD_G * SSD_N + SSD_N * (g + 1)]
    dth = dt[:, h:h + 1]
    ach = ac[:, h:h + 1]
    acth = act[h:h + 1, :]
    xdt = xs * dth
    seg = jnp.where(_tri(), jnp.exp(jnp.minimum(ach - acth, 0.0)), 0.0)
    m = cb * seg
    yd = _nn(m.astype(BF16), xdt.astype(BF16))
    last = ac[BLK - 1:BLK, h:h + 1]
    dec = jnp.exp(last - ach)
    e = jnp.exp(ach)
    yo = _nn(cg.astype(BF16), sp.astype(BF16)) * e
    return xs, bg, cg, dth, ach, xdt, seg, m, yd, last, dec, e, yo


def ssd_fwd(xa, proj, dtb, alog, dskip, normg):
    t = xa.shape[0]
    nb = t // BLK
    gw = SSD_D // SSD_G

    def body(xa_ref, z_ref, sm_ref, dtb_ref, al_ref, ds_ref, ng_ref, y_ref, sp_ref, st):
        c = pl.program_id(0)

        @pl.when(c == 0)
        def _():
            st[...] = jnp.zeros_like(st)

        ok = _valid_rows(BLK, c * BLK)
        xa = xa_ref[...]
        dt, _, ac, act = _ssd_chunk(xa, sm_ref[...], dtb_ref[...], al_ref[...], ok)
        sp_ref[...] = st[...]
        ys = []
        cbs = {}
        for h in range(SSD_H):
            g = h // (SSD_H // SSD_G)
            if g not in cbs:
                bg = xa[:, SSD_D + SSD_N * g:SSD_D + SSD_N * (g + 1)]
                cg = xa[:, SSD_D + SSD_G * SSD_N + SSD_N * g:SSD_D + SSD_G * SSD_N + SSD_N * (g + 1)]
                cbs[g] = _nt(cg.astype(BF16), bg.astype(BF16))
            sp = st[:, SSD_P * h:SSD_P * (h + 1)]
            xs, bg, cg, dth, ach, xdt, seg, m, yd, last, dec, e, yo = _ssd_head(xa, dt, ac, act, h, cbs[g], sp)
            sloc = _tn((bg * dec).astype(BF16), xdt.astype(BF16))
            st[:, SSD_P * h:SSD_P * (h + 1)] = jnp.exp(last) * sp + sloc
            ys.append(yd + yo + ds_ref[:, h:h + 1] * xs)
        y = jnp.concatenate(ys, axis=1)
        z = z_ref[...]
        yg = y * (z * _sig(z))
        outs = []
        for g in range(SSD_G):
            v = yg[:, gw * g:gw * (g + 1)]
            outs.append(v * lax.rsqrt(jnp.mean(v * v, axis=1, keepdims=True) + EPS))
        y_ref[...] = (jnp.concatenate(outs, axis=1) * ng_ref[...]).astype(BF16)

    vec = pl.BlockSpec((1, BLK), lambda c: (0, 0))
    return pl.pallas_call(
        body, name="ssd_fwd", grid=(nb,),
        in_specs=[pl.BlockSpec((BLK, CONV_D), lambda c: (c, 0)),
                  pl.BlockSpec((BLK, SSD_D), lambda c: (c, C_Z // SSD_D)),
                  pl.BlockSpec((BLK, BLK), lambda c: (c, C_SM // BLK)),
                  vec, vec, vec, pl.BlockSpec((1, SSD_D), lambda c: (0, 0))],
        out_specs=[pl.BlockSpec((BLK, SSD_D), lambda c: (c, 0)),
                   pl.BlockSpec((None, SSD_N, SSD_D), lambda c: (c, 0, 0))],
        out_shape=[jax.ShapeDtypeStruct((t, SSD_D), BF16), jax.ShapeDtypeStruct((nb, SSD_N, SSD_D), F32)],
        scratch_shapes=[pltpu.VMEM((SSD_N, SSD_D), F32)],
        compiler_params=_cp("arbitrary"),
    )(xa, proj, proj, dtb, alog, dskip, normg)


def _lane_put(col, lane):
    li = lax.broadcasted_iota(jnp.int32, (col.shape[0], BLK), 1)
    return jnp.where(li == lane, col, 0.0)


def ssd_bwd(dmix, xa, proj, sprev, dtb, alog, dskip, normg):
    t = xa.shape[0]
    nb = t // BLK
    gw = SSD_D // SSD_G
    rev = lambda c: nb - 1 - c

    def body(dy_ref, xa_ref, z_ref, sm_ref, sp_ref, dtb_ref, al_ref, ds_ref, ng_ref,
             dxa_ref, dz_ref, dsm_ref, dng_ref, dds_ref, dal_ref, ddtb_ref, dst):
        c = pl.program_id(0)

        @pl.when(c == 0)
        def _():
            dst[...] = jnp.zeros_like(dst)
            dng_ref[...] = jnp.zeros_like(dng_ref)
            dds_ref[...] = jnp.zeros_like(dds_ref)
            dal_ref[...] = jnp.zeros_like(dal_ref)
            ddtb_ref[...] = jnp.zeros_like(ddtb_ref)

        ok = _valid_rows(BLK, rev(c) * BLK)
        xa = xa_ref[...]
        sm = sm_ref[...]
        dt, amat, ac, act = _ssd_chunk(xa, sm, dtb_ref[...], al_ref[...], ok)
        tri = _tri()
        rowi = lax.broadcasted_iota(jnp.int32, (BLK, 1), 0)
        cbs, heads, ys = {}, [], []
        for h in range(SSD_H):
            g = h // (SSD_H // SSD_G)
            if g not in cbs:
                bg = xa[:, SSD_D + SSD_N * g:SSD_D + SSD_N * (g + 1)]
                cg = xa[:, SSD_D + SSD_G * SSD_N + SSD_N * g:SSD_D + SSD_G * SSD_N + SSD_N * (g + 1)]
                cbs[g] = _nt(cg.astype(BF16), bg.astype(BF16))
            sp = sp_ref[:, SSD_P * h:SSD_P * (h + 1)]
            hd = _ssd_head(xa, dt, ac, act, h, cbs[g], sp)
            heads.append(hd)
            ys.append(hd[8] + hd[12] + ds_ref[:, h:h + 1] * hd[0])
        y = jnp.concatenate(ys, axis=1)
        z = z_ref[...]
        sgz = _sig(z)
        siluz = z * sgz
        yg = y * siluz
        dout = dy_ref[...]
        ng = ng_ref[...]
        dygs, xhs = [], []
        for g in range(SSD_G):
            v = yg[:, gw * g:gw * (g + 1)]
            rr = lax.rsqrt(jnp.mean(v * v, axis=1, keepdims=True) + EPS)
            xh = v * rr
            dxh = dout[:, gw * g:gw * (g + 1)] * ng[:, gw * g:gw * (g + 1)]
            dygs.append(rr * (dxh - xh * jnp.mean(dxh * xh, axis=1, keepdims=True)))
            xhs.append(xh)
        dyg = jnp.concatenate(dygs, axis=1)
        dng_ref[...] += jnp.sum(dout * jnp.concatenate(xhs, axis=1), axis=0, keepdims=True)
        dy = dyg * siluz
        dz_ref[...] = dyg * y * (sgz * (1.0 + z * (1.0 - sgz)))

        dxs_l = []
        db_g = [jnp.zeros((BLK, SSD_N), F32) for _ in range(SSD_G)]
        dc_g = [jnp.zeros((BLK, SSD_N), F32) for _ in range(SSD_G)]
        dac_all = jnp.zeros((BLK, BLK), F32)
        ddt_all = jnp.zeros((BLK, BLK), F32)
        dds_row = jnp.zeros((1, BLK), F32)
        lane1 = lax.broadcasted_iota(jnp.int32, (1, BLK), 1)
        for h in range(SSD_H):
            g = h // (SSD_H // SSD_G)
            xs, bg, cg, dth, ach, xdt, seg, m, yd, last, dec, e, yo = heads[h]
            sp = sp_ref[:, SSD_P * h:SSD_P * (h + 1)]
            dyh = dy[:, SSD_P * h:SSD_P * (h + 1)]
            dyb = dyh.astype(BF16)
            xdtb = xdt.astype(BF16)
            dds_row += jnp.where(lane1 == h, jnp.sum(dyh * xs, keepdims=True), 0.0)
            dxs = ds_ref[:, h:h + 1] * dyh
            dyo = (dyh * e).astype(BF16)
            dc_g[g] += _nt(dyo, sp.astype(BF16))
            dsp = _tn(cg.astype(BF16), dyo)
            dac = jnp.sum(dyh * yo, axis=1, keepdims=True)
            dsn = dst[:, SSD_P * h:SSD_P * (h + 1)]
            gl = jnp.exp(last)
            dst[:, SSD_P * h:SSD_P * (h + 1)] = dsp + gl * dsn
            dlast = jnp.sum(dsn * sp, keepdims=True) * gl
            dsnb = dsn.astype(BF16)
            dbd = _nt(xdtb, dsnb)
            db_g[g] += dbd * dec
            tdec = jnp.sum(dbd * bg, axis=1, keepdims=True) * dec
            dxdt = _nn((bg * dec).astype(BF16), dsnb)
            dlast += jnp.sum(tdec, keepdims=True)
            dac -= tdec
            dm = _nt(dyb, xdtb)
            dxdt += _tn(m.astype(BF16), dyb)
            dcb = (dm * seg).astype(BF16)
            dc_g[g] += _nn(dcb, bg.astype(BF16))
            db_g[g] += _tn(dcb, cg.astype(BF16))
            w = dm * m
            dac += jnp.sum(w, axis=1, keepdims=True) - jnp.sum(w.T, axis=1, keepdims=True)
            dac += jnp.where(rowi == BLK - 1, dlast, 0.0)
            dxs_l.append(dxs + dxdt * dth)
            ddt_all += _lane_put(jnp.sum(dxdt * xs, axis=1, keepdims=True), h)
            dac_all += _lane_put(dac, h)
        da = _nn_hi(_tri(lower=False).astype(F32), dac_all)
        ddt = ddt_all + da * amat
        dal_ref[...] += jnp.sum(da * dt, axis=0, keepdims=True) * amat
        ddtr = jnp.where(ok, ddt * _sig(sm + dtb_ref[...]), 0.0)
        ddtb_ref[...] += jnp.sum(ddtr, axis=0, keepdims=True)
        dds_ref[...] += dds_row
        dsm_ref[...] = ddtr
        dxa_ref[...] = jnp.where(ok, jnp.concatenate(dxs_l + db_g + dc_g, axis=1), 0.0)

    vec = pl.BlockSpec((1, BLK), lambda c: (0, 0))
    nvec = pl.BlockSpec((1, SSD_D), lambda c: (0, 0))
    return pl.pallas_call(
        body, name="ssd_bwd", grid=(nb,),
        in_specs=[pl.BlockSpec((BLK, SSD_D), lambda c: (rev(c), 0)),
                  pl.BlockSpec((BLK, CONV_D), lambda c: (rev(c), 0)),
                  pl.BlockSpec((BLK, SSD_D), lambda c: (rev(c), C_Z // SSD_D)),
                  pl.BlockSpec((BLK, BLK), lambda c: (rev(c), C_SM // BLK)),
                  pl.BlockSpec((None, SSD_N, SSD_D), lambda c: (rev(c), 0, 0)),
                  vec, vec, vec, nvec],
        out_specs=[pl.BlockSpec((BLK, CONV_D), lambda c: (rev(c), 0)),
                   pl.BlockSpec((BLK, SSD_D), lambda c: (rev(c), 0)),
                   pl.BlockSpec((BLK, BLK), lambda c: (rev(c), 0)),
                   nvec, vec, vec, vec],
        out_shape=[jax.ShapeDtypeStruct((t, CONV_D), F32), jax.ShapeDtypeStruct((t, SSD_D), F32),
                   jax.ShapeDtypeStruct((t, BLK), F32), jax.ShapeDtypeStruct((1, SSD_D), F32),
                   jax.ShapeDtypeStruct((1, BLK), F32), jax.ShapeDtypeStruct((1, BLK), F32),
                   jax.ShapeDtypeStruct((1, BLK), F32)],
        scratch_shapes=[pltpu.VMEM((SSD_N, SSD_D), F32)],
        compiler_params=_cp("arbitrary"),
    )(dmix, xa, proj, proj, sprev, dtb, alog, dskip, normg)


def _attn_scores(q_ref, k_ref, h, dq, scale, mask, bias):
    qh = q_ref[:, dq * h:dq * (h + 1)].astype(BF16)
    kh = k_ref[:, dq * h:dq * (h + 1)].astype(BF16)
    s = _nt(qh, kh) * scale
    if bias is not None:
        s = s + bias
    return qh, kh, jnp.where(mask, s, NEG)


def attn_fwd(q, k, v, qcol, kcol, vcol, nh, dq, dv, scale, c_col=None, c_row=None, lane0=0):
    t = q.shape[0]
    tq = BLK
    use_bias = c_col is not None

    def body(*refs):
        if use_bias:
            q_ref, k_ref, v_ref, cc_ref, cr_ref, o_ref, l_ref = refs
        else:
            q_ref, k_ref, v_ref, o_ref, l_ref = refs
        i = pl.program_id(0)
        rowg = i * tq + lax.broadcasted_iota(jnp.int32, (tq, 1), 0)
        col = lax.broadcasted_iota(jnp.int32, (1, t), 1)
        mask = (col <= rowg) & (col >= PAD)
        outs = []
        lse = jnp.zeros((tq, BLK), F32)
        for h in range(nh):
            bias = (cc_ref[:, lane0 + h:lane0 + h + 1] - cr_ref[h:h + 1, :]) if use_bias else None
            _, _, s = _attn_scores(q_ref, k_ref, h, dq, scale, mask, bias)
            m = jnp.max(s, axis=1, keepdims=True)
            p = jnp.exp(s - m)
            l = jnp.sum(p, axis=1, keepdims=True)
            vh = v_ref[:, dv * h:dv * (h + 1)].astype(BF16)
            outs.append(_nn(p.astype(BF16), vh) / l)
            lse += _lane_put(m + jnp.log(l), h)
        o_ref[...] = jnp.concatenate(outs, axis=1).astype(BF16)
        l_ref[...] = lse

    in_specs = [pl.BlockSpec((tq, nh * dq), lambda i: (i, qcol)),
                pl.BlockSpec((t, nh * dq), lambda i: (0, kcol)),
                pl.BlockSpec((t, nh * dv), lambda i: (0, vcol))]
    args = [q, k, v]
    if use_bias:
        in_specs += [pl.BlockSpec((tq, BLK), lambda i: (i, 0)), pl.BlockSpec((8, t), lambda i: (0, 0))]
        args += [c_col, c_row]
    return pl.pallas_call(
        body, name="attn_fwd", grid=(t // tq,),
        in_specs=in_specs,
        out_specs=[pl.BlockSpec((tq, nh * dv), lambda i: (i, 0)), pl.BlockSpec((tq, BLK), lambda i: (i, 0))],
        out_shape=[jax.ShapeDtypeStruct((t, nh * dv), BF16), jax.ShapeDtypeStruct((t, BLK), F32)],
        compiler_params=_cp("arbitrary"),
    )(*args)


def attn_bwd(q, k, v, do, lse, qcol, kcol, vcol, docol, nh, dq, dv, scale, c_col=None, c_row=None, lane0=0):
    t = q.shape[0]
    tq = BLK
    use_bias = c_col is not None

    def body(*refs):
        if use_bias:
            q_ref, k_ref, v_ref, do_ref, l_ref, cc_ref, cr_ref, dq_ref, dk_ref, dv_ref, dcq_ref, dck_ref = refs
        else:
            q_ref, k_ref, v_ref, do_ref, l_ref, dq_ref, dk_ref, dv_ref = refs
        i = pl.program_id(0)

        @pl.when(i == 0)
        def _():
            dk_ref[...] = jnp.zeros_like(dk_ref)
            dv_ref[...] = jnp.zeros_like(dv_ref)
            if use_bias:
                dck_ref[...] = jnp.zeros_like(dck_ref)

        rowg = i * tq + lax.broadcasted_iota(jnp.int32, (tq, 1), 0)
        col = lax.broadcasted_iota(jnp.int32, (1, t), 1)
        mask = (col <= rowg) & (col >= PAD)
        dqs = []
        dcq = jnp.zeros((tq, BLK), F32)
        for h in range(nh):
            bias = (cc_ref[:, lane0 + h:lane0 + h + 1] - cr_ref[h:h + 1, :]) if use_bias else None
            qh, kh, s = _attn_scores(q_ref, k_ref, h, dq, scale, mask, bias)
            p = jnp.where(mask, jnp.exp(s - l_ref[:, h:h + 1]), 0.0)
            vh = v_ref[:, dv * h:dv * (h + 1)].astype(BF16)
            doh = do_ref[:, dv * h:dv * (h + 1)].astype(BF16)
            dp = _nt(doh, vh)
            delta = jnp.sum(p * dp, axis=1, keepdims=True)
            ds = p * (dp - delta)
            dsb = ds.astype(BF16)
            dqs.append(_nn(dsb, kh) * scale)
            dk_ref[:, dq * h:dq * (h + 1)] += _tn(dsb, qh) * scale
            dv_ref[:, dv * h:dv * (h + 1)] += _tn(p.astype(BF16), doh)
            if use_bias:
                dcq += _lane_put(jnp.sum(ds, axis=1, keepdims=True), lane0 + h)
                dck_ref[h:h + 1, :] += jnp.sum(ds, axis=0, keepdims=True)
        dq_ref[...] = jnp.concatenate(dqs, axis=1)
        if use_bias:
            dcq_ref[...] = dcq

    in_specs = [pl.BlockSpec((tq, nh * dq), lambda i: (i, qcol)),
                pl.BlockSpec((t, nh * dq), lambda i: (0, kcol)),
                pl.BlockSpec((t, nh * dv), lambda i: (0, vcol)),
                pl.BlockSpec((tq, nh * dv), lambda i: (i, docol)),
                pl.BlockSpec((tq, BLK), lambda i: (i, 0))]
    args = [q, k, v, do, lse]
    out_specs = [pl.BlockSpec((tq, nh * dq), lambda i: (i, 0)), pl.BlockSpec((t, nh * dq), lambda i: (0, 0)),
                 pl.BlockSpec((t, nh * dv), lambda i: (0, 0))]
    out_shape = [jax.ShapeDtypeStruct((t, nh * dq), F32), jax.ShapeDtypeStruct((t, nh * dq), F32),
                 jax.ShapeDtypeStruct((t, nh * dv), F32)]
    if use_bias:
        in_specs += [pl.BlockSpec((tq, BLK), lambda i: (i, 0)), pl.BlockSpec((8, t), lambda i: (0, 0))]
        args += [c_col, c_row]
        out_specs += [pl.BlockSpec((tq, BLK), lambda i: (i, 0)), pl.BlockSpec((8, t), lambda i: (0, 0))]
        out_shape += [jax.ShapeDtypeStruct((t, BLK), F32), jax.ShapeDtypeStruct((8, t), F32)]
    return pl.pallas_call(
        body, name="attn_bwd", grid=(t // tq,),
        in_specs=in_specs, out_specs=out_specs, out_shape=out_shape,
        compiler_params=_cp("arbitrary"),
    )(*args)


def fox_pre(proj, fb):
    t = proj.shape[0]
    nb = t // BLK

    def body(sm_ref, fb_ref, c_ref, cr_ref):
        x = sm_ref[...] + fb_ref[...]
        lane = lax.broadcasted_iota(jnp.int32, (1, BLK), 1)
        keep = _valid_rows(t, 0) & (lane >= SM_F) & (lane < SM_F + FOX_H)
        logf = jnp.where(keep, jnp.minimum(x, 0.0) - jnp.log(1.0 + jnp.exp(-jnp.abs(x))), 0.0)
        tri = _tri().astype(F32)
        carry = jnp.zeros((1, BLK), F32)
        for b in range(nb):
            cb = _nn_hi(tri, logf[b * BLK:(b + 1) * BLK, :]) + carry
            c_ref[b * BLK:(b + 1) * BLK, :] = cb
            carry = cb[BLK - 1:BLK, :]
        cr_ref[...] = c_ref[...].T[SM_F:SM_F + 8, :]

    return pl.pallas_call(
        body, name="fox_pre", grid=(1,),
        in_specs=[pl.BlockSpec((t, BLK), lambda i: (0, C_SM // BLK)), pl.BlockSpec((1, BLK), lambda i: (0, 0))],
        out_specs=[pl.BlockSpec((t, BLK), lambda i: (0, 0)), pl.BlockSpec((8, t), lambda i: (0, 0))],
        out_shape=[jax.ShapeDtypeStruct((t, BLK), F32), jax.ShapeDtypeStruct((8, t), F32)],
        compiler_params=_cp("arbitrary"),
    )(proj, fb)


def fox_pre_bwd(dcq, dck, proj, fb, dsm_in):
    t = proj.shape[0]
    nb = t // BLK

    def body(dcq_ref, dck_ref, sm_ref, fb_ref, din_ref, dsm_ref, dfb_ref, scr):
        triu = _tri(lower=False).astype(F32)
        carry = jnp.zeros((1, BLK), F32)
        scr[...] = jnp.concatenate([jnp.zeros((SM_F, t), F32), dck_ref[...], jnp.zeros((BLK - SM_F - 8, t), F32)], axis=0).T
        for b in range(nb - 1, -1, -1):
            blk = dcq_ref[b * BLK:(b + 1) * BLK, :] - scr[b * BLK:(b + 1) * BLK, :]
            cb = _nn_hi(triu, blk) + carry
            scr[b * BLK:(b + 1) * BLK, :] = cb
            carry = cb[0:1, :]
        x = sm_ref[...] + fb_ref[...]
        lane = lax.broadcasted_iota(jnp.int32, (1, BLK), 1)
        keep = _valid_rows(t, 0) & (lane >= SM_F) & (lane < SM_F + FOX_H)
        df = jnp.where(keep, scr[...] * _sig(-x), 0.0)
        dfb_ref[...] = jnp.sum(df, axis=0, keepdims=True)
        dsm_ref[...] = din_ref[...] + df

    full = pl.BlockSpec((t, BLK), lambda i: (0, 0))
    return pl.pallas_call(
        body, name="fox_pre_bwd", grid=(1,),
        in_specs=[full, pl.BlockSpec((8, t), lambda i: (0, 0)), pl.BlockSpec((t, BLK), lambda i: (0, C_SM // BLK)),
                  pl.BlockSpec((1, BLK), lambda i: (0, 0)), full],
        out_specs=[full, pl.BlockSpec((1, BLK), lambda i: (0, 0))],
        out_shape=[jax.ShapeDtypeStruct((t, BLK), F32), jax.ShapeDtypeStruct((1, BLK), F32)],
        scratch_shapes=[pltpu.VMEM((t, BLK), F32)],
        compiler_params=_cp("arbitrary"),
    )(dcq, dck, proj, fb, dsm_in)


def _swap_rope(x):
    lane = lax.broadcasted_iota(jnp.int32, (1, BLK), 1)
    return jnp.where((lane >= SM_KR) & (lane < SM_KR + 16), pltpu.roll(x, BLK - 16, 1),
                     jnp.where((lane >= SM_KR + 16) & (lane < SM_KR + 32), pltpu.roll(x, 16, 1), 0.0))


def _rms(x, g):
    r = lax.rsqrt(jnp.mean(x * x, axis=1, keepdims=True) + EPS)
    return r, x * r


def mla_pre(proj, qg, kvg, wq, wk, wv, cosq, sinq):
    t = proj.shape[0]
    tm = _row_tile(t)

    def body(cq_ref, ckv_ref, sm_ref, qg_ref, kvg_ref, wq_ref, wk_ref, wv_ref, cos_ref, sin_ref,
             q_ref, k_ref, v_ref, cqn_ref, ckvn_ref):
        cs, sn = cos_ref[...], sin_ref[...]
        _, xh = _rms(cq_ref[...], None)
        cqn = (xh * qg_ref[...]).astype(BF16)
        cqn_ref[...] = cqn
        qraw = _nn(cqn, wq_ref[...])
        qs = []
        for h in range(MLA_H):
            hb = qraw[:, BLK * h:BLK * (h + 1)]
            qs.append(hb * cs + _swap_rope(hb) * sn)
        q_ref[...] = jnp.concatenate(qs, axis=1).astype(BF16)
        _, kh = _rms(ckv_ref[...], None)
        ckvn = (kh * kvg_ref[...]).astype(BF16)
        ckvn_ref[...] = ckvn
        kraw = _nn(ckvn, wk_ref[...])
        v_ref[...] = _nn(ckvn, wv_ref[...]).astype(BF16)
        lane = lax.broadcasted_iota(jnp.int32, (1, BLK), 1)
        kr = sm_ref[...]
        krr = jnp.where((lane >= SM_KR) & (lane < SM_KR + MLA_ROPE), kr * cs + _swap_rope(kr) * sn, 0.0)
        k_ref[...] = jnp.concatenate([kraw[:, BLK * h:BLK * (h + 1)] + krr for h in range(MLA_H)], axis=1).astype(BF16)

    def rows(w, cb):
        return pl.BlockSpec((tm, w), lambda i: (i, cb))

    def whole(a):
        return pl.BlockSpec(a.shape, lambda i: (0, 0))

    return pl.pallas_call(
        body, name="mla_pre", grid=(t // tm,),
        in_specs=[rows(MLA_QL, C_CQ // MLA_QL), rows(MLA_KVL, C_CKV // MLA_KVL), rows(BLK, C_SM // BLK),
                  whole(qg), whole(kvg), whole(wq), whole(wk), whole(wv), rows(BLK, 0), rows(BLK, 0)],
        out_specs=[rows(512, 0), rows(512, 0), rows(256, 0), rows(MLA_QL, 0), rows(MLA_KVL, 0)],
        out_shape=[jax.ShapeDtypeStruct((t, 512), BF16), jax.ShapeDtypeStruct((t, 512), BF16),
                   jax.ShapeDtypeStruct((t, 256), BF16), jax.ShapeDtypeStruct((t, MLA_QL), BF16),
                   jax.ShapeDtypeStruct((t, MLA_KVL), BF16)],
        compiler_params=_cp("arbitrary"),
    )(proj, proj, proj, qg, kvg, wq, wk, wv, cosq, sinq)


def mla_pre_bwd(dq, dk, dv, proj, cqn, ckvn, qg, kvg, wq, wk, wv, cosq, sinq, dsm_in):
    t = proj.shape[0]
    tm = _row_tile(t)

    def body(dq_ref, dk_ref, dv_ref, cq_ref, ckv_ref, cqn_ref, ckvn_ref, qg_ref, kvg_ref, wq_ref, wk_ref, wv_ref,
             cos_ref, sin_ref, din_ref, dcq_ref, dckv_ref, dsm_ref, dwq_ref, dwk_ref, dwv_ref, dqg_ref, dkvg_ref):
        i = pl.program_id(0)

        @pl.when(i == 0)
        def _():
            for r in (dwq_ref, dwk_ref, dwv_ref, dqg_ref, dkvg_ref):
                r[...] = jnp.zeros_like(r)

        cs, sn = cos_ref[...], sin_ref[...]
        lane = lax.broadcasted_iota(jnp.int32, (1, BLK), 1)

        def unrope(dy):
            return dy * cs + _swap_rope(dy * sn)

        dqp = jnp.concatenate([unrope(dq_ref[:, BLK * h:BLK * (h + 1)]) for h in range(MLA_H)], axis=1).astype(BF16)
        dwq_ref[...] += _tn(cqn_ref[...], dqp)
        dcqn = _nt(dqp, wq_ref[...])
        r, xh = _rms(cq_ref[...], None)
        dqg_ref[...] += jnp.sum(dcqn * xh, axis=0, keepdims=True)
        dxh = dcqn * qg_ref[...]
        dcq_ref[...] = r * (dxh - xh * jnp.mean(dxh * xh, axis=1, keepdims=True))

        dkn, dkr = [], jnp.zeros((tm, BLK), F32)
        for h in range(MLA_H):
            blk = dk_ref[:, BLK * h:BLK * (h + 1)]
            dkn.append(jnp.where(lane < MLA_NOPE, blk, 0.0))
            dkr += jnp.where((lane >= SM_KR) & (lane < SM_KR + MLA_ROPE), blk, 0.0)
        dknb = jnp.concatenate(dkn, axis=1).astype(BF16)
        dvb = dv_ref[...].astype(BF16)
        ckvn = ckvn_ref[...]
        dwk_ref[...] += _tn(ckvn, dknb)
        dwv_ref[...] += _tn(ckvn, dvb)
        dckvn = _nt(dknb, wk_ref[...]) + _nt(dvb, wv_ref[...])
        r2, kh = _rms(ckv_ref[...], None)
        dkvg_ref[...] += jnp.sum(dckvn * kh, axis=0, keepdims=True)
        dkh = dckvn * kvg_ref[...]
        dckv_ref[...] = r2 * (dkh - kh * jnp.mean(dkh * kh, axis=1, keepdims=True))
        dsm_ref[...] = din_ref[...] + jnp.where((lane >= SM_KR) & (lane < SM_KR + MLA_ROPE), unrope(dkr), 0.0)

    def rows(w, cb):
        return pl.BlockSpec((tm, w), lambda i: (i, cb))

    def whole(a):
        return pl.BlockSpec(a.shape, lambda i: (0, 0))

    def wshape(a):
        return jax.ShapeDtypeStruct(a.shape, F32)

    return pl.pallas_call(
        body, name="mla_pre_bwd", grid=(t // tm,),
        in_specs=[rows(512, 0), rows(512, 0), rows(256, 0), rows(MLA_QL, C_CQ // MLA_QL), rows(MLA_KVL, C_CKV // MLA_KVL),
                  rows(MLA_QL, 0), rows(MLA_KVL, 0), whole(qg), whole(kvg), whole(wq), whole(wk), whole(wv),
                  rows(BLK, 0), rows(BLK, 0), rows(BLK, 0)],
        out_specs=[rows(MLA_QL, 0), rows(MLA_KVL, 0), rows(BLK, 0), whole(wq), whole(wk), whole(wv), whole(qg), whole(kvg)],
        out_shape=[jax.ShapeDtypeStruct((t, MLA_QL), F32), jax.ShapeDtypeStruct((t, MLA_KVL), F32),
                   jax.ShapeDtypeStruct((t, BLK), F32), wshape(wq), wshape(wk), wshape(wv), wshape(qg), wshape(kvg)],
        compiler_params=_cp("arbitrary"),
    )(dq, dk, dv, proj, proj, cqn, ckvn, qg, kvg, wq, wk, wv, cosq, sinq, dsm_in)


def _slot_sum(me, own, recv_ref):
    gg = own.astype(F32)
    for s in range(N_DEV):
        gg = gg + jnp.where(me == s, 0.0, recv_ref[s].astype(F32))
    return gg


def adamw(w, m, v, g=None, recv=None, own=None, me_arr=None):
    shape = w.shape
    c = shape[-1]
    from_recv = recv is not None
    if not from_recv:
        me_arr = jnp.zeros((1,), jnp.int32)
    nl = len(recv) if from_recv else 1
    rws = w.size // c // nl
    tr = rws
    for d in (1024, 512, 352, 256, 128, 64, 32, 16, 8):
        if rws % d == 0 and d * c * 4 <= (2 << 20):
            tr = d
            break
    nt = rws // tr
    w2, m2, v2 = (a.reshape(nl, rws, c) for a in (w, m, v))
    if from_recv:
        gin = [a.reshape(N_DEV, rws, c) for a in list(recv) + list(own)]
    else:
        gin = [g.reshape(1, rws, c)]

    def body(me_ref, w_ref, m_ref, v_ref, *rest):
        g_refs, outs = rest[:len(gin)], rest[len(gin):]
        if from_recv:
            g_out, outs = outs[0], outs[1:]
            for li in range(nl):
                @pl.when(pl.program_id(0) == li)
                def _(li=li):
                    g_out[...] = _slot_sum(me_ref[0], g_refs[nl + li][...], g_refs[li])
            gg = g_out[...]
        else:
            gg = g_refs[0][...]
        d_ref, nm_ref, nv_ref = outs
        nm = B1 * m_ref[...] + (1.0 - B1) * gg
        nv = B2 * v_ref[...] + (1.0 - B2) * (gg * gg)
        mh = nm / (1.0 - B1 ** STEP)
        vh = nv / (1.0 - B2 ** STEP)
        d_ref[...] = -LR * (mh / (jnp.sqrt(vh) + AEPS) + WD * w_ref[...])
        nm_ref[...] = nm
        nv_ref[...] = nv

    row = pl.BlockSpec((None, tr, c), lambda l, i, me: (l, i, 0))
    if from_recv:
        gspecs = [pl.BlockSpec((N_DEV, tr, c), lambda l, i, me, li=li: (0, jnp.where(l == li, i, 0), 0))
                  for li in range(nl)]
        gspecs += [pl.BlockSpec((None, tr, c), lambda l, i, me, li=li: (me[0], jnp.where(l == li, i, 0), 0))
                   for li in range(nl)]
    else:
        gspecs = [row]
    nout = 4 if from_recv else 3
    outs = pl.pallas_call(
        body, name="adamw",
        grid_spec=pltpu.PrefetchScalarGridSpec(num_scalar_prefetch=1, grid=(nl, nt), in_specs=[row, row, row] + gspecs,
                                               out_specs=[row] * nout),
        out_shape=[jax.ShapeDtypeStruct((nl, rws, c), F32)] * nout,
        compiler_params=_cp("arbitrary", "arbitrary"),
    )(me_arr, w2, m2, v2, *gin)
    return tuple(o.reshape(shape) for o in outs)


def sum_slots(recv, own=None, me_arr=None):
    _, r, c = recv.shape
    if own is None:
        own, me_arr = recv, jnp.zeros((1,), jnp.int32)
        plain = True
    else:
        plain = False

    def body(me_ref, r_ref, own_ref, o_ref):
        if plain:
            gg = r_ref[0].astype(F32)
            for s in range(1, N_DEV):
                gg = gg + r_ref[s].astype(F32)
            o_ref[...] = gg
        else:
            o_ref[...] = _slot_sum(me_ref[0], own_ref[...], r_ref)

    return pl.pallas_call(
        body, name="sum_slots",
        grid_spec=pltpu.PrefetchScalarGridSpec(
            num_scalar_prefetch=1, grid=(1,),
            in_specs=[pl.BlockSpec((N_DEV, r, c), lambda i, me: (0, 0, 0)),
                      pl.BlockSpec((None, r, c), lambda i, me: (me[0], 0, 0))],
            out_specs=pl.BlockSpec((r, c), lambda i, me: (0, 0))),
        out_shape=jax.ShapeDtypeStruct((r, c), F32),
        compiler_params=_cp("arbitrary"),
    )(me_arr, recv, own)


_FLIPS = [(0, 0, 1), (0, 1, 0), (0, 1, 1), (1, 0, 0), (1, 0, 1), (1, 1, 0), (1, 1, 1)]
_ANY = pl.BlockSpec(memory_space=pl.ANY)


def _mesh_place():
    x, y, c = lax.axis_index("x"), lax.axis_index("y"), lax.axis_index("c")
    me = 4 * x + 2 * y + c
    peers = [((x + fx) % 2, (y + fy) % 2, (c + fc) % 2) for fx, fy, fc in _FLIPS]
    return me, peers


def place_own(src, l, dtype, me_arr):
    _, r, c = src.shape
    tr = r
    for d in (512, 352, 256, 128, 64, 32, 16, 8):
        if r % d == 0 and d * c * 4 <= (2 << 20):
            tr = d
            break

    def body(me_ref, s_ref, o_ref):
        o_ref[...] = s_ref[...].astype(dtype)

    return pl.pallas_call(
        body, name="place_own",
        grid_spec=pltpu.PrefetchScalarGridSpec(
            num_scalar_prefetch=1, grid=(r // tr,),
            in_specs=[pl.BlockSpec((None, tr, c), lambda i, me: (l, i, 0))],
            out_specs=pl.BlockSpec((None, tr, c), lambda i, me: (me[0], i, 0))),
        out_shape=jax.ShapeDtypeStruct((N_DEV, r, c), dtype),
        compiler_params=_cp("arbitrary"),
    )(me_arr, src)


_HBM = pl.BlockSpec(memory_space=pltpu.HBM)
_SEMS = pl.BlockSpec(memory_space=pltpu.SEMAPHORE)
_EFFECT = pltpu.SideEffectType.DATAFLOW_SIDE_EFFECTING


def exchange_start(mode, arrays, name):
    n = len(arrays)
    gather = mode == "gather"
    ns = 0 if gather else n
    zones = list(arrays) if gather else [lax.empty(a.shape, a.dtype) for a in arrays]
    ops = ([] if gather else list(arrays)) + zones

    def body(*refs):
        srcs, lands = refs[:ns], refs[ns:ns + n]
        send_sems, recv_sems = refs[ns + n], refs[ns + n + 1]
        token = refs[-1]
        me, peers = _mesh_place()
        ids = [4 * p[0] + 2 * p[1] + p[2] for p in peers]
        for j in range(n):
            for k in range(N_DEV - 1):
                src = lands[j].at[me] if gather else srcs[j].at[ids[k]]
                pltpu.make_async_remote_copy(src_ref=src, dst_ref=lands[j].at[me],
                                             send_sem=send_sems.at[j * (N_DEV - 1) + k],
                                             recv_sem=recv_sems.at[j * (N_DEV - 1) + k], device_id=peers[k],
                                             device_id_type=pl.DeviceIdType.MESH).start()
        token[...] = jnp.zeros_like(token)

    nsem = n * (N_DEV - 1)
    res = pl.pallas_call(
        body, name=name,
        in_specs=[_HBM] * (ns + n),
        out_specs=(_SEMS, _SEMS, *[_HBM] * (ns + n), pl.BlockSpec(memory_space=pltpu.VMEM)),
        out_shape=(pltpu.SemaphoreType.DMA((nsem,)), pltpu.SemaphoreType.DMA((nsem,)),
                   *[pltpu.HBM(a.shape, a.dtype) for a in ops], jax.ShapeDtypeStruct((8, BLK), F32)),
        input_output_aliases={i: 2 + i for i in range(ns + n)},
        compiler_params=pltpu.CompilerParams(has_side_effects=_EFFECT),
    )(*[pltpu.with_memory_space_constraint(a, pltpu.HBM) for a in ops])
    return dict(gather=gather, send=res[0], recv=res[1], srcs=list(res[2:2 + ns]), lands=list(res[2 + ns:2 + ns + n]),
                token=res[-1])


def exchange_wait(hd, idxs, name, after):
    gather = hd["gather"]
    n = len(idxs)
    ns = 0 if gather else n
    ops = ([] if gather else [hd["srcs"][j] for j in idxs]) + [hd["lands"][j] for j in idxs]

    def body(*refs):
        srcs, lands = refs[:ns], refs[ns:ns + n]
        send_sems, recv_sems = refs[ns + n], refs[ns + n + 1]
        me, peers = _mesh_place()
        ids = [4 * p[0] + 2 * p[1] + p[2] for p in peers]
        for p, j in enumerate(idxs):
            for k in range(N_DEV - 1):
                src = lands[p].at[me] if gather else srcs[p].at[ids[k]]
                cp = pltpu.make_async_remote_copy(src_ref=src, dst_ref=lands[p].at[ids[k]],
                                                  send_sem=send_sems.at[j * (N_DEV - 1) + k],
                                                  recv_sem=recv_sems.at[j * (N_DEV - 1) + k], device_id=peers[k],
                                                  device_id_type=pl.DeviceIdType.MESH)
                cp.wait_send()
                cp.wait_recv()

    res = pl.pallas_call(
        body, name=name,
        in_specs=[_HBM] * (ns + n) + [_SEMS, _SEMS, _ANY],
        out_specs=[_HBM] * (ns + n),
        out_shape=[pltpu.HBM(a.shape, a.dtype) for a in ops],
        input_output_aliases={i: i for i in range(ns + n)},
        compiler_params=pltpu.CompilerParams(has_side_effects=_EFFECT),
    )(*ops, hd["send"], hd["recv"], after)
    return list(res[:ns]), list(res[ns:])


def _pad_cols(a, n):
    return jnp.pad(a, ((0, 0),) * (a.ndim - 1) + ((0, n - a.shape[-1]),))


def w_in_to_padded(w):
    z = lambda n: jnp.zeros(w.shape[:-1] + (n,), w.dtype)
    return jnp.concatenate([
        w[..., 0:1280], w[..., 1288:2056], w[..., 2060:2316], w[..., 2316:2444],
        w[..., 1280:1288], w[..., 2056:2060], z(SM_KR - SM_F - FOX_H), w[..., 2444:2476], z(BLK - SM_KR - MLA_ROPE)], axis=-1)


def w_in_from_padded(g):
    s = C_SM
    return jnp.concatenate([
        g[..., 0:1280], g[..., s + SM_DT:s + SM_DT + 8], g[..., 1280:2048], g[..., s + SM_F:s + SM_F + 4],
        g[..., 2048:2304], g[..., 2304:2432], g[..., s + SM_KR:s + SM_KR + MLA_ROPE]], axis=-1)


def _unshard_cols(gth):
    n, r, c = gth.shape
    return jnp.transpose(gth, (1, 0, 2)).reshape(r, n * c)


def _shard_cols(full):
    r, nc = full.shape
    return jnp.transpose(full.reshape(r, N_DEV, nc // N_DEV), (1, 0, 2))


def mla_weights(uq_g, ukv_g):
    uq = _unshard_cols(uq_g)
    dqh = MLA_NOPE + MLA_ROPE
    wq = jnp.concatenate([_pad_cols(uq[:, dqh * h:dqh * (h + 1)], BLK) for h in range(MLA_H)], axis=1)
    wk = jnp.concatenate([_pad_cols(ukv_g[2 * h], BLK) for h in range(MLA_H)], axis=1)
    wv = jnp.concatenate([ukv_g[2 * h + 1] for h in range(MLA_H)], axis=1)
    return wq, wk, wv


def mla_weight_grads(dwq, dwk, dwv):
    dqh = MLA_NOPE + MLA_ROPE
    duq = _shard_cols(jnp.concatenate([dwq[:, BLK * h:BLK * h + dqh] for h in range(MLA_H)], axis=1))
    parts = []
    for h in range(MLA_H):
        parts += [dwk[:, BLK * h:BLK * h + MLA_NOPE], dwv[:, MLA_V * h:MLA_V * (h + 1)]]
    return duq, jnp.stack(parts, axis=0)


def rope_tables(t):
    pos = (jnp.arange(t, dtype=jnp.int32) - PAD).astype(F32)
    inv_freq = 1.0 / (10000.0 ** (jnp.arange(0, MLA_ROPE, 2, dtype=F32) / MLA_ROPE))
    ang = pos[:, None] * inv_freq[None, :]
    cos, sin = jnp.cos(ang), jnp.sin(ang)
    one, zero = jnp.ones((t, SM_KR), F32), jnp.zeros((t, SM_KR), F32)
    tail = BLK - SM_KR - MLA_ROPE
    cosq = jnp.concatenate([one, cos, cos, jnp.ones((t, tail), F32)], axis=1)
    sinq = jnp.concatenate([zero, -sin, sin, jnp.zeros((t, tail), F32)], axis=1)
    return cosq, sinq


def _lanes(v, off=0):
    return jnp.pad(v.astype(F32), (off, BLK - off - v.shape[0]))[None, :]


def layer_fwd(h, hb, getw, tabs):
    sv = {"h0": h, "h0b": hb}
    W = dict(getw("ffn1", hb))
    u, v, r1, h1, h1b = ffn_fwd(hb, h, W["g1"], W["u1"], W["d1"], W["ln1_g"], W["ln1_b"])
    sv.update(u1=u, v1=v, r1=r1, h1=h1, h1b=h1b)
    W.update(getw("mix", h1b))
    proj = mm_nn(h1b, W["w_in"])
    xa = conv_fwd(proj, W["conv_w"], W["conv_b"])
    y_ssd, sprev = ssd_fwd(xa, proj, W["dtb"], W["alog"], W["dskip"], W["normg"])
    c_col, c_row = fox_pre(proj, W["fb"])
    y_fox, lse_f = attn_fwd(proj, proj, proj, C_FQ // 256, C_FK // 256, C_FV // 256, FOX_H, FOX_DH, FOX_DH,
                            FOX_DH ** -0.5, c_col, c_row, SM_F)
    q, k, vv, cqn, ckvn = mla_pre(proj, W["qg"], W["kvg"], W["wq"], W["wk"], W["wv"], *tabs)
    y_mla, lse_m = attn_fwd(q, k, vv, 0, 0, 0, MLA_H, BLK, MLA_V, (MLA_NOPE + MLA_ROPE) ** -0.5)
    mixcat = jnp.concatenate([y_ssd, y_fox, y_mla], axis=1)
    r2, h2, h2b = mm_res_ln(mixcat[None], W["w_out"][None], h1, ALPHA, 1.0, W["ln2_g"], W["ln2_b"])
    sv.update(proj=proj, xa=xa, sprev=sprev, c_col=c_col, c_row=c_row, lse_f=lse_f, q=q, k=k, v=vv, cqn=cqn, ckvn=ckvn,
              lse_m=lse_m, mixcat=mixcat, r2=r2, h2=h2, h2b=h2b)
    W.update(getw("ffn2", h2b))
    u, v, r3, h3, h3b = ffn_fwd(h2b, h2, W["g2"], W["u2"], W["d2"], W["ln3_g"], W["ln3_b"])
    sv.update(u2=u, v2=v, r3=r3, W=W)
    return h3, h3b, sv


def ffn_bwd(parts, r, gamma, hb_in, u, v, wg, wu, wd, after=None):
    dr, dfb, dg, db = ln_bwd(parts, r, gamma, 0.5, after)
    du, dv, dh = ffn_bwd_act(dfb, u, v, wg, wu, wd)
    dwg, dwu, dwd = ffn_bwd_w(hb_in, dfb, u, v, du, dv)
    return dr, dh, dict(d=dwd, g=dwg, u=dwu, ln_g=dg, ln_b=db)


def layer_bwd(parts, sv, emit, tabs, after):
    G = {}
    W = sv["W"]
    dr3, dh2f, g2 = ffn_bwd(parts, sv["r3"], W["ln3_g"], sv["h2b"], sv["u2"], sv["v2"], W["g2"], W["u2"], W["d2"], after)
    G.update(g2=g2["g"], u2=g2["u"], d2=g2["d"], ln3_g=g2["ln_g"], ln3_b=g2["ln_b"])
    tok = emit("ffn2", G)
    dr2, dmixb, G["ln2_g"], G["ln2_b"] = ln_bwd([(dr3, ALPHA), (dh2f, 1.0)], sv["r2"], W["ln2_g"], 1.0, tok)
    dmc = mm_nt_reduce([(dmixb[None], W["w_out"][None])], D)
    G["w_out"] = mm_tn(sv["mixcat"][None], dmixb[None])[0]
    proj = sv["proj"]
    dxa, dz, dsm, G["normg"], G["dskip"], G["alog"], G["dtb"] = ssd_bwd(
        dmc, sv["xa"], proj, sv["sprev"], W["dtb"], W["alog"], W["dskip"], W["normg"])
    dxbc, G["conv_w"], G["conv_b"] = conv_bwd(dxa, proj, W["conv_w"], W["conv_b"])
    dfq, dfk, dfv, dcq, dck = attn_bwd(proj, proj, proj, dmc, sv["lse_f"], C_FQ // 256, C_FK // 256, C_FV // 256, 2,
                                       FOX_H, FOX_DH, FOX_DH, FOX_DH ** -0.5, sv["c_col"], sv["c_row"], SM_F)
    dsm, G["fb"] = fox_pre_bwd(dcq, dck, proj, W["fb"], dsm)
    dq, dk, dv = attn_bwd(sv["q"], sv["k"], sv["v"], dmc, sv["lse_m"], 0, 0, 0, 3, MLA_H, BLK, MLA_V,
                          (MLA_NOPE + MLA_ROPE) ** -0.5)
    dcql, dckv, dsm, G["wq"], G["wk"], G["wv"], G["qg"], G["kvg"] = mla_pre_bwd(
        dq, dk, dv, proj, sv["cqn"], sv["ckvn"], W["qg"], W["kvg"], W["wq"], W["wk"], W["wv"], *tabs, dsm)
    dproj = jnp.concatenate([dz, dxbc, dfq, dfk, dfv, dcql, dckv, dsm], axis=1).astype(BF16)
    dh1p = mm_nt_reduce([(dproj[None], W["w_in"][None])], D)
    G["w_in"] = mm_tn(sv["h1b"][None], dproj[None])[0]
    tok = emit("mix", G)
    dr1, dh0f, g1 = ffn_bwd([(dr2, ALPHA), (dh1p, 1.0)], sv["r1"], W["ln1_g"], sv["h0b"], sv["u1"], sv["v1"],
                            W["g1"], W["u1"], W["d1"], tok)
    G.update(g1=g1["g"], u1=g1["u"], d1=g1["d"], ln1_g=g1["ln_g"], ln1_b=g1["ln_b"])
    tok = emit("ffn1", G)
    return [(dr1, ALPHA), (dh0f, 1.0)], G, tok


def local_step(x, target, meta_full, getw, emit):
    t = x.shape[0] + BLK
    tabs = rope_tables(t)
    h, hb = build_h0(meta_full, x)
    saved = []
    for l in range(NL):
        h, hb, sv = layer_fwd(h, hb, functools.partial(getw, l), tabs)
        saved.append(sv)
    dy, loss = loss_head(h, target)
    parts = [(dy, 1.0)]
    grads = [None] * NL
    tok = None
    for l in range(NL - 1, -1, -1):
        parts, grads[l], tok = layer_bwd(parts, saved[l], functools.partial(emit, l), tabs, tok)
    gx, gmeta = final_add(parts[0][0], parts[1][0], tok)
    return loss, gx, gmeta, grads


_SMALL = ["ln1_g", "ln1_b", "ln2_g", "ln2_b", "ln3_g", "ln3_b", "conv_b", "ssd_norm_g", "mla_q_norm_g",
          "mla_kv_norm_g", "dt_bias", "a_log", "d_skip", "fox_f_b"]
_SMALL_ROWS = 16
_BIG = ["ffn1_w_gate", "ffn1_w_up", "ffn1_w_down", "w_in", "conv_w", "mla_w_uq", "mla_w_ukv", "w_out",
        "ffn2_w_gate", "ffn2_w_up", "ffn2_w_down"]
_NAMES = ["meta", "ffn1_w_gate", "ffn1_w_up", "ffn1_w_down", "ln1_g", "ln1_b", "w_in", "conv_w", "conv_b", "dt_bias",
          "a_log", "d_skip", "ssd_norm_g", "fox_f_b", "mla_q_norm_g", "mla_w_uq", "mla_kv_norm_g", "mla_w_ukv", "w_out",
          "ln2_g", "ln2_b", "ffn2_w_gate", "ffn2_w_up", "ffn2_w_down", "ln3_g", "ln3_b"]


def pack_small(p):
    rows = []
    for l in range(NL):
        for n in _SMALL:
            rows.append(_pad_cols(p[n][l][None, :].astype(F32), D))
        rows.append(jnp.zeros((_SMALL_ROWS - len(_SMALL), D), F32))
    return jnp.concatenate(rows, axis=0)


def unpack_small(a, like):
    out = {}
    for i, n in enumerate(_SMALL):
        out[n] = jnp.stack([a[l * _SMALL_ROWS + i, :like[n].shape[1]] for l in range(NL)], axis=0)
    return out


_STAGES = {"ffn1": ["ffn1_w_gate", "ffn1_w_up", "ffn1_w_down"],
           "mix": ["w_in", "conv_w", "mla_w_uq", "mla_w_ukv", "w_out"],
           "ffn2": ["ffn2_w_gate", "ffn2_w_up", "ffn2_w_down"]}


_FFN_T = ("ffn1_w_gate", "ffn1_w_up", "ffn2_w_gate", "ffn2_w_up")


def stage_weights(l, stage, g, rep):
    if stage != "mix":
        i = stage[3]
        return {"g" + i: g[f"ffn{i}_w_gate"].reshape(D_FF, D), "u" + i: g[f"ffn{i}_w_up"].reshape(D_FF, D),
                "d" + i: g[f"ffn{i}_w_down"].reshape(D_FF, D),
                "ln1_g" if i == "1" else "ln3_g": rep["ln1_g" if i == "1" else "ln3_g"][l][None, :],
                "ln1_b" if i == "1" else "ln3_b": rep["ln1_b" if i == "1" else "ln3_b"][l][None, :]}
    W = {}
    W["w_in"] = g["w_in"].reshape(D, N_INP)
    W["w_out"] = g["w_out"].reshape(D, D)
    W["wq"], W["wk"], W["wv"] = mla_weights(g["mla_w_uq"], g["mla_w_ukv"])
    W["conv_w"] = _unshard_cols(g["conv_w"])
    for k in ("ln2_g", "ln2_b", "conv_b"):
        W[k] = rep[k][l][None, :]
    W["normg"] = rep["ssd_norm_g"][l][None, :]
    W["qg"] = rep["mla_q_norm_g"][l][None, :]
    W["kvg"] = rep["mla_kv_norm_g"][l][None, :]
    W["dtb"] = _lanes(rep["dt_bias"][l], SM_DT)
    W["alog"] = _lanes(rep["a_log"][l], SM_DT)
    W["dskip"] = _lanes(rep["d_skip"][l], SM_DT)
    W["fb"] = _lanes(rep["fox_f_b"][l], SM_F)
    return W


def small_grads(G):
    return {"ln1_g": G["ln1_g"][0], "ln1_b": G["ln1_b"][0], "ln2_g": G["ln2_g"][0], "ln2_b": G["ln2_b"][0],
            "ln3_g": G["ln3_g"][0], "ln3_b": G["ln3_b"][0], "conv_b": G["conv_b"][0], "ssd_norm_g": G["normg"][0],
            "mla_q_norm_g": G["qg"][0], "mla_kv_norm_g": G["kvg"][0], "dt_bias": G["dtb"][0, :SSD_H],
            "a_log": G["alog"][0, :SSD_H], "d_skip": G["dskip"][0, :SSD_H], "fox_f_b": G["fb"][0, SM_F:SM_F + FOX_H]}


def big_grads(G, stage):
    if stage != "mix":
        i = stage[-1]
        return {f"ffn{i}_w_{k}": G[k[0] + i].reshape(N_DEV, HS, D) for k in ("gate", "up", "down")}
    duq, dukv = mla_weight_grads(G["wq"], G["wk"], G["wv"])
    return {"w_in": G["w_in"].reshape(N_DEV, D // N_DEV, N_INP), "w_out": G["w_out"].reshape(N_DEV, D // N_DEV, D),
            "mla_w_uq": duq, "mla_w_ukv": dukv, "conv_w": _shard_cols(G["conv_w"])}


def kernel(x, meta, ffn1_w_gate, ffn1_w_up, ffn1_w_down, ln1_g, ln1_b, w_in, conv_w, conv_b, dt_bias, a_log, d_skip, ssd_norm_g, fox_f_b, mla_q_norm_g, mla_w_uq, mla_kv_norm_g, mla_w_ukv, w_out, ln2_g, ln2_b, ffn2_w_gate, ffn2_w_up, ffn2_w_down, ln3_g, ln3_b, loss_target, m_meta, m_ffn1_w_gate, m_ffn1_w_up, m_ffn1_w_down, m_ln1_g, m_ln1_b, m_w_in, m_conv_w, m_conv_b, m_dt_bias, m_a_log, m_d_skip, m_ssd_norm_g, m_fox_f_b, m_mla_q_norm_g, m_mla_w_uq, m_mla_kv_norm_g, m_mla_w_ukv, m_w_out, m_ln2_g, m_ln2_b, m_ffn2_w_gate, m_ffn2_w_up, m_ffn2_w_down, m_ln3_g, m_ln3_b, v_meta, v_ffn1_w_gate, v_ffn1_w_up, v_ffn1_w_down, v_ln1_g, v_ln1_b, v_w_in, v_conv_w, v_conv_b, v_dt_bias, v_a_log, v_d_skip, v_ssd_norm_g, v_fox_f_b, v_mla_q_norm_g, v_mla_w_uq, v_mla_kv_norm_g, v_mla_w_ukv, v_w_out, v_ln2_g, v_ln2_b, v_ffn2_w_gate, v_ffn2_w_up, v_ffn2_w_down, v_ln3_g, v_ln3_b):
    vals = (meta, ffn1_w_gate, ffn1_w_up, ffn1_w_down, ln1_g, ln1_b, w_in, conv_w, conv_b, dt_bias, a_log, d_skip, ssd_norm_g, fox_f_b, mla_q_norm_g, mla_w_uq, mla_kv_norm_g, mla_w_ukv, w_out, ln2_g, ln2_b, ffn2_w_gate, ffn2_w_up, ffn2_w_down, ln3_g, ln3_b)
    moms = (m_meta, m_ffn1_w_gate, m_ffn1_w_up, m_ffn1_w_down, m_ln1_g, m_ln1_b, m_w_in, m_conv_w, m_conv_b, m_dt_bias, m_a_log, m_d_skip, m_ssd_norm_g, m_fox_f_b, m_mla_q_norm_g, m_mla_w_uq, m_mla_kv_norm_g, m_mla_w_ukv, m_w_out, m_ln2_g, m_ln2_b, m_ffn2_w_gate, m_ffn2_w_up, m_ffn2_w_down, m_ln3_g, m_ln3_b)
    vars_ = (v_meta, v_ffn1_w_gate, v_ffn1_w_up, v_ffn1_w_down, v_ln1_g, v_ln1_b, v_w_in, v_conv_w, v_conv_b, v_dt_bias, v_a_log, v_d_skip, v_ssd_norm_g, v_fox_f_b, v_mla_q_norm_g, v_mla_w_uq, v_mla_kv_norm_g, v_mla_w_ukv, v_w_out, v_ln2_g, v_ln2_b, v_ffn2_w_gate, v_ffn2_w_up, v_ffn2_w_down, v_ln3_g, v_ln3_b)
    P = dict(zip(_NAMES, vals))
    M = dict(zip(_NAMES, moms))
    V = dict(zip(_NAMES, vars_))
    me = 4 * lax.axis_index("x") + 2 * lax.axis_index("y") + lax.axis_index("c")

    me_arr = me.astype(jnp.int32).reshape(1)
    for n in _FFN_T:
        P[n], M[n], V[n] = (jnp.swapaxes(a[n], 1, 2) for a in (P, M, V))
    src = dict(P)
    src["w_in"] = w_in_to_padded(P["w_in"])
    order = [("meta", 0)] + [(n, l) for l in range(NL) for names in _STAGES.values() for n in names]
    zone_of = {nl_: i for i, nl_ in enumerate(order)}
    zones = [place_own(P["meta"][None], 0, F32, me_arr)]
    zones += [place_own(src[n], l, F32 if n == "conv_w" else BF16, me_arr) for n, l in order[1:]]
    hg = exchange_start("gather", zones, "gather_start")
    meta_full = _unshard_cols(exchange_wait(hg, [0], "gather_wait_meta", hg["token"])[1][0])

    def getw(l, stage, after):
        names = _STAGES[stage]
        lands = exchange_wait(hg, [zone_of[(n, l)] for n in names], f"gather_wait_{l}_{stage}", after)[1]
        return stage_weights(l, stage, dict(zip(names, lands)), P)

    sent = {}

    def emit(l, stage, G):
        bg = big_grads(G, stage)
        sent[(l, stage)] = exchange_start("scatter", [bg[n] for n in _STAGES[stage]], f"scatter_start_{l}_{stage}")
        return sent[(l, stage)]["token"]

    loss, gx, gmeta, grads = local_step(x[0], loss_target[0], meta_full, getw, emit)

    small = jnp.concatenate([pack_small({n: jnp.stack([small_grads(g)[n] for g in grads]) for n in _SMALL}), gmeta], axis=0)
    hs = exchange_start("gather", [place_own(small[None], 0, F32, me_arr)], "small_start")

    out = {}
    after = hs["token"]
    for stage in ("ffn2", "mix", "ffn1"):
        names = _STAGES[stage]
        got = [exchange_wait(sent[(l, stage)], list(range(len(names))), f"scatter_wait_{l}_{stage}", after)
               for l in range(NL - 1, -1, -1)][::-1]
        for i, n in enumerate(names):
            own = [got[l][0][i] for l in range(NL)]
            recv = [got[l][1][i] for l in range(NL)]
            if n == "w_in":
                g = jnp.stack([w_in_from_padded(sum_slots(recv[l], own[l], me_arr)) for l in range(NL)])
                out[n] = (g,) + adamw(P[n], M[n], V[n], g=g)
            else:
                out[n] = adamw(P[n], M[n], V[n], recv=recv, own=own, me_arr=me_arr)
                if n in _FFN_T:
                    out[n] = tuple(jnp.swapaxes(a, 1, 2) for a in out[n])
        after = out[names[-1]][1]
    gsmall = sum_slots(exchange_wait(hs, [0], "small_wait", after)[1][0])
    gm = lax.dynamic_slice(gsmall[NL * _SMALL_ROWS:], (0, me * (D // N_DEV)), (N_META, D // N_DEV))
    out["meta"] = (gm,) + adamw(P["meta"], M["meta"], V["meta"], g=gm)
    gs = gsmall[:NL * _SMALL_ROWS]
    sd, sm_, sv_ = adamw(pack_small(P), pack_small(M), pack_small(V), g=gs)
    ups = [unpack_small(a, P) for a in (gs, sd, sm_, sv_)]
    for n in _SMALL:
        out[n] = tuple(u[n] for u in ups)

    loss_all = lax.psum(loss[0, 0], ("x", "y", "c"))
    flat = [loss_all, gx[None]]
    for k in range(4):
        flat += [out[n][k] for n in _NAMES]
    return tuple(flat)
```

```python
import functools

import jax
import jax.numpy as jnp
from jax import lax
from jax.experimental import pallas as pl
from jax.experimental.pallas import tpu as pltpu

F32, BF16 = jnp.float32, jnp.bfloat16
HI = lax.Precision.HIGHEST

N_DEV = 8
D = 1024
NL = 2
N_META = 16
BLK = 128
PAD = BLK - N_META
D_FF = 2816
HS = D_FF // N_DEV
SSD_H, SSD_P, SSD_N, SSD_G = 8, 64, 64, 2
SSD_D = SSD_H * SSD_P
CONV_K = 4
CONV_D = SSD_D + 2 * SSD_G * SSD_N
FOX_H, FOX_DH = 4, 64
MLA_H, MLA_QL, MLA_KVL, MLA_NOPE, MLA_ROPE, MLA_V = 4, 256, 128, 64, 32, 64
N_IN = 2476
C_Z, C_XBC, C_FQ, C_FK, C_FV, C_CQ, C_CKV, C_SM, N_INP = 0, 512, 1280, 1536, 1792, 2048, 2304, 2432, 2560
SM_DT, SM_F, SM_KR = 0, 8, 64
ALPHA = (2 * NL) ** 0.25
EPS = 1e-5
NEG = -1e30
LR, B1, B2, AEPS, WD, STEP = 0.001, 0.9, 0.999, 1e-08, 0.01, 10
VMEM_MB = 56


def _cp(*sem):
    return pltpu.CompilerParams(dimension_semantics=sem, vmem_limit_bytes=VMEM_MB << 20)


def _nn(a, b):
    return lax.dot_general(a, b, (((1,), (0,)), ((), ())), preferred_element_type=F32)


def _nt(a, b):
    return lax.dot_general(a, b, (((1,), (1,)), ((), ())), preferred_element_type=F32)


def _tn(a, b):
    return lax.dot_general(a, b, (((0,), (0,)), ((), ())), preferred_element_type=F32)


def _nn_hi(a, b):
    return lax.dot_general(a, b, (((1,), (0,)), ((), ())), precision=HI, preferred_element_type=F32)


def _row_tile(t):
    for d in range(640, 15, -16):
        if t % d == 0:
            return d
    raise ValueError(t)


def _sig(x):
    return 1.0 / (1.0 + jnp.exp(-x))


def _tri(lower=True):
    r = lax.broadcasted_iota(jnp.int32, (BLK, BLK), 0)
    c = lax.broadcasted_iota(jnp.int32, (BLK, BLK), 1)
    return (r >= c) if lower else (r <= c)


def build_h0(meta_full, x):
    s = x.shape[0]
    nb = s // BLK + 1

    def body(m_ref, x_ref, h_ref, hb_ref):
        i = pl.program_id(0)

        @pl.when(i == 0)
        def _():
            h = jnp.concatenate([jnp.zeros((PAD, D), F32), m_ref[...]], axis=0)
            h_ref[...] = h
            hb_ref[...] = h.astype(BF16)

        @pl.when(i > 0)
        def _():
            h_ref[...] = x_ref[...]
            hb_ref[...] = x_ref[...].astype(BF16)

    return pl.pallas_call(
        body, name="build_h0", grid=(nb,),
        in_specs=[pl.BlockSpec((N_META, D), lambda i: (0, 0)),
                  pl.BlockSpec((BLK, D), lambda i: (jnp.maximum(i - 1, 0), 0))],
        out_specs=[pl.BlockSpec((BLK, D), lambda i: (i, 0))] * 2,
        out_shape=[jax.ShapeDtypeStruct((nb * BLK, D), F32), jax.ShapeDtypeStruct((nb * BLK, D), BF16)],
        compiler_params=_cp("arbitrary"),
    )(meta_full, x)


FT = 256


def _layer_norm(r, gamma, beta):
    mu = jnp.mean(r, axis=1, keepdims=True)
    xc = r - mu
    var = jnp.mean(xc * xc, axis=1, keepdims=True)
    return xc * lax.rsqrt(var + EPS) * gamma + beta


def ffn_fwd(hb, res, wg, wu, wd, gamma, beta):
    t = hb.shape[0]
    f = wg.shape[0]
    tm = _row_tile(t)
    nj = f // FT

    def body(h_ref, res_ref, wg_ref, wu_ref, wd_ref, g_ref, be_ref, u_ref, v_ref, r_ref, y_ref, yb_ref, acc, us, vs):
        j = pl.program_id(1)

        def up():
            h = h_ref[...]
            u = _nt(h, wg_ref[...])
            v = _nt(h, wu_ref[...])
            u_ref[...] = u.astype(BF16)
            v_ref[...] = v.astype(BF16)
            return u, v

        def down():
            u, v = us[...], vs[...]
            return _nn((u * _sig(u) * v).astype(BF16), wd_ref[...])

        @pl.when(j == 0)
        def _():
            us[...], vs[...] = up()
            acc[...] = jnp.zeros_like(acc)

        @pl.when((j > 0) & (j < nj))
        def _():
            d = down()
            u, v = up()
            acc[...] += d
            us[...] = u
            vs[...] = v

        @pl.when(j == nj)
        def _():
            r = ALPHA * res_ref[...] + 0.5 * (acc[...] + down())
            y = _layer_norm(r, g_ref[...], be_ref[...])
            r_ref[...] = r
            y_ref[...] = y
            yb_ref[...] = y.astype(BF16)

    row = pl.BlockSpec((tm, D), lambda i, j: (i, 0))
    vec = pl.BlockSpec((1, D), lambda i, j: (0, 0))
    wup = pl.BlockSpec((FT, D), lambda i, j: (jnp.minimum(j, nj - 1), 0))
    wdn = pl.BlockSpec((FT, D), lambda i, j: (jnp.maximum(j - 1, 0), 0))
    act = pl.BlockSpec((tm, FT), lambda i, j: (i, jnp.minimum(j, nj - 1)))
    return pl.pallas_call(
        body, name="ffn_fwd", grid=(t // tm, nj + 1),
        in_specs=[row, row, wup, wup, wdn, vec, vec],
        out_specs=[act, act, row, row, row],
        out_shape=[jax.ShapeDtypeStruct((t, f), BF16), jax.ShapeDtypeStruct((t, f), BF16),
                   jax.ShapeDtypeStruct((t, D), F32), jax.ShapeDtypeStruct((t, D), F32),
                   jax.ShapeDtypeStruct((t, D), BF16)],
        scratch_shapes=[pltpu.VMEM((tm, D), F32), pltpu.VMEM((tm, FT), F32), pltpu.VMEM((tm, FT), F32)],
        compiler_params=_cp("arbitrary", "arbitrary"),
    )(hb, res, wg, wu, wd, gamma, beta)


def ffn_bwd_act(dfb, u, v, wg, wu, wd):
    t, f = u.shape
    tm = _row_tile(t)

    nj = f // FT

    def body(df_ref, u_ref, v_ref, wg_ref, wu_ref, wd_ref, du_ref, dv_ref, dh_ref, das):
        j = pl.program_id(1)

        def first():
            return _nt(df_ref[...], wd_ref[...])

        def second():
            da = das[...]
            uu = u_ref[...].astype(F32)
            sg = _sig(uu)
            du = (da * v_ref[...].astype(F32) * (sg * (1.0 + uu * (1.0 - sg)))).astype(BF16)
            dv = (da * uu * sg).astype(BF16)
            du_ref[...] = du
            dv_ref[...] = dv
            return _nn(du, wg_ref[...]) + _nn(dv, wu_ref[...])

        @pl.when(j == 0)
        def _():
            das[...] = first()
            dh_ref[...] = jnp.zeros_like(dh_ref)

        @pl.when((j > 0) & (j < nj))
        def _():
            tot = second()
            da = first()
            dh_ref[...] += tot
            das[...] = da

        @pl.when(j == nj)
        def _():
            dh_ref[...] += second()

    row = pl.BlockSpec((tm, D), lambda i, j: (i, 0))
    wfirst = pl.BlockSpec((FT, D), lambda i, j: (jnp.minimum(j, nj - 1), 0))
    wsecond = pl.BlockSpec((FT, D), lambda i, j: (jnp.maximum(j - 1, 0), 0))
    act = pl.BlockSpec((tm, FT), lambda i, j: (i, jnp.maximum(j - 1, 0)))
    return pl.pallas_call(
        body, name="ffn_bwd_act", grid=(t // tm, nj + 1),
        in_specs=[row, act, act, wsecond, wsecond, wfirst],
        out_specs=[act, act, row],
        out_shape=[jax.ShapeDtypeStruct((t, f), BF16), jax.ShapeDtypeStruct((t, f), BF16),
                   jax.ShapeDtypeStruct((t, D), F32)],
        scratch_shapes=[pltpu.VMEM((tm, FT), F32)],
        compiler_params=_cp("arbitrary", "arbitrary"),
    )(dfb, u, v, wg, wu, wd)


def ffn_bwd_w(hb, dfb, u, v, du, dv):
    t, f = u.shape

    def body(h_ref, df_ref, u_ref, v_ref, du_ref, dv_ref, dwg_ref, dwu_ref, dwd_ref, ht, dft):
        @pl.when(pl.program_id(0) == 0)
        def _():
            ht[...] = h_ref[...].T
            dft[...] = df_ref[...].T

        uu = u_ref[...].astype(F32)
        a = (uu * _sig(uu) * v_ref[...].astype(F32)).astype(BF16)
        dwg_ref[...] = _nn(ht[...], du_ref[...]).T.astype(BF16)
        dwu_ref[...] = _nn(ht[...], dv_ref[...]).T.astype(BF16)
        dwd_ref[...] = _nn(dft[...], a).T.astype(BF16)

    full = pl.BlockSpec((t, D), lambda j: (0, 0))
    act = pl.BlockSpec((t, FT), lambda j: (0, j))
    wsp = pl.BlockSpec((FT, D), lambda j: (j, 0))
    return pl.pallas_call(
        body, name="ffn_bwd_w", grid=(f // FT,),
        in_specs=[full, full, act, act, act, act],
        out_specs=[wsp] * 3,
        out_shape=[jax.ShapeDtypeStruct((f, D), BF16)] * 3,
        scratch_shapes=[pltpu.VMEM((D, t), BF16), pltpu.VMEM((D, t), BF16)],
        compiler_params=_cp("arbitrary"),
    )(hb, dfb, u, v, du, dv)


def mm_res_ln(a, b, res, alpha, scale, gamma, beta):
    g, t, k = a.shape
    tm = _row_tile(t)

    def body(a_ref, b_ref, res_ref, g_ref, be_ref, r_ref, y_ref, yb_ref, acc):
        gi = pl.program_id(1)

        @pl.when(gi == 0)
        def _():
            acc[...] = jnp.zeros_like(acc)

        acc[...] += _nn(a_ref[...], b_ref[...])

        @pl.when(gi == g - 1)
        def _():
            r = alpha * res_ref[...] + scale * acc[...]
            mu = jnp.mean(r, axis=1, keepdims=True)
            xc = r - mu
            var = jnp.mean(xc * xc, axis=1, keepdims=True)
            y = xc * lax.rsqrt(var + EPS) * g_ref[...] + be_ref[...]
            r_ref[...] = r
            y_ref[...] = y
            yb_ref[...] = y.astype(BF16)

    row = pl.BlockSpec((tm, D), lambda i, gi: (i, 0))
    vec = pl.BlockSpec((1, D), lambda i, gi: (0, 0))
    return pl.pallas_call(
        body, name="mm_res_ln", grid=(t // tm, g),
        in_specs=[pl.BlockSpec((None, tm, k), lambda i, gi: (gi, i, 0)),
                  pl.BlockSpec((None, k, D), lambda i, gi: (gi, 0, 0)), row, vec, vec],
        out_specs=[row] * 3,
        out_shape=[jax.ShapeDtypeStruct((t, D), F32), jax.ShapeDtypeStruct((t, D), F32),
                   jax.ShapeDtypeStruct((t, D), BF16)],
        scratch_shapes=[pltpu.VMEM((tm, D), F32)],
        compiler_params=_cp("arbitrary", "arbitrary"),
    )(a, b, res, gamma, beta)


def mm_nn(a, b, tn=512):
    t, k = a.shape
    n = b.shape[1]
    tm = _row_tile(t)

    def body(a_ref, b_ref, o_ref):
        o_ref[...] = _nn(a_ref[...], b_ref[...])

    return pl.pallas_call(
        body, name="mm_nn", grid=(n // tn, t // tm),
        in_specs=[pl.BlockSpec((tm, k), lambda j, i: (i, 0)), pl.BlockSpec((k, tn), lambda j, i: (0, j))],
        out_specs=pl.BlockSpec((tm, tn), lambda j, i: (i, j)),
        out_shape=jax.ShapeDtypeStruct((t, n), F32),
        compiler_params=_cp("arbitrary", "arbitrary"),
    )(a, b)


def mm_nt_reduce(pairs, n):
    g, t, _ = pairs[0][0].shape
    tm = _row_tile(t)
    npair = len(pairs)

    def body(*refs):
        o_ref = refs[-1]
        gi = pl.program_id(1)
        tot = _nt(refs[0][...], refs[1][...])
        for p in range(1, npair):
            tot += _nt(refs[2 * p][...], refs[2 * p + 1][...])

        @pl.when(gi == 0)
        def _():
            o_ref[...] = tot

        @pl.when(gi > 0)
        def _():
            o_ref[...] += tot

    in_specs, args = [], []
    for x, w in pairs:
        k = x.shape[2]
        in_specs += [pl.BlockSpec((None, tm, k), lambda i, gi: (gi, i, 0)),
                     pl.BlockSpec((None, n, k), lambda i, gi: (gi, 0, 0))]
        args += [x, w]
    return pl.pallas_call(
        body, name="mm_nt_reduce", grid=(t // tm, g),
        in_specs=in_specs, out_specs=pl.BlockSpec((tm, n), lambda i, gi: (i, 0)),
        out_shape=jax.ShapeDtypeStruct((t, n), F32),
        compiler_params=_cp("arbitrary", "arbitrary"),
    )(*args)


def mm_tn(x, y, out_dtype=BF16):
    gx, t, k = x.shape
    gy, _, n = y.shape
    g = max(gx, gy)
    tm = _row_tile(t)
    nt = t // tm

    def body(x_ref, y_ref, o_ref, acc):
        i = pl.program_id(1)

        @pl.when(i == 0)
        def _():
            acc[...] = jnp.zeros_like(acc)

        acc[...] += _tn(x_ref[...], y_ref[...])

        @pl.when(i == nt - 1)
        def _():
            o_ref[...] = acc[...].astype(out_dtype)

    return pl.pallas_call(
        body, name="mm_tn", grid=(g, nt),
        in_specs=[pl.BlockSpec((None, tm, k), (lambda gi, i: (gi, i, 0)) if gx > 1 else (lambda gi, i: (0, i, 0))),
                  pl.BlockSpec((None, tm, n), (lambda gi, i: (gi, i, 0)) if gy > 1 else (lambda gi, i: (0, i, 0)))],
        out_specs=pl.BlockSpec((None, k, n), lambda gi, i: (gi, 0, 0)),
        out_shape=jax.ShapeDtypeStruct((g, k, n), out_dtype),
        scratch_shapes=[pltpu.VMEM((k, n), F32)],
        compiler_params=_cp("arbitrary", "arbitrary"),
    )(x, y)


def ln_bwd(parts, r, gamma, out_scale, after=None):
    t = r.shape[0]
    tm = _row_tile(t)
    scales = [s for _, s in parts]
    npart = len(parts)
    extra = [] if after is None else [after]

    def body(*refs):
        refs = refs[len(extra):]
        r_ref, g_ref = refs[npart], refs[npart + 1]
        dr_ref, drb_ref, dg_ref, db_ref = refs[npart + 2:]
        i = pl.program_id(0)
        dy = scales[0] * refs[0][...]
        for p in range(1, npart):
            dy += scales[p] * refs[p][...]
        rr = r_ref[...]
        mu = jnp.mean(rr, axis=1, keepdims=True)
        xc = rr - mu
        rstd = lax.rsqrt(jnp.mean(xc * xc, axis=1, keepdims=True) + EPS)
        xh = xc * rstd
        dxh = dy * g_ref[...]
        m1 = jnp.mean(dxh, axis=1, keepdims=True)
        m2 = jnp.mean(dxh * xh, axis=1, keepdims=True)
        dr = rstd * (dxh - m1 - xh * m2)
        dr_ref[...] = dr
        drb_ref[...] = (out_scale * dr).astype(BF16)
        dg = jnp.sum(dy * xh, axis=0, keepdims=True)
        db = jnp.sum(dy, axis=0, keepdims=True)

        @pl.when(i == 0)
        def _():
            dg_ref[...] = dg
            db_ref[...] = db

        @pl.when(i > 0)
        def _():
            dg_ref[...] += dg
            db_ref[...] += db

    row = pl.BlockSpec((tm, D), lambda i: (i, 0))
    vec = pl.BlockSpec((1, D), lambda i: (0, 0))
    return pl.pallas_call(
        body, name="ln_bwd", grid=(t // tm,),
        in_specs=[_ANY] * len(extra) + [row] * (npart + 1) + [vec],
        out_specs=[row, row, vec, vec],
        out_shape=[jax.ShapeDtypeStruct((t, D), F32), jax.ShapeDtypeStruct((t, D), BF16),
                   jax.ShapeDtypeStruct((1, D), F32), jax.ShapeDtypeStruct((1, D), F32)],
        compiler_params=_cp("arbitrary"),
    )(*extra, *[p for p, _ in parts], r, gamma)


def loss_head(h, target):
    t = h.shape[0]
    nb = t // BLK

    def body(h_ref, t_ref, dy_ref, l_ref):
        i = pl.program_id(0)

        @pl.when(i == 0)
        def _():
            dy_ref[...] = jnp.zeros_like(dy_ref)
            l_ref[...] = jnp.zeros_like(l_ref)

        @pl.when(i > 0)
        def _():
            err = h_ref[...] - t_ref[...]
            dy_ref[...] = err * (1.0 / D)
            l_ref[...] += (0.5 / D) * jnp.sum(err * err, keepdims=True)

    return pl.pallas_call(
        body, name="loss_head", grid=(nb,),
        in_specs=[pl.BlockSpec((BLK, D), lambda i: (i, 0)),
                  pl.BlockSpec((BLK, D), lambda i: (jnp.maximum(i - 1, 0), 0))],
        out_specs=[pl.BlockSpec((BLK, D), lambda i: (i, 0)), pl.BlockSpec((1, 1), lambda i: (0, 0))],
        out_shape=[jax.ShapeDtypeStruct((t, D), F32), jax.ShapeDtypeStruct((1, 1), F32)],
        compiler_params=_cp("arbitrary"),
    )(h, target)


def final_add(dr, dh, after=None):
    t = dr.shape[0]
    nb = t // BLK
    extra = [] if after is None else [after]

    def body(*refs):
        a_ref, b_ref, gx_ref, gm_ref = refs[len(extra):]
        i = pl.program_id(0)
        tot = ALPHA * a_ref[...] + b_ref[...]

        @pl.when(i == 0)
        def _():
            gm_ref[...] = tot[PAD:, :]

        @pl.when(i > 0)
        def _():
            gx_ref[...] = tot

    blk = pl.BlockSpec((BLK, D), lambda i: (i, 0))
    return pl.pallas_call(
        body, name="final_add", grid=(nb,),
        in_specs=[_ANY] * len(extra) + [blk, blk],
        out_specs=[pl.BlockSpec((BLK, D), lambda i: (jnp.maximum(i - 1, 0), 0)),
                   pl.BlockSpec((N_META, D), lambda i: (0, 0))],
        out_shape=[jax.ShapeDtypeStruct((t - BLK, D), F32), jax.ShapeDtypeStruct((N_META, D), F32)],
        compiler_params=_cp("arbitrary"),
    )(*extra, dr, dh)


def _valid_rows(nrows, first_row):
    return (first_row + lax.broadcasted_iota(jnp.int32, (nrows, 1), 0)) >= PAD


def conv_fwd(proj, conv_w, conv_b):
    t = proj.shape[0]
    c0 = C_XBC // BLK

    def body(x_ref, w_ref, b_ref, o_ref):
        ok = _valid_rows(t, 0)
        x = jnp.where(ok, x_ref[...], 0.0)
        w = w_ref[...]
        acc = b_ref[...] + w[CONV_K - 1:CONV_K, :] * x
        for s in range(1, CONV_K):
            acc += w[CONV_K - 1 - s:CONV_K - s, :] * pltpu.roll(x, s, 0)
        o_ref[...] = jnp.where(ok, acc * _sig(acc), 0.0)

    return pl.pallas_call(
        body, name="conv_fwd", grid=(CONV_D // BLK,),
        in_specs=[pl.BlockSpec((t, BLK), lambda j: (0, c0 + j)),
                  pl.BlockSpec((CONV_K, BLK), lambda j: (0, j)), pl.BlockSpec((1, BLK), lambda j: (0, j))],
        out_specs=pl.BlockSpec((t, BLK), lambda j: (0, j)),
        out_shape=jax.ShapeDtypeStruct((t, CONV_D), F32),
        compiler_params=_cp("arbitrary"),
    )(proj, conv_w, conv_b)


def conv_bwd(dxa, proj, conv_w, conv_b):
    t = proj.shape[0]
    c0 = C_XBC // BLK

    def body(d_ref, x_ref, w_ref, b_ref, dx_ref, dw_ref, db_ref):
        ok = _valid_rows(t, 0)
        x = jnp.where(ok, x_ref[...], 0.0)
        w = w_ref[...]
        xs = [x] + [pltpu.roll(x, s, 0) for s in range(1, CONV_K)]
        acc = b_ref[...] + w[CONV_K - 1:CONV_K, :] * x
        for s in range(1, CONV_K):
            acc += w[CONV_K - 1 - s:CONV_K - s, :] * xs[s]
        sg = _sig(acc)
        dxc = jnp.where(ok, d_ref[...] * (sg * (1.0 + acc * (1.0 - sg))), 0.0)
        db_ref[...] = jnp.sum(dxc, axis=0, keepdims=True)
        dw_ref[...] = jnp.concatenate(
            [jnp.sum(dxc * xs[CONV_K - 1 - k], axis=0, keepdims=True) for k in range(CONV_K)], axis=0)
        dx = w[CONV_K - 1:CONV_K, :] * dxc
        for s in range(1, CONV_K):
            dx += w[CONV_K - 1 - s:CONV_K - s, :] * pltpu.roll(dxc, t - s, 0)
        dx_ref[...] = jnp.where(ok, dx, 0.0)

    col = pl.BlockSpec((t, BLK), lambda j: (0, j))
    return pl.pallas_call(
        body, name="conv_bwd", grid=(CONV_D // BLK,),
        in_specs=[col, pl.BlockSpec((t, BLK), lambda j: (0, c0 + j)),
                  pl.BlockSpec((CONV_K, BLK), lambda j: (0, j)), pl.BlockSpec((1, BLK), lambda j: (0, j))],
        out_specs=[col, pl.BlockSpec((CONV_K, BLK), lambda j: (0, j)), pl.BlockSpec((1, BLK), lambda j: (0, j))],
        out_shape=[jax.ShapeDtypeStruct((t, CONV_D), F32), jax.ShapeDtypeStruct((CONV_K, CONV_D), F32),
                   jax.ShapeDtypeStruct((1, CONV_D), F32)],
        compiler_params=_cp("arbitrary"),
    )(dxa, proj, conv_w, conv_b)


def _softplus(x):
    return jnp.maximum(x, 0.0) + jnp.log(1.0 + jnp.exp(-jnp.abs(x)))


def _ssd_chunk(xa, sm, dtb, alog, ok):
    dt = jnp.where(ok, _softplus(sm + dtb), 0.0)
    amat = -jnp.exp(alog)
    a = dt * amat
    ac = _nn_hi(_tri().astype(F32), a)
    act = ac.T
    return dt, amat, ac, act


def _ssd_head(xa, dt, ac, act, h, cb, sp):
    g = h // (SSD_H // SSD_G)
    xs = xa[:, SSD_P * h:SSD_P * (h + 1)]
    bg = xa[:, SSD_D + SSD_N * g:SSD_D + SSD_N * (g + 1)]
    cg = xa[:, SSD_D + SSD_G * SSD_N + SSD_N * g:SSD_D + SSD_G * SSD_N + SSD_N * (g + 1)]
    dth = dt[:, h:h + 1]
    ach = ac[:, h:h + 1]
    acth = act[h:h + 1, :]
    xdt = xs * dth
    seg = jnp.where(_tri(), jnp.exp(jnp.minimum(ach - acth, 0.0)), 0.0)
    m = cb * seg
    yd = _nn(m.astype(BF16), xdt.astype(BF16))
    last = ac[BLK - 1:BLK, h:h + 1]
    dec = jnp.exp(last - ach)
    e = jnp.exp(ach)
    yo = _nn(cg.astype(BF16), sp.astype(BF16)) * e
    return xs, bg, cg, dth, ach, xdt, seg, m, yd, last, dec, e, yo


def ssd_fwd(xa, proj, dtb, alog, dskip, normg):
    t = xa.shape[0]
    nb = t // BLK
    gw = SSD_D // SSD_G

    def body(xa_ref, z_ref, sm_ref, dtb_ref, al_ref, ds_ref, ng_ref, y_ref, sp_ref, st):
        c = pl.program_id(0)

        @pl.when(c == 0)
        def _():
            st[...] = jnp.zeros_like(st)

        ok = _valid_rows(BLK, c * BLK)
        xa = xa_ref[...]
        dt, _, ac, act = _ssd_chunk(xa, sm_ref[...], dtb_ref[...], al_ref[...], ok)
        sp_ref[...] = st[...]
        ys = []
        cbs = {}
        for h in range(SSD_H):
            g = h // (SSD_H // SSD_G)
            if g not in cbs:
                bg = xa[:, SSD_D + SSD_N * g:SSD_D + SSD_N * (g + 1)]
                cg = xa[:, SSD_D + SSD_G * SSD_N + SSD_N * g:SSD_D + SSD_G * SSD_N + SSD_N * (g + 1)]
                cbs[g] = _nt(cg.astype(BF16), bg.astype(BF16))
            sp = st[:, SSD_P * h:SSD_P * (h + 1)]
            xs, bg, cg, dth, ach, xdt, seg, m, yd, last, dec, e, yo = _ssd_head(xa, dt, ac, act, h, cbs[g], sp)
            sloc = _tn((bg * dec).astype(BF16), xdt.astype(BF16))
            st[:, SSD_P * h:SSD_P * (h + 1)] = jnp.exp(last) * sp + sloc
            ys.append(yd + yo + ds_ref[:, h:h + 1] * xs)
        y = jnp.concatenate(ys, axis=1)
        z = z_ref[...]
        yg = y * (z * _sig(z))
        outs = []
        for g in range(SSD_G):
            v = yg[:, gw * g:gw * (g + 1)]
            outs.append(v * lax.rsqrt(jnp.mean(v * v, axis=1, keepdims=True) + EPS))
        y_ref[...] = (jnp.concatenate(outs, axis=1) * ng_ref[...]).astype(BF16)

    vec = pl.BlockSpec((1, BLK), lambda c: (0, 0))
    return pl.pallas_call(
        body, name="ssd_fwd", grid=(nb,),
        in_specs=[pl.BlockSpec((BLK, CONV_D), lambda c: (c, 0)),
                  pl.BlockSpec((BLK, SSD_D), lambda c: (c, C_Z // SSD_D)),
                  pl.BlockSpec((BLK, BLK), lambda c: (c, C_SM // BLK)),
                  vec, vec, vec, pl.BlockSpec((1, SSD_D), lambda c: (0, 0))],
        out_specs=[pl.BlockSpec((BLK, SSD_D), lambda c: (c, 0)),
                   pl.BlockSpec((None, SSD_N, SSD_D), lambda c: (c, 0, 0))],
        out_shape=[jax.ShapeDtypeStruct((t, SSD_D), BF16), jax.ShapeDtypeStruct((nb, SSD_N, SSD_D), F32)],
        scratch_shapes=[pltpu.VMEM((SSD_N, SSD_D), F32)],
        compiler_params=_cp("arbitrary"),
    )(xa, proj, proj, dtb, alog, dskip, normg)


def _lane_put(col, lane):
    li = lax.broadcasted_iota(jnp.int32, (col.shape[0], BLK), 1)
    return jnp.where(li == lane, col, 0.0)


def ssd_bwd(dmix, xa, proj, sprev, dtb, alog, dskip, normg):
    t = xa.shape[0]
    nb = t // BLK
    gw = SSD_D // SSD_G
    rev = lambda c: nb - 1 - c

    def body(dy_ref, xa_ref, z_ref, sm_ref, sp_ref, dtb_ref, al_ref, ds_ref, ng_ref,
             dxa_ref, dz_ref, dsm_ref, dng_ref, dds_ref, dal_ref, ddtb_ref, dst):
        c = pl.program_id(0)

        @pl.when(c == 0)
        def _():
            dst[...] = jnp.zeros_like(dst)
            dng_ref[...] = jnp.zeros_like(dng_ref)
            dds_ref[...] = jnp.zeros_like(dds_ref)
            dal_ref[...] = jnp.zeros_like(dal_ref)
            ddtb_ref[...] = jnp.zeros_like(ddtb_ref)

        ok = _valid_rows(BLK, rev(c) * BLK)
        xa = xa_ref[...]
        sm = sm_ref[...]
        dt, amat, ac, act = _ssd_chunk(xa, sm, dtb_ref[...], al_ref[...], ok)
        tri = _tri()
        rowi = lax.broadcasted_iota(jnp.int32, (BLK, 1), 0)
        cbs, heads, ys = {}, [], []
        for h in range(SSD_H):
            g = h // (SSD_H // SSD_G)
            if g not in cbs:
                bg = xa[:, SSD_D + SSD_N * g:SSD_D + SSD_N * (g + 1)]
                cg = xa[:, SSD_D + SSD_G * SSD_N + SSD_N * g:SSD_D + SSD_G * SSD_N + SSD_N * (g + 1)]
                cbs[g] = _nt(cg.astype(BF16), bg.astype(BF16))
            sp = sp_ref[:, SSD_P * h:SSD_P * (h + 1)]
            hd = _ssd_head(xa, dt, ac, act, h, cbs[g], sp)
            heads.append(hd)
            ys.append(hd[8] + hd[12] + ds_ref[:, h:h + 1] * hd[0])
        y = jnp.concatenate(ys, axis=1)
        z = z_ref[...]
        sgz = _sig(z)
        siluz = z * sgz
        yg = y * siluz
        dout = dy_ref[...]
        ng = ng_ref[...]
        dygs, xhs = [], []
        for g in range(SSD_G):
            v = yg[:, gw * g:gw * (g + 1)]
            rr = lax.rsqrt(jnp.mean(v * v, axis=1, keepdims=True) + EPS)
            xh = v * rr
            dxh = dout[:, gw * g:gw * (g + 1)] * ng[:, gw * g:gw * (g + 1)]
            dygs.append(rr * (dxh - xh * jnp.mean(dxh * xh, axis=1, keepdims=True)))
            xhs.append(xh)
        dyg = jnp.concatenate(dygs, axis=1)
        dng_ref[...] += jnp.sum(dout * jnp.concatenate(xhs, axis=1), axis=0, keepdims=True)
        dy = dyg * siluz
        dz_ref[...] = dyg * y * (sgz * (1.0 + z * (1.0 - sgz)))

        dxs_l = []
        db_g = [jnp.zeros((BLK, SSD_N), F32) for _ in range(SSD_G)]
        dc_g = [jnp.zeros((BLK, SSD_N), F32) for _ in range(SSD_G)]
        dac_all = jnp.zeros((BLK, BLK), F32)
        ddt_all = jnp.zeros((BLK, BLK), F32)
        dds_row = jnp.zeros((1, BLK), F32)
        lane1 = lax.broadcasted_iota(jnp.int32, (1, BLK), 1)
        for h in range(SSD_H):
            g = h // (SSD_H // SSD_G)
            xs, bg, cg, dth, ach, xdt, seg, m, yd, last, dec, e, yo = heads[h]
            sp = sp_ref[:, SSD_P * h:SSD_P * (h + 1)]
            dyh = dy[:, SSD_P * h:SSD_P * (h + 1)]
            dyb = dyh.astype(BF16)
            xdtb = xdt.astype(BF16)
            dds_row += jnp.where(lane1 == h, jnp.sum(dyh * xs, keepdims=True), 0.0)
            dxs = ds_ref[:, h:h + 1] * dyh
            dyo = (dyh * e).astype(BF16)
            dc_g[g] += _nt(dyo, sp.astype(BF16))
            dsp = _tn(cg.astype(BF16), dyo)
            dac = jnp.sum(dyh * yo, axis=1, keepdims=True)
            dsn = dst[:, SSD_P * h:SSD_P * (h + 1)]
            gl = jnp.exp(last)
            dst[:, SSD_P * h:SSD_P * (h + 1)] = dsp + gl * dsn
            dlast = jnp.sum(dsn * sp, keepdims=True) * gl
            dsnb = dsn.astype(BF16)
            dbd = _nt(xdtb, dsnb)
            db_g[g] += dbd * dec
            tdec = jnp.sum(dbd * bg, axis=1, keepdims=True) * dec
            dxdt = _nn((bg * dec).astype(BF16), dsnb)
            dlast += jnp.sum(tdec, keepdims=True)
            dac -= tdec
            dm = _nt(dyb, xdtb)
            dxdt += _tn(m.astype(BF16), dyb)
            dcb = (dm * seg).astype(BF16)
            dc_g[g] += _nn(dcb, bg.astype(BF16))
            db_g[g] += _tn(dcb, cg.astype(BF16))
            w = dm * m
            dac += jnp.sum(w, axis=1, keepdims=True) - jnp.sum(w.T, axis=1, keepdims=True)
            dac += jnp.where(rowi == BLK - 1, dlast, 0.0)
            dxs_l.append(dxs + dxdt * dth)
            ddt_all += _lane_put(jnp.sum(dxdt * xs, axis=1, keepdims=True), h)
            dac_all += _lane_put(dac, h)
        da = _nn_hi(_tri(lower=False).astype(F32), dac_all)
        ddt = ddt_all + da * amat
        dal_ref[...] += jnp.sum(da * dt, axis=0, keepdims=True) * amat
        ddtr = jnp.where(ok, ddt * _sig(sm + dtb_ref[...]), 0.0)
        ddtb_ref[...] += jnp.sum(ddtr, axis=0, keepdims=True)
        dds_ref[...] += dds_row
        dsm_ref[...] = ddtr
        dxa_ref[...] = jnp.where(ok, jnp.concatenate(dxs_l + db_g + dc_g, axis=1), 0.0)

    vec = pl.BlockSpec((1, BLK), lambda c: (0, 0))
    nvec = pl.BlockSpec((1, SSD_D), lambda c: (0, 0))
    return pl.pallas_call(
        body, name="ssd_bwd", grid=(nb,),
        in_specs=[pl.BlockSpec((BLK, SSD_D), lambda c: (rev(c), 0)),
                  pl.BlockSpec((BLK, CONV_D), lambda c: (rev(c), 0)),
                  pl.BlockSpec((BLK, SSD_D), lambda c: (rev(c), C_Z // SSD_D)),
                  pl.BlockSpec((BLK, BLK), lambda c: (rev(c), C_SM // BLK)),
                  pl.BlockSpec((None, SSD_N, SSD_D), lambda c: (rev(c), 0, 0)),
                  vec, vec, vec, nvec],
        out_specs=[pl.BlockSpec((BLK, CONV_D), lambda c: (rev(c), 0)),
                   pl.BlockSpec((BLK, SSD_D), lambda c: (rev(c), 0)),
                   pl.BlockSpec((BLK, BLK), lambda c: (rev(c), 0)),
                   nvec, vec, vec, vec],
        out_shape=[jax.ShapeDtypeStruct((t, CONV_D), F32), jax.ShapeDtypeStruct((t, SSD_D), F32),
                   jax.ShapeDtypeStruct((t, BLK), F32), jax.ShapeDtypeStruct((1, SSD_D), F32),
                   jax.ShapeDtypeStruct((1, BLK), F32), jax.ShapeDtypeStruct((1, BLK), F32),
                   jax.ShapeDtypeStruct((1, BLK), F32)],
        scratch_shapes=[pltpu.VMEM((SSD_N, SSD_D), F32)],
        compiler_params=_cp("arbitrary"),
    )(dmix, xa, proj, proj, sprev, dtb, alog, dskip, normg)


def _attn_scores(q_ref, k_ref, h, dq, scale, mask, bias):
    qh = q_ref[:, dq * h:dq * (h + 1)].astype(BF16)
    kh = k_ref[:, dq * h:dq * (h + 1)].astype(BF16)
    s = _nt(qh, kh) * scale
    if bias is not None:
        s = s + bias
    return qh, kh, jnp.where(mask, s, NEG)


def _segments(nb):
    cuts = sorted({0, nb} | {max(1, round(nb * f)) for f in (0.3, 0.53, 0.77)})
    return list(zip(cuts[:-1], cuts[1:]))


def attn_fwd(q, k, v, qcol, kcol, vcol, nh, dq, dv, scale, c_col=None, c_row=None, lane0=0):
    t = q.shape[0]
    tq = BLK
    use_bias = c_col is not None

    def segment(t0, t1, prev):
        tk = t1 * BLK
        nprev = len(prev)

        def body(*refs):
            refs = refs[nprev:]
            if use_bias:
                q_ref, k_ref, v_ref, cc_ref, cr_ref, o_ref, l_ref = refs
            else:
                q_ref, k_ref, v_ref, o_ref, l_ref = refs
            i = pl.program_id(0)
            rowg = (t0 + i) * tq + lax.broadcasted_iota(jnp.int32, (tq, 1), 0)
            col = lax.broadcasted_iota(jnp.int32, (1, tk), 1)
            mask = (col <= rowg) & (col >= PAD)
            outs = []
            lse = jnp.zeros((tq, BLK), F32)
            for h in range(nh):
                bias = (cc_ref[:, lane0 + h:lane0 + h + 1] - cr_ref[h:h + 1, :]) if use_bias else None
                _, _, s = _attn_scores(q_ref, k_ref, h, dq, scale, mask, bias)
                m = jnp.max(s, axis=1, keepdims=True)
                p = jnp.exp(s - m)
                l = jnp.sum(p, axis=1, keepdims=True)
                vh = v_ref[:, dv * h:dv * (h + 1)].astype(BF16)
                outs.append(_nn(p.astype(BF16), vh) / l)
                lse += _lane_put(m + jnp.log(l), h)
            o_ref[...] = jnp.concatenate(outs, axis=1).astype(BF16)
            l_ref[...] = lse

        in_specs = [_ANY] * nprev + [pl.BlockSpec((tq, nh * dq), lambda i: (t0 + i, qcol)),
                                     pl.BlockSpec((tk, nh * dq), lambda i: (0, kcol)),
                                     pl.BlockSpec((tk, nh * dv), lambda i: (0, vcol))]
        args = list(prev) + [q, k, v]
        if use_bias:
            in_specs += [pl.BlockSpec((tq, BLK), lambda i: (t0 + i, 0)), pl.BlockSpec((8, tk), lambda i: (0, 0))]
            args += [c_col, c_row]
        return pl.pallas_call(
            body, name="attn_fwd", grid=(t1 - t0,),
            in_specs=in_specs,
            out_specs=[pl.BlockSpec((tq, nh * dv), lambda i: (t0 + i, 0)), pl.BlockSpec((tq, BLK), lambda i: (t0 + i, 0))],
            out_shape=[jax.ShapeDtypeStruct((t, nh * dv), BF16), jax.ShapeDtypeStruct((t, BLK), F32)],
            input_output_aliases={p: p for p in range(nprev)},
            compiler_params=_cp("arbitrary"),
        )(*args)

    outs = []
    for t0, t1 in _segments(t // tq):
        outs = segment(t0, t1, outs)
    return outs


def attn_bwd(q, k, v, do, lse, qcol, kcol, vcol, docol, nh, dq, dv, scale, c_col=None, c_row=None, lane0=0):
    t = q.shape[0]
    tq = BLK
    use_bias = c_col is not None

    def segment(t0, t1, prev):
        tk = t1 * BLK
        nprev = len(prev)

        def body(*refs):
            pv, refs = refs[:nprev], refs[nprev:]
            if use_bias:
                q_ref, k_ref, v_ref, do_ref, l_ref, cc_ref, cr_ref, dq_ref, dk_ref, dv_ref, dcq_ref, dck_ref = refs
            else:
                q_ref, k_ref, v_ref, do_ref, l_ref, dq_ref, dk_ref, dv_ref = refs
            i = pl.program_id(0)

            @pl.when(i == 0)
            def _():
                if nprev:
                    dk_ref[...] = pv[1][...]
                    dv_ref[...] = pv[2][...]
                    if use_bias:
                        dck_ref[...] = pv[4][...]
                else:
                    dk_ref[...] = jnp.zeros_like(dk_ref)
                    dv_ref[...] = jnp.zeros_like(dv_ref)
                    if use_bias:
                        dck_ref[...] = jnp.zeros_like(dck_ref)

            rowg = (t0 + i) * tq + lax.broadcasted_iota(jnp.int32, (tq, 1), 0)
            col = lax.broadcasted_iota(jnp.int32, (1, tk), 1)
            mask = (col <= rowg) & (col >= PAD)
            dqs = []
            dcq = jnp.zeros((tq, BLK), F32)
            for h in range(nh):
                bias = (cc_ref[:, lane0 + h:lane0 + h + 1] - cr_ref[h:h + 1, :]) if use_bias else None
                qh, kh, s = _attn_scores(q_ref, k_ref, h, dq, scale, mask, bias)
                p = jnp.where(mask, jnp.exp(s - l_ref[:, h:h + 1]), 0.0)
                vh = v_ref[:, dv * h:dv * (h + 1)].astype(BF16)
                doh = do_ref[:, dv * h:dv * (h + 1)].astype(BF16)
                dp = _nt(doh, vh)
                delta = jnp.sum(p * dp, axis=1, keepdims=True)
                ds = p * (dp - delta)
                dsb = ds.astype(BF16)
                dqs.append(_nn(dsb, kh) * scale)
                dk_ref[:, dq * h:dq * (h + 1)] += _tn(dsb, qh) * scale
                dv_ref[:, dv * h:dv * (h + 1)] += _tn(p.astype(BF16), doh)
                if use_bias:
                    dcq += _lane_put(jnp.sum(ds, axis=1, keepdims=True), lane0 + h)
                    dck_ref[h:h + 1, :] += jnp.sum(ds, axis=0, keepdims=True)
            dq_ref[...] = jnp.concatenate(dqs, axis=1)
            if use_bias:
                dcq_ref[...] = dcq

        keys_q = pl.BlockSpec((tk, nh * dq), lambda i: (0, 0))
        keys_v = pl.BlockSpec((tk, nh * dv), lambda i: (0, 0))
        keys_c = pl.BlockSpec((8, tk), lambda i: (0, 0))
        prev_specs = ([_ANY, keys_q, keys_v] + ([_ANY, keys_c] if use_bias else [])) if nprev else []
        in_specs = prev_specs + [pl.BlockSpec((tq, nh * dq), lambda i: (t0 + i, qcol)),
                                 pl.BlockSpec((tk, nh * dq), lambda i: (0, kcol)),
                                 pl.BlockSpec((tk, nh * dv), lambda i: (0, vcol)),
                                 pl.BlockSpec((tq, nh * dv), lambda i: (t0 + i, docol)),
                                 pl.BlockSpec((tq, BLK), lambda i: (t0 + i, 0))]
        args = list(prev) + [q, k, v, do, lse]
        out_specs = [pl.BlockSpec((tq, nh * dq), lambda i: (t0 + i, 0)), keys_q, keys_v]
        out_shape = [jax.ShapeDtypeStruct((t, nh * dq), F32), jax.ShapeDtypeStruct((t, nh * dq), F32),
                     jax.ShapeDtypeStruct((t, nh * dv), F32)]
        if use_bias:
            in_specs += [pl.BlockSpec((tq, BLK), lambda i: (t0 + i, 0)), keys_c]
            args += [c_col, c_row]
            out_specs += [pl.BlockSpec((tq, BLK), lambda i: (t0 + i, 0)), keys_c]
            out_shape += [jax.ShapeDtypeStruct((t, BLK), F32), jax.ShapeDtypeStruct((8, t), F32)]
        return pl.pallas_call(
            body, name="attn_bwd", grid=(t1 - t0,),
            in_specs=in_specs, out_specs=out_specs, out_shape=out_shape,
            input_output_aliases={p: p for p in range(nprev)},
            compiler_params=_cp("arbitrary"),
        )(*args)

    outs = []
    for t0, t1 in reversed(_segments(t // tq)):
        outs = segment(t0, t1, outs)
    return outs


def fox_pre(proj, fb):
    t = proj.shape[0]
    nb = t // BLK

    def body(sm_ref, fb_ref, c_ref, cr_ref):
        x = sm_ref[...] + fb_ref[...]
        lane = lax.broadcasted_iota(jnp.int32, (1, BLK), 1)
        keep = _valid_rows(t, 0) & (lane >= SM_F) & (lane < SM_F + FOX_H)
        logf = jnp.where(keep, jnp.minimum(x, 0.0) - jnp.log(1.0 + jnp.exp(-jnp.abs(x))), 0.0)
        tri = _tri().astype(F32)
        carry = jnp.zeros((1, BLK), F32)
        for b in range(nb):
            cb = _nn_hi(tri, logf[b * BLK:(b + 1) * BLK, :]) + carry
            c_ref[b * BLK:(b + 1) * BLK, :] = cb
            carry = cb[BLK - 1:BLK, :]
        cr_ref[...] = c_ref[...].T[SM_F:SM_F + 8, :]

    return pl.pallas_call(
        body, name="fox_pre", grid=(1,),
        in_specs=[pl.BlockSpec((t, BLK), lambda i: (0, C_SM // BLK)), pl.BlockSpec((1, BLK), lambda i: (0, 0))],
        out_specs=[pl.BlockSpec((t, BLK), lambda i: (0, 0)), pl.BlockSpec((8, t), lambda i: (0, 0))],
        out_shape=[jax.ShapeDtypeStruct((t, BLK), F32), jax.ShapeDtypeStruct((8, t), F32)],
        compiler_params=_cp("arbitrary"),
    )(proj, fb)


def fox_pre_bwd(dcq, dck, proj, fb, dsm_in):
    t = proj.shape[0]
    nb = t // BLK

    def body(dcq_ref, dck_ref, sm_ref, fb_ref, din_ref, dsm_ref, dfb_ref, scr):
        triu = _tri(lower=False).astype(F32)
        carry = jnp.zeros((1, BLK), F32)
        scr[...] = jnp.concatenate([jnp.zeros((SM_F, t), F32), dck_ref[...], jnp.zeros((BLK - SM_F - 8, t), F32)], axis=0).T
        for b in range(nb - 1, -1, -1):
            blk = dcq_ref[b * BLK:(b + 1) * BLK, :] - scr[b * BLK:(b + 1) * BLK, :]
            cb = _nn_hi(triu, blk) + carry
            scr[b * BLK:(b + 1) * BLK, :] = cb
            carry = cb[0:1, :]
        x = sm_ref[...] + fb_ref[...]
        lane = lax.broadcasted_iota(jnp.int32, (1, BLK), 1)
        keep = _valid_rows(t, 0) & (lane >= SM_F) & (lane < SM_F + FOX_H)
        df = jnp.where(keep, scr[...] * _sig(-x), 0.0)
        dfb_ref[...] = jnp.sum(df, axis=0, keepdims=True)
        dsm_ref[...] = din_ref[...] + df

    full = pl.BlockSpec((t, BLK), lambda i: (0, 0))
    return pl.pallas_call(
        body, name="fox_pre_bwd", grid=(1,),
        in_specs=[full, pl.BlockSpec((8, t), lambda i: (0, 0)), pl.BlockSpec((t, BLK), lambda i: (0, C_SM // BLK)),
                  pl.BlockSpec((1, BLK), lambda i: (0, 0)), full],
        out_specs=[full, pl.BlockSpec((1, BLK), lambda i: (0, 0))],
        out_shape=[jax.ShapeDtypeStruct((t, BLK), F32), jax.ShapeDtypeStruct((1, BLK), F32)],
        scratch_shapes=[pltpu.VMEM((t, BLK), F32)],
        compiler_params=_cp("arbitrary"),
    )(dcq, dck, proj, fb, dsm_in)


def _swap_rope(x):
    lane = lax.broadcasted_iota(jnp.int32, (1, BLK), 1)
    return jnp.where((lane >= SM_KR) & (lane < SM_KR + 16), pltpu.roll(x, BLK - 16, 1),
                     jnp.where((lane >= SM_KR + 16) & (lane < SM_KR + 32), pltpu.roll(x, 16, 1), 0.0))


def _rms(x, g):
    r = lax.rsqrt(jnp.mean(x * x, axis=1, keepdims=True) + EPS)
    return r, x * r


def mla_pre(proj, qg, kvg, wq, wk, wv, cosq, sinq):
    t = proj.shape[0]
    tm = _row_tile(t)

    def body(cq_ref, ckv_ref, sm_ref, qg_ref, kvg_ref, wq_ref, wk_ref, wv_ref, cos_ref, sin_ref,
             q_ref, k_ref, v_ref, cqn_ref, ckvn_ref):
        cs, sn = cos_ref[...], sin_ref[...]
        _, xh = _rms(cq_ref[...], None)
        cqn = (xh * qg_ref[...]).astype(BF16)
        cqn_ref[...] = cqn
        qraw = _nn(cqn, wq_ref[...])
        qs = []
        for h in range(MLA_H):
            hb = qraw[:, BLK * h:BLK * (h + 1)]
            qs.append(hb * cs + _swap_rope(hb) * sn)
        q_ref[...] = jnp.concatenate(qs, axis=1).astype(BF16)
        _, kh = _rms(ckv_ref[...], None)
        ckvn = (kh * kvg_ref[...]).astype(BF16)
        ckvn_ref[...] = ckvn
        kraw = _nn(ckvn, wk_ref[...])
        v_ref[...] = _nn(ckvn, wv_ref[...]).astype(BF16)
        lane = lax.broadcasted_iota(jnp.int32, (1, BLK), 1)
        kr = sm_ref[...]
        krr = jnp.where((lane >= SM_KR) & (lane < SM_KR + MLA_ROPE), kr * cs + _swap_rope(kr) * sn, 0.0)
        k_ref[...] = jnp.concatenate([kraw[:, BLK * h:BLK * (h + 1)] + krr for h in range(MLA_H)], axis=1).astype(BF16)

    def rows(w, cb):
        return pl.BlockSpec((tm, w), lambda i: (i, cb))

    def whole(a):
        return pl.BlockSpec(a.shape, lambda i: (0, 0))

    return pl.pallas_call(
        body, name="mla_pre", grid=(t // tm,),
        in_specs=[rows(MLA_QL, C_CQ // MLA_QL), rows(MLA_KVL, C_CKV // MLA_KVL), rows(BLK, C_SM // BLK),
                  whole(qg), whole(kvg), whole(wq), whole(wk), whole(wv), rows(BLK, 0), rows(BLK, 0)],
        out_specs=[rows(512, 0), rows(512, 0), rows(256, 0), rows(MLA_QL, 0), rows(MLA_KVL, 0)],
        out_shape=[jax.ShapeDtypeStruct((t, 512), BF16), jax.ShapeDtypeStruct((t, 512), BF16),
                   jax.ShapeDtypeStruct((t, 256), BF16), jax.ShapeDtypeStruct((t, MLA_QL), BF16),
                   jax.ShapeDtypeStruct((t, MLA_KVL), BF16)],
        compiler_params=_cp("arbitrary"),
    )(proj, proj, proj, qg, kvg, wq, wk, wv, cosq, sinq)


def mla_pre_bwd(dq, dk, dv, proj, cqn, ckvn, qg, kvg, wq, wk, wv, cosq, sinq, dsm_in):
    t = proj.shape[0]
    tm = _row_tile(t)

    def body(dq_ref, dk_ref, dv_ref, cq_ref, ckv_ref, cqn_ref, ckvn_ref, qg_ref, kvg_ref, wq_ref, wk_ref, wv_ref,
             cos_ref, sin_ref, din_ref, dcq_ref, dckv_ref, dsm_ref, dwq_ref, dwk_ref, dwv_ref, dqg_ref, dkvg_ref):
        i = pl.program_id(0)

        @pl.when(i == 0)
        def _():
            for r in (dwq_ref, dwk_ref, dwv_ref, dqg_ref, dkvg_ref):
                r[...] = jnp.zeros_like(r)

        cs, sn = cos_ref[...], sin_ref[...]
        lane = lax.broadcasted_iota(jnp.int32, (1, BLK), 1)

        def unrope(dy):
            return dy * cs + _swap_rope(dy * sn)

        dqp = jnp.concatenate([unrope(dq_ref[:, BLK * h:BLK * (h + 1)]) for h in range(MLA_H)], axis=1).astype(BF16)
        dwq_ref[...] += _tn(cqn_ref[...], dqp)
        dcqn = _nt(dqp, wq_ref[...])
        r, xh = _rms(cq_ref[...], None)
        dqg_ref[...] += jnp.sum(dcqn * xh, axis=0, keepdims=True)
        dxh = dcqn * qg_ref[...]
        dcq_ref[...] = r * (dxh - xh * jnp.mean(dxh * xh, axis=1, keepdims=True))

        dkn, dkr = [], jnp.zeros((tm, BLK), F32)
        for h in range(MLA_H):
            blk = dk_ref[:, BLK * h:BLK * (h + 1)]
            dkn.append(jnp.where(lane < MLA_NOPE, blk, 0.0))
            dkr += jnp.where((lane >= SM_KR) & (lane < SM_KR + MLA_ROPE), blk, 0.0)
        dknb = jnp.concatenate(dkn, axis=1).astype(BF16)
        dvb = dv_ref[...].astype(BF16)
        ckvn = ckvn_ref[...]
        dwk_ref[...] += _tn(ckvn, dknb)
        dwv_ref[...] += _tn(ckvn, dvb)
        dckvn = _nt(dknb, wk_ref[...]) + _nt(dvb, wv_ref[...])
        r2, kh = _rms(ckv_ref[...], None)
        dkvg_ref[...] += jnp.sum(dckvn * kh, axis=0, keepdims=True)
        dkh = dckvn * kvg_ref[...]
        dckv_ref[...] = r2 * (dkh - kh * jnp.mean(dkh * kh, axis=1, keepdims=True))
        dsm_ref[...] = din_ref[...] + jnp.where((lane >= SM_KR) & (lane < SM_KR + MLA_ROPE), unrope(dkr), 0.0)

    def rows(w, cb):
        return pl.BlockSpec((tm, w), lambda i: (i, cb))

    def whole(a):
        return pl.BlockSpec(a.shape, lambda i: (0, 0))

    def wshape(a):
        return jax.ShapeDtypeStruct(a.shape, F32)

    return pl.pallas_call(
        body, name="mla_pre_bwd", grid=(t // tm,),
        in_specs=[rows(512, 0), rows(512, 0), rows(256, 0), rows(MLA_QL, C_CQ // MLA_QL), rows(MLA_KVL, C_CKV // MLA_KVL),
                  rows(MLA_QL, 0), rows(MLA_KVL, 0), whole(qg), whole(kvg), whole(wq), whole(wk), whole(wv),
                  rows(BLK, 0), rows(BLK, 0), rows(BLK, 0)],
        out_specs=[rows(MLA_QL, 0), rows(MLA_KVL, 0), rows(BLK, 0), whole(wq), whole(wk), whole(wv), whole(qg), whole(kvg)],
        out_shape=[jax.ShapeDtypeStruct((t, MLA_QL), F32), jax.ShapeDtypeStruct((t, MLA_KVL), F32),
                   jax.ShapeDtypeStruct((t, BLK), F32), wshape(wq), wshape(wk), wshape(wv), wshape(qg), wshape(kvg)],
        compiler_params=_cp("arbitrary"),
    )(dq, dk, dv, proj, proj, cqn, ckvn, qg, kvg, wq, wk, wv, cosq, sinq, dsm_in)


def _slot_sum(me, own, recv_ref):
    gg = own.astype(F32)
    for s in range(N_DEV):
        gg = gg + jnp.where(me == s, 0.0, recv_ref[s].astype(F32))
    return gg


def adamw(w, m, v, g=None, recv=None, own=None, me_arr=None):
    shape = w.shape
    c = shape[-1]
    from_recv = recv is not None
    if not from_recv:
        me_arr = jnp.zeros((1,), jnp.int32)
    nl = len(recv) if from_recv else 1
    rws = w.size // c // nl
    tr = rws
    for d in (1024, 512, 352, 256, 128, 64, 32, 16, 8):
        if rws % d == 0 and d * c * 4 <= (2 << 20):
            tr = d
            break
    nt = rws // tr
    w2, m2, v2 = (a.reshape(nl, rws, c) for a in (w, m, v))
    if from_recv:
        gin = [a.reshape(N_DEV, rws, c) for a in list(recv) + list(own)]
    else:
        gin = [g.reshape(1, rws, c)]

    def body(me_ref, w_ref, m_ref, v_ref, *rest):
        g_refs, outs = rest[:len(gin)], rest[len(gin):]
        if from_recv:
            g_out, outs = outs[0], outs[1:]
            for li in range(nl):
                @pl.when(pl.program_id(0) == li)
                def _(li=li):
                    g_out[...] = _slot_sum(me_ref[0], g_refs[nl + li][...], g_refs[li])
            gg = g_out[...]
        else:
            gg = g_refs[0][...]
        d_ref, nm_ref, nv_ref = outs
        nm = B1 * m_ref[...] + (1.0 - B1) * gg
        nv = B2 * v_ref[...] + (1.0 - B2) * (gg * gg)
        mh = nm / (1.0 - B1 ** STEP)
        vh = nv / (1.0 - B2 ** STEP)
        d_ref[...] = -LR * (mh / (jnp.sqrt(vh) + AEPS) + WD * w_ref[...])
        nm_ref[...] = nm
        nv_ref[...] = nv

    row = pl.BlockSpec((None, tr, c), lambda l, i, me: (l, i, 0))
    if from_recv:
        gspecs = [pl.BlockSpec((N_DEV, tr, c), lambda l, i, me, li=li: (0, jnp.where(l == li, i, 0), 0))
                  for li in range(nl)]
        gspecs += [pl.BlockSpec((None, tr, c), lambda l, i, me, li=li: (me[0], jnp.where(l == li, i, 0), 0))
                   for li in range(nl)]
    else:
        gspecs = [row]
    nout = 4 if from_recv else 3
    outs = pl.pallas_call(
        body, name="adamw",
        grid_spec=pltpu.PrefetchScalarGridSpec(num_scalar_prefetch=1, grid=(nl, nt), in_specs=[row, row, row] + gspecs,
                                               out_specs=[row] * nout),
        out_shape=[jax.ShapeDtypeStruct((nl, rws, c), F32)] * nout,
        compiler_params=_cp("arbitrary", "arbitrary"),
    )(me_arr, w2, m2, v2, *gin)
    return tuple(o.reshape(shape) for o in outs)


def sum_slots(recv, own=None, me_arr=None):
    _, r, c = recv.shape
    if own is None:
        own, me_arr = recv, jnp.zeros((1,), jnp.int32)
        plain = True
    else:
        plain = False

    def body(me_ref, r_ref, own_ref, o_ref):
        if plain:
            gg = r_ref[0].astype(F32)
            for s in range(1, N_DEV):
                gg = gg + r_ref[s].astype(F32)
            o_ref[...] = gg
        else:
            o_ref[...] = _slot_sum(me_ref[0], own_ref[...], r_ref)

    return pl.pallas_call(
        body, name="sum_slots",
        grid_spec=pltpu.PrefetchScalarGridSpec(
            num_scalar_prefetch=1, grid=(1,),
            in_specs=[pl.BlockSpec((N_DEV, r, c), lambda i, me: (0, 0, 0)),
                      pl.BlockSpec((None, r, c), lambda i, me: (me[0], 0, 0))],
            out_specs=pl.BlockSpec((r, c), lambda i, me: (0, 0))),
        out_shape=jax.ShapeDtypeStruct((r, c), F32),
        compiler_params=_cp("arbitrary"),
    )(me_arr, recv, own)


_FLIPS = [(0, 0, 1), (0, 1, 0), (0, 1, 1), (1, 0, 0), (1, 0, 1), (1, 1, 0), (1, 1, 1)]
_ANY = pl.BlockSpec(memory_space=pl.ANY)


def _mesh_place():
    x, y, c = lax.axis_index("x"), lax.axis_index("y"), lax.axis_index("c")
    me = 4 * x + 2 * y + c
    peers = [((x + fx) % 2, (y + fy) % 2, (c + fc) % 2) for fx, fy, fc in _FLIPS]
    return me, peers


def place_own(src, l, dtype, me_arr):
    _, r, c = src.shape
    tr = r
    for d in (512, 352, 256, 128, 64, 32, 16, 8):
        if r % d == 0 and d * c * 4 <= (2 << 20):
            tr = d
            break

    def body(me_ref, s_ref, o_ref):
        o_ref[...] = s_ref[...].astype(dtype)

    return pl.pallas_call(
        body, name="place_own",
        grid_spec=pltpu.PrefetchScalarGridSpec(
            num_scalar_prefetch=1, grid=(r // tr,),
            in_specs=[pl.BlockSpec((None, tr, c), lambda i, me: (l, i, 0))],
            out_specs=pl.BlockSpec((None, tr, c), lambda i, me: (me[0], i, 0))),
        out_shape=jax.ShapeDtypeStruct((N_DEV, r, c), dtype),
        compiler_params=_cp("arbitrary"),
    )(me_arr, src)


_HBM = pl.BlockSpec(memory_space=pltpu.HBM)
_SEMS = pl.BlockSpec(memory_space=pltpu.SEMAPHORE)
_EFFECT = pltpu.SideEffectType.DATAFLOW_SIDE_EFFECTING


def exchange_start(mode, arrays, name):
    n = len(arrays)
    gather = mode == "gather"
    ns = 0 if gather else n
    zones = list(arrays) if gather else [lax.empty(a.shape, a.dtype) for a in arrays]
    ops = ([] if gather else list(arrays)) + zones

    def body(*refs):
        srcs, lands = refs[:ns], refs[ns:ns + n]
        send_sems, recv_sems = refs[ns + n], refs[ns + n + 1]
        token = refs[-1]
        me, peers = _mesh_place()
        ids = [4 * p[0] + 2 * p[1] + p[2] for p in peers]
        for j in range(n):
            for k in range(N_DEV - 1):
                src = lands[j].at[me] if gather else srcs[j].at[ids[k]]
                pltpu.make_async_remote_copy(src_ref=src, dst_ref=lands[j].at[me],
                                             send_sem=send_sems.at[j * (N_DEV - 1) + k],
                                             recv_sem=recv_sems.at[j * (N_DEV - 1) + k], device_id=peers[k],
                                             device_id_type=pl.DeviceIdType.MESH).start()
        token[...] = jnp.zeros_like(token)

    nsem = n * (N_DEV - 1)
    res = pl.pallas_call(
        body, name=name,
        in_specs=[_HBM] * (ns + n),
        out_specs=(_SEMS, _SEMS, *[_HBM] * (ns + n), pl.BlockSpec(memory_space=pltpu.VMEM)),
        out_shape=(pltpu.SemaphoreType.DMA((nsem,)), pltpu.SemaphoreType.DMA((nsem,)),
                   *[pltpu.HBM(a.shape, a.dtype) for a in ops], jax.ShapeDtypeStruct((8, BLK), F32)),
        input_output_aliases={i: 2 + i for i in range(ns + n)},
        compiler_params=pltpu.CompilerParams(has_side_effects=_EFFECT),
    )(*[pltpu.with_memory_space_constraint(a, pltpu.HBM) for a in ops])
    return dict(gather=gather, send=res[0], recv=res[1], srcs=list(res[2:2 + ns]), lands=list(res[2 + ns:2 + ns + n]),
                token=res[-1])


def exchange_wait(hd, idxs, name, after):
    gather = hd["gather"]
    n = len(idxs)
    ns = 0 if gather else n
    ops = ([] if gather else [hd["srcs"][j] for j in idxs]) + [hd["lands"][j] for j in idxs]

    def body(*refs):
        srcs, lands = refs[:ns], refs[ns:ns + n]
        send_sems, recv_sems = refs[ns + n], refs[ns + n + 1]
        me, peers = _mesh_place()
        ids = [4 * p[0] + 2 * p[1] + p[2] for p in peers]
        for p, j in enumerate(idxs):
            for k in range(N_DEV - 1):
                src = lands[p].at[me] if gather else srcs[p].at[ids[k]]
                cp = pltpu.make_async_remote_copy(src_ref=src, dst_ref=lands[p].at[ids[k]],
                                                  send_sem=send_sems.at[j * (N_DEV - 1) + k],
                                                  recv_sem=recv_sems.at[j * (N_DEV - 1) + k], device_id=peers[k],
                                                  device_id_type=pl.DeviceIdType.MESH)
                cp.wait_send()
                cp.wait_recv()

    res = pl.pallas_call(
        body, name=name,
        in_specs=[_HBM] * (ns + n) + [_SEMS, _SEMS, _ANY],
        out_specs=[_HBM] * (ns + n),
        out_shape=[pltpu.HBM(a.shape, a.dtype) for a in ops],
        input_output_aliases={i: i for i in range(ns + n)},
        compiler_params=pltpu.CompilerParams(has_side_effects=_EFFECT),
    )(*ops, hd["send"], hd["recv"], after)
    return list(res[:ns]), list(res[ns:])


def _pad_cols(a, n):
    return jnp.pad(a, ((0, 0),) * (a.ndim - 1) + ((0, n - a.shape[-1]),))


def w_in_to_padded(w):
    z = lambda n: jnp.zeros(w.shape[:-1] + (n,), w.dtype)
    return jnp.concatenate([
        w[..., 0:1280], w[..., 1288:2056], w[..., 2060:2316], w[..., 2316:2444],
        w[..., 1280:1288], w[..., 2056:2060], z(SM_KR - SM_F - FOX_H), w[..., 2444:2476], z(BLK - SM_KR - MLA_ROPE)], axis=-1)


def w_in_from_padded(g):
    s = C_SM
    return jnp.concatenate([
        g[..., 0:1280], g[..., s + SM_DT:s + SM_DT + 8], g[..., 1280:2048], g[..., s + SM_F:s + SM_F + 4],
        g[..., 2048:2304], g[..., 2304:2432], g[..., s + SM_KR:s + SM_KR + MLA_ROPE]], axis=-1)


def _unshard_cols(gth):
    n, r, c = gth.shape
    return jnp.transpose(gth, (1, 0, 2)).reshape(r, n * c)


def _shard_cols(full):
    r, nc = full.shape
    return jnp.transpose(full.reshape(r, N_DEV, nc // N_DEV), (1, 0, 2))


def mla_weights(uq_g, ukv_g):
    uq = _unshard_cols(uq_g)
    dqh = MLA_NOPE + MLA_ROPE
    wq = jnp.concatenate([_pad_cols(uq[:, dqh * h:dqh * (h + 1)], BLK) for h in range(MLA_H)], axis=1)
    wk = jnp.concatenate([_pad_cols(ukv_g[2 * h], BLK) for h in range(MLA_H)], axis=1)
    wv = jnp.concatenate([ukv_g[2 * h + 1] for h in range(MLA_H)], axis=1)
    return wq, wk, wv


def mla_weight_grads(dwq, dwk, dwv):
    dqh = MLA_NOPE + MLA_ROPE
    duq = _shard_cols(jnp.concatenate([dwq[:, BLK * h:BLK * h + dqh] for h in range(MLA_H)], axis=1))
    parts = []
    for h in range(MLA_H):
        parts += [dwk[:, BLK * h:BLK * h + MLA_NOPE], dwv[:, MLA_V * h:MLA_V * (h + 1)]]
    return duq, jnp.stack(parts, axis=0)


def rope_tables(t):
    pos = (jnp.arange(t, dtype=jnp.int32) - PAD).astype(F32)
    inv_freq = 1.0 / (10000.0 ** (jnp.arange(0, MLA_ROPE, 2, dtype=F32) / MLA_ROPE))
    ang = pos[:, None] * inv_freq[None, :]
    cos, sin = jnp.cos(ang), jnp.sin(ang)
    one, zero = jnp.ones((t, SM_KR), F32), jnp.zeros((t, SM_KR), F32)
    tail = BLK - SM_KR - MLA_ROPE
    cosq = jnp.concatenate([one, cos, cos, jnp.ones((t, tail), F32)], axis=1)
    sinq = jnp.concatenate([zero, -sin, sin, jnp.zeros((t, tail), F32)], axis=1)
    return cosq, sinq


def _lanes(v, off=0):
    return jnp.pad(v.astype(F32), (off, BLK - off - v.shape[0]))[None, :]


def layer_fwd(h, hb, getw, tabs):
    sv = {"h0": h, "h0b": hb}
    W = dict(getw("ffn1", hb))
    u, v, r1, h1, h1b = ffn_fwd(hb, h, W["g1"], W["u1"], W["d1"], W["ln1_g"], W["ln1_b"])
    sv.update(u1=u, v1=v, r1=r1, h1=h1, h1b=h1b)
    W.update(getw("mix", h1b))
    proj = mm_nn(h1b, W["w_in"])
    xa = conv_fwd(proj, W["conv_w"], W["conv_b"])
    y_ssd, sprev = ssd_fwd(xa, proj, W["dtb"], W["alog"], W["dskip"], W["normg"])
    c_col, c_row = fox_pre(proj, W["fb"])
    y_fox, lse_f = attn_fwd(proj, proj, proj, C_FQ // 256, C_FK // 256, C_FV // 256, FOX_H, FOX_DH, FOX_DH,
                            FOX_DH ** -0.5, c_col, c_row, SM_F)
    q, k, vv, cqn, ckvn = mla_pre(proj, W["qg"], W["kvg"], W["wq"], W["wk"], W["wv"], *tabs)
    y_mla, lse_m = attn_fwd(q, k, vv, 0, 0, 0, MLA_H, BLK, MLA_V, (MLA_NOPE + MLA_ROPE) ** -0.5)
    mixcat = jnp.concatenate([y_ssd, y_fox, y_mla], axis=1)
    r2, h2, h2b = mm_res_ln(mixcat[None], W["w_out"][None], h1, ALPHA, 1.0, W["ln2_g"], W["ln2_b"])
    sv.update(proj=proj, xa=xa, sprev=sprev, c_col=c_col, c_row=c_row, lse_f=lse_f, q=q, k=k, v=vv, cqn=cqn, ckvn=ckvn,
              lse_m=lse_m, mixcat=mixcat, r2=r2, h2=h2, h2b=h2b)
    W.update(getw("ffn2", h2b))
    u, v, r3, h3, h3b = ffn_fwd(h2b, h2, W["g2"], W["u2"], W["d2"], W["ln3_g"], W["ln3_b"])
    sv.update(u2=u, v2=v, r3=r3, W=W)
    return h3, h3b, sv


def ffn_bwd(parts, r, gamma, hb_in, u, v, wg, wu, wd, after=None):
    dr, dfb, dg, db = ln_bwd(parts, r, gamma, 0.5, after)
    du, dv, dh = ffn_bwd_act(dfb, u, v, wg, wu, wd)
    dwg, dwu, dwd = ffn_bwd_w(hb_in, dfb, u, v, du, dv)
    return dr, dh, dict(d=dwd, g=dwg, u=dwu, ln_g=dg, ln_b=db)


def layer_bwd(parts, sv, emit, tabs, after):
    G = {}
    W = sv["W"]
    dr3, dh2f, g2 = ffn_bwd(parts, sv["r3"], W["ln3_g"], sv["h2b"], sv["u2"], sv["v2"], W["g2"], W["u2"], W["d2"], after)
    G.update(g2=g2["g"], u2=g2["u"], d2=g2["d"], ln3_g=g2["ln_g"], ln3_b=g2["ln_b"])
    tok = emit("ffn2", G)
    dr2, dmixb, G["ln2_g"], G["ln2_b"] = ln_bwd([(dr3, ALPHA), (dh2f, 1.0)], sv["r2"], W["ln2_g"], 1.0, tok)
    dmc = mm_nt_reduce([(dmixb[None], W["w_out"][None])], D)
    G["w_out"] = mm_tn(sv["mixcat"][None], dmixb[None])[0]
    proj = sv["proj"]
    dxa, dz, dsm, G["normg"], G["dskip"], G["alog"], G["dtb"] = ssd_bwd(
        dmc, sv["xa"], proj, sv["sprev"], W["dtb"], W["alog"], W["dskip"], W["normg"])
    dxbc, G["conv_w"], G["conv_b"] = conv_bwd(dxa, proj, W["conv_w"], W["conv_b"])
    dfq, dfk, dfv, dcq, dck = attn_bwd(proj, proj, proj, dmc, sv["lse_f"], C_FQ // 256, C_FK // 256, C_FV // 256, 2,
                                       FOX_H, FOX_DH, FOX_DH, FOX_DH ** -0.5, sv["c_col"], sv["c_row"], SM_F)
    dsm, G["fb"] = fox_pre_bwd(dcq, dck, proj, W["fb"], dsm)
    dq, dk, dv = attn_bwd(sv["q"], sv["k"], sv["v"], dmc, sv["lse_m"], 0, 0, 0, 3, MLA_H, BLK, MLA_V,
                          (MLA_NOPE + MLA_ROPE) ** -0.5)
    dcql, dckv, dsm, G["wq"], G["wk"], G["wv"], G["qg"], G["kvg"] = mla_pre_bwd(
        dq, dk, dv, proj, sv["cqn"], sv["ckvn"], W["qg"], W["kvg"], W["wq"], W["wk"], W["wv"], *tabs, dsm)
    dproj = jnp.concatenate([dz, dxbc, dfq, dfk, dfv, dcql, dckv, dsm], axis=1).astype(BF16)
    dh1p = mm_nt_reduce([(dproj[None], W["w_in"][None])], D)
    G["w_in"] = mm_tn(sv["h1b"][None], dproj[None])[0]
    tok = emit("mix", G)
    dr1, dh0f, g1 = ffn_bwd([(dr2, ALPHA), (dh1p, 1.0)], sv["r1"], W["ln1_g"], sv["h0b"], sv["u1"], sv["v1"],
                            W["g1"], W["u1"], W["d1"], tok)
    G.update(g1=g1["g"], u1=g1["u"], d1=g1["d"], ln1_g=g1["ln_g"], ln1_b=g1["ln_b"])
    tok = emit("ffn1", G)
    return [(dr1, ALPHA), (dh0f, 1.0)], G, tok


def local_step(x, target, meta_full, getw, emit):
    t = x.shape[0] + BLK
    tabs = rope_tables(t)
    h, hb = build_h0(meta_full, x)
    saved = []
    for l in range(NL):
        h, hb, sv = layer_fwd(h, hb, functools.partial(getw, l), tabs)
        saved.append(sv)
    dy, loss = loss_head(h, target)
    parts = [(dy, 1.0)]
    grads = [None] * NL
    tok = None
    for l in range(NL - 1, -1, -1):
        parts, grads[l], tok = layer_bwd(parts, saved[l], functools.partial(emit, l), tabs, tok)
    gx, gmeta = final_add(parts[0][0], parts[1][0], tok)
    return loss, gx, gmeta, grads


_SMALL = ["ln1_g", "ln1_b", "ln2_g", "ln2_b", "ln3_g", "ln3_b", "conv_b", "ssd_norm_g", "mla_q_norm_g",
          "mla_kv_norm_g", "dt_bias", "a_log", "d_skip", "fox_f_b"]
_SMALL_ROWS = 16
_BIG = ["ffn1_w_gate", "ffn1_w_up", "ffn1_w_down", "w_in", "conv_w", "mla_w_uq", "mla_w_ukv", "w_out",
        "ffn2_w_gate", "ffn2_w_up", "ffn2_w_down"]
_NAMES = ["meta", "ffn1_w_gate", "ffn1_w_up", "ffn1_w_down", "ln1_g", "ln1_b", "w_in", "conv_w", "conv_b", "dt_bias",
          "a_log", "d_skip", "ssd_norm_g", "fox_f_b", "mla_q_norm_g", "mla_w_uq", "mla_kv_norm_g", "mla_w_ukv", "w_out",
          "ln2_g", "ln2_b", "ffn2_w_gate", "ffn2_w_up", "ffn2_w_down", "ln3_g", "ln3_b"]


def pack_small(p):
    rows = []
    for l in range(NL):
        for n in _SMALL:
            rows.append(_pad_cols(p[n][l][None, :].astype(F32), D))
        rows.append(jnp.zeros((_SMALL_ROWS - len(_SMALL), D), F32))
    return jnp.concatenate(rows, axis=0)


def unpack_small(a, like):
    out = {}
    for i, n in enumerate(_SMALL):
        out[n] = jnp.stack([a[l * _SMALL_ROWS + i, :like[n].shape[1]] for l in range(NL)], axis=0)
    return out


_STAGES = {"ffn1": ["ffn1_w_gate", "ffn1_w_up", "ffn1_w_down"],
           "mix": ["w_in", "conv_w", "mla_w_uq", "mla_w_ukv", "w_out"],
           "ffn2": ["ffn2_w_gate", "ffn2_w_up", "ffn2_w_down"]}


_FFN_T = ("ffn1_w_gate", "ffn1_w_up", "ffn2_w_gate", "ffn2_w_up")


def stage_weights(l, stage, g, rep):
    if stage != "mix":
        i = stage[3]
        return {"g" + i: g[f"ffn{i}_w_gate"].reshape(D_FF, D), "u" + i: g[f"ffn{i}_w_up"].reshape(D_FF, D),
                "d" + i: g[f"ffn{i}_w_down"].reshape(D_FF, D),
                "ln1_g" if i == "1" else "ln3_g": rep["ln1_g" if i == "1" else "ln3_g"][l][None, :],
                "ln1_b" if i == "1" else "ln3_b": rep["ln1_b" if i == "1" else "ln3_b"][l][None, :]}
    W = {}
    W["w_in"] = g["w_in"].reshape(D, N_INP)
    W["w_out"] = g["w_out"].reshape(D, D)
    W["wq"], W["wk"], W["wv"] = mla_weights(g["mla_w_uq"], g["mla_w_ukv"])
    W["conv_w"] = _unshard_cols(g["conv_w"])
    for k in ("ln2_g", "ln2_b", "conv_b"):
        W[k] = rep[k][l][None, :]
    W["normg"] = rep["ssd_norm_g"][l][None, :]
    W["qg"] = rep["mla_q_norm_g"][l][None, :]
    W["kvg"] = rep["mla_kv_norm_g"][l][None, :]
    W["dtb"] = _lanes(rep["dt_bias"][l], SM_DT)
    W["alog"] = _lanes(rep["a_log"][l], SM_DT)
    W["dskip"] = _lanes(rep["d_skip"][l], SM_DT)
    W["fb"] = _lanes(rep["fox_f_b"][l], SM_F)
    return W


def small_grads(G):
    return {"ln1_g": G["ln1_g"][0], "ln1_b": G["ln1_b"][0], "ln2_g": G["ln2_g"][0], "ln2_b": G["ln2_b"][0],
            "ln3_g": G["ln3_g"][0], "ln3_b": G["ln3_b"][0], "conv_b": G["conv_b"][0], "ssd_norm_g": G["normg"][0],
            "mla_q_norm_g": G["qg"][0], "mla_kv_norm_g": G["kvg"][0], "dt_bias": G["dtb"][0, :SSD_H],
            "a_log": G["alog"][0, :SSD_H], "d_skip": G["dskip"][0, :SSD_H], "fox_f_b": G["fb"][0, SM_F:SM_F + FOX_H]}


def big_grads(G, stage):
    if stage != "mix":
        i = stage[-1]
        return {f"ffn{i}_w_{k}": G[k[0] + i].reshape(N_DEV, HS, D) for k in ("gate", "up", "down")}
    duq, dukv = mla_weight_grads(G["wq"], G["wk"], G["wv"])
    return {"w_in": G["w_in"].reshape(N_DEV, D // N_DEV, N_INP), "w_out": G["w_out"].reshape(N_DEV, D // N_DEV, D),
            "mla_w_uq": duq, "mla_w_ukv": dukv, "conv_w": _shard_cols(G["conv_w"])}


def kernel(x, meta, ffn1_w_gate, ffn1_w_up, ffn1_w_down, ln1_g, ln1_b, w_in, conv_w, conv_b, dt_bias, a_log, d_skip, ssd_norm_g, fox_f_b, mla_q_norm_g, mla_w_uq, mla_kv_norm_g, mla_w_ukv, w_out, ln2_g, ln2_b, ffn2_w_gate, ffn2_w_up, ffn2_w_down, ln3_g, ln3_b, loss_target, m_meta, m_ffn1_w_gate, m_ffn1_w_up, m_ffn1_w_down, m_ln1_g, m_ln1_b, m_w_in, m_conv_w, m_conv_b, m_dt_bias, m_a_log, m_d_skip, m_ssd_norm_g, m_fox_f_b, m_mla_q_norm_g, m_mla_w_uq, m_mla_kv_norm_g, m_mla_w_ukv, m_w_out, m_ln2_g, m_ln2_b, m_ffn2_w_gate, m_ffn2_w_up, m_ffn2_w_down, m_ln3_g, m_ln3_b, v_meta, v_ffn1_w_gate, v_ffn1_w_up, v_ffn1_w_down, v_ln1_g, v_ln1_b, v_w_in, v_conv_w, v_conv_b, v_dt_bias, v_a_log, v_d_skip, v_ssd_norm_g, v_fox_f_b, v_mla_q_norm_g, v_mla_w_uq, v_mla_kv_norm_g, v_mla_w_ukv, v_w_out, v_ln2_g, v_ln2_b, v_ffn2_w_gate, v_ffn2_w_up, v_ffn2_w_down, v_ln3_g, v_ln3_b):
    vals = (meta, ffn1_w_gate, ffn1_w_up, ffn1_w_down, ln1_g, ln1_b, w_in, conv_w, conv_b, dt_bias, a_log, d_skip, ssd_norm_g, fox_f_b, mla_q_norm_g, mla_w_uq, mla_kv_norm_g, mla_w_ukv, w_out, ln2_g, ln2_b, ffn2_w_gate, ffn2_w_up, ffn2_w_down, ln3_g, ln3_b)
    moms = (m_meta, m_ffn1_w_gate, m_ffn1_w_up, m_ffn1_w_down, m_ln1_g, m_ln1_b, m_w_in, m_conv_w, m_conv_b, m_dt_bias, m_a_log, m_d_skip, m_ssd_norm_g, m_fox_f_b, m_mla_q_norm_g, m_mla_w_uq, m_mla_kv_norm_g, m_mla_w_ukv, m_w_out, m_ln2_g, m_ln2_b, m_ffn2_w_gate, m_ffn2_w_up, m_ffn2_w_down, m_ln3_g, m_ln3_b)
    vars_ = (v_meta, v_ffn1_w_gate, v_ffn1_w_up, v_ffn1_w_down, v_ln1_g, v_ln1_b, v_w_in, v_conv_w, v_conv_b, v_dt_bias, v_a_log, v_d_skip, v_ssd_norm_g, v_fox_f_b, v_mla_q_norm_g, v_mla_w_uq, v_mla_kv_norm_g, v_mla_w_ukv, v_w_out, v_ln2_g, v_ln2_b, v_ffn2_w_gate, v_ffn2_w_up, v_ffn2_w_down, v_ln3_g, v_ln3_b)
    P = dict(zip(_NAMES, vals))
    M = dict(zip(_NAMES, moms))
    V = dict(zip(_NAMES, vars_))
    me = 4 * lax.axis_index("x") + 2 * lax.axis_index("y") + lax.axis_index("c")

    me_arr = me.astype(jnp.int32).reshape(1)
    for n in _FFN_T:
        P[n], M[n], V[n] = (jnp.swapaxes(a[n], 1, 2) for a in (P, M, V))
    src = dict(P)
    src["w_in"] = w_in_to_padded(P["w_in"])
    order = [("meta", 0)] + [(n, l) for l in range(NL) for names in _STAGES.values() for n in names]
    zone_of = {nl_: i for i, nl_ in enumerate(order)}
    zones = [place_own(P["meta"][None], 0, F32, me_arr)]
    zones += [place_own(src[n], l, F32 if n == "conv_w" else BF16, me_arr) for n, l in order[1:]]
    hg = exchange_start("gather", zones, "gather_start")
    meta_full = _unshard_cols(exchange_wait(hg, [0], "gather_wait_meta", hg["token"])[1][0])

    def getw(l, stage, after):
        names = _STAGES[stage]
        lands = exchange_wait(hg, [zone_of[(n, l)] for n in names], f"gather_wait_{l}_{stage}", after)[1]
        return stage_weights(l, stage, dict(zip(names, lands)), P)

    sent = {}

    def emit(l, stage, G):
        bg = big_grads(G, stage)
        sent[(l, stage)] = exchange_start("scatter", [bg[n] for n in _STAGES[stage]], f"scatter_start_{l}_{stage}")
        return sent[(l, stage)]["token"]

    loss, gx, gmeta, grads = local_step(x[0], loss_target[0], meta_full, getw, emit)

    small = jnp.concatenate([pack_small({n: jnp.stack([small_grads(g)[n] for g in grads]) for n in _SMALL}), gmeta], axis=0)
    hs = exchange_start("gather", [place_own(small[None], 0, F32, me_arr)], "small_start")

    out = {}
    after = hs["token"]
    for stage in ("ffn2", "mix", "ffn1"):
        names = _STAGES[stage]
        got = [exchange_wait(sent[(l, stage)], list(range(len(names))), f"scatter_wait_{l}_{stage}", after)
               for l in range(NL - 1, -1, -1)][::-1]
        for i, n in enumerate(names):
            own = [got[l][0][i] for l in range(NL)]
            recv = [got[l][1][i] for l in range(NL)]
            if n == "w_in":
                g = jnp.stack([w_in_from_padded(sum_slots(recv[l], own[l], me_arr)) for l in range(NL)])
                out[n] = (g,) + adamw(P[n], M[n], V[n], g=g)
            else:
                out[n] = adamw(P[n], M[n], V[n], recv=recv, own=own, me_arr=me_arr)
                if n in _FFN_T:
                    out[n] = tuple(jnp.swapaxes(a, 1, 2) for a in out[n])
        after = out[names[-1]][1]
    gsmall = sum_slots(exchange_wait(hs, [0], "small_wait", after)[1][0])
    gm = lax.dynamic_slice(gsmall[NL * _SMALL_ROWS:], (0, me * (D // N_DEV)), (N_META, D // N_DEV))
    out["meta"] = (gm,) + adamw(P["meta"], M["meta"], V["meta"], g=gm)
    gs = gsmall[:NL * _SMALL_ROWS]
    sd, sm_, sv_ = adamw(pack_small(P), pack_small(M), pack_small(V), g=gs)
    ups = [unpack_small(a, P) for a in (gs, sd, sm_, sv_)]
    for n in _SMALL:
        out[n] = tuple(u[n] for u in ups)

    loss_all = lax.psum(loss[0, 0], ("x", "y", "c"))
    flat = [loss_all, gx[None]]
    for k in range(4):
        flat += [out[n][k] for n in _NAMES]
    return tuple(flat)
```

```python
import functools

import jax
import jax.numpy as jnp
from jax import lax
from jax.experimental import pallas as pl
from jax.experimental.pallas import tpu as pltpu

F32, BF16 = jnp.float32, jnp.bfloat16
HI = lax.Precision.HIGHEST

N_DEV = 8
D = 1024
NL = 2
N_META = 16
BLK = 128
PAD = BLK - N_META
D_FF = 2816
HS = D_FF // N_DEV
SSD_H, SSD_P, SSD_N, SSD_G = 8, 64, 64, 2
SSD_D = SSD_H * SSD_P
CONV_K = 4
CONV_D = SSD_D + 2 * SSD_G * SSD_N
FOX_H, FOX_DH = 4, 64
MLA_H, MLA_QL, MLA_KVL, MLA_NOPE, MLA_ROPE, MLA_V = 4, 256, 128, 64, 32, 64
N_IN = 2476
C_Z, C_XBC, C_FQ, C_FK, C_FV, C_CQ, C_CKV, C_SM, N_INP = 0, 512, 1280, 1536, 1792, 2048, 2304, 2432, 2560
SM_DT, SM_F, SM_KR = 0, 8, 64
ALPHA = (2 * NL) ** 0.25
EPS = 1e-5
NEG = -1e30
LR, B1, B2, AEPS, WD, STEP = 0.001, 0.9, 0.999, 1e-08, 0.01, 10
VMEM_MB = 56


def _cp(*sem):
    return pltpu.CompilerParams(dimension_semantics=sem, vmem_limit_bytes=VMEM_MB << 20)


def _nn(a, b):
    return lax.dot_general(a, b, (((1,), (0,)), ((), ())), preferred_element_type=F32)


def _nt(a, b):
    return lax.dot_general(a, b, (((1,), (1,)), ((), ())), preferred_element_type=F32)


def _tn(a, b):
    return lax.dot_general(a, b, (((0,), (0,)), ((), ())), preferred_element_type=F32)


def _nn_hi(a, b):
    return lax.dot_general(a, b, (((1,), (0,)), ((), ())), precision=HI, preferred_element_type=F32)


def _row_tile(t):
    for d in range(640, 15, -16):
        if t % d == 0:
            return d
    raise ValueError(t)


def _sig(x):
    return 1.0 / (1.0 + jnp.exp(-x))


def _tri(lower=True):
    r = lax.broadcasted_iota(jnp.int32, (BLK, BLK), 0)
    c = lax.broadcasted_iota(jnp.int32, (BLK, BLK), 1)
    return (r >= c) if lower else (r <= c)


def build_h0(meta_full, x):
    s = x.shape[0]
    nb = s // BLK + 1

    def body(m_ref, x_ref, h_ref, hb_ref):
        i = pl.program_id(0)

        @pl.when(i == 0)
        def _():
            h = jnp.concatenate([jnp.zeros((PAD, D), F32), m_ref[...]], axis=0)
            h_ref[...] = h
            hb_ref[...] = h.astype(BF16)

        @pl.when(i > 0)
        def _():
            h_ref[...] = x_ref[...]
            hb_ref[...] = x_ref[...].astype(BF16)

    return pl.pallas_call(
        body, name="build_h0", grid=(nb,),
        in_specs=[pl.BlockSpec((N_META, D), lambda i: (0, 0)),
                  pl.BlockSpec((BLK, D), lambda i: (jnp.maximum(i - 1, 0), 0))],
        out_specs=[pl.BlockSpec((BLK, D), lambda i: (i, 0))] * 2,
        out_shape=[jax.ShapeDtypeStruct((nb * BLK, D), F32), jax.ShapeDtypeStruct((nb * BLK, D), BF16)],
        compiler_params=_cp("arbitrary"),
    )(meta_full, x)


FT = 256


def _layer_norm(r, gamma, beta):
    mu = jnp.mean(r, axis=1, keepdims=True)
    xc = r - mu
    var = jnp.mean(xc * xc, axis=1, keepdims=True)
    return xc * lax.rsqrt(var + EPS) * gamma + beta


def ffn_fwd(hb, res, wg, wu, wd, gamma, beta):
    t = hb.shape[0]
    f = wg.shape[0]
    tm = _row_tile(t)
    nj = f // FT

    def body(h_ref, res_ref, wg_ref, wu_ref, wd_ref, g_ref, be_ref, u_ref, v_ref, r_ref, y_ref, yb_ref, acc, us, vs):
        j = pl.program_id(1)

        def up():
            h = h_ref[...]
            u = _nt(h, wg_ref[...])
            v = _nt(h, wu_ref[...])
            u_ref[...] = u.astype(BF16)
            v_ref[...] = v.astype(BF16)
            return u, v

        def down():
            u, v = us[...], vs[...]
            return _nn((u * _sig(u) * v).astype(BF16), wd_ref[...])

        @pl.when(j == 0)
        def _():
            us[...], vs[...] = up()
            acc[...] = jnp.zeros_like(acc)

        @pl.when((j > 0) & (j < nj))
        def _():
            d = down()
            u, v = up()
            acc[...] += d
            us[...] = u
            vs[...] = v

        @pl.when(j == nj)
        def _():
            r = ALPHA * res_ref[...] + 0.5 * (acc[...] + down())
            y = _layer_norm(r, g_ref[...], be_ref[...])
            r_ref[...] = r
            y_ref[...] = y
            yb_ref[...] = y.astype(BF16)

    row = pl.BlockSpec((tm, D), lambda i, j: (i, 0))
    vec = pl.BlockSpec((1, D), lambda i, j: (0, 0))
    wup = pl.BlockSpec((FT, D), lambda i, j: (jnp.minimum(j, nj - 1), 0))
    wdn = pl.BlockSpec((FT, D), lambda i, j: (jnp.maximum(j - 1, 0), 0))
    act = pl.BlockSpec((tm, FT), lambda i, j: (i, jnp.minimum(j, nj - 1)))
    return pl.pallas_call(
        body, name="ffn_fwd", grid=(t // tm, nj + 1),
        in_specs=[row, row, wup, wup, wdn, vec, vec],
        out_specs=[act, act, row, row, row],
        out_shape=[jax.ShapeDtypeStruct((t, f), BF16), jax.ShapeDtypeStruct((t, f), BF16),
                   jax.ShapeDtypeStruct((t, D), F32), jax.ShapeDtypeStruct((t, D), F32),
                   jax.ShapeDtypeStruct((t, D), BF16)],
        scratch_shapes=[pltpu.VMEM((tm, D), F32), pltpu.VMEM((tm, FT), F32), pltpu.VMEM((tm, FT), F32)],
        compiler_params=_cp("arbitrary", "arbitrary"),
    )(hb, res, wg, wu, wd, gamma, beta)


def ffn_bwd_act(dfb, u, v, wg, wu, wd):
    t, f = u.shape
    tm = _row_tile(t)

    nj = f // FT

    def body(df_ref, u_ref, v_ref, wg_ref, wu_ref, wd_ref, du_ref, dv_ref, dh_ref, das):
        j = pl.program_id(1)

        def first():
            return _nt(df_ref[...], wd_ref[...])

        def second():
            da = das[...]
            uu = u_ref[...].astype(F32)
            sg = _sig(uu)
            du = (da * v_ref[...].astype(F32) * (sg * (1.0 + uu * (1.0 - sg)))).astype(BF16)
            dv = (da * uu * sg).astype(BF16)
            du_ref[...] = du
            dv_ref[...] = dv
            return _nn(du, wg_ref[...]) + _nn(dv, wu_ref[...])

        @pl.when(j == 0)
        def _():
            das[...] = first()
            dh_ref[...] = jnp.zeros_like(dh_ref)

        @pl.when((j > 0) & (j < nj))
        def _():
            tot = second()
            da = first()
            dh_ref[...] += tot
            das[...] = da

        @pl.when(j == nj)
        def _():
            dh_ref[...] += second()

    row = pl.BlockSpec((tm, D), lambda i, j: (i, 0))
    wfirst = pl.BlockSpec((FT, D), lambda i, j: (jnp.minimum(j, nj - 1), 0))
    wsecond = pl.BlockSpec((FT, D), lambda i, j: (jnp.maximum(j - 1, 0), 0))
    act = pl.BlockSpec((tm, FT), lambda i, j: (i, jnp.maximum(j - 1, 0)))
    return pl.pallas_call(
        body, name="ffn_bwd_act", grid=(t // tm, nj + 1),
        in_specs=[row, act, act, wsecond, wsecond, wfirst],
        out_specs=[act, act, row],
        out_shape=[jax.ShapeDtypeStruct((t, f), BF16), jax.ShapeDtypeStruct((t, f), BF16),
                   jax.ShapeDtypeStruct((t, D), F32)],
        scratch_shapes=[pltpu.VMEM((tm, FT), F32)],
        compiler_params=_cp("arbitrary", "arbitrary"),
    )(dfb, u, v, wg, wu, wd)


def ffn_fwd_seq(x, ln_in, wg, wu, wd, ln_out):
    t = x.shape[0]
    f = wg.shape[0]
    nj, nr = f // FT, t // _row_tile(t)
    rc = t // nr
    plain = ln_in is None
    gi, bi = ln_out if plain else ln_in

    def body(x_hbm, gi_ref, bi_ref, go_ref, bo_ref, wg_ref, wu_ref, wd_ref, u_ref, v_ref, r_hbm, yb_hbm,
             acc, hbs, xbuf, sem_in, sem_out):
        j = pl.program_id(0)

        @pl.when(j == 0)
        def _():
            def fetch(k):
                return pltpu.make_async_copy(x_hbm.at[pl.ds(k * rc, rc)], xbuf.at[k % 2], sem_in.at[k % 2])

            fetch(0).start()
            for k in range(nr):
                if k + 1 < nr:
                    fetch(k + 1).start()
                fetch(k).wait()
                h = xbuf[k % 2]
                if not plain:
                    h = _layer_norm(h, gi_ref[...], bi_ref[...])
                acc[k * rc:(k + 1) * rc, :] = ALPHA * h
                hbs[k * rc:(k + 1) * rc, :] = h.astype(BF16)

        for k in range(nr):
            sl = slice(k * rc, (k + 1) * rc)
            h = hbs[sl, :]
            u = _nt(h, wg_ref[...])
            v = _nt(h, wu_ref[...])
            u_ref[sl, :] = u.astype(BF16)
            v_ref[sl, :] = v.astype(BF16)
            acc[sl, :] += _nn((0.5 * u * _sig(u) * v).astype(BF16), wd_ref[...])

        @pl.when(j == nj - 1)
        def _():
            r_cp = pltpu.make_async_copy(acc, r_hbm, sem_out.at[0])
            r_cp.start()
            for k in range(nr):
                sl = slice(k * rc, (k + 1) * rc)
                hbs[sl, :] = _layer_norm(acc[sl, :], go_ref[...], bo_ref[...]).astype(BF16)
            y_cp = pltpu.make_async_copy(hbs, yb_hbm, sem_out.at[1])
            y_cp.start()
            r_cp.wait()
            y_cp.wait()

    vec = pl.BlockSpec((1, D), lambda j: (0, 0))
    wsp = pl.BlockSpec((FT, D), lambda j: (j, 0))
    act = pl.BlockSpec((t, FT), lambda j: (0, j))
    return pl.pallas_call(
        body, name="ffn_fwd_seq", grid=(nj,),
        in_specs=[_ANY, vec, vec, vec, vec, wsp, wsp, wsp],
        out_specs=[act, act, _ANY, _ANY],
        out_shape=[jax.ShapeDtypeStruct((t, f), BF16), jax.ShapeDtypeStruct((t, f), BF16),
                   jax.ShapeDtypeStruct((t, D), F32), jax.ShapeDtypeStruct((t, D), BF16)],
        scratch_shapes=[pltpu.VMEM((t, D), F32), pltpu.VMEM((t, D), BF16), pltpu.VMEM((2, rc, D), F32),
                        pltpu.SemaphoreType.DMA((2,)), pltpu.SemaphoreType.DMA((2,))],
        compiler_params=_cp("arbitrary"),
    )(x, gi, bi, ln_out[0], ln_out[1], wg, wu, wd)


def ffn_bwd_seq(dfb, hb, u, v, wg, wu, wd):
    t, f = u.shape
    nj, nr = f // FT, t // _row_tile(t)
    rc = t // nr

    def body(df_hbm, hb_hbm, u_ref, v_ref, wg_ref, wu_ref, wd_ref, dh_hbm, dwg_ref, dwu_ref, dwd_ref,
             dfs, hbs, hbt, dft, dhacc, dus, dvs, acs, sems):
        j = pl.program_id(0)

        @pl.when(j == 0)
        def _():
            c1 = pltpu.make_async_copy(df_hbm, dfs, sems.at[0])
            c2 = pltpu.make_async_copy(hb_hbm, hbs, sems.at[1])
            c1.start()
            c2.start()
            dhacc[...] = jnp.zeros_like(dhacc)
            c1.wait()
            for c in range(t // BLK):
                dft[:, c * BLK:(c + 1) * BLK] = dfs[c * BLK:(c + 1) * BLK, :].T
            c2.wait()
            for c in range(t // BLK):
                hbt[:, c * BLK:(c + 1) * BLK] = hbs[c * BLK:(c + 1) * BLK, :].T

        for k in range(nr):
            sl = slice(k * rc, (k + 1) * rc)
            da = _nt(dfs[sl, :], wd_ref[...])
            uu = u_ref[sl, :].astype(F32)
            vv = v_ref[sl, :].astype(F32)
            sg = _sig(uu)
            du = (da * vv * (sg * (1.0 + uu * (1.0 - sg)))).astype(BF16)
            dv = (da * uu * sg).astype(BF16)
            dus[sl, :] = du
            dvs[sl, :] = dv
            acs[sl, :] = (uu * sg * vv).astype(BF16)
            dhacc[sl, :] += _nn(du, wg_ref[...]) + _nn(dv, wu_ref[...])
        dwg_ref[...] = _nn(hbt[...], dus[...]).T.astype(BF16)
        dwu_ref[...] = _nn(hbt[...], dvs[...]).T.astype(BF16)
        dwd_ref[...] = _nn(dft[...], acs[...]).T.astype(BF16)

        @pl.when(j == nj - 1)
        def _():
            cp = pltpu.make_async_copy(dhacc, dh_hbm, sems.at[0])
            cp.start()
            cp.wait()

    wsp = pl.BlockSpec((FT, D), lambda j: (j, 0))
    act = pl.BlockSpec((t, FT), lambda j: (0, j))
    return pl.pallas_call(
        body, name="ffn_bwd_seq", grid=(nj,),
        in_specs=[_ANY, _ANY, act, act, wsp, wsp, wsp],
        out_specs=[_ANY, wsp, wsp, wsp],
        out_shape=[jax.ShapeDtypeStruct((t, D), F32)] + [jax.ShapeDtypeStruct((f, D), BF16)] * 3,
        scratch_shapes=[pltpu.VMEM((t, D), BF16), pltpu.VMEM((t, D), BF16), pltpu.VMEM((D, t), BF16),
                        pltpu.VMEM((D, t), BF16), pltpu.VMEM((t, D), F32), pltpu.VMEM((t, FT), BF16),
                        pltpu.VMEM((t, FT), BF16), pltpu.VMEM((t, FT), BF16), pltpu.SemaphoreType.DMA((2,))],
        compiler_params=_cp("arbitrary"),
    )(dfb, hb, u, v, wg, wu, wd)


def mm_res_ln(a, b, x, ln_in, ln_out):
    t, k = a.shape
    tm = _row_tile(t)

    def body(a_ref, b_ref, x_ref, gi_ref, bi_ref, go_ref, bo_ref, r_ref, yb_ref):
        r = ALPHA * _layer_norm(x_ref[...], gi_ref[...], bi_ref[...]) + _nn(a_ref[...], b_ref[...])
        r_ref[...] = r
        yb_ref[...] = _layer_norm(r, go_ref[...], bo_ref[...]).astype(BF16)

    row = pl.BlockSpec((tm, D), lambda i: (i, 0))
    vec = pl.BlockSpec((1, D), lambda i: (0, 0))
    return pl.pallas_call(
        body, name="mm_res_ln", grid=(t // tm,),
        in_specs=[pl.BlockSpec((tm, k), lambda i: (i, 0)), pl.BlockSpec((k, D), lambda i: (0, 0)), row, vec, vec, vec, vec],
        out_specs=[row, row],
        out_shape=[jax.ShapeDtypeStruct((t, D), F32), jax.ShapeDtypeStruct((t, D), BF16)],
        compiler_params=_cp("arbitrary"),
    )(a, b, x, ln_in[0], ln_in[1], ln_out[0], ln_out[1])


def mm_nn(a, b, tn=512):
    t, k = a.shape
    n = b.shape[1]
    tm = _row_tile(t)

    def body(a_ref, b_ref, o_ref):
        o_ref[...] = _nn(a_ref[...], b_ref[...])

    return pl.pallas_call(
        body, name="mm_nn", grid=(n // tn, t // tm),
        in_specs=[pl.BlockSpec((tm, k), lambda j, i: (i, 0)), pl.BlockSpec((k, tn), lambda j, i: (0, j))],
        out_specs=pl.BlockSpec((tm, tn), lambda j, i: (i, j)),
        out_shape=jax.ShapeDtypeStruct((t, n), F32),
        compiler_params=_cp("arbitrary", "arbitrary"),
    )(a, b)


def mm_nt_reduce(pairs, n):
    g, t, _ = pairs[0][0].shape
    tm = _row_tile(t)
    npair = len(pairs)

    def body(*refs):
        o_ref = refs[-1]
        gi = pl.program_id(1)
        tot = _nt(refs[0][...], refs[1][...])
        for p in range(1, npair):
            tot += _nt(refs[2 * p][...], refs[2 * p + 1][...])

        @pl.when(gi == 0)
        def _():
            o_ref[...] = tot

        @pl.when(gi > 0)
        def _():
            o_ref[...] += tot

    in_specs, args = [], []
    for x, w in pairs:
        k = x.shape[2]
        in_specs += [pl.BlockSpec((None, tm, k), lambda i, gi: (gi, i, 0)),
                     pl.BlockSpec((None, n, k), lambda i, gi: (gi, 0, 0))]
        args += [x, w]
    return pl.pallas_call(
        body, name="mm_nt_reduce", grid=(t // tm, g),
        in_specs=in_specs, out_specs=pl.BlockSpec((tm, n), lambda i, gi: (i, 0)),
        out_shape=jax.ShapeDtypeStruct((t, n), F32),
        compiler_params=_cp("arbitrary", "arbitrary"),
    )(*args)


def mm_tn(x, y, out_dtype=BF16):
    gx, t, k = x.shape
    gy, _, n = y.shape
    g = max(gx, gy)
    tm = _row_tile(t)
    nt = t // tm

    def body(x_ref, y_ref, o_ref, acc):
        i = pl.program_id(1)

        @pl.when(i == 0)
        def _():
            acc[...] = jnp.zeros_like(acc)

        acc[...] += _tn(x_ref[...], y_ref[...])

        @pl.when(i == nt - 1)
        def _():
            o_ref[...] = acc[...].astype(out_dtype)

    return pl.pallas_call(
        body, name="mm_tn", grid=(g, nt),
        in_specs=[pl.BlockSpec((None, tm, k), (lambda gi, i: (gi, i, 0)) if gx > 1 else (lambda gi, i: (0, i, 0))),
                  pl.BlockSpec((None, tm, n), (lambda gi, i: (gi, i, 0)) if gy > 1 else (lambda gi, i: (0, i, 0)))],
        out_specs=pl.BlockSpec((None, k, n), lambda gi, i: (gi, 0, 0)),
        out_shape=jax.ShapeDtypeStruct((g, k, n), out_dtype),
        scratch_shapes=[pltpu.VMEM((k, n), F32)],
        compiler_params=_cp("arbitrary", "arbitrary"),
    )(x, y)


def ln_bwd(parts, r, gamma, out_scale, after=None):
    t = r.shape[0]
    tm = _row_tile(t)
    scales = [s for _, s in parts]
    npart = len(parts)
    extra = [] if after is None else [after]

    def body(*refs):
        refs = refs[len(extra):]
        r_ref, g_ref = refs[npart], refs[npart + 1]
        dr_ref, drb_ref, dg_ref, db_ref = refs[npart + 2:]
        i = pl.program_id(0)
        dy = scales[0] * refs[0][...]
        for p in range(1, npart):
            dy += scales[p] * refs[p][...]
        rr = r_ref[...]
        mu = jnp.mean(rr, axis=1, keepdims=True)
        xc = rr - mu
        rstd = lax.rsqrt(jnp.mean(xc * xc, axis=1, keepdims=True) + EPS)
        xh = xc * rstd
        dxh = dy * g_ref[...]
        m1 = jnp.mean(dxh, axis=1, keepdims=True)
        m2 = jnp.mean(dxh * xh, axis=1, keepdims=True)
        dr = rstd * (dxh - m1 - xh * m2)
        dr_ref[...] = dr
        drb_ref[...] = (out_scale * dr).astype(BF16)
        dg = jnp.sum(dy * xh, axis=0, keepdims=True)
        db = jnp.sum(dy, axis=0, keepdims=True)

        @pl.when(i == 0)
        def _():
            dg_ref[...] = dg
            db_ref[...] = db

        @pl.when(i > 0)
        def _():
            dg_ref[...] += dg
            db_ref[...] += db

    row = pl.BlockSpec((tm, D), lambda i: (i, 0))
    vec = pl.BlockSpec((1, D), lambda i: (0, 0))
    return pl.pallas_call(
        body, name="ln_bwd", grid=(t // tm,),
        in_specs=[_ANY] * len(extra) + [row] * (npart + 1) + [vec],
        out_specs=[row, row, vec, vec],
        out_shape=[jax.ShapeDtypeStruct((t, D), F32), jax.ShapeDtypeStruct((t, D), BF16),
                   jax.ShapeDtypeStruct((1, D), F32), jax.ShapeDtypeStruct((1, D), F32)],
        compiler_params=_cp("arbitrary"),
    )(*extra, *[p for p, _ in parts], r, gamma)


def loss_head(r, ln, target):
    t = r.shape[0]
    nb = t // BLK

    def body(r_ref, g_ref, b_ref, t_ref, dy_ref, l_ref):
        i = pl.program_id(0)

        @pl.when(i == 0)
        def _():
            dy_ref[...] = jnp.zeros_like(dy_ref)
            l_ref[...] = jnp.zeros_like(l_ref)

        @pl.when(i > 0)
        def _():
            err = _layer_norm(r_ref[...], g_ref[...], b_ref[...]) - t_ref[...]
            dy_ref[...] = err * (1.0 / D)
            l_ref[...] += (0.5 / D) * jnp.sum(err * err, keepdims=True)

    vec = pl.BlockSpec((1, D), lambda i: (0, 0))
    return pl.pallas_call(
        body, name="loss_head", grid=(nb,),
        in_specs=[pl.BlockSpec((BLK, D), lambda i: (i, 0)), vec, vec,
                  pl.BlockSpec((BLK, D), lambda i: (jnp.maximum(i - 1, 0), 0))],
        out_specs=[pl.BlockSpec((BLK, D), lambda i: (i, 0)), pl.BlockSpec((1, 1), lambda i: (0, 0))],
        out_shape=[jax.ShapeDtypeStruct((t, D), F32), jax.ShapeDtypeStruct((1, 1), F32)],
        compiler_params=_cp("arbitrary"),
    )(r, ln[0], ln[1], target)


def final_add(dr, dh, after=None):
    t = dr.shape[0]
    nb = t // BLK
    extra = [] if after is None else [after]

    def body(*refs):
        a_ref, b_ref, gx_ref, gm_ref = refs[len(extra):]
        i = pl.program_id(0)
        tot = ALPHA * a_ref[...] + b_ref[...]

        @pl.when(i == 0)
        def _():
            gm_ref[...] = tot[PAD:, :]

        @pl.when(i > 0)
        def _():
            gx_ref[...] = tot

    blk = pl.BlockSpec((BLK, D), lambda i: (i, 0))
    return pl.pallas_call(
        body, name="final_add", grid=(nb,),
        in_specs=[_ANY] * len(extra) + [blk, blk],
        out_specs=[pl.BlockSpec((BLK, D), lambda i: (jnp.maximum(i - 1, 0), 0)),
                   pl.BlockSpec((N_META, D), lambda i: (0, 0))],
        out_shape=[jax.ShapeDtypeStruct((t - BLK, D), F32), jax.ShapeDtypeStruct((N_META, D), F32)],
        compiler_params=_cp("arbitrary"),
    )(*extra, dr, dh)


def _valid_rows(nrows, first_row):
    return (first_row + lax.broadcasted_iota(jnp.int32, (nrows, 1), 0)) >= PAD


def conv_fwd(proj, conv_w, conv_b):
    t = proj.shape[0]
    c0 = C_XBC // BLK

    def body(x_ref, w_ref, b_ref, o_ref):
        ok = _valid_rows(t, 0)
        x = jnp.where(ok, x_ref[...], 0.0)
        w = w_ref[...]
        acc = b_ref[...] + w[CONV_K - 1:CONV_K, :] * x
        for s in range(1, CONV_K):
            acc += w[CONV_K - 1 - s:CONV_K - s, :] * pltpu.roll(x, s, 0)
        o_ref[...] = jnp.where(ok, acc * _sig(acc), 0.0)

    return pl.pallas_call(
        body, name="conv_fwd", grid=(CONV_D // BLK,),
        in_specs=[pl.BlockSpec((t, BLK), lambda j: (0, c0 + j)),
                  pl.BlockSpec((CONV_K, BLK), lambda j: (0, j)), pl.BlockSpec((1, BLK), lambda j: (0, j))],
        out_specs=pl.BlockSpec((t, BLK), lambda j: (0, j)),
        out_shape=jax.ShapeDtypeStruct((t, CONV_D), F32),
        compiler_params=_cp("arbitrary"),
    )(proj, conv_w, conv_b)


def conv_bwd(dxa, proj, conv_w, conv_b):
    t = proj.shape[0]
    c0 = C_XBC // BLK

    def body(d_ref, x_ref, w_ref, b_ref, dx_ref, dw_ref, db_ref):
        ok = _valid_rows(t, 0)
        x = jnp.where(ok, x_ref[...], 0.0)
        w = w_ref[...]
        xs = [x] + [pltpu.roll(x, s, 0) for s in range(1, CONV_K)]
        acc = b_ref[...] + w[CONV_K - 1:CONV_K, :] * x
        for s in range(1, CONV_K):
            acc += w[CONV_K - 1 - s:CONV_K - s, :] * xs[s]
        sg = _sig(acc)
        dxc = jnp.where(ok, d_ref[...] * (sg * (1.0 + acc * (1.0 - sg))), 0.0)
        db_ref[...] = jnp.sum(dxc, axis=0, keepdims=True)
        dw_ref[...] = jnp.concatenate(
            [jnp.sum(dxc * xs[CONV_K - 1 - k], axis=0, keepdims=True) for k in range(CONV_K)], axis=0)
        dx = w[CONV_K - 1:CONV_K, :] * dxc
        for s in range(1, CONV_K):
            dx += w[CONV_K - 1 - s:CONV_K - s, :] * pltpu.roll(dxc, t - s, 0)
        dx_ref[...] = jnp.where(ok, dx, 0.0)

    col = pl.BlockSpec((t, BLK), lambda j: (0, j))
    return pl.pallas_call(
        body, name="conv_bwd", grid=(CONV_D // BLK,),
        in_specs=[col, pl.BlockSpec((t, BLK), lambda j: (0, c0 + j)),
                  pl.BlockSpec((CONV_K, BLK), lambda j: (0, j)), pl.BlockSpec((1, BLK), lambda j: (0, j))],
        out_specs=[col, pl.BlockSpec((CONV_K, BLK), lambda j: (0, j)), pl.BlockSpec((1, BLK), lambda j: (0, j))],
        out_shape=[jax.ShapeDtypeStruct((t, CONV_D), F32), jax.ShapeDtypeStruct((CONV_K, CONV_D), F32),
                   jax.ShapeDtypeStruct((1, CONV_D), F32)],
        compiler_params=_cp("arbitrary"),
    )(dxa, proj, conv_w, conv_b)


def _softplus(x):
    return jnp.maximum(x, 0.0) + jnp.log(1.0 + jnp.exp(-jnp.abs(x)))


def _ssd_chunk(xa, sm, dtb, alog, ok):
    dt = jnp.where(ok, _softplus(sm + dtb), 0.0)
    amat = -jnp.exp(alog)
    a = dt * amat
    ac = _nn_hi(_tri().astype(F32), a)
    act = ac.T
    return dt, amat, ac, act


def _ssd_head(xa, dt, ac, act, h, cb, sp):
    g = h // (SSD_H // SSD_G)
    xs = xa[:, SSD_P * h:SSD_P * (h + 1)]
    bg = xa[:, SSD_D + SSD_N * g:SSD_D + SSD_N * (g + 1)]
    cg = xa[:, SSD_D + SSD_G * SSD_N + SSD_N * g:SSD_D + SSD_G * SSD_N + SSD_N * (g + 1)]
    dth = dt[:, h:h + 1]
    ach = ac[:, h:h + 1]
    acth = act[h:h + 1, :]
    xdt = xs * dth
    seg = jnp.where(_tri(), jnp.exp(jnp.minimum(ach - acth, 0.0)), 0.0)
    m = cb * seg
    yd = _nn(m.astype(BF16), xdt.astype(BF16))
    last = ac[BLK - 1:BLK, h:h + 1]
    dec = jnp.exp(last - ach)
    e = jnp.exp(ach)
    yo = _nn(cg.astype(BF16), sp.astype(BF16)) * e
    return xs, bg, cg, dth, ach, xdt, seg, m, yd, last, dec, e, yo


def ssd_fwd(xa, proj, dtb, alog, dskip, normg):
    t = xa.shape[0]
    nb = t // BLK
    gw = SSD_D // SSD_G

    def body(xa_ref, z_ref, sm_ref, dtb_ref, al_ref, ds_ref, ng_ref, y_ref, sp_ref, st):
        c = pl.program_id(0)

        @pl.when(c == 0)
        def _():
            st[...] = jnp.zeros_like(st)

        ok = _valid_rows(BLK, c * BLK)
        xa = xa_ref[...]
        dt, _, ac, act = _ssd_chunk(xa, sm_ref[...], dtb_ref[...], al_ref[...], ok)
        sp_ref[...] = st[...]
        ys = []
        cbs = {}
        for h in range(SSD_H):
            g = h // (SSD_H // SSD_G)
            if g not in cbs:
                bg = xa[:, SSD_D + SSD_N * g:SSD_D + SSD_N * (g + 1)]
                cg = xa[:, SSD_D + SSD_G * SSD_N + SSD_N * g:SSD_D + SSD_G * SSD_N + SSD_N * (g + 1)]
                cbs[g] = _nt(cg.astype(BF16), bg.astype(BF16))
            sp = st[:, SSD_P * h:SSD_P * (h + 1)]
            xs, bg, cg, dth, ach, xdt, seg, m, yd, last, dec, e, yo = _ssd_head(xa, dt, ac, act, h, cbs[g], sp)
            sloc = _tn((bg * dec).astype(BF16), xdt.astype(BF16))
            st[:, SSD_P * h:SSD_P * (h + 1)] = jnp.exp(last) * sp + sloc
            ys.append(yd + yo + ds_ref[:, h:h + 1] * xs)
        y = jnp.concatenate(ys, axis=1)
        z = z_ref[...]
        yg = y * (z * _sig(z))
        outs = []
        for g in range(SSD_G):
            v = yg[:, gw * g:gw * (g + 1)]
            outs.append(v * lax.rsqrt(jnp.mean(v * v, axis=1, keepdims=True) + EPS))
        y_ref[...] = (jnp.concatenate(outs, axis=1) * ng_ref[...]).astype(BF16)

    vec = pl.BlockSpec((1, BLK), lambda c: (0, 0))
    return pl.pallas_call(
        body, name="ssd_fwd", grid=(nb,),
        in_specs=[pl.BlockSpec((BLK, CONV_D), lambda c: (c, 0)),
                  pl.BlockSpec((BLK, SSD_D), lambda c: (c, C_Z // SSD_D)),
                  pl.BlockSpec((BLK, BLK), lambda c: (c, C_SM // BLK)),
                  vec, vec, vec, pl.BlockSpec((1, SSD_D), lambda c: (0, 0))],
        out_specs=[pl.BlockSpec((BLK, SSD_D), lambda c: (c, 0)),
                   pl.BlockSpec((None, SSD_N, SSD_D), lambda c: (c, 0, 0))],
        out_shape=[jax.ShapeDtypeStruct((t, SSD_D), BF16), jax.ShapeDtypeStruct((nb, SSD_N, SSD_D), F32)],
        scratch_shapes=[pltpu.VMEM((SSD_N, SSD_D), F32)],
        compiler_params=_cp("arbitrary"),
    )(xa, proj, proj, dtb, alog, dskip, normg)


def _lane_put(col, lane):
    li = lax.broadcasted_iota(jnp.int32, (col.shape[0], BLK), 1)
    return jnp.where(li == lane, col, 0.0)


def ssd_bwd(dmix, xa, proj, sprev, dtb, alog, dskip, normg):
    t = xa.shape[0]
    nb = t // BLK
    gw = SSD_D // SSD_G
    rev = lambda c: nb - 1 - c

    def body(dy_ref, xa_ref, z_ref, sm_ref, sp_ref, dtb_ref, al_ref, ds_ref, ng_ref,
             dxa_ref, dz_ref, dsm_ref, dng_ref, dds_ref, dal_ref, ddtb_ref, dst):
        c = pl.program_id(0)

        @pl.when(c == 0)
        def _():
            dst[...] = jnp.zeros_like(dst)
            dng_ref[...] = jnp.zeros_like(dng_ref)
            dds_ref[...] = jnp.zeros_like(dds_ref)
            dal_ref[...] = jnp.zeros_like(dal_ref)
            ddtb_ref[...] = jnp.zeros_like(ddtb_ref)

        ok = _valid_rows(BLK, rev(c) * BLK)
        xa = xa_ref[...]
        sm = sm_ref[...]
        dt, amat, ac, act = _ssd_chunk(xa, sm, dtb_ref[...], al_ref[...], ok)
        tri = _tri()
        rowi = lax.broadcasted_iota(jnp.int32, (BLK, 1), 0)
        cbs, heads, ys = {}, [], []
        for h in range(SSD_H):
            g = h // (SSD_H // SSD_G)
            if g not in cbs:
                bg = xa[:, SSD_D + SSD_N * g:SSD_D + SSD_N * (g + 1)]
                cg = xa[:, SSD_D + SSD_G * SSD_N + SSD_N * g:SSD_D + SSD_G * SSD_N + SSD_N * (g + 1)]
                cbs[g] = _nt(cg.astype(BF16), bg.astype(BF16))
            sp = sp_ref[:, SSD_P * h:SSD_P * (h + 1)]
            hd = _ssd_head(xa, dt, ac, act, h, cbs[g], sp)
            heads.append(hd)
            ys.append(hd[8] + hd[12] + ds_ref[:, h:h + 1] * hd[0])
        y = jnp.concatenate(ys, axis=1)
        z = z_ref[...]
        sgz = _sig(z)
        siluz = z * sgz
        yg = y * siluz
        dout = dy_ref[...]
        ng = ng_ref[...]
        dygs, xhs = [], []
        for g in range(SSD_G):
            v = yg[:, gw * g:gw * (g + 1)]
            rr = lax.rsqrt(jnp.mean(v * v, axis=1, keepdims=True) + EPS)
            xh = v * rr
            dxh = dout[:, gw * g:gw * (g + 1)] * ng[:, gw * g:gw * (g + 1)]
            dygs.append(rr * (dxh - xh * jnp.mean(dxh * xh, axis=1, keepdims=True)))
            xhs.append(xh)
        dyg = jnp.concatenate(dygs, axis=1)
        dng_ref[...] += jnp.sum(dout * jnp.concatenate(xhs, axis=1), axis=0, keepdims=True)
        dy = dyg * siluz
        dz_ref[...] = dyg * y * (sgz * (1.0 + z * (1.0 - sgz)))

        dxs_l = []
        db_g = [jnp.zeros((BLK, SSD_N), F32) for _ in range(SSD_G)]
        dc_g = [jnp.zeros((BLK, SSD_N), F32) for _ in range(SSD_G)]
        dac_all = jnp.zeros((BLK, BLK), F32)
        ddt_all = jnp.zeros((BLK, BLK), F32)
        dds_row = jnp.zeros((1, BLK), F32)
        lane1 = lax.broadcasted_iota(jnp.int32, (1, BLK), 1)
        for h in range(SSD_H):
            g = h // (SSD_H // SSD_G)
            xs, bg, cg, dth, ach, xdt, seg, m, yd, last, dec, e, yo = heads[h]
            sp = sp_ref[:, SSD_P * h:SSD_P * (h + 1)]
            dyh = dy[:, SSD_P * h:SSD_P * (h + 1)]
            dyb = dyh.astype(BF16)
            xdtb = xdt.astype(BF16)
            dds_row += jnp.where(lane1 == h, jnp.sum(dyh * xs, keepdims=True), 0.0)
            dxs = ds_ref[:, h:h + 1] * dyh
            dyo = (dyh * e).astype(BF16)
            dc_g[g] += _nt(dyo, sp.astype(BF16))
            dsp = _tn(cg.astype(BF16), dyo)
            dac = jnp.sum(dyh * yo, axis=1, keepdims=True)
            dsn = dst[:, SSD_P * h:SSD_P * (h + 1)]
            gl = jnp.exp(last)
            dst[:, SSD_P * h:SSD_P * (h + 1)] = dsp + gl * dsn
            dlast = jnp.sum(dsn * sp, keepdims=True) * gl
            dsnb = dsn.astype(BF16)
            dbd = _nt(xdtb, dsnb)
            db_g[g] += dbd * dec
            tdec = jnp.sum(dbd * bg, axis=1, keepdims=True) * dec
            dxdt = _nn((bg * dec).astype(BF16), dsnb)
            dlast += jnp.sum(tdec, keepdims=True)
            dac -= tdec
            dm = _nt(dyb, xdtb)
            dxdt += _tn(m.astype(BF16), dyb)
            dcb = (dm * seg).astype(BF16)
            dc_g[g] += _nn(dcb, bg.astype(BF16))
            db_g[g] += _tn(dcb, cg.astype(BF16))
            w = dm * m
            dac += jnp.sum(w, axis=1, keepdims=True) - jnp.sum(w.T, axis=1, keepdims=True)
            dac += jnp.where(rowi == BLK - 1, dlast, 0.0)
            dxs_l.append(dxs + dxdt * dth)
            ddt_all += _lane_put(jnp.sum(dxdt * xs, axis=1, keepdims=True), h)
            dac_all += _lane_put(dac, h)
        da = _nn_hi(_tri(lower=False).astype(F32), dac_all)
        ddt = ddt_all + da * amat
        dal_ref[...] += jnp.sum(da * dt, axis=0, keepdims=True) * amat
        ddtr = jnp.where(ok, ddt * _sig(sm + dtb_ref[...]), 0.0)
        ddtb_ref[...] += jnp.sum(ddtr, axis=0, keepdims=True)
        dds_ref[...] += dds_row
        dsm_ref[...] = ddtr
        dxa_ref[...] = jnp.where(ok, jnp.concatenate(dxs_l + db_g + dc_g, axis=1), 0.0)

    vec = pl.BlockSpec((1, BLK), lambda c: (0, 0))
    nvec = pl.BlockSpec((1, SSD_D), lambda c: (0, 0))
    return pl.pallas_call(
        body, name="ssd_bwd", grid=(nb,),
        in_specs=[pl.BlockSpec((BLK, SSD_D), lambda c: (rev(c), 0)),
                  pl.BlockSpec((BLK, CONV_D), lambda c: (rev(c), 0)),
                  pl.BlockSpec((BLK, SSD_D), lambda c: (rev(c), C_Z // SSD_D)),
                  pl.BlockSpec((BLK, BLK), lambda c: (rev(c), C_SM // BLK)),
                  pl.BlockSpec((None, SSD_N, SSD_D), lambda c: (rev(c), 0, 0)),
                  vec, vec, vec, nvec],
        out_specs=[pl.BlockSpec((BLK, CONV_D), lambda c: (rev(c), 0)),
                   pl.BlockSpec((BLK, SSD_D), lambda c: (rev(c), 0)),
                   pl.BlockSpec((BLK, BLK), lambda c: (rev(c), 0)),
                   nvec, vec, vec, vec],
        out_shape=[jax.ShapeDtypeStruct((t, CONV_D), F32), jax.ShapeDtypeStruct((t, SSD_D), F32),
                   jax.ShapeDtypeStruct((t, BLK), F32), jax.ShapeDtypeStruct((1, SSD_D), F32),
                   jax.ShapeDtypeStruct((1, BLK), F32), jax.ShapeDtypeStruct((1, BLK), F32),
                   jax.ShapeDtypeStruct((1, BLK), F32)],
        scratch_shapes=[pltpu.VMEM((SSD_N, SSD_D), F32)],
        compiler_params=_cp("arbitrary"),
    )(dmix, xa, proj, proj, sprev, dtb, alog, dskip, normg)


def _attn_scores(q_ref, k_ref, h, dq, scale, mask, bias):
    qh = q_ref[:, dq * h:dq * (h + 1)].astype(BF16)
    kh = k_ref[:, dq * h:dq * (h + 1)].astype(BF16)
    s = _nt(qh, kh) * scale
    if bias is not None:
        s = s + bias
    return qh, kh, jnp.where(mask, s, NEG)


def _segments(nb):
    cuts = sorted({0, nb} | {max(1, round(nb * f)) for f in (0.3, 0.53, 0.77)})
    return list(zip(cuts[:-1], cuts[1:]))


def attn_fwd(q, k, v, qcol, kcol, vcol, nh, dq, dv, scale, c_col=None, c_row=None, lane0=0):
    t = q.shape[0]
    tq = BLK
    use_bias = c_col is not None

    def segment(t0, t1, prev):
        tk = t1 * BLK
        nprev = len(prev)

        def body(*refs):
            refs = refs[nprev:]
            if use_bias:
                q_ref, k_ref, v_ref, cc_ref, cr_ref, o_ref, l_ref = refs
            else:
                q_ref, k_ref, v_ref, o_ref, l_ref = refs
            i = pl.program_id(0)
            rowg = (t0 + i) * tq + lax.broadcasted_iota(jnp.int32, (tq, 1), 0)
            col = lax.broadcasted_iota(jnp.int32, (1, tk), 1)
            mask = (col <= rowg) & (col >= PAD)
            outs = []
            lse = jnp.zeros((tq, BLK), F32)
            for h in range(nh):
                bias = (cc_ref[:, lane0 + h:lane0 + h + 1] - cr_ref[h:h + 1, :]) if use_bias else None
                _, _, s = _attn_scores(q_ref, k_ref, h, dq, scale, mask, bias)
                m = jnp.max(s, axis=1, keepdims=True)
                p = jnp.exp(s - m)
                l = jnp.sum(p, axis=1, keepdims=True)
                vh = v_ref[:, dv * h:dv * (h + 1)].astype(BF16)
                outs.append(_nn(p.astype(BF16), vh) / l)
                lse += _lane_put(m + jnp.log(l), h)
            o_ref[...] = jnp.concatenate(outs, axis=1).astype(BF16)
            l_ref[...] = lse

        in_specs = [_ANY] * nprev + [pl.BlockSpec((tq, nh * dq), lambda i: (t0 + i, qcol)),
                                     pl.BlockSpec((tk, nh * dq), lambda i: (0, kcol)),
                                     pl.BlockSpec((tk, nh * dv), lambda i: (0, vcol))]
        args = list(prev) + [q, k, v]
        if use_bias:
            in_specs += [pl.BlockSpec((tq, BLK), lambda i: (t0 + i, 0)), pl.BlockSpec((8, tk), lambda i: (0, 0))]
            args += [c_col, c_row]
        return pl.pallas_call(
            body, name="attn_fwd", grid=(t1 - t0,),
            in_specs=in_specs,
            out_specs=[pl.BlockSpec((tq, nh * dv), lambda i: (t0 + i, 0)), pl.BlockSpec((tq, BLK), lambda i: (t0 + i, 0))],
            out_shape=[jax.ShapeDtypeStruct((t, nh * dv), BF16), jax.ShapeDtypeStruct((t, BLK), F32)],
            input_output_aliases={p: p for p in range(nprev)},
            compiler_params=_cp("arbitrary"),
        )(*args)

    outs = []
    for t0, t1 in _segments(t // tq):
        outs = segment(t0, t1, outs)
    return outs


def attn_bwd(q, k, v, do, lse, qcol, kcol, vcol, docol, nh, dq, dv, scale, c_col=None, c_row=None, lane0=0):
    t = q.shape[0]
    tq = BLK
    use_bias = c_col is not None

    def segment(t0, t1, prev):
        tk = t1 * BLK
        nprev = len(prev)

        def body(*refs):
            pv, refs = refs[:nprev], refs[nprev:]
            if use_bias:
                q_ref, k_ref, v_ref, do_ref, l_ref, cc_ref, cr_ref, dq_ref, dk_ref, dv_ref, dcq_ref, dck_ref = refs
            else:
                q_ref, k_ref, v_ref, do_ref, l_ref, dq_ref, dk_ref, dv_ref = refs
            i = pl.program_id(0)

            @pl.when(i == 0)
            def _():
                if nprev:
                    dk_ref[...] = pv[1][...]
                    dv_ref[...] = pv[2][...]
                    if use_bias:
                        dck_ref[...] = pv[4][...]
                else:
                    dk_ref[...] = jnp.zeros_like(dk_ref)
                    dv_ref[...] = jnp.zeros_like(dv_ref)
                    if use_bias:
                        dck_ref[...] = jnp.zeros_like(dck_ref)

            rowg = (t0 + i) * tq + lax.broadcasted_iota(jnp.int32, (tq, 1), 0)
            col = lax.broadcasted_iota(jnp.int32, (1, tk), 1)
            mask = (col <= rowg) & (col >= PAD)
            dqs = []
            dcq = jnp.zeros((tq, BLK), F32)
            for h in range(nh):
                bias = (cc_ref[:, lane0 + h:lane0 + h + 1] - cr_ref[h:h + 1, :]) if use_bias else None
                qh, kh, s = _attn_scores(q_ref, k_ref, h, dq, scale, mask, bias)
                p = jnp.where(mask, jnp.exp(s - l_ref[:, h:h + 1]), 0.0)
                vh = v_ref[:, dv * h:dv * (h + 1)].astype(BF16)
                doh = do_ref[:, dv * h:dv * (h + 1)].astype(BF16)
                dp = _nt(doh, vh)
                delta = jnp.sum(p * dp, axis=1, keepdims=True)
                ds = p * (dp - delta)
                dsb = ds.astype(BF16)
                dqs.append(_nn(dsb, kh) * scale)
                dk_ref[:, dq * h:dq * (h + 1)] += _tn(dsb, qh) * scale
                dv_ref[:, dv * h:dv * (h + 1)] += _tn(p.astype(BF16), doh)
                if use_bias:
                    dcq += _lane_put(jnp.sum(ds, axis=1, keepdims=True), lane0 + h)
                    dck_ref[h:h + 1, :] += jnp.sum(ds, axis=0, keepdims=True)
            dq_ref[...] = jnp.concatenate(dqs, axis=1)
            if use_bias:
                dcq_ref[...] = dcq

        keys_q = pl.BlockSpec((tk, nh * dq), lambda i: (0, 0))
        keys_v = pl.BlockSpec((tk, nh * dv), lambda i: (0, 0))
        keys_c = pl.BlockSpec((8, tk), lambda i: (0, 0))
        prev_specs = ([_ANY, keys_q, keys_v] + ([_ANY, keys_c] if use_bias else [])) if nprev else []
        in_specs = prev_specs + [pl.BlockSpec((tq, nh * dq), lambda i: (t0 + i, qcol)),
                                 pl.BlockSpec((tk, nh * dq), lambda i: (0, kcol)),
                                 pl.BlockSpec((tk, nh * dv), lambda i: (0, vcol)),
                                 pl.BlockSpec((tq, nh * dv), lambda i: (t0 + i, docol)),
                                 pl.BlockSpec((tq, BLK), lambda i: (t0 + i, 0))]
        args = list(prev) + [q, k, v, do, lse]
        out_specs = [pl.BlockSpec((tq, nh * dq), lambda i: (t0 + i, 0)), keys_q, keys_v]
        out_shape = [jax.ShapeDtypeStruct((t, nh * dq), F32), jax.ShapeDtypeStruct((t, nh * dq), F32),
                     jax.ShapeDtypeStruct((t, nh * dv), F32)]
        if use_bias:
            in_specs += [pl.BlockSpec((tq, BLK), lambda i: (t0 + i, 0)), keys_c]
            args += [c_col, c_row]
            out_specs += [pl.BlockSpec((tq, BLK), lambda i: (t0 + i, 0)), keys_c]
            out_shape += [jax.ShapeDtypeStruct((t, BLK), F32), jax.ShapeDtypeStruct((8, t), F32)]
        return pl.pallas_call(
            body, name="attn_bwd", grid=(t1 - t0,),
            in_specs=in_specs, out_specs=out_specs, out_shape=out_shape,
            input_output_aliases={p: p for p in range(nprev)},
            compiler_params=_cp("arbitrary"),
        )(*args)

    outs = []
    for t0, t1 in reversed(_segments(t // tq)):
        outs = segment(t0, t1, outs)
    return outs


def fox_pre(proj, fb):
    t = proj.shape[0]
    nb = t // BLK

    def body(sm_ref, fb_ref, c_ref, cr_ref):
        x = sm_ref[...] + fb_ref[...]
        lane = lax.broadcasted_iota(jnp.int32, (1, BLK), 1)
        keep = _valid_rows(t, 0) & (lane >= SM_F) & (lane < SM_F + FOX_H)
        logf = jnp.where(keep, jnp.minimum(x, 0.0) - jnp.log(1.0 + jnp.exp(-jnp.abs(x))), 0.0)
        tri = _tri().astype(F32)
        carry = jnp.zeros((1, BLK), F32)
        for b in range(nb):
            cb = _nn_hi(tri, logf[b * BLK:(b + 1) * BLK, :]) + carry
            c_ref[b * BLK:(b + 1) * BLK, :] = cb
            carry = cb[BLK - 1:BLK, :]
        cr_ref[...] = c_ref[...].T[SM_F:SM_F + 8, :]

    return pl.pallas_call(
        body, name="fox_pre", grid=(1,),
        in_specs=[pl.BlockSpec((t, BLK), lambda i: (0, C_SM // BLK)), pl.BlockSpec((1, BLK), lambda i: (0, 0))],
        out_specs=[pl.BlockSpec((t, BLK), lambda i: (0, 0)), pl.BlockSpec((8, t), lambda i: (0, 0))],
        out_shape=[jax.ShapeDtypeStruct((t, BLK), F32), jax.ShapeDtypeStruct((8, t), F32)],
        compiler_params=_cp("arbitrary"),
    )(proj, fb)


def fox_pre_bwd(dcq, dck, proj, fb, dsm_in):
    t = proj.shape[0]
    nb = t // BLK

    def body(dcq_ref, dck_ref, sm_ref, fb_ref, din_ref, dsm_ref, dfb_ref, scr):
        triu = _tri(lower=False).astype(F32)
        carry = jnp.zeros((1, BLK), F32)
        scr[...] = jnp.concatenate([jnp.zeros((SM_F, t), F32), dck_ref[...], jnp.zeros((BLK - SM_F - 8, t), F32)], axis=0).T
        for b in range(nb - 1, -1, -1):
            blk = dcq_ref[b * BLK:(b + 1) * BLK, :] - scr[b * BLK:(b + 1) * BLK, :]
            cb = _nn_hi(triu, blk) + carry
            scr[b * BLK:(b + 1) * BLK, :] = cb
            carry = cb[0:1, :]
        x = sm_ref[...] + fb_ref[...]
        lane = lax.broadcasted_iota(jnp.int32, (1, BLK), 1)
        keep = _valid_rows(t, 0) & (lane >= SM_F) & (lane < SM_F + FOX_H)
        df = jnp.where(keep, scr[...] * _sig(-x), 0.0)
        dfb_ref[...] = jnp.sum(df, axis=0, keepdims=True)
        dsm_ref[...] = din_ref[...] + df

    full = pl.BlockSpec((t, BLK), lambda i: (0, 0))
    return pl.pallas_call(
        body, name="fox_pre_bwd", grid=(1,),
        in_specs=[full, pl.BlockSpec((8, t), lambda i: (0, 0)), pl.BlockSpec((t, BLK), lambda i: (0, C_SM // BLK)),
                  pl.BlockSpec((1, BLK), lambda i: (0, 0)), full],
        out_specs=[full, pl.BlockSpec((1, BLK), lambda i: (0, 0))],
        out_shape=[jax.ShapeDtypeStruct((t, BLK), F32), jax.ShapeDtypeStruct((1, BLK), F32)],
        scratch_shapes=[pltpu.VMEM((t, BLK), F32)],
        compiler_params=_cp("arbitrary"),
    )(dcq, dck, proj, fb, dsm_in)


def _swap_rope(x):
    lane = lax.broadcasted_iota(jnp.int32, (1, BLK), 1)
    return jnp.where((lane >= SM_KR) & (lane < SM_KR + 16), pltpu.roll(x, BLK - 16, 1),
                     jnp.where((lane >= SM_KR + 16) & (lane < SM_KR + 32), pltpu.roll(x, 16, 1), 0.0))


def _rms(x, g):
    r = lax.rsqrt(jnp.mean(x * x, axis=1, keepdims=True) + EPS)
    return r, x * r


def mla_pre(proj, qg, kvg, wq, wk, wv, cosq, sinq):
    t = proj.shape[0]
    tm = _row_tile(t)

    def body(cq_ref, ckv_ref, sm_ref, qg_ref, kvg_ref, wq_ref, wk_ref, wv_ref, cos_ref, sin_ref,
             q_ref, k_ref, v_ref, cqn_ref, ckvn_ref):
        cs, sn = cos_ref[...], sin_ref[...]
        _, xh = _rms(cq_ref[...], None)
        cqn = (xh * qg_ref[...]).astype(BF16)
        cqn_ref[...] = cqn
        qraw = _nn(cqn, wq_ref[...])
        qs = []
        for h in range(MLA_H):
            hb = qraw[:, BLK * h:BLK * (h + 1)]
            qs.append(hb * cs + _swap_rope(hb) * sn)
        q_ref[...] = jnp.concatenate(qs, axis=1).astype(BF16)
        _, kh = _rms(ckv_ref[...], None)
        ckvn = (kh * kvg_ref[...]).astype(BF16)
        ckvn_ref[...] = ckvn
        kraw = _nn(ckvn, wk_ref[...])
        v_ref[...] = _nn(ckvn, wv_ref[...]).astype(BF16)
        lane = lax.broadcasted_iota(jnp.int32, (1, BLK), 1)
        kr = sm_ref[...]
        krr = jnp.where((lane >= SM_KR) & (lane < SM_KR + MLA_ROPE), kr * cs + _swap_rope(kr) * sn, 0.0)
        k_ref[...] = jnp.concatenate([kraw[:, BLK * h:BLK * (h + 1)] + krr for h in range(MLA_H)], axis=1).astype(BF16)

    def rows(w, cb):
        return pl.BlockSpec((tm, w), lambda i: (i, cb))

    def whole(a):
        return pl.BlockSpec(a.shape, lambda i: (0, 0))

    return pl.pallas_call(
        body, name="mla_pre", grid=(t // tm,),
        in_specs=[rows(MLA_QL, C_CQ // MLA_QL), rows(MLA_KVL, C_CKV // MLA_KVL), rows(BLK, C_SM // BLK),
                  whole(qg), whole(kvg), whole(wq), whole(wk), whole(wv), rows(BLK, 0), rows(BLK, 0)],
        out_specs=[rows(512, 0), rows(512, 0), rows(256, 0), rows(MLA_QL, 0), rows(MLA_KVL, 0)],
        out_shape=[jax.ShapeDtypeStruct((t, 512), BF16), jax.ShapeDtypeStruct((t, 512), BF16),
                   jax.ShapeDtypeStruct((t, 256), BF16), jax.ShapeDtypeStruct((t, MLA_QL), BF16),
                   jax.ShapeDtypeStruct((t, MLA_KVL), BF16)],
        compiler_params=_cp("arbitrary"),
    )(proj, proj, proj, qg, kvg, wq, wk, wv, cosq, sinq)


def mla_pre_bwd(dq, dk, dv, proj, cqn, ckvn, qg, kvg, wq, wk, wv, cosq, sinq, dsm_in):
    t = proj.shape[0]
    tm = _row_tile(t)

    def body(dq_ref, dk_ref, dv_ref, cq_ref, ckv_ref, cqn_ref, ckvn_ref, qg_ref, kvg_ref, wq_ref, wk_ref, wv_ref,
             cos_ref, sin_ref, din_ref, dcq_ref, dckv_ref, dsm_ref, dwq_ref, dwk_ref, dwv_ref, dqg_ref, dkvg_ref):
        i = pl.program_id(0)

        @pl.when(i == 0)
        def _():
            for r in (dwq_ref, dwk_ref, dwv_ref, dqg_ref, dkvg_ref):
                r[...] = jnp.zeros_like(r)

        cs, sn = cos_ref[...], sin_ref[...]
        lane = lax.broadcasted_iota(jnp.int32, (1, BLK), 1)

        def unrope(dy):
            return dy * cs + _swap_rope(dy * sn)

        dqp = jnp.concatenate([unrope(dq_ref[:, BLK * h:BLK * (h + 1)]) for h in range(MLA_H)], axis=1).astype(BF16)
        dwq_ref[...] += _tn(cqn_ref[...], dqp)
        dcqn = _nt(dqp, wq_ref[...])
        r, xh = _rms(cq_ref[...], None)
        dqg_ref[...] += jnp.sum(dcqn * xh, axis=0, keepdims=True)
        dxh = dcqn * qg_ref[...]
        dcq_ref[...] = r * (dxh - xh * jnp.mean(dxh * xh, axis=1, keepdims=True))

        dkn, dkr = [], jnp.zeros((tm, BLK), F32)
        for h in range(MLA_H):
            blk = dk_ref[:, BLK * h:BLK * (h + 1)]
            dkn.append(jnp.where(lane < MLA_NOPE, blk, 0.0))
            dkr += jnp.where((lane >= SM_KR) & (lane < SM_KR + MLA_ROPE), blk, 0.0)
        dknb = jnp.concatenate(dkn, axis=1).astype(BF16)
        dvb = dv_ref[...].astype(BF16)
        ckvn = ckvn_ref[...]
        dwk_ref[...] += _tn(ckvn, dknb)
        dwv_ref[...] += _tn(ckvn, dvb)
        dckvn = _nt(dknb, wk_ref[...]) + _nt(dvb, wv_ref[...])
        r2, kh = _rms(ckv_ref[...], None)
        dkvg_ref[...] += jnp.sum(dckvn * kh, axis=0, keepdims=True)
        dkh = dckvn * kvg_ref[...]
        dckv_ref[...] = r2 * (dkh - kh * jnp.mean(dkh * kh, axis=1, keepdims=True))
        dsm_ref[...] = din_ref[...] + jnp.where((lane >= SM_KR) & (lane < SM_KR + MLA_ROPE), unrope(dkr), 0.0)

    def rows(w, cb):
        return pl.BlockSpec((tm, w), lambda i: (i, cb))

    def whole(a):
        return pl.BlockSpec(a.shape, lambda i: (0, 0))

    def wshape(a):
        return jax.ShapeDtypeStruct(a.shape, F32)

    return pl.pallas_call(
        body, name="mla_pre_bwd", grid=(t // tm,),
        in_specs=[rows(512, 0), rows(512, 0), rows(256, 0), rows(MLA_QL, C_CQ // MLA_QL), rows(MLA_KVL, C_CKV // MLA_KVL),
                  rows(MLA_QL, 0), rows(MLA_KVL, 0), whole(qg), whole(kvg), whole(wq), whole(wk), whole(wv),
                  rows(BLK, 0), rows(BLK, 0), rows(BLK, 0)],
        out_specs=[rows(MLA_QL, 0), rows(MLA_KVL, 0), rows(BLK, 0), whole(wq), whole(wk), whole(wv), whole(qg), whole(kvg)],
        out_shape=[jax.ShapeDtypeStruct((t, MLA_QL), F32), jax.ShapeDtypeStruct((t, MLA_KVL), F32),
                   jax.ShapeDtypeStruct((t, BLK), F32), wshape(wq), wshape(wk), wshape(wv), wshape(qg), wshape(kvg)],
        compiler_params=_cp("arbitrary"),
    )(dq, dk, dv, proj, proj, cqn, ckvn, qg, kvg, wq, wk, wv, cosq, sinq, dsm_in)


def _slot_sum(me, own, recv_ref):
    gg = own.astype(F32)
    for s in range(N_DEV):
        gg = gg + jnp.where(me == s, 0.0, recv_ref[s].astype(F32))
    return gg


def adamw(w, m, v, g=None, recv=None, own=None, me_arr=None):
    shape = w.shape
    c = shape[-1]
    from_recv = recv is not None
    if not from_recv:
        me_arr = jnp.zeros((1,), jnp.int32)
    nl = len(recv) if from_recv else 1
    rws = w.size // c // nl
    tr = rws
    for d in (1024, 512, 352, 256, 128, 64, 32, 16, 8):
        if rws % d == 0 and d * c * 4 <= (2 << 20):
            tr = d
            break
    nt = rws // tr
    w2, m2, v2 = (a.reshape(nl, rws, c) for a in (w, m, v))
    if from_recv:
        gin = [a.reshape(N_DEV, rws, c) for a in list(recv) + list(own)]
    else:
        gin = [g.reshape(1, rws, c)]

    def body(me_ref, w_ref, m_ref, v_ref, *rest):
        g_refs, outs = rest[:len(gin)], rest[len(gin):]
        if from_recv:
            g_out, outs = outs[0], outs[1:]
            for li in range(nl):
                @pl.when(pl.program_id(0) == li)
                def _(li=li):
                    g_out[...] = _slot_sum(me_ref[0], g_refs[nl + li][...], g_refs[li])
            gg = g_out[...]
        else:
            gg = g_refs[0][...]
        d_ref, nm_ref, nv_ref = outs
        nm = B1 * m_ref[...] + (1.0 - B1) * gg
        nv = B2 * v_ref[...] + (1.0 - B2) * (gg * gg)
        mh = nm / (1.0 - B1 ** STEP)
        vh = nv / (1.0 - B2 ** STEP)
        d_ref[...] = -LR * (mh / (jnp.sqrt(vh) + AEPS) + WD * w_ref[...])
        nm_ref[...] = nm
        nv_ref[...] = nv

    row = pl.BlockSpec((None, tr, c), lambda l, i, me: (l, i, 0))
    if from_recv:
        gspecs = [pl.BlockSpec((N_DEV, tr, c), lambda l, i, me, li=li: (0, jnp.where(l == li, i, 0), 0))
                  for li in range(nl)]
        gspecs += [pl.BlockSpec((None, tr, c), lambda l, i, me, li=li: (me[0], jnp.where(l == li, i, 0), 0))
                   for li in range(nl)]
    else:
        gspecs = [row]
    nout = 4 if from_recv else 3
    outs = pl.pallas_call(
        body, name="adamw",
        grid_spec=pltpu.PrefetchScalarGridSpec(num_scalar_prefetch=1, grid=(nl, nt), in_specs=[row, row, row] + gspecs,
                                               out_specs=[row] * nout),
        out_shape=[jax.ShapeDtypeStruct((nl, rws, c), F32)] * nout,
        compiler_params=_cp("arbitrary", "arbitrary"),
    )(me_arr, w2, m2, v2, *gin)
    return tuple(o.reshape(shape) for o in outs)


def sum_slots(recv, own=None, me_arr=None):
    _, r, c = recv.shape
    if own is None:
        own, me_arr = recv, jnp.zeros((1,), jnp.int32)
        plain = True
    else:
        plain = False

    def body(me_ref, r_ref, own_ref, o_ref):
        if plain:
            gg = r_ref[0].astype(F32)
            for s in range(1, N_DEV):
                gg = gg + r_ref[s].astype(F32)
            o_ref[...] = gg
        else:
            o_ref[...] = _slot_sum(me_ref[0], own_ref[...], r_ref)

    return pl.pallas_call(
        body, name="sum_slots",
        grid_spec=pltpu.PrefetchScalarGridSpec(
            num_scalar_prefetch=1, grid=(1,),
            in_specs=[pl.BlockSpec((N_DEV, r, c), lambda i, me: (0, 0, 0)),
                      pl.BlockSpec((None, r, c), lambda i, me: (me[0], 0, 0))],
            out_specs=pl.BlockSpec((r, c), lambda i, me: (0, 0))),
        out_shape=jax.ShapeDtypeStruct((r, c), F32),
        compiler_params=_cp("arbitrary"),
    )(me_arr, recv, own)


_FLIPS = [(0, 0, 1), (0, 1, 0), (0, 1, 1), (1, 0, 0), (1, 0, 1), (1, 1, 0), (1, 1, 1)]
_ANY = pl.BlockSpec(memory_space=pl.ANY)


def _mesh_place():
    x, y, c = lax.axis_index("x"), lax.axis_index("y"), lax.axis_index("c")
    me = 4 * x + 2 * y + c
    peers = [((x + fx) % 2, (y + fy) % 2, (c + fc) % 2) for fx, fy, fc in _FLIPS]
    return me, peers


def place_own(src, l, dtype, me_arr):
    _, r, c = src.shape
    tr = r
    for d in (512, 352, 256, 128, 64, 32, 16, 8):
        if r % d == 0 and d * c * 4 <= (2 << 20):
            tr = d
            break

    def body(me_ref, s_ref, o_ref):
        o_ref[...] = s_ref[...].astype(dtype)

    return pl.pallas_call(
        body, name="place_own",
        grid_spec=pltpu.PrefetchScalarGridSpec(
            num_scalar_prefetch=1, grid=(r // tr,),
            in_specs=[pl.BlockSpec((None, tr, c), lambda i, me: (l, i, 0))],
            out_specs=pl.BlockSpec((None, tr, c), lambda i, me: (me[0], i, 0))),
        out_shape=jax.ShapeDtypeStruct((N_DEV, r, c), dtype),
        compiler_params=_cp("arbitrary"),
    )(me_arr, src)


_HBM = pl.BlockSpec(memory_space=pltpu.HBM)
_SEMS = pl.BlockSpec(memory_space=pltpu.SEMAPHORE)
_EFFECT = pltpu.SideEffectType.DATAFLOW_SIDE_EFFECTING


def exchange_start(mode, arrays, name):
    n = len(arrays)
    gather = mode == "gather"
    ns = 0 if gather else n
    zones = list(arrays) if gather else [lax.empty(a.shape, a.dtype) for a in arrays]
    ops = ([] if gather else list(arrays)) + zones

    def body(*refs):
        srcs, lands = refs[:ns], refs[ns:ns + n]
        send_sems, recv_sems = refs[ns + n], refs[ns + n + 1]
        token = refs[-1]
        me, peers = _mesh_place()
        ids = [4 * p[0] + 2 * p[1] + p[2] for p in peers]
        for j in range(n):
            for k in range(N_DEV - 1):
                src = lands[j].at[me] if gather else srcs[j].at[ids[k]]
                pltpu.make_async_remote_copy(src_ref=src, dst_ref=lands[j].at[me],
                                             send_sem=send_sems.at[j * (N_DEV - 1) + k],
                                             recv_sem=recv_sems.at[j * (N_DEV - 1) + k], device_id=peers[k],
                                             device_id_type=pl.DeviceIdType.MESH).start()
        token[...] = jnp.zeros_like(token)

    nsem = n * (N_DEV - 1)
    res = pl.pallas_call(
        body, name=name,
        in_specs=[_HBM] * (ns + n),
        out_specs=(_SEMS, _SEMS, *[_HBM] * (ns + n), pl.BlockSpec(memory_space=pltpu.VMEM)),
        out_shape=(pltpu.SemaphoreType.DMA((nsem,)), pltpu.SemaphoreType.DMA((nsem,)),
                   *[pltpu.HBM(a.shape, a.dtype) for a in ops], jax.ShapeDtypeStruct((8, BLK), F32)),
        input_output_aliases={i: 2 + i for i in range(ns + n)},
        compiler_params=pltpu.CompilerParams(has_side_effects=_EFFECT),
    )(*[pltpu.with_memory_space_constraint(a, pltpu.HBM) for a in ops])
    return dict(gather=gather, send=res[0], recv=res[1], srcs=list(res[2:2 + ns]), lands=list(res[2 + ns:2 + ns + n]),
                token=res[-1])


def exchange_wait(hd, idxs, name, after):
    gather = hd["gather"]
    n = len(idxs)
    ns = 0 if gather else n
    ops = ([] if gather else [hd["srcs"][j] for j in idxs]) + [hd["lands"][j] for j in idxs]

    def body(*refs):
        srcs, lands = refs[:ns], refs[ns:ns + n]
        send_sems, recv_sems = refs[ns + n], refs[ns + n + 1]
        me, peers = _mesh_place()
        ids = [4 * p[0] + 2 * p[1] + p[2] for p in peers]
        for p, j in enumerate(idxs):
            for k in range(N_DEV - 1):
                src = lands[p].at[me] if gather else srcs[p].at[ids[k]]
                cp = pltpu.make_async_remote_copy(src_ref=src, dst_ref=lands[p].at[ids[k]],
                                                  send_sem=send_sems.at[j * (N_DEV - 1) + k],
                                                  recv_sem=recv_sems.at[j * (N_DEV - 1) + k], device_id=peers[k],
                                                  device_id_type=pl.DeviceIdType.MESH)
                cp.wait_send()
                cp.wait_recv()

    res = pl.pallas_call(
        body, name=name,
        in_specs=[_HBM] * (ns + n) + [_SEMS, _SEMS, _ANY],
        out_specs=[_HBM] * (ns + n),
        out_shape=[pltpu.HBM(a.shape, a.dtype) for a in ops],
        input_output_aliases={i: i for i in range(ns + n)},
        compiler_params=pltpu.CompilerParams(has_side_effects=_EFFECT),
    )(*ops, hd["send"], hd["recv"], after)
    return list(res[:ns]), list(res[ns:])


def _pad_cols(a, n):
    return jnp.pad(a, ((0, 0),) * (a.ndim - 1) + ((0, n - a.shape[-1]),))


def w_in_to_padded(w):
    z = lambda n: jnp.zeros(w.shape[:-1] + (n,), w.dtype)
    return jnp.concatenate([
        w[..., 0:1280], w[..., 1288:2056], w[..., 2060:2316], w[..., 2316:2444],
        w[..., 1280:1288], w[..., 2056:2060], z(SM_KR - SM_F - FOX_H), w[..., 2444:2476], z(BLK - SM_KR - MLA_ROPE)], axis=-1)


def w_in_from_padded(g):
    s = C_SM
    return jnp.concatenate([
        g[..., 0:1280], g[..., s + SM_DT:s + SM_DT + 8], g[..., 1280:2048], g[..., s + SM_F:s + SM_F + 4],
        g[..., 2048:2304], g[..., 2304:2432], g[..., s + SM_KR:s + SM_KR + MLA_ROPE]], axis=-1)


def _unshard_cols(gth):
    n, r, c = gth.shape
    return jnp.transpose(gth, (1, 0, 2)).reshape(r, n * c)


def _shard_cols(full):
    r, nc = full.shape
    return jnp.transpose(full.reshape(r, N_DEV, nc // N_DEV), (1, 0, 2))


def mla_weights(uq_g, ukv_g):
    uq = _unshard_cols(uq_g)
    dqh = MLA_NOPE + MLA_ROPE
    wq = jnp.concatenate([_pad_cols(uq[:, dqh * h:dqh * (h + 1)], BLK) for h in range(MLA_H)], axis=1)
    wk = jnp.concatenate([_pad_cols(ukv_g[2 * h], BLK) for h in range(MLA_H)], axis=1)
    wv = jnp.concatenate([ukv_g[2 * h + 1] for h in range(MLA_H)], axis=1)
    return wq, wk, wv


def mla_weight_grads(dwq, dwk, dwv):
    dqh = MLA_NOPE + MLA_ROPE
    duq = _shard_cols(jnp.concatenate([dwq[:, BLK * h:BLK * h + dqh] for h in range(MLA_H)], axis=1))
    parts = []
    for h in range(MLA_H):
        parts += [dwk[:, BLK * h:BLK * h + MLA_NOPE], dwv[:, MLA_V * h:MLA_V * (h + 1)]]
    return duq, jnp.stack(parts, axis=0)


def rope_tables(t):
    pos = (jnp.arange(t, dtype=jnp.int32) - PAD).astype(F32)
    inv_freq = 1.0 / (10000.0 ** (jnp.arange(0, MLA_ROPE, 2, dtype=F32) / MLA_ROPE))
    ang = pos[:, None] * inv_freq[None, :]
    cos, sin = jnp.cos(ang), jnp.sin(ang)
    one, zero = jnp.ones((t, SM_KR), F32), jnp.zeros((t, SM_KR), F32)
    tail = BLK - SM_KR - MLA_ROPE
    cosq = jnp.concatenate([one, cos, cos, jnp.ones((t, tail), F32)], axis=1)
    sinq = jnp.concatenate([zero, -sin, sin, jnp.zeros((t, tail), F32)], axis=1)
    return cosq, sinq


def _lanes(v, off=0):
    return jnp.pad(v.astype(F32), (off, BLK - off - v.shape[0]))[None, :]


def layer_fwd(x, ln, hb, getw, tabs):
    sv = {"h0b": hb}
    W = dict(getw("ffn1", hb))
    ln1 = (W["ln1_g"], W["ln1_b"])
    u, v, r1, h1b = ffn_fwd_seq(x, ln, W["g1"], W["u1"], W["d1"], ln1)
    sv.update(u1=u, v1=v, r1=r1, h1b=h1b)
    W.update(getw("mix", h1b))
    ln2 = (W["ln2_g"], W["ln2_b"])
    proj = mm_nn(h1b, W["w_in"])
    xa = conv_fwd(proj, W["conv_w"], W["conv_b"])
    y_ssd, sprev = ssd_fwd(xa, proj, W["dtb"], W["alog"], W["dskip"], W["normg"])
    c_col, c_row = fox_pre(proj, W["fb"])
    y_fox, lse_f = attn_fwd(proj, proj, proj, C_FQ // 256, C_FK // 256, C_FV // 256, FOX_H, FOX_DH, FOX_DH,
                            FOX_DH ** -0.5, c_col, c_row, SM_F)
    q, k, vv, cqn, ckvn = mla_pre(proj, W["qg"], W["kvg"], W["wq"], W["wk"], W["wv"], *tabs)
    y_mla, lse_m = attn_fwd(q, k, vv, 0, 0, 0, MLA_H, BLK, MLA_V, (MLA_NOPE + MLA_ROPE) ** -0.5)
    mixcat = jnp.concatenate([y_ssd, y_fox, y_mla], axis=1)
    r2, h2b = mm_res_ln(mixcat, W["w_out"], r1, ln1, ln2)
    sv.update(proj=proj, xa=xa, sprev=sprev, c_col=c_col, c_row=c_row, lse_f=lse_f, q=q, k=k, v=vv, cqn=cqn, ckvn=ckvn,
              lse_m=lse_m, mixcat=mixcat, r2=r2, h2b=h2b)
    W.update(getw("ffn2", h2b))
    ln3 = (W["ln3_g"], W["ln3_b"])
    u, v, r3, h3b = ffn_fwd_seq(r2, ln2, W["g2"], W["u2"], W["d2"], ln3)
    sv.update(u2=u, v2=v, r3=r3, W=W)
    return r3, ln3, h3b, sv


def ffn_bwd(parts, r, gamma, hb_in, u, v, wg, wu, wd, after=None):
    dr, dfb, dg, db = ln_bwd(parts, r, gamma, 0.5, after)
    dh, dwg, dwu, dwd = ffn_bwd_seq(dfb, hb_in, u, v, wg, wu, wd)
    return dr, dh, dict(d=dwd, g=dwg, u=dwu, ln_g=dg, ln_b=db)


def layer_bwd(parts, sv, emit, tabs, after):
    G = {}
    W = sv["W"]
    dr3, dh2f, g2 = ffn_bwd(parts, sv["r3"], W["ln3_g"], sv["h2b"], sv["u2"], sv["v2"], W["g2"], W["u2"], W["d2"], after)
    G.update(g2=g2["g"], u2=g2["u"], d2=g2["d"], ln3_g=g2["ln_g"], ln3_b=g2["ln_b"])
    tok = emit("ffn2", G)
    dr2, dmixb, G["ln2_g"], G["ln2_b"] = ln_bwd([(dr3, ALPHA), (dh2f, 1.0)], sv["r2"], W["ln2_g"], 1.0, tok)
    dmc = mm_nt_reduce([(dmixb[None], W["w_out"][None])], D)
    G["w_out"] = mm_tn(sv["mixcat"][None], dmixb[None])[0]
    proj = sv["proj"]
    dxa, dz, dsm, G["normg"], G["dskip"], G["alog"], G["dtb"] = ssd_bwd(
        dmc, sv["xa"], proj, sv["sprev"], W["dtb"], W["alog"], W["dskip"], W["normg"])
    dxbc, G["conv_w"], G["conv_b"] = conv_bwd(dxa, proj, W["conv_w"], W["conv_b"])
    dfq, dfk, dfv, dcq, dck = attn_bwd(proj, proj, proj, dmc, sv["lse_f"], C_FQ // 256, C_FK // 256, C_FV // 256, 2,
                                       FOX_H, FOX_DH, FOX_DH, FOX_DH ** -0.5, sv["c_col"], sv["c_row"], SM_F)
    dsm, G["fb"] = fox_pre_bwd(dcq, dck, proj, W["fb"], dsm)
    dq, dk, dv = attn_bwd(sv["q"], sv["k"], sv["v"], dmc, sv["lse_m"], 0, 0, 0, 3, MLA_H, BLK, MLA_V,
                          (MLA_NOPE + MLA_ROPE) ** -0.5)
    dcql, dckv, dsm, G["wq"], G["wk"], G["wv"], G["qg"], G["kvg"] = mla_pre_bwd(
        dq, dk, dv, proj, sv["cqn"], sv["ckvn"], W["qg"], W["kvg"], W["wq"], W["wk"], W["wv"], *tabs, dsm)
    dproj = jnp.concatenate([dz, dxbc, dfq, dfk, dfv, dcql, dckv, dsm], axis=1).astype(BF16)
    dh1p = mm_nt_reduce([(dproj[None], W["w_in"][None])], D)
    G["w_in"] = mm_tn(sv["h1b"][None], dproj[None])[0]
    tok = emit("mix", G)
    dr1, dh0f, g1 = ffn_bwd([(dr2, ALPHA), (dh1p, 1.0)], sv["r1"], W["ln1_g"], sv["h0b"], sv["u1"], sv["v1"],
                            W["g1"], W["u1"], W["d1"], tok)
    G.update(g1=g1["g"], u1=g1["u"], d1=g1["d"], ln1_g=g1["ln_g"], ln1_b=g1["ln_b"])
    tok = emit("ffn1", G)
    return [(dr1, ALPHA), (dh0f, 1.0)], G, tok


def local_step(x, target, meta_full, getw, emit):
    t = x.shape[0] + BLK
    tabs = rope_tables(t)
    xr, hb = build_h0(meta_full, x)
    ln = None
    saved = []
    for l in range(NL):
        xr, ln, hb, sv = layer_fwd(xr, ln, hb, functools.partial(getw, l), tabs)
        saved.append(sv)
    dy, loss = loss_head(xr, ln, target)
    parts = [(dy, 1.0)]
    grads = [None] * NL
    tok = None
    for l in range(NL - 1, -1, -1):
        parts, grads[l], tok = layer_bwd(parts, saved[l], functools.partial(emit, l), tabs, tok)
    gx, gmeta = final_add(parts[0][0], parts[1][0], tok)
    return loss, gx, gmeta, grads


_SMALL = ["ln1_g", "ln1_b", "ln2_g", "ln2_b", "ln3_g", "ln3_b", "conv_b", "ssd_norm_g", "mla_q_norm_g",
          "mla_kv_norm_g", "dt_bias", "a_log", "d_skip", "fox_f_b"]
_SMALL_ROWS = 16
_BIG = ["ffn1_w_gate", "ffn1_w_up", "ffn1_w_down", "w_in", "conv_w", "mla_w_uq", "mla_w_ukv", "w_out",
        "ffn2_w_gate", "ffn2_w_up", "ffn2_w_down"]
_NAMES = ["meta", "ffn1_w_gate", "ffn1_w_up", "ffn1_w_down", "ln1_g", "ln1_b", "w_in", "conv_w", "conv_b", "dt_bias",
          "a_log", "d_skip", "ssd_norm_g", "fox_f_b", "mla_q_norm_g", "mla_w_uq", "mla_kv_norm_g", "mla_w_ukv", "w_out",
          "ln2_g", "ln2_b", "ffn2_w_gate", "ffn2_w_up", "ffn2_w_down", "ln3_g", "ln3_b"]


def pack_small(p):
    rows = []
    for l in range(NL):
        for n in _SMALL:
            rows.append(_pad_cols(p[n][l][None, :].astype(F32), D))
        rows.append(jnp.zeros((_SMALL_ROWS - len(_SMALL), D), F32))
    return jnp.concatenate(rows, axis=0)


def unpack_small(a, like):
    out = {}
    for i, n in enumerate(_SMALL):
        out[n] = jnp.stack([a[l * _SMALL_ROWS + i, :like[n].shape[1]] for l in range(NL)], axis=0)
    return out


_STAGES = {"ffn1": ["ffn1_w_gate", "ffn1_w_up", "ffn1_w_down"],
           "mix": ["w_in", "conv_w", "mla_w_uq", "mla_w_ukv", "w_out"],
           "ffn2": ["ffn2_w_gate", "ffn2_w_up", "ffn2_w_down"]}


_FFN_T = ("ffn1_w_gate", "ffn1_w_up", "ffn2_w_gate", "ffn2_w_up")


def stage_weights(l, stage, g, rep):
    if stage != "mix":
        i = stage[3]
        return {"g" + i: g[f"ffn{i}_w_gate"].reshape(D_FF, D), "u" + i: g[f"ffn{i}_w_up"].reshape(D_FF, D),
                "d" + i: g[f"ffn{i}_w_down"].reshape(D_FF, D),
                "ln1_g" if i == "1" else "ln3_g": rep["ln1_g" if i == "1" else "ln3_g"][l][None, :],
                "ln1_b" if i == "1" else "ln3_b": rep["ln1_b" if i == "1" else "ln3_b"][l][None, :]}
    W = {}
    W["w_in"] = g["w_in"].reshape(D, N_INP)
    W["w_out"] = g["w_out"].reshape(D, D)
    W["wq"], W["wk"], W["wv"] = mla_weights(g["mla_w_uq"], g["mla_w_ukv"])
    W["conv_w"] = _unshard_cols(g["conv_w"])
    for k in ("ln2_g", "ln2_b", "conv_b"):
        W[k] = rep[k][l][None, :]
    W["normg"] = rep["ssd_norm_g"][l][None, :]
    W["qg"] = rep["mla_q_norm_g"][l][None, :]
    W["kvg"] = rep["mla_kv_norm_g"][l][None, :]
    W["dtb"] = _lanes(rep["dt_bias"][l], SM_DT)
    W["alog"] = _lanes(rep["a_log"][l], SM_DT)
    W["dskip"] = _lanes(rep["d_skip"][l], SM_DT)
    W["fb"] = _lanes(rep["fox_f_b"][l], SM_F)
    return W


def small_grads(G):
    return {"ln1_g": G["ln1_g"][0], "ln1_b": G["ln1_b"][0], "ln2_g": G["ln2_g"][0], "ln2_b": G["ln2_b"][0],
            "ln3_g": G["ln3_g"][0], "ln3_b": G["ln3_b"][0], "conv_b": G["conv_b"][0], "ssd_norm_g": G["normg"][0],
            "mla_q_norm_g": G["qg"][0], "mla_kv_norm_g": G["kvg"][0], "dt_bias": G["dtb"][0, :SSD_H],
            "a_log": G["alog"][0, :SSD_H], "d_skip": G["dskip"][0, :SSD_H], "fox_f_b": G["fb"][0, SM_F:SM_F + FOX_H]}


def big_grads(G, stage):
    if stage != "mix":
        i = stage[-1]
        return {f"ffn{i}_w_{k}": G[k[0] + i].reshape(N_DEV, HS, D) for k in ("gate", "up", "down")}
    duq, dukv = mla_weight_grads(G["wq"], G["wk"], G["wv"])
    return {"w_in": G["w_in"].reshape(N_DEV, D // N_DEV, N_INP), "w_out": G["w_out"].reshape(N_DEV, D // N_DEV, D),
            "mla_w_uq": duq, "mla_w_ukv": dukv, "conv_w": _shard_cols(G["conv_w"])}


def kernel(x, meta, ffn1_w_gate, ffn1_w_up, ffn1_w_down, ln1_g, ln1_b, w_in, conv_w, conv_b, dt_bias, a_log, d_skip, ssd_norm_g, fox_f_b, mla_q_norm_g, mla_w_uq, mla_kv_norm_g, mla_w_ukv, w_out, ln2_g, ln2_b, ffn2_w_gate, ffn2_w_up, ffn2_w_down, ln3_g, ln3_b, loss_target, m_meta, m_ffn1_w_gate, m_ffn1_w_up, m_ffn1_w_down, m_ln1_g, m_ln1_b, m_w_in, m_conv_w, m_conv_b, m_dt_bias, m_a_log, m_d_skip, m_ssd_norm_g, m_fox_f_b, m_mla_q_norm_g, m_mla_w_uq, m_mla_kv_norm_g, m_mla_w_ukv, m_w_out, m_ln2_g, m_ln2_b, m_ffn2_w_gate, m_ffn2_w_up, m_ffn2_w_down, m_ln3_g, m_ln3_b, v_meta, v_ffn1_w_gate, v_ffn1_w_up, v_ffn1_w_down, v_ln1_g, v_ln1_b, v_w_in, v_conv_w, v_conv_b, v_dt_bias, v_a_log, v_d_skip, v_ssd_norm_g, v_fox_f_b, v_mla_q_norm_g, v_mla_w_uq, v_mla_kv_norm_g, v_mla_w_ukv, v_w_out, v_ln2_g, v_ln2_b, v_ffn2_w_gate, v_ffn2_w_up, v_ffn2_w_down, v_ln3_g, v_ln3_b):
    vals = (meta, ffn1_w_gate, ffn1_w_up, ffn1_w_down, ln1_g, ln1_b, w_in, conv_w, conv_b, dt_bias, a_log, d_skip, ssd_norm_g, fox_f_b, mla_q_norm_g, mla_w_uq, mla_kv_norm_g, mla_w_ukv, w_out, ln2_g, ln2_b, ffn2_w_gate, ffn2_w_up, ffn2_w_down, ln3_g, ln3_b)
    moms = (m_meta, m_ffn1_w_gate, m_ffn1_w_up, m_ffn1_w_down, m_ln1_g, m_ln1_b, m_w_in, m_conv_w, m_conv_b, m_dt_bias, m_a_log, m_d_skip, m_ssd_norm_g, m_fox_f_b, m_mla_q_norm_g, m_mla_w_uq, m_mla_kv_norm_g, m_mla_w_ukv, m_w_out, m_ln2_g, m_ln2_b, m_ffn2_w_gate, m_ffn2_w_up, m_ffn2_w_down, m_ln3_g, m_ln3_b)
    vars_ = (v_meta, v_ffn1_w_gate, v_ffn1_w_up, v_ffn1_w_down, v_ln1_g, v_ln1_b, v_w_in, v_conv_w, v_conv_b, v_dt_bias, v_a_log, v_d_skip, v_ssd_norm_g, v_fox_f_b, v_mla_q_norm_g, v_mla_w_uq, v_mla_kv_norm_g, v_mla_w_ukv, v_w_out, v_ln2_g, v_ln2_b, v_ffn2_w_gate, v_ffn2_w_up, v_ffn2_w_down, v_ln3_g, v_ln3_b)
    P = dict(zip(_NAMES, vals))
    M = dict(zip(_NAMES, moms))
    V = dict(zip(_NAMES, vars_))
    me = 4 * lax.axis_index("x") + 2 * lax.axis_index("y") + lax.axis_index("c")

    me_arr = me.astype(jnp.int32).reshape(1)
    for n in _FFN_T:
        P[n], M[n], V[n] = (jnp.swapaxes(a[n], 1, 2) for a in (P, M, V))
    src = dict(P)
    src["w_in"] = w_in_to_padded(P["w_in"])
    order = [("meta", 0)] + [(n, l) for l in range(NL) for names in _STAGES.values() for n in names]
    zone_of = {nl_: i for i, nl_ in enumerate(order)}
    zones = [place_own(P["meta"][None], 0, F32, me_arr)]
    zones += [place_own(src[n], l, F32 if n == "conv_w" else BF16, me_arr) for n, l in order[1:]]
    hg = exchange_start("gather", zones, "gather_start")
    meta_full = _unshard_cols(exchange_wait(hg, [0], "gather_wait_meta", hg["token"])[1][0])

    def getw(l, stage, after):
        names = _STAGES[stage]
        lands = exchange_wait(hg, [zone_of[(n, l)] for n in names], f"gather_wait_{l}_{stage}", after)[1]
        return stage_weights(l, stage, dict(zip(names, lands)), P)

    sent = {}

    def emit(l, stage, G):
        bg = big_grads(G, stage)
        sent[(l, stage)] = exchange_start("scatter", [bg[n] for n in _STAGES[stage]], f"scatter_start_{l}_{stage}")
        return sent[(l, stage)]["token"]

    loss, gx, gmeta, grads = local_step(x[0], loss_target[0], meta_full, getw, emit)

    small = jnp.concatenate([pack_small({n: jnp.stack([small_grads(g)[n] for g in grads]) for n in _SMALL}), gmeta], axis=0)
    hs = exchange_start("gather", [place_own(small[None], 0, F32, me_arr)], "small_start")

    out = {}
    after = hs["token"]
    for stage in ("ffn2", "mix", "ffn1"):
        names = _STAGES[stage]
        got = [exchange_wait(sent[(l, stage)], list(range(len(names))), f"scatter_wait_{l}_{stage}", after)
               for l in range(NL - 1, -1, -1)][::-1]
        for i, n in enumerate(names):
            own = [got[l][0][i] for l in range(NL)]
            recv = [got[l][1][i] for l in range(NL)]
            if n == "w_in":
                g = jnp.stack([w_in_from_padded(sum_slots(recv[l], own[l], me_arr)) for l in range(NL)])
                out[n] = (g,) + adamw(P[n], M[n], V[n], g=g)
            else:
                out[n] = adamw(P[n], M[n], V[n], recv=recv, own=own, me_arr=me_arr)
                if n in _FFN_T:
                    out[n] = tuple(jnp.swapaxes(a, 1, 2) for a in out[n])
        after = out[names[-1]][1]
    gsmall = sum_slots(exchange_wait(hs, [0], "small_wait", after)[1][0])
    gm = lax.dynamic_slice(gsmall[NL * _SMALL_ROWS:], (0, me * (D // N_DEV)), (N_META, D // N_DEV))
    out["meta"] = (gm,) + adamw(P["meta"], M["meta"], V["meta"], g=gm)
    gs = gsmall[:NL * _SMALL_ROWS]
    sd, sm_, sv_ = adamw(pack_small(P), pack_small(M), pack_small(V), g=gs)
    ups = [unpack_small(a, P) for a in (gs, sd, sm_, sv_)]
    for n in _SMALL:
        out[n] = tuple(u[n] for u in ups)

    loss_all = lax.psum(loss[0, 0], ("x", "y", "c"))
    flat = [loss_all, gx[None]]
    for k in range(4):
        flat += [out[n][k] for n in _NAMES]
    return tuple(flat)
```

```python
import functools

import jax
import jax.numpy as jnp
from jax import lax
from jax.experimental import pallas as pl
from jax.experimental.pallas import tpu as pltpu

F32, BF16 = jnp.float32, jnp.bfloat16
HI = lax.Precision.HIGHEST

N_DEV = 8
D = 1024
NL = 2
N_META = 16
BLK = 128
PAD = BLK - N_META
D_FF = 2816
HS = D_FF // N_DEV
SSD_H, SSD_P, SSD_N, SSD_G = 8, 64, 64, 2
SSD_D = SSD_H * SSD_P
CONV_K = 4
CONV_D = SSD_D + 2 * SSD_G * SSD_N
FOX_H, FOX_DH = 4, 64
MLA_H, MLA_QL, MLA_KVL, MLA_NOPE, MLA_ROPE, MLA_V = 4, 256, 128, 64, 32, 64
N_IN = 2476
C_Z, C_XBC, C_FQ, C_FK, C_FV, C_CQ, C_CKV, C_SM, N_INP = 0, 512, 1280, 1536, 1792, 2048, 2304, 2432, 2560
SM_DT, SM_F, SM_KR = 0, 8, 64
ALPHA = (2 * NL) ** 0.25
EPS = 1e-5
NEG = -1e30
LR, B1, B2, AEPS, WD, STEP = 0.001, 0.9, 0.999, 1e-08, 0.01, 10
VMEM_MB = 56


def _cp(*sem):
    return pltpu.CompilerParams(dimension_semantics=sem, vmem_limit_bytes=VMEM_MB << 20)


def _nn(a, b):
    return lax.dot_general(a, b, (((1,), (0,)), ((), ())), preferred_element_type=F32)


def _nt(a, b):
    return lax.dot_general(a, b, (((1,), (1,)), ((), ())), preferred_element_type=F32)


def _tn(a, b):
    return lax.dot_general(a, b, (((0,), (0,)), ((), ())), preferred_element_type=F32)


def _nn_hi(a, b):
    return lax.dot_general(a, b, (((1,), (0,)), ((), ())), precision=HI, preferred_element_type=F32)


def _row_tile(t):
    for d in range(640, 15, -16):
        if t % d == 0:
            return d
    raise ValueError(t)


def _sig(x):
    return 1.0 / (1.0 + jnp.exp(-x))


def _tri(lower=True):
    r = lax.broadcasted_iota(jnp.int32, (BLK, BLK), 0)
    c = lax.broadcasted_iota(jnp.int32, (BLK, BLK), 1)
    return (r >= c) if lower else (r <= c)


def build_h0(meta_full, x):
    s = x.shape[0]
    nb = s // BLK + 1

    def body(m_ref, x_ref, h_ref, hb_ref):
        i = pl.program_id(0)

        @pl.when(i == 0)
        def _():
            h = jnp.concatenate([jnp.zeros((PAD, D), F32), m_ref[...]], axis=0)
            h_ref[...] = h
            hb_ref[...] = h.astype(BF16)

        @pl.when(i > 0)
        def _():
            h_ref[...] = x_ref[...]
            hb_ref[...] = x_ref[...].astype(BF16)

    return pl.pallas_call(
        body, name="build_h0", grid=(nb,),
        in_specs=[pl.BlockSpec((N_META, D), lambda i: (0, 0)),
                  pl.BlockSpec((BLK, D), lambda i: (jnp.maximum(i - 1, 0), 0))],
        out_specs=[pl.BlockSpec((BLK, D), lambda i: (i, 0))] * 2,
        out_shape=[jax.ShapeDtypeStruct((nb * BLK, D), F32), jax.ShapeDtypeStruct((nb * BLK, D), BF16)],
        compiler_params=_cp("arbitrary"),
    )(meta_full, x)


FT = 256


def _layer_norm(r, gamma, beta):
    mu = jnp.mean(r, axis=1, keepdims=True)
    xc = r - mu
    var = jnp.mean(xc * xc, axis=1, keepdims=True)
    return xc * lax.rsqrt(var + EPS) * gamma + beta


def ffn_fwd(hb, res, wg, wu, wd, gamma, beta):
    t = hb.shape[0]
    f = wg.shape[0]
    tm = _row_tile(t)
    nj = f // FT

    def body(h_ref, res_ref, wg_ref, wu_ref, wd_ref, g_ref, be_ref, u_ref, v_ref, r_ref, y_ref, yb_ref, acc, us, vs):
        j = pl.program_id(1)

        def up():
            h = h_ref[...]
            u = _nt(h, wg_ref[...])
            v = _nt(h, wu_ref[...])
            u_ref[...] = u.astype(BF16)
            v_ref[...] = v.astype(BF16)
            return u, v

        def down():
            u, v = us[...], vs[...]
            return _nn((u * _sig(u) * v).astype(BF16), wd_ref[...])

        @pl.when(j == 0)
        def _():
            us[...], vs[...] = up()
            acc[...] = jnp.zeros_like(acc)

        @pl.when((j > 0) & (j < nj))
        def _():
            d = down()
            u, v = up()
            acc[...] += d
            us[...] = u
            vs[...] = v

        @pl.when(j == nj)
        def _():
            r = ALPHA * res_ref[...] + 0.5 * (acc[...] + down())
            y = _layer_norm(r, g_ref[...], be_ref[...])
            r_ref[...] = r
            y_ref[...] = y
            yb_ref[...] = y.astype(BF16)

    row = pl.BlockSpec((tm, D), lambda i, j: (i, 0))
    vec = pl.BlockSpec((1, D), lambda i, j: (0, 0))
    wup = pl.BlockSpec((FT, D), lambda i, j: (jnp.minimum(j, nj - 1), 0))
    wdn = pl.BlockSpec((FT, D), lambda i, j: (jnp.maximum(j - 1, 0), 0))
    act = pl.BlockSpec((tm, FT), lambda i, j: (i, jnp.minimum(j, nj - 1)))
    return pl.pallas_call(
        body, name="ffn_fwd", grid=(t // tm, nj + 1),
        in_specs=[row, row, wup, wup, wdn, vec, vec],
        out_specs=[act, act, row, row, row],
        out_shape=[jax.ShapeDtypeStruct((t, f), BF16), jax.ShapeDtypeStruct((t, f), BF16),
                   jax.ShapeDtypeStruct((t, D), F32), jax.ShapeDtypeStruct((t, D), F32),
                   jax.ShapeDtypeStruct((t, D), BF16)],
        scratch_shapes=[pltpu.VMEM((tm, D), F32), pltpu.VMEM((tm, FT), F32), pltpu.VMEM((tm, FT), F32)],
        compiler_params=_cp("arbitrary", "arbitrary"),
    )(hb, res, wg, wu, wd, gamma, beta)


def ffn_bwd_act(dfb, u, v, wg, wu, wd):
    t, f = u.shape
    tm = _row_tile(t)

    nj = f // FT

    def body(df_ref, u_ref, v_ref, wg_ref, wu_ref, wd_ref, du_ref, dv_ref, dh_ref, das):
        j = pl.program_id(1)

        def first():
            return _nt(df_ref[...], wd_ref[...])

        def second():
            da = das[...]
            uu = u_ref[...].astype(F32)
            sg = _sig(uu)
            du = (da * v_ref[...].astype(F32) * (sg * (1.0 + uu * (1.0 - sg)))).astype(BF16)
            dv = (da * uu * sg).astype(BF16)
            du_ref[...] = du
            dv_ref[...] = dv
            return _nn(du, wg_ref[...]) + _nn(dv, wu_ref[...])

        @pl.when(j == 0)
        def _():
            das[...] = first()
            dh_ref[...] = jnp.zeros_like(dh_ref)

        @pl.when((j > 0) & (j < nj))
        def _():
            tot = second()
            da = first()
            dh_ref[...] += tot
            das[...] = da

        @pl.when(j == nj)
        def _():
            dh_ref[...] += second()

    row = pl.BlockSpec((tm, D), lambda i, j: (i, 0))
    wfirst = pl.BlockSpec((FT, D), lambda i, j: (jnp.minimum(j, nj - 1), 0))
    wsecond = pl.BlockSpec((FT, D), lambda i, j: (jnp.maximum(j - 1, 0), 0))
    act = pl.BlockSpec((tm, FT), lambda i, j: (i, jnp.maximum(j - 1, 0)))
    return pl.pallas_call(
        body, name="ffn_bwd_act", grid=(t // tm, nj + 1),
        in_specs=[row, act, act, wsecond, wsecond, wfirst],
        out_specs=[act, act, row],
        out_shape=[jax.ShapeDtypeStruct((t, f), BF16), jax.ShapeDtypeStruct((t, f), BF16),
                   jax.ShapeDtypeStruct((t, D), F32)],
        scratch_shapes=[pltpu.VMEM((tm, FT), F32)],
        compiler_params=_cp("arbitrary", "arbitrary"),
    )(dfb, u, v, wg, wu, wd)


def ffn_fwd_seq(x, ln_in, wg, wu, wd, ln_out):
    t = x.shape[0]
    f = wg.shape[0]
    nj, nr = f // FT, t // _row_tile(t)
    rc = t // nr
    plain = ln_in is None
    gi, bi = ln_out if plain else ln_in

    def body(x_hbm, gi_ref, bi_ref, go_ref, bo_ref, wg_ref, wu_ref, wd_ref, u_ref, v_ref, r_hbm, yb_hbm,
             acc, hbs, xbuf, sem_in, sem_out):
        j = pl.program_id(0)

        @pl.when(j == 0)
        def _():
            def fetch(k):
                return pltpu.make_async_copy(x_hbm.at[pl.ds(k * rc, rc)], xbuf.at[k % 2], sem_in.at[k % 2])

            fetch(0).start()
            for k in range(nr):
                if k + 1 < nr:
                    fetch(k + 1).start()
                fetch(k).wait()
                h = xbuf[k % 2]
                if not plain:
                    h = _layer_norm(h, gi_ref[...], bi_ref[...])
                acc[k * rc:(k + 1) * rc, :] = ALPHA * h
                hbs[k * rc:(k + 1) * rc, :] = h.astype(BF16)

        for k in range(nr):
            sl = slice(k * rc, (k + 1) * rc)
            h = hbs[sl, :]
            u = _nt(h, wg_ref[...])
            v = _nt(h, wu_ref[...])
            u_ref[sl, :] = u.astype(BF16)
            v_ref[sl, :] = v.astype(BF16)
            acc[sl, :] += _nn((0.5 * u * _sig(u) * v).astype(BF16), wd_ref[...])

        @pl.when(j == nj - 1)
        def _():
            r_cp = pltpu.make_async_copy(acc, r_hbm, sem_out.at[0])
            r_cp.start()
            for k in range(nr):
                sl = slice(k * rc, (k + 1) * rc)
                hbs[sl, :] = _layer_norm(acc[sl, :], go_ref[...], bo_ref[...]).astype(BF16)
            y_cp = pltpu.make_async_copy(hbs, yb_hbm, sem_out.at[1])
            y_cp.start()
            r_cp.wait()
            y_cp.wait()

    vec = pl.BlockSpec((1, D), lambda j: (0, 0))
    wsp = pl.BlockSpec((FT, D), lambda j: (j, 0))
    act = pl.BlockSpec((None, t, FT), lambda j: (j, 0, 0))
    return pl.pallas_call(
        body, name="ffn_fwd_seq", grid=(nj,),
        in_specs=[_ANY, vec, vec, vec, vec, wsp, wsp, wsp],
        out_specs=[act, act, _ANY, _ANY],
        out_shape=[jax.ShapeDtypeStruct((nj, t, FT), BF16), jax.ShapeDtypeStruct((nj, t, FT), BF16),
                   jax.ShapeDtypeStruct((t, D), F32), jax.ShapeDtypeStruct((t, D), BF16)],
        scratch_shapes=[pltpu.VMEM((t, D), F32), pltpu.VMEM((t, D), BF16), pltpu.VMEM((2, rc, D), F32),
                        pltpu.SemaphoreType.DMA((2,)), pltpu.SemaphoreType.DMA((2,))],
        compiler_params=_cp("arbitrary"),
    )(x, gi, bi, ln_out[0], ln_out[1], wg, wu, wd)


def ffn_bwd_seq(parts, r, gamma, hb, u, v, wg, wu, wd, after=None):
    nj, t, _ = u.shape
    f = nj * FT
    nr = t // _row_tile(t)
    rc = t // nr
    nc = t // BLK
    scales = [s for _, s in parts]
    npart = len(parts)
    extra = [] if after is None else [after]

    def body(*refs):
        refs = refs[len(extra):]
        p_hbm, refs = refs[:npart], refs[npart:]
        (r_hbm, g_ref, hb_hbm, u_ref, v_ref, wg_ref, wu_ref, wd_ref, dh_hbm, dwg_ref, dwu_ref, dwd_ref, dg_ref, db_ref,
         dfs, hbt, dft, dhacc, dus, dvs, acs, pbuf, rbuf, hbuf, sems, sem_out) = refs
        j = pl.program_id(0)

        @pl.when(j == 0)
        def _():
            def fetch(c):
                rows = pl.ds(c * BLK, BLK)
                cps = [pltpu.make_async_copy(p_hbm[p].at[rows], pbuf.at[c % 2, p], sems.at[c % 2, p]) for p in range(npart)]
                cps.append(pltpu.make_async_copy(r_hbm.at[rows], rbuf.at[c % 2], sems.at[c % 2, npart]))
                cps.append(pltpu.make_async_copy(hb_hbm.at[rows], hbuf.at[c % 2], sems.at[c % 2, npart + 1]))
                return cps

            for cp in fetch(0):
                cp.start()
            dg = jnp.zeros((1, D), F32)
            db = jnp.zeros((1, D), F32)
            for c in range(nc):
                if c + 1 < nc:
                    for cp in fetch(c + 1):
                        cp.start()
                for cp in fetch(c):
                    cp.wait()
                sl = slice(c * BLK, (c + 1) * BLK)
                dy = scales[0] * pbuf[c % 2, 0]
                for p in range(1, npart):
                    dy += scales[p] * pbuf[c % 2, p]
                rr = rbuf[c % 2]
                xc = rr - jnp.mean(rr, axis=1, keepdims=True)
                rstd = lax.rsqrt(jnp.mean(xc * xc, axis=1, keepdims=True) + EPS)
                xh = xc * rstd
                dxh = dy * g_ref[...]
                dr = rstd * (dxh - jnp.mean(dxh, axis=1, keepdims=True) - xh * jnp.mean(dxh * xh, axis=1, keepdims=True))
                dg += jnp.sum(dy * xh, axis=0, keepdims=True)
                db += jnp.sum(dy, axis=0, keepdims=True)
                dhacc[sl, :] = ALPHA * dr
                dfc = (0.5 * dr).astype(BF16)
                dfs[sl, :] = dfc
                dft[:, sl] = dfc.T
                hbt[:, sl] = hbuf[c % 2].T
            dg_ref[...] = dg
            db_ref[...] = db

        for k in range(nr):
            sl = slice(k * rc, (k + 1) * rc)
            da = _nt(dfs[sl, :], wd_ref[...])
            uu = u_ref[sl, :].astype(F32)
            vv = v_ref[sl, :].astype(F32)
            sg = _sig(uu)
            du = (da * vv * (sg * (1.0 + uu * (1.0 - sg)))).astype(BF16)
            dv = (da * uu * sg).astype(BF16)
            dus[sl, :] = du
            dvs[sl, :] = dv
            acs[sl, :] = (uu * sg * vv).astype(BF16)
            dhacc[sl, :] += _nn(du, wg_ref[...]) + _nn(dv, wu_ref[...])
        dwg_ref[...] = _nn(hbt[...], dus[...]).T.astype(BF16)
        dwu_ref[...] = _nn(hbt[...], dvs[...]).T.astype(BF16)
        dwd_ref[...] = _nn(dft[...], acs[...]).T.astype(BF16)

        @pl.when(j == nj - 1)
        def _():
            cp = pltpu.make_async_copy(dhacc, dh_hbm, sem_out.at[0])
            cp.start()
            cp.wait()

    vec = pl.BlockSpec((1, D), lambda j: (0, 0))
    wsp = pl.BlockSpec((FT, D), lambda j: (j, 0))
    act = pl.BlockSpec((None, t, FT), lambda j: (j, 0, 0))
    return pl.pallas_call(
        body, name="ffn_bwd_seq", grid=(nj,),
        in_specs=[_ANY] * (len(extra) + npart + 1) + [vec, _ANY, act, act, wsp, wsp, wsp],
        out_specs=[_ANY, wsp, wsp, wsp, vec, vec],
        out_shape=[jax.ShapeDtypeStruct((t, D), F32)] + [jax.ShapeDtypeStruct((f, D), BF16)] * 3
        + [jax.ShapeDtypeStruct((1, D), F32)] * 2,
        scratch_shapes=[pltpu.VMEM((t, D), BF16), pltpu.VMEM((D, t), BF16), pltpu.VMEM((D, t), BF16),
                        pltpu.VMEM((t, D), F32), pltpu.VMEM((t, FT), BF16), pltpu.VMEM((t, FT), BF16),
                        pltpu.VMEM((t, FT), BF16), pltpu.VMEM((2, npart, BLK, D), F32), pltpu.VMEM((2, BLK, D), F32),
                        pltpu.VMEM((2, BLK, D), BF16), pltpu.SemaphoreType.DMA((2, npart + 2)),
                        pltpu.SemaphoreType.DMA((1,))],
        compiler_params=_cp("arbitrary"),
    )(*extra, *[p for p, _ in parts], r, gamma, hb, u, v, wg, wu, wd)


def mm_res_ln(a, b, x, ln_in, ln_out):
    t, k = a.shape
    tm = _row_tile(t)

    def body(a_ref, b_ref, x_ref, gi_ref, bi_ref, go_ref, bo_ref, r_ref, yb_ref):
        r = ALPHA * _layer_norm(x_ref[...], gi_ref[...], bi_ref[...]) + _nn(a_ref[...], b_ref[...])
        r_ref[...] = r
        yb_ref[...] = _layer_norm(r, go_ref[...], bo_ref[...]).astype(BF16)

    row = pl.BlockSpec((tm, D), lambda i: (i, 0))
    vec = pl.BlockSpec((1, D), lambda i: (0, 0))
    return pl.pallas_call(
        body, name="mm_res_ln", grid=(t // tm,),
        in_specs=[pl.BlockSpec((tm, k), lambda i: (i, 0)), pl.BlockSpec((k, D), lambda i: (0, 0)), row, vec, vec, vec, vec],
        out_specs=[row, row],
        out_shape=[jax.ShapeDtypeStruct((t, D), F32), jax.ShapeDtypeStruct((t, D), BF16)],
        compiler_params=_cp("arbitrary"),
    )(a, b, x, ln_in[0], ln_in[1], ln_out[0], ln_out[1])


def mm_nn(a, b, tn=512):
    t, k = a.shape
    n = b.shape[1]
    tm = _row_tile(t)

    def body(a_ref, b_ref, o_ref):
        o_ref[...] = _nn(a_ref[...], b_ref[...])

    return pl.pallas_call(
        body, name="mm_nn", grid=(n // tn, t // tm),
        in_specs=[pl.BlockSpec((tm, k), lambda j, i: (i, 0)), pl.BlockSpec((k, tn), lambda j, i: (0, j))],
        out_specs=pl.BlockSpec((tm, tn), lambda j, i: (i, j)),
        out_shape=jax.ShapeDtypeStruct((t, n), F32),
        compiler_params=_cp("arbitrary", "arbitrary"),
    )(a, b)


def mm_nt_reduce(pairs, n):
    g, t, _ = pairs[0][0].shape
    tm = _row_tile(t)
    npair = len(pairs)

    def body(*refs):
        o_ref = refs[-1]
        gi = pl.program_id(1)
        tot = _nt(refs[0][...], refs[1][...])
        for p in range(1, npair):
            tot += _nt(refs[2 * p][...], refs[2 * p + 1][...])

        @pl.when(gi == 0)
        def _():
            o_ref[...] = tot

        @pl.when(gi > 0)
        def _():
            o_ref[...] += tot

    in_specs, args = [], []
    for x, w in pairs:
        k = x.shape[2]
        in_specs += [pl.BlockSpec((None, tm, k), lambda i, gi: (gi, i, 0)),
                     pl.BlockSpec((None, n, k), lambda i, gi: (gi, 0, 0))]
        args += [x, w]
    return pl.pallas_call(
        body, name="mm_nt_reduce", grid=(t // tm, g),
        in_specs=in_specs, out_specs=pl.BlockSpec((tm, n), lambda i, gi: (i, 0)),
        out_shape=jax.ShapeDtypeStruct((t, n), F32),
        compiler_params=_cp("arbitrary", "arbitrary"),
    )(*args)


def mm_tn(x, y, out_dtype=BF16):
    gx, t, k = x.shape
    gy, _, n = y.shape
    g = max(gx, gy)
    tm = _row_tile(t)
    nt = t // tm

    def body(x_ref, y_ref, o_ref, acc):
        i = pl.program_id(1)

        @pl.when(i == 0)
        def _():
            acc[...] = jnp.zeros_like(acc)

        acc[...] += _tn(x_ref[...], y_ref[...])

        @pl.when(i == nt - 1)
        def _():
            o_ref[...] = acc[...].astype(out_dtype)

    return pl.pallas_call(
        body, name="mm_tn", grid=(g, nt),
        in_specs=[pl.BlockSpec((None, tm, k), (lambda gi, i: (gi, i, 0)) if gx > 1 else (lambda gi, i: (0, i, 0))),
                  pl.BlockSpec((None, tm, n), (lambda gi, i: (gi, i, 0)) if gy > 1 else (lambda gi, i: (0, i, 0)))],
        out_specs=pl.BlockSpec((None, k, n), lambda gi, i: (gi, 0, 0)),
        out_shape=jax.ShapeDtypeStruct((g, k, n), out_dtype),
        scratch_shapes=[pltpu.VMEM((k, n), F32)],
        compiler_params=_cp("arbitrary", "arbitrary"),
    )(x, y)


def ln_bwd(parts, r, gamma, out_scale, after=None):
    t = r.shape[0]
    tm = _row_tile(t)
    scales = [s for _, s in parts]
    npart = len(parts)
    extra = [] if after is None else [after]

    def body(*refs):
        refs = refs[len(extra):]
        r_ref, g_ref = refs[npart], refs[npart + 1]
        dr_ref, drb_ref, dg_ref, db_ref = refs[npart + 2:]
        i = pl.program_id(0)
        dy = scales[0] * refs[0][...]
        for p in range(1, npart):
            dy += scales[p] * refs[p][...]
        rr = r_ref[...]
        mu = jnp.mean(rr, axis=1, keepdims=True)
        xc = rr - mu
        rstd = lax.rsqrt(jnp.mean(xc * xc, axis=1, keepdims=True) + EPS)
        xh = xc * rstd
        dxh = dy * g_ref[...]
        m1 = jnp.mean(dxh, axis=1, keepdims=True)
        m2 = jnp.mean(dxh * xh, axis=1, keepdims=True)
        dr = rstd * (dxh - m1 - xh * m2)
        dr_ref[...] = dr
        drb_ref[...] = (out_scale * dr).astype(BF16)
        dg = jnp.sum(dy * xh, axis=0, keepdims=True)
        db = jnp.sum(dy, axis=0, keepdims=True)

        @pl.when(i == 0)
        def _():
            dg_ref[...] = dg
            db_ref[...] = db

        @pl.when(i > 0)
        def _():
            dg_ref[...] += dg
            db_ref[...] += db

    row = pl.BlockSpec((tm, D), lambda i: (i, 0))
    vec = pl.BlockSpec((1, D), lambda i: (0, 0))
    return pl.pallas_call(
        body, name="ln_bwd", grid=(t // tm,),
        in_specs=[_ANY] * len(extra) + [row] * (npart + 1) + [vec],
        out_specs=[row, row, vec, vec],
        out_shape=[jax.ShapeDtypeStruct((t, D), F32), jax.ShapeDtypeStruct((t, D), BF16),
                   jax.ShapeDtypeStruct((1, D), F32), jax.ShapeDtypeStruct((1, D), F32)],
        compiler_params=_cp("arbitrary"),
    )(*extra, *[p for p, _ in parts], r, gamma)


def loss_head(r, ln, target):
    t = r.shape[0]
    nb = t // BLK

    def body(r_ref, g_ref, b_ref, t_ref, dy_ref, l_ref):
        i = pl.program_id(0)

        @pl.when(i == 0)
        def _():
            dy_ref[...] = jnp.zeros_like(dy_ref)
            l_ref[...] = jnp.zeros_like(l_ref)

        @pl.when(i > 0)
        def _():
            err = _layer_norm(r_ref[...], g_ref[...], b_ref[...]) - t_ref[...]
            dy_ref[...] = err * (1.0 / D)
            l_ref[...] += (0.5 / D) * jnp.sum(err * err, keepdims=True)

    vec = pl.BlockSpec((1, D), lambda i: (0, 0))
    return pl.pallas_call(
        body, name="loss_head", grid=(nb,),
        in_specs=[pl.BlockSpec((BLK, D), lambda i: (i, 0)), vec, vec,
                  pl.BlockSpec((BLK, D), lambda i: (jnp.maximum(i - 1, 0), 0))],
        out_specs=[pl.BlockSpec((BLK, D), lambda i: (i, 0)), pl.BlockSpec((1, 1), lambda i: (0, 0))],
        out_shape=[jax.ShapeDtypeStruct((t, D), F32), jax.ShapeDtypeStruct((1, 1), F32)],
        compiler_params=_cp("arbitrary"),
    )(r, ln[0], ln[1], target)


def split_dh0(dh0, after=None):
    t = dh0.shape[0]
    nb = t // BLK
    extra = [] if after is None else [after]

    def body(*refs):
        a_ref, gx_ref, gm_ref = refs[len(extra):]
        i = pl.program_id(0)
        tot = a_ref[...]

        @pl.when(i == 0)
        def _():
            gm_ref[...] = tot[PAD:, :]

        @pl.when(i > 0)
        def _():
            gx_ref[...] = tot

    blk = pl.BlockSpec((BLK, D), lambda i: (i, 0))
    return pl.pallas_call(
        body, name="split_dh0", grid=(nb,),
        in_specs=[_ANY] * len(extra) + [blk],
        out_specs=[pl.BlockSpec((BLK, D), lambda i: (jnp.maximum(i - 1, 0), 0)),
                   pl.BlockSpec((N_META, D), lambda i: (0, 0))],
        out_shape=[jax.ShapeDtypeStruct((t - BLK, D), F32), jax.ShapeDtypeStruct((N_META, D), F32)],
        compiler_params=_cp("arbitrary"),
    )(*extra, dh0)


def _valid_rows(nrows, first_row):
    return (first_row + lax.broadcasted_iota(jnp.int32, (nrows, 1), 0)) >= PAD


def conv_fwd(proj, conv_w, conv_b):
    t = proj.shape[0]
    c0 = C_XBC // BLK

    def body(x_ref, w_ref, b_ref, o_ref):
        ok = _valid_rows(t, 0)
        x = jnp.where(ok, x_ref[...], 0.0)
        w = w_ref[...]
        acc = b_ref[...] + w[CONV_K - 1:CONV_K, :] * x
        for s in range(1, CONV_K):
            acc += w[CONV_K - 1 - s:CONV_K - s, :] * pltpu.roll(x, s, 0)
        o_ref[...] = jnp.where(ok, acc * _sig(acc), 0.0)

    return pl.pallas_call(
        body, name="conv_fwd", grid=(CONV_D // BLK,),
        in_specs=[pl.BlockSpec((t, BLK), lambda j: (0, c0 + j)),
                  pl.BlockSpec((CONV_K, BLK), lambda j: (0, j)), pl.BlockSpec((1, BLK), lambda j: (0, j))],
        out_specs=pl.BlockSpec((t, BLK), lambda j: (0, j)),
        out_shape=jax.ShapeDtypeStruct((t, CONV_D), F32),
        compiler_params=_cp("arbitrary"),
    )(proj, conv_w, conv_b)


def conv_bwd(dxa, proj, conv_w, conv_b):
    t = proj.shape[0]
    c0 = C_XBC // BLK

    def body(d_ref, x_ref, w_ref, b_ref, dx_ref, dw_ref, db_ref):
        ok = _valid_rows(t, 0)
        x = jnp.where(ok, x_ref[...], 0.0)
        w = w_ref[...]
        xs = [x] + [pltpu.roll(x, s, 0) for s in range(1, CONV_K)]
        acc = b_ref[...] + w[CONV_K - 1:CONV_K, :] * x
        for s in range(1, CONV_K):
            acc += w[CONV_K - 1 - s:CONV_K - s, :] * xs[s]
        sg = _sig(acc)
        dxc = jnp.where(ok, d_ref[...] * (sg * (1.0 + acc * (1.0 - sg))), 0.0)
        db_ref[...] = jnp.sum(dxc, axis=0, keepdims=True)
        dw_ref[...] = jnp.concatenate(
            [jnp.sum(dxc * xs[CONV_K - 1 - k], axis=0, keepdims=True) for k in range(CONV_K)], axis=0)
        dx = w[CONV_K - 1:CONV_K, :] * dxc
        for s in range(1, CONV_K):
            dx += w[CONV_K - 1 - s:CONV_K - s, :] * pltpu.roll(dxc, t - s, 0)
        dx_ref[...] = jnp.where(ok, dx, 0.0)

    col = pl.BlockSpec((t, BLK), lambda j: (0, j))
    return pl.pallas_call(
        body, name="conv_bwd", grid=(CONV_D // BLK,),
        in_specs=[col, pl.BlockSpec((t, BLK), lambda j: (0, c0 + j)),
                  pl.BlockSpec((CONV_K, BLK), lambda j: (0, j)), pl.BlockSpec((1, BLK), lambda j: (0, j))],
        out_specs=[col, pl.BlockSpec((CONV_K, BLK), lambda j: (0, j)), pl.BlockSpec((1, BLK), lambda j: (0, j))],
        out_shape=[jax.ShapeDtypeStruct((t, CONV_D), F32), jax.ShapeDtypeStruct((CONV_K, CONV_D), F32),
                   jax.ShapeDtypeStruct((1, CONV_D), F32)],
        compiler_params=_cp("arbitrary"),
    )(dxa, proj, conv_w, conv_b)


def _softplus(x):
    return jnp.maximum(x, 0.0) + jnp.log(1.0 + jnp.exp(-jnp.abs(x)))


def _ssd_chunk(xa, sm, dtb, alog, ok):
    dt = jnp.where(ok, _softplus(sm + dtb), 0.0)
    amat = -jnp.exp(alog)
    a = dt * amat
    ac = _nn_hi(_tri().astype(F32), a)
    act = ac.T
    return dt, amat, ac, act


def _ssd_head(xa, dt, ac, act, h, cb, sp):
    g = h // (SSD_H // SSD_G)
    xs = xa[:, SSD_P * h:SSD_P * (h + 1)]
    bg = xa[:, SSD_D + SSD_N * g:SSD_D + SSD_N * (g + 1)]
    cg = xa[:, SSD_D + SSD_G * SSD_N + SSD_N * g:SSD_D + SSD_G * SSD_N + SSD_N * (g + 1)]
    dth = dt[:, h:h + 1]
    ach = ac[:, h:h + 1]
    acth = act[h:h + 1, :]
    xdt = xs * dth
    seg = jnp.where(_tri(), jnp.exp(jnp.minimum(ach - acth, 0.0)), 0.0)
    m = cb * seg
    yd = _nn(m.astype(BF16), xdt.astype(BF16))
    last = ac[BLK - 1:BLK, h:h + 1]
    dec = jnp.exp(last - ach)
    e = jnp.exp(ach)
    yo = _nn(cg.astype(BF16), sp.astype(BF16)) * e
    return xs, bg, cg, dth, ach, xdt, seg, m, yd, last, dec, e, yo


def ssd_fwd(xa, proj, dtb, alog, dskip, normg):
    t = xa.shape[0]
    nb = t // BLK
    gw = SSD_D // SSD_G

    def body(xa_ref, z_ref, sm_ref, dtb_ref, al_ref, ds_ref, ng_ref, y_ref, sp_ref, st):
        c = pl.program_id(0)

        @pl.when(c == 0)
        def _():
            st[...] = jnp.zeros_like(st)

        ok = _valid_rows(BLK, c * BLK)
        xa = xa_ref[...]
        dt, _, ac, act = _ssd_chunk(xa, sm_ref[...], dtb_ref[...], al_ref[...], ok)
        sp_ref[...] = st[...]
        ys = []
        cbs = {}
        for h in range(SSD_H):
            g = h // (SSD_H // SSD_G)
            if g not in cbs:
                bg = xa[:, SSD_D + SSD_N * g:SSD_D + SSD_N * (g + 1)]
                cg = xa[:, SSD_D + SSD_G * SSD_N + SSD_N * g:SSD_D + SSD_G * SSD_N + SSD_N * (g + 1)]
                cbs[g] = _nt(cg.astype(BF16), bg.astype(BF16))
            sp = st[:, SSD_P * h:SSD_P * (h + 1)]
            xs, bg, cg, dth, ach, xdt, seg, m, yd, last, dec, e, yo = _ssd_head(xa, dt, ac, act, h, cbs[g], sp)
            sloc = _tn((bg * dec).astype(BF16), xdt.astype(BF16))
            st[:, SSD_P * h:SSD_P * (h + 1)] = jnp.exp(last) * sp + sloc
            ys.append(yd + yo + ds_ref[:, h:h + 1] * xs)
        y = jnp.concatenate(ys, axis=1)
        z = z_ref[...]
        yg = y * (z * _sig(z))
        outs = []
        for g in range(SSD_G):
            v = yg[:, gw * g:gw * (g + 1)]
            outs.append(v * lax.rsqrt(jnp.mean(v * v, axis=1, keepdims=True) + EPS))
        y_ref[...] = (jnp.concatenate(outs, axis=1) * ng_ref[...]).astype(BF16)

    vec = pl.BlockSpec((1, BLK), lambda c: (0, 0))
    return pl.pallas_call(
        body, name="ssd_fwd", grid=(nb,),
        in_specs=[pl.BlockSpec((BLK, CONV_D), lambda c: (c, 0)),
                  pl.BlockSpec((BLK, SSD_D), lambda c: (c, C_Z // SSD_D)),
                  pl.BlockSpec((BLK, BLK), lambda c: (c, C_SM // BLK)),
                  vec, vec, vec, pl.BlockSpec((1, SSD_D), lambda c: (0, 0))],
        out_specs=[pl.BlockSpec((BLK, SSD_D), lambda c: (c, 0)),
                   pl.BlockSpec((None, SSD_N, SSD_D), lambda c: (c, 0, 0))],
        out_shape=[jax.ShapeDtypeStruct((t, SSD_D), BF16), jax.ShapeDtypeStruct((nb, SSD_N, SSD_D), F32)],
        scratch_shapes=[pltpu.VMEM((SSD_N, SSD_D), F32)],
        compiler_params=_cp("arbitrary"),
    )(xa, proj, proj, dtb, alog, dskip, normg)


def _lane_put(col, lane):
    li = lax.broadcasted_iota(jnp.int32, (col.shape[0], BLK), 1)
    return jnp.where(li == lane, col, 0.0)


def ssd_bwd(dmix, xa, proj, sprev, dtb, alog, dskip, normg):
    t = xa.shape[0]
    nb = t // BLK
    gw = SSD_D // SSD_G
    rev = lambda c: nb - 1 - c

    def body(dy_ref, xa_ref, z_ref, sm_ref, sp_ref, dtb_ref, al_ref, ds_ref, ng_ref,
             dxa_ref, dz_ref, dsm_ref, dng_ref, dds_ref, dal_ref, ddtb_ref, dst):
        c = pl.program_id(0)

        @pl.when(c == 0)
        def _():
            dst[...] = jnp.zeros_like(dst)
            dng_ref[...] = jnp.zeros_like(dng_ref)
            dds_ref[...] = jnp.zeros_like(dds_ref)
            dal_ref[...] = jnp.zeros_like(dal_ref)
            ddtb_ref[...] = jnp.zeros_like(ddtb_ref)

        ok = _valid_rows(BLK, rev(c) * BLK)
        xa = xa_ref[...]
        sm = sm_ref[...]
        dt, amat, ac, act = _ssd_chunk(xa, sm, dtb_ref[...], al_ref[...], ok)
        tri = _tri()
        rowi = lax.broadcasted_iota(jnp.int32, (BLK, 1), 0)
        cbs, heads, ys = {}, [], []
        for h in range(SSD_H):
            g = h // (SSD_H // SSD_G)
            if g not in cbs:
                bg = xa[:, SSD_D + SSD_N * g:SSD_D + SSD_N * (g + 1)]
                cg = xa[:, SSD_D + SSD_G * SSD_N + SSD_N * g:SSD_D + SSD_G * SSD_N + SSD_N * (g + 1)]
                cbs[g] = _nt(cg.astype(BF16), bg.astype(BF16))
            sp = sp_ref[:, SSD_P * h:SSD_P * (h + 1)]
            hd = _ssd_head(xa, dt, ac, act, h, cbs[g], sp)
            heads.append(hd)
            ys.append(hd[8] + hd[12] + ds_ref[:, h:h + 1] * hd[0])
        y = jnp.concatenate(ys, axis=1)
        z = z_ref[...]
        sgz = _sig(z)
        siluz = z * sgz
        yg = y * siluz
        dout = dy_ref[...]
        ng = ng_ref[...]
        dygs, xhs = [], []
        for g in range(SSD_G):
            v = yg[:, gw * g:gw * (g + 1)]
            rr = lax.rsqrt(jnp.mean(v * v, axis=1, keepdims=True) + EPS)
            xh = v * rr
            dxh = dout[:, gw * g:gw * (g + 1)] * ng[:, gw * g:gw * (g + 1)]
            dygs.append(rr * (dxh - xh * jnp.mean(dxh * xh, axis=1, keepdims=True)))
            xhs.append(xh)
        dyg = jnp.concatenate(dygs, axis=1)
        dng_ref[...] += jnp.sum(dout * jnp.concatenate(xhs, axis=1), axis=0, keepdims=True)
        dy = dyg * siluz
        dz_ref[...] = dyg * y * (sgz * (1.0 + z * (1.0 - sgz)))

        dxs_l = []
        db_g = [jnp.zeros((BLK, SSD_N), F32) for _ in range(SSD_G)]
        dc_g = [jnp.zeros((BLK, SSD_N), F32) for _ in range(SSD_G)]
        dac_all = jnp.zeros((BLK, BLK), F32)
        ddt_all = jnp.zeros((BLK, BLK), F32)
        dds_row = jnp.zeros((1, BLK), F32)
        lane1 = lax.broadcasted_iota(jnp.int32, (1, BLK), 1)
        for h in range(SSD_H):
            g = h // (SSD_H // SSD_G)
            xs, bg, cg, dth, ach, xdt, seg, m, yd, last, dec, e, yo = heads[h]
            sp = sp_ref[:, SSD_P * h:SSD_P * (h + 1)]
            dyh = dy[:, SSD_P * h:SSD_P * (h + 1)]
            dyb = dyh.astype(BF16)
            xdtb = xdt.astype(BF16)
            dds_row += jnp.where(lane1 == h, jnp.sum(dyh * xs, keepdims=True), 0.0)
            dxs = ds_ref[:, h:h + 1] * dyh
            dyo = (dyh * e).astype(BF16)
            dc_g[g] += _nt(dyo, sp.astype(BF16))
            dsp = _tn(cg.astype(BF16), dyo)
            dac = jnp.sum(dyh * yo, axis=1, keepdims=True)
            dsn = dst[:, SSD_P * h:SSD_P * (h + 1)]
            gl = jnp.exp(last)
            dst[:, SSD_P * h:SSD_P * (h + 1)] = dsp + gl * dsn
            dlast = jnp.sum(dsn * sp, keepdims=True) * gl
            dsnb = dsn.astype(BF16)
            dbd = _nt(xdtb, dsnb)
            db_g[g] += dbd * dec
            tdec = jnp.sum(dbd * bg, axis=1, keepdims=True) * dec
            dxdt = _nn((bg * dec).astype(BF16), dsnb)
            dlast += jnp.sum(tdec, keepdims=True)
            dac -= tdec
            dm = _nt(dyb, xdtb)
            dxdt += _tn(m.astype(BF16), dyb)
            dcb = (dm * seg).astype(BF16)
            dc_g[g] += _nn(dcb, bg.astype(BF16))
            db_g[g] += _tn(dcb, cg.astype(BF16))
            w = dm * m
            dac += jnp.sum(w, axis=1, keepdims=True) - jnp.sum(w.T, axis=1, keepdims=True)
            dac += jnp.where(rowi == BLK - 1, dlast, 0.0)
            dxs_l.append(dxs + dxdt * dth)
            ddt_all += _lane_put(jnp.sum(dxdt * xs, axis=1, keepdims=True), h)
            dac_all += _lane_put(dac, h)
        da = _nn_hi(_tri(lower=False).astype(F32), dac_all)
        ddt = ddt_all + da * amat
        dal_ref[...] += jnp.sum(da * dt, axis=0, keepdims=True) * amat
        ddtr = jnp.where(ok, ddt * _sig(sm + dtb_ref[...]), 0.0)
        ddtb_ref[...] += jnp.sum(ddtr, axis=0, keepdims=True)
        dds_ref[...] += dds_row
        dsm_ref[...] = ddtr
        dxa_ref[...] = jnp.where(ok, jnp.concatenate(dxs_l + db_g + dc_g, axis=1), 0.0)

    vec = pl.BlockSpec((1, BLK), lambda c: (0, 0))
    nvec = pl.BlockSpec((1, SSD_D), lambda c: (0, 0))
    return pl.pallas_call(
        body, name="ssd_bwd", grid=(nb,),
        in_specs=[pl.BlockSpec((BLK, SSD_D), lambda c: (rev(c), 0)),
                  pl.BlockSpec((BLK, CONV_D), lambda c: (rev(c), 0)),
                  pl.BlockSpec((BLK, SSD_D), lambda c: (rev(c), C_Z // SSD_D)),
                  pl.BlockSpec((BLK, BLK), lambda c: (rev(c), C_SM // BLK)),
                  pl.BlockSpec((None, SSD_N, SSD_D), lambda c: (rev(c), 0, 0)),
                  vec, vec, vec, nvec],
        out_specs=[pl.BlockSpec((BLK, CONV_D), lambda c: (rev(c), 0)),
                   pl.BlockSpec((BLK, SSD_D), lambda c: (rev(c), 0)),
                   pl.BlockSpec((BLK, BLK), lambda c: (rev(c), 0)),
                   nvec, vec, vec, vec],
        out_shape=[jax.ShapeDtypeStruct((t, CONV_D), F32), jax.ShapeDtypeStruct((t, SSD_D), F32),
                   jax.ShapeDtypeStruct((t, BLK), F32), jax.ShapeDtypeStruct((1, SSD_D), F32),
                   jax.ShapeDtypeStruct((1, BLK), F32), jax.ShapeDtypeStruct((1, BLK), F32),
                   jax.ShapeDtypeStruct((1, BLK), F32)],
        scratch_shapes=[pltpu.VMEM((SSD_N, SSD_D), F32)],
        compiler_params=_cp("arbitrary"),
    )(dmix, xa, proj, proj, sprev, dtb, alog, dskip, normg)


def _attn_scores(q_ref, k_ref, h, dq, scale, mask, bias):
    qh = q_ref[:, dq * h:dq * (h + 1)].astype(BF16)
    kh = k_ref[:, dq * h:dq * (h + 1)].astype(BF16)
    s = _nt(qh, kh) * scale
    if bias is not None:
        s = s + bias
    return qh, kh, jnp.where(mask, s, NEG)


def _segments(nb):
    cuts = sorted({0, nb} | {max(1, round(nb * f)) for f in (0.3, 0.53, 0.77)})
    return list(zip(cuts[:-1], cuts[1:]))


def attn_fwd(q, k, v, qcol, kcol, vcol, nh, dq, dv, scale, c_col=None, c_row=None, lane0=0):
    t = q.shape[0]
    tq = BLK
    use_bias = c_col is not None

    def segment(t0, t1, prev):
        tk = t1 * BLK
        nprev = len(prev)

        def body(*refs):
            refs = refs[nprev:]
            if use_bias:
                q_ref, k_ref, v_ref, cc_ref, cr_ref, o_ref, l_ref = refs
            else:
                q_ref, k_ref, v_ref, o_ref, l_ref = refs
            i = pl.program_id(0)
            rowg = (t0 + i) * tq + lax.broadcasted_iota(jnp.int32, (tq, 1), 0)
            col = lax.broadcasted_iota(jnp.int32, (1, tk), 1)
            mask = (col <= rowg) & (col >= PAD)
            outs = []
            lse = jnp.zeros((tq, BLK), F32)
            for h in range(nh):
                bias = (cc_ref[:, lane0 + h:lane0 + h + 1] - cr_ref[h:h + 1, :]) if use_bias else None
                _, _, s = _attn_scores(q_ref, k_ref, h, dq, scale, mask, bias)
                m = jnp.max(s, axis=1, keepdims=True)
                p = jnp.exp(s - m)
                l = jnp.sum(p, axis=1, keepdims=True)
                vh = v_ref[:, dv * h:dv * (h + 1)].astype(BF16)
                outs.append(_nn(p.astype(BF16), vh) / l)
                lse += _lane_put(m + jnp.log(l), h)
            o_ref[...] = jnp.concatenate(outs, axis=1).astype(BF16)
            l_ref[...] = lse

        in_specs = [_ANY] * nprev + [pl.BlockSpec((tq, nh * dq), lambda i: (t0 + i, qcol)),
                                     pl.BlockSpec((tk, nh * dq), lambda i: (0, kcol)),
                                     pl.BlockSpec((tk, nh * dv), lambda i: (0, vcol))]
        args = list(prev) + [q, k, v]
        if use_bias:
            in_specs += [pl.BlockSpec((tq, BLK), lambda i: (t0 + i, 0)), pl.BlockSpec((8, tk), lambda i: (0, 0))]
            args += [c_col, c_row]
        return pl.pallas_call(
            body, name="attn_fwd", grid=(t1 - t0,),
            in_specs=in_specs,
            out_specs=[pl.BlockSpec((tq, nh * dv), lambda i: (t0 + i, 0)), pl.BlockSpec((tq, BLK), lambda i: (t0 + i, 0))],
            out_shape=[jax.ShapeDtypeStruct((t, nh * dv), BF16), jax.ShapeDtypeStruct((t, BLK), F32)],
            input_output_aliases={p: p for p in range(nprev)},
            compiler_params=_cp("arbitrary"),
        )(*args)

    outs = []
    for t0, t1 in _segments(t // tq):
        outs = segment(t0, t1, outs)
    return outs


def attn_bwd(q, k, v, do, lse, qcol, kcol, vcol, docol, nh, dq, dv, scale, c_col=None, c_row=None, lane0=0):
    t = q.shape[0]
    tq = BLK
    use_bias = c_col is not None

    def segment(t0, t1, prev):
        tk = t1 * BLK
        nprev = len(prev)

        def body(*refs):
            pv, refs = refs[:nprev], refs[nprev:]
            if use_bias:
                q_ref, k_ref, v_ref, do_ref, l_ref, cc_ref, cr_ref, dq_ref, dk_ref, dv_ref, dcq_ref, dck_ref = refs
            else:
                q_ref, k_ref, v_ref, do_ref, l_ref, dq_ref, dk_ref, dv_ref = refs
            i = pl.program_id(0)

            @pl.when(i == 0)
            def _():
                if nprev:
                    dk_ref[...] = pv[1][...]
                    dv_ref[...] = pv[2][...]
                    if use_bias:
                        dck_ref[...] = pv[4][...]
                else:
                    dk_ref[...] = jnp.zeros_like(dk_ref)
                    dv_ref[...] = jnp.zeros_like(dv_ref)
                    if use_bias:
                        dck_ref[...] = jnp.zeros_like(dck_ref)

            rowg = (t0 + i) * tq + lax.broadcasted_iota(jnp.int32, (tq, 1), 0)
            col = lax.broadcasted_iota(jnp.int32, (1, tk), 1)
            mask = (col <= rowg) & (col >= PAD)
            dqs = []
            dcq = jnp.zeros((tq, BLK), F32)
            for h in range(nh):
                bias = (cc_ref[:, lane0 + h:lane0 + h + 1] - cr_ref[h:h + 1, :]) if use_bias else None
                qh, kh, s = _attn_scores(q_ref, k_ref, h, dq, scale, mask, bias)
                p = jnp.where(mask, jnp.exp(s - l_ref[:, h:h + 1]), 0.0)
                vh = v_ref[:, dv * h:dv * (h + 1)].astype(BF16)
                doh = do_ref[:, dv * h:dv * (h + 1)].astype(BF16)
                dp = _nt(doh, vh)
                delta = jnp.sum(p * dp, axis=1, keepdims=True)
                ds = p * (dp - delta)
                dsb = ds.astype(BF16)
                dqs.append(_nn(dsb, kh) * scale)
                dk_ref[:, dq * h:dq * (h + 1)] += _tn(dsb, qh) * scale
                dv_ref[:, dv * h:dv * (h + 1)] += _tn(p.astype(BF16), doh)
                if use_bias:
                    dcq += _lane_put(jnp.sum(ds, axis=1, keepdims=True), lane0 + h)
                    dck_ref[h:h + 1, :] += jnp.sum(ds, axis=0, keepdims=True)
            dq_ref[...] = jnp.concatenate(dqs, axis=1)
            if use_bias:
                dcq_ref[...] = dcq

        keys_q = pl.BlockSpec((tk, nh * dq), lambda i: (0, 0))
        keys_v = pl.BlockSpec((tk, nh * dv), lambda i: (0, 0))
        keys_c = pl.BlockSpec((8, tk), lambda i: (0, 0))
        prev_specs = ([_ANY, keys_q, keys_v] + ([_ANY, keys_c] if use_bias else [])) if nprev else []
        in_specs = prev_specs + [pl.BlockSpec((tq, nh * dq), lambda i: (t0 + i, qcol)),
                                 pl.BlockSpec((tk, nh * dq), lambda i: (0, kcol)),
                                 pl.BlockSpec((tk, nh * dv), lambda i: (0, vcol)),
                                 pl.BlockSpec((tq, nh * dv), lambda i: (t0 + i, docol)),
                                 pl.BlockSpec((tq, BLK), lambda i: (t0 + i, 0))]
        args = list(prev) + [q, k, v, do, lse]
        out_specs = [pl.BlockSpec((tq, nh * dq), lambda i: (t0 + i, 0)), keys_q, keys_v]
        out_shape = [jax.ShapeDtypeStruct((t, nh * dq), F32), jax.ShapeDtypeStruct((t, nh * dq), F32),
                     jax.ShapeDtypeStruct((t, nh * dv), F32)]
        if use_bias:
            in_specs += [pl.BlockSpec((tq, BLK), lambda i: (t0 + i, 0)), keys_c]
            args += [c_col, c_row]
            out_specs += [pl.BlockSpec((tq, BLK), lambda i: (t0 + i, 0)), keys_c]
            out_shape += [jax.ShapeDtypeStruct((t, BLK), F32), jax.ShapeDtypeStruct((8, t), F32)]
        return pl.pallas_call(
            body, name="attn_bwd", grid=(t1 - t0,),
            in_specs=in_specs, out_specs=out_specs, out_shape=out_shape,
            input_output_aliases={p: p for p in range(nprev)},
            compiler_params=_cp("arbitrary"),
        )(*args)

    outs = []
    for t0, t1 in reversed(_segments(t // tq)):
        outs = segment(t0, t1, outs)
    return outs


def fox_pre(proj, fb):
    t = proj.shape[0]
    nb = t // BLK

    def body(sm_ref, fb_ref, c_ref, cr_ref):
        x = sm_ref[...] + fb_ref[...]
        lane = lax.broadcasted_iota(jnp.int32, (1, BLK), 1)
        keep = _valid_rows(t, 0) & (lane >= SM_F) & (lane < SM_F + FOX_H)
        logf = jnp.where(keep, jnp.minimum(x, 0.0) - jnp.log(1.0 + jnp.exp(-jnp.abs(x))), 0.0)
        tri = _tri().astype(F32)
        carry = jnp.zeros((1, BLK), F32)
        for b in range(nb):
            cb = _nn_hi(tri, logf[b * BLK:(b + 1) * BLK, :]) + carry
            c_ref[b * BLK:(b + 1) * BLK, :] = cb
            carry = cb[BLK - 1:BLK, :]
        cr_ref[...] = c_ref[...].T[SM_F:SM_F + 8, :]

    return pl.pallas_call(
        body, name="fox_pre", grid=(1,),
        in_specs=[pl.BlockSpec((t, BLK), lambda i: (0, C_SM // BLK)), pl.BlockSpec((1, BLK), lambda i: (0, 0))],
        out_specs=[pl.BlockSpec((t, BLK), lambda i: (0, 0)), pl.BlockSpec((8, t), lambda i: (0, 0))],
        out_shape=[jax.ShapeDtypeStruct((t, BLK), F32), jax.ShapeDtypeStruct((8, t), F32)],
        compiler_params=_cp("arbitrary"),
    )(proj, fb)


def fox_pre_bwd(dcq, dck, proj, fb, dsm_in):
    t = proj.shape[0]
    nb = t // BLK

    def body(dcq_ref, dck_ref, sm_ref, fb_ref, din_ref, dsm_ref, dfb_ref, scr):
        triu = _tri(lower=False).astype(F32)
        carry = jnp.zeros((1, BLK), F32)
        scr[...] = jnp.concatenate([jnp.zeros((SM_F, t), F32), dck_ref[...], jnp.zeros((BLK - SM_F - 8, t), F32)], axis=0).T
        for b in range(nb - 1, -1, -1):
            blk = dcq_ref[b * BLK:(b + 1) * BLK, :] - scr[b * BLK:(b + 1) * BLK, :]
            cb = _nn_hi(triu, blk) + carry
            scr[b * BLK:(b + 1) * BLK, :] = cb
            carry = cb[0:1, :]
        x = sm_ref[...] + fb_ref[...]
        lane = lax.broadcasted_iota(jnp.int32, (1, BLK), 1)
        keep = _valid_rows(t, 0) & (lane >= SM_F) & (lane < SM_F + FOX_H)
        df = jnp.where(keep, scr[...] * _sig(-x), 0.0)
        dfb_ref[...] = jnp.sum(df, axis=0, keepdims=True)
        dsm_ref[...] = din_ref[...] + df

    full = pl.BlockSpec((t, BLK), lambda i: (0, 0))
    return pl.pallas_call(
        body, name="fox_pre_bwd", grid=(1,),
        in_specs=[full, pl.BlockSpec((8, t), lambda i: (0, 0)), pl.BlockSpec((t, BLK), lambda i: (0, C_SM // BLK)),
                  pl.BlockSpec((1, BLK), lambda i: (0, 0)), full],
        out_specs=[full, pl.BlockSpec((1, BLK), lambda i: (0, 0))],
        out_shape=[jax.ShapeDtypeStruct((t, BLK), F32), jax.ShapeDtypeStruct((1, BLK), F32)],
        scratch_shapes=[pltpu.VMEM((t, BLK), F32)],
        compiler_params=_cp("arbitrary"),
    )(dcq, dck, proj, fb, dsm_in)


def _swap_rope(x):
    lane = lax.broadcasted_iota(jnp.int32, (1, BLK), 1)
    return jnp.where((lane >= SM_KR) & (lane < SM_KR + 16), pltpu.roll(x, BLK - 16, 1),
                     jnp.where((lane >= SM_KR + 16) & (lane < SM_KR + 32), pltpu.roll(x, 16, 1), 0.0))


def _rms(x, g):
    r = lax.rsqrt(jnp.mean(x * x, axis=1, keepdims=True) + EPS)
    return r, x * r


def mla_pre(proj, qg, kvg, wq, wk, wv, cosq, sinq):
    t = proj.shape[0]
    tm = _row_tile(t)

    def body(cq_ref, ckv_ref, sm_ref, qg_ref, kvg_ref, wq_ref, wk_ref, wv_ref, cos_ref, sin_ref,
             q_ref, k_ref, v_ref, cqn_ref, ckvn_ref):
        cs, sn = cos_ref[...], sin_ref[...]
        _, xh = _rms(cq_ref[...], None)
        cqn = (xh * qg_ref[...]).astype(BF16)
        cqn_ref[...] = cqn
        qraw = _nn(cqn, wq_ref[...])
        qs = []
        for h in range(MLA_H):
            hb = qraw[:, BLK * h:BLK * (h + 1)]
            qs.append(hb * cs + _swap_rope(hb) * sn)
        q_ref[...] = jnp.concatenate(qs, axis=1).astype(BF16)
        _, kh = _rms(ckv_ref[...], None)
        ckvn = (kh * kvg_ref[...]).astype(BF16)
        ckvn_ref[...] = ckvn
        kraw = _nn(ckvn, wk_ref[...])
        v_ref[...] = _nn(ckvn, wv_ref[...]).astype(BF16)
        lane = lax.broadcasted_iota(jnp.int32, (1, BLK), 1)
        kr = sm_ref[...]
        krr = jnp.where((lane >= SM_KR) & (lane < SM_KR + MLA_ROPE), kr * cs + _swap_rope(kr) * sn, 0.0)
        k_ref[...] = jnp.concatenate([kraw[:, BLK * h:BLK * (h + 1)] + krr for h in range(MLA_H)], axis=1).astype(BF16)

    def rows(w, cb):
        return pl.BlockSpec((tm, w), lambda i: (i, cb))

    def whole(a):
        return pl.BlockSpec(a.shape, lambda i: (0, 0))

    return pl.pallas_call(
        body, name="mla_pre", grid=(t // tm,),
        in_specs=[rows(MLA_QL, C_CQ // MLA_QL), rows(MLA_KVL, C_CKV // MLA_KVL), rows(BLK, C_SM // BLK),
                  whole(qg), whole(kvg), whole(wq), whole(wk), whole(wv), rows(BLK, 0), rows(BLK, 0)],
        out_specs=[rows(512, 0), rows(512, 0), rows(256, 0), rows(MLA_QL, 0), rows(MLA_KVL, 0)],
        out_shape=[jax.ShapeDtypeStruct((t, 512), BF16), jax.ShapeDtypeStruct((t, 512), BF16),
                   jax.ShapeDtypeStruct((t, 256), BF16), jax.ShapeDtypeStruct((t, MLA_QL), BF16),
                   jax.ShapeDtypeStruct((t, MLA_KVL), BF16)],
        compiler_params=_cp("arbitrary"),
    )(proj, proj, proj, qg, kvg, wq, wk, wv, cosq, sinq)


def mla_pre_bwd(dq, dk, dv, proj, cqn, ckvn, qg, kvg, wq, wk, wv, cosq, sinq, dsm_in):
    t = proj.shape[0]
    tm = _row_tile(t)

    def body(dq_ref, dk_ref, dv_ref, cq_ref, ckv_ref, cqn_ref, ckvn_ref, qg_ref, kvg_ref, wq_ref, wk_ref, wv_ref,
             cos_ref, sin_ref, din_ref, dcq_ref, dckv_ref, dsm_ref, dwq_ref, dwk_ref, dwv_ref, dqg_ref, dkvg_ref):
        i = pl.program_id(0)

        @pl.when(i == 0)
        def _():
            for r in (dwq_ref, dwk_ref, dwv_ref, dqg_ref, dkvg_ref):
                r[...] = jnp.zeros_like(r)

        cs, sn = cos_ref[...], sin_ref[...]
        lane = lax.broadcasted_iota(jnp.int32, (1, BLK), 1)

        def unrope(dy):
            return dy * cs + _swap_rope(dy * sn)

        dqp = jnp.concatenate([unrope(dq_ref[:, BLK * h:BLK * (h + 1)]) for h in range(MLA_H)], axis=1).astype(BF16)
        dwq_ref[...] += _tn(cqn_ref[...], dqp)
        dcqn = _nt(dqp, wq_ref[...])
        r, xh = _rms(cq_ref[...], None)
        dqg_ref[...] += jnp.sum(dcqn * xh, axis=0, keepdims=True)
        dxh = dcqn * qg_ref[...]
        dcq_ref[...] = r * (dxh - xh * jnp.mean(dxh * xh, axis=1, keepdims=True))

        dkn, dkr = [], jnp.zeros((tm, BLK), F32)
        for h in range(MLA_H):
            blk = dk_ref[:, BLK * h:BLK * (h + 1)]
            dkn.append(jnp.where(lane < MLA_NOPE, blk, 0.0))
            dkr += jnp.where((lane >= SM_KR) & (lane < SM_KR + MLA_ROPE), blk, 0.0)
        dknb = jnp.concatenate(dkn, axis=1).astype(BF16)
        dvb = dv_ref[...].astype(BF16)
        ckvn = ckvn_ref[...]
        dwk_ref[...] += _tn(ckvn, dknb)
        dwv_ref[...] += _tn(ckvn, dvb)
        dckvn = _nt(dknb, wk_ref[...]) + _nt(dvb, wv_ref[...])
        r2, kh = _rms(ckv_ref[...], None)
        dkvg_ref[...] += jnp.sum(dckvn * kh, axis=0, keepdims=True)
        dkh = dckvn * kvg_ref[...]
        dckv_ref[...] = r2 * (dkh - kh * jnp.mean(dkh * kh, axis=1, keepdims=True))
        dsm_ref[...] = din_ref[...] + jnp.where((lane >= SM_KR) & (lane < SM_KR + MLA_ROPE), unrope(dkr), 0.0)

    def rows(w, cb):
        return pl.BlockSpec((tm, w), lambda i: (i, cb))

    def whole(a):
        return pl.BlockSpec(a.shape, lambda i: (0, 0))

    def wshape(a):
        return jax.ShapeDtypeStruct(a.shape, F32)

    return pl.pallas_call(
        body, name="mla_pre_bwd", grid=(t // tm,),
        in_specs=[rows(512, 0), rows(512, 0), rows(256, 0), rows(MLA_QL, C_CQ // MLA_QL), rows(MLA_KVL, C_CKV // MLA_KVL),
                  rows(MLA_QL, 0), rows(MLA_KVL, 0), whole(qg), whole(kvg), whole(wq), whole(wk), whole(wv),
                  rows(BLK, 0), rows(BLK, 0), rows(BLK, 0)],
        out_specs=[rows(MLA_QL, 0), rows(MLA_KVL, 0), rows(BLK, 0), whole(wq), whole(wk), whole(wv), whole(qg), whole(kvg)],
        out_shape=[jax.ShapeDtypeStruct((t, MLA_QL), F32), jax.ShapeDtypeStruct((t, MLA_KVL), F32),
                   jax.ShapeDtypeStruct((t, BLK), F32), wshape(wq), wshape(wk), wshape(wv), wshape(qg), wshape(kvg)],
        compiler_params=_cp("arbitrary"),
    )(dq, dk, dv, proj, proj, cqn, ckvn, qg, kvg, wq, wk, wv, cosq, sinq, dsm_in)


def _slot_sum(me, own, recv_ref):
    gg = own.astype(F32)
    for s in range(N_DEV):
        gg = gg + jnp.where(me == s, 0.0, recv_ref[s].astype(F32))
    return gg


def adamw(w, m, v, g=None, recv=None, own=None, me_arr=None):
    shape = w.shape
    c = shape[-1]
    from_recv = recv is not None
    if not from_recv:
        me_arr = jnp.zeros((1,), jnp.int32)
    nl = len(recv) if from_recv else 1
    rws = w.size // c // nl
    tr = rws
    for d in (1024, 512, 352, 256, 128, 64, 32, 16, 8):
        if rws % d == 0 and d * c * 4 <= (2 << 20):
            tr = d
            break
    nt = rws // tr
    w2, m2, v2 = (a.reshape(nl, rws, c) for a in (w, m, v))
    if from_recv:
        gin = [a.reshape(N_DEV, rws, c) for a in list(recv) + list(own)]
    else:
        gin = [g.reshape(1, rws, c)]

    def body(me_ref, w_ref, m_ref, v_ref, *rest):
        g_refs, outs = rest[:len(gin)], rest[len(gin):]
        if from_recv:
            g_out, outs = outs[0], outs[1:]
            for li in range(nl):
                @pl.when(pl.program_id(0) == li)
                def _(li=li):
                    g_out[...] = _slot_sum(me_ref[0], g_refs[nl + li][...], g_refs[li])
            gg = g_out[...]
        else:
            gg = g_refs[0][...]
        d_ref, nm_ref, nv_ref = outs
        nm = B1 * m_ref[...] + (1.0 - B1) * gg
        nv = B2 * v_ref[...] + (1.0 - B2) * (gg * gg)
        mh = nm / (1.0 - B1 ** STEP)
        vh = nv / (1.0 - B2 ** STEP)
        d_ref[...] = -LR * (mh / (jnp.sqrt(vh) + AEPS) + WD * w_ref[...])
        nm_ref[...] = nm
        nv_ref[...] = nv

    row = pl.BlockSpec((None, tr, c), lambda l, i, me: (l, i, 0))
    if from_recv:
        gspecs = [pl.BlockSpec((N_DEV, tr, c), lambda l, i, me, li=li: (0, jnp.where(l == li, i, 0), 0))
                  for li in range(nl)]
        gspecs += [pl.BlockSpec((None, tr, c), lambda l, i, me, li=li: (me[0], jnp.where(l == li, i, 0), 0))
                   for li in range(nl)]
    else:
        gspecs = [row]
    nout = 4 if from_recv else 3
    outs = pl.pallas_call(
        body, name="adamw",
        grid_spec=pltpu.PrefetchScalarGridSpec(num_scalar_prefetch=1, grid=(nl, nt), in_specs=[row, row, row] + gspecs,
                                               out_specs=[row] * nout),
        out_shape=[jax.ShapeDtypeStruct((nl, rws, c), F32)] * nout,
        compiler_params=_cp("arbitrary", "arbitrary"),
    )(me_arr, w2, m2, v2, *gin)
    return tuple(o.reshape(shape) for o in outs)


def sum_slots(recv, own=None, me_arr=None):
    _, r, c = recv.shape
    if own is None:
        own, me_arr = recv, jnp.zeros((1,), jnp.int32)
        plain = True
    else:
        plain = False

    def body(me_ref, r_ref, own_ref, o_ref):
        if plain:
            gg = r_ref[0].astype(F32)
            for s in range(1, N_DEV):
                gg = gg + r_ref[s].astype(F32)
            o_ref[...] = gg
        else:
            o_ref[...] = _slot_sum(me_ref[0], own_ref[...], r_ref)

    return pl.pallas_call(
        body, name="sum_slots",
        grid_spec=pltpu.PrefetchScalarGridSpec(
            num_scalar_prefetch=1, grid=(1,),
            in_specs=[pl.BlockSpec((N_DEV, r, c), lambda i, me: (0, 0, 0)),
                      pl.BlockSpec((None, r, c), lambda i, me: (me[0], 0, 0))],
            out_specs=pl.BlockSpec((r, c), lambda i, me: (0, 0))),
        out_shape=jax.ShapeDtypeStruct((r, c), F32),
        compiler_params=_cp("arbitrary"),
    )(me_arr, recv, own)


_FLIPS = [(0, 0, 1), (0, 1, 0), (0, 1, 1), (1, 0, 0), (1, 0, 1), (1, 1, 0), (1, 1, 1)]
_ANY = pl.BlockSpec(memory_space=pl.ANY)


def _mesh_place():
    x, y, c = lax.axis_index("x"), lax.axis_index("y"), lax.axis_index("c")
    me = 4 * x + 2 * y + c
    peers = [((x + fx) % 2, (y + fy) % 2, (c + fc) % 2) for fx, fy, fc in _FLIPS]
    return me, peers


def place_own(src, l, dtype, me_arr):
    _, r, c = src.shape
    tr = r
    for d in (512, 352, 256, 128, 64, 32, 16, 8):
        if r % d == 0 and d * c * 4 <= (2 << 20):
            tr = d
            break

    def body(me_ref, s_ref, o_ref):
        o_ref[...] = s_ref[...].astype(dtype)

    return pl.pallas_call(
        body, name="place_own",
        grid_spec=pltpu.PrefetchScalarGridSpec(
            num_scalar_prefetch=1, grid=(r // tr,),
            in_specs=[pl.BlockSpec((None, tr, c), lambda i, me: (l, i, 0))],
            out_specs=pl.BlockSpec((None, tr, c), lambda i, me: (me[0], i, 0))),
        out_shape=jax.ShapeDtypeStruct((N_DEV, r, c), dtype),
        compiler_params=_cp("arbitrary"),
    )(me_arr, src)


_HBM = pl.BlockSpec(memory_space=pltpu.HBM)
_SEMS = pl.BlockSpec(memory_space=pltpu.SEMAPHORE)
_EFFECT = pltpu.SideEffectType.DATAFLOW_SIDE_EFFECTING


def exchange_start(mode, arrays, name):
    n = len(arrays)
    gather = mode == "gather"
    ns = 0 if gather else n
    zones = list(arrays) if gather else [lax.empty(a.shape, a.dtype) for a in arrays]
    ops = ([] if gather else list(arrays)) + zones

    def body(*refs):
        srcs, lands = refs[:ns], refs[ns:ns + n]
        send_sems, recv_sems = refs[ns + n], refs[ns + n + 1]
        token = refs[-1]
        me, peers = _mesh_place()
        ids = [4 * p[0] + 2 * p[1] + p[2] for p in peers]
        for j in range(n):
            for k in range(N_DEV - 1):
                src = lands[j].at[me] if gather else srcs[j].at[ids[k]]
                pltpu.make_async_remote_copy(src_ref=src, dst_ref=lands[j].at[me],
                                             send_sem=send_sems.at[j * (N_DEV - 1) + k],
                                             recv_sem=recv_sems.at[j * (N_DEV - 1) + k], device_id=peers[k],
                                             device_id_type=pl.DeviceIdType.MESH).start()
        token[...] = jnp.zeros_like(token)

    nsem = n * (N_DEV - 1)
    res = pl.pallas_call(
        body, name=name,
        in_specs=[_HBM] * (ns + n),
        out_specs=(_SEMS, _SEMS, *[_HBM] * (ns + n), pl.BlockSpec(memory_space=pltpu.VMEM)),
        out_shape=(pltpu.SemaphoreType.DMA((nsem,)), pltpu.SemaphoreType.DMA((nsem,)),
                   *[pltpu.HBM(a.shape, a.dtype) for a in ops], jax.ShapeDtypeStruct((8, BLK), F32)),
        input_output_aliases={i: 2 + i for i in range(ns + n)},
        compiler_params=pltpu.CompilerParams(has_side_effects=_EFFECT),
    )(*[pltpu.with_memory_space_constraint(a, pltpu.HBM) for a in ops])
    return dict(gather=gather, send=res[0], recv=res[1], srcs=list(res[2:2 + ns]), lands=list(res[2 + ns:2 + ns + n]),
                token=res[-1])


def exchange_wait(hd, idxs, name, after):
    gather = hd["gather"]
    n = len(idxs)
    ns = 0 if gather else n
    ops = ([] if gather else [hd["srcs"][j] for j in idxs]) + [hd["lands"][j] for j in idxs]

    def body(*refs):
        srcs, lands = refs[:ns], refs[ns:ns + n]
        send_sems, recv_sems = refs[ns + n], refs[ns + n + 1]
        me, peers = _mesh_place()
        ids = [4 * p[0] + 2 * p[1] + p[2] for p in peers]
        for p, j in enumerate(idxs):
            for k in range(N_DEV - 1):
                src = lands[p].at[me] if gather else srcs[p].at[ids[k]]
                cp = pltpu.make_async_remote_copy(src_ref=src, dst_ref=lands[p].at[ids[k]],
                                                  send_sem=send_sems.at[j * (N_DEV - 1) + k],
                                                  recv_sem=recv_sems.at[j * (N_DEV - 1) + k], device_id=peers[k],
                                                  device_id_type=pl.DeviceIdType.MESH)
                cp.wait_send()
                cp.wait_recv()

    res = pl.pallas_call(
        body, name=name,
        in_specs=[_HBM] * (ns + n) + [_SEMS, _SEMS, _ANY],
        out_specs=[_HBM] * (ns + n),
        out_shape=[pltpu.HBM(a.shape, a.dtype) for a in ops],
        input_output_aliases={i: i for i in range(ns + n)},
        compiler_params=pltpu.CompilerParams(has_side_effects=_EFFECT),
    )(*ops, hd["send"], hd["recv"], after)
    return list(res[:ns]), list(res[ns:])


def _pad_cols(a, n):
    return jnp.pad(a, ((0, 0),) * (a.ndim - 1) + ((0, n - a.shape[-1]),))


def w_in_to_padded(w):
    z = lambda n: jnp.zeros(w.shape[:-1] + (n,), w.dtype)
    return jnp.concatenate([
        w[..., 0:1280], w[..., 1288:2056], w[..., 2060:2316], w[..., 2316:2444],
        w[..., 1280:1288], w[..., 2056:2060], z(SM_KR - SM_F - FOX_H), w[..., 2444:2476], z(BLK - SM_KR - MLA_ROPE)], axis=-1)


def w_in_from_padded(g):
    s = C_SM
    return jnp.concatenate([
        g[..., 0:1280], g[..., s + SM_DT:s + SM_DT + 8], g[..., 1280:2048], g[..., s + SM_F:s + SM_F + 4],
        g[..., 2048:2304], g[..., 2304:2432], g[..., s + SM_KR:s + SM_KR + MLA_ROPE]], axis=-1)


def _unshard_cols(gth):
    n, r, c = gth.shape
    return jnp.transpose(gth, (1, 0, 2)).reshape(r, n * c)


def _shard_cols(full):
    r, nc = full.shape
    return jnp.transpose(full.reshape(r, N_DEV, nc // N_DEV), (1, 0, 2))


def mla_weights(uq_g, ukv_g):
    uq = _unshard_cols(uq_g)
    dqh = MLA_NOPE + MLA_ROPE
    wq = jnp.concatenate([_pad_cols(uq[:, dqh * h:dqh * (h + 1)], BLK) for h in range(MLA_H)], axis=1)
    wk = jnp.concatenate([_pad_cols(ukv_g[2 * h], BLK) for h in range(MLA_H)], axis=1)
    wv = jnp.concatenate([ukv_g[2 * h + 1] for h in range(MLA_H)], axis=1)
    return wq, wk, wv


def mla_weight_grads(dwq, dwk, dwv):
    dqh = MLA_NOPE + MLA_ROPE
    duq = _shard_cols(jnp.concatenate([dwq[:, BLK * h:BLK * h + dqh] for h in range(MLA_H)], axis=1))
    parts = []
    for h in range(MLA_H):
        parts += [dwk[:, BLK * h:BLK * h + MLA_NOPE], dwv[:, MLA_V * h:MLA_V * (h + 1)]]
    return duq, jnp.stack(parts, axis=0)


def rope_tables(t):
    pos = (jnp.arange(t, dtype=jnp.int32) - PAD).astype(F32)
    inv_freq = 1.0 / (10000.0 ** (jnp.arange(0, MLA_ROPE, 2, dtype=F32) / MLA_ROPE))
    ang = pos[:, None] * inv_freq[None, :]
    cos, sin = jnp.cos(ang), jnp.sin(ang)
    one, zero = jnp.ones((t, SM_KR), F32), jnp.zeros((t, SM_KR), F32)
    tail = BLK - SM_KR - MLA_ROPE
    cosq = jnp.concatenate([one, cos, cos, jnp.ones((t, tail), F32)], axis=1)
    sinq = jnp.concatenate([zero, -sin, sin, jnp.zeros((t, tail), F32)], axis=1)
    return cosq, sinq


def _lanes(v, off=0):
    return jnp.pad(v.astype(F32), (off, BLK - off - v.shape[0]))[None, :]


def layer_fwd(x, ln, hb, getw, tabs):
    sv = {"h0b": hb}
    W = dict(getw("ffn1", hb))
    ln1 = (W["ln1_g"], W["ln1_b"])
    u, v, r1, h1b = ffn_fwd_seq(x, ln, W["g1"], W["u1"], W["d1"], ln1)
    sv.update(u1=u, v1=v, r1=r1, h1b=h1b)
    W.update(getw("mix", h1b))
    ln2 = (W["ln2_g"], W["ln2_b"])
    proj = mm_nn(h1b, W["w_in"])
    xa = conv_fwd(proj, W["conv_w"], W["conv_b"])
    y_ssd, sprev = ssd_fwd(xa, proj, W["dtb"], W["alog"], W["dskip"], W["normg"])
    c_col, c_row = fox_pre(proj, W["fb"])
    y_fox, lse_f = attn_fwd(proj, proj, proj, C_FQ // 256, C_FK // 256, C_FV // 256, FOX_H, FOX_DH, FOX_DH,
                            FOX_DH ** -0.5, c_col, c_row, SM_F)
    q, k, vv, cqn, ckvn = mla_pre(proj, W["qg"], W["kvg"], W["wq"], W["wk"], W["wv"], *tabs)
    y_mla, lse_m = attn_fwd(q, k, vv, 0, 0, 0, MLA_H, BLK, MLA_V, (MLA_NOPE + MLA_ROPE) ** -0.5)
    mixcat = jnp.concatenate([y_ssd, y_fox, y_mla], axis=1)
    r2, h2b = mm_res_ln(mixcat, W["w_out"], r1, ln1, ln2)
    sv.update(proj=proj, xa=xa, sprev=sprev, c_col=c_col, c_row=c_row, lse_f=lse_f, q=q, k=k, v=vv, cqn=cqn, ckvn=ckvn,
              lse_m=lse_m, mixcat=mixcat, r2=r2, h2b=h2b)
    W.update(getw("ffn2", h2b))
    ln3 = (W["ln3_g"], W["ln3_b"])
    u, v, r3, h3b = ffn_fwd_seq(r2, ln2, W["g2"], W["u2"], W["d2"], ln3)
    sv.update(u2=u, v2=v, r3=r3, W=W)
    return r3, ln3, h3b, sv


def ffn_bwd(parts, r, gamma, hb_in, u, v, wg, wu, wd, after=None):
    dh, dwg, dwu, dwd, dg, db = ffn_bwd_seq(parts, r, gamma, hb_in, u, v, wg, wu, wd, after)
    return dh, dict(d=dwd, g=dwg, u=dwu, ln_g=dg, ln_b=db)


def layer_bwd(parts, sv, emit, tabs, after):
    G = {}
    W = sv["W"]
    dh2, g2 = ffn_bwd(parts, sv["r3"], W["ln3_g"], sv["h2b"], sv["u2"], sv["v2"], W["g2"], W["u2"], W["d2"], after)
    G.update(g2=g2["g"], u2=g2["u"], d2=g2["d"], ln3_g=g2["ln_g"], ln3_b=g2["ln_b"])
    tok = emit("ffn2", G)
    dr2, dmixb, G["ln2_g"], G["ln2_b"] = ln_bwd([(dh2, 1.0)], sv["r2"], W["ln2_g"], 1.0, tok)
    dmc = mm_nt_reduce([(dmixb[None], W["w_out"][None])], D)
    G["w_out"] = mm_tn(sv["mixcat"][None], dmixb[None])[0]
    proj = sv["proj"]
    dxa, dz, dsm, G["normg"], G["dskip"], G["alog"], G["dtb"] = ssd_bwd(
        dmc, sv["xa"], proj, sv["sprev"], W["dtb"], W["alog"], W["dskip"], W["normg"])
    dxbc, G["conv_w"], G["conv_b"] = conv_bwd(dxa, proj, W["conv_w"], W["conv_b"])
    dfq, dfk, dfv, dcq, dck = attn_bwd(proj, proj, proj, dmc, sv["lse_f"], C_FQ // 256, C_FK // 256, C_FV // 256, 2,
                                       FOX_H, FOX_DH, FOX_DH, FOX_DH ** -0.5, sv["c_col"], sv["c_row"], SM_F)
    dsm, G["fb"] = fox_pre_bwd(dcq, dck, proj, W["fb"], dsm)
    dq, dk, dv = attn_bwd(sv["q"], sv["k"], sv["v"], dmc, sv["lse_m"], 0, 0, 0, 3, MLA_H, BLK, MLA_V,
                          (MLA_NOPE + MLA_ROPE) ** -0.5)
    dcql, dckv, dsm, G["wq"], G["wk"], G["wv"], G["qg"], G["kvg"] = mla_pre_bwd(
        dq, dk, dv, proj, sv["cqn"], sv["ckvn"], W["qg"], W["kvg"], W["wq"], W["wk"], W["wv"], *tabs, dsm)
    dproj = jnp.concatenate([dz, dxbc, dfq, dfk, dfv, dcql, dckv, dsm], axis=1).astype(BF16)
    dh1p = mm_nt_reduce([(dproj[None], W["w_in"][None])], D)
    G["w_in"] = mm_tn(sv["h1b"][None], dproj[None])[0]
    tok = emit("mix", G)
    dh0, g1 = ffn_bwd([(dr2, ALPHA), (dh1p, 1.0)], sv["r1"], W["ln1_g"], sv["h0b"], sv["u1"], sv["v1"],
                      W["g1"], W["u1"], W["d1"], tok)
    G.update(g1=g1["g"], u1=g1["u"], d1=g1["d"], ln1_g=g1["ln_g"], ln1_b=g1["ln_b"])
    tok = emit("ffn1", G)
    return [(dh0, 1.0)], G, tok


def local_step(x, target, meta_full, getw, emit):
    t = x.shape[0] + BLK
    tabs = rope_tables(t)
    xr, hb = build_h0(meta_full, x)
    ln = None
    saved = []
    for l in range(NL):
        xr, ln, hb, sv = layer_fwd(xr, ln, hb, functools.partial(getw, l), tabs)
        saved.append(sv)
    dy, loss = loss_head(xr, ln, target)
    parts = [(dy, 1.0)]
    grads = [None] * NL
    tok = None
    for l in range(NL - 1, -1, -1):
        parts, grads[l], tok = layer_bwd(parts, saved[l], functools.partial(emit, l), tabs, tok)
    gx, gmeta = split_dh0(parts[0][0], tok)
    return loss, gx, gmeta, grads


_SMALL = ["ln1_g", "ln1_b", "ln2_g", "ln2_b", "ln3_g", "ln3_b", "conv_b", "ssd_norm_g", "mla_q_norm_g",
          "mla_kv_norm_g", "dt_bias", "a_log", "d_skip", "fox_f_b"]
_SMALL_ROWS = 16
_BIG = ["ffn1_w_gate", "ffn1_w_up", "ffn1_w_down", "w_in", "conv_w", "mla_w_uq", "mla_w_ukv", "w_out",
        "ffn2_w_gate", "ffn2_w_up", "ffn2_w_down"]
_NAMES = ["meta", "ffn1_w_gate", "ffn1_w_up", "ffn1_w_down", "ln1_g", "ln1_b", "w_in", "conv_w", "conv_b", "dt_bias",
          "a_log", "d_skip", "ssd_norm_g", "fox_f_b", "mla_q_norm_g", "mla_w_uq", "mla_kv_norm_g", "mla_w_ukv", "w_out",
          "ln2_g", "ln2_b", "ffn2_w_gate", "ffn2_w_up", "ffn2_w_down", "ln3_g", "ln3_b"]


def pack_small(p):
    rows = []
    for l in range(NL):
        for n in _SMALL:
            rows.append(_pad_cols(p[n][l][None, :].astype(F32), D))
        rows.append(jnp.zeros((_SMALL_ROWS - len(_SMALL), D), F32))
    return jnp.concatenate(rows, axis=0)


def unpack_small(a, like):
    out = {}
    for i, n in enumerate(_SMALL):
        out[n] = jnp.stack([a[l * _SMALL_ROWS + i, :like[n].shape[1]] for l in range(NL)], axis=0)
    return out


_STAGES = {"ffn1": ["ffn1_w_gate", "ffn1_w_up", "ffn1_w_down"],
           "mix": ["w_in", "conv_w", "mla_w_uq", "mla_w_ukv", "w_out"],
           "ffn2": ["ffn2_w_gate", "ffn2_w_up", "ffn2_w_down"]}


_FFN_T = ("ffn1_w_gate", "ffn1_w_up", "ffn2_w_gate", "ffn2_w_up")


def stage_weights(l, stage, g, rep):
    if stage != "mix":
        i = stage[3]
        return {"g" + i: g[f"ffn{i}_w_gate"].reshape(D_FF, D), "u" + i: g[f"ffn{i}_w_up"].reshape(D_FF, D),
                "d" + i: g[f"ffn{i}_w_down"].reshape(D_FF, D),
                "ln1_g" if i == "1" else "ln3_g": rep["ln1_g" if i == "1" else "ln3_g"][l][None, :],
                "ln1_b" if i == "1" else "ln3_b": rep["ln1_b" if i == "1" else "ln3_b"][l][None, :]}
    W = {}
    W["w_in"] = g["w_in"].reshape(D, N_INP)
    W["w_out"] = g["w_out"].reshape(D, D)
    W["wq"], W["wk"], W["wv"] = mla_weights(g["mla_w_uq"], g["mla_w_ukv"])
    W["conv_w"] = _unshard_cols(g["conv_w"])
    for k in ("ln2_g", "ln2_b", "conv_b"):
        W[k] = rep[k][l][None, :]
    W["normg"] = rep["ssd_norm_g"][l][None, :]
    W["qg"] = rep["mla_q_norm_g"][l][None, :]
    W["kvg"] = rep["mla_kv_norm_g"][l][None, :]
    W["dtb"] = _lanes(rep["dt_bias"][l], SM_DT)
    W["alog"] = _lanes(rep["a_log"][l], SM_DT)
    W["dskip"] = _lanes(rep["d_skip"][l], SM_DT)
    W["fb"] = _lanes(rep["fox_f_b"][l], SM_F)
    return W


def small_grads(G):
    return {"ln1_g": G["ln1_g"][0], "ln1_b": G["ln1_b"][0], "ln2_g": G["ln2_g"][0], "ln2_b": G["ln2_b"][0],
            "ln3_g": G["ln3_g"][0], "ln3_b": G["ln3_b"][0], "conv_b": G["conv_b"][0], "ssd_norm_g": G["normg"][0],
            "mla_q_norm_g": G["qg"][0], "mla_kv_norm_g": G["kvg"][0], "dt_bias": G["dtb"][0, :SSD_H],
            "a_log": G["alog"][0, :SSD_H], "d_skip": G["dskip"][0, :SSD_H], "fox_f_b": G["fb"][0, SM_F:SM_F + FOX_H]}


def big_grads(G, stage):
    if stage != "mix":
        i = stage[-1]
        return {f"ffn{i}_w_{k}": G[k[0] + i].reshape(N_DEV, HS, D) for k in ("gate", "up", "down")}
    duq, dukv = mla_weight_grads(G["wq"], G["wk"], G["wv"])
    return {"w_in": G["w_in"].reshape(N_DEV, D // N_DEV, N_INP), "w_out": G["w_out"].reshape(N_DEV, D // N_DEV, D),
            "mla_w_uq": duq, "mla_w_ukv": dukv, "conv_w": _shard_cols(G["conv_w"])}


def kernel(x, meta, ffn1_w_gate, ffn1_w_up, ffn1_w_down, ln1_g, ln1_b, w_in, conv_w, conv_b, dt_bias, a_log, d_skip, ssd_norm_g, fox_f_b, mla_q_norm_g, mla_w_uq, mla_kv_norm_g, mla_w_ukv, w_out, ln2_g, ln2_b, ffn2_w_gate, ffn2_w_up, ffn2_w_down, ln3_g, ln3_b, loss_target, m_meta, m_ffn1_w_gate, m_ffn1_w_up, m_ffn1_w_down, m_ln1_g, m_ln1_b, m_w_in, m_conv_w, m_conv_b, m_dt_bias, m_a_log, m_d_skip, m_ssd_norm_g, m_fox_f_b, m_mla_q_norm_g, m_mla_w_uq, m_mla_kv_norm_g, m_mla_w_ukv, m_w_out, m_ln2_g, m_ln2_b, m_ffn2_w_gate, m_ffn2_w_up, m_ffn2_w_down, m_ln3_g, m_ln3_b, v_meta, v_ffn1_w_gate, v_ffn1_w_up, v_ffn1_w_down, v_ln1_g, v_ln1_b, v_w_in, v_conv_w, v_conv_b, v_dt_bias, v_a_log, v_d_skip, v_ssd_norm_g, v_fox_f_b, v_mla_q_norm_g, v_mla_w_uq, v_mla_kv_norm_g, v_mla_w_ukv, v_w_out, v_ln2_g, v_ln2_b, v_ffn2_w_gate, v_ffn2_w_up, v_ffn2_w_down, v_ln3_g, v_ln3_b):
    vals = (meta, ffn1_w_gate, ffn1_w_up, ffn1_w_down, ln1_g, ln1_b, w_in, conv_w, conv_b, dt_bias, a_log, d_skip, ssd_norm_g, fox_f_b, mla_q_norm_g, mla_w_uq, mla_kv_norm_g, mla_w_ukv, w_out, ln2_g, ln2_b, ffn2_w_gate, ffn2_w_up, ffn2_w_down, ln3_g, ln3_b)
    moms = (m_meta, m_ffn1_w_gate, m_ffn1_w_up, m_ffn1_w_down, m_ln1_g, m_ln1_b, m_w_in, m_conv_w, m_conv_b, m_dt_bias, m_a_log, m_d_skip, m_ssd_norm_g, m_fox_f_b, m_mla_q_norm_g, m_mla_w_uq, m_mla_kv_norm_g, m_mla_w_ukv, m_w_out, m_ln2_g, m_ln2_b, m_ffn2_w_gate, m_ffn2_w_up, m_ffn2_w_down, m_ln3_g, m_ln3_b)
    vars_ = (v_meta, v_ffn1_w_gate, v_ffn1_w_up, v_ffn1_w_down, v_ln1_g, v_ln1_b, v_w_in, v_conv_w, v_conv_b, v_dt_bias, v_a_log, v_d_skip, v_ssd_norm_g, v_fox_f_b, v_mla_q_norm_g, v_mla_w_uq, v_mla_kv_norm_g, v_mla_w_ukv, v_w_out, v_ln2_g, v_ln2_b, v_ffn2_w_gate, v_ffn2_w_up, v_ffn2_w_down, v_ln3_g, v_ln3_b)
    P = dict(zip(_NAMES, vals))
    M = dict(zip(_NAMES, moms))
    V = dict(zip(_NAMES, vars_))
    me = 4 * lax.axis_index("x") + 2 * lax.axis_index("y") + lax.axis_index("c")

    me_arr = me.astype(jnp.int32).reshape(1)
    for n in _FFN_T:
        P[n], M[n], V[n] = (jnp.swapaxes(a[n], 1, 2) for a in (P, M, V))
    src = dict(P)
    src["w_in"] = w_in_to_padded(P["w_in"])
    order = [("meta", 0)] + [(n, l) for l in range(NL) for names in _STAGES.values() for n in names]
    zone_of = {nl_: i for i, nl_ in enumerate(order)}
    zones = [place_own(P["meta"][None], 0, F32, me_arr)]
    zones += [place_own(src[n], l, F32 if n == "conv_w" else BF16, me_arr) for n, l in order[1:]]
    hg = exchange_start("gather", zones, "gather_start")
    meta_full = _unshard_cols(exchange_wait(hg, [0], "gather_wait_meta", hg["token"])[1][0])

    def getw(l, stage, after):
        names = _STAGES[stage]
        lands = exchange_wait(hg, [zone_of[(n, l)] for n in names], f"gather_wait_{l}_{stage}", after)[1]
        return stage_weights(l, stage, dict(zip(names, lands)), P)

    sent = {}

    def emit(l, stage, G):
        bg = big_grads(G, stage)
        sent[(l, stage)] = exchange_start("scatter", [bg[n] for n in _STAGES[stage]], f"scatter_start_{l}_{stage}")
        return sent[(l, stage)]["token"]

    loss, gx, gmeta, grads = local_step(x[0], loss_target[0], meta_full, getw, emit)

    small = jnp.concatenate([pack_small({n: jnp.stack([small_grads(g)[n] for g in grads]) for n in _SMALL}), gmeta], axis=0)
    hs = exchange_start("gather", [place_own(small[None], 0, F32, me_arr)], "small_start")

    out = {}
    after = hs["token"]
    for stage in ("ffn2", "mix", "ffn1"):
        names = _STAGES[stage]
        got = [exchange_wait(sent[(l, stage)], list(range(len(names))), f"scatter_wait_{l}_{stage}", after)
               for l in range(NL - 1, -1, -1)][::-1]
        for i, n in enumerate(names):
            own = [got[l][0][i] for l in range(NL)]
            recv = [got[l][1][i] for l in range(NL)]
            if n == "w_in":
                g = jnp.stack([w_in_from_padded(sum_slots(recv[l], own[l], me_arr)) for l in range(NL)])
                out[n] = (g,) + adamw(P[n], M[n], V[n], g=g)
            else:
                out[n] = adamw(P[n], M[n], V[n], recv=recv, own=own, me_arr=me_arr)
                if n in _FFN_T:
                    out[n] = tuple(jnp.swapaxes(a, 1, 2) for a in out[n])
        after = out[names[-1]][1]
    gsmall = sum_slots(exchange_wait(hs, [0], "small_wait", after)[1][0])
    gm = lax.dynamic_slice(gsmall[NL * _SMALL_ROWS:], (0, me * (D // N_DEV)), (N_META, D // N_DEV))
    out["meta"] = (gm,) + adamw(P["meta"], M["meta"], V["meta"], g=gm)
    gs = gsmall[:NL * _SMALL_ROWS]
    sd, sm_, sv_ = adamw(pack_small(P), pack_small(M), pack_small(V), g=gs)
    ups = [unpack_small(a, P) for a in (gs, sd, sm_, sv_)]
    for n in _SMALL:
        out[n] = tuple(u[n] for u in ups)

    loss_all = lax.psum(loss[0, 0], ("x", "y", "c"))
    flat = [loss_all, gx[None]]
    for k in range(4):
        flat += [out[n][k] for n in _NAMES]
    return tuple(flat)
```

```python
import functools

import jax
import jax.numpy as jnp
from jax import lax
from jax.experimental import pallas as pl
from jax.experimental.pallas import tpu as pltpu

F32, BF16 = jnp.float32, jnp.bfloat16
HI = lax.Precision.HIGHEST

N_DEV = 8
D = 1024
NL = 2
N_META = 16
BLK = 128
PAD = BLK - N_META
D_FF = 2816
HS = D_FF // N_DEV
SSD_H, SSD_P, SSD_N, SSD_G = 8, 64, 64, 2
SSD_D = SSD_H * SSD_P
CONV_K = 4
CONV_D = SSD_D + 2 * SSD_G * SSD_N
FOX_H, FOX_DH = 4, 64
MLA_H, MLA_QL, MLA_KVL, MLA_NOPE, MLA_ROPE, MLA_V = 4, 256, 128, 64, 32, 64
N_IN = 2476
C_Z, C_XBC, C_FQ, C_FK, C_FV, C_CQ, C_CKV, C_SM, N_INP = 0, 512, 1280, 1536, 1792, 2048, 2304, 2432, 2560
SM_DT, SM_F, SM_KR = 0, 8, 64
ALPHA = (2 * NL) ** 0.25
EPS = 1e-5
NEG = -1e30
LR, B1, B2, AEPS, WD, STEP = 0.001, 0.9, 0.999, 1e-08, 0.01, 10
VMEM_MB = 56


def _cp(*sem):
    return pltpu.CompilerParams(dimension_semantics=sem, vmem_limit_bytes=VMEM_MB << 20)


def _nn(a, b):
    return lax.dot_general(a, b, (((1,), (0,)), ((), ())), preferred_element_type=F32)


def _nt(a, b):
    return lax.dot_general(a, b, (((1,), (1,)), ((), ())), preferred_element_type=F32)


def _tn(a, b):
    return lax.dot_general(a, b, (((0,), (0,)), ((), ())), preferred_element_type=F32)


def _nn_hi(a, b):
    return lax.dot_general(a, b, (((1,), (0,)), ((), ())), precision=HI, preferred_element_type=F32)


def _row_tile(t):
    for d in range(640, 15, -16):
        if t % d == 0:
            return d
    raise ValueError(t)


def _sig(x):
    return 1.0 / (1.0 + jnp.exp(-x))


def _tri(lower=True):
    r = lax.broadcasted_iota(jnp.int32, (BLK, BLK), 0)
    c = lax.broadcasted_iota(jnp.int32, (BLK, BLK), 1)
    return (r >= c) if lower else (r <= c)


def build_h0(meta_full, x):
    s = x.shape[0]
    nb = s // BLK + 1

    def body(m_ref, x_ref, h_ref, hb_ref):
        i = pl.program_id(0)

        @pl.when(i == 0)
        def _():
            h = jnp.concatenate([jnp.zeros((PAD, D), F32), m_ref[...]], axis=0)
            h_ref[...] = h
            hb_ref[...] = h.astype(BF16)

        @pl.when(i > 0)
        def _():
            h_ref[...] = x_ref[...]
            hb_ref[...] = x_ref[...].astype(BF16)

    return pl.pallas_call(
        body, name="build_h0", grid=(nb,),
        in_specs=[pl.BlockSpec((N_META, D), lambda i: (0, 0)),
                  pl.BlockSpec((BLK, D), lambda i: (jnp.maximum(i - 1, 0), 0))],
        out_specs=[pl.BlockSpec((BLK, D), lambda i: (i, 0))] * 2,
        out_shape=[jax.ShapeDtypeStruct((nb * BLK, D), F32), jax.ShapeDtypeStruct((nb * BLK, D), BF16)],
        compiler_params=_cp("arbitrary"),
    )(meta_full, x)


FT = 256


def _layer_norm(r, gamma, beta):
    mu = jnp.mean(r, axis=1, keepdims=True)
    xc = r - mu
    var = jnp.mean(xc * xc, axis=1, keepdims=True)
    return xc * lax.rsqrt(var + EPS) * gamma + beta


def ffn_fwd(hb, res, wg, wu, wd, gamma, beta):
    t = hb.shape[0]
    f = wg.shape[0]
    tm = _row_tile(t)
    nj = f // FT

    def body(h_ref, res_ref, wg_ref, wu_ref, wd_ref, g_ref, be_ref, u_ref, v_ref, r_ref, y_ref, yb_ref, acc, us, vs):
        j = pl.program_id(1)

        def up():
            h = h_ref[...]
            u = _nt(h, wg_ref[...])
            v = _nt(h, wu_ref[...])
            u_ref[...] = u.astype(BF16)
            v_ref[...] = v.astype(BF16)
            return u, v

        def down():
            u, v = us[...], vs[...]
            return _nn((u * _sig(u) * v).astype(BF16), wd_ref[...])

        @pl.when(j == 0)
        def _():
            us[...], vs[...] = up()
            acc[...] = jnp.zeros_like(acc)

        @pl.when((j > 0) & (j < nj))
        def _():
            d = down()
            u, v = up()
            acc[...] += d
            us[...] = u
            vs[...] = v

        @pl.when(j == nj)
        def _():
            r = ALPHA * res_ref[...] + 0.5 * (acc[...] + down())
            y = _layer_norm(r, g_ref[...], be_ref[...])
            r_ref[...] = r
            y_ref[...] = y
            yb_ref[...] = y.astype(BF16)

    row = pl.BlockSpec((tm, D), lambda i, j: (i, 0))
    vec = pl.BlockSpec((1, D), lambda i, j: (0, 0))
    wup = pl.BlockSpec((FT, D), lambda i, j: (jnp.minimum(j, nj - 1), 0))
    wdn = pl.BlockSpec((FT, D), lambda i, j: (jnp.maximum(j - 1, 0), 0))
    act = pl.BlockSpec((tm, FT), lambda i, j: (i, jnp.minimum(j, nj - 1)))
    return pl.pallas_call(
        body, name="ffn_fwd", grid=(t // tm, nj + 1),
        in_specs=[row, row, wup, wup, wdn, vec, vec],
        out_specs=[act, act, row, row, row],
        out_shape=[jax.ShapeDtypeStruct((t, f), BF16), jax.ShapeDtypeStruct((t, f), BF16),
                   jax.ShapeDtypeStruct((t, D), F32), jax.ShapeDtypeStruct((t, D), F32),
                   jax.ShapeDtypeStruct((t, D), BF16)],
        scratch_shapes=[pltpu.VMEM((tm, D), F32), pltpu.VMEM((tm, FT), F32), pltpu.VMEM((tm, FT), F32)],
        compiler_params=_cp("arbitrary", "arbitrary"),
    )(hb, res, wg, wu, wd, gamma, beta)


def ffn_bwd_act(dfb, u, v, wg, wu, wd):
    t, f = u.shape
    tm = _row_tile(t)

    nj = f // FT

    def body(df_ref, u_ref, v_ref, wg_ref, wu_ref, wd_ref, du_ref, dv_ref, dh_ref, das):
        j = pl.program_id(1)

        def first():
            return _nt(df_ref[...], wd_ref[...])

        def second():
            da = das[...]
            uu = u_ref[...].astype(F32)
            sg = _sig(uu)
            du = (da * v_ref[...].astype(F32) * (sg * (1.0 + uu * (1.0 - sg)))).astype(BF16)
            dv = (da * uu * sg).astype(BF16)
            du_ref[...] = du
            dv_ref[...] = dv
            return _nn(du, wg_ref[...]) + _nn(dv, wu_ref[...])

        @pl.when(j == 0)
        def _():
            das[...] = first()
            dh_ref[...] = jnp.zeros_like(dh_ref)

        @pl.when((j > 0) & (j < nj))
        def _():
            tot = second()
            da = first()
            dh_ref[...] += tot
            das[...] = da

        @pl.when(j == nj)
        def _():
            dh_ref[...] += second()

    row = pl.BlockSpec((tm, D), lambda i, j: (i, 0))
    wfirst = pl.BlockSpec((FT, D), lambda i, j: (jnp.minimum(j, nj - 1), 0))
    wsecond = pl.BlockSpec((FT, D), lambda i, j: (jnp.maximum(j - 1, 0), 0))
    act = pl.BlockSpec((tm, FT), lambda i, j: (i, jnp.maximum(j - 1, 0)))
    return pl.pallas_call(
        body, name="ffn_bwd_act", grid=(t // tm, nj + 1),
        in_specs=[row, act, act, wsecond, wsecond, wfirst],
        out_specs=[act, act, row],
        out_shape=[jax.ShapeDtypeStruct((t, f), BF16), jax.ShapeDtypeStruct((t, f), BF16),
                   jax.ShapeDtypeStruct((t, D), F32)],
        scratch_shapes=[pltpu.VMEM((tm, FT), F32)],
        compiler_params=_cp("arbitrary", "arbitrary"),
    )(dfb, u, v, wg, wu, wd)


def ffn_fwd_seq(x, ln_in, wg, wu, wd, ln_out):
    t = x.shape[0]
    f = wg.shape[0]
    nj, nr = f // FT, t // _row_tile(t)
    rc = t // nr
    plain = ln_in is None
    gi, bi = ln_out if plain else ln_in

    def body(x_hbm, gi_ref, bi_ref, go_ref, bo_ref, wg_ref, wu_ref, wd_ref, u_ref, v_ref, r_hbm, yb_hbm,
             acc, hbs, xbuf, sem_in, sem_out):
        j = pl.program_id(0)

        @pl.when(j == 0)
        def _():
            def fetch(k):
                return pltpu.make_async_copy(x_hbm.at[pl.ds(k * rc, rc)], xbuf.at[k % 2], sem_in.at[k % 2])

            fetch(0).start()
            for k in range(nr):
                if k + 1 < nr:
                    fetch(k + 1).start()
                fetch(k).wait()
                h = xbuf[k % 2]
                if not plain:
                    h = _layer_norm(h, gi_ref[...], bi_ref[...])
                acc[k * rc:(k + 1) * rc, :] = ALPHA * h
                hbs[k * rc:(k + 1) * rc, :] = h.astype(BF16)

        for k in range(nr):
            sl = slice(k * rc, (k + 1) * rc)
            h = hbs[sl, :]
            u = _nt(h, wg_ref[...])
            v = _nt(h, wu_ref[...])
            u_ref[sl, :] = u.astype(BF16)
            v_ref[sl, :] = v.astype(BF16)
            acc[sl, :] += _nn((0.5 * u * _sig(u) * v).astype(BF16), wd_ref[...])

        @pl.when(j == nj - 1)
        def _():
            r_cp = pltpu.make_async_copy(acc, r_hbm, sem_out.at[0])
            r_cp.start()
            for k in range(nr):
                sl = slice(k * rc, (k + 1) * rc)
                hbs[sl, :] = _layer_norm(acc[sl, :], go_ref[...], bo_ref[...]).astype(BF16)
            y_cp = pltpu.make_async_copy(hbs, yb_hbm, sem_out.at[1])
            y_cp.start()
            r_cp.wait()
            y_cp.wait()

    vec = pl.BlockSpec((1, D), lambda j: (0, 0))
    wsp = pl.BlockSpec((FT, D), lambda j: (j, 0))
    act = pl.BlockSpec((None, t, FT), lambda j: (j, 0, 0))
    return pl.pallas_call(
        body, name="ffn_fwd_seq", grid=(nj,),
        in_specs=[_ANY, vec, vec, vec, vec, wsp, wsp, wsp],
        out_specs=[act, act, _ANY, _ANY],
        out_shape=[jax.ShapeDtypeStruct((nj, t, FT), BF16), jax.ShapeDtypeStruct((nj, t, FT), BF16),
                   jax.ShapeDtypeStruct((t, D), F32), jax.ShapeDtypeStruct((t, D), BF16)],
        scratch_shapes=[pltpu.VMEM((t, D), F32), pltpu.VMEM((t, D), BF16), pltpu.VMEM((2, rc, D), F32),
                        pltpu.SemaphoreType.DMA((2,)), pltpu.SemaphoreType.DMA((2,))],
        compiler_params=_cp("arbitrary"),
    )(x, gi, bi, ln_out[0], ln_out[1], wg, wu, wd)


def ffn_bwd_seq(parts, r, gamma, hb, u, v, wg, wu, wd, after=None):
    nj, t, _ = u.shape
    f = nj * FT
    nr = t // _row_tile(t)
    rc = t // nr
    nc = t // BLK
    scales = [s for _, s in parts]
    npart = len(parts)
    extra = [] if after is None else [after]

    def body(*refs):
        refs = refs[len(extra):]
        p_hbm, refs = refs[:npart], refs[npart:]
        (r_hbm, g_ref, hb_hbm, u_ref, v_ref, wg_ref, wu_ref, wd_ref, dh_hbm, dwg_ref, dwu_ref, dwd_ref, dg_ref, db_ref,
         dfs, hbt, dft, dhacc, dus, dvs, acs, pbuf, rbuf, hbuf, sems, sem_out) = refs
        j = pl.program_id(0)

        @pl.when(j == 0)
        def _():
            def fetch(c):
                rows = pl.ds(c * BLK, BLK)
                cps = [pltpu.make_async_copy(p_hbm[p].at[rows], pbuf.at[c % 2, p], sems.at[c % 2, p]) for p in range(npart)]
                cps.append(pltpu.make_async_copy(r_hbm.at[rows], rbuf.at[c % 2], sems.at[c % 2, npart]))
                cps.append(pltpu.make_async_copy(hb_hbm.at[rows], hbuf.at[c % 2], sems.at[c % 2, npart + 1]))
                return cps

            for cp in fetch(0):
                cp.start()
            dg = jnp.zeros((1, D), F32)
            db = jnp.zeros((1, D), F32)
            for c in range(nc):
                if c + 1 < nc:
                    for cp in fetch(c + 1):
                        cp.start()
                for cp in fetch(c):
                    cp.wait()
                sl = slice(c * BLK, (c + 1) * BLK)
                dy = scales[0] * pbuf[c % 2, 0]
                for p in range(1, npart):
                    dy += scales[p] * pbuf[c % 2, p]
                rr = rbuf[c % 2]
                xc = rr - jnp.mean(rr, axis=1, keepdims=True)
                rstd = lax.rsqrt(jnp.mean(xc * xc, axis=1, keepdims=True) + EPS)
                xh = xc * rstd
                dxh = dy * g_ref[...]
                dr = rstd * (dxh - jnp.mean(dxh, axis=1, keepdims=True) - xh * jnp.mean(dxh * xh, axis=1, keepdims=True))
                dg += jnp.sum(dy * xh, axis=0, keepdims=True)
                db += jnp.sum(dy, axis=0, keepdims=True)
                dhacc[sl, :] = ALPHA * dr
                dfc = (0.5 * dr).astype(BF16)
                dfs[sl, :] = dfc
                dft[:, sl] = dfc.T
                hbt[:, sl] = hbuf[c % 2].T
            dg_ref[...] = dg
            db_ref[...] = db

        for k in range(nr):
            sl = slice(k * rc, (k + 1) * rc)
            da = _nt(dfs[sl, :], wd_ref[...])
            uu = u_ref[sl, :].astype(F32)
            vv = v_ref[sl, :].astype(F32)
            sg = _sig(uu)
            du = (da * vv * (sg * (1.0 + uu * (1.0 - sg)))).astype(BF16)
            dv = (da * uu * sg).astype(BF16)
            dus[sl, :] = du
            dvs[sl, :] = dv
            acs[sl, :] = (uu * sg * vv).astype(BF16)
            dhacc[sl, :] += _nn(du, wg_ref[...]) + _nn(dv, wu_ref[...])
        dwg_ref[...] = _nn(hbt[...], dus[...]).T.astype(BF16)
        dwu_ref[...] = _nn(hbt[...], dvs[...]).T.astype(BF16)
        dwd_ref[...] = _nn(dft[...], acs[...]).T.astype(BF16)

        @pl.when(j == nj - 1)
        def _():
            cp = pltpu.make_async_copy(dhacc, dh_hbm, sem_out.at[0])
            cp.start()
            cp.wait()

    vec = pl.BlockSpec((1, D), lambda j: (0, 0))
    wsp = pl.BlockSpec((FT, D), lambda j: (j, 0))
    act = pl.BlockSpec((None, t, FT), lambda j: (j, 0, 0))
    return pl.pallas_call(
        body, name="ffn_bwd_seq", grid=(nj,),
        in_specs=[_ANY] * (len(extra) + npart + 1) + [vec, _ANY, act, act, wsp, wsp, wsp],
        out_specs=[_ANY, wsp, wsp, wsp, vec, vec],
        out_shape=[jax.ShapeDtypeStruct((t, D), F32)] + [jax.ShapeDtypeStruct((f, D), BF16)] * 3
        + [jax.ShapeDtypeStruct((1, D), F32)] * 2,
        scratch_shapes=[pltpu.VMEM((t, D), BF16), pltpu.VMEM((D, t), BF16), pltpu.VMEM((D, t), BF16),
                        pltpu.VMEM((t, D), F32), pltpu.VMEM((t, FT), BF16), pltpu.VMEM((t, FT), BF16),
                        pltpu.VMEM((t, FT), BF16), pltpu.VMEM((2, npart, BLK, D), F32), pltpu.VMEM((2, BLK, D), F32),
                        pltpu.VMEM((2, BLK, D), BF16), pltpu.SemaphoreType.DMA((2, npart + 2)),
                        pltpu.SemaphoreType.DMA((1,))],
        compiler_params=_cp("arbitrary"),
    )(*extra, *[p for p, _ in parts], r, gamma, hb, u, v, wg, wu, wd)


def mm_res_ln(a, b, x, ln_in, ln_out):
    t, k = a.shape
    tm = _row_tile(t)

    def body(a_ref, b_ref, x_ref, gi_ref, bi_ref, go_ref, bo_ref, r_ref, yb_ref):
        r = ALPHA * _layer_norm(x_ref[...], gi_ref[...], bi_ref[...]) + _nn(a_ref[...], b_ref[...])
        r_ref[...] = r
        yb_ref[...] = _layer_norm(r, go_ref[...], bo_ref[...]).astype(BF16)

    row = pl.BlockSpec((tm, D), lambda i: (i, 0))
    vec = pl.BlockSpec((1, D), lambda i: (0, 0))
    return pl.pallas_call(
        body, name="mm_res_ln", grid=(t // tm,),
        in_specs=[pl.BlockSpec((tm, k), lambda i: (i, 0)), pl.BlockSpec((k, D), lambda i: (0, 0)), row, vec, vec, vec, vec],
        out_specs=[row, row],
        out_shape=[jax.ShapeDtypeStruct((t, D), F32), jax.ShapeDtypeStruct((t, D), BF16)],
        compiler_params=_cp("arbitrary"),
    )(a, b, x, ln_in[0], ln_in[1], ln_out[0], ln_out[1])


def mm_nn(a, b, tn=512):
    t, k = a.shape
    n = b.shape[1]
    tm = _row_tile(t)

    def body(a_ref, b_ref, o_ref):
        o_ref[...] = _nn(a_ref[...], b_ref[...])

    return pl.pallas_call(
        body, name="mm_nn", grid=(n // tn, t // tm),
        in_specs=[pl.BlockSpec((tm, k), lambda j, i: (i, 0)), pl.BlockSpec((k, tn), lambda j, i: (0, j))],
        out_specs=pl.BlockSpec((tm, tn), lambda j, i: (i, j)),
        out_shape=jax.ShapeDtypeStruct((t, n), F32),
        compiler_params=_cp("arbitrary", "arbitrary"),
    )(a, b)


def mm_nt_reduce(pairs, n):
    g, t, _ = pairs[0][0].shape
    tm = _row_tile(t)
    npair = len(pairs)

    def body(*refs):
        o_ref = refs[-1]
        gi = pl.program_id(1)
        tot = _nt(refs[0][...], refs[1][...])
        for p in range(1, npair):
            tot += _nt(refs[2 * p][...], refs[2 * p + 1][...])

        @pl.when(gi == 0)
        def _():
            o_ref[...] = tot

        @pl.when(gi > 0)
        def _():
            o_ref[...] += tot

    in_specs, args = [], []
    for x, w in pairs:
        k = x.shape[2]
        in_specs += [pl.BlockSpec((None, tm, k), lambda i, gi: (gi, i, 0)),
                     pl.BlockSpec((None, n, k), lambda i, gi: (gi, 0, 0))]
        args += [x, w]
    return pl.pallas_call(
        body, name="mm_nt_reduce", grid=(t // tm, g),
        in_specs=in_specs, out_specs=pl.BlockSpec((tm, n), lambda i, gi: (i, 0)),
        out_shape=jax.ShapeDtypeStruct((t, n), F32),
        compiler_params=_cp("arbitrary", "arbitrary"),
    )(*args)


def mm_tn(x, y, out_dtype=BF16):
    gx, t, k = x.shape
    gy, _, n = y.shape
    g = max(gx, gy)
    tm = _row_tile(t)
    nt = t // tm

    def body(x_ref, y_ref, o_ref, acc):
        i = pl.program_id(1)

        @pl.when(i == 0)
        def _():
            acc[...] = jnp.zeros_like(acc)

        acc[...] += _tn(x_ref[...], y_ref[...])

        @pl.when(i == nt - 1)
        def _():
            o_ref[...] = acc[...].astype(out_dtype)

    return pl.pallas_call(
        body, name="mm_tn", grid=(g, nt),
        in_specs=[pl.BlockSpec((None, tm, k), (lambda gi, i: (gi, i, 0)) if gx > 1 else (lambda gi, i: (0, i, 0))),
                  pl.BlockSpec((None, tm, n), (lambda gi, i: (gi, i, 0)) if gy > 1 else (lambda gi, i: (0, i, 0)))],
        out_specs=pl.BlockSpec((None, k, n), lambda gi, i: (gi, 0, 0)),
        out_shape=jax.ShapeDtypeStruct((g, k, n), out_dtype),
        scratch_shapes=[pltpu.VMEM((k, n), F32)],
        compiler_params=_cp("arbitrary", "arbitrary"),
    )(x, y)


def ln_bwd(parts, r, gamma, out_scale, after=None):
    t = r.shape[0]
    tm = _row_tile(t)
    scales = [s for _, s in parts]
    npart = len(parts)
    extra = [] if after is None else [after]

    def body(*refs):
        refs = refs[len(extra):]
        r_ref, g_ref = refs[npart], refs[npart + 1]
        dr_ref, drb_ref, dg_ref, db_ref = refs[npart + 2:]
        i = pl.program_id(0)
        dy = scales[0] * refs[0][...]
        for p in range(1, npart):
            dy += scales[p] * refs[p][...]
        rr = r_ref[...]
        mu = jnp.mean(rr, axis=1, keepdims=True)
        xc = rr - mu
        rstd = lax.rsqrt(jnp.mean(xc * xc, axis=1, keepdims=True) + EPS)
        xh = xc * rstd
        dxh = dy * g_ref[...]
        m1 = jnp.mean(dxh, axis=1, keepdims=True)
        m2 = jnp.mean(dxh * xh, axis=1, keepdims=True)
        dr = rstd * (dxh - m1 - xh * m2)
        dr_ref[...] = dr
        drb_ref[...] = (out_scale * dr).astype(BF16)
        dg = jnp.sum(dy * xh, axis=0, keepdims=True)
        db = jnp.sum(dy, axis=0, keepdims=True)

        @pl.when(i == 0)
        def _():
            dg_ref[...] = dg
            db_ref[...] = db

        @pl.when(i > 0)
        def _():
            dg_ref[...] += dg
            db_ref[...] += db

    row = pl.BlockSpec((tm, D), lambda i: (i, 0))
    vec = pl.BlockSpec((1, D), lambda i: (0, 0))
    return pl.pallas_call(
        body, name="ln_bwd", grid=(t // tm,),
        in_specs=[_ANY] * len(extra) + [row] * (npart + 1) + [vec],
        out_specs=[row, row, vec, vec],
        out_shape=[jax.ShapeDtypeStruct((t, D), F32), jax.ShapeDtypeStruct((t, D), BF16),
                   jax.ShapeDtypeStruct((1, D), F32), jax.ShapeDtypeStruct((1, D), F32)],
        compiler_params=_cp("arbitrary"),
    )(*extra, *[p for p, _ in parts], r, gamma)


def loss_head(r, ln, target):
    t = r.shape[0]
    nb = t // BLK

    def body(r_ref, g_ref, b_ref, t_ref, dy_ref, l_ref):
        i = pl.program_id(0)

        @pl.when(i == 0)
        def _():
            dy_ref[...] = jnp.zeros_like(dy_ref)
            l_ref[...] = jnp.zeros_like(l_ref)

        @pl.when(i > 0)
        def _():
            err = _layer_norm(r_ref[...], g_ref[...], b_ref[...]) - t_ref[...]
            dy_ref[...] = err * (1.0 / D)
            l_ref[...] += (0.5 / D) * jnp.sum(err * err, keepdims=True)

    vec = pl.BlockSpec((1, D), lambda i: (0, 0))
    return pl.pallas_call(
        body, name="loss_head", grid=(nb,),
        in_specs=[pl.BlockSpec((BLK, D), lambda i: (i, 0)), vec, vec,
                  pl.BlockSpec((BLK, D), lambda i: (jnp.maximum(i - 1, 0), 0))],
        out_specs=[pl.BlockSpec((BLK, D), lambda i: (i, 0)), pl.BlockSpec((1, 1), lambda i: (0, 0))],
        out_shape=[jax.ShapeDtypeStruct((t, D), F32), jax.ShapeDtypeStruct((1, 1), F32)],
        compiler_params=_cp("arbitrary"),
    )(r, ln[0], ln[1], target)


def split_dh0(dh0, after=None):
    t = dh0.shape[0]
    nb = t // BLK
    extra = [] if after is None else [after]

    def body(*refs):
        a_ref, gx_ref, gm_ref = refs[len(extra):]
        i = pl.program_id(0)
        tot = a_ref[...]

        @pl.when(i == 0)
        def _():
            gm_ref[...] = tot[PAD:, :]

        @pl.when(i > 0)
        def _():
            gx_ref[...] = tot

    blk = pl.BlockSpec((BLK, D), lambda i: (i, 0))
    return pl.pallas_call(
        body, name="split_dh0", grid=(nb,),
        in_specs=[_ANY] * len(extra) + [blk],
        out_specs=[pl.BlockSpec((BLK, D), lambda i: (jnp.maximum(i - 1, 0), 0)),
                   pl.BlockSpec((N_META, D), lambda i: (0, 0))],
        out_shape=[jax.ShapeDtypeStruct((t - BLK, D), F32), jax.ShapeDtypeStruct((N_META, D), F32)],
        compiler_params=_cp("arbitrary"),
    )(*extra, dh0)


def _valid_rows(nrows, first_row):
    return (first_row + lax.broadcasted_iota(jnp.int32, (nrows, 1), 0)) >= PAD


def conv_fwd(proj, conv_w, conv_b):
    t = proj.shape[0]
    c0 = C_XBC // BLK

    def body(x_ref, w_ref, b_ref, o_ref):
        ok = _valid_rows(t, 0)
        x = jnp.where(ok, x_ref[...], 0.0)
        w = w_ref[...]
        acc = b_ref[...] + w[CONV_K - 1:CONV_K, :] * x
        for s in range(1, CONV_K):
            acc += w[CONV_K - 1 - s:CONV_K - s, :] * pltpu.roll(x, s, 0)
        o_ref[...] = jnp.where(ok, acc * _sig(acc), 0.0)

    return pl.pallas_call(
        body, name="conv_fwd", grid=(CONV_D // BLK,),
        in_specs=[pl.BlockSpec((t, BLK), lambda j: (0, c0 + j)),
                  pl.BlockSpec((CONV_K, BLK), lambda j: (0, j)), pl.BlockSpec((1, BLK), lambda j: (0, j))],
        out_specs=pl.BlockSpec((t, BLK), lambda j: (0, j)),
        out_shape=jax.ShapeDtypeStruct((t, CONV_D), F32),
        compiler_params=_cp("arbitrary"),
    )(proj, conv_w, conv_b)


def conv_bwd(dxa, proj, conv_w, conv_b):
    t = proj.shape[0]
    c0 = C_XBC // BLK

    def body(d_ref, x_ref, w_ref, b_ref, dx_ref, dw_ref, db_ref):
        ok = _valid_rows(t, 0)
        x = jnp.where(ok, x_ref[...], 0.0)
        w = w_ref[...]
        xs = [x] + [pltpu.roll(x, s, 0) for s in range(1, CONV_K)]
        acc = b_ref[...] + w[CONV_K - 1:CONV_K, :] * x
        for s in range(1, CONV_K):
            acc += w[CONV_K - 1 - s:CONV_K - s, :] * xs[s]
        sg = _sig(acc)
        dxc = jnp.where(ok, d_ref[...] * (sg * (1.0 + acc * (1.0 - sg))), 0.0)
        db_ref[...] = jnp.sum(dxc, axis=0, keepdims=True)
        dw_ref[...] = jnp.concatenate(
            [jnp.sum(dxc * xs[CONV_K - 1 - k], axis=0, keepdims=True) for k in range(CONV_K)], axis=0)
        dx = w[CONV_K - 1:CONV_K, :] * dxc
        for s in range(1, CONV_K):
            dx += w[CONV_K - 1 - s:CONV_K - s, :] * pltpu.roll(dxc, t - s, 0)
        dx_ref[...] = jnp.where(ok, dx, 0.0)

    col = pl.BlockSpec((t, BLK), lambda j: (0, j))
    return pl.pallas_call(
        body, name="conv_bwd", grid=(CONV_D // BLK,),
        in_specs=[col, pl.BlockSpec((t, BLK), lambda j: (0, c0 + j)),
                  pl.BlockSpec((CONV_K, BLK), lambda j: (0, j)), pl.BlockSpec((1, BLK), lambda j: (0, j))],
        out_specs=[col, pl.BlockSpec((CONV_K, BLK), lambda j: (0, j)), pl.BlockSpec((1, BLK), lambda j: (0, j))],
        out_shape=[jax.ShapeDtypeStruct((t, CONV_D), F32), jax.ShapeDtypeStruct((CONV_K, CONV_D), F32),
                   jax.ShapeDtypeStruct((1, CONV_D), F32)],
        compiler_params=_cp("arbitrary"),
    )(dxa, proj, conv_w, conv_b)


def _softplus(x):
    return jnp.maximum(x, 0.0) + jnp.log(1.0 + jnp.exp(-jnp.abs(x)))


def _ssd_chunk(xa, sm, dtb, alog, ok):
    dt = jnp.where(ok, _softplus(sm + dtb), 0.0)
    amat = -jnp.exp(alog)
    a = dt * amat
    ac = _nn_hi(_tri().astype(F32), a)
    act = ac.T
    return dt, amat, ac, act


def _ssd_head(xa, dt, ac, act, h, cb, sp):
    g = h // (SSD_H // SSD_G)
    xs = xa[:, SSD_P * h:SSD_P * (h + 1)]
    bg = xa[:, SSD_D + SSD_N * g:SSD_D + SSD_N * (g + 1)]
    cg = xa[:, SSD_D + SSD_G * SSD_N + SSD_N * g:SSD_D + SSD_G * SSD_N + SSD_N * (g + 1)]
    dth = dt[:, h:h + 1]
    ach = ac[:, h:h + 1]
    acth = act[h:h + 1, :]
    xdt = xs * dth
    seg = jnp.where(_tri(), jnp.exp(jnp.minimum(ach - acth, 0.0)), 0.0)
    m = cb * seg
    yd = _nn(m.astype(BF16), xdt.astype(BF16))
    last = ac[BLK - 1:BLK, h:h + 1]
    dec = jnp.exp(last - ach)
    e = jnp.exp(ach)
    yo = _nn(cg.astype(BF16), sp.astype(BF16)) * e
    return xs, bg, cg, dth, ach, xdt, seg, m, yd, last, dec, e, yo


def ssd_fwd(xa, proj, dtb, alog, dskip, normg):
    t = xa.shape[0]
    nb = t // BLK
    gw = SSD_D // SSD_G

    def body(xa_ref, z_ref, sm_ref, dtb_ref, al_ref, ds_ref, ng_ref, y_ref, sp_ref, st):
        c = pl.program_id(0)

        @pl.when(c == 0)
        def _():
            st[...] = jnp.zeros_like(st)

        ok = _valid_rows(BLK, c * BLK)
        xa = xa_ref[...]
        dt, _, ac, act = _ssd_chunk(xa, sm_ref[...], dtb_ref[...], al_ref[...], ok)
        sp_ref[...] = st[...]
        ys = []
        cbs = {}
        for h in range(SSD_H):
            g = h // (SSD_H // SSD_G)
            if g not in cbs:
                bg = xa[:, SSD_D + SSD_N * g:SSD_D + SSD_N * (g + 1)]
                cg = xa[:, SSD_D + SSD_G * SSD_N + SSD_N * g:SSD_D + SSD_G * SSD_N + SSD_N * (g + 1)]
                cbs[g] = _nt(cg.astype(BF16), bg.astype(BF16))
            sp = st[:, SSD_P * h:SSD_P * (h + 1)]
            xs, bg, cg, dth, ach, xdt, seg, m, yd, last, dec, e, yo = _ssd_head(xa, dt, ac, act, h, cbs[g], sp)
            sloc = _tn((bg * dec).astype(BF16), xdt.astype(BF16))
            st[:, SSD_P * h:SSD_P * (h + 1)] = jnp.exp(last) * sp + sloc
            ys.append(yd + yo + ds_ref[:, h:h + 1] * xs)
        y = jnp.concatenate(ys, axis=1)
        z = z_ref[...]
        yg = y * (z * _sig(z))
        outs = []
        for g in range(SSD_G):
            v = yg[:, gw * g:gw * (g + 1)]
            outs.append(v * lax.rsqrt(jnp.mean(v * v, axis=1, keepdims=True) + EPS))
        y_ref[...] = (jnp.concatenate(outs, axis=1) * ng_ref[...]).astype(BF16)

    vec = pl.BlockSpec((1, BLK), lambda c: (0, 0))
    return pl.pallas_call(
        body, name="ssd_fwd", grid=(nb,),
        in_specs=[pl.BlockSpec((BLK, CONV_D), lambda c: (c, 0)),
                  pl.BlockSpec((BLK, SSD_D), lambda c: (c, C_Z // SSD_D)),
                  pl.BlockSpec((BLK, BLK), lambda c: (c, C_SM // BLK)),
                  vec, vec, vec, pl.BlockSpec((1, SSD_D), lambda c: (0, 0))],
        out_specs=[pl.BlockSpec((BLK, SSD_D), lambda c: (c, 0)),
                   pl.BlockSpec((None, SSD_N, SSD_D), lambda c: (c, 0, 0))],
        out_shape=[jax.ShapeDtypeStruct((t, SSD_D), BF16), jax.ShapeDtypeStruct((nb, SSD_N, SSD_D), F32)],
        scratch_shapes=[pltpu.VMEM((SSD_N, SSD_D), F32)],
        compiler_params=_cp("arbitrary"),
    )(xa, proj, proj, dtb, alog, dskip, normg)


def _lane_put(col, lane):
    li = lax.broadcasted_iota(jnp.int32, (col.shape[0], BLK), 1)
    return jnp.where(li == lane, col, 0.0)


def ssd_bwd(dmix, xa, proj, sprev, dtb, alog, dskip, normg):
    t = xa.shape[0]
    nb = t // BLK
    gw = SSD_D // SSD_G
    rev = lambda c: nb - 1 - c

    def body(dy_ref, xa_ref, z_ref, sm_ref, sp_ref, dtb_ref, al_ref, ds_ref, ng_ref,
             dxa_ref, dz_ref, dsm_ref, dng_ref, dds_ref, dal_ref, ddtb_ref, dst):
        c = pl.program_id(0)

        @pl.when(c == 0)
        def _():
            dst[...] = jnp.zeros_like(dst)
            dng_ref[...] = jnp.zeros_like(dng_ref)
            dds_ref[...] = jnp.zeros_like(dds_ref)
            dal_ref[...] = jnp.zeros_like(dal_ref)
            ddtb_ref[...] = jnp.zeros_like(ddtb_ref)

        ok = _valid_rows(BLK, rev(c) * BLK)
        xa = xa_ref[...]
        sm = sm_ref[...]
        dt, amat, ac, act = _ssd_chunk(xa, sm, dtb_ref[...], al_ref[...], ok)
        tri = _tri()
        rowi = lax.broadcasted_iota(jnp.int32, (BLK, 1), 0)
        cbs, heads, ys = {}, [], []
        for h in range(SSD_H):
            g = h // (SSD_H // SSD_G)
            if g not in cbs:
                bg = xa[:, SSD_D + SSD_N * g:SSD_D + SSD_N * (g + 1)]
                cg = xa[:, SSD_D + SSD_G * SSD_N + SSD_N * g:SSD_D + SSD_G * SSD_N + SSD_N * (g + 1)]
                cbs[g] = _nt(cg.astype(BF16), bg.astype(BF16))
            sp = sp_ref[:, SSD_P * h:SSD_P * (h + 1)]
            hd = _ssd_head(xa, dt, ac, act, h, cbs[g], sp)
            heads.append(hd)
            ys.append(hd[8] + hd[12] + ds_ref[:, h:h + 1] * hd[0])
        y = jnp.concatenate(ys, axis=1)
        z = z_ref[...]
        sgz = _sig(z)
        siluz = z * sgz
        yg = y * siluz
        dout = dy_ref[...]
        ng = ng_ref[...]
        dygs, xhs = [], []
        for g in range(SSD_G):
            v = yg[:, gw * g:gw * (g + 1)]
            rr = lax.rsqrt(jnp.mean(v * v, axis=1, keepdims=True) + EPS)
            xh = v * rr
            dxh = dout[:, gw * g:gw * (g + 1)] * ng[:, gw * g:gw * (g + 1)]
            dygs.append(rr * (dxh - xh * jnp.mean(dxh * xh, axis=1, keepdims=True)))
            xhs.append(xh)
        dyg = jnp.concatenate(dygs, axis=1)
        dng_ref[...] += jnp.sum(dout * jnp.concatenate(xhs, axis=1), axis=0, keepdims=True)
        dy = dyg * siluz
        dz_ref[...] = dyg * y * (sgz * (1.0 + z * (1.0 - sgz)))

        dxs_l = []
        db_g = [jnp.zeros((BLK, SSD_N), F32) for _ in range(SSD_G)]
        dc_g = [jnp.zeros((BLK, SSD_N), F32) for _ in range(SSD_G)]
        dac_all = jnp.zeros((BLK, BLK), F32)
        ddt_all = jnp.zeros((BLK, BLK), F32)
        dds_row = jnp.zeros((1, BLK), F32)
        lane1 = lax.broadcasted_iota(jnp.int32, (1, BLK), 1)
        for h in range(SSD_H):
            g = h // (SSD_H // SSD_G)
            xs, bg, cg, dth, ach, xdt, seg, m, yd, last, dec, e, yo = heads[h]
            sp = sp_ref[:, SSD_P * h:SSD_P * (h + 1)]
            dyh = dy[:, SSD_P * h:SSD_P * (h + 1)]
            dyb = dyh.astype(BF16)
            xdtb = xdt.astype(BF16)
            dds_row += jnp.where(lane1 == h, jnp.sum(dyh * xs, keepdims=True), 0.0)
            dxs = ds_ref[:, h:h + 1] * dyh
            dyo = (dyh * e).astype(BF16)
            dc_g[g] += _nt(dyo, sp.astype(BF16))
            dsp = _tn(cg.astype(BF16), dyo)
            dac = jnp.sum(dyh * yo, axis=1, keepdims=True)
            dsn = dst[:, SSD_P * h:SSD_P * (h + 1)]
            gl = jnp.exp(last)
            dst[:, SSD_P * h:SSD_P * (h + 1)] = dsp + gl * dsn
            dlast = jnp.sum(dsn * sp, keepdims=True) * gl
            dsnb = dsn.astype(BF16)
            dbd = _nt(xdtb, dsnb)
            db_g[g] += dbd * dec
            tdec = jnp.sum(dbd * bg, axis=1, keepdims=True) * dec
            dxdt = _nn((bg * dec).astype(BF16), dsnb)
            dlast += jnp.sum(tdec, keepdims=True)
            dac -= tdec
            dm = _nt(dyb, xdtb)
            dxdt += _tn(m.astype(BF16), dyb)
            dcb = (dm * seg).astype(BF16)
            dc_g[g] += _nn(dcb, bg.astype(BF16))
            db_g[g] += _tn(dcb, cg.astype(BF16))
            w = dm * m
            dac += jnp.sum(w, axis=1, keepdims=True) - jnp.sum(w.T, axis=1, keepdims=True)
            dac += jnp.where(rowi == BLK - 1, dlast, 0.0)
            dxs_l.append(dxs + dxdt * dth)
            ddt_all += _lane_put(jnp.sum(dxdt * xs, axis=1, keepdims=True), h)
            dac_all += _lane_put(dac, h)
        da = _nn_hi(_tri(lower=False).astype(F32), dac_all)
        ddt = ddt_all + da * amat
        dal_ref[...] += jnp.sum(da * dt, axis=0, keepdims=True) * amat
        ddtr = jnp.where(ok, ddt * _sig(sm + dtb_ref[...]), 0.0)
        ddtb_ref[...] += jnp.sum(ddtr, axis=0, keepdims=True)
        dds_ref[...] += dds_row
        dsm_ref[...] = ddtr
        dxa_ref[...] = jnp.where(ok, jnp.concatenate(dxs_l + db_g + dc_g, axis=1), 0.0)

    vec = pl.BlockSpec((1, BLK), lambda c: (0, 0))
    nvec = pl.BlockSpec((1, SSD_D), lambda c: (0, 0))
    return pl.pallas_call(
        body, name="ssd_bwd", grid=(nb,),
        in_specs=[pl.BlockSpec((BLK, SSD_D), lambda c: (rev(c), 0)),
                  pl.BlockSpec((BLK, CONV_D), lambda c: (rev(c), 0)),
                  pl.BlockSpec((BLK, SSD_D), lambda c: (rev(c), C_Z // SSD_D)),
                  pl.BlockSpec((BLK, BLK), lambda c: (rev(c), C_SM // BLK)),
                  pl.BlockSpec((None, SSD_N, SSD_D), lambda c: (rev(c), 0, 0)),
                  vec, vec, vec, nvec],
        out_specs=[pl.BlockSpec((BLK, CONV_D), lambda c: (rev(c), 0)),
                   pl.BlockSpec((BLK, SSD_D), lambda c: (rev(c), 0)),
                   pl.BlockSpec((BLK, BLK), lambda c: (rev(c), 0)),
                   nvec, vec, vec, vec],
        out_shape=[jax.ShapeDtypeStruct((t, CONV_D), F32), jax.ShapeDtypeStruct((t, SSD_D), F32),
                   jax.ShapeDtypeStruct((t, BLK), F32), jax.ShapeDtypeStruct((1, SSD_D), F32),
                   jax.ShapeDtypeStruct((1, BLK), F32), jax.ShapeDtypeStruct((1, BLK), F32),
                   jax.ShapeDtypeStruct((1, BLK), F32)],
        scratch_shapes=[pltpu.VMEM((SSD_N, SSD_D), F32)],
        compiler_params=_cp("arbitrary"),
    )(dmix, xa, proj, proj, sprev, dtb, alog, dskip, normg)


def _attn_scores(q_ref, k_ref, h, dq, scale, mask, bias):
    qh = q_ref[:, dq * h:dq * (h + 1)].astype(BF16)
    kh = k_ref[:, dq * h:dq * (h + 1)].astype(BF16)
    s = _nt(qh, kh) * scale
    if bias is not None:
        s = s + bias
    return qh, kh, jnp.where(mask, s, NEG)


def _segments(nb):
    cuts = sorted({0, nb} | {max(1, round(nb * f)) for f in (0.3, 0.53, 0.77)})
    return list(zip(cuts[:-1], cuts[1:]))


def attn_fwd(q, k, v, qcol, kcol, vcol, nh, dq, dv, scale, c_col=None, c_row=None, lane0=0):
    t = q.shape[0]
    tq = BLK
    use_bias = c_col is not None

    def segment(t0, t1, prev):
        tk = t1 * BLK
        nprev = len(prev)

        def body(*refs):
            refs = refs[nprev:]
            if use_bias:
                q_ref, k_ref, v_ref, cc_ref, cr_ref, o_ref, l_ref = refs
            else:
                q_ref, k_ref, v_ref, o_ref, l_ref = refs
            i = pl.program_id(0)
            rowg = (t0 + i) * tq + lax.broadcasted_iota(jnp.int32, (tq, 1), 0)
            col = lax.broadcasted_iota(jnp.int32, (1, tk), 1)
            mask = (col <= rowg) & (col >= PAD)
            outs = []
            lse = jnp.zeros((tq, BLK), F32)
            for h in range(nh):
                bias = (cc_ref[:, lane0 + h:lane0 + h + 1] - cr_ref[h:h + 1, :]) if use_bias else None
                _, _, s = _attn_scores(q_ref, k_ref, h, dq, scale, mask, bias)
                m = jnp.max(s, axis=1, keepdims=True)
                p = jnp.exp(s - m)
                l = jnp.sum(p, axis=1, keepdims=True)
                vh = v_ref[:, dv * h:dv * (h + 1)].astype(BF16)
                outs.append(_nn(p.astype(BF16), vh) / l)
                lse += _lane_put(m + jnp.log(l), h)
            o_ref[...] = jnp.concatenate(outs, axis=1).astype(BF16)
            l_ref[...] = lse.T[0:8, :]

        in_specs = [_ANY] * nprev + [pl.BlockSpec((tq, nh * dq), lambda i: (t0 + i, qcol)),
                                     pl.BlockSpec((tk, nh * dq), lambda i: (0, kcol)),
                                     pl.BlockSpec((tk, nh * dv), lambda i: (0, vcol))]
        args = list(prev) + [q, k, v]
        if use_bias:
            in_specs += [pl.BlockSpec((tq, BLK), lambda i: (t0 + i, 0)), pl.BlockSpec((8, tk), lambda i: (0, 0))]
            args += [c_col, c_row]
        return pl.pallas_call(
            body, name="attn_fwd", grid=(t1 - t0,),
            in_specs=in_specs,
            out_specs=[pl.BlockSpec((tq, nh * dv), lambda i: (t0 + i, 0)), pl.BlockSpec((8, tq), lambda i: (0, t0 + i))],
            out_shape=[jax.ShapeDtypeStruct((t, nh * dv), BF16), jax.ShapeDtypeStruct((8, t), F32)],
            input_output_aliases={p: p for p in range(nprev)},
            compiler_params=_cp("arbitrary"),
        )(*args)

    outs = []
    for t0, t1 in _segments(t // tq):
        outs = segment(t0, t1, outs)
    return outs


def attn_bwd(q, k, v, do, lse_row, o, qcol, kcol, vcol, docol, ocol, nh, dq, dv, scale, c_col=None, c_row=None, lane0=0):
    t = q.shape[0]
    tq = BLK
    use_bias = c_col is not None

    def segment(t0, t1, prev):
        tk = t1 * BLK
        nprev = len(prev)

        def body(*refs):
            pv, refs = refs[:nprev], refs[nprev:]
            kt = refs[-1]
            if use_bias:
                q_ref, k_ref, v_ref, do_ref, l_ref, o_ref, cc_ref, cr_ref, dq_ref, dk_ref, dv_ref, dcq_ref, dck_ref = refs[:-1]
            else:
                q_ref, k_ref, v_ref, do_ref, l_ref, o_ref, dq_ref, dk_ref, dv_ref = refs[:-1]
            i = pl.program_id(0)

            @pl.when(i == 0)
            def _():
                kt[...] = k_ref[...].astype(BF16).T
                if nprev:
                    dk_ref[...] = pv[1][...]
                    dv_ref[...] = pv[2][...]
                    if use_bias:
                        dck_ref[...] = pv[4][...]
                else:
                    dk_ref[...] = jnp.zeros_like(dk_ref)
                    dv_ref[...] = jnp.zeros_like(dv_ref)
                    if use_bias:
                        dck_ref[...] = jnp.zeros_like(dck_ref)

            key = lax.broadcasted_iota(jnp.int32, (tk, 1), 0)
            qry = (t0 + i) * tq + lax.broadcasted_iota(jnp.int32, (1, tq), 1)
            mask = (key <= qry) & (key >= PAD)
            dot = (do_ref[...].astype(F32) * o_ref[...].astype(F32)).T
            lane = lax.broadcasted_iota(jnp.int32, (1, BLK), 1)
            dqts, dcqs = [], []
            for h in range(nh):
                qh = q_ref[:, dq * h:dq * (h + 1)].astype(BF16)
                kh = k_ref[:, dq * h:dq * (h + 1)].astype(BF16)
                vh = v_ref[:, dv * h:dv * (h + 1)].astype(BF16)
                doh = do_ref[:, dv * h:dv * (h + 1)].astype(BF16)
                delta = jnp.sum(dot[dv * h:dv * (h + 1), :], axis=0, keepdims=True)
                st = _nt(kh, qh) * scale
                if use_bias:
                    st = st + (cr_ref[h:h + 1, :] - cc_ref[h])
                pt = jnp.exp(jnp.where(mask, st, NEG) - l_ref[h:h + 1, :])
                dst = pt * (_nt(vh, doh) - delta)
                dsb = dst.astype(BF16)
                dk_ref[:, dq * h:dq * (h + 1)] += _nn(dsb, qh) * scale
                dv_ref[:, dv * h:dv * (h + 1)] += _nn(pt.astype(BF16), doh)
                dqts.append(_nn(kt[dq * h:dq * (h + 1), :], dsb))
                if use_bias:
                    dcqs.append(jnp.sum(dst, axis=0, keepdims=True))
                    dck_ref[h] += dst
            dq_ref[...] = jnp.concatenate(dqts, axis=0).T * scale
            if use_bias:
                dcq_ref[...] = jnp.concatenate(dcqs + [jnp.zeros((8 - nh, tq), F32)], axis=0)

        keys_q = pl.BlockSpec((tk, nh * dq), lambda i: (0, 0))
        keys_v = pl.BlockSpec((tk, nh * dv), lambda i: (0, 0))
        keys_c = pl.BlockSpec((nh, tk, BLK), lambda i: (0, 0, 0))
        qrow = pl.BlockSpec((8, tq), lambda i: (0, t0 + i))
        prev_specs = ([_ANY, keys_q, keys_v] + ([_ANY, keys_c] if use_bias else [])) if nprev else []
        in_specs = prev_specs + [pl.BlockSpec((tq, nh * dq), lambda i: (t0 + i, qcol)),
                                 pl.BlockSpec((tk, nh * dq), lambda i: (0, kcol)),
                                 pl.BlockSpec((tk, nh * dv), lambda i: (0, vcol)),
                                 pl.BlockSpec((tq, nh * dv), lambda i: (t0 + i, docol)),
                                 qrow,
                                 pl.BlockSpec((tq, nh * dv), lambda i: (t0 + i, ocol))]
        args = list(prev) + [q, k, v, do, lse_row, o]
        out_specs = [pl.BlockSpec((tq, nh * dq), lambda i: (t0 + i, 0)), keys_q, keys_v]
        out_shape = [jax.ShapeDtypeStruct((t, nh * dq), F32), jax.ShapeDtypeStruct((t, nh * dq), F32),
                     jax.ShapeDtypeStruct((t, nh * dv), F32)]
        if use_bias:
            in_specs += [keys_c, qrow]
            args += [c_col, c_row]
            out_specs += [qrow, keys_c]
            out_shape += [jax.ShapeDtypeStruct((8, t), F32), jax.ShapeDtypeStruct((nh, t, BLK), F32)]
        return pl.pallas_call(
            body, name="attn_bwd", grid=(t1 - t0,),
            in_specs=in_specs, out_specs=out_specs, out_shape=out_shape,
            scratch_shapes=[pltpu.VMEM((nh * dq, tk), BF16)],
            input_output_aliases={p: p for p in range(nprev)},
            compiler_params=_cp("arbitrary"),
        )(*args)

    outs = []
    for t0, t1 in reversed(_segments(t // tq)):
        outs = segment(t0, t1, outs)
    return outs


def fox_pre(proj, fb):
    t = proj.shape[0]
    nb = t // BLK

    def body(sm_ref, fb_ref, c_ref, cr_ref, cb_ref):
        x = sm_ref[...] + fb_ref[...]
        lane = lax.broadcasted_iota(jnp.int32, (1, BLK), 1)
        keep = _valid_rows(t, 0) & (lane >= SM_F) & (lane < SM_F + FOX_H)
        logf = jnp.where(keep, jnp.minimum(x, 0.0) - jnp.log(1.0 + jnp.exp(-jnp.abs(x))), 0.0)
        tri = _tri().astype(F32)
        carry = jnp.zeros((1, BLK), F32)
        for b in range(nb):
            cb = _nn_hi(tri, logf[b * BLK:(b + 1) * BLK, :]) + carry
            c_ref[b * BLK:(b + 1) * BLK, :] = cb
            carry = cb[BLK - 1:BLK, :]
        cr_ref[...] = c_ref[...].T[SM_F:SM_F + 8, :]
        for h in range(FOX_H):
            cb_ref[h] = jnp.broadcast_to(c_ref[:, SM_F + h:SM_F + h + 1], (t, BLK))

    return pl.pallas_call(
        body, name="fox_pre", grid=(1,),
        in_specs=[pl.BlockSpec((t, BLK), lambda i: (0, C_SM // BLK)), pl.BlockSpec((1, BLK), lambda i: (0, 0))],
        out_specs=[pl.BlockSpec((t, BLK), lambda i: (0, 0)), pl.BlockSpec((8, t), lambda i: (0, 0)),
                   pl.BlockSpec((FOX_H, t, BLK), lambda i: (0, 0, 0))],
        out_shape=[jax.ShapeDtypeStruct((t, BLK), F32), jax.ShapeDtypeStruct((8, t), F32),
                   jax.ShapeDtypeStruct((FOX_H, t, BLK), F32)],
        compiler_params=_cp("arbitrary"),
    )(proj, fb)


def fox_pre_bwd(dcq, dck, proj, fb, dsm_in):
    t = proj.shape[0]
    nb = t // BLK

    def body(dcq_ref, dck_ref, sm_ref, fb_ref, din_ref, dsm_ref, dfb_ref, scr):
        triu = _tri(lower=False).astype(F32)
        carry = jnp.zeros((1, BLK), F32)
        scr[...] = jnp.concatenate([jnp.zeros((SM_F, t), F32), dcq_ref[...], jnp.zeros((BLK - SM_F - 8, t), F32)], axis=0).T
        lane = lax.broadcasted_iota(jnp.int32, (1, BLK), 1)
        for b in range(nb - 1, -1, -1):
            blk = scr[b * BLK:(b + 1) * BLK, :]
            for h in range(FOX_H):
                blk -= jnp.where(lane == SM_F + h, jnp.sum(dck_ref[h, b * BLK:(b + 1) * BLK, :], axis=1, keepdims=True), 0.0)
            cb = _nn_hi(triu, blk) + carry
            scr[b * BLK:(b + 1) * BLK, :] = cb
            carry = cb[0:1, :]
        x = sm_ref[...] + fb_ref[...]
        lane = lax.broadcasted_iota(jnp.int32, (1, BLK), 1)
        keep = _valid_rows(t, 0) & (lane >= SM_F) & (lane < SM_F + FOX_H)
        df = jnp.where(keep, scr[...] * _sig(-x), 0.0)
        dfb_ref[...] = jnp.sum(df, axis=0, keepdims=True)
        dsm_ref[...] = din_ref[...] + df

    full = pl.BlockSpec((t, BLK), lambda i: (0, 0))
    return pl.pallas_call(
        body, name="fox_pre_bwd", grid=(1,),
        in_specs=[pl.BlockSpec((8, t), lambda i: (0, 0)), pl.BlockSpec((FOX_H, t, BLK), lambda i: (0, 0, 0)),
                  pl.BlockSpec((t, BLK), lambda i: (0, C_SM // BLK)), pl.BlockSpec((1, BLK), lambda i: (0, 0)), full],
        out_specs=[full, pl.BlockSpec((1, BLK), lambda i: (0, 0))],
        out_shape=[jax.ShapeDtypeStruct((t, BLK), F32), jax.ShapeDtypeStruct((1, BLK), F32)],
        scratch_shapes=[pltpu.VMEM((t, BLK), F32)],
        compiler_params=_cp("arbitrary"),
    )(dcq, dck, proj, fb, dsm_in)


def _swap_rope(x):
    lane = lax.broadcasted_iota(jnp.int32, (1, BLK), 1)
    return jnp.where((lane >= SM_KR) & (lane < SM_KR + 16), pltpu.roll(x, BLK - 16, 1),
                     jnp.where((lane >= SM_KR + 16) & (lane < SM_KR + 32), pltpu.roll(x, 16, 1), 0.0))


def _rms(x, g):
    r = lax.rsqrt(jnp.mean(x * x, axis=1, keepdims=True) + EPS)
    return r, x * r


def mla_pre(proj, qg, kvg, wq, wk, wv, cosq, sinq):
    t = proj.shape[0]
    tm = _row_tile(t)

    def body(cq_ref, ckv_ref, sm_ref, qg_ref, kvg_ref, wq_ref, wk_ref, wv_ref, cos_ref, sin_ref,
             q_ref, k_ref, v_ref, cqn_ref, ckvn_ref):
        cs, sn = cos_ref[...], sin_ref[...]
        _, xh = _rms(cq_ref[...], None)
        cqn = (xh * qg_ref[...]).astype(BF16)
        cqn_ref[...] = cqn
        qraw = _nn(cqn, wq_ref[...])
        qs = []
        for h in range(MLA_H):
            hb = qraw[:, BLK * h:BLK * (h + 1)]
            qs.append(hb * cs + _swap_rope(hb) * sn)
        q_ref[...] = jnp.concatenate(qs, axis=1).astype(BF16)
        _, kh = _rms(ckv_ref[...], None)
        ckvn = (kh * kvg_ref[...]).astype(BF16)
        ckvn_ref[...] = ckvn
        kraw = _nn(ckvn, wk_ref[...])
        v_ref[...] = _nn(ckvn, wv_ref[...]).astype(BF16)
        lane = lax.broadcasted_iota(jnp.int32, (1, BLK), 1)
        kr = sm_ref[...]
        krr = jnp.where((lane >= SM_KR) & (lane < SM_KR + MLA_ROPE), kr * cs + _swap_rope(kr) * sn, 0.0)
        k_ref[...] = jnp.concatenate([kraw[:, BLK * h:BLK * (h + 1)] + krr for h in range(MLA_H)], axis=1).astype(BF16)

    def rows(w, cb):
        return pl.BlockSpec((tm, w), lambda i: (i, cb))

    def whole(a):
        return pl.BlockSpec(a.shape, lambda i: (0, 0))

    return pl.pallas_call(
        body, name="mla_pre", grid=(t // tm,),
        in_specs=[rows(MLA_QL, C_CQ // MLA_QL), rows(MLA_KVL, C_CKV // MLA_KVL), rows(BLK, C_SM // BLK),
                  whole(qg), whole(kvg), whole(wq), whole(wk), whole(wv), rows(BLK, 0), rows(BLK, 0)],
        out_specs=[rows(512, 0), rows(512, 0), rows(256, 0), rows(MLA_QL, 0), rows(MLA_KVL, 0)],
        out_shape=[jax.ShapeDtypeStruct((t, 512), BF16), jax.ShapeDtypeStruct((t, 512), BF16),
                   jax.ShapeDtypeStruct((t, 256), BF16), jax.ShapeDtypeStruct((t, MLA_QL), BF16),
                   jax.ShapeDtypeStruct((t, MLA_KVL), BF16)],
        compiler_params=_cp("arbitrary"),
    )(proj, proj, proj, qg, kvg, wq, wk, wv, cosq, sinq)


def mla_pre_bwd(dq, dk, dv, proj, cqn, ckvn, qg, kvg, wq, wk, wv, cosq, sinq, dsm_in):
    t = proj.shape[0]
    tm = _row_tile(t)

    def body(dq_ref, dk_ref, dv_ref, cq_ref, ckv_ref, cqn_ref, ckvn_ref, qg_ref, kvg_ref, wq_ref, wk_ref, wv_ref,
             cos_ref, sin_ref, din_ref, dcq_ref, dckv_ref, dsm_ref, dwq_ref, dwk_ref, dwv_ref, dqg_ref, dkvg_ref):
        i = pl.program_id(0)

        @pl.when(i == 0)
        def _():
            for r in (dwq_ref, dwk_ref, dwv_ref, dqg_ref, dkvg_ref):
                r[...] = jnp.zeros_like(r)

        cs, sn = cos_ref[...], sin_ref[...]
        lane = lax.broadcasted_iota(jnp.int32, (1, BLK), 1)

        def unrope(dy):
            return dy * cs + _swap_rope(dy * sn)

        dqp = jnp.concatenate([unrope(dq_ref[:, BLK * h:BLK * (h + 1)]) for h in range(MLA_H)], axis=1).astype(BF16)
        dwq_ref[...] += _tn(cqn_ref[...], dqp)
        dcqn = _nt(dqp, wq_ref[...])
        r, xh = _rms(cq_ref[...], None)
        dqg_ref[...] += jnp.sum(dcqn * xh, axis=0, keepdims=True)
        dxh = dcqn * qg_ref[...]
        dcq_ref[...] = r * (dxh - xh * jnp.mean(dxh * xh, axis=1, keepdims=True))

        dkn, dkr = [], jnp.zeros((tm, BLK), F32)
        for h in range(MLA_H):
            blk = dk_ref[:, BLK * h:BLK * (h + 1)]
            dkn.append(jnp.where(lane < MLA_NOPE, blk, 0.0))
            dkr += jnp.where((lane >= SM_KR) & (lane < SM_KR + MLA_ROPE), blk, 0.0)
        dknb = jnp.concatenate(dkn, axis=1).astype(BF16)
        dvb = dv_ref[...].astype(BF16)
        ckvn = ckvn_ref[...]
        dwk_ref[...] += _tn(ckvn, dknb)
        dwv_ref[...] += _tn(ckvn, dvb)
        dckvn = _nt(dknb, wk_ref[...]) + _nt(dvb, wv_ref[...])
        r2, kh = _rms(ckv_ref[...], None)
        dkvg_ref[...] += jnp.sum(dckvn * kh, axis=0, keepdims=True)
        dkh = dckvn * kvg_ref[...]
        dckv_ref[...] = r2 * (dkh - kh * jnp.mean(dkh * kh, axis=1, keepdims=True))
        dsm_ref[...] = din_ref[...] + jnp.where((lane >= SM_KR) & (lane < SM_KR + MLA_ROPE), unrope(dkr), 0.0)

    def rows(w, cb):
        return pl.BlockSpec((tm, w), lambda i: (i, cb))

    def whole(a):
        return pl.BlockSpec(a.shape, lambda i: (0, 0))

    def wshape(a):
        return jax.ShapeDtypeStruct(a.shape, F32)

    return pl.pallas_call(
        body, name="mla_pre_bwd", grid=(t // tm,),
        in_specs=[rows(512, 0), rows(512, 0), rows(256, 0), rows(MLA_QL, C_CQ // MLA_QL), rows(MLA_KVL, C_CKV // MLA_KVL),
                  rows(MLA_QL, 0), rows(MLA_KVL, 0), whole(qg), whole(kvg), whole(wq), whole(wk), whole(wv),
                  rows(BLK, 0), rows(BLK, 0), rows(BLK, 0)],
        out_specs=[rows(MLA_QL, 0), rows(MLA_KVL, 0), rows(BLK, 0), whole(wq), whole(wk), whole(wv), whole(qg), whole(kvg)],
        out_shape=[jax.ShapeDtypeStruct((t, MLA_QL), F32), jax.ShapeDtypeStruct((t, MLA_KVL), F32),
                   jax.ShapeDtypeStruct((t, BLK), F32), wshape(wq), wshape(wk), wshape(wv), wshape(qg), wshape(kvg)],
        compiler_params=_cp("arbitrary"),
    )(dq, dk, dv, proj, proj, cqn, ckvn, qg, kvg, wq, wk, wv, cosq, sinq, dsm_in)


def _slot_sum(me, own, recv_ref):
    gg = own.astype(F32)
    for s in range(N_DEV):
        gg = gg + jnp.where(me == s, 0.0, recv_ref[s].astype(F32))
    return gg


def adamw(w, m, v, g=None, recv=None, own=None, me_arr=None):
    shape = w.shape
    c = shape[-1]
    from_recv = recv is not None
    if not from_recv:
        me_arr = jnp.zeros((1,), jnp.int32)
    nl = len(recv) if from_recv else 1
    rws = w.size // c // nl
    tr = rws
    for d in (1024, 512, 352, 256, 128, 64, 32, 16, 8):
        if rws % d == 0 and d * c * 4 <= (2 << 20):
            tr = d
            break
    nt = rws // tr
    w2, m2, v2 = (a.reshape(nl, rws, c) for a in (w, m, v))
    if from_recv:
        gin = [a.reshape(N_DEV, rws, c) for a in list(recv) + list(own)]
    else:
        gin = [g.reshape(1, rws, c)]

    def body(me_ref, w_ref, m_ref, v_ref, *rest):
        g_refs, outs = rest[:len(gin)], rest[len(gin):]
        if from_recv:
            g_out, outs = outs[0], outs[1:]
            for li in range(nl):
                @pl.when(pl.program_id(0) == li)
                def _(li=li):
                    g_out[...] = _slot_sum(me_ref[0], g_refs[nl + li][...], g_refs[li])
            gg = g_out[...]
        else:
            gg = g_refs[0][...]
        d_ref, nm_ref, nv_ref = outs
        nm = B1 * m_ref[...] + (1.0 - B1) * gg
        nv = B2 * v_ref[...] + (1.0 - B2) * (gg * gg)
        mh = nm / (1.0 - B1 ** STEP)
        vh = nv / (1.0 - B2 ** STEP)
        d_ref[...] = -LR * (mh / (jnp.sqrt(vh) + AEPS) + WD * w_ref[...])
        nm_ref[...] = nm
        nv_ref[...] = nv

    row = pl.BlockSpec((None, tr, c), lambda l, i, me: (l, i, 0))
    if from_recv:
        gspecs = [pl.BlockSpec((N_DEV, tr, c), lambda l, i, me, li=li: (0, jnp.where(l == li, i, 0), 0))
                  for li in range(nl)]
        gspecs += [pl.BlockSpec((None, tr, c), lambda l, i, me, li=li: (me[0], jnp.where(l == li, i, 0), 0))
                   for li in range(nl)]
    else:
        gspecs = [row]
    nout = 4 if from_recv else 3
    outs = pl.pallas_call(
        body, name="adamw",
        grid_spec=pltpu.PrefetchScalarGridSpec(num_scalar_prefetch=1, grid=(nl, nt), in_specs=[row, row, row] + gspecs,
                                               out_specs=[row] * nout),
        out_shape=[jax.ShapeDtypeStruct((nl, rws, c), F32)] * nout,
        compiler_params=_cp("arbitrary", "arbitrary"),
    )(me_arr, w2, m2, v2, *gin)
    return tuple(o.reshape(shape) for o in outs)


def sum_slots(recv, own=None, me_arr=None):
    _, r, c = recv.shape
    if own is None:
        own, me_arr = recv, jnp.zeros((1,), jnp.int32)
        plain = True
    else:
        plain = False

    def body(me_ref, r_ref, own_ref, o_ref):
        if plain:
            gg = r_ref[0].astype(F32)
            for s in range(1, N_DEV):
                gg = gg + r_ref[s].astype(F32)
            o_ref[...] = gg
        else:
            o_ref[...] = _slot_sum(me_ref[0], own_ref[...], r_ref)

    return pl.pallas_call(
        body, name="sum_slots",
        grid_spec=pltpu.PrefetchScalarGridSpec(
            num_scalar_prefetch=1, grid=(1,),
            in_specs=[pl.BlockSpec((N_DEV, r, c), lambda i, me: (0, 0, 0)),
                      pl.BlockSpec((None, r, c), lambda i, me: (me[0], 0, 0))],
            out_specs=pl.BlockSpec((r, c), lambda i, me: (0, 0))),
        out_shape=jax.ShapeDtypeStruct((r, c), F32),
        compiler_params=_cp("arbitrary"),
    )(me_arr, recv, own)


_FLIPS = [(0, 0, 1), (0, 1, 0), (0, 1, 1), (1, 0, 0), (1, 0, 1), (1, 1, 0), (1, 1, 1)]
_ANY = pl.BlockSpec(memory_space=pl.ANY)


def _mesh_place():
    x, y, c = lax.axis_index("x"), lax.axis_index("y"), lax.axis_index("c")
    me = 4 * x + 2 * y + c
    peers = [((x + fx) % 2, (y + fy) % 2, (c + fc) % 2) for fx, fy, fc in _FLIPS]
    return me, peers


def place_own(src, l, dtype, me_arr):
    _, r, c = src.shape
    tr = r
    for d in (512, 352, 256, 128, 64, 32, 16, 8):
        if r % d == 0 and d * c * 4 <= (2 << 20):
            tr = d
            break

    def body(me_ref, s_ref, o_ref):
        o_ref[...] = s_ref[...].astype(dtype)

    return pl.pallas_call(
        body, name="place_own",
        grid_spec=pltpu.PrefetchScalarGridSpec(
            num_scalar_prefetch=1, grid=(r // tr,),
            in_specs=[pl.BlockSpec((None, tr, c), lambda i, me: (l, i, 0))],
            out_specs=pl.BlockSpec((None, tr, c), lambda i, me: (me[0], i, 0))),
        out_shape=jax.ShapeDtypeStruct((N_DEV, r, c), dtype),
        compiler_params=_cp("arbitrary"),
    )(me_arr, src)


_HBM = pl.BlockSpec(memory_space=pltpu.HBM)
_SEMS = pl.BlockSpec(memory_space=pltpu.SEMAPHORE)
_EFFECT = pltpu.SideEffectType.DATAFLOW_SIDE_EFFECTING


def exchange_start(mode, arrays, name):
    n = len(arrays)
    gather = mode == "gather"
    ns = 0 if gather else n
    zones = list(arrays) if gather else [lax.empty(a.shape, a.dtype) for a in arrays]
    ops = ([] if gather else list(arrays)) + zones

    def body(*refs):
        srcs, lands = refs[:ns], refs[ns:ns + n]
        send_sems, recv_sems = refs[ns + n], refs[ns + n + 1]
        token = refs[-1]
        me, peers = _mesh_place()
        ids = [4 * p[0] + 2 * p[1] + p[2] for p in peers]
        for j in range(n):
            for k in range(N_DEV - 1):
                src = lands[j].at[me] if gather else srcs[j].at[ids[k]]
                pltpu.make_async_remote_copy(src_ref=src, dst_ref=lands[j].at[me],
                                             send_sem=send_sems.at[j * (N_DEV - 1) + k],
                                             recv_sem=recv_sems.at[j * (N_DEV - 1) + k], device_id=peers[k],
                                             device_id_type=pl.DeviceIdType.MESH).start()
        token[...] = jnp.zeros_like(token)

    nsem = n * (N_DEV - 1)
    res = pl.pallas_call(
        body, name=name,
        in_specs=[_HBM] * (ns + n),
        out_specs=(_SEMS, _SEMS, *[_HBM] * (ns + n), pl.BlockSpec(memory_space=pltpu.VMEM)),
        out_shape=(pltpu.SemaphoreType.DMA((nsem,)), pltpu.SemaphoreType.DMA((nsem,)),
                   *[pltpu.HBM(a.shape, a.dtype) for a in ops], jax.ShapeDtypeStruct((8, BLK), F32)),
        input_output_aliases={i: 2 + i for i in range(ns + n)},
        compiler_params=pltpu.CompilerParams(has_side_effects=_EFFECT),
    )(*[pltpu.with_memory_space_constraint(a, pltpu.HBM) for a in ops])
    return dict(gather=gather, send=res[0], recv=res[1], srcs=list(res[2:2 + ns]), lands=list(res[2 + ns:2 + ns + n]),
                token=res[-1])


def exchange_wait(hd, idxs, name, after):
    gather = hd["gather"]
    n = len(idxs)
    ns = 0 if gather else n
    ops = ([] if gather else [hd["srcs"][j] for j in idxs]) + [hd["lands"][j] for j in idxs]

    def body(*refs):
        srcs, lands = refs[:ns], refs[ns:ns + n]
        send_sems, recv_sems = refs[ns + n], refs[ns + n + 1]
        me, peers = _mesh_place()
        ids = [4 * p[0] + 2 * p[1] + p[2] for p in peers]
        for p, j in enumerate(idxs):
            for k in range(N_DEV - 1):
                src = lands[p].at[me] if gather else srcs[p].at[ids[k]]
                cp = pltpu.make_async_remote_copy(src_ref=src, dst_ref=lands[p].at[ids[k]],
                                                  send_sem=send_sems.at[j * (N_DEV - 1) + k],
                                                  recv_sem=recv_sems.at[j * (N_DEV - 1) + k], device_id=peers[k],
                                                  device_id_type=pl.DeviceIdType.MESH)
                cp.wait_send()
                cp.wait_recv()

    res = pl.pallas_call(
        body, name=name,
        in_specs=[_HBM] * (ns + n) + [_SEMS, _SEMS, _ANY],
        out_specs=[_HBM] * (ns + n),
        out_shape=[pltpu.HBM(a.shape, a.dtype) for a in ops],
        input_output_aliases={i: i for i in range(ns + n)},
        compiler_params=pltpu.CompilerParams(has_side_effects=_EFFECT),
    )(*ops, hd["send"], hd["recv"], after)
    return list(res[:ns]), list(res[ns:])


def _pad_cols(a, n):
    return jnp.pad(a, ((0, 0),) * (a.ndim - 1) + ((0, n - a.shape[-1]),))


def w_in_to_padded(w):
    z = lambda n: jnp.zeros(w.shape[:-1] + (n,), w.dtype)
    return jnp.concatenate([
        w[..., 0:1280], w[..., 1288:2056], w[..., 2060:2316], w[..., 2316:2444],
        w[..., 1280:1288], w[..., 2056:2060], z(SM_KR - SM_F - FOX_H), w[..., 2444:2476], z(BLK - SM_KR - MLA_ROPE)], axis=-1)


def w_in_from_padded(g):
    s = C_SM
    return jnp.concatenate([
        g[..., 0:1280], g[..., s + SM_DT:s + SM_DT + 8], g[..., 1280:2048], g[..., s + SM_F:s + SM_F + 4],
        g[..., 2048:2304], g[..., 2304:2432], g[..., s + SM_KR:s + SM_KR + MLA_ROPE]], axis=-1)


def _unshard_cols(gth):
    n, r, c = gth.shape
    return jnp.transpose(gth, (1, 0, 2)).reshape(r, n * c)


def _shard_cols(full):
    r, nc = full.shape
    return jnp.transpose(full.reshape(r, N_DEV, nc // N_DEV), (1, 0, 2))


def mla_weights(uq_g, ukv_g):
    uq = _unshard_cols(uq_g)
    dqh = MLA_NOPE + MLA_ROPE
    wq = jnp.concatenate([_pad_cols(uq[:, dqh * h:dqh * (h + 1)], BLK) for h in range(MLA_H)], axis=1)
    wk = jnp.concatenate([_pad_cols(ukv_g[2 * h], BLK) for h in range(MLA_H)], axis=1)
    wv = jnp.concatenate([ukv_g[2 * h + 1] for h in range(MLA_H)], axis=1)
    return wq, wk, wv


def mla_weight_grads(dwq, dwk, dwv):
    dqh = MLA_NOPE + MLA_ROPE
    duq = _shard_cols(jnp.concatenate([dwq[:, BLK * h:BLK * h + dqh] for h in range(MLA_H)], axis=1))
    parts = []
    for h in range(MLA_H):
        parts += [dwk[:, BLK * h:BLK * h + MLA_NOPE], dwv[:, MLA_V * h:MLA_V * (h + 1)]]
    return duq, jnp.stack(parts, axis=0)


def rope_tables(t):
    pos = (jnp.arange(t, dtype=jnp.int32) - PAD).astype(F32)
    inv_freq = 1.0 / (10000.0 ** (jnp.arange(0, MLA_ROPE, 2, dtype=F32) / MLA_ROPE))
    ang = pos[:, None] * inv_freq[None, :]
    cos, sin = jnp.cos(ang), jnp.sin(ang)
    one, zero = jnp.ones((t, SM_KR), F32), jnp.zeros((t, SM_KR), F32)
    tail = BLK - SM_KR - MLA_ROPE
    cosq = jnp.concatenate([one, cos, cos, jnp.ones((t, tail), F32)], axis=1)
    sinq = jnp.concatenate([zero, -sin, sin, jnp.zeros((t, tail), F32)], axis=1)
    return cosq, sinq


def _lanes(v, off=0):
    return jnp.pad(v.astype(F32), (off, BLK - off - v.shape[0]))[None, :]


def layer_fwd(x, ln, hb, getw, tabs):
    sv = {"h0b": hb}
    W = dict(getw("ffn1", hb))
    ln1 = (W["ln1_g"], W["ln1_b"])
    u, v, r1, h1b = ffn_fwd_seq(x, ln, W["g1"], W["u1"], W["d1"], ln1)
    sv.update(u1=u, v1=v, r1=r1, h1b=h1b)
    W.update(getw("mix", h1b))
    ln2 = (W["ln2_g"], W["ln2_b"])
    proj = mm_nn(h1b, W["w_in"])
    xa = conv_fwd(proj, W["conv_w"], W["conv_b"])
    y_ssd, sprev = ssd_fwd(xa, proj, W["dtb"], W["alog"], W["dskip"], W["normg"])
    c_col, c_row, c_keys = fox_pre(proj, W["fb"])
    y_fox, lse_f = attn_fwd(proj, proj, proj, C_FQ // 256, C_FK // 256, C_FV // 256, FOX_H, FOX_DH, FOX_DH,
                            FOX_DH ** -0.5, c_col, c_row, SM_F)
    q, k, vv, cqn, ckvn = mla_pre(proj, W["qg"], W["kvg"], W["wq"], W["wk"], W["wv"], *tabs)
    y_mla, lse_m = attn_fwd(q, k, vv, 0, 0, 0, MLA_H, BLK, MLA_V, (MLA_NOPE + MLA_ROPE) ** -0.5)
    mixcat = jnp.concatenate([y_ssd, y_fox, y_mla], axis=1)
    r2, h2b = mm_res_ln(mixcat, W["w_out"], r1, ln1, ln2)
    sv.update(proj=proj, xa=xa, sprev=sprev, c_keys=c_keys, c_row=c_row, lse_f=lse_f, q=q, k=k, v=vv, cqn=cqn, ckvn=ckvn,
              lse_m=lse_m, mixcat=mixcat, r2=r2, h2b=h2b)
    W.update(getw("ffn2", h2b))
    ln3 = (W["ln3_g"], W["ln3_b"])
    u, v, r3, h3b = ffn_fwd_seq(r2, ln2, W["g2"], W["u2"], W["d2"], ln3)
    sv.update(u2=u, v2=v, r3=r3, W=W)
    return r3, ln3, h3b, sv


def ffn_bwd(parts, r, gamma, hb_in, u, v, wg, wu, wd, after=None):
    dh, dwg, dwu, dwd, dg, db = ffn_bwd_seq(parts, r, gamma, hb_in, u, v, wg, wu, wd, after)
    return dh, dict(d=dwd, g=dwg, u=dwu, ln_g=dg, ln_b=db)


def layer_bwd(parts, sv, emit, tabs, after):
    G = {}
    W = sv["W"]
    dh2, g2 = ffn_bwd(parts, sv["r3"], W["ln3_g"], sv["h2b"], sv["u2"], sv["v2"], W["g2"], W["u2"], W["d2"], after)
    G.update(g2=g2["g"], u2=g2["u"], d2=g2["d"], ln3_g=g2["ln_g"], ln3_b=g2["ln_b"])
    tok = emit("ffn2", G)
    dr2, dmixb, G["ln2_g"], G["ln2_b"] = ln_bwd([(dh2, 1.0)], sv["r2"], W["ln2_g"], 1.0, tok)
    dmc = mm_nt_reduce([(dmixb[None], W["w_out"][None])], D)
    G["w_out"] = mm_tn(sv["mixcat"][None], dmixb[None])[0]
    proj = sv["proj"]
    dxa, dz, dsm, G["normg"], G["dskip"], G["alog"], G["dtb"] = ssd_bwd(
        dmc, sv["xa"], proj, sv["sprev"], W["dtb"], W["alog"], W["dskip"], W["normg"])
    dxbc, G["conv_w"], G["conv_b"] = conv_bwd(dxa, proj, W["conv_w"], W["conv_b"])
    dfq, dfk, dfv, dcq, dck = attn_bwd(proj, proj, proj, dmc, sv["lse_f"], sv["mixcat"], C_FQ // 256, C_FK // 256,
                                       C_FV // 256, 2, 2, FOX_H, FOX_DH, FOX_DH, FOX_DH ** -0.5, sv["c_keys"], sv["c_row"])
    dsm, G["fb"] = fox_pre_bwd(dcq, dck, proj, W["fb"], dsm)
    dq, dk, dv = attn_bwd(sv["q"], sv["k"], sv["v"], dmc, sv["lse_m"], sv["mixcat"], 0, 0, 0, 3, 3, MLA_H, BLK, MLA_V,
                          (MLA_NOPE + MLA_ROPE) ** -0.5)
    dcql, dckv, dsm, G["wq"], G["wk"], G["wv"], G["qg"], G["kvg"] = mla_pre_bwd(
        dq, dk, dv, proj, sv["cqn"], sv["ckvn"], W["qg"], W["kvg"], W["wq"], W["wk"], W["wv"], *tabs, dsm)
    dproj = jnp.concatenate([dz, dxbc, dfq, dfk, dfv, dcql, dckv, dsm], axis=1).astype(BF16)
    dh1p = mm_nt_reduce([(dproj[None], W["w_in"][None])], D)
    G["w_in"] = mm_tn(sv["h1b"][None], dproj[None])[0]
    tok = emit("mix", G)
    dh0, g1 = ffn_bwd([(dr2, ALPHA), (dh1p, 1.0)], sv["r1"], W["ln1_g"], sv["h0b"], sv["u1"], sv["v1"],
                      W["g1"], W["u1"], W["d1"], tok)
    G.update(g1=g1["g"], u1=g1["u"], d1=g1["d"], ln1_g=g1["ln_g"], ln1_b=g1["ln_b"])
    tok = emit("ffn1", G)
    return [(dh0, 1.0)], G, tok


def local_step(x, target, meta_full, getw, emit):
    t = x.shape[0] + BLK
    tabs = rope_tables(t)
    xr, hb = build_h0(meta_full, x)
    ln = None
    saved = []
    for l in range(NL):
        xr, ln, hb, sv = layer_fwd(xr, ln, hb, functools.partial(getw, l), tabs)
        saved.append(sv)
    dy, loss = loss_head(xr, ln, target)
    parts = [(dy, 1.0)]
    grads = [None] * NL
    tok = None
    for l in range(NL - 1, -1, -1):
        parts, grads[l], tok = layer_bwd(parts, saved[l], functools.partial(emit, l), tabs, tok)
    gx, gmeta = split_dh0(parts[0][0], tok)
    return loss, gx, gmeta, grads


_SMALL = ["ln1_g", "ln1_b", "ln2_g", "ln2_b", "ln3_g", "ln3_b", "conv_b", "ssd_norm_g", "mla_q_norm_g",
          "mla_kv_norm_g", "dt_bias", "a_log", "d_skip", "fox_f_b"]
_SMALL_ROWS = 16
_BIG = ["ffn1_w_gate", "ffn1_w_up", "ffn1_w_down", "w_in", "conv_w", "mla_w_uq", "mla_w_ukv", "w_out",
        "ffn2_w_gate", "ffn2_w_up", "ffn2_w_down"]
_NAMES = ["meta", "ffn1_w_gate", "ffn1_w_up", "ffn1_w_down", "ln1_g", "ln1_b", "w_in", "conv_w", "conv_b", "dt_bias",
          "a_log", "d_skip", "ssd_norm_g", "fox_f_b", "mla_q_norm_g", "mla_w_uq", "mla_kv_norm_g", "mla_w_ukv", "w_out",
          "ln2_g", "ln2_b", "ffn2_w_gate", "ffn2_w_up", "ffn2_w_down", "ln3_g", "ln3_b"]


def pack_small(p):
    rows = []
    for l in range(NL):
        for n in _SMALL:
            rows.append(_pad_cols(p[n][l][None, :].astype(F32), D))
        rows.append(jnp.zeros((_SMALL_ROWS - len(_SMALL), D), F32))
    return jnp.concatenate(rows, axis=0)


def unpack_small(a, like):
    out = {}
    for i, n in enumerate(_SMALL):
        out[n] = jnp.stack([a[l * _SMALL_ROWS + i, :like[n].shape[1]] for l in range(NL)], axis=0)
    return out


_STAGES = {"ffn1": ["ffn1_w_gate", "ffn1_w_up", "ffn1_w_down"],
           "mix": ["w_in", "conv_w", "mla_w_uq", "mla_w_ukv", "w_out"],
           "ffn2": ["ffn2_w_gate", "ffn2_w_up", "ffn2_w_down"]}


_FFN_T = ("ffn1_w_gate", "ffn1_w_up", "ffn2_w_gate", "ffn2_w_up")


def stage_weights(l, stage, g, rep):
    if stage != "mix":
        i = stage[3]
        return {"g" + i: g[f"ffn{i}_w_gate"].reshape(D_FF, D), "u" + i: g[f"ffn{i}_w_up"].reshape(D_FF, D),
                "d" + i: g[f"ffn{i}_w_down"].reshape(D_FF, D),
                "ln1_g" if i == "1" else "ln3_g": rep["ln1_g" if i == "1" else "ln3_g"][l][None, :],
                "ln1_b" if i == "1" else "ln3_b": rep["ln1_b" if i == "1" else "ln3_b"][l][None, :]}
    W = {}
    W["w_in"] = g["w_in"].reshape(D, N_INP)
    W["w_out"] = g["w_out"].reshape(D, D)
    W["wq"], W["wk"], W["wv"] = mla_weights(g["mla_w_uq"], g["mla_w_ukv"])
    W["conv_w"] = _unshard_cols(g["conv_w"])
    for k in ("ln2_g", "ln2_b", "conv_b"):
        W[k] = rep[k][l][None, :]
    W["normg"] = rep["ssd_norm_g"][l][None, :]
    W["qg"] = rep["mla_q_norm_g"][l][None, :]
    W["kvg"] = rep["mla_kv_norm_g"][l][None, :]
    W["dtb"] = _lanes(rep["dt_bias"][l], SM_DT)
    W["alog"] = _lanes(rep["a_log"][l], SM_DT)
    W["dskip"] = _lanes(rep["d_skip"][l], SM_DT)
    W["fb"] = _lanes(rep["fox_f_b"][l], SM_F)
    return W


def small_grads(G):
    return {"ln1_g": G["ln1_g"][0], "ln1_b": G["ln1_b"][0], "ln2_g": G["ln2_g"][0], "ln2_b": G["ln2_b"][0],
            "ln3_g": G["ln3_g"][0], "ln3_b": G["ln3_b"][0], "conv_b": G["conv_b"][0], "ssd_norm_g": G["normg"][0],
            "mla_q_norm_g": G["qg"][0], "mla_kv_norm_g": G["kvg"][0], "dt_bias": G["dtb"][0, :SSD_H],
            "a_log": G["alog"][0, :SSD_H], "d_skip": G["dskip"][0, :SSD_H], "fox_f_b": G["fb"][0, SM_F:SM_F + FOX_H]}


def big_grads(G, stage):
    if stage != "mix":
        i = stage[-1]
        return {f"ffn{i}_w_{k}": G[k[0] + i].reshape(N_DEV, HS, D) for k in ("gate", "up", "down")}
    duq, dukv = mla_weight_grads(G["wq"], G["wk"], G["wv"])
    return {"w_in": G["w_in"].reshape(N_DEV, D // N_DEV, N_INP), "w_out": G["w_out"].reshape(N_DEV, D // N_DEV, D),
            "mla_w_uq": duq, "mla_w_ukv": dukv, "conv_w": _shard_cols(G["conv_w"])}


def kernel(x, meta, ffn1_w_gate, ffn1_w_up, ffn1_w_down, ln1_g, ln1_b, w_in, conv_w, conv_b, dt_bias, a_log, d_skip, ssd_norm_g, fox_f_b, mla_q_norm_g, mla_w_uq, mla_kv_norm_g, mla_w_ukv, w_out, ln2_g, ln2_b, ffn2_w_gate, ffn2_w_up, ffn2_w_down, ln3_g, ln3_b, loss_target, m_meta, m_ffn1_w_gate, m_ffn1_w_up, m_ffn1_w_down, m_ln1_g, m_ln1_b, m_w_in, m_conv_w, m_conv_b, m_dt_bias, m_a_log, m_d_skip, m_ssd_norm_g, m_fox_f_b, m_mla_q_norm_g, m_mla_w_uq, m_mla_kv_norm_g, m_mla_w_ukv, m_w_out, m_ln2_g, m_ln2_b, m_ffn2_w_gate, m_ffn2_w_up, m_ffn2_w_down, m_ln3_g, m_ln3_b, v_meta, v_ffn1_w_gate, v_ffn1_w_up, v_ffn1_w_down, v_ln1_g, v_ln1_b, v_w_in, v_conv_w, v_conv_b, v_dt_bias, v_a_log, v_d_skip, v_ssd_norm_g, v_fox_f_b, v_mla_q_norm_g, v_mla_w_uq, v_mla_kv_norm_g, v_mla_w_ukv, v_w_out, v_ln2_g, v_ln2_b, v_ffn2_w_gate, v_ffn2_w_up, v_ffn2_w_down, v_ln3_g, v_ln3_b):
    vals = (meta, ffn1_w_gate, ffn1_w_up, ffn1_w_down, ln1_g, ln1_b, w_in, conv_w, conv_b, dt_bias, a_log, d_skip, ssd_norm_g, fox_f_b, mla_q_norm_g, mla_w_uq, mla_kv_norm_g, mla_w_ukv, w_out, ln2_g, ln2_b, ffn2_w_gate, ffn2_w_up, ffn2_w_down, ln3_g, ln3_b)
    moms = (m_meta, m_ffn1_w_gate, m_ffn1_w_up, m_ffn1_w_down, m_ln1_g, m_ln1_b, m_w_in, m_conv_w, m_conv_b, m_dt_bias, m_a_log, m_d_skip, m_ssd_norm_g, m_fox_f_b, m_mla_q_norm_g, m_mla_w_uq, m_mla_kv_norm_g, m_mla_w_ukv, m_w_out, m_ln2_g, m_ln2_b, m_ffn2_w_gate, m_ffn2_w_up, m_ffn2_w_down, m_ln3_g, m_ln3_b)
    vars_ = (v_meta, v_ffn1_w_gate, v_ffn1_w_up, v_ffn1_w_down, v_ln1_g, v_ln1_b, v_w_in, v_conv_w, v_conv_b, v_dt_bias, v_a_log, v_d_skip, v_ssd_norm_g, v_fox_f_b, v_mla_q_norm_g, v_mla_w_uq, v_mla_kv_norm_g, v_mla_w_ukv, v_w_out, v_ln2_g, v_ln2_b, v_ffn2_w_gate, v_ffn2_w_up, v_ffn2_w_down, v_ln3_g, v_ln3_b)
    P = dict(zip(_NAMES, vals))
    M = dict(zip(_NAMES, moms))
    V = dict(zip(_NAMES, vars_))
    me = 4 * lax.axis_index("x") + 2 * lax.axis_index("y") + lax.axis_index("c")

    me_arr = me.astype(jnp.int32).reshape(1)
    for n in _FFN_T:
        P[n], M[n], V[n] = (jnp.swapaxes(a[n], 1, 2) for a in (P, M, V))
    src = dict(P)
    src["w_in"] = w_in_to_padded(P["w_in"])
    order = [("meta", 0)] + [(n, l) for l in range(NL) for names in _STAGES.values() for n in names]
    zone_of = {nl_: i for i, nl_ in enumerate(order)}
    zones = [place_own(P["meta"][None], 0, F32, me_arr)]
    zones += [place_own(src[n], l, F32 if n == "conv_w" else BF16, me_arr) for n, l in order[1:]]
    hg = exchange_start("gather", zones, "gather_start")
    meta_full = _unshard_cols(exchange_wait(hg, [0], "gather_wait_meta", hg["token"])[1][0])

    def getw(l, stage, after):
        names = _STAGES[stage]
        lands = exchange_wait(hg, [zone_of[(n, l)] for n in names], f"gather_wait_{l}_{stage}", after)[1]
        return stage_weights(l, stage, dict(zip(names, lands)), P)

    sent = {}

    def emit(l, stage, G):
        bg = big_grads(G, stage)
        sent[(l, stage)] = exchange_start("scatter", [bg[n] for n in _STAGES[stage]], f"scatter_start_{l}_{stage}")
        return sent[(l, stage)]["token"]

    loss, gx, gmeta, grads = local_step(x[0], loss_target[0], meta_full, getw, emit)

    small = jnp.concatenate([pack_small({n: jnp.stack([small_grads(g)[n] for g in grads]) for n in _SMALL}), gmeta], axis=0)
    hs = exchange_start("gather", [place_own(small[None], 0, F32, me_arr)], "small_start")

    out = {}
    after = hs["token"]
    for stage in ("ffn2", "mix", "ffn1"):
        names = _STAGES[stage]
        got = [exchange_wait(sent[(l, stage)], list(range(len(names))), f"scatter_wait_{l}_{stage}", after)
               for l in range(NL - 1, -1, -1)][::-1]
        for i, n in enumerate(names):
            own = [got[l][0][i] for l in range(NL)]
            recv = [got[l][1][i] for l in range(NL)]
            if n == "w_in":
                g = jnp.stack([w_in_from_padded(sum_slots(recv[l], own[l], me_arr)) for l in range(NL)])
                out[n] = (g,) + adamw(P[n], M[n], V[n], g=g)
            else:
                out[n] = adamw(P[n], M[n], V[n], recv=recv, own=own, me_arr=me_arr)
                if n in _FFN_T:
                    out[n] = tuple(jnp.swapaxes(a, 1, 2) for a in out[n])
        after = out[names[-1]][1]
    gsmall = sum_slots(exchange_wait(hs, [0], "small_wait", after)[1][0])
    gm = lax.dynamic_slice(gsmall[NL * _SMALL_ROWS:], (0, me * (D // N_DEV)), (N_META, D // N_DEV))
    out["meta"] = (gm,) + adamw(P["meta"], M["meta"], V["meta"], g=gm)
    gs = gsmall[:NL * _SMALL_ROWS]
    sd, sm_, sv_ = adamw(pack_small(P), pack_small(M), pack_small(V), g=gs)
    ups = [unpack_small(a, P) for a in (gs, sd, sm_, sv_)]
    for n in _SMALL:
        out[n] = tuple(u[n] for u in ups)

    loss_all = lax.psum(loss[0, 0], ("x", "y", "c"))
    flat = [loss_all, gx[None]]
    for k in range(4):
        flat += [out[n][k] for n in _NAMES]
    return tuple(flat)
```

```python
import functools

import jax
import jax.numpy as jnp
from jax import lax
from jax.experimental import pallas as pl
from jax.experimental.pallas import tpu as pltpu

F32, BF16 = jnp.float32, jnp.bfloat16
HI = lax.Precision.HIGHEST

N_DEV = 8
D = 1024
NL = 2
N_META = 16
BLK = 128
PAD = BLK - N_META
D_FF = 2816
HS = D_FF // N_DEV
SSD_H, SSD_P, SSD_N, SSD_G = 8, 64, 64, 2
SSD_D = SSD_H * SSD_P
CONV_K = 4
CONV_D = SSD_D + 2 * SSD_G * SSD_N
FOX_H, FOX_DH = 4, 64
MLA_H, MLA_QL, MLA_KVL, MLA_NOPE, MLA_ROPE, MLA_V = 4, 256, 128, 64, 32, 64
N_IN = 2476
C_Z, C_XBC, C_FQ, C_FK, C_FV, C_CQ, C_CKV, C_SM, N_INP = 0, 512, 1280, 1536, 1792, 2048, 2304, 2432, 2560
SM_DT, SM_F, SM_KR = 0, 8, 64
ALPHA = (2 * NL) ** 0.25
EPS = 1e-5
NEG = -1e30
LR, B1, B2, AEPS, WD, STEP = 0.001, 0.9, 0.999, 1e-08, 0.01, 10
VMEM_MB = 56


def _cp(*sem):
    return pltpu.CompilerParams(dimension_semantics=sem, vmem_limit_bytes=VMEM_MB << 20)


def _nn(a, b):
    return lax.dot_general(a, b, (((1,), (0,)), ((), ())), preferred_element_type=F32)


def _nt(a, b):
    return lax.dot_general(a, b, (((1,), (1,)), ((), ())), preferred_element_type=F32)


def _tn(a, b):
    return lax.dot_general(a, b, (((0,), (0,)), ((), ())), preferred_element_type=F32)


def _nn_hi(a, b):
    return lax.dot_general(a, b, (((1,), (0,)), ((), ())), precision=HI, preferred_element_type=F32)


def _row_tile(t):
    for d in range(640, 15, -16):
        if t % d == 0:
            return d
    raise ValueError(t)


def _sig(x):
    return 1.0 / (1.0 + jnp.exp(-x))


def _tri(lower=True):
    r = lax.broadcasted_iota(jnp.int32, (BLK, BLK), 0)
    c = lax.broadcasted_iota(jnp.int32, (BLK, BLK), 1)
    return (r >= c) if lower else (r <= c)


def build_h0(meta_full, x):
    s = x.shape[0]
    nb = s // BLK + 1

    def body(m_ref, x_ref, h_ref, hb_ref):
        i = pl.program_id(0)

        @pl.when(i == 0)
        def _():
            h = jnp.concatenate([jnp.zeros((PAD, D), F32), m_ref[...]], axis=0)
            h_ref[...] = h
            hb_ref[...] = h.astype(BF16)

        @pl.when(i > 0)
        def _():
            h_ref[...] = x_ref[...]
            hb_ref[...] = x_ref[...].astype(BF16)

    return pl.pallas_call(
        body, name="build_h0", grid=(nb,),
        in_specs=[pl.BlockSpec((N_META, D), lambda i: (0, 0)),
                  pl.BlockSpec((BLK, D), lambda i: (jnp.maximum(i - 1, 0), 0))],
        out_specs=[pl.BlockSpec((BLK, D), lambda i: (i, 0))] * 2,
        out_shape=[jax.ShapeDtypeStruct((nb * BLK, D), F32), jax.ShapeDtypeStruct((nb * BLK, D), BF16)],
        compiler_params=_cp("arbitrary"),
    )(meta_full, x)


FT = 256


def _layer_norm(r, gamma, beta):
    mu = jnp.mean(r, axis=1, keepdims=True)
    xc = r - mu
    var = jnp.mean(xc * xc, axis=1, keepdims=True)
    return xc * lax.rsqrt(var + EPS) * gamma + beta


def ffn_fwd(hb, res, wg, wu, wd, gamma, beta):
    t = hb.shape[0]
    f = wg.shape[0]
    tm = _row_tile(t)
    nj = f // FT

    def body(h_ref, res_ref, wg_ref, wu_ref, wd_ref, g_ref, be_ref, u_ref, v_ref, r_ref, y_ref, yb_ref, acc, us, vs):
        j = pl.program_id(1)

        def up():
            h = h_ref[...]
            u = _nt(h, wg_ref[...])
            v = _nt(h, wu_ref[...])
            u_ref[...] = u.astype(BF16)
            v_ref[...] = v.astype(BF16)
            return u, v

        def down():
            u, v = us[...], vs[...]
            return _nn((u * _sig(u) * v).astype(BF16), wd_ref[...])

        @pl.when(j == 0)
        def _():
            us[...], vs[...] = up()
            acc[...] = jnp.zeros_like(acc)

        @pl.when((j > 0) & (j < nj))
        def _():
            d = down()
            u, v = up()
            acc[...] += d
            us[...] = u
            vs[...] = v

        @pl.when(j == nj)
        def _():
            r = ALPHA * res_ref[...] + 0.5 * (acc[...] + down())
            y = _layer_norm(r, g_ref[...], be_ref[...])
            r_ref[...] = r
            y_ref[...] = y
            yb_ref[...] = y.astype(BF16)

    row = pl.BlockSpec((tm, D), lambda i, j: (i, 0))
    vec = pl.BlockSpec((1, D), lambda i, j: (0, 0))
    wup = pl.BlockSpec((FT, D), lambda i, j: (jnp.minimum(j, nj - 1), 0))
    wdn = pl.BlockSpec((FT, D), lambda i, j: (jnp.maximum(j - 1, 0), 0))
    act = pl.BlockSpec((tm, FT), lambda i, j: (i, jnp.minimum(j, nj - 1)))
    return pl.pallas_call(
        body, name="ffn_fwd", grid=(t // tm, nj + 1),
        in_specs=[row, row, wup, wup, wdn, vec, vec],
        out_specs=[act, act, row, row, row],
        out_shape=[jax.ShapeDtypeStruct((t, f), BF16), jax.ShapeDtypeStruct((t, f), BF16),
                   jax.ShapeDtypeStruct((t, D), F32), jax.ShapeDtypeStruct((t, D), F32),
                   jax.ShapeDtypeStruct((t, D), BF16)],
        scratch_shapes=[pltpu.VMEM((tm, D), F32), pltpu.VMEM((tm, FT), F32), pltpu.VMEM((tm, FT), F32)],
        compiler_params=_cp("arbitrary", "arbitrary"),
    )(hb, res, wg, wu, wd, gamma, beta)


def ffn_bwd_act(dfb, u, v, wg, wu, wd):
    t, f = u.shape
    tm = _row_tile(t)

    nj = f // FT

    def body(df_ref, u_ref, v_ref, wg_ref, wu_ref, wd_ref, du_ref, dv_ref, dh_ref, das):
        j = pl.program_id(1)

        def first():
            return _nt(df_ref[...], wd_ref[...])

        def second():
            da = das[...]
            uu = u_ref[...].astype(F32)
            sg = _sig(uu)
            du = (da * v_ref[...].astype(F32) * (sg * (1.0 + uu * (1.0 - sg)))).astype(BF16)
            dv = (da * uu * sg).astype(BF16)
            du_ref[...] = du
            dv_ref[...] = dv
            return _nn(du, wg_ref[...]) + _nn(dv, wu_ref[...])

        @pl.when(j == 0)
        def _():
            das[...] = first()
            dh_ref[...] = jnp.zeros_like(dh_ref)

        @pl.when((j > 0) & (j < nj))
        def _():
            tot = second()
            da = first()
            dh_ref[...] += tot
            das[...] = da

        @pl.when(j == nj)
        def _():
            dh_ref[...] += second()

    row = pl.BlockSpec((tm, D), lambda i, j: (i, 0))
    wfirst = pl.BlockSpec((FT, D), lambda i, j: (jnp.minimum(j, nj - 1), 0))
    wsecond = pl.BlockSpec((FT, D), lambda i, j: (jnp.maximum(j - 1, 0), 0))
    act = pl.BlockSpec((tm, FT), lambda i, j: (i, jnp.maximum(j - 1, 0)))
    return pl.pallas_call(
        body, name="ffn_bwd_act", grid=(t // tm, nj + 1),
        in_specs=[row, act, act, wsecond, wsecond, wfirst],
        out_specs=[act, act, row],
        out_shape=[jax.ShapeDtypeStruct((t, f), BF16), jax.ShapeDtypeStruct((t, f), BF16),
                   jax.ShapeDtypeStruct((t, D), F32)],
        scratch_shapes=[pltpu.VMEM((tm, FT), F32)],
        compiler_params=_cp("arbitrary", "arbitrary"),
    )(dfb, u, v, wg, wu, wd)


def ffn_fwd_seq(x, ln_in, wg, wu, wd, ln_out):
    t = x.shape[0]
    f = wg.shape[0]
    nj, nr = f // FT, t // _row_tile(t)
    rc = t // nr
    plain = ln_in is None
    gi, bi = ln_out if plain else ln_in

    def body(x_hbm, gi_ref, bi_ref, go_ref, bo_ref, wg_ref, wu_ref, wd_ref, u_ref, v_ref, r_hbm, yb_hbm,
             acc, hbs, xbuf, sem_in, sem_out):
        j = pl.program_id(0)

        @pl.when(j == 0)
        def _():
            def fetch(k):
                return pltpu.make_async_copy(x_hbm.at[pl.ds(k * rc, rc)], xbuf.at[k % 2], sem_in.at[k % 2])

            fetch(0).start()
            for k in range(nr):
                if k + 1 < nr:
                    fetch(k + 1).start()
                fetch(k).wait()
                h = xbuf[k % 2]
                if not plain:
                    h = _layer_norm(h, gi_ref[...], bi_ref[...])
                acc[k * rc:(k + 1) * rc, :] = ALPHA * h
                hbs[k * rc:(k + 1) * rc, :] = h.astype(BF16)

        for k in range(nr):
            sl = slice(k * rc, (k + 1) * rc)
            h = hbs[sl, :]
            u = _nt(h, wg_ref[...])
            v = _nt(h, wu_ref[...])
            u_ref[sl, :] = u.astype(BF16)
            v_ref[sl, :] = v.astype(BF16)
            acc[sl, :] += _nn((0.5 * u * _sig(u) * v).astype(BF16), wd_ref[...])

        @pl.when(j == nj - 1)
        def _():
            r_cp = pltpu.make_async_copy(acc, r_hbm, sem_out.at[0])
            r_cp.start()
            for k in range(nr):
                sl = slice(k * rc, (k + 1) * rc)
                hbs[sl, :] = _layer_norm(acc[sl, :], go_ref[...], bo_ref[...]).astype(BF16)
            y_cp = pltpu.make_async_copy(hbs, yb_hbm, sem_out.at[1])
            y_cp.start()
            r_cp.wait()
            y_cp.wait()

    vec = pl.BlockSpec((1, D), lambda j: (0, 0))
    wsp = pl.BlockSpec((FT, D), lambda j: (j, 0))
    act = pl.BlockSpec((None, t, FT), lambda j: (j, 0, 0))
    return pl.pallas_call(
        body, name="ffn_fwd_seq", grid=(nj,),
        in_specs=[_ANY, vec, vec, vec, vec, wsp, wsp, wsp],
        out_specs=[act, act, _ANY, _ANY],
        out_shape=[jax.ShapeDtypeStruct((nj, t, FT), BF16), jax.ShapeDtypeStruct((nj, t, FT), BF16),
                   jax.ShapeDtypeStruct((t, D), F32), jax.ShapeDtypeStruct((t, D), BF16)],
        scratch_shapes=[pltpu.VMEM((t, D), F32), pltpu.VMEM((t, D), BF16), pltpu.VMEM((2, rc, D), F32),
                        pltpu.SemaphoreType.DMA((2,)), pltpu.SemaphoreType.DMA((2,))],
        compiler_params=_cp("arbitrary"),
    )(x, gi, bi, ln_out[0], ln_out[1], wg, wu, wd)


def ffn_bwd_seq(parts, r, gamma, hb, u, v, wg, wu, wd, after=None):
    nj, t, _ = u.shape
    f = nj * FT
    nr = t // _row_tile(t)
    rc = t // nr
    nc = t // BLK
    scales = [s for _, s in parts]
    npart = len(parts)
    extra = [] if after is None else [after]

    def body(*refs):
        refs = refs[len(extra):]
        p_hbm, refs = refs[:npart], refs[npart:]
        (r_hbm, g_ref, hb_hbm, u_ref, v_ref, wg_ref, wu_ref, wd_ref, dh_hbm, dwg_ref, dwu_ref, dwd_ref, dg_ref, db_ref,
         dfs, hbt, dft, dhacc, dus, dvs, acs, pbuf, rbuf, hbuf, sems, sem_out) = refs
        j = pl.program_id(0)

        @pl.when(j == 0)
        def _():
            def fetch(c):
                rows = pl.ds(c * BLK, BLK)
                cps = [pltpu.make_async_copy(p_hbm[p].at[rows], pbuf.at[c % 2, p], sems.at[c % 2, p]) for p in range(npart)]
                cps.append(pltpu.make_async_copy(r_hbm.at[rows], rbuf.at[c % 2], sems.at[c % 2, npart]))
                cps.append(pltpu.make_async_copy(hb_hbm.at[rows], hbuf.at[c % 2], sems.at[c % 2, npart + 1]))
                return cps

            for cp in fetch(0):
                cp.start()
            dg = jnp.zeros((1, D), F32)
            db = jnp.zeros((1, D), F32)
            for c in range(nc):
                if c + 1 < nc:
                    for cp in fetch(c + 1):
                        cp.start()
                for cp in fetch(c):
                    cp.wait()
                sl = slice(c * BLK, (c + 1) * BLK)
                dy = scales[0] * pbuf[c % 2, 0]
                for p in range(1, npart):
                    dy += scales[p] * pbuf[c % 2, p]
                rr = rbuf[c % 2]
                xc = rr - jnp.mean(rr, axis=1, keepdims=True)
                rstd = lax.rsqrt(jnp.mean(xc * xc, axis=1, keepdims=True) + EPS)
                xh = xc * rstd
                dxh = dy * g_ref[...]
                dr = rstd * (dxh - jnp.mean(dxh, axis=1, keepdims=True) - xh * jnp.mean(dxh * xh, axis=1, keepdims=True))
                dg += jnp.sum(dy * xh, axis=0, keepdims=True)
                db += jnp.sum(dy, axis=0, keepdims=True)
                dhacc[sl, :] = ALPHA * dr
                dfc = (0.5 * dr).astype(BF16)
                dfs[sl, :] = dfc
                dft[:, sl] = dfc.T
                hbt[:, sl] = hbuf[c % 2].T
            dg_ref[...] = dg
            db_ref[...] = db

        for k in range(nr):
            sl = slice(k * rc, (k + 1) * rc)
            da = _nt(dfs[sl, :], wd_ref[...])
            uu = u_ref[sl, :].astype(F32)
            vv = v_ref[sl, :].astype(F32)
            sg = _sig(uu)
            du = (da * vv * (sg * (1.0 + uu * (1.0 - sg)))).astype(BF16)
            dv = (da * uu * sg).astype(BF16)
            dus[sl, :] = du
            dvs[sl, :] = dv
            acs[sl, :] = (uu * sg * vv).astype(BF16)
            dhacc[sl, :] += _nn(du, wg_ref[...]) + _nn(dv, wu_ref[...])
        dwg_ref[...] = _nn(hbt[...], dus[...]).T.astype(BF16)
        dwu_ref[...] = _nn(hbt[...], dvs[...]).T.astype(BF16)
        dwd_ref[...] = _nn(dft[...], acs[...]).T.astype(BF16)

        @pl.when(j == nj - 1)
        def _():
            cp = pltpu.make_async_copy(dhacc, dh_hbm, sem_out.at[0])
            cp.start()
            cp.wait()

    vec = pl.BlockSpec((1, D), lambda j: (0, 0))
    wsp = pl.BlockSpec((FT, D), lambda j: (j, 0))
    act = pl.BlockSpec((None, t, FT), lambda j: (j, 0, 0))
    return pl.pallas_call(
        body, name="ffn_bwd_seq", grid=(nj,),
        in_specs=[_ANY] * (len(extra) + npart + 1) + [vec, _ANY, act, act, wsp, wsp, wsp],
        out_specs=[_ANY, wsp, wsp, wsp, vec, vec],
        out_shape=[jax.ShapeDtypeStruct((t, D), F32)] + [jax.ShapeDtypeStruct((f, D), BF16)] * 3
        + [jax.ShapeDtypeStruct((1, D), F32)] * 2,
        scratch_shapes=[pltpu.VMEM((t, D), BF16), pltpu.VMEM((D, t), BF16), pltpu.VMEM((D, t), BF16),
                        pltpu.VMEM((t, D), F32), pltpu.VMEM((t, FT), BF16), pltpu.VMEM((t, FT), BF16),
                        pltpu.VMEM((t, FT), BF16), pltpu.VMEM((2, npart, BLK, D), F32), pltpu.VMEM((2, BLK, D), F32),
                        pltpu.VMEM((2, BLK, D), BF16), pltpu.SemaphoreType.DMA((2, npart + 2)),
                        pltpu.SemaphoreType.DMA((1,))],
        compiler_params=_cp("arbitrary"),
    )(*extra, *[p for p, _ in parts], r, gamma, hb, u, v, wg, wu, wd)


def mm_res_ln(a, b, x, ln_in, ln_out):
    t, k = a.shape
    tm = _row_tile(t)

    def body(a_ref, b_ref, x_ref, gi_ref, bi_ref, go_ref, bo_ref, r_ref, yb_ref):
        r = ALPHA * _layer_norm(x_ref[...], gi_ref[...], bi_ref[...]) + _nn(a_ref[...], b_ref[...])
        r_ref[...] = r
        yb_ref[...] = _layer_norm(r, go_ref[...], bo_ref[...]).astype(BF16)

    row = pl.BlockSpec((tm, D), lambda i: (i, 0))
    vec = pl.BlockSpec((1, D), lambda i: (0, 0))
    return pl.pallas_call(
        body, name="mm_res_ln", grid=(t // tm,),
        in_specs=[pl.BlockSpec((tm, k), lambda i: (i, 0)), pl.BlockSpec((k, D), lambda i: (0, 0)), row, vec, vec, vec, vec],
        out_specs=[row, row],
        out_shape=[jax.ShapeDtypeStruct((t, D), F32), jax.ShapeDtypeStruct((t, D), BF16)],
        compiler_params=_cp("arbitrary"),
    )(a, b, x, ln_in[0], ln_in[1], ln_out[0], ln_out[1])


def mm_nn(a, b, tn=512):
    t, k = a.shape
    n = b.shape[1]
    tm = _row_tile(t)

    def body(a_ref, b_ref, o_ref):
        o_ref[...] = _nn(a_ref[...], b_ref[...])

    return pl.pallas_call(
        body, name="mm_nn", grid=(n // tn, t // tm),
        in_specs=[pl.BlockSpec((tm, k), lambda j, i: (i, 0)), pl.BlockSpec((k, tn), lambda j, i: (0, j))],
        out_specs=pl.BlockSpec((tm, tn), lambda j, i: (i, j)),
        out_shape=jax.ShapeDtypeStruct((t, n), F32),
        compiler_params=_cp("arbitrary", "arbitrary"),
    )(a, b)


def mm_nt_reduce(pairs, n):
    g, t, _ = pairs[0][0].shape
    tm = _row_tile(t)
    npair = len(pairs)

    def body(*refs):
        o_ref = refs[-1]
        gi = pl.program_id(1)
        tot = _nt(refs[0][...], refs[1][...])
        for p in range(1, npair):
            tot += _nt(refs[2 * p][...], refs[2 * p + 1][...])

        @pl.when(gi == 0)
        def _():
            o_ref[...] = tot

        @pl.when(gi > 0)
        def _():
            o_ref[...] += tot

    in_specs, args = [], []
    for x, w in pairs:
        k = x.shape[2]
        in_specs += [pl.BlockSpec((None, tm, k), lambda i, gi: (gi, i, 0)),
                     pl.BlockSpec((None, n, k), lambda i, gi: (gi, 0, 0))]
        args += [x, w]
    return pl.pallas_call(
        body, name="mm_nt_reduce", grid=(t // tm, g),
        in_specs=in_specs, out_specs=pl.BlockSpec((tm, n), lambda i, gi: (i, 0)),
        out_shape=jax.ShapeDtypeStruct((t, n), F32),
        compiler_params=_cp("arbitrary", "arbitrary"),
    )(*args)


def mm_tn(x, y, out_dtype=BF16):
    gx, t, k = x.shape
    gy, _, n = y.shape
    g = max(gx, gy)
    tm = _row_tile(t)
    nt = t // tm

    def body(x_ref, y_ref, o_ref, acc):
        i = pl.program_id(1)

        @pl.when(i == 0)
        def _():
            acc[...] = jnp.zeros_like(acc)

        acc[...] += _tn(x_ref[...], y_ref[...])

        @pl.when(i == nt - 1)
        def _():
            o_ref[...] = acc[...].astype(out_dtype)

    return pl.pallas_call(
        body, name="mm_tn", grid=(g, nt),
        in_specs=[pl.BlockSpec((None, tm, k), (lambda gi, i: (gi, i, 0)) if gx > 1 else (lambda gi, i: (0, i, 0))),
                  pl.BlockSpec((None, tm, n), (lambda gi, i: (gi, i, 0)) if gy > 1 else (lambda gi, i: (0, i, 0)))],
        out_specs=pl.BlockSpec((None, k, n), lambda gi, i: (gi, 0, 0)),
        out_shape=jax.ShapeDtypeStruct((g, k, n), out_dtype),
        scratch_shapes=[pltpu.VMEM((k, n), F32)],
        compiler_params=_cp("arbitrary", "arbitrary"),
    )(x, y)


def ln_bwd(parts, r, gamma, out_scale, after=None):
    t = r.shape[0]
    tm = _row_tile(t)
    scales = [s for _, s in parts]
    npart = len(parts)
    extra = [] if after is None else [after]

    def body(*refs):
        refs = refs[len(extra):]
        r_ref, g_ref = refs[npart], refs[npart + 1]
        dr_ref, drb_ref, dg_ref, db_ref = refs[npart + 2:]
        i = pl.program_id(0)
        dy = scales[0] * refs[0][...]
        for p in range(1, npart):
            dy += scales[p] * refs[p][...]
        rr = r_ref[...]
        mu = jnp.mean(rr, axis=1, keepdims=True)
        xc = rr - mu
        rstd = lax.rsqrt(jnp.mean(xc * xc, axis=1, keepdims=True) + EPS)
        xh = xc * rstd
        dxh = dy * g_ref[...]
        m1 = jnp.mean(dxh, axis=1, keepdims=True)
        m2 = jnp.mean(dxh * xh, axis=1, keepdims=True)
        dr = rstd * (dxh - m1 - xh * m2)
        dr_ref[...] = dr
        drb_ref[...] = (out_scale * dr).astype(BF16)
        dg = jnp.sum(dy * xh, axis=0, keepdims=True)
        db = jnp.sum(dy, axis=0, keepdims=True)

        @pl.when(i == 0)
        def _():
            dg_ref[...] = dg
            db_ref[...] = db

        @pl.when(i > 0)
        def _():
            dg_ref[...] += dg
            db_ref[...] += db

    row = pl.BlockSpec((tm, D), lambda i: (i, 0))
    vec = pl.BlockSpec((1, D), lambda i: (0, 0))
    return pl.pallas_call(
        body, name="ln_bwd", grid=(t // tm,),
        in_specs=[_ANY] * len(extra) + [row] * (npart + 1) + [vec],
        out_specs=[row, row, vec, vec],
        out_shape=[jax.ShapeDtypeStruct((t, D), F32), jax.ShapeDtypeStruct((t, D), BF16),
                   jax.ShapeDtypeStruct((1, D), F32), jax.ShapeDtypeStruct((1, D), F32)],
        compiler_params=_cp("arbitrary"),
    )(*extra, *[p for p, _ in parts], r, gamma)


def loss_head(r, ln, target):
    t = r.shape[0]
    nb = t // BLK

    def body(r_ref, g_ref, b_ref, t_ref, dy_ref, l_ref):
        i = pl.program_id(0)

        @pl.when(i == 0)
        def _():
            dy_ref[...] = jnp.zeros_like(dy_ref)
            l_ref[...] = jnp.zeros_like(l_ref)

        @pl.when(i > 0)
        def _():
            err = _layer_norm(r_ref[...], g_ref[...], b_ref[...]) - t_ref[...]
            dy_ref[...] = err * (1.0 / D)
            l_ref[...] += (0.5 / D) * jnp.sum(err * err, keepdims=True)

    vec = pl.BlockSpec((1, D), lambda i: (0, 0))
    return pl.pallas_call(
        body, name="loss_head", grid=(nb,),
        in_specs=[pl.BlockSpec((BLK, D), lambda i: (i, 0)), vec, vec,
                  pl.BlockSpec((BLK, D), lambda i: (jnp.maximum(i - 1, 0), 0))],
        out_specs=[pl.BlockSpec((BLK, D), lambda i: (i, 0)), pl.BlockSpec((1, 1), lambda i: (0, 0))],
        out_shape=[jax.ShapeDtypeStruct((t, D), F32), jax.ShapeDtypeStruct((1, 1), F32)],
        compiler_params=_cp("arbitrary"),
    )(r, ln[0], ln[1], target)


def split_dh0(dh0, after=None):
    t = dh0.shape[0]
    nb = t // BLK
    extra = [] if after is None else [after]

    def body(*refs):
        a_ref, gx_ref, gm_ref = refs[len(extra):]
        i = pl.program_id(0)
        tot = a_ref[...]

        @pl.when(i == 0)
        def _():
            gm_ref[...] = tot[PAD:, :]

        @pl.when(i > 0)
        def _():
            gx_ref[...] = tot

    blk = pl.BlockSpec((BLK, D), lambda i: (i, 0))
    return pl.pallas_call(
        body, name="split_dh0", grid=(nb,),
        in_specs=[_ANY] * len(extra) + [blk],
        out_specs=[pl.BlockSpec((BLK, D), lambda i: (jnp.maximum(i - 1, 0), 0)),
                   pl.BlockSpec((N_META, D), lambda i: (0, 0))],
        out_shape=[jax.ShapeDtypeStruct((t - BLK, D), F32), jax.ShapeDtypeStruct((N_META, D), F32)],
        compiler_params=_cp("arbitrary"),
    )(*extra, dh0)


def _valid_rows(nrows, first_row):
    return (first_row + lax.broadcasted_iota(jnp.int32, (nrows, 1), 0)) >= PAD


def conv_fwd(proj, conv_w, conv_b):
    t = proj.shape[0]
    c0 = C_XBC // BLK

    def body(x_ref, w_ref, b_ref, o_ref):
        ok = _valid_rows(t, 0)
        x = jnp.where(ok, x_ref[...], 0.0)
        w = w_ref[...]
        acc = b_ref[...] + w[CONV_K - 1:CONV_K, :] * x
        for s in range(1, CONV_K):
            acc += w[CONV_K - 1 - s:CONV_K - s, :] * pltpu.roll(x, s, 0)
        o_ref[...] = jnp.where(ok, acc * _sig(acc), 0.0)

    return pl.pallas_call(
        body, name="conv_fwd", grid=(CONV_D // BLK,),
        in_specs=[pl.BlockSpec((t, BLK), lambda j: (0, c0 + j)),
                  pl.BlockSpec((CONV_K, BLK), lambda j: (0, j)), pl.BlockSpec((1, BLK), lambda j: (0, j))],
        out_specs=pl.BlockSpec((t, BLK), lambda j: (0, j)),
        out_shape=jax.ShapeDtypeStruct((t, CONV_D), F32),
        compiler_params=_cp("arbitrary"),
    )(proj, conv_w, conv_b)


def conv_bwd(dxa, proj, conv_w, conv_b):
    t = proj.shape[0]
    c0 = C_XBC // BLK

    def body(d_ref, x_ref, w_ref, b_ref, dx_ref, dw_ref, db_ref):
        ok = _valid_rows(t, 0)
        x = jnp.where(ok, x_ref[...], 0.0)
        w = w_ref[...]
        xs = [x] + [pltpu.roll(x, s, 0) for s in range(1, CONV_K)]
        acc = b_ref[...] + w[CONV_K - 1:CONV_K, :] * x
        for s in range(1, CONV_K):
            acc += w[CONV_K - 1 - s:CONV_K - s, :] * xs[s]
        sg = _sig(acc)
        dxc = jnp.where(ok, d_ref[...] * (sg * (1.0 + acc * (1.0 - sg))), 0.0)
        db_ref[...] = jnp.sum(dxc, axis=0, keepdims=True)
        dw_ref[...] = jnp.concatenate(
            [jnp.sum(dxc * xs[CONV_K - 1 - k], axis=0, keepdims=True) for k in range(CONV_K)], axis=0)
        dx = w[CONV_K - 1:CONV_K, :] * dxc
        for s in range(1, CONV_K):
            dx += w[CONV_K - 1 - s:CONV_K - s, :] * pltpu.roll(dxc, t - s, 0)
        dx_ref[...] = jnp.where(ok, dx, 0.0)

    col = pl.BlockSpec((t, BLK), lambda j: (0, j))
    return pl.pallas_call(
        body, name="conv_bwd", grid=(CONV_D // BLK,),
        in_specs=[col, pl.BlockSpec((t, BLK), lambda j: (0, c0 + j)),
                  pl.BlockSpec((CONV_K, BLK), lambda j: (0, j)), pl.BlockSpec((1, BLK), lambda j: (0, j))],
        out_specs=[col, pl.BlockSpec((CONV_K, BLK), lambda j: (0, j)), pl.BlockSpec((1, BLK), lambda j: (0, j))],
        out_shape=[jax.ShapeDtypeStruct((t, CONV_D), F32), jax.ShapeDtypeStruct((CONV_K, CONV_D), F32),
                   jax.ShapeDtypeStruct((1, CONV_D), F32)],
        compiler_params=_cp("arbitrary"),
    )(dxa, proj, conv_w, conv_b)


def _softplus(x):
    return jnp.maximum(x, 0.0) + jnp.log(1.0 + jnp.exp(-jnp.abs(x)))


GW = SSD_D // SSD_G
HPG = SSD_H // SSD_G


def _head_expand():
    r = lax.broadcasted_iota(jnp.int32, (BLK, SSD_D), 0)
    c = lax.broadcasted_iota(jnp.int32, (BLK, SSD_D), 1)
    rt = lax.broadcasted_iota(jnp.int32, (SSD_D, BLK), 0)
    ct = lax.broadcasted_iota(jnp.int32, (SSD_D, BLK), 1)
    return (c // SSD_P == r).astype(F32), (rt // SSD_P == ct).astype(F32)


def _ssd_chunk(xa, sm, dtb, alog, dskip, ok, sp):
    e, et = _head_expand()
    dt = jnp.where(ok, _softplus(sm + dtb), 0.0)
    amat = -jnp.exp(alog)
    tri = _tri()
    ac = _nn_hi(tri.astype(F32), dt * amat)
    act = ac.T
    ace, dte, dse = _nn_hi(ac, e), _nn_hi(dt, e), _nn_hi(dskip, e)
    laste = ace[BLK - 1:BLK, :]
    ee, dece, gle = jnp.exp(ace), jnp.exp(laste - ace), jnp.exp(laste)
    xs = xa[:, :SSD_D]
    xdt = xs * dte
    decx = dece * xdt
    xdtb = xdt.astype(BF16)
    d = dict(e=e, et=et, dt=dt, amat=amat, tri=tri, ac=ac, act=act, dte=dte, dse=dse, ee=ee, dece=dece, gle=gle, xs=xs,
             xdt=xdt, xdtb=xdtb, decx=decx, bg=[], cg=[], cb=[], yo=[], seg=[], m=[], new_s=[])
    ys = []
    for g in range(SSD_G):
        cols = slice(GW * g, GW * (g + 1))
        bg = xa[:, SSD_D + SSD_N * g:SSD_D + SSD_N * (g + 1)].astype(BF16)
        cg = xa[:, SSD_D + SSD_G * SSD_N + SSD_N * g:SSD_D + SSD_G * SSD_N + SSD_N * (g + 1)].astype(BF16)
        spg = sp[:, cols]
        sloc = _tn(bg, decx[:, cols].astype(BF16))
        yo = _nn(cg, spg.astype(BF16)) * ee[:, cols]
        cb = _nt(cg, bg)
        d["new_s"].append(gle[:, cols] * spg + sloc)
        yds = []
        for h in range(HPG * g, HPG * (g + 1)):
            seg = jnp.where(tri, jnp.exp(jnp.minimum(ac[:, h:h + 1] - act[h:h + 1, :], 0.0)), 0.0)
            m = cb * seg
            yds.append(_nn(m.astype(BF16), xdtb[:, SSD_P * h:SSD_P * (h + 1)]))
            d["seg"].append(seg)
            d["m"].append(m)
        ys.append(jnp.concatenate(yds, axis=1) + yo)
        for k, val in (("bg", bg), ("cg", cg), ("cb", cb), ("yo", yo)):
            d[k].append(val)
    d["y"] = jnp.concatenate(ys, axis=1) + dse * xs
    return d


def ssd_fwd(xa, proj, dtb, alog, dskip, normg):
    t = xa.shape[0]
    nb = t // BLK
    gw = SSD_D // SSD_G

    def body(xa_ref, z_ref, sm_ref, dtb_ref, al_ref, ds_ref, ng_ref, y_ref, sp_ref, st):
        c = pl.program_id(0)

        @pl.when(c == 0)
        def _():
            st[...] = jnp.zeros_like(st)

        ok = _valid_rows(BLK, c * BLK)
        sp = st[...]
        sp_ref[...] = sp
        d = _ssd_chunk(xa_ref[...], sm_ref[...], dtb_ref[...], al_ref[...], ds_ref[...], ok, sp)
        st[...] = jnp.concatenate(d["new_s"], axis=1)
        y = d["y"]
        z = z_ref[...]
        yg = y * (z * _sig(z))
        outs = []
        for g in range(SSD_G):
            v = yg[:, gw * g:gw * (g + 1)]
            outs.append(v * lax.rsqrt(jnp.mean(v * v, axis=1, keepdims=True) + EPS))
        y_ref[...] = (jnp.concatenate(outs, axis=1) * ng_ref[...]).astype(BF16)

    vec = pl.BlockSpec((1, BLK), lambda c: (0, 0))
    return pl.pallas_call(
        body, name="ssd_fwd", grid=(nb,),
        in_specs=[pl.BlockSpec((BLK, CONV_D), lambda c: (c, 0)),
                  pl.BlockSpec((BLK, SSD_D), lambda c: (c, C_Z // SSD_D)),
                  pl.BlockSpec((BLK, BLK), lambda c: (c, C_SM // BLK)),
                  vec, vec, vec, pl.BlockSpec((1, SSD_D), lambda c: (0, 0))],
        out_specs=[pl.BlockSpec((BLK, SSD_D), lambda c: (c, 0)),
                   pl.BlockSpec((None, SSD_N, SSD_D), lambda c: (c, 0, 0))],
        out_shape=[jax.ShapeDtypeStruct((t, SSD_D), BF16), jax.ShapeDtypeStruct((nb, SSD_N, SSD_D), F32)],
        scratch_shapes=[pltpu.VMEM((SSD_N, SSD_D), F32)],
        compiler_params=_cp("arbitrary"),
    )(xa, proj, proj, dtb, alog, dskip, normg)


def _lane_put(col, lane):
    li = lax.broadcasted_iota(jnp.int32, (col.shape[0], BLK), 1)
    return jnp.where(li == lane, col, 0.0)


def ssd_bwd(dmix, xa, proj, sprev, dtb, alog, dskip, normg):
    t = xa.shape[0]
    nb = t // BLK
    gw = SSD_D // SSD_G
    rev = lambda c: nb - 1 - c

    def body(dy_ref, xa_ref, z_ref, sm_ref, sp_ref, dtb_ref, al_ref, ds_ref, ng_ref,
             dxa_ref, dz_ref, dsm_ref, dng_ref, dds_ref, dal_ref, ddtb_ref, dst):
        c = pl.program_id(0)

        @pl.when(c == 0)
        def _():
            dst[...] = jnp.zeros_like(dst)
            dng_ref[...] = jnp.zeros_like(dng_ref)
            dds_ref[...] = jnp.zeros_like(dds_ref)
            dal_ref[...] = jnp.zeros_like(dal_ref)
            ddtb_ref[...] = jnp.zeros_like(ddtb_ref)

        ok = _valid_rows(BLK, rev(c) * BLK)
        sm = sm_ref[...]
        sp = sp_ref[...]
        d = _ssd_chunk(xa_ref[...], sm, dtb_ref[...], al_ref[...], ds_ref[...], ok, sp)
        dt, amat, ac, act, tri, et, xs, xdt = (d[k] for k in ("dt", "amat", "ac", "act", "tri", "et", "xs", "xdt"))
        rowi = lax.broadcasted_iota(jnp.int32, (BLK, 1), 0)
        y = d["y"]
        z = z_ref[...]
        sgz = _sig(z)
        siluz = z * sgz
        yg = y * siluz
        dout = dy_ref[...]
        ng = ng_ref[...]
        dygs, xhs = [], []
        for g in range(SSD_G):
            v = yg[:, gw * g:gw * (g + 1)]
            rr = lax.rsqrt(jnp.mean(v * v, axis=1, keepdims=True) + EPS)
            xh = v * rr
            dxh = dout[:, gw * g:gw * (g + 1)] * ng[:, gw * g:gw * (g + 1)]
            dygs.append(rr * (dxh - xh * jnp.mean(dxh * xh, axis=1, keepdims=True)))
            xhs.append(xh)
        dyg = jnp.concatenate(dygs, axis=1)
        dng_ref[...] += jnp.sum(dout * jnp.concatenate(xhs, axis=1), axis=0, keepdims=True)
        dy = dyg * siluz
        dz_ref[...] = dyg * y * (sgz * (1.0 + z * (1.0 - sgz)))

        triu = _tri(lower=False)
        dyb = dy.astype(BF16)
        dsn = dst[...]
        dds_ref[...] += _nn_hi(jnp.sum(dy * xs, axis=0, keepdims=True), et)
        dac_all = _nn_hi(dy * jnp.concatenate(d["yo"], axis=1), et)
        dyo = (dy * d["ee"]).astype(BF16)
        gl = jnp.exp(ac[BLK - 1:BLK, :])
        dlast = _nn_hi(jnp.sum(dsn * sp, axis=0, keepdims=True), et) * gl
        bds, db_g, dc_g, dxdt_i, new_dst = [], [], [], [], []
        for g in range(SSD_G):
            cols = slice(GW * g, GW * (g + 1))
            bg, cg = d["bg"][g], d["cg"][g]
            dsng = dsn[:, cols].astype(BF16)
            dc = _nt(dyo[:, cols], sp[:, cols].astype(BF16))
            new_dst.append(_tn(cg, dyo[:, cols]) + d["gle"][:, cols] * dsn[:, cols])
            bds.append(_nn(bg, dsng))
            db = _nt(d["decx"][:, cols].astype(BF16), dsng)
            cbt = _nt(bg, cg)
            dcb = jnp.zeros((BLK, BLK), F32)
            for h in range(HPG * g, HPG * (g + 1)):
                hc = slice(SSD_P * h, SSD_P * (h + 1))
                dm = _nt(dyb[:, hc], d["xdtb"][:, hc])
                dcb += dm * d["seg"][h]
                w = dm * d["m"][h]
                dac_all += _lane_put(jnp.sum(w, axis=1, keepdims=True) - jnp.sum(w.T, axis=1, keepdims=True), h)
                segt = jnp.where(triu, jnp.exp(jnp.minimum(act[h:h + 1, :] - ac[:, h:h + 1], 0.0)), 0.0)
                dxdt_i.append(_nn((cbt * segt).astype(BF16), dyb[:, hc]))
            dcbb = dcb.astype(BF16)
            dc_g.append(dc + _nn(dcbb, bg))
            db_g.append(db + _tn(dcbb, cg))
        dst[...] = jnp.concatenate(new_dst, axis=1)
        bds = jnp.concatenate(bds, axis=1)
        tdec = jnp.exp(ac[BLK - 1:BLK, :] - ac) * _nn_hi(xdt * bds, et)
        dlast += jnp.sum(tdec, axis=0, keepdims=True)
        dac_all += jnp.where(rowi == BLK - 1, dlast, 0.0) - tdec
        dxdt = d["dece"] * bds + jnp.concatenate(dxdt_i, axis=1)
        da = _nn_hi(triu.astype(F32), dac_all)
        ddt = _nn_hi(dxdt * xs, et) + da * amat
        dal_ref[...] += jnp.sum(da * dt, axis=0, keepdims=True) * amat
        ddtr = jnp.where(ok, ddt * _sig(sm + dtb_ref[...]), 0.0)
        ddtb_ref[...] += jnp.sum(ddtr, axis=0, keepdims=True)
        dsm_ref[...] = ddtr
        dxs = d["dse"] * dy + dxdt * d["dte"]
        dxa_ref[...] = jnp.where(ok, jnp.concatenate([dxs] + db_g + dc_g, axis=1), 0.0)

    vec = pl.BlockSpec((1, BLK), lambda c: (0, 0))
    nvec = pl.BlockSpec((1, SSD_D), lambda c: (0, 0))
    return pl.pallas_call(
        body, name="ssd_bwd", grid=(nb,),
        in_specs=[pl.BlockSpec((BLK, SSD_D), lambda c: (rev(c), 0)),
                  pl.BlockSpec((BLK, CONV_D), lambda c: (rev(c), 0)),
                  pl.BlockSpec((BLK, SSD_D), lambda c: (rev(c), C_Z // SSD_D)),
                  pl.BlockSpec((BLK, BLK), lambda c: (rev(c), C_SM // BLK)),
                  pl.BlockSpec((None, SSD_N, SSD_D), lambda c: (rev(c), 0, 0)),
                  vec, vec, vec, nvec],
        out_specs=[pl.BlockSpec((BLK, CONV_D), lambda c: (rev(c), 0)),
                   pl.BlockSpec((BLK, SSD_D), lambda c: (rev(c), 0)),
                   pl.BlockSpec((BLK, BLK), lambda c: (rev(c), 0)),
                   nvec, vec, vec, vec],
        out_shape=[jax.ShapeDtypeStruct((t, CONV_D), F32), jax.ShapeDtypeStruct((t, SSD_D), F32),
                   jax.ShapeDtypeStruct((t, BLK), F32), jax.ShapeDtypeStruct((1, SSD_D), F32),
                   jax.ShapeDtypeStruct((1, BLK), F32), jax.ShapeDtypeStruct((1, BLK), F32),
                   jax.ShapeDtypeStruct((1, BLK), F32)],
        scratch_shapes=[pltpu.VMEM((SSD_N, SSD_D), F32)],
        compiler_params=_cp("arbitrary"),
    )(dmix, xa, proj, proj, sprev, dtb, alog, dskip, normg)


def _attn_scores(q_ref, k_ref, h, dq, scale, mask, bias):
    qh = q_ref[:, dq * h:dq * (h + 1)].astype(BF16)
    kh = k_ref[:, dq * h:dq * (h + 1)].astype(BF16)
    s = _nt(qh, kh) * scale
    if bias is not None:
        s = s + bias
    return qh, kh, jnp.where(mask, s, NEG)


def _segments(nb):
    cuts = sorted({0, nb} | {max(1, round(nb * f)) for f in (0.3, 0.53, 0.77)})
    return list(zip(cuts[:-1], cuts[1:]))


def attn_fwd(q, k, v, qcol, kcol, vcol, nh, dq, dv, scale, c_col=None, c_row=None, lane0=0):
    t = q.shape[0]
    tq = BLK
    use_bias = c_col is not None

    def segment(t0, t1, prev):
        tk = t1 * BLK
        nprev = len(prev)

        def body(*refs):
            refs = refs[nprev:]
            if use_bias:
                q_ref, k_ref, v_ref, cc_ref, cr_ref, o_ref, l_ref = refs
            else:
                q_ref, k_ref, v_ref, o_ref, l_ref = refs
            i = pl.program_id(0)
            rowg = (t0 + i) * tq + lax.broadcasted_iota(jnp.int32, (tq, 1), 0)
            col = lax.broadcasted_iota(jnp.int32, (1, tk), 1)
            mask = (col <= rowg) & (col >= PAD)
            outs = []
            lse = jnp.zeros((tq, BLK), F32)
            for h in range(nh):
                bias = (cc_ref[:, lane0 + h:lane0 + h + 1] - cr_ref[h:h + 1, :]) if use_bias else None
                _, _, s = _attn_scores(q_ref, k_ref, h, dq, scale, mask, bias)
                m = jnp.max(s, axis=1, keepdims=True)
                p = jnp.exp(s - m)
                l = jnp.sum(p, axis=1, keepdims=True)
                vh = v_ref[:, dv * h:dv * (h + 1)].astype(BF16)
                outs.append(_nn(p.astype(BF16), vh) / l)
                lse += _lane_put(m + jnp.log(l), h)
            o_ref[...] = jnp.concatenate(outs, axis=1).astype(BF16)
            l_ref[...] = lse.T[0:8, :]

        in_specs = [_ANY] * nprev + [pl.BlockSpec((tq, nh * dq), lambda i: (t0 + i, qcol)),
                                     pl.BlockSpec((tk, nh * dq), lambda i: (0, kcol)),
                                     pl.BlockSpec((tk, nh * dv), lambda i: (0, vcol))]
        args = list(prev) + [q, k, v]
        if use_bias:
            in_specs += [pl.BlockSpec((tq, BLK), lambda i: (t0 + i, 0)), pl.BlockSpec((8, tk), lambda i: (0, 0))]
            args += [c_col, c_row]
        return pl.pallas_call(
            body, name="attn_fwd", grid=(t1 - t0,),
            in_specs=in_specs,
            out_specs=[pl.BlockSpec((tq, nh * dv), lambda i: (t0 + i, 0)), pl.BlockSpec((8, tq), lambda i: (0, t0 + i))],
            out_shape=[jax.ShapeDtypeStruct((t, nh * dv), BF16), jax.ShapeDtypeStruct((8, t), F32)],
            input_output_aliases={p: p for p in range(nprev)},
            compiler_params=_cp("arbitrary"),
        )(*args)

    outs = []
    for t0, t1 in _segments(t // tq):
        outs = segment(t0, t1, outs)
    return outs


def attn_bwd(q, k, v, do, lse_row, o, qcol, kcol, vcol, docol, ocol, nh, dq, dv, scale, c_col=None, c_row=None, lane0=0):
    t = q.shape[0]
    tq = BLK
    use_bias = c_col is not None

    def segment(t0, t1, prev):
        tk = t1 * BLK
        nprev = len(prev)

        def body(*refs):
            pv, refs = refs[:nprev], refs[nprev:]
            kt = refs[-1]
            if use_bias:
                q_ref, k_ref, v_ref, do_ref, l_ref, o_ref, cc_ref, cr_ref, dq_ref, dk_ref, dv_ref, dcq_ref, dck_ref = refs[:-1]
            else:
                q_ref, k_ref, v_ref, do_ref, l_ref, o_ref, dq_ref, dk_ref, dv_ref = refs[:-1]
            i = pl.program_id(0)

            @pl.when(i == 0)
            def _():
                kt[...] = k_ref[...].astype(BF16).T
                if nprev:
                    dk_ref[...] = pv[1][...]
                    dv_ref[...] = pv[2][...]
                    if use_bias:
                        dck_ref[...] = pv[4][...]
                else:
                    dk_ref[...] = jnp.zeros_like(dk_ref)
                    dv_ref[...] = jnp.zeros_like(dv_ref)
                    if use_bias:
                        dck_ref[...] = jnp.zeros_like(dck_ref)

            key = lax.broadcasted_iota(jnp.int32, (tk, 1), 0)
            qry = (t0 + i) * tq + lax.broadcasted_iota(jnp.int32, (1, tq), 1)
            mask = (key <= qry) & (key >= PAD)
            dot = (do_ref[...].astype(F32) * o_ref[...].astype(F32)).T
            lane = lax.broadcasted_iota(jnp.int32, (1, BLK), 1)
            dqts, dcqs = [], []
            for h in range(nh):
                qh = q_ref[:, dq * h:dq * (h + 1)].astype(BF16)
                kh = k_ref[:, dq * h:dq * (h + 1)].astype(BF16)
                vh = v_ref[:, dv * h:dv * (h + 1)].astype(BF16)
                doh = do_ref[:, dv * h:dv * (h + 1)].astype(BF16)
                delta = jnp.sum(dot[dv * h:dv * (h + 1), :], axis=0, keepdims=True)
                st = _nt(kh, qh) * scale
                if use_bias:
                    st = st + (cr_ref[h:h + 1, :] - cc_ref[h])
                pt = jnp.exp(jnp.where(mask, st, NEG) - l_ref[h:h + 1, :])
                dst = pt * (_nt(vh, doh) - delta)
                dsb = dst.astype(BF16)
                dk_ref[:, dq * h:dq * (h + 1)] += _nn(dsb, qh) * scale
                dv_ref[:, dv * h:dv * (h + 1)] += _nn(pt.astype(BF16), doh)
                dqts.append(_nn(kt[dq * h:dq * (h + 1), :], dsb))
                if use_bias:
                    dcqs.append(jnp.sum(dst, axis=0, keepdims=True))
                    dck_ref[h] += dst
            dq_ref[...] = jnp.concatenate(dqts, axis=0).T * scale
            if use_bias:
                dcq_ref[...] = jnp.concatenate(dcqs + [jnp.zeros((8 - nh, tq), F32)], axis=0)

        keys_q = pl.BlockSpec((tk, nh * dq), lambda i: (0, 0))
        keys_v = pl.BlockSpec((tk, nh * dv), lambda i: (0, 0))
        keys_c = pl.BlockSpec((nh, tk, BLK), lambda i: (0, 0, 0))
        qrow = pl.BlockSpec((8, tq), lambda i: (0, t0 + i))
        prev_specs = ([_ANY, keys_q, keys_v] + ([_ANY, keys_c] if use_bias else [])) if nprev else []
        in_specs = prev_specs + [pl.BlockSpec((tq, nh * dq), lambda i: (t0 + i, qcol)),
                                 pl.BlockSpec((tk, nh * dq), lambda i: (0, kcol)),
                                 pl.BlockSpec((tk, nh * dv), lambda i: (0, vcol)),
                                 pl.BlockSpec((tq, nh * dv), lambda i: (t0 + i, docol)),
                                 qrow,
                                 pl.BlockSpec((tq, nh * dv), lambda i: (t0 + i, ocol))]
        args = list(prev) + [q, k, v, do, lse_row, o]
        out_specs = [pl.BlockSpec((tq, nh * dq), lambda i: (t0 + i, 0)), keys_q, keys_v]
        out_shape = [jax.ShapeDtypeStruct((t, nh * dq), F32), jax.ShapeDtypeStruct((t, nh * dq), F32),
                     jax.ShapeDtypeStruct((t, nh * dv), F32)]
        if use_bias:
            in_specs += [keys_c, qrow]
            args += [c_col, c_row]
            out_specs += [qrow, keys_c]
            out_shape += [jax.ShapeDtypeStruct((8, t), F32), jax.ShapeDtypeStruct((nh, t, BLK), F32)]
        return pl.pallas_call(
            body, name="attn_bwd", grid=(t1 - t0,),
            in_specs=in_specs, out_specs=out_specs, out_shape=out_shape,
            scratch_shapes=[pltpu.VMEM((nh * dq, tk), BF16)],
            input_output_aliases={p: p for p in range(nprev)},
            compiler_params=_cp("arbitrary"),
        )(*args)

    outs = []
    for t0, t1 in reversed(_segments(t // tq)):
        outs = segment(t0, t1, outs)
    return outs


def fox_pre(proj, fb):
    t = proj.shape[0]
    nb = t // BLK

    def body(sm_ref, fb_ref, c_ref, cr_ref, cb_ref):
        x = sm_ref[...] + fb_ref[...]
        lane = lax.broadcasted_iota(jnp.int32, (1, BLK), 1)
        keep = _valid_rows(t, 0) & (lane >= SM_F) & (lane < SM_F + FOX_H)
        logf = jnp.where(keep, jnp.minimum(x, 0.0) - jnp.log(1.0 + jnp.exp(-jnp.abs(x))), 0.0)
        tri = _tri().astype(F32)
        carry = jnp.zeros((1, BLK), F32)
        for b in range(nb):
            cb = _nn_hi(tri, logf[b * BLK:(b + 1) * BLK, :]) + carry
            c_ref[b * BLK:(b + 1) * BLK, :] = cb
            carry = cb[BLK - 1:BLK, :]
        cr_ref[...] = c_ref[...].T[SM_F:SM_F + 8, :]
        for h in range(FOX_H):
            cb_ref[h] = jnp.broadcast_to(c_ref[:, SM_F + h:SM_F + h + 1], (t, BLK))

    return pl.pallas_call(
        body, name="fox_pre", grid=(1,),
        in_specs=[pl.BlockSpec((t, BLK), lambda i: (0, C_SM // BLK)), pl.BlockSpec((1, BLK), lambda i: (0, 0))],
        out_specs=[pl.BlockSpec((t, BLK), lambda i: (0, 0)), pl.BlockSpec((8, t), lambda i: (0, 0)),
                   pl.BlockSpec((FOX_H, t, BLK), lambda i: (0, 0, 0))],
        out_shape=[jax.ShapeDtypeStruct((t, BLK), F32), jax.ShapeDtypeStruct((8, t), F32),
                   jax.ShapeDtypeStruct((FOX_H, t, BLK), F32)],
        compiler_params=_cp("arbitrary"),
    )(proj, fb)


def fox_pre_bwd(dcq, dck, proj, fb, dsm_in):
    t = proj.shape[0]
    nb = t // BLK

    def body(dcq_ref, dck_ref, sm_ref, fb_ref, din_ref, dsm_ref, dfb_ref, scr):
        triu = _tri(lower=False).astype(F32)
        carry = jnp.zeros((1, BLK), F32)
        scr[...] = jnp.concatenate([jnp.zeros((SM_F, t), F32), dcq_ref[...], jnp.zeros((BLK - SM_F - 8, t), F32)], axis=0).T
        lane = lax.broadcasted_iota(jnp.int32, (1, BLK), 1)
        for b in range(nb - 1, -1, -1):
            blk = scr[b * BLK:(b + 1) * BLK, :]
            for h in range(FOX_H):
                blk -= jnp.where(lane == SM_F + h, jnp.sum(dck_ref[h, b * BLK:(b + 1) * BLK, :], axis=1, keepdims=True), 0.0)
            cb = _nn_hi(triu, blk) + carry
            scr[b * BLK:(b + 1) * BLK, :] = cb
            carry = cb[0:1, :]
        x = sm_ref[...] + fb_ref[...]
        lane = lax.broadcasted_iota(jnp.int32, (1, BLK), 1)
        keep = _valid_rows(t, 0) & (lane >= SM_F) & (lane < SM_F + FOX_H)
        df = jnp.where(keep, scr[...] * _sig(-x), 0.0)
        dfb_ref[...] = jnp.sum(df, axis=0, keepdims=True)
        dsm_ref[...] = din_ref[...] + df

    full = pl.BlockSpec((t, BLK), lambda i: (0, 0))
    return pl.pallas_call(
        body, name="fox_pre_bwd", grid=(1,),
        in_specs=[pl.BlockSpec((8, t), lambda i: (0, 0)), pl.BlockSpec((FOX_H, t, BLK), lambda i: (0, 0, 0)),
                  pl.BlockSpec((t, BLK), lambda i: (0, C_SM // BLK)), pl.BlockSpec((1, BLK), lambda i: (0, 0)), full],
        out_specs=[full, pl.BlockSpec((1, BLK), lambda i: (0, 0))],
        out_shape=[jax.ShapeDtypeStruct((t, BLK), F32), jax.ShapeDtypeStruct((1, BLK), F32)],
        scratch_shapes=[pltpu.VMEM((t, BLK), F32)],
        compiler_params=_cp("arbitrary"),
    )(dcq, dck, proj, fb, dsm_in)


def _swap_rope(x):
    lane = lax.broadcasted_iota(jnp.int32, (1, BLK), 1)
    return jnp.where((lane >= SM_KR) & (lane < SM_KR + 16), pltpu.roll(x, BLK - 16, 1),
                     jnp.where((lane >= SM_KR + 16) & (lane < SM_KR + 32), pltpu.roll(x, 16, 1), 0.0))


def _rms(x, g):
    r = lax.rsqrt(jnp.mean(x * x, axis=1, keepdims=True) + EPS)
    return r, x * r


def mla_pre(proj, qg, kvg, wq, wk, wv, cosq, sinq):
    t = proj.shape[0]
    tm = _row_tile(t)

    def body(cq_ref, ckv_ref, sm_ref, qg_ref, kvg_ref, wq_ref, wk_ref, wv_ref, cos_ref, sin_ref,
             q_ref, k_ref, v_ref, cqn_ref, ckvn_ref):
        cs, sn = cos_ref[...], sin_ref[...]
        _, xh = _rms(cq_ref[...], None)
        cqn = (xh * qg_ref[...]).astype(BF16)
        cqn_ref[...] = cqn
        qraw = _nn(cqn, wq_ref[...])
        qs = []
        for h in range(MLA_H):
            hb = qraw[:, BLK * h:BLK * (h + 1)]
            qs.append(hb * cs + _swap_rope(hb) * sn)
        q_ref[...] = jnp.concatenate(qs, axis=1).astype(BF16)
        _, kh = _rms(ckv_ref[...], None)
        ckvn = (kh * kvg_ref[...]).astype(BF16)
        ckvn_ref[...] = ckvn
        kraw = _nn(ckvn, wk_ref[...])
        v_ref[...] = _nn(ckvn, wv_ref[...]).astype(BF16)
        lane = lax.broadcasted_iota(jnp.int32, (1, BLK), 1)
        kr = sm_ref[...]
        krr = jnp.where((lane >= SM_KR) & (lane < SM_KR + MLA_ROPE), kr * cs + _swap_rope(kr) * sn, 0.0)
        k_ref[...] = jnp.concatenate([kraw[:, BLK * h:BLK * (h + 1)] + krr for h in range(MLA_H)], axis=1).astype(BF16)

    def rows(w, cb):
        return pl.BlockSpec((tm, w), lambda i: (i, cb))

    def whole(a):
        return pl.BlockSpec(a.shape, lambda i: (0, 0))

    return pl.pallas_call(
        body, name="mla_pre", grid=(t // tm,),
        in_specs=[rows(MLA_QL, C_CQ // MLA_QL), rows(MLA_KVL, C_CKV // MLA_KVL), rows(BLK, C_SM // BLK),
                  whole(qg), whole(kvg), whole(wq), whole(wk), whole(wv), rows(BLK, 0), rows(BLK, 0)],
        out_specs=[rows(512, 0), rows(512, 0), rows(256, 0), rows(MLA_QL, 0), rows(MLA_KVL, 0)],
        out_shape=[jax.ShapeDtypeStruct((t, 512), BF16), jax.ShapeDtypeStruct((t, 512), BF16),
                   jax.ShapeDtypeStruct((t, 256), BF16), jax.ShapeDtypeStruct((t, MLA_QL), BF16),
                   jax.ShapeDtypeStruct((t, MLA_KVL), BF16)],
        compiler_params=_cp("arbitrary"),
    )(proj, proj, proj, qg, kvg, wq, wk, wv, cosq, sinq)


def mla_pre_bwd(dq, dk, dv, proj, cqn, ckvn, qg, kvg, wq, wk, wv, cosq, sinq, dsm_in):
    t = proj.shape[0]
    tm = _row_tile(t)

    def body(dq_ref, dk_ref, dv_ref, cq_ref, ckv_ref, cqn_ref, ckvn_ref, qg_ref, kvg_ref, wq_ref, wk_ref, wv_ref,
             cos_ref, sin_ref, din_ref, dcq_ref, dckv_ref, dsm_ref, dwq_ref, dwk_ref, dwv_ref, dqg_ref, dkvg_ref):
        i = pl.program_id(0)

        @pl.when(i == 0)
        def _():
            for r in (dwq_ref, dwk_ref, dwv_ref, dqg_ref, dkvg_ref):
                r[...] = jnp.zeros_like(r)

        cs, sn = cos_ref[...], sin_ref[...]
        lane = lax.broadcasted_iota(jnp.int32, (1, BLK), 1)

        def unrope(dy):
            return dy * cs + _swap_rope(dy * sn)

        dqp = jnp.concatenate([unrope(dq_ref[:, BLK * h:BLK * (h + 1)]) for h in range(MLA_H)], axis=1).astype(BF16)
        dwq_ref[...] += _tn(cqn_ref[...], dqp)
        dcqn = _nt(dqp, wq_ref[...])
        r, xh = _rms(cq_ref[...], None)
        dqg_ref[...] += jnp.sum(dcqn * xh, axis=0, keepdims=True)
        dxh = dcqn * qg_ref[...]
        dcq_ref[...] = r * (dxh - xh * jnp.mean(dxh * xh, axis=1, keepdims=True))

        dkn, dkr = [], jnp.zeros((tm, BLK), F32)
        for h in range(MLA_H):
            blk = dk_ref[:, BLK * h:BLK * (h + 1)]
            dkn.append(jnp.where(lane < MLA_NOPE, blk, 0.0))
            dkr += jnp.where((lane >= SM_KR) & (lane < SM_KR + MLA_ROPE), blk, 0.0)
        dknb = jnp.concatenate(dkn, axis=1).astype(BF16)
        dvb = dv_ref[...].astype(BF16)
        ckvn = ckvn_ref[...]
        dwk_ref[...] += _tn(ckvn, dknb)
        dwv_ref[...] += _tn(ckvn, dvb)
        dckvn = _nt(dknb, wk_ref[...]) + _nt(dvb, wv_ref[...])
        r2, kh = _rms(ckv_ref[...], None)
        dkvg_ref[...] += jnp.sum(dckvn * kh, axis=0, keepdims=True)
        dkh = dckvn * kvg_ref[...]
        dckv_ref[...] = r2 * (dkh - kh * jnp.mean(dkh * kh, axis=1, keepdims=True))
        dsm_ref[...] = din_ref[...] + jnp.where((lane >= SM_KR) & (lane < SM_KR + MLA_ROPE), unrope(dkr), 0.0)

    def rows(w, cb):
        return pl.BlockSpec((tm, w), lambda i: (i, cb))

    def whole(a):
        return pl.BlockSpec(a.shape, lambda i: (0, 0))

    def wshape(a):
        return jax.ShapeDtypeStruct(a.shape, F32)

    return pl.pallas_call(
        body, name="mla_pre_bwd", grid=(t // tm,),
        in_specs=[rows(512, 0), rows(512, 0), rows(256, 0), rows(MLA_QL, C_CQ // MLA_QL), rows(MLA_KVL, C_CKV // MLA_KVL),
                  rows(MLA_QL, 0), rows(MLA_KVL, 0), whole(qg), whole(kvg), whole(wq), whole(wk), whole(wv),
                  rows(BLK, 0), rows(BLK, 0), rows(BLK, 0)],
        out_specs=[rows(MLA_QL, 0), rows(MLA_KVL, 0), rows(BLK, 0), whole(wq), whole(wk), whole(wv), whole(qg), whole(kvg)],
        out_shape=[jax.ShapeDtypeStruct((t, MLA_QL), F32), jax.ShapeDtypeStruct((t, MLA_KVL), F32),
                   jax.ShapeDtypeStruct((t, BLK), F32), wshape(wq), wshape(wk), wshape(wv), wshape(qg), wshape(kvg)],
        compiler_params=_cp("arbitrary"),
    )(dq, dk, dv, proj, proj, cqn, ckvn, qg, kvg, wq, wk, wv, cosq, sinq, dsm_in)


def _slot_sum(me, own, recv_ref):
    gg = own.astype(F32)
    for s in range(N_DEV):
        gg = gg + jnp.where(me == s, 0.0, recv_ref[s].astype(F32))
    return gg


def adamw(w, m, v, g=None, recv=None, own=None, me_arr=None):
    shape = w.shape
    c = shape[-1]
    from_recv = recv is not None
    if not from_recv:
        me_arr = jnp.zeros((1,), jnp.int32)
    nl = len(recv) if from_recv else 1
    rws = w.size // c // nl
    tr = rws
    for d in (1024, 512, 352, 256, 128, 64, 32, 16, 8):
        if rws % d == 0 and d * c * 4 <= (2 << 20):
            tr = d
            break
    nt = rws // tr
    w2, m2, v2 = (a.reshape(nl, rws, c) for a in (w, m, v))
    if from_recv:
        gin = [a.reshape(N_DEV, rws, c) for a in list(recv) + list(own)]
    else:
        gin = [g.reshape(1, rws, c)]

    def body(me_ref, w_ref, m_ref, v_ref, *rest):
        g_refs, outs = rest[:len(gin)], rest[len(gin):]
        if from_recv:
            g_out, outs = outs[0], outs[1:]
            for li in range(nl):
                @pl.when(pl.program_id(0) == li)
                def _(li=li):
                    g_out[...] = _slot_sum(me_ref[0], g_refs[nl + li][...], g_refs[li])
            gg = g_out[...]
        else:
            gg = g_refs[0][...]
        d_ref, nm_ref, nv_ref = outs
        nm = B1 * m_ref[...] + (1.0 - B1) * gg
        nv = B2 * v_ref[...] + (1.0 - B2) * (gg * gg)
        mh = nm / (1.0 - B1 ** STEP)
        vh = nv / (1.0 - B2 ** STEP)
        d_ref[...] = -LR * (mh / (jnp.sqrt(vh) + AEPS) + WD * w_ref[...])
        nm_ref[...] = nm
        nv_ref[...] = nv

    row = pl.BlockSpec((None, tr, c), lambda l, i, me: (l, i, 0))
    if from_recv:
        gspecs = [pl.BlockSpec((N_DEV, tr, c), lambda l, i, me, li=li: (0, jnp.where(l == li, i, 0), 0))
                  for li in range(nl)]
        gspecs += [pl.BlockSpec((None, tr, c), lambda l, i, me, li=li: (me[0], jnp.where(l == li, i, 0), 0))
                   for li in range(nl)]
    else:
        gspecs = [row]
    nout = 4 if from_recv else 3
    outs = pl.pallas_call(
        body, name="adamw",
        grid_spec=pltpu.PrefetchScalarGridSpec(num_scalar_prefetch=1, grid=(nl, nt), in_specs=[row, row, row] + gspecs,
                                               out_specs=[row] * nout),
        out_shape=[jax.ShapeDtypeStruct((nl, rws, c), F32)] * nout,
        compiler_params=_cp("arbitrary", "arbitrary"),
    )(me_arr, w2, m2, v2, *gin)
    return tuple(o.reshape(shape) for o in outs)


def sum_slots(recv, own=None, me_arr=None):
    _, r, c = recv.shape
    if own is None:
        own, me_arr = recv, jnp.zeros((1,), jnp.int32)
        plain = True
    else:
        plain = False

    def body(me_ref, r_ref, own_ref, o_ref):
        if plain:
            gg = r_ref[0].astype(F32)
            for s in range(1, N_DEV):
                gg = gg + r_ref[s].astype(F32)
            o_ref[...] = gg
        else:
            o_ref[...] = _slot_sum(me_ref[0], own_ref[...], r_ref)

    return pl.pallas_call(
        body, name="sum_slots",
        grid_spec=pltpu.PrefetchScalarGridSpec(
            num_scalar_prefetch=1, grid=(1,),
            in_specs=[pl.BlockSpec((N_DEV, r, c), lambda i, me: (0, 0, 0)),
                      pl.BlockSpec((None, r, c), lambda i, me: (me[0], 0, 0))],
            out_specs=pl.BlockSpec((r, c), lambda i, me: (0, 0))),
        out_shape=jax.ShapeDtypeStruct((r, c), F32),
        compiler_params=_cp("arbitrary"),
    )(me_arr, recv, own)


_FLIPS = [(0, 0, 1), (0, 1, 0), (0, 1, 1), (1, 0, 0), (1, 0, 1), (1, 1, 0), (1, 1, 1)]
_ANY = pl.BlockSpec(memory_space=pl.ANY)


def _mesh_place():
    x, y, c = lax.axis_index("x"), lax.axis_index("y"), lax.axis_index("c")
    me = 4 * x + 2 * y + c
    peers = [((x + fx) % 2, (y + fy) % 2, (c + fc) % 2) for fx, fy, fc in _FLIPS]
    return me, peers


def place_own(src, l, dtype, me_arr):
    _, r, c = src.shape
    tr = r
    for d in (512, 352, 256, 128, 64, 32, 16, 8):
        if r % d == 0 and d * c * 4 <= (2 << 20):
            tr = d
            break

    def body(me_ref, s_ref, o_ref):
        o_ref[...] = s_ref[...].astype(dtype)

    return pl.pallas_call(
        body, name="place_own",
        grid_spec=pltpu.PrefetchScalarGridSpec(
            num_scalar_prefetch=1, grid=(r // tr,),
            in_specs=[pl.BlockSpec((None, tr, c), lambda i, me: (l, i, 0))],
            out_specs=pl.BlockSpec((None, tr, c), lambda i, me: (me[0], i, 0))),
        out_shape=jax.ShapeDtypeStruct((N_DEV, r, c), dtype),
        compiler_params=_cp("arbitrary"),
    )(me_arr, src)


_HBM = pl.BlockSpec(memory_space=pltpu.HBM)
_SEMS = pl.BlockSpec(memory_space=pltpu.SEMAPHORE)
_EFFECT = pltpu.SideEffectType.DATAFLOW_SIDE_EFFECTING


def exchange_start(mode, arrays, name):
    n = len(arrays)
    gather = mode == "gather"
    ns = 0 if gather else n
    zones = list(arrays) if gather else [lax.empty(a.shape, a.dtype) for a in arrays]
    ops = ([] if gather else list(arrays)) + zones

    def body(*refs):
        srcs, lands = refs[:ns], refs[ns:ns + n]
        send_sems, recv_sems = refs[ns + n], refs[ns + n + 1]
        token = refs[-1]
        me, peers = _mesh_place()
        ids = [4 * p[0] + 2 * p[1] + p[2] for p in peers]
        for j in range(n):
            for k in range(N_DEV - 1):
                src = lands[j].at[me] if gather else srcs[j].at[ids[k]]
                pltpu.make_async_remote_copy(src_ref=src, dst_ref=lands[j].at[me],
                                             send_sem=send_sems.at[j * (N_DEV - 1) + k],
                                             recv_sem=recv_sems.at[j * (N_DEV - 1) + k], device_id=peers[k],
                                             device_id_type=pl.DeviceIdType.MESH).start()
        token[...] = jnp.zeros_like(token)

    nsem = n * (N_DEV - 1)
    res = pl.pallas_call(
        body, name=name,
        in_specs=[_HBM] * (ns + n),
        out_specs=(_SEMS, _SEMS, *[_HBM] * (ns + n), pl.BlockSpec(memory_space=pltpu.VMEM)),
        out_shape=(pltpu.SemaphoreType.DMA((nsem,)), pltpu.SemaphoreType.DMA((nsem,)),
                   *[pltpu.HBM(a.shape, a.dtype) for a in ops], jax.ShapeDtypeStruct((8, BLK), F32)),
        input_output_aliases={i: 2 + i for i in range(ns + n)},
        compiler_params=pltpu.CompilerParams(has_side_effects=_EFFECT),
    )(*[pltpu.with_memory_space_constraint(a, pltpu.HBM) for a in ops])
    return dict(gather=gather, send=res[0], recv=res[1], srcs=list(res[2:2 + ns]), lands=list(res[2 + ns:2 + ns + n]),
                token=res[-1])


def exchange_wait(hd, idxs, name, after):
    gather = hd["gather"]
    n = len(idxs)
    ns = 0 if gather else n
    ops = ([] if gather else [hd["srcs"][j] for j in idxs]) + [hd["lands"][j] for j in idxs]

    def body(*refs):
        srcs, lands = refs[:ns], refs[ns:ns + n]
        send_sems, recv_sems = refs[ns + n], refs[ns + n + 1]
        me, peers = _mesh_place()
        ids = [4 * p[0] + 2 * p[1] + p[2] for p in peers]
        for p, j in enumerate(idxs):
            for k in range(N_DEV - 1):
                src = lands[p].at[me] if gather else srcs[p].at[ids[k]]
                cp = pltpu.make_async_remote_copy(src_ref=src, dst_ref=lands[p].at[ids[k]],
                                                  send_sem=send_sems.at[j * (N_DEV - 1) + k],
                                                  recv_sem=recv_sems.at[j * (N_DEV - 1) + k], device_id=peers[k],
                                                  device_id_type=pl.DeviceIdType.MESH)
                cp.wait_send()
                cp.wait_recv()

    res = pl.pallas_call(
        body, name=name,
        in_specs=[_HBM] * (ns + n) + [_SEMS, _SEMS, _ANY],
        out_specs=[_HBM] * (ns + n),
        out_shape=[pltpu.HBM(a.shape, a.dtype) for a in ops],
        input_output_aliases={i: i for i in range(ns + n)},
        compiler_params=pltpu.CompilerParams(has_side_effects=_EFFECT),
    )(*ops, hd["send"], hd["recv"], after)
    return list(res[:ns]), list(res[ns:])


def _pad_cols(a, n):
    return jnp.pad(a, ((0, 0),) * (a.ndim - 1) + ((0, n - a.shape[-1]),))


def w_in_to_padded(w):
    z = lambda n: jnp.zeros(w.shape[:-1] + (n,), w.dtype)
    return jnp.concatenate([
        w[..., 0:1280], w[..., 1288:2056], w[..., 2060:2316], w[..., 2316:2444],
        w[..., 1280:1288], w[..., 2056:2060], z(SM_KR - SM_F - FOX_H), w[..., 2444:2476], z(BLK - SM_KR - MLA_ROPE)], axis=-1)


def w_in_from_padded(g):
    s = C_SM
    return jnp.concatenate([
        g[..., 0:1280], g[..., s + SM_DT:s + SM_DT + 8], g[..., 1280:2048], g[..., s + SM_F:s + SM_F + 4],
        g[..., 2048:2304], g[..., 2304:2432], g[..., s + SM_KR:s + SM_KR + MLA_ROPE]], axis=-1)


def _unshard_cols(gth):
    n, r, c = gth.shape
    return jnp.transpose(gth, (1, 0, 2)).reshape(r, n * c)


def _shard_cols(full):
    r, nc = full.shape
    return jnp.transpose(full.reshape(r, N_DEV, nc // N_DEV), (1, 0, 2))


def mla_weights(uq_g, ukv_g):
    uq = _unshard_cols(uq_g)
    dqh = MLA_NOPE + MLA_ROPE
    wq = jnp.concatenate([_pad_cols(uq[:, dqh * h:dqh * (h + 1)], BLK) for h in range(MLA_H)], axis=1)
    wk = jnp.concatenate([_pad_cols(ukv_g[2 * h], BLK) for h in range(MLA_H)], axis=1)
    wv = jnp.concatenate([ukv_g[2 * h + 1] for h in range(MLA_H)], axis=1)
    return wq, wk, wv


def mla_weight_grads(dwq, dwk, dwv):
    dqh = MLA_NOPE + MLA_ROPE
    duq = _shard_cols(jnp.concatenate([dwq[:, BLK * h:BLK * h + dqh] for h in range(MLA_H)], axis=1))
    parts = []
    for h in range(MLA_H):
        parts += [dwk[:, BLK * h:BLK * h + MLA_NOPE], dwv[:, MLA_V * h:MLA_V * (h + 1)]]
    return duq, jnp.stack(parts, axis=0)


def rope_tables(t):
    pos = (jnp.arange(t, dtype=jnp.int32) - PAD).astype(F32)
    inv_freq = 1.0 / (10000.0 ** (jnp.arange(0, MLA_ROPE, 2, dtype=F32) / MLA_ROPE))
    ang = pos[:, None] * inv_freq[None, :]
    cos, sin = jnp.cos(ang), jnp.sin(ang)
    one, zero = jnp.ones((t, SM_KR), F32), jnp.zeros((t, SM_KR), F32)
    tail = BLK - SM_KR - MLA_ROPE
    cosq = jnp.concatenate([one, cos, cos, jnp.ones((t, tail), F32)], axis=1)
    sinq = jnp.concatenate([zero, -sin, sin, jnp.zeros((t, tail), F32)], axis=1)
    return cosq, sinq


def _lanes(v, off=0):
    return jnp.pad(v.astype(F32), (off, BLK - off - v.shape[0]))[None, :]


def layer_fwd(x, ln, hb, getw, tabs):
    sv = {"h0b": hb}
    W = dict(getw("ffn1", hb))
    ln1 = (W["ln1_g"], W["ln1_b"])
    u, v, r1, h1b = ffn_fwd_seq(x, ln, W["g1"], W["u1"], W["d1"], ln1)
    sv.update(u1=u, v1=v, r1=r1, h1b=h1b)
    W.update(getw("mix", h1b))
    ln2 = (W["ln2_g"], W["ln2_b"])
    proj = mm_nn(h1b, W["w_in"])
    xa = conv_fwd(proj, W["conv_w"], W["conv_b"])
    y_ssd, sprev = ssd_fwd(xa, proj, W["dtb"], W["alog"], W["dskip"], W["normg"])
    c_col, c_row, c_keys = fox_pre(proj, W["fb"])
    y_fox, lse_f = attn_fwd(proj, proj, proj, C_FQ // 256, C_FK // 256, C_FV // 256, FOX_H, FOX_DH, FOX_DH,
                            FOX_DH ** -0.5, c_col, c_row, SM_F)
    q, k, vv, cqn, ckvn = mla_pre(proj, W["qg"], W["kvg"], W["wq"], W["wk"], W["wv"], *tabs)
    y_mla, lse_m = attn_fwd(q, k, vv, 0, 0, 0, MLA_H, BLK, MLA_V, (MLA_NOPE + MLA_ROPE) ** -0.5)
    mixcat = jnp.concatenate([y_ssd, y_fox, y_mla], axis=1)
    r2, h2b = mm_res_ln(mixcat, W["w_out"], r1, ln1, ln2)
    sv.update(proj=proj, xa=xa, sprev=sprev, c_keys=c_keys, c_row=c_row, lse_f=lse_f, q=q, k=k, v=vv, cqn=cqn, ckvn=ckvn,
              lse_m=lse_m, mixcat=mixcat, r2=r2, h2b=h2b)
    W.update(getw("ffn2", h2b))
    ln3 = (W["ln3_g"], W["ln3_b"])
    u, v, r3, h3b = ffn_fwd_seq(r2, ln2, W["g2"], W["u2"], W["d2"], ln3)
    sv.update(u2=u, v2=v, r3=r3, W=W)
    return r3, ln3, h3b, sv


def ffn_bwd(parts, r, gamma, hb_in, u, v, wg, wu, wd, after=None):
    dh, dwg, dwu, dwd, dg, db = ffn_bwd_seq(parts, r, gamma, hb_in, u, v, wg, wu, wd, after)
    return dh, dict(d=dwd, g=dwg, u=dwu, ln_g=dg, ln_b=db)


def layer_bwd(parts, sv, emit, tabs, after):
    G = {}
    W = sv["W"]
    dh2, g2 = ffn_bwd(parts, sv["r3"], W["ln3_g"], sv["h2b"], sv["u2"], sv["v2"], W["g2"], W["u2"], W["d2"], after)
    G.update(g2=g2["g"], u2=g2["u"], d2=g2["d"], ln3_g=g2["ln_g"], ln3_b=g2["ln_b"])
    tok = emit("ffn2", G)
    dr2, dmixb, G["ln2_g"], G["ln2_b"] = ln_bwd([(dh2, 1.0)], sv["r2"], W["ln2_g"], 1.0, tok)
    dmc = mm_nt_reduce([(dmixb[None], W["w_out"][None])], D)
    G["w_out"] = mm_tn(sv["mixcat"][None], dmixb[None])[0]
    proj = sv["proj"]
    dxa, dz, dsm, G["normg"], G["dskip"], G["alog"], G["dtb"] = ssd_bwd(
        dmc, sv["xa"], proj, sv["sprev"], W["dtb"], W["alog"], W["dskip"], W["normg"])
    dxbc, G["conv_w"], G["conv_b"] = conv_bwd(dxa, proj, W["conv_w"], W["conv_b"])
    dfq, dfk, dfv, dcq, dck = attn_bwd(proj, proj, proj, dmc, sv["lse_f"], sv["mixcat"], C_FQ // 256, C_FK // 256,
                                       C_FV // 256, 2, 2, FOX_H, FOX_DH, FOX_DH, FOX_DH ** -0.5, sv["c_keys"], sv["c_row"])
    dsm, G["fb"] = fox_pre_bwd(dcq, dck, proj, W["fb"], dsm)
    dq, dk, dv = attn_bwd(sv["q"], sv["k"], sv["v"], dmc, sv["lse_m"], sv["mixcat"], 0, 0, 0, 3, 3, MLA_H, BLK, MLA_V,
                          (MLA_NOPE + MLA_ROPE) ** -0.5)
    dcql, dckv, dsm, G["wq"], G["wk"], G["wv"], G["qg"], G["kvg"] = mla_pre_bwd(
        dq, dk, dv, proj, sv["cqn"], sv["ckvn"], W["qg"], W["kvg"], W["wq"], W["wk"], W["wv"], *tabs, dsm)
    dproj = jnp.concatenate([dz, dxbc, dfq, dfk, dfv, dcql, dckv, dsm], axis=1).astype(BF16)
    dh1p = mm_nt_reduce([(dproj[None], W["w_in"][None])], D)
    G["w_in"] = mm_tn(sv["h1b"][None], dproj[None])[0]
    tok = emit("mix", G)
    dh0, g1 = ffn_bwd([(dr2, ALPHA), (dh1p, 1.0)], sv["r1"], W["ln1_g"], sv["h0b"], sv["u1"], sv["v1"],
                      W["g1"], W["u1"], W["d1"], tok)
    G.update(g1=g1["g"], u1=g1["u"], d1=g1["d"], ln1_g=g1["ln_g"], ln1_b=g1["ln_b"])
    tok = emit("ffn1", G)
    return [(dh0, 1.0)], G, tok


def local_step(x, target, meta_full, getw, emit):
    t = x.shape[0] + BLK
    tabs = rope_tables(t)
    xr, hb = build_h0(meta_full, x)
    ln = None
    saved = []
    for l in range(NL):
        xr, ln, hb, sv = layer_fwd(xr, ln, hb, functools.partial(getw, l), tabs)
        saved.append(sv)
    dy, loss = loss_head(xr, ln, target)
    parts = [(dy, 1.0)]
    grads = [None] * NL
    tok = None
    for l in range(NL - 1, -1, -1):
        parts, grads[l], tok = layer_bwd(parts, saved[l], functools.partial(emit, l), tabs, tok)
    gx, gmeta = split_dh0(parts[0][0], tok)
    return loss, gx, gmeta, grads


_SMALL = ["ln1_g", "ln1_b", "ln2_g", "ln2_b", "ln3_g", "ln3_b", "conv_b", "ssd_norm_g", "mla_q_norm_g",
          "mla_kv_norm_g", "dt_bias", "a_log", "d_skip", "fox_f_b"]
_SMALL_ROWS = 16
_BIG = ["ffn1_w_gate", "ffn1_w_up", "ffn1_w_down", "w_in", "conv_w", "mla_w_uq", "mla_w_ukv", "w_out",
        "ffn2_w_gate", "ffn2_w_up", "ffn2_w_down"]
_NAMES = ["meta", "ffn1_w_gate", "ffn1_w_up", "ffn1_w_down", "ln1_g", "ln1_b", "w_in", "conv_w", "conv_b", "dt_bias",
          "a_log", "d_skip", "ssd_norm_g", "fox_f_b", "mla_q_norm_g", "mla_w_uq", "mla_kv_norm_g", "mla_w_ukv", "w_out",
          "ln2_g", "ln2_b", "ffn2_w_gate", "ffn2_w_up", "ffn2_w_down", "ln3_g", "ln3_b"]


def pack_small(p):
    rows = []
    for l in range(NL):
        for n in _SMALL:
            rows.append(_pad_cols(p[n][l][None, :].astype(F32), D))
        rows.append(jnp.zeros((_SMALL_ROWS - len(_SMALL), D), F32))
    return jnp.concatenate(rows, axis=0)


def unpack_small(a, like):
    out = {}
    for i, n in enumerate(_SMALL):
        out[n] = jnp.stack([a[l * _SMALL_ROWS + i, :like[n].shape[1]] for l in range(NL)], axis=0)
    return out


_STAGES = {"ffn1": ["ffn1_w_gate", "ffn1_w_up", "ffn1_w_down"],
           "mix": ["w_in", "conv_w", "mla_w_uq", "mla_w_ukv", "w_out"],
           "ffn2": ["ffn2_w_gate", "ffn2_w_up", "ffn2_w_down"]}


_FFN_T = ("ffn1_w_gate", "ffn1_w_up", "ffn2_w_gate", "ffn2_w_up")


def stage_weights(l, stage, g, rep):
    if stage != "mix":
        i = stage[3]
        return {"g" + i: g[f"ffn{i}_w_gate"].reshape(D_FF, D), "u" + i: g[f"ffn{i}_w_up"].reshape(D_FF, D),
                "d" + i: g[f"ffn{i}_w_down"].reshape(D_FF, D),
                "ln1_g" if i == "1" else "ln3_g": rep["ln1_g" if i == "1" else "ln3_g"][l][None, :],
                "ln1_b" if i == "1" else "ln3_b": rep["ln1_b" if i == "1" else "ln3_b"][l][None, :]}
    W = {}
    W["w_in"] = g["w_in"].reshape(D, N_INP)
    W["w_out"] = g["w_out"].reshape(D, D)
    W["wq"], W["wk"], W["wv"] = mla_weights(g["mla_w_uq"], g["mla_w_ukv"])
    W["conv_w"] = _unshard_cols(g["conv_w"])
    for k in ("ln2_g", "ln2_b", "conv_b"):
        W[k] = rep[k][l][None, :]
    W["normg"] = rep["ssd_norm_g"][l][None, :]
    W["qg"] = rep["mla_q_norm_g"][l][None, :]
    W["kvg"] = rep["mla_kv_norm_g"][l][None, :]
    W["dtb"] = _lanes(rep["dt_bias"][l], SM_DT)
    W["alog"] = _lanes(rep["a_log"][l], SM_DT)
    W["dskip"] = _lanes(rep["d_skip"][l], SM_DT)
    W["fb"] = _lanes(rep["fox_f_b"][l], SM_F)
    return W


def small_grads(G):
    return {"ln1_g": G["ln1_g"][0], "ln1_b": G["ln1_b"][0], "ln2_g": G["ln2_g"][0], "ln2_b": G["ln2_b"][0],
            "ln3_g": G["ln3_g"][0], "ln3_b": G["ln3_b"][0], "conv_b": G["conv_b"][0], "ssd_norm_g": G["normg"][0],
            "mla_q_norm_g": G["qg"][0], "mla_kv_norm_g": G["kvg"][0], "dt_bias": G["dtb"][0, :SSD_H],
            "a_log": G["alog"][0, :SSD_H], "d_skip": G["dskip"][0, :SSD_H], "fox_f_b": G["fb"][0, SM_F:SM_F + FOX_H]}


def big_grads(G, stage):
    if stage != "mix":
        i = stage[-1]
        return {f"ffn{i}_w_{k}": G[k[0] + i].reshape(N_DEV, HS, D) for k in ("gate", "up", "down")}
    duq, dukv = mla_weight_grads(G["wq"], G["wk"], G["wv"])
    return {"w_in": G["w_in"].reshape(N_DEV, D // N_DEV, N_INP), "w_out": G["w_out"].reshape(N_DEV, D // N_DEV, D),
            "mla_w_uq": duq, "mla_w_ukv": dukv, "conv_w": _shard_cols(G["conv_w"])}


def kernel(x, meta, ffn1_w_gate, ffn1_w_up, ffn1_w_down, ln1_g, ln1_b, w_in, conv_w, conv_b, dt_bias, a_log, d_skip, ssd_norm_g, fox_f_b, mla_q_norm_g, mla_w_uq, mla_kv_norm_g, mla_w_ukv, w_out, ln2_g, ln2_b, ffn2_w_gate, ffn2_w_up, ffn2_w_down, ln3_g, ln3_b, loss_target, m_meta, m_ffn1_w_gate, m_ffn1_w_up, m_ffn1_w_down, m_ln1_g, m_ln1_b, m_w_in, m_conv_w, m_conv_b, m_dt_bias, m_a_log, m_d_skip, m_ssd_norm_g, m_fox_f_b, m_mla_q_norm_g, m_mla_w_uq, m_mla_kv_norm_g, m_mla_w_ukv, m_w_out, m_ln2_g, m_ln2_b, m_ffn2_w_gate, m_ffn2_w_up, m_ffn2_w_down, m_ln3_g, m_ln3_b, v_meta, v_ffn1_w_gate, v_ffn1_w_up, v_ffn1_w_down, v_ln1_g, v_ln1_b, v_w_in, v_conv_w, v_conv_b, v_dt_bias, v_a_log, v_d_skip, v_ssd_norm_g, v_fox_f_b, v_mla_q_norm_g, v_mla_w_uq, v_mla_kv_norm_g, v_mla_w_ukv, v_w_out, v_ln2_g, v_ln2_b, v_ffn2_w_gate, v_ffn2_w_up, v_ffn2_w_down, v_ln3_g, v_ln3_b):
    vals = (meta, ffn1_w_gate, ffn1_w_up, ffn1_w_down, ln1_g, ln1_b, w_in, conv_w, conv_b, dt_bias, a_log, d_skip, ssd_norm_g, fox_f_b, mla_q_norm_g, mla_w_uq, mla_kv_norm_g, mla_w_ukv, w_out, ln2_g, ln2_b, ffn2_w_gate, ffn2_w_up, ffn2_w_down, ln3_g, ln3_b)
    moms = (m_meta, m_ffn1_w_gate, m_ffn1_w_up, m_ffn1_w_down, m_ln1_g, m_ln1_b, m_w_in, m_conv_w, m_conv_b, m_dt_bias, m_a_log, m_d_skip, m_ssd_norm_g, m_fox_f_b, m_mla_q_norm_g, m_mla_w_uq, m_mla_kv_norm_g, m_mla_w_ukv, m_w_out, m_ln2_g, m_ln2_b, m_ffn2_w_gate, m_ffn2_w_up, m_ffn2_w_down, m_ln3_g, m_ln3_b)
    vars_ = (v_meta, v_ffn1_w_gate, v_ffn1_w_up, v_ffn1_w_down, v_ln1_g, v_ln1_b, v_w_in, v_conv_w, v_conv_b, v_dt_bias, v_a_log, v_d_skip, v_ssd_norm_g, v_fox_f_b, v_mla_q_norm_g, v_mla_w_uq, v_mla_kv_norm_g, v_mla_w_ukv, v_w_out, v_ln2_g, v_ln2_b, v_ffn2_w_gate, v_ffn2_w_up, v_ffn2_w_down, v_ln3_g, v_ln3_b)
    P = dict(zip(_NAMES, vals))
    M = dict(zip(_NAMES, moms))
    V = dict(zip(_NAMES, vars_))
    me = 4 * lax.axis_index("x") + 2 * lax.axis_index("y") + lax.axis_index("c")

    me_arr = me.astype(jnp.int32).reshape(1)
    for n in _FFN_T:
        P[n], M[n], V[n] = (jnp.swapaxes(a[n], 1, 2) for a in (P, M, V))
    src = dict(P)
    src["w_in"] = w_in_to_padded(P["w_in"])
    order = [("meta", 0)] + [(n, l) for l in range(NL) for names in _STAGES.values() for n in names]
    zone_of = {nl_: i for i, nl_ in enumerate(order)}
    zones = [place_own(P["meta"][None], 0, F32, me_arr)]
    zones += [place_own(src[n], l, F32 if n == "conv_w" else BF16, me_arr) for n, l in order[1:]]
    hg = exchange_start("gather", zones, "gather_start")
    meta_full = _unshard_cols(exchange_wait(hg, [0], "gather_wait_meta", hg["token"])[1][0])

    def getw(l, stage, after):
        names = _STAGES[stage]
        lands = exchange_wait(hg, [zone_of[(n, l)] for n in names], f"gather_wait_{l}_{stage}", after)[1]
        return stage_weights(l, stage, dict(zip(names, lands)), P)

    sent = {}

    def emit(l, stage, G):
        bg = big_grads(G, stage)
        sent[(l, stage)] = exchange_start("scatter", [bg[n] for n in _STAGES[stage]], f"scatter_start_{l}_{stage}")
        return sent[(l, stage)]["token"]

    loss, gx, gmeta, grads = local_step(x[0], loss_target[0], meta_full, getw, emit)

    small = jnp.concatenate([pack_small({n: jnp.stack([small_grads(g)[n] for g in grads]) for n in _SMALL}), gmeta], axis=0)
    hs = exchange_start("gather", [place_own(small[None], 0, F32, me_arr)], "small_start")

    out = {}
    after = hs["token"]
    for stage in ("ffn2", "mix", "ffn1"):
        names = _STAGES[stage]
        got = [exchange_wait(sent[(l, stage)], list(range(len(names))), f"scatter_wait_{l}_{stage}", after)
               for l in range(NL - 1, -1, -1)][::-1]
        for i, n in enumerate(names):
            own = [got[l][0][i] for l in range(NL)]
            recv = [got[l][1][i] for l in range(NL)]
            if n == "w_in":
                g = jnp.stack([w_in_from_padded(sum_slots(recv[l], own[l], me_arr)) for l in range(NL)])
                out[n] = (g,) + adamw(P[n], M[n], V[n], g=g)
            else:
                out[n] = adamw(P[n], M[n], V[n], recv=recv, own=own, me_arr=me_arr)
                if n in _FFN_T:
                    out[n] = tuple(jnp.swapaxes(a, 1, 2) for a in out[n])
        after = out[names[-1]][1]
    gsmall = sum_slots(exchange_wait(hs, [0], "small_wait", after)[1][0])
    gm = lax.dynamic_slice(gsmall[NL * _SMALL_ROWS:], (0, me * (D // N_DEV)), (N_META, D // N_DEV))
    out["meta"] = (gm,) + adamw(P["meta"], M["meta"], V["meta"], g=gm)
    gs = gsmall[:NL * _SMALL_ROWS]
    sd, sm_, sv_ = adamw(pack_small(P), pack_small(M), pack_small(V), g=gs)
    ups = [unpack_small(a, P) for a in (gs, sd, sm_, sv_)]
    for n in _SMALL:
        out[n] = tuple(u[n] for u in ups)

    loss_all = lax.psum(loss[0, 0], ("x", "y", "c"))
    flat = [loss_all, gx[None]]
    for k in range(4):
        flat += [out[n][k] for n in _NAMES]
    return tuple(flat)
```

```python
import functools

import jax
import jax.numpy as jnp
from jax import lax
from jax.experimental import pallas as pl
from jax.experimental.pallas import tpu as pltpu

F32, BF16 = jnp.float32, jnp.bfloat16
HI = lax.Precision.HIGHEST

N_DEV = 8
D = 1024
NL = 2
N_META = 16
BLK = 128
PAD = BLK - N_META
D_FF = 2816
HS = D_FF // N_DEV
SSD_H, SSD_P, SSD_N, SSD_G = 8, 64, 64, 2
SSD_D = SSD_H * SSD_P
CONV_K = 4
CONV_D = SSD_D + 2 * SSD_G * SSD_N
FOX_H, FOX_DH = 4, 64
MLA_H, MLA_QL, MLA_KVL, MLA_NOPE, MLA_ROPE, MLA_V = 4, 256, 128, 64, 32, 64
N_IN = 2476
C_Z, C_XBC, C_FQ, C_FK, C_FV, C_CQ, C_CKV, C_SM, N_INP = 0, 512, 1280, 1536, 1792, 2048, 2304, 2432, 2560
SM_DT, SM_F, SM_KR = 0, 8, 64
ALPHA = (2 * NL) ** 0.25
EPS = 1e-5
NEG = -1e30
LR, B1, B2, AEPS, WD, STEP = 0.001, 0.9, 0.999, 1e-08, 0.01, 10
VMEM_MB = 56


def _cp(*sem):
    return pltpu.CompilerParams(dimension_semantics=sem, vmem_limit_bytes=VMEM_MB << 20)


def _nn(a, b):
    return lax.dot_general(a, b, (((1,), (0,)), ((), ())), preferred_element_type=F32)


def _nt(a, b):
    return lax.dot_general(a, b, (((1,), (1,)), ((), ())), preferred_element_type=F32)


def _tn(a, b):
    return lax.dot_general(a, b, (((0,), (0,)), ((), ())), preferred_element_type=F32)


def _nn_hi(a, b):
    return lax.dot_general(a, b, (((1,), (0,)), ((), ())), precision=HI, preferred_element_type=F32)


def _row_tile(t):
    for d in range(640, 15, -16):
        if t % d == 0:
            return d
    raise ValueError(t)


def _sig(x):
    return 1.0 / (1.0 + jnp.exp(-x))


def _tri(lower=True):
    r = lax.broadcasted_iota(jnp.int32, (BLK, BLK), 0)
    c = lax.broadcasted_iota(jnp.int32, (BLK, BLK), 1)
    return (r >= c) if lower else (r <= c)


def build_h0(meta_full, x):
    s = x.shape[0]
    nb = s // BLK + 1

    def body(m_ref, x_ref, h_ref, hb_ref):
        i = pl.program_id(0)

        @pl.when(i == 0)
        def _():
            h = jnp.concatenate([jnp.zeros((PAD, D), F32), m_ref[...]], axis=0)
            h_ref[...] = h
            hb_ref[...] = h.astype(BF16)

        @pl.when(i > 0)
        def _():
            h_ref[...] = x_ref[...]
            hb_ref[...] = x_ref[...].astype(BF16)

    return pl.pallas_call(
        body, name="build_h0", grid=(nb,),
        in_specs=[pl.BlockSpec((N_META, D), lambda i: (0, 0)),
                  pl.BlockSpec((BLK, D), lambda i: (jnp.maximum(i - 1, 0), 0))],
        out_specs=[pl.BlockSpec((BLK, D), lambda i: (i, 0))] * 2,
        out_shape=[jax.ShapeDtypeStruct((nb * BLK, D), F32), jax.ShapeDtypeStruct((nb * BLK, D), BF16)],
        compiler_params=_cp("arbitrary"),
    )(meta_full, x)


FT = 256


def _layer_norm(r, gamma, beta):
    mu = jnp.mean(r, axis=1, keepdims=True)
    xc = r - mu
    var = jnp.mean(xc * xc, axis=1, keepdims=True)
    return xc * lax.rsqrt(var + EPS) * gamma + beta


def ffn_fwd(hb, res, wg, wu, wd, gamma, beta):
    t = hb.shape[0]
    f = wg.shape[0]
    tm = _row_tile(t)
    nj = f // FT

    def body(h_ref, res_ref, wg_ref, wu_ref, wd_ref, g_ref, be_ref, u_ref, v_ref, r_ref, y_ref, yb_ref, acc, us, vs):
        j = pl.program_id(1)

        def up():
            h = h_ref[...]
            u = _nt(h, wg_ref[...])
            v = _nt(h, wu_ref[...])
            u_ref[...] = u.astype(BF16)
            v_ref[...] = v.astype(BF16)
            return u, v

        def down():
            u, v = us[...], vs[...]
            return _nn((u * _sig(u) * v).astype(BF16), wd_ref[...])

        @pl.when(j == 0)
        def _():
            us[...], vs[...] = up()
            acc[...] = jnp.zeros_like(acc)

        @pl.when((j > 0) & (j < nj))
        def _():
            d = down()
            u, v = up()
            acc[...] += d
            us[...] = u
            vs[...] = v

        @pl.when(j == nj)
        def _():
            r = ALPHA * res_ref[...] + 0.5 * (acc[...] + down())
            y = _layer_norm(r, g_ref[...], be_ref[...])
            r_ref[...] = r
            y_ref[...] = y
            yb_ref[...] = y.astype(BF16)

    row = pl.BlockSpec((tm, D), lambda i, j: (i, 0))
    vec = pl.BlockSpec((1, D), lambda i, j: (0, 0))
    wup = pl.BlockSpec((FT, D), lambda i, j: (jnp.minimum(j, nj - 1), 0))
    wdn = pl.BlockSpec((FT, D), lambda i, j: (jnp.maximum(j - 1, 0), 0))
    act = pl.BlockSpec((tm, FT), lambda i, j: (i, jnp.minimum(j, nj - 1)))
    return pl.pallas_call(
        body, name="ffn_fwd", grid=(t // tm, nj + 1),
        in_specs=[row, row, wup, wup, wdn, vec, vec],
        out_specs=[act, act, row, row, row],
        out_shape=[jax.ShapeDtypeStruct((t, f), BF16), jax.ShapeDtypeStruct((t, f), BF16),
                   jax.ShapeDtypeStruct((t, D), F32), jax.ShapeDtypeStruct((t, D), F32),
                   jax.ShapeDtypeStruct((t, D), BF16)],
        scratch_shapes=[pltpu.VMEM((tm, D), F32), pltpu.VMEM((tm, FT), F32), pltpu.VMEM((tm, FT), F32)],
        compiler_params=_cp("arbitrary", "arbitrary"),
    )(hb, res, wg, wu, wd, gamma, beta)


def ffn_bwd_act(dfb, u, v, wg, wu, wd):
    t, f = u.shape
    tm = _row_tile(t)

    nj = f // FT

    def body(df_ref, u_ref, v_ref, wg_ref, wu_ref, wd_ref, du_ref, dv_ref, dh_ref, das):
        j = pl.program_id(1)

        def first():
            return _nt(df_ref[...], wd_ref[...])

        def second():
            da = das[...]
            uu = u_ref[...].astype(F32)
            sg = _sig(uu)
            du = (da * v_ref[...].astype(F32) * (sg * (1.0 + uu * (1.0 - sg)))).astype(BF16)
            dv = (da * uu * sg).astype(BF16)
            du_ref[...] = du
            dv_ref[...] = dv
            return _nn(du, wg_ref[...]) + _nn(dv, wu_ref[...])

        @pl.when(j == 0)
        def _():
            das[...] = first()
            dh_ref[...] = jnp.zeros_like(dh_ref)

        @pl.when((j > 0) & (j < nj))
        def _():
            tot = second()
            da = first()
            dh_ref[...] += tot
            das[...] = da

        @pl.when(j == nj)
        def _():
            dh_ref[...] += second()

    row = pl.BlockSpec((tm, D), lambda i, j: (i, 0))
    wfirst = pl.BlockSpec((FT, D), lambda i, j: (jnp.minimum(j, nj - 1), 0))
    wsecond = pl.BlockSpec((FT, D), lambda i, j: (jnp.maximum(j - 1, 0), 0))
    act = pl.BlockSpec((tm, FT), lambda i, j: (i, jnp.maximum(j - 1, 0)))
    return pl.pallas_call(
        body, name="ffn_bwd_act", grid=(t // tm, nj + 1),
        in_specs=[row, act, act, wsecond, wsecond, wfirst],
        out_specs=[act, act, row],
        out_shape=[jax.ShapeDtypeStruct((t, f), BF16), jax.ShapeDtypeStruct((t, f), BF16),
                   jax.ShapeDtypeStruct((t, D), F32)],
        scratch_shapes=[pltpu.VMEM((tm, FT), F32)],
        compiler_params=_cp("arbitrary", "arbitrary"),
    )(dfb, u, v, wg, wu, wd)


def ffn_fwd_seq(x, ln_in, wg, wu, wd, ln_out):
    t = x.shape[0]
    f = wg.shape[0]
    nj, nr = f // FT, t // _row_tile(t)
    rc = t // nr
    plain = ln_in is None
    gi, bi = ln_out if plain else ln_in

    def body(x_hbm, gi_ref, bi_ref, go_ref, bo_ref, wg_ref, wu_ref, wd_ref, u_ref, v_ref, r_hbm, yb_hbm,
             acc, hbs, xbuf, sem_in, sem_out):
        j = pl.program_id(0)

        @pl.when(j == 0)
        def _():
            def fetch(k):
                return pltpu.make_async_copy(x_hbm.at[pl.ds(k * rc, rc)], xbuf.at[k % 2], sem_in.at[k % 2])

            fetch(0).start()
            for k in range(nr):
                if k + 1 < nr:
                    fetch(k + 1).start()
                fetch(k).wait()
                h = xbuf[k % 2]
                if not plain:
                    h = _layer_norm(h, gi_ref[...], bi_ref[...])
                acc[k * rc:(k + 1) * rc, :] = ALPHA * h
                hbs[k * rc:(k + 1) * rc, :] = h.astype(BF16)

        for k in range(nr):
            sl = slice(k * rc, (k + 1) * rc)
            h = hbs[sl, :]
            u = _nt(h, wg_ref[...])
            v = _nt(h, wu_ref[...])
            u_ref[sl, :] = u.astype(BF16)
            v_ref[sl, :] = v.astype(BF16)
            acc[sl, :] += _nn((0.5 * u * _sig(u) * v).astype(BF16), wd_ref[...])

        @pl.when(j == nj - 1)
        def _():
            r_cp = pltpu.make_async_copy(acc, r_hbm, sem_out.at[0])
            r_cp.start()
            for k in range(nr):
                sl = slice(k * rc, (k + 1) * rc)
                hbs[sl, :] = _layer_norm(acc[sl, :], go_ref[...], bo_ref[...]).astype(BF16)
            y_cp = pltpu.make_async_copy(hbs, yb_hbm, sem_out.at[1])
            y_cp.start()
            r_cp.wait()
            y_cp.wait()

    vec = pl.BlockSpec((1, D), lambda j: (0, 0))
    wsp = pl.BlockSpec((FT, D), lambda j: (j, 0))
    act = pl.BlockSpec((None, t, FT), lambda j: (j, 0, 0))
    return pl.pallas_call(
        body, name="ffn_fwd_seq", grid=(nj,),
        in_specs=[_ANY, vec, vec, vec, vec, wsp, wsp, wsp],
        out_specs=[act, act, _ANY, _ANY],
        out_shape=[jax.ShapeDtypeStruct((nj, t, FT), BF16), jax.ShapeDtypeStruct((nj, t, FT), BF16),
                   jax.ShapeDtypeStruct((t, D), F32), jax.ShapeDtypeStruct((t, D), BF16)],
        scratch_shapes=[pltpu.VMEM((t, D), F32), pltpu.VMEM((t, D), BF16), pltpu.VMEM((2, rc, D), F32),
                        pltpu.SemaphoreType.DMA((2,)), pltpu.SemaphoreType.DMA((2,))],
        compiler_params=_cp("arbitrary"),
    )(x, gi, bi, ln_out[0], ln_out[1], wg, wu, wd)


def ffn_bwd_seq(parts, r, gamma, hb, u, v, wg, wu, wd, after=None):
    nj, t, _ = u.shape
    f = nj * FT
    nr = t // _row_tile(t)
    rc = t // nr
    nc = t // BLK
    scales = [s for _, s in parts]
    npart = len(parts)
    extra = [] if after is None else [after]

    def body(*refs):
        refs = refs[len(extra):]
        p_hbm, refs = refs[:npart], refs[npart:]
        (r_hbm, g_ref, hb_hbm, u_ref, v_ref, wg_ref, wu_ref, wd_ref, dh_hbm, dwg_ref, dwu_ref, dwd_ref, dg_ref, db_ref,
         dfs, hbt, dft, dhacc, dus, dvs, acs, pbuf, rbuf, hbuf, sems, sem_out) = refs
        j = pl.program_id(0)

        @pl.when(j == 0)
        def _():
            def fetch(c):
                rows = pl.ds(c * BLK, BLK)
                cps = [pltpu.make_async_copy(p_hbm[p].at[rows], pbuf.at[c % 2, p], sems.at[c % 2, p]) for p in range(npart)]
                cps.append(pltpu.make_async_copy(r_hbm.at[rows], rbuf.at[c % 2], sems.at[c % 2, npart]))
                cps.append(pltpu.make_async_copy(hb_hbm.at[rows], hbuf.at[c % 2], sems.at[c % 2, npart + 1]))
                return cps

            for cp in fetch(0):
                cp.start()
            dg = jnp.zeros((1, D), F32)
            db = jnp.zeros((1, D), F32)
            for c in range(nc):
                if c + 1 < nc:
                    for cp in fetch(c + 1):
                        cp.start()
                for cp in fetch(c):
                    cp.wait()
                sl = slice(c * BLK, (c + 1) * BLK)
                dy = scales[0] * pbuf[c % 2, 0]
                for p in range(1, npart):
                    dy += scales[p] * pbuf[c % 2, p]
                rr = rbuf[c % 2]
                xc = rr - jnp.mean(rr, axis=1, keepdims=True)
                rstd = lax.rsqrt(jnp.mean(xc * xc, axis=1, keepdims=True) + EPS)
                xh = xc * rstd
                dxh = dy * g_ref[...]
                dr = rstd * (dxh - jnp.mean(dxh, axis=1, keepdims=True) - xh * jnp.mean(dxh * xh, axis=1, keepdims=True))
                dg += jnp.sum(dy * xh, axis=0, keepdims=True)
                db += jnp.sum(dy, axis=0, keepdims=True)
                dhacc[sl, :] = ALPHA * dr
                dfc = (0.5 * dr).astype(BF16)
                dfs[sl, :] = dfc
                dft[:, sl] = dfc.T
                hbt[:, sl] = hbuf[c % 2].T
            dg_ref[...] = dg
            db_ref[...] = db

        for k in range(nr):
            sl = slice(k * rc, (k + 1) * rc)
            da = _nt(dfs[sl, :], wd_ref[...])
            uu = u_ref[sl, :].astype(F32)
            vv = v_ref[sl, :].astype(F32)
            sg = _sig(uu)
            du = (da * vv * (sg * (1.0 + uu * (1.0 - sg)))).astype(BF16)
            dv = (da * uu * sg).astype(BF16)
            dus[sl, :] = du
            dvs[sl, :] = dv
            acs[sl, :] = (uu * sg * vv).astype(BF16)
            dhacc[sl, :] += _nn(du, wg_ref[...]) + _nn(dv, wu_ref[...])
        dwg_ref[...] = _nn(hbt[...], dus[...]).T.astype(BF16)
        dwu_ref[...] = _nn(hbt[...], dvs[...]).T.astype(BF16)
        dwd_ref[...] = _nn(dft[...], acs[...]).T.astype(BF16)

        @pl.when(j == nj - 1)
        def _():
            cp = pltpu.make_async_copy(dhacc, dh_hbm, sem_out.at[0])
            cp.start()
            cp.wait()

    vec = pl.BlockSpec((1, D), lambda j: (0, 0))
    wsp = pl.BlockSpec((FT, D), lambda j: (j, 0))
    act = pl.BlockSpec((None, t, FT), lambda j: (j, 0, 0))
    return pl.pallas_call(
        body, name="ffn_bwd_seq", grid=(nj,),
        in_specs=[_ANY] * (len(extra) + npart + 1) + [vec, _ANY, act, act, wsp, wsp, wsp],
        out_specs=[_ANY, wsp, wsp, wsp, vec, vec],
        out_shape=[jax.ShapeDtypeStruct((t, D), F32)] + [jax.ShapeDtypeStruct((f, D), BF16)] * 3
        + [jax.ShapeDtypeStruct((1, D), F32)] * 2,
        scratch_shapes=[pltpu.VMEM((t, D), BF16), pltpu.VMEM((D, t), BF16), pltpu.VMEM((D, t), BF16),
                        pltpu.VMEM((t, D), F32), pltpu.VMEM((t, FT), BF16), pltpu.VMEM((t, FT), BF16),
                        pltpu.VMEM((t, FT), BF16), pltpu.VMEM((2, npart, BLK, D), F32), pltpu.VMEM((2, BLK, D), F32),
                        pltpu.VMEM((2, BLK, D), BF16), pltpu.SemaphoreType.DMA((2, npart + 2)),
                        pltpu.SemaphoreType.DMA((1,))],
        compiler_params=_cp("arbitrary"),
    )(*extra, *[p for p, _ in parts], r, gamma, hb, u, v, wg, wu, wd)


def mm_res_ln(a, b, x, ln_in, ln_out):
    t, k = a.shape
    tm = _row_tile(t)

    def body(a_ref, b_ref, x_ref, gi_ref, bi_ref, go_ref, bo_ref, r_ref, yb_ref):
        r = ALPHA * _layer_norm(x_ref[...], gi_ref[...], bi_ref[...]) + _nn(a_ref[...], b_ref[...])
        r_ref[...] = r
        yb_ref[...] = _layer_norm(r, go_ref[...], bo_ref[...]).astype(BF16)

    row = pl.BlockSpec((tm, D), lambda i: (i, 0))
    vec = pl.BlockSpec((1, D), lambda i: (0, 0))
    return pl.pallas_call(
        body, name="mm_res_ln", grid=(t // tm,),
        in_specs=[pl.BlockSpec((tm, k), lambda i: (i, 0)), pl.BlockSpec((k, D), lambda i: (0, 0)), row, vec, vec, vec, vec],
        out_specs=[row, row],
        out_shape=[jax.ShapeDtypeStruct((t, D), F32), jax.ShapeDtypeStruct((t, D), BF16)],
        compiler_params=_cp("arbitrary"),
    )(a, b, x, ln_in[0], ln_in[1], ln_out[0], ln_out[1])


def mm_nn(a, b, tn=512):
    t, k = a.shape
    n = b.shape[1]
    tm = _row_tile(t)

    def body(a_ref, b_ref, o_ref):
        o_ref[...] = _nn(a_ref[...], b_ref[...])

    return pl.pallas_call(
        body, name="mm_nn", grid=(n // tn, t // tm),
        in_specs=[pl.BlockSpec((tm, k), lambda j, i: (i, 0)), pl.BlockSpec((k, tn), lambda j, i: (0, j))],
        out_specs=pl.BlockSpec((tm, tn), lambda j, i: (i, j)),
        out_shape=jax.ShapeDtypeStruct((t, n), F32),
        compiler_params=_cp("arbitrary", "arbitrary"),
    )(a, b)


def mm_nt_reduce(pairs, n):
    g, t, _ = pairs[0][0].shape
    tm = _row_tile(t)
    npair = len(pairs)

    def body(*refs):
        o_ref = refs[-1]
        gi = pl.program_id(1)
        tot = _nt(refs[0][...], refs[1][...])
        for p in range(1, npair):
            tot += _nt(refs[2 * p][...], refs[2 * p + 1][...])

        @pl.when(gi == 0)
        def _():
            o_ref[...] = tot

        @pl.when(gi > 0)
        def _():
            o_ref[...] += tot

    in_specs, args = [], []
    for x, w in pairs:
        k = x.shape[2]
        in_specs += [pl.BlockSpec((None, tm, k), lambda i, gi: (gi, i, 0)),
                     pl.BlockSpec((None, n, k), lambda i, gi: (gi, 0, 0))]
        args += [x, w]
    return pl.pallas_call(
        body, name="mm_nt_reduce", grid=(t // tm, g),
        in_specs=in_specs, out_specs=pl.BlockSpec((tm, n), lambda i, gi: (i, 0)),
        out_shape=jax.ShapeDtypeStruct((t, n), F32),
        compiler_params=_cp("arbitrary", "arbitrary"),
    )(*args)


def mm_tn(x, y, out_dtype=BF16):
    gx, t, k = x.shape
    gy, _, n = y.shape
    g = max(gx, gy)
    tm = _row_tile(t)
    nt = t // tm

    def body(x_ref, y_ref, o_ref, acc):
        i = pl.program_id(1)

        @pl.when(i == 0)
        def _():
            acc[...] = jnp.zeros_like(acc)

        acc[...] += _tn(x_ref[...], y_ref[...])

        @pl.when(i == nt - 1)
        def _():
            o_ref[...] = acc[...].astype(out_dtype)

    return pl.pallas_call(
        body, name="mm_tn", grid=(g, nt),
        in_specs=[pl.BlockSpec((None, tm, k), (lambda gi, i: (gi, i, 0)) if gx > 1 else (lambda gi, i: (0, i, 0))),
                  pl.BlockSpec((None, tm, n), (lambda gi, i: (gi, i, 0)) if gy > 1 else (lambda gi, i: (0, i, 0)))],
        out_specs=pl.BlockSpec((None, k, n), lambda gi, i: (gi, 0, 0)),
        out_shape=jax.ShapeDtypeStruct((g, k, n), out_dtype),
        scratch_shapes=[pltpu.VMEM((k, n), F32)],
        compiler_params=_cp("arbitrary", "arbitrary"),
    )(x, y)


def ln_bwd(parts, r, gamma, out_scale, after=None):
    t = r.shape[0]
    tm = _row_tile(t)
    scales = [s for _, s in parts]
    npart = len(parts)
    extra = [] if after is None else [after]

    def body(*refs):
        refs = refs[len(extra):]
        r_ref, g_ref = refs[npart], refs[npart + 1]
        dr_ref, drb_ref, dg_ref, db_ref = refs[npart + 2:]
        i = pl.program_id(0)
        dy = scales[0] * refs[0][...]
        for p in range(1, npart):
            dy += scales[p] * refs[p][...]
        rr = r_ref[...]
        mu = jnp.mean(rr, axis=1, keepdims=True)
        xc = rr - mu
        rstd = lax.rsqrt(jnp.mean(xc * xc, axis=1, keepdims=True) + EPS)
        xh = xc * rstd
        dxh = dy * g_ref[...]
        m1 = jnp.mean(dxh, axis=1, keepdims=True)
        m2 = jnp.mean(dxh * xh, axis=1, keepdims=True)
        dr = rstd * (dxh - m1 - xh * m2)
        dr_ref[...] = dr
        drb_ref[...] = (out_scale * dr).astype(BF16)
        dg = jnp.sum(dy * xh, axis=0, keepdims=True)
        db = jnp.sum(dy, axis=0, keepdims=True)

        @pl.when(i == 0)
        def _():
            dg_ref[...] = dg
            db_ref[...] = db

        @pl.when(i > 0)
        def _():
            dg_ref[...] += dg
            db_ref[...] += db

    row = pl.BlockSpec((tm, D), lambda i: (i, 0))
    vec = pl.BlockSpec((1, D), lambda i: (0, 0))
    return pl.pallas_call(
        body, name="ln_bwd", grid=(t // tm,),
        in_specs=[_ANY] * len(extra) + [row] * (npart + 1) + [vec],
        out_specs=[row, row, vec, vec],
        out_shape=[jax.ShapeDtypeStruct((t, D), F32), jax.ShapeDtypeStruct((t, D), BF16),
                   jax.ShapeDtypeStruct((1, D), F32), jax.ShapeDtypeStruct((1, D), F32)],
        compiler_params=_cp("arbitrary"),
    )(*extra, *[p for p, _ in parts], r, gamma)


def loss_head(r, ln, target):
    t = r.shape[0]
    nb = t // BLK

    def body(r_ref, g_ref, b_ref, t_ref, dy_ref, l_ref):
        i = pl.program_id(0)

        @pl.when(i == 0)
        def _():
            dy_ref[...] = jnp.zeros_like(dy_ref)
            l_ref[...] = jnp.zeros_like(l_ref)

        @pl.when(i > 0)
        def _():
            err = _layer_norm(r_ref[...], g_ref[...], b_ref[...]) - t_ref[...]
            dy_ref[...] = err * (1.0 / D)
            l_ref[...] += (0.5 / D) * jnp.sum(err * err, keepdims=True)

    vec = pl.BlockSpec((1, D), lambda i: (0, 0))
    return pl.pallas_call(
        body, name="loss_head", grid=(nb,),
        in_specs=[pl.BlockSpec((BLK, D), lambda i: (i, 0)), vec, vec,
                  pl.BlockSpec((BLK, D), lambda i: (jnp.maximum(i - 1, 0), 0))],
        out_specs=[pl.BlockSpec((BLK, D), lambda i: (i, 0)), pl.BlockSpec((1, 1), lambda i: (0, 0))],
        out_shape=[jax.ShapeDtypeStruct((t, D), F32), jax.ShapeDtypeStruct((1, 1), F32)],
        compiler_params=_cp("arbitrary"),
    )(r, ln[0], ln[1], target)


def split_dh0(dh0, after=None):
    t = dh0.shape[0]
    nb = t // BLK
    extra = [] if after is None else [after]

    def body(*refs):
        a_ref, gx_ref, gm_ref = refs[len(extra):]
        i = pl.program_id(0)
        tot = a_ref[...]

        @pl.when(i == 0)
        def _():
            gm_ref[...] = tot[PAD:, :]

        @pl.when(i > 0)
        def _():
            gx_ref[...] = tot

    blk = pl.BlockSpec((BLK, D), lambda i: (i, 0))
    return pl.pallas_call(
        body, name="split_dh0", grid=(nb,),
        in_specs=[_ANY] * len(extra) + [blk],
        out_specs=[pl.BlockSpec((BLK, D), lambda i: (jnp.maximum(i - 1, 0), 0)),
                   pl.BlockSpec((N_META, D), lambda i: (0, 0))],
        out_shape=[jax.ShapeDtypeStruct((t - BLK, D), F32), jax.ShapeDtypeStruct((N_META, D), F32)],
        compiler_params=_cp("arbitrary"),
    )(*extra, dh0)


def _valid_rows(nrows, first_row):
    return (first_row + lax.broadcasted_iota(jnp.int32, (nrows, 1), 0)) >= PAD


def conv_fwd(proj, conv_w, conv_b):
    t = proj.shape[0]
    c0 = C_XBC // BLK

    def body(x_ref, w_ref, b_ref, o_ref):
        ok = _valid_rows(t, 0)
        x = jnp.where(ok, x_ref[...], 0.0)
        w = w_ref[...]
        acc = b_ref[...] + w[CONV_K - 1:CONV_K, :] * x
        for s in range(1, CONV_K):
            acc += w[CONV_K - 1 - s:CONV_K - s, :] * pltpu.roll(x, s, 0)
        o_ref[...] = jnp.where(ok, acc * _sig(acc), 0.0)

    return pl.pallas_call(
        body, name="conv_fwd", grid=(CONV_D // BLK,),
        in_specs=[pl.BlockSpec((t, BLK), lambda j: (0, c0 + j)),
                  pl.BlockSpec((CONV_K, BLK), lambda j: (0, j)), pl.BlockSpec((1, BLK), lambda j: (0, j))],
        out_specs=pl.BlockSpec((t, BLK), lambda j: (0, j)),
        out_shape=jax.ShapeDtypeStruct((t, CONV_D), F32),
        compiler_params=_cp("arbitrary"),
    )(proj, conv_w, conv_b)


def conv_bwd(dxa, proj, conv_w, conv_b):
    t = proj.shape[0]
    c0 = C_XBC // BLK

    def body(d_ref, x_ref, w_ref, b_ref, dx_ref, dw_ref, db_ref):
        ok = _valid_rows(t, 0)
        x = jnp.where(ok, x_ref[...], 0.0)
        w = w_ref[...]
        xs = [x] + [pltpu.roll(x, s, 0) for s in range(1, CONV_K)]
        acc = b_ref[...] + w[CONV_K - 1:CONV_K, :] * x
        for s in range(1, CONV_K):
            acc += w[CONV_K - 1 - s:CONV_K - s, :] * xs[s]
        sg = _sig(acc)
        dxc = jnp.where(ok, d_ref[...] * (sg * (1.0 + acc * (1.0 - sg))), 0.0)
        db_ref[...] = jnp.sum(dxc, axis=0, keepdims=True)
        dw_ref[...] = jnp.concatenate(
            [jnp.sum(dxc * xs[CONV_K - 1 - k], axis=0, keepdims=True) for k in range(CONV_K)], axis=0)
        dx = w[CONV_K - 1:CONV_K, :] * dxc
        for s in range(1, CONV_K):
            dx += w[CONV_K - 1 - s:CONV_K - s, :] * pltpu.roll(dxc, t - s, 0)
        dx_ref[...] = jnp.where(ok, dx, 0.0)

    col = pl.BlockSpec((t, BLK), lambda j: (0, j))
    return pl.pallas_call(
        body, name="conv_bwd", grid=(CONV_D // BLK,),
        in_specs=[col, pl.BlockSpec((t, BLK), lambda j: (0, c0 + j)),
                  pl.BlockSpec((CONV_K, BLK), lambda j: (0, j)), pl.BlockSpec((1, BLK), lambda j: (0, j))],
        out_specs=[col, pl.BlockSpec((CONV_K, BLK), lambda j: (0, j)), pl.BlockSpec((1, BLK), lambda j: (0, j))],
        out_shape=[jax.ShapeDtypeStruct((t, CONV_D), F32), jax.ShapeDtypeStruct((CONV_K, CONV_D), F32),
                   jax.ShapeDtypeStruct((1, CONV_D), F32)],
        compiler_params=_cp("arbitrary"),
    )(dxa, proj, conv_w, conv_b)


def _softplus(x):
    return jnp.maximum(x, 0.0) + jnp.log(1.0 + jnp.exp(-jnp.abs(x)))


GW = SSD_D // SSD_G
HPG = SSD_H // SSD_G


def _head_expand():
    r = lax.broadcasted_iota(jnp.int32, (BLK, SSD_D), 0)
    c = lax.broadcasted_iota(jnp.int32, (BLK, SSD_D), 1)
    rt = lax.broadcasted_iota(jnp.int32, (SSD_D, BLK), 0)
    ct = lax.broadcasted_iota(jnp.int32, (SSD_D, BLK), 1)
    return (c // SSD_P == r).astype(F32), (rt // SSD_P == ct).astype(F32)


def _ssd_chunk(xa, sm, dtb, alog, dskip, ok, sp):
    e, et = _head_expand()
    dt = jnp.where(ok, _softplus(sm + dtb), 0.0)
    amat = -jnp.exp(alog)
    tri = _tri()
    ac = _nn_hi(tri.astype(F32), dt * amat)
    act = ac.T
    ace, dte, dse = _nn_hi(ac, e), _nn_hi(dt, e), _nn_hi(dskip, e)
    laste = ace[BLK - 1:BLK, :]
    ee, dece, gle = jnp.exp(ace), jnp.exp(laste - ace), jnp.exp(laste)
    xs = xa[:, :SSD_D]
    xdt = xs * dte
    decx = dece * xdt
    xdtb = xdt.astype(BF16)
    d = dict(e=e, et=et, dt=dt, amat=amat, tri=tri, ac=ac, act=act, dte=dte, dse=dse, ee=ee, dece=dece, gle=gle, xs=xs,
             xdt=xdt, xdtb=xdtb, decx=decx, bg=[], cg=[], cb=[], yo=[], seg=[], m=[], new_s=[])
    ys = []
    for g in range(SSD_G):
        cols = slice(GW * g, GW * (g + 1))
        bg = xa[:, SSD_D + SSD_N * g:SSD_D + SSD_N * (g + 1)].astype(BF16)
        cg = xa[:, SSD_D + SSD_G * SSD_N + SSD_N * g:SSD_D + SSD_G * SSD_N + SSD_N * (g + 1)].astype(BF16)
        spg = sp[:, cols]
        sloc = _tn(bg, decx[:, cols].astype(BF16))
        yo = _nn(cg, spg.astype(BF16)) * ee[:, cols]
        cb = _nt(cg, bg)
        d["new_s"].append(gle[:, cols] * spg + sloc)
        yds = []
        for h in range(HPG * g, HPG * (g + 1)):
            seg = jnp.where(tri, jnp.exp(jnp.minimum(ac[:, h:h + 1] - act[h:h + 1, :], 0.0)), 0.0)
            m = cb * seg
            yds.append(_nn(m.astype(BF16), xdtb[:, SSD_P * h:SSD_P * (h + 1)]))
            d["seg"].append(seg)
            d["m"].append(m)
        ys.append(jnp.concatenate(yds, axis=1) + yo)
        for k, val in (("bg", bg), ("cg", cg), ("cb", cb), ("yo", yo)):
            d[k].append(val)
    d["y"] = jnp.concatenate(ys, axis=1) + dse * xs
    return d


def ssd_fwd(xa, proj, dtb, alog, dskip, normg):
    t = xa.shape[0]
    nb = t // BLK
    gw = SSD_D // SSD_G

    def body(xa_ref, z_ref, sm_ref, dtb_ref, al_ref, ds_ref, ng_ref, y_ref, sp_ref, st):
        c = pl.program_id(0)

        @pl.when(c == 0)
        def _():
            st[...] = jnp.zeros_like(st)

        ok = _valid_rows(BLK, c * BLK)
        sp = st[...]
        sp_ref[...] = sp
        d = _ssd_chunk(xa_ref[...], sm_ref[...], dtb_ref[...], al_ref[...], ds_ref[...], ok, sp)
        st[...] = jnp.concatenate(d["new_s"], axis=1)
        y = d["y"]
        z = z_ref[...]
        yg = y * (z * _sig(z))
        outs = []
        for g in range(SSD_G):
            v = yg[:, gw * g:gw * (g + 1)]
            outs.append(v * lax.rsqrt(jnp.mean(v * v, axis=1, keepdims=True) + EPS))
        y_ref[...] = (jnp.concatenate(outs, axis=1) * ng_ref[...]).astype(BF16)

    vec = pl.BlockSpec((1, BLK), lambda c: (0, 0))
    return pl.pallas_call(
        body, name="ssd_fwd", grid=(nb,),
        in_specs=[pl.BlockSpec((BLK, CONV_D), lambda c: (c, 0)),
                  pl.BlockSpec((BLK, SSD_D), lambda c: (c, C_Z // SSD_D)),
                  pl.BlockSpec((BLK, BLK), lambda c: (c, C_SM // BLK)),
                  vec, vec, vec, pl.BlockSpec((1, SSD_D), lambda c: (0, 0))],
        out_specs=[pl.BlockSpec((BLK, SSD_D), lambda c: (c, 0)),
                   pl.BlockSpec((None, SSD_N, SSD_D), lambda c: (c, 0, 0))],
        out_shape=[jax.ShapeDtypeStruct((t, SSD_D), BF16), jax.ShapeDtypeStruct((nb, SSD_N, SSD_D), F32)],
        scratch_shapes=[pltpu.VMEM((SSD_N, SSD_D), F32)],
        compiler_params=_cp("arbitrary"),
    )(xa, proj, proj, dtb, alog, dskip, normg)


def _lane_put(col, lane):
    li = lax.broadcasted_iota(jnp.int32, (col.shape[0], BLK), 1)
    return jnp.where(li == lane, col, 0.0)


def ssd_bwd(dmix, xa, proj, sprev, dtb, alog, dskip, normg):
    t = xa.shape[0]
    nb = t // BLK
    gw = SSD_D // SSD_G
    rev = lambda c: nb - 1 - c

    def body(dy_ref, xa_ref, z_ref, sm_ref, sp_ref, dtb_ref, al_ref, ds_ref, ng_ref,
             dxa_ref, dz_ref, dsm_ref, dng_ref, dds_ref, dal_ref, ddtb_ref, dst):
        c = pl.program_id(0)

        @pl.when(c == 0)
        def _():
            dst[...] = jnp.zeros_like(dst)
            dng_ref[...] = jnp.zeros_like(dng_ref)
            dds_ref[...] = jnp.zeros_like(dds_ref)
            dal_ref[...] = jnp.zeros_like(dal_ref)
            ddtb_ref[...] = jnp.zeros_like(ddtb_ref)

        ok = _valid_rows(BLK, rev(c) * BLK)
        sm = sm_ref[...]
        sp = sp_ref[...]
        d = _ssd_chunk(xa_ref[...], sm, dtb_ref[...], al_ref[...], ds_ref[...], ok, sp)
        dt, amat, ac, act, tri, et, xs, xdt = (d[k] for k in ("dt", "amat", "ac", "act", "tri", "et", "xs", "xdt"))
        rowi = lax.broadcasted_iota(jnp.int32, (BLK, 1), 0)
        y = d["y"]
        z = z_ref[...]
        sgz = _sig(z)
        siluz = z * sgz
        yg = y * siluz
        dout = dy_ref[...]
        ng = ng_ref[...]
        dygs, xhs = [], []
        for g in range(SSD_G):
            v = yg[:, gw * g:gw * (g + 1)]
            rr = lax.rsqrt(jnp.mean(v * v, axis=1, keepdims=True) + EPS)
            xh = v * rr
            dxh = dout[:, gw * g:gw * (g + 1)] * ng[:, gw * g:gw * (g + 1)]
            dygs.append(rr * (dxh - xh * jnp.mean(dxh * xh, axis=1, keepdims=True)))
            xhs.append(xh)
        dyg = jnp.concatenate(dygs, axis=1)
        dng_ref[...] += jnp.sum(dout * jnp.concatenate(xhs, axis=1), axis=0, keepdims=True)
        dy = dyg * siluz
        dz_ref[...] = dyg * y * (sgz * (1.0 + z * (1.0 - sgz)))

        triu = _tri(lower=False)
        dyb = dy.astype(BF16)
        dsn = dst[...]
        dds_ref[...] += _nn_hi(jnp.sum(dy * xs, axis=0, keepdims=True), et)
        dac_all = _nn_hi(dy * jnp.concatenate(d["yo"], axis=1), et)
        dyo = (dy * d["ee"]).astype(BF16)
        gl = jnp.exp(ac[BLK - 1:BLK, :])
        dlast = _nn_hi(jnp.sum(dsn * sp, axis=0, keepdims=True), et) * gl
        bds, db_g, dc_g, dxdt_i, new_dst = [], [], [], [], []
        for g in range(SSD_G):
            cols = slice(GW * g, GW * (g + 1))
            bg, cg = d["bg"][g], d["cg"][g]
            dsng = dsn[:, cols].astype(BF16)
            dc = _nt(dyo[:, cols], sp[:, cols].astype(BF16))
            new_dst.append(_tn(cg, dyo[:, cols]) + d["gle"][:, cols] * dsn[:, cols])
            bds.append(_nn(bg, dsng))
            db = _nt(d["decx"][:, cols].astype(BF16), dsng)
            cbt = _nt(bg, cg)
            dcb = jnp.zeros((BLK, BLK), F32)
            for h in range(HPG * g, HPG * (g + 1)):
                hc = slice(SSD_P * h, SSD_P * (h + 1))
                dm = _nt(dyb[:, hc], d["xdtb"][:, hc])
                dcb += dm * d["seg"][h]
                w = dm * d["m"][h]
                dac_all += _lane_put(jnp.sum(w, axis=1, keepdims=True) - jnp.sum(w.T, axis=1, keepdims=True), h)
                segt = jnp.where(triu, jnp.exp(jnp.minimum(act[h:h + 1, :] - ac[:, h:h + 1], 0.0)), 0.0)
                dxdt_i.append(_nn((cbt * segt).astype(BF16), dyb[:, hc]))
            dcbb = dcb.astype(BF16)
            dc_g.append(dc + _nn(dcbb, bg))
            db_g.append(db + _tn(dcbb, cg))
        dst[...] = jnp.concatenate(new_dst, axis=1)
        bds = jnp.concatenate(bds, axis=1)
        tdec = jnp.exp(ac[BLK - 1:BLK, :] - ac) * _nn_hi(xdt * bds, et)
        dlast += jnp.sum(tdec, axis=0, keepdims=True)
        dac_all += jnp.where(rowi == BLK - 1, dlast, 0.0) - tdec
        dxdt = d["dece"] * bds + jnp.concatenate(dxdt_i, axis=1)
        da = _nn_hi(triu.astype(F32), dac_all)
        ddt = _nn_hi(dxdt * xs, et) + da * amat
        dal_ref[...] += jnp.sum(da * dt, axis=0, keepdims=True) * amat
        ddtr = jnp.where(ok, ddt * _sig(sm + dtb_ref[...]), 0.0)
        ddtb_ref[...] += jnp.sum(ddtr, axis=0, keepdims=True)
        dsm_ref[...] = ddtr
        dxs = d["dse"] * dy + dxdt * d["dte"]
        dxa_ref[...] = jnp.where(ok, jnp.concatenate([dxs] + db_g + dc_g, axis=1), 0.0)

    vec = pl.BlockSpec((1, BLK), lambda c: (0, 0))
    nvec = pl.BlockSpec((1, SSD_D), lambda c: (0, 0))
    return pl.pallas_call(
        body, name="ssd_bwd", grid=(nb,),
        in_specs=[pl.BlockSpec((BLK, SSD_D), lambda c: (rev(c), 0)),
                  pl.BlockSpec((BLK, CONV_D), lambda c: (rev(c), 0)),
                  pl.BlockSpec((BLK, SSD_D), lambda c: (rev(c), C_Z // SSD_D)),
                  pl.BlockSpec((BLK, BLK), lambda c: (rev(c), C_SM // BLK)),
                  pl.BlockSpec((None, SSD_N, SSD_D), lambda c: (rev(c), 0, 0)),
                  vec, vec, vec, nvec],
        out_specs=[pl.BlockSpec((BLK, CONV_D), lambda c: (rev(c), 0)),
                   pl.BlockSpec((BLK, SSD_D), lambda c: (rev(c), 0)),
                   pl.BlockSpec((BLK, BLK), lambda c: (rev(c), 0)),
                   nvec, vec, vec, vec],
        out_shape=[jax.ShapeDtypeStruct((t, CONV_D), F32), jax.ShapeDtypeStruct((t, SSD_D), F32),
                   jax.ShapeDtypeStruct((t, BLK), F32), jax.ShapeDtypeStruct((1, SSD_D), F32),
                   jax.ShapeDtypeStruct((1, BLK), F32), jax.ShapeDtypeStruct((1, BLK), F32),
                   jax.ShapeDtypeStruct((1, BLK), F32)],
        scratch_shapes=[pltpu.VMEM((SSD_N, SSD_D), F32)],
        compiler_params=_cp("arbitrary"),
    )(dmix, xa, proj, proj, sprev, dtb, alog, dskip, normg)


def _attn_scores(q_ref, k_ref, h, dq, scale, mask, bias):
    qh = q_ref[:, dq * h:dq * (h + 1)].astype(BF16)
    kh = k_ref[:, dq * h:dq * (h + 1)].astype(BF16)
    s = _nt(qh, kh) * scale
    if bias is not None:
        s = s + bias
    return qh, kh, jnp.where(mask, s, NEG)


def _segments(nb):
    cuts = sorted({0, nb} | {max(1, round(nb * f)) for f in (0.3, 0.53, 0.77)})
    return list(zip(cuts[:-1], cuts[1:]))


def attn_fwd(q, k, v, qcol, kcol, vcol, nh, dq, dv, scale, c_col=None, c_row=None, lane0=0):
    t = q.shape[0]
    tq = BLK
    use_bias = c_col is not None

    def segment(t0, t1, prev):
        tk = t1 * BLK
        nprev = len(prev)

        def body(*refs):
            refs = refs[nprev:]
            if use_bias:
                q_ref, k_ref, v_ref, cc_ref, cr_ref, o_ref, l_ref = refs
            else:
                q_ref, k_ref, v_ref, o_ref, l_ref = refs
            i = pl.program_id(0)
            rowg = (t0 + i) * tq + lax.broadcasted_iota(jnp.int32, (tq, 1), 0)
            col = lax.broadcasted_iota(jnp.int32, (1, tk), 1)
            mask = (col <= rowg) & (col >= PAD)
            outs = []
            lse = jnp.zeros((tq, BLK), F32)
            for h in range(nh):
                bias = (cc_ref[:, lane0 + h:lane0 + h + 1] - cr_ref[h:h + 1, :]) if use_bias else None
                _, _, s = _attn_scores(q_ref, k_ref, h, dq, scale, mask, bias)
                m = jnp.max(s, axis=1, keepdims=True)
                p = jnp.exp(s - m)
                l = jnp.sum(p, axis=1, keepdims=True)
                vh = v_ref[:, dv * h:dv * (h + 1)].astype(BF16)
                outs.append(_nn(p.astype(BF16), vh) / l)
                lse += _lane_put(m + jnp.log(l), h)
            o_ref[...] = jnp.concatenate(outs, axis=1).astype(BF16)
            l_ref[...] = lse.T[0:8, :]

        in_specs = [_ANY] * nprev + [pl.BlockSpec((tq, nh * dq), lambda i: (t0 + i, qcol)),
                                     pl.BlockSpec((tk, nh * dq), lambda i: (0, kcol)),
                                     pl.BlockSpec((tk, nh * dv), lambda i: (0, vcol))]
        args = list(prev) + [q, k, v]
        if use_bias:
            in_specs += [pl.BlockSpec((tq, BLK), lambda i: (t0 + i, 0)), pl.BlockSpec((8, tk), lambda i: (0, 0))]
            args += [c_col, c_row]
        return pl.pallas_call(
            body, name="attn_fwd", grid=(t1 - t0,),
            in_specs=in_specs,
            out_specs=[pl.BlockSpec((tq, nh * dv), lambda i: (t0 + i, 0)), pl.BlockSpec((8, tq), lambda i: (0, t0 + i))],
            out_shape=[jax.ShapeDtypeStruct((t, nh * dv), BF16), jax.ShapeDtypeStruct((8, t), F32)],
            input_output_aliases={p: p for p in range(nprev)},
            compiler_params=_cp("arbitrary"),
        )(*args)

    outs = []
    for t0, t1 in _segments(t // tq):
        outs = segment(t0, t1, outs)
    return outs


def attn_bwd(q, k, v, do, lse_row, o, qcol, kcol, vcol, docol, ocol, nh, dq, dv, scale, c_col=None, c_row=None, lane0=0):
    t = q.shape[0]
    tq = BLK
    use_bias = c_col is not None

    def segment(t0, t1, prev):
        tk = t1 * BLK
        nprev = len(prev)

        def body(*refs):
            pv, refs = refs[:nprev], refs[nprev:]
            kt = refs[-1]
            if use_bias:
                q_ref, k_ref, v_ref, do_ref, l_ref, o_ref, cc_ref, cr_ref, dq_ref, dk_ref, dv_ref, dcq_ref, dck_ref = refs[:-1]
            else:
                q_ref, k_ref, v_ref, do_ref, l_ref, o_ref, dq_ref, dk_ref, dv_ref = refs[:-1]
            i = pl.program_id(0)

            @pl.when(i == 0)
            def _():
                kt[...] = k_ref[...].astype(BF16).T
                if nprev:
                    dk_ref[...] = pv[1][...]
                    dv_ref[...] = pv[2][...]
                    if use_bias:
                        dck_ref[...] = pv[4][...]
                else:
                    dk_ref[...] = jnp.zeros_like(dk_ref)
                    dv_ref[...] = jnp.zeros_like(dv_ref)
                    if use_bias:
                        dck_ref[...] = jnp.zeros_like(dck_ref)

            key = lax.broadcasted_iota(jnp.int32, (tk, 1), 0)
            qry = (t0 + i) * tq + lax.broadcasted_iota(jnp.int32, (1, tq), 1)
            mask = (key <= qry) & (key >= PAD)
            dot = (do_ref[...].astype(F32) * o_ref[...].astype(F32)).T
            lane = lax.broadcasted_iota(jnp.int32, (1, BLK), 1)
            dqts, dcqs = [], []
            for h in range(nh):
                qh = q_ref[:, dq * h:dq * (h + 1)].astype(BF16)
                kh = k_ref[:, dq * h:dq * (h + 1)].astype(BF16)
                vh = v_ref[:, dv * h:dv * (h + 1)].astype(BF16)
                doh = do_ref[:, dv * h:dv * (h + 1)].astype(BF16)
                delta = jnp.sum(dot[dv * h:dv * (h + 1), :], axis=0, keepdims=True)
                st = _nt(kh, qh) * scale
                if use_bias:
                    st = st + (cr_ref[h:h + 1, :] - cc_ref[h])
                pt = jnp.exp(jnp.where(mask, st, NEG) - l_ref[h:h + 1, :])
                dst = pt * (_nt(vh, doh) - delta)
                dsb = dst.astype(BF16)
                dk_ref[:, dq * h:dq * (h + 1)] += _nn(dsb, qh) * scale
                dv_ref[:, dv * h:dv * (h + 1)] += _nn(pt.astype(BF16), doh)
                dqts.append(_nn(kt[dq * h:dq * (h + 1), :], dsb))
                if use_bias:
                    dcqs.append(jnp.sum(dst, axis=0, keepdims=True))
                    dck_ref[h] += dst
            dq_ref[...] = jnp.concatenate(dqts, axis=0).T * scale
            if use_bias:
                dcq_ref[...] = jnp.concatenate(dcqs + [jnp.zeros((8 - nh, tq), F32)], axis=0)

        keys_q = pl.BlockSpec((tk, nh * dq), lambda i: (0, 0))
        keys_v = pl.BlockSpec((tk, nh * dv), lambda i: (0, 0))
        keys_c = pl.BlockSpec((nh, tk, BLK), lambda i: (0, 0, 0))
        qrow = pl.BlockSpec((8, tq), lambda i: (0, t0 + i))
        prev_specs = ([_ANY, keys_q, keys_v] + ([_ANY, keys_c] if use_bias else [])) if nprev else []
        in_specs = prev_specs + [pl.BlockSpec((tq, nh * dq), lambda i: (t0 + i, qcol)),
                                 pl.BlockSpec((tk, nh * dq), lambda i: (0, kcol)),
                                 pl.BlockSpec((tk, nh * dv), lambda i: (0, vcol)),
                                 pl.BlockSpec((tq, nh * dv), lambda i: (t0 + i, docol)),
                                 qrow,
                                 pl.BlockSpec((tq, nh * dv), lambda i: (t0 + i, ocol))]
        args = list(prev) + [q, k, v, do, lse_row, o]
        out_specs = [pl.BlockSpec((tq, nh * dq), lambda i: (t0 + i, 0)), keys_q, keys_v]
        out_shape = [jax.ShapeDtypeStruct((t, nh * dq), F32), jax.ShapeDtypeStruct((t, nh * dq), F32),
                     jax.ShapeDtypeStruct((t, nh * dv), F32)]
        if use_bias:
            in_specs += [keys_c, qrow]
            args += [c_col, c_row]
            out_specs += [qrow, keys_c]
            out_shape += [jax.ShapeDtypeStruct((8, t), F32), jax.ShapeDtypeStruct((nh, t, BLK), F32)]
        return pl.pallas_call(
            body, name="attn_bwd", grid=(t1 - t0,),
            in_specs=in_specs, out_specs=out_specs, out_shape=out_shape,
            scratch_shapes=[pltpu.VMEM((nh * dq, tk), BF16)],
            input_output_aliases={p: p for p in range(nprev)},
            compiler_params=_cp("arbitrary"),
        )(*args)

    outs = []
    for t0, t1 in reversed(_segments(t // tq)):
        outs = segment(t0, t1, outs)
    return outs


def fox_pre(proj, fb):
    t = proj.shape[0]
    nb = t // BLK

    def body(sm_ref, fb_ref, c_ref, cr_ref, cb_ref):
        x = sm_ref[...] + fb_ref[...]
        lane = lax.broadcasted_iota(jnp.int32, (1, BLK), 1)
        keep = _valid_rows(t, 0) & (lane >= SM_F) & (lane < SM_F + FOX_H)
        logf = jnp.where(keep, jnp.minimum(x, 0.0) - jnp.log(1.0 + jnp.exp(-jnp.abs(x))), 0.0)
        tri = _tri().astype(F32)
        carry = jnp.zeros((1, BLK), F32)
        for b in range(nb):
            cb = _nn_hi(tri, logf[b * BLK:(b + 1) * BLK, :]) + carry
            c_ref[b * BLK:(b + 1) * BLK, :] = cb
            carry = cb[BLK - 1:BLK, :]
        cr_ref[...] = c_ref[...].T[SM_F:SM_F + 8, :]
        for h in range(FOX_H):
            cb_ref[h] = jnp.broadcast_to(c_ref[:, SM_F + h:SM_F + h + 1], (t, BLK))

    return pl.pallas_call(
        body, name="fox_pre", grid=(1,),
        in_specs=[pl.BlockSpec((t, BLK), lambda i: (0, C_SM // BLK)), pl.BlockSpec((1, BLK), lambda i: (0, 0))],
        out_specs=[pl.BlockSpec((t, BLK), lambda i: (0, 0)), pl.BlockSpec((8, t), lambda i: (0, 0)),
                   pl.BlockSpec((FOX_H, t, BLK), lambda i: (0, 0, 0))],
        out_shape=[jax.ShapeDtypeStruct((t, BLK), F32), jax.ShapeDtypeStruct((8, t), F32),
                   jax.ShapeDtypeStruct((FOX_H, t, BLK), F32)],
        compiler_params=_cp("arbitrary"),
    )(proj, fb)


def fox_pre_bwd(dcq, dck, proj, fb, dsm_in):
    t = proj.shape[0]
    nb = t // BLK

    def body(dcq_ref, dck_ref, sm_ref, fb_ref, din_ref, dsm_ref, dfb_ref, scr):
        triu = _tri(lower=False).astype(F32)
        carry = jnp.zeros((1, BLK), F32)
        scr[...] = jnp.concatenate([jnp.zeros((SM_F, t), F32), dcq_ref[...], jnp.zeros((BLK - SM_F - 8, t), F32)], axis=0).T
        lane = lax.broadcasted_iota(jnp.int32, (1, BLK), 1)
        for b in range(nb - 1, -1, -1):
            blk = scr[b * BLK:(b + 1) * BLK, :]
            for h in range(FOX_H):
                blk -= jnp.where(lane == SM_F + h, jnp.sum(dck_ref[h, b * BLK:(b + 1) * BLK, :], axis=1, keepdims=True), 0.0)
            cb = _nn_hi(triu, blk) + carry
            scr[b * BLK:(b + 1) * BLK, :] = cb
            carry = cb[0:1, :]
        x = sm_ref[...] + fb_ref[...]
        lane = lax.broadcasted_iota(jnp.int32, (1, BLK), 1)
        keep = _valid_rows(t, 0) & (lane >= SM_F) & (lane < SM_F + FOX_H)
        df = jnp.where(keep, scr[...] * _sig(-x), 0.0)
        dfb_ref[...] = jnp.sum(df, axis=0, keepdims=True)
        dsm_ref[...] = din_ref[...] + df

    full = pl.BlockSpec((t, BLK), lambda i: (0, 0))
    return pl.pallas_call(
        body, name="fox_pre_bwd", grid=(1,),
        in_specs=[pl.BlockSpec((8, t), lambda i: (0, 0)), pl.BlockSpec((FOX_H, t, BLK), lambda i: (0, 0, 0)),
                  pl.BlockSpec((t, BLK), lambda i: (0, C_SM // BLK)), pl.BlockSpec((1, BLK), lambda i: (0, 0)), full],
        out_specs=[full, pl.BlockSpec((1, BLK), lambda i: (0, 0))],
        out_shape=[jax.ShapeDtypeStruct((t, BLK), F32), jax.ShapeDtypeStruct((1, BLK), F32)],
        scratch_shapes=[pltpu.VMEM((t, BLK), F32)],
        compiler_params=_cp("arbitrary"),
    )(dcq, dck, proj, fb, dsm_in)


def _swap_rope(x):
    lane = lax.broadcasted_iota(jnp.int32, (1, BLK), 1)
    return jnp.where((lane >= SM_KR) & (lane < SM_KR + 16), pltpu.roll(x, BLK - 16, 1),
                     jnp.where((lane >= SM_KR + 16) & (lane < SM_KR + 32), pltpu.roll(x, 16, 1), 0.0))


def _rms(x, g):
    r = lax.rsqrt(jnp.mean(x * x, axis=1, keepdims=True) + EPS)
    return r, x * r


def mla_pre(proj, qg, kvg, wq, wk, wv, cosq, sinq):
    t = proj.shape[0]
    tm = _row_tile(t)

    def body(cq_ref, ckv_ref, sm_ref, qg_ref, kvg_ref, wq_ref, wk_ref, wv_ref, cos_ref, sin_ref,
             q_ref, k_ref, v_ref, cqn_ref, ckvn_ref):
        cs, sn = cos_ref[...], sin_ref[...]
        _, xh = _rms(cq_ref[...], None)
        cqn = (xh * qg_ref[...]).astype(BF16)
        cqn_ref[...] = cqn
        qraw = _nn(cqn, wq_ref[...])
        qs = []
        for h in range(MLA_H):
            hb = qraw[:, BLK * h:BLK * (h + 1)]
            qs.append(hb * cs + _swap_rope(hb) * sn)
        q_ref[...] = jnp.concatenate(qs, axis=1).astype(BF16)
        _, kh = _rms(ckv_ref[...], None)
        ckvn = (kh * kvg_ref[...]).astype(BF16)
        ckvn_ref[...] = ckvn
        kraw = _nn(ckvn, wk_ref[...])
        v_ref[...] = _nn(ckvn, wv_ref[...]).astype(BF16)
        lane = lax.broadcasted_iota(jnp.int32, (1, BLK), 1)
        kr = sm_ref[...]
        krr = jnp.where((lane >= SM_KR) & (lane < SM_KR + MLA_ROPE), kr * cs + _swap_rope(kr) * sn, 0.0)
        k_ref[...] = jnp.concatenate([kraw[:, BLK * h:BLK * (h + 1)] + krr for h in range(MLA_H)], axis=1).astype(BF16)

    def rows(w, cb):
        return pl.BlockSpec((tm, w), lambda i: (i, cb))

    def whole(a):
        return pl.BlockSpec(a.shape, lambda i: (0, 0))

    return pl.pallas_call(
        body, name="mla_pre", grid=(t // tm,),
        in_specs=[rows(MLA_QL, C_CQ // MLA_QL), rows(MLA_KVL, C_CKV // MLA_KVL), rows(BLK, C_SM // BLK),
                  whole(qg), whole(kvg), whole(wq), whole(wk), whole(wv), rows(BLK, 0), rows(BLK, 0)],
        out_specs=[rows(512, 0), rows(512, 0), rows(256, 0), rows(MLA_QL, 0), rows(MLA_KVL, 0)],
        out_shape=[jax.ShapeDtypeStruct((t, 512), BF16), jax.ShapeDtypeStruct((t, 512), BF16),
                   jax.ShapeDtypeStruct((t, 256), BF16), jax.ShapeDtypeStruct((t, MLA_QL), BF16),
                   jax.ShapeDtypeStruct((t, MLA_KVL), BF16)],
        compiler_params=_cp("arbitrary"),
    )(proj, proj, proj, qg, kvg, wq, wk, wv, cosq, sinq)


def mla_pre_bwd(dq, dk, dv, proj, cqn, ckvn, qg, kvg, wq, wk, wv, cosq, sinq, dsm_in):
    t = proj.shape[0]
    tm = _row_tile(t)

    def body(dq_ref, dk_ref, dv_ref, cq_ref, ckv_ref, cqn_ref, ckvn_ref, qg_ref, kvg_ref, wq_ref, wk_ref, wv_ref,
             cos_ref, sin_ref, din_ref, dcq_ref, dckv_ref, dsm_ref, dwq_ref, dwk_ref, dwv_ref, dqg_ref, dkvg_ref):
        i = pl.program_id(0)

        @pl.when(i == 0)
        def _():
            for r in (dwq_ref, dwk_ref, dwv_ref, dqg_ref, dkvg_ref):
                r[...] = jnp.zeros_like(r)

        cs, sn = cos_ref[...], sin_ref[...]
        lane = lax.broadcasted_iota(jnp.int32, (1, BLK), 1)

        def unrope(dy):
            return dy * cs + _swap_rope(dy * sn)

        dqp = jnp.concatenate([unrope(dq_ref[:, BLK * h:BLK * (h + 1)]) for h in range(MLA_H)], axis=1).astype(BF16)
        dwq_ref[...] += _tn(cqn_ref[...], dqp)
        dcqn = _nt(dqp, wq_ref[...])
        r, xh = _rms(cq_ref[...], None)
        dqg_ref[...] += jnp.sum(dcqn * xh, axis=0, keepdims=True)
        dxh = dcqn * qg_ref[...]
        dcq_ref[...] = r * (dxh - xh * jnp.mean(dxh * xh, axis=1, keepdims=True))

        dkn, dkr = [], jnp.zeros((tm, BLK), F32)
        for h in range(MLA_H):
            blk = dk_ref[:, BLK * h:BLK * (h + 1)]
            dkn.append(jnp.where(lane < MLA_NOPE, blk, 0.0))
            dkr += jnp.where((lane >= SM_KR) & (lane < SM_KR + MLA_ROPE), blk, 0.0)
        dknb = jnp.concatenate(dkn, axis=1).astype(BF16)
        dvb = dv_ref[...].astype(BF16)
        ckvn = ckvn_ref[...]
        dwk_ref[...] += _tn(ckvn, dknb)
        dwv_ref[...] += _tn(ckvn, dvb)
        dckvn = _nt(dknb, wk_ref[...]) + _nt(dvb, wv_ref[...])
        r2, kh = _rms(ckv_ref[...], None)
        dkvg_ref[...] += jnp.sum(dckvn * kh, axis=0, keepdims=True)
        dkh = dckvn * kvg_ref[...]
        dckv_ref[...] = r2 * (dkh - kh * jnp.mean(dkh * kh, axis=1, keepdims=True))
        dsm_ref[...] = din_ref[...] + jnp.where((lane >= SM_KR) & (lane < SM_KR + MLA_ROPE), unrope(dkr), 0.0)

    def rows(w, cb):
        return pl.BlockSpec((tm, w), lambda i: (i, cb))

    def whole(a):
        return pl.BlockSpec(a.shape, lambda i: (0, 0))

    def wshape(a):
        return jax.ShapeDtypeStruct(a.shape, F32)

    return pl.pallas_call(
        body, name="mla_pre_bwd", grid=(t // tm,),
        in_specs=[rows(512, 0), rows(512, 0), rows(256, 0), rows(MLA_QL, C_CQ // MLA_QL), rows(MLA_KVL, C_CKV // MLA_KVL),
                  rows(MLA_QL, 0), rows(MLA_KVL, 0), whole(qg), whole(kvg), whole(wq), whole(wk), whole(wv),
                  rows(BLK, 0), rows(BLK, 0), rows(BLK, 0)],
        out_specs=[rows(MLA_QL, 0), rows(MLA_KVL, 0), rows(BLK, 0), whole(wq), whole(wk), whole(wv), whole(qg), whole(kvg)],
        out_shape=[jax.ShapeDtypeStruct((t, MLA_QL), F32), jax.ShapeDtypeStruct((t, MLA_KVL), F32),
                   jax.ShapeDtypeStruct((t, BLK), F32), wshape(wq), wshape(wk), wshape(wv), wshape(qg), wshape(kvg)],
        compiler_params=_cp("arbitrary"),
    )(dq, dk, dv, proj, proj, cqn, ckvn, qg, kvg, wq, wk, wv, cosq, sinq, dsm_in)


def _slot_sum(me, own, recv_ref):
    gg = own.astype(F32)
    for s in range(N_DEV):
        gg = gg + jnp.where(me == s, 0.0, recv_ref[s].astype(F32))
    return gg


def adamw(w, m, v, g=None, recv=None, own=None, me_arr=None):
    shape = w.shape
    c = shape[-1]
    from_recv = recv is not None
    if not from_recv:
        me_arr = jnp.zeros((1,), jnp.int32)
    nl = len(recv) if from_recv else 1
    rws = w.size // c // nl
    tr = rws
    for d in (1024, 512, 352, 256, 128, 64, 32, 16, 8):
        if rws % d == 0 and d * c * 4 <= (2 << 20):
            tr = d
            break
    nt = rws // tr
    w2, m2, v2 = (a.reshape(nl, rws, c) for a in (w, m, v))
    if from_recv:
        gin = [a.reshape(N_DEV, rws, c) for a in list(recv) + list(own)]
    else:
        gin = [g.reshape(1, rws, c)]

    def body(me_ref, w_ref, m_ref, v_ref, *rest):
        g_refs, outs = rest[:len(gin)], rest[len(gin):]
        if from_recv:
            g_out, outs = outs[0], outs[1:]
            for li in range(nl):
                @pl.when(pl.program_id(0) == li)
                def _(li=li):
                    g_out[...] = _slot_sum(me_ref[0], g_refs[nl + li][...], g_refs[li])
            gg = g_out[...]
        else:
            gg = g_refs[0][...]
        d_ref, nm_ref, nv_ref = outs
        nm = B1 * m_ref[...] + (1.0 - B1) * gg
        nv = B2 * v_ref[...] + (1.0 - B2) * (gg * gg)
        mh = nm / (1.0 - B1 ** STEP)
        vh = nv / (1.0 - B2 ** STEP)
        d_ref[...] = -LR * (mh / (jnp.sqrt(vh) + AEPS) + WD * w_ref[...])
        nm_ref[...] = nm
        nv_ref[...] = nv

    row = pl.BlockSpec((None, tr, c), lambda l, i, me: (l, i, 0))
    if from_recv:
        gspecs = [pl.BlockSpec((N_DEV, tr, c), lambda l, i, me, li=li: (0, jnp.where(l == li, i, 0), 0))
                  for li in range(nl)]
        gspecs += [pl.BlockSpec((None, tr, c), lambda l, i, me, li=li: (me[0], jnp.where(l == li, i, 0), 0))
                   for li in range(nl)]
    else:
        gspecs = [row]
    nout = 4 if from_recv else 3
    outs = pl.pallas_call(
        body, name="adamw",
        grid_spec=pltpu.PrefetchScalarGridSpec(num_scalar_prefetch=1, grid=(nl, nt), in_specs=[row, row, row] + gspecs,
                                               out_specs=[row] * nout),
        out_shape=[jax.ShapeDtypeStruct((nl, rws, c), F32)] * nout,
        compiler_params=_cp("arbitrary", "arbitrary"),
    )(me_arr, w2, m2, v2, *gin)
    return tuple(o.reshape(shape) for o in outs)


def sum_slots(recv, own=None, me_arr=None):
    _, r, c = recv.shape
    if own is None:
        own, me_arr = recv, jnp.zeros((1,), jnp.int32)
        plain = True
    else:
        plain = False

    def body(me_ref, r_ref, own_ref, o_ref):
        if plain:
            gg = r_ref[0].astype(F32)
            for s in range(1, N_DEV):
                gg = gg + r_ref[s].astype(F32)
            o_ref[...] = gg
        else:
            o_ref[...] = _slot_sum(me_ref[0], own_ref[...], r_ref)

    return pl.pallas_call(
        body, name="sum_slots",
        grid_spec=pltpu.PrefetchScalarGridSpec(
            num_scalar_prefetch=1, grid=(1,),
            in_specs=[pl.BlockSpec((N_DEV, r, c), lambda i, me: (0, 0, 0)),
                      pl.BlockSpec((None, r, c), lambda i, me: (me[0], 0, 0))],
            out_specs=pl.BlockSpec((r, c), lambda i, me: (0, 0))),
        out_shape=jax.ShapeDtypeStruct((r, c), F32),
        compiler_params=_cp("arbitrary"),
    )(me_arr, recv, own)


_FLIPS = [(0, 0, 1), (0, 1, 0), (0, 1, 1), (1, 0, 0), (1, 0, 1), (1, 1, 0), (1, 1, 1)]
_ANY = pl.BlockSpec(memory_space=pl.ANY)


def _mesh_place():
    x, y, c = lax.axis_index("x"), lax.axis_index("y"), lax.axis_index("c")
    me = 4 * x + 2 * y + c
    peers = [((x + fx) % 2, (y + fy) % 2, (c + fc) % 2) for fx, fy, fc in _FLIPS]
    return me, peers


def place_own(src, l, dtype, me_arr):
    _, r, c = src.shape
    tr = r
    for d in (512, 352, 256, 128, 64, 32, 16, 8):
        if r % d == 0 and d * c * 4 <= (2 << 20):
            tr = d
            break

    def body(me_ref, s_ref, o_ref):
        o_ref[...] = s_ref[...].astype(dtype)

    return pl.pallas_call(
        body, name="place_own",
        grid_spec=pltpu.PrefetchScalarGridSpec(
            num_scalar_prefetch=1, grid=(r // tr,),
            in_specs=[pl.BlockSpec((None, tr, c), lambda i, me: (l, i, 0))],
            out_specs=pl.BlockSpec((None, tr, c), lambda i, me: (me[0], i, 0))),
        out_shape=jax.ShapeDtypeStruct((N_DEV, r, c), dtype),
        compiler_params=_cp("arbitrary"),
    )(me_arr, src)


_HBM = pl.BlockSpec(memory_space=pltpu.HBM)
_SEMS = pl.BlockSpec(memory_space=pltpu.SEMAPHORE)
_EFFECT = pltpu.SideEffectType.DATAFLOW_SIDE_EFFECTING


def exchange_start(mode, arrays, name):
    n = len(arrays)
    gather = mode == "gather"
    ns = 0 if gather else n
    zones = list(arrays) if gather else [lax.empty(a.shape, a.dtype) for a in arrays]
    ops = ([] if gather else list(arrays)) + zones

    def body(*refs):
        srcs, lands = refs[:ns], refs[ns:ns + n]
        send_sems, recv_sems = refs[ns + n], refs[ns + n + 1]
        token = refs[-1]
        me, peers = _mesh_place()
        ids = [4 * p[0] + 2 * p[1] + p[2] for p in peers]
        for j in range(n):
            for k in range(N_DEV - 1):
                src = lands[j].at[me] if gather else srcs[j].at[ids[k]]
                pltpu.make_async_remote_copy(src_ref=src, dst_ref=lands[j].at[me],
                                             send_sem=send_sems.at[j * (N_DEV - 1) + k],
                                             recv_sem=recv_sems.at[j * (N_DEV - 1) + k], device_id=peers[k],
                                             device_id_type=pl.DeviceIdType.MESH).start()
        token[...] = jnp.zeros_like(token)

    nsem = n * (N_DEV - 1)
    res = pl.pallas_call(
        body, name=name,
        in_specs=[_HBM] * (ns + n),
        out_specs=(_SEMS, _SEMS, *[_HBM] * (ns + n), pl.BlockSpec(memory_space=pltpu.VMEM)),
        out_shape=(pltpu.SemaphoreType.DMA((nsem,)), pltpu.SemaphoreType.DMA((nsem,)),
                   *[pltpu.HBM(a.shape, a.dtype) for a in ops], jax.ShapeDtypeStruct((8, BLK), F32)),
        input_output_aliases={i: 2 + i for i in range(ns + n)},
        compiler_params=pltpu.CompilerParams(has_side_effects=_EFFECT),
    )(*[pltpu.with_memory_space_constraint(a, pltpu.HBM) for a in ops])
    return dict(gather=gather, send=res[0], recv=res[1], srcs=list(res[2:2 + ns]), lands=list(res[2 + ns:2 + ns + n]),
                token=res[-1])


def exchange_wait(hd, idxs, name, after):
    gather = hd["gather"]
    n = len(idxs)
    ns = 0 if gather else n
    ops = ([] if gather else [hd["srcs"][j] for j in idxs]) + [hd["lands"][j] for j in idxs]

    def body(*refs):
        srcs, lands = refs[:ns], refs[ns:ns + n]
        send_sems, recv_sems = refs[ns + n], refs[ns + n + 1]
        me, peers = _mesh_place()
        ids = [4 * p[0] + 2 * p[1] + p[2] for p in peers]
        for p, j in enumerate(idxs):
            for k in range(N_DEV - 1):
                src = lands[p].at[me] if gather else srcs[p].at[ids[k]]
                cp = pltpu.make_async_remote_copy(src_ref=src, dst_ref=lands[p].at[ids[k]],
                                                  send_sem=send_sems.at[j * (N_DEV - 1) + k],
                                                  recv_sem=recv_sems.at[j * (N_DEV - 1) + k], device_id=peers[k],
                                                  device_id_type=pl.DeviceIdType.MESH)
                cp.wait_send()
                cp.wait_recv()

    res = pl.pallas_call(
        body, name=name,
        in_specs=[_HBM] * (ns + n) + [_SEMS, _SEMS, _ANY],
        out_specs=[_HBM] * (ns + n),
        out_shape=[pltpu.HBM(a.shape, a.dtype) for a in ops],
        input_output_aliases={i: i for i in range(ns + n)},
        compiler_params=pltpu.CompilerParams(has_side_effects=_EFFECT),
    )(*ops, hd["send"], hd["recv"], after)
    return list(res[:ns]), list(res[ns:])


def _chip_place():
    x, y, c = lax.axis_index("x"), lax.axis_index("y"), lax.axis_index("c")
    chips = [((x + 1) % 2, y), (x, (y + 1) % 2), ((x + 1) % 2, (y + 1) % 2)]
    ident = lambda p: 4 * p[0] + 2 * p[1] + p[2]
    return dict(me=4 * x + 2 * y + c, sib=(x, y, 1 - c), sib_id=4 * x + 2 * y + 1 - c,
                same=[(cx, cy, c) for cx, cy in chips], same_ids=[ident((cx, cy, c)) for cx, cy in chips],
                other_ids=[ident((cx, cy, 1 - c)) for cx, cy in chips])


def _remote(src, dst, send_sem, recv_sem, dev):
    return pltpu.make_async_remote_copy(src_ref=src, dst_ref=dst, send_sem=send_sem, recv_sem=recv_sem, device_id=dev,
                                        device_id_type=pl.DeviceIdType.MESH)


def gather_start(zones, name):
    n = len(zones)

    def body(*refs):
        lands, send_sems, recv_sems, token = refs[:n], refs[n], refs[n + 1], refs[-1]
        pc = _chip_place()
        for j in range(n):
            own = lands[j].at[pc["me"]]
            for k, dev in enumerate([pc["sib"]] + pc["same"]):
                _remote(own, own, send_sems.at[4 * j + k], recv_sems.at[4 * j + k], dev).start()
        token[...] = jnp.zeros_like(token)

    res = pl.pallas_call(
        body, name=name,
        in_specs=[_HBM] * n,
        out_specs=(_SEMS, _SEMS, *[_HBM] * n, pl.BlockSpec(memory_space=pltpu.VMEM)),
        out_shape=(pltpu.SemaphoreType.DMA((4 * n,)), pltpu.SemaphoreType.DMA((4 * n,)),
                   *[pltpu.HBM(a.shape, a.dtype) for a in zones], jax.ShapeDtypeStruct((8, BLK), F32)),
        input_output_aliases={i: 2 + i for i in range(n)},
        compiler_params=pltpu.CompilerParams(has_side_effects=_EFFECT),
    )(*[pltpu.with_memory_space_constraint(a, pltpu.HBM) for a in zones])
    return dict(send=res[0], recv=res[1], lands=list(res[2:2 + n]), token=res[-1])


def gather_relay(hd, idxs, name, after):
    n = len(idxs)

    def body(*refs):
        lands, send_sems, recv_sems = refs[:n], refs[n], refs[n + 1]
        fsend, frecv, token = refs[n + 3 + n], refs[n + 4 + n], refs[-1]
        pc = _chip_place()
        for p, j in enumerate(idxs):
            for k in range(3):
                _remote(lands[p].at[pc["me"]], lands[p].at[pc["same_ids"][k]], send_sems.at[4 * j + 1 + k],
                        recv_sems.at[4 * j + 1 + k], pc["same"][k]).wait_recv()
        for p in range(n):
            for k in range(3):
                got = lands[p].at[pc["same_ids"][k]]
                _remote(got, got, fsend.at[3 * p + k], frecv.at[3 * p + k], pc["sib"]).start()
        token[...] = jnp.zeros_like(token)

    ops = [hd["lands"][j] for j in idxs]
    res = pl.pallas_call(
        body, name=name,
        in_specs=[_HBM] * n + [_SEMS, _SEMS, _ANY],
        out_specs=(*[_HBM] * n, _SEMS, _SEMS, pl.BlockSpec(memory_space=pltpu.VMEM)),
        out_shape=(*[pltpu.HBM(a.shape, a.dtype) for a in ops], pltpu.SemaphoreType.DMA((3 * n,)),
                   pltpu.SemaphoreType.DMA((3 * n,)), jax.ShapeDtypeStruct((8, BLK), F32)),
        input_output_aliases={i: i for i in range(n)},
        compiler_params=pltpu.CompilerParams(has_side_effects=_EFFECT),
    )(*ops, hd["send"], hd["recv"], after)
    return dict(lands=list(res[:n]), fsend=res[n], frecv=res[n + 1], token=res[-1])


def gather_wait(hd, rl, idxs, name, after):
    n = len(idxs)

    def body(*refs):
        lands, send_sems, recv_sems, fsend, frecv = refs[:n], refs[n], refs[n + 1], refs[n + 2], refs[n + 3]
        pc = _chip_place()
        for p, j in enumerate(idxs):
            own = lands[p].at[pc["me"]]
            for k, dev in enumerate([pc["sib"]] + pc["same"]):
                _remote(own, own, send_sems.at[4 * j + k], recv_sems.at[4 * j + k], dev).wait_send()
            _remote(own, lands[p].at[pc["sib_id"]], send_sems.at[4 * j], recv_sems.at[4 * j], pc["sib"]).wait_recv()
            for k in range(3):
                cp = _remote(lands[p].at[pc["same_ids"][k]], lands[p].at[pc["other_ids"][k]], fsend.at[3 * p + k],
                             frecv.at[3 * p + k], pc["sib"])
                cp.wait_send()
                cp.wait_recv()

    res = pl.pallas_call(
        body, name=name,
        in_specs=[_HBM] * n + [_SEMS, _SEMS, _SEMS, _SEMS, _ANY],
        out_specs=[_HBM] * n,
        out_shape=[pltpu.HBM(a.shape, a.dtype) for a in rl["lands"]],
        input_output_aliases={i: i for i in range(n)},
        compiler_params=pltpu.CompilerParams(has_side_effects=_EFFECT),
    )(*rl["lands"], hd["send"], hd["recv"], rl["fsend"], rl["frecv"], after)
    return list(res)


def _pad_cols(a, n):
    return jnp.pad(a, ((0, 0),) * (a.ndim - 1) + ((0, n - a.shape[-1]),))


def w_in_to_padded(w):
    z = lambda n: jnp.zeros(w.shape[:-1] + (n,), w.dtype)
    return jnp.concatenate([
        w[..., 0:1280], w[..., 1288:2056], w[..., 2060:2316], w[..., 2316:2444],
        w[..., 1280:1288], w[..., 2056:2060], z(SM_KR - SM_F - FOX_H), w[..., 2444:2476], z(BLK - SM_KR - MLA_ROPE)], axis=-1)


def w_in_from_padded(g):
    s = C_SM
    return jnp.concatenate([
        g[..., 0:1280], g[..., s + SM_DT:s + SM_DT + 8], g[..., 1280:2048], g[..., s + SM_F:s + SM_F + 4],
        g[..., 2048:2304], g[..., 2304:2432], g[..., s + SM_KR:s + SM_KR + MLA_ROPE]], axis=-1)


def _unshard_cols(gth):
    n, r, c = gth.shape
    return jnp.transpose(gth, (1, 0, 2)).reshape(r, n * c)


def _shard_cols(full):
    r, nc = full.shape
    return jnp.transpose(full.reshape(r, N_DEV, nc // N_DEV), (1, 0, 2))


def mla_weights(uq_g, ukv_g):
    uq = _unshard_cols(uq_g)
    dqh = MLA_NOPE + MLA_ROPE
    wq = jnp.concatenate([_pad_cols(uq[:, dqh * h:dqh * (h + 1)], BLK) for h in range(MLA_H)], axis=1)
    wk = jnp.concatenate([_pad_cols(ukv_g[2 * h], BLK) for h in range(MLA_H)], axis=1)
    wv = jnp.concatenate([ukv_g[2 * h + 1] for h in range(MLA_H)], axis=1)
    return wq, wk, wv


def mla_weight_grads(dwq, dwk, dwv):
    dqh = MLA_NOPE + MLA_ROPE
    duq = _shard_cols(jnp.concatenate([dwq[:, BLK * h:BLK * h + dqh] for h in range(MLA_H)], axis=1))
    parts = []
    for h in range(MLA_H):
        parts += [dwk[:, BLK * h:BLK * h + MLA_NOPE], dwv[:, MLA_V * h:MLA_V * (h + 1)]]
    return duq, jnp.stack(parts, axis=0)


def rope_tables(t):
    pos = (jnp.arange(t, dtype=jnp.int32) - PAD).astype(F32)
    inv_freq = 1.0 / (10000.0 ** (jnp.arange(0, MLA_ROPE, 2, dtype=F32) / MLA_ROPE))
    ang = pos[:, None] * inv_freq[None, :]
    cos, sin = jnp.cos(ang), jnp.sin(ang)
    one, zero = jnp.ones((t, SM_KR), F32), jnp.zeros((t, SM_KR), F32)
    tail = BLK - SM_KR - MLA_ROPE
    cosq = jnp.concatenate([one, cos, cos, jnp.ones((t, tail), F32)], axis=1)
    sinq = jnp.concatenate([zero, -sin, sin, jnp.zeros((t, tail), F32)], axis=1)
    return cosq, sinq


def _lanes(v, off=0):
    return jnp.pad(v.astype(F32), (off, BLK - off - v.shape[0]))[None, :]


def layer_fwd(x, ln, hb, getw, tabs, ahead):
    sv = {"h0b": hb}
    W = dict(getw("ffn1", hb))
    ln1 = (W["ln1_g"], W["ln1_b"])
    u, v, r1, h1b = ffn_fwd_seq(x, ln, W["g1"], W["u1"], W["d1"], ln1)
    sv.update(u1=u, v1=v, r1=r1, h1b=h1b)
    W.update(getw("mix", h1b))
    ln2 = (W["ln2_g"], W["ln2_b"])
    proj = mm_nn(h1b, W["w_in"])
    xa = conv_fwd(proj, W["conv_w"], W["conv_b"])
    y_ssd, sprev = ssd_fwd(xa, proj, W["dtb"], W["alog"], W["dskip"], W["normg"])
    c_col, c_row, c_keys = fox_pre(proj, W["fb"])
    y_fox, lse_f = attn_fwd(proj, proj, proj, C_FQ // 256, C_FK // 256, C_FV // 256, FOX_H, FOX_DH, FOX_DH,
                            FOX_DH ** -0.5, c_col, c_row, SM_F)
    ahead(0, "ffn2", y_fox)
    q, k, vv, cqn, ckvn = mla_pre(proj, W["qg"], W["kvg"], W["wq"], W["wk"], W["wv"], *tabs)
    y_mla, lse_m = attn_fwd(q, k, vv, 0, 0, 0, MLA_H, BLK, MLA_V, (MLA_NOPE + MLA_ROPE) ** -0.5)
    mixcat = jnp.concatenate([y_ssd, y_fox, y_mla], axis=1)
    r2, h2b = mm_res_ln(mixcat, W["w_out"], r1, ln1, ln2)
    sv.update(proj=proj, xa=xa, sprev=sprev, c_keys=c_keys, c_row=c_row, lse_f=lse_f, q=q, k=k, v=vv, cqn=cqn, ckvn=ckvn,
              lse_m=lse_m, mixcat=mixcat, r2=r2, h2b=h2b)
    W.update(getw("ffn2", h2b))
    ahead(1, "ffn1", h2b)
    ln3 = (W["ln3_g"], W["ln3_b"])
    u, v, r3, h3b = ffn_fwd_seq(r2, ln2, W["g2"], W["u2"], W["d2"], ln3)
    sv.update(u2=u, v2=v, r3=r3, W=W)
    return r3, ln3, h3b, sv


def ffn_bwd(parts, r, gamma, hb_in, u, v, wg, wu, wd, after=None):
    dh, dwg, dwu, dwd, dg, db = ffn_bwd_seq(parts, r, gamma, hb_in, u, v, wg, wu, wd, after)
    return dh, dict(d=dwd, g=dwg, u=dwu, ln_g=dg, ln_b=db)


def layer_bwd(parts, sv, emit, tabs, after):
    G = {}
    W = sv["W"]
    dh2, g2 = ffn_bwd(parts, sv["r3"], W["ln3_g"], sv["h2b"], sv["u2"], sv["v2"], W["g2"], W["u2"], W["d2"], after)
    G.update(g2=g2["g"], u2=g2["u"], d2=g2["d"], ln3_g=g2["ln_g"], ln3_b=g2["ln_b"])
    tok = emit("ffn2", G)
    dr2, dmixb, G["ln2_g"], G["ln2_b"] = ln_bwd([(dh2, 1.0)], sv["r2"], W["ln2_g"], 1.0, tok)
    dmc = mm_nt_reduce([(dmixb[None], W["w_out"][None])], D)
    G["w_out"] = mm_tn(sv["mixcat"][None], dmixb[None])[0]
    proj = sv["proj"]
    dxa, dz, dsm, G["normg"], G["dskip"], G["alog"], G["dtb"] = ssd_bwd(
        dmc, sv["xa"], proj, sv["sprev"], W["dtb"], W["alog"], W["dskip"], W["normg"])
    dxbc, G["conv_w"], G["conv_b"] = conv_bwd(dxa, proj, W["conv_w"], W["conv_b"])
    dfq, dfk, dfv, dcq, dck = attn_bwd(proj, proj, proj, dmc, sv["lse_f"], sv["mixcat"], C_FQ // 256, C_FK // 256,
                                       C_FV // 256, 2, 2, FOX_H, FOX_DH, FOX_DH, FOX_DH ** -0.5, sv["c_keys"], sv["c_row"])
    dsm, G["fb"] = fox_pre_bwd(dcq, dck, proj, W["fb"], dsm)
    dq, dk, dv = attn_bwd(sv["q"], sv["k"], sv["v"], dmc, sv["lse_m"], sv["mixcat"], 0, 0, 0, 3, 3, MLA_H, BLK, MLA_V,
                          (MLA_NOPE + MLA_ROPE) ** -0.5)
    dcql, dckv, dsm, G["wq"], G["wk"], G["wv"], G["qg"], G["kvg"] = mla_pre_bwd(
        dq, dk, dv, proj, sv["cqn"], sv["ckvn"], W["qg"], W["kvg"], W["wq"], W["wk"], W["wv"], *tabs, dsm)
    dproj = jnp.concatenate([dz, dxbc, dfq, dfk, dfv, dcql, dckv, dsm], axis=1).astype(BF16)
    dh1p = mm_nt_reduce([(dproj[None], W["w_in"][None])], D)
    G["w_in"] = mm_tn(sv["h1b"][None], dproj[None])[0]
    tok = emit("mix", G)
    dh0, g1 = ffn_bwd([(dr2, ALPHA), (dh1p, 1.0)], sv["r1"], W["ln1_g"], sv["h0b"], sv["u1"], sv["v1"],
                      W["g1"], W["u1"], W["d1"], tok)
    G.update(g1=g1["g"], u1=g1["u"], d1=g1["d"], ln1_g=g1["ln_g"], ln1_b=g1["ln_b"])
    tok = emit("ffn1", G)
    return [(dh0, 1.0)], G, tok


def local_step(x, target, meta_full, getw, emit, ahead=lambda l, stage, after: None):
    t = x.shape[0] + BLK
    tabs = rope_tables(t)
    xr, hb = build_h0(meta_full, x)
    ln = None
    saved = []
    for l in range(NL):
        xr, ln, hb, sv = layer_fwd(xr, ln, hb, functools.partial(getw, l), tabs,
                                   lambda dl, stage, after, l=l: ahead(l + dl, stage, after))
        saved.append(sv)
    dy, loss = loss_head(xr, ln, target)
    parts = [(dy, 1.0)]
    grads = [None] * NL
    tok = None
    for l in range(NL - 1, -1, -1):
        parts, grads[l], tok = layer_bwd(parts, saved[l], functools.partial(emit, l), tabs, tok)
    gx, gmeta = split_dh0(parts[0][0], tok)
    return loss, gx, gmeta, grads


_SMALL = ["ln1_g", "ln1_b", "ln2_g", "ln2_b", "ln3_g", "ln3_b", "conv_b", "ssd_norm_g", "mla_q_norm_g",
          "mla_kv_norm_g", "dt_bias", "a_log", "d_skip", "fox_f_b"]
_SMALL_ROWS = 16
_BIG = ["ffn1_w_gate", "ffn1_w_up", "ffn1_w_down", "w_in", "conv_w", "mla_w_uq", "mla_w_ukv", "w_out",
        "ffn2_w_gate", "ffn2_w_up", "ffn2_w_down"]
_NAMES = ["meta", "ffn1_w_gate", "ffn1_w_up", "ffn1_w_down", "ln1_g", "ln1_b", "w_in", "conv_w", "conv_b", "dt_bias",
          "a_log", "d_skip", "ssd_norm_g", "fox_f_b", "mla_q_norm_g", "mla_w_uq", "mla_kv_norm_g", "mla_w_ukv", "w_out",
          "ln2_g", "ln2_b", "ffn2_w_gate", "ffn2_w_up", "ffn2_w_down", "ln3_g", "ln3_b"]


def pack_small(p):
    rows = []
    for l in range(NL):
        for n in _SMALL:
            rows.append(_pad_cols(p[n][l][None, :].astype(F32), D))
        rows.append(jnp.zeros((_SMALL_ROWS - len(_SMALL), D), F32))
    return jnp.concatenate(rows, axis=0)


def unpack_small(a, like):
    out = {}
    for i, n in enumerate(_SMALL):
        out[n] = jnp.stack([a[l * _SMALL_ROWS + i, :like[n].shape[1]] for l in range(NL)], axis=0)
    return out


_STAGES = {"ffn1": ["ffn1_w_gate", "ffn1_w_up", "ffn1_w_down"],
           "mix": ["w_in", "conv_w", "mla_w_uq", "mla_w_ukv", "w_out"],
           "ffn2": ["ffn2_w_gate", "ffn2_w_up", "ffn2_w_down"]}


_FFN_T = ("ffn1_w_gate", "ffn1_w_up", "ffn2_w_gate", "ffn2_w_up")


def stage_weights(l, stage, g, rep):
    if stage != "mix":
        i = stage[3]
        return {"g" + i: g[f"ffn{i}_w_gate"].reshape(D_FF, D), "u" + i: g[f"ffn{i}_w_up"].reshape(D_FF, D),
                "d" + i: g[f"ffn{i}_w_down"].reshape(D_FF, D),
                "ln1_g" if i == "1" else "ln3_g": rep["ln1_g" if i == "1" else "ln3_g"][l][None, :],
                "ln1_b" if i == "1" else "ln3_b": rep["ln1_b" if i == "1" else "ln3_b"][l][None, :]}
    W = {}
    W["w_in"] = g["w_in"].reshape(D, N_INP)
    W["w_out"] = g["w_out"].reshape(D, D)
    W["wq"], W["wk"], W["wv"] = mla_weights(g["mla_w_uq"], g["mla_w_ukv"])
    W["conv_w"] = _unshard_cols(g["conv_w"])
    for k in ("ln2_g", "ln2_b", "conv_b"):
        W[k] = rep[k][l][None, :]
    W["normg"] = rep["ssd_norm_g"][l][None, :]
    W["qg"] = rep["mla_q_norm_g"][l][None, :]
    W["kvg"] = rep["mla_kv_norm_g"][l][None, :]
    W["dtb"] = _lanes(rep["dt_bias"][l], SM_DT)
    W["alog"] = _lanes(rep["a_log"][l], SM_DT)
    W["dskip"] = _lanes(rep["d_skip"][l], SM_DT)
    W["fb"] = _lanes(rep["fox_f_b"][l], SM_F)
    return W


def small_grads(G):
    return {"ln1_g": G["ln1_g"][0], "ln1_b": G["ln1_b"][0], "ln2_g": G["ln2_g"][0], "ln2_b": G["ln2_b"][0],
            "ln3_g": G["ln3_g"][0], "ln3_b": G["ln3_b"][0], "conv_b": G["conv_b"][0], "ssd_norm_g": G["normg"][0],
            "mla_q_norm_g": G["qg"][0], "mla_kv_norm_g": G["kvg"][0], "dt_bias": G["dtb"][0, :SSD_H],
            "a_log": G["alog"][0, :SSD_H], "d_skip": G["dskip"][0, :SSD_H], "fox_f_b": G["fb"][0, SM_F:SM_F + FOX_H]}


def big_grads(G, stage):
    if stage != "mix":
        i = stage[-1]
        return {f"ffn{i}_w_{k}": G[k[0] + i].reshape(N_DEV, HS, D) for k in ("gate", "up", "down")}
    duq, dukv = mla_weight_grads(G["wq"], G["wk"], G["wv"])
    return {"w_in": G["w_in"].reshape(N_DEV, D // N_DEV, N_INP), "w_out": G["w_out"].reshape(N_DEV, D // N_DEV, D),
            "mla_w_uq": duq, "mla_w_ukv": dukv, "conv_w": _shard_cols(G["conv_w"])}


def kernel(x, meta, ffn1_w_gate, ffn1_w_up, ffn1_w_down, ln1_g, ln1_b, w_in, conv_w, conv_b, dt_bias, a_log, d_skip, ssd_norm_g, fox_f_b, mla_q_norm_g, mla_w_uq, mla_kv_norm_g, mla_w_ukv, w_out, ln2_g, ln2_b, ffn2_w_gate, ffn2_w_up, ffn2_w_down, ln3_g, ln3_b, loss_target, m_meta, m_ffn1_w_gate, m_ffn1_w_up, m_ffn1_w_down, m_ln1_g, m_ln1_b, m_w_in, m_conv_w, m_conv_b, m_dt_bias, m_a_log, m_d_skip, m_ssd_norm_g, m_fox_f_b, m_mla_q_norm_g, m_mla_w_uq, m_mla_kv_norm_g, m_mla_w_ukv, m_w_out, m_ln2_g, m_ln2_b, m_ffn2_w_gate, m_ffn2_w_up, m_ffn2_w_down, m_ln3_g, m_ln3_b, v_meta, v_ffn1_w_gate, v_ffn1_w_up, v_ffn1_w_down, v_ln1_g, v_ln1_b, v_w_in, v_conv_w, v_conv_b, v_dt_bias, v_a_log, v_d_skip, v_ssd_norm_g, v_fox_f_b, v_mla_q_norm_g, v_mla_w_uq, v_mla_kv_norm_g, v_mla_w_ukv, v_w_out, v_ln2_g, v_ln2_b, v_ffn2_w_gate, v_ffn2_w_up, v_ffn2_w_down, v_ln3_g, v_ln3_b):
    vals = (meta, ffn1_w_gate, ffn1_w_up, ffn1_w_down, ln1_g, ln1_b, w_in, conv_w, conv_b, dt_bias, a_log, d_skip, ssd_norm_g, fox_f_b, mla_q_norm_g, mla_w_uq, mla_kv_norm_g, mla_w_ukv, w_out, ln2_g, ln2_b, ffn2_w_gate, ffn2_w_up, ffn2_w_down, ln3_g, ln3_b)
    moms = (m_meta, m_ffn1_w_gate, m_ffn1_w_up, m_ffn1_w_down, m_ln1_g, m_ln1_b, m_w_in, m_conv_w, m_conv_b, m_dt_bias, m_a_log, m_d_skip, m_ssd_norm_g, m_fox_f_b, m_mla_q_norm_g, m_mla_w_uq, m_mla_kv_norm_g, m_mla_w_ukv, m_w_out, m_ln2_g, m_ln2_b, m_ffn2_w_gate, m_ffn2_w_up, m_ffn2_w_down, m_ln3_g, m_ln3_b)
    vars_ = (v_meta, v_ffn1_w_gate, v_ffn1_w_up, v_ffn1_w_down, v_ln1_g, v_ln1_b, v_w_in, v_conv_w, v_conv_b, v_dt_bias, v_a_log, v_d_skip, v_ssd_norm_g, v_fox_f_b, v_mla_q_norm_g, v_mla_w_uq, v_mla_kv_norm_g, v_mla_w_ukv, v_w_out, v_ln2_g, v_ln2_b, v_ffn2_w_gate, v_ffn2_w_up, v_ffn2_w_down, v_ln3_g, v_ln3_b)
    P = dict(zip(_NAMES, vals))
    M = dict(zip(_NAMES, moms))
    V = dict(zip(_NAMES, vars_))
    me = 4 * lax.axis_index("x") + 2 * lax.axis_index("y") + lax.axis_index("c")

    me_arr = me.astype(jnp.int32).reshape(1)
    for n in _FFN_T:
        P[n], M[n], V[n] = (jnp.swapaxes(a[n], 1, 2) for a in (P, M, V))
    src = dict(P)
    src["w_in"] = w_in_to_padded(P["w_in"])
    order = [("meta", 0)] + [(n, l) for l in range(NL) for names in _STAGES.values() for n in names]
    zone_of = {nl_: i for i, nl_ in enumerate(order)}
    zones = [place_own(P["meta"][None], 0, F32, me_arr)]
    zones += [place_own(src[n], l, F32 if n == "conv_w" else BF16, me_arr) for n, l in order[1:]]
    hg = gather_start(zones, "gather_start")
    relays = {}

    def ahead(l, stage, after):
        if l < NL and (l, stage) not in relays:
            idxs = [0] if stage == "meta" else [zone_of[(n, l)] for n in _STAGES[stage]]
            relays[(l, stage)] = (idxs, gather_relay(hg, idxs, f"gather_relay_{l}_{stage}", after))

    def arrived(l, stage, after):
        ahead(l, stage, after)
        idxs, rl = relays[(l, stage)]
        return gather_wait(hg, rl, idxs, f"gather_wait_{l}_{stage}", after)

    meta_full = _unshard_cols(arrived(0, "meta", hg["token"])[0])

    def getw(l, stage, after):
        return stage_weights(l, stage, dict(zip(_STAGES[stage], arrived(l, stage, after))), P)

    sent = {}

    def emit(l, stage, G):
        bg = big_grads(G, stage)
        sent[(l, stage)] = exchange_start("scatter", [bg[n] for n in _STAGES[stage]], f"scatter_start_{l}_{stage}")
        return sent[(l, stage)]["token"]

    loss, gx, gmeta, grads = local_step(x[0], loss_target[0], meta_full, getw, emit, ahead)

    small = jnp.concatenate([pack_small({n: jnp.stack([small_grads(g)[n] for g in grads]) for n in _SMALL}), gmeta], axis=0)
    hs = exchange_start("gather", [place_own(small[None], 0, F32, me_arr)], "small_start")

    out = {}
    after = hs["token"]
    for stage in ("ffn2", "mix", "ffn1"):
        names = _STAGES[stage]
        got = [exchange_wait(sent[(l, stage)], list(range(len(names))), f"scatter_wait_{l}_{stage}", after)
               for l in range(NL - 1, -1, -1)][::-1]
        for i, n in enumerate(names):
            own = [got[l][0][i] for l in range(NL)]
            recv = [got[l][1][i] for l in range(NL)]
            if n == "w_in":
                g = jnp.stack([w_in_from_padded(sum_slots(recv[l], own[l], me_arr)) for l in range(NL)])
                out[n] = (g,) + adamw(P[n], M[n], V[n], g=g)
            else:
                out[n] = adamw(P[n], M[n], V[n], recv=recv, own=own, me_arr=me_arr)
                if n in _FFN_T:
                    out[n] = tuple(jnp.swapaxes(a, 1, 2) for a in out[n])
        after = out[names[-1]][1]
    gsmall = sum_slots(exchange_wait(hs, [0], "small_wait", after)[1][0])
    gm = lax.dynamic_slice(gsmall[NL * _SMALL_ROWS:], (0, me * (D // N_DEV)), (N_META, D // N_DEV))
    out["meta"] = (gm,) + adamw(P["meta"], M["meta"], V["meta"], g=gm)
    gs = gsmall[:NL * _SMALL_ROWS]
    sd, sm_, sv_ = adamw(pack_small(P), pack_small(M), pack_small(V), g=gs)
    ups = [unpack_small(a, P) for a in (gs, sd, sm_, sv_)]
    for n in _SMALL:
        out[n] = tuple(u[n] for u in ups)

    loss_all = lax.psum(loss[0, 0], ("x", "y", "c"))
    flat = [loss_all, gx[None]]
    for k in range(4):
        flat += [out[n][k] for n in _NAMES]
    return tuple(flat)
```

```python
import functools

import jax
import jax.numpy as jnp
from jax import lax
from jax.experimental import pallas as pl
from jax.experimental.pallas import tpu as pltpu

F32, BF16 = jnp.float32, jnp.bfloat16
HI = lax.Precision.HIGHEST

N_DEV = 8
D = 1024
NL = 2
N_META = 16
BLK = 128
PAD = BLK - N_META
D_FF = 2816
HS = D_FF // N_DEV
SSD_H, SSD_P, SSD_N, SSD_G = 8, 64, 64, 2
SSD_D = SSD_H * SSD_P
CONV_K = 4
CONV_D = SSD_D + 2 * SSD_G * SSD_N
FOX_H, FOX_DH = 4, 64
MLA_H, MLA_QL, MLA_KVL, MLA_NOPE, MLA_ROPE, MLA_V = 4, 256, 128, 64, 32, 64
N_IN = 2476
C_Z, C_XBC, C_FQ, C_FK, C_FV, C_CQ, C_CKV, C_SM, N_INP = 0, 512, 1280, 1536, 1792, 2048, 2304, 2432, 2560
SM_DT, SM_F, SM_KR = 0, 8, 64
ALPHA = (2 * NL) ** 0.25
EPS = 1e-5
NEG = -1e30
LR, B1, B2, AEPS, WD, STEP = 0.001, 0.9, 0.999, 1e-08, 0.01, 10
VMEM_MB = 56


def _cp(*sem):
    return pltpu.CompilerParams(dimension_semantics=sem, vmem_limit_bytes=VMEM_MB << 20)


def _nn(a, b):
    return lax.dot_general(a, b, (((1,), (0,)), ((), ())), preferred_element_type=F32)


def _nt(a, b):
    return lax.dot_general(a, b, (((1,), (1,)), ((), ())), preferred_element_type=F32)


def _tn(a, b):
    return lax.dot_general(a, b, (((0,), (0,)), ((), ())), preferred_element_type=F32)


def _nn_hi(a, b):
    return lax.dot_general(a, b, (((1,), (0,)), ((), ())), precision=HI, preferred_element_type=F32)


def _row_tile(t):
    for d in range(640, 15, -16):
        if t % d == 0:
            return d
    raise ValueError(t)


def _sig(x):
    return 1.0 / (1.0 + jnp.exp(-x))


def _tri(lower=True):
    r = lax.broadcasted_iota(jnp.int32, (BLK, BLK), 0)
    c = lax.broadcasted_iota(jnp.int32, (BLK, BLK), 1)
    return (r >= c) if lower else (r <= c)


def build_h0(meta_full, x):
    s = x.shape[0]
    nb = s // BLK + 1

    def body(m_ref, x_ref, h_ref, hb_ref):
        i = pl.program_id(0)

        @pl.when(i == 0)
        def _():
            h = jnp.concatenate([jnp.zeros((PAD, D), F32), m_ref[...]], axis=0)
            h_ref[...] = h
            hb_ref[...] = h.astype(BF16)

        @pl.when(i > 0)
        def _():
            h_ref[...] = x_ref[...]
            hb_ref[...] = x_ref[...].astype(BF16)

    return pl.pallas_call(
        body, name="build_h0", grid=(nb,),
        in_specs=[pl.BlockSpec((N_META, D), lambda i: (0, 0)),
                  pl.BlockSpec((BLK, D), lambda i: (jnp.maximum(i - 1, 0), 0))],
        out_specs=[pl.BlockSpec((BLK, D), lambda i: (i, 0))] * 2,
        out_shape=[jax.ShapeDtypeStruct((nb * BLK, D), F32), jax.ShapeDtypeStruct((nb * BLK, D), BF16)],
        compiler_params=_cp("arbitrary"),
    )(meta_full, x)


FT = 256


def _layer_norm(r, gamma, beta):
    mu = jnp.mean(r, axis=1, keepdims=True)
    xc = r - mu
    var = jnp.mean(xc * xc, axis=1, keepdims=True)
    return xc * lax.rsqrt(var + EPS) * gamma + beta


def ffn_fwd(hb, res, wg, wu, wd, gamma, beta):
    t = hb.shape[0]
    f = wg.shape[0]
    tm = _row_tile(t)
    nj = f // FT

    def body(h_ref, res_ref, wg_ref, wu_ref, wd_ref, g_ref, be_ref, u_ref, v_ref, r_ref, y_ref, yb_ref, acc, us, vs):
        j = pl.program_id(1)

        def up():
            h = h_ref[...]
            u = _nt(h, wg_ref[...])
            v = _nt(h, wu_ref[...])
            u_ref[...] = u.astype(BF16)
            v_ref[...] = v.astype(BF16)
            return u, v

        def down():
            u, v = us[...], vs[...]
            return _nn((u * _sig(u) * v).astype(BF16), wd_ref[...])

        @pl.when(j == 0)
        def _():
            us[...], vs[...] = up()
            acc[...] = jnp.zeros_like(acc)

        @pl.when((j > 0) & (j < nj))
        def _():
            d = down()
            u, v = up()
            acc[...] += d
            us[...] = u
            vs[...] = v

        @pl.when(j == nj)
        def _():
            r = ALPHA * res_ref[...] + 0.5 * (acc[...] + down())
            y = _layer_norm(r, g_ref[...], be_ref[...])
            r_ref[...] = r
            y_ref[...] = y
            yb_ref[...] = y.astype(BF16)

    row = pl.BlockSpec((tm, D), lambda i, j: (i, 0))
    vec = pl.BlockSpec((1, D), lambda i, j: (0, 0))
    wup = pl.BlockSpec((FT, D), lambda i, j: (jnp.minimum(j, nj - 1), 0))
    wdn = pl.BlockSpec((FT, D), lambda i, j: (jnp.maximum(j - 1, 0), 0))
    act = pl.BlockSpec((tm, FT), lambda i, j: (i, jnp.minimum(j, nj - 1)))
    return pl.pallas_call(
        body, name="ffn_fwd", grid=(t // tm, nj + 1),
        in_specs=[row, row, wup, wup, wdn, vec, vec],
        out_specs=[act, act, row, row, row],
        out_shape=[jax.ShapeDtypeStruct((t, f), BF16), jax.ShapeDtypeStruct((t, f), BF16),
                   jax.ShapeDtypeStruct((t, D), F32), jax.ShapeDtypeStruct((t, D), F32),
                   jax.ShapeDtypeStruct((t, D), BF16)],
        scratch_shapes=[pltpu.VMEM((tm, D), F32), pltpu.VMEM((tm, FT), F32), pltpu.VMEM((tm, FT), F32)],
        compiler_params=_cp("arbitrary", "arbitrary"),
    )(hb, res, wg, wu, wd, gamma, beta)


def ffn_bwd_act(dfb, u, v, wg, wu, wd):
    t, f = u.shape
    tm = _row_tile(t)

    nj = f // FT

    def body(df_ref, u_ref, v_ref, wg_ref, wu_ref, wd_ref, du_ref, dv_ref, dh_ref, das):
        j = pl.program_id(1)

        def first():
            return _nt(df_ref[...], wd_ref[...])

        def second():
            da = das[...]
            uu = u_ref[...].astype(F32)
            sg = _sig(uu)
            du = (da * v_ref[...].astype(F32) * (sg * (1.0 + uu * (1.0 - sg)))).astype(BF16)
            dv = (da * uu * sg).astype(BF16)
            du_ref[...] = du
            dv_ref[...] = dv
            return _nn(du, wg_ref[...]) + _nn(dv, wu_ref[...])

        @pl.when(j == 0)
        def _():
            das[...] = first()
            dh_ref[...] = jnp.zeros_like(dh_ref)

        @pl.when((j > 0) & (j < nj))
        def _():
            tot = second()
            da = first()
            dh_ref[...] += tot
            das[...] = da

        @pl.when(j == nj)
        def _():
            dh_ref[...] += second()

    row = pl.BlockSpec((tm, D), lambda i, j: (i, 0))
    wfirst = pl.BlockSpec((FT, D), lambda i, j: (jnp.minimum(j, nj - 1), 0))
    wsecond = pl.BlockSpec((FT, D), lambda i, j: (jnp.maximum(j - 1, 0), 0))
    act = pl.BlockSpec((tm, FT), lambda i, j: (i, jnp.maximum(j - 1, 0)))
    return pl.pallas_call(
        body, name="ffn_bwd_act", grid=(t // tm, nj + 1),
        in_specs=[row, act, act, wsecond, wsecond, wfirst],
        out_specs=[act, act, row],
        out_shape=[jax.ShapeDtypeStruct((t, f), BF16), jax.ShapeDtypeStruct((t, f), BF16),
                   jax.ShapeDtypeStruct((t, D), F32)],
        scratch_shapes=[pltpu.VMEM((tm, FT), F32)],
        compiler_params=_cp("arbitrary", "arbitrary"),
    )(dfb, u, v, wg, wu, wd)


def ffn_fwd_seq(x, ln_in, wg, wu, wd, ln_out):
    t = x.shape[0]
    f = wg.shape[0]
    nj, nr = f // FT, t // _row_tile(t)
    rc = t // nr
    plain = ln_in is None
    gi, bi = ln_out if plain else ln_in

    def body(x_hbm, gi_ref, bi_ref, go_ref, bo_ref, wg_ref, wu_ref, wd_ref, u_ref, v_ref, r_hbm, yb_hbm,
             acc, hbs, xbuf, sem_in, sem_out):
        j = pl.program_id(0)

        @pl.when(j == 0)
        def _():
            def fetch(k):
                return pltpu.make_async_copy(x_hbm.at[pl.ds(k * rc, rc)], xbuf.at[k % 2], sem_in.at[k % 2])

            fetch(0).start()
            for k in range(nr):
                if k + 1 < nr:
                    fetch(k + 1).start()
                fetch(k).wait()
                h = xbuf[k % 2]
                if not plain:
                    h = _layer_norm(h, gi_ref[...], bi_ref[...])
                acc[k * rc:(k + 1) * rc, :] = ALPHA * h
                hbs[k * rc:(k + 1) * rc, :] = h.astype(BF16)

        for k in range(nr):
            sl = slice(k * rc, (k + 1) * rc)
            h = hbs[sl, :]
            u = _nt(h, wg_ref[...])
            v = _nt(h, wu_ref[...])
            u_ref[sl, :] = u.astype(BF16)
            v_ref[sl, :] = v.astype(BF16)
            acc[sl, :] += _nn((0.5 * u * _sig(u) * v).astype(BF16), wd_ref[...])

        @pl.when(j == nj - 1)
        def _():
            r_cp = pltpu.make_async_copy(acc, r_hbm, sem_out.at[0])
            r_cp.start()
            for k in range(nr):
                sl = slice(k * rc, (k + 1) * rc)
                hbs[sl, :] = _layer_norm(acc[sl, :], go_ref[...], bo_ref[...]).astype(BF16)
            y_cp = pltpu.make_async_copy(hbs, yb_hbm, sem_out.at[1])
            y_cp.start()
            r_cp.wait()
            y_cp.wait()

    vec = pl.BlockSpec((1, D), lambda j: (0, 0))
    wsp = pl.BlockSpec((FT, D), lambda j: (j, 0))
    act = pl.BlockSpec((None, t, FT), lambda j: (j, 0, 0))
    return pl.pallas_call(
        body, name="ffn_fwd_seq", grid=(nj,),
        in_specs=[_ANY, vec, vec, vec, vec, wsp, wsp, wsp],
        out_specs=[act, act, _ANY, _ANY],
        out_shape=[jax.ShapeDtypeStruct((nj, t, FT), BF16), jax.ShapeDtypeStruct((nj, t, FT), BF16),
                   jax.ShapeDtypeStruct((t, D), F32), jax.ShapeDtypeStruct((t, D), BF16)],
        scratch_shapes=[pltpu.VMEM((t, D), F32), pltpu.VMEM((t, D), BF16), pltpu.VMEM((2, rc, D), F32),
                        pltpu.SemaphoreType.DMA((2,)), pltpu.SemaphoreType.DMA((2,))],
        compiler_params=_cp("arbitrary"),
    )(x, gi, bi, ln_out[0], ln_out[1], wg, wu, wd)


def ffn_bwd_seq(parts, r, gamma, hb, u, v, wg, wu, wd, after=None):
    nj, t, _ = u.shape
    f = nj * FT
    nr = t // _row_tile(t)
    rc = t // nr
    nc = t // BLK
    scales = [s for _, s in parts]
    npart = len(parts)
    extra = [] if after is None else [after]

    def body(*refs):
        refs = refs[len(extra):]
        p_hbm, refs = refs[:npart], refs[npart:]
        (r_hbm, g_ref, hb_hbm, u_ref, v_ref, wg_ref, wu_ref, wd_ref, dh_hbm, dwg_ref, dwu_ref, dwd_ref, dg_ref, db_ref,
         dfs, hbt, dft, dhacc, dus, dvs, acs, pbuf, rbuf, hbuf, sems, sem_out) = refs
        j = pl.program_id(0)

        @pl.when(j == 0)
        def _():
            def fetch(c):
                rows = pl.ds(c * BLK, BLK)
                cps = [pltpu.make_async_copy(p_hbm[p].at[rows], pbuf.at[c % 2, p], sems.at[c % 2, p]) for p in range(npart)]
                cps.append(pltpu.make_async_copy(r_hbm.at[rows], rbuf.at[c % 2], sems.at[c % 2, npart]))
                cps.append(pltpu.make_async_copy(hb_hbm.at[rows], hbuf.at[c % 2], sems.at[c % 2, npart + 1]))
                return cps

            for cp in fetch(0):
                cp.start()
            dg = jnp.zeros((1, D), F32)
            db = jnp.zeros((1, D), F32)
            for c in range(nc):
                if c + 1 < nc:
                    for cp in fetch(c + 1):
                        cp.start()
                for cp in fetch(c):
                    cp.wait()
                sl = slice(c * BLK, (c + 1) * BLK)
                dy = scales[0] * pbuf[c % 2, 0]
                for p in range(1, npart):
                    dy += scales[p] * pbuf[c % 2, p]
                rr = rbuf[c % 2]
                xc = rr - jnp.mean(rr, axis=1, keepdims=True)
                rstd = lax.rsqrt(jnp.mean(xc * xc, axis=1, keepdims=True) + EPS)
                xh = xc * rstd
                dxh = dy * g_ref[...]
                dr = rstd * (dxh - jnp.mean(dxh, axis=1, keepdims=True) - xh * jnp.mean(dxh * xh, axis=1, keepdims=True))
                dg += jnp.sum(dy * xh, axis=0, keepdims=True)
                db += jnp.sum(dy, axis=0, keepdims=True)
                dhacc[sl, :] = ALPHA * dr
                dfc = (0.5 * dr).astype(BF16)
                dfs[sl, :] = dfc
                dft[:, sl] = dfc.T
                hbt[:, sl] = hbuf[c % 2].T
            dg_ref[...] = dg
            db_ref[...] = db

        for k in range(nr):
            sl = slice(k * rc, (k + 1) * rc)
            da = _nt(dfs[sl, :], wd_ref[...])
            uu = u_ref[sl, :].astype(F32)
            vv = v_ref[sl, :].astype(F32)
            sg = _sig(uu)
            du = (da * vv * (sg * (1.0 + uu * (1.0 - sg)))).astype(BF16)
            dv = (da * uu * sg).astype(BF16)
            dus[sl, :] = du
            dvs[sl, :] = dv
            acs[sl, :] = (uu * sg * vv).astype(BF16)
            dhacc[sl, :] += _nn(du, wg_ref[...]) + _nn(dv, wu_ref[...])
        dwg_ref[...] = _nn(hbt[...], dus[...]).astype(BF16).T
        dwu_ref[...] = _nn(hbt[...], dvs[...]).astype(BF16).T
        dwd_ref[...] = _nn(dft[...], acs[...]).astype(BF16).T

        @pl.when(j == nj - 1)
        def _():
            cp = pltpu.make_async_copy(dhacc, dh_hbm, sem_out.at[0])
            cp.start()
            cp.wait()

    vec = pl.BlockSpec((1, D), lambda j: (0, 0))
    wsp = pl.BlockSpec((FT, D), lambda j: (j, 0))
    act = pl.BlockSpec((None, t, FT), lambda j: (j, 0, 0))
    return pl.pallas_call(
        body, name="ffn_bwd_seq", grid=(nj,),
        in_specs=[_ANY] * (len(extra) + npart + 1) + [vec, _ANY, act, act, wsp, wsp, wsp],
        out_specs=[_ANY, wsp, wsp, wsp, vec, vec],
        out_shape=[jax.ShapeDtypeStruct((t, D), F32)] + [jax.ShapeDtypeStruct((f, D), BF16)] * 3
        + [jax.ShapeDtypeStruct((1, D), F32)] * 2,
        scratch_shapes=[pltpu.VMEM((t, D), BF16), pltpu.VMEM((D, t), BF16), pltpu.VMEM((D, t), BF16),
                        pltpu.VMEM((t, D), F32), pltpu.VMEM((t, FT), BF16), pltpu.VMEM((t, FT), BF16),
                        pltpu.VMEM((t, FT), BF16), pltpu.VMEM((2, npart, BLK, D), F32), pltpu.VMEM((2, BLK, D), F32),
                        pltpu.VMEM((2, BLK, D), BF16), pltpu.SemaphoreType.DMA((2, npart + 2)),
                        pltpu.SemaphoreType.DMA((1,))],
        compiler_params=_cp("arbitrary"),
    )(*extra, *[p for p, _ in parts], r, gamma, hb, u, v, wg, wu, wd)


def mm_res_ln(a, b, x, ln_in, ln_out):
    t, k = a.shape
    tm = _row_tile(t)

    def body(a_ref, b_ref, x_ref, gi_ref, bi_ref, go_ref, bo_ref, r_ref, yb_ref):
        r = ALPHA * _layer_norm(x_ref[...], gi_ref[...], bi_ref[...]) + _nn(a_ref[...], b_ref[...])
        r_ref[...] = r
        yb_ref[...] = _layer_norm(r, go_ref[...], bo_ref[...]).astype(BF16)

    row = pl.BlockSpec((tm, D), lambda i: (i, 0))
    vec = pl.BlockSpec((1, D), lambda i: (0, 0))
    return pl.pallas_call(
        body, name="mm_res_ln", grid=(t // tm,),
        in_specs=[pl.BlockSpec((tm, k), lambda i: (i, 0)), pl.BlockSpec((k, D), lambda i: (0, 0)), row, vec, vec, vec, vec],
        out_specs=[row, row],
        out_shape=[jax.ShapeDtypeStruct((t, D), F32), jax.ShapeDtypeStruct((t, D), BF16)],
        compiler_params=_cp("arbitrary"),
    )(a, b, x, ln_in[0], ln_in[1], ln_out[0], ln_out[1])


def mm_nn(a, b):
    t, k = a.shape
    n = tn = b.shape[1]
    tm = _row_tile(t)

    def body(a_ref, b_ref, o_ref):
        o_ref[...] = _nn(a_ref[...], b_ref[...])

    return pl.pallas_call(
        body, name="mm_nn", grid=(t // tm, n // tn),
        in_specs=[pl.BlockSpec((tm, k), lambda i, j: (i, 0)), pl.BlockSpec((k, tn), lambda i, j: (0, j))],
        out_specs=pl.BlockSpec((tm, tn), lambda i, j: (i, j)),
        out_shape=jax.ShapeDtypeStruct((t, n), F32),
        compiler_params=_cp("arbitrary", "arbitrary"),
    )(a, b)


def mm_nt_reduce(pairs, n):
    g, t, _ = pairs[0][0].shape
    tm = _row_tile(t)
    npair = len(pairs)

    def body(*refs):
        o_ref = refs[-1]
        gi = pl.program_id(1)
        tot = _nt(refs[0][...], refs[1][...])
        for p in range(1, npair):
            tot += _nt(refs[2 * p][...], refs[2 * p + 1][...])

        @pl.when(gi == 0)
        def _():
            o_ref[...] = tot

        @pl.when(gi > 0)
        def _():
            o_ref[...] += tot

    in_specs, args = [], []
    for x, w in pairs:
        k = x.shape[2]
        in_specs += [pl.BlockSpec((None, tm, k), lambda i, gi: (gi, i, 0)),
                     pl.BlockSpec((None, n, k), lambda i, gi: (gi, 0, 0))]
        args += [x, w]
    return pl.pallas_call(
        body, name="mm_nt_reduce", grid=(t // tm, g),
        in_specs=in_specs, out_specs=pl.BlockSpec((tm, n), lambda i, gi: (i, 0)),
        out_shape=jax.ShapeDtypeStruct((t, n), F32),
        compiler_params=_cp("arbitrary", "arbitrary"),
    )(*args)


def mm_tn(x, y, out_dtype=BF16):
    gx, t, k = x.shape
    gy, _, n = y.shape
    g = max(gx, gy)
    tm = _row_tile(t)
    nt = t // tm

    def body(x_ref, y_ref, o_ref, acc):
        i = pl.program_id(1)

        @pl.when(i == 0)
        def _():
            acc[...] = jnp.zeros_like(acc)

        acc[...] += _tn(x_ref[...], y_ref[...])

        @pl.when(i == nt - 1)
        def _():
            o_ref[...] = acc[...].astype(out_dtype)

    return pl.pallas_call(
        body, name="mm_tn", grid=(g, nt),
        in_specs=[pl.BlockSpec((None, tm, k), (lambda gi, i: (gi, i, 0)) if gx > 1 else (lambda gi, i: (0, i, 0))),
                  pl.BlockSpec((None, tm, n), (lambda gi, i: (gi, i, 0)) if gy > 1 else (lambda gi, i: (0, i, 0)))],
        out_specs=pl.BlockSpec((None, k, n), lambda gi, i: (gi, 0, 0)),
        out_shape=jax.ShapeDtypeStruct((g, k, n), out_dtype),
        scratch_shapes=[pltpu.VMEM((k, n), F32)],
        compiler_params=_cp("arbitrary", "arbitrary"),
    )(x, y)


def ln_bwd(parts, r, gamma, out_scale, after=None):
    t = r.shape[0]
    tm = _row_tile(t)
    scales = [s for _, s in parts]
    npart = len(parts)
    extra = [] if after is None else [after]

    def body(*refs):
        refs = refs[len(extra):]
        r_ref, g_ref = refs[npart], refs[npart + 1]
        dr_ref, drb_ref, dg_ref, db_ref = refs[npart + 2:]
        i = pl.program_id(0)
        dy = scales[0] * refs[0][...]
        for p in range(1, npart):
            dy += scales[p] * refs[p][...]
        rr = r_ref[...]
        mu = jnp.mean(rr, axis=1, keepdims=True)
        xc = rr - mu
        rstd = lax.rsqrt(jnp.mean(xc * xc, axis=1, keepdims=True) + EPS)
        xh = xc * rstd
        dxh = dy * g_ref[...]
        m1 = jnp.mean(dxh, axis=1, keepdims=True)
        m2 = jnp.mean(dxh * xh, axis=1, keepdims=True)
        dr = rstd * (dxh - m1 - xh * m2)
        dr_ref[...] = dr
        drb_ref[...] = (out_scale * dr).astype(BF16)
        dg = jnp.sum(dy * xh, axis=0, keepdims=True)
        db = jnp.sum(dy, axis=0, keepdims=True)

        @pl.when(i == 0)
        def _():
            dg_ref[...] = dg
            db_ref[...] = db

        @pl.when(i > 0)
        def _():
            dg_ref[...] += dg
            db_ref[...] += db

    row = pl.BlockSpec((tm, D), lambda i: (i, 0))
    vec = pl.BlockSpec((1, D), lambda i: (0, 0))
    return pl.pallas_call(
        body, name="ln_bwd", grid=(t // tm,),
        in_specs=[_ANY] * len(extra) + [row] * (npart + 1) + [vec],
        out_specs=[row, row, vec, vec],
        out_shape=[jax.ShapeDtypeStruct((t, D), F32), jax.ShapeDtypeStruct((t, D), BF16),
                   jax.ShapeDtypeStruct((1, D), F32), jax.ShapeDtypeStruct((1, D), F32)],
        compiler_params=_cp("arbitrary"),
    )(*extra, *[p for p, _ in parts], r, gamma)


def loss_head(r, ln, target):
    t = r.shape[0]
    nb = t // BLK

    def body(r_ref, g_ref, b_ref, t_ref, dy_ref, l_ref):
        i = pl.program_id(0)

        @pl.when(i == 0)
        def _():
            dy_ref[...] = jnp.zeros_like(dy_ref)
            l_ref[...] = jnp.zeros_like(l_ref)

        @pl.when(i > 0)
        def _():
            err = _layer_norm(r_ref[...], g_ref[...], b_ref[...]) - t_ref[...]
            dy_ref[...] = err * (1.0 / D)
            l_ref[...] += (0.5 / D) * jnp.sum(err * err, keepdims=True)

    vec = pl.BlockSpec((1, D), lambda i: (0, 0))
    return pl.pallas_call(
        body, name="loss_head", grid=(nb,),
        in_specs=[pl.BlockSpec((BLK, D), lambda i: (i, 0)), vec, vec,
                  pl.BlockSpec((BLK, D), lambda i: (jnp.maximum(i - 1, 0), 0))],
        out_specs=[pl.BlockSpec((BLK, D), lambda i: (i, 0)), pl.BlockSpec((1, 1), lambda i: (0, 0))],
        out_shape=[jax.ShapeDtypeStruct((t, D), F32), jax.ShapeDtypeStruct((1, 1), F32)],
        compiler_params=_cp("arbitrary"),
    )(r, ln[0], ln[1], target)


def split_dh0(dh0, after=None):
    t = dh0.shape[0]
    nb = t // BLK
    extra = [] if after is None else [after]

    def body(*refs):
        a_ref, gx_ref, gm_ref = refs[len(extra):]
        i = pl.program_id(0)
        tot = a_ref[...]

        @pl.when(i == 0)
        def _():
            gm_ref[...] = tot[PAD:, :]

        @pl.when(i > 0)
        def _():
            gx_ref[...] = tot

    blk = pl.BlockSpec((BLK, D), lambda i: (i, 0))
    return pl.pallas_call(
        body, name="split_dh0", grid=(nb,),
        in_specs=[_ANY] * len(extra) + [blk],
        out_specs=[pl.BlockSpec((BLK, D), lambda i: (jnp.maximum(i - 1, 0), 0)),
                   pl.BlockSpec((N_META, D), lambda i: (0, 0))],
        out_shape=[jax.ShapeDtypeStruct((t - BLK, D), F32), jax.ShapeDtypeStruct((N_META, D), F32)],
        compiler_params=_cp("arbitrary"),
    )(*extra, dh0)


def _valid_rows(nrows, first_row):
    return (first_row + lax.broadcasted_iota(jnp.int32, (nrows, 1), 0)) >= PAD


def conv_fwd(proj, conv_w, conv_b):
    t = proj.shape[0]
    c0 = C_XBC // BLK

    def body(x_ref, w_ref, b_ref, o_ref):
        ok = _valid_rows(t, 0)
        x = jnp.where(ok, x_ref[...], 0.0)
        w = w_ref[...]
        acc = b_ref[...] + w[CONV_K - 1:CONV_K, :] * x
        for s in range(1, CONV_K):
            acc += w[CONV_K - 1 - s:CONV_K - s, :] * pltpu.roll(x, s, 0)
        o_ref[...] = jnp.where(ok, acc * _sig(acc), 0.0)

    return pl.pallas_call(
        body, name="conv_fwd", grid=(CONV_D // BLK,),
        in_specs=[pl.BlockSpec((t, BLK), lambda j: (0, c0 + j)),
                  pl.BlockSpec((CONV_K, BLK), lambda j: (0, j)), pl.BlockSpec((1, BLK), lambda j: (0, j))],
        out_specs=pl.BlockSpec((t, BLK), lambda j: (0, j)),
        out_shape=jax.ShapeDtypeStruct((t, CONV_D), F32),
        compiler_params=_cp("arbitrary"),
    )(proj, conv_w, conv_b)


def conv_bwd(dxa, proj, conv_w, conv_b):
    t = proj.shape[0]
    c0 = C_XBC // BLK

    def body(d_ref, x_ref, w_ref, b_ref, dx_ref, dw_ref, db_ref):
        ok = _valid_rows(t, 0)
        x = jnp.where(ok, x_ref[...], 0.0)
        w = w_ref[...]
        xs = [x] + [pltpu.roll(x, s, 0) for s in range(1, CONV_K)]
        acc = b_ref[...] + w[CONV_K - 1:CONV_K, :] * x
        for s in range(1, CONV_K):
            acc += w[CONV_K - 1 - s:CONV_K - s, :] * xs[s]
        sg = _sig(acc)
        dxc = jnp.where(ok, d_ref[...] * (sg * (1.0 + acc * (1.0 - sg))), 0.0)
        db_ref[...] = jnp.sum(dxc, axis=0, keepdims=True)
        dw_ref[...] = jnp.concatenate(
            [jnp.sum(dxc * xs[CONV_K - 1 - k], axis=0, keepdims=True) for k in range(CONV_K)], axis=0)
        dx = w[CONV_K - 1:CONV_K, :] * dxc
        for s in range(1, CONV_K):
            dx += w[CONV_K - 1 - s:CONV_K - s, :] * pltpu.roll(dxc, t - s, 0)
        dx_ref[...] = jnp.where(ok, dx, 0.0)

    col = pl.BlockSpec((t, BLK), lambda j: (0, j))
    return pl.pallas_call(
        body, name="conv_bwd", grid=(CONV_D // BLK,),
        in_specs=[col, pl.BlockSpec((t, BLK), lambda j: (0, c0 + j)),
                  pl.BlockSpec((CONV_K, BLK), lambda j: (0, j)), pl.BlockSpec((1, BLK), lambda j: (0, j))],
        out_specs=[col, pl.BlockSpec((CONV_K, BLK), lambda j: (0, j)), pl.BlockSpec((1, BLK), lambda j: (0, j))],
        out_shape=[jax.ShapeDtypeStruct((t, CONV_D), F32), jax.ShapeDtypeStruct((CONV_K, CONV_D), F32),
                   jax.ShapeDtypeStruct((1, CONV_D), F32)],
        compiler_params=_cp("arbitrary"),
    )(dxa, proj, conv_w, conv_b)


def _softplus(x):
    return jnp.maximum(x, 0.0) + jnp.log(1.0 + jnp.exp(-jnp.abs(x)))


GW = SSD_D // SSD_G
HPG = SSD_H // SSD_G


def _head_expand():
    r = lax.broadcasted_iota(jnp.int32, (BLK, SSD_D), 0)
    c = lax.broadcasted_iota(jnp.int32, (BLK, SSD_D), 1)
    rt = lax.broadcasted_iota(jnp.int32, (SSD_D, BLK), 0)
    ct = lax.broadcasted_iota(jnp.int32, (SSD_D, BLK), 1)
    return (c // SSD_P == r).astype(F32), (rt // SSD_P == ct).astype(F32)


def _ssd_chunk(xa, sm, dtb, alog, dskip, ok, sp):
    e, et = _head_expand()
    dt = jnp.where(ok, _softplus(sm + dtb), 0.0)
    amat = -jnp.exp(alog)
    tri = _tri()
    ac = _nn_hi(tri.astype(F32), dt * amat)
    act = ac.T
    ace, dte, dse = _nn_hi(ac, e), _nn_hi(dt, e), _nn_hi(dskip, e)
    laste = ace[BLK - 1:BLK, :]
    ee, dece, gle = jnp.exp(ace), jnp.exp(laste - ace), jnp.exp(laste)
    xs = xa[:, :SSD_D]
    xdt = xs * dte
    decx = dece * xdt
    xdtb = xdt.astype(BF16)
    d = dict(e=e, et=et, dt=dt, amat=amat, tri=tri, ac=ac, act=act, dte=dte, dse=dse, ee=ee, dece=dece, gle=gle, xs=xs,
             xdt=xdt, xdtb=xdtb, decx=decx, bg=[], cg=[], cb=[], yo=[], seg=[], m=[], new_s=[])
    ys = []
    for g in range(SSD_G):
        cols = slice(GW * g, GW * (g + 1))
        bg = xa[:, SSD_D + SSD_N * g:SSD_D + SSD_N * (g + 1)].astype(BF16)
        cg = xa[:, SSD_D + SSD_G * SSD_N + SSD_N * g:SSD_D + SSD_G * SSD_N + SSD_N * (g + 1)].astype(BF16)
        spg = sp[:, cols]
        sloc = _tn(bg, decx[:, cols].astype(BF16))
        yo = _nn(cg, spg.astype(BF16)) * ee[:, cols]
        cb = _nt(cg, bg)
        d["new_s"].append(gle[:, cols] * spg + sloc)
        yds = []
        for h in range(HPG * g, HPG * (g + 1)):
            seg = jnp.where(tri, jnp.exp(jnp.minimum(ac[:, h:h + 1] - act[h:h + 1, :], 0.0)), 0.0)
            m = cb * seg
            yds.append(_nn(m.astype(BF16), xdtb[:, SSD_P * h:SSD_P * (h + 1)]))
            d["seg"].append(seg)
            d["m"].append(m)
        ys.append(jnp.concatenate(yds, axis=1) + yo)
        for k, val in (("bg", bg), ("cg", cg), ("cb", cb), ("yo", yo)):
            d[k].append(val)
    d["y"] = jnp.concatenate(ys, axis=1) + dse * xs
    return d


def ssd_fwd(xa, proj, dtb, alog, dskip, normg):
    t = xa.shape[0]
    nb = t // BLK
    gw = SSD_D // SSD_G

    def body(xa_ref, z_ref, sm_ref, dtb_ref, al_ref, ds_ref, ng_ref, y_ref, sp_ref, st):
        c = pl.program_id(0)

        @pl.when(c == 0)
        def _():
            st[...] = jnp.zeros_like(st)

        ok = _valid_rows(BLK, c * BLK)
        sp = st[...]
        sp_ref[...] = sp
        d = _ssd_chunk(xa_ref[...], sm_ref[...], dtb_ref[...], al_ref[...], ds_ref[...], ok, sp)
        st[...] = jnp.concatenate(d["new_s"], axis=1)
        y = d["y"]
        z = z_ref[...]
        yg = y * (z * _sig(z))
        outs = []
        for g in range(SSD_G):
            v = yg[:, gw * g:gw * (g + 1)]
            outs.append(v * lax.rsqrt(jnp.mean(v * v, axis=1, keepdims=True) + EPS))
        y_ref[...] = (jnp.concatenate(outs, axis=1) * ng_ref[...]).astype(BF16)

    vec = pl.BlockSpec((1, BLK), lambda c: (0, 0))
    return pl.pallas_call(
        body, name="ssd_fwd", grid=(nb,),
        in_specs=[pl.BlockSpec((BLK, CONV_D), lambda c: (c, 0)),
                  pl.BlockSpec((BLK, SSD_D), lambda c: (c, C_Z // SSD_D)),
                  pl.BlockSpec((BLK, BLK), lambda c: (c, C_SM // BLK)),
                  vec, vec, vec, pl.BlockSpec((1, SSD_D), lambda c: (0, 0))],
        out_specs=[pl.BlockSpec((BLK, SSD_D), lambda c: (c, 0)),
                   pl.BlockSpec((None, SSD_N, SSD_D), lambda c: (c, 0, 0))],
        out_shape=[jax.ShapeDtypeStruct((t, SSD_D), BF16), jax.ShapeDtypeStruct((nb, SSD_N, SSD_D), F32)],
        scratch_shapes=[pltpu.VMEM((SSD_N, SSD_D), F32)],
        compiler_params=_cp("arbitrary"),
    )(xa, proj, proj, dtb, alog, dskip, normg)


def _lane_put(col, lane):
    li = lax.broadcasted_iota(jnp.int32, (col.shape[0], BLK), 1)
    return jnp.where(li == lane, col, 0.0)


def ssd_bwd(dmix, xa, proj, sprev, dtb, alog, dskip, normg):
    t = xa.shape[0]
    nb = t // BLK
    gw = SSD_D // SSD_G
    rev = lambda c: nb - 1 - c

    def body(dy_ref, xa_ref, z_ref, sm_ref, sp_ref, dtb_ref, al_ref, ds_ref, ng_ref,
             dxa_ref, dz_ref, dsm_ref, dng_ref, dds_ref, dal_ref, ddtb_ref, dst):
        c = pl.program_id(0)

        @pl.when(c == 0)
        def _():
            dst[...] = jnp.zeros_like(dst)
            dng_ref[...] = jnp.zeros_like(dng_ref)
            dds_ref[...] = jnp.zeros_like(dds_ref)
            dal_ref[...] = jnp.zeros_like(dal_ref)
            ddtb_ref[...] = jnp.zeros_like(ddtb_ref)

        ok = _valid_rows(BLK, rev(c) * BLK)
        sm = sm_ref[...]
        sp = sp_ref[...]
        d = _ssd_chunk(xa_ref[...], sm, dtb_ref[...], al_ref[...], ds_ref[...], ok, sp)
        dt, amat, ac, act, tri, et, xs, xdt = (d[k] for k in ("dt", "amat", "ac", "act", "tri", "et", "xs", "xdt"))
        rowi = lax.broadcasted_iota(jnp.int32, (BLK, 1), 0)
        y = d["y"]
        z = z_ref[...]
        sgz = _sig(z)
        siluz = z * sgz
        yg = y * siluz
        dout = dy_ref[...]
        ng = ng_ref[...]
        dygs, xhs = [], []
        for g in range(SSD_G):
            v = yg[:, gw * g:gw * (g + 1)]
            rr = lax.rsqrt(jnp.mean(v * v, axis=1, keepdims=True) + EPS)
            xh = v * rr
            dxh = dout[:, gw * g:gw * (g + 1)] * ng[:, gw * g:gw * (g + 1)]
            dygs.append(rr * (dxh - xh * jnp.mean(dxh * xh, axis=1, keepdims=True)))
            xhs.append(xh)
        dyg = jnp.concatenate(dygs, axis=1)
        dng_ref[...] += jnp.sum(dout * jnp.concatenate(xhs, axis=1), axis=0, keepdims=True)
        dy = dyg * siluz
        dz_ref[...] = dyg * y * (sgz * (1.0 + z * (1.0 - sgz)))

        triu = _tri(lower=False)
        dyb = dy.astype(BF16)
        dsn = dst[...]
        dds_ref[...] += _nn_hi(jnp.sum(dy * xs, axis=0, keepdims=True), et)
        dac_all = _nn_hi(dy * jnp.concatenate(d["yo"], axis=1), et)
        dyo = (dy * d["ee"]).astype(BF16)
        gl = jnp.exp(ac[BLK - 1:BLK, :])
        dlast = _nn_hi(jnp.sum(dsn * sp, axis=0, keepdims=True), et) * gl
        bds, db_g, dc_g, dxdt_i, new_dst = [], [], [], [], []
        for g in range(SSD_G):
            cols = slice(GW * g, GW * (g + 1))
            bg, cg = d["bg"][g], d["cg"][g]
            dsng = dsn[:, cols].astype(BF16)
            dc = _nt(dyo[:, cols], sp[:, cols].astype(BF16))
            new_dst.append(_tn(cg, dyo[:, cols]) + d["gle"][:, cols] * dsn[:, cols])
            bds.append(_nn(bg, dsng))
            db = _nt(d["decx"][:, cols].astype(BF16), dsng)
            cbt = _nt(bg, cg)
            dcb = jnp.zeros((BLK, BLK), F32)
            for h in range(HPG * g, HPG * (g + 1)):
                hc = slice(SSD_P * h, SSD_P * (h + 1))
                dm = _nt(dyb[:, hc], d["xdtb"][:, hc])
                dcb += dm * d["seg"][h]
                w = dm * d["m"][h]
                dac_all += _lane_put(jnp.sum(w, axis=1, keepdims=True) - jnp.sum(w.T, axis=1, keepdims=True), h)
                segt = jnp.where(triu, jnp.exp(jnp.minimum(act[h:h + 1, :] - ac[:, h:h + 1], 0.0)), 0.0)
                dxdt_i.append(_nn((cbt * segt).astype(BF16), dyb[:, hc]))
            dcbb = dcb.astype(BF16)
            dc_g.append(dc + _nn(dcbb, bg))
            db_g.append(db + _tn(dcbb, cg))
        dst[...] = jnp.concatenate(new_dst, axis=1)
        bds = jnp.concatenate(bds, axis=1)
        tdec = jnp.exp(ac[BLK - 1:BLK, :] - ac) * _nn_hi(xdt * bds, et)
        dlast += jnp.sum(tdec, axis=0, keepdims=True)
        dac_all += jnp.where(rowi == BLK - 1, dlast, 0.0) - tdec
        dxdt = d["dece"] * bds + jnp.concatenate(dxdt_i, axis=1)
        da = _nn_hi(triu.astype(F32), dac_all)
        ddt = _nn_hi(dxdt * xs, et) + da * amat
        dal_ref[...] += jnp.sum(da * dt, axis=0, keepdims=True) * amat
        ddtr = jnp.where(ok, ddt * _sig(sm + dtb_ref[...]), 0.0)
        ddtb_ref[...] += jnp.sum(ddtr, axis=0, keepdims=True)
        dsm_ref[...] = ddtr
        dxs = d["dse"] * dy + dxdt * d["dte"]
        dxa_ref[...] = jnp.where(ok, jnp.concatenate([dxs] + db_g + dc_g, axis=1), 0.0)

    vec = pl.BlockSpec((1, BLK), lambda c: (0, 0))
    nvec = pl.BlockSpec((1, SSD_D), lambda c: (0, 0))
    return pl.pallas_call(
        body, name="ssd_bwd", grid=(nb,),
        in_specs=[pl.BlockSpec((BLK, SSD_D), lambda c: (rev(c), 0)),
                  pl.BlockSpec((BLK, CONV_D), lambda c: (rev(c), 0)),
                  pl.BlockSpec((BLK, SSD_D), lambda c: (rev(c), C_Z // SSD_D)),
                  pl.BlockSpec((BLK, BLK), lambda c: (rev(c), C_SM // BLK)),
                  pl.BlockSpec((None, SSD_N, SSD_D), lambda c: (rev(c), 0, 0)),
                  vec, vec, vec, nvec],
        out_specs=[pl.BlockSpec((BLK, CONV_D), lambda c: (rev(c), 0)),
                   pl.BlockSpec((BLK, SSD_D), lambda c: (rev(c), 0)),
                   pl.BlockSpec((BLK, BLK), lambda c: (rev(c), 0)),
                   nvec, vec, vec, vec],
        out_shape=[jax.ShapeDtypeStruct((t, CONV_D), F32), jax.ShapeDtypeStruct((t, SSD_D), F32),
                   jax.ShapeDtypeStruct((t, BLK), F32), jax.ShapeDtypeStruct((1, SSD_D), F32),
                   jax.ShapeDtypeStruct((1, BLK), F32), jax.ShapeDtypeStruct((1, BLK), F32),
                   jax.ShapeDtypeStruct((1, BLK), F32)],
        scratch_shapes=[pltpu.VMEM((SSD_N, SSD_D), F32)],
        compiler_params=_cp("arbitrary"),
    )(dmix, xa, proj, proj, sprev, dtb, alog, dskip, normg)


def _attn_scores(q_ref, k_ref, h, dq, scale, mask, bias):
    qh = q_ref[:, dq * h:dq * (h + 1)].astype(BF16)
    kh = k_ref[:, dq * h:dq * (h + 1)].astype(BF16)
    s = _nt(qh, kh) * scale
    if bias is not None:
        s = s + bias
    return qh, kh, jnp.where(mask, s, NEG)


def _segments(nb):
    cuts = sorted({0, nb} | {max(1, round(nb * f)) for f in (0.3, 0.53, 0.77)})
    return list(zip(cuts[:-1], cuts[1:]))


def attn_fwd(q, k, v, qcol, kcol, vcol, nh, dq, dv, scale, c_col=None, c_row=None, lane0=0):
    t = q.shape[0]
    tq = BLK
    use_bias = c_col is not None

    def segment(t0, t1, prev):
        tk = t1 * BLK
        nprev = len(prev)

        def body(*refs):
            refs = refs[nprev:]
            if use_bias:
                q_ref, k_ref, v_ref, cc_ref, cr_ref, o_ref, l_ref = refs
            else:
                q_ref, k_ref, v_ref, o_ref, l_ref = refs
            i = pl.program_id(0)
            rowg = (t0 + i) * tq + lax.broadcasted_iota(jnp.int32, (tq, 1), 0)
            col = lax.broadcasted_iota(jnp.int32, (1, tk), 1)
            mask = (col <= rowg) & (col >= PAD)
            outs = []
            lse = jnp.zeros((tq, BLK), F32)
            for h in range(nh):
                bias = (cc_ref[:, lane0 + h:lane0 + h + 1] - cr_ref[h:h + 1, :]) if use_bias else None
                _, _, s = _attn_scores(q_ref, k_ref, h, dq, scale, mask, bias)
                m = jnp.max(s, axis=1, keepdims=True)
                p = jnp.exp(s - m)
                l = jnp.sum(p, axis=1, keepdims=True)
                vh = v_ref[:, dv * h:dv * (h + 1)].astype(BF16)
                outs.append(_nn(p.astype(BF16), vh) / l)
                lse += _lane_put(m + jnp.log(l), h)
            o_ref[...] = jnp.concatenate(outs, axis=1).astype(BF16)
            l_ref[...] = lse.T[0:8, :]

        in_specs = [_ANY] * nprev + [pl.BlockSpec((tq, nh * dq), lambda i: (t0 + i, qcol)),
                                     pl.BlockSpec((tk, nh * dq), lambda i: (0, kcol)),
                                     pl.BlockSpec((tk, nh * dv), lambda i: (0, vcol))]
        args = list(prev) + [q, k, v]
        if use_bias:
            in_specs += [pl.BlockSpec((tq, BLK), lambda i: (t0 + i, 0)), pl.BlockSpec((8, tk), lambda i: (0, 0))]
            args += [c_col, c_row]
        return pl.pallas_call(
            body, name="attn_fwd", grid=(t1 - t0,),
            in_specs=in_specs,
            out_specs=[pl.BlockSpec((tq, nh * dv), lambda i: (t0 + i, 0)), pl.BlockSpec((8, tq), lambda i: (0, t0 + i))],
            out_shape=[jax.ShapeDtypeStruct((t, nh * dv), BF16), jax.ShapeDtypeStruct((8, t), F32)],
            input_output_aliases={p: p for p in range(nprev)},
            compiler_params=_cp("arbitrary"),
        )(*args)

    outs = []
    for t0, t1 in _segments(t // tq):
        outs = segment(t0, t1, outs)
    return outs


def attn_bwd(q, k, v, do, lse_row, o, qcol, kcol, vcol, docol, ocol, nh, dq, dv, scale, c_col=None, c_row=None, lane0=0):
    t = q.shape[0]
    tq = BLK
    use_bias = c_col is not None

    def segment(t0, t1, prev):
        tk = t1 * BLK
        nprev = len(prev)

        def body(*refs):
            pv, refs = refs[:nprev], refs[nprev:]
            kt = refs[-1]
            if use_bias:
                q_ref, k_ref, v_ref, do_ref, l_ref, o_ref, cc_ref, cr_ref, dq_ref, dk_ref, dv_ref, dcq_ref, dck_ref = refs[:-1]
            else:
                q_ref, k_ref, v_ref, do_ref, l_ref, o_ref, dq_ref, dk_ref, dv_ref = refs[:-1]
            i = pl.program_id(0)

            @pl.when(i == 0)
            def _():
                kt[...] = k_ref[...].astype(BF16).T
                if nprev:
                    dk_ref[...] = pv[1][...]
                    dv_ref[...] = pv[2][...]
                    if use_bias:
                        dck_ref[...] = pv[4][...]
                else:
                    dk_ref[...] = jnp.zeros_like(dk_ref)
                    dv_ref[...] = jnp.zeros_like(dv_ref)
                    if use_bias:
                        dck_ref[...] = jnp.zeros_like(dck_ref)

            key = lax.broadcasted_iota(jnp.int32, (tk, 1), 0)
            qry = (t0 + i) * tq + lax.broadcasted_iota(jnp.int32, (1, tq), 1)
            mask = (key <= qry) & (key >= PAD)
            dot = (do_ref[...].astype(F32) * o_ref[...].astype(F32)).T
            lane = lax.broadcasted_iota(jnp.int32, (1, BLK), 1)
            dqts, dcqs = [], []
            for h in range(nh):
                qh = q_ref[:, dq * h:dq * (h + 1)].astype(BF16)
                kh = k_ref[:, dq * h:dq * (h + 1)].astype(BF16)
                vh = v_ref[:, dv * h:dv * (h + 1)].astype(BF16)
                doh = do_ref[:, dv * h:dv * (h + 1)].astype(BF16)
                delta = jnp.sum(dot[dv * h:dv * (h + 1), :], axis=0, keepdims=True)
                st = _nt(kh, qh) * scale
                if use_bias:
                    st = st + (cr_ref[h:h + 1, :] - cc_ref[h])
                pt = jnp.exp(jnp.where(mask, st, NEG) - l_ref[h:h + 1, :])
                dst = pt * (_nt(vh, doh) - delta)
                dsb = dst.astype(BF16)
                dk_ref[:, dq * h:dq * (h + 1)] += _nn(dsb, qh) * scale
                dv_ref[:, dv * h:dv * (h + 1)] += _nn(pt.astype(BF16), doh)
                dqts.append(_nn(kt[dq * h:dq * (h + 1), :], dsb))
                if use_bias:
                    dcqs.append(jnp.sum(dst, axis=0, keepdims=True))
                    dck_ref[h] += dst
            dq_ref[...] = jnp.concatenate(dqts, axis=0).T * scale
            if use_bias:
                dcq_ref[...] = jnp.concatenate(dcqs + [jnp.zeros((8 - nh, tq), F32)], axis=0)

        keys_q = pl.BlockSpec((tk, nh * dq), lambda i: (0, 0))
        keys_v = pl.BlockSpec((tk, nh * dv), lambda i: (0, 0))
        keys_c = pl.BlockSpec((nh, tk, BLK), lambda i: (0, 0, 0))
        qrow = pl.BlockSpec((8, tq), lambda i: (0, t0 + i))
        prev_specs = ([_ANY, keys_q, keys_v] + ([_ANY, keys_c] if use_bias else [])) if nprev else []
        in_specs = prev_specs + [pl.BlockSpec((tq, nh * dq), lambda i: (t0 + i, qcol)),
                                 pl.BlockSpec((tk, nh * dq), lambda i: (0, kcol)),
                                 pl.BlockSpec((tk, nh * dv), lambda i: (0, vcol)),
                                 pl.BlockSpec((tq, nh * dv), lambda i: (t0 + i, docol)),
                                 qrow,
                                 pl.BlockSpec((tq, nh * dv), lambda i: (t0 + i, ocol))]
        args = list(prev) + [q, k, v, do, lse_row, o]
        out_specs = [pl.BlockSpec((tq, nh * dq), lambda i: (t0 + i, 0)), keys_q, keys_v]
        out_shape = [jax.ShapeDtypeStruct((t, nh * dq), F32), jax.ShapeDtypeStruct((t, nh * dq), F32),
                     jax.ShapeDtypeStruct((t, nh * dv), F32)]
        if use_bias:
            in_specs += [keys_c, qrow]
            args += [c_col, c_row]
            out_specs += [qrow, keys_c]
            out_shape += [jax.ShapeDtypeStruct((8, t), F32), jax.ShapeDtypeStruct((nh, t, BLK), F32)]
        return pl.pallas_call(
            body, name="attn_bwd", grid=(t1 - t0,),
            in_specs=in_specs, out_specs=out_specs, out_shape=out_shape,
            scratch_shapes=[pltpu.VMEM((nh * dq, tk), BF16)],
            input_output_aliases={p: p for p in range(nprev)},
            compiler_params=_cp("arbitrary"),
        )(*args)

    outs = []
    for t0, t1 in reversed(_segments(t // tq)):
        outs = segment(t0, t1, outs)
    return outs


def fox_pre(proj, fb):
    t = proj.shape[0]
    nb = t // BLK

    def body(sm_ref, fb_ref, c_ref, cr_ref, cb_ref):
        x = sm_ref[...] + fb_ref[...]
        lane = lax.broadcasted_iota(jnp.int32, (1, BLK), 1)
        keep = _valid_rows(t, 0) & (lane >= SM_F) & (lane < SM_F + FOX_H)
        logf = jnp.where(keep, jnp.minimum(x, 0.0) - jnp.log(1.0 + jnp.exp(-jnp.abs(x))), 0.0)
        tri = _tri().astype(F32)
        carry = jnp.zeros((1, BLK), F32)
        for b in range(nb):
            cb = _nn_hi(tri, logf[b * BLK:(b + 1) * BLK, :]) + carry
            c_ref[b * BLK:(b + 1) * BLK, :] = cb
            carry = cb[BLK - 1:BLK, :]
        cr_ref[...] = c_ref[...].T[SM_F:SM_F + 8, :]
        for h in range(FOX_H):
            cb_ref[h] = jnp.broadcast_to(c_ref[:, SM_F + h:SM_F + h + 1], (t, BLK))

    return pl.pallas_call(
        body, name="fox_pre", grid=(1,),
        in_specs=[pl.BlockSpec((t, BLK), lambda i: (0, C_SM // BLK)), pl.BlockSpec((1, BLK), lambda i: (0, 0))],
        out_specs=[pl.BlockSpec((t, BLK), lambda i: (0, 0)), pl.BlockSpec((8, t), lambda i: (0, 0)),
                   pl.BlockSpec((FOX_H, t, BLK), lambda i: (0, 0, 0))],
        out_shape=[jax.ShapeDtypeStruct((t, BLK), F32), jax.ShapeDtypeStruct((8, t), F32),
                   jax.ShapeDtypeStruct((FOX_H, t, BLK), F32)],
        compiler_params=_cp("arbitrary"),
    )(proj, fb)


def fox_pre_bwd(dcq, dck, proj, fb, dsm_in):
    t = proj.shape[0]
    nb = t // BLK

    def body(dcq_ref, dck_ref, sm_ref, fb_ref, din_ref, dsm_ref, dfb_ref, scr):
        triu = _tri(lower=False).astype(F32)
        carry = jnp.zeros((1, BLK), F32)
        scr[...] = jnp.concatenate([jnp.zeros((SM_F, t), F32), dcq_ref[...], jnp.zeros((BLK - SM_F - 8, t), F32)], axis=0).T
        lane = lax.broadcasted_iota(jnp.int32, (1, BLK), 1)
        for b in range(nb - 1, -1, -1):
            blk = scr[b * BLK:(b + 1) * BLK, :]
            for h in range(FOX_H):
                blk -= jnp.where(lane == SM_F + h, jnp.sum(dck_ref[h, b * BLK:(b + 1) * BLK, :], axis=1, keepdims=True), 0.0)
            cb = _nn_hi(triu, blk) + carry
            scr[b * BLK:(b + 1) * BLK, :] = cb
            carry = cb[0:1, :]
        x = sm_ref[...] + fb_ref[...]
        lane = lax.broadcasted_iota(jnp.int32, (1, BLK), 1)
        keep = _valid_rows(t, 0) & (lane >= SM_F) & (lane < SM_F + FOX_H)
        df = jnp.where(keep, scr[...] * _sig(-x), 0.0)
        dfb_ref[...] = jnp.sum(df, axis=0, keepdims=True)
        dsm_ref[...] = din_ref[...] + df

    full = pl.BlockSpec((t, BLK), lambda i: (0, 0))
    return pl.pallas_call(
        body, name="fox_pre_bwd", grid=(1,),
        in_specs=[pl.BlockSpec((8, t), lambda i: (0, 0)), pl.BlockSpec((FOX_H, t, BLK), lambda i: (0, 0, 0)),
                  pl.BlockSpec((t, BLK), lambda i: (0, C_SM // BLK)), pl.BlockSpec((1, BLK), lambda i: (0, 0)), full],
        out_specs=[full, pl.BlockSpec((1, BLK), lambda i: (0, 0))],
        out_shape=[jax.ShapeDtypeStruct((t, BLK), F32), jax.ShapeDtypeStruct((1, BLK), F32)],
        scratch_shapes=[pltpu.VMEM((t, BLK), F32)],
        compiler_params=_cp("arbitrary"),
    )(dcq, dck, proj, fb, dsm_in)


def _swap_rope(x):
    lane = lax.broadcasted_iota(jnp.int32, (1, BLK), 1)
    return jnp.where((lane >= SM_KR) & (lane < SM_KR + 16), pltpu.roll(x, BLK - 16, 1),
                     jnp.where((lane >= SM_KR + 16) & (lane < SM_KR + 32), pltpu.roll(x, 16, 1), 0.0))


def _rms(x, g):
    r = lax.rsqrt(jnp.mean(x * x, axis=1, keepdims=True) + EPS)
    return r, x * r


def mla_pre(proj, qg, kvg, wq, wk, wv, cosq, sinq):
    t = proj.shape[0]
    tm = _row_tile(t)

    def body(cq_ref, ckv_ref, sm_ref, qg_ref, kvg_ref, wq_ref, wk_ref, wv_ref, cos_ref, sin_ref,
             q_ref, k_ref, v_ref, cqn_ref, ckvn_ref):
        cs, sn = cos_ref[...], sin_ref[...]
        _, xh = _rms(cq_ref[...], None)
        cqn = (xh * qg_ref[...]).astype(BF16)
        cqn_ref[...] = cqn
        qraw = _nn(cqn, wq_ref[...])
        qs = []
        for h in range(MLA_H):
            hb = qraw[:, BLK * h:BLK * (h + 1)]
            qs.append(hb * cs + _swap_rope(hb) * sn)
        q_ref[...] = jnp.concatenate(qs, axis=1).astype(BF16)
        _, kh = _rms(ckv_ref[...], None)
        ckvn = (kh * kvg_ref[...]).astype(BF16)
        ckvn_ref[...] = ckvn
        kraw = _nn(ckvn, wk_ref[...])
        v_ref[...] = _nn(ckvn, wv_ref[...]).astype(BF16)
        lane = lax.broadcasted_iota(jnp.int32, (1, BLK), 1)
        kr = sm_ref[...]
        krr = jnp.where((lane >= SM_KR) & (lane < SM_KR + MLA_ROPE), kr * cs + _swap_rope(kr) * sn, 0.0)
        k_ref[...] = jnp.concatenate([kraw[:, BLK * h:BLK * (h + 1)] + krr for h in range(MLA_H)], axis=1).astype(BF16)

    def rows(w, cb):
        return pl.BlockSpec((tm, w), lambda i: (i, cb))

    def whole(a):
        return pl.BlockSpec(a.shape, lambda i: (0, 0))

    return pl.pallas_call(
        body, name="mla_pre", grid=(t // tm,),
        in_specs=[rows(MLA_QL, C_CQ // MLA_QL), rows(MLA_KVL, C_CKV // MLA_KVL), rows(BLK, C_SM // BLK),
                  whole(qg), whole(kvg), whole(wq), whole(wk), whole(wv), rows(BLK, 0), rows(BLK, 0)],
        out_specs=[rows(512, 0), rows(512, 0), rows(256, 0), rows(MLA_QL, 0), rows(MLA_KVL, 0)],
        out_shape=[jax.ShapeDtypeStruct((t, 512), BF16), jax.ShapeDtypeStruct((t, 512), BF16),
                   jax.ShapeDtypeStruct((t, 256), BF16), jax.ShapeDtypeStruct((t, MLA_QL), BF16),
                   jax.ShapeDtypeStruct((t, MLA_KVL), BF16)],
        compiler_params=_cp("arbitrary"),
    )(proj, proj, proj, qg, kvg, wq, wk, wv, cosq, sinq)


def mla_pre_bwd(dq, dk, dv, proj, cqn, ckvn, qg, kvg, wq, wk, wv, cosq, sinq, dsm_in):
    t = proj.shape[0]
    tm = _row_tile(t)

    def body(dq_ref, dk_ref, dv_ref, cq_ref, ckv_ref, cqn_ref, ckvn_ref, qg_ref, kvg_ref, wq_ref, wk_ref, wv_ref,
             cos_ref, sin_ref, din_ref, dcq_ref, dckv_ref, dsm_ref, dwq_ref, dwk_ref, dwv_ref, dqg_ref, dkvg_ref):
        i = pl.program_id(0)

        @pl.when(i == 0)
        def _():
            for r in (dwq_ref, dwk_ref, dwv_ref, dqg_ref, dkvg_ref):
                r[...] = jnp.zeros_like(r)

        cs, sn = cos_ref[...], sin_ref[...]
        lane = lax.broadcasted_iota(jnp.int32, (1, BLK), 1)

        def unrope(dy):
            return dy * cs + _swap_rope(dy * sn)

        dqp = jnp.concatenate([unrope(dq_ref[:, BLK * h:BLK * (h + 1)]) for h in range(MLA_H)], axis=1).astype(BF16)
        dwq_ref[...] += _tn(cqn_ref[...], dqp)
        dcqn = _nt(dqp, wq_ref[...])
        r, xh = _rms(cq_ref[...], None)
        dqg_ref[...] += jnp.sum(dcqn * xh, axis=0, keepdims=True)
        dxh = dcqn * qg_ref[...]
        dcq_ref[...] = r * (dxh - xh * jnp.mean(dxh * xh, axis=1, keepdims=True))

        dkn, dkr = [], jnp.zeros((tm, BLK), F32)
        for h in range(MLA_H):
            blk = dk_ref[:, BLK * h:BLK * (h + 1)]
            dkn.append(jnp.where(lane < MLA_NOPE, blk, 0.0))
            dkr += jnp.where((lane >= SM_KR) & (lane < SM_KR + MLA_ROPE), blk, 0.0)
        dknb = jnp.concatenate(dkn, axis=1).astype(BF16)
        dvb = dv_ref[...].astype(BF16)
        ckvn = ckvn_ref[...]
        dwk_ref[...] += _tn(ckvn, dknb)
        dwv_ref[...] += _tn(ckvn, dvb)
        dckvn = _nt(dknb, wk_ref[...]) + _nt(dvb, wv_ref[...])
        r2, kh = _rms(ckv_ref[...], None)
        dkvg_ref[...] += jnp.sum(dckvn * kh, axis=0, keepdims=True)
        dkh = dckvn * kvg_ref[...]
        dckv_ref[...] = r2 * (dkh - kh * jnp.mean(dkh * kh, axis=1, keepdims=True))
        dsm_ref[...] = din_ref[...] + jnp.where((lane >= SM_KR) & (lane < SM_KR + MLA_ROPE), unrope(dkr), 0.0)

    def rows(w, cb):
        return pl.BlockSpec((tm, w), lambda i: (i, cb))

    def whole(a):
        return pl.BlockSpec(a.shape, lambda i: (0, 0))

    def wshape(a):
        return jax.ShapeDtypeStruct(a.shape, F32)

    return pl.pallas_call(
        body, name="mla_pre_bwd", grid=(t // tm,),
        in_specs=[rows(512, 0), rows(512, 0), rows(256, 0), rows(MLA_QL, C_CQ // MLA_QL), rows(MLA_KVL, C_CKV // MLA_KVL),
                  rows(MLA_QL, 0), rows(MLA_KVL, 0), whole(qg), whole(kvg), whole(wq), whole(wk), whole(wv),
                  rows(BLK, 0), rows(BLK, 0), rows(BLK, 0)],
        out_specs=[rows(MLA_QL, 0), rows(MLA_KVL, 0), rows(BLK, 0), whole(wq), whole(wk), whole(wv), whole(qg), whole(kvg)],
        out_shape=[jax.ShapeDtypeStruct((t, MLA_QL), F32), jax.ShapeDtypeStruct((t, MLA_KVL), F32),
                   jax.ShapeDtypeStruct((t, BLK), F32), wshape(wq), wshape(wk), wshape(wv), wshape(qg), wshape(kvg)],
        compiler_params=_cp("arbitrary"),
    )(dq, dk, dv, proj, proj, cqn, ckvn, qg, kvg, wq, wk, wv, cosq, sinq, dsm_in)


def _slot_sum(me, own, recv_ref):
    gg = own.astype(F32)
    for s in range(N_DEV):
        gg = gg + jnp.where(me == s, 0.0, recv_ref[s].astype(F32))
    return gg


def adamw(w, m, v, g=None, recv=None, own=None, me_arr=None):
    shape = w.shape
    c = shape[-1]
    from_recv = recv is not None
    if not from_recv:
        me_arr = jnp.zeros((1,), jnp.int32)
    nl = len(recv) if from_recv else 1
    rws = w.size // c // nl
    tr = rws
    for d in (1024, 512, 352, 256, 128, 64, 32, 16, 8):
        if rws % d == 0 and d * c * 4 <= (2 << 20):
            tr = d
            break
    nt = rws // tr
    w2, m2, v2 = (a.reshape(nl, rws, c) for a in (w, m, v))
    if from_recv:
        gin = [a.reshape(N_DEV, rws, c) for a in list(recv) + list(own)]
    else:
        gin = [g.reshape(1, rws, c)]

    def body(me_ref, w_ref, m_ref, v_ref, *rest):
        g_refs, outs = rest[:len(gin)], rest[len(gin):]
        if from_recv:
            g_out, outs = outs[0], outs[1:]
            for li in range(nl):
                @pl.when(pl.program_id(0) == li)
                def _(li=li):
                    g_out[...] = _slot_sum(me_ref[0], g_refs[nl + li][...], g_refs[li])
            gg = g_out[...]
        else:
            gg = g_refs[0][...]
        d_ref, nm_ref, nv_ref = outs
        nm = B1 * m_ref[...] + (1.0 - B1) * gg
        nv = B2 * v_ref[...] + (1.0 - B2) * (gg * gg)
        mh = nm / (1.0 - B1 ** STEP)
        vh = nv / (1.0 - B2 ** STEP)
        d_ref[...] = -LR * (mh / (jnp.sqrt(vh) + AEPS) + WD * w_ref[...])
        nm_ref[...] = nm
        nv_ref[...] = nv

    row = pl.BlockSpec((None, tr, c), lambda l, i, me: (l, i, 0))
    if from_recv:
        gspecs = [pl.BlockSpec((N_DEV, tr, c), lambda l, i, me, li=li: (0, jnp.where(l == li, i, 0), 0))
                  for li in range(nl)]
        gspecs += [pl.BlockSpec((None, tr, c), lambda l, i, me, li=li: (me[0], jnp.where(l == li, i, 0), 0))
                   for li in range(nl)]
    else:
        gspecs = [row]
    nout = 4 if from_recv else 3
    outs = pl.pallas_call(
        body, name="adamw",
        grid_spec=pltpu.PrefetchScalarGridSpec(num_scalar_prefetch=1, grid=(nl, nt), in_specs=[row, row, row] + gspecs,
                                               out_specs=[row] * nout),
        out_shape=[jax.ShapeDtypeStruct((nl, rws, c), F32)] * nout,
        compiler_params=_cp("arbitrary", "arbitrary"),
    )(me_arr, w2, m2, v2, *gin)
    return tuple(o.reshape(shape) for o in outs)


def sum_slots(recv, own=None, me_arr=None):
    _, r, c = recv.shape
    if own is None:
        own, me_arr = recv, jnp.zeros((1,), jnp.int32)
        plain = True
    else:
        plain = False

    def body(me_ref, r_ref, own_ref, o_ref):
        if plain:
            gg = r_ref[0].astype(F32)
            for s in range(1, N_DEV):
                gg = gg + r_ref[s].astype(F32)
            o_ref[...] = gg
        else:
            o_ref[...] = _slot_sum(me_ref[0], own_ref[...], r_ref)

    return pl.pallas_call(
        body, name="sum_slots",
        grid_spec=pltpu.PrefetchScalarGridSpec(
            num_scalar_prefetch=1, grid=(1,),
            in_specs=[pl.BlockSpec((N_DEV, r, c), lambda i, me: (0, 0, 0)),
                      pl.BlockSpec((None, r, c), lambda i, me: (me[0], 0, 0))],
            out_specs=pl.BlockSpec((r, c), lambda i, me: (0, 0))),
        out_shape=jax.ShapeDtypeStruct((r, c), F32),
        compiler_params=_cp("arbitrary"),
    )(me_arr, recv, own)


_FLIPS = [(0, 0, 1), (0, 1, 0), (0, 1, 1), (1, 0, 0), (1, 0, 1), (1, 1, 0), (1, 1, 1)]
_ANY = pl.BlockSpec(memory_space=pl.ANY)


def _mesh_place():
    x, y, c = lax.axis_index("x"), lax.axis_index("y"), lax.axis_index("c")
    me = 4 * x + 2 * y + c
    peers = [((x + fx) % 2, (y + fy) % 2, (c + fc) % 2) for fx, fy, fc in _FLIPS]
    return me, peers


def place_own(src, l, dtype, me_arr):
    _, r, c = src.shape
    tr = r
    for d in (512, 352, 256, 128, 64, 32, 16, 8):
        if r % d == 0 and d * c * 4 <= (2 << 20):
            tr = d
            break

    def body(me_ref, s_ref, o_ref):
        o_ref[...] = s_ref[...].astype(dtype)

    return pl.pallas_call(
        body, name="place_own",
        grid_spec=pltpu.PrefetchScalarGridSpec(
            num_scalar_prefetch=1, grid=(r // tr,),
            in_specs=[pl.BlockSpec((None, tr, c), lambda i, me: (l, i, 0))],
            out_specs=pl.BlockSpec((None, tr, c), lambda i, me: (me[0], i, 0))),
        out_shape=jax.ShapeDtypeStruct((N_DEV, r, c), dtype),
        compiler_params=_cp("arbitrary"),
    )(me_arr, src)


_HBM = pl.BlockSpec(memory_space=pltpu.HBM)
_SEMS = pl.BlockSpec(memory_space=pltpu.SEMAPHORE)
_EFFECT = pltpu.SideEffectType.DATAFLOW_SIDE_EFFECTING


def exchange_start(mode, arrays, name):
    n = len(arrays)
    gather = mode == "gather"
    ns = 0 if gather else n
    zones = list(arrays) if gather else [lax.empty(a.shape, a.dtype) for a in arrays]
    ops = ([] if gather else list(arrays)) + zones

    def body(*refs):
        srcs, lands = refs[:ns], refs[ns:ns + n]
        send_sems, recv_sems = refs[ns + n], refs[ns + n + 1]
        token = refs[-1]
        me, peers = _mesh_place()
        ids = [4 * p[0] + 2 * p[1] + p[2] for p in peers]
        for j in range(n):
            for k in range(N_DEV - 1):
                src = lands[j].at[me] if gather else srcs[j].at[ids[k]]
                pltpu.make_async_remote_copy(src_ref=src, dst_ref=lands[j].at[me],
                                             send_sem=send_sems.at[j * (N_DEV - 1) + k],
                                             recv_sem=recv_sems.at[j * (N_DEV - 1) + k], device_id=peers[k],
                                             device_id_type=pl.DeviceIdType.MESH).start()
        token[...] = jnp.zeros_like(token)

    nsem = n * (N_DEV - 1)
    res = pl.pallas_call(
        body, name=name,
        in_specs=[_HBM] * (ns + n),
        out_specs=(_SEMS, _SEMS, *[_HBM] * (ns + n), pl.BlockSpec(memory_space=pltpu.VMEM)),
        out_shape=(pltpu.SemaphoreType.DMA((nsem,)), pltpu.SemaphoreType.DMA((nsem,)),
                   *[pltpu.HBM(a.shape, a.dtype) for a in ops], jax.ShapeDtypeStruct((8, BLK), F32)),
        input_output_aliases={i: 2 + i for i in range(ns + n)},
        compiler_params=pltpu.CompilerParams(has_side_effects=_EFFECT),
    )(*[pltpu.with_memory_space_constraint(a, pltpu.HBM) for a in ops])
    return dict(gather=gather, send=res[0], recv=res[1], srcs=list(res[2:2 + ns]), lands=list(res[2 + ns:2 + ns + n]),
                token=res[-1])


def exchange_wait(hd, idxs, name, after):
    gather = hd["gather"]
    n = len(idxs)
    ns = 0 if gather else n
    ops = ([] if gather else [hd["srcs"][j] for j in idxs]) + [hd["lands"][j] for j in idxs]

    def body(*refs):
        srcs, lands = refs[:ns], refs[ns:ns + n]
        send_sems, recv_sems = refs[ns + n], refs[ns + n + 1]
        me, peers = _mesh_place()
        ids = [4 * p[0] + 2 * p[1] + p[2] for p in peers]
        for p, j in enumerate(idxs):
            for k in range(N_DEV - 1):
                src = lands[p].at[me] if gather else srcs[p].at[ids[k]]
                cp = pltpu.make_async_remote_copy(src_ref=src, dst_ref=lands[p].at[ids[k]],
                                                  send_sem=send_sems.at[j * (N_DEV - 1) + k],
                                                  recv_sem=recv_sems.at[j * (N_DEV - 1) + k], device_id=peers[k],
                                                  device_id_type=pl.DeviceIdType.MESH)
                cp.wait_send()
                cp.wait_recv()

    res = pl.pallas_call(
        body, name=name,
        in_specs=[_HBM] * (ns + n) + [_SEMS, _SEMS, _ANY],
        out_specs=[_HBM] * (ns + n),
        out_shape=[pltpu.HBM(a.shape, a.dtype) for a in ops],
        input_output_aliases={i: i for i in range(ns + n)},
        compiler_params=pltpu.CompilerParams(has_side_effects=_EFFECT),
    )(*ops, hd["send"], hd["recv"], after)
    return list(res[:ns]), list(res[ns:])


def _chip_place():
    x, y, c = lax.axis_index("x"), lax.axis_index("y"), lax.axis_index("c")
    chips = [((x + 1) % 2, y), (x, (y + 1) % 2), ((x + 1) % 2, (y + 1) % 2)]
    ident = lambda p: 4 * p[0] + 2 * p[1] + p[2]
    return dict(me=4 * x + 2 * y + c, sib=(x, y, 1 - c), sib_id=4 * x + 2 * y + 1 - c,
                same=[(cx, cy, c) for cx, cy in chips], same_ids=[ident((cx, cy, c)) for cx, cy in chips],
                other_ids=[ident((cx, cy, 1 - c)) for cx, cy in chips])


def _remote(src, dst, send_sem, recv_sem, dev):
    return pltpu.make_async_remote_copy(src_ref=src, dst_ref=dst, send_sem=send_sem, recv_sem=recv_sem, device_id=dev,
                                        device_id_type=pl.DeviceIdType.MESH)


def gather_start(zones, name):
    n = len(zones)

    def body(*refs):
        lands, send_sems, recv_sems, token = refs[:n], refs[n], refs[n + 1], refs[-1]
        pc = _chip_place()
        for j in range(n):
            own = lands[j].at[pc["me"]]
            for k, dev in enumerate([pc["sib"]] + pc["same"]):
                _remote(own, own, send_sems.at[4 * j + k], recv_sems.at[4 * j + k], dev).start()
        token[...] = jnp.zeros_like(token)

    res = pl.pallas_call(
        body, name=name,
        in_specs=[_HBM] * n,
        out_specs=(_SEMS, _SEMS, *[_HBM] * n, pl.BlockSpec(memory_space=pltpu.VMEM)),
        out_shape=(pltpu.SemaphoreType.DMA((4 * n,)), pltpu.SemaphoreType.DMA((4 * n,)),
                   *[pltpu.HBM(a.shape, a.dtype) for a in zones], jax.ShapeDtypeStruct((8, BLK), F32)),
        input_output_aliases={i: 2 + i for i in range(n)},
        compiler_params=pltpu.CompilerParams(has_side_effects=_EFFECT),
    )(*[pltpu.with_memory_space_constraint(a, pltpu.HBM) for a in zones])
    return dict(send=res[0], recv=res[1], lands=list(res[2:2 + n]), token=res[-1])


def gather_relay(hd, idxs, name, after):
    n = len(idxs)

    def body(*refs):
        lands, send_sems, recv_sems = refs[:n], refs[n], refs[n + 1]
        fsend, frecv, token = refs[n + 3 + n], refs[n + 4 + n], refs[-1]
        pc = _chip_place()
        for p, j in enumerate(idxs):
            for k in range(3):
                _remote(lands[p].at[pc["me"]], lands[p].at[pc["same_ids"][k]], send_sems.at[4 * j + 1 + k],
                        recv_sems.at[4 * j + 1 + k], pc["same"][k]).wait_recv()
        for p in range(n):
            for k in range(3):
                got = lands[p].at[pc["same_ids"][k]]
                _remote(got, got, fsend.at[3 * p + k], frecv.at[3 * p + k], pc["sib"]).start()
        token[...] = jnp.zeros_like(token)

    ops = [hd["lands"][j] for j in idxs]
    res = pl.pallas_call(
        body, name=name,
        in_specs=[_HBM] * n + [_SEMS, _SEMS, _ANY],
        out_specs=(*[_HBM] * n, _SEMS, _SEMS, pl.BlockSpec(memory_space=pltpu.VMEM)),
        out_shape=(*[pltpu.HBM(a.shape, a.dtype) for a in ops], pltpu.SemaphoreType.DMA((3 * n,)),
                   pltpu.SemaphoreType.DMA((3 * n,)), jax.ShapeDtypeStruct((8, BLK), F32)),
        input_output_aliases={i: i for i in range(n)},
        compiler_params=pltpu.CompilerParams(has_side_effects=_EFFECT),
    )(*ops, hd["send"], hd["recv"], after)
    return dict(lands=list(res[:n]), fsend=res[n], frecv=res[n + 1], token=res[-1])


def gather_wait(hd, rl, idxs, name, after):
    n = len(idxs)

    def body(*refs):
        lands, send_sems, recv_sems, fsend, frecv = refs[:n], refs[n], refs[n + 1], refs[n + 2], refs[n + 3]
        pc = _chip_place()
        for p, j in enumerate(idxs):
            own = lands[p].at[pc["me"]]
            for k, dev in enumerate([pc["sib"]] + pc["same"]):
                _remote(own, own, send_sems.at[4 * j + k], recv_sems.at[4 * j + k], dev).wait_send()
            _remote(own, lands[p].at[pc["sib_id"]], send_sems.at[4 * j], recv_sems.at[4 * j], pc["sib"]).wait_recv()
            for k in range(3):
                cp = _remote(lands[p].at[pc["same_ids"][k]], lands[p].at[pc["other_ids"][k]], fsend.at[3 * p + k],
                             frecv.at[3 * p + k], pc["sib"])
                cp.wait_send()
                cp.wait_recv()

    res = pl.pallas_call(
        body, name=name,
        in_specs=[_HBM] * n + [_SEMS, _SEMS, _SEMS, _SEMS, _ANY],
        out_specs=[_HBM] * n,
        out_shape=[pltpu.HBM(a.shape, a.dtype) for a in rl["lands"]],
        input_output_aliases={i: i for i in range(n)},
        compiler_params=pltpu.CompilerParams(has_side_effects=_EFFECT),
    )(*rl["lands"], hd["send"], hd["recv"], rl["fsend"], rl["frecv"], after)
    return list(res)


def _pad_cols(a, n):
    return jnp.pad(a, ((0, 0),) * (a.ndim - 1) + ((0, n - a.shape[-1]),))


def w_in_to_padded(w):
    z = lambda n: jnp.zeros(w.shape[:-1] + (n,), w.dtype)
    return jnp.concatenate([
        w[..., 0:1280], w[..., 1288:2056], w[..., 2060:2316], w[..., 2316:2444],
        w[..., 1280:1288], w[..., 2056:2060], z(SM_KR - SM_F - FOX_H), w[..., 2444:2476], z(BLK - SM_KR - MLA_ROPE)], axis=-1)


def w_in_from_padded(g):
    s = C_SM
    return jnp.concatenate([
        g[..., 0:1280], g[..., s + SM_DT:s + SM_DT + 8], g[..., 1280:2048], g[..., s + SM_F:s + SM_F + 4],
        g[..., 2048:2304], g[..., 2304:2432], g[..., s + SM_KR:s + SM_KR + MLA_ROPE]], axis=-1)


def _unshard_cols(gth):
    n, r, c = gth.shape
    return jnp.transpose(gth, (1, 0, 2)).reshape(r, n * c)


def _shard_cols(full):
    r, nc = full.shape
    return jnp.transpose(full.reshape(r, N_DEV, nc // N_DEV), (1, 0, 2))


def mla_weights(uq_g, ukv_g):
    uq = _unshard_cols(uq_g)
    dqh = MLA_NOPE + MLA_ROPE
    wq = jnp.concatenate([_pad_cols(uq[:, dqh * h:dqh * (h + 1)], BLK) for h in range(MLA_H)], axis=1)
    wk = jnp.concatenate([_pad_cols(ukv_g[2 * h], BLK) for h in range(MLA_H)], axis=1)
    wv = jnp.concatenate([ukv_g[2 * h + 1] for h in range(MLA_H)], axis=1)
    return wq, wk, wv


def mla_weight_grads(dwq, dwk, dwv):
    dqh = MLA_NOPE + MLA_ROPE
    duq = _shard_cols(jnp.concatenate([dwq[:, BLK * h:BLK * h + dqh] for h in range(MLA_H)], axis=1))
    parts = []
    for h in range(MLA_H):
        parts += [dwk[:, BLK * h:BLK * h + MLA_NOPE], dwv[:, MLA_V * h:MLA_V * (h + 1)]]
    return duq, jnp.stack(parts, axis=0)


def rope_tables(t):
    pos = (jnp.arange(t, dtype=jnp.int32) - PAD).astype(F32)
    inv_freq = 1.0 / (10000.0 ** (jnp.arange(0, MLA_ROPE, 2, dtype=F32) / MLA_ROPE))
    ang = pos[:, None] * inv_freq[None, :]
    cos, sin = jnp.cos(ang), jnp.sin(ang)
    one, zero = jnp.ones((t, SM_KR), F32), jnp.zeros((t, SM_KR), F32)
    tail = BLK - SM_KR - MLA_ROPE
    cosq = jnp.concatenate([one, cos, cos, jnp.ones((t, tail), F32)], axis=1)
    sinq = jnp.concatenate([zero, -sin, sin, jnp.zeros((t, tail), F32)], axis=1)
    return cosq, sinq


def _lanes(v, off=0):
    return jnp.pad(v.astype(F32), (off, BLK - off - v.shape[0]))[None, :]


def layer_fwd(x, ln, hb, getw, tabs, ahead):
    sv = {"h0b": hb}
    W = dict(getw("ffn1", hb))
    ln1 = (W["ln1_g"], W["ln1_b"])
    u, v, r1, h1b = ffn_fwd_seq(x, ln, W["g1"], W["u1"], W["d1"], ln1)
    sv.update(u1=u, v1=v, r1=r1, h1b=h1b)
    W.update(getw("mix", h1b))
    ln2 = (W["ln2_g"], W["ln2_b"])
    proj = mm_nn(h1b, W["w_in"])
    xa = conv_fwd(proj, W["conv_w"], W["conv_b"])
    y_ssd, sprev = ssd_fwd(xa, proj, W["dtb"], W["alog"], W["dskip"], W["normg"])
    c_col, c_row, c_keys = fox_pre(proj, W["fb"])
    y_fox, lse_f = attn_fwd(proj, proj, proj, C_FQ // 256, C_FK // 256, C_FV // 256, FOX_H, FOX_DH, FOX_DH,
                            FOX_DH ** -0.5, c_col, c_row, SM_F)
    ahead(0, "ffn2", y_fox)
    q, k, vv, cqn, ckvn = mla_pre(proj, W["qg"], W["kvg"], W["wq"], W["wk"], W["wv"], *tabs)
    y_mla, lse_m = attn_fwd(q, k, vv, 0, 0, 0, MLA_H, BLK, MLA_V, (MLA_NOPE + MLA_ROPE) ** -0.5)
    mixcat = jnp.concatenate([y_ssd, y_fox, y_mla], axis=1)
    r2, h2b = mm_res_ln(mixcat, W["w_out"], r1, ln1, ln2)
    sv.update(proj=proj, xa=xa, sprev=sprev, c_keys=c_keys, c_row=c_row, lse_f=lse_f, q=q, k=k, v=vv, cqn=cqn, ckvn=ckvn,
              lse_m=lse_m, mixcat=mixcat, r2=r2, h2b=h2b)
    W.update(getw("ffn2", h2b))
    ahead(1, "ffn1", h2b)
    ln3 = (W["ln3_g"], W["ln3_b"])
    u, v, r3, h3b = ffn_fwd_seq(r2, ln2, W["g2"], W["u2"], W["d2"], ln3)
    sv.update(u2=u, v2=v, r3=r3, W=W)
    return r3, ln3, h3b, sv


def ffn_bwd(parts, r, gamma, hb_in, u, v, wg, wu, wd, after=None):
    dh, dwg, dwu, dwd, dg, db = ffn_bwd_seq(parts, r, gamma, hb_in, u, v, wg, wu, wd, after)
    return dh, dict(d=dwd, g=dwg, u=dwu, ln_g=dg, ln_b=db)


def layer_bwd(parts, sv, emit, tabs, after):
    G = {}
    W = sv["W"]
    dh2, g2 = ffn_bwd(parts, sv["r3"], W["ln3_g"], sv["h2b"], sv["u2"], sv["v2"], W["g2"], W["u2"], W["d2"], after)
    G.update(g2=g2["g"], u2=g2["u"], d2=g2["d"], ln3_g=g2["ln_g"], ln3_b=g2["ln_b"])
    tok = emit("ffn2", G)
    dr2, dmixb, G["ln2_g"], G["ln2_b"] = ln_bwd([(dh2, 1.0)], sv["r2"], W["ln2_g"], 1.0, tok)
    dmc = mm_nt_reduce([(dmixb[None], W["w_out"][None])], D)
    G["w_out"] = mm_tn(sv["mixcat"][None], dmixb[None])[0]
    proj = sv["proj"]
    dxa, dz, dsm, G["normg"], G["dskip"], G["alog"], G["dtb"] = ssd_bwd(
        dmc, sv["xa"], proj, sv["sprev"], W["dtb"], W["alog"], W["dskip"], W["normg"])
    dxbc, G["conv_w"], G["conv_b"] = conv_bwd(dxa, proj, W["conv_w"], W["conv_b"])
    dfq, dfk, dfv, dcq, dck = attn_bwd(proj, proj, proj, dmc, sv["lse_f"], sv["mixcat"], C_FQ // 256, C_FK // 256,
                                       C_FV // 256, 2, 2, FOX_H, FOX_DH, FOX_DH, FOX_DH ** -0.5, sv["c_keys"], sv["c_row"])
    dsm, G["fb"] = fox_pre_bwd(dcq, dck, proj, W["fb"], dsm)
    dq, dk, dv = attn_bwd(sv["q"], sv["k"], sv["v"], dmc, sv["lse_m"], sv["mixcat"], 0, 0, 0, 3, 3, MLA_H, BLK, MLA_V,
                          (MLA_NOPE + MLA_ROPE) ** -0.5)
    dcql, dckv, dsm, G["wq"], G["wk"], G["wv"], G["qg"], G["kvg"] = mla_pre_bwd(
        dq, dk, dv, proj, sv["cqn"], sv["ckvn"], W["qg"], W["kvg"], W["wq"], W["wk"], W["wv"], *tabs, dsm)
    dproj = jnp.concatenate([dz, dxbc, dfq, dfk, dfv, dcql, dckv, dsm], axis=1).astype(BF16)
    dh1p = mm_nt_reduce([(dproj[None], W["w_in"][None])], D)
    G["w_in"] = mm_tn(sv["h1b"][None], dproj[None])[0]
    tok = emit("mix", G)
    dh0, g1 = ffn_bwd([(dr2, ALPHA), (dh1p, 1.0)], sv["r1"], W["ln1_g"], sv["h0b"], sv["u1"], sv["v1"],
                      W["g1"], W["u1"], W["d1"], tok)
    G.update(g1=g1["g"], u1=g1["u"], d1=g1["d"], ln1_g=g1["ln_g"], ln1_b=g1["ln_b"])
    tok = emit("ffn1", G)
    return [(dh0, 1.0)], G, tok


def local_step(x, target, meta_full, getw, emit, ahead=lambda l, stage, after: None):
    t = x.shape[0] + BLK
    tabs = rope_tables(t)
    xr, hb = build_h0(meta_full, x)
    ln = None
    saved = []
    for l in range(NL):
        xr, ln, hb, sv = layer_fwd(xr, ln, hb, functools.partial(getw, l), tabs,
                                   lambda dl, stage, after, l=l: ahead(l + dl, stage, after))
        saved.append(sv)
    dy, loss = loss_head(xr, ln, target)
    parts = [(dy, 1.0)]
    grads = [None] * NL
    tok = None
    for l in range(NL - 1, -1, -1):
        parts, grads[l], tok = layer_bwd(parts, saved[l], functools.partial(emit, l), tabs, tok)
    gx, gmeta = split_dh0(parts[0][0], tok)
    return loss, gx, gmeta, grads


_SMALL = ["ln1_g", "ln1_b", "ln2_g", "ln2_b", "ln3_g", "ln3_b", "conv_b", "ssd_norm_g", "mla_q_norm_g",
          "mla_kv_norm_g", "dt_bias", "a_log", "d_skip", "fox_f_b"]
_SMALL_ROWS = 8
_BIG = ["ffn1_w_gate", "ffn1_w_up", "ffn1_w_down", "w_in", "conv_w", "mla_w_uq", "mla_w_ukv", "w_out",
        "ffn2_w_gate", "ffn2_w_up", "ffn2_w_down"]
_NAMES = ["meta", "ffn1_w_gate", "ffn1_w_up", "ffn1_w_down", "ln1_g", "ln1_b", "w_in", "conv_w", "conv_b", "dt_bias",
          "a_log", "d_skip", "ssd_norm_g", "fox_f_b", "mla_q_norm_g", "mla_w_uq", "mla_kv_norm_g", "mla_w_ukv", "w_out",
          "ln2_g", "ln2_b", "ffn2_w_gate", "ffn2_w_up", "ffn2_w_down", "ln3_g", "ln3_b"]


def pack_small(p):
    flat = jnp.concatenate([p[n].astype(F32) for n in _SMALL], axis=1)
    return _pad_cols(flat, _SMALL_ROWS * D).reshape(NL * _SMALL_ROWS, D)


def unpack_small(a, like):
    flat = a.reshape(NL, _SMALL_ROWS * D)
    out, at = {}, 0
    for n in _SMALL:
        out[n] = flat[:, at:at + like[n].shape[1]]
        at += like[n].shape[1]
    return out


_STAGES = {"ffn1": ["ffn1_w_gate", "ffn1_w_up", "ffn1_w_down"],
           "mix": ["w_in", "conv_w", "mla_w_uq", "mla_w_ukv", "w_out"],
           "ffn2": ["ffn2_w_gate", "ffn2_w_up", "ffn2_w_down"]}


_FFN_T = ("ffn1_w_gate", "ffn1_w_up", "ffn2_w_gate", "ffn2_w_up")


def stage_weights(l, stage, g, rep):
    if stage != "mix":
        i = stage[3]
        return {"g" + i: g[f"ffn{i}_w_gate"].reshape(D_FF, D), "u" + i: g[f"ffn{i}_w_up"].reshape(D_FF, D),
                "d" + i: g[f"ffn{i}_w_down"].reshape(D_FF, D),
                "ln1_g" if i == "1" else "ln3_g": rep["ln1_g" if i == "1" else "ln3_g"][l][None, :],
                "ln1_b" if i == "1" else "ln3_b": rep["ln1_b" if i == "1" else "ln3_b"][l][None, :]}
    W = {}
    W["w_in"] = g["w_in"].reshape(D, N_INP)
    W["w_out"] = g["w_out"].reshape(D, D)
    W["wq"], W["wk"], W["wv"] = mla_weights(g["mla_w_uq"], g["mla_w_ukv"])
    W["conv_w"] = _unshard_cols(g["conv_w"])
    for k in ("ln2_g", "ln2_b", "conv_b"):
        W[k] = rep[k][l][None, :]
    W["normg"] = rep["ssd_norm_g"][l][None, :]
    W["qg"] = rep["mla_q_norm_g"][l][None, :]
    W["kvg"] = rep["mla_kv_norm_g"][l][None, :]
    W["dtb"] = _lanes(rep["dt_bias"][l], SM_DT)
    W["alog"] = _lanes(rep["a_log"][l], SM_DT)
    W["dskip"] = _lanes(rep["d_skip"][l], SM_DT)
    W["fb"] = _lanes(rep["fox_f_b"][l], SM_F)
    return W


def small_grads(G):
    return {"ln1_g": G["ln1_g"][0], "ln1_b": G["ln1_b"][0], "ln2_g": G["ln2_g"][0], "ln2_b": G["ln2_b"][0],
            "ln3_g": G["ln3_g"][0], "ln3_b": G["ln3_b"][0], "conv_b": G["conv_b"][0], "ssd_norm_g": G["normg"][0],
            "mla_q_norm_g": G["qg"][0], "mla_kv_norm_g": G["kvg"][0], "dt_bias": G["dtb"][0, :SSD_H],
            "a_log": G["alog"][0, :SSD_H], "d_skip": G["dskip"][0, :SSD_H], "fox_f_b": G["fb"][0, SM_F:SM_F + FOX_H]}


def big_grads(G, stage):
    if stage != "mix":
        i = stage[-1]
        return {f"ffn{i}_w_{k}": G[k[0] + i].reshape(N_DEV, HS, D) for k in ("gate", "up", "down")}
    duq, dukv = mla_weight_grads(G["wq"], G["wk"], G["wv"])
    return {"w_in": G["w_in"].reshape(N_DEV, D // N_DEV, N_INP), "w_out": G["w_out"].reshape(N_DEV, D // N_DEV, D),
            "mla_w_uq": duq, "mla_w_ukv": dukv, "conv_w": _shard_cols(G["conv_w"])}


def kernel(x, meta, ffn1_w_gate, ffn1_w_up, ffn1_w_down, ln1_g, ln1_b, w_in, conv_w, conv_b, dt_bias, a_log, d_skip, ssd_norm_g, fox_f_b, mla_q_norm_g, mla_w_uq, mla_kv_norm_g, mla_w_ukv, w_out, ln2_g, ln2_b, ffn2_w_gate, ffn2_w_up, ffn2_w_down, ln3_g, ln3_b, loss_target, m_meta, m_ffn1_w_gate, m_ffn1_w_up, m_ffn1_w_down, m_ln1_g, m_ln1_b, m_w_in, m_conv_w, m_conv_b, m_dt_bias, m_a_log, m_d_skip, m_ssd_norm_g, m_fox_f_b, m_mla_q_norm_g, m_mla_w_uq, m_mla_kv_norm_g, m_mla_w_ukv, m_w_out, m_ln2_g, m_ln2_b, m_ffn2_w_gate, m_ffn2_w_up, m_ffn2_w_down, m_ln3_g, m_ln3_b, v_meta, v_ffn1_w_gate, v_ffn1_w_up, v_ffn1_w_down, v_ln1_g, v_ln1_b, v_w_in, v_conv_w, v_conv_b, v_dt_bias, v_a_log, v_d_skip, v_ssd_norm_g, v_fox_f_b, v_mla_q_norm_g, v_mla_w_uq, v_mla_kv_norm_g, v_mla_w_ukv, v_w_out, v_ln2_g, v_ln2_b, v_ffn2_w_gate, v_ffn2_w_up, v_ffn2_w_down, v_ln3_g, v_ln3_b):
    vals = (meta, ffn1_w_gate, ffn1_w_up, ffn1_w_down, ln1_g, ln1_b, w_in, conv_w, conv_b, dt_bias, a_log, d_skip, ssd_norm_g, fox_f_b, mla_q_norm_g, mla_w_uq, mla_kv_norm_g, mla_w_ukv, w_out, ln2_g, ln2_b, ffn2_w_gate, ffn2_w_up, ffn2_w_down, ln3_g, ln3_b)
    moms = (m_meta, m_ffn1_w_gate, m_ffn1_w_up, m_ffn1_w_down, m_ln1_g, m_ln1_b, m_w_in, m_conv_w, m_conv_b, m_dt_bias, m_a_log, m_d_skip, m_ssd_norm_g, m_fox_f_b, m_mla_q_norm_g, m_mla_w_uq, m_mla_kv_norm_g, m_mla_w_ukv, m_w_out, m_ln2_g, m_ln2_b, m_ffn2_w_gate, m_ffn2_w_up, m_ffn2_w_down, m_ln3_g, m_ln3_b)
    vars_ = (v_meta, v_ffn1_w_gate, v_ffn1_w_up, v_ffn1_w_down, v_ln1_g, v_ln1_b, v_w_in, v_conv_w, v_conv_b, v_dt_bias, v_a_log, v_d_skip, v_ssd_norm_g, v_fox_f_b, v_mla_q_norm_g, v_mla_w_uq, v_mla_kv_norm_g, v_mla_w_ukv, v_w_out, v_ln2_g, v_ln2_b, v_ffn2_w_gate, v_ffn2_w_up, v_ffn2_w_down, v_ln3_g, v_ln3_b)
    P = dict(zip(_NAMES, vals))
    M = dict(zip(_NAMES, moms))
    V = dict(zip(_NAMES, vars_))
    me = 4 * lax.axis_index("x") + 2 * lax.axis_index("y") + lax.axis_index("c")

    me_arr = me.astype(jnp.int32).reshape(1)
    for n in _FFN_T:
        P[n], M[n], V[n] = (jnp.swapaxes(a[n], 1, 2) for a in (P, M, V))
    src = dict(P)
    src["w_in"] = w_in_to_padded(P["w_in"])
    order = [("meta", 0)] + [(n, l) for l in range(NL) for names in _STAGES.values() for n in names]
    zone_of = {nl_: i for i, nl_ in enumerate(order)}
    zones = [place_own(P["meta"][None], 0, F32, me_arr)]
    zones += [place_own(src[n], l, F32 if n == "conv_w" else BF16, me_arr) for n, l in order[1:]]
    hg = gather_start(zones, "gather_start")
    relays = {}

    def ahead(l, stage, after):
        if l < NL and (l, stage) not in relays:
            idxs = [0] if stage == "meta" else [zone_of[(n, l)] for n in _STAGES[stage]]
            relays[(l, stage)] = (idxs, gather_relay(hg, idxs, f"gather_relay_{l}_{stage}", after))

    def arrived(l, stage, after):
        ahead(l, stage, after)
        idxs, rl = relays[(l, stage)]
        return gather_wait(hg, rl, idxs, f"gather_wait_{l}_{stage}", after)

    meta_full = _unshard_cols(arrived(0, "meta", hg["token"])[0])

    def getw(l, stage, after):
        return stage_weights(l, stage, dict(zip(_STAGES[stage], arrived(l, stage, after))), P)

    sent = {}

    def emit(l, stage, G):
        bg = big_grads(G, stage)
        sent[(l, stage)] = exchange_start("scatter", [bg[n] for n in _STAGES[stage]], f"scatter_start_{l}_{stage}")
        return sent[(l, stage)]["token"]

    loss, gx, gmeta, grads = local_step(x[0], loss_target[0], meta_full, getw, emit, ahead)

    small = jnp.concatenate([pack_small({n: jnp.stack([small_grads(g)[n] for g in grads]) for n in _SMALL}), gmeta], axis=0)
    hs = exchange_start("gather", [place_own(small[None], 0, F32, me_arr)], "small_start")

    out = {}
    after = hs["token"]
    for stage in ("ffn2", "mix", "ffn1"):
        names = _STAGES[stage]
        got = [exchange_wait(sent[(l, stage)], list(range(len(names))), f"scatter_wait_{l}_{stage}", after)
               for l in range(NL - 1, -1, -1)][::-1]
        for i, n in enumerate(names):
            own = [got[l][0][i] for l in range(NL)]
            recv = [got[l][1][i] for l in range(NL)]
            if n == "w_in":
                g = jnp.stack([w_in_from_padded(sum_slots(recv[l], own[l], me_arr)) for l in range(NL)])
                out[n] = (g,) + adamw(P[n], M[n], V[n], g=g)
            else:
                out[n] = adamw(P[n], M[n], V[n], recv=recv, own=own, me_arr=me_arr)
                if n in _FFN_T:
                    out[n] = tuple(jnp.swapaxes(a, 1, 2) for a in out[n])
        after = out[names[-1]][1]
    gsmall = sum_slots(exchange_wait(hs, [0], "small_wait", after)[1][0])
    gm = lax.dynamic_slice(gsmall[NL * _SMALL_ROWS:], (0, me * (D // N_DEV)), (N_META, D // N_DEV))
    out["meta"] = (gm,) + adamw(P["meta"], M["meta"], V["meta"], g=gm)
    gs = gsmall[:NL * _SMALL_ROWS]
    sd, sm_, sv_ = adamw(pack_small(P), pack_small(M), pack_small(V), g=gs)
    ups = [unpack_small(a, P) for a in (gs, sd, sm_, sv_)]
    for n in _SMALL:
        out[n] = tuple(u[n] for u in ups)

    loss_all = lax.psum(loss[0, 0], ("x", "y", "c"))
    flat = [loss_all, gx[None]]
    for k in range(4):
        flat += [out[n][k] for n in _NAMES]
    return tuple(flat)
```

```python
import functools

import jax
import jax.numpy as jnp
from jax import lax
from jax.experimental import pallas as pl
from jax.experimental.pallas import tpu as pltpu

F32, BF16 = jnp.float32, jnp.bfloat16
HI = lax.Precision.HIGHEST

N_DEV = 8
D = 1024
NL = 2
N_META = 16
BLK = 128
PAD = BLK - N_META
D_FF = 2816
HS = D_FF // N_DEV
SSD_H, SSD_P, SSD_N, SSD_G = 8, 64, 64, 2
SSD_D = SSD_H * SSD_P
CONV_K = 4
CONV_D = SSD_D + 2 * SSD_G * SSD_N
FOX_H, FOX_DH = 4, 64
MLA_H, MLA_QL, MLA_KVL, MLA_NOPE, MLA_ROPE, MLA_V = 4, 256, 128, 64, 32, 64
N_IN = 2476
C_Z, C_XBC, C_FQ, C_FK, C_FV, C_CQ, C_CKV, C_SM, N_INP = 0, 512, 1280, 1536, 1792, 2048, 2304, 2432, 2560
SM_DT, SM_F, SM_KR = 0, 8, 64
ALPHA = (2 * NL) ** 0.25
EPS = 1e-5
NEG = -1e30
LR, B1, B2, AEPS, WD, STEP = 0.001, 0.9, 0.999, 1e-08, 0.01, 10
VMEM_MB = 56


def _cp(*sem):
    return pltpu.CompilerParams(dimension_semantics=sem, vmem_limit_bytes=VMEM_MB << 20)


def _nn(a, b):
    return lax.dot_general(a, b, (((1,), (0,)), ((), ())), preferred_element_type=F32)


def _nt(a, b):
    return lax.dot_general(a, b, (((1,), (1,)), ((), ())), preferred_element_type=F32)


def _tn(a, b):
    return lax.dot_general(a, b, (((0,), (0,)), ((), ())), preferred_element_type=F32)


def _nn_hi(a, b):
    return lax.dot_general(a, b, (((1,), (0,)), ((), ())), precision=HI, preferred_element_type=F32)


def _row_tile(t):
    for d in range(640, 15, -16):
        if t % d == 0:
            return d
    raise ValueError(t)


def _sig(x):
    return 1.0 / (1.0 + jnp.exp(-x))


def _tri(lower=True):
    r = lax.broadcasted_iota(jnp.int32, (BLK, BLK), 0)
    c = lax.broadcasted_iota(jnp.int32, (BLK, BLK), 1)
    return (r >= c) if lower else (r <= c)


def build_h0(meta_full, x):
    s = x.shape[0]
    nb = s // BLK + 1

    def body(m_ref, x_ref, h_ref, hb_ref):
        i = pl.program_id(0)

        @pl.when(i == 0)
        def _():
            h = jnp.concatenate([jnp.zeros((PAD, D), F32), m_ref[...]], axis=0)
            h_ref[...] = h
            hb_ref[...] = h.astype(BF16)

        @pl.when(i > 0)
        def _():
            h_ref[...] = x_ref[...]
            hb_ref[...] = x_ref[...].astype(BF16)

    return pl.pallas_call(
        body, name="build_h0", grid=(nb,),
        in_specs=[pl.BlockSpec((N_META, D), lambda i: (0, 0)),
                  pl.BlockSpec((BLK, D), lambda i: (jnp.maximum(i - 1, 0), 0))],
        out_specs=[pl.BlockSpec((BLK, D), lambda i: (i, 0))] * 2,
        out_shape=[jax.ShapeDtypeStruct((nb * BLK, D), F32), jax.ShapeDtypeStruct((nb * BLK, D), BF16)],
        compiler_params=_cp("arbitrary"),
    )(meta_full, x)


FT = 256


def _layer_norm(r, gamma, beta):
    mu = jnp.mean(r, axis=1, keepdims=True)
    xc = r - mu
    var = jnp.mean(xc * xc, axis=1, keepdims=True)
    return xc * lax.rsqrt(var + EPS) * gamma + beta


def ffn_fwd_seq(x, ln_in, wg, wu, wd, ln_out):
    t = x.shape[0]
    f = wg.shape[0]
    nj, nr = f // FT, t // _row_tile(t)
    rc = t // nr
    plain = ln_in is None
    gi, bi = ln_out if plain else ln_in

    def body(x_hbm, gi_ref, bi_ref, go_ref, bo_ref, wg_ref, wu_ref, wd_ref, u_ref, v_ref, r_hbm, yb_hbm,
             acc, hbs, xbuf, sem_in, sem_out):
        j = pl.program_id(0)

        @pl.when(j == 0)
        def _():
            def fetch(k):
                return pltpu.make_async_copy(x_hbm.at[pl.ds(k * rc, rc)], xbuf.at[k % 2], sem_in.at[k % 2])

            fetch(0).start()
            for k in range(nr):
                if k + 1 < nr:
                    fetch(k + 1).start()
                fetch(k).wait()
                h = xbuf[k % 2]
                if not plain:
                    h = _layer_norm(h, gi_ref[...], bi_ref[...])
                acc[k * rc:(k + 1) * rc, :] = ALPHA * h
                hbs[k * rc:(k + 1) * rc, :] = h.astype(BF16)

        for k in range(nr):
            sl = slice(k * rc, (k + 1) * rc)
            h = hbs[sl, :]
            u = _nt(h, wg_ref[...])
            v = _nt(h, wu_ref[...])
            u_ref[sl, :] = u.astype(BF16)
            v_ref[sl, :] = v.astype(BF16)
            acc[sl, :] += _nn((0.5 * u * _sig(u) * v).astype(BF16), wd_ref[...])

        @pl.when(j == nj - 1)
        def _():
            r_cp = pltpu.make_async_copy(acc, r_hbm, sem_out.at[0])
            r_cp.start()
            for k in range(nr):
                sl = slice(k * rc, (k + 1) * rc)
                hbs[sl, :] = _layer_norm(acc[sl, :], go_ref[...], bo_ref[...]).astype(BF16)
            y_cp = pltpu.make_async_copy(hbs, yb_hbm, sem_out.at[1])
            y_cp.start()
            r_cp.wait()
            y_cp.wait()

    vec = pl.BlockSpec((1, D), lambda j: (0, 0))
    wsp = pl.BlockSpec((FT, D), lambda j: (j, 0))
    act = pl.BlockSpec((None, t, FT), lambda j: (j, 0, 0))
    return pl.pallas_call(
        body, name="ffn_fwd_seq", grid=(nj,),
        in_specs=[_ANY, vec, vec, vec, vec, wsp, wsp, wsp],
        out_specs=[act, act, _ANY, _ANY],
        out_shape=[jax.ShapeDtypeStruct((nj, t, FT), BF16), jax.ShapeDtypeStruct((nj, t, FT), BF16),
                   jax.ShapeDtypeStruct((t, D), F32), jax.ShapeDtypeStruct((t, D), BF16)],
        scratch_shapes=[pltpu.VMEM((t, D), F32), pltpu.VMEM((t, D), BF16), pltpu.VMEM((2, rc, D), F32),
                        pltpu.SemaphoreType.DMA((2,)), pltpu.SemaphoreType.DMA((2,))],
        compiler_params=_cp("arbitrary"),
    )(x, gi, bi, ln_out[0], ln_out[1], wg, wu, wd)


def ffn_bwd_seq(parts, r, gamma, hb, u, v, wg, wu, wd, after=None):
    nj, t, _ = u.shape
    f = nj * FT
    nr = t // _row_tile(t)
    rc = t // nr
    nc = t // BLK
    scales = [s for _, s in parts]
    npart = len(parts)
    extra = [] if after is None else [after]

    def body(*refs):
        refs = refs[len(extra):]
        p_hbm, refs = refs[:npart], refs[npart:]
        (r_hbm, g_ref, hb_hbm, u_ref, v_ref, wg_ref, wu_ref, wd_ref, dh_hbm, dwg_ref, dwu_ref, dwd_ref, dg_ref, db_ref,
         dfs, hbt, dft, dhacc, dus, dvs, acs, pbuf, rbuf, hbuf, sems, sem_out) = refs
        j = pl.program_id(0)

        @pl.when(j == 0)
        def _():
            def fetch(c):
                rows = pl.ds(c * BLK, BLK)
                cps = [pltpu.make_async_copy(p_hbm[p].at[rows], pbuf.at[c % 2, p], sems.at[c % 2, p]) for p in range(npart)]
                cps.append(pltpu.make_async_copy(r_hbm.at[rows], rbuf.at[c % 2], sems.at[c % 2, npart]))
                cps.append(pltpu.make_async_copy(hb_hbm.at[rows], hbuf.at[c % 2], sems.at[c % 2, npart + 1]))
                return cps

            for cp in fetch(0):
                cp.start()
            dg = jnp.zeros((1, D), F32)
            db = jnp.zeros((1, D), F32)
            for c in range(nc):
                if c + 1 < nc:
                    for cp in fetch(c + 1):
                        cp.start()
                for cp in fetch(c):
                    cp.wait()
                sl = slice(c * BLK, (c + 1) * BLK)
                dy = scales[0] * pbuf[c % 2, 0]
                for p in range(1, npart):
                    dy += scales[p] * pbuf[c % 2, p]
                rr = rbuf[c % 2]
                xc = rr - jnp.mean(rr, axis=1, keepdims=True)
                rstd = lax.rsqrt(jnp.mean(xc * xc, axis=1, keepdims=True) + EPS)
                xh = xc * rstd
                dxh = dy * g_ref[...]
                dr = rstd * (dxh - jnp.mean(dxh, axis=1, keepdims=True) - xh * jnp.mean(dxh * xh, axis=1, keepdims=True))
                dg += jnp.sum(dy * xh, axis=0, keepdims=True)
                db += jnp.sum(dy, axis=0, keepdims=True)
                dhacc[sl, :] = ALPHA * dr
                dfc = (0.5 * dr).astype(BF16)
                dfs[sl, :] = dfc
                dft[:, sl] = dfc.T
                hbt[:, sl] = hbuf[c % 2].T
            dg_ref[...] = dg
            db_ref[...] = db

        for k in range(nr):
            sl = slice(k * rc, (k + 1) * rc)
            da = _nt(dfs[sl, :], wd_ref[...])
            uu = u_ref[sl, :].astype(F32)
            vv = v_ref[sl, :].astype(F32)
            sg = _sig(uu)
            du = (da * vv * (sg * (1.0 + uu * (1.0 - sg)))).astype(BF16)
            dv = (da * uu * sg).astype(BF16)
            dus[sl, :] = du
            dvs[sl, :] = dv
            acs[sl, :] = (uu * sg * vv).astype(BF16)
            dhacc[sl, :] += _nn(du, wg_ref[...]) + _nn(dv, wu_ref[...])
        dwg_ref[...] = _nn(hbt[...], dus[...]).astype(BF16).T
        dwu_ref[...] = _nn(hbt[...], dvs[...]).astype(BF16).T
        dwd_ref[...] = _nn(dft[...], acs[...]).astype(BF16).T

        @pl.when(j == nj - 1)
        def _():
            cp = pltpu.make_async_copy(dhacc, dh_hbm, sem_out.at[0])
            cp.start()
            cp.wait()

    vec = pl.BlockSpec((1, D), lambda j: (0, 0))
    wsp = pl.BlockSpec((FT, D), lambda j: (j, 0))
    act = pl.BlockSpec((None, t, FT), lambda j: (j, 0, 0))
    return pl.pallas_call(
        body, name="ffn_bwd_seq", grid=(nj,),
        in_specs=[_ANY] * (len(extra) + npart + 1) + [vec, _ANY, act, act, wsp, wsp, wsp],
        out_specs=[_ANY, wsp, wsp, wsp, vec, vec],
        out_shape=[jax.ShapeDtypeStruct((t, D), F32)] + [jax.ShapeDtypeStruct((f, D), BF16)] * 3
        + [jax.ShapeDtypeStruct((1, D), F32)] * 2,
        scratch_shapes=[pltpu.VMEM((t, D), BF16), pltpu.VMEM((D, t), BF16), pltpu.VMEM((D, t), BF16),
                        pltpu.VMEM((t, D), F32), pltpu.VMEM((t, FT), BF16), pltpu.VMEM((t, FT), BF16),
                        pltpu.VMEM((t, FT), BF16), pltpu.VMEM((2, npart, BLK, D), F32), pltpu.VMEM((2, BLK, D), F32),
                        pltpu.VMEM((2, BLK, D), BF16), pltpu.SemaphoreType.DMA((2, npart + 2)),
                        pltpu.SemaphoreType.DMA((1,))],
        compiler_params=_cp("arbitrary"),
    )(*extra, *[p for p, _ in parts], r, gamma, hb, u, v, wg, wu, wd)


def mm_res_ln(a, b, x, ln_in, ln_out):
    t, k = a.shape
    tm = _row_tile(t)

    def body(a_ref, b_ref, x_ref, gi_ref, bi_ref, go_ref, bo_ref, r_ref, yb_ref):
        r = ALPHA * _layer_norm(x_ref[...], gi_ref[...], bi_ref[...]) + _nn(a_ref[...], b_ref[...])
        r_ref[...] = r
        yb_ref[...] = _layer_norm(r, go_ref[...], bo_ref[...]).astype(BF16)

    row = pl.BlockSpec((tm, D), lambda i: (i, 0))
    vec = pl.BlockSpec((1, D), lambda i: (0, 0))
    return pl.pallas_call(
        body, name="mm_res_ln", grid=(t // tm,),
        in_specs=[pl.BlockSpec((tm, k), lambda i: (i, 0)), pl.BlockSpec((k, D), lambda i: (0, 0)), row, vec, vec, vec, vec],
        out_specs=[row, row],
        out_shape=[jax.ShapeDtypeStruct((t, D), F32), jax.ShapeDtypeStruct((t, D), BF16)],
        compiler_params=_cp("arbitrary"),
    )(a, b, x, ln_in[0], ln_in[1], ln_out[0], ln_out[1])


def mm_nn(a, b):
    t, k = a.shape
    n = tn = b.shape[1]
    tm = _row_tile(t)

    def body(a_ref, b_ref, o_ref):
        o_ref[...] = _nn(a_ref[...], b_ref[...])

    return pl.pallas_call(
        body, name="mm_nn", grid=(t // tm, n // tn),
        in_specs=[pl.BlockSpec((tm, k), lambda i, j: (i, 0)), pl.BlockSpec((k, tn), lambda i, j: (0, j))],
        out_specs=pl.BlockSpec((tm, tn), lambda i, j: (i, j)),
        out_shape=jax.ShapeDtypeStruct((t, n), F32),
        compiler_params=_cp("arbitrary", "arbitrary"),
    )(a, b)


def mm_nt_reduce(pairs, n):
    g, t, _ = pairs[0][0].shape
    tm = _row_tile(t)
    npair = len(pairs)

    def body(*refs):
        o_ref = refs[-1]
        gi = pl.program_id(1)
        tot = _nt(refs[0][...], refs[1][...])
        for p in range(1, npair):
            tot += _nt(refs[2 * p][...], refs[2 * p + 1][...])

        @pl.when(gi == 0)
        def _():
            o_ref[...] = tot

        @pl.when(gi > 0)
        def _():
            o_ref[...] += tot

    in_specs, args = [], []
    for x, w in pairs:
        k = x.shape[2]
        in_specs += [pl.BlockSpec((None, tm, k), lambda i, gi: (gi, i, 0)),
                     pl.BlockSpec((None, n, k), lambda i, gi: (gi, 0, 0))]
        args += [x, w]
    return pl.pallas_call(
        body, name="mm_nt_reduce", grid=(t // tm, g),
        in_specs=in_specs, out_specs=pl.BlockSpec((tm, n), lambda i, gi: (i, 0)),
        out_shape=jax.ShapeDtypeStruct((t, n), F32),
        compiler_params=_cp("arbitrary", "arbitrary"),
    )(*args)


def mm_tn(x, y, out_dtype=BF16):
    gx, t, k = x.shape
    gy, _, n = y.shape
    g = max(gx, gy)
    tm = _row_tile(t)
    nt = t // tm

    def body(x_ref, y_ref, o_ref, acc):
        i = pl.program_id(1)

        @pl.when(i == 0)
        def _():
            acc[...] = jnp.zeros_like(acc)

        acc[...] += _tn(x_ref[...], y_ref[...])

        @pl.when(i == nt - 1)
        def _():
            o_ref[...] = acc[...].astype(out_dtype)

    return pl.pallas_call(
        body, name="mm_tn", grid=(g, nt),
        in_specs=[pl.BlockSpec((None, tm, k), (lambda gi, i: (gi, i, 0)) if gx > 1 else (lambda gi, i: (0, i, 0))),
                  pl.BlockSpec((None, tm, n), (lambda gi, i: (gi, i, 0)) if gy > 1 else (lambda gi, i: (0, i, 0)))],
        out_specs=pl.BlockSpec((None, k, n), lambda gi, i: (gi, 0, 0)),
        out_shape=jax.ShapeDtypeStruct((g, k, n), out_dtype),
        scratch_shapes=[pltpu.VMEM((k, n), F32)],
        compiler_params=_cp("arbitrary", "arbitrary"),
    )(x, y)


def ln_bwd(parts, r, gamma, out_scale, after=None):
    t = r.shape[0]
    tm = _row_tile(t)
    scales = [s for _, s in parts]
    npart = len(parts)
    extra = [] if after is None else [after]

    def body(*refs):
        refs = refs[len(extra):]
        r_ref, g_ref = refs[npart], refs[npart + 1]
        dr_ref, drb_ref, dg_ref, db_ref = refs[npart + 2:]
        i = pl.program_id(0)
        dy = scales[0] * refs[0][...]
        for p in range(1, npart):
            dy += scales[p] * refs[p][...]
        rr = r_ref[...]
        mu = jnp.mean(rr, axis=1, keepdims=True)
        xc = rr - mu
        rstd = lax.rsqrt(jnp.mean(xc * xc, axis=1, keepdims=True) + EPS)
        xh = xc * rstd
        dxh = dy * g_ref[...]
        m1 = jnp.mean(dxh, axis=1, keepdims=True)
        m2 = jnp.mean(dxh * xh, axis=1, keepdims=True)
        dr = rstd * (dxh - m1 - xh * m2)
        dr_ref[...] = dr
        drb_ref[...] = (out_scale * dr).astype(BF16)
        dg = jnp.sum(dy * xh, axis=0, keepdims=True)
        db = jnp.sum(dy, axis=0, keepdims=True)

        @pl.when(i == 0)
        def _():
            dg_ref[...] = dg
            db_ref[...] = db

        @pl.when(i > 0)
        def _():
            dg_ref[...] += dg
            db_ref[...] += db

    row = pl.BlockSpec((tm, D), lambda i: (i, 0))
    vec = pl.BlockSpec((1, D), lambda i: (0, 0))
    return pl.pallas_call(
        body, name="ln_bwd", grid=(t // tm,),
        in_specs=[_ANY] * len(extra) + [row] * (npart + 1) + [vec],
        out_specs=[row, row, vec, vec],
        out_shape=[jax.ShapeDtypeStruct((t, D), F32), jax.ShapeDtypeStruct((t, D), BF16),
                   jax.ShapeDtypeStruct((1, D), F32), jax.ShapeDtypeStruct((1, D), F32)],
        compiler_params=_cp("arbitrary"),
    )(*extra, *[p for p, _ in parts], r, gamma)


def loss_head(r, ln, target):
    t = r.shape[0]
    nb = t // BLK

    def body(r_ref, g_ref, b_ref, t_ref, dy_ref, l_ref):
        i = pl.program_id(0)

        @pl.when(i == 0)
        def _():
            dy_ref[...] = jnp.zeros_like(dy_ref)
            l_ref[...] = jnp.zeros_like(l_ref)

        @pl.when(i > 0)
        def _():
            err = _layer_norm(r_ref[...], g_ref[...], b_ref[...]) - t_ref[...]
            dy_ref[...] = err * (1.0 / D)
            l_ref[...] += (0.5 / D) * jnp.sum(err * err, keepdims=True)

    vec = pl.BlockSpec((1, D), lambda i: (0, 0))
    return pl.pallas_call(
        body, name="loss_head", grid=(nb,),
        in_specs=[pl.BlockSpec((BLK, D), lambda i: (i, 0)), vec, vec,
                  pl.BlockSpec((BLK, D), lambda i: (jnp.maximum(i - 1, 0), 0))],
        out_specs=[pl.BlockSpec((BLK, D), lambda i: (i, 0)), pl.BlockSpec((1, 1), lambda i: (0, 0))],
        out_shape=[jax.ShapeDtypeStruct((t, D), F32), jax.ShapeDtypeStruct((1, 1), F32)],
        compiler_params=_cp("arbitrary"),
    )(r, ln[0], ln[1], target)


def split_dh0(dh0, after=None):
    t = dh0.shape[0]
    nb = t // BLK
    extra = [] if after is None else [after]

    def body(*refs):
        a_ref, gx_ref, gm_ref = refs[len(extra):]
        i = pl.program_id(0)
        tot = a_ref[...]

        @pl.when(i == 0)
        def _():
            gm_ref[...] = tot[PAD:, :]

        @pl.when(i > 0)
        def _():
            gx_ref[...] = tot

    blk = pl.BlockSpec((BLK, D), lambda i: (i, 0))
    return pl.pallas_call(
        body, name="split_dh0", grid=(nb,),
        in_specs=[_ANY] * len(extra) + [blk],
        out_specs=[pl.BlockSpec((BLK, D), lambda i: (jnp.maximum(i - 1, 0), 0)),
                   pl.BlockSpec((N_META, D), lambda i: (0, 0))],
        out_shape=[jax.ShapeDtypeStruct((t - BLK, D), F32), jax.ShapeDtypeStruct((N_META, D), F32)],
        compiler_params=_cp("arbitrary"),
    )(*extra, dh0)


def _valid_rows(nrows, first_row):
    return (first_row + lax.broadcasted_iota(jnp.int32, (nrows, 1), 0)) >= PAD


def conv_fwd(proj, conv_w, conv_b):
    t = proj.shape[0]
    c0 = C_XBC // BLK

    def body(x_ref, w_ref, b_ref, o_ref):
        ok = _valid_rows(t, 0)
        x = jnp.where(ok, x_ref[...], 0.0)
        w = w_ref[...]
        acc = b_ref[...] + w[CONV_K - 1:CONV_K, :] * x
        for s in range(1, CONV_K):
            acc += w[CONV_K - 1 - s:CONV_K - s, :] * pltpu.roll(x, s, 0)
        o_ref[...] = jnp.where(ok, acc * _sig(acc), 0.0)

    return pl.pallas_call(
        body, name="conv_fwd", grid=(CONV_D // BLK,),
        in_specs=[pl.BlockSpec((t, BLK), lambda j: (0, c0 + j)),
                  pl.BlockSpec((CONV_K, BLK), lambda j: (0, j)), pl.BlockSpec((1, BLK), lambda j: (0, j))],
        out_specs=pl.BlockSpec((t, BLK), lambda j: (0, j)),
        out_shape=jax.ShapeDtypeStruct((t, CONV_D), F32),
        compiler_params=_cp("arbitrary"),
    )(proj, conv_w, conv_b)


def conv_bwd(dxa, proj, conv_w, conv_b):
    t = proj.shape[0]
    c0 = C_XBC // BLK

    def body(d_ref, x_ref, w_ref, b_ref, dx_ref, dw_ref, db_ref):
        ok = _valid_rows(t, 0)
        x = jnp.where(ok, x_ref[...], 0.0)
        w = w_ref[...]
        xs = [x] + [pltpu.roll(x, s, 0) for s in range(1, CONV_K)]
        acc = b_ref[...] + w[CONV_K - 1:CONV_K, :] * x
        for s in range(1, CONV_K):
            acc += w[CONV_K - 1 - s:CONV_K - s, :] * xs[s]
        sg = _sig(acc)
        dxc = jnp.where(ok, d_ref[...] * (sg * (1.0 + acc * (1.0 - sg))), 0.0)
        db_ref[...] = jnp.sum(dxc, axis=0, keepdims=True)
        dw_ref[...] = jnp.concatenate(
            [jnp.sum(dxc * xs[CONV_K - 1 - k], axis=0, keepdims=True) for k in range(CONV_K)], axis=0)
        dx = w[CONV_K - 1:CONV_K, :] * dxc
        for s in range(1, CONV_K):
            dx += w[CONV_K - 1 - s:CONV_K - s, :] * pltpu.roll(dxc, t - s, 0)
        dx_ref[...] = jnp.where(ok, dx, 0.0)

    col = pl.BlockSpec((t, BLK), lambda j: (0, j))
    return pl.pallas_call(
        body, name="conv_bwd", grid=(CONV_D // BLK,),
        in_specs=[col, pl.BlockSpec((t, BLK), lambda j: (0, c0 + j)),
                  pl.BlockSpec((CONV_K, BLK), lambda j: (0, j)), pl.BlockSpec((1, BLK), lambda j: (0, j))],
        out_specs=[col, pl.BlockSpec((CONV_K, BLK), lambda j: (0, j)), pl.BlockSpec((1, BLK), lambda j: (0, j))],
        out_shape=[jax.ShapeDtypeStruct((t, CONV_D), F32), jax.ShapeDtypeStruct((CONV_K, CONV_D), F32),
                   jax.ShapeDtypeStruct((1, CONV_D), F32)],
        compiler_params=_cp("arbitrary"),
    )(dxa, proj, conv_w, conv_b)


def _softplus(x):
    return jnp.maximum(x, 0.0) + jnp.log(1.0 + jnp.exp(-jnp.abs(x)))


GW = SSD_D // SSD_G
HPG = SSD_H // SSD_G


def _head_expand():
    r = lax.broadcasted_iota(jnp.int32, (BLK, SSD_D), 0)
    c = lax.broadcasted_iota(jnp.int32, (BLK, SSD_D), 1)
    rt = lax.broadcasted_iota(jnp.int32, (SSD_D, BLK), 0)
    ct = lax.broadcasted_iota(jnp.int32, (SSD_D, BLK), 1)
    return (c // SSD_P == r).astype(F32), (rt // SSD_P == ct).astype(F32)


def _ssd_chunk(xa, sm, dtb, alog, dskip, ok, sp):
    e, et = _head_expand()
    dt = jnp.where(ok, _softplus(sm + dtb), 0.0)
    amat = -jnp.exp(alog)
    tri = _tri()
    ac = _nn_hi(tri.astype(F32), dt * amat)
    act = ac.T
    ace, dte, dse = _nn_hi(ac, e), _nn_hi(dt, e), _nn_hi(dskip, e)
    laste = ace[BLK - 1:BLK, :]
    ee, dece, gle = jnp.exp(ace), jnp.exp(laste - ace), jnp.exp(laste)
    xs = xa[:, :SSD_D]
    xdt = xs * dte
    decx = dece * xdt
    xdtb = xdt.astype(BF16)
    d = dict(e=e, et=et, dt=dt, amat=amat, tri=tri, ac=ac, act=act, dte=dte, dse=dse, ee=ee, dece=dece, gle=gle, xs=xs,
             xdt=xdt, xdtb=xdtb, decx=decx, bg=[], cg=[], cb=[], yo=[], seg=[], m=[], new_s=[])
    ys = []
    for g in range(SSD_G):
        cols = slice(GW * g, GW * (g + 1))
        bg = xa[:, SSD_D + SSD_N * g:SSD_D + SSD_N * (g + 1)].astype(BF16)
        cg = xa[:, SSD_D + SSD_G * SSD_N + SSD_N * g:SSD_D + SSD_G * SSD_N + SSD_N * (g + 1)].astype(BF16)
        spg = sp[:, cols]
        sloc = _tn(bg, decx[:, cols].astype(BF16))
        yo = _nn(cg, spg.astype(BF16)) * ee[:, cols]
        cb = _nt(cg, bg)
        d["new_s"].append(gle[:, cols] * spg + sloc)
        yds = []
        for h in range(HPG * g, HPG * (g + 1)):
            seg = jnp.where(tri, jnp.exp(jnp.minimum(ac[:, h:h + 1] - act[h:h + 1, :], 0.0)), 0.0)
            m = cb * seg
            yds.append(_nn(m.astype(BF16), xdtb[:, SSD_P * h:SSD_P * (h + 1)]))
            d["seg"].append(seg)
            d["m"].append(m)
        ys.append(jnp.concatenate(yds, axis=1) + yo)
        for k, val in (("bg", bg), ("cg", cg), ("cb", cb), ("yo", yo)):
            d[k].append(val)
    d["y"] = jnp.concatenate(ys, axis=1) + dse * xs
    return d


def ssd_fwd(xa, proj, dtb, alog, dskip, normg):
    t = xa.shape[0]
    nb = t // BLK
    gw = SSD_D // SSD_G

    def body(xa_ref, z_ref, sm_ref, dtb_ref, al_ref, ds_ref, ng_ref, y_ref, sp_ref, st):
        c = pl.program_id(0)

        @pl.when(c == 0)
        def _():
            st[...] = jnp.zeros_like(st)

        ok = _valid_rows(BLK, c * BLK)
        sp = st[...]
        sp_ref[...] = sp
        d = _ssd_chunk(xa_ref[...], sm_ref[...], dtb_ref[...], al_ref[...], ds_ref[...], ok, sp)
        st[...] = jnp.concatenate(d["new_s"], axis=1)
        y = d["y"]
        z = z_ref[...]
        yg = y * (z * _sig(z))
        outs = []
        for g in range(SSD_G):
            v = yg[:, gw * g:gw * (g + 1)]
            outs.append(v * lax.rsqrt(jnp.mean(v * v, axis=1, keepdims=True) + EPS))
        y_ref[...] = (jnp.concatenate(outs, axis=1) * ng_ref[...]).astype(BF16)

    vec = pl.BlockSpec((1, BLK), lambda c: (0, 0))
    return pl.pallas_call(
        body, name="ssd_fwd", grid=(nb,),
        in_specs=[pl.BlockSpec((BLK, CONV_D), lambda c: (c, 0)),
                  pl.BlockSpec((BLK, SSD_D), lambda c: (c, C_Z // SSD_D)),
                  pl.BlockSpec((BLK, BLK), lambda c: (c, C_SM // BLK)),
                  vec, vec, vec, pl.BlockSpec((1, SSD_D), lambda c: (0, 0))],
        out_specs=[pl.BlockSpec((BLK, SSD_D), lambda c: (c, 0)),
                   pl.BlockSpec((None, SSD_N, SSD_D), lambda c: (c, 0, 0))],
        out_shape=[jax.ShapeDtypeStruct((t, SSD_D), BF16), jax.ShapeDtypeStruct((nb, SSD_N, SSD_D), F32)],
        scratch_shapes=[pltpu.VMEM((SSD_N, SSD_D), F32)],
        compiler_params=_cp("arbitrary"),
    )(xa, proj, proj, dtb, alog, dskip, normg)


def _lane_put(col, lane):
    li = lax.broadcasted_iota(jnp.int32, (col.shape[0], BLK), 1)
    return jnp.where(li == lane, col, 0.0)


def ssd_bwd(dmix, xa, proj, sprev, dtb, alog, dskip, normg):
    t = xa.shape[0]
    nb = t // BLK
    gw = SSD_D // SSD_G
    rev = lambda c: nb - 1 - c

    def body(dy_ref, xa_ref, z_ref, sm_ref, sp_ref, dtb_ref, al_ref, ds_ref, ng_ref,
             dxa_ref, dz_ref, dsm_ref, dng_ref, dds_ref, dal_ref, ddtb_ref, dst):
        c = pl.program_id(0)

        @pl.when(c == 0)
        def _():
            dst[...] = jnp.zeros_like(dst)
            dng_ref[...] = jnp.zeros_like(dng_ref)
            dds_ref[...] = jnp.zeros_like(dds_ref)
            dal_ref[...] = jnp.zeros_like(dal_ref)
            ddtb_ref[...] = jnp.zeros_like(ddtb_ref)

        ok = _valid_rows(BLK, rev(c) * BLK)
        sm = sm_ref[...]
        sp = sp_ref[...]
        d = _ssd_chunk(xa_ref[...], sm, dtb_ref[...], al_ref[...], ds_ref[...], ok, sp)
        dt, amat, ac, act, tri, et, xs, xdt = (d[k] for k in ("dt", "amat", "ac", "act", "tri", "et", "xs", "xdt"))
        rowi = lax.broadcasted_iota(jnp.int32, (BLK, 1), 0)
        y = d["y"]
        z = z_ref[...]
        sgz = _sig(z)
        siluz = z * sgz
        yg = y * siluz
        dout = dy_ref[...]
        ng = ng_ref[...]
        dygs, xhs = [], []
        for g in range(SSD_G):
            v = yg[:, gw * g:gw * (g + 1)]
            rr = lax.rsqrt(jnp.mean(v * v, axis=1, keepdims=True) + EPS)
            xh = v * rr
            dxh = dout[:, gw * g:gw * (g + 1)] * ng[:, gw * g:gw * (g + 1)]
            dygs.append(rr * (dxh - xh * jnp.mean(dxh * xh, axis=1, keepdims=True)))
            xhs.append(xh)
        dyg = jnp.concatenate(dygs, axis=1)
        dng_ref[...] += jnp.sum(dout * jnp.concatenate(xhs, axis=1), axis=0, keepdims=True)
        dy = dyg * siluz
        dz_ref[...] = dyg * y * (sgz * (1.0 + z * (1.0 - sgz)))

        triu = _tri(lower=False)
        dyb = dy.astype(BF16)
        dsn = dst[...]
        dds_ref[...] += _nn_hi(jnp.sum(dy * xs, axis=0, keepdims=True), et)
        dac_all = _nn_hi(dy * jnp.concatenate(d["yo"], axis=1), et)
        dyo = (dy * d["ee"]).astype(BF16)
        gl = jnp.exp(ac[BLK - 1:BLK, :])
        dlast = _nn_hi(jnp.sum(dsn * sp, axis=0, keepdims=True), et) * gl
        bds, db_g, dc_g, dxdt_i, new_dst = [], [], [], [], []
        for g in range(SSD_G):
            cols = slice(GW * g, GW * (g + 1))
            bg, cg = d["bg"][g], d["cg"][g]
            dsng = dsn[:, cols].astype(BF16)
            dc = _nt(dyo[:, cols], sp[:, cols].astype(BF16))
            new_dst.append(_tn(cg, dyo[:, cols]) + d["gle"][:, cols] * dsn[:, cols])
            bds.append(_nn(bg, dsng))
            db = _nt(d["decx"][:, cols].astype(BF16), dsng)
            cbt = _nt(bg, cg)
            dcb = jnp.zeros((BLK, BLK), F32)
            for h in range(HPG * g, HPG * (g + 1)):
                hc = slice(SSD_P * h, SSD_P * (h + 1))
                dm = _nt(dyb[:, hc], d["xdtb"][:, hc])
                dcb += dm * d["seg"][h]
                w = dm * d["m"][h]
                dac_all += _lane_put(jnp.sum(w, axis=1, keepdims=True) - jnp.sum(w.T, axis=1, keepdims=True), h)
                segt = jnp.where(triu, jnp.exp(jnp.minimum(act[h:h + 1, :] - ac[:, h:h + 1], 0.0)), 0.0)
                dxdt_i.append(_nn((cbt * segt).astype(BF16), dyb[:, hc]))
            dcbb = dcb.astype(BF16)
            dc_g.append(dc + _nn(dcbb, bg))
            db_g.append(db + _tn(dcbb, cg))
        dst[...] = jnp.concatenate(new_dst, axis=1)
        bds = jnp.concatenate(bds, axis=1)
        tdec = jnp.exp(ac[BLK - 1:BLK, :] - ac) * _nn_hi(xdt * bds, et)
        dlast += jnp.sum(tdec, axis=0, keepdims=True)
        dac_all += jnp.where(rowi == BLK - 1, dlast, 0.0) - tdec
        dxdt = d["dece"] * bds + jnp.concatenate(dxdt_i, axis=1)
        da = _nn_hi(triu.astype(F32), dac_all)
        ddt = _nn_hi(dxdt * xs, et) + da * amat
        dal_ref[...] += jnp.sum(da * dt, axis=0, keepdims=True) * amat
        ddtr = jnp.where(ok, ddt * _sig(sm + dtb_ref[...]), 0.0)
        ddtb_ref[...] += jnp.sum(ddtr, axis=0, keepdims=True)
        dsm_ref[...] = ddtr
        dxs = d["dse"] * dy + dxdt * d["dte"]
        dxa_ref[...] = jnp.where(ok, jnp.concatenate([dxs] + db_g + dc_g, axis=1), 0.0)

    vec = pl.BlockSpec((1, BLK), lambda c: (0, 0))
    nvec = pl.BlockSpec((1, SSD_D), lambda c: (0, 0))
    return pl.pallas_call(
        body, name="ssd_bwd", grid=(nb,),
        in_specs=[pl.BlockSpec((BLK, SSD_D), lambda c: (rev(c), 0)),
                  pl.BlockSpec((BLK, CONV_D), lambda c: (rev(c), 0)),
                  pl.BlockSpec((BLK, SSD_D), lambda c: (rev(c), C_Z // SSD_D)),
                  pl.BlockSpec((BLK, BLK), lambda c: (rev(c), C_SM // BLK)),
                  pl.BlockSpec((None, SSD_N, SSD_D), lambda c: (rev(c), 0, 0)),
                  vec, vec, vec, nvec],
        out_specs=[pl.BlockSpec((BLK, CONV_D), lambda c: (rev(c), 0)),
                   pl.BlockSpec((BLK, SSD_D), lambda c: (rev(c), 0)),
                   pl.BlockSpec((BLK, BLK), lambda c: (rev(c), 0)),
                   nvec, vec, vec, vec],
        out_shape=[jax.ShapeDtypeStruct((t, CONV_D), F32), jax.ShapeDtypeStruct((t, SSD_D), F32),
                   jax.ShapeDtypeStruct((t, BLK), F32), jax.ShapeDtypeStruct((1, SSD_D), F32),
                   jax.ShapeDtypeStruct((1, BLK), F32), jax.ShapeDtypeStruct((1, BLK), F32),
                   jax.ShapeDtypeStruct((1, BLK), F32)],
        scratch_shapes=[pltpu.VMEM((SSD_N, SSD_D), F32)],
        compiler_params=_cp("arbitrary"),
    )(dmix, xa, proj, proj, sprev, dtb, alog, dskip, normg)


def _attn_scores(q_ref, k_ref, h, dq, scale, mask, bias):
    qh = q_ref[:, dq * h:dq * (h + 1)].astype(BF16)
    kh = k_ref[:, dq * h:dq * (h + 1)].astype(BF16)
    s = _nt(qh, kh) * scale
    if bias is not None:
        s = s + bias
    return qh, kh, jnp.where(mask, s, NEG)


def _segments(nb):
    cuts = sorted({0, nb} | {max(1, round(nb * f)) for f in (0.3, 0.53, 0.77)})
    return list(zip(cuts[:-1], cuts[1:]))


def attn_fwd(q, k, v, qcol, kcol, vcol, nh, dq, dv, scale, c_col=None, c_row=None, lane0=0):
    t = q.shape[0]
    tq = BLK
    use_bias = c_col is not None

    def segment(t0, t1, prev):
        tk = t1 * BLK
        nprev = len(prev)

        def body(*refs):
            refs = refs[nprev:]
            if use_bias:
                q_ref, k_ref, v_ref, cc_ref, cr_ref, o_ref, l_ref = refs
            else:
                q_ref, k_ref, v_ref, o_ref, l_ref = refs
            i = pl.program_id(0)
            rowg = (t0 + i) * tq + lax.broadcasted_iota(jnp.int32, (tq, 1), 0)
            col = lax.broadcasted_iota(jnp.int32, (1, tk), 1)
            mask = (col <= rowg) & (col >= PAD)
            outs = []
            lse = jnp.zeros((tq, BLK), F32)
            for h in range(nh):
                bias = (cc_ref[:, lane0 + h:lane0 + h + 1] - cr_ref[h:h + 1, :]) if use_bias else None
                _, _, s = _attn_scores(q_ref, k_ref, h, dq, scale, mask, bias)
                m = jnp.max(s, axis=1, keepdims=True)
                p = jnp.exp(s - m)
                l = jnp.sum(p, axis=1, keepdims=True)
                vh = v_ref[:, dv * h:dv * (h + 1)].astype(BF16)
                outs.append(_nn(p.astype(BF16), vh) / l)
                lse += _lane_put(m + jnp.log(l), h)
            o_ref[...] = jnp.concatenate(outs, axis=1).astype(BF16)
            l_ref[...] = lse.T[0:8, :]

        in_specs = [_ANY] * nprev + [pl.BlockSpec((tq, nh * dq), lambda i: (t0 + i, qcol)),
                                     pl.BlockSpec((tk, nh * dq), lambda i: (0, kcol)),
                                     pl.BlockSpec((tk, nh * dv), lambda i: (0, vcol))]
        args = list(prev) + [q, k, v]
        if use_bias:
            in_specs += [pl.BlockSpec((tq, BLK), lambda i: (t0 + i, 0)), pl.BlockSpec((8, tk), lambda i: (0, 0))]
            args += [c_col, c_row]
        return pl.pallas_call(
            body, name="attn_fwd", grid=(t1 - t0,),
            in_specs=in_specs,
            out_specs=[pl.BlockSpec((tq, nh * dv), lambda i: (t0 + i, 0)), pl.BlockSpec((8, tq), lambda i: (0, t0 + i))],
            out_shape=[jax.ShapeDtypeStruct((t, nh * dv), BF16), jax.ShapeDtypeStruct((8, t), F32)],
            input_output_aliases={p: p for p in range(nprev)},
            compiler_params=_cp("arbitrary"),
        )(*args)

    outs = []
    for t0, t1 in _segments(t // tq):
        outs = segment(t0, t1, outs)
    return outs


def attn_bwd(q, k, v, do, lse_row, o, qcol, kcol, vcol, docol, ocol, nh, dq, dv, scale, c_col=None, c_row=None):
    t = q.shape[0]
    tq = BLK
    use_bias = c_col is not None

    def segment(t0, t1, prev):
        tk = t1 * BLK
        nprev = len(prev)

        def body(*refs):
            pv, refs = refs[:nprev], refs[nprev:]
            kt = refs[-1]
            if use_bias:
                q_ref, k_ref, v_ref, do_ref, l_ref, o_ref, cc_ref, cr_ref, dq_ref, dk_ref, dv_ref, dcq_ref, dck_ref = refs[:-1]
            else:
                q_ref, k_ref, v_ref, do_ref, l_ref, o_ref, dq_ref, dk_ref, dv_ref = refs[:-1]
            i = pl.program_id(0)

            @pl.when(i == 0)
            def _():
                kt[...] = k_ref[...].astype(BF16).T
                if nprev:
                    dk_ref[...] = pv[1][...]
                    dv_ref[...] = pv[2][...]
                    if use_bias:
                        dck_ref[...] = pv[4][...]
                else:
                    dk_ref[...] = jnp.zeros_like(dk_ref)
                    dv_ref[...] = jnp.zeros_like(dv_ref)
                    if use_bias:
                        dck_ref[...] = jnp.zeros_like(dck_ref)

            key = lax.broadcasted_iota(jnp.int32, (tk, 1), 0)
            qry = (t0 + i) * tq + lax.broadcasted_iota(jnp.int32, (1, tq), 1)
            mask = (key <= qry) & (key >= PAD)
            dot = (do_ref[...].astype(F32) * o_ref[...].astype(F32)).T
            dqts, dcqs = [], []
            for h in range(nh):
                qh = q_ref[:, dq * h:dq * (h + 1)].astype(BF16)
                kh = k_ref[:, dq * h:dq * (h + 1)].astype(BF16)
                vh = v_ref[:, dv * h:dv * (h + 1)].astype(BF16)
                doh = do_ref[:, dv * h:dv * (h + 1)].astype(BF16)
                delta = jnp.sum(dot[dv * h:dv * (h + 1), :], axis=0, keepdims=True)
                st = _nt(kh, qh) * scale
                if use_bias:
                    st = st + (cr_ref[h:h + 1, :] - cc_ref[h])
                pt = jnp.exp(jnp.where(mask, st, NEG) - l_ref[h:h + 1, :])
                dst = pt * (_nt(vh, doh) - delta)
                dsb = dst.astype(BF16)
                dk_ref[:, dq * h:dq * (h + 1)] += _nn(dsb, qh) * scale
                dv_ref[:, dv * h:dv * (h + 1)] += _nn(pt.astype(BF16), doh)
                dqts.append(_nn(kt[dq * h:dq * (h + 1), :], dsb))
                if use_bias:
                    dcqs.append(jnp.sum(dst, axis=0, keepdims=True))
                    dck_ref[h] += dst
            dq_ref[...] = jnp.concatenate(dqts, axis=0).T * scale
            if use_bias:
                dcq_ref[...] = jnp.concatenate(dcqs + [jnp.zeros((8 - nh, tq), F32)], axis=0)

        keys_q = pl.BlockSpec((tk, nh * dq), lambda i: (0, 0))
        keys_v = pl.BlockSpec((tk, nh * dv), lambda i: (0, 0))
        keys_c = pl.BlockSpec((nh, tk, BLK), lambda i: (0, 0, 0))
        qrow = pl.BlockSpec((8, tq), lambda i: (0, t0 + i))
        prev_specs = ([_ANY, keys_q, keys_v] + ([_ANY, keys_c] if use_bias else [])) if nprev else []
        in_specs = prev_specs + [pl.BlockSpec((tq, nh * dq), lambda i: (t0 + i, qcol)),
                                 pl.BlockSpec((tk, nh * dq), lambda i: (0, kcol)),
                                 pl.BlockSpec((tk, nh * dv), lambda i: (0, vcol)),
                                 pl.BlockSpec((tq, nh * dv), lambda i: (t0 + i, docol)),
                                 qrow,
                                 pl.BlockSpec((tq, nh * dv), lambda i: (t0 + i, ocol))]
        args = list(prev) + [q, k, v, do, lse_row, o]
        out_specs = [pl.BlockSpec((tq, nh * dq), lambda i: (t0 + i, 0)), keys_q, keys_v]
        out_shape = [jax.ShapeDtypeStruct((t, nh * dq), F32), jax.ShapeDtypeStruct((t, nh * dq), F32),
                     jax.ShapeDtypeStruct((t, nh * dv), F32)]
        if use_bias:
            in_specs += [keys_c, qrow]
            args += [c_col, c_row]
            out_specs += [qrow, keys_c]
            out_shape += [jax.ShapeDtypeStruct((8, t), F32), jax.ShapeDtypeStruct((nh, t, BLK), F32)]
        return pl.pallas_call(
            body, name="attn_bwd", grid=(t1 - t0,),
            in_specs=in_specs, out_specs=out_specs, out_shape=out_shape,
            scratch_shapes=[pltpu.VMEM((nh * dq, tk), BF16)],
            input_output_aliases={p: p for p in range(nprev)},
            compiler_params=_cp("arbitrary"),
        )(*args)

    outs = []
    for t0, t1 in reversed(_segments(t // tq)):
        outs = segment(t0, t1, outs)
    return outs


def fox_pre(proj, fb):
    t = proj.shape[0]
    nb = t // BLK

    def body(sm_ref, fb_ref, c_ref, cr_ref, cb_ref):
        x = sm_ref[...] + fb_ref[...]
        lane = lax.broadcasted_iota(jnp.int32, (1, BLK), 1)
        keep = _valid_rows(t, 0) & (lane >= SM_F) & (lane < SM_F + FOX_H)
        logf = jnp.where(keep, jnp.minimum(x, 0.0) - jnp.log(1.0 + jnp.exp(-jnp.abs(x))), 0.0)
        tri = _tri().astype(F32)
        carry = jnp.zeros((1, BLK), F32)
        for b in range(nb):
            cb = _nn_hi(tri, logf[b * BLK:(b + 1) * BLK, :]) + carry
            c_ref[b * BLK:(b + 1) * BLK, :] = cb
            carry = cb[BLK - 1:BLK, :]
        cr_ref[...] = c_ref[...].T[SM_F:SM_F + 8, :]
        for h in range(FOX_H):
            cb_ref[h] = jnp.broadcast_to(c_ref[:, SM_F + h:SM_F + h + 1], (t, BLK))

    return pl.pallas_call(
        body, name="fox_pre", grid=(1,),
        in_specs=[pl.BlockSpec((t, BLK), lambda i: (0, C_SM // BLK)), pl.BlockSpec((1, BLK), lambda i: (0, 0))],
        out_specs=[pl.BlockSpec((t, BLK), lambda i: (0, 0)), pl.BlockSpec((8, t), lambda i: (0, 0)),
                   pl.BlockSpec((FOX_H, t, BLK), lambda i: (0, 0, 0))],
        out_shape=[jax.ShapeDtypeStruct((t, BLK), F32), jax.ShapeDtypeStruct((8, t), F32),
                   jax.ShapeDtypeStruct((FOX_H, t, BLK), F32)],
        compiler_params=_cp("arbitrary"),
    )(proj, fb)


def fox_pre_bwd(dcq, dck, proj, fb, dsm_in):
    t = proj.shape[0]
    nb = t // BLK

    def body(dcq_ref, dck_ref, sm_ref, fb_ref, din_ref, dsm_ref, dfb_ref, scr):
        triu = _tri(lower=False).astype(F32)
        carry = jnp.zeros((1, BLK), F32)
        scr[...] = jnp.concatenate([jnp.zeros((SM_F, t), F32), dcq_ref[...], jnp.zeros((BLK - SM_F - 8, t), F32)], axis=0).T
        lane = lax.broadcasted_iota(jnp.int32, (1, BLK), 1)
        for b in range(nb - 1, -1, -1):
            blk = scr[b * BLK:(b + 1) * BLK, :]
            for h in range(FOX_H):
                blk -= jnp.where(lane == SM_F + h, jnp.sum(dck_ref[h, b * BLK:(b + 1) * BLK, :], axis=1, keepdims=True), 0.0)
            cb = _nn_hi(triu, blk) + carry
            scr[b * BLK:(b + 1) * BLK, :] = cb
            carry = cb[0:1, :]
        x = sm_ref[...] + fb_ref[...]
        lane = lax.broadcasted_iota(jnp.int32, (1, BLK), 1)
        keep = _valid_rows(t, 0) & (lane >= SM_F) & (lane < SM_F + FOX_H)
        df = jnp.where(keep, scr[...] * _sig(-x), 0.0)
        dfb_ref[...] = jnp.sum(df, axis=0, keepdims=True)
        dsm_ref[...] = din_ref[...] + df

    full = pl.BlockSpec((t, BLK), lambda i: (0, 0))
    return pl.pallas_call(
        body, name="fox_pre_bwd", grid=(1,),
        in_specs=[pl.BlockSpec((8, t), lambda i: (0, 0)), pl.BlockSpec((FOX_H, t, BLK), lambda i: (0, 0, 0)),
                  pl.BlockSpec((t, BLK), lambda i: (0, C_SM // BLK)), pl.BlockSpec((1, BLK), lambda i: (0, 0)), full],
        out_specs=[full, pl.BlockSpec((1, BLK), lambda i: (0, 0))],
        out_shape=[jax.ShapeDtypeStruct((t, BLK), F32), jax.ShapeDtypeStruct((1, BLK), F32)],
        scratch_shapes=[pltpu.VMEM((t, BLK), F32)],
        compiler_params=_cp("arbitrary"),
    )(dcq, dck, proj, fb, dsm_in)


def _swap_rope(x):
    lane = lax.broadcasted_iota(jnp.int32, (1, BLK), 1)
    return jnp.where((lane >= SM_KR) & (lane < SM_KR + 16), pltpu.roll(x, BLK - 16, 1),
                     jnp.where((lane >= SM_KR + 16) & (lane < SM_KR + 32), pltpu.roll(x, 16, 1), 0.0))


def _rms(x, g):
    r = lax.rsqrt(jnp.mean(x * x, axis=1, keepdims=True) + EPS)
    return r, x * r


def mla_pre(proj, qg, kvg, wq, wk, wv, cosq, sinq):
    t = proj.shape[0]
    tm = _row_tile(t)

    def body(cq_ref, ckv_ref, sm_ref, qg_ref, kvg_ref, wq_ref, wk_ref, wv_ref, cos_ref, sin_ref,
             q_ref, k_ref, v_ref, cqn_ref, ckvn_ref):
        cs, sn = cos_ref[...], sin_ref[...]
        _, xh = _rms(cq_ref[...], None)
        cqn = (xh * qg_ref[...]).astype(BF16)
        cqn_ref[...] = cqn
        qraw = _nn(cqn, wq_ref[...])
        qs = []
        for h in range(MLA_H):
            hb = qraw[:, BLK * h:BLK * (h + 1)]
            qs.append(hb * cs + _swap_rope(hb) * sn)
        q_ref[...] = jnp.concatenate(qs, axis=1).astype(BF16)
        _, kh = _rms(ckv_ref[...], None)
        ckvn = (kh * kvg_ref[...]).astype(BF16)
        ckvn_ref[...] = ckvn
        kraw = _nn(ckvn, wk_ref[...])
        v_ref[...] = _nn(ckvn, wv_ref[...]).astype(BF16)
        lane = lax.broadcasted_iota(jnp.int32, (1, BLK), 1)
        kr = sm_ref[...]
        krr = jnp.where((lane >= SM_KR) & (lane < SM_KR + MLA_ROPE), kr * cs + _swap_rope(kr) * sn, 0.0)
        k_ref[...] = jnp.concatenate([kraw[:, BLK * h:BLK * (h + 1)] + krr for h in range(MLA_H)], axis=1).astype(BF16)

    def rows(w, cb):
        return pl.BlockSpec((tm, w), lambda i: (i, cb))

    def whole(a):
        return pl.BlockSpec(a.shape, lambda i: (0, 0))

    return pl.pallas_call(
        body, name="mla_pre", grid=(t // tm,),
        in_specs=[rows(MLA_QL, C_CQ // MLA_QL), rows(MLA_KVL, C_CKV // MLA_KVL), rows(BLK, C_SM // BLK),
                  whole(qg), whole(kvg), whole(wq), whole(wk), whole(wv), rows(BLK, 0), rows(BLK, 0)],
        out_specs=[rows(512, 0), rows(512, 0), rows(256, 0), rows(MLA_QL, 0), rows(MLA_KVL, 0)],
        out_shape=[jax.ShapeDtypeStruct((t, 512), BF16), jax.ShapeDtypeStruct((t, 512), BF16),
                   jax.ShapeDtypeStruct((t, 256), BF16), jax.ShapeDtypeStruct((t, MLA_QL), BF16),
                   jax.ShapeDtypeStruct((t, MLA_KVL), BF16)],
        compiler_params=_cp("arbitrary"),
    )(proj, proj, proj, qg, kvg, wq, wk, wv, cosq, sinq)


def mla_pre_bwd(dq, dk, dv, proj, cqn, ckvn, qg, kvg, wq, wk, wv, cosq, sinq, dsm_in):
    t = proj.shape[0]
    tm = _row_tile(t)

    def body(dq_ref, dk_ref, dv_ref, cq_ref, ckv_ref, cqn_ref, ckvn_ref, qg_ref, kvg_ref, wq_ref, wk_ref, wv_ref,
             cos_ref, sin_ref, din_ref, dcq_ref, dckv_ref, dsm_ref, dwq_ref, dwk_ref, dwv_ref, dqg_ref, dkvg_ref):
        i = pl.program_id(0)

        @pl.when(i == 0)
        def _():
            for r in (dwq_ref, dwk_ref, dwv_ref, dqg_ref, dkvg_ref):
                r[...] = jnp.zeros_like(r)

        cs, sn = cos_ref[...], sin_ref[...]
        lane = lax.broadcasted_iota(jnp.int32, (1, BLK), 1)

        def unrope(dy):
            return dy * cs + _swap_rope(dy * sn)

        dqp = jnp.concatenate([unrope(dq_ref[:, BLK * h:BLK * (h + 1)]) for h in range(MLA_H)], axis=1).astype(BF16)
        dwq_ref[...] += _tn(cqn_ref[...], dqp)
        dcqn = _nt(dqp, wq_ref[...])
        r, xh = _rms(cq_ref[...], None)
        dqg_ref[...] += jnp.sum(dcqn * xh, axis=0, keepdims=True)
        dxh = dcqn * qg_ref[...]
        dcq_ref[...] = r * (dxh - xh * jnp.mean(dxh * xh, axis=1, keepdims=True))

        dkn, dkr = [], jnp.zeros((tm, BLK), F32)
        for h in range(MLA_H):
            blk = dk_ref[:, BLK * h:BLK * (h + 1)]
            dkn.append(jnp.where(lane < MLA_NOPE, blk, 0.0))
            dkr += jnp.where((lane >= SM_KR) & (lane < SM_KR + MLA_ROPE), blk, 0.0)
        dknb = jnp.concatenate(dkn, axis=1).astype(BF16)
        dvb = dv_ref[...].astype(BF16)
        ckvn = ckvn_ref[...]
        dwk_ref[...] += _tn(ckvn, dknb)
        dwv_ref[...] += _tn(ckvn, dvb)
        dckvn = _nt(dknb, wk_ref[...]) + _nt(dvb, wv_ref[...])
        r2, kh = _rms(ckv_ref[...], None)
        dkvg_ref[...] += jnp.sum(dckvn * kh, axis=0, keepdims=True)
        dkh = dckvn * kvg_ref[...]
        dckv_ref[...] = r2 * (dkh - kh * jnp.mean(dkh * kh, axis=1, keepdims=True))
        dsm_ref[...] = din_ref[...] + jnp.where((lane >= SM_KR) & (lane < SM_KR + MLA_ROPE), unrope(dkr), 0.0)

    def rows(w, cb):
        return pl.BlockSpec((tm, w), lambda i: (i, cb))

    def whole(a):
        return pl.BlockSpec(a.shape, lambda i: (0, 0))

    def wshape(a):
        return jax.ShapeDtypeStruct(a.shape, F32)

    return pl.pallas_call(
        body, name="mla_pre_bwd", grid=(t // tm,),
        in_specs=[rows(512, 0), rows(512, 0), rows(256, 0), rows(MLA_QL, C_CQ // MLA_QL), rows(MLA_KVL, C_CKV // MLA_KVL),
                  rows(MLA_QL, 0), rows(MLA_KVL, 0), whole(qg), whole(kvg), whole(wq), whole(wk), whole(wv),
                  rows(BLK, 0), rows(BLK, 0), rows(BLK, 0)],
        out_specs=[rows(MLA_QL, 0), rows(MLA_KVL, 0), rows(BLK, 0), whole(wq), whole(wk), whole(wv), whole(qg), whole(kvg)],
        out_shape=[jax.ShapeDtypeStruct((t, MLA_QL), F32), jax.ShapeDtypeStruct((t, MLA_KVL), F32),
                   jax.ShapeDtypeStruct((t, BLK), F32), wshape(wq), wshape(wk), wshape(wv), wshape(qg), wshape(kvg)],
        compiler_params=_cp("arbitrary"),
    )(dq, dk, dv, proj, proj, cqn, ckvn, qg, kvg, wq, wk, wv, cosq, sinq, dsm_in)


def _slot_sum(me, own, recv_ref):
    gg = own.astype(F32)
    for s in range(N_DEV):
        gg = gg + jnp.where(me == s, 0.0, recv_ref[s].astype(F32))
    return gg


def adamw(w, m, v, g=None, recv=None, own=None, me_arr=None):
    shape = w.shape
    c = shape[-1]
    from_recv = recv is not None
    if not from_recv:
        me_arr = jnp.zeros((1,), jnp.int32)
    nl = len(recv) if from_recv else 1
    rws = w.size // c // nl
    tr = rws
    for d in (1024, 512, 352, 256, 128, 64, 32, 16, 8):
        if rws % d == 0 and d * c * 4 <= (2 << 20):
            tr = d
            break
    nt = rws // tr
    w2, m2, v2 = (a.reshape(nl, rws, c) for a in (w, m, v))
    if from_recv:
        gin = [a.reshape(N_DEV, rws, c) for a in list(recv) + list(own)]
    else:
        gin = [g.reshape(1, rws, c)]

    def body(me_ref, w_ref, m_ref, v_ref, *rest):
        g_refs, outs = rest[:len(gin)], rest[len(gin):]
        if from_recv:
            g_out, outs = outs[0], outs[1:]
            for li in range(nl):
                @pl.when(pl.program_id(0) == li)
                def _(li=li):
                    g_out[...] = _slot_sum(me_ref[0], g_refs[nl + li][...], g_refs[li])
            gg = g_out[...]
        else:
            gg = g_refs[0][...]
        d_ref, nm_ref, nv_ref = outs
        nm = B1 * m_ref[...] + (1.0 - B1) * gg
        nv = B2 * v_ref[...] + (1.0 - B2) * (gg * gg)
        mh = nm / (1.0 - B1 ** STEP)
        vh = nv / (1.0 - B2 ** STEP)
        d_ref[...] = -LR * (mh / (jnp.sqrt(vh) + AEPS) + WD * w_ref[...])
        nm_ref[...] = nm
        nv_ref[...] = nv

    row = pl.BlockSpec((None, tr, c), lambda l, i, me: (l, i, 0))
    if from_recv:
        gspecs = [pl.BlockSpec((N_DEV, tr, c), lambda l, i, me, li=li: (0, jnp.where(l == li, i, 0), 0))
                  for li in range(nl)]
        gspecs += [pl.BlockSpec((None, tr, c), lambda l, i, me, li=li: (me[0], jnp.where(l == li, i, 0), 0))
                   for li in range(nl)]
    else:
        gspecs = [row]
    nout = 4 if from_recv else 3
    outs = pl.pallas_call(
        body, name="adamw",
        grid_spec=pltpu.PrefetchScalarGridSpec(num_scalar_prefetch=1, grid=(nl, nt), in_specs=[row, row, row] + gspecs,
                                               out_specs=[row] * nout),
        out_shape=[jax.ShapeDtypeStruct((nl, rws, c), F32)] * nout,
        compiler_params=_cp("arbitrary", "arbitrary"),
    )(me_arr, w2, m2, v2, *gin)
    return tuple(o.reshape(shape) for o in outs)


def sum_slots(recv, own=None, me_arr=None):
    _, r, c = recv.shape
    if own is None:
        own, me_arr = recv, jnp.zeros((1,), jnp.int32)
        plain = True
    else:
        plain = False

    def body(me_ref, r_ref, own_ref, o_ref):
        if plain:
            gg = r_ref[0].astype(F32)
            for s in range(1, N_DEV):
                gg = gg + r_ref[s].astype(F32)
            o_ref[...] = gg
        else:
            o_ref[...] = _slot_sum(me_ref[0], own_ref[...], r_ref)

    return pl.pallas_call(
        body, name="sum_slots",
        grid_spec=pltpu.PrefetchScalarGridSpec(
            num_scalar_prefetch=1, grid=(1,),
            in_specs=[pl.BlockSpec((N_DEV, r, c), lambda i, me: (0, 0, 0)),
                      pl.BlockSpec((None, r, c), lambda i, me: (me[0], 0, 0))],
            out_specs=pl.BlockSpec((r, c), lambda i, me: (0, 0))),
        out_shape=jax.ShapeDtypeStruct((r, c), F32),
        compiler_params=_cp("arbitrary"),
    )(me_arr, recv, own)


_FLIPS = [(0, 0, 1), (0, 1, 0), (0, 1, 1), (1, 0, 0), (1, 0, 1), (1, 1, 0), (1, 1, 1)]
_ANY = pl.BlockSpec(memory_space=pl.ANY)


def _mesh_place():
    x, y, c = lax.axis_index("x"), lax.axis_index("y"), lax.axis_index("c")
    me = 4 * x + 2 * y + c
    peers = [((x + fx) % 2, (y + fy) % 2, (c + fc) % 2) for fx, fy, fc in _FLIPS]
    return me, peers


def place_own(src, l, dtype, me_arr):
    _, r, c = src.shape
    tr = r
    for d in (512, 352, 256, 128, 64, 32, 16, 8):
        if r % d == 0 and d * c * 4 <= (2 << 20):
            tr = d
            break

    def body(me_ref, s_ref, o_ref):
        o_ref[...] = s_ref[...].astype(dtype)

    return pl.pallas_call(
        body, name="place_own",
        grid_spec=pltpu.PrefetchScalarGridSpec(
            num_scalar_prefetch=1, grid=(r // tr,),
            in_specs=[pl.BlockSpec((None, tr, c), lambda i, me: (l, i, 0))],
            out_specs=pl.BlockSpec((None, tr, c), lambda i, me: (me[0], i, 0))),
        out_shape=jax.ShapeDtypeStruct((N_DEV, r, c), dtype),
        compiler_params=_cp("arbitrary"),
    )(me_arr, src)


_HBM = pl.BlockSpec(memory_space=pltpu.HBM)
_SEMS = pl.BlockSpec(memory_space=pltpu.SEMAPHORE)
_EFFECT = pltpu.SideEffectType.DATAFLOW_SIDE_EFFECTING


def exchange_start(mode, arrays, name, after=None):
    n = len(arrays)
    gather = mode == "gather"
    ns = 0 if gather else n
    zones = list(arrays) if gather else [lax.empty(a.shape, a.dtype) for a in arrays]
    ops = ([] if gather else list(arrays)) + zones
    extra = [] if after is None else [after]

    def body(*refs):
        srcs, lands = refs[:ns], refs[ns:ns + n]
        send_sems, recv_sems = refs[ns + n + len(extra)], refs[ns + n + len(extra) + 1]
        token = refs[-1]
        me, peers = _mesh_place()
        ids = [4 * p[0] + 2 * p[1] + p[2] for p in peers]
        for j in range(n):
            for k in range(N_DEV - 1):
                src = lands[j].at[me] if gather else srcs[j].at[ids[k]]
                pltpu.make_async_remote_copy(src_ref=src, dst_ref=lands[j].at[me],
                                             send_sem=send_sems.at[j * (N_DEV - 1) + k],
                                             recv_sem=recv_sems.at[j * (N_DEV - 1) + k], device_id=peers[k],
                                             device_id_type=pl.DeviceIdType.MESH).start()
        token[...] = jnp.zeros_like(token)

    nsem = n * (N_DEV - 1)
    res = pl.pallas_call(
        body, name=name,
        in_specs=[_HBM] * (ns + n) + [_ANY] * len(extra),
        out_specs=(_SEMS, _SEMS, *[_HBM] * (ns + n), pl.BlockSpec(memory_space=pltpu.VMEM)),
        out_shape=(pltpu.SemaphoreType.DMA((nsem,)), pltpu.SemaphoreType.DMA((nsem,)),
                   *[pltpu.HBM(a.shape, a.dtype) for a in ops], jax.ShapeDtypeStruct((8, BLK), F32)),
        input_output_aliases={i: 2 + i for i in range(ns + n)},
        compiler_params=pltpu.CompilerParams(has_side_effects=_EFFECT),
    )(*[pltpu.with_memory_space_constraint(a, pltpu.HBM) for a in ops], *extra)
    return dict(gather=gather, send=res[0], recv=res[1], srcs=list(res[2:2 + ns]), lands=list(res[2 + ns:2 + ns + n]),
                token=res[-1])


def exchange_wait(hd, idxs, name, after):
    gather = hd["gather"]
    n = len(idxs)
    ns = 0 if gather else n
    ops = ([] if gather else [hd["srcs"][j] for j in idxs]) + [hd["lands"][j] for j in idxs]

    def body(*refs):
        srcs, lands = refs[:ns], refs[ns:ns + n]
        send_sems, recv_sems = refs[ns + n], refs[ns + n + 1]
        me, peers = _mesh_place()
        ids = [4 * p[0] + 2 * p[1] + p[2] for p in peers]
        for p, j in enumerate(idxs):
            for k in range(N_DEV - 1):
                src = lands[p].at[me] if gather else srcs[p].at[ids[k]]
                cp = pltpu.make_async_remote_copy(src_ref=src, dst_ref=lands[p].at[ids[k]],
                                                  send_sem=send_sems.at[j * (N_DEV - 1) + k],
                                                  recv_sem=recv_sems.at[j * (N_DEV - 1) + k], device_id=peers[k],
                                                  device_id_type=pl.DeviceIdType.MESH)
                cp.wait_send()
                cp.wait_recv()

    res = pl.pallas_call(
        body, name=name,
        in_specs=[_HBM] * (ns + n) + [_SEMS, _SEMS, _ANY],
        out_specs=[_HBM] * (ns + n),
        out_shape=[pltpu.HBM(a.shape, a.dtype) for a in ops],
        input_output_aliases={i: i for i in range(ns + n)},
        compiler_params=pltpu.CompilerParams(has_side_effects=_EFFECT),
    )(*ops, hd["send"], hd["recv"], after)
    return list(res[:ns]), list(res[ns:])


def _chip_place():
    x, y, c = lax.axis_index("x"), lax.axis_index("y"), lax.axis_index("c")
    chips = [((x + 1) % 2, y), (x, (y + 1) % 2), ((x + 1) % 2, (y + 1) % 2)]
    ident = lambda p: 4 * p[0] + 2 * p[1] + p[2]
    return dict(me=4 * x + 2 * y + c, sib=(x, y, 1 - c), sib_id=4 * x + 2 * y + 1 - c,
                same=[(cx, cy, c) for cx, cy in chips], same_ids=[ident((cx, cy, c)) for cx, cy in chips],
                other_ids=[ident((cx, cy, 1 - c)) for cx, cy in chips])


def _remote(src, dst, send_sem, recv_sem, dev):
    return pltpu.make_async_remote_copy(src_ref=src, dst_ref=dst, send_sem=send_sem, recv_sem=recv_sem, device_id=dev,
                                        device_id_type=pl.DeviceIdType.MESH)


def gather_start(zones, name):
    n = len(zones)

    def body(*refs):
        lands, send_sems, recv_sems, token = refs[:n], refs[n], refs[n + 1], refs[-1]
        pc = _chip_place()
        for j in range(n):
            own = lands[j].at[pc["me"]]
            for k, dev in enumerate([pc["sib"]] + pc["same"]):
                _remote(own, own, send_sems.at[4 * j + k], recv_sems.at[4 * j + k], dev).start()
        token[...] = jnp.zeros_like(token)

    res = pl.pallas_call(
        body, name=name,
        in_specs=[_HBM] * n,
        out_specs=(_SEMS, _SEMS, *[_HBM] * n, pl.BlockSpec(memory_space=pltpu.VMEM)),
        out_shape=(pltpu.SemaphoreType.DMA((4 * n,)), pltpu.SemaphoreType.DMA((4 * n,)),
                   *[pltpu.HBM(a.shape, a.dtype) for a in zones], jax.ShapeDtypeStruct((8, BLK), F32)),
        input_output_aliases={i: 2 + i for i in range(n)},
        compiler_params=pltpu.CompilerParams(has_side_effects=_EFFECT),
    )(*[pltpu.with_memory_space_constraint(a, pltpu.HBM) for a in zones])
    return dict(send=res[0], recv=res[1], lands=list(res[2:2 + n]), token=res[-1])


def gather_relay(hd, idxs, name, after):
    n = len(idxs)

    def body(*refs):
        lands, send_sems, recv_sems = refs[:n], refs[n], refs[n + 1]
        fsend, frecv, token = refs[n + 3 + n], refs[n + 4 + n], refs[-1]
        pc = _chip_place()
        for p, j in enumerate(idxs):
            for k in range(3):
                _remote(lands[p].at[pc["me"]], lands[p].at[pc["same_ids"][k]], send_sems.at[4 * j + 1 + k],
                        recv_sems.at[4 * j + 1 + k], pc["same"][k]).wait_recv()
        for p in range(n):
            for k in range(3):
                got = lands[p].at[pc["same_ids"][k]]
                _remote(got, got, fsend.at[3 * p + k], frecv.at[3 * p + k], pc["sib"]).start()
        token[...] = jnp.zeros_like(token)

    ops = [hd["lands"][j] for j in idxs]
    res = pl.pallas_call(
        body, name=name,
        in_specs=[_HBM] * n + [_SEMS, _SEMS, _ANY],
        out_specs=(*[_HBM] * n, _SEMS, _SEMS, pl.BlockSpec(memory_space=pltpu.VMEM)),
        out_shape=(*[pltpu.HBM(a.shape, a.dtype) for a in ops], pltpu.SemaphoreType.DMA((3 * n,)),
                   pltpu.SemaphoreType.DMA((3 * n,)), jax.ShapeDtypeStruct((8, BLK), F32)),
        input_output_aliases={i: i for i in range(n)},
        compiler_params=pltpu.CompilerParams(has_side_effects=_EFFECT),
    )(*ops, hd["send"], hd["recv"], after)
    return dict(lands=list(res[:n]), fsend=res[n], frecv=res[n + 1], token=res[-1])


def gather_wait(hd, rl, idxs, name, after):
    n = len(idxs)

    def body(*refs):
        lands, send_sems, recv_sems, fsend, frecv = refs[:n], refs[n], refs[n + 1], refs[n + 2], refs[n + 3]
        pc = _chip_place()
        for p, j in enumerate(idxs):
            own = lands[p].at[pc["me"]]
            for k, dev in enumerate([pc["sib"]] + pc["same"]):
                _remote(own, own, send_sems.at[4 * j + k], recv_sems.at[4 * j + k], dev).wait_send()
            _remote(own, lands[p].at[pc["sib_id"]], send_sems.at[4 * j], recv_sems.at[4 * j], pc["sib"]).wait_recv()
            for k in range(3):
                cp = _remote(lands[p].at[pc["same_ids"][k]], lands[p].at[pc["other_ids"][k]], fsend.at[3 * p + k],
                             frecv.at[3 * p + k], pc["sib"])
                cp.wait_send()
                cp.wait_recv()

    res = pl.pallas_call(
        body, name=name,
        in_specs=[_HBM] * n + [_SEMS, _SEMS, _SEMS, _SEMS, _ANY],
        out_specs=[_HBM] * n,
        out_shape=[pltpu.HBM(a.shape, a.dtype) for a in rl["lands"]],
        input_output_aliases={i: i for i in range(n)},
        compiler_params=pltpu.CompilerParams(has_side_effects=_EFFECT),
    )(*rl["lands"], hd["send"], hd["recv"], rl["fsend"], rl["frecv"], after)
    return list(res)


def _pad_cols(a, n):
    return jnp.pad(a, ((0, 0),) * (a.ndim - 1) + ((0, n - a.shape[-1]),))


def w_in_to_padded(w):
    z = lambda n: jnp.zeros(w.shape[:-1] + (n,), w.dtype)
    return jnp.concatenate([
        w[..., 0:1280], w[..., 1288:2056], w[..., 2060:2316], w[..., 2316:2444],
        w[..., 1280:1288], w[..., 2056:2060], z(SM_KR - SM_F - FOX_H), w[..., 2444:2476], z(BLK - SM_KR - MLA_ROPE)], axis=-1)


def w_in_from_padded(g):
    s = C_SM
    return jnp.concatenate([
        g[..., 0:1280], g[..., s + SM_DT:s + SM_DT + 8], g[..., 1280:2048], g[..., s + SM_F:s + SM_F + 4],
        g[..., 2048:2304], g[..., 2304:2432], g[..., s + SM_KR:s + SM_KR + MLA_ROPE]], axis=-1)


def _unshard_cols(gth):
    n, r, c = gth.shape
    return jnp.transpose(gth, (1, 0, 2)).reshape(r, n * c)


def _shard_cols(full):
    r, nc = full.shape
    return jnp.transpose(full.reshape(r, N_DEV, nc // N_DEV), (1, 0, 2))


def mla_weights(uq_g, ukv_g):
    uq = _unshard_cols(uq_g)
    dqh = MLA_NOPE + MLA_ROPE
    wq = jnp.concatenate([_pad_cols(uq[:, dqh * h:dqh * (h + 1)], BLK) for h in range(MLA_H)], axis=1)
    wk = jnp.concatenate([_pad_cols(ukv_g[2 * h], BLK) for h in range(MLA_H)], axis=1)
    wv = jnp.concatenate([ukv_g[2 * h + 1] for h in range(MLA_H)], axis=1)
    return wq, wk, wv


def mla_weight_grads(dwq, dwk, dwv):
    dqh = MLA_NOPE + MLA_ROPE
    duq = _shard_cols(jnp.concatenate([dwq[:, BLK * h:BLK * h + dqh] for h in range(MLA_H)], axis=1))
    parts = []
    for h in range(MLA_H):
        parts += [dwk[:, BLK * h:BLK * h + MLA_NOPE], dwv[:, MLA_V * h:MLA_V * (h + 1)]]
    return duq, jnp.stack(parts, axis=0)


def rope_tables(t):
    pos = (jnp.arange(t, dtype=jnp.int32) - PAD).astype(F32)
    inv_freq = 1.0 / (10000.0 ** (jnp.arange(0, MLA_ROPE, 2, dtype=F32) / MLA_ROPE))
    ang = pos[:, None] * inv_freq[None, :]
    cos, sin = jnp.cos(ang), jnp.sin(ang)
    one, zero = jnp.ones((t, SM_KR), F32), jnp.zeros((t, SM_KR), F32)
    tail = BLK - SM_KR - MLA_ROPE
    cosq = jnp.concatenate([one, cos, cos, jnp.ones((t, tail), F32)], axis=1)
    sinq = jnp.concatenate([zero, -sin, sin, jnp.zeros((t, tail), F32)], axis=1)
    return cosq, sinq


def _lanes(v, off=0):
    return jnp.pad(v.astype(F32), (off, BLK - off - v.shape[0]))[None, :]


def layer_fwd(x, ln, hb, getw, tabs, ahead):
    sv = {"h0b": hb}
    W = dict(getw("ffn1", hb))
    ln1 = (W["ln1_g"], W["ln1_b"])
    u, v, r1, h1b = ffn_fwd_seq(x, ln, W["g1"], W["u1"], W["d1"], ln1)
    sv.update(u1=u, v1=v, r1=r1, h1b=h1b)
    W.update(getw("mix", h1b))
    ln2 = (W["ln2_g"], W["ln2_b"])
    proj = mm_nn(h1b, W["w_in"])
    xa = conv_fwd(proj, W["conv_w"], W["conv_b"])
    y_ssd, sprev = ssd_fwd(xa, proj, W["dtb"], W["alog"], W["dskip"], W["normg"])
    c_col, c_row, c_keys = fox_pre(proj, W["fb"])
    y_fox, lse_f = attn_fwd(proj, proj, proj, C_FQ // 256, C_FK // 256, C_FV // 256, FOX_H, FOX_DH, FOX_DH,
                            FOX_DH ** -0.5, c_col, c_row, SM_F)
    ahead(0, "ffn2", y_fox)
    q, k, vv, cqn, ckvn = mla_pre(proj, W["qg"], W["kvg"], W["wq"], W["wk"], W["wv"], *tabs)
    y_mla, lse_m = attn_fwd(q, k, vv, 0, 0, 0, MLA_H, BLK, MLA_V, (MLA_NOPE + MLA_ROPE) ** -0.5)
    mixcat = jnp.concatenate([y_ssd, y_fox, y_mla], axis=1)
    r2, h2b = mm_res_ln(mixcat, W["w_out"], r1, ln1, ln2)
    sv.update(proj=proj, xa=xa, sprev=sprev, c_keys=c_keys, c_row=c_row, lse_f=lse_f, q=q, k=k, v=vv, cqn=cqn, ckvn=ckvn,
              lse_m=lse_m, mixcat=mixcat, r2=r2, h2b=h2b)
    W.update(getw("ffn2", h2b))
    ahead(1, "ffn1", h2b)
    ln3 = (W["ln3_g"], W["ln3_b"])
    u, v, r3, h3b = ffn_fwd_seq(r2, ln2, W["g2"], W["u2"], W["d2"], ln3)
    sv.update(u2=u, v2=v, r3=r3, W=W)
    return r3, ln3, h3b, sv


def ffn_bwd(parts, r, gamma, hb_in, u, v, wg, wu, wd, after=None):
    dh, dwg, dwu, dwd, dg, db = ffn_bwd_seq(parts, r, gamma, hb_in, u, v, wg, wu, wd, after)
    return dh, dict(d=dwd, g=dwg, u=dwu, ln_g=dg, ln_b=db)


def layer_bwd(parts, sv, emit, tabs, after):
    G = {}
    W = sv["W"]
    dh2, g2 = ffn_bwd(parts, sv["r3"], W["ln3_g"], sv["h2b"], sv["u2"], sv["v2"], W["g2"], W["u2"], W["d2"], after)
    G.update(g2=g2["g"], u2=g2["u"], d2=g2["d"], ln3_g=g2["ln_g"], ln3_b=g2["ln_b"])
    tok = emit("ffn2", G)
    dr2, dmixb, G["ln2_g"], G["ln2_b"] = ln_bwd([(dh2, 1.0)], sv["r2"], W["ln2_g"], 1.0, tok)
    dmc = mm_nt_reduce([(dmixb[None], W["w_out"][None])], D)
    G["w_out"] = mm_tn(sv["mixcat"][None], dmixb[None])[0]
    proj = sv["proj"]
    dxa, dz, dsm, G["normg"], G["dskip"], G["alog"], G["dtb"] = ssd_bwd(
        dmc, sv["xa"], proj, sv["sprev"], W["dtb"], W["alog"], W["dskip"], W["normg"])
    dxbc, G["conv_w"], G["conv_b"] = conv_bwd(dxa, proj, W["conv_w"], W["conv_b"])
    dfq, dfk, dfv, dcq, dck = attn_bwd(proj, proj, proj, dmc, sv["lse_f"], sv["mixcat"], C_FQ // 256, C_FK // 256,
                                       C_FV // 256, 2, 2, FOX_H, FOX_DH, FOX_DH, FOX_DH ** -0.5, sv["c_keys"], sv["c_row"])
    dsm, G["fb"] = fox_pre_bwd(dcq, dck, proj, W["fb"], dsm)
    dq, dk, dv = attn_bwd(sv["q"], sv["k"], sv["v"], dmc, sv["lse_m"], sv["mixcat"], 0, 0, 0, 3, 3, MLA_H, BLK, MLA_V,
                          (MLA_NOPE + MLA_ROPE) ** -0.5)
    dcql, dckv, dsm, G["wq"], G["wk"], G["wv"], G["qg"], G["kvg"] = mla_pre_bwd(
        dq, dk, dv, proj, sv["cqn"], sv["ckvn"], W["qg"], W["kvg"], W["wq"], W["wk"], W["wv"], *tabs, dsm)
    dproj = jnp.concatenate([dz, dxbc, dfq, dfk, dfv, dcql, dckv, dsm], axis=1).astype(BF16)
    dh1p = mm_nt_reduce([(dproj[None], W["w_in"][None])], D)
    G["w_in"] = mm_tn(sv["h1b"][None], dproj[None])[0]
    tok = emit("mix", G)
    dh0, g1 = ffn_bwd([(dr2, ALPHA), (dh1p, 1.0)], sv["r1"], W["ln1_g"], sv["h0b"], sv["u1"], sv["v1"],
                      W["g1"], W["u1"], W["d1"], tok)
    G.update(g1=g1["g"], u1=g1["u"], d1=g1["d"], ln1_g=g1["ln_g"], ln1_b=g1["ln_b"])
    tok = emit("ffn1", G)
    return [(dh0, 1.0)], G, tok


def local_step(x, target, meta_full, getw, emit, ahead=lambda l, stage, after: None):
    t = x.shape[0] + BLK
    tabs = rope_tables(t)
    xr, hb = build_h0(meta_full, x)
    ln = None
    saved = []
    for l in range(NL):
        xr, ln, hb, sv = layer_fwd(xr, ln, hb, functools.partial(getw, l), tabs,
                                   lambda dl, stage, after, l=l: ahead(l + dl, stage, after))
        saved.append(sv)
    dy, loss = loss_head(xr, ln, target)
    parts = [(dy, 1.0)]
    grads = [None] * NL
    tok = None
    for l in range(NL - 1, -1, -1):
        parts, grads[l], tok = layer_bwd(parts, saved[l], functools.partial(emit, l), tabs, tok)
    gx, gmeta = split_dh0(parts[0][0], tok)
    return loss, gx, gmeta, grads


_SMALL = ["ln1_g", "ln1_b", "ln2_g", "ln2_b", "ln3_g", "ln3_b", "conv_b", "ssd_norm_g", "mla_q_norm_g",
          "mla_kv_norm_g", "dt_bias", "a_log", "d_skip", "fox_f_b"]
_SMALL_ROWS = 8
_BIG = ["ffn1_w_gate", "ffn1_w_up", "ffn1_w_down", "w_in", "conv_w", "mla_w_uq", "mla_w_ukv", "w_out",
        "ffn2_w_gate", "ffn2_w_up", "ffn2_w_down"]
_NAMES = ["meta", "ffn1_w_gate", "ffn1_w_up", "ffn1_w_down", "ln1_g", "ln1_b", "w_in", "conv_w", "conv_b", "dt_bias",
          "a_log", "d_skip", "ssd_norm_g", "fox_f_b", "mla_q_norm_g", "mla_w_uq", "mla_kv_norm_g", "mla_w_ukv", "w_out",
          "ln2_g", "ln2_b", "ffn2_w_gate", "ffn2_w_up", "ffn2_w_down", "ln3_g", "ln3_b"]


def pack_small(p):
    flat = jnp.concatenate([p[n].astype(F32) for n in _SMALL], axis=1)
    return _pad_cols(flat, _SMALL_ROWS * D).reshape(NL * _SMALL_ROWS, D)


def unpack_small(a, like):
    flat = a.reshape(NL, _SMALL_ROWS * D)
    out, at = {}, 0
    for n in _SMALL:
        out[n] = flat[:, at:at + like[n].shape[1]]
        at += like[n].shape[1]
    return out


_STAGES = {"ffn1": ["ffn1_w_gate", "ffn1_w_up", "ffn1_w_down"],
           "mix": ["w_in", "conv_w", "mla_w_uq", "mla_w_ukv", "w_out"],
           "ffn2": ["ffn2_w_gate", "ffn2_w_up", "ffn2_w_down"]}


_FFN_T = ("ffn1_w_gate", "ffn1_w_up", "ffn2_w_gate", "ffn2_w_up")


def stage_weights(l, stage, g, rep):
    if stage != "mix":
        i = stage[3]
        return {"g" + i: g[f"ffn{i}_w_gate"].reshape(D_FF, D), "u" + i: g[f"ffn{i}_w_up"].reshape(D_FF, D),
                "d" + i: g[f"ffn{i}_w_down"].reshape(D_FF, D),
                "ln1_g" if i == "1" else "ln3_g": rep["ln1_g" if i == "1" else "ln3_g"][l][None, :],
                "ln1_b" if i == "1" else "ln3_b": rep["ln1_b" if i == "1" else "ln3_b"][l][None, :]}
    W = {}
    W["w_in"] = g["w_in"].reshape(D, N_INP)
    W["w_out"] = g["w_out"].reshape(D, D)
    W["wq"], W["wk"], W["wv"] = mla_weights(g["mla_w_uq"], g["mla_w_ukv"])
    W["conv_w"] = _unshard_cols(g["conv_w"])
    for k in ("ln2_g", "ln2_b", "conv_b"):
        W[k] = rep[k][l][None, :]
    W["normg"] = rep["ssd_norm_g"][l][None, :]
    W["qg"] = rep["mla_q_norm_g"][l][None, :]
    W["kvg"] = rep["mla_kv_norm_g"][l][None, :]
    W["dtb"] = _lanes(rep["dt_bias"][l], SM_DT)
    W["alog"] = _lanes(rep["a_log"][l], SM_DT)
    W["dskip"] = _lanes(rep["d_skip"][l], SM_DT)
    W["fb"] = _lanes(rep["fox_f_b"][l], SM_F)
    return W


def small_grads(G):
    return {"ln1_g": G["ln1_g"][0], "ln1_b": G["ln1_b"][0], "ln2_g": G["ln2_g"][0], "ln2_b": G["ln2_b"][0],
            "ln3_g": G["ln3_g"][0], "ln3_b": G["ln3_b"][0], "conv_b": G["conv_b"][0], "ssd_norm_g": G["normg"][0],
            "mla_q_norm_g": G["qg"][0], "mla_kv_norm_g": G["kvg"][0], "dt_bias": G["dtb"][0, :SSD_H],
            "a_log": G["alog"][0, :SSD_H], "d_skip": G["dskip"][0, :SSD_H], "fox_f_b": G["fb"][0, SM_F:SM_F + FOX_H]}


def big_grads(G, stage):
    if stage != "mix":
        i = stage[-1]
        return {f"ffn{i}_w_{k}": G[k[0] + i].reshape(N_DEV, HS, D) for k in ("gate", "up", "down")}
    duq, dukv = mla_weight_grads(G["wq"], G["wk"], G["wv"])
    return {"w_in": G["w_in"].reshape(N_DEV, D // N_DEV, N_INP), "w_out": G["w_out"].reshape(N_DEV, D // N_DEV, D),
            "mla_w_uq": duq, "mla_w_ukv": dukv, "conv_w": _shard_cols(G["conv_w"])}


def kernel(x, meta, ffn1_w_gate, ffn1_w_up, ffn1_w_down, ln1_g, ln1_b, w_in, conv_w, conv_b, dt_bias, a_log, d_skip, ssd_norm_g, fox_f_b, mla_q_norm_g, mla_w_uq, mla_kv_norm_g, mla_w_ukv, w_out, ln2_g, ln2_b, ffn2_w_gate, ffn2_w_up, ffn2_w_down, ln3_g, ln3_b, loss_target, m_meta, m_ffn1_w_gate, m_ffn1_w_up, m_ffn1_w_down, m_ln1_g, m_ln1_b, m_w_in, m_conv_w, m_conv_b, m_dt_bias, m_a_log, m_d_skip, m_ssd_norm_g, m_fox_f_b, m_mla_q_norm_g, m_mla_w_uq, m_mla_kv_norm_g, m_mla_w_ukv, m_w_out, m_ln2_g, m_ln2_b, m_ffn2_w_gate, m_ffn2_w_up, m_ffn2_w_down, m_ln3_g, m_ln3_b, v_meta, v_ffn1_w_gate, v_ffn1_w_up, v_ffn1_w_down, v_ln1_g, v_ln1_b, v_w_in, v_conv_w, v_conv_b, v_dt_bias, v_a_log, v_d_skip, v_ssd_norm_g, v_fox_f_b, v_mla_q_norm_g, v_mla_w_uq, v_mla_kv_norm_g, v_mla_w_ukv, v_w_out, v_ln2_g, v_ln2_b, v_ffn2_w_gate, v_ffn2_w_up, v_ffn2_w_down, v_ln3_g, v_ln3_b):
    vals = (meta, ffn1_w_gate, ffn1_w_up, ffn1_w_down, ln1_g, ln1_b, w_in, conv_w, conv_b, dt_bias, a_log, d_skip, ssd_norm_g, fox_f_b, mla_q_norm_g, mla_w_uq, mla_kv_norm_g, mla_w_ukv, w_out, ln2_g, ln2_b, ffn2_w_gate, ffn2_w_up, ffn2_w_down, ln3_g, ln3_b)
    moms = (m_meta, m_ffn1_w_gate, m_ffn1_w_up, m_ffn1_w_down, m_ln1_g, m_ln1_b, m_w_in, m_conv_w, m_conv_b, m_dt_bias, m_a_log, m_d_skip, m_ssd_norm_g, m_fox_f_b, m_mla_q_norm_g, m_mla_w_uq, m_mla_kv_norm_g, m_mla_w_ukv, m_w_out, m_ln2_g, m_ln2_b, m_ffn2_w_gate, m_ffn2_w_up, m_ffn2_w_down, m_ln3_g, m_ln3_b)
    vars_ = (v_meta, v_ffn1_w_gate, v_ffn1_w_up, v_ffn1_w_down, v_ln1_g, v_ln1_b, v_w_in, v_conv_w, v_conv_b, v_dt_bias, v_a_log, v_d_skip, v_ssd_norm_g, v_fox_f_b, v_mla_q_norm_g, v_mla_w_uq, v_mla_kv_norm_g, v_mla_w_ukv, v_w_out, v_ln2_g, v_ln2_b, v_ffn2_w_gate, v_ffn2_w_up, v_ffn2_w_down, v_ln3_g, v_ln3_b)
    P = dict(zip(_NAMES, vals))
    M = dict(zip(_NAMES, moms))
    V = dict(zip(_NAMES, vars_))
    me = 4 * lax.axis_index("x") + 2 * lax.axis_index("y") + lax.axis_index("c")

    me_arr = me.astype(jnp.int32).reshape(1)
    for n in _FFN_T:
        P[n], M[n], V[n] = (jnp.swapaxes(a[n], 1, 2) for a in (P, M, V))
    src = dict(P)
    src["w_in"] = w_in_to_padded(P["w_in"])
    order = [("meta", 0)] + [(n, l) for l in range(NL) for names in _STAGES.values() for n in names]
    zone_of = {nl_: i for i, nl_ in enumerate(order)}
    zones = [place_own(P["meta"][None], 0, F32, me_arr)]
    zones += [place_own(src[n], l, F32 if n == "conv_w" else BF16, me_arr) for n, l in order[1:]]
    hg = gather_start(zones, "gather_start")
    relays = {}

    def ahead(l, stage, after):
        if l < NL and (l, stage) not in relays:
            idxs = [0] if stage == "meta" else [zone_of[(n, l)] for n in _STAGES[stage]]
            relays[(l, stage)] = (idxs, gather_relay(hg, idxs, f"gather_relay_{l}_{stage}", after))

    def arrived(l, stage, after):
        ahead(l, stage, after)
        idxs, rl = relays[(l, stage)]
        return gather_wait(hg, rl, idxs, f"gather_wait_{l}_{stage}", after)

    meta_full = _unshard_cols(arrived(0, "meta", hg["token"])[0])

    def getw(l, stage, after):
        return stage_weights(l, stage, dict(zip(_STAGES[stage], arrived(l, stage, after))), P)

    sent, last = {}, {}

    def emit(l, stage, G):
        bg = big_grads(G, stage)
        if (l, stage) == (0, "ffn1"):
            last.update(bg)
            return None
        sent[(l, stage)] = exchange_start("scatter", [bg[n] for n in _STAGES[stage]], f"scatter_start_{l}_{stage}")
        return sent[(l, stage)]["token"]

    loss, gx, gmeta, grads = local_step(x[0], loss_target[0], meta_full, getw, emit, ahead)

    small = jnp.concatenate([pack_small({n: jnp.stack([small_grads(g)[n] for g in grads]) for n in _SMALL}), gmeta], axis=0)
    hs = exchange_start("gather", [place_own(small[None], 0, F32, me_arr)], "small_start")
    sent[(0, "ffn1")] = exchange_start("scatter", [last[n] for n in _STAGES["ffn1"]], "scatter_start_0_ffn1", hs["token"])

    out = {}
    after = sent[(0, "ffn1")]["token"]
    for stage in ("ffn2", "mix", "small", "ffn1"):
        if stage == "small":
            gsmall = sum_slots(exchange_wait(hs, [0], "small_wait", after)[1][0])
            gm = lax.dynamic_slice(gsmall[NL * _SMALL_ROWS:], (0, me * (D // N_DEV)), (N_META, D // N_DEV))
            out["meta"] = (gm,) + adamw(P["meta"], M["meta"], V["meta"], g=gm)
            gs = gsmall[:NL * _SMALL_ROWS]
            sd, sm_, sv_ = adamw(pack_small(P), pack_small(M), pack_small(V), g=gs)
            ups = [unpack_small(a, P) for a in (gs, sd, sm_, sv_)]
            for n in _SMALL:
                out[n] = tuple(u[n] for u in ups)
            after = sd
            continue
        names = _STAGES[stage]
        got = [exchange_wait(sent[(l, stage)], list(range(len(names))), f"scatter_wait_{l}_{stage}", after)
               for l in range(NL - 1, -1, -1)][::-1]
        for i, n in enumerate(names):
            own = [got[l][0][i] for l in range(NL)]
            recv = [got[l][1][i] for l in range(NL)]
            if n == "w_in":
                g = jnp.stack([w_in_from_padded(sum_slots(recv[l], own[l], me_arr)) for l in range(NL)])
                out[n] = (g,) + adamw(P[n], M[n], V[n], g=g)
            else:
                out[n] = adamw(P[n], M[n], V[n], recv=recv, own=own, me_arr=me_arr)
                if n in _FFN_T:
                    out[n] = tuple(jnp.swapaxes(a, 1, 2) for a in out[n])
        after = out[names[-1]][1]

    loss_all = lax.psum(loss[0, 0], ("x", "y", "c"))
    flat = [loss_all, gx[None]]
    for k in range(4):
        flat += [out[n][k] for n in _NAMES]
    return tuple(flat)
```

```python
import functools

import jax
import jax.numpy as jnp
from jax import lax
from jax.experimental import pallas as pl
from jax.experimental.pallas import tpu as pltpu

F32, BF16 = jnp.float32, jnp.bfloat16
HI = lax.Precision.HIGHEST

N_DEV = 8
D = 1024
NL = 2
N_META = 16
BLK = 128
PAD = BLK - N_META
D_FF = 2816
HS = D_FF // N_DEV
SSD_H, SSD_P, SSD_N, SSD_G = 8, 64, 64, 2
SSD_D = SSD_H * SSD_P
CONV_K = 4
CONV_D = SSD_D + 2 * SSD_G * SSD_N
FOX_H, FOX_DH = 4, 64
MLA_H, MLA_QL, MLA_KVL, MLA_NOPE, MLA_ROPE, MLA_V = 4, 256, 128, 64, 32, 64
N_IN = 2476
C_Z, C_XBC, C_FQ, C_FK, C_FV, C_CQ, C_CKV, C_SM, N_INP = 0, 512, 1280, 1536, 1792, 2048, 2304, 2432, 2560
SM_DT, SM_F, SM_KR = 0, 8, 64
ALPHA = (2 * NL) ** 0.25
EPS = 1e-5
NEG = -1e30
LR, B1, B2, AEPS, WD, STEP = 0.001, 0.9, 0.999, 1e-08, 0.01, 10
VMEM_MB = 56


def _cp(*sem):
    return pltpu.CompilerParams(dimension_semantics=sem, vmem_limit_bytes=VMEM_MB << 20)


def _nn(a, b):
    return lax.dot_general(a, b, (((1,), (0,)), ((), ())), preferred_element_type=F32)


def _nt(a, b):
    return lax.dot_general(a, b, (((1,), (1,)), ((), ())), preferred_element_type=F32)


def _tn(a, b):
    return lax.dot_general(a, b, (((0,), (0,)), ((), ())), preferred_element_type=F32)


def _nn_hi(a, b):
    return lax.dot_general(a, b, (((1,), (0,)), ((), ())), precision=HI, preferred_element_type=F32)


def _row_tile(t):
    for d in range(640, 15, -16):
        if t % d == 0:
            return d
    raise ValueError(t)


def _sig(x):
    return 1.0 / (1.0 + jnp.exp(-x))


def _tri(lower=True):
    r = lax.broadcasted_iota(jnp.int32, (BLK, BLK), 0)
    c = lax.broadcasted_iota(jnp.int32, (BLK, BLK), 1)
    return (r >= c) if lower else (r <= c)


def build_h0(meta_full, x):
    s = x.shape[0]
    nb = s // BLK + 1

    def body(m_ref, x_ref, h_ref, hb_ref):
        i = pl.program_id(0)

        @pl.when(i == 0)
        def _():
            h = jnp.concatenate([jnp.zeros((PAD, D), F32), m_ref[...]], axis=0)
            h_ref[...] = h
            hb_ref[...] = h.astype(BF16)

        @pl.when(i > 0)
        def _():
            h_ref[...] = x_ref[...]
            hb_ref[...] = x_ref[...].astype(BF16)

    return pl.pallas_call(
        body, name="build_h0", grid=(nb,),
        in_specs=[pl.BlockSpec((N_META, D), lambda i: (0, 0)),
                  pl.BlockSpec((BLK, D), lambda i: (jnp.maximum(i - 1, 0), 0))],
        out_specs=[pl.BlockSpec((BLK, D), lambda i: (i, 0))] * 2,
        out_shape=[jax.ShapeDtypeStruct((nb * BLK, D), F32), jax.ShapeDtypeStruct((nb * BLK, D), BF16)],
        compiler_params=_cp("arbitrary"),
    )(meta_full, x)


FT = 256


def _layer_norm(r, gamma, beta):
    mu = jnp.mean(r, axis=1, keepdims=True)
    xc = r - mu
    var = jnp.mean(xc * xc, axis=1, keepdims=True)
    return xc * lax.rsqrt(var + EPS) * gamma + beta


def ffn_fwd_seq(x, ln_in, wg, wu, wd, ln_out):
    t = x.shape[0]
    f = wg.shape[0]
    nj, nr = f // FT, t // _row_tile(t)
    rc = t // nr
    plain = ln_in is None
    gi, bi = ln_out if plain else ln_in

    def body(x_hbm, gi_ref, bi_ref, go_ref, bo_ref, wg_ref, wu_ref, wd_ref, u_ref, v_ref, r_hbm, yb_hbm,
             acc, hbs, xbuf, sem_in, sem_out):
        j = pl.program_id(0)

        @pl.when(j == 0)
        def _():
            def fetch(k):
                return pltpu.make_async_copy(x_hbm.at[pl.ds(k * rc, rc)], xbuf.at[k % 2], sem_in.at[k % 2])

            fetch(0).start()
            for k in range(nr):
                if k + 1 < nr:
                    fetch(k + 1).start()
                fetch(k).wait()
                h = xbuf[k % 2]
                if not plain:
                    h = _layer_norm(h, gi_ref[...], bi_ref[...])
                acc[k * rc:(k + 1) * rc, :] = ALPHA * h
                hbs[k * rc:(k + 1) * rc, :] = h.astype(BF16)

        for k in range(nr):
            sl = slice(k * rc, (k + 1) * rc)
            h = hbs[sl, :]
            u = _nt(h, wg_ref[...])
            v = _nt(h, wu_ref[...])
            u_ref[sl, :] = u.astype(BF16)
            v_ref[sl, :] = v.astype(BF16)
            acc[sl, :] += _nn((0.5 * u * _sig(u) * v).astype(BF16), wd_ref[...])

        @pl.when(j == nj - 1)
        def _():
            r_cp = pltpu.make_async_copy(acc, r_hbm, sem_out.at[0])
            r_cp.start()
            for k in range(nr):
                sl = slice(k * rc, (k + 1) * rc)
                hbs[sl, :] = _layer_norm(acc[sl, :], go_ref[...], bo_ref[...]).astype(BF16)
            y_cp = pltpu.make_async_copy(hbs, yb_hbm, sem_out.at[1])
            y_cp.start()
            r_cp.wait()
            y_cp.wait()

    vec = pl.BlockSpec((1, D), lambda j: (0, 0))
    wsp = pl.BlockSpec((FT, D), lambda j: (j, 0))
    act = pl.BlockSpec((None, t, FT), lambda j: (j, 0, 0))
    return pl.pallas_call(
        body, name="ffn_fwd_seq", grid=(nj,),
        in_specs=[_ANY, vec, vec, vec, vec, wsp, wsp, wsp],
        out_specs=[act, act, _ANY, _ANY],
        out_shape=[jax.ShapeDtypeStruct((nj, t, FT), BF16), jax.ShapeDtypeStruct((nj, t, FT), BF16),
                   jax.ShapeDtypeStruct((t, D), F32), jax.ShapeDtypeStruct((t, D), BF16)],
        scratch_shapes=[pltpu.VMEM((t, D), F32), pltpu.VMEM((t, D), BF16), pltpu.VMEM((2, rc, D), F32),
                        pltpu.SemaphoreType.DMA((2,)), pltpu.SemaphoreType.DMA((2,))],
        compiler_params=_cp("arbitrary"),
    )(x, gi, bi, ln_out[0], ln_out[1], wg, wu, wd)


def ffn_bwd_seq(parts, r, gamma, hb, u, v, wg, wu, wd, after=None):
    nj, t, _ = u.shape
    f = nj * FT
    nr = t // _row_tile(t)
    rc = t // nr
    nc = t // BLK
    scales = [s for _, s in parts]
    npart = len(parts)
    extra = [] if after is None else [after]

    def body(*refs):
        refs = refs[len(extra):]
        p_hbm, refs = refs[:npart], refs[npart:]
        (r_hbm, g_ref, hb_hbm, u_ref, v_ref, wg_ref, wu_ref, wd_ref, dh_hbm, dwg_ref, dwu_ref, dwd_ref, dg_ref, db_ref,
         dfs, hbt, dft, dhacc, dus, dvs, acs, pbuf, rbuf, hbuf, sems, sem_out) = refs
        j = pl.program_id(0)

        @pl.when(j == 0)
        def _():
            def fetch(c):
                rows = pl.ds(c * BLK, BLK)
                cps = [pltpu.make_async_copy(p_hbm[p].at[rows], pbuf.at[c % 2, p], sems.at[c % 2, p]) for p in range(npart)]
                cps.append(pltpu.make_async_copy(r_hbm.at[rows], rbuf.at[c % 2], sems.at[c % 2, npart]))
                cps.append(pltpu.make_async_copy(hb_hbm.at[rows], hbuf.at[c % 2], sems.at[c % 2, npart + 1]))
                return cps

            for cp in fetch(0):
                cp.start()
            dg = jnp.zeros((1, D), F32)
            db = jnp.zeros((1, D), F32)
            for c in range(nc):
                if c + 1 < nc:
                    for cp in fetch(c + 1):
                        cp.start()
                for cp in fetch(c):
                    cp.wait()
                sl = slice(c * BLK, (c + 1) * BLK)
                dy = scales[0] * pbuf[c % 2, 0]
                for p in range(1, npart):
                    dy += scales[p] * pbuf[c % 2, p]
                rr = rbuf[c % 2]
                xc = rr - jnp.mean(rr, axis=1, keepdims=True)
                rstd = lax.rsqrt(jnp.mean(xc * xc, axis=1, keepdims=True) + EPS)
                xh = xc * rstd
                dxh = dy * g_ref[...]
                dr = rstd * (dxh - jnp.mean(dxh, axis=1, keepdims=True) - xh * jnp.mean(dxh * xh, axis=1, keepdims=True))
                dg += jnp.sum(dy * xh, axis=0, keepdims=True)
                db += jnp.sum(dy, axis=0, keepdims=True)
                dhacc[sl, :] = ALPHA * dr
                dfc = (0.5 * dr).astype(BF16)
                dfs[sl, :] = dfc
                dft[:, sl] = dfc.T
                hbt[:, sl] = hbuf[c % 2].T
            dg_ref[...] = dg
            db_ref[...] = db

        for k in range(nr):
            sl = slice(k * rc, (k + 1) * rc)
            da = _nt(dfs[sl, :], wd_ref[...])
            uu = u_ref[sl, :].astype(F32)
            vv = v_ref[sl, :].astype(F32)
            sg = _sig(uu)
            du = (da * vv * (sg * (1.0 + uu * (1.0 - sg)))).astype(BF16)
            dv = (da * uu * sg).astype(BF16)
            dus[sl, :] = du
            dvs[sl, :] = dv
            acs[sl, :] = (uu * sg * vv).astype(BF16)
            dhacc[sl, :] += _nn(du, wg_ref[...]) + _nn(dv, wu_ref[...])
        dwg_ref[...] = _nn(hbt[...], dus[...]).astype(BF16).T
        dwu_ref[...] = _nn(hbt[...], dvs[...]).astype(BF16).T
        dwd_ref[...] = _nn(dft[...], acs[...]).astype(BF16).T

        @pl.when(j == nj - 1)
        def _():
            cp = pltpu.make_async_copy(dhacc, dh_hbm, sem_out.at[0])
            cp.start()
            cp.wait()

    vec = pl.BlockSpec((1, D), lambda j: (0, 0))
    wsp = pl.BlockSpec((FT, D), lambda j: (j, 0))
    act = pl.BlockSpec((None, t, FT), lambda j: (j, 0, 0))
    return pl.pallas_call(
        body, name="ffn_bwd_seq", grid=(nj,),
        in_specs=[_ANY] * (len(extra) + npart + 1) + [vec, _ANY, act, act, wsp, wsp, wsp],
        out_specs=[_ANY, wsp, wsp, wsp, vec, vec],
        out_shape=[jax.ShapeDtypeStruct((t, D), F32)] + [jax.ShapeDtypeStruct((f, D), BF16)] * 3
        + [jax.ShapeDtypeStruct((1, D), F32)] * 2,
        scratch_shapes=[pltpu.VMEM((t, D), BF16), pltpu.VMEM((D, t), BF16), pltpu.VMEM((D, t), BF16),
                        pltpu.VMEM((t, D), F32), pltpu.VMEM((t, FT), BF16), pltpu.VMEM((t, FT), BF16),
                        pltpu.VMEM((t, FT), BF16), pltpu.VMEM((2, npart, BLK, D), F32), pltpu.VMEM((2, BLK, D), F32),
                        pltpu.VMEM((2, BLK, D), BF16), pltpu.SemaphoreType.DMA((2, npart + 2)),
                        pltpu.SemaphoreType.DMA((1,))],
        compiler_params=_cp("arbitrary"),
    )(*extra, *[p for p, _ in parts], r, gamma, hb, u, v, wg, wu, wd)


def mm_res_ln(a, b, x, ln_in, ln_out):
    t, k = a.shape
    tm = _row_tile(t)

    def body(a_ref, b_ref, x_ref, gi_ref, bi_ref, go_ref, bo_ref, r_ref, yb_ref):
        r = ALPHA * _layer_norm(x_ref[...], gi_ref[...], bi_ref[...]) + _nn(a_ref[...], b_ref[...])
        r_ref[...] = r
        yb_ref[...] = _layer_norm(r, go_ref[...], bo_ref[...]).astype(BF16)

    row = pl.BlockSpec((tm, D), lambda i: (i, 0))
    vec = pl.BlockSpec((1, D), lambda i: (0, 0))
    return pl.pallas_call(
        body, name="mm_res_ln", grid=(t // tm,),
        in_specs=[pl.BlockSpec((tm, k), lambda i: (i, 0)), pl.BlockSpec((k, D), lambda i: (0, 0)), row, vec, vec, vec, vec],
        out_specs=[row, row],
        out_shape=[jax.ShapeDtypeStruct((t, D), F32), jax.ShapeDtypeStruct((t, D), BF16)],
        compiler_params=_cp("arbitrary"),
    )(a, b, x, ln_in[0], ln_in[1], ln_out[0], ln_out[1])


def mm_nn(a, b):
    t, k = a.shape
    n = tn = b.shape[1]
    tm = _row_tile(t)

    def body(a_ref, b_ref, o_ref):
        o_ref[...] = _nn(a_ref[...], b_ref[...])

    return pl.pallas_call(
        body, name="mm_nn", grid=(t // tm, n // tn),
        in_specs=[pl.BlockSpec((tm, k), lambda i, j: (i, 0)), pl.BlockSpec((k, tn), lambda i, j: (0, j))],
        out_specs=pl.BlockSpec((tm, tn), lambda i, j: (i, j)),
        out_shape=jax.ShapeDtypeStruct((t, n), F32),
        compiler_params=_cp("arbitrary", "arbitrary"),
    )(a, b)


def mm_nt_reduce(pairs, n):
    g, t, _ = pairs[0][0].shape
    tm = _row_tile(t)
    npair = len(pairs)

    def body(*refs):
        o_ref = refs[-1]
        gi = pl.program_id(1)
        tot = _nt(refs[0][...], refs[1][...])
        for p in range(1, npair):
            tot += _nt(refs[2 * p][...], refs[2 * p + 1][...])

        @pl.when(gi == 0)
        def _():
            o_ref[...] = tot

        @pl.when(gi > 0)
        def _():
            o_ref[...] += tot

    in_specs, args = [], []
    for x, w in pairs:
        k = x.shape[2]
        in_specs += [pl.BlockSpec((None, tm, k), lambda i, gi: (gi, i, 0)),
                     pl.BlockSpec((None, n, k), lambda i, gi: (gi, 0, 0))]
        args += [x, w]
    return pl.pallas_call(
        body, name="mm_nt_reduce", grid=(t // tm, g),
        in_specs=in_specs, out_specs=pl.BlockSpec((tm, n), lambda i, gi: (i, 0)),
        out_shape=jax.ShapeDtypeStruct((t, n), F32),
        compiler_params=_cp("arbitrary", "arbitrary"),
    )(*args)


def mm_tn(x, y, out_dtype=BF16):
    gx, t, k = x.shape
    gy, _, n = y.shape
    g = max(gx, gy)
    tm = _row_tile(t)
    nt = t // tm

    def body(x_ref, y_ref, o_ref, acc):
        i = pl.program_id(1)

        @pl.when(i == 0)
        def _():
            acc[...] = jnp.zeros_like(acc)

        acc[...] += _tn(x_ref[...], y_ref[...])

        @pl.when(i == nt - 1)
        def _():
            o_ref[...] = acc[...].astype(out_dtype)

    return pl.pallas_call(
        body, name="mm_tn", grid=(g, nt),
        in_specs=[pl.BlockSpec((None, tm, k), (lambda gi, i: (gi, i, 0)) if gx > 1 else (lambda gi, i: (0, i, 0))),
                  pl.BlockSpec((None, tm, n), (lambda gi, i: (gi, i, 0)) if gy > 1 else (lambda gi, i: (0, i, 0)))],
        out_specs=pl.BlockSpec((None, k, n), lambda gi, i: (gi, 0, 0)),
        out_shape=jax.ShapeDtypeStruct((g, k, n), out_dtype),
        scratch_shapes=[pltpu.VMEM((k, n), F32)],
        compiler_params=_cp("arbitrary", "arbitrary"),
    )(x, y)


def ln_bwd(parts, r, gamma, out_scale, after=None):
    t = r.shape[0]
    tm = _row_tile(t)
    scales = [s for _, s in parts]
    npart = len(parts)
    extra = [] if after is None else [after]

    def body(*refs):
        refs = refs[len(extra):]
        r_ref, g_ref = refs[npart], refs[npart + 1]
        dr_ref, drb_ref, dg_ref, db_ref = refs[npart + 2:]
        i = pl.program_id(0)
        dy = scales[0] * refs[0][...]
        for p in range(1, npart):
            dy += scales[p] * refs[p][...]
        rr = r_ref[...]
        mu = jnp.mean(rr, axis=1, keepdims=True)
        xc = rr - mu
        rstd = lax.rsqrt(jnp.mean(xc * xc, axis=1, keepdims=True) + EPS)
        xh = xc * rstd
        dxh = dy * g_ref[...]
        m1 = jnp.mean(dxh, axis=1, keepdims=True)
        m2 = jnp.mean(dxh * xh, axis=1, keepdims=True)
        dr = rstd * (dxh - m1 - xh * m2)
        dr_ref[...] = dr
        drb_ref[...] = (out_scale * dr).astype(BF16)
        dg = jnp.sum(dy * xh, axis=0, keepdims=True)
        db = jnp.sum(dy, axis=0, keepdims=True)

        @pl.when(i == 0)
        def _():
            dg_ref[...] = dg
            db_ref[...] = db

        @pl.when(i > 0)
        def _():
            dg_ref[...] += dg
            db_ref[...] += db

    row = pl.BlockSpec((tm, D), lambda i: (i, 0))
    vec = pl.BlockSpec((1, D), lambda i: (0, 0))
    return pl.pallas_call(
        body, name="ln_bwd", grid=(t // tm,),
        in_specs=[_ANY] * len(extra) + [row] * (npart + 1) + [vec],
        out_specs=[row, row, vec, vec],
        out_shape=[jax.ShapeDtypeStruct((t, D), F32), jax.ShapeDtypeStruct((t, D), BF16),
                   jax.ShapeDtypeStruct((1, D), F32), jax.ShapeDtypeStruct((1, D), F32)],
        compiler_params=_cp("arbitrary"),
    )(*extra, *[p for p, _ in parts], r, gamma)


def loss_head(r, ln, target):
    t = r.shape[0]
    nb = t // BLK

    def body(r_ref, g_ref, b_ref, t_ref, dy_ref, l_ref):
        i = pl.program_id(0)

        @pl.when(i == 0)
        def _():
            dy_ref[...] = jnp.zeros_like(dy_ref)
            l_ref[...] = jnp.zeros_like(l_ref)

        @pl.when(i > 0)
        def _():
            err = _layer_norm(r_ref[...], g_ref[...], b_ref[...]) - t_ref[...]
            dy_ref[...] = err * (1.0 / D)
            l_ref[...] += (0.5 / D) * jnp.sum(err * err, keepdims=True)

    vec = pl.BlockSpec((1, D), lambda i: (0, 0))
    return pl.pallas_call(
        body, name="loss_head", grid=(nb,),
        in_specs=[pl.BlockSpec((BLK, D), lambda i: (i, 0)), vec, vec,
                  pl.BlockSpec((BLK, D), lambda i: (jnp.maximum(i - 1, 0), 0))],
        out_specs=[pl.BlockSpec((BLK, D), lambda i: (i, 0)), pl.BlockSpec((1, 1), lambda i: (0, 0))],
        out_shape=[jax.ShapeDtypeStruct((t, D), F32), jax.ShapeDtypeStruct((1, 1), F32)],
        compiler_params=_cp("arbitrary"),
    )(r, ln[0], ln[1], target)


def split_dh0(dh0, after=None):
    t = dh0.shape[0]
    nb = t // BLK
    extra = [] if after is None else [after]

    def body(*refs):
        a_ref, gx_ref, gm_ref = refs[len(extra):]
        i = pl.program_id(0)
        tot = a_ref[...]

        @pl.when(i == 0)
        def _():
            gm_ref[...] = tot[PAD:, :]

        @pl.when(i > 0)
        def _():
            gx_ref[...] = tot

    blk = pl.BlockSpec((BLK, D), lambda i: (i, 0))
    return pl.pallas_call(
        body, name="split_dh0", grid=(nb,),
        in_specs=[_ANY] * len(extra) + [blk],
        out_specs=[pl.BlockSpec((BLK, D), lambda i: (jnp.maximum(i - 1, 0), 0)),
                   pl.BlockSpec((N_META, D), lambda i: (0, 0))],
        out_shape=[jax.ShapeDtypeStruct((t - BLK, D), F32), jax.ShapeDtypeStruct((N_META, D), F32)],
        compiler_params=_cp("arbitrary"),
    )(*extra, dh0)


def _valid_rows(nrows, first_row):
    return (first_row + lax.broadcasted_iota(jnp.int32, (nrows, 1), 0)) >= PAD


def conv_fwd(proj, conv_w, conv_b):
    t = proj.shape[0]
    c0 = C_XBC // BLK

    def body(x_ref, w_ref, b_ref, o_ref):
        ok = _valid_rows(t, 0)
        x = jnp.where(ok, x_ref[...], 0.0)
        w = w_ref[...]
        acc = b_ref[...] + w[CONV_K - 1:CONV_K, :] * x
        for s in range(1, CONV_K):
            acc += w[CONV_K - 1 - s:CONV_K - s, :] * pltpu.roll(x, s, 0)
        o_ref[...] = jnp.where(ok, acc * _sig(acc), 0.0)

    return pl.pallas_call(
        body, name="conv_fwd", grid=(CONV_D // BLK,),
        in_specs=[pl.BlockSpec((t, BLK), lambda j: (0, c0 + j)),
                  pl.BlockSpec((CONV_K, BLK), lambda j: (0, j)), pl.BlockSpec((1, BLK), lambda j: (0, j))],
        out_specs=pl.BlockSpec((t, BLK), lambda j: (0, j)),
        out_shape=jax.ShapeDtypeStruct((t, CONV_D), F32),
        compiler_params=_cp("arbitrary"),
    )(proj, conv_w, conv_b)


def conv_bwd(dxa, proj, conv_w, conv_b):
    t = proj.shape[0]
    c0 = C_XBC // BLK

    def body(d_ref, x_ref, w_ref, b_ref, dx_ref, dw_ref, db_ref):
        ok = _valid_rows(t, 0)
        x = jnp.where(ok, x_ref[...], 0.0)
        w = w_ref[...]
        xs = [x] + [pltpu.roll(x, s, 0) for s in range(1, CONV_K)]
        acc = b_ref[...] + w[CONV_K - 1:CONV_K, :] * x
        for s in range(1, CONV_K):
            acc += w[CONV_K - 1 - s:CONV_K - s, :] * xs[s]
        sg = _sig(acc)
        dxc = jnp.where(ok, d_ref[...] * (sg * (1.0 + acc * (1.0 - sg))), 0.0)
        db_ref[...] = jnp.sum(dxc, axis=0, keepdims=True)
        dw_ref[...] = jnp.concatenate(
            [jnp.sum(dxc * xs[CONV_K - 1 - k], axis=0, keepdims=True) for k in range(CONV_K)], axis=0)
        dx = w[CONV_K - 1:CONV_K, :] * dxc
        for s in range(1, CONV_K):
            dx += w[CONV_K - 1 - s:CONV_K - s, :] * pltpu.roll(dxc, t - s, 0)
        dx_ref[...] = jnp.where(ok, dx, 0.0)

    col = pl.BlockSpec((t, BLK), lambda j: (0, j))
    return pl.pallas_call(
        body, name="conv_bwd", grid=(CONV_D // BLK,),
        in_specs=[col, pl.BlockSpec((t, BLK), lambda j: (0, c0 + j)),
                  pl.BlockSpec((CONV_K, BLK), lambda j: (0, j)), pl.BlockSpec((1, BLK), lambda j: (0, j))],
        out_specs=[col, pl.BlockSpec((CONV_K, BLK), lambda j: (0, j)), pl.BlockSpec((1, BLK), lambda j: (0, j))],
        out_shape=[jax.ShapeDtypeStruct((t, CONV_D), F32), jax.ShapeDtypeStruct((CONV_K, CONV_D), F32),
                   jax.ShapeDtypeStruct((1, CONV_D), F32)],
        compiler_params=_cp("arbitrary"),
    )(dxa, proj, conv_w, conv_b)


def _softplus(x):
    return jnp.maximum(x, 0.0) + jnp.log(1.0 + jnp.exp(-jnp.abs(x)))


GW = SSD_D // SSD_G
HPG = SSD_H // SSD_G


def _head_expand():
    r = lax.broadcasted_iota(jnp.int32, (BLK, SSD_D), 0)
    c = lax.broadcasted_iota(jnp.int32, (BLK, SSD_D), 1)
    rt = lax.broadcasted_iota(jnp.int32, (SSD_D, BLK), 0)
    ct = lax.broadcasted_iota(jnp.int32, (SSD_D, BLK), 1)
    return (c // SSD_P == r).astype(F32), (rt // SSD_P == ct).astype(F32)


def _ssd_chunk(xa, sm, dtb, alog, dskip, ok, sp):
    e, et = _head_expand()
    dt = jnp.where(ok, _softplus(sm + dtb), 0.0)
    amat = -jnp.exp(alog)
    tri = _tri()
    ac = _nn_hi(tri.astype(F32), dt * amat)
    act = ac.T
    ace, dte, dse = _nn_hi(ac, e), _nn_hi(dt, e), _nn_hi(dskip, e)
    laste = ace[BLK - 1:BLK, :]
    ee, dece, gle = jnp.exp(ace), jnp.exp(laste - ace), jnp.exp(laste)
    xs = xa[:, :SSD_D]
    xdt = xs * dte
    decx = dece * xdt
    xdtb = xdt.astype(BF16)
    d = dict(e=e, et=et, dt=dt, amat=amat, tri=tri, ac=ac, act=act, dte=dte, dse=dse, ee=ee, dece=dece, gle=gle, xs=xs,
             xdt=xdt, xdtb=xdtb, decx=decx, bg=[], cg=[], cb=[], yo=[], seg=[], m=[], new_s=[])
    ys = []
    for g in range(SSD_G):
        cols = slice(GW * g, GW * (g + 1))
        bg = xa[:, SSD_D + SSD_N * g:SSD_D + SSD_N * (g + 1)].astype(BF16)
        cg = xa[:, SSD_D + SSD_G * SSD_N + SSD_N * g:SSD_D + SSD_G * SSD_N + SSD_N * (g + 1)].astype(BF16)
        spg = sp[:, cols]
        sloc = _tn(bg, decx[:, cols].astype(BF16))
        yo = _nn(cg, spg.astype(BF16)) * ee[:, cols]
        cb = _nt(cg, bg)
        d["new_s"].append(gle[:, cols] * spg + sloc)
        yds = []
        for h in range(HPG * g, HPG * (g + 1)):
            seg = jnp.where(tri, jnp.exp(jnp.minimum(ac[:, h:h + 1] - act[h:h + 1, :], 0.0)), 0.0)
            m = cb * seg
            yds.append(_nn(m.astype(BF16), xdtb[:, SSD_P * h:SSD_P * (h + 1)]))
            d["seg"].append(seg)
            d["m"].append(m)
        ys.append(jnp.concatenate(yds, axis=1) + yo)
        for k, val in (("bg", bg), ("cg", cg), ("cb", cb), ("yo", yo)):
            d[k].append(val)
    d["y"] = jnp.concatenate(ys, axis=1) + dse * xs
    return d


def ssd_fwd(xa, proj, dtb, alog, dskip, normg):
    t = xa.shape[0]
    nb = t // BLK
    gw = SSD_D // SSD_G

    def body(xa_ref, z_ref, sm_ref, dtb_ref, al_ref, ds_ref, ng_ref, y_ref, sp_ref, st):
        c = pl.program_id(0)

        @pl.when(c == 0)
        def _():
            st[...] = jnp.zeros_like(st)

        ok = _valid_rows(BLK, c * BLK)
        sp = st[...]
        sp_ref[...] = sp
        d = _ssd_chunk(xa_ref[...], sm_ref[...], dtb_ref[...], al_ref[...], ds_ref[...], ok, sp)
        st[...] = jnp.concatenate(d["new_s"], axis=1)
        y = d["y"]
        z = z_ref[...]
        yg = y * (z * _sig(z))
        outs = []
        for g in range(SSD_G):
            v = yg[:, gw * g:gw * (g + 1)]
            outs.append(v * lax.rsqrt(jnp.mean(v * v, axis=1, keepdims=True) + EPS))
        y_ref[...] = (jnp.concatenate(outs, axis=1) * ng_ref[...]).astype(BF16)

    vec = pl.BlockSpec((1, BLK), lambda c: (0, 0))
    return pl.pallas_call(
        body, name="ssd_fwd", grid=(nb,),
        in_specs=[pl.BlockSpec((BLK, CONV_D), lambda c: (c, 0)),
                  pl.BlockSpec((BLK, SSD_D), lambda c: (c, C_Z // SSD_D)),
                  pl.BlockSpec((BLK, BLK), lambda c: (c, C_SM // BLK)),
                  vec, vec, vec, pl.BlockSpec((1, SSD_D), lambda c: (0, 0))],
        out_specs=[pl.BlockSpec((BLK, SSD_D), lambda c: (c, 0)),
                   pl.BlockSpec((None, SSD_N, SSD_D), lambda c: (c, 0, 0))],
        out_shape=[jax.ShapeDtypeStruct((t, SSD_D), BF16), jax.ShapeDtypeStruct((nb, SSD_N, SSD_D), F32)],
        scratch_shapes=[pltpu.VMEM((SSD_N, SSD_D), F32)],
        compiler_params=_cp("arbitrary"),
    )(xa, proj, proj, dtb, alog, dskip, normg)


def _lane_put(col, lane):
    li = lax.broadcasted_iota(jnp.int32, (col.shape[0], BLK), 1)
    return jnp.where(li == lane, col, 0.0)


def ssd_bwd(dmix, xa, proj, sprev, dtb, alog, dskip, normg):
    t = xa.shape[0]
    nb = t // BLK
    gw = SSD_D // SSD_G
    rev = lambda c: nb - 1 - c

    def body(dy_ref, xa_ref, z_ref, sm_ref, sp_ref, dtb_ref, al_ref, ds_ref, ng_ref,
             dxa_ref, dz_ref, dsm_ref, dng_ref, dds_ref, dal_ref, ddtb_ref, dst):
        c = pl.program_id(0)

        @pl.when(c == 0)
        def _():
            dst[...] = jnp.zeros_like(dst)
            dng_ref[...] = jnp.zeros_like(dng_ref)
            dds_ref[...] = jnp.zeros_like(dds_ref)
            dal_ref[...] = jnp.zeros_like(dal_ref)
            ddtb_ref[...] = jnp.zeros_like(ddtb_ref)

        ok = _valid_rows(BLK, rev(c) * BLK)
        sm = sm_ref[...]
        sp = sp_ref[...]
        d = _ssd_chunk(xa_ref[...], sm, dtb_ref[...], al_ref[...], ds_ref[...], ok, sp)
        dt, amat, ac, act, tri, et, xs, xdt = (d[k] for k in ("dt", "amat", "ac", "act", "tri", "et", "xs", "xdt"))
        rowi = lax.broadcasted_iota(jnp.int32, (BLK, 1), 0)
        y = d["y"]
        z = z_ref[...]
        sgz = _sig(z)
        siluz = z * sgz
        yg = y * siluz
        dout = dy_ref[...]
        ng = ng_ref[...]
        dygs, xhs = [], []
        for g in range(SSD_G):
            v = yg[:, gw * g:gw * (g + 1)]
            rr = lax.rsqrt(jnp.mean(v * v, axis=1, keepdims=True) + EPS)
            xh = v * rr
            dxh = dout[:, gw * g:gw * (g + 1)] * ng[:, gw * g:gw * (g + 1)]
            dygs.append(rr * (dxh - xh * jnp.mean(dxh * xh, axis=1, keepdims=True)))
            xhs.append(xh)
        dyg = jnp.concatenate(dygs, axis=1)
        dng_ref[...] += jnp.sum(dout * jnp.concatenate(xhs, axis=1), axis=0, keepdims=True)
        dy = dyg * siluz
        dz_ref[...] = dyg * y * (sgz * (1.0 + z * (1.0 - sgz)))

        triu = _tri(lower=False)
        dyb = dy.astype(BF16)
        dsn = dst[...]
        dds_ref[...] += _nn_hi(jnp.sum(dy * xs, axis=0, keepdims=True), et)
        dac_all = _nn_hi(dy * jnp.concatenate(d["yo"], axis=1), et)
        dyo = (dy * d["ee"]).astype(BF16)
        gl = jnp.exp(ac[BLK - 1:BLK, :])
        dlast = _nn_hi(jnp.sum(dsn * sp, axis=0, keepdims=True), et) * gl
        bds, db_g, dc_g, dxdt_i, new_dst = [], [], [], [], []
        for g in range(SSD_G):
            cols = slice(GW * g, GW * (g + 1))
            bg, cg = d["bg"][g], d["cg"][g]
            dsng = dsn[:, cols].astype(BF16)
            dc = _nt(dyo[:, cols], sp[:, cols].astype(BF16))
            new_dst.append(_tn(cg, dyo[:, cols]) + d["gle"][:, cols] * dsn[:, cols])
            bds.append(_nn(bg, dsng))
            db = _nt(d["decx"][:, cols].astype(BF16), dsng)
            cbt = _nt(bg, cg)
            dcb = jnp.zeros((BLK, BLK), F32)
            for h in range(HPG * g, HPG * (g + 1)):
                hc = slice(SSD_P * h, SSD_P * (h + 1))
                dm = _nt(dyb[:, hc], d["xdtb"][:, hc])
                dcb += dm * d["seg"][h]
                w = dm * d["m"][h]
                dac_all += _lane_put(jnp.sum(w, axis=1, keepdims=True) - jnp.sum(w.T, axis=1, keepdims=True), h)
                segt = jnp.where(triu, jnp.exp(jnp.minimum(act[h:h + 1, :] - ac[:, h:h + 1], 0.0)), 0.0)
                dxdt_i.append(_nn((cbt * segt).astype(BF16), dyb[:, hc]))
            dcbb = dcb.astype(BF16)
            dc_g.append(dc + _nn(dcbb, bg))
            db_g.append(db + _tn(dcbb, cg))
        dst[...] = jnp.concatenate(new_dst, axis=1)
        bds = jnp.concatenate(bds, axis=1)
        tdec = jnp.exp(ac[BLK - 1:BLK, :] - ac) * _nn_hi(xdt * bds, et)
        dlast += jnp.sum(tdec, axis=0, keepdims=True)
        dac_all += jnp.where(rowi == BLK - 1, dlast, 0.0) - tdec
        dxdt = d["dece"] * bds + jnp.concatenate(dxdt_i, axis=1)
        da = _nn_hi(triu.astype(F32), dac_all)
        ddt = _nn_hi(dxdt * xs, et) + da * amat
        dal_ref[...] += jnp.sum(da * dt, axis=0, keepdims=True) * amat
        ddtr = jnp.where(ok, ddt * _sig(sm + dtb_ref[...]), 0.0)
        ddtb_ref[...] += jnp.sum(ddtr, axis=0, keepdims=True)
        dsm_ref[...] = ddtr
        dxs = d["dse"] * dy + dxdt * d["dte"]
        dxa_ref[...] = jnp.where(ok, jnp.concatenate([dxs] + db_g + dc_g, axis=1), 0.0)

    vec = pl.BlockSpec((1, BLK), lambda c: (0, 0))
    nvec = pl.BlockSpec((1, SSD_D), lambda c: (0, 0))
    return pl.pallas_call(
        body, name="ssd_bwd", grid=(nb,),
        in_specs=[pl.BlockSpec((BLK, SSD_D), lambda c: (rev(c), 0)),
                  pl.BlockSpec((BLK, CONV_D), lambda c: (rev(c), 0)),
                  pl.BlockSpec((BLK, SSD_D), lambda c: (rev(c), C_Z // SSD_D)),
                  pl.BlockSpec((BLK, BLK), lambda c: (rev(c), C_SM // BLK)),
                  pl.BlockSpec((None, SSD_N, SSD_D), lambda c: (rev(c), 0, 0)),
                  vec, vec, vec, nvec],
        out_specs=[pl.BlockSpec((BLK, CONV_D), lambda c: (rev(c), 0)),
                   pl.BlockSpec((BLK, SSD_D), lambda c: (rev(c), 0)),
                   pl.BlockSpec((BLK, BLK), lambda c: (rev(c), 0)),
                   nvec, vec, vec, vec],
        out_shape=[jax.ShapeDtypeStruct((t, CONV_D), F32), jax.ShapeDtypeStruct((t, SSD_D), F32),
                   jax.ShapeDtypeStruct((t, BLK), F32), jax.ShapeDtypeStruct((1, SSD_D), F32),
                   jax.ShapeDtypeStruct((1, BLK), F32), jax.ShapeDtypeStruct((1, BLK), F32),
                   jax.ShapeDtypeStruct((1, BLK), F32)],
        scratch_shapes=[pltpu.VMEM((SSD_N, SSD_D), F32)],
        compiler_params=_cp("arbitrary"),
    )(dmix, xa, proj, proj, sprev, dtb, alog, dskip, normg)


def _attn_scores(q_ref, k_ref, h, dq, scale, mask, bias):
    qh = q_ref[:, dq * h:dq * (h + 1)].astype(BF16)
    kh = k_ref[:, dq * h:dq * (h + 1)].astype(BF16)
    s = _nt(qh, kh) * scale
    if bias is not None:
        s = s + bias
    return qh, kh, jnp.where(mask, s, NEG)


def _segments(nb):
    cuts = sorted({0, nb} | {max(1, round(nb * f)) for f in (0.3, 0.53, 0.77)})
    return list(zip(cuts[:-1], cuts[1:]))


def attn_fwd(q, k, v, qcol, kcol, vcol, nh, dq, dv, scale, c_col=None, c_row=None, lane0=0):
    t = q.shape[0]
    tq = BLK
    use_bias = c_col is not None

    def segment(t0, t1, prev):
        tk = t1 * BLK
        nprev = len(prev)

        def body(*refs):
            refs = refs[nprev:]
            if use_bias:
                q_ref, k_ref, v_ref, cc_ref, cr_ref, o_ref, l_ref = refs
            else:
                q_ref, k_ref, v_ref, o_ref, l_ref = refs
            i = pl.program_id(0)
            rowg = (t0 + i) * tq + lax.broadcasted_iota(jnp.int32, (tq, 1), 0)
            col = lax.broadcasted_iota(jnp.int32, (1, tk), 1)
            mask = (col <= rowg) & (col >= PAD)
            outs = []
            lse = jnp.zeros((tq, BLK), F32)
            for h in range(nh):
                bias = (cc_ref[:, lane0 + h:lane0 + h + 1] - cr_ref[h:h + 1, :]) if use_bias else None
                _, _, s = _attn_scores(q_ref, k_ref, h, dq, scale, mask, bias)
                m = jnp.max(s, axis=1, keepdims=True)
                p = jnp.exp(s - m)
                l = jnp.sum(p, axis=1, keepdims=True)
                vh = v_ref[:, dv * h:dv * (h + 1)].astype(BF16)
                outs.append(_nn(p.astype(BF16), vh) / l)
                lse += _lane_put(m + jnp.log(l), h)
            o_ref[...] = jnp.concatenate(outs, axis=1).astype(BF16)
            l_ref[...] = lse.T[0:8, :]

        in_specs = [_ANY] * nprev + [pl.BlockSpec((tq, nh * dq), lambda i: (t0 + i, qcol)),
                                     pl.BlockSpec((tk, nh * dq), lambda i: (0, kcol)),
                                     pl.BlockSpec((tk, nh * dv), lambda i: (0, vcol))]
        args = list(prev) + [q, k, v]
        if use_bias:
            in_specs += [pl.BlockSpec((tq, BLK), lambda i: (t0 + i, 0)), pl.BlockSpec((8, tk), lambda i: (0, 0))]
            args += [c_col, c_row]
        return pl.pallas_call(
            body, name="attn_fwd", grid=(t1 - t0,),
            in_specs=in_specs,
            out_specs=[pl.BlockSpec((tq, nh * dv), lambda i: (t0 + i, 0)), pl.BlockSpec((8, tq), lambda i: (0, t0 + i))],
            out_shape=[jax.ShapeDtypeStruct((t, nh * dv), BF16), jax.ShapeDtypeStruct((8, t), F32)],
            input_output_aliases={p: p for p in range(nprev)},
            compiler_params=_cp("arbitrary"),
        )(*args)

    outs = []
    for t0, t1 in _segments(t // tq):
        outs = segment(t0, t1, outs)
    return outs


def attn_bwd(q, k, v, do, lse_row, o, qcol, kcol, vcol, docol, ocol, nh, dq, dv, scale, c_col=None, c_row=None):
    t = q.shape[0]
    tq = BLK
    use_bias = c_col is not None

    def segment(t0, t1, prev):
        tk = t1 * BLK
        nprev = len(prev)

        def body(*refs):
            pv, refs = refs[:nprev], refs[nprev:]
            kt = refs[-1]
            if use_bias:
                q_ref, k_ref, v_ref, do_ref, l_ref, o_ref, cc_ref, cr_ref, dq_ref, dk_ref, dv_ref, dcq_ref, dck_ref = refs[:-1]
            else:
                q_ref, k_ref, v_ref, do_ref, l_ref, o_ref, dq_ref, dk_ref, dv_ref = refs[:-1]
            i = pl.program_id(0)

            @pl.when(i == 0)
            def _():
                kt[...] = k_ref[...].astype(BF16).T
                if nprev:
                    dk_ref[...] = pv[1][...]
                    dv_ref[...] = pv[2][...]
                    if use_bias:
                        dck_ref[...] = pv[4][...]
                else:
                    dk_ref[...] = jnp.zeros_like(dk_ref)
                    dv_ref[...] = jnp.zeros_like(dv_ref)
                    if use_bias:
                        dck_ref[...] = jnp.zeros_like(dck_ref)

            key = lax.broadcasted_iota(jnp.int32, (tk, 1), 0)
            qry = (t0 + i) * tq + lax.broadcasted_iota(jnp.int32, (1, tq), 1)
            mask = (key <= qry) & (key >= PAD)
            dot = (do_ref[...].astype(F32) * o_ref[...].astype(F32)).T
            dqts, dcqs = [], []
            for h in range(nh):
                qh = q_ref[:, dq * h:dq * (h + 1)].astype(BF16)
                kh = k_ref[:, dq * h:dq * (h + 1)].astype(BF16)
                vh = v_ref[:, dv * h:dv * (h + 1)].astype(BF16)
                doh = do_ref[:, dv * h:dv * (h + 1)].astype(BF16)
                delta = jnp.sum(dot[dv * h:dv * (h + 1), :], axis=0, keepdims=True)
                st = _nt(kh, qh) * scale
                if use_bias:
                    st = st + (cr_ref[h:h + 1, :] - cc_ref[h])
                pt = jnp.exp(jnp.where(mask, st, NEG) - l_ref[h:h + 1, :])
                dst = pt * (_nt(vh, doh) - delta)
                dsb = dst.astype(BF16)
                dk_ref[:, dq * h:dq * (h + 1)] += _nn(dsb, qh) * scale
                dv_ref[:, dv * h:dv * (h + 1)] += _nn(pt.astype(BF16), doh)
                dqts.append(_nn(kt[dq * h:dq * (h + 1), :], dsb))
                if use_bias:
                    dcqs.append(jnp.sum(dst, axis=0, keepdims=True))
                    dck_ref[h] += dst
            dq_ref[...] = jnp.concatenate(dqts, axis=0).T * scale
            if use_bias:
                dcq_ref[...] = jnp.concatenate(dcqs + [jnp.zeros((8 - nh, tq), F32)], axis=0)

        keys_q = pl.BlockSpec((tk, nh * dq), lambda i: (0, 0))
        keys_v = pl.BlockSpec((tk, nh * dv), lambda i: (0, 0))
        keys_c = pl.BlockSpec((nh, tk, BLK), lambda i: (0, 0, 0))
        qrow = pl.BlockSpec((8, tq), lambda i: (0, t0 + i))
        prev_specs = ([_ANY, keys_q, keys_v] + ([_ANY, keys_c] if use_bias else [])) if nprev else []
        in_specs = prev_specs + [pl.BlockSpec((tq, nh * dq), lambda i: (t0 + i, qcol)),
                                 pl.BlockSpec((tk, nh * dq), lambda i: (0, kcol)),
                                 pl.BlockSpec((tk, nh * dv), lambda i: (0, vcol)),
                                 pl.BlockSpec((tq, nh * dv), lambda i: (t0 + i, docol)),
                                 qrow,
                                 pl.BlockSpec((tq, nh * dv), lambda i: (t0 + i, ocol))]
        args = list(prev) + [q, k, v, do, lse_row, o]
        out_specs = [pl.BlockSpec((tq, nh * dq), lambda i: (t0 + i, 0)), keys_q, keys_v]
        out_shape = [jax.ShapeDtypeStruct((t, nh * dq), F32), jax.ShapeDtypeStruct((t, nh * dq), F32),
                     jax.ShapeDtypeStruct((t, nh * dv), F32)]
        if use_bias:
            in_specs += [keys_c, qrow]
            args += [c_col, c_row]
            out_specs += [qrow, keys_c]
            out_shape += [jax.ShapeDtypeStruct((8, t), F32), jax.ShapeDtypeStruct((nh, t, BLK), F32)]
        return pl.pallas_call(
            body, name="attn_bwd", grid=(t1 - t0,),
            in_specs=in_specs, out_specs=out_specs, out_shape=out_shape,
            scratch_shapes=[pltpu.VMEM((nh * dq, tk), BF16)],
            input_output_aliases={p: p for p in range(nprev)},
            compiler_params=_cp("arbitrary"),
        )(*args)

    outs = []
    for t0, t1 in reversed(_segments(t // tq)):
        outs = segment(t0, t1, outs)
    return outs


def fox_pre(proj, fb):
    t = proj.shape[0]
    nb = t // BLK

    def body(sm_ref, fb_ref, c_ref, cr_ref, cb_ref):
        x = sm_ref[...] + fb_ref[...]
        lane = lax.broadcasted_iota(jnp.int32, (1, BLK), 1)
        keep = _valid_rows(t, 0) & (lane >= SM_F) & (lane < SM_F + FOX_H)
        logf = jnp.where(keep, jnp.minimum(x, 0.0) - jnp.log(1.0 + jnp.exp(-jnp.abs(x))), 0.0)
        tri = _tri().astype(F32)
        carry = jnp.zeros((1, BLK), F32)
        for b in range(nb):
            cb = _nn_hi(tri, logf[b * BLK:(b + 1) * BLK, :]) + carry
            c_ref[b * BLK:(b + 1) * BLK, :] = cb
            carry = cb[BLK - 1:BLK, :]
        cr_ref[...] = c_ref[...].T[SM_F:SM_F + 8, :]
        for h in range(FOX_H):
            cb_ref[h] = jnp.broadcast_to(c_ref[:, SM_F + h:SM_F + h + 1], (t, BLK))

    return pl.pallas_call(
        body, name="fox_pre", grid=(1,),
        in_specs=[pl.BlockSpec((t, BLK), lambda i: (0, C_SM // BLK)), pl.BlockSpec((1, BLK), lambda i: (0, 0))],
        out_specs=[pl.BlockSpec((t, BLK), lambda i: (0, 0)), pl.BlockSpec((8, t), lambda i: (0, 0)),
                   pl.BlockSpec((FOX_H, t, BLK), lambda i: (0, 0, 0))],
        out_shape=[jax.ShapeDtypeStruct((t, BLK), F32), jax.ShapeDtypeStruct((8, t), F32),
                   jax.ShapeDtypeStruct((FOX_H, t, BLK), F32)],
        compiler_params=_cp("arbitrary"),
    )(proj, fb)


def fox_pre_bwd(dcq, dck, proj, fb, dsm_in):
    t = proj.shape[0]
    nb = t // BLK

    def body(dcq_ref, dck_ref, sm_ref, fb_ref, din_ref, dsm_ref, dfb_ref, scr):
        triu = _tri(lower=False).astype(F32)
        carry = jnp.zeros((1, BLK), F32)
        scr[...] = jnp.concatenate([jnp.zeros((SM_F, t), F32), dcq_ref[...], jnp.zeros((BLK - SM_F - 8, t), F32)], axis=0).T
        lane = lax.broadcasted_iota(jnp.int32, (1, BLK), 1)
        for b in range(nb - 1, -1, -1):
            blk = scr[b * BLK:(b + 1) * BLK, :]
            for h in range(FOX_H):
                blk -= jnp.where(lane == SM_F + h, jnp.sum(dck_ref[h, b * BLK:(b + 1) * BLK, :], axis=1, keepdims=True), 0.0)
            cb = _nn_hi(triu, blk) + carry
            scr[b * BLK:(b + 1) * BLK, :] = cb
            carry = cb[0:1, :]
        x = sm_ref[...] + fb_ref[...]
        lane = lax.broadcasted_iota(jnp.int32, (1, BLK), 1)
        keep = _valid_rows(t, 0) & (lane >= SM_F) & (lane < SM_F + FOX_H)
        df = jnp.where(keep, scr[...] * _sig(-x), 0.0)
        dfb_ref[...] = jnp.sum(df, axis=0, keepdims=True)
        dsm_ref[...] = din_ref[...] + df

    full = pl.BlockSpec((t, BLK), lambda i: (0, 0))
    return pl.pallas_call(
        body, name="fox_pre_bwd", grid=(1,),
        in_specs=[pl.BlockSpec((8, t), lambda i: (0, 0)), pl.BlockSpec((FOX_H, t, BLK), lambda i: (0, 0, 0)),
                  pl.BlockSpec((t, BLK), lambda i: (0, C_SM // BLK)), pl.BlockSpec((1, BLK), lambda i: (0, 0)), full],
        out_specs=[full, pl.BlockSpec((1, BLK), lambda i: (0, 0))],
        out_shape=[jax.ShapeDtypeStruct((t, BLK), F32), jax.ShapeDtypeStruct((1, BLK), F32)],
        scratch_shapes=[pltpu.VMEM((t, BLK), F32)],
        compiler_params=_cp("arbitrary"),
    )(dcq, dck, proj, fb, dsm_in)


def _swap_rope(x):
    lane = lax.broadcasted_iota(jnp.int32, (1, BLK), 1)
    return jnp.where((lane >= SM_KR) & (lane < SM_KR + 16), pltpu.roll(x, BLK - 16, 1),
                     jnp.where((lane >= SM_KR + 16) & (lane < SM_KR + 32), pltpu.roll(x, 16, 1), 0.0))


def _rms(x, g):
    r = lax.rsqrt(jnp.mean(x * x, axis=1, keepdims=True) + EPS)
    return r, x * r


def mla_pre(proj, qg, kvg, wq, wk, wv, cosq, sinq):
    t = proj.shape[0]
    tm = _row_tile(t)

    def body(cq_ref, ckv_ref, sm_ref, qg_ref, kvg_ref, wq_ref, wk_ref, wv_ref, cos_ref, sin_ref,
             q_ref, k_ref, v_ref, cqn_ref, ckvn_ref):
        cs, sn = cos_ref[...], sin_ref[...]
        _, xh = _rms(cq_ref[...], None)
        cqn = (xh * qg_ref[...]).astype(BF16)
        cqn_ref[...] = cqn
        qraw = _nn(cqn, wq_ref[...])
        qs = []
        for h in range(MLA_H):
            hb = qraw[:, BLK * h:BLK * (h + 1)]
            qs.append(hb * cs + _swap_rope(hb) * sn)
        q_ref[...] = jnp.concatenate(qs, axis=1).astype(BF16)
        _, kh = _rms(ckv_ref[...], None)
        ckvn = (kh * kvg_ref[...]).astype(BF16)
        ckvn_ref[...] = ckvn
        kraw = _nn(ckvn, wk_ref[...])
        v_ref[...] = _nn(ckvn, wv_ref[...]).astype(BF16)
        lane = lax.broadcasted_iota(jnp.int32, (1, BLK), 1)
        kr = sm_ref[...]
        krr = jnp.where((lane >= SM_KR) & (lane < SM_KR + MLA_ROPE), kr * cs + _swap_rope(kr) * sn, 0.0)
        k_ref[...] = jnp.concatenate([kraw[:, BLK * h:BLK * (h + 1)] + krr for h in range(MLA_H)], axis=1).astype(BF16)

    def rows(w, cb):
        return pl.BlockSpec((tm, w), lambda i: (i, cb))

    def whole(a):
        return pl.BlockSpec(a.shape, lambda i: (0, 0))

    return pl.pallas_call(
        body, name="mla_pre", grid=(t // tm,),
        in_specs=[rows(MLA_QL, C_CQ // MLA_QL), rows(MLA_KVL, C_CKV // MLA_KVL), rows(BLK, C_SM // BLK),
                  whole(qg), whole(kvg), whole(wq), whole(wk), whole(wv), rows(BLK, 0), rows(BLK, 0)],
        out_specs=[rows(512, 0), rows(512, 0), rows(256, 0), rows(MLA_QL, 0), rows(MLA_KVL, 0)],
        out_shape=[jax.ShapeDtypeStruct((t, 512), BF16), jax.ShapeDtypeStruct((t, 512), BF16),
                   jax.ShapeDtypeStruct((t, 256), BF16), jax.ShapeDtypeStruct((t, MLA_QL), BF16),
                   jax.ShapeDtypeStruct((t, MLA_KVL), BF16)],
        compiler_params=_cp("arbitrary"),
    )(proj, proj, proj, qg, kvg, wq, wk, wv, cosq, sinq)


def mla_pre_bwd(dq, dk, dv, proj, cqn, ckvn, qg, kvg, wq, wk, wv, cosq, sinq, dsm_in):
    t = proj.shape[0]
    tm = _row_tile(t)

    def body(dq_ref, dk_ref, dv_ref, cq_ref, ckv_ref, cqn_ref, ckvn_ref, qg_ref, kvg_ref, wq_ref, wk_ref, wv_ref,
             cos_ref, sin_ref, din_ref, dcq_ref, dckv_ref, dsm_ref, dwq_ref, dwk_ref, dwv_ref, dqg_ref, dkvg_ref):
        i = pl.program_id(0)

        @pl.when(i == 0)
        def _():
            for r in (dwq_ref, dwk_ref, dwv_ref, dqg_ref, dkvg_ref):
                r[...] = jnp.zeros_like(r)

        cs, sn = cos_ref[...], sin_ref[...]
        lane = lax.broadcasted_iota(jnp.int32, (1, BLK), 1)

        def unrope(dy):
            return dy * cs + _swap_rope(dy * sn)

        dqp = jnp.concatenate([unrope(dq_ref[:, BLK * h:BLK * (h + 1)]) for h in range(MLA_H)], axis=1).astype(BF16)
        dwq_ref[...] += _tn(cqn_ref[...], dqp)
        dcqn = _nt(dqp, wq_ref[...])
        r, xh = _rms(cq_ref[...], None)
        dqg_ref[...] += jnp.sum(dcqn * xh, axis=0, keepdims=True)
        dxh = dcqn * qg_ref[...]
        dcq_ref[...] = r * (dxh - xh * jnp.mean(dxh * xh, axis=1, keepdims=True))

        dkn, dkr = [], jnp.zeros((tm, BLK), F32)
        for h in range(MLA_H):
            blk = dk_ref[:, BLK * h:BLK * (h + 1)]
            dkn.append(jnp.where(lane < MLA_NOPE, blk, 0.0))
            dkr += jnp.where((lane >= SM_KR) & (lane < SM_KR + MLA_ROPE), blk, 0.0)
        dknb = jnp.concatenate(dkn, axis=1).astype(BF16)
        dvb = dv_ref[...].astype(BF16)
        ckvn = ckvn_ref[...]
        dwk_ref[...] += _tn(ckvn, dknb)
        dwv_ref[...] += _tn(ckvn, dvb)
        dckvn = _nt(dknb, wk_ref[...]) + _nt(dvb, wv_ref[...])
        r2, kh = _rms(ckv_ref[...], None)
        dkvg_ref[...] += jnp.sum(dckvn * kh, axis=0, keepdims=True)
        dkh = dckvn * kvg_ref[...]
        dckv_ref[...] = r2 * (dkh - kh * jnp.mean(dkh * kh, axis=1, keepdims=True))
        dsm_ref[...] = din_ref[...] + jnp.where((lane >= SM_KR) & (lane < SM_KR + MLA_ROPE), unrope(dkr), 0.0)

    def rows(w, cb):
        return pl.BlockSpec((tm, w), lambda i: (i, cb))

    def whole(a):
        return pl.BlockSpec(a.shape, lambda i: (0, 0))

    def wshape(a):
        return jax.ShapeDtypeStruct(a.shape, F32)

    return pl.pallas_call(
        body, name="mla_pre_bwd", grid=(t // tm,),
        in_specs=[rows(512, 0), rows(512, 0), rows(256, 0), rows(MLA_QL, C_CQ // MLA_QL), rows(MLA_KVL, C_CKV // MLA_KVL),
                  rows(MLA_QL, 0), rows(MLA_KVL, 0), whole(qg), whole(kvg), whole(wq), whole(wk), whole(wv),
                  rows(BLK, 0), rows(BLK, 0), rows(BLK, 0)],
        out_specs=[rows(MLA_QL, 0), rows(MLA_KVL, 0), rows(BLK, 0), whole(wq), whole(wk), whole(wv), whole(qg), whole(kvg)],
        out_shape=[jax.ShapeDtypeStruct((t, MLA_QL), F32), jax.ShapeDtypeStruct((t, MLA_KVL), F32),
                   jax.ShapeDtypeStruct((t, BLK), F32), wshape(wq), wshape(wk), wshape(wv), wshape(qg), wshape(kvg)],
        compiler_params=_cp("arbitrary"),
    )(dq, dk, dv, proj, proj, cqn, ckvn, qg, kvg, wq, wk, wv, cosq, sinq, dsm_in)


def _slot_sum(me, own, recv_ref):
    gg = own.astype(F32)
    for s in range(N_DEV):
        gg = gg + jnp.where(me == s, 0.0, recv_ref[s].astype(F32))
    return gg


def adamw(w, m, v, g=None, recv=None, own=None, me_arr=None):
    shape = w.shape
    c = shape[-1]
    from_recv = recv is not None
    if not from_recv:
        me_arr = jnp.zeros((1,), jnp.int32)
    nl = len(recv) if from_recv else 1
    rws = w.size // c // nl
    tr = rws
    for d in (1024, 512, 352, 256, 128, 64, 32, 16, 8):
        if rws % d == 0 and d * c * 4 <= (2 << 20):
            tr = d
            break
    nt = rws // tr
    w2, m2, v2 = (a.reshape(nl, rws, c) for a in (w, m, v))
    if from_recv:
        gin = [a.reshape(N_DEV, rws, c) for a in list(recv) + list(own)]
    else:
        gin = [g.reshape(1, rws, c)]

    def body(me_ref, w_ref, m_ref, v_ref, *rest):
        g_refs, outs = rest[:len(gin)], rest[len(gin):]
        if from_recv:
            g_out, outs = outs[0], outs[1:]
            for li in range(nl):
                @pl.when(pl.program_id(0) == li)
                def _(li=li):
                    g_out[...] = _slot_sum(me_ref[0], g_refs[nl + li][...], g_refs[li])
            gg = g_out[...]
        else:
            gg = g_refs[0][...]
        d_ref, nm_ref, nv_ref = outs
        nm = B1 * m_ref[...] + (1.0 - B1) * gg
        nv = B2 * v_ref[...] + (1.0 - B2) * (gg * gg)
        mh = nm / (1.0 - B1 ** STEP)
        vh = nv / (1.0 - B2 ** STEP)
        d_ref[...] = -LR * (mh / (jnp.sqrt(vh) + AEPS) + WD * w_ref[...])
        nm_ref[...] = nm
        nv_ref[...] = nv

    row = pl.BlockSpec((None, tr, c), lambda l, i, me: (l, i, 0))
    if from_recv:
        gspecs = [pl.BlockSpec((N_DEV, tr, c), lambda l, i, me, li=li: (0, jnp.where(l == li, i, 0), 0))
                  for li in range(nl)]
        gspecs += [pl.BlockSpec((None, tr, c), lambda l, i, me, li=li: (me[0], jnp.where(l == li, i, 0), 0))
                   for li in range(nl)]
    else:
        gspecs = [row]
    nout = 4 if from_recv else 3
    outs = pl.pallas_call(
        body, name="adamw",
        grid_spec=pltpu.PrefetchScalarGridSpec(num_scalar_prefetch=1, grid=(nl, nt), in_specs=[row, row, row] + gspecs,
                                               out_specs=[row] * nout),
        out_shape=[jax.ShapeDtypeStruct((nl, rws, c), F32)] * nout,
        compiler_params=_cp("arbitrary", "arbitrary"),
    )(me_arr, w2, m2, v2, *gin)
    return tuple(o.reshape(shape) for o in outs)


def sum_slots(recv, own=None, me_arr=None):
    _, r, c = recv.shape
    if own is None:
        own, me_arr = recv, jnp.zeros((1,), jnp.int32)
        plain = True
    else:
        plain = False

    def body(me_ref, r_ref, own_ref, o_ref):
        if plain:
            gg = r_ref[0].astype(F32)
            for s in range(1, N_DEV):
                gg = gg + r_ref[s].astype(F32)
            o_ref[...] = gg
        else:
            o_ref[...] = _slot_sum(me_ref[0], own_ref[...], r_ref)

    return pl.pallas_call(
        body, name="sum_slots",
        grid_spec=pltpu.PrefetchScalarGridSpec(
            num_scalar_prefetch=1, grid=(1,),
            in_specs=[pl.BlockSpec((N_DEV, r, c), lambda i, me: (0, 0, 0)),
                      pl.BlockSpec((None, r, c), lambda i, me: (me[0], 0, 0))],
            out_specs=pl.BlockSpec((r, c), lambda i, me: (0, 0))),
        out_shape=jax.ShapeDtypeStruct((r, c), F32),
        compiler_params=_cp("arbitrary"),
    )(me_arr, recv, own)


_FLIPS = [(0, 0, 1), (0, 1, 0), (0, 1, 1), (1, 0, 0), (1, 0, 1), (1, 1, 0), (1, 1, 1)]
_ANY = pl.BlockSpec(memory_space=pl.ANY)


def _mesh_place():
    x, y, c = lax.axis_index("x"), lax.axis_index("y"), lax.axis_index("c")
    me = 4 * x + 2 * y + c
    peers = [((x + fx) % 2, (y + fy) % 2, (c + fc) % 2) for fx, fy, fc in _FLIPS]
    return me, peers


def place_own(src, l, dtype, me_arr):
    _, r, c = src.shape
    tr = r
    for d in (512, 352, 256, 128, 64, 32, 16, 8):
        if r % d == 0 and d * c * 4 <= (2 << 20):
            tr = d
            break

    def body(me_ref, s_ref, o_ref):
        o_ref[...] = s_ref[...].astype(dtype)

    return pl.pallas_call(
        body, name="place_own",
        grid_spec=pltpu.PrefetchScalarGridSpec(
            num_scalar_prefetch=1, grid=(r // tr,),
            in_specs=[pl.BlockSpec((None, tr, c), lambda i, me: (l, i, 0))],
            out_specs=pl.BlockSpec((None, tr, c), lambda i, me: (me[0], i, 0))),
        out_shape=jax.ShapeDtypeStruct((N_DEV, r, c), dtype),
        compiler_params=_cp("arbitrary"),
    )(me_arr, src)


_HBM = pl.BlockSpec(memory_space=pltpu.HBM)
_SEMS = pl.BlockSpec(memory_space=pltpu.SEMAPHORE)
_EFFECT = pltpu.SideEffectType.DATAFLOW_SIDE_EFFECTING


def exchange_start(mode, arrays, name, after=None):
    n = len(arrays)
    gather = mode == "gather"
    ns = 0 if gather else n
    zones = list(arrays) if gather else [lax.empty(a.shape, a.dtype) for a in arrays]
    ops = ([] if gather else list(arrays)) + zones
    extra = [] if after is None else [after]

    def body(*refs):
        srcs, lands = refs[:ns], refs[ns:ns + n]
        send_sems, recv_sems = refs[ns + n + len(extra)], refs[ns + n + len(extra) + 1]
        token = refs[-1]
        me, peers = _mesh_place()
        ids = [4 * p[0] + 2 * p[1] + p[2] for p in peers]
        for j in range(n):
            for k in range(N_DEV - 1):
                src = lands[j].at[me] if gather else srcs[j].at[ids[k]]
                pltpu.make_async_remote_copy(src_ref=src, dst_ref=lands[j].at[me],
                                             send_sem=send_sems.at[j * (N_DEV - 1) + k],
                                             recv_sem=recv_sems.at[j * (N_DEV - 1) + k], device_id=peers[k],
                                             device_id_type=pl.DeviceIdType.MESH).start()
        token[...] = jnp.zeros_like(token)

    nsem = n * (N_DEV - 1)
    res = pl.pallas_call(
        body, name=name,
        in_specs=[_HBM] * (ns + n) + [_ANY] * len(extra),
        out_specs=(_SEMS, _SEMS, *[_HBM] * (ns + n), pl.BlockSpec(memory_space=pltpu.VMEM)),
        out_shape=(pltpu.SemaphoreType.DMA((nsem,)), pltpu.SemaphoreType.DMA((nsem,)),
                   *[pltpu.HBM(a.shape, a.dtype) for a in ops], jax.ShapeDtypeStruct((8, BLK), F32)),
        input_output_aliases={i: 2 + i for i in range(ns + n)},
        compiler_params=pltpu.CompilerParams(has_side_effects=_EFFECT),
    )(*[pltpu.with_memory_space_constraint(a, pltpu.HBM) for a in ops], *extra)
    return dict(gather=gather, send=res[0], recv=res[1], srcs=list(res[2:2 + ns]), lands=list(res[2 + ns:2 + ns + n]),
                token=res[-1])


def exchange_wait(hd, idxs, name, after):
    gather = hd["gather"]
    n = len(idxs)
    ns = 0 if gather else n
    ops = ([] if gather else [hd["srcs"][j] for j in idxs]) + [hd["lands"][j] for j in idxs]

    def body(*refs):
        srcs, lands = refs[:ns], refs[ns:ns + n]
        send_sems, recv_sems = refs[ns + n], refs[ns + n + 1]
        me, peers = _mesh_place()
        ids = [4 * p[0] + 2 * p[1] + p[2] for p in peers]
        for p, j in enumerate(idxs):
            for k in range(N_DEV - 1):
                src = lands[p].at[me] if gather else srcs[p].at[ids[k]]
                cp = pltpu.make_async_remote_copy(src_ref=src, dst_ref=lands[p].at[ids[k]],
                                                  send_sem=send_sems.at[j * (N_DEV - 1) + k],
                                                  recv_sem=recv_sems.at[j * (N_DEV - 1) + k], device_id=peers[k],
                                                  device_id_type=pl.DeviceIdType.MESH)
                cp.wait_send()
                cp.wait_recv()

    res = pl.pallas_call(
        body, name=name,
        in_specs=[_HBM] * (ns + n) + [_SEMS, _SEMS, _ANY],
        out_specs=[_HBM] * (ns + n),
        out_shape=[pltpu.HBM(a.shape, a.dtype) for a in ops],
        input_output_aliases={i: i for i in range(ns + n)},
        compiler_params=pltpu.CompilerParams(has_side_effects=_EFFECT),
    )(*ops, hd["send"], hd["recv"], after)
    return list(res[:ns]), list(res[ns:])


def _chip_place():
    x, y, c = lax.axis_index("x"), lax.axis_index("y"), lax.axis_index("c")
    chips = [((x + 1) % 2, y), (x, (y + 1) % 2), ((x + 1) % 2, (y + 1) % 2)]
    ident = lambda p: 4 * p[0] + 2 * p[1] + p[2]
    return dict(me=4 * x + 2 * y + c, sib=(x, y, 1 - c), sib_id=4 * x + 2 * y + 1 - c,
                same=[(cx, cy, c) for cx, cy in chips], same_ids=[ident((cx, cy, c)) for cx, cy in chips],
                other_ids=[ident((cx, cy, 1 - c)) for cx, cy in chips])


def _remote(src, dst, send_sem, recv_sem, dev):
    return pltpu.make_async_remote_copy(src_ref=src, dst_ref=dst, send_sem=send_sem, recv_sem=recv_sem, device_id=dev,
                                        device_id_type=pl.DeviceIdType.MESH)


def gather_start(zones, name):
    n = len(zones)

    def body(*refs):
        lands, send_sems, recv_sems, token = refs[:n], refs[n], refs[n + 1], refs[-1]
        pc = _chip_place()
        for j in range(n):
            own = lands[j].at[pc["me"]]
            for k, dev in enumerate([pc["sib"]] + pc["same"]):
                _remote(own, own, send_sems.at[4 * j + k], recv_sems.at[4 * j + k], dev).start()
        token[...] = jnp.zeros_like(token)

    res = pl.pallas_call(
        body, name=name,
        in_specs=[_HBM] * n,
        out_specs=(_SEMS, _SEMS, *[_HBM] * n, pl.BlockSpec(memory_space=pltpu.VMEM)),
        out_shape=(pltpu.SemaphoreType.DMA((4 * n,)), pltpu.SemaphoreType.DMA((4 * n,)),
                   *[pltpu.HBM(a.shape, a.dtype) for a in zones], jax.ShapeDtypeStruct((8, BLK), F32)),
        input_output_aliases={i: 2 + i for i in range(n)},
        compiler_params=pltpu.CompilerParams(has_side_effects=_EFFECT),
    )(*[pltpu.with_memory_space_constraint(a, pltpu.HBM) for a in zones])
    return dict(send=res[0], recv=res[1], lands=list(res[2:2 + n]), token=res[-1])


def gather_relay(hd, idxs, name, after):
    n = len(idxs)

    def body(*refs):
        lands, send_sems, recv_sems = refs[:n], refs[n], refs[n + 1]
        fsend, frecv, token = refs[n + 3 + n], refs[n + 4 + n], refs[-1]
        pc = _chip_place()
        for p, j in enumerate(idxs):
            for k in range(3):
                _remote(lands[p].at[pc["me"]], lands[p].at[pc["same_ids"][k]], send_sems.at[4 * j + 1 + k],
                        recv_sems.at[4 * j + 1 + k], pc["same"][k]).wait_recv()
        for p in range(n):
            for k in range(3):
                got = lands[p].at[pc["same_ids"][k]]
                _remote(got, got, fsend.at[3 * p + k], frecv.at[3 * p + k], pc["sib"]).start()
        token[...] = jnp.zeros_like(token)

    ops = [hd["lands"][j] for j in idxs]
    res = pl.pallas_call(
        body, name=name,
        in_specs=[_HBM] * n + [_SEMS, _SEMS, _ANY],
        out_specs=(*[_HBM] * n, _SEMS, _SEMS, pl.BlockSpec(memory_space=pltpu.VMEM)),
        out_shape=(*[pltpu.HBM(a.shape, a.dtype) for a in ops], pltpu.SemaphoreType.DMA((3 * n,)),
                   pltpu.SemaphoreType.DMA((3 * n,)), jax.ShapeDtypeStruct((8, BLK), F32)),
        input_output_aliases={i: i for i in range(n)},
        compiler_params=pltpu.CompilerParams(has_side_effects=_EFFECT),
    )(*ops, hd["send"], hd["recv"], after)
    return dict(lands=list(res[:n]), fsend=res[n], frecv=res[n + 1], token=res[-1])


def gather_wait(hd, rl, idxs, name, after):
    n = len(idxs)

    def body(*refs):
        lands, send_sems, recv_sems, fsend, frecv = refs[:n], refs[n], refs[n + 1], refs[n + 2], refs[n + 3]
        pc = _chip_place()
        for p, j in enumerate(idxs):
            own = lands[p].at[pc["me"]]
            for k, dev in enumerate([pc["sib"]] + pc["same"]):
                _remote(own, own, send_sems.at[4 * j + k], recv_sems.at[4 * j + k], dev).wait_send()
            _remote(own, lands[p].at[pc["sib_id"]], send_sems.at[4 * j], recv_sems.at[4 * j], pc["sib"]).wait_recv()
            for k in range(3):
                cp = _remote(lands[p].at[pc["same_ids"][k]], lands[p].at[pc["other_ids"][k]], fsend.at[3 * p + k],
                             frecv.at[3 * p + k], pc["sib"])
                cp.wait_send()
                cp.wait_recv()

    res = pl.pallas_call(
        body, name=name,
        in_specs=[_HBM] * n + [_SEMS, _SEMS, _SEMS, _SEMS, _ANY],
        out_specs=[_HBM] * n,
        out_shape=[pltpu.HBM(a.shape, a.dtype) for a in rl["lands"]],
        input_output_aliases={i: i for i in range(n)},
        compiler_params=pltpu.CompilerParams(has_side_effects=_EFFECT),
    )(*rl["lands"], hd["send"], hd["recv"], rl["fsend"], rl["frecv"], after)
    return list(res)


def _pad_cols(a, n):
    return jnp.pad(a, ((0, 0),) * (a.ndim - 1) + ((0, n - a.shape[-1]),))


def w_in_to_padded(w):
    z = lambda n: jnp.zeros(w.shape[:-1] + (n,), w.dtype)
    return jnp.concatenate([
        w[..., 0:1280], w[..., 1288:2056], w[..., 2060:2316], w[..., 2316:2444],
        w[..., 1280:1288], w[..., 2056:2060], z(SM_KR - SM_F - FOX_H), w[..., 2444:2476], z(BLK - SM_KR - MLA_ROPE)], axis=-1)


def w_in_from_padded(g):
    s = C_SM
    return jnp.concatenate([
        g[..., 0:1280], g[..., s + SM_DT:s + SM_DT + 8], g[..., 1280:2048], g[..., s + SM_F:s + SM_F + 4],
        g[..., 2048:2304], g[..., 2304:2432], g[..., s + SM_KR:s + SM_KR + MLA_ROPE]], axis=-1)


def _unshard_cols(gth):
    n, r, c = gth.shape
    return jnp.transpose(gth, (1, 0, 2)).reshape(r, n * c)


def _shard_cols(full):
    r, nc = full.shape
    return jnp.transpose(full.reshape(r, N_DEV, nc // N_DEV), (1, 0, 2))


def mla_weights(uq_g, ukv_g):
    uq = _unshard_cols(uq_g)
    dqh = MLA_NOPE + MLA_ROPE
    wq = jnp.concatenate([_pad_cols(uq[:, dqh * h:dqh * (h + 1)], BLK) for h in range(MLA_H)], axis=1)
    wk = jnp.concatenate([_pad_cols(ukv_g[2 * h], BLK) for h in range(MLA_H)], axis=1)
    wv = jnp.concatenate([ukv_g[2 * h + 1] for h in range(MLA_H)], axis=1)
    return wq, wk, wv


def mla_weight_grads(dwq, dwk, dwv):
    dqh = MLA_NOPE + MLA_ROPE
    duq = _shard_cols(jnp.concatenate([dwq[:, BLK * h:BLK * h + dqh] for h in range(MLA_H)], axis=1))
    parts = []
    for h in range(MLA_H):
        parts += [dwk[:, BLK * h:BLK * h + MLA_NOPE], dwv[:, MLA_V * h:MLA_V * (h + 1)]]
    return duq, jnp.stack(parts, axis=0)


def rope_tables(t):
    pos = (jnp.arange(t, dtype=jnp.int32) - PAD).astype(F32)
    inv_freq = 1.0 / (10000.0 ** (jnp.arange(0, MLA_ROPE, 2, dtype=F32) / MLA_ROPE))
    ang = pos[:, None] * inv_freq[None, :]
    cos, sin = jnp.cos(ang), jnp.sin(ang)
    one, zero = jnp.ones((t, SM_KR), F32), jnp.zeros((t, SM_KR), F32)
    tail = BLK - SM_KR - MLA_ROPE
    cosq = jnp.concatenate([one, cos, cos, jnp.ones((t, tail), F32)], axis=1)
    sinq = jnp.concatenate([zero, -sin, sin, jnp.zeros((t, tail), F32)], axis=1)
    return cosq, sinq


def _lanes(v, off=0):
    return jnp.pad(v.astype(F32), (off, BLK - off - v.shape[0]))[None, :]


def layer_fwd(x, ln, hb, getw, tabs, ahead):
    sv = {"h0b": hb}
    W = dict(getw("ffn1", hb))
    ln1 = (W["ln1_g"], W["ln1_b"])
    u, v, r1, h1b = ffn_fwd_seq(x, ln, W["g1"], W["u1"], W["d1"], ln1)
    sv.update(u1=u, v1=v, r1=r1, h1b=h1b)
    W.update(getw("mix", h1b))
    ln2 = (W["ln2_g"], W["ln2_b"])
    proj = mm_nn(h1b, W["w_in"])
    xa = conv_fwd(proj, W["conv_w"], W["conv_b"])
    y_ssd, sprev = ssd_fwd(xa, proj, W["dtb"], W["alog"], W["dskip"], W["normg"])
    c_col, c_row, c_keys = fox_pre(proj, W["fb"])
    y_fox, lse_f = attn_fwd(proj, proj, proj, C_FQ // 256, C_FK // 256, C_FV // 256, FOX_H, FOX_DH, FOX_DH,
                            FOX_DH ** -0.5, c_col, c_row, SM_F)
    ahead(0, "ffn2", y_fox)
    q, k, vv, cqn, ckvn = mla_pre(proj, W["qg"], W["kvg"], W["wq"], W["wk"], W["wv"], *tabs)
    y_mla, lse_m = attn_fwd(q, k, vv, 0, 0, 0, MLA_H, BLK, MLA_V, (MLA_NOPE + MLA_ROPE) ** -0.5)
    mixcat = jnp.concatenate([y_ssd, y_fox, y_mla], axis=1)
    r2, h2b = mm_res_ln(mixcat, W["w_out"], r1, ln1, ln2)
    sv.update(proj=proj, xa=xa, sprev=sprev, c_keys=c_keys, c_row=c_row, lse_f=lse_f, q=q, k=k, v=vv, cqn=cqn, ckvn=ckvn,
              lse_m=lse_m, mixcat=mixcat, r2=r2, h2b=h2b)
    W.update(getw("ffn2", h2b))
    ahead(1, "ffn1", h2b)
    ln3 = (W["ln3_g"], W["ln3_b"])
    u, v, r3, h3b = ffn_fwd_seq(r2, ln2, W["g2"], W["u2"], W["d2"], ln3)
    sv.update(u2=u, v2=v, r3=r3, W=W)
    return r3, ln3, h3b, sv


def ffn_bwd(parts, r, gamma, hb_in, u, v, wg, wu, wd, after=None):
    dh, dwg, dwu, dwd, dg, db = ffn_bwd_seq(parts, r, gamma, hb_in, u, v, wg, wu, wd, after)
    return dh, dict(d=dwd, g=dwg, u=dwu, ln_g=dg, ln_b=db)


def layer_bwd(parts, sv, emit, tabs, after):
    G = {}
    W = sv["W"]
    dh2, g2 = ffn_bwd(parts, sv["r3"], W["ln3_g"], sv["h2b"], sv["u2"], sv["v2"], W["g2"], W["u2"], W["d2"], after)
    G.update(g2=g2["g"], u2=g2["u"], d2=g2["d"], ln3_g=g2["ln_g"], ln3_b=g2["ln_b"])
    tok = emit("ffn2", G)
    dr2, dmixb, G["ln2_g"], G["ln2_b"] = ln_bwd([(dh2, 1.0)], sv["r2"], W["ln2_g"], 1.0, tok)
    dmc = mm_nt_reduce([(dmixb[None], W["w_out"][None])], D)
    G["w_out"] = mm_tn(sv["mixcat"][None], dmixb[None])[0]
    proj = sv["proj"]
    dxa, dz, dsm, G["normg"], G["dskip"], G["alog"], G["dtb"] = ssd_bwd(
        dmc, sv["xa"], proj, sv["sprev"], W["dtb"], W["alog"], W["dskip"], W["normg"])
    dxbc, G["conv_w"], G["conv_b"] = conv_bwd(dxa, proj, W["conv_w"], W["conv_b"])
    dfq, dfk, dfv, dcq, dck = attn_bwd(proj, proj, proj, dmc, sv["lse_f"], sv["mixcat"], C_FQ // 256, C_FK // 256,
                                       C_FV // 256, 2, 2, FOX_H, FOX_DH, FOX_DH, FOX_DH ** -0.5, sv["c_keys"], sv["c_row"])
    dsm, G["fb"] = fox_pre_bwd(dcq, dck, proj, W["fb"], dsm)
    dq, dk, dv = attn_bwd(sv["q"], sv["k"], sv["v"], dmc, sv["lse_m"], sv["mixcat"], 0, 0, 0, 3, 3, MLA_H, BLK, MLA_V,
                          (MLA_NOPE + MLA_ROPE) ** -0.5)
    dcql, dckv, dsm, G["wq"], G["wk"], G["wv"], G["qg"], G["kvg"] = mla_pre_bwd(
        dq, dk, dv, proj, sv["cqn"], sv["ckvn"], W["qg"], W["kvg"], W["wq"], W["wk"], W["wv"], *tabs, dsm)
    dproj = jnp.concatenate([dz, dxbc, dfq, dfk, dfv, dcql, dckv, dsm], axis=1).astype(BF16)
    dh1p = mm_nt_reduce([(dproj[None], W["w_in"][None])], D)
    G["w_in"] = mm_tn(sv["h1b"][None], dproj[None])[0]
    tok = emit("mix", G)
    dh0, g1 = ffn_bwd([(dr2, ALPHA), (dh1p, 1.0)], sv["r1"], W["ln1_g"], sv["h0b"], sv["u1"], sv["v1"],
                      W["g1"], W["u1"], W["d1"], tok)
    G.update(g1=g1["g"], u1=g1["u"], d1=g1["d"], ln1_g=g1["ln_g"], ln1_b=g1["ln_b"])
    tok = emit("ffn1", G)
    return [(dh0, 1.0)], G, tok


def local_step(x, target, meta_full, getw, emit, ahead=lambda l, stage, after: None):
    t = x.shape[0] + BLK
    tabs = rope_tables(t)
    xr, hb = build_h0(meta_full, x)
    ln = None
    saved = []
    for l in range(NL):
        xr, ln, hb, sv = layer_fwd(xr, ln, hb, functools.partial(getw, l), tabs,
                                   lambda dl, stage, after, l=l: ahead(l + dl, stage, after))
        saved.append(sv)
    dy, loss = loss_head(xr, ln, target)
    parts = [(dy, 1.0)]
    grads = [None] * NL
    tok = None
    for l in range(NL - 1, -1, -1):
        parts, grads[l], tok = layer_bwd(parts, saved[l], functools.partial(emit, l), tabs, tok)
    gx, gmeta = split_dh0(parts[0][0], tok)
    return loss, gx, gmeta, grads


_SMALL = ["ln1_g", "ln1_b", "ln2_g", "ln2_b", "ln3_g", "ln3_b", "conv_b", "ssd_norm_g", "mla_q_norm_g",
          "mla_kv_norm_g", "dt_bias", "a_log", "d_skip", "fox_f_b"]
_SMALL_ROWS = 8
_BIG = ["ffn1_w_gate", "ffn1_w_up", "ffn1_w_down", "w_in", "conv_w", "mla_w_uq", "mla_w_ukv", "w_out",
        "ffn2_w_gate", "ffn2_w_up", "ffn2_w_down"]
_NAMES = ["meta", "ffn1_w_gate", "ffn1_w_up", "ffn1_w_down", "ln1_g", "ln1_b", "w_in", "conv_w", "conv_b", "dt_bias",
          "a_log", "d_skip", "ssd_norm_g", "fox_f_b", "mla_q_norm_g", "mla_w_uq", "mla_kv_norm_g", "mla_w_ukv", "w_out",
          "ln2_g", "ln2_b", "ffn2_w_gate", "ffn2_w_up", "ffn2_w_down", "ln3_g", "ln3_b"]


def pack_small(p):
    flat = jnp.concatenate([p[n].astype(F32) for n in _SMALL], axis=1)
    return _pad_cols(flat, _SMALL_ROWS * D).reshape(NL * _SMALL_ROWS, D)


def unpack_small(a, like):
    flat = a.reshape(NL, _SMALL_ROWS * D)
    out, at = {}, 0
    for n in _SMALL:
        out[n] = flat[:, at:at + like[n].shape[1]]
        at += like[n].shape[1]
    return out


_STAGES = {"ffn1": ["ffn1_w_gate", "ffn1_w_up", "ffn1_w_down"],
           "mix": ["w_in", "conv_w", "mla_w_uq", "mla_w_ukv", "w_out"],
           "ffn2": ["ffn2_w_gate", "ffn2_w_up", "ffn2_w_down"]}


_FFN_T = ("ffn1_w_gate", "ffn1_w_up", "ffn2_w_gate", "ffn2_w_up")


def stage_weights(l, stage, g, rep):
    if stage != "mix":
        i = stage[3]
        return {"g" + i: g[f"ffn{i}_w_gate"].reshape(D_FF, D), "u" + i: g[f"ffn{i}_w_up"].reshape(D_FF, D),
                "d" + i: g[f"ffn{i}_w_down"].reshape(D_FF, D),
                "ln1_g" if i == "1" else "ln3_g": rep["ln1_g" if i == "1" else "ln3_g"][l][None, :],
                "ln1_b" if i == "1" else "ln3_b": rep["ln1_b" if i == "1" else "ln3_b"][l][None, :]}
    W = {}
    W["w_in"] = g["w_in"].reshape(D, N_INP)
    W["w_out"] = g["w_out"].reshape(D, D)
    W["wq"], W["wk"], W["wv"] = mla_weights(g["mla_w_uq"], g["mla_w_ukv"])
    W["conv_w"] = _unshard_cols(g["conv_w"])
    for k in ("ln2_g", "ln2_b", "conv_b"):
        W[k] = rep[k][l][None, :]
    W["normg"] = rep["ssd_norm_g"][l][None, :]
    W["qg"] = rep["mla_q_norm_g"][l][None, :]
    W["kvg"] = rep["mla_kv_norm_g"][l][None, :]
    W["dtb"] = _lanes(rep["dt_bias"][l], SM_DT)
    W["alog"] = _lanes(rep["a_log"][l], SM_DT)
    W["dskip"] = _lanes(rep["d_skip"][l], SM_DT)
    W["fb"] = _lanes(rep["fox_f_b"][l], SM_F)
    return W


def small_grads(G):
    return {"ln1_g": G["ln1_g"][0], "ln1_b": G["ln1_b"][0], "ln2_g": G["ln2_g"][0], "ln2_b": G["ln2_b"][0],
            "ln3_g": G["ln3_g"][0], "ln3_b": G["ln3_b"][0], "conv_b": G["conv_b"][0], "ssd_norm_g": G["normg"][0],
            "mla_q_norm_g": G["qg"][0], "mla_kv_norm_g": G["kvg"][0], "dt_bias": G["dtb"][0, :SSD_H],
            "a_log": G["alog"][0, :SSD_H], "d_skip": G["dskip"][0, :SSD_H], "fox_f_b": G["fb"][0, SM_F:SM_F + FOX_H]}


def big_grads(G, stage):
    if stage != "mix":
        i = stage[-1]
        return {f"ffn{i}_w_{k}": G[k[0] + i].reshape(N_DEV, HS, D) for k in ("gate", "up", "down")}
    duq, dukv = mla_weight_grads(G["wq"], G["wk"], G["wv"])
    return {"w_in": G["w_in"].reshape(N_DEV, D // N_DEV, N_INP), "w_out": G["w_out"].reshape(N_DEV, D // N_DEV, D),
            "mla_w_uq": duq, "mla_w_ukv": dukv, "conv_w": _shard_cols(G["conv_w"])}


def kernel(x, meta, ffn1_w_gate, ffn1_w_up, ffn1_w_down, ln1_g, ln1_b, w_in, conv_w, conv_b, dt_bias, a_log, d_skip, ssd_norm_g, fox_f_b, mla_q_norm_g, mla_w_uq, mla_kv_norm_g, mla_w_ukv, w_out, ln2_g, ln2_b, ffn2_w_gate, ffn2_w_up, ffn2_w_down, ln3_g, ln3_b, loss_target, m_meta, m_ffn1_w_gate, m_ffn1_w_up, m_ffn1_w_down, m_ln1_g, m_ln1_b, m_w_in, m_conv_w, m_conv_b, m_dt_bias, m_a_log, m_d_skip, m_ssd_norm_g, m_fox_f_b, m_mla_q_norm_g, m_mla_w_uq, m_mla_kv_norm_g, m_mla_w_ukv, m_w_out, m_ln2_g, m_ln2_b, m_ffn2_w_gate, m_ffn2_w_up, m_ffn2_w_down, m_ln3_g, m_ln3_b, v_meta, v_ffn1_w_gate, v_ffn1_w_up, v_ffn1_w_down, v_ln1_g, v_ln1_b, v_w_in, v_conv_w, v_conv_b, v_dt_bias, v_a_log, v_d_skip, v_ssd_norm_g, v_fox_f_b, v_mla_q_norm_g, v_mla_w_uq, v_mla_kv_norm_g, v_mla_w_ukv, v_w_out, v_ln2_g, v_ln2_b, v_ffn2_w_gate, v_ffn2_w_up, v_ffn2_w_down, v_ln3_g, v_ln3_b):
    vals = (meta, ffn1_w_gate, ffn1_w_up, ffn1_w_down, ln1_g, ln1_b, w_in, conv_w, conv_b, dt_bias, a_log, d_skip, ssd_norm_g, fox_f_b, mla_q_norm_g, mla_w_uq, mla_kv_norm_g, mla_w_ukv, w_out, ln2_g, ln2_b, ffn2_w_gate, ffn2_w_up, ffn2_w_down, ln3_g, ln3_b)
    moms = (m_meta, m_ffn1_w_gate, m_ffn1_w_up, m_ffn1_w_down, m_ln1_g, m_ln1_b, m_w_in, m_conv_w, m_conv_b, m_dt_bias, m_a_log, m_d_skip, m_ssd_norm_g, m_fox_f_b, m_mla_q_norm_g, m_mla_w_uq, m_mla_kv_norm_g, m_mla_w_ukv, m_w_out, m_ln2_g, m_ln2_b, m_ffn2_w_gate, m_ffn2_w_up, m_ffn2_w_down, m_ln3_g, m_ln3_b)
    vars_ = (v_meta, v_ffn1_w_gate, v_ffn1_w_up, v_ffn1_w_down, v_ln1_g, v_ln1_b, v_w_in, v_conv_w, v_conv_b, v_dt_bias, v_a_log, v_d_skip, v_ssd_norm_g, v_fox_f_b, v_mla_q_norm_g, v_mla_w_uq, v_mla_kv_norm_g, v_mla_w_ukv, v_w_out, v_ln2_g, v_ln2_b, v_ffn2_w_gate, v_ffn2_w_up, v_ffn2_w_down, v_ln3_g, v_ln3_b)
    P = dict(zip(_NAMES, vals))
    M = dict(zip(_NAMES, moms))
    V = dict(zip(_NAMES, vars_))
    me = 4 * lax.axis_index("x") + 2 * lax.axis_index("y") + lax.axis_index("c")

    me_arr = me.astype(jnp.int32).reshape(1)
    for n in _FFN_T:
        P[n], M[n], V[n] = (jnp.swapaxes(a[n], 1, 2) for a in (P, M, V))
    src = dict(P)
    src["w_in"] = w_in_to_padded(P["w_in"])
    order = [("meta", 0)] + [(n, l) for l in range(NL) for names in _STAGES.values() for n in names]
    nfirst = 1 + len(_STAGES["ffn1"])

    def place(n, l):
        return place_own(P["meta"][None] if n == "meta" else src[n], l, F32 if n in ("meta", "conv_w") else BF16, me_arr)

    hg_first = gather_start([place(n, l) for n, l in order[:nfirst]], "gather_start_first")
    hg_rest = gather_start([place(n, l) for n, l in order[nfirst:]], "gather_start_rest")
    zone_of = {nl_: ((hg_first, i) if i < nfirst else (hg_rest, i - nfirst)) for i, nl_ in enumerate(order)}
    relays = {}

    def ahead(l, stage, after):
        if l < NL and (l, stage) not in relays:
            zs = [zone_of[("meta", 0)]] if stage == "meta" else [zone_of[(n, l)] for n in _STAGES[stage]]
            hg, idxs = zs[0][0], [i for _, i in zs]
            relays[(l, stage)] = (hg, idxs, gather_relay(hg, idxs, f"gather_relay_{l}_{stage}", after))

    def arrived(l, stage, after):
        ahead(l, stage, after)
        hg, idxs, rl = relays[(l, stage)]
        return gather_wait(hg, rl, idxs, f"gather_wait_{l}_{stage}", after)

    meta_full = _unshard_cols(arrived(0, "meta", hg_rest["token"])[0])

    def getw(l, stage, after):
        return stage_weights(l, stage, dict(zip(_STAGES[stage], arrived(l, stage, after))), P)

    sent = {}

    def emit(l, stage, G):
        bg = big_grads(G, stage)
        sent[(l, stage)] = exchange_start("scatter", [bg[n] for n in _STAGES[stage]], f"scatter_start_{l}_{stage}")
        return sent[(l, stage)]["token"]

    loss, gx, gmeta, grads = local_step(x[0], loss_target[0], meta_full, getw, emit, ahead)

    small = jnp.concatenate([pack_small({n: jnp.stack([small_grads(g)[n] for g in grads]) for n in _SMALL}), gmeta], axis=0)
    hs = exchange_start("gather", [place_own(small[None], 0, F32, me_arr)], "small_start")

    out = {}
    after = hs["token"]
    for stage in ("ffn2", "mix", "ffn1"):
        names = _STAGES[stage]
        got = [exchange_wait(sent[(l, stage)], list(range(len(names))), f"scatter_wait_{l}_{stage}", after)
               for l in range(NL - 1, -1, -1)][::-1]
        for i, n in enumerate(names):
            own = [got[l][0][i] for l in range(NL)]
            recv = [got[l][1][i] for l in range(NL)]
            if n == "w_in":
                g = jnp.stack([w_in_from_padded(sum_slots(recv[l], own[l], me_arr)) for l in range(NL)])
                out[n] = (g,) + adamw(P[n], M[n], V[n], g=g)
            else:
                out[n] = adamw(P[n], M[n], V[n], recv=recv, own=own, me_arr=me_arr)
                if n in _FFN_T:
                    out[n] = tuple(jnp.swapaxes(a, 1, 2) for a in out[n])
        after = out[names[-1]][1]
    gsmall = sum_slots(exchange_wait(hs, [0], "small_wait", after)[1][0])
    gm = lax.dynamic_slice(gsmall[NL * _SMALL_ROWS:], (0, me * (D // N_DEV)), (N_META, D // N_DEV))
    out["meta"] = (gm,) + adamw(P["meta"], M["meta"], V["meta"], g=gm)
    gs = gsmall[:NL * _SMALL_ROWS]
    sd, sm_, sv_ = adamw(pack_small(P), pack_small(M), pack_small(V), g=gs)
    ups = [unpack_small(a, P) for a in (gs, sd, sm_, sv_)]
    for n in _SMALL:
        out[n] = tuple(u[n] for u in ups)

    loss_all = lax.psum(loss[0, 0], ("x", "y", "c"))
    flat = [loss_all, gx[None]]
    for k in range(4):
        flat += [out[n][k] for n in _NAMES]
    return tuple(flat)
```

```python
import functools

import jax
import jax.numpy as jnp
from jax import lax
from jax.experimental import pallas as pl
from jax.experimental.pallas import tpu as pltpu

F32, BF16 = jnp.float32, jnp.bfloat16
HI = lax.Precision.HIGHEST

N_DEV = 8
D = 1024
NL = 2
N_META = 16
BLK = 128
PAD = BLK - N_META
D_FF = 2816
HS = D_FF // N_DEV
SSD_H, SSD_P, SSD_N, SSD_G = 8, 64, 64, 2
SSD_D = SSD_H * SSD_P
CONV_K = 4
CONV_D = SSD_D + 2 * SSD_G * SSD_N
FOX_H, FOX_DH = 4, 64
MLA_H, MLA_QL, MLA_KVL, MLA_NOPE, MLA_ROPE, MLA_V = 4, 256, 128, 64, 32, 64
N_IN = 2476
C_Z, C_XBC, C_FQ, C_FK, C_FV, C_CQ, C_CKV, C_SM, N_INP = 0, 512, 1280, 1536, 1792, 2048, 2304, 2432, 2560
SM_DT, SM_F, SM_KR = 0, 8, 64
ALPHA = (2 * NL) ** 0.25
EPS = 1e-5
NEG = -1e30
LR, B1, B2, AEPS, WD, STEP = 0.001, 0.9, 0.999, 1e-08, 0.01, 10
VMEM_MB = 56


def _cp(*sem):
    return pltpu.CompilerParams(dimension_semantics=sem, vmem_limit_bytes=VMEM_MB << 20)


def _nn(a, b):
    return lax.dot_general(a, b, (((1,), (0,)), ((), ())), preferred_element_type=F32)


def _nt(a, b):
    return lax.dot_general(a, b, (((1,), (1,)), ((), ())), preferred_element_type=F32)


def _tn(a, b):
    return lax.dot_general(a, b, (((0,), (0,)), ((), ())), preferred_element_type=F32)


def _nn_hi(a, b):
    return lax.dot_general(a, b, (((1,), (0,)), ((), ())), precision=HI, preferred_element_type=F32)


def _row_tile(t):
    for d in range(640, 15, -16):
        if t % d == 0:
            return d
    raise ValueError(t)


def _sig(x):
    return 1.0 / (1.0 + jnp.exp(-x))


def _tri(lower=True):
    r = lax.broadcasted_iota(jnp.int32, (BLK, BLK), 0)
    c = lax.broadcasted_iota(jnp.int32, (BLK, BLK), 1)
    return (r >= c) if lower else (r <= c)


def build_h0(meta_full, x):
    s = x.shape[0]
    nb = s // BLK + 1

    def body(m_ref, x_ref, h_ref, hb_ref):
        i = pl.program_id(0)

        @pl.when(i == 0)
        def _():
            h = jnp.concatenate([jnp.zeros((PAD, D), F32), m_ref[...]], axis=0)
            h_ref[...] = h
            hb_ref[...] = h.astype(BF16)

        @pl.when(i > 0)
        def _():
            h_ref[...] = x_ref[...]
            hb_ref[...] = x_ref[...].astype(BF16)

    return pl.pallas_call(
        body, name="build_h0", grid=(nb,),
        in_specs=[pl.BlockSpec((N_META, D), lambda i: (0, 0)),
                  pl.BlockSpec((BLK, D), lambda i: (jnp.maximum(i - 1, 0), 0))],
        out_specs=[pl.BlockSpec((BLK, D), lambda i: (i, 0))] * 2,
        out_shape=[jax.ShapeDtypeStruct((nb * BLK, D), F32), jax.ShapeDtypeStruct((nb * BLK, D), BF16)],
        compiler_params=_cp("arbitrary"),
    )(meta_full, x)


FT = 256


def _layer_norm(r, gamma, beta):
    mu = jnp.mean(r, axis=1, keepdims=True)
    xc = r - mu
    var = jnp.mean(xc * xc, axis=1, keepdims=True)
    return xc * lax.rsqrt(var + EPS) * gamma + beta


def ffn_fwd_seq(x, ln_in, wg, wu, wd, ln_out):
    t = x.shape[0]
    f = wg.shape[0]
    nj, nr = f // FT, t // _row_tile(t)
    rc = t // nr
    plain = ln_in is None
    gi, bi = ln_out if plain else ln_in

    def body(x_hbm, gi_ref, bi_ref, go_ref, bo_ref, wg_ref, wu_ref, wd_ref, u_ref, v_ref, r_hbm, yb_hbm,
             acc, hbs, xbuf, sem_in, sem_out):
        j = pl.program_id(0)

        @pl.when(j == 0)
        def _():
            def fetch(k):
                return pltpu.make_async_copy(x_hbm.at[pl.ds(k * rc, rc)], xbuf.at[k % 2], sem_in.at[k % 2])

            fetch(0).start()
            for k in range(nr):
                if k + 1 < nr:
                    fetch(k + 1).start()
                fetch(k).wait()
                h = xbuf[k % 2]
                if not plain:
                    h = _layer_norm(h, gi_ref[...], bi_ref[...])
                acc[k * rc:(k + 1) * rc, :] = ALPHA * h
                hbs[k * rc:(k + 1) * rc, :] = h.astype(BF16)

        for k in range(nr):
            sl = slice(k * rc, (k + 1) * rc)
            h = hbs[sl, :]
            u = _nt(h, wg_ref[...])
            v = _nt(h, wu_ref[...])
            u_ref[sl, :] = u.astype(BF16)
            v_ref[sl, :] = v.astype(BF16)
            acc[sl, :] += _nn((0.5 * u * _sig(u) * v).astype(BF16), wd_ref[...])

        @pl.when(j == nj - 1)
        def _():
            r_cp = pltpu.make_async_copy(acc, r_hbm, sem_out.at[0])
            r_cp.start()
            for k in range(nr):
                sl = slice(k * rc, (k + 1) * rc)
                hbs[sl, :] = _layer_norm(acc[sl, :], go_ref[...], bo_ref[...]).astype(BF16)
            y_cp = pltpu.make_async_copy(hbs, yb_hbm, sem_out.at[1])
            y_cp.start()
            r_cp.wait()
            y_cp.wait()

    vec = pl.BlockSpec((1, D), lambda j: (0, 0))
    wsp = pl.BlockSpec((FT, D), lambda j: (j, 0))
    act = pl.BlockSpec((None, t, FT), lambda j: (j, 0, 0))
    return pl.pallas_call(
        body, name="ffn_fwd_seq", grid=(nj,),
        in_specs=[_ANY, vec, vec, vec, vec, wsp, wsp, wsp],
        out_specs=[act, act, _ANY, _ANY],
        out_shape=[jax.ShapeDtypeStruct((nj, t, FT), BF16), jax.ShapeDtypeStruct((nj, t, FT), BF16),
                   jax.ShapeDtypeStruct((t, D), F32), jax.ShapeDtypeStruct((t, D), BF16)],
        scratch_shapes=[pltpu.VMEM((t, D), F32), pltpu.VMEM((t, D), BF16), pltpu.VMEM((2, rc, D), F32),
                        pltpu.SemaphoreType.DMA((2,)), pltpu.SemaphoreType.DMA((2,))],
        compiler_params=_cp("arbitrary"),
    )(x, gi, bi, ln_out[0], ln_out[1], wg, wu, wd)


def ffn_bwd_seq(parts, r, gamma, hb, u, v, wg, wu, wd, after=None):
    nj, t, _ = u.shape
    f = nj * FT
    nr = t // _row_tile(t)
    rc = t // nr
    nc = t // BLK
    scales = [s for _, s in parts]
    npart = len(parts)
    extra = [] if after is None else [after]

    def body(*refs):
        refs = refs[len(extra):]
        p_hbm, refs = refs[:npart], refs[npart:]
        (r_hbm, g_ref, hb_hbm, u_ref, v_ref, wg_ref, wu_ref, wd_ref, dh_hbm, dwg_ref, dwu_ref, dwd_ref, dg_ref, db_ref,
         dfs, hbt, dft, dhacc, dus, dvs, acs, pbuf, rbuf, hbuf, sems, sem_out) = refs
        j = pl.program_id(0)

        @pl.when(j == 0)
        def _():
            def fetch(c):
                rows = pl.ds(c * BLK, BLK)
                cps = [pltpu.make_async_copy(p_hbm[p].at[rows], pbuf.at[c % 2, p], sems.at[c % 2, p]) for p in range(npart)]
                cps.append(pltpu.make_async_copy(r_hbm.at[rows], rbuf.at[c % 2], sems.at[c % 2, npart]))
                cps.append(pltpu.make_async_copy(hb_hbm.at[rows], hbuf.at[c % 2], sems.at[c % 2, npart + 1]))
                return cps

            for cp in fetch(0):
                cp.start()
            dg = jnp.zeros((1, D), F32)
            db = jnp.zeros((1, D), F32)
            for c in range(nc):
                if c + 1 < nc:
                    for cp in fetch(c + 1):
                        cp.start()
                for cp in fetch(c):
                    cp.wait()
                sl = slice(c * BLK, (c + 1) * BLK)
                dy = scales[0] * pbuf[c % 2, 0]
                for p in range(1, npart):
                    dy += scales[p] * pbuf[c % 2, p]
                rr = rbuf[c % 2]
                xc = rr - jnp.mean(rr, axis=1, keepdims=True)
                rstd = lax.rsqrt(jnp.mean(xc * xc, axis=1, keepdims=True) + EPS)
                xh = xc * rstd
                dxh = dy * g_ref[...]
                dr = rstd * (dxh - jnp.mean(dxh, axis=1, keepdims=True) - xh * jnp.mean(dxh * xh, axis=1, keepdims=True))
                dg += jnp.sum(dy * xh, axis=0, keepdims=True)
                db += jnp.sum(dy, axis=0, keepdims=True)
                dhacc[sl, :] = ALPHA * dr
                dfc = (0.5 * dr).astype(BF16)
                dfs[sl, :] = dfc
                dft[:, sl] = dfc.T
                hbt[:, sl] = hbuf[c % 2].T
            dg_ref[...] = dg
            db_ref[...] = db

        for k in range(nr):
            sl = slice(k * rc, (k + 1) * rc)
            da = _nt(dfs[sl, :], wd_ref[...])
            uu = u_ref[sl, :].astype(F32)
            vv = v_ref[sl, :].astype(F32)
            sg = _sig(uu)
            du = (da * vv * (sg * (1.0 + uu * (1.0 - sg)))).astype(BF16)
            dv = (da * uu * sg).astype(BF16)
            dus[sl, :] = du
            dvs[sl, :] = dv
            acs[sl, :] = (uu * sg * vv).astype(BF16)
            dhacc[sl, :] += _nn(du, wg_ref[...]) + _nn(dv, wu_ref[...])
        dwg_ref[...] = _nn(hbt[...], dus[...]).astype(BF16).T
        dwu_ref[...] = _nn(hbt[...], dvs[...]).astype(BF16).T
        dwd_ref[...] = _nn(dft[...], acs[...]).astype(BF16).T

        @pl.when(j == nj - 1)
        def _():
            cp = pltpu.make_async_copy(dhacc, dh_hbm, sem_out.at[0])
            cp.start()
            cp.wait()

    vec = pl.BlockSpec((1, D), lambda j: (0, 0))
    wsp = pl.BlockSpec((FT, D), lambda j: (j, 0))
    act = pl.BlockSpec((None, t, FT), lambda j: (j, 0, 0))
    return pl.pallas_call(
        body, name="ffn_bwd_seq", grid=(nj,),
        in_specs=[_ANY] * (len(extra) + npart + 1) + [vec, _ANY, act, act, wsp, wsp, wsp],
        out_specs=[_ANY, wsp, wsp, wsp, vec, vec],
        out_shape=[jax.ShapeDtypeStruct((t, D), F32)] + [jax.ShapeDtypeStruct((f, D), BF16)] * 3
        + [jax.ShapeDtypeStruct((1, D), F32)] * 2,
        scratch_shapes=[pltpu.VMEM((t, D), BF16), pltpu.VMEM((D, t), BF16), pltpu.VMEM((D, t), BF16),
                        pltpu.VMEM((t, D), F32), pltpu.VMEM((t, FT), BF16), pltpu.VMEM((t, FT), BF16),
                        pltpu.VMEM((t, FT), BF16), pltpu.VMEM((2, npart, BLK, D), F32), pltpu.VMEM((2, BLK, D), F32),
                        pltpu.VMEM((2, BLK, D), BF16), pltpu.SemaphoreType.DMA((2, npart + 2)),
                        pltpu.SemaphoreType.DMA((1,))],
        compiler_params=_cp("arbitrary"),
    )(*extra, *[p for p, _ in parts], r, gamma, hb, u, v, wg, wu, wd)


def mm_res_ln(a, b, x, ln_in, ln_out):
    t, k = a.shape
    tm = _row_tile(t)

    def body(a_ref, b_ref, x_ref, gi_ref, bi_ref, go_ref, bo_ref, r_ref, yb_ref):
        r = ALPHA * _layer_norm(x_ref[...], gi_ref[...], bi_ref[...]) + _nn(a_ref[...], b_ref[...])
        r_ref[...] = r
        yb_ref[...] = _layer_norm(r, go_ref[...], bo_ref[...]).astype(BF16)

    row = pl.BlockSpec((tm, D), lambda i: (i, 0))
    vec = pl.BlockSpec((1, D), lambda i: (0, 0))
    return pl.pallas_call(
        body, name="mm_res_ln", grid=(t // tm,),
        in_specs=[pl.BlockSpec((tm, k), lambda i: (i, 0)), pl.BlockSpec((k, D), lambda i: (0, 0)), row, vec, vec, vec, vec],
        out_specs=[row, row],
        out_shape=[jax.ShapeDtypeStruct((t, D), F32), jax.ShapeDtypeStruct((t, D), BF16)],
        compiler_params=_cp("arbitrary"),
    )(a, b, x, ln_in[0], ln_in[1], ln_out[0], ln_out[1])


def mm_nn(a, b):
    t, k = a.shape
    n = tn = b.shape[1]
    tm = _row_tile(t)

    def body(a_ref, b_ref, o_ref):
        o_ref[...] = _nn(a_ref[...], b_ref[...])

    return pl.pallas_call(
        body, name="mm_nn", grid=(t // tm, n // tn),
        in_specs=[pl.BlockSpec((tm, k), lambda i, j: (i, 0)), pl.BlockSpec((k, tn), lambda i, j: (0, j))],
        out_specs=pl.BlockSpec((tm, tn), lambda i, j: (i, j)),
        out_shape=jax.ShapeDtypeStruct((t, n), F32),
        compiler_params=_cp("arbitrary", "arbitrary"),
    )(a, b)


def mm_nt_reduce(pairs, n):
    g, t, _ = pairs[0][0].shape
    tm = _row_tile(t)
    npair = len(pairs)

    def body(*refs):
        o_ref = refs[-1]
        gi = pl.program_id(1)
        tot = _nt(refs[0][...], refs[1][...])
        for p in range(1, npair):
            tot += _nt(refs[2 * p][...], refs[2 * p + 1][...])

        @pl.when(gi == 0)
        def _():
            o_ref[...] = tot

        @pl.when(gi > 0)
        def _():
            o_ref[...] += tot

    in_specs, args = [], []
    for x, w in pairs:
        k = x.shape[2]
        in_specs += [pl.BlockSpec((None, tm, k), lambda i, gi: (gi, i, 0)),
                     pl.BlockSpec((None, n, k), lambda i, gi: (gi, 0, 0))]
        args += [x, w]
    return pl.pallas_call(
        body, name="mm_nt_reduce", grid=(t // tm, g),
        in_specs=in_specs, out_specs=pl.BlockSpec((tm, n), lambda i, gi: (i, 0)),
        out_shape=jax.ShapeDtypeStruct((t, n), F32),
        compiler_params=_cp("arbitrary", "arbitrary"),
    )(*args)


def mm_tn(x, y, out_dtype=BF16):
    gx, t, k = x.shape
    gy, _, n = y.shape
    g = max(gx, gy)
    tm = _row_tile(t)
    nt = t // tm

    def body(x_ref, y_ref, o_ref, acc):
        i = pl.program_id(1)

        @pl.when(i == 0)
        def _():
            acc[...] = jnp.zeros_like(acc)

        acc[...] += _tn(x_ref[...], y_ref[...])

        @pl.when(i == nt - 1)
        def _():
            o_ref[...] = acc[...].astype(out_dtype)

    return pl.pallas_call(
        body, name="mm_tn", grid=(g, nt),
        in_specs=[pl.BlockSpec((None, tm, k), (lambda gi, i: (gi, i, 0)) if gx > 1 else (lambda gi, i: (0, i, 0))),
                  pl.BlockSpec((None, tm, n), (lambda gi, i: (gi, i, 0)) if gy > 1 else (lambda gi, i: (0, i, 0)))],
        out_specs=pl.BlockSpec((None, k, n), lambda gi, i: (gi, 0, 0)),
        out_shape=jax.ShapeDtypeStruct((g, k, n), out_dtype),
        scratch_shapes=[pltpu.VMEM((k, n), F32)],
        compiler_params=_cp("arbitrary", "arbitrary"),
    )(x, y)


def ln_bwd(parts, r, gamma, out_scale, after=None):
    t = r.shape[0]
    tm = _row_tile(t)
    scales = [s for _, s in parts]
    npart = len(parts)
    extra = [] if after is None else [after]

    def body(*refs):
        refs = refs[len(extra):]
        r_ref, g_ref = refs[npart], refs[npart + 1]
        dr_ref, drb_ref, dg_ref, db_ref = refs[npart + 2:]
        i = pl.program_id(0)
        dy = scales[0] * refs[0][...]
        for p in range(1, npart):
            dy += scales[p] * refs[p][...]
        rr = r_ref[...]
        mu = jnp.mean(rr, axis=1, keepdims=True)
        xc = rr - mu
        rstd = lax.rsqrt(jnp.mean(xc * xc, axis=1, keepdims=True) + EPS)
        xh = xc * rstd
        dxh = dy * g_ref[...]
        m1 = jnp.mean(dxh, axis=1, keepdims=True)
        m2 = jnp.mean(dxh * xh, axis=1, keepdims=True)
        dr = rstd * (dxh - m1 - xh * m2)
        dr_ref[...] = dr
        drb_ref[...] = (out_scale * dr).astype(BF16)
        dg = jnp.sum(dy * xh, axis=0, keepdims=True)
        db = jnp.sum(dy, axis=0, keepdims=True)

        @pl.when(i == 0)
        def _():
            dg_ref[...] = dg
            db_ref[...] = db

        @pl.when(i > 0)
        def _():
            dg_ref[...] += dg
            db_ref[...] += db

    row = pl.BlockSpec((tm, D), lambda i: (i, 0))
    vec = pl.BlockSpec((1, D), lambda i: (0, 0))
    return pl.pallas_call(
        body, name="ln_bwd", grid=(t // tm,),
        in_specs=[_ANY] * len(extra) + [row] * (npart + 1) + [vec],
        out_specs=[row, row, vec, vec],
        out_shape=[jax.ShapeDtypeStruct((t, D), F32), jax.ShapeDtypeStruct((t, D), BF16),
                   jax.ShapeDtypeStruct((1, D), F32), jax.ShapeDtypeStruct((1, D), F32)],
        compiler_params=_cp("arbitrary"),
    )(*extra, *[p for p, _ in parts], r, gamma)


def loss_head(r, ln, target):
    t = r.shape[0]
    nb = t // BLK

    def body(r_ref, g_ref, b_ref, t_ref, dy_ref, l_ref):
        i = pl.program_id(0)

        @pl.when(i == 0)
        def _():
            dy_ref[...] = jnp.zeros_like(dy_ref)
            l_ref[...] = jnp.zeros_like(l_ref)

        @pl.when(i > 0)
        def _():
            err = _layer_norm(r_ref[...], g_ref[...], b_ref[...]) - t_ref[...]
            dy_ref[...] = err * (1.0 / D)
            l_ref[...] += (0.5 / D) * jnp.sum(err * err, keepdims=True)

    vec = pl.BlockSpec((1, D), lambda i: (0, 0))
    return pl.pallas_call(
        body, name="loss_head", grid=(nb,),
        in_specs=[pl.BlockSpec((BLK, D), lambda i: (i, 0)), vec, vec,
                  pl.BlockSpec((BLK, D), lambda i: (jnp.maximum(i - 1, 0), 0))],
        out_specs=[pl.BlockSpec((BLK, D), lambda i: (i, 0)), pl.BlockSpec((1, 1), lambda i: (0, 0))],
        out_shape=[jax.ShapeDtypeStruct((t, D), F32), jax.ShapeDtypeStruct((1, 1), F32)],
        compiler_params=_cp("arbitrary"),
    )(r, ln[0], ln[1], target)


def split_dh0(dh0, after=None):
    t = dh0.shape[0]
    nb = t // BLK
    extra = [] if after is None else [after]

    def body(*refs):
        a_ref, gx_ref, gm_ref = refs[len(extra):]
        i = pl.program_id(0)
        tot = a_ref[...]

        @pl.when(i == 0)
        def _():
            gm_ref[...] = tot[PAD:, :]

        @pl.when(i > 0)
        def _():
            gx_ref[...] = tot

    blk = pl.BlockSpec((BLK, D), lambda i: (i, 0))
    return pl.pallas_call(
        body, name="split_dh0", grid=(nb,),
        in_specs=[_ANY] * len(extra) + [blk],
        out_specs=[pl.BlockSpec((BLK, D), lambda i: (jnp.maximum(i - 1, 0), 0)),
                   pl.BlockSpec((N_META, D), lambda i: (0, 0))],
        out_shape=[jax.ShapeDtypeStruct((t - BLK, D), F32), jax.ShapeDtypeStruct((N_META, D), F32)],
        compiler_params=_cp("arbitrary"),
    )(*extra, dh0)


def _valid_rows(nrows, first_row):
    return (first_row + lax.broadcasted_iota(jnp.int32, (nrows, 1), 0)) >= PAD


def conv_fwd(proj, conv_w, conv_b):
    t = proj.shape[0]
    c0 = C_XBC // BLK

    def body(x_ref, w_ref, b_ref, o_ref):
        ok = _valid_rows(t, 0)
        x = jnp.where(ok, x_ref[...], 0.0)
        w = w_ref[...]
        acc = b_ref[...] + w[CONV_K - 1:CONV_K, :] * x
        for s in range(1, CONV_K):
            acc += w[CONV_K - 1 - s:CONV_K - s, :] * pltpu.roll(x, s, 0)
        o_ref[...] = jnp.where(ok, acc * _sig(acc), 0.0)

    return pl.pallas_call(
        body, name="conv_fwd", grid=(CONV_D // BLK,),
        in_specs=[pl.BlockSpec((t, BLK), lambda j: (0, c0 + j)),
                  pl.BlockSpec((CONV_K, BLK), lambda j: (0, j)), pl.BlockSpec((1, BLK), lambda j: (0, j))],
        out_specs=pl.BlockSpec((t, BLK), lambda j: (0, j)),
        out_shape=jax.ShapeDtypeStruct((t, CONV_D), F32),
        compiler_params=_cp("arbitrary"),
    )(proj, conv_w, conv_b)


def conv_bwd(dxa, proj, conv_w, conv_b):
    t = proj.shape[0]
    c0 = C_XBC // BLK

    def body(d_ref, x_ref, w_ref, b_ref, dx_ref, dw_ref, db_ref):
        ok = _valid_rows(t, 0)
        x = jnp.where(ok, x_ref[...], 0.0)
        w = w_ref[...]
        xs = [x] + [pltpu.roll(x, s, 0) for s in range(1, CONV_K)]
        acc = b_ref[...] + w[CONV_K - 1:CONV_K, :] * x
        for s in range(1, CONV_K):
            acc += w[CONV_K - 1 - s:CONV_K - s, :] * xs[s]
        sg = _sig(acc)
        dxc = jnp.where(ok, d_ref[...] * (sg * (1.0 + acc * (1.0 - sg))), 0.0)
        db_ref[...] = jnp.sum(dxc, axis=0, keepdims=True)
        dw_ref[...] = jnp.concatenate(
            [jnp.sum(dxc * xs[CONV_K - 1 - k], axis=0, keepdims=True) for k in range(CONV_K)], axis=0)
        dx = w[CONV_K - 1:CONV_K, :] * dxc
        for s in range(1, CONV_K):
            dx += w[CONV_K - 1 - s:CONV_K - s, :] * pltpu.roll(dxc, t - s, 0)
        dx_ref[...] = jnp.where(ok, dx, 0.0)

    col = pl.BlockSpec((t, BLK), lambda j: (0, j))
    return pl.pallas_call(
        body, name="conv_bwd", grid=(CONV_D // BLK,),
        in_specs=[col, pl.BlockSpec((t, BLK), lambda j: (0, c0 + j)),
                  pl.BlockSpec((CONV_K, BLK), lambda j: (0, j)), pl.BlockSpec((1, BLK), lambda j: (0, j))],
        out_specs=[col, pl.BlockSpec((CONV_K, BLK), lambda j: (0, j)), pl.BlockSpec((1, BLK), lambda j: (0, j))],
        out_shape=[jax.ShapeDtypeStruct((t, CONV_D), F32), jax.ShapeDtypeStruct((CONV_K, CONV_D), F32),
                   jax.ShapeDtypeStruct((1, CONV_D), F32)],
        compiler_params=_cp("arbitrary"),
    )(dxa, proj, conv_w, conv_b)


def _softplus(x):
    return jnp.maximum(x, 0.0) + jnp.log(1.0 + jnp.exp(-jnp.abs(x)))


GW = SSD_D // SSD_G
HPG = SSD_H // SSD_G


def _head_expand():
    r = lax.broadcasted_iota(jnp.int32, (BLK, SSD_D), 0)
    c = lax.broadcasted_iota(jnp.int32, (BLK, SSD_D), 1)
    rt = lax.broadcasted_iota(jnp.int32, (SSD_D, BLK), 0)
    ct = lax.broadcasted_iota(jnp.int32, (SSD_D, BLK), 1)
    return (c // SSD_P == r).astype(F32), (rt // SSD_P == ct).astype(F32)


def _ssd_chunk(xa, sm, dtb, alog, dskip, ok, sp):
    e, et = _head_expand()
    dt = jnp.where(ok, _softplus(sm + dtb), 0.0)
    amat = -jnp.exp(alog)
    tri = _tri()
    ac = _nn_hi(tri.astype(F32), dt * amat)
    act = ac.T
    ace, dte, dse = _nn_hi(ac, e), _nn_hi(dt, e), _nn_hi(dskip, e)
    laste = ace[BLK - 1:BLK, :]
    ee, dece, gle = jnp.exp(ace), jnp.exp(laste - ace), jnp.exp(laste)
    xs = xa[:, :SSD_D]
    xdt = xs * dte
    decx = dece * xdt
    xdtb = xdt.astype(BF16)
    d = dict(e=e, et=et, dt=dt, amat=amat, tri=tri, ac=ac, act=act, dte=dte, dse=dse, ee=ee, dece=dece, gle=gle, xs=xs,
             xdt=xdt, xdtb=xdtb, decx=decx, bg=[], cg=[], cb=[], yo=[], seg=[], m=[], new_s=[])
    ys = []
    for g in range(SSD_G):
        cols = slice(GW * g, GW * (g + 1))
        bg = xa[:, SSD_D + SSD_N * g:SSD_D + SSD_N * (g + 1)].astype(BF16)
        cg = xa[:, SSD_D + SSD_G * SSD_N + SSD_N * g:SSD_D + SSD_G * SSD_N + SSD_N * (g + 1)].astype(BF16)
        spg = sp[:, cols]
        sloc = _tn(bg, decx[:, cols].astype(BF16))
        yo = _nn(cg, spg.astype(BF16)) * ee[:, cols]
        cb = _nt(cg, bg)
        d["new_s"].append(gle[:, cols] * spg + sloc)
        yds = []
        for h in range(HPG * g, HPG * (g + 1)):
            seg = jnp.where(tri, jnp.exp(jnp.minimum(ac[:, h:h + 1] - act[h:h + 1, :], 0.0)), 0.0)
            m = cb * seg
            yds.append(_nn(m.astype(BF16), xdtb[:, SSD_P * h:SSD_P * (h + 1)]))
            d["seg"].append(seg)
            d["m"].append(m)
        ys.append(jnp.concatenate(yds, axis=1) + yo)
        for k, val in (("bg", bg), ("cg", cg), ("cb", cb), ("yo", yo)):
            d[k].append(val)
    d["y"] = jnp.concatenate(ys, axis=1) + dse * xs
    return d


def ssd_fwd(xa, proj, dtb, alog, dskip, normg):
    t = xa.shape[0]
    nb = t // BLK
    gw = SSD_D // SSD_G

    def body(xa_ref, z_ref, sm_ref, dtb_ref, al_ref, ds_ref, ng_ref, y_ref, sp_ref, st):
        c = pl.program_id(0)

        @pl.when(c == 0)
        def _():
            st[...] = jnp.zeros_like(st)

        ok = _valid_rows(BLK, c * BLK)
        sp = st[...]
        sp_ref[...] = sp
        d = _ssd_chunk(xa_ref[...], sm_ref[...], dtb_ref[...], al_ref[...], ds_ref[...], ok, sp)
        st[...] = jnp.concatenate(d["new_s"], axis=1)
        y = d["y"]
        z = z_ref[...]
        yg = y * (z * _sig(z))
        outs = []
        for g in range(SSD_G):
            v = yg[:, gw * g:gw * (g + 1)]
            outs.append(v * lax.rsqrt(jnp.mean(v * v, axis=1, keepdims=True) + EPS))
        y_ref[...] = (jnp.concatenate(outs, axis=1) * ng_ref[...]).astype(BF16)

    vec = pl.BlockSpec((1, BLK), lambda c: (0, 0))
    return pl.pallas_call(
        body, name="ssd_fwd", grid=(nb,),
        in_specs=[pl.BlockSpec((BLK, CONV_D), lambda c: (c, 0)),
                  pl.BlockSpec((BLK, SSD_D), lambda c: (c, C_Z // SSD_D)),
                  pl.BlockSpec((BLK, BLK), lambda c: (c, C_SM // BLK)),
                  vec, vec, vec, pl.BlockSpec((1, SSD_D), lambda c: (0, 0))],
        out_specs=[pl.BlockSpec((BLK, SSD_D), lambda c: (c, 0)),
                   pl.BlockSpec((None, SSD_N, SSD_D), lambda c: (c, 0, 0))],
        out_shape=[jax.ShapeDtypeStruct((t, SSD_D), BF16), jax.ShapeDtypeStruct((nb, SSD_N, SSD_D), F32)],
        scratch_shapes=[pltpu.VMEM((SSD_N, SSD_D), F32)],
        compiler_params=_cp("arbitrary"),
    )(xa, proj, proj, dtb, alog, dskip, normg)


def _lane_put(col, lane):
    li = lax.broadcasted_iota(jnp.int32, (col.shape[0], BLK), 1)
    return jnp.where(li == lane, col, 0.0)


def ssd_bwd(dmix, xa, proj, sprev, dtb, alog, dskip, normg):
    t = xa.shape[0]
    nb = t // BLK
    gw = SSD_D // SSD_G
    rev = lambda c: nb - 1 - c

    def body(dy_ref, xa_ref, z_ref, sm_ref, sp_ref, dtb_ref, al_ref, ds_ref, ng_ref,
             dxa_ref, dz_ref, dsm_ref, dng_ref, dds_ref, dal_ref, ddtb_ref, dst):
        c = pl.program_id(0)

        @pl.when(c == 0)
        def _():
            dst[...] = jnp.zeros_like(dst)
            dng_ref[...] = jnp.zeros_like(dng_ref)
            dds_ref[...] = jnp.zeros_like(dds_ref)
            dal_ref[...] = jnp.zeros_like(dal_ref)
            ddtb_ref[...] = jnp.zeros_like(ddtb_ref)

        ok = _valid_rows(BLK, rev(c) * BLK)
        sm = sm_ref[...]
        sp = sp_ref[...]
        d = _ssd_chunk(xa_ref[...], sm, dtb_ref[...], al_ref[...], ds_ref[...], ok, sp)
        dt, amat, ac, act, tri, et, xs, xdt = (d[k] for k in ("dt", "amat", "ac", "act", "tri", "et", "xs", "xdt"))
        rowi = lax.broadcasted_iota(jnp.int32, (BLK, 1), 0)
        y = d["y"]
        z = z_ref[...]
        sgz = _sig(z)
        siluz = z * sgz
        yg = y * siluz
        dout = dy_ref[...]
        ng = ng_ref[...]
        dygs, xhs = [], []
        for g in range(SSD_G):
            v = yg[:, gw * g:gw * (g + 1)]
            rr = lax.rsqrt(jnp.mean(v * v, axis=1, keepdims=True) + EPS)
            xh = v * rr
            dxh = dout[:, gw * g:gw * (g + 1)] * ng[:, gw * g:gw * (g + 1)]
            dygs.append(rr * (dxh - xh * jnp.mean(dxh * xh, axis=1, keepdims=True)))
            xhs.append(xh)
        dyg = jnp.concatenate(dygs, axis=1)
        dng_ref[...] += jnp.sum(dout * jnp.concatenate(xhs, axis=1), axis=0, keepdims=True)
        dy = dyg * siluz
        dz_ref[...] = dyg * y * (sgz * (1.0 + z * (1.0 - sgz)))

        triu = _tri(lower=False)
        dyb = dy.astype(BF16)
        dsn = dst[...]
        dds_ref[...] += _nn_hi(jnp.sum(dy * xs, axis=0, keepdims=True), et)
        dac_all = _nn_hi(dy * jnp.concatenate(d["yo"], axis=1), et)
        dyo = (dy * d["ee"]).astype(BF16)
        gl = jnp.exp(ac[BLK - 1:BLK, :])
        dlast = _nn_hi(jnp.sum(dsn * sp, axis=0, keepdims=True), et) * gl
        bds, db_g, dc_g, dxdt_i, new_dst = [], [], [], [], []
        for g in range(SSD_G):
            cols = slice(GW * g, GW * (g + 1))
            bg, cg = d["bg"][g], d["cg"][g]
            dsng = dsn[:, cols].astype(BF16)
            dc = _nt(dyo[:, cols], sp[:, cols].astype(BF16))
            new_dst.append(_tn(cg, dyo[:, cols]) + d["gle"][:, cols] * dsn[:, cols])
            bds.append(_nn(bg, dsng))
            db = _nt(d["decx"][:, cols].astype(BF16), dsng)
            cbt = _nt(bg, cg)
            dcb = jnp.zeros((BLK, BLK), F32)
            for h in range(HPG * g, HPG * (g + 1)):
                hc = slice(SSD_P * h, SSD_P * (h + 1))
                dm = _nt(dyb[:, hc], d["xdtb"][:, hc])
                dcb += dm * d["seg"][h]
                w = dm * d["m"][h]
                dac_all += _lane_put(jnp.sum(w, axis=1, keepdims=True) - jnp.sum(w.T, axis=1, keepdims=True), h)
                segt = jnp.where(triu, jnp.exp(jnp.minimum(act[h:h + 1, :] - ac[:, h:h + 1], 0.0)), 0.0)
                dxdt_i.append(_nn((cbt * segt).astype(BF16), dyb[:, hc]))
            dcbb = dcb.astype(BF16)
            dc_g.append(dc + _nn(dcbb, bg))
            db_g.append(db + _tn(dcbb, cg))
        dst[...] = jnp.concatenate(new_dst, axis=1)
        bds = jnp.concatenate(bds, axis=1)
        tdec = jnp.exp(ac[BLK - 1:BLK, :] - ac) * _nn_hi(xdt * bds, et)
        dlast += jnp.sum(tdec, axis=0, keepdims=True)
        dac_all += jnp.where(rowi == BLK - 1, dlast, 0.0) - tdec
        dxdt = d["dece"] * bds + jnp.concatenate(dxdt_i, axis=1)
        da = _nn_hi(triu.astype(F32), dac_all)
        ddt = _nn_hi(dxdt * xs, et) + da * amat
        dal_ref[...] += jnp.sum(da * dt, axis=0, keepdims=True) * amat
        ddtr = jnp.where(ok, ddt * _sig(sm + dtb_ref[...]), 0.0)
        ddtb_ref[...] += jnp.sum(ddtr, axis=0, keepdims=True)
        dsm_ref[...] = ddtr
        dxs = d["dse"] * dy + dxdt * d["dte"]
        dxa_ref[...] = jnp.where(ok, jnp.concatenate([dxs] + db_g + dc_g, axis=1), 0.0)

    vec = pl.BlockSpec((1, BLK), lambda c: (0, 0))
    nvec = pl.BlockSpec((1, SSD_D), lambda c: (0, 0))
    return pl.pallas_call(
        body, name="ssd_bwd", grid=(nb,),
        in_specs=[pl.BlockSpec((BLK, SSD_D), lambda c: (rev(c), 0)),
                  pl.BlockSpec((BLK, CONV_D), lambda c: (rev(c), 0)),
                  pl.BlockSpec((BLK, SSD_D), lambda c: (rev(c), C_Z // SSD_D)),
                  pl.BlockSpec((BLK, BLK), lambda c: (rev(c), C_SM // BLK)),
                  pl.BlockSpec((None, SSD_N, SSD_D), lambda c: (rev(c), 0, 0)),
                  vec, vec, vec, nvec],
        out_specs=[pl.BlockSpec((BLK, CONV_D), lambda c: (rev(c), 0)),
                   pl.BlockSpec((BLK, SSD_D), lambda c: (rev(c), 0)),
                   pl.BlockSpec((BLK, BLK), lambda c: (rev(c), 0)),
                   nvec, vec, vec, vec],
        out_shape=[jax.ShapeDtypeStruct((t, CONV_D), F32), jax.ShapeDtypeStruct((t, SSD_D), F32),
                   jax.ShapeDtypeStruct((t, BLK), F32), jax.ShapeDtypeStruct((1, SSD_D), F32),
                   jax.ShapeDtypeStruct((1, BLK), F32), jax.ShapeDtypeStruct((1, BLK), F32),
                   jax.ShapeDtypeStruct((1, BLK), F32)],
        scratch_shapes=[pltpu.VMEM((SSD_N, SSD_D), F32)],
        compiler_params=_cp("arbitrary"),
    )(dmix, xa, proj, proj, sprev, dtb, alog, dskip, normg)


def _segments(nb):
    cuts = sorted({0, nb} | {max(1, round(nb * f)) for f in (0.3, 0.53, 0.77)})
    return list(zip(cuts[:-1], cuts[1:]))


def attn_fwd(q, k, v, qcol, kcol, vcol, nh, dq, dv, scale, c_col=None, c_row=None, lane0=0):
    t = q.shape[0]
    tq = BLK
    use_bias = c_col is not None

    def body(*refs):
        if use_bias:
            q_ref, k_ref, v_ref, cc_ref, cr_ref, o_ref, l_ref = refs
        else:
            q_ref, k_ref, v_ref, o_ref, l_ref = refs
        i = pl.program_id(0)
        rowg = i * tq + lax.broadcasted_iota(jnp.int32, (tq, 1), 0)

        def tile(tk):
            col = lax.broadcasted_iota(jnp.int32, (1, tk), 1)
            mask = (col <= rowg) & (col >= PAD)
            outs = []
            lse = jnp.zeros((tq, BLK), F32)
            for h in range(nh):
                s = _nt(q_ref[:, dq * h:dq * (h + 1)].astype(BF16), k_ref[0:tk, dq * h:dq * (h + 1)].astype(BF16)) * scale
                if use_bias:
                    s = s + (cc_ref[:, lane0 + h:lane0 + h + 1] - cr_ref[h:h + 1, 0:tk])
                s = jnp.where(mask, s, NEG)
                m = jnp.max(s, axis=1, keepdims=True)
                p = jnp.exp(s - m)
                l = jnp.sum(p, axis=1, keepdims=True)
                outs.append(_nn(p.astype(BF16), v_ref[0:tk, dv * h:dv * (h + 1)].astype(BF16)) / l)
                lse += _lane_put(m + jnp.log(l), h)
            o_ref[...] = jnp.concatenate(outs, axis=1).astype(BF16)
            l_ref[...] = lse.T[0:8, :]

        for t0, t1 in _segments(t // tq):
            pl.when((i >= t0) & (i < t1))(functools.partial(tile, t1 * BLK))

    in_specs = [pl.BlockSpec((tq, nh * dq), lambda i: (i, qcol)),
                pl.BlockSpec((t, nh * dq), lambda i: (0, kcol)),
                pl.BlockSpec((t, nh * dv), lambda i: (0, vcol))]
    args = [q, k, v]
    if use_bias:
        in_specs += [pl.BlockSpec((tq, BLK), lambda i: (i, 0)), pl.BlockSpec((8, t), lambda i: (0, 0))]
        args += [c_col, c_row]
    return pl.pallas_call(
        body, name="attn_fwd", grid=(t // tq,),
        in_specs=in_specs,
        out_specs=[pl.BlockSpec((tq, nh * dv), lambda i: (i, 0)), pl.BlockSpec((8, tq), lambda i: (0, i))],
        out_shape=[jax.ShapeDtypeStruct((t, nh * dv), BF16), jax.ShapeDtypeStruct((8, t), F32)],
        compiler_params=_cp("arbitrary"),
    )(*args)


def attn_bwd(q, k, v, do, lse_row, o, qcol, kcol, vcol, docol, ocol, nh, dq, dv, scale, c_col=None, c_row=None, lane0=0):
    t = q.shape[0]
    tq = BLK
    use_bias = c_col is not None
    nq = t // tq

    def body(*refs):
        if use_bias:
            (q_ref, k_ref, v_ref, do_ref, l_ref, o_ref, cc_ref, cr_ref, dq_ref, dk_ref, dv_ref, dcq_ref, dck_ref,
             kt, ckb, dacc) = refs
        else:
            q_ref, k_ref, v_ref, do_ref, l_ref, o_ref, dq_ref, dk_ref, dv_ref, kt = refs
        i = pl.program_id(0)

        @pl.when(i == 0)
        def _():
            kt[...] = k_ref[...].astype(BF16).T
            dk_ref[...] = jnp.zeros_like(dk_ref)
            dv_ref[...] = jnp.zeros_like(dv_ref)
            if use_bias:
                dacc[...] = jnp.zeros_like(dacc)
                for h in range(nh):
                    ckb[h] = jnp.broadcast_to(cc_ref[:, lane0 + h:lane0 + h + 1], (t, BLK))

        qry = i * tq + lax.broadcasted_iota(jnp.int32, (1, tq), 1)
        dot = (do_ref[...].astype(F32) * o_ref[...].astype(F32)).T

        def tile(tk):
            key = lax.broadcasted_iota(jnp.int32, (tk, 1), 0)
            mask = (key <= qry) & (key >= PAD)
            dqts, dcqs = [], []
            for h in range(nh):
                qh = q_ref[:, dq * h:dq * (h + 1)].astype(BF16)
                kh = k_ref[0:tk, dq * h:dq * (h + 1)].astype(BF16)
                vh = v_ref[0:tk, dv * h:dv * (h + 1)].astype(BF16)
                doh = do_ref[:, dv * h:dv * (h + 1)].astype(BF16)
                delta = jnp.sum(dot[dv * h:dv * (h + 1), :], axis=0, keepdims=True)
                st = _nt(kh, qh) * scale
                if use_bias:
                    st = st + (cr_ref[h:h + 1, :] - ckb[h, 0:tk, :])
                pt = jnp.exp(jnp.where(mask, st, NEG) - l_ref[h:h + 1, :])
                dst = pt * (_nt(vh, doh) - delta)
                dsb = dst.astype(BF16)
                dk_ref[0:tk, dq * h:dq * (h + 1)] += _nn(dsb, qh) * scale
                dv_ref[0:tk, dv * h:dv * (h + 1)] += _nn(pt.astype(BF16), doh)
                dqts.append(_nn(kt[dq * h:dq * (h + 1), 0:tk], dsb))
                if use_bias:
                    dcqs.append(jnp.sum(dst, axis=0, keepdims=True))
                    dacc[h, 0:tk, :] += dst
            dq_ref[...] = jnp.concatenate(dqts, axis=0).T * scale
            if use_bias:
                dcq_ref[...] = jnp.concatenate(dcqs + [jnp.zeros((8 - nh, tq), F32)], axis=0)

        for t0, t1 in _segments(nq):
            pl.when((i >= t0) & (i < t1))(functools.partial(tile, t1 * BLK))

        if use_bias:
            @pl.when(i == nq - 1)
            def _():
                lane = lax.broadcasted_iota(jnp.int32, (1, BLK), 1)
                tot = jnp.zeros((t, BLK), F32)
                for h in range(nh):
                    tot += jnp.where(lane == lane0 + h, jnp.sum(dacc[h], axis=1, keepdims=True), 0.0)
                dck_ref[...] = tot

    keys_q = pl.BlockSpec((t, nh * dq), lambda i: (0, 0))
    keys_v = pl.BlockSpec((t, nh * dv), lambda i: (0, 0))
    keys_c = pl.BlockSpec((t, BLK), lambda i: (0, 0))
    qrow = pl.BlockSpec((8, tq), lambda i: (0, i))
    in_specs = [pl.BlockSpec((tq, nh * dq), lambda i: (i, qcol)),
                pl.BlockSpec((t, nh * dq), lambda i: (0, kcol)),
                pl.BlockSpec((t, nh * dv), lambda i: (0, vcol)),
                pl.BlockSpec((tq, nh * dv), lambda i: (i, docol)),
                qrow,
                pl.BlockSpec((tq, nh * dv), lambda i: (i, ocol))]
    args = [q, k, v, do, lse_row, o]
    out_specs = [pl.BlockSpec((tq, nh * dq), lambda i: (i, 0)), keys_q, keys_v]
    out_shape = [jax.ShapeDtypeStruct((t, nh * dq), F32), jax.ShapeDtypeStruct((t, nh * dq), F32),
                 jax.ShapeDtypeStruct((t, nh * dv), F32)]
    scratch = [pltpu.VMEM((nh * dq, t), BF16)]
    if use_bias:
        in_specs += [keys_c, qrow]
        args += [c_col, c_row]
        out_specs += [qrow, keys_c]
        out_shape += [jax.ShapeDtypeStruct((8, t), F32), jax.ShapeDtypeStruct((t, BLK), F32)]
        scratch += [pltpu.VMEM((nh, t, BLK), F32), pltpu.VMEM((nh, t, BLK), F32)]
    return pl.pallas_call(
        body, name="attn_bwd", grid=(nq,),
        in_specs=in_specs, out_specs=out_specs, out_shape=out_shape, scratch_shapes=scratch,
        compiler_params=_cp("arbitrary"),
    )(*args)


def fox_pre(proj, fb):
    t = proj.shape[0]
    nb = t // BLK

    def body(sm_ref, fb_ref, c_ref, cr_ref):
        x = sm_ref[...] + fb_ref[...]
        lane = lax.broadcasted_iota(jnp.int32, (1, BLK), 1)
        keep = _valid_rows(t, 0) & (lane >= SM_F) & (lane < SM_F + FOX_H)
        logf = jnp.where(keep, jnp.minimum(x, 0.0) - jnp.log(1.0 + jnp.exp(-jnp.abs(x))), 0.0)
        tri = _tri().astype(F32)
        carry = jnp.zeros((1, BLK), F32)
        for b in range(nb):
            cb = _nn_hi(tri, logf[b * BLK:(b + 1) * BLK, :]) + carry
            c_ref[b * BLK:(b + 1) * BLK, :] = cb
            carry = cb[BLK - 1:BLK, :]
        cr_ref[...] = c_ref[...].T[SM_F:SM_F + 8, :]

    return pl.pallas_call(
        body, name="fox_pre", grid=(1,),
        in_specs=[pl.BlockSpec((t, BLK), lambda i: (0, C_SM // BLK)), pl.BlockSpec((1, BLK), lambda i: (0, 0))],
        out_specs=[pl.BlockSpec((t, BLK), lambda i: (0, 0)), pl.BlockSpec((8, t), lambda i: (0, 0))],
        out_shape=[jax.ShapeDtypeStruct((t, BLK), F32), jax.ShapeDtypeStruct((8, t), F32)],
        compiler_params=_cp("arbitrary"),
    )(proj, fb)


def fox_pre_bwd(dcq, dck, proj, fb, dsm_in):
    t = proj.shape[0]
    nb = t // BLK

    def body(dcq_ref, dck_ref, sm_ref, fb_ref, din_ref, dsm_ref, dfb_ref, scr):
        triu = _tri(lower=False).astype(F32)
        carry = jnp.zeros((1, BLK), F32)
        scr[...] = jnp.concatenate([jnp.zeros((SM_F, t), F32), dcq_ref[...], jnp.zeros((BLK - SM_F - 8, t), F32)], axis=0).T
        for b in range(nb - 1, -1, -1):
            blk = scr[b * BLK:(b + 1) * BLK, :] - dck_ref[b * BLK:(b + 1) * BLK, :]
            cb = _nn_hi(triu, blk) + carry
            scr[b * BLK:(b + 1) * BLK, :] = cb
            carry = cb[0:1, :]
        x = sm_ref[...] + fb_ref[...]
        lane = lax.broadcasted_iota(jnp.int32, (1, BLK), 1)
        keep = _valid_rows(t, 0) & (lane >= SM_F) & (lane < SM_F + FOX_H)
        df = jnp.where(keep, scr[...] * _sig(-x), 0.0)
        dfb_ref[...] = jnp.sum(df, axis=0, keepdims=True)
        dsm_ref[...] = din_ref[...] + df

    full = pl.BlockSpec((t, BLK), lambda i: (0, 0))
    return pl.pallas_call(
        body, name="fox_pre_bwd", grid=(1,),
        in_specs=[pl.BlockSpec((8, t), lambda i: (0, 0)), full,
                  pl.BlockSpec((t, BLK), lambda i: (0, C_SM // BLK)), pl.BlockSpec((1, BLK), lambda i: (0, 0)), full],
        out_specs=[full, pl.BlockSpec((1, BLK), lambda i: (0, 0))],
        out_shape=[jax.ShapeDtypeStruct((t, BLK), F32), jax.ShapeDtypeStruct((1, BLK), F32)],
        scratch_shapes=[pltpu.VMEM((t, BLK), F32)],
        compiler_params=_cp("arbitrary"),
    )(dcq, dck, proj, fb, dsm_in)


def _swap_rope(x):
    lane = lax.broadcasted_iota(jnp.int32, (1, BLK), 1)
    return jnp.where((lane >= SM_KR) & (lane < SM_KR + 16), pltpu.roll(x, BLK - 16, 1),
                     jnp.where((lane >= SM_KR + 16) & (lane < SM_KR + 32), pltpu.roll(x, 16, 1), 0.0))


def _rms(x, g):
    r = lax.rsqrt(jnp.mean(x * x, axis=1, keepdims=True) + EPS)
    return r, x * r


def mla_pre(proj, qg, kvg, wq, wk, wv, cosq, sinq):
    t = proj.shape[0]
    tm = _row_tile(t)

    def body(cq_ref, ckv_ref, sm_ref, qg_ref, kvg_ref, wq_ref, wk_ref, wv_ref, cos_ref, sin_ref,
             q_ref, k_ref, v_ref, cqn_ref, ckvn_ref):
        cs, sn = cos_ref[...], sin_ref[...]
        _, xh = _rms(cq_ref[...], None)
        cqn = (xh * qg_ref[...]).astype(BF16)
        cqn_ref[...] = cqn
        qraw = _nn(cqn, wq_ref[...])
        qs = []
        for h in range(MLA_H):
            hb = qraw[:, BLK * h:BLK * (h + 1)]
            qs.append(hb * cs + _swap_rope(hb) * sn)
        q_ref[...] = jnp.concatenate(qs, axis=1).astype(BF16)
        _, kh = _rms(ckv_ref[...], None)
        ckvn = (kh * kvg_ref[...]).astype(BF16)
        ckvn_ref[...] = ckvn
        kraw = _nn(ckvn, wk_ref[...])
        v_ref[...] = _nn(ckvn, wv_ref[...]).astype(BF16)
        lane = lax.broadcasted_iota(jnp.int32, (1, BLK), 1)
        kr = sm_ref[...]
        krr = jnp.where((lane >= SM_KR) & (lane < SM_KR + MLA_ROPE), kr * cs + _swap_rope(kr) * sn, 0.0)
        k_ref[...] = jnp.concatenate([kraw[:, BLK * h:BLK * (h + 1)] + krr for h in range(MLA_H)], axis=1).astype(BF16)

    def rows(w, cb):
        return pl.BlockSpec((tm, w), lambda i: (i, cb))

    def whole(a):
        return pl.BlockSpec(a.shape, lambda i: (0, 0))

    return pl.pallas_call(
        body, name="mla_pre", grid=(t // tm,),
        in_specs=[rows(MLA_QL, C_CQ // MLA_QL), rows(MLA_KVL, C_CKV // MLA_KVL), rows(BLK, C_SM // BLK),
                  whole(qg), whole(kvg), whole(wq), whole(wk), whole(wv), rows(BLK, 0), rows(BLK, 0)],
        out_specs=[rows(512, 0), rows(512, 0), rows(256, 0), rows(MLA_QL, 0), rows(MLA_KVL, 0)],
        out_shape=[jax.ShapeDtypeStruct((t, 512), BF16), jax.ShapeDtypeStruct((t, 512), BF16),
                   jax.ShapeDtypeStruct((t, 256), BF16), jax.ShapeDtypeStruct((t, MLA_QL), BF16),
                   jax.ShapeDtypeStruct((t, MLA_KVL), BF16)],
        compiler_params=_cp("arbitrary"),
    )(proj, proj, proj, qg, kvg, wq, wk, wv, cosq, sinq)


def mla_pre_bwd(dq, dk, dv, proj, cqn, ckvn, qg, kvg, wq, wk, wv, cosq, sinq, dsm_in):
    t = proj.shape[0]
    tm = _row_tile(t)

    def body(dq_ref, dk_ref, dv_ref, cq_ref, ckv_ref, cqn_ref, ckvn_ref, qg_ref, kvg_ref, wq_ref, wk_ref, wv_ref,
             cos_ref, sin_ref, din_ref, dcq_ref, dckv_ref, dsm_ref, dwq_ref, dwk_ref, dwv_ref, dqg_ref, dkvg_ref):
        i = pl.program_id(0)

        @pl.when(i == 0)
        def _():
            for r in (dwq_ref, dwk_ref, dwv_ref, dqg_ref, dkvg_ref):
                r[...] = jnp.zeros_like(r)

        cs, sn = cos_ref[...], sin_ref[...]
        lane = lax.broadcasted_iota(jnp.int32, (1, BLK), 1)

        def unrope(dy):
            return dy * cs + _swap_rope(dy * sn)

        dqp = jnp.concatenate([unrope(dq_ref[:, BLK * h:BLK * (h + 1)]) for h in range(MLA_H)], axis=1).astype(BF16)
        dwq_ref[...] += _tn(cqn_ref[...], dqp)
        dcqn = _nt(dqp, wq_ref[...])
        r, xh = _rms(cq_ref[...], None)
        dqg_ref[...] += jnp.sum(dcqn * xh, axis=0, keepdims=True)
        dxh = dcqn * qg_ref[...]
        dcq_ref[...] = r * (dxh - xh * jnp.mean(dxh * xh, axis=1, keepdims=True))

        dkn, dkr = [], jnp.zeros((tm, BLK), F32)
        for h in range(MLA_H):
            blk = dk_ref[:, BLK * h:BLK * (h + 1)]
            dkn.append(jnp.where(lane < MLA_NOPE, blk, 0.0))
            dkr += jnp.where((lane >= SM_KR) & (lane < SM_KR + MLA_ROPE), blk, 0.0)
        dknb = jnp.concatenate(dkn, axis=1).astype(BF16)
        dvb = dv_ref[...].astype(BF16)
        ckvn = ckvn_ref[...]
        dwk_ref[...] += _tn(ckvn, dknb)
        dwv_ref[...] += _tn(ckvn, dvb)
        dckvn = _nt(dknb, wk_ref[...]) + _nt(dvb, wv_ref[...])
        r2, kh = _rms(ckv_ref[...], None)
        dkvg_ref[...] += jnp.sum(dckvn * kh, axis=0, keepdims=True)
        dkh = dckvn * kvg_ref[...]
        dckv_ref[...] = r2 * (dkh - kh * jnp.mean(dkh * kh, axis=1, keepdims=True))
        dsm_ref[...] = din_ref[...] + jnp.where((lane >= SM_KR) & (lane < SM_KR + MLA_ROPE), unrope(dkr), 0.0)

    def rows(w, cb):
        return pl.BlockSpec((tm, w), lambda i: (i, cb))

    def whole(a):
        return pl.BlockSpec(a.shape, lambda i: (0, 0))

    def wshape(a):
        return jax.ShapeDtypeStruct(a.shape, F32)

    return pl.pallas_call(
        body, name="mla_pre_bwd", grid=(t // tm,),
        in_specs=[rows(512, 0), rows(512, 0), rows(256, 0), rows(MLA_QL, C_CQ // MLA_QL), rows(MLA_KVL, C_CKV // MLA_KVL),
                  rows(MLA_QL, 0), rows(MLA_KVL, 0), whole(qg), whole(kvg), whole(wq), whole(wk), whole(wv),
                  rows(BLK, 0), rows(BLK, 0), rows(BLK, 0)],
        out_specs=[rows(MLA_QL, 0), rows(MLA_KVL, 0), rows(BLK, 0), whole(wq), whole(wk), whole(wv), whole(qg), whole(kvg)],
        out_shape=[jax.ShapeDtypeStruct((t, MLA_QL), F32), jax.ShapeDtypeStruct((t, MLA_KVL), F32),
                   jax.ShapeDtypeStruct((t, BLK), F32), wshape(wq), wshape(wk), wshape(wv), wshape(qg), wshape(kvg)],
        compiler_params=_cp("arbitrary"),
    )(dq, dk, dv, proj, proj, cqn, ckvn, qg, kvg, wq, wk, wv, cosq, sinq, dsm_in)


def _slot_sum(me, own, recv_ref):
    gg = own.astype(F32)
    for s in range(N_DEV):
        gg = gg + jnp.where(me == s, 0.0, recv_ref[s].astype(F32))
    return gg


def adamw(w, m, v, g=None, recv=None, own=None, me_arr=None):
    shape = w.shape
    c = shape[-1]
    from_recv = recv is not None
    if not from_recv:
        me_arr = jnp.zeros((1,), jnp.int32)
    nl = len(recv) if from_recv else 1
    rws = w.size // c // nl
    tr = rws
    for d in (1024, 512, 352, 256, 128, 64, 32, 16, 8):
        if rws % d == 0 and d * c * 4 <= (2 << 20):
            tr = d
            break
    nt = rws // tr
    w2, m2, v2 = (a.reshape(nl, rws, c) for a in (w, m, v))
    if from_recv:
        gin = [a.reshape(N_DEV, rws, c) for a in list(recv) + list(own)]
    else:
        gin = [g.reshape(1, rws, c)]

    def body(me_ref, w_ref, m_ref, v_ref, *rest):
        g_refs, outs = rest[:len(gin)], rest[len(gin):]
        if from_recv:
            g_out, outs = outs[0], outs[1:]
            for li in range(nl):
                @pl.when(pl.program_id(0) == li)
                def _(li=li):
                    g_out[...] = _slot_sum(me_ref[0], g_refs[nl + li][...], g_refs[li])
            gg = g_out[...]
        else:
            gg = g_refs[0][...]
        d_ref, nm_ref, nv_ref = outs
        nm = B1 * m_ref[...] + (1.0 - B1) * gg
        nv = B2 * v_ref[...] + (1.0 - B2) * (gg * gg)
        mh = nm / (1.0 - B1 ** STEP)
        vh = nv / (1.0 - B2 ** STEP)
        d_ref[...] = -LR * (mh / (jnp.sqrt(vh) + AEPS) + WD * w_ref[...])
        nm_ref[...] = nm
        nv_ref[...] = nv

    row = pl.BlockSpec((None, tr, c), lambda l, i, me: (l, i, 0))
    if from_recv:
        gspecs = [pl.BlockSpec((N_DEV, tr, c), lambda l, i, me, li=li: (0, jnp.where(l == li, i, 0), 0))
                  for li in range(nl)]
        gspecs += [pl.BlockSpec((None, tr, c), lambda l, i, me, li=li: (me[0], jnp.where(l == li, i, 0), 0))
                   for li in range(nl)]
    else:
        gspecs = [row]
    nout = 4 if from_recv else 3
    outs = pl.pallas_call(
        body, name="adamw",
        grid_spec=pltpu.PrefetchScalarGridSpec(num_scalar_prefetch=1, grid=(nl, nt), in_specs=[row, row, row] + gspecs,
                                               out_specs=[row] * nout),
        out_shape=[jax.ShapeDtypeStruct((nl, rws, c), F32)] * nout,
        compiler_params=_cp("arbitrary", "arbitrary"),
    )(me_arr, w2, m2, v2, *gin)
    return tuple(o.reshape(shape) for o in outs)


def sum_slots(recv, own=None, me_arr=None):
    _, r, c = recv.shape
    if own is None:
        own, me_arr = recv, jnp.zeros((1,), jnp.int32)
        plain = True
    else:
        plain = False

    def body(me_ref, r_ref, own_ref, o_ref):
        if plain:
            gg = r_ref[0].astype(F32)
            for s in range(1, N_DEV):
                gg = gg + r_ref[s].astype(F32)
            o_ref[...] = gg
        else:
            o_ref[...] = _slot_sum(me_ref[0], own_ref[...], r_ref)

    return pl.pallas_call(
        body, name="sum_slots",
        grid_spec=pltpu.PrefetchScalarGridSpec(
            num_scalar_prefetch=1, grid=(1,),
            in_specs=[pl.BlockSpec((N_DEV, r, c), lambda i, me: (0, 0, 0)),
                      pl.BlockSpec((None, r, c), lambda i, me: (me[0], 0, 0))],
            out_specs=pl.BlockSpec((r, c), lambda i, me: (0, 0))),
        out_shape=jax.ShapeDtypeStruct((r, c), F32),
        compiler_params=_cp("arbitrary"),
    )(me_arr, recv, own)


_FLIPS = [(0, 0, 1), (0, 1, 0), (0, 1, 1), (1, 0, 0), (1, 0, 1), (1, 1, 0), (1, 1, 1)]
_ANY = pl.BlockSpec(memory_space=pl.ANY)


def _mesh_place():
    x, y, c = lax.axis_index("x"), lax.axis_index("y"), lax.axis_index("c")
    me = 4 * x + 2 * y + c
    peers = [((x + fx) % 2, (y + fy) % 2, (c + fc) % 2) for fx, fy, fc in _FLIPS]
    return me, peers


def place_own(src, l, dtype, me_arr):
    _, r, c = src.shape
    tr = r
    for d in (512, 352, 256, 128, 64, 32, 16, 8):
        if r % d == 0 and d * c * 4 <= (2 << 20):
            tr = d
            break

    def body(me_ref, s_ref, o_ref):
        o_ref[...] = s_ref[...].astype(dtype)

    return pl.pallas_call(
        body, name="place_own",
        grid_spec=pltpu.PrefetchScalarGridSpec(
            num_scalar_prefetch=1, grid=(r // tr,),
            in_specs=[pl.BlockSpec((None, tr, c), lambda i, me: (l, i, 0))],
            out_specs=pl.BlockSpec((None, tr, c), lambda i, me: (me[0], i, 0))),
        out_shape=jax.ShapeDtypeStruct((N_DEV, r, c), dtype),
        compiler_params=_cp("arbitrary"),
    )(me_arr, src)


_HBM = pl.BlockSpec(memory_space=pltpu.HBM)
_SEMS = pl.BlockSpec(memory_space=pltpu.SEMAPHORE)
_EFFECT = pltpu.SideEffectType.DATAFLOW_SIDE_EFFECTING


def exchange_start(mode, arrays, name, after=None):
    n = len(arrays)
    gather = mode == "gather"
    ns = 0 if gather else n
    zones = list(arrays) if gather else [lax.empty(a.shape, a.dtype) for a in arrays]
    ops = ([] if gather else list(arrays)) + zones
    extra = [] if after is None else [after]

    def body(*refs):
        srcs, lands = refs[:ns], refs[ns:ns + n]
        send_sems, recv_sems = refs[ns + n + len(extra)], refs[ns + n + len(extra) + 1]
        token = refs[-1]
        me, peers = _mesh_place()
        ids = [4 * p[0] + 2 * p[1] + p[2] for p in peers]
        for j in range(n):
            for k in range(N_DEV - 1):
                src = lands[j].at[me] if gather else srcs[j].at[ids[k]]
                pltpu.make_async_remote_copy(src_ref=src, dst_ref=lands[j].at[me],
                                             send_sem=send_sems.at[j * (N_DEV - 1) + k],
                                             recv_sem=recv_sems.at[j * (N_DEV - 1) + k], device_id=peers[k],
                                             device_id_type=pl.DeviceIdType.MESH).start()
        token[...] = jnp.zeros_like(token)

    nsem = n * (N_DEV - 1)
    res = pl.pallas_call(
        body, name=name,
        in_specs=[_HBM] * (ns + n) + [_ANY] * len(extra),
        out_specs=(_SEMS, _SEMS, *[_HBM] * (ns + n), pl.BlockSpec(memory_space=pltpu.VMEM)),
        out_shape=(pltpu.SemaphoreType.DMA((nsem,)), pltpu.SemaphoreType.DMA((nsem,)),
                   *[pltpu.HBM(a.shape, a.dtype) for a in ops], jax.ShapeDtypeStruct((8, BLK), F32)),
        input_output_aliases={i: 2 + i for i in range(ns + n)},
        compiler_params=pltpu.CompilerParams(has_side_effects=_EFFECT),
    )(*[pltpu.with_memory_space_constraint(a, pltpu.HBM) for a in ops], *extra)
    return dict(gather=gather, send=res[0], recv=res[1], srcs=list(res[2:2 + ns]), lands=list(res[2 + ns:2 + ns + n]),
                token=res[-1])


def exchange_wait(hd, idxs, name, after):
    gather = hd["gather"]
    n = len(idxs)
    ns = 0 if gather else n
    ops = ([] if gather else [hd["srcs"][j] for j in idxs]) + [hd["lands"][j] for j in idxs]

    def body(*refs):
        srcs, lands = refs[:ns], refs[ns:ns + n]
        send_sems, recv_sems = refs[ns + n], refs[ns + n + 1]
        me, peers = _mesh_place()
        ids = [4 * p[0] + 2 * p[1] + p[2] for p in peers]
        for p, j in enumerate(idxs):
            for k in range(N_DEV - 1):
                src = lands[p].at[me] if gather else srcs[p].at[ids[k]]
                cp = pltpu.make_async_remote_copy(src_ref=src, dst_ref=lands[p].at[ids[k]],
                                                  send_sem=send_sems.at[j * (N_DEV - 1) + k],
                                                  recv_sem=recv_sems.at[j * (N_DEV - 1) + k], device_id=peers[k],
                                                  device_id_type=pl.DeviceIdType.MESH)
                cp.wait_send()
                cp.wait_recv()

    res = pl.pallas_call(
        body, name=name,
        in_specs=[_HBM] * (ns + n) + [_SEMS, _SEMS, _ANY],
        out_specs=[_HBM] * (ns + n),
        out_shape=[pltpu.HBM(a.shape, a.dtype) for a in ops],
        input_output_aliases={i: i for i in range(ns + n)},
        compiler_params=pltpu.CompilerParams(has_side_effects=_EFFECT),
    )(*ops, hd["send"], hd["recv"], after)
    return list(res[:ns]), list(res[ns:])


def _chip_place():
    x, y, c = lax.axis_index("x"), lax.axis_index("y"), lax.axis_index("c")
    chips = [((x + 1) % 2, y), (x, (y + 1) % 2), ((x + 1) % 2, (y + 1) % 2)]
    ident = lambda p: 4 * p[0] + 2 * p[1] + p[2]
    return dict(me=4 * x + 2 * y + c, sib=(x, y, 1 - c), sib_id=4 * x + 2 * y + 1 - c,
                same=[(cx, cy, c) for cx, cy in chips], same_ids=[ident((cx, cy, c)) for cx, cy in chips],
                other_ids=[ident((cx, cy, 1 - c)) for cx, cy in chips])


def _remote(src, dst, send_sem, recv_sem, dev):
    return pltpu.make_async_remote_copy(src_ref=src, dst_ref=dst, send_sem=send_sem, recv_sem=recv_sem, device_id=dev,
                                        device_id_type=pl.DeviceIdType.MESH)


def gather_start(zones, name):
    n = len(zones)

    def body(*refs):
        lands, send_sems, recv_sems, token = refs[:n], refs[n], refs[n + 1], refs[-1]
        pc = _chip_place()
        for j in range(n):
            own = lands[j].at[pc["me"]]
            for k, dev in enumerate([pc["sib"]] + pc["same"]):
                _remote(own, own, send_sems.at[4 * j + k], recv_sems.at[4 * j + k], dev).start()
        token[...] = jnp.zeros_like(token)

    res = pl.pallas_call(
        body, name=name,
        in_specs=[_HBM] * n,
        out_specs=(_SEMS, _SEMS, *[_HBM] * n, pl.BlockSpec(memory_space=pltpu.VMEM)),
        out_shape=(pltpu.SemaphoreType.DMA((4 * n,)), pltpu.SemaphoreType.DMA((4 * n,)),
                   *[pltpu.HBM(a.shape, a.dtype) for a in zones], jax.ShapeDtypeStruct((8, BLK), F32)),
        input_output_aliases={i: 2 + i for i in range(n)},
        compiler_params=pltpu.CompilerParams(has_side_effects=_EFFECT),
    )(*[pltpu.with_memory_space_constraint(a, pltpu.HBM) for a in zones])
    return dict(send=res[0], recv=res[1], lands=list(res[2:2 + n]), token=res[-1])


def gather_relay(hd, idxs, name, after):
    n = len(idxs)

    def body(*refs):
        lands, send_sems, recv_sems = refs[:n], refs[n], refs[n + 1]
        fsend, frecv, token = refs[n + 3 + n], refs[n + 4 + n], refs[-1]
        pc = _chip_place()
        for p, j in enumerate(idxs):
            for k in range(3):
                _remote(lands[p].at[pc["me"]], lands[p].at[pc["same_ids"][k]], send_sems.at[4 * j + 1 + k],
                        recv_sems.at[4 * j + 1 + k], pc["same"][k]).wait_recv()
        for p in range(n):
            for k in range(3):
                got = lands[p].at[pc["same_ids"][k]]
                _remote(got, got, fsend.at[3 * p + k], frecv.at[3 * p + k], pc["sib"]).start()
        token[...] = jnp.zeros_like(token)

    ops = [hd["lands"][j] for j in idxs]
    res = pl.pallas_call(
        body, name=name,
        in_specs=[_HBM] * n + [_SEMS, _SEMS, _ANY],
        out_specs=(*[_HBM] * n, _SEMS, _SEMS, pl.BlockSpec(memory_space=pltpu.VMEM)),
        out_shape=(*[pltpu.HBM(a.shape, a.dtype) for a in ops], pltpu.SemaphoreType.DMA((3 * n,)),
                   pltpu.SemaphoreType.DMA((3 * n,)), jax.ShapeDtypeStruct((8, BLK), F32)),
        input_output_aliases={i: i for i in range(n)},
        compiler_params=pltpu.CompilerParams(has_side_effects=_EFFECT),
    )(*ops, hd["send"], hd["recv"], after)
    return dict(lands=list(res[:n]), fsend=res[n], frecv=res[n + 1], token=res[-1])


def gather_wait(hd, rl, idxs, name, after):
    n = len(idxs)

    def body(*refs):
        lands, send_sems, recv_sems, fsend, frecv = refs[:n], refs[n], refs[n + 1], refs[n + 2], refs[n + 3]
        pc = _chip_place()
        for p, j in enumerate(idxs):
            own = lands[p].at[pc["me"]]
            for k, dev in enumerate([pc["sib"]] + pc["same"]):
                _remote(own, own, send_sems.at[4 * j + k], recv_sems.at[4 * j + k], dev).wait_send()
            _remote(own, lands[p].at[pc["sib_id"]], send_sems.at[4 * j], recv_sems.at[4 * j], pc["sib"]).wait_recv()
            for k in range(3):
                cp = _remote(lands[p].at[pc["same_ids"][k]], lands[p].at[pc["other_ids"][k]], fsend.at[3 * p + k],
                             frecv.at[3 * p + k], pc["sib"])
                cp.wait_send()
                cp.wait_recv()

    res = pl.pallas_call(
        body, name=name,
        in_specs=[_HBM] * n + [_SEMS, _SEMS, _SEMS, _SEMS, _ANY],
        out_specs=[_HBM] * n,
        out_shape=[pltpu.HBM(a.shape, a.dtype) for a in rl["lands"]],
        input_output_aliases={i: i for i in range(n)},
        compiler_params=pltpu.CompilerParams(has_side_effects=_EFFECT),
    )(*rl["lands"], hd["send"], hd["recv"], rl["fsend"], rl["frecv"], after)
    return list(res)


def _pad_cols(a, n):
    return jnp.pad(a, ((0, 0),) * (a.ndim - 1) + ((0, n - a.shape[-1]),))


def w_in_to_padded(w):
    z = lambda n: jnp.zeros(w.shape[:-1] + (n,), w.dtype)
    return jnp.concatenate([
        w[..., 0:1280], w[..., 1288:2056], w[..., 2060:2316], w[..., 2316:2444],
        w[..., 1280:1288], w[..., 2056:2060], z(SM_KR - SM_F - FOX_H), w[..., 2444:2476], z(BLK - SM_KR - MLA_ROPE)], axis=-1)


def w_in_from_padded(g):
    s = C_SM
    return jnp.concatenate([
        g[..., 0:1280], g[..., s + SM_DT:s + SM_DT + 8], g[..., 1280:2048], g[..., s + SM_F:s + SM_F + 4],
        g[..., 2048:2304], g[..., 2304:2432], g[..., s + SM_KR:s + SM_KR + MLA_ROPE]], axis=-1)


def _unshard_cols(gth):
    n, r, c = gth.shape
    return jnp.transpose(gth, (1, 0, 2)).reshape(r, n * c)


def _shard_cols(full):
    r, nc = full.shape
    return jnp.transpose(full.reshape(r, N_DEV, nc // N_DEV), (1, 0, 2))


def mla_weights(uq_g, ukv_g):
    uq = _unshard_cols(uq_g)
    dqh = MLA_NOPE + MLA_ROPE
    wq = jnp.concatenate([_pad_cols(uq[:, dqh * h:dqh * (h + 1)], BLK) for h in range(MLA_H)], axis=1)
    wk = jnp.concatenate([_pad_cols(ukv_g[2 * h], BLK) for h in range(MLA_H)], axis=1)
    wv = jnp.concatenate([ukv_g[2 * h + 1] for h in range(MLA_H)], axis=1)
    return wq, wk, wv


def mla_weight_grads(dwq, dwk, dwv):
    dqh = MLA_NOPE + MLA_ROPE
    duq = _shard_cols(jnp.concatenate([dwq[:, BLK * h:BLK * h + dqh] for h in range(MLA_H)], axis=1))
    parts = []
    for h in range(MLA_H):
        parts += [dwk[:, BLK * h:BLK * h + MLA_NOPE], dwv[:, MLA_V * h:MLA_V * (h + 1)]]
    return duq, jnp.stack(parts, axis=0)


def rope_tables(t):
    pos = (jnp.arange(t, dtype=jnp.int32) - PAD).astype(F32)
    inv_freq = 1.0 / (10000.0 ** (jnp.arange(0, MLA_ROPE, 2, dtype=F32) / MLA_ROPE))
    ang = pos[:, None] * inv_freq[None, :]
    cos, sin = jnp.cos(ang), jnp.sin(ang)
    one, zero = jnp.ones((t, SM_KR), F32), jnp.zeros((t, SM_KR), F32)
    tail = BLK - SM_KR - MLA_ROPE
    cosq = jnp.concatenate([one, cos, cos, jnp.ones((t, tail), F32)], axis=1)
    sinq = jnp.concatenate([zero, -sin, sin, jnp.zeros((t, tail), F32)], axis=1)
    return cosq, sinq


def _lanes(v, off=0):
    return jnp.pad(v.astype(F32), (off, BLK - off - v.shape[0]))[None, :]


def layer_fwd(x, ln, hb, getw, tabs, ahead):
    sv = {"h0b": hb}
    W = dict(getw("ffn1", hb))
    ln1 = (W["ln1_g"], W["ln1_b"])
    u, v, r1, h1b = ffn_fwd_seq(x, ln, W["g1"], W["u1"], W["d1"], ln1)
    sv.update(u1=u, v1=v, r1=r1, h1b=h1b)
    W.update(getw("mix", h1b))
    ln2 = (W["ln2_g"], W["ln2_b"])
    proj = mm_nn(h1b, W["w_in"])
    xa = conv_fwd(proj, W["conv_w"], W["conv_b"])
    y_ssd, sprev = ssd_fwd(xa, proj, W["dtb"], W["alog"], W["dskip"], W["normg"])
    c_col, c_row = fox_pre(proj, W["fb"])
    y_fox, lse_f = attn_fwd(proj, proj, proj, C_FQ // 256, C_FK // 256, C_FV // 256, FOX_H, FOX_DH, FOX_DH,
                            FOX_DH ** -0.5, c_col, c_row, SM_F)
    ahead(0, "ffn2", y_fox)
    q, k, vv, cqn, ckvn = mla_pre(proj, W["qg"], W["kvg"], W["wq"], W["wk"], W["wv"], *tabs)
    y_mla, lse_m = attn_fwd(q, k, vv, 0, 0, 0, MLA_H, BLK, MLA_V, (MLA_NOPE + MLA_ROPE) ** -0.5)
    mixcat = jnp.concatenate([y_ssd, y_fox, y_mla], axis=1)
    r2, h2b = mm_res_ln(mixcat, W["w_out"], r1, ln1, ln2)
    sv.update(proj=proj, xa=xa, sprev=sprev, c_col=c_col, c_row=c_row, lse_f=lse_f, q=q, k=k, v=vv, cqn=cqn, ckvn=ckvn,
              lse_m=lse_m, mixcat=mixcat, r2=r2, h2b=h2b)
    W.update(getw("ffn2", h2b))
    ahead(1, "ffn1", h2b)
    ln3 = (W["ln3_g"], W["ln3_b"])
    u, v, r3, h3b = ffn_fwd_seq(r2, ln2, W["g2"], W["u2"], W["d2"], ln3)
    sv.update(u2=u, v2=v, r3=r3, W=W)
    return r3, ln3, h3b, sv


def ffn_bwd(parts, r, gamma, hb_in, u, v, wg, wu, wd, after=None):
    dh, dwg, dwu, dwd, dg, db = ffn_bwd_seq(parts, r, gamma, hb_in, u, v, wg, wu, wd, after)
    return dh, dict(d=dwd, g=dwg, u=dwu, ln_g=dg, ln_b=db)


def layer_bwd(parts, sv, emit, tabs, after):
    G = {}
    W = sv["W"]
    dh2, g2 = ffn_bwd(parts, sv["r3"], W["ln3_g"], sv["h2b"], sv["u2"], sv["v2"], W["g2"], W["u2"], W["d2"], after)
    G.update(g2=g2["g"], u2=g2["u"], d2=g2["d"], ln3_g=g2["ln_g"], ln3_b=g2["ln_b"])
    tok = emit("ffn2", G)
    dr2, dmixb, G["ln2_g"], G["ln2_b"] = ln_bwd([(dh2, 1.0)], sv["r2"], W["ln2_g"], 1.0, tok)
    dmc = mm_nt_reduce([(dmixb[None], W["w_out"][None])], D)
    G["w_out"] = mm_tn(sv["mixcat"][None], dmixb[None])[0]
    proj = sv["proj"]
    dxa, dz, dsm, G["normg"], G["dskip"], G["alog"], G["dtb"] = ssd_bwd(
        dmc, sv["xa"], proj, sv["sprev"], W["dtb"], W["alog"], W["dskip"], W["normg"])
    dxbc, G["conv_w"], G["conv_b"] = conv_bwd(dxa, proj, W["conv_w"], W["conv_b"])
    dfq, dfk, dfv, dcq, dck = attn_bwd(proj, proj, proj, dmc, sv["lse_f"], sv["mixcat"], C_FQ // 256, C_FK // 256,
                                       C_FV // 256, 2, 2, FOX_H, FOX_DH, FOX_DH, FOX_DH ** -0.5, sv["c_col"], sv["c_row"], SM_F)
    dsm, G["fb"] = fox_pre_bwd(dcq, dck, proj, W["fb"], dsm)
    dq, dk, dv = attn_bwd(sv["q"], sv["k"], sv["v"], dmc, sv["lse_m"], sv["mixcat"], 0, 0, 0, 3, 3, MLA_H, BLK, MLA_V,
                          (MLA_NOPE + MLA_ROPE) ** -0.5)
    dcql, dckv, dsm, G["wq"], G["wk"], G["wv"], G["qg"], G["kvg"] = mla_pre_bwd(
        dq, dk, dv, proj, sv["cqn"], sv["ckvn"], W["qg"], W["kvg"], W["wq"], W["wk"], W["wv"], *tabs, dsm)
    dproj = jnp.concatenate([dz, dxbc, dfq, dfk, dfv, dcql, dckv, dsm], axis=1).astype(BF16)
    dh1p = mm_nt_reduce([(dproj[None], W["w_in"][None])], D)
    G["w_in"] = mm_tn(sv["h1b"][None], dproj[None])[0]
    tok = emit("mix", G)
    dh0, g1 = ffn_bwd([(dr2, ALPHA), (dh1p, 1.0)], sv["r1"], W["ln1_g"], sv["h0b"], sv["u1"], sv["v1"],
                      W["g1"], W["u1"], W["d1"], tok)
    G.update(g1=g1["g"], u1=g1["u"], d1=g1["d"], ln1_g=g1["ln_g"], ln1_b=g1["ln_b"])
    tok = emit("ffn1", G)
    return [(dh0, 1.0)], G, tok


def local_step(x, target, meta_full, getw, emit, ahead=lambda l, stage, after: None):
    t = x.shape[0] + BLK
    tabs = rope_tables(t)
    xr, hb = build_h0(meta_full, x)
    ln = None
    saved = []
    for l in range(NL):
        xr, ln, hb, sv = layer_fwd(xr, ln, hb, functools.partial(getw, l), tabs,
                                   lambda dl, stage, after, l=l: ahead(l + dl, stage, after))
        saved.append(sv)
    dy, loss = loss_head(xr, ln, target)
    parts = [(dy, 1.0)]
    grads = [None] * NL
    tok = None
    for l in range(NL - 1, -1, -1):
        parts, grads[l], tok = layer_bwd(parts, saved[l], functools.partial(emit, l), tabs, tok)
    gx, gmeta = split_dh0(parts[0][0], tok)
    return loss, gx, gmeta, grads


_SMALL = ["ln1_g", "ln1_b", "ln2_g", "ln2_b", "ln3_g", "ln3_b", "conv_b", "ssd_norm_g", "mla_q_norm_g",
          "mla_kv_norm_g", "dt_bias", "a_log", "d_skip", "fox_f_b"]
_SMALL_ROWS = 8
_BIG = ["ffn1_w_gate", "ffn1_w_up", "ffn1_w_down", "w_in", "conv_w", "mla_w_uq", "mla_w_ukv", "w_out",
        "ffn2_w_gate", "ffn2_w_up", "ffn2_w_down"]
_NAMES = ["meta", "ffn1_w_gate", "ffn1_w_up", "ffn1_w_down", "ln1_g", "ln1_b", "w_in", "conv_w", "conv_b", "dt_bias",
          "a_log", "d_skip", "ssd_norm_g", "fox_f_b", "mla_q_norm_g", "mla_w_uq", "mla_kv_norm_g", "mla_w_ukv", "w_out",
          "ln2_g", "ln2_b", "ffn2_w_gate", "ffn2_w_up", "ffn2_w_down", "ln3_g", "ln3_b"]


def pack_small(p):
    flat = jnp.concatenate([p[n].astype(F32) for n in _SMALL], axis=1)
    return _pad_cols(flat, _SMALL_ROWS * D).reshape(NL * _SMALL_ROWS, D)


def unpack_small(a, like):
    flat = a.reshape(NL, _SMALL_ROWS * D)
    out, at = {}, 0
    for n in _SMALL:
        out[n] = flat[:, at:at + like[n].shape[1]]
        at += like[n].shape[1]
    return out


_STAGES = {"ffn1": ["ffn1_w_gate", "ffn1_w_up", "ffn1_w_down"],
           "mix": ["w_in", "conv_w", "mla_w_uq", "mla_w_ukv", "w_out"],
           "ffn2": ["ffn2_w_gate", "ffn2_w_up", "ffn2_w_down"]}


_FFN_T = ("ffn1_w_gate", "ffn1_w_up", "ffn2_w_gate", "ffn2_w_up")


def stage_weights(l, stage, g, rep):
    if stage != "mix":
        i = stage[3]
        return {"g" + i: g[f"ffn{i}_w_gate"].reshape(D_FF, D), "u" + i: g[f"ffn{i}_w_up"].reshape(D_FF, D),
                "d" + i: g[f"ffn{i}_w_down"].reshape(D_FF, D),
                "ln1_g" if i == "1" else "ln3_g": rep["ln1_g" if i == "1" else "ln3_g"][l][None, :],
                "ln1_b" if i == "1" else "ln3_b": rep["ln1_b" if i == "1" else "ln3_b"][l][None, :]}
    W = {}
    W["w_in"] = g["w_in"].reshape(D, N_INP)
    W["w_out"] = g["w_out"].reshape(D, D)
    W["wq"], W["wk"], W["wv"] = mla_weights(g["mla_w_uq"], g["mla_w_ukv"])
    W["conv_w"] = _unshard_cols(g["conv_w"])
    for k in ("ln2_g", "ln2_b", "conv_b"):
        W[k] = rep[k][l][None, :]
    W["normg"] = rep["ssd_norm_g"][l][None, :]
    W["qg"] = rep["mla_q_norm_g"][l][None, :]
    W["kvg"] = rep["mla_kv_norm_g"][l][None, :]
    W["dtb"] = _lanes(rep["dt_bias"][l], SM_DT)
    W["alog"] = _lanes(rep["a_log"][l], SM_DT)
    W["dskip"] = _lanes(rep["d_skip"][l], SM_DT)
    W["fb"] = _lanes(rep["fox_f_b"][l], SM_F)
    return W


def small_grads(G):
    return {"ln1_g": G["ln1_g"][0], "ln1_b": G["ln1_b"][0], "ln2_g": G["ln2_g"][0], "ln2_b": G["ln2_b"][0],
            "ln3_g": G["ln3_g"][0], "ln3_b": G["ln3_b"][0], "conv_b": G["conv_b"][0], "ssd_norm_g": G["normg"][0],
            "mla_q_norm_g": G["qg"][0], "mla_kv_norm_g": G["kvg"][0], "dt_bias": G["dtb"][0, :SSD_H],
            "a_log": G["alog"][0, :SSD_H], "d_skip": G["dskip"][0, :SSD_H], "fox_f_b": G["fb"][0, SM_F:SM_F + FOX_H]}


def big_grads(G, stage):
    if stage != "mix":
        i = stage[-1]
        return {f"ffn{i}_w_{k}": G[k[0] + i].reshape(N_DEV, HS, D) for k in ("gate", "up", "down")}
    duq, dukv = mla_weight_grads(G["wq"], G["wk"], G["wv"])
    return {"w_in": G["w_in"].reshape(N_DEV, D // N_DEV, N_INP), "w_out": G["w_out"].reshape(N_DEV, D // N_DEV, D),
            "mla_w_uq": duq, "mla_w_ukv": dukv, "conv_w": _shard_cols(G["conv_w"])}


def kernel(x, meta, ffn1_w_gate, ffn1_w_up, ffn1_w_down, ln1_g, ln1_b, w_in, conv_w, conv_b, dt_bias, a_log, d_skip, ssd_norm_g, fox_f_b, mla_q_norm_g, mla_w_uq, mla_kv_norm_g, mla_w_ukv, w_out, ln2_g, ln2_b, ffn2_w_gate, ffn2_w_up, ffn2_w_down, ln3_g, ln3_b, loss_target, m_meta, m_ffn1_w_gate, m_ffn1_w_up, m_ffn1_w_down, m_ln1_g, m_ln1_b, m_w_in, m_conv_w, m_conv_b, m_dt_bias, m_a_log, m_d_skip, m_ssd_norm_g, m_fox_f_b, m_mla_q_norm_g, m_mla_w_uq, m_mla_kv_norm_g, m_mla_w_ukv, m_w_out, m_ln2_g, m_ln2_b, m_ffn2_w_gate, m_ffn2_w_up, m_ffn2_w_down, m_ln3_g, m_ln3_b, v_meta, v_ffn1_w_gate, v_ffn1_w_up, v_ffn1_w_down, v_ln1_g, v_ln1_b, v_w_in, v_conv_w, v_conv_b, v_dt_bias, v_a_log, v_d_skip, v_ssd_norm_g, v_fox_f_b, v_mla_q_norm_g, v_mla_w_uq, v_mla_kv_norm_g, v_mla_w_ukv, v_w_out, v_ln2_g, v_ln2_b, v_ffn2_w_gate, v_ffn2_w_up, v_ffn2_w_down, v_ln3_g, v_ln3_b):
    vals = (meta, ffn1_w_gate, ffn1_w_up, ffn1_w_down, ln1_g, ln1_b, w_in, conv_w, conv_b, dt_bias, a_log, d_skip, ssd_norm_g, fox_f_b, mla_q_norm_g, mla_w_uq, mla_kv_norm_g, mla_w_ukv, w_out, ln2_g, ln2_b, ffn2_w_gate, ffn2_w_up, ffn2_w_down, ln3_g, ln3_b)
    moms = (m_meta, m_ffn1_w_gate, m_ffn1_w_up, m_ffn1_w_down, m_ln1_g, m_ln1_b, m_w_in, m_conv_w, m_conv_b, m_dt_bias, m_a_log, m_d_skip, m_ssd_norm_g, m_fox_f_b, m_mla_q_norm_g, m_mla_w_uq, m_mla_kv_norm_g, m_mla_w_ukv, m_w_out, m_ln2_g, m_ln2_b, m_ffn2_w_gate, m_ffn2_w_up, m_ffn2_w_down, m_ln3_g, m_ln3_b)
    vars_ = (v_meta, v_ffn1_w_gate, v_ffn1_w_up, v_ffn1_w_down, v_ln1_g, v_ln1_b, v_w_in, v_conv_w, v_conv_b, v_dt_bias, v_a_log, v_d_skip, v_ssd_norm_g, v_fox_f_b, v_mla_q_norm_g, v_mla_w_uq, v_mla_kv_norm_g, v_mla_w_ukv, v_w_out, v_ln2_g, v_ln2_b, v_ffn2_w_gate, v_ffn2_w_up, v_ffn2_w_down, v_ln3_g, v_ln3_b)
    P = dict(zip(_NAMES, vals))
    M = dict(zip(_NAMES, moms))
    V = dict(zip(_NAMES, vars_))
    me = 4 * lax.axis_index("x") + 2 * lax.axis_index("y") + lax.axis_index("c")

    me_arr = me.astype(jnp.int32).reshape(1)
    for n in _FFN_T:
        P[n], M[n], V[n] = (jnp.swapaxes(a[n], 1, 2) for a in (P, M, V))
    src = dict(P)
    src["w_in"] = w_in_to_padded(P["w_in"])
    order = [("meta", 0)] + [(n, l) for l in range(NL) for names in _STAGES.values() for n in names]
    nfirst = 1 + len(_STAGES["ffn1"])

    def place(n, l):
        return place_own(P["meta"][None] if n == "meta" else src[n], l, F32 if n in ("meta", "conv_w") else BF16, me_arr)

    hg_first = gather_start([place(n, l) for n, l in order[:nfirst]], "gather_start_first")
    hg_rest = gather_start([place(n, l) for n, l in order[nfirst:]], "gather_start_rest")
    zone_of = {nl_: ((hg_first, i) if i < nfirst else (hg_rest, i - nfirst)) for i, nl_ in enumerate(order)}
    relays = {}

    def ahead(l, stage, after):
        if l < NL and (l, stage) not in relays:
            zs = [zone_of[("meta", 0)]] if stage == "meta" else [zone_of[(n, l)] for n in _STAGES[stage]]
            hg, idxs = zs[0][0], [i for _, i in zs]
            relays[(l, stage)] = (hg, idxs, gather_relay(hg, idxs, f"gather_relay_{l}_{stage}", after))

    def arrived(l, stage, after):
        ahead(l, stage, after)
        hg, idxs, rl = relays[(l, stage)]
        return gather_wait(hg, rl, idxs, f"gather_wait_{l}_{stage}", after)

    meta_full = _unshard_cols(arrived(0, "meta", hg_rest["token"])[0])

    def getw(l, stage, after):
        return stage_weights(l, stage, dict(zip(_STAGES[stage], arrived(l, stage, after))), P)

    sent = {}

    def emit(l, stage, G):
        bg = big_grads(G, stage)
        sent[(l, stage)] = exchange_start("scatter", [bg[n] for n in _STAGES[stage]], f"scatter_start_{l}_{stage}")
        return sent[(l, stage)]["token"]

    loss, gx, gmeta, grads = local_step(x[0], loss_target[0], meta_full, getw, emit, ahead)

    small = jnp.concatenate([pack_small({n: jnp.stack([small_grads(g)[n] for g in grads]) for n in _SMALL}), gmeta], axis=0)
    hs = exchange_start("gather", [place_own(small[None], 0, F32, me_arr)], "small_start")

    out = {}
    after = hs["token"]
    for stage in ("ffn2", "mix", "ffn1"):
        names = _STAGES[stage]
        got = [exchange_wait(sent[(l, stage)], list(range(len(names))), f"scatter_wait_{l}_{stage}", after)
               for l in range(NL - 1, -1, -1)][::-1]
        for i, n in enumerate(names):
            own = [got[l][0][i] for l in range(NL)]
            recv = [got[l][1][i] for l in range(NL)]
            if n == "w_in":
                g = jnp.stack([w_in_from_padded(sum_slots(recv[l], own[l], me_arr)) for l in range(NL)])
                out[n] = (g,) + adamw(P[n], M[n], V[n], g=g)
            else:
                out[n] = adamw(P[n], M[n], V[n], recv=recv, own=own, me_arr=me_arr)
                if n in _FFN_T:
                    out[n] = tuple(jnp.swapaxes(a, 1, 2) for a in out[n])
        after = out[names[-1]][1]
    gsmall = sum_slots(exchange_wait(hs, [0], "small_wait", after)[1][0])
    gm = lax.dynamic_slice(gsmall[NL * _SMALL_ROWS:], (0, me * (D // N_DEV)), (N_META, D // N_DEV))
    out["meta"] = (gm,) + adamw(P["meta"], M["meta"], V["meta"], g=gm)
    gs = gsmall[:NL * _SMALL_ROWS]
    sd, sm_, sv_ = adamw(pack_small(P), pack_small(M), pack_small(V), g=gs)
    ups = [unpack_small(a, P) for a in (gs, sd, sm_, sv_)]
    for n in _SMALL:
        out[n] = tuple(u[n] for u in ups)

    loss_all = lax.psum(loss[0, 0], ("x", "y", "c"))
    flat = [loss_all, gx[None]]
    for k in range(4):
        flat += [out[n][k] for n in _NAMES]
    return tuple(flat)
```

```python
import functools

import jax
import jax.numpy as jnp
from jax import lax
from jax.experimental import pallas as pl
from jax.experimental.pallas import tpu as pltpu

F32, BF16 = jnp.float32, jnp.bfloat16
HI = lax.Precision.HIGHEST

N_DEV = 8
D = 1024
NL = 2
N_META = 16
BLK = 128
PAD = BLK - N_META
D_FF = 2816
HS = D_FF // N_DEV
SSD_H, SSD_P, SSD_N, SSD_G = 8, 64, 64, 2
SSD_D = SSD_H * SSD_P
CONV_K = 4
CONV_D = SSD_D + 2 * SSD_G * SSD_N
FOX_H, FOX_DH = 4, 64
MLA_H, MLA_QL, MLA_KVL, MLA_NOPE, MLA_ROPE, MLA_V = 4, 256, 128, 64, 32, 64
N_IN = 2476
C_Z, C_XBC, C_FQ, C_FK, C_FV, C_CQ, C_CKV, C_SM, N_INP = 0, 512, 1280, 1536, 1792, 2048, 2304, 2432, 2560
SM_DT, SM_F, SM_KR = 0, 8, 64
ALPHA = (2 * NL) ** 0.25
EPS = 1e-5
NEG = -1e30
LR, B1, B2, AEPS, WD, STEP = 0.001, 0.9, 0.999, 1e-08, 0.01, 10
VMEM_MB = 56


def _cp(*sem):
    return pltpu.CompilerParams(dimension_semantics=sem, vmem_limit_bytes=VMEM_MB << 20)


def _nn(a, b):
    return lax.dot_general(a, b, (((1,), (0,)), ((), ())), preferred_element_type=F32)


def _nt(a, b):
    return lax.dot_general(a, b, (((1,), (1,)), ((), ())), preferred_element_type=F32)


def _tn(a, b):
    return lax.dot_general(a, b, (((0,), (0,)), ((), ())), preferred_element_type=F32)


def _nn_hi(a, b):
    return lax.dot_general(a, b, (((1,), (0,)), ((), ())), precision=HI, preferred_element_type=F32)


def _row_tile(t):
    for d in range(640, 15, -16):
        if t % d == 0:
            return d
    raise ValueError(t)


def _sig(x):
    return 1.0 / (1.0 + jnp.exp(-x))


def _tri(lower=True):
    r = lax.broadcasted_iota(jnp.int32, (BLK, BLK), 0)
    c = lax.broadcasted_iota(jnp.int32, (BLK, BLK), 1)
    return (r >= c) if lower else (r <= c)


def build_h0(meta_full, x):
    s = x.shape[0]
    nb = s // BLK + 1

    def body(m_ref, x_ref, h_ref, hb_ref):
        i = pl.program_id(0)

        @pl.when(i == 0)
        def _():
            h = jnp.concatenate([jnp.zeros((PAD, D), F32), m_ref[...]], axis=0)
            h_ref[...] = h
            hb_ref[...] = h.astype(BF16)

        @pl.when(i > 0)
        def _():
            h_ref[...] = x_ref[...]
            hb_ref[...] = x_ref[...].astype(BF16)

    return pl.pallas_call(
        body, name="build_h0", grid=(nb,),
        in_specs=[pl.BlockSpec((N_META, D), lambda i: (0, 0)),
                  pl.BlockSpec((BLK, D), lambda i: (jnp.maximum(i - 1, 0), 0))],
        out_specs=[pl.BlockSpec((BLK, D), lambda i: (i, 0))] * 2,
        out_shape=[jax.ShapeDtypeStruct((nb * BLK, D), F32), jax.ShapeDtypeStruct((nb * BLK, D), BF16)],
        compiler_params=_cp("arbitrary"),
    )(meta_full, x)


FT = 256


def _layer_norm(r, gamma, beta):
    mu = jnp.mean(r, axis=1, keepdims=True)
    xc = r - mu
    var = jnp.mean(xc * xc, axis=1, keepdims=True)
    return xc * lax.rsqrt(var + EPS) * gamma + beta


def ffn_fwd_seq(x, ln_in, wg, wu, wd, ln_out):
    t = x.shape[0]
    f = wg.shape[0]
    nj, nr = f // FT, t // _row_tile(t)
    rc = t // nr
    plain = ln_in is None
    gi, bi = ln_out if plain else ln_in

    def body(x_hbm, gi_ref, bi_ref, go_ref, bo_ref, wg_ref, wu_ref, wd_ref, u_ref, v_ref, r_hbm, yb_hbm,
             acc, hbs, xbuf, sem_in, sem_out):
        j = pl.program_id(0)

        @pl.when(j == 0)
        def _():
            def fetch(k):
                return pltpu.make_async_copy(x_hbm.at[pl.ds(k * rc, rc)], xbuf.at[k % 2], sem_in.at[k % 2])

            fetch(0).start()
            for k in range(nr):
                if k + 1 < nr:
                    fetch(k + 1).start()
                fetch(k).wait()
                h = xbuf[k % 2]
                if not plain:
                    h = _layer_norm(h, gi_ref[...], bi_ref[...])
                acc[k * rc:(k + 1) * rc, :] = ALPHA * h
                hbs[k * rc:(k + 1) * rc, :] = h.astype(BF16)

        for k in range(nr):
            sl = slice(k * rc, (k + 1) * rc)
            h = hbs[sl, :]
            u = _nt(h, wg_ref[...])
            v = _nt(h, wu_ref[...])
            u_ref[sl, :] = u.astype(BF16)
            v_ref[sl, :] = v.astype(BF16)
            acc[sl, :] += _nn((0.5 * u * _sig(u) * v).astype(BF16), wd_ref[...])

        @pl.when(j == nj - 1)
        def _():
            r_cp = pltpu.make_async_copy(acc, r_hbm, sem_out.at[0])
            r_cp.start()
            for k in range(nr):
                sl = slice(k * rc, (k + 1) * rc)
                hbs[sl, :] = _layer_norm(acc[sl, :], go_ref[...], bo_ref[...]).astype(BF16)
            y_cp = pltpu.make_async_copy(hbs, yb_hbm, sem_out.at[1])
            y_cp.start()
            r_cp.wait()
            y_cp.wait()

    vec = pl.BlockSpec((1, D), lambda j: (0, 0))
    wsp = pl.BlockSpec((FT, D), lambda j: (j, 0))
    act = pl.BlockSpec((None, t, FT), lambda j: (j, 0, 0))
    return pl.pallas_call(
        body, name="ffn_fwd_seq", grid=(nj,),
        in_specs=[_ANY, vec, vec, vec, vec, wsp, wsp, wsp],
        out_specs=[act, act, _ANY, _ANY],
        out_shape=[jax.ShapeDtypeStruct((nj, t, FT), BF16), jax.ShapeDtypeStruct((nj, t, FT), BF16),
                   jax.ShapeDtypeStruct((t, D), F32), jax.ShapeDtypeStruct((t, D), BF16)],
        scratch_shapes=[pltpu.VMEM((t, D), F32), pltpu.VMEM((t, D), BF16), pltpu.VMEM((2, rc, D), F32),
                        pltpu.SemaphoreType.DMA((2,)), pltpu.SemaphoreType.DMA((2,))],
        compiler_params=_cp("arbitrary"),
    )(x, gi, bi, ln_out[0], ln_out[1], wg, wu, wd)


def ffn_bwd_seq(parts, r, gamma, hb, u, v, wg, wu, wd, after=None):
    nj, t, _ = u.shape
    f = nj * FT
    nr = t // _row_tile(t)
    rc = t // nr
    nc = t // BLK
    scales = [s for _, s in parts]
    npart = len(parts)
    extra = [] if after is None else [after]

    def body(*refs):
        refs = refs[len(extra):]
        p_hbm, refs = refs[:npart], refs[npart:]
        (r_hbm, g_ref, hb_hbm, u_ref, v_ref, wg_ref, wu_ref, wd_ref, dh_hbm, dwg_ref, dwu_ref, dwd_ref, dg_ref, db_ref,
         dfs, hbt, dft, dhacc, dus, dvs, acs, pbuf, rbuf, hbuf, sems, sem_out) = refs
        j = pl.program_id(0)

        @pl.when(j == 0)
        def _():
            def fetch(c):
                rows = pl.ds(c * BLK, BLK)
                cps = [pltpu.make_async_copy(p_hbm[p].at[rows], pbuf.at[c % 2, p], sems.at[c % 2, p]) for p in range(npart)]
                cps.append(pltpu.make_async_copy(r_hbm.at[rows], rbuf.at[c % 2], sems.at[c % 2, npart]))
                cps.append(pltpu.make_async_copy(hb_hbm.at[rows], hbuf.at[c % 2], sems.at[c % 2, npart + 1]))
                return cps

            for cp in fetch(0):
                cp.start()
            dg = jnp.zeros((1, D), F32)
            db = jnp.zeros((1, D), F32)
            for c in range(nc):
                if c + 1 < nc:
                    for cp in fetch(c + 1):
                        cp.start()
                for cp in fetch(c):
                    cp.wait()
                sl = slice(c * BLK, (c + 1) * BLK)
                dy = scales[0] * pbuf[c % 2, 0]
                for p in range(1, npart):
                    dy += scales[p] * pbuf[c % 2, p]
                rr = rbuf[c % 2]
                xc = rr - jnp.mean(rr, axis=1, keepdims=True)
                rstd = lax.rsqrt(jnp.mean(xc * xc, axis=1, keepdims=True) + EPS)
                xh = xc * rstd
                dxh = dy * g_ref[...]
                dr = rstd * (dxh - jnp.mean(dxh, axis=1, keepdims=True) - xh * jnp.mean(dxh * xh, axis=1, keepdims=True))
                dg += jnp.sum(dy * xh, axis=0, keepdims=True)
                db += jnp.sum(dy, axis=0, keepdims=True)
                dhacc[sl, :] = ALPHA * dr
                dfc = (0.5 * dr).astype(BF16)
                dfs[sl, :] = dfc
                dft[:, sl] = dfc.T
                hbt[:, sl] = hbuf[c % 2].T
            dg_ref[...] = dg
            db_ref[...] = db

        for k in range(nr):
            sl = slice(k * rc, (k + 1) * rc)
            da = _nt(dfs[sl, :], wd_ref[...])
            uu = u_ref[sl, :].astype(F32)
            vv = v_ref[sl, :].astype(F32)
            sg = _sig(uu)
            du = (da * vv * (sg * (1.0 + uu * (1.0 - sg)))).astype(BF16)
            dv = (da * uu * sg).astype(BF16)
            dus[sl, :] = du
            dvs[sl, :] = dv
            acs[sl, :] = (uu * sg * vv).astype(BF16)
            dhacc[sl, :] += _nn(du, wg_ref[...]) + _nn(dv, wu_ref[...])
        dwg_ref[...] = _nn(hbt[...], dus[...]).astype(BF16).T
        dwu_ref[...] = _nn(hbt[...], dvs[...]).astype(BF16).T
        dwd_ref[...] = _nn(dft[...], acs[...]).astype(BF16).T

        @pl.when(j == nj - 1)
        def _():
            cp = pltpu.make_async_copy(dhacc, dh_hbm, sem_out.at[0])
            cp.start()
            cp.wait()

    vec = pl.BlockSpec((1, D), lambda j: (0, 0))
    wsp = pl.BlockSpec((FT, D), lambda j: (j, 0))
    act = pl.BlockSpec((None, t, FT), lambda j: (j, 0, 0))
    return pl.pallas_call(
        body, name="ffn_bwd_seq", grid=(nj,),
        in_specs=[_ANY] * (len(extra) + npart + 1) + [vec, _ANY, act, act, wsp, wsp, wsp],
        out_specs=[_ANY, wsp, wsp, wsp, vec, vec],
        out_shape=[jax.ShapeDtypeStruct((t, D), F32)] + [jax.ShapeDtypeStruct((f, D), BF16)] * 3
        + [jax.ShapeDtypeStruct((1, D), F32)] * 2,
        scratch_shapes=[pltpu.VMEM((t, D), BF16), pltpu.VMEM((D, t), BF16), pltpu.VMEM((D, t), BF16),
                        pltpu.VMEM((t, D), F32), pltpu.VMEM((t, FT), BF16), pltpu.VMEM((t, FT), BF16),
                        pltpu.VMEM((t, FT), BF16), pltpu.VMEM((2, npart, BLK, D), F32), pltpu.VMEM((2, BLK, D), F32),
                        pltpu.VMEM((2, BLK, D), BF16), pltpu.SemaphoreType.DMA((2, npart + 2)),
                        pltpu.SemaphoreType.DMA((1,))],
        compiler_params=_cp("arbitrary"),
    )(*extra, *[p for p, _ in parts], r, gamma, hb, u, v, wg, wu, wd)


def mm_res_ln(a, b, x, ln_in, ln_out):
    t, k = a.shape
    tm = _row_tile(t)

    def body(a_ref, b_ref, x_ref, gi_ref, bi_ref, go_ref, bo_ref, r_ref, yb_ref):
        r = ALPHA * _layer_norm(x_ref[...], gi_ref[...], bi_ref[...]) + _nn(a_ref[...], b_ref[...])
        r_ref[...] = r
        yb_ref[...] = _layer_norm(r, go_ref[...], bo_ref[...]).astype(BF16)

    row = pl.BlockSpec((tm, D), lambda i: (i, 0))
    vec = pl.BlockSpec((1, D), lambda i: (0, 0))
    return pl.pallas_call(
        body, name="mm_res_ln", grid=(t // tm,),
        in_specs=[pl.BlockSpec((tm, k), lambda i: (i, 0)), pl.BlockSpec((k, D), lambda i: (0, 0)), row, vec, vec, vec, vec],
        out_specs=[row, row],
        out_shape=[jax.ShapeDtypeStruct((t, D), F32), jax.ShapeDtypeStruct((t, D), BF16)],
        compiler_params=_cp("arbitrary"),
    )(a, b, x, ln_in[0], ln_in[1], ln_out[0], ln_out[1])


def mm_nn(a, b):
    t, k = a.shape
    n = tn = b.shape[1]
    tm = _row_tile(t)

    def body(a_ref, b_ref, o_ref):
        o_ref[...] = _nn(a_ref[...], b_ref[...])

    return pl.pallas_call(
        body, name="mm_nn", grid=(t // tm, n // tn),
        in_specs=[pl.BlockSpec((tm, k), lambda i, j: (i, 0)), pl.BlockSpec((k, tn), lambda i, j: (0, j))],
        out_specs=pl.BlockSpec((tm, tn), lambda i, j: (i, j)),
        out_shape=jax.ShapeDtypeStruct((t, n), F32),
        compiler_params=_cp("arbitrary", "arbitrary"),
    )(a, b)


def mm_nt_reduce(pairs, n):
    g, t, _ = pairs[0][0].shape
    tm = _row_tile(t)
    npair = len(pairs)

    def body(*refs):
        o_ref = refs[-1]
        gi = pl.program_id(1)
        tot = _nt(refs[0][...], refs[1][...])
        for p in range(1, npair):
            tot += _nt(refs[2 * p][...], refs[2 * p + 1][...])

        @pl.when(gi == 0)
        def _():
            o_ref[...] = tot

        @pl.when(gi > 0)
        def _():
            o_ref[...] += tot

    in_specs, args = [], []
    for x, w in pairs:
        k = x.shape[2]
        in_specs += [pl.BlockSpec((None, tm, k), lambda i, gi: (gi, i, 0)),
                     pl.BlockSpec((None, n, k), lambda i, gi: (gi, 0, 0))]
        args += [x, w]
    return pl.pallas_call(
        body, name="mm_nt_reduce", grid=(t // tm, g),
        in_specs=in_specs, out_specs=pl.BlockSpec((tm, n), lambda i, gi: (i, 0)),
        out_shape=jax.ShapeDtypeStruct((t, n), F32),
        compiler_params=_cp("arbitrary", "arbitrary"),
    )(*args)


def mm_tn(x, y, out_dtype=BF16):
    gx, t, k = x.shape
    gy, _, n = y.shape
    g = max(gx, gy)
    tm = _row_tile(t)
    nt = t // tm

    def body(x_ref, y_ref, o_ref, acc):
        i = pl.program_id(1)

        @pl.when(i == 0)
        def _():
            acc[...] = jnp.zeros_like(acc)

        acc[...] += _tn(x_ref[...], y_ref[...])

        @pl.when(i == nt - 1)
        def _():
            o_ref[...] = acc[...].astype(out_dtype)

    return pl.pallas_call(
        body, name="mm_tn", grid=(g, nt),
        in_specs=[pl.BlockSpec((None, tm, k), (lambda gi, i: (gi, i, 0)) if gx > 1 else (lambda gi, i: (0, i, 0))),
                  pl.BlockSpec((None, tm, n), (lambda gi, i: (gi, i, 0)) if gy > 1 else (lambda gi, i: (0, i, 0)))],
        out_specs=pl.BlockSpec((None, k, n), lambda gi, i: (gi, 0, 0)),
        out_shape=jax.ShapeDtypeStruct((g, k, n), out_dtype),
        scratch_shapes=[pltpu.VMEM((k, n), F32)],
        compiler_params=_cp("arbitrary", "arbitrary"),
    )(x, y)


def ln_bwd(parts, r, gamma, out_scale, after=None):
    t = r.shape[0]
    tm = _row_tile(t)
    scales = [s for _, s in parts]
    npart = len(parts)
    extra = [] if after is None else [after]

    def body(*refs):
        refs = refs[len(extra):]
        r_ref, g_ref = refs[npart], refs[npart + 1]
        dr_ref, drb_ref, dg_ref, db_ref = refs[npart + 2:]
        i = pl.program_id(0)
        dy = scales[0] * refs[0][...]
        for p in range(1, npart):
            dy += scales[p] * refs[p][...]
        rr = r_ref[...]
        mu = jnp.mean(rr, axis=1, keepdims=True)
        xc = rr - mu
        rstd = lax.rsqrt(jnp.mean(xc * xc, axis=1, keepdims=True) + EPS)
        xh = xc * rstd
        dxh = dy * g_ref[...]
        m1 = jnp.mean(dxh, axis=1, keepdims=True)
        m2 = jnp.mean(dxh * xh, axis=1, keepdims=True)
        dr = rstd * (dxh - m1 - xh * m2)
        dr_ref[...] = dr
        drb_ref[...] = (out_scale * dr).astype(BF16)
        dg = jnp.sum(dy * xh, axis=0, keepdims=True)
        db = jnp.sum(dy, axis=0, keepdims=True)

        @pl.when(i == 0)
        def _():
            dg_ref[...] = dg
            db_ref[...] = db

        @pl.when(i > 0)
        def _():
            dg_ref[...] += dg
            db_ref[...] += db

    row = pl.BlockSpec((tm, D), lambda i: (i, 0))
    vec = pl.BlockSpec((1, D), lambda i: (0, 0))
    return pl.pallas_call(
        body, name="ln_bwd", grid=(t // tm,),
        in_specs=[_ANY] * len(extra) + [row] * (npart + 1) + [vec],
        out_specs=[row, row, vec, vec],
        out_shape=[jax.ShapeDtypeStruct((t, D), F32), jax.ShapeDtypeStruct((t, D), BF16),
                   jax.ShapeDtypeStruct((1, D), F32), jax.ShapeDtypeStruct((1, D), F32)],
        compiler_params=_cp("arbitrary"),
    )(*extra, *[p for p, _ in parts], r, gamma)


def loss_head(r, ln, target):
    t = r.shape[0]
    nb = t // BLK

    def body(r_ref, g_ref, b_ref, t_ref, dy_ref, l_ref):
        i = pl.program_id(0)

        @pl.when(i == 0)
        def _():
            dy_ref[...] = jnp.zeros_like(dy_ref)
            l_ref[...] = jnp.zeros_like(l_ref)

        @pl.when(i > 0)
        def _():
            err = _layer_norm(r_ref[...], g_ref[...], b_ref[...]) - t_ref[...]
            dy_ref[...] = err * (1.0 / D)
            l_ref[...] += (0.5 / D) * jnp.sum(err * err, keepdims=True)

    vec = pl.BlockSpec((1, D), lambda i: (0, 0))
    return pl.pallas_call(
        body, name="loss_head", grid=(nb,),
        in_specs=[pl.BlockSpec((BLK, D), lambda i: (i, 0)), vec, vec,
                  pl.BlockSpec((BLK, D), lambda i: (jnp.maximum(i - 1, 0), 0))],
        out_specs=[pl.BlockSpec((BLK, D), lambda i: (i, 0)), pl.BlockSpec((1, 1), lambda i: (0, 0))],
        out_shape=[jax.ShapeDtypeStruct((t, D), F32), jax.ShapeDtypeStruct((1, 1), F32)],
        compiler_params=_cp("arbitrary"),
    )(r, ln[0], ln[1], target)


def split_dh0(dh0, after=None):
    t = dh0.shape[0]
    nb = t // BLK
    extra = [] if after is None else [after]

    def body(*refs):
        a_ref, gx_ref, gm_ref = refs[len(extra):]
        i = pl.program_id(0)
        tot = a_ref[...]

        @pl.when(i == 0)
        def _():
            gm_ref[...] = tot[PAD:, :]

        @pl.when(i > 0)
        def _():
            gx_ref[...] = tot

    blk = pl.BlockSpec((BLK, D), lambda i: (i, 0))
    return pl.pallas_call(
        body, name="split_dh0", grid=(nb,),
        in_specs=[_ANY] * len(extra) + [blk],
        out_specs=[pl.BlockSpec((BLK, D), lambda i: (jnp.maximum(i - 1, 0), 0)),
                   pl.BlockSpec((N_META, D), lambda i: (0, 0))],
        out_shape=[jax.ShapeDtypeStruct((t - BLK, D), F32), jax.ShapeDtypeStruct((N_META, D), F32)],
        compiler_params=_cp("arbitrary"),
    )(*extra, dh0)


def _valid_rows(nrows, first_row):
    return (first_row + lax.broadcasted_iota(jnp.int32, (nrows, 1), 0)) >= PAD


def conv_fwd(proj, conv_w, conv_b):
    t = proj.shape[0]
    c0 = C_XBC // BLK

    def body(x_ref, w_ref, b_ref, o_ref):
        ok = _valid_rows(t, 0)
        x = jnp.where(ok, x_ref[...], 0.0)
        w = w_ref[...]
        acc = b_ref[...] + w[CONV_K - 1:CONV_K, :] * x
        for s in range(1, CONV_K):
            acc += w[CONV_K - 1 - s:CONV_K - s, :] * pltpu.roll(x, s, 0)
        o_ref[...] = jnp.where(ok, acc * _sig(acc), 0.0)

    return pl.pallas_call(
        body, name="conv_fwd", grid=(CONV_D // BLK,),
        in_specs=[pl.BlockSpec((t, BLK), lambda j: (0, c0 + j)),
                  pl.BlockSpec((CONV_K, BLK), lambda j: (0, j)), pl.BlockSpec((1, BLK), lambda j: (0, j))],
        out_specs=pl.BlockSpec((t, BLK), lambda j: (0, j)),
        out_shape=jax.ShapeDtypeStruct((t, CONV_D), F32),
        compiler_params=_cp("arbitrary"),
    )(proj, conv_w, conv_b)


def conv_bwd(dxa, proj, conv_w, conv_b):
    t = proj.shape[0]
    c0 = C_XBC // BLK

    def body(d_ref, x_ref, w_ref, b_ref, dx_ref, dw_ref, db_ref):
        ok = _valid_rows(t, 0)
        x = jnp.where(ok, x_ref[...], 0.0)
        w = w_ref[...]
        xs = [x] + [pltpu.roll(x, s, 0) for s in range(1, CONV_K)]
        acc = b_ref[...] + w[CONV_K - 1:CONV_K, :] * x
        for s in range(1, CONV_K):
            acc += w[CONV_K - 1 - s:CONV_K - s, :] * xs[s]
        sg = _sig(acc)
        dxc = jnp.where(ok, d_ref[...] * (sg * (1.0 + acc * (1.0 - sg))), 0.0)
        db_ref[...] = jnp.sum(dxc, axis=0, keepdims=True)
        dw_ref[...] = jnp.concatenate(
            [jnp.sum(dxc * xs[CONV_K - 1 - k], axis=0, keepdims=True) for k in range(CONV_K)], axis=0)
        dx = w[CONV_K - 1:CONV_K, :] * dxc
        for s in range(1, CONV_K):
            dx += w[CONV_K - 1 - s:CONV_K - s, :] * pltpu.roll(dxc, t - s, 0)
        dx_ref[...] = jnp.where(ok, dx, 0.0)

    col = pl.BlockSpec((t, BLK), lambda j: (0, j))
    return pl.pallas_call(
        body, name="conv_bwd", grid=(CONV_D // BLK,),
        in_specs=[col, pl.BlockSpec((t, BLK), lambda j: (0, c0 + j)),
                  pl.BlockSpec((CONV_K, BLK), lambda j: (0, j)), pl.BlockSpec((1, BLK), lambda j: (0, j))],
        out_specs=[col, pl.BlockSpec((CONV_K, BLK), lambda j: (0, j)), pl.BlockSpec((1, BLK), lambda j: (0, j))],
        out_shape=[jax.ShapeDtypeStruct((t, CONV_D), F32), jax.ShapeDtypeStruct((CONV_K, CONV_D), F32),
                   jax.ShapeDtypeStruct((1, CONV_D), F32)],
        compiler_params=_cp("arbitrary"),
    )(dxa, proj, conv_w, conv_b)


def _softplus(x):
    return jnp.maximum(x, 0.0) + jnp.log(1.0 + jnp.exp(-jnp.abs(x)))


GW = SSD_D // SSD_G
HPG = SSD_H // SSD_G


def _head_expand():
    r = lax.broadcasted_iota(jnp.int32, (BLK, SSD_D), 0)
    c = lax.broadcasted_iota(jnp.int32, (BLK, SSD_D), 1)
    rt = lax.broadcasted_iota(jnp.int32, (SSD_D, BLK), 0)
    ct = lax.broadcasted_iota(jnp.int32, (SSD_D, BLK), 1)
    return (c // SSD_P == r).astype(F32), (rt // SSD_P == ct).astype(F32)


def _ssd_chunk(xa, sm, dtb, alog, dskip, ok, sp):
    e, et = _head_expand()
    dt = jnp.where(ok, _softplus(sm + dtb), 0.0)
    amat = -jnp.exp(alog)
    tri = _tri()
    ac = _nn_hi(tri.astype(F32), dt * amat)
    act = ac.T
    ace, dte, dse = _nn_hi(ac, e), _nn_hi(dt, e), _nn_hi(dskip, e)
    laste = ace[BLK - 1:BLK, :]
    ee, dece, gle = jnp.exp(ace), jnp.exp(laste - ace), jnp.exp(laste)
    xs = xa[:, :SSD_D]
    xdt = xs * dte
    decx = dece * xdt
    xdtb = xdt.astype(BF16)
    d = dict(e=e, et=et, dt=dt, amat=amat, tri=tri, ac=ac, act=act, dte=dte, dse=dse, ee=ee, dece=dece, gle=gle, xs=xs,
             xdt=xdt, xdtb=xdtb, decx=decx, bg=[], cg=[], cb=[], yo=[], seg=[], m=[], new_s=[])
    ys = []
    for g in range(SSD_G):
        cols = slice(GW * g, GW * (g + 1))
        bg = xa[:, SSD_D + SSD_N * g:SSD_D + SSD_N * (g + 1)].astype(BF16)
        cg = xa[:, SSD_D + SSD_G * SSD_N + SSD_N * g:SSD_D + SSD_G * SSD_N + SSD_N * (g + 1)].astype(BF16)
        spg = sp[:, cols]
        sloc = _tn(bg, decx[:, cols].astype(BF16))
        yo = _nn(cg, spg.astype(BF16)) * ee[:, cols]
        cb = _nt(cg, bg)
        d["new_s"].append(gle[:, cols] * spg + sloc)
        yds = []
        for h in range(HPG * g, HPG * (g + 1)):
            seg = jnp.where(tri, jnp.exp(jnp.minimum(ac[:, h:h + 1] - act[h:h + 1, :], 0.0)), 0.0)
            m = cb * seg
            yds.append(_nn(m.astype(BF16), xdtb[:, SSD_P * h:SSD_P * (h + 1)]))
            d["seg"].append(seg)
            d["m"].append(m)
        ys.append(jnp.concatenate(yds, axis=1) + yo)
        for k, val in (("bg", bg), ("cg", cg), ("cb", cb), ("yo", yo)):
            d[k].append(val)
    d["y"] = jnp.concatenate(ys, axis=1) + dse * xs
    return d


def ssd_fwd(xa, proj, dtb, alog, dskip, normg):
    t = xa.shape[0]
    nb = t // BLK
    gw = SSD_D // SSD_G

    def body(xa_ref, z_ref, sm_ref, dtb_ref, al_ref, ds_ref, ng_ref, y_ref, sp_ref, st):
        c = pl.program_id(0)

        @pl.when(c == 0)
        def _():
            st[...] = jnp.zeros_like(st)

        ok = _valid_rows(BLK, c * BLK)
        sp = st[...]
        sp_ref[...] = sp
        d = _ssd_chunk(xa_ref[...], sm_ref[...], dtb_ref[...], al_ref[...], ds_ref[...], ok, sp)
        st[...] = jnp.concatenate(d["new_s"], axis=1)
        y = d["y"]
        z = z_ref[...]
        yg = y * (z * _sig(z))
        outs = []
        for g in range(SSD_G):
            v = yg[:, gw * g:gw * (g + 1)]
            outs.append(v * lax.rsqrt(jnp.mean(v * v, axis=1, keepdims=True) + EPS))
        y_ref[...] = (jnp.concatenate(outs, axis=1) * ng_ref[...]).astype(BF16)

    vec = pl.BlockSpec((1, BLK), lambda c: (0, 0))
    return pl.pallas_call(
        body, name="ssd_fwd", grid=(nb,),
        in_specs=[pl.BlockSpec((BLK, CONV_D), lambda c: (c, 0)),
                  pl.BlockSpec((BLK, SSD_D), lambda c: (c, C_Z // SSD_D)),
                  pl.BlockSpec((BLK, BLK), lambda c: (c, C_SM // BLK)),
                  vec, vec, vec, pl.BlockSpec((1, SSD_D), lambda c: (0, 0))],
        out_specs=[pl.BlockSpec((BLK, SSD_D), lambda c: (c, 0)),
                   pl.BlockSpec((None, SSD_N, SSD_D), lambda c: (c, 0, 0))],
        out_shape=[jax.ShapeDtypeStruct((t, SSD_D), BF16), jax.ShapeDtypeStruct((nb, SSD_N, SSD_D), F32)],
        scratch_shapes=[pltpu.VMEM((SSD_N, SSD_D), F32)],
        compiler_params=_cp("arbitrary"),
    )(xa, proj, proj, dtb, alog, dskip, normg)


def _lane_put(col, lane):
    li = lax.broadcasted_iota(jnp.int32, (col.shape[0], BLK), 1)
    return jnp.where(li == lane, col, 0.0)


def ssd_bwd(dmix, xa, proj, sprev, dtb, alog, dskip, normg):
    t = xa.shape[0]
    nb = t // BLK
    gw = SSD_D // SSD_G
    rev = lambda c: nb - 1 - c

    def body(dy_ref, xa_ref, z_ref, sm_ref, sp_ref, dtb_ref, al_ref, ds_ref, ng_ref,
             dxa_ref, dz_ref, dsm_ref, dng_ref, dds_ref, dal_ref, ddtb_ref, dst):
        c = pl.program_id(0)

        @pl.when(c == 0)
        def _():
            dst[...] = jnp.zeros_like(dst)
            dng_ref[...] = jnp.zeros_like(dng_ref)
            dds_ref[...] = jnp.zeros_like(dds_ref)
            dal_ref[...] = jnp.zeros_like(dal_ref)
            ddtb_ref[...] = jnp.zeros_like(ddtb_ref)

        ok = _valid_rows(BLK, rev(c) * BLK)
        sm = sm_ref[...]
        sp = sp_ref[...]
        d = _ssd_chunk(xa_ref[...], sm, dtb_ref[...], al_ref[...], ds_ref[...], ok, sp)
        dt, amat, ac, act, tri, et, xs, xdt = (d[k] for k in ("dt", "amat", "ac", "act", "tri", "et", "xs", "xdt"))
        rowi = lax.broadcasted_iota(jnp.int32, (BLK, 1), 0)
        y = d["y"]
        z = z_ref[...]
        sgz = _sig(z)
        siluz = z * sgz
        yg = y * siluz
        dout = dy_ref[...]
        ng = ng_ref[...]
        dygs, xhs = [], []
        for g in range(SSD_G):
            v = yg[:, gw * g:gw * (g + 1)]
            rr = lax.rsqrt(jnp.mean(v * v, axis=1, keepdims=True) + EPS)
            xh = v * rr
            dxh = dout[:, gw * g:gw * (g + 1)] * ng[:, gw * g:gw * (g + 1)]
            dygs.append(rr * (dxh - xh * jnp.mean(dxh * xh, axis=1, keepdims=True)))
            xhs.append(xh)
        dyg = jnp.concatenate(dygs, axis=1)
        dng_ref[...] += jnp.sum(dout * jnp.concatenate(xhs, axis=1), axis=0, keepdims=True)
        dy = dyg * siluz
        dz_ref[...] = dyg * y * (sgz * (1.0 + z * (1.0 - sgz)))

        triu = _tri(lower=False)
        dyb = dy.astype(BF16)
        dsn = dst[...]
        dds_ref[...] += _nn_hi(jnp.sum(dy * xs, axis=0, keepdims=True), et)
        dac_all = _nn_hi(dy * jnp.concatenate(d["yo"], axis=1), et)
        dyo = (dy * d["ee"]).astype(BF16)
        gl = jnp.exp(ac[BLK - 1:BLK, :])
        dlast = _nn_hi(jnp.sum(dsn * sp, axis=0, keepdims=True), et) * gl
        bds, db_g, dc_g, dxdt_i, new_dst = [], [], [], [], []
        for g in range(SSD_G):
            cols = slice(GW * g, GW * (g + 1))
            bg, cg = d["bg"][g], d["cg"][g]
            dsng = dsn[:, cols].astype(BF16)
            dc = _nt(dyo[:, cols], sp[:, cols].astype(BF16))
            new_dst.append(_tn(cg, dyo[:, cols]) + d["gle"][:, cols] * dsn[:, cols])
            bds.append(_nn(bg, dsng))
            db = _nt(d["decx"][:, cols].astype(BF16), dsng)
            cbt = _nt(bg, cg)
            dcb = jnp.zeros((BLK, BLK), F32)
            for h in range(HPG * g, HPG * (g + 1)):
                hc = slice(SSD_P * h, SSD_P * (h + 1))
                dm = _nt(dyb[:, hc], d["xdtb"][:, hc])
                dcb += dm * d["seg"][h]
                w = dm * d["m"][h]
                dac_all += _lane_put(jnp.sum(w, axis=1, keepdims=True) - jnp.sum(w.T, axis=1, keepdims=True), h)
                segt = jnp.where(triu, jnp.exp(jnp.minimum(act[h:h + 1, :] - ac[:, h:h + 1], 0.0)), 0.0)
                dxdt_i.append(_nn((cbt * segt).astype(BF16), dyb[:, hc]))
            dcbb = dcb.astype(BF16)
            dc_g.append(dc + _nn(dcbb, bg))
            db_g.append(db + _tn(dcbb, cg))
        dst[...] = jnp.concatenate(new_dst, axis=1)
        bds = jnp.concatenate(bds, axis=1)
        tdec = jnp.exp(ac[BLK - 1:BLK, :] - ac) * _nn_hi(xdt * bds, et)
        dlast += jnp.sum(tdec, axis=0, keepdims=True)
        dac_all += jnp.where(rowi == BLK - 1, dlast, 0.0) - tdec
        dxdt = d["dece"] * bds + jnp.concatenate(dxdt_i, axis=1)
        da = _nn_hi(triu.astype(F32), dac_all)
        ddt = _nn_hi(dxdt * xs, et) + da * amat
        dal_ref[...] += jnp.sum(da * dt, axis=0, keepdims=True) * amat
        ddtr = jnp.where(ok, ddt * _sig(sm + dtb_ref[...]), 0.0)
        ddtb_ref[...] += jnp.sum(ddtr, axis=0, keepdims=True)
        dsm_ref[...] = ddtr
        dxs = d["dse"] * dy + dxdt * d["dte"]
        dxa_ref[...] = jnp.where(ok, jnp.concatenate([dxs] + db_g + dc_g, axis=1), 0.0)

    vec = pl.BlockSpec((1, BLK), lambda c: (0, 0))
    nvec = pl.BlockSpec((1, SSD_D), lambda c: (0, 0))
    return pl.pallas_call(
        body, name="ssd_bwd", grid=(nb,),
        in_specs=[pl.BlockSpec((BLK, SSD_D), lambda c: (rev(c), 0)),
                  pl.BlockSpec((BLK, CONV_D), lambda c: (rev(c), 0)),
                  pl.BlockSpec((BLK, SSD_D), lambda c: (rev(c), C_Z // SSD_D)),
                  pl.BlockSpec((BLK, BLK), lambda c: (rev(c), C_SM // BLK)),
                  pl.BlockSpec((None, SSD_N, SSD_D), lambda c: (rev(c), 0, 0)),
                  vec, vec, vec, nvec],
        out_specs=[pl.BlockSpec((BLK, CONV_D), lambda c: (rev(c), 0)),
                   pl.BlockSpec((BLK, SSD_D), lambda c: (rev(c), 0)),
                   pl.BlockSpec((BLK, BLK), lambda c: (rev(c), 0)),
                   nvec, vec, vec, vec],
        out_shape=[jax.ShapeDtypeStruct((t, CONV_D), F32), jax.ShapeDtypeStruct((t, SSD_D), F32),
                   jax.ShapeDtypeStruct((t, BLK), F32), jax.ShapeDtypeStruct((1, SSD_D), F32),
                   jax.ShapeDtypeStruct((1, BLK), F32), jax.ShapeDtypeStruct((1, BLK), F32),
                   jax.ShapeDtypeStruct((1, BLK), F32)],
        scratch_shapes=[pltpu.VMEM((SSD_N, SSD_D), F32)],
        compiler_params=_cp("arbitrary"),
    )(dmix, xa, proj, proj, sprev, dtb, alog, dskip, normg)


def _segments(nb):
    cuts = list(range(0, nb, 2)) + [nb]
    return list(zip(cuts[:-1], cuts[1:]))


def attn_fwd(q, k, v, qcol, kcol, vcol, nh, dq, dv, scale, c_col=None, c_row=None, lane0=0):
    t = q.shape[0]
    tq = BLK
    use_bias = c_col is not None

    def body(*refs):
        if use_bias:
            q_ref, k_ref, v_ref, cc_ref, cr_ref, o_ref, l_ref = refs
        else:
            q_ref, k_ref, v_ref, o_ref, l_ref = refs
        i = pl.program_id(0)
        rowg = i * tq + lax.broadcasted_iota(jnp.int32, (tq, 1), 0)

        def tile(tk):
            col = lax.broadcasted_iota(jnp.int32, (1, tk), 1)
            mask = (col <= rowg) & (col >= PAD)
            outs = []
            lse = jnp.zeros((tq, BLK), F32)
            for h in range(nh):
                s = _nt(q_ref[:, dq * h:dq * (h + 1)].astype(BF16), k_ref[0:tk, dq * h:dq * (h + 1)].astype(BF16)) * scale
                if use_bias:
                    s = s + (cc_ref[:, lane0 + h:lane0 + h + 1] - cr_ref[h:h + 1, 0:tk])
                s = jnp.where(mask, s, NEG)
                m = jnp.max(s, axis=1, keepdims=True)
                p = jnp.exp(s - m)
                l = jnp.sum(p, axis=1, keepdims=True)
                outs.append(_nn(p.astype(BF16), v_ref[0:tk, dv * h:dv * (h + 1)].astype(BF16)) / l)
                lse += _lane_put(m + jnp.log(l), h)
            o_ref[...] = jnp.concatenate(outs, axis=1).astype(BF16)
            l_ref[...] = lse.T[0:8, :]

        for t0, t1 in _segments(t // tq):
            pl.when((i >= t0) & (i < t1))(functools.partial(tile, t1 * BLK))

    in_specs = [pl.BlockSpec((tq, nh * dq), lambda i: (i, qcol)),
                pl.BlockSpec((t, nh * dq), lambda i: (0, kcol)),
                pl.BlockSpec((t, nh * dv), lambda i: (0, vcol))]
    args = [q, k, v]
    if use_bias:
        in_specs += [pl.BlockSpec((tq, BLK), lambda i: (i, 0)), pl.BlockSpec((8, t), lambda i: (0, 0))]
        args += [c_col, c_row]
    return pl.pallas_call(
        body, name="attn_fwd", grid=(t // tq,),
        in_specs=in_specs,
        out_specs=[pl.BlockSpec((tq, nh * dv), lambda i: (i, 0)), pl.BlockSpec((8, tq), lambda i: (0, i))],
        out_shape=[jax.ShapeDtypeStruct((t, nh * dv), BF16), jax.ShapeDtypeStruct((8, t), F32)],
        compiler_params=_cp("arbitrary"),
    )(*args)


def attn_bwd(q, k, v, do, lse_row, o, qcol, kcol, vcol, docol, ocol, nh, dq, dv, scale, c_col=None, c_row=None, lane0=0):
    t = q.shape[0]
    tq = BLK
    use_bias = c_col is not None
    nq = t // tq

    def body(*refs):
        if use_bias:
            (q_ref, k_ref, v_ref, do_ref, l_ref, o_ref, cc_ref, cr_ref, dq_ref, dk_ref, dv_ref, dcq_ref, dck_ref,
             kt, ckb, dacc) = refs
        else:
            q_ref, k_ref, v_ref, do_ref, l_ref, o_ref, dq_ref, dk_ref, dv_ref, kt = refs
        i = pl.program_id(0)

        @pl.when(i == 0)
        def _():
            kt[...] = k_ref[...].astype(BF16).T
            dk_ref[...] = jnp.zeros_like(dk_ref)
            dv_ref[...] = jnp.zeros_like(dv_ref)
            if use_bias:
                dacc[...] = jnp.zeros_like(dacc)
                for h in range(nh):
                    ckb[h] = jnp.broadcast_to(cc_ref[:, lane0 + h:lane0 + h + 1], (t, BLK))

        qry = i * tq + lax.broadcasted_iota(jnp.int32, (1, tq), 1)
        dot = (do_ref[...].astype(F32) * o_ref[...].astype(F32)).T

        def tile(tk):
            key = lax.broadcasted_iota(jnp.int32, (tk, 1), 0)
            mask = (key <= qry) & (key >= PAD)
            dqts, dcqs = [], []
            for h in range(nh):
                qh = q_ref[:, dq * h:dq * (h + 1)].astype(BF16)
                kh = k_ref[0:tk, dq * h:dq * (h + 1)].astype(BF16)
                vh = v_ref[0:tk, dv * h:dv * (h + 1)].astype(BF16)
                doh = do_ref[:, dv * h:dv * (h + 1)].astype(BF16)
                delta = jnp.sum(dot[dv * h:dv * (h + 1), :], axis=0, keepdims=True)
                st = _nt(kh, qh) * scale
                if use_bias:
                    st = st + (cr_ref[h:h + 1, :] - ckb[h, 0:tk, :])
                pt = jnp.exp(jnp.where(mask, st, NEG) - l_ref[h:h + 1, :])
                dst = pt * (_nt(vh, doh) - delta)
                dsb = dst.astype(BF16)
                dk_ref[0:tk, dq * h:dq * (h + 1)] += _nn(dsb, qh) * scale
                dv_ref[0:tk, dv * h:dv * (h + 1)] += _nn(pt.astype(BF16), doh)
                dqts.append(_nn(kt[dq * h:dq * (h + 1), 0:tk], dsb))
                if use_bias:
                    dcqs.append(jnp.sum(dst, axis=0, keepdims=True))
                    dacc[h, 0:tk, :] += dst
            dq_ref[...] = jnp.concatenate(dqts, axis=0).T * scale
            if use_bias:
                dcq_ref[...] = jnp.concatenate(dcqs + [jnp.zeros((8 - nh, tq), F32)], axis=0)

        for t0, t1 in _segments(nq):
            pl.when((i >= t0) & (i < t1))(functools.partial(tile, t1 * BLK))

        if use_bias:
            @pl.when(i == nq - 1)
            def _():
                lane = lax.broadcasted_iota(jnp.int32, (1, BLK), 1)
                tot = jnp.zeros((t, BLK), F32)
                for h in range(nh):
                    tot += jnp.where(lane == lane0 + h, jnp.sum(dacc[h], axis=1, keepdims=True), 0.0)
                dck_ref[...] = tot

    keys_q = pl.BlockSpec((t, nh * dq), lambda i: (0, 0))
    keys_v = pl.BlockSpec((t, nh * dv), lambda i: (0, 0))
    keys_c = pl.BlockSpec((t, BLK), lambda i: (0, 0))
    qrow = pl.BlockSpec((8, tq), lambda i: (0, i))
    in_specs = [pl.BlockSpec((tq, nh * dq), lambda i: (i, qcol)),
                pl.BlockSpec((t, nh * dq), lambda i: (0, kcol)),
                pl.BlockSpec((t, nh * dv), lambda i: (0, vcol)),
                pl.BlockSpec((tq, nh * dv), lambda i: (i, docol)),
                qrow,
                pl.BlockSpec((tq, nh * dv), lambda i: (i, ocol))]
    args = [q, k, v, do, lse_row, o]
    out_specs = [pl.BlockSpec((tq, nh * dq), lambda i: (i, 0)), keys_q, keys_v]
    out_shape = [jax.ShapeDtypeStruct((t, nh * dq), F32), jax.ShapeDtypeStruct((t, nh * dq), F32),
                 jax.ShapeDtypeStruct((t, nh * dv), F32)]
    scratch = [pltpu.VMEM((nh * dq, t), BF16)]
    if use_bias:
        in_specs += [keys_c, qrow]
        args += [c_col, c_row]
        out_specs += [qrow, keys_c]
        out_shape += [jax.ShapeDtypeStruct((8, t), F32), jax.ShapeDtypeStruct((t, BLK), F32)]
        scratch += [pltpu.VMEM((nh, t, BLK), F32), pltpu.VMEM((nh, t, BLK), F32)]
    return pl.pallas_call(
        body, name="attn_bwd", grid=(nq,),
        in_specs=in_specs, out_specs=out_specs, out_shape=out_shape, scratch_shapes=scratch,
        compiler_params=_cp("arbitrary"),
    )(*args)


def fox_pre(proj, fb):
    t = proj.shape[0]
    nb = t // BLK

    def body(sm_ref, fb_ref, c_ref, cr_ref):
        x = sm_ref[...] + fb_ref[...]
        lane = lax.broadcasted_iota(jnp.int32, (1, BLK), 1)
        keep = _valid_rows(t, 0) & (lane >= SM_F) & (lane < SM_F + FOX_H)
        logf = jnp.where(keep, jnp.minimum(x, 0.0) - jnp.log(1.0 + jnp.exp(-jnp.abs(x))), 0.0)
        tri = _tri().astype(F32)
        carry = jnp.zeros((1, BLK), F32)
        for b in range(nb):
            cb = _nn_hi(tri, logf[b * BLK:(b + 1) * BLK, :]) + carry
            c_ref[b * BLK:(b + 1) * BLK, :] = cb
            carry = cb[BLK - 1:BLK, :]
        cr_ref[...] = c_ref[...].T[SM_F:SM_F + 8, :]

    return pl.pallas_call(
        body, name="fox_pre", grid=(1,),
        in_specs=[pl.BlockSpec((t, BLK), lambda i: (0, C_SM // BLK)), pl.BlockSpec((1, BLK), lambda i: (0, 0))],
        out_specs=[pl.BlockSpec((t, BLK), lambda i: (0, 0)), pl.BlockSpec((8, t), lambda i: (0, 0))],
        out_shape=[jax.ShapeDtypeStruct((t, BLK), F32), jax.ShapeDtypeStruct((8, t), F32)],
        compiler_params=_cp("arbitrary"),
    )(proj, fb)


def fox_pre_bwd(dcq, dck, proj, fb, dsm_in):
    t = proj.shape[0]
    nb = t // BLK

    def body(dcq_ref, dck_ref, sm_ref, fb_ref, din_ref, dsm_ref, dfb_ref, scr):
        triu = _tri(lower=False).astype(F32)
        carry = jnp.zeros((1, BLK), F32)
        scr[...] = jnp.concatenate([jnp.zeros((SM_F, t), F32), dcq_ref[...], jnp.zeros((BLK - SM_F - 8, t), F32)], axis=0).T
        for b in range(nb - 1, -1, -1):
            blk = scr[b * BLK:(b + 1) * BLK, :] - dck_ref[b * BLK:(b + 1) * BLK, :]
            cb = _nn_hi(triu, blk) + carry
            scr[b * BLK:(b + 1) * BLK, :] = cb
            carry = cb[0:1, :]
        x = sm_ref[...] + fb_ref[...]
        lane = lax.broadcasted_iota(jnp.int32, (1, BLK), 1)
        keep = _valid_rows(t, 0) & (lane >= SM_F) & (lane < SM_F + FOX_H)
        df = jnp.where(keep, scr[...] * _sig(-x), 0.0)
        dfb_ref[...] = jnp.sum(df, axis=0, keepdims=True)
        dsm_ref[...] = din_ref[...] + df

    full = pl.BlockSpec((t, BLK), lambda i: (0, 0))
    return pl.pallas_call(
        body, name="fox_pre_bwd", grid=(1,),
        in_specs=[pl.BlockSpec((8, t), lambda i: (0, 0)), full,
                  pl.BlockSpec((t, BLK), lambda i: (0, C_SM // BLK)), pl.BlockSpec((1, BLK), lambda i: (0, 0)), full],
        out_specs=[full, pl.BlockSpec((1, BLK), lambda i: (0, 0))],
        out_shape=[jax.ShapeDtypeStruct((t, BLK), F32), jax.ShapeDtypeStruct((1, BLK), F32)],
        scratch_shapes=[pltpu.VMEM((t, BLK), F32)],
        compiler_params=_cp("arbitrary"),
    )(dcq, dck, proj, fb, dsm_in)


def _swap_rope(x):
    lane = lax.broadcasted_iota(jnp.int32, (1, BLK), 1)
    return jnp.where((lane >= SM_KR) & (lane < SM_KR + 16), pltpu.roll(x, BLK - 16, 1),
                     jnp.where((lane >= SM_KR + 16) & (lane < SM_KR + 32), pltpu.roll(x, 16, 1), 0.0))


def _rms(x, g):
    r = lax.rsqrt(jnp.mean(x * x, axis=1, keepdims=True) + EPS)
    return r, x * r


def mla_pre(proj, qg, kvg, wq, wk, wv, cosq, sinq):
    t = proj.shape[0]
    tm = _row_tile(t)

    def body(cq_ref, ckv_ref, sm_ref, qg_ref, kvg_ref, wq_ref, wk_ref, wv_ref, cos_ref, sin_ref,
             q_ref, k_ref, v_ref, cqn_ref, ckvn_ref):
        cs, sn = cos_ref[...], sin_ref[...]
        _, xh = _rms(cq_ref[...], None)
        cqn = (xh * qg_ref[...]).astype(BF16)
        cqn_ref[...] = cqn
        qraw = _nn(cqn, wq_ref[...])
        qs = []
        for h in range(MLA_H):
            hb = qraw[:, BLK * h:BLK * (h + 1)]
            qs.append(hb * cs + _swap_rope(hb) * sn)
        q_ref[...] = jnp.concatenate(qs, axis=1).astype(BF16)
        _, kh = _rms(ckv_ref[...], None)
        ckvn = (kh * kvg_ref[...]).astype(BF16)
        ckvn_ref[...] = ckvn
        kraw = _nn(ckvn, wk_ref[...])
        v_ref[...] = _nn(ckvn, wv_ref[...]).astype(BF16)
        lane = lax.broadcasted_iota(jnp.int32, (1, BLK), 1)
        kr = sm_ref[...]
        krr = jnp.where((lane >= SM_KR) & (lane < SM_KR + MLA_ROPE), kr * cs + _swap_rope(kr) * sn, 0.0)
        k_ref[...] = jnp.concatenate([kraw[:, BLK * h:BLK * (h + 1)] + krr for h in range(MLA_H)], axis=1).astype(BF16)

    def rows(w, cb):
        return pl.BlockSpec((tm, w), lambda i: (i, cb))

    def whole(a):
        return pl.BlockSpec(a.shape, lambda i: (0, 0))

    return pl.pallas_call(
        body, name="mla_pre", grid=(t // tm,),
        in_specs=[rows(MLA_QL, C_CQ // MLA_QL), rows(MLA_KVL, C_CKV // MLA_KVL), rows(BLK, C_SM // BLK),
                  whole(qg), whole(kvg), whole(wq), whole(wk), whole(wv), rows(BLK, 0), rows(BLK, 0)],
        out_specs=[rows(512, 0), rows(512, 0), rows(256, 0), rows(MLA_QL, 0), rows(MLA_KVL, 0)],
        out_shape=[jax.ShapeDtypeStruct((t, 512), BF16), jax.ShapeDtypeStruct((t, 512), BF16),
                   jax.ShapeDtypeStruct((t, 256), BF16), jax.ShapeDtypeStruct((t, MLA_QL), BF16),
                   jax.ShapeDtypeStruct((t, MLA_KVL), BF16)],
        compiler_params=_cp("arbitrary"),
    )(proj, proj, proj, qg, kvg, wq, wk, wv, cosq, sinq)


def mla_pre_bwd(dq, dk, dv, proj, cqn, ckvn, qg, kvg, wq, wk, wv, cosq, sinq, dsm_in):
    t = proj.shape[0]
    tm = _row_tile(t)

    def body(dq_ref, dk_ref, dv_ref, cq_ref, ckv_ref, cqn_ref, ckvn_ref, qg_ref, kvg_ref, wq_ref, wk_ref, wv_ref,
             cos_ref, sin_ref, din_ref, dcq_ref, dckv_ref, dsm_ref, dwq_ref, dwk_ref, dwv_ref, dqg_ref, dkvg_ref):
        i = pl.program_id(0)

        @pl.when(i == 0)
        def _():
            for r in (dwq_ref, dwk_ref, dwv_ref, dqg_ref, dkvg_ref):
                r[...] = jnp.zeros_like(r)

        cs, sn = cos_ref[...], sin_ref[...]
        lane = lax.broadcasted_iota(jnp.int32, (1, BLK), 1)

        def unrope(dy):
            return dy * cs + _swap_rope(dy * sn)

        dqp = jnp.concatenate([unrope(dq_ref[:, BLK * h:BLK * (h + 1)]) for h in range(MLA_H)], axis=1).astype(BF16)
        dwq_ref[...] += _tn(cqn_ref[...], dqp)
        dcqn = _nt(dqp, wq_ref[...])
        r, xh = _rms(cq_ref[...], None)
        dqg_ref[...] += jnp.sum(dcqn * xh, axis=0, keepdims=True)
        dxh = dcqn * qg_ref[...]
        dcq_ref[...] = r * (dxh - xh * jnp.mean(dxh * xh, axis=1, keepdims=True))

        dkn, dkr = [], jnp.zeros((tm, BLK), F32)
        for h in range(MLA_H):
            blk = dk_ref[:, BLK * h:BLK * (h + 1)]
            dkn.append(jnp.where(lane < MLA_NOPE, blk, 0.0))
            dkr += jnp.where((lane >= SM_KR) & (lane < SM_KR + MLA_ROPE), blk, 0.0)
        dknb = jnp.concatenate(dkn, axis=1).astype(BF16)
        dvb = dv_ref[...].astype(BF16)
        ckvn = ckvn_ref[...]
        dwk_ref[...] += _tn(ckvn, dknb)
        dwv_ref[...] += _tn(ckvn, dvb)
        dckvn = _nt(dknb, wk_ref[...]) + _nt(dvb, wv_ref[...])
        r2, kh = _rms(ckv_ref[...], None)
        dkvg_ref[...] += jnp.sum(dckvn * kh, axis=0, keepdims=True)
        dkh = dckvn * kvg_ref[...]
        dckv_ref[...] = r2 * (dkh - kh * jnp.mean(dkh * kh, axis=1, keepdims=True))
        dsm_ref[...] = din_ref[...] + jnp.where((lane >= SM_KR) & (lane < SM_KR + MLA_ROPE), unrope(dkr), 0.0)

    def rows(w, cb):
        return pl.BlockSpec((tm, w), lambda i: (i, cb))

    def whole(a):
        return pl.BlockSpec(a.shape, lambda i: (0, 0))

    def wshape(a):
        return jax.ShapeDtypeStruct(a.shape, F32)

    return pl.pallas_call(
        body, name="mla_pre_bwd", grid=(t // tm,),
        in_specs=[rows(512, 0), rows(512, 0), rows(256, 0), rows(MLA_QL, C_CQ // MLA_QL), rows(MLA_KVL, C_CKV // MLA_KVL),
                  rows(MLA_QL, 0), rows(MLA_KVL, 0), whole(qg), whole(kvg), whole(wq), whole(wk), whole(wv),
                  rows(BLK, 0), rows(BLK, 0), rows(BLK, 0)],
        out_specs=[rows(MLA_QL, 0), rows(MLA_KVL, 0), rows(BLK, 0), whole(wq), whole(wk), whole(wv), whole(qg), whole(kvg)],
        out_shape=[jax.ShapeDtypeStruct((t, MLA_QL), F32), jax.ShapeDtypeStruct((t, MLA_KVL), F32),
                   jax.ShapeDtypeStruct((t, BLK), F32), wshape(wq), wshape(wk), wshape(wv), wshape(qg), wshape(kvg)],
        compiler_params=_cp("arbitrary"),
    )(dq, dk, dv, proj, proj, cqn, ckvn, qg, kvg, wq, wk, wv, cosq, sinq, dsm_in)


def _slot_sum(me, own, recv_ref):
    gg = own.astype(F32)
    for s in range(N_DEV):
        gg = gg + jnp.where(me == s, 0.0, recv_ref[s].astype(F32))
    return gg


def adamw(w, m, v, g=None, recv=None, own=None, me_arr=None):
    shape = w.shape
    c = shape[-1]
    from_recv = recv is not None
    if not from_recv:
        me_arr = jnp.zeros((1,), jnp.int32)
    nl = len(recv) if from_recv else 1
    rws = w.size // c // nl
    tr = rws
    for d in (1024, 512, 352, 256, 128, 64, 32, 16, 8):
        if rws % d == 0 and d * c * 4 <= (2 << 20):
            tr = d
            break
    nt = rws // tr
    w2, m2, v2 = (a.reshape(nl, rws, c) for a in (w, m, v))
    if from_recv:
        gin = [a.reshape(N_DEV, rws, c) for a in list(recv) + list(own)]
    else:
        gin = [g.reshape(1, rws, c)]

    def body(me_ref, w_ref, m_ref, v_ref, *rest):
        g_refs, outs = rest[:len(gin)], rest[len(gin):]
        if from_recv:
            g_out, outs = outs[0], outs[1:]
            for li in range(nl):
                @pl.when(pl.program_id(0) == li)
                def _(li=li):
                    g_out[...] = _slot_sum(me_ref[0], g_refs[nl + li][...], g_refs[li])
            gg = g_out[...]
        else:
            gg = g_refs[0][...]
        d_ref, nm_ref, nv_ref = outs
        nm = B1 * m_ref[...] + (1.0 - B1) * gg
        nv = B2 * v_ref[...] + (1.0 - B2) * (gg * gg)
        mh = nm / (1.0 - B1 ** STEP)
        vh = nv / (1.0 - B2 ** STEP)
        d_ref[...] = -LR * (mh / (jnp.sqrt(vh) + AEPS) + WD * w_ref[...])
        nm_ref[...] = nm
        nv_ref[...] = nv

    row = pl.BlockSpec((None, tr, c), lambda l, i, me: (l, i, 0))
    if from_recv:
        gspecs = [pl.BlockSpec((N_DEV, tr, c), lambda l, i, me, li=li: (0, jnp.where(l == li, i, 0), 0))
                  for li in range(nl)]
        gspecs += [pl.BlockSpec((None, tr, c), lambda l, i, me, li=li: (me[0], jnp.where(l == li, i, 0), 0))
                   for li in range(nl)]
    else:
        gspecs = [row]
    nout = 4 if from_recv else 3
    outs = pl.pallas_call(
        body, name="adamw",
        grid_spec=pltpu.PrefetchScalarGridSpec(num_scalar_prefetch=1, grid=(nl, nt), in_specs=[row, row, row] + gspecs,
                                               out_specs=[row] * nout),
        out_shape=[jax.ShapeDtypeStruct((nl, rws, c), F32)] * nout,
        compiler_params=_cp("arbitrary", "arbitrary"),
    )(me_arr, w2, m2, v2, *gin)
    return tuple(o.reshape(shape) for o in outs)


def sum_slots(recv, own=None, me_arr=None):
    _, r, c = recv.shape
    if own is None:
        own, me_arr = recv, jnp.zeros((1,), jnp.int32)
        plain = True
    else:
        plain = False

    def body(me_ref, r_ref, own_ref, o_ref):
        if plain:
            gg = r_ref[0].astype(F32)
            for s in range(1, N_DEV):
                gg = gg + r_ref[s].astype(F32)
            o_ref[...] = gg
        else:
            o_ref[...] = _slot_sum(me_ref[0], own_ref[...], r_ref)

    return pl.pallas_call(
        body, name="sum_slots",
        grid_spec=pltpu.PrefetchScalarGridSpec(
            num_scalar_prefetch=1, grid=(1,),
            in_specs=[pl.BlockSpec((N_DEV, r, c), lambda i, me: (0, 0, 0)),
                      pl.BlockSpec((None, r, c), lambda i, me: (me[0], 0, 0))],
            out_specs=pl.BlockSpec((r, c), lambda i, me: (0, 0))),
        out_shape=jax.ShapeDtypeStruct((r, c), F32),
        compiler_params=_cp("arbitrary"),
    )(me_arr, recv, own)


_FLIPS = [(0, 0, 1), (0, 1, 0), (0, 1, 1), (1, 0, 0), (1, 0, 1), (1, 1, 0), (1, 1, 1)]
_ANY = pl.BlockSpec(memory_space=pl.ANY)


def _mesh_place():
    x, y, c = lax.axis_index("x"), lax.axis_index("y"), lax.axis_index("c")
    me = 4 * x + 2 * y + c
    peers = [((x + fx) % 2, (y + fy) % 2, (c + fc) % 2) for fx, fy, fc in _FLIPS]
    return me, peers


def place_own(src, l, dtype, me_arr):
    _, r, c = src.shape
    tr = r
    for d in (512, 352, 256, 128, 64, 32, 16, 8):
        if r % d == 0 and d * c * 4 <= (2 << 20):
            tr = d
            break

    def body(me_ref, s_ref, o_ref):
        o_ref[...] = s_ref[...].astype(dtype)

    return pl.pallas_call(
        body, name="place_own",
        grid_spec=pltpu.PrefetchScalarGridSpec(
            num_scalar_prefetch=1, grid=(r // tr,),
            in_specs=[pl.BlockSpec((None, tr, c), lambda i, me: (l, i, 0))],
            out_specs=pl.BlockSpec((None, tr, c), lambda i, me: (me[0], i, 0))),
        out_shape=jax.ShapeDtypeStruct((N_DEV, r, c), dtype),
        compiler_params=_cp("arbitrary"),
    )(me_arr, src)


_HBM = pl.BlockSpec(memory_space=pltpu.HBM)
_SEMS = pl.BlockSpec(memory_space=pltpu.SEMAPHORE)
_EFFECT = pltpu.SideEffectType.DATAFLOW_SIDE_EFFECTING


def exchange_start(mode, arrays, name, after=None):
    n = len(arrays)
    gather = mode == "gather"
    ns = 0 if gather else n
    zones = list(arrays) if gather else [lax.empty(a.shape, a.dtype) for a in arrays]
    ops = ([] if gather else list(arrays)) + zones
    extra = [] if after is None else [after]

    def body(*refs):
        srcs, lands = refs[:ns], refs[ns:ns + n]
        send_sems, recv_sems = refs[ns + n + len(extra)], refs[ns + n + len(extra) + 1]
        token = refs[-1]
        me, peers = _mesh_place()
        ids = [4 * p[0] + 2 * p[1] + p[2] for p in peers]
        for j in range(n):
            for k in range(N_DEV - 1):
                src = lands[j].at[me] if gather else srcs[j].at[ids[k]]
                pltpu.make_async_remote_copy(src_ref=src, dst_ref=lands[j].at[me],
                                             send_sem=send_sems.at[j * (N_DEV - 1) + k],
                                             recv_sem=recv_sems.at[j * (N_DEV - 1) + k], device_id=peers[k],
                                             device_id_type=pl.DeviceIdType.MESH).start()
        token[...] = jnp.zeros_like(token)

    nsem = n * (N_DEV - 1)
    res = pl.pallas_call(
        body, name=name,
        in_specs=[_HBM] * (ns + n) + [_ANY] * len(extra),
        out_specs=(_SEMS, _SEMS, *[_HBM] * (ns + n), pl.BlockSpec(memory_space=pltpu.VMEM)),
        out_shape=(pltpu.SemaphoreType.DMA((nsem,)), pltpu.SemaphoreType.DMA((nsem,)),
                   *[pltpu.HBM(a.shape, a.dtype) for a in ops], jax.ShapeDtypeStruct((8, BLK), F32)),
        input_output_aliases={i: 2 + i for i in range(ns + n)},
        compiler_params=pltpu.CompilerParams(has_side_effects=_EFFECT),
    )(*[pltpu.with_memory_space_constraint(a, pltpu.HBM) for a in ops], *extra)
    return dict(gather=gather, send=res[0], recv=res[1], srcs=list(res[2:2 + ns]), lands=list(res[2 + ns:2 + ns + n]),
                token=res[-1])


def exchange_wait(hd, idxs, name, after):
    gather = hd["gather"]
    n = len(idxs)
    ns = 0 if gather else n
    ops = ([] if gather else [hd["srcs"][j] for j in idxs]) + [hd["lands"][j] for j in idxs]

    def body(*refs):
        srcs, lands = refs[:ns], refs[ns:ns + n]
        send_sems, recv_sems = refs[ns + n], refs[ns + n + 1]
        me, peers = _mesh_place()
        ids = [4 * p[0] + 2 * p[1] + p[2] for p in peers]
        for p, j in enumerate(idxs):
            for k in range(N_DEV - 1):
                src = lands[p].at[me] if gather else srcs[p].at[ids[k]]
                cp = pltpu.make_async_remote_copy(src_ref=src, dst_ref=lands[p].at[ids[k]],
                                                  send_sem=send_sems.at[j * (N_DEV - 1) + k],
                                                  recv_sem=recv_sems.at[j * (N_DEV - 1) + k], device_id=peers[k],
                                                  device_id_type=pl.DeviceIdType.MESH)
                cp.wait_send()
                cp.wait_recv()

    res = pl.pallas_call(
        body, name=name,
        in_specs=[_HBM] * (ns + n) + [_SEMS, _SEMS, _ANY],
        out_specs=[_HBM] * (ns + n),
        out_shape=[pltpu.HBM(a.shape, a.dtype) for a in ops],
        input_output_aliases={i: i for i in range(ns + n)},
        compiler_params=pltpu.CompilerParams(has_side_effects=_EFFECT),
    )(*ops, hd["send"], hd["recv"], after)
    return list(res[:ns]), list(res[ns:])


def _chip_place():
    x, y, c = lax.axis_index("x"), lax.axis_index("y"), lax.axis_index("c")
    chips = [((x + 1) % 2, y), (x, (y + 1) % 2), ((x + 1) % 2, (y + 1) % 2)]
    ident = lambda p: 4 * p[0] + 2 * p[1] + p[2]
    return dict(me=4 * x + 2 * y + c, sib=(x, y, 1 - c), sib_id=4 * x + 2 * y + 1 - c,
                same=[(cx, cy, c) for cx, cy in chips], same_ids=[ident((cx, cy, c)) for cx, cy in chips],
                other_ids=[ident((cx, cy, 1 - c)) for cx, cy in chips])


def _remote(src, dst, send_sem, recv_sem, dev):
    return pltpu.make_async_remote_copy(src_ref=src, dst_ref=dst, send_sem=send_sem, recv_sem=recv_sem, device_id=dev,
                                        device_id_type=pl.DeviceIdType.MESH)


def gather_start(zones, name):
    n = len(zones)

    def body(*refs):
        lands, send_sems, recv_sems, token = refs[:n], refs[n], refs[n + 1], refs[-1]
        pc = _chip_place()
        for j in range(n):
            own = lands[j].at[pc["me"]]
            for k, dev in enumerate([pc["sib"]] + pc["same"]):
                _remote(own, own, send_sems.at[4 * j + k], recv_sems.at[4 * j + k], dev).start()
        token[...] = jnp.zeros_like(token)

    res = pl.pallas_call(
        body, name=name,
        in_specs=[_HBM] * n,
        out_specs=(_SEMS, _SEMS, *[_HBM] * n, pl.BlockSpec(memory_space=pltpu.VMEM)),
        out_shape=(pltpu.SemaphoreType.DMA((4 * n,)), pltpu.SemaphoreType.DMA((4 * n,)),
                   *[pltpu.HBM(a.shape, a.dtype) for a in zones], jax.ShapeDtypeStruct((8, BLK), F32)),
        input_output_aliases={i: 2 + i for i in range(n)},
        compiler_params=pltpu.CompilerParams(has_side_effects=_EFFECT),
    )(*[pltpu.with_memory_space_constraint(a, pltpu.HBM) for a in zones])
    return dict(send=res[0], recv=res[1], lands=list(res[2:2 + n]), token=res[-1])


def gather_relay(hd, idxs, name, after):
    n = len(idxs)

    def body(*refs):
        lands, send_sems, recv_sems = refs[:n], refs[n], refs[n + 1]
        fsend, frecv, token = refs[n + 3 + n], refs[n + 4 + n], refs[-1]
        pc = _chip_place()
        for p, j in enumerate(idxs):
            for k in range(3):
                _remote(lands[p].at[pc["me"]], lands[p].at[pc["same_ids"][k]], send_sems.at[4 * j + 1 + k],
                        recv_sems.at[4 * j + 1 + k], pc["same"][k]).wait_recv()
        for p in range(n):
            for k in range(3):
                got = lands[p].at[pc["same_ids"][k]]
                _remote(got, got, fsend.at[3 * p + k], frecv.at[3 * p + k], pc["sib"]).start()
        token[...] = jnp.zeros_like(token)

    ops = [hd["lands"][j] for j in idxs]
    res = pl.pallas_call(
        body, name=name,
        in_specs=[_HBM] * n + [_SEMS, _SEMS, _ANY],
        out_specs=(*[_HBM] * n, _SEMS, _SEMS, pl.BlockSpec(memory_space=pltpu.VMEM)),
        out_shape=(*[pltpu.HBM(a.shape, a.dtype) for a in ops], pltpu.SemaphoreType.DMA((3 * n,)),
                   pltpu.SemaphoreType.DMA((3 * n,)), jax.ShapeDtypeStruct((8, BLK), F32)),
        input_output_aliases={i: i for i in range(n)},
        compiler_params=pltpu.CompilerParams(has_side_effects=_EFFECT),
    )(*ops, hd["send"], hd["recv"], after)
    return dict(lands=list(res[:n]), fsend=res[n], frecv=res[n + 1], token=res[-1])


def gather_wait(hd, rl, idxs, name, after):
    n = len(idxs)

    def body(*refs):
        lands, send_sems, recv_sems, fsend, frecv = refs[:n], refs[n], refs[n + 1], refs[n + 2], refs[n + 3]
        pc = _chip_place()
        for p, j in enumerate(idxs):
            own = lands[p].at[pc["me"]]
            for k, dev in enumerate([pc["sib"]] + pc["same"]):
                _remote(own, own, send_sems.at[4 * j + k], recv_sems.at[4 * j + k], dev).wait_send()
            _remote(own, lands[p].at[pc["sib_id"]], send_sems.at[4 * j], recv_sems.at[4 * j], pc["sib"]).wait_recv()
            for k in range(3):
                cp = _remote(lands[p].at[pc["same_ids"][k]], lands[p].at[pc["other_ids"][k]], fsend.at[3 * p + k],
                             frecv.at[3 * p + k], pc["sib"])
                cp.wait_send()
                cp.wait_recv()

    res = pl.pallas_call(
        body, name=name,
        in_specs=[_HBM] * n + [_SEMS, _SEMS, _SEMS, _SEMS, _ANY],
        out_specs=[_HBM] * n,
        out_shape=[pltpu.HBM(a.shape, a.dtype) for a in rl["lands"]],
        input_output_aliases={i: i for i in range(n)},
        compiler_params=pltpu.CompilerParams(has_side_effects=_EFFECT),
    )(*rl["lands"], hd["send"], hd["recv"], rl["fsend"], rl["frecv"], after)
    return list(res)


def _pad_cols(a, n):
    return jnp.pad(a, ((0, 0),) * (a.ndim - 1) + ((0, n - a.shape[-1]),))


def w_in_to_padded(w):
    z = lambda n: jnp.zeros(w.shape[:-1] + (n,), w.dtype)
    return jnp.concatenate([
        w[..., 0:1280], w[..., 1288:2056], w[..., 2060:2316], w[..., 2316:2444],
        w[..., 1280:1288], w[..., 2056:2060], z(SM_KR - SM_F - FOX_H), w[..., 2444:2476], z(BLK - SM_KR - MLA_ROPE)], axis=-1)


def w_in_from_padded(g):
    s = C_SM
    return jnp.concatenate([
        g[..., 0:1280], g[..., s + SM_DT:s + SM_DT + 8], g[..., 1280:2048], g[..., s + SM_F:s + SM_F + 4],
        g[..., 2048:2304], g[..., 2304:2432], g[..., s + SM_KR:s + SM_KR + MLA_ROPE]], axis=-1)


def _unshard_cols(gth):
    n, r, c = gth.shape
    return jnp.transpose(gth, (1, 0, 2)).reshape(r, n * c)


def _shard_cols(full):
    r, nc = full.shape
    return jnp.transpose(full.reshape(r, N_DEV, nc // N_DEV), (1, 0, 2))


def mla_weights(uq_g, ukv_g):
    uq = _unshard_cols(uq_g)
    dqh = MLA_NOPE + MLA_ROPE
    wq = jnp.concatenate([_pad_cols(uq[:, dqh * h:dqh * (h + 1)], BLK) for h in range(MLA_H)], axis=1)
    wk = jnp.concatenate([_pad_cols(ukv_g[2 * h], BLK) for h in range(MLA_H)], axis=1)
    wv = jnp.concatenate([ukv_g[2 * h + 1] for h in range(MLA_H)], axis=1)
    return wq, wk, wv


def mla_weight_grads(dwq, dwk, dwv):
    dqh = MLA_NOPE + MLA_ROPE
    duq = _shard_cols(jnp.concatenate([dwq[:, BLK * h:BLK * h + dqh] for h in range(MLA_H)], axis=1))
    parts = []
    for h in range(MLA_H):
        parts += [dwk[:, BLK * h:BLK * h + MLA_NOPE], dwv[:, MLA_V * h:MLA_V * (h + 1)]]
    return duq, jnp.stack(parts, axis=0)


def rope_tables(t):
    pos = (jnp.arange(t, dtype=jnp.int32) - PAD).astype(F32)
    inv_freq = 1.0 / (10000.0 ** (jnp.arange(0, MLA_ROPE, 2, dtype=F32) / MLA_ROPE))
    ang = pos[:, None] * inv_freq[None, :]
    cos, sin = jnp.cos(ang), jnp.sin(ang)
    one, zero = jnp.ones((t, SM_KR), F32), jnp.zeros((t, SM_KR), F32)
    tail = BLK - SM_KR - MLA_ROPE
    cosq = jnp.concatenate([one, cos, cos, jnp.ones((t, tail), F32)], axis=1)
    sinq = jnp.concatenate([zero, -sin, sin, jnp.zeros((t, tail), F32)], axis=1)
    return cosq, sinq


def _lanes(v, off=0):
    return jnp.pad(v.astype(F32), (off, BLK - off - v.shape[0]))[None, :]


def layer_fwd(x, ln, hb, getw, tabs, ahead):
    sv = {"h0b": hb}
    W = dict(getw("ffn1", hb))
    ln1 = (W["ln1_g"], W["ln1_b"])
    u, v, r1, h1b = ffn_fwd_seq(x, ln, W["g1"], W["u1"], W["d1"], ln1)
    sv.update(u1=u, v1=v, r1=r1, h1b=h1b)
    W.update(getw("mix", h1b))
    ln2 = (W["ln2_g"], W["ln2_b"])
    proj = mm_nn(h1b, W["w_in"])
    xa = conv_fwd(proj, W["conv_w"], W["conv_b"])
    y_ssd, sprev = ssd_fwd(xa, proj, W["dtb"], W["alog"], W["dskip"], W["normg"])
    c_col, c_row = fox_pre(proj, W["fb"])
    y_fox, lse_f = attn_fwd(proj, proj, proj, C_FQ // 256, C_FK // 256, C_FV // 256, FOX_H, FOX_DH, FOX_DH,
                            FOX_DH ** -0.5, c_col, c_row, SM_F)
    ahead(0, "ffn2", y_fox)
    q, k, vv, cqn, ckvn = mla_pre(proj, W["qg"], W["kvg"], W["wq"], W["wk"], W["wv"], *tabs)
    y_mla, lse_m = attn_fwd(q, k, vv, 0, 0, 0, MLA_H, BLK, MLA_V, (MLA_NOPE + MLA_ROPE) ** -0.5)
    mixcat = jnp.concatenate([y_ssd, y_fox, y_mla], axis=1)
    r2, h2b = mm_res_ln(mixcat, W["w_out"], r1, ln1, ln2)
    sv.update(proj=proj, xa=xa, sprev=sprev, c_col=c_col, c_row=c_row, lse_f=lse_f, q=q, k=k, v=vv, cqn=cqn, ckvn=ckvn,
              lse_m=lse_m, mixcat=mixcat, r2=r2, h2b=h2b)
    W.update(getw("ffn2", h2b))
    ahead(1, "ffn1", h2b)
    ln3 = (W["ln3_g"], W["ln3_b"])
    u, v, r3, h3b = ffn_fwd_seq(r2, ln2, W["g2"], W["u2"], W["d2"], ln3)
    sv.update(u2=u, v2=v, r3=r3, W=W)
    return r3, ln3, h3b, sv


def ffn_bwd(parts, r, gamma, hb_in, u, v, wg, wu, wd, after=None):
    dh, dwg, dwu, dwd, dg, db = ffn_bwd_seq(parts, r, gamma, hb_in, u, v, wg, wu, wd, after)
    return dh, dict(d=dwd, g=dwg, u=dwu, ln_g=dg, ln_b=db)


def layer_bwd(parts, sv, emit, tabs, after):
    G = {}
    W = sv["W"]
    dh2, g2 = ffn_bwd(parts, sv["r3"], W["ln3_g"], sv["h2b"], sv["u2"], sv["v2"], W["g2"], W["u2"], W["d2"], after)
    G.update(g2=g2["g"], u2=g2["u"], d2=g2["d"], ln3_g=g2["ln_g"], ln3_b=g2["ln_b"])
    tok = emit("ffn2", G)
    dr2, dmixb, G["ln2_g"], G["ln2_b"] = ln_bwd([(dh2, 1.0)], sv["r2"], W["ln2_g"], 1.0, tok)
    dmc = mm_nt_reduce([(dmixb[None], W["w_out"][None])], D)
    G["w_out"] = mm_tn(sv["mixcat"][None], dmixb[None])[0]
    proj = sv["proj"]
    dxa, dz, dsm, G["normg"], G["dskip"], G["alog"], G["dtb"] = ssd_bwd(
        dmc, sv["xa"], proj, sv["sprev"], W["dtb"], W["alog"], W["dskip"], W["normg"])
    dxbc, G["conv_w"], G["conv_b"] = conv_bwd(dxa, proj, W["conv_w"], W["conv_b"])
    dfq, dfk, dfv, dcq, dck = attn_bwd(proj, proj, proj, dmc, sv["lse_f"], sv["mixcat"], C_FQ // 256, C_FK // 256,
                                       C_FV // 256, 2, 2, FOX_H, FOX_DH, FOX_DH, FOX_DH ** -0.5, sv["c_col"], sv["c_row"], SM_F)
    dsm, G["fb"] = fox_pre_bwd(dcq, dck, proj, W["fb"], dsm)
    dq, dk, dv = attn_bwd(sv["q"], sv["k"], sv["v"], dmc, sv["lse_m"], sv["mixcat"], 0, 0, 0, 3, 3, MLA_H, BLK, MLA_V,
                          (MLA_NOPE + MLA_ROPE) ** -0.5)
    dcql, dckv, dsm, G["wq"], G["wk"], G["wv"], G["qg"], G["kvg"] = mla_pre_bwd(
        dq, dk, dv, proj, sv["cqn"], sv["ckvn"], W["qg"], W["kvg"], W["wq"], W["wk"], W["wv"], *tabs, dsm)
    dproj = jnp.concatenate([dz, dxbc, dfq, dfk, dfv, dcql, dckv, dsm], axis=1).astype(BF16)
    dh1p = mm_nt_reduce([(dproj[None], W["w_in"][None])], D)
    G["w_in"] = mm_tn(sv["h1b"][None], dproj[None])[0]
    tok = emit("mix", G)
    dh0, g1 = ffn_bwd([(dr2, ALPHA), (dh1p, 1.0)], sv["r1"], W["ln1_g"], sv["h0b"], sv["u1"], sv["v1"],
                      W["g1"], W["u1"], W["d1"], tok)
    G.update(g1=g1["g"], u1=g1["u"], d1=g1["d"], ln1_g=g1["ln_g"], ln1_b=g1["ln_b"])
    tok = emit("ffn1", G)
    return [(dh0, 1.0)], G, tok


def local_step(x, target, meta_full, getw, emit, ahead=lambda l, stage, after: None):
    t = x.shape[0] + BLK
    tabs = rope_tables(t)
    xr, hb = build_h0(meta_full, x)
    ln = None
    saved = []
    for l in range(NL):
        xr, ln, hb, sv = layer_fwd(xr, ln, hb, functools.partial(getw, l), tabs,
                                   lambda dl, stage, after, l=l: ahead(l + dl, stage, after))
        saved.append(sv)
    dy, loss = loss_head(xr, ln, target)
    parts = [(dy, 1.0)]
    grads = [None] * NL
    tok = None
    for l in range(NL - 1, -1, -1):
        parts, grads[l], tok = layer_bwd(parts, saved[l], functools.partial(emit, l), tabs, tok)
    gx, gmeta = split_dh0(parts[0][0], tok)
    return loss, gx, gmeta, grads


_SMALL = ["ln1_g", "ln1_b", "ln2_g", "ln2_b", "ln3_g", "ln3_b", "conv_b", "ssd_norm_g", "mla_q_norm_g",
          "mla_kv_norm_g", "dt_bias", "a_log", "d_skip", "fox_f_b"]
_SMALL_ROWS = 8
_BIG = ["ffn1_w_gate", "ffn1_w_up", "ffn1_w_down", "w_in", "conv_w", "mla_w_uq", "mla_w_ukv", "w_out",
        "ffn2_w_gate", "ffn2_w_up", "ffn2_w_down"]
_NAMES = ["meta", "ffn1_w_gate", "ffn1_w_up", "ffn1_w_down", "ln1_g", "ln1_b", "w_in", "conv_w", "conv_b", "dt_bias",
          "a_log", "d_skip", "ssd_norm_g", "fox_f_b", "mla_q_norm_g", "mla_w_uq", "mla_kv_norm_g", "mla_w_ukv", "w_out",
          "ln2_g", "ln2_b", "ffn2_w_gate", "ffn2_w_up", "ffn2_w_down", "ln3_g", "ln3_b"]


def pack_small(p):
    flat = jnp.concatenate([p[n].astype(F32) for n in _SMALL], axis=1)
    return _pad_cols(flat, _SMALL_ROWS * D).reshape(NL * _SMALL_ROWS, D)


def unpack_small(a, like):
    flat = a.reshape(NL, _SMALL_ROWS * D)
    out, at = {}, 0
    for n in _SMALL:
        out[n] = flat[:, at:at + like[n].shape[1]]
        at += like[n].shape[1]
    return out


_STAGES = {"ffn1": ["ffn1_w_gate", "ffn1_w_up", "ffn1_w_down"],
           "mix": ["w_in", "conv_w", "mla_w_uq", "mla_w_ukv", "w_out"],
           "ffn2": ["ffn2_w_gate", "ffn2_w_up", "ffn2_w_down"]}


_FFN_T = ("ffn1_w_gate", "ffn1_w_up", "ffn2_w_gate", "ffn2_w_up")


def stage_weights(l, stage, g, rep):
    if stage != "mix":
        i = stage[3]
        return {"g" + i: g[f"ffn{i}_w_gate"].reshape(D_FF, D), "u" + i: g[f"ffn{i}_w_up"].reshape(D_FF, D),
                "d" + i: g[f"ffn{i}_w_down"].reshape(D_FF, D),
                "ln1_g" if i == "1" else "ln3_g": rep["ln1_g" if i == "1" else "ln3_g"][l][None, :],
                "ln1_b" if i == "1" else "ln3_b": rep["ln1_b" if i == "1" else "ln3_b"][l][None, :]}
    W = {}
    W["w_in"] = g["w_in"].reshape(D, N_INP)
    W["w_out"] = g["w_out"].reshape(D, D)
    W["wq"], W["wk"], W["wv"] = mla_weights(g["mla_w_uq"], g["mla_w_ukv"])
    W["conv_w"] = _unshard_cols(g["conv_w"])
    for k in ("ln2_g", "ln2_b", "conv_b"):
        W[k] = rep[k][l][None, :]
    W["normg"] = rep["ssd_norm_g"][l][None, :]
    W["qg"] = rep["mla_q_norm_g"][l][None, :]
    W["kvg"] = rep["mla_kv_norm_g"][l][None, :]
    W["dtb"] = _lanes(rep["dt_bias"][l], SM_DT)
    W["alog"] = _lanes(rep["a_log"][l], SM_DT)
    W["dskip"] = _lanes(rep["d_skip"][l], SM_DT)
    W["fb"] = _lanes(rep["fox_f_b"][l], SM_F)
    return W


def small_grads(G):
    return {"ln1_g": G["ln1_g"][0], "ln1_b": G["ln1_b"][0], "ln2_g": G["ln2_g"][0], "ln2_b": G["ln2_b"][0],
            "ln3_g": G["ln3_g"][0], "ln3_b": G["ln3_b"][0], "conv_b": G["conv_b"][0], "ssd_norm_g": G["normg"][0],
            "mla_q_norm_g": G["qg"][0], "mla_kv_norm_g": G["kvg"][0], "dt_bias": G["dtb"][0, :SSD_H],
            "a_log": G["alog"][0, :SSD_H], "d_skip": G["dskip"][0, :SSD_H], "fox_f_b": G["fb"][0, SM_F:SM_F + FOX_H]}


def big_grads(G, stage):
    if stage != "mix":
        i = stage[-1]
        return {f"ffn{i}_w_{k}": G[k[0] + i].reshape(N_DEV, HS, D) for k in ("gate", "up", "down")}
    duq, dukv = mla_weight_grads(G["wq"], G["wk"], G["wv"])
    return {"w_in": G["w_in"].reshape(N_DEV, D // N_DEV, N_INP), "w_out": G["w_out"].reshape(N_DEV, D // N_DEV, D),
            "mla_w_uq": duq, "mla_w_ukv": dukv, "conv_w": _shard_cols(G["conv_w"])}


def kernel(x, meta, ffn1_w_gate, ffn1_w_up, ffn1_w_down, ln1_g, ln1_b, w_in, conv_w, conv_b, dt_bias, a_log, d_skip, ssd_norm_g, fox_f_b, mla_q_norm_g, mla_w_uq, mla_kv_norm_g, mla_w_ukv, w_out, ln2_g, ln2_b, ffn2_w_gate, ffn2_w_up, ffn2_w_down, ln3_g, ln3_b, loss_target, m_meta, m_ffn1_w_gate, m_ffn1_w_up, m_ffn1_w_down, m_ln1_g, m_ln1_b, m_w_in, m_conv_w, m_conv_b, m_dt_bias, m_a_log, m_d_skip, m_ssd_norm_g, m_fox_f_b, m_mla_q_norm_g, m_mla_w_uq, m_mla_kv_norm_g, m_mla_w_ukv, m_w_out, m_ln2_g, m_ln2_b, m_ffn2_w_gate, m_ffn2_w_up, m_ffn2_w_down, m_ln3_g, m_ln3_b, v_meta, v_ffn1_w_gate, v_ffn1_w_up, v_ffn1_w_down, v_ln1_g, v_ln1_b, v_w_in, v_conv_w, v_conv_b, v_dt_bias, v_a_log, v_d_skip, v_ssd_norm_g, v_fox_f_b, v_mla_q_norm_g, v_mla_w_uq, v_mla_kv_norm_g, v_mla_w_ukv, v_w_out, v_ln2_g, v_ln2_b, v_ffn2_w_gate, v_ffn2_w_up, v_ffn2_w_down, v_ln3_g, v_ln3_b):
    vals = (meta, ffn1_w_gate, ffn1_w_up, ffn1_w_down, ln1_g, ln1_b, w_in, conv_w, conv_b, dt_bias, a_log, d_skip, ssd_norm_g, fox_f_b, mla_q_norm_g, mla_w_uq, mla_kv_norm_g, mla_w_ukv, w_out, ln2_g, ln2_b, ffn2_w_gate, ffn2_w_up, ffn2_w_down, ln3_g, ln3_b)
    moms = (m_meta, m_ffn1_w_gate, m_ffn1_w_up, m_ffn1_w_down, m_ln1_g, m_ln1_b, m_w_in, m_conv_w, m_conv_b, m_dt_bias, m_a_log, m_d_skip, m_ssd_norm_g, m_fox_f_b, m_mla_q_norm_g, m_mla_w_uq, m_mla_kv_norm_g, m_mla_w_ukv, m_w_out, m_ln2_g, m_ln2_b, m_ffn2_w_gate, m_ffn2_w_up, m_ffn2_w_down, m_ln3_g, m_ln3_b)
    vars_ = (v_meta, v_ffn1_w_gate, v_ffn1_w_up, v_ffn1_w_down, v_ln1_g, v_ln1_b, v_w_in, v_conv_w, v_conv_b, v_dt_bias, v_a_log, v_d_skip, v_ssd_norm_g, v_fox_f_b, v_mla_q_norm_g, v_mla_w_uq, v_mla_kv_norm_g, v_mla_w_ukv, v_w_out, v_ln2_g, v_ln2_b, v_ffn2_w_gate, v_ffn2_w_up, v_ffn2_w_down, v_ln3_g, v_ln3_b)
    P = dict(zip(_NAMES, vals))
    M = dict(zip(_NAMES, moms))
    V = dict(zip(_NAMES, vars_))
    me = 4 * lax.axis_index("x") + 2 * lax.axis_index("y") + lax.axis_index("c")

    me_arr = me.astype(jnp.int32).reshape(1)
    for n in _FFN_T:
        P[n], M[n], V[n] = (jnp.swapaxes(a[n], 1, 2) for a in (P, M, V))
    src = dict(P)
    src["w_in"] = w_in_to_padded(P["w_in"])
    order = [("meta", 0)] + [(n, l) for l in range(NL) for names in _STAGES.values() for n in names]
    nfirst = 1 + len(_STAGES["ffn1"])

    def place(n, l):
        return place_own(P["meta"][None] if n == "meta" else src[n], l, F32 if n in ("meta", "conv_w") else BF16, me_arr)

    hg_first = gather_start([place(n, l) for n, l in order[:nfirst]], "gather_start_first")
    hg_rest = gather_start([place(n, l) for n, l in order[nfirst:]], "gather_start_rest")
    zone_of = {nl_: ((hg_first, i) if i < nfirst else (hg_rest, i - nfirst)) for i, nl_ in enumerate(order)}
    relays = {}

    def ahead(l, stage, after):
        if l < NL and (l, stage) not in relays:
            zs = [zone_of[("meta", 0)]] if stage == "meta" else [zone_of[(n, l)] for n in _STAGES[stage]]
            hg, idxs = zs[0][0], [i for _, i in zs]
            relays[(l, stage)] = (hg, idxs, gather_relay(hg, idxs, f"gather_relay_{l}_{stage}", after))

    def arrived(l, stage, after):
        ahead(l, stage, after)
        hg, idxs, rl = relays[(l, stage)]
        return gather_wait(hg, rl, idxs, f"gather_wait_{l}_{stage}", after)

    meta_full = _unshard_cols(arrived(0, "meta", hg_rest["token"])[0])

    def getw(l, stage, after):
        return stage_weights(l, stage, dict(zip(_STAGES[stage], arrived(l, stage, after))), P)

    sent = {}

    def emit(l, stage, G):
        bg = big_grads(G, stage)
        sent[(l, stage)] = exchange_start("scatter", [bg[n] for n in _STAGES[stage]], f"scatter_start_{l}_{stage}")
        return sent[(l, stage)]["token"]

    loss, gx, gmeta, grads = local_step(x[0], loss_target[0], meta_full, getw, emit, ahead)

    small = jnp.concatenate([pack_small({n: jnp.stack([small_grads(g)[n] for g in grads]) for n in _SMALL}), gmeta,
                             jnp.pad(loss, ((0, 7), (0, D - 1)))], axis=0)
    hs = exchange_start("gather", [place_own(small[None], 0, F32, me_arr)], "small_start")

    out = {}
    after = hs["token"]
    for stage in ("ffn2", "mix", "ffn1"):
        names = _STAGES[stage]
        got = [exchange_wait(sent[(l, stage)], list(range(len(names))), f"scatter_wait_{l}_{stage}", after)
               for l in range(NL - 1, -1, -1)][::-1]
        for i, n in enumerate(names):
            own = [got[l][0][i] for l in range(NL)]
            recv = [got[l][1][i] for l in range(NL)]
            if n == "w_in":
                g = jnp.stack([w_in_from_padded(sum_slots(recv[l], own[l], me_arr)) for l in range(NL)])
                out[n] = (g,) + adamw(P[n], M[n], V[n], g=g)
            else:
                out[n] = adamw(P[n], M[n], V[n], recv=recv, own=own, me_arr=me_arr)
                if n in _FFN_T:
                    out[n] = tuple(jnp.swapaxes(a, 1, 2) for a in out[n])
        after = out[names[-1]][1]
    gsmall = sum_slots(exchange_wait(hs, [0], "small_wait", after)[1][0])
    gm = lax.dynamic_slice(gsmall[NL * _SMALL_ROWS:], (0, me * (D // N_DEV)), (N_META, D // N_DEV))
    out["meta"] = (gm,) + adamw(P["meta"], M["meta"], V["meta"], g=gm)
    gs = gsmall[:NL * _SMALL_ROWS]
    sd, sm_, sv_ = adamw(pack_small(P), pack_small(M), pack_small(V), g=gs)
    ups = [unpack_small(a, P) for a in (gs, sd, sm_, sv_)]
    for n in _SMALL:
        out[n] = tuple(u[n] for u in ups)

    loss_all = gsmall[NL * _SMALL_ROWS + N_META, 0]
    flat = [loss_all, gx[None]]
    for k in range(4):
        flat += [out[n][k] for n in _NAMES]
    return tuple(flat)
```

```python
import functools

import jax
import jax.numpy as jnp
from jax import lax
from jax.experimental import pallas as pl
from jax.experimental.pallas import tpu as pltpu

F32, BF16 = jnp.float32, jnp.bfloat16
HI = lax.Precision.HIGHEST

N_DEV = 8
D = 1024
NL = 2
N_META = 16
BLK = 128
PAD = BLK - N_META
D_FF = 2816
HS = D_FF // N_DEV
SSD_H, SSD_P, SSD_N, SSD_G = 8, 64, 64, 2
SSD_D = SSD_H * SSD_P
CONV_K = 4
CONV_D = SSD_D + 2 * SSD_G * SSD_N
FOX_H, FOX_DH = 4, 64
MLA_H, MLA_QL, MLA_KVL, MLA_NOPE, MLA_ROPE, MLA_V = 4, 256, 128, 64, 32, 64
N_IN = 2476
C_Z, C_XBC, C_FQ, C_FK, C_FV, C_CQ, C_CKV, C_SM, N_INP = 0, 512, 1280, 1536, 1792, 2048, 2304, 2432, 2560
SM_DT, SM_F, SM_KR = 0, 8, 64
ALPHA = (2 * NL) ** 0.25
EPS = 1e-5
NEG = -1e30
LR, B1, B2, AEPS, WD, STEP = 0.001, 0.9, 0.999, 1e-08, 0.01, 10
VMEM_MB = 56


def _cp(*sem):
    return pltpu.CompilerParams(dimension_semantics=sem, vmem_limit_bytes=VMEM_MB << 20)


def _nn(a, b):
    return lax.dot_general(a, b, (((1,), (0,)), ((), ())), preferred_element_type=F32)


def _nt(a, b):
    return lax.dot_general(a, b, (((1,), (1,)), ((), ())), preferred_element_type=F32)


def _tn(a, b):
    return lax.dot_general(a, b, (((0,), (0,)), ((), ())), preferred_element_type=F32)


def _nn_hi(a, b):
    return lax.dot_general(a, b, (((1,), (0,)), ((), ())), precision=HI, preferred_element_type=F32)


def _row_tile(t):
    for d in range(640, 15, -16):
        if t % d == 0:
            return d
    raise ValueError(t)


def _sig(x):
    return 1.0 / (1.0 + jnp.exp(-x))


def _tri(lower=True):
    r = lax.broadcasted_iota(jnp.int32, (BLK, BLK), 0)
    c = lax.broadcasted_iota(jnp.int32, (BLK, BLK), 1)
    return (r >= c) if lower else (r <= c)


def build_h0(meta_full, x):
    s = x.shape[0]
    nb = s // BLK + 1

    def body(m_ref, x_ref, h_ref, hb_ref):
        i = pl.program_id(0)

        @pl.when(i == 0)
        def _():
            h = jnp.concatenate([jnp.zeros((PAD, D), F32), m_ref[...]], axis=0)
            h_ref[...] = h
            hb_ref[...] = h.astype(BF16)

        @pl.when(i > 0)
        def _():
            h_ref[...] = x_ref[...]
            hb_ref[...] = x_ref[...].astype(BF16)

    return pl.pallas_call(
        body, name="build_h0", grid=(nb,),
        in_specs=[pl.BlockSpec((N_META, D), lambda i: (0, 0)),
                  pl.BlockSpec((BLK, D), lambda i: (jnp.maximum(i - 1, 0), 0))],
        out_specs=[pl.BlockSpec((BLK, D), lambda i: (i, 0))] * 2,
        out_shape=[jax.ShapeDtypeStruct((nb * BLK, D), F32), jax.ShapeDtypeStruct((nb * BLK, D), BF16)],
        compiler_params=_cp("arbitrary"),
    )(meta_full, x)


FT = 256


def _layer_norm(r, gamma, beta):
    mu = jnp.mean(r, axis=1, keepdims=True)
    xc = r - mu
    var = jnp.mean(xc * xc, axis=1, keepdims=True)
    return xc * lax.rsqrt(var + EPS) * gamma + beta


def ffn_fwd_seq(x, ln_in, wg, wu, wd, ln_out):
    t = x.shape[0]
    f = wg.shape[0]
    nj, nr = f // FT, t // _row_tile(t)
    rc = t // nr
    plain = ln_in is None
    gi, bi = ln_out if plain else ln_in

    def body(x_hbm, gi_ref, bi_ref, go_ref, bo_ref, wg_ref, wu_ref, wd_ref, u_ref, v_ref, r_hbm, yb_hbm,
             acc, hbs, xbuf, sem_in, sem_out):
        j = pl.program_id(0)

        @pl.when(j == 0)
        def _():
            def fetch(k):
                return pltpu.make_async_copy(x_hbm.at[pl.ds(k * rc, rc)], xbuf.at[k % 2], sem_in.at[k % 2])

            fetch(0).start()
            for k in range(nr):
                if k + 1 < nr:
                    fetch(k + 1).start()
                fetch(k).wait()
                h = xbuf[k % 2]
                if not plain:
                    h = _layer_norm(h, gi_ref[...], bi_ref[...])
                acc[k * rc:(k + 1) * rc, :] = ALPHA * h
                hbs[k * rc:(k + 1) * rc, :] = h.astype(BF16)

        for k in range(nr):
            sl = slice(k * rc, (k + 1) * rc)
            h = hbs[sl, :]
            u = _nt(h, wg_ref[...])
            v = _nt(h, wu_ref[...])
            u_ref[sl, :] = u.astype(BF16)
            v_ref[sl, :] = v.astype(BF16)
            acc[sl, :] += _nn((0.5 * u * _sig(u) * v).astype(BF16), wd_ref[...])

        @pl.when(j == nj - 1)
        def _():
            r_cp = pltpu.make_async_copy(acc, r_hbm, sem_out.at[0])
            r_cp.start()
            for k in range(nr):
                sl = slice(k * rc, (k + 1) * rc)
                hbs[sl, :] = _layer_norm(acc[sl, :], go_ref[...], bo_ref[...]).astype(BF16)
            y_cp = pltpu.make_async_copy(hbs, yb_hbm, sem_out.at[1])
            y_cp.start()
            r_cp.wait()
            y_cp.wait()

    vec = pl.BlockSpec((1, D), lambda j: (0, 0))
    wsp = pl.BlockSpec((FT, D), lambda j: (j, 0))
    act = pl.BlockSpec((None, t, FT), lambda j: (j, 0, 0))
    return pl.pallas_call(
        body, name="ffn_fwd_seq", grid=(nj,),
        in_specs=[_ANY, vec, vec, vec, vec, wsp, wsp, wsp],
        out_specs=[act, act, _ANY, _ANY],
        out_shape=[jax.ShapeDtypeStruct((nj, t, FT), BF16), jax.ShapeDtypeStruct((nj, t, FT), BF16),
                   jax.ShapeDtypeStruct((t, D), F32), jax.ShapeDtypeStruct((t, D), BF16)],
        scratch_shapes=[pltpu.VMEM((t, D), F32), pltpu.VMEM((t, D), BF16), pltpu.VMEM((2, rc, D), F32),
                        pltpu.SemaphoreType.DMA((2,)), pltpu.SemaphoreType.DMA((2,))],
        compiler_params=_cp("arbitrary"),
    )(x, gi, bi, ln_out[0], ln_out[1], wg, wu, wd)


def ffn_bwd_seq(parts, r, gamma, hb, u, v, wg, wu, wd, after=None):
    nj, t, _ = u.shape
    f = nj * FT
    nr = t // _row_tile(t)
    rc = t // nr
    nc = t // BLK
    scales = [s for _, s in parts]
    npart = len(parts)
    extra = [] if after is None else [after]

    def body(*refs):
        refs = refs[len(extra):]
        p_hbm, refs = refs[:npart], refs[npart:]
        (r_hbm, g_ref, hb_hbm, u_ref, v_ref, wg_ref, wu_ref, wd_ref, dh_hbm, dwg_ref, dwu_ref, dwd_ref, dg_ref, db_ref,
         dfs, hbt, dft, dhacc, dus, dvs, acs, pbuf, rbuf, hbuf, sems, sem_out) = refs
        j = pl.program_id(0)

        @pl.when(j == 0)
        def _():
            def fetch(c):
                rows = pl.ds(c * BLK, BLK)
                cps = [pltpu.make_async_copy(p_hbm[p].at[rows], pbuf.at[c % 2, p], sems.at[c % 2, p]) for p in range(npart)]
                cps.append(pltpu.make_async_copy(r_hbm.at[rows], rbuf.at[c % 2], sems.at[c % 2, npart]))
                cps.append(pltpu.make_async_copy(hb_hbm.at[rows], hbuf.at[c % 2], sems.at[c % 2, npart + 1]))
                return cps

            for cp in fetch(0):
                cp.start()
            dg = jnp.zeros((1, D), F32)
            db = jnp.zeros((1, D), F32)
            for c in range(nc):
                if c + 1 < nc:
                    for cp in fetch(c + 1):
                        cp.start()
                for cp in fetch(c):
                    cp.wait()
                sl = slice(c * BLK, (c + 1) * BLK)
                dy = scales[0] * pbuf[c % 2, 0]
                for p in range(1, npart):
                    dy += scales[p] * pbuf[c % 2, p]
                rr = rbuf[c % 2]
                xc = rr - jnp.mean(rr, axis=1, keepdims=True)
                rstd = lax.rsqrt(jnp.mean(xc * xc, axis=1, keepdims=True) + EPS)
                xh = xc * rstd
                dxh = dy * g_ref[...]
                dr = rstd * (dxh - jnp.mean(dxh, axis=1, keepdims=True) - xh * jnp.mean(dxh * xh, axis=1, keepdims=True))
                dg += jnp.sum(dy * xh, axis=0, keepdims=True)
                db += jnp.sum(dy, axis=0, keepdims=True)
                dhacc[sl, :] = ALPHA * dr
                dfc = (0.5 * dr).astype(BF16)
                dfs[sl, :] = dfc
                dft[:, sl] = dfc.T
                hbt[:, sl] = hbuf[c % 2].T
            dg_ref[...] = dg
            db_ref[...] = db

        for k in range(nr):
            sl = slice(k * rc, (k + 1) * rc)
            da = _nt(dfs[sl, :], wd_ref[...])
            uu = u_ref[sl, :].astype(F32)
            vv = v_ref[sl, :].astype(F32)
            sg = _sig(uu)
            du = (da * vv * (sg * (1.0 + uu * (1.0 - sg)))).astype(BF16)
            dv = (da * uu * sg).astype(BF16)
            dus[sl, :] = du
            dvs[sl, :] = dv
            acs[sl, :] = (uu * sg * vv).astype(BF16)
            dhacc[sl, :] += _nn(du, wg_ref[...]) + _nn(dv, wu_ref[...])
        dwg_ref[...] = _nn(hbt[...], dus[...]).astype(BF16).T
        dwu_ref[...] = _nn(hbt[...], dvs[...]).astype(BF16).T
        dwd_ref[...] = _nn(dft[...], acs[...]).astype(BF16).T

        @pl.when(j == nj - 1)
        def _():
            cp = pltpu.make_async_copy(dhacc, dh_hbm, sem_out.at[0])
            cp.start()
            cp.wait()

    vec = pl.BlockSpec((1, D), lambda j: (0, 0))
    wsp = pl.BlockSpec((FT, D), lambda j: (j, 0))
    act = pl.BlockSpec((None, t, FT), lambda j: (j, 0, 0))
    return pl.pallas_call(
        body, name="ffn_bwd_seq", grid=(nj,),
        in_specs=[_ANY] * (len(extra) + npart + 1) + [vec, _ANY, act, act, wsp, wsp, wsp],
        out_specs=[_ANY, wsp, wsp, wsp, vec, vec],
        out_shape=[jax.ShapeDtypeStruct((t, D), F32)] + [jax.ShapeDtypeStruct((f, D), BF16)] * 3
        + [jax.ShapeDtypeStruct((1, D), F32)] * 2,
        scratch_shapes=[pltpu.VMEM((t, D), BF16), pltpu.VMEM((D, t), BF16), pltpu.VMEM((D, t), BF16),
                        pltpu.VMEM((t, D), F32), pltpu.VMEM((t, FT), BF16), pltpu.VMEM((t, FT), BF16),
                        pltpu.VMEM((t, FT), BF16), pltpu.VMEM((2, npart, BLK, D), F32), pltpu.VMEM((2, BLK, D), F32),
                        pltpu.VMEM((2, BLK, D), BF16), pltpu.SemaphoreType.DMA((2, npart + 2)),
                        pltpu.SemaphoreType.DMA((1,))],
        compiler_params=_cp("arbitrary"),
    )(*extra, *[p for p, _ in parts], r, gamma, hb, u, v, wg, wu, wd)


def mm_res_ln(a, b, x, ln_in, ln_out):
    t, k = a.shape
    tm = _row_tile(t)

    def body(a_ref, b_ref, x_ref, gi_ref, bi_ref, go_ref, bo_ref, r_ref, yb_ref):
        r = ALPHA * _layer_norm(x_ref[...], gi_ref[...], bi_ref[...]) + _nn(a_ref[...], b_ref[...])
        r_ref[...] = r
        yb_ref[...] = _layer_norm(r, go_ref[...], bo_ref[...]).astype(BF16)

    row = pl.BlockSpec((tm, D), lambda i: (i, 0))
    vec = pl.BlockSpec((1, D), lambda i: (0, 0))
    return pl.pallas_call(
        body, name="mm_res_ln", grid=(t // tm,),
        in_specs=[pl.BlockSpec((tm, k), lambda i: (i, 0)), pl.BlockSpec((k, D), lambda i: (0, 0)), row, vec, vec, vec, vec],
        out_specs=[row, row],
        out_shape=[jax.ShapeDtypeStruct((t, D), F32), jax.ShapeDtypeStruct((t, D), BF16)],
        compiler_params=_cp("arbitrary"),
    )(a, b, x, ln_in[0], ln_in[1], ln_out[0], ln_out[1])


def mm_nn(a, b):
    t, k = a.shape
    n = tn = b.shape[1]
    tm = _row_tile(t)

    def body(a_ref, b_ref, o_ref):
        o_ref[...] = _nn(a_ref[...], b_ref[...])

    return pl.pallas_call(
        body, name="mm_nn", grid=(t // tm, n // tn),
        in_specs=[pl.BlockSpec((tm, k), lambda i, j: (i, 0)), pl.BlockSpec((k, tn), lambda i, j: (0, j))],
        out_specs=pl.BlockSpec((tm, tn), lambda i, j: (i, j)),
        out_shape=jax.ShapeDtypeStruct((t, n), F32),
        compiler_params=_cp("arbitrary", "arbitrary"),
    )(a, b)


def mm_nt_reduce(pairs, n):
    g, t, _ = pairs[0][0].shape
    tm = _row_tile(t)
    npair = len(pairs)

    def body(*refs):
        o_ref = refs[-1]
        gi = pl.program_id(1)
        tot = _nt(refs[0][...], refs[1][...])
        for p in range(1, npair):
            tot += _nt(refs[2 * p][...], refs[2 * p + 1][...])

        @pl.when(gi == 0)
        def _():
            o_ref[...] = tot

        @pl.when(gi > 0)
        def _():
            o_ref[...] += tot

    in_specs, args = [], []
    for x, w in pairs:
        k = x.shape[2]
        in_specs += [pl.BlockSpec((None, tm, k), lambda i, gi: (gi, i, 0)),
                     pl.BlockSpec((None, n, k), lambda i, gi: (gi, 0, 0))]
        args += [x, w]
    return pl.pallas_call(
        body, name="mm_nt_reduce", grid=(t // tm, g),
        in_specs=in_specs, out_specs=pl.BlockSpec((tm, n), lambda i, gi: (i, 0)),
        out_shape=jax.ShapeDtypeStruct((t, n), F32),
        compiler_params=_cp("arbitrary", "arbitrary"),
    )(*args)


def mm_tn(x, y, out_dtype=BF16):
    gx, t, k = x.shape
    gy, _, n = y.shape
    g = max(gx, gy)
    tm = _row_tile(t)
    nt = t // tm

    def body(x_ref, y_ref, o_ref, acc):
        i = pl.program_id(1)

        @pl.when(i == 0)
        def _():
            acc[...] = jnp.zeros_like(acc)

        acc[...] += _tn(x_ref[...], y_ref[...])

        @pl.when(i == nt - 1)
        def _():
            o_ref[...] = acc[...].astype(out_dtype)

    return pl.pallas_call(
        body, name="mm_tn", grid=(g, nt),
        in_specs=[pl.BlockSpec((None, tm, k), (lambda gi, i: (gi, i, 0)) if gx > 1 else (lambda gi, i: (0, i, 0))),
                  pl.BlockSpec((None, tm, n), (lambda gi, i: (gi, i, 0)) if gy > 1 else (lambda gi, i: (0, i, 0)))],
        out_specs=pl.BlockSpec((None, k, n), lambda gi, i: (gi, 0, 0)),
        out_shape=jax.ShapeDtypeStruct((g, k, n), out_dtype),
        scratch_shapes=[pltpu.VMEM((k, n), F32)],
        compiler_params=_cp("arbitrary", "arbitrary"),
    )(x, y)


def ln_bwd(parts, r, gamma, out_scale, after=None):
    t = r.shape[0]
    tm = _row_tile(t)
    scales = [s for _, s in parts]
    npart = len(parts)
    extra = [] if after is None else [after]

    def body(*refs):
        refs = refs[len(extra):]
        r_ref, g_ref = refs[npart], refs[npart + 1]
        dr_ref, drb_ref, dg_ref, db_ref = refs[npart + 2:]
        i = pl.program_id(0)
        dy = scales[0] * refs[0][...]
        for p in range(1, npart):
            dy += scales[p] * refs[p][...]
        rr = r_ref[...]
        mu = jnp.mean(rr, axis=1, keepdims=True)
        xc = rr - mu
        rstd = lax.rsqrt(jnp.mean(xc * xc, axis=1, keepdims=True) + EPS)
        xh = xc * rstd
        dxh = dy * g_ref[...]
        m1 = jnp.mean(dxh, axis=1, keepdims=True)
        m2 = jnp.mean(dxh * xh, axis=1, keepdims=True)
        dr = rstd * (dxh - m1 - xh * m2)
        dr_ref[...] = dr
        drb_ref[...] = (out_scale * dr).astype(BF16)
        dg = jnp.sum(dy * xh, axis=0, keepdims=True)
        db = jnp.sum(dy, axis=0, keepdims=True)

        @pl.when(i == 0)
        def _():
            dg_ref[...] = dg
            db_ref[...] = db

        @pl.when(i > 0)
        def _():
            dg_ref[...] += dg
            db_ref[...] += db

    row = pl.BlockSpec((tm, D), lambda i: (i, 0))
    vec = pl.BlockSpec((1, D), lambda i: (0, 0))
    return pl.pallas_call(
        body, name="ln_bwd", grid=(t // tm,),
        in_specs=[_ANY] * len(extra) + [row] * (npart + 1) + [vec],
        out_specs=[row, row, vec, vec],
        out_shape=[jax.ShapeDtypeStruct((t, D), F32), jax.ShapeDtypeStruct((t, D), BF16),
                   jax.ShapeDtypeStruct((1, D), F32), jax.ShapeDtypeStruct((1, D), F32)],
        compiler_params=_cp("arbitrary"),
    )(*extra, *[p for p, _ in parts], r, gamma)


def loss_head(r, ln, target):
    t = r.shape[0]
    nb = t // BLK

    def body(r_ref, g_ref, b_ref, t_ref, dy_ref, l_ref):
        i = pl.program_id(0)

        @pl.when(i == 0)
        def _():
            dy_ref[...] = jnp.zeros_like(dy_ref)
            l_ref[...] = jnp.zeros_like(l_ref)

        @pl.when(i > 0)
        def _():
            err = _layer_norm(r_ref[...], g_ref[...], b_ref[...]) - t_ref[...]
            dy_ref[...] = err * (1.0 / D)
            l_ref[...] += (0.5 / D) * jnp.sum(err * err, keepdims=True)

    vec = pl.BlockSpec((1, D), lambda i: (0, 0))
    return pl.pallas_call(
        body, name="loss_head", grid=(nb,),
        in_specs=[pl.BlockSpec((BLK, D), lambda i: (i, 0)), vec, vec,
                  pl.BlockSpec((BLK, D), lambda i: (jnp.maximum(i - 1, 0), 0))],
        out_specs=[pl.BlockSpec((BLK, D), lambda i: (i, 0)), pl.BlockSpec((1, 1), lambda i: (0, 0))],
        out_shape=[jax.ShapeDtypeStruct((t, D), F32), jax.ShapeDtypeStruct((1, 1), F32)],
        compiler_params=_cp("arbitrary"),
    )(r, ln[0], ln[1], target)


def split_dh0(dh0, after=None):
    t = dh0.shape[0]
    nb = t // BLK
    extra = [] if after is None else [after]

    def body(*refs):
        a_ref, gx_ref, gm_ref = refs[len(extra):]
        i = pl.program_id(0)
        tot = a_ref[...]

        @pl.when(i == 0)
        def _():
            gm_ref[...] = tot[PAD:, :]

        @pl.when(i > 0)
        def _():
            gx_ref[...] = tot

    blk = pl.BlockSpec((BLK, D), lambda i: (i, 0))
    return pl.pallas_call(
        body, name="split_dh0", grid=(nb,),
        in_specs=[_ANY] * len(extra) + [blk],
        out_specs=[pl.BlockSpec((BLK, D), lambda i: (jnp.maximum(i - 1, 0), 0)),
                   pl.BlockSpec((N_META, D), lambda i: (0, 0))],
        out_shape=[jax.ShapeDtypeStruct((t - BLK, D), F32), jax.ShapeDtypeStruct((N_META, D), F32)],
        compiler_params=_cp("arbitrary"),
    )(*extra, dh0)


def _valid_rows(nrows, first_row):
    return (first_row + lax.broadcasted_iota(jnp.int32, (nrows, 1), 0)) >= PAD


def conv_fwd(proj, conv_w, conv_b):
    t = proj.shape[0]
    c0 = C_XBC // BLK

    def body(x_ref, w_ref, b_ref, o_ref):
        ok = _valid_rows(t, 0)
        x = jnp.where(ok, x_ref[...], 0.0)
        w = w_ref[...]
        acc = b_ref[...] + w[CONV_K - 1:CONV_K, :] * x
        for s in range(1, CONV_K):
            acc += w[CONV_K - 1 - s:CONV_K - s, :] * pltpu.roll(x, s, 0)
        o_ref[...] = jnp.where(ok, acc * _sig(acc), 0.0)

    return pl.pallas_call(
        body, name="conv_fwd", grid=(CONV_D // BLK,),
        in_specs=[pl.BlockSpec((t, BLK), lambda j: (0, c0 + j)),
                  pl.BlockSpec((CONV_K, BLK), lambda j: (0, j)), pl.BlockSpec((1, BLK), lambda j: (0, j))],
        out_specs=pl.BlockSpec((t, BLK), lambda j: (0, j)),
        out_shape=jax.ShapeDtypeStruct((t, CONV_D), F32),
        compiler_params=_cp("arbitrary"),
    )(proj, conv_w, conv_b)


def conv_bwd(dxa, proj, conv_w, conv_b):
    t = proj.shape[0]
    c0 = C_XBC // BLK

    def body(d_ref, x_ref, w_ref, b_ref, dx_ref, dw_ref, db_ref):
        ok = _valid_rows(t, 0)
        x = jnp.where(ok, x_ref[...], 0.0)
        w = w_ref[...]
        xs = [x] + [pltpu.roll(x, s, 0) for s in range(1, CONV_K)]
        acc = b_ref[...] + w[CONV_K - 1:CONV_K, :] * x
        for s in range(1, CONV_K):
            acc += w[CONV_K - 1 - s:CONV_K - s, :] * xs[s]
        sg = _sig(acc)
        dxc = jnp.where(ok, d_ref[...] * (sg * (1.0 + acc * (1.0 - sg))), 0.0)
        db_ref[...] = jnp.sum(dxc, axis=0, keepdims=True)
        dw_ref[...] = jnp.concatenate(
            [jnp.sum(dxc * xs[CONV_K - 1 - k], axis=0, keepdims=True) for k in range(CONV_K)], axis=0)
        dx = w[CONV_K - 1:CONV_K, :] * dxc
        for s in range(1, CONV_K):
            dx += w[CONV_K - 1 - s:CONV_K - s, :] * pltpu.roll(dxc, t - s, 0)
        dx_ref[...] = jnp.where(ok, dx, 0.0)

    col = pl.BlockSpec((t, BLK), lambda j: (0, j))
    return pl.pallas_call(
        body, name="conv_bwd", grid=(CONV_D // BLK,),
        in_specs=[col, pl.BlockSpec((t, BLK), lambda j: (0, c0 + j)),
                  pl.BlockSpec((CONV_K, BLK), lambda j: (0, j)), pl.BlockSpec((1, BLK), lambda j: (0, j))],
        out_specs=[col, pl.BlockSpec((CONV_K, BLK), lambda j: (0, j)), pl.BlockSpec((1, BLK), lambda j: (0, j))],
        out_shape=[jax.ShapeDtypeStruct((t, CONV_D), F32), jax.ShapeDtypeStruct((CONV_K, CONV_D), F32),
                   jax.ShapeDtypeStruct((1, CONV_D), F32)],
        compiler_params=_cp("arbitrary"),
    )(dxa, proj, conv_w, conv_b)


def _softplus(x):
    return jnp.maximum(x, 0.0) + jnp.log(1.0 + jnp.exp(-jnp.abs(x)))


GW = SSD_D // SSD_G
HPG = SSD_H // SSD_G


def _head_expand():
    r = lax.broadcasted_iota(jnp.int32, (BLK, SSD_D), 0)
    c = lax.broadcasted_iota(jnp.int32, (BLK, SSD_D), 1)
    rt = lax.broadcasted_iota(jnp.int32, (SSD_D, BLK), 0)
    ct = lax.broadcasted_iota(jnp.int32, (SSD_D, BLK), 1)
    return (c // SSD_P == r).astype(F32), (rt // SSD_P == ct).astype(F32)


def _ssd_chunk(xa, sm, dtb, alog, dskip, ok, sp):
    e, et = _head_expand()
    dt = jnp.where(ok, _softplus(sm + dtb), 0.0)
    amat = -jnp.exp(alog)
    tri = _tri()
    ac = _nn_hi(tri.astype(F32), dt * amat)
    act = ac.T
    ace, dte, dse = _nn_hi(ac, e), _nn_hi(dt, e), _nn_hi(dskip, e)
    laste = ace[BLK - 1:BLK, :]
    ee, dece, gle = jnp.exp(ace), jnp.exp(laste - ace), jnp.exp(laste)
    xs = xa[:, :SSD_D]
    xdt = xs * dte
    decx = dece * xdt
    xdtb = xdt.astype(BF16)
    d = dict(e=e, et=et, dt=dt, amat=amat, tri=tri, ac=ac, act=act, dte=dte, dse=dse, ee=ee, dece=dece, gle=gle, xs=xs,
             xdt=xdt, xdtb=xdtb, decx=decx, bg=[], cg=[], cb=[], yo=[], seg=[], m=[], new_s=[])
    ys = []
    for g in range(SSD_G):
        cols = slice(GW * g, GW * (g + 1))
        bg = xa[:, SSD_D + SSD_N * g:SSD_D + SSD_N * (g + 1)].astype(BF16)
        cg = xa[:, SSD_D + SSD_G * SSD_N + SSD_N * g:SSD_D + SSD_G * SSD_N + SSD_N * (g + 1)].astype(BF16)
        spg = sp[:, cols]
        sloc = _tn(bg, decx[:, cols].astype(BF16))
        yo = _nn(cg, spg.astype(BF16)) * ee[:, cols]
        cb = _nt(cg, bg)
        d["new_s"].append(gle[:, cols] * spg + sloc)
        yds = []
        for h in range(HPG * g, HPG * (g + 1)):
            seg = jnp.where(tri, jnp.exp(jnp.minimum(ac[:, h:h + 1] - act[h:h + 1, :], 0.0)), 0.0)
            m = cb * seg
            yds.append(_nn(m.astype(BF16), xdtb[:, SSD_P * h:SSD_P * (h + 1)]))
            d["seg"].append(seg)
            d["m"].append(m)
        ys.append(jnp.concatenate(yds, axis=1) + yo)
        for k, val in (("bg", bg), ("cg", cg), ("cb", cb), ("yo", yo)):
            d[k].append(val)
    d["y"] = jnp.concatenate(ys, axis=1) + dse * xs
    return d


def ssd_fwd(xa, proj, dtb, alog, dskip, normg):
    t = xa.shape[0]
    nb = t // BLK
    gw = SSD_D // SSD_G

    def body(xa_ref, z_ref, sm_ref, dtb_ref, al_ref, ds_ref, ng_ref, y_ref, sp_ref, st):
        c = pl.program_id(0)

        @pl.when(c == 0)
        def _():
            st[...] = jnp.zeros_like(st)

        ok = _valid_rows(BLK, c * BLK)
        sp = st[...]
        sp_ref[...] = sp
        d = _ssd_chunk(xa_ref[...], sm_ref[...], dtb_ref[...], al_ref[...], ds_ref[...], ok, sp)
        st[...] = jnp.concatenate(d["new_s"], axis=1)
        y = d["y"]
        z = z_ref[...]
        yg = y * (z * _sig(z))
        outs = []
        for g in range(SSD_G):
            v = yg[:, gw * g:gw * (g + 1)]
            outs.append(v * lax.rsqrt(jnp.mean(v * v, axis=1, keepdims=True) + EPS))
        y_ref[...] = (jnp.concatenate(outs, axis=1) * ng_ref[...]).astype(BF16)

    vec = pl.BlockSpec((1, BLK), lambda c: (0, 0))
    return pl.pallas_call(
        body, name="ssd_fwd", grid=(nb,),
        in_specs=[pl.BlockSpec((BLK, CONV_D), lambda c: (c, 0)),
                  pl.BlockSpec((BLK, SSD_D), lambda c: (c, C_Z // SSD_D)),
                  pl.BlockSpec((BLK, BLK), lambda c: (c, C_SM // BLK)),
                  vec, vec, vec, pl.BlockSpec((1, SSD_D), lambda c: (0, 0))],
        out_specs=[pl.BlockSpec((BLK, SSD_D), lambda c: (c, 0)),
                   pl.BlockSpec((None, SSD_N, SSD_D), lambda c: (c, 0, 0))],
        out_shape=[jax.ShapeDtypeStruct((t, SSD_D), BF16), jax.ShapeDtypeStruct((nb, SSD_N, SSD_D), F32)],
        scratch_shapes=[pltpu.VMEM((SSD_N, SSD_D), F32)],
        compiler_params=_cp("arbitrary"),
    )(xa, proj, proj, dtb, alog, dskip, normg)


def _lane_put(col, lane):
    li = lax.broadcasted_iota(jnp.int32, (col.shape[0], BLK), 1)
    return jnp.where(li == lane, col, 0.0)


def ssd_bwd(dmix, xa, proj, sprev, dtb, alog, dskip, normg):
    t = xa.shape[0]
    nb = t // BLK
    gw = SSD_D // SSD_G
    rev = lambda c: nb - 1 - c

    def body(dy_ref, xa_ref, z_ref, sm_ref, sp_ref, dtb_ref, al_ref, ds_ref, ng_ref,
             dxa_ref, dz_ref, dsm_ref, dng_ref, dds_ref, dal_ref, ddtb_ref, dst):
        c = pl.program_id(0)

        @pl.when(c == 0)
        def _():
            dst[...] = jnp.zeros_like(dst)
            dng_ref[...] = jnp.zeros_like(dng_ref)
            dds_ref[...] = jnp.zeros_like(dds_ref)
            dal_ref[...] = jnp.zeros_like(dal_ref)
            ddtb_ref[...] = jnp.zeros_like(ddtb_ref)

        ok = _valid_rows(BLK, rev(c) * BLK)
        sm = sm_ref[...]
        sp = sp_ref[...]
        d = _ssd_chunk(xa_ref[...], sm, dtb_ref[...], al_ref[...], ds_ref[...], ok, sp)
        dt, amat, ac, act, tri, et, xs, xdt = (d[k] for k in ("dt", "amat", "ac", "act", "tri", "et", "xs", "xdt"))
        rowi = lax.broadcasted_iota(jnp.int32, (BLK, 1), 0)
        y = d["y"]
        z = z_ref[...]
        sgz = _sig(z)
        siluz = z * sgz
        yg = y * siluz
        dout = dy_ref[...]
        ng = ng_ref[...]
        dygs, xhs = [], []
        for g in range(SSD_G):
            v = yg[:, gw * g:gw * (g + 1)]
            rr = lax.rsqrt(jnp.mean(v * v, axis=1, keepdims=True) + EPS)
            xh = v * rr
            dxh = dout[:, gw * g:gw * (g + 1)] * ng[:, gw * g:gw * (g + 1)]
            dygs.append(rr * (dxh - xh * jnp.mean(dxh * xh, axis=1, keepdims=True)))
            xhs.append(xh)
        dyg = jnp.concatenate(dygs, axis=1)
        dng_ref[...] += jnp.sum(dout * jnp.concatenate(xhs, axis=1), axis=0, keepdims=True)
        dy = dyg * siluz
        dz_ref[...] = dyg * y * (sgz * (1.0 + z * (1.0 - sgz)))

        triu = _tri(lower=False)
        dyb = dy.astype(BF16)
        dsn = dst[...]
        dds_ref[...] += _nn_hi(jnp.sum(dy * xs, axis=0, keepdims=True), et)
        dac_all = _nn_hi(dy * jnp.concatenate(d["yo"], axis=1), et)
        dyo = (dy * d["ee"]).astype(BF16)
        gl = jnp.exp(ac[BLK - 1:BLK, :])
        dlast = _nn_hi(jnp.sum(dsn * sp, axis=0, keepdims=True), et) * gl
        bds, db_g, dc_g, dxdt_i, new_dst = [], [], [], [], []
        for g in range(SSD_G):
            cols = slice(GW * g, GW * (g + 1))
            bg, cg = d["bg"][g], d["cg"][g]
            dsng = dsn[:, cols].astype(BF16)
            dc = _nt(dyo[:, cols], sp[:, cols].astype(BF16))
            new_dst.append(_tn(cg, dyo[:, cols]) + d["gle"][:, cols] * dsn[:, cols])
            bds.append(_nn(bg, dsng))
            db = _nt(d["decx"][:, cols].astype(BF16), dsng)
            cbt = _nt(bg, cg)
            dcb = jnp.zeros((BLK, BLK), F32)
            for h in range(HPG * g, HPG * (g + 1)):
                hc = slice(SSD_P * h, SSD_P * (h + 1))
                dm = _nt(dyb[:, hc], d["xdtb"][:, hc])
                dcb += dm * d["seg"][h]
                w = dm * d["m"][h]
                dac_all += _lane_put(jnp.sum(w, axis=1, keepdims=True) - jnp.sum(w.T, axis=1, keepdims=True), h)
                segt = jnp.where(triu, jnp.exp(jnp.minimum(act[h:h + 1, :] - ac[:, h:h + 1], 0.0)), 0.0)
                dxdt_i.append(_nn((cbt * segt).astype(BF16), dyb[:, hc]))
            dcbb = dcb.astype(BF16)
            dc_g.append(dc + _nn(dcbb, bg))
            db_g.append(db + _tn(dcbb, cg))
        dst[...] = jnp.concatenate(new_dst, axis=1)
        bds = jnp.concatenate(bds, axis=1)
        tdec = jnp.exp(ac[BLK - 1:BLK, :] - ac) * _nn_hi(xdt * bds, et)
        dlast += jnp.sum(tdec, axis=0, keepdims=True)
        dac_all += jnp.where(rowi == BLK - 1, dlast, 0.0) - tdec
        dxdt = d["dece"] * bds + jnp.concatenate(dxdt_i, axis=1)
        da = _nn_hi(triu.astype(F32), dac_all)
        ddt = _nn_hi(dxdt * xs, et) + da * amat
        dal_ref[...] += jnp.sum(da * dt, axis=0, keepdims=True) * amat
        ddtr = jnp.where(ok, ddt * _sig(sm + dtb_ref[...]), 0.0)
        ddtb_ref[...] += jnp.sum(ddtr, axis=0, keepdims=True)
        dsm_ref[...] = ddtr
        dxs = d["dse"] * dy + dxdt * d["dte"]
        dxa_ref[...] = jnp.where(ok, jnp.concatenate([dxs] + db_g + dc_g, axis=1), 0.0)

    vec = pl.BlockSpec((1, BLK), lambda c: (0, 0))
    nvec = pl.BlockSpec((1, SSD_D), lambda c: (0, 0))
    return pl.pallas_call(
        body, name="ssd_bwd", grid=(nb,),
        in_specs=[pl.BlockSpec((BLK, SSD_D), lambda c: (rev(c), 0)),
                  pl.BlockSpec((BLK, CONV_D), lambda c: (rev(c), 0)),
                  pl.BlockSpec((BLK, SSD_D), lambda c: (rev(c), C_Z // SSD_D)),
                  pl.BlockSpec((BLK, BLK), lambda c: (rev(c), C_SM // BLK)),
                  pl.BlockSpec((None, SSD_N, SSD_D), lambda c: (rev(c), 0, 0)),
                  vec, vec, vec, nvec],
        out_specs=[pl.BlockSpec((BLK, CONV_D), lambda c: (rev(c), 0)),
                   pl.BlockSpec((BLK, SSD_D), lambda c: (rev(c), 0)),
                   pl.BlockSpec((BLK, BLK), lambda c: (rev(c), 0)),
                   nvec, vec, vec, vec],
        out_shape=[jax.ShapeDtypeStruct((t, CONV_D), F32), jax.ShapeDtypeStruct((t, SSD_D), F32),
                   jax.ShapeDtypeStruct((t, BLK), F32), jax.ShapeDtypeStruct((1, SSD_D), F32),
                   jax.ShapeDtypeStruct((1, BLK), F32), jax.ShapeDtypeStruct((1, BLK), F32),
                   jax.ShapeDtypeStruct((1, BLK), F32)],
        scratch_shapes=[pltpu.VMEM((SSD_N, SSD_D), F32)],
        compiler_params=_cp("arbitrary"),
    )(dmix, xa, proj, proj, sprev, dtb, alog, dskip, normg)


def _segments(nb, fine):
    if fine:
        cuts = list(range(0, nb, 2)) + [nb]
    else:
        cuts = sorted({0, nb} | {max(1, round(nb * f)) for f in (0.3, 0.53, 0.77)})
    return list(zip(cuts[:-1], cuts[1:]))


def attn_fwd(q, k, v, qcol, kcol, vcol, nh, dq, dv, scale, c_col=None, c_row=None, lane0=0):
    t = q.shape[0]
    tq = BLK
    use_bias = c_col is not None

    def body(*refs):
        if use_bias:
            q_ref, k_ref, v_ref, cc_ref, cr_ref, o_ref, l_ref = refs
        else:
            q_ref, k_ref, v_ref, o_ref, l_ref = refs
        i = pl.program_id(0)
        rowg = i * tq + lax.broadcasted_iota(jnp.int32, (tq, 1), 0)

        def tile(tk):
            col = lax.broadcasted_iota(jnp.int32, (1, tk), 1)
            mask = (col <= rowg) & (col >= PAD)
            outs = []
            lse = jnp.zeros((tq, BLK), F32)
            for h in range(nh):
                s = _nt(q_ref[:, dq * h:dq * (h + 1)].astype(BF16), k_ref[0:tk, dq * h:dq * (h + 1)].astype(BF16)) * scale
                if use_bias:
                    s = s + (cc_ref[:, lane0 + h:lane0 + h + 1] - cr_ref[h:h + 1, 0:tk])
                s = jnp.where(mask, s, NEG)
                m = jnp.max(s, axis=1, keepdims=True)
                p = jnp.exp(s - m)
                l = jnp.sum(p, axis=1, keepdims=True)
                outs.append(_nn(p.astype(BF16), v_ref[0:tk, dv * h:dv * (h + 1)].astype(BF16)) / l)
                lse += _lane_put(m + jnp.log(l), h)
            o_ref[...] = jnp.concatenate(outs, axis=1).astype(BF16)
            l_ref[...] = lse.T[0:8, :]

        for t0, t1 in _segments(t // tq, True):
            pl.when((i >= t0) & (i < t1))(functools.partial(tile, t1 * BLK))

    in_specs = [pl.BlockSpec((tq, nh * dq), lambda i: (i, qcol)),
                pl.BlockSpec((t, nh * dq), lambda i: (0, kcol)),
                pl.BlockSpec((t, nh * dv), lambda i: (0, vcol))]
    args = [q, k, v]
    if use_bias:
        in_specs += [pl.BlockSpec((tq, BLK), lambda i: (i, 0)), pl.BlockSpec((8, t), lambda i: (0, 0))]
        args += [c_col, c_row]
    return pl.pallas_call(
        body, name="attn_fwd", grid=(t // tq,),
        in_specs=in_specs,
        out_specs=[pl.BlockSpec((tq, nh * dv), lambda i: (i, 0)), pl.BlockSpec((8, tq), lambda i: (0, i))],
        out_shape=[jax.ShapeDtypeStruct((t, nh * dv), BF16), jax.ShapeDtypeStruct((8, t), F32)],
        compiler_params=_cp("arbitrary"),
    )(*args)


def attn_bwd(q, k, v, do, lse_row, o, qcol, kcol, vcol, docol, ocol, nh, dq, dv, scale, c_col=None, c_row=None, lane0=0):
    t = q.shape[0]
    tq = BLK
    use_bias = c_col is not None
    nq = t // tq

    def body(*refs):
        if use_bias:
            (q_ref, k_ref, v_ref, do_ref, l_ref, o_ref, cc_ref, cr_ref, dq_ref, dk_ref, dv_ref, dcq_ref, dck_ref,
             kt, ckb, dacc) = refs
        else:
            q_ref, k_ref, v_ref, do_ref, l_ref, o_ref, dq_ref, dk_ref, dv_ref, kt = refs
        i = pl.program_id(0)

        @pl.when(i == 0)
        def _():
            kt[...] = k_ref[...].astype(BF16).T
            dk_ref[...] = jnp.zeros_like(dk_ref)
            dv_ref[...] = jnp.zeros_like(dv_ref)
            if use_bias:
                dacc[...] = jnp.zeros_like(dacc)
                for h in range(nh):
                    ckb[h] = jnp.broadcast_to(cc_ref[:, lane0 + h:lane0 + h + 1], (t, BLK))

        qry = i * tq + lax.broadcasted_iota(jnp.int32, (1, tq), 1)
        dot = (do_ref[...].astype(F32) * o_ref[...].astype(F32)).T

        def tile(tk):
            key = lax.broadcasted_iota(jnp.int32, (tk, 1), 0)
            mask = (key <= qry) & (key >= PAD)
            dqts, dcqs = [], []
            for h in range(nh):
                qh = q_ref[:, dq * h:dq * (h + 1)].astype(BF16)
                kh = k_ref[0:tk, dq * h:dq * (h + 1)].astype(BF16)
                vh = v_ref[0:tk, dv * h:dv * (h + 1)].astype(BF16)
                doh = do_ref[:, dv * h:dv * (h + 1)].astype(BF16)
                delta = jnp.sum(dot[dv * h:dv * (h + 1), :], axis=0, keepdims=True)
                st = _nt(kh, qh) * scale
                if use_bias:
                    st = st + (cr_ref[h:h + 1, :] - ckb[h, 0:tk, :])
                pt = jnp.exp(jnp.where(mask, st, NEG) - l_ref[h:h + 1, :])
                dst = pt * (_nt(vh, doh) - delta)
                dsb = dst.astype(BF16)
                dk_ref[0:tk, dq * h:dq * (h + 1)] += _nn(dsb, qh) * scale
                dv_ref[0:tk, dv * h:dv * (h + 1)] += _nn(pt.astype(BF16), doh)
                dqts.append(_nn(kt[dq * h:dq * (h + 1), 0:tk], dsb))
                if use_bias:
                    dcqs.append(jnp.sum(dst, axis=0, keepdims=True))
                    dacc[h, 0:tk, :] += dst
            dq_ref[...] = jnp.concatenate(dqts, axis=0).T * scale
            if use_bias:
                dcq_ref[...] = jnp.concatenate(dcqs + [jnp.zeros((8 - nh, tq), F32)], axis=0)

        for t0, t1 in _segments(nq, not use_bias):
            pl.when((i >= t0) & (i < t1))(functools.partial(tile, t1 * BLK))

        if use_bias:
            @pl.when(i == nq - 1)
            def _():
                lane = lax.broadcasted_iota(jnp.int32, (1, BLK), 1)
                tot = jnp.zeros((t, BLK), F32)
                for h in range(nh):
                    tot += jnp.where(lane == lane0 + h, jnp.sum(dacc[h], axis=1, keepdims=True), 0.0)
                dck_ref[...] = tot

    keys_q = pl.BlockSpec((t, nh * dq), lambda i: (0, 0))
    keys_v = pl.BlockSpec((t, nh * dv), lambda i: (0, 0))
    keys_c = pl.BlockSpec((t, BLK), lambda i: (0, 0))
    qrow = pl.BlockSpec((8, tq), lambda i: (0, i))
    in_specs = [pl.BlockSpec((tq, nh * dq), lambda i: (i, qcol)),
                pl.BlockSpec((t, nh * dq), lambda i: (0, kcol)),
                pl.BlockSpec((t, nh * dv), lambda i: (0, vcol)),
                pl.BlockSpec((tq, nh * dv), lambda i: (i, docol)),
                qrow,
                pl.BlockSpec((tq, nh * dv), lambda i: (i, ocol))]
    args = [q, k, v, do, lse_row, o]
    out_specs = [pl.BlockSpec((tq, nh * dq), lambda i: (i, 0)), keys_q, keys_v]
    out_shape = [jax.ShapeDtypeStruct((t, nh * dq), F32), jax.ShapeDtypeStruct((t, nh * dq), F32),
                 jax.ShapeDtypeStruct((t, nh * dv), F32)]
    scratch = [pltpu.VMEM((nh * dq, t), BF16)]
    if use_bias:
        in_specs += [keys_c, qrow]
        args += [c_col, c_row]
        out_specs += [qrow, keys_c]
        out_shape += [jax.ShapeDtypeStruct((8, t), F32), jax.ShapeDtypeStruct((t, BLK), F32)]
        scratch += [pltpu.VMEM((nh, t, BLK), F32), pltpu.VMEM((nh, t, BLK), F32)]
    return pl.pallas_call(
        body, name="attn_bwd", grid=(nq,),
        in_specs=in_specs, out_specs=out_specs, out_shape=out_shape, scratch_shapes=scratch,
        compiler_params=_cp("arbitrary"),
    )(*args)


def fox_pre(proj, fb):
    t = proj.shape[0]
    nb = t // BLK

    def body(sm_ref, fb_ref, c_ref, cr_ref):
        x = sm_ref[...] + fb_ref[...]
        lane = lax.broadcasted_iota(jnp.int32, (1, BLK), 1)
        keep = _valid_rows(t, 0) & (lane >= SM_F) & (lane < SM_F + FOX_H)
        logf = jnp.where(keep, jnp.minimum(x, 0.0) - jnp.log(1.0 + jnp.exp(-jnp.abs(x))), 0.0)
        tri = _tri().astype(F32)
        carry = jnp.zeros((1, BLK), F32)
        for b in range(nb):
            cb = _nn_hi(tri, logf[b * BLK:(b + 1) * BLK, :]) + carry
            c_ref[b * BLK:(b + 1) * BLK, :] = cb
            carry = cb[BLK - 1:BLK, :]
        cr_ref[...] = c_ref[...].T[SM_F:SM_F + 8, :]

    return pl.pallas_call(
        body, name="fox_pre", grid=(1,),
        in_specs=[pl.BlockSpec((t, BLK), lambda i: (0, C_SM // BLK)), pl.BlockSpec((1, BLK), lambda i: (0, 0))],
        out_specs=[pl.BlockSpec((t, BLK), lambda i: (0, 0)), pl.BlockSpec((8, t), lambda i: (0, 0))],
        out_shape=[jax.ShapeDtypeStruct((t, BLK), F32), jax.ShapeDtypeStruct((8, t), F32)],
        compiler_params=_cp("arbitrary"),
    )(proj, fb)


def fox_pre_bwd(dcq, dck, proj, fb, dsm_in):
    t = proj.shape[0]
    nb = t // BLK

    def body(dcq_ref, dck_ref, sm_ref, fb_ref, din_ref, dsm_ref, dfb_ref, scr):
        triu = _tri(lower=False).astype(F32)
        carry = jnp.zeros((1, BLK), F32)
        scr[...] = jnp.concatenate([jnp.zeros((SM_F, t), F32), dcq_ref[...], jnp.zeros((BLK - SM_F - 8, t), F32)], axis=0).T
        for b in range(nb - 1, -1, -1):
            blk = scr[b * BLK:(b + 1) * BLK, :] - dck_ref[b * BLK:(b + 1) * BLK, :]
            cb = _nn_hi(triu, blk) + carry
            scr[b * BLK:(b + 1) * BLK, :] = cb
            carry = cb[0:1, :]
        x = sm_ref[...] + fb_ref[...]
        lane = lax.broadcasted_iota(jnp.int32, (1, BLK), 1)
        keep = _valid_rows(t, 0) & (lane >= SM_F) & (lane < SM_F + FOX_H)
        df = jnp.where(keep, scr[...] * _sig(-x), 0.0)
        dfb_ref[...] = jnp.sum(df, axis=0, keepdims=True)
        dsm_ref[...] = din_ref[...] + df

    full = pl.BlockSpec((t, BLK), lambda i: (0, 0))
    return pl.pallas_call(
        body, name="fox_pre_bwd", grid=(1,),
        in_specs=[pl.BlockSpec((8, t), lambda i: (0, 0)), full,
                  pl.BlockSpec((t, BLK), lambda i: (0, C_SM // BLK)), pl.BlockSpec((1, BLK), lambda i: (0, 0)), full],
        out_specs=[full, pl.BlockSpec((1, BLK), lambda i: (0, 0))],
        out_shape=[jax.ShapeDtypeStruct((t, BLK), F32), jax.ShapeDtypeStruct((1, BLK), F32)],
        scratch_shapes=[pltpu.VMEM((t, BLK), F32)],
        compiler_params=_cp("arbitrary"),
    )(dcq, dck, proj, fb, dsm_in)


def _swap_rope(x):
    lane = lax.broadcasted_iota(jnp.int32, (1, BLK), 1)
    return jnp.where((lane >= SM_KR) & (lane < SM_KR + 16), pltpu.roll(x, BLK - 16, 1),
                     jnp.where((lane >= SM_KR + 16) & (lane < SM_KR + 32), pltpu.roll(x, 16, 1), 0.0))


def _rms(x, g):
    r = lax.rsqrt(jnp.mean(x * x, axis=1, keepdims=True) + EPS)
    return r, x * r


def mla_pre(proj, qg, kvg, wq, wk, wv, cosq, sinq):
    t = proj.shape[0]
    tm = _row_tile(t)

    def body(cq_ref, ckv_ref, sm_ref, qg_ref, kvg_ref, wq_ref, wk_ref, wv_ref, cos_ref, sin_ref,
             q_ref, k_ref, v_ref, cqn_ref, ckvn_ref):
        cs, sn = cos_ref[...], sin_ref[...]
        _, xh = _rms(cq_ref[...], None)
        cqn = (xh * qg_ref[...]).astype(BF16)
        cqn_ref[...] = cqn
        qraw = _nn(cqn, wq_ref[...])
        qs = []
        for h in range(MLA_H):
            hb = qraw[:, BLK * h:BLK * (h + 1)]
            qs.append(hb * cs + _swap_rope(hb) * sn)
        q_ref[...] = jnp.concatenate(qs, axis=1).astype(BF16)
        _, kh = _rms(ckv_ref[...], None)
        ckvn = (kh * kvg_ref[...]).astype(BF16)
        ckvn_ref[...] = ckvn
        kraw = _nn(ckvn, wk_ref[...])
        v_ref[...] = _nn(ckvn, wv_ref[...]).astype(BF16)
        lane = lax.broadcasted_iota(jnp.int32, (1, BLK), 1)
        kr = sm_ref[...]
        krr = jnp.where((lane >= SM_KR) & (lane < SM_KR + MLA_ROPE), kr * cs + _swap_rope(kr) * sn, 0.0)
        k_ref[...] = jnp.concatenate([kraw[:, BLK * h:BLK * (h + 1)] + krr for h in range(MLA_H)], axis=1).astype(BF16)

    def rows(w, cb):
        return pl.BlockSpec((tm, w), lambda i: (i, cb))

    def whole(a):
        return pl.BlockSpec(a.shape, lambda i: (0, 0))

    return pl.pallas_call(
        body, name="mla_pre", grid=(t // tm,),
        in_specs=[rows(MLA_QL, C_CQ // MLA_QL), rows(MLA_KVL, C_CKV // MLA_KVL), rows(BLK, C_SM // BLK),
                  whole(qg), whole(kvg), whole(wq), whole(wk), whole(wv), rows(BLK, 0), rows(BLK, 0)],
        out_specs=[rows(512, 0), rows(512, 0), rows(256, 0), rows(MLA_QL, 0), rows(MLA_KVL, 0)],
        out_shape=[jax.ShapeDtypeStruct((t, 512), BF16), jax.ShapeDtypeStruct((t, 512), BF16),
                   jax.ShapeDtypeStruct((t, 256), BF16), jax.ShapeDtypeStruct((t, MLA_QL), BF16),
                   jax.ShapeDtypeStruct((t, MLA_KVL), BF16)],
        compiler_params=_cp("arbitrary"),
    )(proj, proj, proj, qg, kvg, wq, wk, wv, cosq, sinq)


def mla_pre_bwd(dq, dk, dv, proj, cqn, ckvn, qg, kvg, wq, wk, wv, cosq, sinq, dsm_in):
    t = proj.shape[0]
    tm = _row_tile(t)

    def body(dq_ref, dk_ref, dv_ref, cq_ref, ckv_ref, cqn_ref, ckvn_ref, qg_ref, kvg_ref, wq_ref, wk_ref, wv_ref,
             cos_ref, sin_ref, din_ref, dcq_ref, dckv_ref, dsm_ref, dwq_ref, dwk_ref, dwv_ref, dqg_ref, dkvg_ref):
        i = pl.program_id(0)

        @pl.when(i == 0)
        def _():
            for r in (dwq_ref, dwk_ref, dwv_ref, dqg_ref, dkvg_ref):
                r[...] = jnp.zeros_like(r)

        cs, sn = cos_ref[...], sin_ref[...]
        lane = lax.broadcasted_iota(jnp.int32, (1, BLK), 1)

        def unrope(dy):
            return dy * cs + _swap_rope(dy * sn)

        dqp = jnp.concatenate([unrope(dq_ref[:, BLK * h:BLK * (h + 1)]) for h in range(MLA_H)], axis=1).astype(BF16)
        dwq_ref[...] += _tn(cqn_ref[...], dqp)
        dcqn = _nt(dqp, wq_ref[...])
        r, xh = _rms(cq_ref[...], None)
        dqg_ref[...] += jnp.sum(dcqn * xh, axis=0, keepdims=True)
        dxh = dcqn * qg_ref[...]
        dcq_ref[...] = r * (dxh - xh * jnp.mean(dxh * xh, axis=1, keepdims=True))

        dkn, dkr = [], jnp.zeros((tm, BLK), F32)
        for h in range(MLA_H):
            blk = dk_ref[:, BLK * h:BLK * (h + 1)]
            dkn.append(jnp.where(lane < MLA_NOPE, blk, 0.0))
            dkr += jnp.where((lane >= SM_KR) & (lane < SM_KR + MLA_ROPE), blk, 0.0)
        dknb = jnp.concatenate(dkn, axis=1).astype(BF16)
        dvb = dv_ref[...].astype(BF16)
        ckvn = ckvn_ref[...]
        dwk_ref[...] += _tn(ckvn, dknb)
        dwv_ref[...] += _tn(ckvn, dvb)
        dckvn = _nt(dknb, wk_ref[...]) + _nt(dvb, wv_ref[...])
        r2, kh = _rms(ckv_ref[...], None)
        dkvg_ref[...] += jnp.sum(dckvn * kh, axis=0, keepdims=True)
        dkh = dckvn * kvg_ref[...]
        dckv_ref[...] = r2 * (dkh - kh * jnp.mean(dkh * kh, axis=1, keepdims=True))
        dsm_ref[...] = din_ref[...] + jnp.where((lane >= SM_KR) & (lane < SM_KR + MLA_ROPE), unrope(dkr), 0.0)

    def rows(w, cb):
        return pl.BlockSpec((tm, w), lambda i: (i, cb))

    def whole(a):
        return pl.BlockSpec(a.shape, lambda i: (0, 0))

    def wshape(a):
        return jax.ShapeDtypeStruct(a.shape, F32)

    return pl.pallas_call(
        body, name="mla_pre_bwd", grid=(t // tm,),
        in_specs=[rows(512, 0), rows(512, 0), rows(256, 0), rows(MLA_QL, C_CQ // MLA_QL), rows(MLA_KVL, C_CKV // MLA_KVL),
                  rows(MLA_QL, 0), rows(MLA_KVL, 0), whole(qg), whole(kvg), whole(wq), whole(wk), whole(wv),
                  rows(BLK, 0), rows(BLK, 0), rows(BLK, 0)],
        out_specs=[rows(MLA_QL, 0), rows(MLA_KVL, 0), rows(BLK, 0), whole(wq), whole(wk), whole(wv), whole(qg), whole(kvg)],
        out_shape=[jax.ShapeDtypeStruct((t, MLA_QL), F32), jax.ShapeDtypeStruct((t, MLA_KVL), F32),
                   jax.ShapeDtypeStruct((t, BLK), F32), wshape(wq), wshape(wk), wshape(wv), wshape(qg), wshape(kvg)],
        compiler_params=_cp("arbitrary"),
    )(dq, dk, dv, proj, proj, cqn, ckvn, qg, kvg, wq, wk, wv, cosq, sinq, dsm_in)


def _slot_sum(me, own, recv_ref):
    gg = own.astype(F32)
    for s in range(N_DEV):
        gg = gg + jnp.where(me == s, 0.0, recv_ref[s].astype(F32))
    return gg


def adamw(w, m, v, g=None, recv=None, own=None, me_arr=None):
    shape = w.shape
    c = shape[-1]
    from_recv = recv is not None
    if not from_recv:
        me_arr = jnp.zeros((1,), jnp.int32)
    nl = len(recv) if from_recv else 1
    rws = w.size // c // nl
    tr = rws
    for d in (1024, 512, 352, 256, 128, 64, 32, 16, 8):
        if rws % d == 0 and d * c * 4 <= (2 << 20):
            tr = d
            break
    nt = rws // tr
    w2, m2, v2 = (a.reshape(nl, rws, c) for a in (w, m, v))
    if from_recv:
        gin = [a.reshape(N_DEV, rws, c) for a in list(recv) + list(own)]
    else:
        gin = [g.reshape(1, rws, c)]

    def body(me_ref, w_ref, m_ref, v_ref, *rest):
        g_refs, outs = rest[:len(gin)], rest[len(gin):]
        if from_recv:
            g_out, outs = outs[0], outs[1:]
            for li in range(nl):
                @pl.when(pl.program_id(0) == li)
                def _(li=li):
                    g_out[...] = _slot_sum(me_ref[0], g_refs[nl + li][...], g_refs[li])
            gg = g_out[...]
        else:
            gg = g_refs[0][...]
        d_ref, nm_ref, nv_ref = outs
        nm = B1 * m_ref[...] + (1.0 - B1) * gg
        nv = B2 * v_ref[...] + (1.0 - B2) * (gg * gg)
        mh = nm / (1.0 - B1 ** STEP)
        vh = nv / (1.0 - B2 ** STEP)
        d_ref[...] = -LR * (mh / (jnp.sqrt(vh) + AEPS) + WD * w_ref[...])
        nm_ref[...] = nm
        nv_ref[...] = nv

    row = pl.BlockSpec((None, tr, c), lambda l, i, me: (l, i, 0))
    if from_recv:
        gspecs = [pl.BlockSpec((N_DEV, tr, c), lambda l, i, me, li=li: (0, jnp.where(l == li, i, 0), 0))
                  for li in range(nl)]
        gspecs += [pl.BlockSpec((None, tr, c), lambda l, i, me, li=li: (me[0], jnp.where(l == li, i, 0), 0))
                   for li in range(nl)]
    else:
        gspecs = [row]
    nout = 4 if from_recv else 3
    outs = pl.pallas_call(
        body, name="adamw",
        grid_spec=pltpu.PrefetchScalarGridSpec(num_scalar_prefetch=1, grid=(nl, nt), in_specs=[row, row, row] + gspecs,
                                               out_specs=[row] * nout),
        out_shape=[jax.ShapeDtypeStruct((nl, rws, c), F32)] * nout,
        compiler_params=_cp("arbitrary", "arbitrary"),
    )(me_arr, w2, m2, v2, *gin)
    return tuple(o.reshape(shape) for o in outs)


def sum_slots(recv, own=None, me_arr=None):
    _, r, c = recv.shape
    if own is None:
        own, me_arr = recv, jnp.zeros((1,), jnp.int32)
        plain = True
    else:
        plain = False

    def body(me_ref, r_ref, own_ref, o_ref):
        if plain:
            gg = r_ref[0].astype(F32)
            for s in range(1, N_DEV):
                gg = gg + r_ref[s].astype(F32)
            o_ref[...] = gg
        else:
            o_ref[...] = _slot_sum(me_ref[0], own_ref[...], r_ref)

    return pl.pallas_call(
        body, name="sum_slots",
        grid_spec=pltpu.PrefetchScalarGridSpec(
            num_scalar_prefetch=1, grid=(1,),
            in_specs=[pl.BlockSpec((N_DEV, r, c), lambda i, me: (0, 0, 0)),
                      pl.BlockSpec((None, r, c), lambda i, me: (me[0], 0, 0))],
            out_specs=pl.BlockSpec((r, c), lambda i, me: (0, 0))),
        out_shape=jax.ShapeDtypeStruct((r, c), F32),
        compiler_params=_cp("arbitrary"),
    )(me_arr, recv, own)


_FLIPS = [(0, 0, 1), (0, 1, 0), (0, 1, 1), (1, 0, 0), (1, 0, 1), (1, 1, 0), (1, 1, 1)]
_ANY = pl.BlockSpec(memory_space=pl.ANY)


def _mesh_place():
    x, y, c = lax.axis_index("x"), lax.axis_index("y"), lax.axis_index("c")
    me = 4 * x + 2 * y + c
    peers = [((x + fx) % 2, (y + fy) % 2, (c + fc) % 2) for fx, fy, fc in _FLIPS]
    return me, peers


def place_own(src, l, dtype, me_arr):
    _, r, c = src.shape
    tr = r
    for d in (512, 352, 256, 128, 64, 32, 16, 8):
        if r % d == 0 and d * c * 4 <= (2 << 20):
            tr = d
            break

    def body(me_ref, s_ref, o_ref):
        o_ref[...] = s_ref[...].astype(dtype)

    return pl.pallas_call(
        body, name="place_own",
        grid_spec=pltpu.PrefetchScalarGridSpec(
            num_scalar_prefetch=1, grid=(r // tr,),
            in_specs=[pl.BlockSpec((None, tr, c), lambda i, me: (l, i, 0))],
            out_specs=pl.BlockSpec((None, tr, c), lambda i, me: (me[0], i, 0))),
        out_shape=jax.ShapeDtypeStruct((N_DEV, r, c), dtype),
        compiler_params=_cp("arbitrary"),
    )(me_arr, src)


_HBM = pl.BlockSpec(memory_space=pltpu.HBM)
_SEMS = pl.BlockSpec(memory_space=pltpu.SEMAPHORE)
_EFFECT = pltpu.SideEffectType.DATAFLOW_SIDE_EFFECTING


def exchange_start(mode, arrays, name, after=None):
    n = len(arrays)
    gather = mode == "gather"
    ns = 0 if gather else n
    zones = list(arrays) if gather else [lax.empty(a.shape, a.dtype) for a in arrays]
    ops = ([] if gather else list(arrays)) + zones
    extra = [] if after is None else [after]

    def body(*refs):
        srcs, lands = refs[:ns], refs[ns:ns + n]
        send_sems, recv_sems = refs[ns + n + len(extra)], refs[ns + n + len(extra) + 1]
        token = refs[-1]
        me, peers = _mesh_place()
        ids = [4 * p[0] + 2 * p[1] + p[2] for p in peers]
        for j in range(n):
            for k in range(N_DEV - 1):
                src = lands[j].at[me] if gather else srcs[j].at[ids[k]]
                pltpu.make_async_remote_copy(src_ref=src, dst_ref=lands[j].at[me],
                                             send_sem=send_sems.at[j * (N_DEV - 1) + k],
                                             recv_sem=recv_sems.at[j * (N_DEV - 1) + k], device_id=peers[k],
                                             device_id_type=pl.DeviceIdType.MESH).start()
        token[...] = jnp.zeros_like(token)

    nsem = n * (N_DEV - 1)
    res = pl.pallas_call(
        body, name=name,
        in_specs=[_HBM] * (ns + n) + [_ANY] * len(extra),
        out_specs=(_SEMS, _SEMS, *[_HBM] * (ns + n), pl.BlockSpec(memory_space=pltpu.VMEM)),
        out_shape=(pltpu.SemaphoreType.DMA((nsem,)), pltpu.SemaphoreType.DMA((nsem,)),
                   *[pltpu.HBM(a.shape, a.dtype) for a in ops], jax.ShapeDtypeStruct((8, BLK), F32)),
        input_output_aliases={i: 2 + i for i in range(ns + n)},
        compiler_params=pltpu.CompilerParams(has_side_effects=_EFFECT),
    )(*[pltpu.with_memory_space_constraint(a, pltpu.HBM) for a in ops], *extra)
    return dict(gather=gather, send=res[0], recv=res[1], srcs=list(res[2:2 + ns]), lands=list(res[2 + ns:2 + ns + n]),
                token=res[-1])


def exchange_wait(hd, idxs, name, after):
    gather = hd["gather"]
    n = len(idxs)
    ns = 0 if gather else n
    ops = ([] if gather else [hd["srcs"][j] for j in idxs]) + [hd["lands"][j] for j in idxs]

    def body(*refs):
        srcs, lands = refs[:ns], refs[ns:ns + n]
        send_sems, recv_sems = refs[ns + n], refs[ns + n + 1]
        me, peers = _mesh_place()
        ids = [4 * p[0] + 2 * p[1] + p[2] for p in peers]
        for p, j in enumerate(idxs):
            for k in range(N_DEV - 1):
                src = lands[p].at[me] if gather else srcs[p].at[ids[k]]
                cp = pltpu.make_async_remote_copy(src_ref=src, dst_ref=lands[p].at[ids[k]],
                                                  send_sem=send_sems.at[j * (N_DEV - 1) + k],
                                                  recv_sem=recv_sems.at[j * (N_DEV - 1) + k], device_id=peers[k],
                                                  device_id_type=pl.DeviceIdType.MESH)
                cp.wait_send()
                cp.wait_recv()

    res = pl.pallas_call(
        body, name=name,
        in_specs=[_HBM] * (ns + n) + [_SEMS, _SEMS, _ANY],
        out_specs=[_HBM] * (ns + n),
        out_shape=[pltpu.HBM(a.shape, a.dtype) for a in ops],
        input_output_aliases={i: i for i in range(ns + n)},
        compiler_params=pltpu.CompilerParams(has_side_effects=_EFFECT),
    )(*ops, hd["send"], hd["recv"], after)
    return list(res[:ns]), list(res[ns:])


def _chip_place():
    x, y, c = lax.axis_index("x"), lax.axis_index("y"), lax.axis_index("c")
    chips = [((x + 1) % 2, y), (x, (y + 1) % 2), ((x + 1) % 2, (y + 1) % 2)]
    ident = lambda p: 4 * p[0] + 2 * p[1] + p[2]
    return dict(me=4 * x + 2 * y + c, sib=(x, y, 1 - c), sib_id=4 * x + 2 * y + 1 - c,
                same=[(cx, cy, c) for cx, cy in chips], same_ids=[ident((cx, cy, c)) for cx, cy in chips],
                other_ids=[ident((cx, cy, 1 - c)) for cx, cy in chips])


def _remote(src, dst, send_sem, recv_sem, dev):
    return pltpu.make_async_remote_copy(src_ref=src, dst_ref=dst, send_sem=send_sem, recv_sem=recv_sem, device_id=dev,
                                        device_id_type=pl.DeviceIdType.MESH)


def gather_start(zones, name):
    n = len(zones)

    def body(*refs):
        lands, send_sems, recv_sems, token = refs[:n], refs[n], refs[n + 1], refs[-1]
        pc = _chip_place()
        for j in range(n):
            own = lands[j].at[pc["me"]]
            for k, dev in enumerate([pc["sib"]] + pc["same"]):
                _remote(own, own, send_sems.at[4 * j + k], recv_sems.at[4 * j + k], dev).start()
        token[...] = jnp.zeros_like(token)

    res = pl.pallas_call(
        body, name=name,
        in_specs=[_HBM] * n,
        out_specs=(_SEMS, _SEMS, *[_HBM] * n, pl.BlockSpec(memory_space=pltpu.VMEM)),
        out_shape=(pltpu.SemaphoreType.DMA((4 * n,)), pltpu.SemaphoreType.DMA((4 * n,)),
                   *[pltpu.HBM(a.shape, a.dtype) for a in zones], jax.ShapeDtypeStruct((8, BLK), F32)),
        input_output_aliases={i: 2 + i for i in range(n)},
        compiler_params=pltpu.CompilerParams(has_side_effects=_EFFECT),
    )(*[pltpu.with_memory_space_constraint(a, pltpu.HBM) for a in zones])
    return dict(send=res[0], recv=res[1], lands=list(res[2:2 + n]), token=res[-1])


def gather_relay(hd, idxs, name, after):
    n = len(idxs)

    def body(*refs):
        lands, send_sems, recv_sems = refs[:n], refs[n], refs[n + 1]
        fsend, frecv, token = refs[n + 3 + n], refs[n + 4 + n], refs[-1]
        pc = _chip_place()
        for p, j in enumerate(idxs):
            for k in range(3):
                _remote(lands[p].at[pc["me"]], lands[p].at[pc["same_ids"][k]], send_sems.at[4 * j + 1 + k],
                        recv_sems.at[4 * j + 1 + k], pc["same"][k]).wait_recv()
        for p in range(n):
            for k in range(3):
                got = lands[p].at[pc["same_ids"][k]]
                _remote(got, got, fsend.at[3 * p + k], frecv.at[3 * p + k], pc["sib"]).start()
        token[...] = jnp.zeros_like(token)

    ops = [hd["lands"][j] for j in idxs]
    res = pl.pallas_call(
        body, name=name,
        in_specs=[_HBM] * n + [_SEMS, _SEMS, _ANY],
        out_specs=(*[_HBM] * n, _SEMS, _SEMS, pl.BlockSpec(memory_space=pltpu.VMEM)),
        out_shape=(*[pltpu.HBM(a.shape, a.dtype) for a in ops], pltpu.SemaphoreType.DMA((3 * n,)),
                   pltpu.SemaphoreType.DMA((3 * n,)), jax.ShapeDtypeStruct((8, BLK), F32)),
        input_output_aliases={i: i for i in range(n)},
        compiler_params=pltpu.CompilerParams(has_side_effects=_EFFECT),
    )(*ops, hd["send"], hd["recv"], after)
    return dict(lands=list(res[:n]), fsend=res[n], frecv=res[n + 1], token=res[-1])


def gather_wait(hd, rl, idxs, name, after):
    n = len(idxs)

    def body(*refs):
        lands, send_sems, recv_sems, fsend, frecv = refs[:n], refs[n], refs[n + 1], refs[n + 2], refs[n + 3]
        pc = _chip_place()
        for p, j in enumerate(idxs):
            own = lands[p].at[pc["me"]]
            for k, dev in enumerate([pc["sib"]] + pc["same"]):
                _remote(own, own, send_sems.at[4 * j + k], recv_sems.at[4 * j + k], dev).wait_send()
            _remote(own, lands[p].at[pc["sib_id"]], send_sems.at[4 * j], recv_sems.at[4 * j], pc["sib"]).wait_recv()
            for k in range(3):
                cp = _remote(lands[p].at[pc["same_ids"][k]], lands[p].at[pc["other_ids"][k]], fsend.at[3 * p + k],
                             frecv.at[3 * p + k], pc["sib"])
                cp.wait_send()
                cp.wait_recv()

    res = pl.pallas_call(
        body, name=name,
        in_specs=[_HBM] * n + [_SEMS, _SEMS, _SEMS, _SEMS, _ANY],
        out_specs=[_HBM] * n,
        out_shape=[pltpu.HBM(a.shape, a.dtype) for a in rl["lands"]],
        input_output_aliases={i: i for i in range(n)},
        compiler_params=pltpu.CompilerParams(has_side_effects=_EFFECT),
    )(*rl["lands"], hd["send"], hd["recv"], rl["fsend"], rl["frecv"], after)
    return list(res)


def _pad_cols(a, n):
    return jnp.pad(a, ((0, 0),) * (a.ndim - 1) + ((0, n - a.shape[-1]),))


def w_in_to_padded(w):
    z = lambda n: jnp.zeros(w.shape[:-1] + (n,), w.dtype)
    return jnp.concatenate([
        w[..., 0:1280], w[..., 1288:2056], w[..., 2060:2316], w[..., 2316:2444],
        w[..., 1280:1288], w[..., 2056:2060], z(SM_KR - SM_F - FOX_H), w[..., 2444:2476], z(BLK - SM_KR - MLA_ROPE)], axis=-1)


def w_in_from_padded(g):
    s = C_SM
    return jnp.concatenate([
        g[..., 0:1280], g[..., s + SM_DT:s + SM_DT + 8], g[..., 1280:2048], g[..., s + SM_F:s + SM_F + 4],
        g[..., 2048:2304], g[..., 2304:2432], g[..., s + SM_KR:s + SM_KR + MLA_ROPE]], axis=-1)


def _unshard_cols(gth):
    n, r, c = gth.shape
    return jnp.transpose(gth, (1, 0, 2)).reshape(r, n * c)


def _shard_cols(full):
    r, nc = full.shape
    return jnp.transpose(full.reshape(r, N_DEV, nc // N_DEV), (1, 0, 2))


def mla_weights(uq_g, ukv_g):
    uq = _unshard_cols(uq_g)
    dqh = MLA_NOPE + MLA_ROPE
    wq = jnp.concatenate([_pad_cols(uq[:, dqh * h:dqh * (h + 1)], BLK) for h in range(MLA_H)], axis=1)
    wk = jnp.concatenate([_pad_cols(ukv_g[2 * h], BLK) for h in range(MLA_H)], axis=1)
    wv = jnp.concatenate([ukv_g[2 * h + 1] for h in range(MLA_H)], axis=1)
    return wq, wk, wv


def mla_weight_grads(dwq, dwk, dwv):
    dqh = MLA_NOPE + MLA_ROPE
    duq = _shard_cols(jnp.concatenate([dwq[:, BLK * h:BLK * h + dqh] for h in range(MLA_H)], axis=1))
    parts = []
    for h in range(MLA_H):
        parts += [dwk[:, BLK * h:BLK * h + MLA_NOPE], dwv[:, MLA_V * h:MLA_V * (h + 1)]]
    return duq, jnp.stack(parts, axis=0)


def rope_tables(t):
    pos = (jnp.arange(t, dtype=jnp.int32) - PAD).astype(F32)
    inv_freq = 1.0 / (10000.0 ** (jnp.arange(0, MLA_ROPE, 2, dtype=F32) / MLA_ROPE))
    ang = pos[:, None] * inv_freq[None, :]
    cos, sin = jnp.cos(ang), jnp.sin(ang)
    one, zero = jnp.ones((t, SM_KR), F32), jnp.zeros((t, SM_KR), F32)
    tail = BLK - SM_KR - MLA_ROPE
    cosq = jnp.concatenate([one, cos, cos, jnp.ones((t, tail), F32)], axis=1)
    sinq = jnp.concatenate([zero, -sin, sin, jnp.zeros((t, tail), F32)], axis=1)
    return cosq, sinq


def _lanes(v, off=0):
    return jnp.pad(v.astype(F32), (off, BLK - off - v.shape[0]))[None, :]


def layer_fwd(x, ln, hb, getw, tabs, ahead):
    sv = {"h0b": hb}
    W = dict(getw("ffn1", hb))
    ln1 = (W["ln1_g"], W["ln1_b"])
    u, v, r1, h1b = ffn_fwd_seq(x, ln, W["g1"], W["u1"], W["d1"], ln1)
    sv.update(u1=u, v1=v, r1=r1, h1b=h1b)
    W.update(getw("mix", h1b))
    ln2 = (W["ln2_g"], W["ln2_b"])
    proj = mm_nn(h1b, W["w_in"])
    xa = conv_fwd(proj, W["conv_w"], W["conv_b"])
    y_ssd, sprev = ssd_fwd(xa, proj, W["dtb"], W["alog"], W["dskip"], W["normg"])
    c_col, c_row = fox_pre(proj, W["fb"])
    y_fox, lse_f = attn_fwd(proj, proj, proj, C_FQ // 256, C_FK // 256, C_FV // 256, FOX_H, FOX_DH, FOX_DH,
                            FOX_DH ** -0.5, c_col, c_row, SM_F)
    ahead(0, "ffn2", y_fox)
    q, k, vv, cqn, ckvn = mla_pre(proj, W["qg"], W["kvg"], W["wq"], W["wk"], W["wv"], *tabs)
    y_mla, lse_m = attn_fwd(q, k, vv, 0, 0, 0, MLA_H, BLK, MLA_V, (MLA_NOPE + MLA_ROPE) ** -0.5)
    mixcat = jnp.concatenate([y_ssd, y_fox, y_mla], axis=1)
    r2, h2b = mm_res_ln(mixcat, W["w_out"], r1, ln1, ln2)
    sv.update(proj=proj, xa=xa, sprev=sprev, c_col=c_col, c_row=c_row, lse_f=lse_f, q=q, k=k, v=vv, cqn=cqn, ckvn=ckvn,
              lse_m=lse_m, mixcat=mixcat, r2=r2, h2b=h2b)
    W.update(getw("ffn2", h2b))
    ahead(1, "ffn1", h2b)
    ln3 = (W["ln3_g"], W["ln3_b"])
    u, v, r3, h3b = ffn_fwd_seq(r2, ln2, W["g2"], W["u2"], W["d2"], ln3)
    sv.update(u2=u, v2=v, r3=r3, W=W)
    return r3, ln3, h3b, sv


def ffn_bwd(parts, r, gamma, hb_in, u, v, wg, wu, wd, after=None):
    dh, dwg, dwu, dwd, dg, db = ffn_bwd_seq(parts, r, gamma, hb_in, u, v, wg, wu, wd, after)
    return dh, dict(d=dwd, g=dwg, u=dwu, ln_g=dg, ln_b=db)


def layer_bwd(parts, sv, emit, tabs, after):
    G = {}
    W = sv["W"]
    dh2, g2 = ffn_bwd(parts, sv["r3"], W["ln3_g"], sv["h2b"], sv["u2"], sv["v2"], W["g2"], W["u2"], W["d2"], after)
    G.update(g2=g2["g"], u2=g2["u"], d2=g2["d"], ln3_g=g2["ln_g"], ln3_b=g2["ln_b"])
    tok = emit("ffn2", G)
    dr2, dmixb, G["ln2_g"], G["ln2_b"] = ln_bwd([(dh2, 1.0)], sv["r2"], W["ln2_g"], 1.0, tok)
    dmc = mm_nt_reduce([(dmixb[None], W["w_out"][None])], D)
    G["w_out"] = mm_tn(sv["mixcat"][None], dmixb[None])[0]
    proj = sv["proj"]
    dxa, dz, dsm, G["normg"], G["dskip"], G["alog"], G["dtb"] = ssd_bwd(
        dmc, sv["xa"], proj, sv["sprev"], W["dtb"], W["alog"], W["dskip"], W["normg"])
    dxbc, G["conv_w"], G["conv_b"] = conv_bwd(dxa, proj, W["conv_w"], W["conv_b"])
    dfq, dfk, dfv, dcq, dck = attn_bwd(proj, proj, proj, dmc, sv["lse_f"], sv["mixcat"], C_FQ // 256, C_FK // 256,
                                       C_FV // 256, 2, 2, FOX_H, FOX_DH, FOX_DH, FOX_DH ** -0.5, sv["c_col"], sv["c_row"], SM_F)
    dsm, G["fb"] = fox_pre_bwd(dcq, dck, proj, W["fb"], dsm)
    dq, dk, dv = attn_bwd(sv["q"], sv["k"], sv["v"], dmc, sv["lse_m"], sv["mixcat"], 0, 0, 0, 3, 3, MLA_H, BLK, MLA_V,
                          (MLA_NOPE + MLA_ROPE) ** -0.5)
    dcql, dckv, dsm, G["wq"], G["wk"], G["wv"], G["qg"], G["kvg"] = mla_pre_bwd(
        dq, dk, dv, proj, sv["cqn"], sv["ckvn"], W["qg"], W["kvg"], W["wq"], W["wk"], W["wv"], *tabs, dsm)
    dproj = jnp.concatenate([dz, dxbc, dfq, dfk, dfv, dcql, dckv, dsm], axis=1).astype(BF16)
    dh1p = mm_nt_reduce([(dproj[None], W["w_in"][None])], D)
    G["w_in"] = mm_tn(sv["h1b"][None], dproj[None])[0]
    tok = emit("mix", G)
    dh0, g1 = ffn_bwd([(dr2, ALPHA), (dh1p, 1.0)], sv["r1"], W["ln1_g"], sv["h0b"], sv["u1"], sv["v1"],
                      W["g1"], W["u1"], W["d1"], tok)
    G.update(g1=g1["g"], u1=g1["u"], d1=g1["d"], ln1_g=g1["ln_g"], ln1_b=g1["ln_b"])
    tok = emit("ffn1", G)
    return [(dh0, 1.0)], G, tok


def local_step(x, target, meta_full, getw, emit, ahead=lambda l, stage, after: None):
    t = x.shape[0] + BLK
    tabs = rope_tables(t)
    xr, hb = build_h0(meta_full, x)
    ln = None
    saved = []
    for l in range(NL):
        xr, ln, hb, sv = layer_fwd(xr, ln, hb, functools.partial(getw, l), tabs,
                                   lambda dl, stage, after, l=l: ahead(l + dl, stage, after))
        saved.append(sv)
    dy, loss = loss_head(xr, ln, target)
    parts = [(dy, 1.0)]
    grads = [None] * NL
    tok = None
    for l in range(NL - 1, -1, -1):
        parts, grads[l], tok = layer_bwd(parts, saved[l], functools.partial(emit, l), tabs, tok)
    gx, gmeta = split_dh0(parts[0][0], tok)
    return loss, gx, gmeta, grads


_SMALL = ["ln1_g", "ln1_b", "ln2_g", "ln2_b", "ln3_g", "ln3_b", "conv_b", "ssd_norm_g", "mla_q_norm_g",
          "mla_kv_norm_g", "dt_bias", "a_log", "d_skip", "fox_f_b"]
_SMALL_ROWS = 8
_BIG = ["ffn1_w_gate", "ffn1_w_up", "ffn1_w_down", "w_in", "conv_w", "mla_w_uq", "mla_w_ukv", "w_out",
        "ffn2_w_gate", "ffn2_w_up", "ffn2_w_down"]
_NAMES = ["meta", "ffn1_w_gate", "ffn1_w_up", "ffn1_w_down", "ln1_g", "ln1_b", "w_in", "conv_w", "conv_b", "dt_bias",
          "a_log", "d_skip", "ssd_norm_g", "fox_f_b", "mla_q_norm_g", "mla_w_uq", "mla_kv_norm_g", "mla_w_ukv", "w_out",
          "ln2_g", "ln2_b", "ffn2_w_gate", "ffn2_w_up", "ffn2_w_down", "ln3_g", "ln3_b"]


def pack_small(p):
    flat = jnp.concatenate([p[n].astype(F32) for n in _SMALL], axis=1)
    return _pad_cols(flat, _SMALL_ROWS * D).reshape(NL * _SMALL_ROWS, D)


def unpack_small(a, like):
    flat = a.reshape(NL, _SMALL_ROWS * D)
    out, at = {}, 0
    for n in _SMALL:
        out[n] = flat[:, at:at + like[n].shape[1]]
        at += like[n].shape[1]
    return out


_STAGES = {"ffn1": ["ffn1_w_gate", "ffn1_w_up", "ffn1_w_down"],
           "mix": ["w_in", "conv_w", "mla_w_uq", "mla_w_ukv", "w_out"],
           "ffn2": ["ffn2_w_gate", "ffn2_w_up", "ffn2_w_down"]}


_FFN_T = ("ffn1_w_gate", "ffn1_w_up", "ffn2_w_gate", "ffn2_w_up")


def stage_weights(l, stage, g, rep):
    if stage != "mix":
        i = stage[3]
        return {"g" + i: g[f"ffn{i}_w_gate"].reshape(D_FF, D), "u" + i: g[f"ffn{i}_w_up"].reshape(D_FF, D),
                "d" + i: g[f"ffn{i}_w_down"].reshape(D_FF, D),
                "ln1_g" if i == "1" else "ln3_g": rep["ln1_g" if i == "1" else "ln3_g"][l][None, :],
                "ln1_b" if i == "1" else "ln3_b": rep["ln1_b" if i == "1" else "ln3_b"][l][None, :]}
    W = {}
    W["w_in"] = g["w_in"].reshape(D, N_INP)
    W["w_out"] = g["w_out"].reshape(D, D)
    W["wq"], W["wk"], W["wv"] = mla_weights(g["mla_w_uq"], g["mla_w_ukv"])
    W["conv_w"] = _unshard_cols(g["conv_w"])
    for k in ("ln2_g", "ln2_b", "conv_b"):
        W[k] = rep[k][l][None, :]
    W["normg"] = rep["ssd_norm_g"][l][None, :]
    W["qg"] = rep["mla_q_norm_g"][l][None, :]
    W["kvg"] = rep["mla_kv_norm_g"][l][None, :]
    W["dtb"] = _lanes(rep["dt_bias"][l], SM_DT)
    W["alog"] = _lanes(rep["a_log"][l], SM_DT)
    W["dskip"] = _lanes(rep["d_skip"][l], SM_DT)
    W["fb"] = _lanes(rep["fox_f_b"][l], SM_F)
    return W


def small_grads(G):
    return {"ln1_g": G["ln1_g"][0], "ln1_b": G["ln1_b"][0], "ln2_g": G["ln2_g"][0], "ln2_b": G["ln2_b"][0],
            "ln3_g": G["ln3_g"][0], "ln3_b": G["ln3_b"][0], "conv_b": G["conv_b"][0], "ssd_norm_g": G["normg"][0],
            "mla_q_norm_g": G["qg"][0], "mla_kv_norm_g": G["kvg"][0], "dt_bias": G["dtb"][0, :SSD_H],
            "a_log": G["alog"][0, :SSD_H], "d_skip": G["dskip"][0, :SSD_H], "fox_f_b": G["fb"][0, SM_F:SM_F + FOX_H]}


def big_grads(G, stage):
    if stage != "mix":
        i = stage[-1]
        return {f"ffn{i}_w_{k}": G[k[0] + i].reshape(N_DEV, HS, D) for k in ("gate", "up", "down")}
    duq, dukv = mla_weight_grads(G["wq"], G["wk"], G["wv"])
    return {"w_in": G["w_in"].reshape(N_DEV, D // N_DEV, N_INP), "w_out": G["w_out"].reshape(N_DEV, D // N_DEV, D),
            "mla_w_uq": duq, "mla_w_ukv": dukv, "conv_w": _shard_cols(G["conv_w"])}


def kernel(x, meta, ffn1_w_gate, ffn1_w_up, ffn1_w_down, ln1_g, ln1_b, w_in, conv_w, conv_b, dt_bias, a_log, d_skip, ssd_norm_g, fox_f_b, mla_q_norm_g, mla_w_uq, mla_kv_norm_g, mla_w_ukv, w_out, ln2_g, ln2_b, ffn2_w_gate, ffn2_w_up, ffn2_w_down, ln3_g, ln3_b, loss_target, m_meta, m_ffn1_w_gate, m_ffn1_w_up, m_ffn1_w_down, m_ln1_g, m_ln1_b, m_w_in, m_conv_w, m_conv_b, m_dt_bias, m_a_log, m_d_skip, m_ssd_norm_g, m_fox_f_b, m_mla_q_norm_g, m_mla_w_uq, m_mla_kv_norm_g, m_mla_w_ukv, m_w_out, m_ln2_g, m_ln2_b, m_ffn2_w_gate, m_ffn2_w_up, m_ffn2_w_down, m_ln3_g, m_ln3_b, v_meta, v_ffn1_w_gate, v_ffn1_w_up, v_ffn1_w_down, v_ln1_g, v_ln1_b, v_w_in, v_conv_w, v_conv_b, v_dt_bias, v_a_log, v_d_skip, v_ssd_norm_g, v_fox_f_b, v_mla_q_norm_g, v_mla_w_uq, v_mla_kv_norm_g, v_mla_w_ukv, v_w_out, v_ln2_g, v_ln2_b, v_ffn2_w_gate, v_ffn2_w_up, v_ffn2_w_down, v_ln3_g, v_ln3_b):
    vals = (meta, ffn1_w_gate, ffn1_w_up, ffn1_w_down, ln1_g, ln1_b, w_in, conv_w, conv_b, dt_bias, a_log, d_skip, ssd_norm_g, fox_f_b, mla_q_norm_g, mla_w_uq, mla_kv_norm_g, mla_w_ukv, w_out, ln2_g, ln2_b, ffn2_w_gate, ffn2_w_up, ffn2_w_down, ln3_g, ln3_b)
    moms = (m_meta, m_ffn1_w_gate, m_ffn1_w_up, m_ffn1_w_down, m_ln1_g, m_ln1_b, m_w_in, m_conv_w, m_conv_b, m_dt_bias, m_a_log, m_d_skip, m_ssd_norm_g, m_fox_f_b, m_mla_q_norm_g, m_mla_w_uq, m_mla_kv_norm_g, m_mla_w_ukv, m_w_out, m_ln2_g, m_ln2_b, m_ffn2_w_gate, m_ffn2_w_up, m_ffn2_w_down, m_ln3_g, m_ln3_b)
    vars_ = (v_meta, v_ffn1_w_gate, v_ffn1_w_up, v_ffn1_w_down, v_ln1_g, v_ln1_b, v_w_in, v_conv_w, v_conv_b, v_dt_bias, v_a_log, v_d_skip, v_ssd_norm_g, v_fox_f_b, v_mla_q_norm_g, v_mla_w_uq, v_mla_kv_norm_g, v_mla_w_ukv, v_w_out, v_ln2_g, v_ln2_b, v_ffn2_w_gate, v_ffn2_w_up, v_ffn2_w_down, v_ln3_g, v_ln3_b)
    P = dict(zip(_NAMES, vals))
    M = dict(zip(_NAMES, moms))
    V = dict(zip(_NAMES, vars_))
    me = 4 * lax.axis_index("x") + 2 * lax.axis_index("y") + lax.axis_index("c")

    me_arr = me.astype(jnp.int32).reshape(1)
    for n in _FFN_T:
        P[n], M[n], V[n] = (jnp.swapaxes(a[n], 1, 2) for a in (P, M, V))
    src = dict(P)
    src["w_in"] = w_in_to_padded(P["w_in"])
    order = [("meta", 0)] + [(n, l) for l in range(NL) for names in _STAGES.values() for n in names]
    nfirst = 1 + len(_STAGES["ffn1"])

    def place(n, l):
        return place_own(P["meta"][None] if n == "meta" else src[n], l, F32 if n in ("meta", "conv_w") else BF16, me_arr)

    hg_first = gather_start([place(n, l) for n, l in order[:nfirst]], "gather_start_first")
    hg_rest = gather_start([place(n, l) for n, l in order[nfirst:]], "gather_start_rest")
    zone_of = {nl_: ((hg_first, i) if i < nfirst else (hg_rest, i - nfirst)) for i, nl_ in enumerate(order)}
    relays = {}

    def ahead(l, stage, after):
        if l < NL and (l, stage) not in relays:
            zs = [zone_of[("meta", 0)]] if stage == "meta" else [zone_of[(n, l)] for n in _STAGES[stage]]
            hg, idxs = zs[0][0], [i for _, i in zs]
            relays[(l, stage)] = (hg, idxs, gather_relay(hg, idxs, f"gather_relay_{l}_{stage}", after))

    def arrived(l, stage, after):
        ahead(l, stage, after)
        hg, idxs, rl = relays[(l, stage)]
        return gather_wait(hg, rl, idxs, f"gather_wait_{l}_{stage}", after)

    meta_full = _unshard_cols(arrived(0, "meta", hg_rest["token"])[0])

    def getw(l, stage, after):
        return stage_weights(l, stage, dict(zip(_STAGES[stage], arrived(l, stage, after))), P)

    sent = {}

    def emit(l, stage, G):
        bg = big_grads(G, stage)
        sent[(l, stage)] = exchange_start("scatter", [bg[n] for n in _STAGES[stage]], f"scatter_start_{l}_{stage}")
        return sent[(l, stage)]["token"]

    loss, gx, gmeta, grads = local_step(x[0], loss_target[0], meta_full, getw, emit, ahead)

    small = jnp.concatenate([pack_small({n: jnp.stack([small_grads(g)[n] for g in grads]) for n in _SMALL}), gmeta,
                             jnp.pad(loss, ((0, 7), (0, D - 1)))], axis=0)
    hs = exchange_start("gather", [place_own(small[None], 0, F32, me_arr)], "small_start")

    out = {}
    after = hs["token"]
    for stage in ("ffn2", "mix", "ffn1"):
        names = _STAGES[stage]
        got = [exchange_wait(sent[(l, stage)], list(range(len(names))), f"scatter_wait_{l}_{stage}", after)
               for l in range(NL - 1, -1, -1)][::-1]
        for i, n in enumerate(names):
            own = [got[l][0][i] for l in range(NL)]
            recv = [got[l][1][i] for l in range(NL)]
            if n == "w_in":
                g = jnp.stack([w_in_from_padded(sum_slots(recv[l], own[l], me_arr)) for l in range(NL)])
                out[n] = (g,) + adamw(P[n], M[n], V[n], g=g)
            else:
                out[n] = adamw(P[n], M[n], V[n], recv=recv, own=own, me_arr=me_arr)
                if n in _FFN_T:
                    out[n] = tuple(jnp.swapaxes(a, 1, 2) for a in out[n])
        after = out[names[-1]][1]
    gsmall = sum_slots(exchange_wait(hs, [0], "small_wait", after)[1][0])
    gm = lax.dynamic_slice(gsmall[NL * _SMALL_ROWS:], (0, me * (D // N_DEV)), (N_META, D // N_DEV))
    out["meta"] = (gm,) + adamw(P["meta"], M["meta"], V["meta"], g=gm)
    gs = gsmall[:NL * _SMALL_ROWS]
    sd, sm_, sv_ = adamw(pack_small(P), pack_small(M), pack_small(V), g=gs)
    ups = [unpack_small(a, P) for a in (gs, sd, sm_, sv_)]
    for n in _SMALL:
        out[n] = tuple(u[n] for u in ups)

    loss_all = gsmall[NL * _SMALL_ROWS + N_META, 0]
    flat = [loss_all, gx[None]]
    for k in range(4):
        flat += [out[n][k] for n in _NAMES]
    return tuple(flat)
```

```python
import functools

import jax
import jax.numpy as jnp
from jax import lax
from jax.experimental import pallas as pl
from jax.experimental.pallas import tpu as pltpu

F32, BF16 = jnp.float32, jnp.bfloat16
HI = lax.Precision.HIGHEST

N_DEV = 8
D = 1024
NL = 2
N_META = 16
BLK = 128
PAD = BLK - N_META
D_FF = 2816
HS = D_FF // N_DEV
SSD_H, SSD_P, SSD_N, SSD_G = 8, 64, 64, 2
SSD_D = SSD_H * SSD_P
CONV_K = 4
CONV_D = SSD_D + 2 * SSD_G * SSD_N
FOX_H, FOX_DH = 4, 64
MLA_H, MLA_QL, MLA_KVL, MLA_NOPE, MLA_ROPE, MLA_V = 4, 256, 128, 64, 32, 64
N_IN = 2476
C_Z, C_XBC, C_FQ, C_FK, C_FV, C_CQ, C_CKV, C_SM, N_INP = 0, 512, 1280, 1536, 1792, 2048, 2304, 2432, 2560
SM_DT, SM_F, SM_KR = 0, 8, 64
ALPHA = (2 * NL) ** 0.25
EPS = 1e-5
NEG = -1e30
LR, B1, B2, AEPS, WD, STEP = 0.001, 0.9, 0.999, 1e-08, 0.01, 10
VMEM_MB = 56


def _cp(*sem):
    return pltpu.CompilerParams(dimension_semantics=sem, vmem_limit_bytes=VMEM_MB << 20)


def _nn(a, b):
    return lax.dot_general(a, b, (((1,), (0,)), ((), ())), preferred_element_type=F32)


def _nt(a, b):
    return lax.dot_general(a, b, (((1,), (1,)), ((), ())), preferred_element_type=F32)


def _tn(a, b):
    return lax.dot_general(a, b, (((0,), (0,)), ((), ())), preferred_element_type=F32)


def _nn_hi(a, b):
    return lax.dot_general(a, b, (((1,), (0,)), ((), ())), precision=HI, preferred_element_type=F32)


def _row_tile(t):
    for d in range(640, 15, -16):
        if t % d == 0:
            return d
    raise ValueError(t)


def _sig(x):
    return 1.0 / (1.0 + jnp.exp(-x))


def _tri(lower=True):
    r = lax.broadcasted_iota(jnp.int32, (BLK, BLK), 0)
    c = lax.broadcasted_iota(jnp.int32, (BLK, BLK), 1)
    return (r >= c) if lower else (r <= c)


def build_h0(meta_full, x):
    s = x.shape[0]
    nb = s // BLK + 1

    def body(m_ref, x_ref, h_ref, hb_ref):
        i = pl.program_id(0)

        @pl.when(i == 0)
        def _():
            h = jnp.concatenate([jnp.zeros((PAD, D), F32), m_ref[...]], axis=0)
            h_ref[...] = h
            hb_ref[...] = h.astype(BF16)

        @pl.when(i > 0)
        def _():
            h_ref[...] = x_ref[...]
            hb_ref[...] = x_ref[...].astype(BF16)

    return pl.pallas_call(
        body, name="build_h0", grid=(nb,),
        in_specs=[pl.BlockSpec((N_META, D), lambda i: (0, 0)),
                  pl.BlockSpec((BLK, D), lambda i: (jnp.maximum(i - 1, 0), 0))],
        out_specs=[pl.BlockSpec((BLK, D), lambda i: (i, 0))] * 2,
        out_shape=[jax.ShapeDtypeStruct((nb * BLK, D), F32), jax.ShapeDtypeStruct((nb * BLK, D), BF16)],
        compiler_params=_cp("arbitrary"),
    )(meta_full, x)


FT = 256


def _layer_norm(r, gamma, beta):
    mu = jnp.mean(r, axis=1, keepdims=True)
    xc = r - mu
    var = jnp.mean(xc * xc, axis=1, keepdims=True)
    return xc * lax.rsqrt(var + EPS) * gamma + beta


def ffn_fwd_seq(x, ln_in, wg, wu, wd, ln_out):
    t = x.shape[0]
    f = wg.shape[0]
    nj, nr = f // FT, t // _row_tile(t)
    rc = t // nr
    plain = ln_in is None
    gi, bi = ln_out if plain else ln_in

    def body(x_hbm, gi_ref, bi_ref, go_ref, bo_ref, wg_ref, wu_ref, wd_ref, u_ref, v_ref, r_hbm, yb_hbm,
             acc, hbs, xbuf, sem_in, sem_out):
        j = pl.program_id(0)

        @pl.when(j == 0)
        def _():
            def fetch(k):
                return pltpu.make_async_copy(x_hbm.at[pl.ds(k * rc, rc)], xbuf.at[k % 2], sem_in.at[k % 2])

            fetch(0).start()
            for k in range(nr):
                if k + 1 < nr:
                    fetch(k + 1).start()
                fetch(k).wait()
                h = xbuf[k % 2]
                if not plain:
                    h = _layer_norm(h, gi_ref[...], bi_ref[...])
                acc[k * rc:(k + 1) * rc, :] = ALPHA * h
                hbs[k * rc:(k + 1) * rc, :] = h.astype(BF16)

        for k in range(nr):
            sl = slice(k * rc, (k + 1) * rc)
            h = hbs[sl, :]
            u = _nt(h, wg_ref[...])
            v = _nt(h, wu_ref[...])
            u_ref[sl, :] = u.astype(BF16)
            v_ref[sl, :] = v.astype(BF16)
            acc[sl, :] += _nn((0.5 * u * _sig(u) * v).astype(BF16), wd_ref[...])

        @pl.when(j == nj - 1)
        def _():
            r_cp = pltpu.make_async_copy(acc, r_hbm, sem_out.at[0])
            r_cp.start()
            for k in range(nr):
                sl = slice(k * rc, (k + 1) * rc)
                hbs[sl, :] = _layer_norm(acc[sl, :], go_ref[...], bo_ref[...]).astype(BF16)
            y_cp = pltpu.make_async_copy(hbs, yb_hbm, sem_out.at[1])
            y_cp.start()
            r_cp.wait()
            y_cp.wait()

    vec = pl.BlockSpec((1, D), lambda j: (0, 0))
    wsp = pl.BlockSpec((FT, D), lambda j: (j, 0))
    act = pl.BlockSpec((None, t, FT), lambda j: (j, 0, 0))
    return pl.pallas_call(
        body, name="ffn_fwd_seq", grid=(nj,),
        in_specs=[_ANY, vec, vec, vec, vec, wsp, wsp, wsp],
        out_specs=[act, act, _ANY, _ANY],
        out_shape=[jax.ShapeDtypeStruct((nj, t, FT), BF16), jax.ShapeDtypeStruct((nj, t, FT), BF16),
                   jax.ShapeDtypeStruct((t, D), F32), jax.ShapeDtypeStruct((t, D), BF16)],
        scratch_shapes=[pltpu.VMEM((t, D), F32), pltpu.VMEM((t, D), BF16), pltpu.VMEM((2, rc, D), F32),
                        pltpu.SemaphoreType.DMA((2,)), pltpu.SemaphoreType.DMA((2,))],
        compiler_params=_cp("arbitrary"),
    )(x, gi, bi, ln_out[0], ln_out[1], wg, wu, wd)


def ffn_bwd_seq(parts, r, gamma, hb, u, v, wg, wu, wd, after=None):
    nj, t, _ = u.shape
    f = nj * FT
    nr = t // _row_tile(t)
    rc = t // nr
    nc = t // BLK
    scales = [s for _, s in parts]
    npart = len(parts)
    extra = [] if after is None else [after]

    def body(*refs):
        refs = refs[len(extra):]
        p_hbm, refs = refs[:npart], refs[npart:]
        (r_hbm, g_ref, hb_hbm, u_ref, v_ref, wg_ref, wu_ref, wd_ref, dh_hbm, dwg_ref, dwu_ref, dwd_ref, dg_ref, db_ref,
         dfs, hbt, dft, dhacc, dus, dvs, acs, pbuf, rbuf, hbuf, sems, sem_out) = refs
        j = pl.program_id(0)

        @pl.when(j == 0)
        def _():
            def fetch(c):
                rows = pl.ds(c * BLK, BLK)
                cps = [pltpu.make_async_copy(p_hbm[p].at[rows], pbuf.at[c % 2, p], sems.at[c % 2, p]) for p in range(npart)]
                cps.append(pltpu.make_async_copy(r_hbm.at[rows], rbuf.at[c % 2], sems.at[c % 2, npart]))
                cps.append(pltpu.make_async_copy(hb_hbm.at[rows], hbuf.at[c % 2], sems.at[c % 2, npart + 1]))
                return cps

            for cp in fetch(0):
                cp.start()
            dg = jnp.zeros((1, D), F32)
            db = jnp.zeros((1, D), F32)
            for c in range(nc):
                if c + 1 < nc:
                    for cp in fetch(c + 1):
                        cp.start()
                for cp in fetch(c):
                    cp.wait()
                sl = slice(c * BLK, (c + 1) * BLK)
                dy = scales[0] * pbuf[c % 2, 0]
                for p in range(1, npart):
                    dy += scales[p] * pbuf[c % 2, p]
                rr = rbuf[c % 2]
                xc = rr - jnp.mean(rr, axis=1, keepdims=True)
                rstd = lax.rsqrt(jnp.mean(xc * xc, axis=1, keepdims=True) + EPS)
                xh = xc * rstd
                dxh = dy * g_ref[...]
                dr = rstd * (dxh - jnp.mean(dxh, axis=1, keepdims=True) - xh * jnp.mean(dxh * xh, axis=1, keepdims=True))
                dg += jnp.sum(dy * xh, axis=0, keepdims=True)
                db += jnp.sum(dy, axis=0, keepdims=True)
                dhacc[sl, :] = ALPHA * dr
                dfc = (0.5 * dr).astype(BF16)
                dfs[sl, :] = dfc
                dft[:, sl] = dfc.T
                hbt[:, sl] = hbuf[c % 2].T
            dg_ref[...] = dg
            db_ref[...] = db

        for k in range(nr):
            sl = slice(k * rc, (k + 1) * rc)
            da = _nt(dfs[sl, :], wd_ref[...])
            uu = u_ref[sl, :].astype(F32)
            vv = v_ref[sl, :].astype(F32)
            sg = _sig(uu)
            du = (da * vv * (sg * (1.0 + uu * (1.0 - sg)))).astype(BF16)
            dv = (da * uu * sg).astype(BF16)
            dus[sl, :] = du
            dvs[sl, :] = dv
            acs[sl, :] = (uu * sg * vv).astype(BF16)
            dhacc[sl, :] += _nn(du, wg_ref[...]) + _nn(dv, wu_ref[...])
        @pl.when(j == nj - 1)
        def _():
            pltpu.make_async_copy(dhacc, dh_hbm, sem_out.at[0]).start()

        dwg_ref[...] = _nn(hbt[...], dus[...]).astype(BF16).T
        dwu_ref[...] = _nn(hbt[...], dvs[...]).astype(BF16).T
        dwd_ref[...] = _nn(dft[...], acs[...]).astype(BF16).T

        @pl.when(j == nj - 1)
        def _():
            pltpu.make_async_copy(dhacc, dh_hbm, sem_out.at[0]).wait()

    vec = pl.BlockSpec((1, D), lambda j: (0, 0))
    wsp = pl.BlockSpec((FT, D), lambda j: (j, 0))
    act = pl.BlockSpec((None, t, FT), lambda j: (j, 0, 0))
    return pl.pallas_call(
        body, name="ffn_bwd_seq", grid=(nj,),
        in_specs=[_ANY] * (len(extra) + npart + 1) + [vec, _ANY, act, act, wsp, wsp, wsp],
        out_specs=[_ANY, wsp, wsp, wsp, vec, vec],
        out_shape=[jax.ShapeDtypeStruct((t, D), F32)] + [jax.ShapeDtypeStruct((f, D), BF16)] * 3
        + [jax.ShapeDtypeStruct((1, D), F32)] * 2,
        scratch_shapes=[pltpu.VMEM((t, D), BF16), pltpu.VMEM((D, t), BF16), pltpu.VMEM((D, t), BF16),
                        pltpu.VMEM((t, D), F32), pltpu.VMEM((t, FT), BF16), pltpu.VMEM((t, FT), BF16),
                        pltpu.VMEM((t, FT), BF16), pltpu.VMEM((2, npart, BLK, D), F32), pltpu.VMEM((2, BLK, D), F32),
                        pltpu.VMEM((2, BLK, D), BF16), pltpu.SemaphoreType.DMA((2, npart + 2)),
                        pltpu.SemaphoreType.DMA((1,))],
        compiler_params=_cp("arbitrary"),
    )(*extra, *[p for p, _ in parts], r, gamma, hb, u, v, wg, wu, wd)


def mm_res_ln(a, b, x, ln_in, ln_out):
    t, k = a.shape
    tm = _row_tile(t)

    def body(a_ref, b_ref, x_ref, gi_ref, bi_ref, go_ref, bo_ref, r_ref, yb_ref):
        r = ALPHA * _layer_norm(x_ref[...], gi_ref[...], bi_ref[...]) + _nn(a_ref[...], b_ref[...])
        r_ref[...] = r
        yb_ref[...] = _layer_norm(r, go_ref[...], bo_ref[...]).astype(BF16)

    row = pl.BlockSpec((tm, D), lambda i: (i, 0))
    vec = pl.BlockSpec((1, D), lambda i: (0, 0))
    return pl.pallas_call(
        body, name="mm_res_ln", grid=(t // tm,),
        in_specs=[pl.BlockSpec((tm, k), lambda i: (i, 0)), pl.BlockSpec((k, D), lambda i: (0, 0)), row, vec, vec, vec, vec],
        out_specs=[row, row],
        out_shape=[jax.ShapeDtypeStruct((t, D), F32), jax.ShapeDtypeStruct((t, D), BF16)],
        compiler_params=_cp("arbitrary"),
    )(a, b, x, ln_in[0], ln_in[1], ln_out[0], ln_out[1])


def mm_nn(a, b):
    t, k = a.shape
    n = tn = b.shape[1]
    tm = _row_tile(t)

    def body(a_ref, b_ref, o_ref):
        o_ref[...] = _nn(a_ref[...], b_ref[...])

    return pl.pallas_call(
        body, name="mm_nn", grid=(t // tm, n // tn),
        in_specs=[pl.BlockSpec((tm, k), lambda i, j: (i, 0)), pl.BlockSpec((k, tn), lambda i, j: (0, j))],
        out_specs=pl.BlockSpec((tm, tn), lambda i, j: (i, j)),
        out_shape=jax.ShapeDtypeStruct((t, n), F32),
        compiler_params=_cp("arbitrary", "arbitrary"),
    )(a, b)


def oproj_bwd(dy, r, gamma, mixcat, w_out, after=None):
    t = r.shape[0]
    tm = _row_tile(t)
    nt = t // tm
    extra = [] if after is None else [after]

    def body(*refs):
        dy_ref, r_ref, g_ref, m_ref, w_ref, dr_ref, dm_ref, dw_ref, dg_ref, db_ref, acc = refs[len(extra):]
        i = pl.program_id(0)
        dy = dy_ref[...]
        rr = r_ref[...]
        xc = rr - jnp.mean(rr, axis=1, keepdims=True)
        rstd = lax.rsqrt(jnp.mean(xc * xc, axis=1, keepdims=True) + EPS)
        xh = xc * rstd
        dxh = dy * g_ref[...]
        dr = rstd * (dxh - jnp.mean(dxh, axis=1, keepdims=True) - xh * jnp.mean(dxh * xh, axis=1, keepdims=True))
        dr_ref[...] = dr
        drb = dr.astype(BF16)
        dm_ref[...] = _nt(drb, w_ref[...])
        dw = _tn(m_ref[...], drb)
        dg = jnp.sum(dy * xh, axis=0, keepdims=True)
        db = jnp.sum(dy, axis=0, keepdims=True)

        @pl.when(i == 0)
        def _():
            acc[...] = dw
            dg_ref[...] = dg
            db_ref[...] = db

        @pl.when(i > 0)
        def _():
            acc[...] += dw
            dg_ref[...] += dg
            db_ref[...] += db

        @pl.when(i == nt - 1)
        def _():
            dw_ref[...] = acc[...].astype(BF16)

    row = pl.BlockSpec((tm, D), lambda i: (i, 0))
    vec = pl.BlockSpec((1, D), lambda i: (0, 0))
    mat = pl.BlockSpec((D, D), lambda i: (0, 0))
    return pl.pallas_call(
        body, name="oproj_bwd", grid=(nt,),
        in_specs=[_ANY] * len(extra) + [row, row, vec, row, mat],
        out_specs=[row, row, mat, vec, vec],
        out_shape=[jax.ShapeDtypeStruct((t, D), F32), jax.ShapeDtypeStruct((t, D), F32), jax.ShapeDtypeStruct((D, D), BF16),
                   jax.ShapeDtypeStruct((1, D), F32), jax.ShapeDtypeStruct((1, D), F32)],
        scratch_shapes=[pltpu.VMEM((D, D), F32)],
        compiler_params=_cp("arbitrary"),
    )(*extra, dy, r, gamma, mixcat, w_out)


def proj_bwd(dproj, hb, w_in):
    t, n = dproj.shape
    tm = _row_tile(t)
    nt = t // tm

    def body(dp_ref, h_ref, w_ref, dh_ref, dw_ref, acc):
        i = pl.program_id(0)
        dp = dp_ref[...]
        dh_ref[...] = _nt(dp, w_ref[...])
        dw = _tn(h_ref[...], dp)

        @pl.when(i == 0)
        def _():
            acc[...] = dw

        @pl.when(i > 0)
        def _():
            acc[...] += dw

        @pl.when(i == nt - 1)
        def _():
            dw_ref[...] = acc[...].astype(BF16)

    mat = pl.BlockSpec((D, n), lambda i: (0, 0))
    return pl.pallas_call(
        body, name="proj_bwd", grid=(nt,),
        in_specs=[pl.BlockSpec((tm, n), lambda i: (i, 0)), pl.BlockSpec((tm, D), lambda i: (i, 0)), mat],
        out_specs=[pl.BlockSpec((tm, D), lambda i: (i, 0)), mat],
        out_shape=[jax.ShapeDtypeStruct((t, D), F32), jax.ShapeDtypeStruct((D, n), BF16)],
        scratch_shapes=[pltpu.VMEM((D, n), F32)],
        compiler_params=_cp("arbitrary"),
    )(dproj, hb, w_in)


def loss_head(r, ln, target):
    t = r.shape[0]
    nb = t // BLK

    def body(r_ref, g_ref, b_ref, t_ref, dy_ref, l_ref):
        i = pl.program_id(0)

        @pl.when(i == 0)
        def _():
            dy_ref[...] = jnp.zeros_like(dy_ref)
            l_ref[...] = jnp.zeros_like(l_ref)

        @pl.when(i > 0)
        def _():
            err = _layer_norm(r_ref[...], g_ref[...], b_ref[...]) - t_ref[...]
            dy_ref[...] = err * (1.0 / D)
            l_ref[...] += (0.5 / D) * jnp.sum(err * err, keepdims=True)

    vec = pl.BlockSpec((1, D), lambda i: (0, 0))
    return pl.pallas_call(
        body, name="loss_head", grid=(nb,),
        in_specs=[pl.BlockSpec((BLK, D), lambda i: (i, 0)), vec, vec,
                  pl.BlockSpec((BLK, D), lambda i: (jnp.maximum(i - 1, 0), 0))],
        out_specs=[pl.BlockSpec((BLK, D), lambda i: (i, 0)), pl.BlockSpec((1, 1), lambda i: (0, 0))],
        out_shape=[jax.ShapeDtypeStruct((t, D), F32), jax.ShapeDtypeStruct((1, 1), F32)],
        compiler_params=_cp("arbitrary"),
    )(r, ln[0], ln[1], target)


def split_dh0(dh0, after=None):
    t = dh0.shape[0]
    nb = t // BLK
    extra = [] if after is None else [after]

    def body(*refs):
        a_ref, gx_ref, gm_ref = refs[len(extra):]
        i = pl.program_id(0)
        tot = a_ref[...]

        @pl.when(i == 0)
        def _():
            gm_ref[...] = tot[PAD:, :]

        @pl.when(i > 0)
        def _():
            gx_ref[...] = tot

    blk = pl.BlockSpec((BLK, D), lambda i: (i, 0))
    return pl.pallas_call(
        body, name="split_dh0", grid=(nb,),
        in_specs=[_ANY] * len(extra) + [blk],
        out_specs=[pl.BlockSpec((BLK, D), lambda i: (jnp.maximum(i - 1, 0), 0)),
                   pl.BlockSpec((N_META, D), lambda i: (0, 0))],
        out_shape=[jax.ShapeDtypeStruct((t - BLK, D), F32), jax.ShapeDtypeStruct((N_META, D), F32)],
        compiler_params=_cp("arbitrary"),
    )(*extra, dh0)


def _valid_rows(nrows, first_row):
    return (first_row + lax.broadcasted_iota(jnp.int32, (nrows, 1), 0)) >= PAD


def conv_fwd(proj, conv_w, conv_b):
    t = proj.shape[0]
    c0 = C_XBC // BLK

    def body(x_ref, w_ref, b_ref, o_ref):
        ok = _valid_rows(t, 0)
        x = jnp.where(ok, x_ref[...], 0.0)
        w = w_ref[...]
        acc = b_ref[...] + w[CONV_K - 1:CONV_K, :] * x
        for s in range(1, CONV_K):
            acc += w[CONV_K - 1 - s:CONV_K - s, :] * pltpu.roll(x, s, 0)
        o_ref[...] = jnp.where(ok, acc * _sig(acc), 0.0)

    return pl.pallas_call(
        body, name="conv_fwd", grid=(CONV_D // BLK,),
        in_specs=[pl.BlockSpec((t, BLK), lambda j: (0, c0 + j)),
                  pl.BlockSpec((CONV_K, BLK), lambda j: (0, j)), pl.BlockSpec((1, BLK), lambda j: (0, j))],
        out_specs=pl.BlockSpec((t, BLK), lambda j: (0, j)),
        out_shape=jax.ShapeDtypeStruct((t, CONV_D), F32),
        compiler_params=_cp("arbitrary"),
    )(proj, conv_w, conv_b)


def conv_bwd(dxa, proj, conv_w, conv_b):
    t = proj.shape[0]
    c0 = C_XBC // BLK

    def body(d_ref, x_ref, w_ref, b_ref, dx_ref, dw_ref, db_ref):
        ok = _valid_rows(t, 0)
        x = jnp.where(ok, x_ref[...], 0.0)
        w = w_ref[...]
        xs = [x] + [pltpu.roll(x, s, 0) for s in range(1, CONV_K)]
        acc = b_ref[...] + w[CONV_K - 1:CONV_K, :] * x
        for s in range(1, CONV_K):
            acc += w[CONV_K - 1 - s:CONV_K - s, :] * xs[s]
        sg = _sig(acc)
        dxc = jnp.where(ok, d_ref[...] * (sg * (1.0 + acc * (1.0 - sg))), 0.0)
        db_ref[...] = jnp.sum(dxc, axis=0, keepdims=True)
        dw_ref[...] = jnp.concatenate(
            [jnp.sum(dxc * xs[CONV_K - 1 - k], axis=0, keepdims=True) for k in range(CONV_K)], axis=0)
        dx = w[CONV_K - 1:CONV_K, :] * dxc
        for s in range(1, CONV_K):
            dx += w[CONV_K - 1 - s:CONV_K - s, :] * pltpu.roll(dxc, t - s, 0)
        dx_ref[...] = jnp.where(ok, dx, 0.0)

    col = pl.BlockSpec((t, BLK), lambda j: (0, j))
    return pl.pallas_call(
        body, name="conv_bwd", grid=(CONV_D // BLK,),
        in_specs=[col, pl.BlockSpec((t, BLK), lambda j: (0, c0 + j)),
                  pl.BlockSpec((CONV_K, BLK), lambda j: (0, j)), pl.BlockSpec((1, BLK), lambda j: (0, j))],
        out_specs=[col, pl.BlockSpec((CONV_K, BLK), lambda j: (0, j)), pl.BlockSpec((1, BLK), lambda j: (0, j))],
        out_shape=[jax.ShapeDtypeStruct((t, CONV_D), F32), jax.ShapeDtypeStruct((CONV_K, CONV_D), F32),
                   jax.ShapeDtypeStruct((1, CONV_D), F32)],
        compiler_params=_cp("arbitrary"),
    )(dxa, proj, conv_w, conv_b)


def _softplus(x):
    return jnp.maximum(x, 0.0) + jnp.log(1.0 + jnp.exp(-jnp.abs(x)))


GW = SSD_D // SSD_G
HPG = SSD_H // SSD_G


def _head_expand():
    r = lax.broadcasted_iota(jnp.int32, (BLK, SSD_D), 0)
    c = lax.broadcasted_iota(jnp.int32, (BLK, SSD_D), 1)
    rt = lax.broadcasted_iota(jnp.int32, (SSD_D, BLK), 0)
    ct = lax.broadcasted_iota(jnp.int32, (SSD_D, BLK), 1)
    return (c // SSD_P == r).astype(F32), (rt // SSD_P == ct).astype(F32)


def _ssd_chunk(xa, sm, dtb, alog, dskip, ok, sp):
    e, et = _head_expand()
    dt = jnp.where(ok, _softplus(sm + dtb), 0.0)
    amat = -jnp.exp(alog)
    tri = _tri()
    ac = _nn_hi(tri.astype(F32), dt * amat)
    act = ac.T
    ace, dte, dse = _nn_hi(ac, e), _nn_hi(dt, e), _nn_hi(dskip, e)
    laste = ace[BLK - 1:BLK, :]
    ee, dece, gle = jnp.exp(ace), jnp.exp(laste - ace), jnp.exp(laste)
    xs = xa[:, :SSD_D]
    xdt = xs * dte
    decx = dece * xdt
    xdtb = xdt.astype(BF16)
    d = dict(e=e, et=et, dt=dt, amat=amat, tri=tri, ac=ac, act=act, dte=dte, dse=dse, ee=ee, dece=dece, gle=gle, xs=xs,
             xdt=xdt, xdtb=xdtb, decx=decx, bg=[], cg=[], cb=[], yo=[], seg=[], m=[], new_s=[])
    ys = []
    for g in range(SSD_G):
        cols = slice(GW * g, GW * (g + 1))
        bg = xa[:, SSD_D + SSD_N * g:SSD_D + SSD_N * (g + 1)].astype(BF16)
        cg = xa[:, SSD_D + SSD_G * SSD_N + SSD_N * g:SSD_D + SSD_G * SSD_N + SSD_N * (g + 1)].astype(BF16)
        spg = sp[:, cols]
        sloc = _tn(bg, decx[:, cols].astype(BF16))
        yo = _nn(cg, spg.astype(BF16)) * ee[:, cols]
        cb = _nt(cg, bg)
        d["new_s"].append(gle[:, cols] * spg + sloc)
        yds = []
        for h in range(HPG * g, HPG * (g + 1)):
            seg = jnp.where(tri, jnp.exp(jnp.minimum(ac[:, h:h + 1] - act[h:h + 1, :], 0.0)), 0.0)
            m = cb * seg
            yds.append(_nn(m.astype(BF16), xdtb[:, SSD_P * h:SSD_P * (h + 1)]))
            d["seg"].append(seg)
            d["m"].append(m)
        ys.append(jnp.concatenate(yds, axis=1) + yo)
        for k, val in (("bg", bg), ("cg", cg), ("cb", cb), ("yo", yo)):
            d[k].append(val)
    d["y"] = jnp.concatenate(ys, axis=1) + dse * xs
    return d


def ssd_fwd(xa, proj, dtb, alog, dskip, normg):
    t = xa.shape[0]
    nb = t // BLK
    gw = SSD_D // SSD_G

    def body(xa_ref, z_ref, sm_ref, dtb_ref, al_ref, ds_ref, ng_ref, y_ref, sp_ref, st):
        c = pl.program_id(0)

        @pl.when(c == 0)
        def _():
            st[...] = jnp.zeros_like(st)

        ok = _valid_rows(BLK, c * BLK)
        sp = st[...]
        sp_ref[...] = sp
        d = _ssd_chunk(xa_ref[...], sm_ref[...], dtb_ref[...], al_ref[...], ds_ref[...], ok, sp)
        st[...] = jnp.concatenate(d["new_s"], axis=1)
        y = d["y"]
        z = z_ref[...]
        yg = y * (z * _sig(z))
        outs = []
        for g in range(SSD_G):
            v = yg[:, gw * g:gw * (g + 1)]
            outs.append(v * lax.rsqrt(jnp.mean(v * v, axis=1, keepdims=True) + EPS))
        y_ref[...] = (jnp.concatenate(outs, axis=1) * ng_ref[...]).astype(BF16)

    vec = pl.BlockSpec((1, BLK), lambda c: (0, 0))
    return pl.pallas_call(
        body, name="ssd_fwd", grid=(nb,),
        in_specs=[pl.BlockSpec((BLK, CONV_D), lambda c: (c, 0)),
                  pl.BlockSpec((BLK, SSD_D), lambda c: (c, C_Z // SSD_D)),
                  pl.BlockSpec((BLK, BLK), lambda c: (c, C_SM // BLK)),
                  vec, vec, vec, pl.BlockSpec((1, SSD_D), lambda c: (0, 0))],
        out_specs=[pl.BlockSpec((BLK, SSD_D), lambda c: (c, 0)),
                   pl.BlockSpec((None, SSD_N, SSD_D), lambda c: (c, 0, 0))],
        out_shape=[jax.ShapeDtypeStruct((t, SSD_D), BF16), jax.ShapeDtypeStruct((nb, SSD_N, SSD_D), F32)],
        scratch_shapes=[pltpu.VMEM((SSD_N, SSD_D), F32)],
        compiler_params=_cp("arbitrary"),
    )(xa, proj, proj, dtb, alog, dskip, normg)


def _lane_put(col, lane):
    li = lax.broadcasted_iota(jnp.int32, (col.shape[0], BLK), 1)
    return jnp.where(li == lane, col, 0.0)


def ssd_bwd(dmix, xa, proj, sprev, dtb, alog, dskip, normg):
    t = xa.shape[0]
    nb = t // BLK
    gw = SSD_D // SSD_G
    rev = lambda c: nb - 1 - c

    def body(dy_ref, xa_ref, z_ref, sm_ref, sp_ref, dtb_ref, al_ref, ds_ref, ng_ref,
             dxa_ref, dz_ref, dsm_ref, dng_ref, dds_ref, dal_ref, ddtb_ref, dst):
        c = pl.program_id(0)

        @pl.when(c == 0)
        def _():
            dst[...] = jnp.zeros_like(dst)
            dng_ref[...] = jnp.zeros_like(dng_ref)
            dds_ref[...] = jnp.zeros_like(dds_ref)
            dal_ref[...] = jnp.zeros_like(dal_ref)
            ddtb_ref[...] = jnp.zeros_like(ddtb_ref)

        ok = _valid_rows(BLK, rev(c) * BLK)
        sm = sm_ref[...]
        sp = sp_ref[...]
        d = _ssd_chunk(xa_ref[...], sm, dtb_ref[...], al_ref[...], ds_ref[...], ok, sp)
        dt, amat, ac, act, tri, et, xs, xdt = (d[k] for k in ("dt", "amat", "ac", "act", "tri", "et", "xs", "xdt"))
        rowi = lax.broadcasted_iota(jnp.int32, (BLK, 1), 0)
        y = d["y"]
        z = z_ref[...]
        sgz = _sig(z)
        siluz = z * sgz
        yg = y * siluz
        dout = dy_ref[...]
        ng = ng_ref[...]
        dygs, xhs = [], []
        for g in range(SSD_G):
            v = yg[:, gw * g:gw * (g + 1)]
            rr = lax.rsqrt(jnp.mean(v * v, axis=1, keepdims=True) + EPS)
            xh = v * rr
            dxh = dout[:, gw * g:gw * (g + 1)] * ng[:, gw * g:gw * (g + 1)]
            dygs.append(rr * (dxh - xh * jnp.mean(dxh * xh, axis=1, keepdims=True)))
            xhs.append(xh)
        dyg = jnp.concatenate(dygs, axis=1)
        dng_ref[...] += jnp.sum(dout * jnp.concatenate(xhs, axis=1), axis=0, keepdims=True)
        dy = dyg * siluz
        dz_ref[...] = dyg * y * (sgz * (1.0 + z * (1.0 - sgz)))

        triu = _tri(lower=False)
        dyb = dy.astype(BF16)
        dsn = dst[...]
        dds_ref[...] += _nn_hi(jnp.sum(dy * xs, axis=0, keepdims=True), et)
        dac_all = _nn_hi(dy * jnp.concatenate(d["yo"], axis=1), et)
        dyo = (dy * d["ee"]).astype(BF16)
        gl = jnp.exp(ac[BLK - 1:BLK, :])
        dlast = _nn_hi(jnp.sum(dsn * sp, axis=0, keepdims=True), et) * gl
        bds, db_g, dc_g, dxdt_i, new_dst = [], [], [], [], []
        for g in range(SSD_G):
            cols = slice(GW * g, GW * (g + 1))
            bg, cg = d["bg"][g], d["cg"][g]
            dsng = dsn[:, cols].astype(BF16)
            dc = _nt(dyo[:, cols], sp[:, cols].astype(BF16))
            new_dst.append(_tn(cg, dyo[:, cols]) + d["gle"][:, cols] * dsn[:, cols])
            bds.append(_nn(bg, dsng))
            db = _nt(d["decx"][:, cols].astype(BF16), dsng)
            cbt = _nt(bg, cg)
            dcb = jnp.zeros((BLK, BLK), F32)
            for h in range(HPG * g, HPG * (g + 1)):
                hc = slice(SSD_P * h, SSD_P * (h + 1))
                dm = _nt(dyb[:, hc], d["xdtb"][:, hc])
                dcb += dm * d["seg"][h]
                w = dm * d["m"][h]
                dac_all += _lane_put(jnp.sum(w, axis=1, keepdims=True) - jnp.sum(w.T, axis=1, keepdims=True), h)
                segt = jnp.where(triu, jnp.exp(jnp.minimum(act[h:h + 1, :] - ac[:, h:h + 1], 0.0)), 0.0)
                dxdt_i.append(_nn((cbt * segt).astype(BF16), dyb[:, hc]))
            dcbb = dcb.astype(BF16)
            dc_g.append(dc + _nn(dcbb, bg))
            db_g.append(db + _tn(dcbb, cg))
        dst[...] = jnp.concatenate(new_dst, axis=1)
        bds = jnp.concatenate(bds, axis=1)
        tdec = jnp.exp(ac[BLK - 1:BLK, :] - ac) * _nn_hi(xdt * bds, et)
        dlast += jnp.sum(tdec, axis=0, keepdims=True)
        dac_all += jnp.where(rowi == BLK - 1, dlast, 0.0) - tdec
        dxdt = d["dece"] * bds + jnp.concatenate(dxdt_i, axis=1)
        da = _nn_hi(triu.astype(F32), dac_all)
        ddt = _nn_hi(dxdt * xs, et) + da * amat
        dal_ref[...] += jnp.sum(da * dt, axis=0, keepdims=True) * amat
        ddtr = jnp.where(ok, ddt * _sig(sm + dtb_ref[...]), 0.0)
        ddtb_ref[...] += jnp.sum(ddtr, axis=0, keepdims=True)
        dsm_ref[...] = ddtr
        dxs = d["dse"] * dy + dxdt * d["dte"]
        dxa_ref[...] = jnp.where(ok, jnp.concatenate([dxs] + db_g + dc_g, axis=1), 0.0)

    vec = pl.BlockSpec((1, BLK), lambda c: (0, 0))
    nvec = pl.BlockSpec((1, SSD_D), lambda c: (0, 0))
    return pl.pallas_call(
        body, name="ssd_bwd", grid=(nb,),
        in_specs=[pl.BlockSpec((BLK, SSD_D), lambda c: (rev(c), 0)),
                  pl.BlockSpec((BLK, CONV_D), lambda c: (rev(c), 0)),
                  pl.BlockSpec((BLK, SSD_D), lambda c: (rev(c), C_Z // SSD_D)),
                  pl.BlockSpec((BLK, BLK), lambda c: (rev(c), C_SM // BLK)),
                  pl.BlockSpec((None, SSD_N, SSD_D), lambda c: (rev(c), 0, 0)),
                  vec, vec, vec, nvec],
        out_specs=[pl.BlockSpec((BLK, CONV_D), lambda c: (rev(c), 0)),
                   pl.BlockSpec((BLK, SSD_D), lambda c: (rev(c), 0)),
                   pl.BlockSpec((BLK, BLK), lambda c: (rev(c), 0)),
                   nvec, vec, vec, vec],
        out_shape=[jax.ShapeDtypeStruct((t, CONV_D), F32), jax.ShapeDtypeStruct((t, SSD_D), F32),
                   jax.ShapeDtypeStruct((t, BLK), F32), jax.ShapeDtypeStruct((1, SSD_D), F32),
                   jax.ShapeDtypeStruct((1, BLK), F32), jax.ShapeDtypeStruct((1, BLK), F32),
                   jax.ShapeDtypeStruct((1, BLK), F32)],
        scratch_shapes=[pltpu.VMEM((SSD_N, SSD_D), F32)],
        compiler_params=_cp("arbitrary"),
    )(dmix, xa, proj, proj, sprev, dtb, alog, dskip, normg)


def _segments(nb, fine):
    if fine:
        cuts = list(range(0, nb, 2)) + [nb]
    else:
        cuts = sorted({0, nb} | {max(1, round(nb * f)) for f in (0.3, 0.53, 0.77)})
    return list(zip(cuts[:-1], cuts[1:]))


def attn_fwd(q, k, v, qcol, kcol, vcol, nh, dq, dv, scale, c_col=None, c_row=None, lane0=0):
    t = q.shape[0]
    tq = BLK
    use_bias = c_col is not None

    def body(*refs):
        if use_bias:
            q_ref, k_ref, v_ref, cc_ref, cr_ref, o_ref, l_ref = refs
        else:
            q_ref, k_ref, v_ref, o_ref, l_ref = refs
        i = pl.program_id(0)
        rowg = i * tq + lax.broadcasted_iota(jnp.int32, (tq, 1), 0)

        def tile(tk):
            col = lax.broadcasted_iota(jnp.int32, (1, tk), 1)
            mask = (col <= rowg) & (col >= PAD)
            outs = []
            lse = jnp.zeros((tq, BLK), F32)
            for h in range(nh):
                s = _nt(q_ref[:, dq * h:dq * (h + 1)].astype(BF16), k_ref[0:tk, dq * h:dq * (h + 1)].astype(BF16)) * scale
                if use_bias:
                    s = s + (cc_ref[:, lane0 + h:lane0 + h + 1] - cr_ref[h:h + 1, 0:tk])
                s = jnp.where(mask, s, NEG)
                m = jnp.max(s, axis=1, keepdims=True)
                p = jnp.exp(s - m)
                l = jnp.sum(p, axis=1, keepdims=True)
                outs.append(_nn(p.astype(BF16), v_ref[0:tk, dv * h:dv * (h + 1)].astype(BF16)) / l)
                lse += _lane_put(m + jnp.log(l), h)
            o_ref[...] = jnp.concatenate(outs, axis=1).astype(BF16)
            l_ref[...] = lse.T[0:8, :]

        for t0, t1 in _segments(t // tq, True):
            pl.when((i >= t0) & (i < t1))(functools.partial(tile, t1 * BLK))

    in_specs = [pl.BlockSpec((tq, nh * dq), lambda i: (i, qcol)),
                pl.BlockSpec((t, nh * dq), lambda i: (0, kcol)),
                pl.BlockSpec((t, nh * dv), lambda i: (0, vcol))]
    args = [q, k, v]
    if use_bias:
        in_specs += [pl.BlockSpec((tq, BLK), lambda i: (i, 0)), pl.BlockSpec((8, t), lambda i: (0, 0))]
        args += [c_col, c_row]
    return pl.pallas_call(
        body, name="attn_fwd", grid=(t // tq,),
        in_specs=in_specs,
        out_specs=[pl.BlockSpec((tq, nh * dv), lambda i: (i, 0)), pl.BlockSpec((8, tq), lambda i: (0, i))],
        out_shape=[jax.ShapeDtypeStruct((t, nh * dv), BF16), jax.ShapeDtypeStruct((8, t), F32)],
        compiler_params=_cp("arbitrary"),
    )(*args)


def attn_bwd(q, k, v, do, lse_row, o, qcol, kcol, vcol, docol, ocol, nh, dq, dv, scale, c_col=None, c_row=None, lane0=0):
    t = q.shape[0]
    tq = BLK
    use_bias = c_col is not None
    nq = t // tq

    def body(*refs):
        if use_bias:
            (q_ref, k_ref, v_ref, do_ref, l_ref, o_ref, cc_ref, cr_ref, dq_ref, dk_ref, dv_ref, dcq_ref, dck_ref,
             kt, ckb, dacc) = refs
        else:
            q_ref, k_ref, v_ref, do_ref, l_ref, o_ref, dq_ref, dk_ref, dv_ref, kt = refs
        i = pl.program_id(0)

        @pl.when(i == 0)
        def _():
            kt[...] = k_ref[...].astype(BF16).T
            dk_ref[...] = jnp.zeros_like(dk_ref)
            dv_ref[...] = jnp.zeros_like(dv_ref)
            if use_bias:
                dacc[...] = jnp.zeros_like(dacc)
                for h in range(nh):
                    ckb[h] = jnp.broadcast_to(cc_ref[:, lane0 + h:lane0 + h + 1], (t, BLK))

        qry = i * tq + lax.broadcasted_iota(jnp.int32, (1, tq), 1)
        dot = (do_ref[...].astype(F32) * o_ref[...].astype(F32)).T

        def tile(tk):
            key = lax.broadcasted_iota(jnp.int32, (tk, 1), 0)
            mask = (key <= qry) & (key >= PAD)
            dqts, dcqs = [], []
            for h in range(nh):
                qh = q_ref[:, dq * h:dq * (h + 1)].astype(BF16)
                kh = k_ref[0:tk, dq * h:dq * (h + 1)].astype(BF16)
                vh = v_ref[0:tk, dv * h:dv * (h + 1)].astype(BF16)
                doh = do_ref[:, dv * h:dv * (h + 1)].astype(BF16)
                delta = jnp.sum(dot[dv * h:dv * (h + 1), :], axis=0, keepdims=True)
                st = _nt(kh, qh) * scale
                if use_bias:
                    st = st + (cr_ref[h:h + 1, :] - ckb[h, 0:tk, :])
                pt = jnp.exp(jnp.where(mask, st, NEG) - l_ref[h:h + 1, :])
                dst = pt * (_nt(vh, doh) - delta)
                dsb = dst.astype(BF16)
                dk_ref[0:tk, dq * h:dq * (h + 1)] += _nn(dsb, qh) * scale
                dv_ref[0:tk, dv * h:dv * (h + 1)] += _nn(pt.astype(BF16), doh)
                dqts.append(_nn(kt[dq * h:dq * (h + 1), 0:tk], dsb))
                if use_bias:
                    dcqs.append(jnp.sum(dst, axis=0, keepdims=True))
                    dacc[h, 0:tk, :] += dst
            dq_ref[...] = jnp.concatenate(dqts, axis=0).T * scale
            if use_bias:
                dcq_ref[...] = jnp.concatenate(dcqs + [jnp.zeros((8 - nh, tq), F32)], axis=0)

        for t0, t1 in _segments(nq, not use_bias):
            pl.when((i >= t0) & (i < t1))(functools.partial(tile, t1 * BLK))

        if use_bias:
            @pl.when(i == nq - 1)
            def _():
                lane = lax.broadcasted_iota(jnp.int32, (1, BLK), 1)
                tot = jnp.zeros((t, BLK), F32)
                for h in range(nh):
                    tot += jnp.where(lane == lane0 + h, jnp.sum(dacc[h], axis=1, keepdims=True), 0.0)
                dck_ref[...] = tot

    keys_q = pl.BlockSpec((t, nh * dq), lambda i: (0, 0))
    keys_v = pl.BlockSpec((t, nh * dv), lambda i: (0, 0))
    keys_c = pl.BlockSpec((t, BLK), lambda i: (0, 0))
    qrow = pl.BlockSpec((8, tq), lambda i: (0, i))
    in_specs = [pl.BlockSpec((tq, nh * dq), lambda i: (i, qcol)),
                pl.BlockSpec((t, nh * dq), lambda i: (0, kcol)),
                pl.BlockSpec((t, nh * dv), lambda i: (0, vcol)),
                pl.BlockSpec((tq, nh * dv), lambda i: (i, docol)),
                qrow,
                pl.BlockSpec((tq, nh * dv), lambda i: (i, ocol))]
    args = [q, k, v, do, lse_row, o]
    out_specs = [pl.BlockSpec((tq, nh * dq), lambda i: (i, 0)), keys_q, keys_v]
    out_shape = [jax.ShapeDtypeStruct((t, nh * dq), F32), jax.ShapeDtypeStruct((t, nh * dq), F32),
                 jax.ShapeDtypeStruct((t, nh * dv), F32)]
    scratch = [pltpu.VMEM((nh * dq, t), BF16)]
    if use_bias:
        in_specs += [keys_c, qrow]
        args += [c_col, c_row]
        out_specs += [qrow, keys_c]
        out_shape += [jax.ShapeDtypeStruct((8, t), F32), jax.ShapeDtypeStruct((t, BLK), F32)]
        scratch += [pltpu.VMEM((nh, t, BLK), F32), pltpu.VMEM((nh, t, BLK), F32)]
    return pl.pallas_call(
        body, name="attn_bwd", grid=(nq,),
        in_specs=in_specs, out_specs=out_specs, out_shape=out_shape, scratch_shapes=scratch,
        compiler_params=_cp("arbitrary"),
    )(*args)


def fox_pre(proj, fb):
    t = proj.shape[0]
    nb = t // BLK

    def body(sm_ref, fb_ref, c_ref, cr_ref):
        x = sm_ref[...] + fb_ref[...]
        lane = lax.broadcasted_iota(jnp.int32, (1, BLK), 1)
        keep = _valid_rows(t, 0) & (lane >= SM_F) & (lane < SM_F + FOX_H)
        logf = jnp.where(keep, jnp.minimum(x, 0.0) - jnp.log(1.0 + jnp.exp(-jnp.abs(x))), 0.0)
        tri = _tri().astype(F32)
        carry = jnp.zeros((1, BLK), F32)
        for b in range(nb):
            cb = _nn_hi(tri, logf[b * BLK:(b + 1) * BLK, :]) + carry
            c_ref[b * BLK:(b + 1) * BLK, :] = cb
            carry = cb[BLK - 1:BLK, :]
        cr_ref[...] = c_ref[...].T[SM_F:SM_F + 8, :]

    return pl.pallas_call(
        body, name="fox_pre", grid=(1,),
        in_specs=[pl.BlockSpec((t, BLK), lambda i: (0, C_SM // BLK)), pl.BlockSpec((1, BLK), lambda i: (0, 0))],
        out_specs=[pl.BlockSpec((t, BLK), lambda i: (0, 0)), pl.BlockSpec((8, t), lambda i: (0, 0))],
        out_shape=[jax.ShapeDtypeStruct((t, BLK), F32), jax.ShapeDtypeStruct((8, t), F32)],
        compiler_params=_cp("arbitrary"),
    )(proj, fb)


def fox_pre_bwd(dcq, dck, proj, fb, dsm_in):
    t = proj.shape[0]
    nb = t // BLK

    def body(dcq_ref, dck_ref, sm_ref, fb_ref, din_ref, dsm_ref, dfb_ref, scr):
        triu = _tri(lower=False).astype(F32)
        carry = jnp.zeros((1, BLK), F32)
        scr[...] = jnp.concatenate([jnp.zeros((SM_F, t), F32), dcq_ref[...], jnp.zeros((BLK - SM_F - 8, t), F32)], axis=0).T
        for b in range(nb - 1, -1, -1):
            blk = scr[b * BLK:(b + 1) * BLK, :] - dck_ref[b * BLK:(b + 1) * BLK, :]
            cb = _nn_hi(triu, blk) + carry
            scr[b * BLK:(b + 1) * BLK, :] = cb
            carry = cb[0:1, :]
        x = sm_ref[...] + fb_ref[...]
        lane = lax.broadcasted_iota(jnp.int32, (1, BLK), 1)
        keep = _valid_rows(t, 0) & (lane >= SM_F) & (lane < SM_F + FOX_H)
        df = jnp.where(keep, scr[...] * _sig(-x), 0.0)
        dfb_ref[...] = jnp.sum(df, axis=0, keepdims=True)
        dsm_ref[...] = din_ref[...] + df

    full = pl.BlockSpec((t, BLK), lambda i: (0, 0))
    return pl.pallas_call(
        body, name="fox_pre_bwd", grid=(1,),
        in_specs=[pl.BlockSpec((8, t), lambda i: (0, 0)), full,
                  pl.BlockSpec((t, BLK), lambda i: (0, C_SM // BLK)), pl.BlockSpec((1, BLK), lambda i: (0, 0)), full],
        out_specs=[full, pl.BlockSpec((1, BLK), lambda i: (0, 0))],
        out_shape=[jax.ShapeDtypeStruct((t, BLK), F32), jax.ShapeDtypeStruct((1, BLK), F32)],
        scratch_shapes=[pltpu.VMEM((t, BLK), F32)],
        compiler_params=_cp("arbitrary"),
    )(dcq, dck, proj, fb, dsm_in)


def _swap_rope(x):
    lane = lax.broadcasted_iota(jnp.int32, (1, BLK), 1)
    return jnp.where((lane >= SM_KR) & (lane < SM_KR + 16), pltpu.roll(x, BLK - 16, 1),
                     jnp.where((lane >= SM_KR + 16) & (lane < SM_KR + 32), pltpu.roll(x, 16, 1), 0.0))


def _rms(x, g):
    r = lax.rsqrt(jnp.mean(x * x, axis=1, keepdims=True) + EPS)
    return r, x * r


def mla_pre(proj, qg, kvg, wq, wk, wv, cosq, sinq):
    t = proj.shape[0]
    tm = _row_tile(t)

    def body(cq_ref, ckv_ref, sm_ref, qg_ref, kvg_ref, wq_ref, wk_ref, wv_ref, cos_ref, sin_ref,
             q_ref, k_ref, v_ref, cqn_ref, ckvn_ref):
        cs, sn = cos_ref[...], sin_ref[...]
        _, xh = _rms(cq_ref[...], None)
        cqn = (xh * qg_ref[...]).astype(BF16)
        cqn_ref[...] = cqn
        qraw = _nn(cqn, wq_ref[...])
        qs = []
        for h in range(MLA_H):
            hb = qraw[:, BLK * h:BLK * (h + 1)]
            qs.append(hb * cs + _swap_rope(hb) * sn)
        q_ref[...] = jnp.concatenate(qs, axis=1).astype(BF16)
        _, kh = _rms(ckv_ref[...], None)
        ckvn = (kh * kvg_ref[...]).astype(BF16)
        ckvn_ref[...] = ckvn
        kraw = _nn(ckvn, wk_ref[...])
        v_ref[...] = _nn(ckvn, wv_ref[...]).astype(BF16)
        lane = lax.broadcasted_iota(jnp.int32, (1, BLK), 1)
        kr = sm_ref[...]
        krr = jnp.where((lane >= SM_KR) & (lane < SM_KR + MLA_ROPE), kr * cs + _swap_rope(kr) * sn, 0.0)
        k_ref[...] = jnp.concatenate([kraw[:, BLK * h:BLK * (h + 1)] + krr for h in range(MLA_H)], axis=1).astype(BF16)

    def rows(w, cb):
        return pl.BlockSpec((tm, w), lambda i: (i, cb))

    def whole(a):
        return pl.BlockSpec(a.shape, lambda i: (0, 0))

    return pl.pallas_call(
        body, name="mla_pre", grid=(t // tm,),
        in_specs=[rows(MLA_QL, C_CQ // MLA_QL), rows(MLA_KVL, C_CKV // MLA_KVL), rows(BLK, C_SM // BLK),
                  whole(qg), whole(kvg), whole(wq), whole(wk), whole(wv), rows(BLK, 0), rows(BLK, 0)],
        out_specs=[rows(512, 0), rows(512, 0), rows(256, 0), rows(MLA_QL, 0), rows(MLA_KVL, 0)],
        out_shape=[jax.ShapeDtypeStruct((t, 512), BF16), jax.ShapeDtypeStruct((t, 512), BF16),
                   jax.ShapeDtypeStruct((t, 256), BF16), jax.ShapeDtypeStruct((t, MLA_QL), BF16),
                   jax.ShapeDtypeStruct((t, MLA_KVL), BF16)],
        compiler_params=_cp("arbitrary"),
    )(proj, proj, proj, qg, kvg, wq, wk, wv, cosq, sinq)


def mla_pre_bwd(dq, dk, dv, proj, cqn, ckvn, qg, kvg, wq, wk, wv, cosq, sinq, dsm_in):
    t = proj.shape[0]
    tm = _row_tile(t)

    def body(dq_ref, dk_ref, dv_ref, cq_ref, ckv_ref, cqn_ref, ckvn_ref, qg_ref, kvg_ref, wq_ref, wk_ref, wv_ref,
             cos_ref, sin_ref, din_ref, dcq_ref, dckv_ref, dsm_ref, dwq_ref, dwk_ref, dwv_ref, dqg_ref, dkvg_ref):
        i = pl.program_id(0)

        @pl.when(i == 0)
        def _():
            for r in (dwq_ref, dwk_ref, dwv_ref, dqg_ref, dkvg_ref):
                r[...] = jnp.zeros_like(r)

        cs, sn = cos_ref[...], sin_ref[...]
        lane = lax.broadcasted_iota(jnp.int32, (1, BLK), 1)

        def unrope(dy):
            return dy * cs + _swap_rope(dy * sn)

        dqp = jnp.concatenate([unrope(dq_ref[:, BLK * h:BLK * (h + 1)]) for h in range(MLA_H)], axis=1).astype(BF16)
        dwq_ref[...] += _tn(cqn_ref[...], dqp)
        dcqn = _nt(dqp, wq_ref[...])
        r, xh = _rms(cq_ref[...], None)
        dqg_ref[...] += jnp.sum(dcqn * xh, axis=0, keepdims=True)
        dxh = dcqn * qg_ref[...]
        dcq_ref[...] = r * (dxh - xh * jnp.mean(dxh * xh, axis=1, keepdims=True))

        dkn, dkr = [], jnp.zeros((tm, BLK), F32)
        for h in range(MLA_H):
            blk = dk_ref[:, BLK * h:BLK * (h + 1)]
            dkn.append(jnp.where(lane < MLA_NOPE, blk, 0.0))
            dkr += jnp.where((lane >= SM_KR) & (lane < SM_KR + MLA_ROPE), blk, 0.0)
        dknb = jnp.concatenate(dkn, axis=1).astype(BF16)
        dvb = dv_ref[...].astype(BF16)
        ckvn = ckvn_ref[...]
        dwk_ref[...] += _tn(ckvn, dknb)
        dwv_ref[...] += _tn(ckvn, dvb)
        dckvn = _nt(dknb, wk_ref[...]) + _nt(dvb, wv_ref[...])
        r2, kh = _rms(ckv_ref[...], None)
        dkvg_ref[...] += jnp.sum(dckvn * kh, axis=0, keepdims=True)
        dkh = dckvn * kvg_ref[...]
        dckv_ref[...] = r2 * (dkh - kh * jnp.mean(dkh * kh, axis=1, keepdims=True))
        dsm_ref[...] = din_ref[...] + jnp.where((lane >= SM_KR) & (lane < SM_KR + MLA_ROPE), unrope(dkr), 0.0)

    def rows(w, cb):
        return pl.BlockSpec((tm, w), lambda i: (i, cb))

    def whole(a):
        return pl.BlockSpec(a.shape, lambda i: (0, 0))

    def wshape(a):
        return jax.ShapeDtypeStruct(a.shape, F32)

    return pl.pallas_call(
        body, name="mla_pre_bwd", grid=(t // tm,),
        in_specs=[rows(512, 0), rows(512, 0), rows(256, 0), rows(MLA_QL, C_CQ // MLA_QL), rows(MLA_KVL, C_CKV // MLA_KVL),
                  rows(MLA_QL, 0), rows(MLA_KVL, 0), whole(qg), whole(kvg), whole(wq), whole(wk), whole(wv),
                  rows(BLK, 0), rows(BLK, 0), rows(BLK, 0)],
        out_specs=[rows(MLA_QL, 0), rows(MLA_KVL, 0), rows(BLK, 0), whole(wq), whole(wk), whole(wv), whole(qg), whole(kvg)],
        out_shape=[jax.ShapeDtypeStruct((t, MLA_QL), F32), jax.ShapeDtypeStruct((t, MLA_KVL), F32),
                   jax.ShapeDtypeStruct((t, BLK), F32), wshape(wq), wshape(wk), wshape(wv), wshape(qg), wshape(kvg)],
        compiler_params=_cp("arbitrary"),
    )(dq, dk, dv, proj, proj, cqn, ckvn, qg, kvg, wq, wk, wv, cosq, sinq, dsm_in)


def _slot_sum(me, own, recv_ref):
    gg = own.astype(F32)
    for s in range(N_DEV):
        gg = gg + jnp.where(me == s, 0.0, recv_ref[s].astype(F32))
    return gg


def adamw(w, m, v, g=None, recv=None, own=None, me_arr=None):
    shape = w.shape
    c = shape[-1]
    from_recv = recv is not None
    if not from_recv:
        me_arr = jnp.zeros((1,), jnp.int32)
    nl = len(recv) if from_recv else 1
    rws = w.size // c // nl
    tr = rws
    for d in (1024, 512, 352, 256, 128, 64, 32, 16, 8):
        if rws % d == 0 and d * c * 4 <= (2 << 20):
            tr = d
            break
    nt = rws // tr
    w2, m2, v2 = (a.reshape(nl, rws, c) for a in (w, m, v))
    if from_recv:
        gin = [a.reshape(N_DEV, rws, c) for a in list(recv) + list(own)]
    else:
        gin = [g.reshape(1, rws, c)]

    def body(me_ref, w_ref, m_ref, v_ref, *rest):
        g_refs, outs = rest[:len(gin)], rest[len(gin):]
        if from_recv:
            g_out, outs = outs[0], outs[1:]
            for li in range(nl):
                @pl.when(pl.program_id(0) == li)
                def _(li=li):
                    g_out[...] = _slot_sum(me_ref[0], g_refs[nl + li][...], g_refs[li])
            gg = g_out[...]
        else:
            gg = g_refs[0][...]
        d_ref, nm_ref, nv_ref = outs
        nm = B1 * m_ref[...] + (1.0 - B1) * gg
        nv = B2 * v_ref[...] + (1.0 - B2) * (gg * gg)
        mh = nm / (1.0 - B1 ** STEP)
        vh = nv / (1.0 - B2 ** STEP)
        d_ref[...] = -LR * (mh / (jnp.sqrt(vh) + AEPS) + WD * w_ref[...])
        nm_ref[...] = nm
        nv_ref[...] = nv

    row = pl.BlockSpec((None, tr, c), lambda l, i, me: (l, i, 0))
    if from_recv:
        gspecs = [pl.BlockSpec((N_DEV, tr, c), lambda l, i, me, li=li: (0, jnp.where(l == li, i, 0), 0))
                  for li in range(nl)]
        gspecs += [pl.BlockSpec((None, tr, c), lambda l, i, me, li=li: (me[0], jnp.where(l == li, i, 0), 0))
                   for li in range(nl)]
    else:
        gspecs = [row]
    nout = 4 if from_recv else 3
    outs = pl.pallas_call(
        body, name="adamw",
        grid_spec=pltpu.PrefetchScalarGridSpec(num_scalar_prefetch=1, grid=(nl, nt), in_specs=[row, row, row] + gspecs,
                                               out_specs=[row] * nout),
        out_shape=[jax.ShapeDtypeStruct((nl, rws, c), F32)] * nout,
        compiler_params=_cp("arbitrary", "arbitrary"),
    )(me_arr, w2, m2, v2, *gin)
    return tuple(o.reshape(shape) for o in outs)


def sum_slots(recv, own=None, me_arr=None):
    _, r, c = recv.shape
    if own is None:
        own, me_arr = recv, jnp.zeros((1,), jnp.int32)
        plain = True
    else:
        plain = False

    def body(me_ref, r_ref, own_ref, o_ref):
        if plain:
            gg = r_ref[0].astype(F32)
            for s in range(1, N_DEV):
                gg = gg + r_ref[s].astype(F32)
            o_ref[...] = gg
        else:
            o_ref[...] = _slot_sum(me_ref[0], own_ref[...], r_ref)

    return pl.pallas_call(
        body, name="sum_slots",
        grid_spec=pltpu.PrefetchScalarGridSpec(
            num_scalar_prefetch=1, grid=(1,),
            in_specs=[pl.BlockSpec((N_DEV, r, c), lambda i, me: (0, 0, 0)),
                      pl.BlockSpec((None, r, c), lambda i, me: (me[0], 0, 0))],
            out_specs=pl.BlockSpec((r, c), lambda i, me: (0, 0))),
        out_shape=jax.ShapeDtypeStruct((r, c), F32),
        compiler_params=_cp("arbitrary"),
    )(me_arr, recv, own)


_FLIPS = [(0, 0, 1), (0, 1, 0), (0, 1, 1), (1, 0, 0), (1, 0, 1), (1, 1, 0), (1, 1, 1)]
_ANY = pl.BlockSpec(memory_space=pl.ANY)


def _mesh_place():
    x, y, c = lax.axis_index("x"), lax.axis_index("y"), lax.axis_index("c")
    me = 4 * x + 2 * y + c
    peers = [((x + fx) % 2, (y + fy) % 2, (c + fc) % 2) for fx, fy, fc in _FLIPS]
    return me, peers


def place_own(src, l, dtype, me_arr):
    _, r, c = src.shape
    tr = r
    for d in (512, 352, 256, 128, 64, 32, 16, 8):
        if r % d == 0 and d * c * 4 <= (2 << 20):
            tr = d
            break

    def body(me_ref, s_ref, o_ref):
        o_ref[...] = s_ref[...].astype(dtype)

    return pl.pallas_call(
        body, name="place_own",
        grid_spec=pltpu.PrefetchScalarGridSpec(
            num_scalar_prefetch=1, grid=(r // tr,),
            in_specs=[pl.BlockSpec((None, tr, c), lambda i, me: (l, i, 0))],
            out_specs=pl.BlockSpec((None, tr, c), lambda i, me: (me[0], i, 0))),
        out_shape=jax.ShapeDtypeStruct((N_DEV, r, c), dtype),
        compiler_params=_cp("arbitrary"),
    )(me_arr, src)


_HBM = pl.BlockSpec(memory_space=pltpu.HBM)
_SEMS = pl.BlockSpec(memory_space=pltpu.SEMAPHORE)
_EFFECT = pltpu.SideEffectType.DATAFLOW_SIDE_EFFECTING


def exchange_start(mode, arrays, name, after=None):
    n = len(arrays)
    gather = mode == "gather"
    ns = 0 if gather else n
    zones = list(arrays) if gather else [lax.empty(a.shape, a.dtype) for a in arrays]
    ops = ([] if gather else list(arrays)) + zones
    extra = [] if after is None else [after]

    def body(*refs):
        srcs, lands = refs[:ns], refs[ns:ns + n]
        send_sems, recv_sems = refs[ns + n + len(extra)], refs[ns + n + len(extra) + 1]
        token = refs[-1]
        me, peers = _mesh_place()
        ids = [4 * p[0] + 2 * p[1] + p[2] for p in peers]
        for j in range(n):
            for k in range(N_DEV - 1):
                src = lands[j].at[me] if gather else srcs[j].at[ids[k]]
                pltpu.make_async_remote_copy(src_ref=src, dst_ref=lands[j].at[me],
                                             send_sem=send_sems.at[j * (N_DEV - 1) + k],
                                             recv_sem=recv_sems.at[j * (N_DEV - 1) + k], device_id=peers[k],
                                             device_id_type=pl.DeviceIdType.MESH).start()
        token[...] = jnp.zeros_like(token)

    nsem = n * (N_DEV - 1)
    res = pl.pallas_call(
        body, name=name,
        in_specs=[_HBM] * (ns + n) + [_ANY] * len(extra),
        out_specs=(_SEMS, _SEMS, *[_HBM] * (ns + n), pl.BlockSpec(memory_space=pltpu.VMEM)),
        out_shape=(pltpu.SemaphoreType.DMA((nsem,)), pltpu.SemaphoreType.DMA((nsem,)),
                   *[pltpu.HBM(a.shape, a.dtype) for a in ops], jax.ShapeDtypeStruct((8, BLK), F32)),
        input_output_aliases={i: 2 + i for i in range(ns + n)},
        compiler_params=pltpu.CompilerParams(has_side_effects=_EFFECT),
    )(*[pltpu.with_memory_space_constraint(a, pltpu.HBM) for a in ops], *extra)
    return dict(gather=gather, send=res[0], recv=res[1], srcs=list(res[2:2 + ns]), lands=list(res[2 + ns:2 + ns + n]),
                token=res[-1])


def exchange_wait(hd, idxs, name, after):
    gather = hd["gather"]
    n = len(idxs)
    ns = 0 if gather else n
    ops = ([] if gather else [hd["srcs"][j] for j in idxs]) + [hd["lands"][j] for j in idxs]

    def body(*refs):
        srcs, lands = refs[:ns], refs[ns:ns + n]
        send_sems, recv_sems = refs[ns + n], refs[ns + n + 1]
        me, peers = _mesh_place()
        ids = [4 * p[0] + 2 * p[1] + p[2] for p in peers]
        for p, j in enumerate(idxs):
            for k in range(N_DEV - 1):
                src = lands[p].at[me] if gather else srcs[p].at[ids[k]]
                cp = pltpu.make_async_remote_copy(src_ref=src, dst_ref=lands[p].at[ids[k]],
                                                  send_sem=send_sems.at[j * (N_DEV - 1) + k],
                                                  recv_sem=recv_sems.at[j * (N_DEV - 1) + k], device_id=peers[k],
                                                  device_id_type=pl.DeviceIdType.MESH)
                cp.wait_send()
                cp.wait_recv()

    res = pl.pallas_call(
        body, name=name,
        in_specs=[_HBM] * (ns + n) + [_SEMS, _SEMS, _ANY],
        out_specs=[_HBM] * (ns + n),
        out_shape=[pltpu.HBM(a.shape, a.dtype) for a in ops],
        input_output_aliases={i: i for i in range(ns + n)},
        compiler_params=pltpu.CompilerParams(has_side_effects=_EFFECT),
    )(*ops, hd["send"], hd["recv"], after)
    return list(res[:ns]), list(res[ns:])


def _chip_place():
    x, y, c = lax.axis_index("x"), lax.axis_index("y"), lax.axis_index("c")
    chips = [((x + 1) % 2, y), (x, (y + 1) % 2), ((x + 1) % 2, (y + 1) % 2)]
    ident = lambda p: 4 * p[0] + 2 * p[1] + p[2]
    return dict(me=4 * x + 2 * y + c, sib=(x, y, 1 - c), sib_id=4 * x + 2 * y + 1 - c,
                same=[(cx, cy, c) for cx, cy in chips], same_ids=[ident((cx, cy, c)) for cx, cy in chips],
                other_ids=[ident((cx, cy, 1 - c)) for cx, cy in chips])


def _remote(src, dst, send_sem, recv_sem, dev):
    return pltpu.make_async_remote_copy(src_ref=src, dst_ref=dst, send_sem=send_sem, recv_sem=recv_sem, device_id=dev,
                                        device_id_type=pl.DeviceIdType.MESH)


def gather_start(zones, name):
    n = len(zones)

    def body(*refs):
        lands, send_sems, recv_sems, token = refs[:n], refs[n], refs[n + 1], refs[-1]
        pc = _chip_place()
        for j in range(n):
            own = lands[j].at[pc["me"]]
            for k, dev in enumerate([pc["sib"]] + pc["same"]):
                _remote(own, own, send_sems.at[4 * j + k], recv_sems.at[4 * j + k], dev).start()
        token[...] = jnp.zeros_like(token)

    res = pl.pallas_call(
        body, name=name,
        in_specs=[_HBM] * n,
        out_specs=(_SEMS, _SEMS, *[_HBM] * n, pl.BlockSpec(memory_space=pltpu.VMEM)),
        out_shape=(pltpu.SemaphoreType.DMA((4 * n,)), pltpu.SemaphoreType.DMA((4 * n,)),
                   *[pltpu.HBM(a.shape, a.dtype) for a in zones], jax.ShapeDtypeStruct((8, BLK), F32)),
        input_output_aliases={i: 2 + i for i in range(n)},
        compiler_params=pltpu.CompilerParams(has_side_effects=_EFFECT),
    )(*[pltpu.with_memory_space_constraint(a, pltpu.HBM) for a in zones])
    return dict(send=res[0], recv=res[1], lands=list(res[2:2 + n]), token=res[-1])


def gather_relay(hd, idxs, name, after):
    n = len(idxs)

    def body(*refs):
        lands, send_sems, recv_sems = refs[:n], refs[n], refs[n + 1]
        fsend, frecv, token = refs[n + 3 + n], refs[n + 4 + n], refs[-1]
        pc = _chip_place()
        for p, j in enumerate(idxs):
            for k in range(3):
                _remote(lands[p].at[pc["me"]], lands[p].at[pc["same_ids"][k]], send_sems.at[4 * j + 1 + k],
                        recv_sems.at[4 * j + 1 + k], pc["same"][k]).wait_recv()
        for p in range(n):
            for k in range(3):
                got = lands[p].at[pc["same_ids"][k]]
                _remote(got, got, fsend.at[3 * p + k], frecv.at[3 * p + k], pc["sib"]).start()
        token[...] = jnp.zeros_like(token)

    ops = [hd["lands"][j] for j in idxs]
    res = pl.pallas_call(
        body, name=name,
        in_specs=[_HBM] * n + [_SEMS, _SEMS, _ANY],
        out_specs=(*[_HBM] * n, _SEMS, _SEMS, pl.BlockSpec(memory_space=pltpu.VMEM)),
        out_shape=(*[pltpu.HBM(a.shape, a.dtype) for a in ops], pltpu.SemaphoreType.DMA((3 * n,)),
                   pltpu.SemaphoreType.DMA((3 * n,)), jax.ShapeDtypeStruct((8, BLK), F32)),
        input_output_aliases={i: i for i in range(n)},
        compiler_params=pltpu.CompilerParams(has_side_effects=_EFFECT),
    )(*ops, hd["send"], hd["recv"], after)
    return dict(lands=list(res[:n]), fsend=res[n], frecv=res[n + 1], token=res[-1])


def gather_wait(hd, rl, idxs, name, after):
    n = len(idxs)

    def body(*refs):
        lands, send_sems, recv_sems, fsend, frecv = refs[:n], refs[n], refs[n + 1], refs[n + 2], refs[n + 3]
        pc = _chip_place()
        for p, j in enumerate(idxs):
            own = lands[p].at[pc["me"]]
            for k, dev in enumerate([pc["sib"]] + pc["same"]):
                _remote(own, own, send_sems.at[4 * j + k], recv_sems.at[4 * j + k], dev).wait_send()
            _remote(own, lands[p].at[pc["sib_id"]], send_sems.at[4 * j], recv_sems.at[4 * j], pc["sib"]).wait_recv()
            for k in range(3):
                cp = _remote(lands[p].at[pc["same_ids"][k]], lands[p].at[pc["other_ids"][k]], fsend.at[3 * p + k],
                             frecv.at[3 * p + k], pc["sib"])
                cp.wait_send()
                cp.wait_recv()

    res = pl.pallas_call(
        body, name=name,
        in_specs=[_HBM] * n + [_SEMS, _SEMS, _SEMS, _SEMS, _ANY],
        out_specs=[_HBM] * n,
        out_shape=[pltpu.HBM(a.shape, a.dtype) for a in rl["lands"]],
        input_output_aliases={i: i for i in range(n)},
        compiler_params=pltpu.CompilerParams(has_side_effects=_EFFECT),
    )(*rl["lands"], hd["send"], hd["recv"], rl["fsend"], rl["frecv"], after)
    return list(res)


def _pad_cols(a, n):
    return jnp.pad(a, ((0, 0),) * (a.ndim - 1) + ((0, n - a.shape[-1]),))


def w_in_to_padded(w):
    z = lambda n: jnp.zeros(w.shape[:-1] + (n,), w.dtype)
    return jnp.concatenate([
        w[..., 0:1280], w[..., 1288:2056], w[..., 2060:2316], w[..., 2316:2444],
        w[..., 1280:1288], w[..., 2056:2060], z(SM_KR - SM_F - FOX_H), w[..., 2444:2476], z(BLK - SM_KR - MLA_ROPE)], axis=-1)


def w_in_from_padded(g):
    s = C_SM
    return jnp.concatenate([
        g[..., 0:1280], g[..., s + SM_DT:s + SM_DT + 8], g[..., 1280:2048], g[..., s + SM_F:s + SM_F + 4],
        g[..., 2048:2304], g[..., 2304:2432], g[..., s + SM_KR:s + SM_KR + MLA_ROPE]], axis=-1)


def _unshard_cols(gth):
    n, r, c = gth.shape
    return jnp.transpose(gth, (1, 0, 2)).reshape(r, n * c)


def _shard_cols(full):
    r, nc = full.shape
    return jnp.transpose(full.reshape(r, N_DEV, nc // N_DEV), (1, 0, 2))


def mla_weights(uq_g, ukv_g):
    uq = _unshard_cols(uq_g)
    dqh = MLA_NOPE + MLA_ROPE
    wq = jnp.concatenate([_pad_cols(uq[:, dqh * h:dqh * (h + 1)], BLK) for h in range(MLA_H)], axis=1)
    wk = jnp.concatenate([_pad_cols(ukv_g[2 * h], BLK) for h in range(MLA_H)], axis=1)
    wv = jnp.concatenate([ukv_g[2 * h + 1] for h in range(MLA_H)], axis=1)
    return wq, wk, wv


def mla_weight_grads(dwq, dwk, dwv):
    dqh = MLA_NOPE + MLA_ROPE
    duq = _shard_cols(jnp.concatenate([dwq[:, BLK * h:BLK * h + dqh] for h in range(MLA_H)], axis=1))
    parts = []
    for h in range(MLA_H):
        parts += [dwk[:, BLK * h:BLK * h + MLA_NOPE], dwv[:, MLA_V * h:MLA_V * (h + 1)]]
    return duq, jnp.stack(parts, axis=0)


def rope_tables(t):
    pos = (jnp.arange(t, dtype=jnp.int32) - PAD).astype(F32)
    inv_freq = 1.0 / (10000.0 ** (jnp.arange(0, MLA_ROPE, 2, dtype=F32) / MLA_ROPE))
    ang = pos[:, None] * inv_freq[None, :]
    cos, sin = jnp.cos(ang), jnp.sin(ang)
    one, zero = jnp.ones((t, SM_KR), F32), jnp.zeros((t, SM_KR), F32)
    tail = BLK - SM_KR - MLA_ROPE
    cosq = jnp.concatenate([one, cos, cos, jnp.ones((t, tail), F32)], axis=1)
    sinq = jnp.concatenate([zero, -sin, sin, jnp.zeros((t, tail), F32)], axis=1)
    return cosq, sinq


def _lanes(v, off=0):
    return jnp.pad(v.astype(F32), (off, BLK - off - v.shape[0]))[None, :]


def layer_fwd(x, ln, hb, getw, tabs, ahead):
    sv = {"h0b": hb}
    W = dict(getw("ffn1", hb))
    ln1 = (W["ln1_g"], W["ln1_b"])
    u, v, r1, h1b = ffn_fwd_seq(x, ln, W["g1"], W["u1"], W["d1"], ln1)
    sv.update(u1=u, v1=v, r1=r1, h1b=h1b)
    W.update(getw("mix", h1b))
    ln2 = (W["ln2_g"], W["ln2_b"])
    proj = mm_nn(h1b, W["w_in"])
    xa = conv_fwd(proj, W["conv_w"], W["conv_b"])
    y_ssd, sprev = ssd_fwd(xa, proj, W["dtb"], W["alog"], W["dskip"], W["normg"])
    c_col, c_row = fox_pre(proj, W["fb"])
    y_fox, lse_f = attn_fwd(proj, proj, proj, C_FQ // 256, C_FK // 256, C_FV // 256, FOX_H, FOX_DH, FOX_DH,
                            FOX_DH ** -0.5, c_col, c_row, SM_F)
    ahead(0, "ffn2", y_fox)
    q, k, vv, cqn, ckvn = mla_pre(proj, W["qg"], W["kvg"], W["wq"], W["wk"], W["wv"], *tabs)
    y_mla, lse_m = attn_fwd(q, k, vv, 0, 0, 0, MLA_H, BLK, MLA_V, (MLA_NOPE + MLA_ROPE) ** -0.5)
    mixcat = jnp.concatenate([y_ssd, y_fox, y_mla], axis=1)
    r2, h2b = mm_res_ln(mixcat, W["w_out"], r1, ln1, ln2)
    sv.update(proj=proj, xa=xa, sprev=sprev, c_col=c_col, c_row=c_row, lse_f=lse_f, q=q, k=k, v=vv, cqn=cqn, ckvn=ckvn,
              lse_m=lse_m, mixcat=mixcat, r2=r2, h2b=h2b)
    W.update(getw("ffn2", h2b))
    ahead(1, "ffn1", h2b)
    ln3 = (W["ln3_g"], W["ln3_b"])
    u, v, r3, h3b = ffn_fwd_seq(r2, ln2, W["g2"], W["u2"], W["d2"], ln3)
    sv.update(u2=u, v2=v, r3=r3, W=W)
    return r3, ln3, h3b, sv


def ffn_bwd(parts, r, gamma, hb_in, u, v, wg, wu, wd, after=None):
    dh, dwg, dwu, dwd, dg, db = ffn_bwd_seq(parts, r, gamma, hb_in, u, v, wg, wu, wd, after)
    return dh, dict(d=dwd, g=dwg, u=dwu, ln_g=dg, ln_b=db)


def layer_bwd(parts, sv, emit, tabs, after):
    G = {}
    W = sv["W"]
    dh2, g2 = ffn_bwd(parts, sv["r3"], W["ln3_g"], sv["h2b"], sv["u2"], sv["v2"], W["g2"], W["u2"], W["d2"], after)
    G.update(g2=g2["g"], u2=g2["u"], d2=g2["d"], ln3_g=g2["ln_g"], ln3_b=g2["ln_b"])
    tok = emit("ffn2", G)
    dr2, dmc, G["w_out"], G["ln2_g"], G["ln2_b"] = oproj_bwd(dh2, sv["r2"], W["ln2_g"], sv["mixcat"], W["w_out"], tok)
    proj = sv["proj"]
    dxa, dz, dsm, G["normg"], G["dskip"], G["alog"], G["dtb"] = ssd_bwd(
        dmc, sv["xa"], proj, sv["sprev"], W["dtb"], W["alog"], W["dskip"], W["normg"])
    dxbc, G["conv_w"], G["conv_b"] = conv_bwd(dxa, proj, W["conv_w"], W["conv_b"])
    dfq, dfk, dfv, dcq, dck = attn_bwd(proj, proj, proj, dmc, sv["lse_f"], sv["mixcat"], C_FQ // 256, C_FK // 256,
                                       C_FV // 256, 2, 2, FOX_H, FOX_DH, FOX_DH, FOX_DH ** -0.5, sv["c_col"], sv["c_row"], SM_F)
    dsm, G["fb"] = fox_pre_bwd(dcq, dck, proj, W["fb"], dsm)
    dq, dk, dv = attn_bwd(sv["q"], sv["k"], sv["v"], dmc, sv["lse_m"], sv["mixcat"], 0, 0, 0, 3, 3, MLA_H, BLK, MLA_V,
                          (MLA_NOPE + MLA_ROPE) ** -0.5)
    dcql, dckv, dsm, G["wq"], G["wk"], G["wv"], G["qg"], G["kvg"] = mla_pre_bwd(
        dq, dk, dv, proj, sv["cqn"], sv["ckvn"], W["qg"], W["kvg"], W["wq"], W["wk"], W["wv"], *tabs, dsm)
    dproj = jnp.concatenate([dz, dxbc, dfq, dfk, dfv, dcql, dckv, dsm], axis=1).astype(BF16)
    dh1p, G["w_in"] = proj_bwd(dproj, sv["h1b"], W["w_in"])
    tok = emit("mix", G)
    dh0, g1 = ffn_bwd([(dr2, ALPHA), (dh1p, 1.0)], sv["r1"], W["ln1_g"], sv["h0b"], sv["u1"], sv["v1"],
                      W["g1"], W["u1"], W["d1"], tok)
    G.update(g1=g1["g"], u1=g1["u"], d1=g1["d"], ln1_g=g1["ln_g"], ln1_b=g1["ln_b"])
    tok = emit("ffn1", G)
    return [(dh0, 1.0)], G, tok


def local_step(x, target, meta_full, getw, emit, ahead=lambda l, stage, after: None):
    t = x.shape[0] + BLK
    tabs = rope_tables(t)
    xr, hb = build_h0(meta_full, x)
    ln = None
    saved = []
    for l in range(NL):
        xr, ln, hb, sv = layer_fwd(xr, ln, hb, functools.partial(getw, l), tabs,
                                   lambda dl, stage, after, l=l: ahead(l + dl, stage, after))
        saved.append(sv)
    dy, loss = loss_head(xr, ln, target)
    parts = [(dy, 1.0)]
    grads = [None] * NL
    tok = None
    for l in range(NL - 1, -1, -1):
        parts, grads[l], tok = layer_bwd(parts, saved[l], functools.partial(emit, l), tabs, tok)
    gx, gmeta = split_dh0(parts[0][0], tok)
    return loss, gx, gmeta, grads


_SMALL = ["ln1_g", "ln1_b", "ln2_g", "ln2_b", "ln3_g", "ln3_b", "conv_b", "ssd_norm_g", "mla_q_norm_g",
          "mla_kv_norm_g", "dt_bias", "a_log", "d_skip", "fox_f_b"]
_SMALL_ROWS = 8
_BIG = ["ffn1_w_gate", "ffn1_w_up", "ffn1_w_down", "w_in", "conv_w", "mla_w_uq", "mla_w_ukv", "w_out",
        "ffn2_w_gate", "ffn2_w_up", "ffn2_w_down"]
_NAMES = ["meta", "ffn1_w_gate", "ffn1_w_up", "ffn1_w_down", "ln1_g", "ln1_b", "w_in", "conv_w", "conv_b", "dt_bias",
          "a_log", "d_skip", "ssd_norm_g", "fox_f_b", "mla_q_norm_g", "mla_w_uq", "mla_kv_norm_g", "mla_w_ukv", "w_out",
          "ln2_g", "ln2_b", "ffn2_w_gate", "ffn2_w_up", "ffn2_w_down", "ln3_g", "ln3_b"]


def pack_small(p):
    flat = jnp.concatenate([p[n].astype(F32) for n in _SMALL], axis=1)
    return _pad_cols(flat, _SMALL_ROWS * D).reshape(NL * _SMALL_ROWS, D)


def unpack_small(a, like):
    flat = a.reshape(NL, _SMALL_ROWS * D)
    out, at = {}, 0
    for n in _SMALL:
        out[n] = flat[:, at:at + like[n].shape[1]]
        at += like[n].shape[1]
    return out


_STAGES = {"ffn1": ["ffn1_w_gate", "ffn1_w_up", "ffn1_w_down"],
           "mix": ["w_in", "conv_w", "mla_w_uq", "mla_w_ukv", "w_out"],
           "ffn2": ["ffn2_w_gate", "ffn2_w_up", "ffn2_w_down"]}


_FFN_T = ("ffn1_w_gate", "ffn1_w_up", "ffn2_w_gate", "ffn2_w_up")


def stage_weights(l, stage, g, rep):
    if stage != "mix":
        i = stage[3]
        return {"g" + i: g[f"ffn{i}_w_gate"].reshape(D_FF, D), "u" + i: g[f"ffn{i}_w_up"].reshape(D_FF, D),
                "d" + i: g[f"ffn{i}_w_down"].reshape(D_FF, D),
                "ln1_g" if i == "1" else "ln3_g": rep["ln1_g" if i == "1" else "ln3_g"][l][None, :],
                "ln1_b" if i == "1" else "ln3_b": rep["ln1_b" if i == "1" else "ln3_b"][l][None, :]}
    W = {}
    W["w_in"] = g["w_in"].reshape(D, N_INP)
    W["w_out"] = g["w_out"].reshape(D, D)
    W["wq"], W["wk"], W["wv"] = mla_weights(g["mla_w_uq"], g["mla_w_ukv"])
    W["conv_w"] = _unshard_cols(g["conv_w"])
    for k in ("ln2_g", "ln2_b", "conv_b"):
        W[k] = rep[k][l][None, :]
    W["normg"] = rep["ssd_norm_g"][l][None, :]
    W["qg"] = rep["mla_q_norm_g"][l][None, :]
    W["kvg"] = rep["mla_kv_norm_g"][l][None, :]
    W["dtb"] = _lanes(rep["dt_bias"][l], SM_DT)
    W["alog"] = _lanes(rep["a_log"][l], SM_DT)
    W["dskip"] = _lanes(rep["d_skip"][l], SM_DT)
    W["fb"] = _lanes(rep["fox_f_b"][l], SM_F)
    return W


def small_grads(G):
    return {"ln1_g": G["ln1_g"][0], "ln1_b": G["ln1_b"][0], "ln2_g": G["ln2_g"][0], "ln2_b": G["ln2_b"][0],
            "ln3_g": G["ln3_g"][0], "ln3_b": G["ln3_b"][0], "conv_b": G["conv_b"][0], "ssd_norm_g": G["normg"][0],
            "mla_q_norm_g": G["qg"][0], "mla_kv_norm_g": G["kvg"][0], "dt_bias": G["dtb"][0, :SSD_H],
            "a_log": G["alog"][0, :SSD_H], "d_skip": G["dskip"][0, :SSD_H], "fox_f_b": G["fb"][0, SM_F:SM_F + FOX_H]}


def big_grads(G, stage):
    if stage != "mix":
        i = stage[-1]
        return {f"ffn{i}_w_{k}": G[k[0] + i].reshape(N_DEV, HS, D) for k in ("gate", "up", "down")}
    duq, dukv = mla_weight_grads(G["wq"], G["wk"], G["wv"])
    return {"w_in": G["w_in"].reshape(N_DEV, D // N_DEV, N_INP), "w_out": G["w_out"].reshape(N_DEV, D // N_DEV, D),
            "mla_w_uq": duq, "mla_w_ukv": dukv, "conv_w": _shard_cols(G["conv_w"])}


def kernel(x, meta, ffn1_w_gate, ffn1_w_up, ffn1_w_down, ln1_g, ln1_b, w_in, conv_w, conv_b, dt_bias, a_log, d_skip, ssd_norm_g, fox_f_b, mla_q_norm_g, mla_w_uq, mla_kv_norm_g, mla_w_ukv, w_out, ln2_g, ln2_b, ffn2_w_gate, ffn2_w_up, ffn2_w_down, ln3_g, ln3_b, loss_target, m_meta, m_ffn1_w_gate, m_ffn1_w_up, m_ffn1_w_down, m_ln1_g, m_ln1_b, m_w_in, m_conv_w, m_conv_b, m_dt_bias, m_a_log, m_d_skip, m_ssd_norm_g, m_fox_f_b, m_mla_q_norm_g, m_mla_w_uq, m_mla_kv_norm_g, m_mla_w_ukv, m_w_out, m_ln2_g, m_ln2_b, m_ffn2_w_gate, m_ffn2_w_up, m_ffn2_w_down, m_ln3_g, m_ln3_b, v_meta, v_ffn1_w_gate, v_ffn1_w_up, v_ffn1_w_down, v_ln1_g, v_ln1_b, v_w_in, v_conv_w, v_conv_b, v_dt_bias, v_a_log, v_d_skip, v_ssd_norm_g, v_fox_f_b, v_mla_q_norm_g, v_mla_w_uq, v_mla_kv_norm_g, v_mla_w_ukv, v_w_out, v_ln2_g, v_ln2_b, v_ffn2_w_gate, v_ffn2_w_up, v_ffn2_w_down, v_ln3_g, v_ln3_b):
    vals = (meta, ffn1_w_gate, ffn1_w_up, ffn1_w_down, ln1_g, ln1_b, w_in, conv_w, conv_b, dt_bias, a_log, d_skip, ssd_norm_g, fox_f_b, mla_q_norm_g, mla_w_uq, mla_kv_norm_g, mla_w_ukv, w_out, ln2_g, ln2_b, ffn2_w_gate, ffn2_w_up, ffn2_w_down, ln3_g, ln3_b)
    moms = (m_meta, m_ffn1_w_gate, m_ffn1_w_up, m_ffn1_w_down, m_ln1_g, m_ln1_b, m_w_in, m_conv_w, m_conv_b, m_dt_bias, m_a_log, m_d_skip, m_ssd_norm_g, m_fox_f_b, m_mla_q_norm_g, m_mla_w_uq, m_mla_kv_norm_g, m_mla_w_ukv, m_w_out, m_ln2_g, m_ln2_b, m_ffn2_w_gate, m_ffn2_w_up, m_ffn2_w_down, m_ln3_g, m_ln3_b)
    vars_ = (v_meta, v_ffn1_w_gate, v_ffn1_w_up, v_ffn1_w_down, v_ln1_g, v_ln1_b, v_w_in, v_conv_w, v_conv_b, v_dt_bias, v_a_log, v_d_skip, v_ssd_norm_g, v_fox_f_b, v_mla_q_norm_g, v_mla_w_uq, v_mla_kv_norm_g, v_mla_w_ukv, v_w_out, v_ln2_g, v_ln2_b, v_ffn2_w_gate, v_ffn2_w_up, v_ffn2_w_down, v_ln3_g, v_ln3_b)
    P = dict(zip(_NAMES, vals))
    M = dict(zip(_NAMES, moms))
    V = dict(zip(_NAMES, vars_))
    me = 4 * lax.axis_index("x") + 2 * lax.axis_index("y") + lax.axis_index("c")

    me_arr = me.astype(jnp.int32).reshape(1)
    for n in _FFN_T:
        P[n], M[n], V[n] = (jnp.swapaxes(a[n], 1, 2) for a in (P, M, V))
    src = dict(P)
    src["w_in"] = w_in_to_padded(P["w_in"])
    order = [("meta", 0)] + [(n, l) for l in range(NL) for names in _STAGES.values() for n in names]
    nfirst = 1 + len(_STAGES["ffn1"])

    def place(n, l):
        return place_own(P["meta"][None] if n == "meta" else src[n], l, F32 if n in ("meta", "conv_w") else BF16, me_arr)

    hg_first = gather_start([place(n, l) for n, l in order[:nfirst]], "gather_start_first")
    hg_rest = gather_start([place(n, l) for n, l in order[nfirst:]], "gather_start_rest")
    zone_of = {nl_: ((hg_first, i) if i < nfirst else (hg_rest, i - nfirst)) for i, nl_ in enumerate(order)}
    relays = {}

    def ahead(l, stage, after):
        if l < NL and (l, stage) not in relays:
            zs = [zone_of[("meta", 0)]] if stage == "meta" else [zone_of[(n, l)] for n in _STAGES[stage]]
            hg, idxs = zs[0][0], [i for _, i in zs]
            relays[(l, stage)] = (hg, idxs, gather_relay(hg, idxs, f"gather_relay_{l}_{stage}", after))

    def arrived(l, stage, after):
        ahead(l, stage, after)
        hg, idxs, rl = relays[(l, stage)]
        return gather_wait(hg, rl, idxs, f"gather_wait_{l}_{stage}", after)

    meta_full = _unshard_cols(arrived(0, "meta", hg_rest["token"])[0])

    def getw(l, stage, after):
        return stage_weights(l, stage, dict(zip(_STAGES[stage], arrived(l, stage, after))), P)

    sent = {}

    def emit(l, stage, G):
        bg = big_grads(G, stage)
        sent[(l, stage)] = exchange_start("scatter", [bg[n] for n in _STAGES[stage]], f"scatter_start_{l}_{stage}")
        return sent[(l, stage)]["token"]

    loss, gx, gmeta, grads = local_step(x[0], loss_target[0], meta_full, getw, emit, ahead)

    small = jnp.concatenate([pack_small({n: jnp.stack([small_grads(g)[n] for g in grads]) for n in _SMALL}), gmeta,
                             jnp.pad(loss, ((0, 7), (0, D - 1)))], axis=0)
    hs = exchange_start("gather", [place_own(small[None], 0, F32, me_arr)], "small_start")

    out = {}
    after = hs["token"]
    for stage in ("ffn2", "mix", "ffn1"):
        names = _STAGES[stage]
        got = [exchange_wait(sent[(l, stage)], list(range(len(names))), f"scatter_wait_{l}_{stage}", after)
               for l in range(NL - 1, -1, -1)][::-1]
        for i, n in enumerate(names):
            own = [got[l][0][i] for l in range(NL)]
            recv = [got[l][1][i] for l in range(NL)]
            if n == "w_in":
                g = jnp.stack([w_in_from_padded(sum_slots(recv[l], own[l], me_arr)) for l in range(NL)])
                out[n] = (g,) + adamw(P[n], M[n], V[n], g=g)
            else:
                out[n] = adamw(P[n], M[n], V[n], recv=recv, own=own, me_arr=me_arr)
                if n in _FFN_T:
                    out[n] = tuple(jnp.swapaxes(a, 1, 2) for a in out[n])
        after = out[names[-1]][1]
    gsmall = sum_slots(exchange_wait(hs, [0], "small_wait", after)[1][0])
    gm = lax.dynamic_slice(gsmall[NL * _SMALL_ROWS:], (0, me * (D // N_DEV)), (N_META, D // N_DEV))
    out["meta"] = (gm,) + adamw(P["meta"], M["meta"], V["meta"], g=gm)
    gs = gsmall[:NL * _SMALL_ROWS]
    sd, sm_, sv_ = adamw(pack_small(P), pack_small(M), pack_small(V), g=gs)
    ups = [unpack_small(a, P) for a in (gs, sd, sm_, sv_)]
    for n in _SMALL:
        out[n] = tuple(u[n] for u in ups)

    loss_all = gsmall[NL * _SMALL_ROWS + N_META, 0]
    flat = [loss_all, gx[None]]
    for k in range(4):
        flat += [out[n][k] for n in _NAMES]
    return tuple(flat)
```

```python
import functools

import jax
import jax.numpy as jnp
from jax import lax
from jax.experimental import pallas as pl
from jax.experimental.pallas import tpu as pltpu

F32, BF16 = jnp.float32, jnp.bfloat16
HI = lax.Precision.HIGHEST

N_DEV = 8
D = 1024
NL = 2
N_META = 16
BLK = 128
PAD = BLK - N_META
D_FF = 2816
HS = D_FF // N_DEV
SSD_H, SSD_P, SSD_N, SSD_G = 8, 64, 64, 2
SSD_D = SSD_H * SSD_P
CONV_K = 4
CONV_D = SSD_D + 2 * SSD_G * SSD_N
FOX_H, FOX_DH = 4, 64
MLA_H, MLA_QL, MLA_KVL, MLA_NOPE, MLA_ROPE, MLA_V = 4, 256, 128, 64, 32, 64
N_IN = 2476
C_Z, C_XBC, C_FQ, C_FK, C_FV, C_CQ, C_CKV, C_SM, N_INP = 0, 512, 1280, 1536, 1792, 2048, 2304, 2432, 2560
SM_DT, SM_F, SM_KR = 0, 8, 64
ALPHA = (2 * NL) ** 0.25
EPS = 1e-5
NEG = -1e30
LR, B1, B2, AEPS, WD, STEP = 0.001, 0.9, 0.999, 1e-08, 0.01, 10
VMEM_MB = 56


def _cp(*sem):
    return pltpu.CompilerParams(dimension_semantics=sem, vmem_limit_bytes=VMEM_MB << 20)


def _nn(a, b):
    return lax.dot_general(a, b, (((1,), (0,)), ((), ())), preferred_element_type=F32)


def _nt(a, b):
    return lax.dot_general(a, b, (((1,), (1,)), ((), ())), preferred_element_type=F32)


def _tn(a, b):
    return lax.dot_general(a, b, (((0,), (0,)), ((), ())), preferred_element_type=F32)


def _nn_hi(a, b):
    return lax.dot_general(a, b, (((1,), (0,)), ((), ())), precision=HI, preferred_element_type=F32)


def _row_tile(t):
    for d in range(640, 15, -16):
        if t % d == 0:
            return d
    raise ValueError(t)


def _sig(x):
    return 1.0 / (1.0 + jnp.exp(-x))


def _tri(lower=True):
    r = lax.broadcasted_iota(jnp.int32, (BLK, BLK), 0)
    c = lax.broadcasted_iota(jnp.int32, (BLK, BLK), 1)
    return (r >= c) if lower else (r <= c)


def build_h0(meta_full, x):
    s = x.shape[0]
    nb = s // BLK + 1

    def body(m_ref, x_ref, h_ref, hb_ref):
        i = pl.program_id(0)

        @pl.when(i == 0)
        def _():
            h = jnp.concatenate([jnp.zeros((PAD, D), F32), m_ref[...]], axis=0)
            h_ref[...] = h
            hb_ref[...] = h.astype(BF16)

        @pl.when(i > 0)
        def _():
            h_ref[...] = x_ref[...]
            hb_ref[...] = x_ref[...].astype(BF16)

    return pl.pallas_call(
        body, name="build_h0", grid=(nb,),
        in_specs=[pl.BlockSpec((N_META, D), lambda i: (0, 0)),
                  pl.BlockSpec((BLK, D), lambda i: (jnp.maximum(i - 1, 0), 0))],
        out_specs=[pl.BlockSpec((BLK, D), lambda i: (i, 0))] * 2,
        out_shape=[jax.ShapeDtypeStruct((nb * BLK, D), F32), jax.ShapeDtypeStruct((nb * BLK, D), BF16)],
        compiler_params=_cp("arbitrary"),
    )(meta_full, x)


FT = 256


def _layer_norm(r, gamma, beta):
    mu = jnp.mean(r, axis=1, keepdims=True)
    xc = r - mu
    var = jnp.mean(xc * xc, axis=1, keepdims=True)
    return xc * lax.rsqrt(var + EPS) * gamma + beta


def ffn_fwd_seq(x, ln_in, wg, wu, wd, ln_out):
    t = x.shape[0]
    f = wg.shape[0]
    nj, nr = f // FT, t // _row_tile(t)
    rc = t // nr
    plain = ln_in is None
    gi, bi = ln_out if plain else ln_in

    def body(x_hbm, gi_ref, bi_ref, go_ref, bo_ref, wg_ref, wu_ref, wd_ref, u_ref, v_ref, r_hbm, yb_hbm,
             acc, hbs, xbuf, sem_in, sem_out):
        j = pl.program_id(0)

        @pl.when(j == 0)
        def _():
            def fetch(k):
                return pltpu.make_async_copy(x_hbm.at[pl.ds(k * rc, rc)], xbuf.at[k % 2], sem_in.at[k % 2])

            fetch(0).start()
            for k in range(nr):
                if k + 1 < nr:
                    fetch(k + 1).start()
                fetch(k).wait()
                h = xbuf[k % 2]
                if not plain:
                    h = _layer_norm(h, gi_ref[...], bi_ref[...])
                acc[k * rc:(k + 1) * rc, :] = ALPHA * h
                hbs[k * rc:(k + 1) * rc, :] = h.astype(BF16)

        for k in range(nr):
            sl = slice(k * rc, (k + 1) * rc)
            h = hbs[sl, :]
            u = _nt(h, wg_ref[...])
            v = _nt(h, wu_ref[...])
            u_ref[sl, :] = u.astype(BF16)
            v_ref[sl, :] = v.astype(BF16)
            acc[sl, :] += _nn((0.5 * u * _sig(u) * v).astype(BF16), wd_ref[...])

        @pl.when(j == nj - 1)
        def _():
            r_cp = pltpu.make_async_copy(acc, r_hbm, sem_out.at[0])
            r_cp.start()
            for k in range(nr):
                sl = slice(k * rc, (k + 1) * rc)
                hbs[sl, :] = _layer_norm(acc[sl, :], go_ref[...], bo_ref[...]).astype(BF16)
            y_cp = pltpu.make_async_copy(hbs, yb_hbm, sem_out.at[1])
            y_cp.start()
            r_cp.wait()
            y_cp.wait()

    vec = pl.BlockSpec((1, D), lambda j: (0, 0))
    wsp = pl.BlockSpec((FT, D), lambda j: (j, 0))
    act = pl.BlockSpec((None, t, FT), lambda j: (j, 0, 0))
    return pl.pallas_call(
        body, name="ffn_fwd_seq", grid=(nj,),
        in_specs=[_ANY, vec, vec, vec, vec, wsp, wsp, wsp],
        out_specs=[act, act, _ANY, _ANY],
        out_shape=[jax.ShapeDtypeStruct((nj, t, FT), BF16), jax.ShapeDtypeStruct((nj, t, FT), BF16),
                   jax.ShapeDtypeStruct((t, D), F32), jax.ShapeDtypeStruct((t, D), BF16)],
        scratch_shapes=[pltpu.VMEM((t, D), F32), pltpu.VMEM((t, D), BF16), pltpu.VMEM((2, rc, D), F32),
                        pltpu.SemaphoreType.DMA((2,)), pltpu.SemaphoreType.DMA((2,))],
        compiler_params=_cp("arbitrary"),
    )(x, gi, bi, ln_out[0], ln_out[1], wg, wu, wd)


def ffn_bwd_seq(parts, r, gamma, hb, u, v, wg, wu, wd, after=None):
    nj, t, _ = u.shape
    f = nj * FT
    nr = t // _row_tile(t)
    rc = t // nr
    nc = t // BLK
    scales = [s for _, s in parts]
    npart = len(parts)
    extra = [] if after is None else [after]

    def body(*refs):
        refs = refs[len(extra):]
        p_hbm, refs = refs[:npart], refs[npart:]
        (r_hbm, g_ref, hb_hbm, u_ref, v_ref, wg_ref, wu_ref, wd_ref, dh_hbm, dwg_ref, dwu_ref, dwd_ref, dg_ref, db_ref,
         dfs, hbt, dft, dhacc, dus, dvs, acs, pbuf, rbuf, hbuf, sems, sem_out) = refs
        j = pl.program_id(0)

        @pl.when(j == 0)
        def _():
            def fetch(c):
                rows = pl.ds(c * BLK, BLK)
                cps = [pltpu.make_async_copy(p_hbm[p].at[rows], pbuf.at[c % 2, p], sems.at[c % 2, p]) for p in range(npart)]
                cps.append(pltpu.make_async_copy(r_hbm.at[rows], rbuf.at[c % 2], sems.at[c % 2, npart]))
                cps.append(pltpu.make_async_copy(hb_hbm.at[rows], hbuf.at[c % 2], sems.at[c % 2, npart + 1]))
                return cps

            for cp in fetch(0):
                cp.start()
            dg = jnp.zeros((1, D), F32)
            db = jnp.zeros((1, D), F32)
            for c in range(nc):
                if c + 1 < nc:
                    for cp in fetch(c + 1):
                        cp.start()
                for cp in fetch(c):
                    cp.wait()
                sl = slice(c * BLK, (c + 1) * BLK)
                dy = scales[0] * pbuf[c % 2, 0]
                for p in range(1, npart):
                    dy += scales[p] * pbuf[c % 2, p]
                rr = rbuf[c % 2]
                xc = rr - jnp.mean(rr, axis=1, keepdims=True)
                rstd = lax.rsqrt(jnp.mean(xc * xc, axis=1, keepdims=True) + EPS)
                xh = xc * rstd
                dxh = dy * g_ref[...]
                dr = rstd * (dxh - jnp.mean(dxh, axis=1, keepdims=True) - xh * jnp.mean(dxh * xh, axis=1, keepdims=True))
                dg += jnp.sum(dy * xh, axis=0, keepdims=True)
                db += jnp.sum(dy, axis=0, keepdims=True)
                dhacc[sl, :] = ALPHA * dr
                dfc = (0.5 * dr).astype(BF16)
                dfs[sl, :] = dfc
                dft[:, sl] = dfc.T
                hbt[:, sl] = hbuf[c % 2].T
            dg_ref[...] = dg
            db_ref[...] = db

        for k in range(nr):
            sl = slice(k * rc, (k + 1) * rc)
            da = _nt(dfs[sl, :], wd_ref[...])
            uu = u_ref[sl, :].astype(F32)
            vv = v_ref[sl, :].astype(F32)
            sg = _sig(uu)
            du = (da * vv * (sg * (1.0 + uu * (1.0 - sg)))).astype(BF16)
            dv = (da * uu * sg).astype(BF16)
            dus[sl, :] = du
            dvs[sl, :] = dv
            acs[sl, :] = (uu * sg * vv).astype(BF16)
            dhacc[sl, :] += _nn(du, wg_ref[...]) + _nn(dv, wu_ref[...])
        @pl.when(j == nj - 1)
        def _():
            pltpu.make_async_copy(dhacc, dh_hbm, sem_out.at[0]).start()

        dwg_ref[...] = _nn(hbt[...], dus[...]).astype(BF16).T
        dwu_ref[...] = _nn(hbt[...], dvs[...]).astype(BF16).T
        dwd_ref[...] = _nn(dft[...], acs[...]).astype(BF16).T

        @pl.when(j == nj - 1)
        def _():
            pltpu.make_async_copy(dhacc, dh_hbm, sem_out.at[0]).wait()

    vec = pl.BlockSpec((1, D), lambda j: (0, 0))
    wsp = pl.BlockSpec((FT, D), lambda j: (j, 0))
    act = pl.BlockSpec((None, t, FT), lambda j: (j, 0, 0))
    return pl.pallas_call(
        body, name="ffn_bwd_seq", grid=(nj,),
        in_specs=[_ANY] * (len(extra) + npart + 1) + [vec, _ANY, act, act, wsp, wsp, wsp],
        out_specs=[_ANY, wsp, wsp, wsp, vec, vec],
        out_shape=[jax.ShapeDtypeStruct((t, D), F32)] + [jax.ShapeDtypeStruct((f, D), BF16)] * 3
        + [jax.ShapeDtypeStruct((1, D), F32)] * 2,
        scratch_shapes=[pltpu.VMEM((t, D), BF16), pltpu.VMEM((D, t), BF16), pltpu.VMEM((D, t), BF16),
                        pltpu.VMEM((t, D), F32), pltpu.VMEM((t, FT), BF16), pltpu.VMEM((t, FT), BF16),
                        pltpu.VMEM((t, FT), BF16), pltpu.VMEM((2, npart, BLK, D), F32), pltpu.VMEM((2, BLK, D), F32),
                        pltpu.VMEM((2, BLK, D), BF16), pltpu.SemaphoreType.DMA((2, npart + 2)),
                        pltpu.SemaphoreType.DMA((1,))],
        compiler_params=_cp("arbitrary"),
    )(*extra, *[p for p, _ in parts], r, gamma, hb, u, v, wg, wu, wd)


def mm_res_ln(a, b, x, ln_in, ln_out):
    t, k = a.shape
    tm = _row_tile(t)

    def body(a_ref, b_ref, x_ref, gi_ref, bi_ref, go_ref, bo_ref, r_ref, yb_ref):
        r = ALPHA * _layer_norm(x_ref[...], gi_ref[...], bi_ref[...]) + _nn(a_ref[...], b_ref[...])
        r_ref[...] = r
        yb_ref[...] = _layer_norm(r, go_ref[...], bo_ref[...]).astype(BF16)

    row = pl.BlockSpec((tm, D), lambda i: (i, 0))
    vec = pl.BlockSpec((1, D), lambda i: (0, 0))
    return pl.pallas_call(
        body, name="mm_res_ln", grid=(t // tm,),
        in_specs=[pl.BlockSpec((tm, k), lambda i: (i, 0)), pl.BlockSpec((k, D), lambda i: (0, 0)), row, vec, vec, vec, vec],
        out_specs=[row, row],
        out_shape=[jax.ShapeDtypeStruct((t, D), F32), jax.ShapeDtypeStruct((t, D), BF16)],
        compiler_params=_cp("arbitrary"),
    )(a, b, x, ln_in[0], ln_in[1], ln_out[0], ln_out[1])


def mm_nn(a, b):
    t, k = a.shape
    n = tn = b.shape[1]
    tm = _row_tile(t)

    def body(a_ref, b_ref, o_ref):
        o_ref[...] = _nn(a_ref[...], b_ref[...])

    return pl.pallas_call(
        body, name="mm_nn", grid=(t // tm, n // tn),
        in_specs=[pl.BlockSpec((tm, k), lambda i, j: (i, 0)), pl.BlockSpec((k, tn), lambda i, j: (0, j))],
        out_specs=pl.BlockSpec((tm, tn), lambda i, j: (i, j)),
        out_shape=jax.ShapeDtypeStruct((t, n), F32),
        compiler_params=_cp("arbitrary", "arbitrary"),
    )(a, b)


def oproj_bwd(dy, r, gamma, mixcat, w_out, after=None):
    t = r.shape[0]
    tm = _row_tile(t)
    nt = t // tm
    extra = [] if after is None else [after]

    def body(*refs):
        dy_ref, r_ref, g_ref, m_ref, w_ref, dr_ref, dm_ref, dw_ref, dg_ref, db_ref, acc = refs[len(extra):]
        i = pl.program_id(0)
        dy = dy_ref[...]
        rr = r_ref[...]
        xc = rr - jnp.mean(rr, axis=1, keepdims=True)
        rstd = lax.rsqrt(jnp.mean(xc * xc, axis=1, keepdims=True) + EPS)
        xh = xc * rstd
        dxh = dy * g_ref[...]
        dr = rstd * (dxh - jnp.mean(dxh, axis=1, keepdims=True) - xh * jnp.mean(dxh * xh, axis=1, keepdims=True))
        dr_ref[...] = dr
        drb = dr.astype(BF16)
        dm_ref[...] = _nt(drb, w_ref[...])
        dw = _tn(m_ref[...], drb)
        dg = jnp.sum(dy * xh, axis=0, keepdims=True)
        db = jnp.sum(dy, axis=0, keepdims=True)

        @pl.when(i == 0)
        def _():
            acc[...] = dw
            dg_ref[...] = dg
            db_ref[...] = db

        @pl.when(i > 0)
        def _():
            acc[...] += dw
            dg_ref[...] += dg
            db_ref[...] += db

        @pl.when(i == nt - 1)
        def _():
            dw_ref[...] = acc[...].astype(BF16)

    row = pl.BlockSpec((tm, D), lambda i: (i, 0))
    vec = pl.BlockSpec((1, D), lambda i: (0, 0))
    mat = pl.BlockSpec((D, D), lambda i: (0, 0))
    return pl.pallas_call(
        body, name="oproj_bwd", grid=(nt,),
        in_specs=[_ANY] * len(extra) + [row, row, vec, row, mat],
        out_specs=[row, row, mat, vec, vec],
        out_shape=[jax.ShapeDtypeStruct((t, D), F32), jax.ShapeDtypeStruct((t, D), F32), jax.ShapeDtypeStruct((D, D), BF16),
                   jax.ShapeDtypeStruct((1, D), F32), jax.ShapeDtypeStruct((1, D), F32)],
        scratch_shapes=[pltpu.VMEM((D, D), F32)],
        compiler_params=_cp("arbitrary"),
    )(*extra, dy, r, gamma, mixcat, w_out)


def proj_bwd(dproj, hb, w_in):
    t, n = dproj.shape
    tm = _row_tile(t)
    nt = t // tm

    def body(dp_ref, h_ref, w_ref, dh_ref, dw_ref, acc):
        i = pl.program_id(0)
        dp = dp_ref[...]
        dh_ref[...] = _nt(dp, w_ref[...])
        dw = _tn(h_ref[...], dp)

        @pl.when(i == 0)
        def _():
            acc[...] = dw

        @pl.when(i > 0)
        def _():
            acc[...] += dw

        @pl.when(i == nt - 1)
        def _():
            dw_ref[...] = acc[...].astype(BF16)

    mat = pl.BlockSpec((D, n), lambda i: (0, 0))
    return pl.pallas_call(
        body, name="proj_bwd", grid=(nt,),
        in_specs=[pl.BlockSpec((tm, n), lambda i: (i, 0)), pl.BlockSpec((tm, D), lambda i: (i, 0)), mat],
        out_specs=[pl.BlockSpec((tm, D), lambda i: (i, 0)), mat],
        out_shape=[jax.ShapeDtypeStruct((t, D), F32), jax.ShapeDtypeStruct((D, n), BF16)],
        scratch_shapes=[pltpu.VMEM((D, n), F32)],
        compiler_params=_cp("arbitrary"),
    )(dproj, hb, w_in)


def loss_head(r, ln, target):
    t = r.shape[0]
    nb = t // BLK

    def body(r_ref, g_ref, b_ref, t_ref, dy_ref, l_ref):
        i = pl.program_id(0)

        @pl.when(i == 0)
        def _():
            dy_ref[...] = jnp.zeros_like(dy_ref)
            l_ref[...] = jnp.zeros_like(l_ref)

        @pl.when(i > 0)
        def _():
            err = _layer_norm(r_ref[...], g_ref[...], b_ref[...]) - t_ref[...]
            dy_ref[...] = err * (1.0 / D)
            l_ref[...] += (0.5 / D) * jnp.sum(err * err, keepdims=True)

    vec = pl.BlockSpec((1, D), lambda i: (0, 0))
    return pl.pallas_call(
        body, name="loss_head", grid=(nb,),
        in_specs=[pl.BlockSpec((BLK, D), lambda i: (i, 0)), vec, vec,
                  pl.BlockSpec((BLK, D), lambda i: (jnp.maximum(i - 1, 0), 0))],
        out_specs=[pl.BlockSpec((BLK, D), lambda i: (i, 0)), pl.BlockSpec((1, 1), lambda i: (0, 0))],
        out_shape=[jax.ShapeDtypeStruct((t, D), F32), jax.ShapeDtypeStruct((1, 1), F32)],
        compiler_params=_cp("arbitrary"),
    )(r, ln[0], ln[1], target)


def split_dh0(dh0, after=None):
    t = dh0.shape[0]
    nb = t // BLK
    extra = [] if after is None else [after]

    def body(*refs):
        a_ref, gx_ref, gm_ref = refs[len(extra):]
        i = pl.program_id(0)
        tot = a_ref[...]

        @pl.when(i == 0)
        def _():
            gm_ref[...] = tot[PAD:, :]

        @pl.when(i > 0)
        def _():
            gx_ref[...] = tot

    blk = pl.BlockSpec((BLK, D), lambda i: (i, 0))
    return pl.pallas_call(
        body, name="split_dh0", grid=(nb,),
        in_specs=[_ANY] * len(extra) + [blk],
        out_specs=[pl.BlockSpec((BLK, D), lambda i: (jnp.maximum(i - 1, 0), 0)),
                   pl.BlockSpec((N_META, D), lambda i: (0, 0))],
        out_shape=[jax.ShapeDtypeStruct((t - BLK, D), F32), jax.ShapeDtypeStruct((N_META, D), F32)],
        compiler_params=_cp("arbitrary"),
    )(*extra, dh0)


def _valid_rows(nrows, first_row):
    return (first_row + lax.broadcasted_iota(jnp.int32, (nrows, 1), 0)) >= PAD


def conv_fwd(proj, conv_w, conv_b):
    t = proj.shape[0]
    c0 = C_XBC // BLK

    def body(x_ref, w_ref, b_ref, o_ref):
        ok = _valid_rows(t, 0)
        x = jnp.where(ok, x_ref[...], 0.0)
        w = w_ref[...]
        acc = b_ref[...] + w[CONV_K - 1:CONV_K, :] * x
        for s in range(1, CONV_K):
            acc += w[CONV_K - 1 - s:CONV_K - s, :] * pltpu.roll(x, s, 0)
        o_ref[...] = jnp.where(ok, acc * _sig(acc), 0.0)

    return pl.pallas_call(
        body, name="conv_fwd", grid=(CONV_D // BLK,),
        in_specs=[pl.BlockSpec((t, BLK), lambda j: (0, c0 + j)),
                  pl.BlockSpec((CONV_K, BLK), lambda j: (0, j)), pl.BlockSpec((1, BLK), lambda j: (0, j))],
        out_specs=pl.BlockSpec((t, BLK), lambda j: (0, j)),
        out_shape=jax.ShapeDtypeStruct((t, CONV_D), F32),
        compiler_params=_cp("arbitrary"),
    )(proj, conv_w, conv_b)


def conv_bwd(dxa, proj, conv_w, conv_b):
    t = proj.shape[0]
    c0 = C_XBC // BLK

    def body(d_ref, x_ref, w_ref, b_ref, dx_ref, dw_ref, db_ref):
        ok = _valid_rows(t, 0)
        x = jnp.where(ok, x_ref[...], 0.0)
        w = w_ref[...]
        xs = [x] + [pltpu.roll(x, s, 0) for s in range(1, CONV_K)]
        acc = b_ref[...] + w[CONV_K - 1:CONV_K, :] * x
        for s in range(1, CONV_K):
            acc += w[CONV_K - 1 - s:CONV_K - s, :] * xs[s]
        sg = _sig(acc)
        dxc = jnp.where(ok, d_ref[...] * (sg * (1.0 + acc * (1.0 - sg))), 0.0)
        db_ref[...] = jnp.sum(dxc, axis=0, keepdims=True)
        dw_ref[...] = jnp.concatenate(
            [jnp.sum(dxc * xs[CONV_K - 1 - k], axis=0, keepdims=True) for k in range(CONV_K)], axis=0)
        dx = w[CONV_K - 1:CONV_K, :] * dxc
        for s in range(1, CONV_K):
            dx += w[CONV_K - 1 - s:CONV_K - s, :] * pltpu.roll(dxc, t - s, 0)
        dx_ref[...] = jnp.where(ok, dx, 0.0)

    col = pl.BlockSpec((t, BLK), lambda j: (0, j))
    return pl.pallas_call(
        body, name="conv_bwd", grid=(CONV_D // BLK,),
        in_specs=[col, pl.BlockSpec((t, BLK), lambda j: (0, c0 + j)),
                  pl.BlockSpec((CONV_K, BLK), lambda j: (0, j)), pl.BlockSpec((1, BLK), lambda j: (0, j))],
        out_specs=[col, pl.BlockSpec((CONV_K, BLK), lambda j: (0, j)), pl.BlockSpec((1, BLK), lambda j: (0, j))],
        out_shape=[jax.ShapeDtypeStruct((t, CONV_D), F32), jax.ShapeDtypeStruct((CONV_K, CONV_D), F32),
                   jax.ShapeDtypeStruct((1, CONV_D), F32)],
        compiler_params=_cp("arbitrary"),
    )(dxa, proj, conv_w, conv_b)


def _softplus(x):
    return jnp.maximum(x, 0.0) + jnp.log(1.0 + jnp.exp(-jnp.abs(x)))


GW = SSD_D // SSD_G
HPG = SSD_H // SSD_G


def _head_expand():
    r = lax.broadcasted_iota(jnp.int32, (BLK, SSD_D), 0)
    c = lax.broadcasted_iota(jnp.int32, (BLK, SSD_D), 1)
    rt = lax.broadcasted_iota(jnp.int32, (SSD_D, BLK), 0)
    ct = lax.broadcasted_iota(jnp.int32, (SSD_D, BLK), 1)
    return (c // SSD_P == r).astype(F32), (rt // SSD_P == ct).astype(F32)


def _ssd_chunk(xa, sm, dtb, alog, dskip, ok, sp):
    e, et = _head_expand()
    dt = jnp.where(ok, _softplus(sm + dtb), 0.0)
    amat = -jnp.exp(alog)
    tri = _tri()
    ac = _nn_hi(tri.astype(F32), dt * amat)
    act = ac.T
    ace, dte, dse = _nn_hi(ac, e), _nn_hi(dt, e), _nn_hi(dskip, e)
    laste = ace[BLK - 1:BLK, :]
    ee, dece, gle = jnp.exp(ace), jnp.exp(laste - ace), jnp.exp(laste)
    xs = xa[:, :SSD_D]
    xdt = xs * dte
    decx = dece * xdt
    xdtb = xdt.astype(BF16)
    d = dict(e=e, et=et, dt=dt, amat=amat, tri=tri, ac=ac, act=act, dte=dte, dse=dse, ee=ee, dece=dece, gle=gle, xs=xs,
             xdt=xdt, xdtb=xdtb, decx=decx, bg=[], cg=[], cb=[], yo=[], seg=[], m=[], new_s=[])
    ys = []
    for g in range(SSD_G):
        cols = slice(GW * g, GW * (g + 1))
        bg = xa[:, SSD_D + SSD_N * g:SSD_D + SSD_N * (g + 1)].astype(BF16)
        cg = xa[:, SSD_D + SSD_G * SSD_N + SSD_N * g:SSD_D + SSD_G * SSD_N + SSD_N * (g + 1)].astype(BF16)
        spg = sp[:, cols]
        sloc = _tn(bg, decx[:, cols].astype(BF16))
        yo = _nn(cg, spg.astype(BF16)) * ee[:, cols]
        cb = _nt(cg, bg)
        d["new_s"].append(gle[:, cols] * spg + sloc)
        yds = []
        for h in range(HPG * g, HPG * (g + 1)):
            seg = jnp.where(tri, jnp.exp(jnp.minimum(ac[:, h:h + 1] - act[h:h + 1, :], 0.0)), 0.0)
            m = cb * seg
            yds.append(_nn(m.astype(BF16), xdtb[:, SSD_P * h:SSD_P * (h + 1)]))
            d["seg"].append(seg)
            d["m"].append(m)
        ys.append(jnp.concatenate(yds, axis=1) + yo)
        for k, val in (("bg", bg), ("cg", cg), ("cb", cb), ("yo", yo)):
            d[k].append(val)
    d["y"] = jnp.concatenate(ys, axis=1) + dse * xs
    return d


def ssd_fwd(xa, proj, dtb, alog, dskip, normg):
    t = xa.shape[0]
    nb = t // BLK
    gw = SSD_D // SSD_G

    def body(xa_ref, z_ref, sm_ref, dtb_ref, al_ref, ds_ref, ng_ref, y_ref, sp_ref, st):
        c = pl.program_id(0)

        @pl.when(c == 0)
        def _():
            st[...] = jnp.zeros_like(st)

        ok = _valid_rows(BLK, c * BLK)
        sp = st[...]
        sp_ref[...] = sp
        d = _ssd_chunk(xa_ref[...], sm_ref[...], dtb_ref[...], al_ref[...], ds_ref[...], ok, sp)
        st[...] = jnp.concatenate(d["new_s"], axis=1)
        y = d["y"]
        z = z_ref[...]
        yg = y * (z * _sig(z))
        outs = []
        for g in range(SSD_G):
            v = yg[:, gw * g:gw * (g + 1)]
            outs.append(v * lax.rsqrt(jnp.mean(v * v, axis=1, keepdims=True) + EPS))
        y_ref[...] = (jnp.concatenate(outs, axis=1) * ng_ref[...]).astype(BF16)

    vec = pl.BlockSpec((1, BLK), lambda c: (0, 0))
    return pl.pallas_call(
        body, name="ssd_fwd", grid=(nb,),
        in_specs=[pl.BlockSpec((BLK, CONV_D), lambda c: (c, 0)),
                  pl.BlockSpec((BLK, SSD_D), lambda c: (c, C_Z // SSD_D)),
                  pl.BlockSpec((BLK, BLK), lambda c: (c, C_SM // BLK)),
                  vec, vec, vec, pl.BlockSpec((1, SSD_D), lambda c: (0, 0))],
        out_specs=[pl.BlockSpec((BLK, SSD_D), lambda c: (c, 0)),
                   pl.BlockSpec((None, SSD_N, SSD_D), lambda c: (c, 0, 0))],
        out_shape=[jax.ShapeDtypeStruct((t, SSD_D), BF16), jax.ShapeDtypeStruct((nb, SSD_N, SSD_D), F32)],
        scratch_shapes=[pltpu.VMEM((SSD_N, SSD_D), F32)],
        compiler_params=_cp("arbitrary"),
    )(xa, proj, proj, dtb, alog, dskip, normg)


def _lane_put(col, lane):
    li = lax.broadcasted_iota(jnp.int32, (col.shape[0], BLK), 1)
    return jnp.where(li == lane, col, 0.0)


def ssd_bwd(dmix, xa, proj, sprev, dtb, alog, dskip, normg):
    t = xa.shape[0]
    nb = t // BLK
    gw = SSD_D // SSD_G
    rev = lambda c: nb - 1 - c

    def body(dy_ref, xa_ref, z_ref, sm_ref, sp_ref, dtb_ref, al_ref, ds_ref, ng_ref,
             dxa_ref, dz_ref, dsm_ref, dng_ref, dds_ref, dal_ref, ddtb_ref, dst):
        c = pl.program_id(0)

        @pl.when(c == 0)
        def _():
            dst[...] = jnp.zeros_like(dst)
            dng_ref[...] = jnp.zeros_like(dng_ref)
            dds_ref[...] = jnp.zeros_like(dds_ref)
            dal_ref[...] = jnp.zeros_like(dal_ref)
            ddtb_ref[...] = jnp.zeros_like(ddtb_ref)

        ok = _valid_rows(BLK, rev(c) * BLK)
        sm = sm_ref[...]
        sp = sp_ref[...]
        d = _ssd_chunk(xa_ref[...], sm, dtb_ref[...], al_ref[...], ds_ref[...], ok, sp)
        dt, amat, ac, act, tri, et, xs, xdt = (d[k] for k in ("dt", "amat", "ac", "act", "tri", "et", "xs", "xdt"))
        rowi = lax.broadcasted_iota(jnp.int32, (BLK, 1), 0)
        y = d["y"]
        z = z_ref[...]
        sgz = _sig(z)
        siluz = z * sgz
        yg = y * siluz
        dout = dy_ref[...]
        ng = ng_ref[...]
        dygs, xhs = [], []
        for g in range(SSD_G):
            v = yg[:, gw * g:gw * (g + 1)]
            rr = lax.rsqrt(jnp.mean(v * v, axis=1, keepdims=True) + EPS)
            xh = v * rr
            dxh = dout[:, gw * g:gw * (g + 1)] * ng[:, gw * g:gw * (g + 1)]
            dygs.append(rr * (dxh - xh * jnp.mean(dxh * xh, axis=1, keepdims=True)))
            xhs.append(xh)
        dyg = jnp.concatenate(dygs, axis=1)
        dng_ref[...] += jnp.sum(dout * jnp.concatenate(xhs, axis=1), axis=0, keepdims=True)
        dy = dyg * siluz
        dz_ref[...] = dyg * y * (sgz * (1.0 + z * (1.0 - sgz)))

        triu = _tri(lower=False)
        dyb = dy.astype(BF16)
        dsn = dst[...]
        dds_ref[...] += _nn_hi(jnp.sum(dy * xs, axis=0, keepdims=True), et)
        dac_all = _nn_hi(dy * jnp.concatenate(d["yo"], axis=1), et)
        dyo = (dy * d["ee"]).astype(BF16)
        gl = jnp.exp(ac[BLK - 1:BLK, :])
        dlast = _nn_hi(jnp.sum(dsn * sp, axis=0, keepdims=True), et) * gl
        bds, db_g, dc_g, dxdt_i, new_dst = [], [], [], [], []
        for g in range(SSD_G):
            cols = slice(GW * g, GW * (g + 1))
            bg, cg = d["bg"][g], d["cg"][g]
            dsng = dsn[:, cols].astype(BF16)
            dc = _nt(dyo[:, cols], sp[:, cols].astype(BF16))
            new_dst.append(_tn(cg, dyo[:, cols]) + d["gle"][:, cols] * dsn[:, cols])
            bds.append(_nn(bg, dsng))
            db = _nt(d["decx"][:, cols].astype(BF16), dsng)
            cbt = _nt(bg, cg)
            dcb = jnp.zeros((BLK, BLK), F32)
            for h in range(HPG * g, HPG * (g + 1)):
                hc = slice(SSD_P * h, SSD_P * (h + 1))
                dm = _nt(dyb[:, hc], d["xdtb"][:, hc])
                dcb += dm * d["seg"][h]
                w = dm * d["m"][h]
                dac_all += _lane_put(jnp.sum(w, axis=1, keepdims=True) - jnp.sum(w.T, axis=1, keepdims=True), h)
                segt = jnp.where(triu, jnp.exp(jnp.minimum(act[h:h + 1, :] - ac[:, h:h + 1], 0.0)), 0.0)
                dxdt_i.append(_nn((cbt * segt).astype(BF16), dyb[:, hc]))
            dcbb = dcb.astype(BF16)
            dc_g.append(dc + _nn(dcbb, bg))
            db_g.append(db + _tn(dcbb, cg))
        dst[...] = jnp.concatenate(new_dst, axis=1)
        bds = jnp.concatenate(bds, axis=1)
        tdec = jnp.exp(ac[BLK - 1:BLK, :] - ac) * _nn_hi(xdt * bds, et)
        dlast += jnp.sum(tdec, axis=0, keepdims=True)
        dac_all += jnp.where(rowi == BLK - 1, dlast, 0.0) - tdec
        dxdt = d["dece"] * bds + jnp.concatenate(dxdt_i, axis=1)
        da = _nn_hi(triu.astype(F32), dac_all)
        ddt = _nn_hi(dxdt * xs, et) + da * amat
        dal_ref[...] += jnp.sum(da * dt, axis=0, keepdims=True) * amat
        ddtr = jnp.where(ok, ddt * _sig(sm + dtb_ref[...]), 0.0)
        ddtb_ref[...] += jnp.sum(ddtr, axis=0, keepdims=True)
        dsm_ref[...] = ddtr
        dxs = d["dse"] * dy + dxdt * d["dte"]
        dxa_ref[...] = jnp.where(ok, jnp.concatenate([dxs] + db_g + dc_g, axis=1), 0.0)

    vec = pl.BlockSpec((1, BLK), lambda c: (0, 0))
    nvec = pl.BlockSpec((1, SSD_D), lambda c: (0, 0))
    return pl.pallas_call(
        body, name="ssd_bwd", grid=(nb,),
        in_specs=[pl.BlockSpec((BLK, SSD_D), lambda c: (rev(c), 0)),
                  pl.BlockSpec((BLK, CONV_D), lambda c: (rev(c), 0)),
                  pl.BlockSpec((BLK, SSD_D), lambda c: (rev(c), C_Z // SSD_D)),
                  pl.BlockSpec((BLK, BLK), lambda c: (rev(c), C_SM // BLK)),
                  pl.BlockSpec((None, SSD_N, SSD_D), lambda c: (rev(c), 0, 0)),
                  vec, vec, vec, nvec],
        out_specs=[pl.BlockSpec((BLK, CONV_D), lambda c: (rev(c), 0)),
                   pl.BlockSpec((BLK, SSD_D), lambda c: (rev(c), 0)),
                   pl.BlockSpec((BLK, BLK), lambda c: (rev(c), 0)),
                   nvec, vec, vec, vec],
        out_shape=[jax.ShapeDtypeStruct((t, CONV_D), F32), jax.ShapeDtypeStruct((t, SSD_D), F32),
                   jax.ShapeDtypeStruct((t, BLK), F32), jax.ShapeDtypeStruct((1, SSD_D), F32),
                   jax.ShapeDtypeStruct((1, BLK), F32), jax.ShapeDtypeStruct((1, BLK), F32),
                   jax.ShapeDtypeStruct((1, BLK), F32)],
        scratch_shapes=[pltpu.VMEM((SSD_N, SSD_D), F32)],
        compiler_params=_cp("arbitrary"),
    )(dmix, xa, proj, proj, sprev, dtb, alog, dskip, normg)


def _segments(nb, fine):
    if fine:
        cuts = list(range(0, nb, 2)) + [nb]
    else:
        cuts = sorted({0, nb} | {max(1, round(nb * f)) for f in (0.3, 0.53, 0.77)})
    return list(zip(cuts[:-1], cuts[1:]))


def attn_fwd(q, k, v, qcol, kcol, vcol, nh, dq, dv, scale, c_col=None, c_row=None, lane0=0):
    t = q.shape[0]
    tq = BLK
    use_bias = c_col is not None

    def body(*refs):
        if use_bias:
            q_ref, k_ref, v_ref, cc_ref, cr_ref, o_ref, l_ref = refs
        else:
            q_ref, k_ref, v_ref, o_ref, l_ref = refs
        i = pl.program_id(0)
        rowg = i * tq + lax.broadcasted_iota(jnp.int32, (tq, 1), 0)

        def tile(tk):
            col = lax.broadcasted_iota(jnp.int32, (1, tk), 1)
            mask = (col <= rowg) & (col >= PAD)
            outs = []
            lse = jnp.zeros((tq, BLK), F32)
            for h in range(nh):
                s = _nt(q_ref[:, dq * h:dq * (h + 1)].astype(BF16), k_ref[0:tk, dq * h:dq * (h + 1)].astype(BF16)) * scale
                if use_bias:
                    s = s + (cc_ref[:, lane0 + h:lane0 + h + 1] - cr_ref[h:h + 1, 0:tk])
                s = jnp.where(mask, s, NEG)
                m = jnp.max(s, axis=1, keepdims=True)
                p = jnp.exp(s - m)
                l = jnp.sum(p, axis=1, keepdims=True)
                outs.append(_nn(p.astype(BF16), v_ref[0:tk, dv * h:dv * (h + 1)].astype(BF16)) / l)
                lse += _lane_put(m + jnp.log(l), h)
            o_ref[...] = jnp.concatenate(outs, axis=1).astype(BF16)
            l_ref[...] = lse.T[0:8, :]

        for t0, t1 in _segments(t // tq, True):
            pl.when((i >= t0) & (i < t1))(functools.partial(tile, t1 * BLK))

    in_specs = [pl.BlockSpec((tq, nh * dq), lambda i: (i, qcol)),
                pl.BlockSpec((t, nh * dq), lambda i: (0, kcol)),
                pl.BlockSpec((t, nh * dv), lambda i: (0, vcol))]
    args = [q, k, v]
    if use_bias:
        in_specs += [pl.BlockSpec((tq, BLK), lambda i: (i, 0)), pl.BlockSpec((8, t), lambda i: (0, 0))]
        args += [c_col, c_row]
    return pl.pallas_call(
        body, name="attn_fwd", grid=(t // tq,),
        in_specs=in_specs,
        out_specs=[pl.BlockSpec((tq, nh * dv), lambda i: (i, 0)), pl.BlockSpec((8, tq), lambda i: (0, i))],
        out_shape=[jax.ShapeDtypeStruct((t, nh * dv), BF16), jax.ShapeDtypeStruct((8, t), F32)],
        compiler_params=_cp("arbitrary"),
    )(*args)


def attn_bwd(q, k, v, do, lse_row, o, qcol, kcol, vcol, docol, ocol, nh, dq, dv, scale, c_col=None, c_row=None, lane0=0):
    t = q.shape[0]
    tq = BLK
    use_bias = c_col is not None
    nq = t // tq

    def body(*refs):
        if use_bias:
            (q_ref, k_ref, v_ref, do_ref, l_ref, o_ref, cc_ref, cr_ref, dq_ref, dk_ref, dv_ref, dcq_ref, dck_ref,
             kt, ckb, dacc) = refs
        else:
            q_ref, k_ref, v_ref, do_ref, l_ref, o_ref, dq_ref, dk_ref, dv_ref, kt = refs
        i = pl.program_id(0)

        @pl.when(i == 0)
        def _():
            kt[...] = k_ref[...].astype(BF16).T
            dk_ref[...] = jnp.zeros_like(dk_ref)
            dv_ref[...] = jnp.zeros_like(dv_ref)
            if use_bias:
                dacc[...] = jnp.zeros_like(dacc)
                for h in range(nh):
                    ckb[h] = jnp.broadcast_to(cc_ref[:, lane0 + h:lane0 + h + 1], (t, BLK))

        qry = i * tq + lax.broadcasted_iota(jnp.int32, (1, tq), 1)
        dot = (do_ref[...].astype(F32) * o_ref[...].astype(F32)).T

        def tile(tk):
            key = lax.broadcasted_iota(jnp.int32, (tk, 1), 0)
            mask = (key <= qry) & (key >= PAD)
            dqts, dcqs = [], []
            for h in range(nh):
                qh = q_ref[:, dq * h:dq * (h + 1)].astype(BF16)
                kh = k_ref[0:tk, dq * h:dq * (h + 1)].astype(BF16)
                vh = v_ref[0:tk, dv * h:dv * (h + 1)].astype(BF16)
                doh = do_ref[:, dv * h:dv * (h + 1)].astype(BF16)
                delta = jnp.sum(dot[dv * h:dv * (h + 1), :], axis=0, keepdims=True)
                st = _nt(kh, qh) * scale
                if use_bias:
                    st = st + (cr_ref[h:h + 1, :] - ckb[h, 0:tk, :])
                pt = jnp.exp(jnp.where(mask, st, NEG) - l_ref[h:h + 1, :])
                dst = pt * (_nt(vh, doh) - delta)
                dsb = dst.astype(BF16)
                dk_ref[0:tk, dq * h:dq * (h + 1)] += _nn(dsb, qh) * scale
                dv_ref[0:tk, dv * h:dv * (h + 1)] += _nn(pt.astype(BF16), doh)
                dqts.append(_nn(kt[dq * h:dq * (h + 1), 0:tk], dsb))
                if use_bias:
                    dcqs.append(jnp.sum(dst, axis=0, keepdims=True))
                    dacc[h, 0:tk, :] += dst
            dq_ref[...] = jnp.concatenate(dqts, axis=0).T * scale
            if use_bias:
                dcq_ref[...] = jnp.concatenate(dcqs + [jnp.zeros((8 - nh, tq), F32)], axis=0)

        for t0, t1 in _segments(nq, not use_bias):
            pl.when((i >= t0) & (i < t1))(functools.partial(tile, t1 * BLK))

        if use_bias:
            @pl.when(i == nq - 1)
            def _():
                lane = lax.broadcasted_iota(jnp.int32, (1, BLK), 1)
                tot = jnp.zeros((t, BLK), F32)
                for h in range(nh):
                    tot += jnp.where(lane == lane0 + h, jnp.sum(dacc[h], axis=1, keepdims=True), 0.0)
                dck_ref[...] = tot

    keys_q = pl.BlockSpec((t, nh * dq), lambda i: (0, 0))
    keys_v = pl.BlockSpec((t, nh * dv), lambda i: (0, 0))
    keys_c = pl.BlockSpec((t, BLK), lambda i: (0, 0))
    qrow = pl.BlockSpec((8, tq), lambda i: (0, i))
    in_specs = [pl.BlockSpec((tq, nh * dq), lambda i: (i, qcol)),
                pl.BlockSpec((t, nh * dq), lambda i: (0, kcol)),
                pl.BlockSpec((t, nh * dv), lambda i: (0, vcol)),
                pl.BlockSpec((tq, nh * dv), lambda i: (i, docol)),
                qrow,
                pl.BlockSpec((tq, nh * dv), lambda i: (i, ocol))]
    args = [q, k, v, do, lse_row, o]
    out_specs = [pl.BlockSpec((tq, nh * dq), lambda i: (i, 0)), keys_q, keys_v]
    out_shape = [jax.ShapeDtypeStruct((t, nh * dq), F32), jax.ShapeDtypeStruct((t, nh * dq), F32),
                 jax.ShapeDtypeStruct((t, nh * dv), F32)]
    scratch = [pltpu.VMEM((nh * dq, t), BF16)]
    if use_bias:
        in_specs += [keys_c, qrow]
        args += [c_col, c_row]
        out_specs += [qrow, keys_c]
        out_shape += [jax.ShapeDtypeStruct((8, t), F32), jax.ShapeDtypeStruct((t, BLK), F32)]
        scratch += [pltpu.VMEM((nh, t, BLK), F32), pltpu.VMEM((nh, t, BLK), F32)]
    return pl.pallas_call(
        body, name="attn_bwd", grid=(nq,),
        in_specs=in_specs, out_specs=out_specs, out_shape=out_shape, scratch_shapes=scratch,
        compiler_params=_cp("arbitrary"),
    )(*args)


def fox_pre(proj, fb):
    t = proj.shape[0]
    nb = t // BLK

    def body(sm_ref, fb_ref, c_ref, cr_ref):
        x = sm_ref[...] + fb_ref[...]
        lane = lax.broadcasted_iota(jnp.int32, (1, BLK), 1)
        keep = _valid_rows(t, 0) & (lane >= SM_F) & (lane < SM_F + FOX_H)
        logf = jnp.where(keep, jnp.minimum(x, 0.0) - jnp.log(1.0 + jnp.exp(-jnp.abs(x))), 0.0)
        tri = _tri().astype(F32)
        carry = jnp.zeros((1, BLK), F32)
        for b in range(nb):
            cb = _nn_hi(tri, logf[b * BLK:(b + 1) * BLK, :]) + carry
            c_ref[b * BLK:(b + 1) * BLK, :] = cb
            carry = cb[BLK - 1:BLK, :]
        cr_ref[...] = c_ref[...].T[SM_F:SM_F + 8, :]

    return pl.pallas_call(
        body, name="fox_pre", grid=(1,),
        in_specs=[pl.BlockSpec((t, BLK), lambda i: (0, C_SM // BLK)), pl.BlockSpec((1, BLK), lambda i: (0, 0))],
        out_specs=[pl.BlockSpec((t, BLK), lambda i: (0, 0)), pl.BlockSpec((8, t), lambda i: (0, 0))],
        out_shape=[jax.ShapeDtypeStruct((t, BLK), F32), jax.ShapeDtypeStruct((8, t), F32)],
        compiler_params=_cp("arbitrary"),
    )(proj, fb)


def fox_pre_bwd(dcq, dck, proj, fb, dsm_in):
    t = proj.shape[0]
    nb = t // BLK

    def body(dcq_ref, dck_ref, sm_ref, fb_ref, din_ref, dsm_ref, dfb_ref, scr):
        triu = _tri(lower=False).astype(F32)
        carry = jnp.zeros((1, BLK), F32)
        scr[...] = jnp.concatenate([jnp.zeros((SM_F, t), F32), dcq_ref[...], jnp.zeros((BLK - SM_F - 8, t), F32)], axis=0).T
        for b in range(nb - 1, -1, -1):
            blk = scr[b * BLK:(b + 1) * BLK, :] - dck_ref[b * BLK:(b + 1) * BLK, :]
            cb = _nn_hi(triu, blk) + carry
            scr[b * BLK:(b + 1) * BLK, :] = cb
            carry = cb[0:1, :]
        x = sm_ref[...] + fb_ref[...]
        lane = lax.broadcasted_iota(jnp.int32, (1, BLK), 1)
        keep = _valid_rows(t, 0) & (lane >= SM_F) & (lane < SM_F + FOX_H)
        df = jnp.where(keep, scr[...] * _sig(-x), 0.0)
        dfb_ref[...] = jnp.sum(df, axis=0, keepdims=True)
        dsm_ref[...] = din_ref[...] + df

    full = pl.BlockSpec((t, BLK), lambda i: (0, 0))
    return pl.pallas_call(
        body, name="fox_pre_bwd", grid=(1,),
        in_specs=[pl.BlockSpec((8, t), lambda i: (0, 0)), full,
                  pl.BlockSpec((t, BLK), lambda i: (0, C_SM // BLK)), pl.BlockSpec((1, BLK), lambda i: (0, 0)), full],
        out_specs=[full, pl.BlockSpec((1, BLK), lambda i: (0, 0))],
        out_shape=[jax.ShapeDtypeStruct((t, BLK), F32), jax.ShapeDtypeStruct((1, BLK), F32)],
        scratch_shapes=[pltpu.VMEM((t, BLK), F32)],
        compiler_params=_cp("arbitrary"),
    )(dcq, dck, proj, fb, dsm_in)


def _swap_rope(x):
    lane = lax.broadcasted_iota(jnp.int32, (1, BLK), 1)
    return jnp.where((lane >= SM_KR) & (lane < SM_KR + 16), pltpu.roll(x, BLK - 16, 1),
                     jnp.where((lane >= SM_KR + 16) & (lane < SM_KR + 32), pltpu.roll(x, 16, 1), 0.0))


def _rms(x, g):
    r = lax.rsqrt(jnp.mean(x * x, axis=1, keepdims=True) + EPS)
    return r, x * r


def mla_pre(proj, qg, kvg, wq, wk, wv, cosq, sinq):
    t = proj.shape[0]
    tm = _row_tile(t)

    def body(cq_ref, ckv_ref, sm_ref, qg_ref, kvg_ref, wq_ref, wk_ref, wv_ref, cos_ref, sin_ref,
             q_ref, k_ref, v_ref, cqn_ref, ckvn_ref):
        cs, sn = cos_ref[...], sin_ref[...]
        _, xh = _rms(cq_ref[...], None)
        cqn = (xh * qg_ref[...]).astype(BF16)
        cqn_ref[...] = cqn
        qraw = _nn(cqn, wq_ref[...])
        qs = []
        for h in range(MLA_H):
            hb = qraw[:, BLK * h:BLK * (h + 1)]
            qs.append(hb * cs + _swap_rope(hb) * sn)
        q_ref[...] = jnp.concatenate(qs, axis=1).astype(BF16)
        _, kh = _rms(ckv_ref[...], None)
        ckvn = (kh * kvg_ref[...]).astype(BF16)
        ckvn_ref[...] = ckvn
        kraw = _nn(ckvn, wk_ref[...])
        v_ref[...] = _nn(ckvn, wv_ref[...]).astype(BF16)
        lane = lax.broadcasted_iota(jnp.int32, (1, BLK), 1)
        kr = sm_ref[...]
        krr = jnp.where((lane >= SM_KR) & (lane < SM_KR + MLA_ROPE), kr * cs + _swap_rope(kr) * sn, 0.0)
        k_ref[...] = jnp.concatenate([kraw[:, BLK * h:BLK * (h + 1)] + krr for h in range(MLA_H)], axis=1).astype(BF16)

    def rows(w, cb):
        return pl.BlockSpec((tm, w), lambda i: (i, cb))

    def whole(a):
        return pl.BlockSpec(a.shape, lambda i: (0, 0))

    return pl.pallas_call(
        body, name="mla_pre", grid=(t // tm,),
        in_specs=[rows(MLA_QL, C_CQ // MLA_QL), rows(MLA_KVL, C_CKV // MLA_KVL), rows(BLK, C_SM // BLK),
                  whole(qg), whole(kvg), whole(wq), whole(wk), whole(wv), rows(BLK, 0), rows(BLK, 0)],
        out_specs=[rows(512, 0), rows(512, 0), rows(256, 0), rows(MLA_QL, 0), rows(MLA_KVL, 0)],
        out_shape=[jax.ShapeDtypeStruct((t, 512), BF16), jax.ShapeDtypeStruct((t, 512), BF16),
                   jax.ShapeDtypeStruct((t, 256), BF16), jax.ShapeDtypeStruct((t, MLA_QL), BF16),
                   jax.ShapeDtypeStruct((t, MLA_KVL), BF16)],
        compiler_params=_cp("arbitrary"),
    )(proj, proj, proj, qg, kvg, wq, wk, wv, cosq, sinq)


def mla_pre_bwd(dq, dk, dv, proj, cqn, ckvn, qg, kvg, wq, wk, wv, cosq, sinq, dsm_in):
    t = proj.shape[0]
    tm = _row_tile(t)

    def body(dq_ref, dk_ref, dv_ref, cq_ref, ckv_ref, cqn_ref, ckvn_ref, qg_ref, kvg_ref, wq_ref, wk_ref, wv_ref,
             cos_ref, sin_ref, din_ref, dcq_ref, dckv_ref, dsm_ref, dwq_ref, dwk_ref, dwv_ref, dqg_ref, dkvg_ref):
        i = pl.program_id(0)

        @pl.when(i == 0)
        def _():
            for r in (dwq_ref, dwk_ref, dwv_ref, dqg_ref, dkvg_ref):
                r[...] = jnp.zeros_like(r)

        cs, sn = cos_ref[...], sin_ref[...]
        lane = lax.broadcasted_iota(jnp.int32, (1, BLK), 1)

        def unrope(dy):
            return dy * cs + _swap_rope(dy * sn)

        dqp = jnp.concatenate([unrope(dq_ref[:, BLK * h:BLK * (h + 1)]) for h in range(MLA_H)], axis=1).astype(BF16)
        dwq_ref[...] += _tn(cqn_ref[...], dqp)
        dcqn = _nt(dqp, wq_ref[...])
        r, xh = _rms(cq_ref[...], None)
        dqg_ref[...] += jnp.sum(dcqn * xh, axis=0, keepdims=True)
        dxh = dcqn * qg_ref[...]
        dcq_ref[...] = r * (dxh - xh * jnp.mean(dxh * xh, axis=1, keepdims=True))

        dkn, dkr = [], jnp.zeros((tm, BLK), F32)
        for h in range(MLA_H):
            blk = dk_ref[:, BLK * h:BLK * (h + 1)]
            dkn.append(jnp.where(lane < MLA_NOPE, blk, 0.0))
            dkr += jnp.where((lane >= SM_KR) & (lane < SM_KR + MLA_ROPE), blk, 0.0)
        dknb = jnp.concatenate(dkn, axis=1).astype(BF16)
        dvb = dv_ref[...].astype(BF16)
        ckvn = ckvn_ref[...]
        dwk_ref[...] += _tn(ckvn, dknb)
        dwv_ref[...] += _tn(ckvn, dvb)
        dckvn = _nt(dknb, wk_ref[...]) + _nt(dvb, wv_ref[...])
        r2, kh = _rms(ckv_ref[...], None)
        dkvg_ref[...] += jnp.sum(dckvn * kh, axis=0, keepdims=True)
        dkh = dckvn * kvg_ref[...]
        dckv_ref[...] = r2 * (dkh - kh * jnp.mean(dkh * kh, axis=1, keepdims=True))
        dsm_ref[...] = din_ref[...] + jnp.where((lane >= SM_KR) & (lane < SM_KR + MLA_ROPE), unrope(dkr), 0.0)

    def rows(w, cb):
        return pl.BlockSpec((tm, w), lambda i: (i, cb))

    def whole(a):
        return pl.BlockSpec(a.shape, lambda i: (0, 0))

    def wshape(a):
        return jax.ShapeDtypeStruct(a.shape, F32)

    return pl.pallas_call(
        body, name="mla_pre_bwd", grid=(t // tm,),
        in_specs=[rows(512, 0), rows(512, 0), rows(256, 0), rows(MLA_QL, C_CQ // MLA_QL), rows(MLA_KVL, C_CKV // MLA_KVL),
                  rows(MLA_QL, 0), rows(MLA_KVL, 0), whole(qg), whole(kvg), whole(wq), whole(wk), whole(wv),
                  rows(BLK, 0), rows(BLK, 0), rows(BLK, 0)],
        out_specs=[rows(MLA_QL, 0), rows(MLA_KVL, 0), rows(BLK, 0), whole(wq), whole(wk), whole(wv), whole(qg), whole(kvg)],
        out_shape=[jax.ShapeDtypeStruct((t, MLA_QL), F32), jax.ShapeDtypeStruct((t, MLA_KVL), F32),
                   jax.ShapeDtypeStruct((t, BLK), F32), wshape(wq), wshape(wk), wshape(wv), wshape(qg), wshape(kvg)],
        compiler_params=_cp("arbitrary"),
    )(dq, dk, dv, proj, proj, cqn, ckvn, qg, kvg, wq, wk, wv, cosq, sinq, dsm_in)


def _slot_sum(me, own, recv_ref):
    gg = own.astype(F32)
    for s in range(N_DEV):
        gg = gg + jnp.where(me == s, 0.0, recv_ref[s].astype(F32))
    return gg


def adamw(w, m, v, g=None, recv=None, own=None, me_arr=None):
    shape = w.shape
    c = shape[-1]
    from_recv = recv is not None
    if not from_recv:
        me_arr = jnp.zeros((1,), jnp.int32)
    nl = len(recv) if from_recv else 1
    rws = w.size // c // nl
    tr = rws
    for d in (1024, 512, 352, 256, 128, 64, 32, 16, 8):
        if rws % d == 0 and d * c * 4 <= (2 << 20):
            tr = d
            break
    nt = rws // tr
    w2, m2, v2 = (a.reshape(nl, rws, c) for a in (w, m, v))
    if from_recv:
        gin = [a.reshape(N_DEV, rws, c) for a in list(recv) + list(own)]
    else:
        gin = [g.reshape(1, rws, c)]

    def body(me_ref, w_ref, m_ref, v_ref, *rest):
        g_refs, outs = rest[:len(gin)], rest[len(gin):]
        if from_recv:
            g_out, outs = outs[0], outs[1:]
            for li in range(nl):
                @pl.when(pl.program_id(0) == li)
                def _(li=li):
                    g_out[...] = _slot_sum(me_ref[0], g_refs[nl + li][...], g_refs[li])
            gg = g_out[...]
        else:
            gg = g_refs[0][...]
        d_ref, nm_ref, nv_ref = outs
        nm = B1 * m_ref[...] + (1.0 - B1) * gg
        nv = B2 * v_ref[...] + (1.0 - B2) * (gg * gg)
        mh = nm / (1.0 - B1 ** STEP)
        vh = nv / (1.0 - B2 ** STEP)
        d_ref[...] = -LR * (mh / (jnp.sqrt(vh) + AEPS) + WD * w_ref[...])
        nm_ref[...] = nm
        nv_ref[...] = nv

    row = pl.BlockSpec((None, tr, c), lambda l, i, me: (l, i, 0))
    if from_recv:
        gspecs = [pl.BlockSpec((N_DEV, tr, c), lambda l, i, me, li=li: (0, jnp.where(l == li, i, 0), 0))
                  for li in range(nl)]
        gspecs += [pl.BlockSpec((None, tr, c), lambda l, i, me, li=li: (me[0], jnp.where(l == li, i, 0), 0))
                   for li in range(nl)]
    else:
        gspecs = [row]
    nout = 4 if from_recv else 3
    outs = pl.pallas_call(
        body, name="adamw",
        grid_spec=pltpu.PrefetchScalarGridSpec(num_scalar_prefetch=1, grid=(nl, nt), in_specs=[row, row, row] + gspecs,
                                               out_specs=[row] * nout),
        out_shape=[jax.ShapeDtypeStruct((nl, rws, c), F32)] * nout,
        compiler_params=_cp("arbitrary", "arbitrary"),
    )(me_arr, w2, m2, v2, *gin)
    return tuple(o.reshape(shape) for o in outs)


def sum_slots(recv, own=None, me_arr=None):
    _, r, c = recv.shape
    if own is None:
        own, me_arr = recv, jnp.zeros((1,), jnp.int32)
        plain = True
    else:
        plain = False

    def body(me_ref, r_ref, own_ref, o_ref):
        if plain:
            gg = r_ref[0].astype(F32)
            for s in range(1, N_DEV):
                gg = gg + r_ref[s].astype(F32)
            o_ref[...] = gg
        else:
            o_ref[...] = _slot_sum(me_ref[0], own_ref[...], r_ref)

    return pl.pallas_call(
        body, name="sum_slots",
        grid_spec=pltpu.PrefetchScalarGridSpec(
            num_scalar_prefetch=1, grid=(1,),
            in_specs=[pl.BlockSpec((N_DEV, r, c), lambda i, me: (0, 0, 0)),
                      pl.BlockSpec((None, r, c), lambda i, me: (me[0], 0, 0))],
            out_specs=pl.BlockSpec((r, c), lambda i, me: (0, 0))),
        out_shape=jax.ShapeDtypeStruct((r, c), F32),
        compiler_params=_cp("arbitrary"),
    )(me_arr, recv, own)


_FLIPS = [(0, 0, 1), (0, 1, 0), (0, 1, 1), (1, 0, 0), (1, 0, 1), (1, 1, 0), (1, 1, 1)]
_ANY = pl.BlockSpec(memory_space=pl.ANY)


def _mesh_place():
    x, y, c = lax.axis_index("x"), lax.axis_index("y"), lax.axis_index("c")
    me = 4 * x + 2 * y + c
    peers = [((x + fx) % 2, (y + fy) % 2, (c + fc) % 2) for fx, fy, fc in _FLIPS]
    return me, peers


def place_own(src, l, dtype, me_arr):
    _, r, c = src.shape
    tr = r
    for d in (512, 352, 256, 128, 64, 32, 16, 8):
        if r % d == 0 and d * c * 4 <= (2 << 20):
            tr = d
            break

    def body(me_ref, s_ref, o_ref):
        o_ref[...] = s_ref[...].astype(dtype)

    return pl.pallas_call(
        body, name="place_own",
        grid_spec=pltpu.PrefetchScalarGridSpec(
            num_scalar_prefetch=1, grid=(r // tr,),
            in_specs=[pl.BlockSpec((None, tr, c), lambda i, me: (l, i, 0))],
            out_specs=pl.BlockSpec((None, tr, c), lambda i, me: (me[0], i, 0))),
        out_shape=jax.ShapeDtypeStruct((N_DEV, r, c), dtype),
        compiler_params=_cp("arbitrary"),
    )(me_arr, src)


_HBM = pl.BlockSpec(memory_space=pltpu.HBM)
_SEMS = pl.BlockSpec(memory_space=pltpu.SEMAPHORE)
_EFFECT = pltpu.SideEffectType.DATAFLOW_SIDE_EFFECTING


def exchange_start(mode, arrays, name, after=None):
    n = len(arrays)
    gather = mode == "gather"
    ns = 0 if gather else n
    zones = list(arrays) if gather else [lax.empty(a.shape, a.dtype) for a in arrays]
    ops = ([] if gather else list(arrays)) + zones
    extra = [] if after is None else [after]

    def body(*refs):
        srcs, lands = refs[:ns], refs[ns:ns + n]
        send_sems, recv_sems = refs[ns + n + len(extra)], refs[ns + n + len(extra) + 1]
        token = refs[-1]
        me, peers = _mesh_place()
        ids = [4 * p[0] + 2 * p[1] + p[2] for p in peers]
        for j in range(n):
            for k in range(N_DEV - 1):
                src = lands[j].at[me] if gather else srcs[j].at[ids[k]]
                pltpu.make_async_remote_copy(src_ref=src, dst_ref=lands[j].at[me],
                                             send_sem=send_sems.at[j * (N_DEV - 1) + k],
                                             recv_sem=recv_sems.at[j * (N_DEV - 1) + k], device_id=peers[k],
                                             device_id_type=pl.DeviceIdType.MESH).start()
        token[...] = jnp.zeros_like(token)

    nsem = n * (N_DEV - 1)
    res = pl.pallas_call(
        body, name=name,
        in_specs=[_HBM] * (ns + n) + [_ANY] * len(extra),
        out_specs=(_SEMS, _SEMS, *[_HBM] * (ns + n), pl.BlockSpec(memory_space=pltpu.VMEM)),
        out_shape=(pltpu.SemaphoreType.DMA((nsem,)), pltpu.SemaphoreType.DMA((nsem,)),
                   *[pltpu.HBM(a.shape, a.dtype) for a in ops], jax.ShapeDtypeStruct((8, BLK), F32)),
        input_output_aliases={i: 2 + i for i in range(ns + n)},
        compiler_params=pltpu.CompilerParams(has_side_effects=_EFFECT),
    )(*[pltpu.with_memory_space_constraint(a, pltpu.HBM) for a in ops], *extra)
    return dict(gather=gather, send=res[0], recv=res[1], srcs=list(res[2:2 + ns]), lands=list(res[2 + ns:2 + ns + n]),
                token=res[-1])


def exchange_wait(hd, idxs, name, after):
    gather = hd["gather"]
    n = len(idxs)
    ns = 0 if gather else n
    ops = ([] if gather else [hd["srcs"][j] for j in idxs]) + [hd["lands"][j] for j in idxs]

    def body(*refs):
        srcs, lands = refs[:ns], refs[ns:ns + n]
        send_sems, recv_sems = refs[ns + n], refs[ns + n + 1]
        me, peers = _mesh_place()
        ids = [4 * p[0] + 2 * p[1] + p[2] for p in peers]
        for p, j in enumerate(idxs):
            for k in range(N_DEV - 1):
                src = lands[p].at[me] if gather else srcs[p].at[ids[k]]
                cp = pltpu.make_async_remote_copy(src_ref=src, dst_ref=lands[p].at[ids[k]],
                                                  send_sem=send_sems.at[j * (N_DEV - 1) + k],
                                                  recv_sem=recv_sems.at[j * (N_DEV - 1) + k], device_id=peers[k],
                                                  device_id_type=pl.DeviceIdType.MESH)
                cp.wait_send()
                cp.wait_recv()

    res = pl.pallas_call(
        body, name=name,
        in_specs=[_HBM] * (ns + n) + [_SEMS, _SEMS, _ANY],
        out_specs=[_HBM] * (ns + n),
        out_shape=[pltpu.HBM(a.shape, a.dtype) for a in ops],
        input_output_aliases={i: i for i in range(ns + n)},
        compiler_params=pltpu.CompilerParams(has_side_effects=_EFFECT),
    )(*ops, hd["send"], hd["recv"], after)
    return list(res[:ns]), list(res[ns:])


def _chip_place():
    x, y, c = lax.axis_index("x"), lax.axis_index("y"), lax.axis_index("c")
    chips = [((x + 1) % 2, y), (x, (y + 1) % 2), ((x + 1) % 2, (y + 1) % 2)]
    ident = lambda p: 4 * p[0] + 2 * p[1] + p[2]
    return dict(me=4 * x + 2 * y + c, sib=(x, y, 1 - c), sib_id=4 * x + 2 * y + 1 - c,
                same=[(cx, cy, c) for cx, cy in chips], same_ids=[ident((cx, cy, c)) for cx, cy in chips],
                other_ids=[ident((cx, cy, 1 - c)) for cx, cy in chips])


def _remote(src, dst, send_sem, recv_sem, dev):
    return pltpu.make_async_remote_copy(src_ref=src, dst_ref=dst, send_sem=send_sem, recv_sem=recv_sem, device_id=dev,
                                        device_id_type=pl.DeviceIdType.MESH)


def gather_start(zones, name):
    n = len(zones)

    def body(*refs):
        lands, send_sems, recv_sems, token = refs[:n], refs[n], refs[n + 1], refs[-1]
        pc = _chip_place()
        for j in range(n):
            own = lands[j].at[pc["me"]]
            for k, dev in enumerate([pc["sib"]] + pc["same"]):
                _remote(own, own, send_sems.at[4 * j + k], recv_sems.at[4 * j + k], dev).start()
        token[...] = jnp.zeros_like(token)

    res = pl.pallas_call(
        body, name=name,
        in_specs=[_HBM] * n,
        out_specs=(_SEMS, _SEMS, *[_HBM] * n, pl.BlockSpec(memory_space=pltpu.VMEM)),
        out_shape=(pltpu.SemaphoreType.DMA((4 * n,)), pltpu.SemaphoreType.DMA((4 * n,)),
                   *[pltpu.HBM(a.shape, a.dtype) for a in zones], jax.ShapeDtypeStruct((8, BLK), F32)),
        input_output_aliases={i: 2 + i for i in range(n)},
        compiler_params=pltpu.CompilerParams(has_side_effects=_EFFECT),
    )(*[pltpu.with_memory_space_constraint(a, pltpu.HBM) for a in zones])
    return dict(send=res[0], recv=res[1], lands=list(res[2:2 + n]), token=res[-1])


def gather_relay(hd, idxs, name, after):
    n = len(idxs)

    def body(*refs):
        lands, send_sems, recv_sems = refs[:n], refs[n], refs[n + 1]
        fsend, frecv, token = refs[n + 3 + n], refs[n + 4 + n], refs[-1]
        pc = _chip_place()
        for p, j in enumerate(idxs):
            for k in range(3):
                _remote(lands[p].at[pc["me"]], lands[p].at[pc["same_ids"][k]], send_sems.at[4 * j + 1 + k],
                        recv_sems.at[4 * j + 1 + k], pc["same"][k]).wait_recv()
        for p in range(n):
            for k in range(3):
                got = lands[p].at[pc["same_ids"][k]]
                _remote(got, got, fsend.at[3 * p + k], frecv.at[3 * p + k], pc["sib"]).start()
        token[...] = jnp.zeros_like(token)

    ops = [hd["lands"][j] for j in idxs]
    res = pl.pallas_call(
        body, name=name,
        in_specs=[_HBM] * n + [_SEMS, _SEMS, _ANY],
        out_specs=(*[_HBM] * n, _SEMS, _SEMS, pl.BlockSpec(memory_space=pltpu.VMEM)),
        out_shape=(*[pltpu.HBM(a.shape, a.dtype) for a in ops], pltpu.SemaphoreType.DMA((3 * n,)),
                   pltpu.SemaphoreType.DMA((3 * n,)), jax.ShapeDtypeStruct((8, BLK), F32)),
        input_output_aliases={i: i for i in range(n)},
        compiler_params=pltpu.CompilerParams(has_side_effects=_EFFECT),
    )(*ops, hd["send"], hd["recv"], after)
    return dict(lands=list(res[:n]), fsend=res[n], frecv=res[n + 1], token=res[-1])


def gather_wait(hd, rl, idxs, name, after):
    n = len(idxs)

    def body(*refs):
        lands, send_sems, recv_sems, fsend, frecv = refs[:n], refs[n], refs[n + 1], refs[n + 2], refs[n + 3]
        pc = _chip_place()
        for p, j in enumerate(idxs):
            own = lands[p].at[pc["me"]]
            for k, dev in enumerate([pc["sib"]] + pc["same"]):
                _remote(own, own, send_sems.at[4 * j + k], recv_sems.at[4 * j + k], dev).wait_send()
            _remote(own, lands[p].at[pc["sib_id"]], send_sems.at[4 * j], recv_sems.at[4 * j], pc["sib"]).wait_recv()
            for k in range(3):
                cp = _remote(lands[p].at[pc["same_ids"][k]], lands[p].at[pc["other_ids"][k]], fsend.at[3 * p + k],
                             frecv.at[3 * p + k], pc["sib"])
                cp.wait_send()
                cp.wait_recv()

    res = pl.pallas_call(
        body, name=name,
        in_specs=[_HBM] * n + [_SEMS, _SEMS, _SEMS, _SEMS, _ANY],
        out_specs=[_HBM] * n,
        out_shape=[pltpu.HBM(a.shape, a.dtype) for a in rl["lands"]],
        input_output_aliases={i: i for i in range(n)},
        compiler_params=pltpu.CompilerParams(has_side_effects=_EFFECT),
    )(*rl["lands"], hd["send"], hd["recv"], rl["fsend"], rl["frecv"], after)
    return list(res)


def _pad_cols(a, n):
    return jnp.pad(a, ((0, 0),) * (a.ndim - 1) + ((0, n - a.shape[-1]),))


def w_in_to_padded(w):
    z = lambda n: jnp.zeros(w.shape[:-1] + (n,), w.dtype)
    return jnp.concatenate([
        w[..., 0:1280], w[..., 1288:2056], w[..., 2060:2316], w[..., 2316:2444],
        w[..., 1280:1288], w[..., 2056:2060], z(SM_KR - SM_F - FOX_H), w[..., 2444:2476], z(BLK - SM_KR - MLA_ROPE)], axis=-1)


def w_in_from_padded(g):
    s = C_SM
    return jnp.concatenate([
        g[..., 0:1280], g[..., s + SM_DT:s + SM_DT + 8], g[..., 1280:2048], g[..., s + SM_F:s + SM_F + 4],
        g[..., 2048:2304], g[..., 2304:2432], g[..., s + SM_KR:s + SM_KR + MLA_ROPE]], axis=-1)


def _unshard_cols(gth):
    n, r, c = gth.shape
    return jnp.transpose(gth, (1, 0, 2)).reshape(r, n * c)


def _shard_cols(full):
    r, nc = full.shape
    return jnp.transpose(full.reshape(r, N_DEV, nc // N_DEV), (1, 0, 2))


def mla_weights(uq_g, ukv_g):
    uq = _unshard_cols(uq_g)
    dqh = MLA_NOPE + MLA_ROPE
    wq = jnp.concatenate([_pad_cols(uq[:, dqh * h:dqh * (h + 1)], BLK) for h in range(MLA_H)], axis=1)
    wk = jnp.concatenate([_pad_cols(ukv_g[2 * h], BLK) for h in range(MLA_H)], axis=1)
    wv = jnp.concatenate([ukv_g[2 * h + 1] for h in range(MLA_H)], axis=1)
    return wq, wk, wv


def mla_weight_grads(dwq, dwk, dwv):
    dqh = MLA_NOPE + MLA_ROPE
    duq = _shard_cols(jnp.concatenate([dwq[:, BLK * h:BLK * h + dqh] for h in range(MLA_H)], axis=1))
    parts = []
    for h in range(MLA_H):
        parts += [dwk[:, BLK * h:BLK * h + MLA_NOPE], dwv[:, MLA_V * h:MLA_V * (h + 1)]]
    return duq, jnp.stack(parts, axis=0)


def rope_tables(t):
    pos = (jnp.arange(t, dtype=jnp.int32) - PAD).astype(F32)
    inv_freq = 1.0 / (10000.0 ** (jnp.arange(0, MLA_ROPE, 2, dtype=F32) / MLA_ROPE))
    ang = pos[:, None] * inv_freq[None, :]
    cos, sin = jnp.cos(ang), jnp.sin(ang)
    one, zero = jnp.ones((t, SM_KR), F32), jnp.zeros((t, SM_KR), F32)
    tail = BLK - SM_KR - MLA_ROPE
    cosq = jnp.concatenate([one, cos, cos, jnp.ones((t, tail), F32)], axis=1)
    sinq = jnp.concatenate([zero, -sin, sin, jnp.zeros((t, tail), F32)], axis=1)
    return cosq, sinq


def _lanes(v, off=0):
    return jnp.pad(v.astype(F32), (off, BLK - off - v.shape[0]))[None, :]


def layer_fwd(x, ln, hb, getw, tabs, ahead):
    sv = {"h0b": hb}
    def behind(vec, tok):
        return vec if tok is None else vec + 0.0 * tok[0:1, 0:1]

    W = dict(getw("ffn1", hb))
    ln1 = (behind(W["ln1_g"], ahead(0, "mix", hb, 1)), W["ln1_b"])
    u, v, r1, h1b = ffn_fwd_seq(x, ln, W["g1"], W["u1"], W["d1"], ln1)
    sv.update(u1=u, v1=v, r1=r1, h1b=h1b)
    W.update(getw("mix", h1b))
    ln2 = (W["ln2_g"], W["ln2_b"])
    proj = mm_nn(h1b, W["w_in"])
    xa = conv_fwd(proj, W["conv_w"], W["conv_b"])
    y_ssd, sprev = ssd_fwd(xa, proj, W["dtb"], W["alog"], W["dskip"], W["normg"])
    c_col, c_row = fox_pre(proj, W["fb"])
    y_fox, lse_f = attn_fwd(proj, proj, proj, C_FQ // 256, C_FK // 256, C_FV // 256, FOX_H, FOX_DH, FOX_DH,
                            FOX_DH ** -0.5, c_col, c_row, SM_F)
    q, k, vv, cqn, ckvn = mla_pre(proj, behind(W["qg"], ahead(0, "ffn2", y_fox)), W["kvg"], W["wq"], W["wk"], W["wv"], *tabs)
    y_mla, lse_m = attn_fwd(q, k, vv, 0, 0, 0, MLA_H, BLK, MLA_V, (MLA_NOPE + MLA_ROPE) ** -0.5)
    mixcat = jnp.concatenate([y_ssd, y_fox, y_mla], axis=1)
    r2, h2b = mm_res_ln(mixcat, W["w_out"], r1, ln1, ln2)
    sv.update(proj=proj, xa=xa, sprev=sprev, c_col=c_col, c_row=c_row, lse_f=lse_f, q=q, k=k, v=vv, cqn=cqn, ckvn=ckvn,
              lse_m=lse_m, mixcat=mixcat, r2=r2, h2b=h2b)
    W.update(getw("ffn2", h2b))
    ln3 = (behind(W["ln3_g"], ahead(1, "ffn1", h2b)), W["ln3_b"])
    u, v, r3, h3b = ffn_fwd_seq(r2, ln2, W["g2"], W["u2"], W["d2"], ln3)
    sv.update(u2=u, v2=v, r3=r3, W=W)
    return r3, ln3, h3b, sv


def ffn_bwd(parts, r, gamma, hb_in, u, v, wg, wu, wd, after=None):
    dh, dwg, dwu, dwd, dg, db = ffn_bwd_seq(parts, r, gamma, hb_in, u, v, wg, wu, wd, after)
    return dh, dict(d=dwd, g=dwg, u=dwu, ln_g=dg, ln_b=db)


def layer_bwd(parts, sv, emit, tabs, after):
    G = {}
    W = sv["W"]
    dh2, g2 = ffn_bwd(parts, sv["r3"], W["ln3_g"], sv["h2b"], sv["u2"], sv["v2"], W["g2"], W["u2"], W["d2"], after)
    G.update(g2=g2["g"], u2=g2["u"], d2=g2["d"], ln3_g=g2["ln_g"], ln3_b=g2["ln_b"])
    tok = emit("ffn2", G)
    dr2, dmc, G["w_out"], G["ln2_g"], G["ln2_b"] = oproj_bwd(dh2, sv["r2"], W["ln2_g"], sv["mixcat"], W["w_out"], tok)
    proj = sv["proj"]
    dxa, dz, dsm, G["normg"], G["dskip"], G["alog"], G["dtb"] = ssd_bwd(
        dmc, sv["xa"], proj, sv["sprev"], W["dtb"], W["alog"], W["dskip"], W["normg"])
    dxbc, G["conv_w"], G["conv_b"] = conv_bwd(dxa, proj, W["conv_w"], W["conv_b"])
    dfq, dfk, dfv, dcq, dck = attn_bwd(proj, proj, proj, dmc, sv["lse_f"], sv["mixcat"], C_FQ // 256, C_FK // 256,
                                       C_FV // 256, 2, 2, FOX_H, FOX_DH, FOX_DH, FOX_DH ** -0.5, sv["c_col"], sv["c_row"], SM_F)
    dsm, G["fb"] = fox_pre_bwd(dcq, dck, proj, W["fb"], dsm)
    dq, dk, dv = attn_bwd(sv["q"], sv["k"], sv["v"], dmc, sv["lse_m"], sv["mixcat"], 0, 0, 0, 3, 3, MLA_H, BLK, MLA_V,
                          (MLA_NOPE + MLA_ROPE) ** -0.5)
    dcql, dckv, dsm, G["wq"], G["wk"], G["wv"], G["qg"], G["kvg"] = mla_pre_bwd(
        dq, dk, dv, proj, sv["cqn"], sv["ckvn"], W["qg"], W["kvg"], W["wq"], W["wk"], W["wv"], *tabs, dsm)
    dproj = jnp.concatenate([dz, dxbc, dfq, dfk, dfv, dcql, dckv, dsm], axis=1).astype(BF16)
    dh1p, G["w_in"] = proj_bwd(dproj, sv["h1b"], W["w_in"])
    tok = emit("mix", G)
    dh0, g1 = ffn_bwd([(dr2, ALPHA), (dh1p, 1.0)], sv["r1"], W["ln1_g"], sv["h0b"], sv["u1"], sv["v1"],
                      W["g1"], W["u1"], W["d1"], tok)
    G.update(g1=g1["g"], u1=g1["u"], d1=g1["d"], ln1_g=g1["ln_g"], ln1_b=g1["ln_b"])
    tok = emit("ffn1", G)
    return [(dh0, 1.0)], G, tok


def local_step(x, target, meta_full, getw, emit, ahead=lambda l, stage, after, min_layer=0: None):
    t = x.shape[0] + BLK
    tabs = rope_tables(t)
    xr, hb = build_h0(meta_full, x)
    ln = None
    saved = []
    for l in range(NL):
        xr, ln, hb, sv = layer_fwd(xr, ln, hb, functools.partial(getw, l), tabs,
                                   lambda dl, stage, after, min_layer=0, l=l: ahead(l + dl, stage, after, min_layer))
        saved.append(sv)
    dy, loss = loss_head(xr, ln, target)
    parts = [(dy, 1.0)]
    grads = [None] * NL
    tok = None
    for l in range(NL - 1, -1, -1):
        parts, grads[l], tok = layer_bwd(parts, saved[l], functools.partial(emit, l), tabs, tok)
    gx, gmeta = split_dh0(parts[0][0], tok)
    return loss, gx, gmeta, grads


_SMALL = ["ln1_g", "ln1_b", "ln2_g", "ln2_b", "ln3_g", "ln3_b", "conv_b", "ssd_norm_g", "mla_q_norm_g",
          "mla_kv_norm_g", "dt_bias", "a_log", "d_skip", "fox_f_b"]
_SMALL_ROWS = 8
_BIG = ["ffn1_w_gate", "ffn1_w_up", "ffn1_w_down", "w_in", "conv_w", "mla_w_uq", "mla_w_ukv", "w_out",
        "ffn2_w_gate", "ffn2_w_up", "ffn2_w_down"]
_NAMES = ["meta", "ffn1_w_gate", "ffn1_w_up", "ffn1_w_down", "ln1_g", "ln1_b", "w_in", "conv_w", "conv_b", "dt_bias",
          "a_log", "d_skip", "ssd_norm_g", "fox_f_b", "mla_q_norm_g", "mla_w_uq", "mla_kv_norm_g", "mla_w_ukv", "w_out",
          "ln2_g", "ln2_b", "ffn2_w_gate", "ffn2_w_up", "ffn2_w_down", "ln3_g", "ln3_b"]


def pack_small(p):
    flat = jnp.concatenate([p[n].astype(F32) for n in _SMALL], axis=1)
    return _pad_cols(flat, _SMALL_ROWS * D).reshape(NL * _SMALL_ROWS, D)


def unpack_small(a, like):
    flat = a.reshape(NL, _SMALL_ROWS * D)
    out, at = {}, 0
    for n in _SMALL:
        out[n] = flat[:, at:at + like[n].shape[1]]
        at += like[n].shape[1]
    return out


_STAGES = {"ffn1": ["ffn1_w_gate", "ffn1_w_up", "ffn1_w_down"],
           "mix": ["w_in", "conv_w", "mla_w_uq", "mla_w_ukv", "w_out"],
           "ffn2": ["ffn2_w_gate", "ffn2_w_up", "ffn2_w_down"]}


_FFN_T = ("ffn1_w_gate", "ffn1_w_up", "ffn2_w_gate", "ffn2_w_up")


def stage_weights(l, stage, g, rep):
    if stage != "mix":
        i = stage[3]
        return {"g" + i: g[f"ffn{i}_w_gate"].reshape(D_FF, D), "u" + i: g[f"ffn{i}_w_up"].reshape(D_FF, D),
                "d" + i: g[f"ffn{i}_w_down"].reshape(D_FF, D),
                "ln1_g" if i == "1" else "ln3_g": rep["ln1_g" if i == "1" else "ln3_g"][l][None, :],
                "ln1_b" if i == "1" else "ln3_b": rep["ln1_b" if i == "1" else "ln3_b"][l][None, :]}
    W = {}
    W["w_in"] = g["w_in"].reshape(D, N_INP)
    W["w_out"] = g["w_out"].reshape(D, D)
    W["wq"], W["wk"], W["wv"] = mla_weights(g["mla_w_uq"], g["mla_w_ukv"])
    W["conv_w"] = _unshard_cols(g["conv_w"])
    for k in ("ln2_g", "ln2_b", "conv_b"):
        W[k] = rep[k][l][None, :]
    W["normg"] = rep["ssd_norm_g"][l][None, :]
    W["qg"] = rep["mla_q_norm_g"][l][None, :]
    W["kvg"] = rep["mla_kv_norm_g"][l][None, :]
    W["dtb"] = _lanes(rep["dt_bias"][l], SM_DT)
    W["alog"] = _lanes(rep["a_log"][l], SM_DT)
    W["dskip"] = _lanes(rep["d_skip"][l], SM_DT)
    W["fb"] = _lanes(rep["fox_f_b"][l], SM_F)
    return W


def small_grads(G):
    return {"ln1_g": G["ln1_g"][0], "ln1_b": G["ln1_b"][0], "ln2_g": G["ln2_g"][0], "ln2_b": G["ln2_b"][0],
            "ln3_g": G["ln3_g"][0], "ln3_b": G["ln3_b"][0], "conv_b": G["conv_b"][0], "ssd_norm_g": G["normg"][0],
            "mla_q_norm_g": G["qg"][0], "mla_kv_norm_g": G["kvg"][0], "dt_bias": G["dtb"][0, :SSD_H],
            "a_log": G["alog"][0, :SSD_H], "d_skip": G["dskip"][0, :SSD_H], "fox_f_b": G["fb"][0, SM_F:SM_F + FOX_H]}


def big_grads(G, stage):
    if stage != "mix":
        i = stage[-1]
        return {f"ffn{i}_w_{k}": G[k[0] + i].reshape(N_DEV, HS, D) for k in ("gate", "up", "down")}
    duq, dukv = mla_weight_grads(G["wq"], G["wk"], G["wv"])
    return {"w_in": G["w_in"].reshape(N_DEV, D // N_DEV, N_INP), "w_out": G["w_out"].reshape(N_DEV, D // N_DEV, D),
            "mla_w_uq": duq, "mla_w_ukv": dukv, "conv_w": _shard_cols(G["conv_w"])}


def kernel(x, meta, ffn1_w_gate, ffn1_w_up, ffn1_w_down, ln1_g, ln1_b, w_in, conv_w, conv_b, dt_bias, a_log, d_skip, ssd_norm_g, fox_f_b, mla_q_norm_g, mla_w_uq, mla_kv_norm_g, mla_w_ukv, w_out, ln2_g, ln2_b, ffn2_w_gate, ffn2_w_up, ffn2_w_down, ln3_g, ln3_b, loss_target, m_meta, m_ffn1_w_gate, m_ffn1_w_up, m_ffn1_w_down, m_ln1_g, m_ln1_b, m_w_in, m_conv_w, m_conv_b, m_dt_bias, m_a_log, m_d_skip, m_ssd_norm_g, m_fox_f_b, m_mla_q_norm_g, m_mla_w_uq, m_mla_kv_norm_g, m_mla_w_ukv, m_w_out, m_ln2_g, m_ln2_b, m_ffn2_w_gate, m_ffn2_w_up, m_ffn2_w_down, m_ln3_g, m_ln3_b, v_meta, v_ffn1_w_gate, v_ffn1_w_up, v_ffn1_w_down, v_ln1_g, v_ln1_b, v_w_in, v_conv_w, v_conv_b, v_dt_bias, v_a_log, v_d_skip, v_ssd_norm_g, v_fox_f_b, v_mla_q_norm_g, v_mla_w_uq, v_mla_kv_norm_g, v_mla_w_ukv, v_w_out, v_ln2_g, v_ln2_b, v_ffn2_w_gate, v_ffn2_w_up, v_ffn2_w_down, v_ln3_g, v_ln3_b):
    vals = (meta, ffn1_w_gate, ffn1_w_up, ffn1_w_down, ln1_g, ln1_b, w_in, conv_w, conv_b, dt_bias, a_log, d_skip, ssd_norm_g, fox_f_b, mla_q_norm_g, mla_w_uq, mla_kv_norm_g, mla_w_ukv, w_out, ln2_g, ln2_b, ffn2_w_gate, ffn2_w_up, ffn2_w_down, ln3_g, ln3_b)
    moms = (m_meta, m_ffn1_w_gate, m_ffn1_w_up, m_ffn1_w_down, m_ln1_g, m_ln1_b, m_w_in, m_conv_w, m_conv_b, m_dt_bias, m_a_log, m_d_skip, m_ssd_norm_g, m_fox_f_b, m_mla_q_norm_g, m_mla_w_uq, m_mla_kv_norm_g, m_mla_w_ukv, m_w_out, m_ln2_g, m_ln2_b, m_ffn2_w_gate, m_ffn2_w_up, m_ffn2_w_down, m_ln3_g, m_ln3_b)
    vars_ = (v_meta, v_ffn1_w_gate, v_ffn1_w_up, v_ffn1_w_down, v_ln1_g, v_ln1_b, v_w_in, v_conv_w, v_conv_b, v_dt_bias, v_a_log, v_d_skip, v_ssd_norm_g, v_fox_f_b, v_mla_q_norm_g, v_mla_w_uq, v_mla_kv_norm_g, v_mla_w_ukv, v_w_out, v_ln2_g, v_ln2_b, v_ffn2_w_gate, v_ffn2_w_up, v_ffn2_w_down, v_ln3_g, v_ln3_b)
    P = dict(zip(_NAMES, vals))
    M = dict(zip(_NAMES, moms))
    V = dict(zip(_NAMES, vars_))
    me = 4 * lax.axis_index("x") + 2 * lax.axis_index("y") + lax.axis_index("c")

    me_arr = me.astype(jnp.int32).reshape(1)
    for n in _FFN_T:
        P[n], M[n], V[n] = (jnp.swapaxes(a[n], 1, 2) for a in (P, M, V))
    src = dict(P)
    src["w_in"] = w_in_to_padded(P["w_in"])
    order = [("meta", 0)] + [(n, l) for l in range(NL) for names in _STAGES.values() for n in names]
    nfirst = 1 + len(_STAGES["ffn1"])

    def place(n, l):
        return place_own(P["meta"][None] if n == "meta" else src[n], l, F32 if n in ("meta", "conv_w") else BF16, me_arr)

    hg_first = gather_start([place(n, l) for n, l in order[:nfirst]], "gather_start_first")
    hg_rest = gather_start([place(n, l) for n, l in order[nfirst:]], "gather_start_rest")
    zone_of = {nl_: ((hg_first, i) if i < nfirst else (hg_rest, i - nfirst)) for i, nl_ in enumerate(order)}
    relays = {}

    def ahead(l, stage, after, min_layer=0):
        if not min_layer <= l < NL:
            return None
        if (l, stage) not in relays:
            zs = [zone_of[("meta", 0)]] if stage == "meta" else [zone_of[(n, l)] for n in _STAGES[stage]]
            hg, idxs = zs[0][0], [i for _, i in zs]
            relays[(l, stage)] = (hg, idxs, gather_relay(hg, idxs, f"gather_relay_{l}_{stage}", after))
        return relays[(l, stage)][2]["token"]

    def arrived(l, stage, after):
        ahead(l, stage, after)
        hg, idxs, rl = relays[(l, stage)]
        return gather_wait(hg, rl, idxs, f"gather_wait_{l}_{stage}", after)

    meta_full = _unshard_cols(arrived(0, "meta", hg_rest["token"])[0])

    def getw(l, stage, after):
        return stage_weights(l, stage, dict(zip(_STAGES[stage], arrived(l, stage, after))), P)

    sent = {}

    def emit(l, stage, G):
        bg = big_grads(G, stage)
        sent[(l, stage)] = exchange_start("scatter", [bg[n] for n in _STAGES[stage]], f"scatter_start_{l}_{stage}")
        return sent[(l, stage)]["token"]

    loss, gx, gmeta, grads = local_step(x[0], loss_target[0], meta_full, getw, emit, ahead)

    small = jnp.concatenate([pack_small({n: jnp.stack([small_grads(g)[n] for g in grads]) for n in _SMALL}), gmeta,
                             jnp.pad(loss, ((0, 7), (0, D - 1)))], axis=0)
    hs = exchange_start("gather", [place_own(small[None], 0, F32, me_arr)], "small_start")

    out = {}
    after = hs["token"]
    for stage in ("ffn2", "mix", "ffn1"):
        names = _STAGES[stage]
        got = [exchange_wait(sent[(l, stage)], list(range(len(names))), f"scatter_wait_{l}_{stage}", after)
               for l in range(NL - 1, -1, -1)][::-1]
        for i, n in enumerate(names):
            own = [got[l][0][i] for l in range(NL)]
            recv = [got[l][1][i] for l in range(NL)]
            if n == "w_in":
                g = jnp.stack([w_in_from_padded(sum_slots(recv[l], own[l], me_arr)) for l in range(NL)])
                out[n] = (g,) + adamw(P[n], M[n], V[n], g=g)
            else:
                out[n] = adamw(P[n], M[n], V[n], recv=recv, own=own, me_arr=me_arr)
                if n in _FFN_T:
                    out[n] = tuple(jnp.swapaxes(a, 1, 2) for a in out[n])
        after = out[names[-1]][1]
    gsmall = sum_slots(exchange_wait(hs, [0], "small_wait", after)[1][0])
    gm = lax.dynamic_slice(gsmall[NL * _SMALL_ROWS:], (0, me * (D // N_DEV)), (N_META, D // N_DEV))
    out["meta"] = (gm,) + adamw(P["meta"], M["meta"], V["meta"], g=gm)
    gs = gsmall[:NL * _SMALL_ROWS]
    sd, sm_, sv_ = adamw(pack_small(P), pack_small(M), pack_small(V), g=gs)
    ups = [unpack_small(a, P) for a in (gs, sd, sm_, sv_)]
    for n in _SMALL:
        out[n] = tuple(u[n] for u in ups)

    loss_all = gsmall[NL * _SMALL_ROWS + N_META, 0]
    flat = [loss_all, gx[None]]
    for k in range(4):
        flat += [out[n][k] for n in _NAMES]
    return tuple(flat)
```

```python
import functools

import jax
import jax.numpy as jnp
from jax import lax
from jax.experimental import pallas as pl
from jax.experimental.pallas import tpu as pltpu

F32, BF16 = jnp.float32, jnp.bfloat16
HI = lax.Precision.HIGHEST

N_DEV = 8
D = 1024
NL = 2
N_META = 16
BLK = 128
PAD = BLK - N_META
D_FF = 2816
HS = D_FF // N_DEV
SSD_H, SSD_P, SSD_N, SSD_G = 8, 64, 64, 2
SSD_D = SSD_H * SSD_P
CONV_K = 4
CONV_D = SSD_D + 2 * SSD_G * SSD_N
FOX_H, FOX_DH = 4, 64
MLA_H, MLA_QL, MLA_KVL, MLA_NOPE, MLA_ROPE, MLA_V = 4, 256, 128, 64, 32, 64
N_IN = 2476
C_Z, C_XBC, C_FQ, C_FK, C_FV, C_CQ, C_CKV, C_SM, N_INP = 0, 512, 1280, 1536, 1792, 2048, 2304, 2432, 2560
SM_DT, SM_F, SM_KR = 0, 8, 64
ALPHA = (2 * NL) ** 0.25
EPS = 1e-5
NEG = -1e30
LR, B1, B2, AEPS, WD, STEP = 0.001, 0.9, 0.999, 1e-08, 0.01, 10
VMEM_MB = 56


def _cp(*sem):
    return pltpu.CompilerParams(dimension_semantics=sem, vmem_limit_bytes=VMEM_MB << 20)


def _nn(a, b):
    return lax.dot_general(a, b, (((1,), (0,)), ((), ())), preferred_element_type=F32)


def _nt(a, b):
    return lax.dot_general(a, b, (((1,), (1,)), ((), ())), preferred_element_type=F32)


def _tn(a, b):
    return lax.dot_general(a, b, (((0,), (0,)), ((), ())), preferred_element_type=F32)


def _nn_hi(a, b):
    return lax.dot_general(a, b, (((1,), (0,)), ((), ())), precision=HI, preferred_element_type=F32)


def _row_tile(t):
    for d in range(640, 15, -16):
        if t % d == 0:
            return d
    raise ValueError(t)


def _sig(x):
    return 1.0 / (1.0 + jnp.exp(-x))


def _tri(lower=True):
    r = lax.broadcasted_iota(jnp.int32, (BLK, BLK), 0)
    c = lax.broadcasted_iota(jnp.int32, (BLK, BLK), 1)
    return (r >= c) if lower else (r <= c)


def build_h0(meta_full, x):
    s = x.shape[0]
    nb = s // BLK + 1

    def body(m_ref, x_ref, h_ref, hb_ref):
        i = pl.program_id(0)

        @pl.when(i == 0)
        def _():
            h = jnp.concatenate([jnp.zeros((PAD, D), F32), m_ref[...]], axis=0)
            h_ref[...] = h
            hb_ref[...] = h.astype(BF16)

        @pl.when(i > 0)
        def _():
            h_ref[...] = x_ref[...]
            hb_ref[...] = x_ref[...].astype(BF16)

    return pl.pallas_call(
        body, name="build_h0", grid=(nb,),
        in_specs=[pl.BlockSpec((N_META, D), lambda i: (0, 0)),
                  pl.BlockSpec((BLK, D), lambda i: (jnp.maximum(i - 1, 0), 0))],
        out_specs=[pl.BlockSpec((BLK, D), lambda i: (i, 0))] * 2,
        out_shape=[jax.ShapeDtypeStruct((nb * BLK, D), F32), jax.ShapeDtypeStruct((nb * BLK, D), BF16)],
        compiler_params=_cp("arbitrary"),
    )(meta_full, x)


FT = 256


def _layer_norm(r, gamma, beta):
    mu = jnp.mean(r, axis=1, keepdims=True)
    xc = r - mu
    var = jnp.mean(xc * xc, axis=1, keepdims=True)
    return xc * lax.rsqrt(var + EPS) * gamma + beta


def ffn_fwd_seq(x, ln_in, wg, wu, wd, ln_out):
    t = x.shape[0]
    f = wg.shape[0]
    nj, nr = f // FT, t // _row_tile(t)
    rc = t // nr
    plain = ln_in is None
    gi, bi = ln_out if plain else ln_in

    def body(x_hbm, gi_ref, bi_ref, go_ref, bo_ref, wg_ref, wu_ref, wd_ref, u_ref, v_ref, r_hbm, yb_hbm,
             acc, hbs, xbuf, sem_in, sem_out):
        j = pl.program_id(0)

        @pl.when(j == 0)
        def _():
            def fetch(k):
                return pltpu.make_async_copy(x_hbm.at[pl.ds(k * rc, rc)], xbuf.at[k % 2], sem_in.at[k % 2])

            fetch(0).start()
            for k in range(nr):
                if k + 1 < nr:
                    fetch(k + 1).start()
                fetch(k).wait()
                h = xbuf[k % 2]
                if not plain:
                    h = _layer_norm(h, gi_ref[...], bi_ref[...])
                acc[k * rc:(k + 1) * rc, :] = ALPHA * h
                hbs[k * rc:(k + 1) * rc, :] = h.astype(BF16)

        def chunk(k, last):
            sl = slice(k * rc, (k + 1) * rc)
            h = hbs[sl, :]
            u = _nt(h, wg_ref[...])
            v = _nt(h, wu_ref[...])
            u_ref[sl, :] = u.astype(BF16)
            v_ref[sl, :] = v.astype(BF16)
            acc[sl, :] += _nn((0.5 * u * _sig(u) * v).astype(BF16), wd_ref[...])
            if not last:
                return []
            rows = pl.ds(k * rc, rc)
            cps = [pltpu.make_async_copy(acc.at[rows], r_hbm.at[rows], sem_out.at[2 * k])]
            cps[0].start()
            hbs[sl, :] = _layer_norm(acc[sl, :], go_ref[...], bo_ref[...]).astype(BF16)
            cps.append(pltpu.make_async_copy(hbs.at[rows], yb_hbm.at[rows], sem_out.at[2 * k + 1]))
            cps[1].start()
            return cps

        @pl.when(j < nj - 1)
        def _():
            for k in range(nr):
                chunk(k, False)

        @pl.when(j == nj - 1)
        def _():
            cps = []
            for k in range(nr):
                cps += chunk(k, True)
            for cp in cps:
                cp.wait()

    vec = pl.BlockSpec((1, D), lambda j: (0, 0))
    wsp = pl.BlockSpec((FT, D), lambda j: (j, 0))
    act = pl.BlockSpec((None, t, FT), lambda j: (j, 0, 0))
    return pl.pallas_call(
        body, name="ffn_fwd_seq", grid=(nj,),
        in_specs=[_ANY, vec, vec, vec, vec, wsp, wsp, wsp],
        out_specs=[act, act, _ANY, _ANY],
        out_shape=[jax.ShapeDtypeStruct((nj, t, FT), BF16), jax.ShapeDtypeStruct((nj, t, FT), BF16),
                   jax.ShapeDtypeStruct((t, D), F32), jax.ShapeDtypeStruct((t, D), BF16)],
        scratch_shapes=[pltpu.VMEM((t, D), F32), pltpu.VMEM((t, D), BF16), pltpu.VMEM((2, rc, D), F32),
                        pltpu.SemaphoreType.DMA((2,)), pltpu.SemaphoreType.DMA((2 * nr,))],
        compiler_params=_cp("arbitrary"),
    )(x, gi, bi, ln_out[0], ln_out[1], wg, wu, wd)


def ffn_bwd_seq(parts, r, gamma, hb, u, v, wg, wu, wd, after=None):
    nj, t, _ = u.shape
    f = nj * FT
    nr = t // _row_tile(t)
    rc = t // nr
    nc = t // BLK
    scales = [s for _, s in parts]
    npart = len(parts)
    extra = [] if after is None else [after]

    def body(*refs):
        refs = refs[len(extra):]
        p_hbm, refs = refs[:npart], refs[npart:]
        (r_hbm, g_ref, hb_hbm, u_ref, v_ref, wg_ref, wu_ref, wd_ref, dh_hbm, dwg_ref, dwu_ref, dwd_ref, dg_ref, db_ref,
         dfs, hbt, dft, dhacc, dus, dvs, acs, pbuf, rbuf, hbuf, sems, sem_out) = refs
        j = pl.program_id(0)

        @pl.when(j == 0)
        def _():
            def fetch(c):
                rows = pl.ds(c * BLK, BLK)
                cps = [pltpu.make_async_copy(p_hbm[p].at[rows], pbuf.at[c % 2, p], sems.at[c % 2, p]) for p in range(npart)]
                cps.append(pltpu.make_async_copy(r_hbm.at[rows], rbuf.at[c % 2], sems.at[c % 2, npart]))
                cps.append(pltpu.make_async_copy(hb_hbm.at[rows], hbuf.at[c % 2], sems.at[c % 2, npart + 1]))
                return cps

            for cp in fetch(0):
                cp.start()
            dg = jnp.zeros((1, D), F32)
            db = jnp.zeros((1, D), F32)
            for c in range(nc):
                if c + 1 < nc:
                    for cp in fetch(c + 1):
                        cp.start()
                for cp in fetch(c):
                    cp.wait()
                sl = slice(c * BLK, (c + 1) * BLK)
                dy = scales[0] * pbuf[c % 2, 0]
                for p in range(1, npart):
                    dy += scales[p] * pbuf[c % 2, p]
                rr = rbuf[c % 2]
                xc = rr - jnp.mean(rr, axis=1, keepdims=True)
                rstd = lax.rsqrt(jnp.mean(xc * xc, axis=1, keepdims=True) + EPS)
                xh = xc * rstd
                dxh = dy * g_ref[...]
                dr = rstd * (dxh - jnp.mean(dxh, axis=1, keepdims=True) - xh * jnp.mean(dxh * xh, axis=1, keepdims=True))
                dg += jnp.sum(dy * xh, axis=0, keepdims=True)
                db += jnp.sum(dy, axis=0, keepdims=True)
                dhacc[sl, :] = ALPHA * dr
                dfc = (0.5 * dr).astype(BF16)
                dfs[sl, :] = dfc
                dft[:, sl] = dfc.T
                hbt[:, sl] = hbuf[c % 2].T
            dg_ref[...] = dg
            db_ref[...] = db

        for k in range(nr):
            sl = slice(k * rc, (k + 1) * rc)
            da = _nt(dfs[sl, :], wd_ref[...])
            uu = u_ref[sl, :].astype(F32)
            vv = v_ref[sl, :].astype(F32)
            sg = _sig(uu)
            du = (da * vv * (sg * (1.0 + uu * (1.0 - sg)))).astype(BF16)
            dv = (da * uu * sg).astype(BF16)
            dus[sl, :] = du
            dvs[sl, :] = dv
            acs[sl, :] = (uu * sg * vv).astype(BF16)
            dhacc[sl, :] += _nn(du, wg_ref[...]) + _nn(dv, wu_ref[...])
        @pl.when(j == nj - 1)
        def _():
            pltpu.make_async_copy(dhacc, dh_hbm, sem_out.at[0]).start()

        dwg_ref[...] = _nn(hbt[...], dus[...]).astype(BF16).T
        dwu_ref[...] = _nn(hbt[...], dvs[...]).astype(BF16).T
        dwd_ref[...] = _nn(dft[...], acs[...]).astype(BF16).T

        @pl.when(j == nj - 1)
        def _():
            pltpu.make_async_copy(dhacc, dh_hbm, sem_out.at[0]).wait()

    vec = pl.BlockSpec((1, D), lambda j: (0, 0))
    wsp = pl.BlockSpec((FT, D), lambda j: (j, 0))
    act = pl.BlockSpec((None, t, FT), lambda j: (j, 0, 0))
    return pl.pallas_call(
        body, name="ffn_bwd_seq", grid=(nj,),
        in_specs=[_ANY] * (len(extra) + npart + 1) + [vec, _ANY, act, act, wsp, wsp, wsp],
        out_specs=[_ANY, wsp, wsp, wsp, vec, vec],
        out_shape=[jax.ShapeDtypeStruct((t, D), F32)] + [jax.ShapeDtypeStruct((f, D), BF16)] * 3
        + [jax.ShapeDtypeStruct((1, D), F32)] * 2,
        scratch_shapes=[pltpu.VMEM((t, D), BF16), pltpu.VMEM((D, t), BF16), pltpu.VMEM((D, t), BF16),
                        pltpu.VMEM((t, D), F32), pltpu.VMEM((t, FT), BF16), pltpu.VMEM((t, FT), BF16),
                        pltpu.VMEM((t, FT), BF16), pltpu.VMEM((2, npart, BLK, D), F32), pltpu.VMEM((2, BLK, D), F32),
                        pltpu.VMEM((2, BLK, D), BF16), pltpu.SemaphoreType.DMA((2, npart + 2)),
                        pltpu.SemaphoreType.DMA((1,))],
        compiler_params=_cp("arbitrary"),
    )(*extra, *[p for p, _ in parts], r, gamma, hb, u, v, wg, wu, wd)


def mm_res_ln(a, b, x, ln_in, ln_out):
    t, k = a.shape
    tm = _row_tile(t)

    def body(a_ref, b_ref, x_ref, gi_ref, bi_ref, go_ref, bo_ref, r_ref, yb_ref):
        r = ALPHA * _layer_norm(x_ref[...], gi_ref[...], bi_ref[...]) + _nn(a_ref[...], b_ref[...])
        r_ref[...] = r
        yb_ref[...] = _layer_norm(r, go_ref[...], bo_ref[...]).astype(BF16)

    row = pl.BlockSpec((tm, D), lambda i: (i, 0))
    vec = pl.BlockSpec((1, D), lambda i: (0, 0))
    return pl.pallas_call(
        body, name="mm_res_ln", grid=(t // tm,),
        in_specs=[pl.BlockSpec((tm, k), lambda i: (i, 0)), pl.BlockSpec((k, D), lambda i: (0, 0)), row, vec, vec, vec, vec],
        out_specs=[row, row],
        out_shape=[jax.ShapeDtypeStruct((t, D), F32), jax.ShapeDtypeStruct((t, D), BF16)],
        compiler_params=_cp("arbitrary"),
    )(a, b, x, ln_in[0], ln_in[1], ln_out[0], ln_out[1])


def mm_nn(a, b):
    t, k = a.shape
    n = tn = b.shape[1]
    tm = _row_tile(t)

    def body(a_ref, b_ref, o_ref):
        o_ref[...] = _nn(a_ref[...], b_ref[...])

    return pl.pallas_call(
        body, name="mm_nn", grid=(t // tm, n // tn),
        in_specs=[pl.BlockSpec((tm, k), lambda i, j: (i, 0)), pl.BlockSpec((k, tn), lambda i, j: (0, j))],
        out_specs=pl.BlockSpec((tm, tn), lambda i, j: (i, j)),
        out_shape=jax.ShapeDtypeStruct((t, n), F32),
        compiler_params=_cp("arbitrary", "arbitrary"),
    )(a, b)


def oproj_bwd(dy, r, gamma, mixcat, w_out, after=None):
    t = r.shape[0]
    tm = _row_tile(t)
    nt = t // tm
    extra = [] if after is None else [after]

    def body(*refs):
        dy_ref, r_ref, g_ref, m_ref, w_ref, dr_ref, dm_ref, dw_ref, dg_ref, db_ref, acc = refs[len(extra):]
        i = pl.program_id(0)
        dy = dy_ref[...]
        rr = r_ref[...]
        xc = rr - jnp.mean(rr, axis=1, keepdims=True)
        rstd = lax.rsqrt(jnp.mean(xc * xc, axis=1, keepdims=True) + EPS)
        xh = xc * rstd
        dxh = dy * g_ref[...]
        dr = rstd * (dxh - jnp.mean(dxh, axis=1, keepdims=True) - xh * jnp.mean(dxh * xh, axis=1, keepdims=True))
        dr_ref[...] = dr
        drb = dr.astype(BF16)
        dm_ref[...] = _nt(drb, w_ref[...])
        dw = _tn(m_ref[...], drb)
        dg = jnp.sum(dy * xh, axis=0, keepdims=True)
        db = jnp.sum(dy, axis=0, keepdims=True)

        @pl.when(i == 0)
        def _():
            acc[...] = dw
            dg_ref[...] = dg
            db_ref[...] = db

        @pl.when(i > 0)
        def _():
            acc[...] += dw
            dg_ref[...] += dg
            db_ref[...] += db

        @pl.when(i == nt - 1)
        def _():
            dw_ref[...] = acc[...].astype(BF16)

    row = pl.BlockSpec((tm, D), lambda i: (i, 0))
    vec = pl.BlockSpec((1, D), lambda i: (0, 0))
    mat = pl.BlockSpec((D, D), lambda i: (0, 0))
    return pl.pallas_call(
        body, name="oproj_bwd", grid=(nt,),
        in_specs=[_ANY] * len(extra) + [row, row, vec, row, mat],
        out_specs=[row, row, mat, vec, vec],
        out_shape=[jax.ShapeDtypeStruct((t, D), F32), jax.ShapeDtypeStruct((t, D), F32), jax.ShapeDtypeStruct((D, D), BF16),
                   jax.ShapeDtypeStruct((1, D), F32), jax.ShapeDtypeStruct((1, D), F32)],
        scratch_shapes=[pltpu.VMEM((D, D), F32)],
        compiler_params=_cp("arbitrary"),
    )(*extra, dy, r, gamma, mixcat, w_out)


def proj_bwd(dproj, hb, w_in):
    t, n = dproj.shape
    tm = _row_tile(t)
    nt = t // tm

    def body(dp_ref, h_ref, w_ref, dh_ref, dw_ref, acc):
        i = pl.program_id(0)
        dp = dp_ref[...]
        dh_ref[...] = _nt(dp, w_ref[...])
        dw = _tn(h_ref[...], dp)

        @pl.when(i == 0)
        def _():
            acc[...] = dw

        @pl.when(i > 0)
        def _():
            acc[...] += dw

        @pl.when(i == nt - 1)
        def _():
            dw_ref[...] = acc[...].astype(BF16)

    mat = pl.BlockSpec((D, n), lambda i: (0, 0))
    return pl.pallas_call(
        body, name="proj_bwd", grid=(nt,),
        in_specs=[pl.BlockSpec((tm, n), lambda i: (i, 0)), pl.BlockSpec((tm, D), lambda i: (i, 0)), mat],
        out_specs=[pl.BlockSpec((tm, D), lambda i: (i, 0)), mat],
        out_shape=[jax.ShapeDtypeStruct((t, D), F32), jax.ShapeDtypeStruct((D, n), BF16)],
        scratch_shapes=[pltpu.VMEM((D, n), F32)],
        compiler_params=_cp("arbitrary"),
    )(dproj, hb, w_in)


def loss_head(r, ln, target):
    t = r.shape[0]
    nb = t // BLK

    def body(r_ref, g_ref, b_ref, t_ref, dy_ref, l_ref):
        i = pl.program_id(0)

        @pl.when(i == 0)
        def _():
            dy_ref[...] = jnp.zeros_like(dy_ref)
            l_ref[...] = jnp.zeros_like(l_ref)

        @pl.when(i > 0)
        def _():
            err = _layer_norm(r_ref[...], g_ref[...], b_ref[...]) - t_ref[...]
            dy_ref[...] = err * (1.0 / D)
            l_ref[...] += (0.5 / D) * jnp.sum(err * err, keepdims=True)

    vec = pl.BlockSpec((1, D), lambda i: (0, 0))
    return pl.pallas_call(
        body, name="loss_head", grid=(nb,),
        in_specs=[pl.BlockSpec((BLK, D), lambda i: (i, 0)), vec, vec,
                  pl.BlockSpec((BLK, D), lambda i: (jnp.maximum(i - 1, 0), 0))],
        out_specs=[pl.BlockSpec((BLK, D), lambda i: (i, 0)), pl.BlockSpec((1, 1), lambda i: (0, 0))],
        out_shape=[jax.ShapeDtypeStruct((t, D), F32), jax.ShapeDtypeStruct((1, 1), F32)],
        compiler_params=_cp("arbitrary"),
    )(r, ln[0], ln[1], target)


def split_dh0(dh0, after=None):
    t = dh0.shape[0]
    nb = t // BLK
    extra = [] if after is None else [after]

    def body(*refs):
        a_ref, gx_ref, gm_ref = refs[len(extra):]
        i = pl.program_id(0)
        tot = a_ref[...]

        @pl.when(i == 0)
        def _():
            gm_ref[...] = tot[PAD:, :]

        @pl.when(i > 0)
        def _():
            gx_ref[...] = tot

    blk = pl.BlockSpec((BLK, D), lambda i: (i, 0))
    return pl.pallas_call(
        body, name="split_dh0", grid=(nb,),
        in_specs=[_ANY] * len(extra) + [blk],
        out_specs=[pl.BlockSpec((BLK, D), lambda i: (jnp.maximum(i - 1, 0), 0)),
                   pl.BlockSpec((N_META, D), lambda i: (0, 0))],
        out_shape=[jax.ShapeDtypeStruct((t - BLK, D), F32), jax.ShapeDtypeStruct((N_META, D), F32)],
        compiler_params=_cp("arbitrary"),
    )(*extra, dh0)


def _valid_rows(nrows, first_row):
    return (first_row + lax.broadcasted_iota(jnp.int32, (nrows, 1), 0)) >= PAD


def conv_fwd(proj, conv_w, conv_b):
    t = proj.shape[0]
    c0 = C_XBC // BLK

    def body(x_ref, w_ref, b_ref, o_ref):
        ok = _valid_rows(t, 0)
        x = jnp.where(ok, x_ref[...], 0.0)
        w = w_ref[...]
        acc = b_ref[...] + w[CONV_K - 1:CONV_K, :] * x
        for s in range(1, CONV_K):
            acc += w[CONV_K - 1 - s:CONV_K - s, :] * pltpu.roll(x, s, 0)
        o_ref[...] = jnp.where(ok, acc * _sig(acc), 0.0)

    return pl.pallas_call(
        body, name="conv_fwd", grid=(CONV_D // BLK,),
        in_specs=[pl.BlockSpec((t, BLK), lambda j: (0, c0 + j)),
                  pl.BlockSpec((CONV_K, BLK), lambda j: (0, j)), pl.BlockSpec((1, BLK), lambda j: (0, j))],
        out_specs=pl.BlockSpec((t, BLK), lambda j: (0, j)),
        out_shape=jax.ShapeDtypeStruct((t, CONV_D), F32),
        compiler_params=_cp("arbitrary"),
    )(proj, conv_w, conv_b)


def conv_bwd(dxa, proj, conv_w, conv_b):
    t = proj.shape[0]
    c0 = C_XBC // BLK

    def body(d_ref, x_ref, w_ref, b_ref, dx_ref, dw_ref, db_ref):
        ok = _valid_rows(t, 0)
        x = jnp.where(ok, x_ref[...], 0.0)
        w = w_ref[...]
        xs = [x] + [pltpu.roll(x, s, 0) for s in range(1, CONV_K)]
        acc = b_ref[...] + w[CONV_K - 1:CONV_K, :] * x
        for s in range(1, CONV_K):
            acc += w[CONV_K - 1 - s:CONV_K - s, :] * xs[s]
        sg = _sig(acc)
        dxc = jnp.where(ok, d_ref[...] * (sg * (1.0 + acc * (1.0 - sg))), 0.0)
        db_ref[...] = jnp.sum(dxc, axis=0, keepdims=True)
        dw_ref[...] = jnp.concatenate(
            [jnp.sum(dxc * xs[CONV_K - 1 - k], axis=0, keepdims=True) for k in range(CONV_K)], axis=0)
        dx = w[CONV_K - 1:CONV_K, :] * dxc
        for s in range(1, CONV_K):
            dx += w[CONV_K - 1 - s:CONV_K - s, :] * pltpu.roll(dxc, t - s, 0)
        dx_ref[...] = jnp.where(ok, dx, 0.0)

    col = pl.BlockSpec((t, BLK), lambda j: (0, j))
    return pl.pallas_call(
        body, name="conv_bwd", grid=(CONV_D // BLK,),
        in_specs=[col, pl.BlockSpec((t, BLK), lambda j: (0, c0 + j)),
                  pl.BlockSpec((CONV_K, BLK), lambda j: (0, j)), pl.BlockSpec((1, BLK), lambda j: (0, j))],
        out_specs=[col, pl.BlockSpec((CONV_K, BLK), lambda j: (0, j)), pl.BlockSpec((1, BLK), lambda j: (0, j))],
        out_shape=[jax.ShapeDtypeStruct((t, CONV_D), F32), jax.ShapeDtypeStruct((CONV_K, CONV_D), F32),
                   jax.ShapeDtypeStruct((1, CONV_D), F32)],
        compiler_params=_cp("arbitrary"),
    )(dxa, proj, conv_w, conv_b)


def _softplus(x):
    return jnp.maximum(x, 0.0) + jnp.log(1.0 + jnp.exp(-jnp.abs(x)))


GW = SSD_D // SSD_G
HPG = SSD_H // SSD_G


def _head_expand():
    r = lax.broadcasted_iota(jnp.int32, (BLK, SSD_D), 0)
    c = lax.broadcasted_iota(jnp.int32, (BLK, SSD_D), 1)
    rt = lax.broadcasted_iota(jnp.int32, (SSD_D, BLK), 0)
    ct = lax.broadcasted_iota(jnp.int32, (SSD_D, BLK), 1)
    return (c // SSD_P == r).astype(F32), (rt // SSD_P == ct).astype(F32)


def _ssd_chunk(xa, sm, dtb, alog, dskip, ok, sp):
    e, et = _head_expand()
    dt = jnp.where(ok, _softplus(sm + dtb), 0.0)
    amat = -jnp.exp(alog)
    tri = _tri()
    ac = _nn_hi(tri.astype(F32), dt * amat)
    act = ac.T
    ace, dte, dse = _nn_hi(ac, e), _nn_hi(dt, e), _nn_hi(dskip, e)
    laste = ace[BLK - 1:BLK, :]
    ee, dece, gle = jnp.exp(ace), jnp.exp(laste - ace), jnp.exp(laste)
    xs = xa[:, :SSD_D]
    xdt = xs * dte
    decx = dece * xdt
    xdtb = xdt.astype(BF16)
    d = dict(e=e, et=et, dt=dt, amat=amat, tri=tri, ac=ac, act=act, dte=dte, dse=dse, ee=ee, dece=dece, gle=gle, xs=xs,
             xdt=xdt, xdtb=xdtb, decx=decx, bg=[], cg=[], cb=[], yo=[], seg=[], m=[], new_s=[])
    ys = []
    for g in range(SSD_G):
        cols = slice(GW * g, GW * (g + 1))
        bg = xa[:, SSD_D + SSD_N * g:SSD_D + SSD_N * (g + 1)].astype(BF16)
        cg = xa[:, SSD_D + SSD_G * SSD_N + SSD_N * g:SSD_D + SSD_G * SSD_N + SSD_N * (g + 1)].astype(BF16)
        spg = sp[:, cols]
        sloc = _tn(bg, decx[:, cols].astype(BF16))
        yo = _nn(cg, spg.astype(BF16)) * ee[:, cols]
        cb = _nt(cg, bg)
        d["new_s"].append(gle[:, cols] * spg + sloc)
        yds = []
        for h in range(HPG * g, HPG * (g + 1)):
            seg = jnp.where(tri, jnp.exp(jnp.minimum(ac[:, h:h + 1] - act[h:h + 1, :], 0.0)), 0.0)
            m = cb * seg
            yds.append(_nn(m.astype(BF16), xdtb[:, SSD_P * h:SSD_P * (h + 1)]))
            d["seg"].append(seg)
            d["m"].append(m)
        ys.append(jnp.concatenate(yds, axis=1) + yo)
        for k, val in (("bg", bg), ("cg", cg), ("cb", cb), ("yo", yo)):
            d[k].append(val)
    d["y"] = jnp.concatenate(ys, axis=1) + dse * xs
    return d


def ssd_fwd(xa, proj, dtb, alog, dskip, normg):
    t = xa.shape[0]
    nb = t // BLK
    gw = SSD_D // SSD_G

    def body(xa_ref, z_ref, sm_ref, dtb_ref, al_ref, ds_ref, ng_ref, y_ref, sp_ref, st):
        c = pl.program_id(0)

        @pl.when(c == 0)
        def _():
            st[...] = jnp.zeros_like(st)

        ok = _valid_rows(BLK, c * BLK)
        sp = st[...]
        sp_ref[...] = sp
        d = _ssd_chunk(xa_ref[...], sm_ref[...], dtb_ref[...], al_ref[...], ds_ref[...], ok, sp)
        st[...] = jnp.concatenate(d["new_s"], axis=1)
        y = d["y"]
        z = z_ref[...]
        yg = y * (z * _sig(z))
        outs = []
        for g in range(SSD_G):
            v = yg[:, gw * g:gw * (g + 1)]
            outs.append(v * lax.rsqrt(jnp.mean(v * v, axis=1, keepdims=True) + EPS))
        y_ref[...] = (jnp.concatenate(outs, axis=1) * ng_ref[...]).astype(BF16)

    vec = pl.BlockSpec((1, BLK), lambda c: (0, 0))
    return pl.pallas_call(
        body, name="ssd_fwd", grid=(nb,),
        in_specs=[pl.BlockSpec((BLK, CONV_D), lambda c: (c, 0)),
                  pl.BlockSpec((BLK, SSD_D), lambda c: (c, C_Z // SSD_D)),
                  pl.BlockSpec((BLK, BLK), lambda c: (c, C_SM // BLK)),
                  vec, vec, vec, pl.BlockSpec((1, SSD_D), lambda c: (0, 0))],
        out_specs=[pl.BlockSpec((BLK, SSD_D), lambda c: (c, 0)),
                   pl.BlockSpec((None, SSD_N, SSD_D), lambda c: (c, 0, 0))],
        out_shape=[jax.ShapeDtypeStruct((t, SSD_D), BF16), jax.ShapeDtypeStruct((nb, SSD_N, SSD_D), F32)],
        scratch_shapes=[pltpu.VMEM((SSD_N, SSD_D), F32)],
        compiler_params=_cp("arbitrary"),
    )(xa, proj, proj, dtb, alog, dskip, normg)


def _lane_put(col, lane):
    li = lax.broadcasted_iota(jnp.int32, (col.shape[0], BLK), 1)
    return jnp.where(li == lane, col, 0.0)


def ssd_bwd(dmix, xa, proj, sprev, dtb, alog, dskip, normg):
    t = xa.shape[0]
    nb = t // BLK
    gw = SSD_D // SSD_G
    rev = lambda c: nb - 1 - c

    def body(dy_ref, xa_ref, z_ref, sm_ref, sp_ref, dtb_ref, al_ref, ds_ref, ng_ref,
             dxa_ref, dz_ref, dsm_ref, dng_ref, dds_ref, dal_ref, ddtb_ref, dst):
        c = pl.program_id(0)

        @pl.when(c == 0)
        def _():
            dst[...] = jnp.zeros_like(dst)
            dng_ref[...] = jnp.zeros_like(dng_ref)
            dds_ref[...] = jnp.zeros_like(dds_ref)
            dal_ref[...] = jnp.zeros_like(dal_ref)
            ddtb_ref[...] = jnp.zeros_like(ddtb_ref)

        ok = _valid_rows(BLK, rev(c) * BLK)
        sm = sm_ref[...]
        sp = sp_ref[...]
        d = _ssd_chunk(xa_ref[...], sm, dtb_ref[...], al_ref[...], ds_ref[...], ok, sp)
        dt, amat, ac, act, tri, et, xs, xdt = (d[k] for k in ("dt", "amat", "ac", "act", "tri", "et", "xs", "xdt"))
        rowi = lax.broadcasted_iota(jnp.int32, (BLK, 1), 0)
        y = d["y"]
        z = z_ref[...]
        sgz = _sig(z)
        siluz = z * sgz
        yg = y * siluz
        dout = dy_ref[...]
        ng = ng_ref[...]
        dygs, xhs = [], []
        for g in range(SSD_G):
            v = yg[:, gw * g:gw * (g + 1)]
            rr = lax.rsqrt(jnp.mean(v * v, axis=1, keepdims=True) + EPS)
            xh = v * rr
            dxh = dout[:, gw * g:gw * (g + 1)] * ng[:, gw * g:gw * (g + 1)]
            dygs.append(rr * (dxh - xh * jnp.mean(dxh * xh, axis=1, keepdims=True)))
            xhs.append(xh)
        dyg = jnp.concatenate(dygs, axis=1)
        dng_ref[...] += jnp.sum(dout * jnp.concatenate(xhs, axis=1), axis=0, keepdims=True)
        dy = dyg * siluz
        dz_ref[...] = dyg * y * (sgz * (1.0 + z * (1.0 - sgz)))

        triu = _tri(lower=False)
        dyb = dy.astype(BF16)
        dsn = dst[...]
        dds_ref[...] += _nn_hi(jnp.sum(dy * xs, axis=0, keepdims=True), et)
        dac_all = _nn_hi(dy * jnp.concatenate(d["yo"], axis=1), et)
        dyo = (dy * d["ee"]).astype(BF16)
        gl = jnp.exp(ac[BLK - 1:BLK, :])
        dlast = _nn_hi(jnp.sum(dsn * sp, axis=0, keepdims=True), et) * gl
        bds, db_g, dc_g, dxdt_i, new_dst = [], [], [], [], []
        for g in range(SSD_G):
            cols = slice(GW * g, GW * (g + 1))
            bg, cg = d["bg"][g], d["cg"][g]
            dsng = dsn[:, cols].astype(BF16)
            dc = _nt(dyo[:, cols], sp[:, cols].astype(BF16))
            new_dst.append(_tn(cg, dyo[:, cols]) + d["gle"][:, cols] * dsn[:, cols])
            bds.append(_nn(bg, dsng))
            db = _nt(d["decx"][:, cols].astype(BF16), dsng)
            cbt = _nt(bg, cg)
            dcb = jnp.zeros((BLK, BLK), F32)
            for h in range(HPG * g, HPG * (g + 1)):
                hc = slice(SSD_P * h, SSD_P * (h + 1))
                dm = _nt(dyb[:, hc], d["xdtb"][:, hc])
                dcb += dm * d["seg"][h]
                w = dm * d["m"][h]
                dac_all += _lane_put(jnp.sum(w, axis=1, keepdims=True) - jnp.sum(w.T, axis=1, keepdims=True), h)
                segt = jnp.where(triu, jnp.exp(jnp.minimum(act[h:h + 1, :] - ac[:, h:h + 1], 0.0)), 0.0)
                dxdt_i.append(_nn((cbt * segt).astype(BF16), dyb[:, hc]))
            dcbb = dcb.astype(BF16)
            dc_g.append(dc + _nn(dcbb, bg))
            db_g.append(db + _tn(dcbb, cg))
        dst[...] = jnp.concatenate(new_dst, axis=1)
        bds = jnp.concatenate(bds, axis=1)
        tdec = jnp.exp(ac[BLK - 1:BLK, :] - ac) * _nn_hi(xdt * bds, et)
        dlast += jnp.sum(tdec, axis=0, keepdims=True)
        dac_all += jnp.where(rowi == BLK - 1, dlast, 0.0) - tdec
        dxdt = d["dece"] * bds + jnp.concatenate(dxdt_i, axis=1)
        da = _nn_hi(triu.astype(F32), dac_all)
        ddt = _nn_hi(dxdt * xs, et) + da * amat
        dal_ref[...] += jnp.sum(da * dt, axis=0, keepdims=True) * amat
        ddtr = jnp.where(ok, ddt * _sig(sm + dtb_ref[...]), 0.0)
        ddtb_ref[...] += jnp.sum(ddtr, axis=0, keepdims=True)
        dsm_ref[...] = ddtr
        dxs = d["dse"] * dy + dxdt * d["dte"]
        dxa_ref[...] = jnp.where(ok, jnp.concatenate([dxs] + db_g + dc_g, axis=1), 0.0)

    vec = pl.BlockSpec((1, BLK), lambda c: (0, 0))
    nvec = pl.BlockSpec((1, SSD_D), lambda c: (0, 0))
    return pl.pallas_call(
        body, name="ssd_bwd", grid=(nb,),
        in_specs=[pl.BlockSpec((BLK, SSD_D), lambda c: (rev(c), 0)),
                  pl.BlockSpec((BLK, CONV_D), lambda c: (rev(c), 0)),
                  pl.BlockSpec((BLK, SSD_D), lambda c: (rev(c), C_Z // SSD_D)),
                  pl.BlockSpec((BLK, BLK), lambda c: (rev(c), C_SM // BLK)),
                  pl.BlockSpec((None, SSD_N, SSD_D), lambda c: (rev(c), 0, 0)),
                  vec, vec, vec, nvec],
        out_specs=[pl.BlockSpec((BLK, CONV_D), lambda c: (rev(c), 0)),
                   pl.BlockSpec((BLK, SSD_D), lambda c: (rev(c), 0)),
                   pl.BlockSpec((BLK, BLK), lambda c: (rev(c), 0)),
                   nvec, vec, vec, vec],
        out_shape=[jax.ShapeDtypeStruct((t, CONV_D), F32), jax.ShapeDtypeStruct((t, SSD_D), F32),
                   jax.ShapeDtypeStruct((t, BLK), F32), jax.ShapeDtypeStruct((1, SSD_D), F32),
                   jax.ShapeDtypeStruct((1, BLK), F32), jax.ShapeDtypeStruct((1, BLK), F32),
                   jax.ShapeDtypeStruct((1, BLK), F32)],
        scratch_shapes=[pltpu.VMEM((SSD_N, SSD_D), F32)],
        compiler_params=_cp("arbitrary"),
    )(dmix, xa, proj, proj, sprev, dtb, alog, dskip, normg)


def _segments(nb, fine):
    if fine:
        cuts = list(range(0, nb, 2)) + [nb]
    else:
        cuts = sorted({0, nb} | {max(1, round(nb * f)) for f in (0.3, 0.53, 0.77)})
    return list(zip(cuts[:-1], cuts[1:]))


def attn_fwd(q, k, v, qcol, kcol, vcol, nh, dq, dv, scale, c_col=None, c_row=None, lane0=0):
    t = q.shape[0]
    tq = BLK
    use_bias = c_col is not None

    def body(*refs):
        if use_bias:
            q_ref, k_ref, v_ref, cc_ref, cr_ref, o_ref, l_ref = refs
        else:
            q_ref, k_ref, v_ref, o_ref, l_ref = refs
        i = pl.program_id(0)
        rowg = i * tq + lax.broadcasted_iota(jnp.int32, (tq, 1), 0)

        def tile(tk):
            col = lax.broadcasted_iota(jnp.int32, (1, tk), 1)
            mask = (col <= rowg) & (col >= PAD)
            outs = []
            lse = jnp.zeros((tq, BLK), F32)
            for h in range(nh):
                s = _nt(q_ref[:, dq * h:dq * (h + 1)].astype(BF16), k_ref[0:tk, dq * h:dq * (h + 1)].astype(BF16)) * scale
                if use_bias:
                    s = s + (cc_ref[:, lane0 + h:lane0 + h + 1] - cr_ref[h:h + 1, 0:tk])
                s = jnp.where(mask, s, NEG)
                m = jnp.max(s, axis=1, keepdims=True)
                p = jnp.exp(s - m)
                l = jnp.sum(p, axis=1, keepdims=True)
                outs.append(_nn(p.astype(BF16), v_ref[0:tk, dv * h:dv * (h + 1)].astype(BF16)) / l)
                lse += _lane_put(m + jnp.log(l), h)
            o_ref[...] = jnp.concatenate(outs, axis=1).astype(BF16)
            l_ref[...] = lse.T[0:8, :]

        for t0, t1 in _segments(t // tq, True):
            pl.when((i >= t0) & (i < t1))(functools.partial(tile, t1 * BLK))

    in_specs = [pl.BlockSpec((tq, nh * dq), lambda i: (i, qcol)),
                pl.BlockSpec((t, nh * dq), lambda i: (0, kcol)),
                pl.BlockSpec((t, nh * dv), lambda i: (0, vcol))]
    args = [q, k, v]
    if use_bias:
        in_specs += [pl.BlockSpec((tq, BLK), lambda i: (i, 0)), pl.BlockSpec((8, t), lambda i: (0, 0))]
        args += [c_col, c_row]
    return pl.pallas_call(
        body, name="attn_fwd", grid=(t // tq,),
        in_specs=in_specs,
        out_specs=[pl.BlockSpec((tq, nh * dv), lambda i: (i, 0)), pl.BlockSpec((8, tq), lambda i: (0, i))],
        out_shape=[jax.ShapeDtypeStruct((t, nh * dv), BF16), jax.ShapeDtypeStruct((8, t), F32)],
        compiler_params=_cp("arbitrary"),
    )(*args)


def attn_bwd(q, k, v, do, lse_row, o, qcol, kcol, vcol, docol, ocol, nh, dq, dv, scale, c_col=None, c_row=None, lane0=0):
    t = q.shape[0]
    tq = BLK
    use_bias = c_col is not None
    nq = t // tq

    def body(*refs):
        if use_bias:
            (q_ref, k_ref, v_ref, do_ref, l_ref, o_ref, cc_ref, cr_ref, dq_ref, dk_ref, dv_ref, dcq_ref, dck_ref,
             kt, ckb, dacc) = refs
        else:
            q_ref, k_ref, v_ref, do_ref, l_ref, o_ref, dq_ref, dk_ref, dv_ref, kt = refs
        i = pl.program_id(0)

        @pl.when(i == 0)
        def _():
            kt[...] = k_ref[...].astype(BF16).T
            dk_ref[...] = jnp.zeros_like(dk_ref)
            dv_ref[...] = jnp.zeros_like(dv_ref)
            if use_bias:
                dacc[...] = jnp.zeros_like(dacc)
                for h in range(nh):
                    ckb[h] = jnp.broadcast_to(cc_ref[:, lane0 + h:lane0 + h + 1], (t, BLK))

        qry = i * tq + lax.broadcasted_iota(jnp.int32, (1, tq), 1)
        dot = (do_ref[...].astype(F32) * o_ref[...].astype(F32)).T

        def tile(tk):
            key = lax.broadcasted_iota(jnp.int32, (tk, 1), 0)
            mask = (key <= qry) & (key >= PAD)
            dqts, dcqs = [], []
            for h in range(nh):
                qh = q_ref[:, dq * h:dq * (h + 1)].astype(BF16)
                kh = k_ref[0:tk, dq * h:dq * (h + 1)].astype(BF16)
                vh = v_ref[0:tk, dv * h:dv * (h + 1)].astype(BF16)
                doh = do_ref[:, dv * h:dv * (h + 1)].astype(BF16)
                delta = jnp.sum(dot[dv * h:dv * (h + 1), :], axis=0, keepdims=True)
                st = _nt(kh, qh) * scale
                if use_bias:
                    st = st + (cr_ref[h:h + 1, :] - ckb[h, 0:tk, :])
                pt = jnp.exp(jnp.where(mask, st, NEG) - l_ref[h:h + 1, :])
                dst = pt * (_nt(vh, doh) - delta)
                dsb = dst.astype(BF16)
                dk_ref[0:tk, dq * h:dq * (h + 1)] += _nn(dsb, qh) * scale
                dv_ref[0:tk, dv * h:dv * (h + 1)] += _nn(pt.astype(BF16), doh)
                dqts.append(_nn(kt[dq * h:dq * (h + 1), 0:tk], dsb))
                if use_bias:
                    dcqs.append(jnp.sum(dst, axis=0, keepdims=True))
                    dacc[h, 0:tk, :] += dst
            dq_ref[...] = jnp.concatenate(dqts, axis=0).T * scale
            if use_bias:
                dcq_ref[...] = jnp.concatenate(dcqs + [jnp.zeros((8 - nh, tq), F32)], axis=0)

        for t0, t1 in _segments(nq, not use_bias):
            pl.when((i >= t0) & (i < t1))(functools.partial(tile, t1 * BLK))

        if use_bias:
            @pl.when(i == nq - 1)
            def _():
                lane = lax.broadcasted_iota(jnp.int32, (1, BLK), 1)
                tot = jnp.zeros((t, BLK), F32)
                for h in range(nh):
                    tot += jnp.where(lane == lane0 + h, jnp.sum(dacc[h], axis=1, keepdims=True), 0.0)
                dck_ref[...] = tot

    keys_q = pl.BlockSpec((t, nh * dq), lambda i: (0, 0))
    keys_v = pl.BlockSpec((t, nh * dv), lambda i: (0, 0))
    keys_c = pl.BlockSpec((t, BLK), lambda i: (0, 0))
    qrow = pl.BlockSpec((8, tq), lambda i: (0, i))
    in_specs = [pl.BlockSpec((tq, nh * dq), lambda i: (i, qcol)),
                pl.BlockSpec((t, nh * dq), lambda i: (0, kcol)),
                pl.BlockSpec((t, nh * dv), lambda i: (0, vcol)),
                pl.BlockSpec((tq, nh * dv), lambda i: (i, docol)),
                qrow,
                pl.BlockSpec((tq, nh * dv), lambda i: (i, ocol))]
    args = [q, k, v, do, lse_row, o]
    out_specs = [pl.BlockSpec((tq, nh * dq), lambda i: (i, 0)), keys_q, keys_v]
    out_shape = [jax.ShapeDtypeStruct((t, nh * dq), F32), jax.ShapeDtypeStruct((t, nh * dq), F32),
                 jax.ShapeDtypeStruct((t, nh * dv), F32)]
    scratch = [pltpu.VMEM((nh * dq, t), BF16)]
    if use_bias:
        in_specs += [keys_c, qrow]
        args += [c_col, c_row]
        out_specs += [qrow, keys_c]
        out_shape += [jax.ShapeDtypeStruct((8, t), F32), jax.ShapeDtypeStruct((t, BLK), F32)]
        scratch += [pltpu.VMEM((nh, t, BLK), F32), pltpu.VMEM((nh, t, BLK), F32)]
    return pl.pallas_call(
        body, name="attn_bwd", grid=(nq,),
        in_specs=in_specs, out_specs=out_specs, out_shape=out_shape, scratch_shapes=scratch,
        compiler_params=_cp("arbitrary"),
    )(*args)


def fox_pre(proj, fb):
    t = proj.shape[0]
    nb = t // BLK

    def body(sm_ref, fb_ref, c_ref, cr_ref):
        x = sm_ref[...] + fb_ref[...]
        lane = lax.broadcasted_iota(jnp.int32, (1, BLK), 1)
        keep = _valid_rows(t, 0) & (lane >= SM_F) & (lane < SM_F + FOX_H)
        logf = jnp.where(keep, jnp.minimum(x, 0.0) - jnp.log(1.0 + jnp.exp(-jnp.abs(x))), 0.0)
        tri = _tri().astype(F32)
        carry = jnp.zeros((1, BLK), F32)
        for b in range(nb):
            cb = _nn_hi(tri, logf[b * BLK:(b + 1) * BLK, :]) + carry
            c_ref[b * BLK:(b + 1) * BLK, :] = cb
            carry = cb[BLK - 1:BLK, :]
        cr_ref[...] = c_ref[...].T[SM_F:SM_F + 8, :]

    return pl.pallas_call(
        body, name="fox_pre", grid=(1,),
        in_specs=[pl.BlockSpec((t, BLK), lambda i: (0, C_SM // BLK)), pl.BlockSpec((1, BLK), lambda i: (0, 0))],
        out_specs=[pl.BlockSpec((t, BLK), lambda i: (0, 0)), pl.BlockSpec((8, t), lambda i: (0, 0))],
        out_shape=[jax.ShapeDtypeStruct((t, BLK), F32), jax.ShapeDtypeStruct((8, t), F32)],
        compiler_params=_cp("arbitrary"),
    )(proj, fb)


def fox_pre_bwd(dcq, dck, proj, fb, dsm_in):
    t = proj.shape[0]
    nb = t // BLK

    def body(dcq_ref, dck_ref, sm_ref, fb_ref, din_ref, dsm_ref, dfb_ref, scr):
        triu = _tri(lower=False).astype(F32)
        carry = jnp.zeros((1, BLK), F32)
        scr[...] = jnp.concatenate([jnp.zeros((SM_F, t), F32), dcq_ref[...], jnp.zeros((BLK - SM_F - 8, t), F32)], axis=0).T
        for b in range(nb - 1, -1, -1):
            blk = scr[b * BLK:(b + 1) * BLK, :] - dck_ref[b * BLK:(b + 1) * BLK, :]
            cb = _nn_hi(triu, blk) + carry
            scr[b * BLK:(b + 1) * BLK, :] = cb
            carry = cb[0:1, :]
        x = sm_ref[...] + fb_ref[...]
        lane = lax.broadcasted_iota(jnp.int32, (1, BLK), 1)
        keep = _valid_rows(t, 0) & (lane >= SM_F) & (lane < SM_F + FOX_H)
        df = jnp.where(keep, scr[...] * _sig(-x), 0.0)
        dfb_ref[...] = jnp.sum(df, axis=0, keepdims=True)
        dsm_ref[...] = din_ref[...] + df

    full = pl.BlockSpec((t, BLK), lambda i: (0, 0))
    return pl.pallas_call(
        body, name="fox_pre_bwd", grid=(1,),
        in_specs=[pl.BlockSpec((8, t), lambda i: (0, 0)), full,
                  pl.BlockSpec((t, BLK), lambda i: (0, C_SM // BLK)), pl.BlockSpec((1, BLK), lambda i: (0, 0)), full],
        out_specs=[full, pl.BlockSpec((1, BLK), lambda i: (0, 0))],
        out_shape=[jax.ShapeDtypeStruct((t, BLK), F32), jax.ShapeDtypeStruct((1, BLK), F32)],
        scratch_shapes=[pltpu.VMEM((t, BLK), F32)],
        compiler_params=_cp("arbitrary"),
    )(dcq, dck, proj, fb, dsm_in)


def _swap_rope(x):
    lane = lax.broadcasted_iota(jnp.int32, (1, BLK), 1)
    return jnp.where((lane >= SM_KR) & (lane < SM_KR + 16), pltpu.roll(x, BLK - 16, 1),
                     jnp.where((lane >= SM_KR + 16) & (lane < SM_KR + 32), pltpu.roll(x, 16, 1), 0.0))


def _rms(x, g):
    r = lax.rsqrt(jnp.mean(x * x, axis=1, keepdims=True) + EPS)
    return r, x * r


def mla_pre(proj, qg, kvg, wq, wk, wv, cosq, sinq):
    t = proj.shape[0]
    tm = _row_tile(t)

    def body(cq_ref, ckv_ref, sm_ref, qg_ref, kvg_ref, wq_ref, wk_ref, wv_ref, cos_ref, sin_ref,
             q_ref, k_ref, v_ref, cqn_ref, ckvn_ref):
        cs, sn = cos_ref[...], sin_ref[...]
        _, xh = _rms(cq_ref[...], None)
        cqn = (xh * qg_ref[...]).astype(BF16)
        cqn_ref[...] = cqn
        qraw = _nn(cqn, wq_ref[...])
        qs = []
        for h in range(MLA_H):
            hb = qraw[:, BLK * h:BLK * (h + 1)]
            qs.append(hb * cs + _swap_rope(hb) * sn)
        q_ref[...] = jnp.concatenate(qs, axis=1).astype(BF16)
        _, kh = _rms(ckv_ref[...], None)
        ckvn = (kh * kvg_ref[...]).astype(BF16)
        ckvn_ref[...] = ckvn
        kraw = _nn(ckvn, wk_ref[...])
        v_ref[...] = _nn(ckvn, wv_ref[...]).astype(BF16)
        lane = lax.broadcasted_iota(jnp.int32, (1, BLK), 1)
        kr = sm_ref[...]
        krr = jnp.where((lane >= SM_KR) & (lane < SM_KR + MLA_ROPE), kr * cs + _swap_rope(kr) * sn, 0.0)
        k_ref[...] = jnp.concatenate([kraw[:, BLK * h:BLK * (h + 1)] + krr for h in range(MLA_H)], axis=1).astype(BF16)

    def rows(w, cb):
        return pl.BlockSpec((tm, w), lambda i: (i, cb))

    def whole(a):
        return pl.BlockSpec(a.shape, lambda i: (0, 0))

    return pl.pallas_call(
        body, name="mla_pre", grid=(t // tm,),
        in_specs=[rows(MLA_QL, C_CQ // MLA_QL), rows(MLA_KVL, C_CKV // MLA_KVL), rows(BLK, C_SM // BLK),
                  whole(qg), whole(kvg), whole(wq), whole(wk), whole(wv), rows(BLK, 0), rows(BLK, 0)],
        out_specs=[rows(512, 0), rows(512, 0), rows(256, 0), rows(MLA_QL, 0), rows(MLA_KVL, 0)],
        out_shape=[jax.ShapeDtypeStruct((t, 512), BF16), jax.ShapeDtypeStruct((t, 512), BF16),
                   jax.ShapeDtypeStruct((t, 256), BF16), jax.ShapeDtypeStruct((t, MLA_QL), BF16),
                   jax.ShapeDtypeStruct((t, MLA_KVL), BF16)],
        compiler_params=_cp("arbitrary"),
    )(proj, proj, proj, qg, kvg, wq, wk, wv, cosq, sinq)


def mla_pre_bwd(dq, dk, dv, proj, cqn, ckvn, qg, kvg, wq, wk, wv, cosq, sinq, dsm_in):
    t = proj.shape[0]
    tm = _row_tile(t)

    def body(dq_ref, dk_ref, dv_ref, cq_ref, ckv_ref, cqn_ref, ckvn_ref, qg_ref, kvg_ref, wq_ref, wk_ref, wv_ref,
             cos_ref, sin_ref, din_ref, dcq_ref, dckv_ref, dsm_ref, dwq_ref, dwk_ref, dwv_ref, dqg_ref, dkvg_ref):
        i = pl.program_id(0)

        @pl.when(i == 0)
        def _():
            for r in (dwq_ref, dwk_ref, dwv_ref, dqg_ref, dkvg_ref):
                r[...] = jnp.zeros_like(r)

        cs, sn = cos_ref[...], sin_ref[...]
        lane = lax.broadcasted_iota(jnp.int32, (1, BLK), 1)

        def unrope(dy):
            return dy * cs + _swap_rope(dy * sn)

        dqp = jnp.concatenate([unrope(dq_ref[:, BLK * h:BLK * (h + 1)]) for h in range(MLA_H)], axis=1).astype(BF16)
        dwq_ref[...] += _tn(cqn_ref[...], dqp)
        dcqn = _nt(dqp, wq_ref[...])
        r, xh = _rms(cq_ref[...], None)
        dqg_ref[...] += jnp.sum(dcqn * xh, axis=0, keepdims=True)
        dxh = dcqn * qg_ref[...]
        dcq_ref[...] = r * (dxh - xh * jnp.mean(dxh * xh, axis=1, keepdims=True))

        dkn, dkr = [], jnp.zeros((tm, BLK), F32)
        for h in range(MLA_H):
            blk = dk_ref[:, BLK * h:BLK * (h + 1)]
            dkn.append(jnp.where(lane < MLA_NOPE, blk, 0.0))
            dkr += jnp.where((lane >= SM_KR) & (lane < SM_KR + MLA_ROPE), blk, 0.0)
        dknb = jnp.concatenate(dkn, axis=1).astype(BF16)
        dvb = dv_ref[...].astype(BF16)
        ckvn = ckvn_ref[...]
        dwk_ref[...] += _tn(ckvn, dknb)
        dwv_ref[...] += _tn(ckvn, dvb)
        dckvn = _nt(dknb, wk_ref[...]) + _nt(dvb, wv_ref[...])
        r2, kh = _rms(ckv_ref[...], None)
        dkvg_ref[...] += jnp.sum(dckvn * kh, axis=0, keepdims=True)
        dkh = dckvn * kvg_ref[...]
        dckv_ref[...] = r2 * (dkh - kh * jnp.mean(dkh * kh, axis=1, keepdims=True))
        dsm_ref[...] = din_ref[...] + jnp.where((lane >= SM_KR) & (lane < SM_KR + MLA_ROPE), unrope(dkr), 0.0)

    def rows(w, cb):
        return pl.BlockSpec((tm, w), lambda i: (i, cb))

    def whole(a):
        return pl.BlockSpec(a.shape, lambda i: (0, 0))

    def wshape(a):
        return jax.ShapeDtypeStruct(a.shape, F32)

    return pl.pallas_call(
        body, name="mla_pre_bwd", grid=(t // tm,),
        in_specs=[rows(512, 0), rows(512, 0), rows(256, 0), rows(MLA_QL, C_CQ // MLA_QL), rows(MLA_KVL, C_CKV // MLA_KVL),
                  rows(MLA_QL, 0), rows(MLA_KVL, 0), whole(qg), whole(kvg), whole(wq), whole(wk), whole(wv),
                  rows(BLK, 0), rows(BLK, 0), rows(BLK, 0)],
        out_specs=[rows(MLA_QL, 0), rows(MLA_KVL, 0), rows(BLK, 0), whole(wq), whole(wk), whole(wv), whole(qg), whole(kvg)],
        out_shape=[jax.ShapeDtypeStruct((t, MLA_QL), F32), jax.ShapeDtypeStruct((t, MLA_KVL), F32),
                   jax.ShapeDtypeStruct((t, BLK), F32), wshape(wq), wshape(wk), wshape(wv), wshape(qg), wshape(kvg)],
        compiler_params=_cp("arbitrary"),
    )(dq, dk, dv, proj, proj, cqn, ckvn, qg, kvg, wq, wk, wv, cosq, sinq, dsm_in)


def _slot_sum(me, own, recv_ref):
    gg = own.astype(F32)
    for s in range(N_DEV):
        gg = gg + jnp.where(me == s, 0.0, recv_ref[s].astype(F32))
    return gg


def adamw(w, m, v, g=None, recv=None, own=None, me_arr=None):
    shape = w.shape
    c = shape[-1]
    from_recv = recv is not None
    if not from_recv:
        me_arr = jnp.zeros((1,), jnp.int32)
    nl = len(recv) if from_recv else 1
    rws = w.size // c // nl
    tr = rws
    for d in (1024, 512, 352, 256, 128, 64, 32, 16, 8):
        if rws % d == 0 and d * c * 4 <= (2 << 20):
            tr = d
            break
    nt = rws // tr
    w2, m2, v2 = (a.reshape(nl, rws, c) for a in (w, m, v))
    if from_recv:
        gin = [a.reshape(N_DEV, rws, c) for a in list(recv) + list(own)]
    else:
        gin = [g.reshape(1, rws, c)]

    def body(me_ref, w_ref, m_ref, v_ref, *rest):
        g_refs, outs = rest[:len(gin)], rest[len(gin):]
        if from_recv:
            g_out, outs = outs[0], outs[1:]
            for li in range(nl):
                @pl.when(pl.program_id(0) == li)
                def _(li=li):
                    g_out[...] = _slot_sum(me_ref[0], g_refs[nl + li][...], g_refs[li])
            gg = g_out[...]
        else:
            gg = g_refs[0][...]
        d_ref, nm_ref, nv_ref = outs
        nm = B1 * m_ref[...] + (1.0 - B1) * gg
        nv = B2 * v_ref[...] + (1.0 - B2) * (gg * gg)
        mh = nm / (1.0 - B1 ** STEP)
        vh = nv / (1.0 - B2 ** STEP)
        d_ref[...] = -LR * (mh / (jnp.sqrt(vh) + AEPS) + WD * w_ref[...])
        nm_ref[...] = nm
        nv_ref[...] = nv

    row = pl.BlockSpec((None, tr, c), lambda l, i, me: (l, i, 0))
    if from_recv:
        gspecs = [pl.BlockSpec((N_DEV, tr, c), lambda l, i, me, li=li: (0, jnp.where(l == li, i, 0), 0))
                  for li in range(nl)]
        gspecs += [pl.BlockSpec((None, tr, c), lambda l, i, me, li=li: (me[0], jnp.where(l == li, i, 0), 0))
                   for li in range(nl)]
    else:
        gspecs = [row]
    nout = 4 if from_recv else 3
    outs = pl.pallas_call(
        body, name="adamw",
        grid_spec=pltpu.PrefetchScalarGridSpec(num_scalar_prefetch=1, grid=(nl, nt), in_specs=[row, row, row] + gspecs,
                                               out_specs=[row] * nout),
        out_shape=[jax.ShapeDtypeStruct((nl, rws, c), F32)] * nout,
        compiler_params=_cp("arbitrary", "arbitrary"),
    )(me_arr, w2, m2, v2, *gin)
    return tuple(o.reshape(shape) for o in outs)


def sum_slots(recv, own=None, me_arr=None):
    _, r, c = recv.shape
    if own is None:
        own, me_arr = recv, jnp.zeros((1,), jnp.int32)
        plain = True
    else:
        plain = False

    def body(me_ref, r_ref, own_ref, o_ref):
        if plain:
            gg = r_ref[0].astype(F32)
            for s in range(1, N_DEV):
                gg = gg + r_ref[s].astype(F32)
            o_ref[...] = gg
        else:
            o_ref[...] = _slot_sum(me_ref[0], own_ref[...], r_ref)

    return pl.pallas_call(
        body, name="sum_slots",
        grid_spec=pltpu.PrefetchScalarGridSpec(
            num_scalar_prefetch=1, grid=(1,),
            in_specs=[pl.BlockSpec((N_DEV, r, c), lambda i, me: (0, 0, 0)),
                      pl.BlockSpec((None, r, c), lambda i, me: (me[0], 0, 0))],
            out_specs=pl.BlockSpec((r, c), lambda i, me: (0, 0))),
        out_shape=jax.ShapeDtypeStruct((r, c), F32),
        compiler_params=_cp("arbitrary"),
    )(me_arr, recv, own)


_FLIPS = [(0, 0, 1), (0, 1, 0), (0, 1, 1), (1, 0, 0), (1, 0, 1), (1, 1, 0), (1, 1, 1)]
_ANY = pl.BlockSpec(memory_space=pl.ANY)


def _mesh_place():
    x, y, c = lax.axis_index("x"), lax.axis_index("y"), lax.axis_index("c")
    me = 4 * x + 2 * y + c
    peers = [((x + fx) % 2, (y + fy) % 2, (c + fc) % 2) for fx, fy, fc in _FLIPS]
    return me, peers


def place_own(src, l, dtype, me_arr):
    _, r, c = src.shape
    tr = r
    for d in (512, 352, 256, 128, 64, 32, 16, 8):
        if r % d == 0 and d * c * 4 <= (2 << 20):
            tr = d
            break

    def body(me_ref, s_ref, o_ref):
        o_ref[...] = s_ref[...].astype(dtype)

    return pl.pallas_call(
        body, name="place_own",
        grid_spec=pltpu.PrefetchScalarGridSpec(
            num_scalar_prefetch=1, grid=(r // tr,),
            in_specs=[pl.BlockSpec((None, tr, c), lambda i, me: (l, i, 0))],
            out_specs=pl.BlockSpec((None, tr, c), lambda i, me: (me[0], i, 0))),
        out_shape=jax.ShapeDtypeStruct((N_DEV, r, c), dtype),
        compiler_params=_cp("arbitrary"),
    )(me_arr, src)


_HBM = pl.BlockSpec(memory_space=pltpu.HBM)
_SEMS = pl.BlockSpec(memory_space=pltpu.SEMAPHORE)
_EFFECT = pltpu.SideEffectType.DATAFLOW_SIDE_EFFECTING


def exchange_start(mode, arrays, name, after=None):
    n = len(arrays)
    gather = mode == "gather"
    ns = 0 if gather else n
    zones = list(arrays) if gather else [lax.empty(a.shape, a.dtype) for a in arrays]
    ops = ([] if gather else list(arrays)) + zones
    extra = [] if after is None else [after]

    def body(*refs):
        srcs, lands = refs[:ns], refs[ns:ns + n]
        send_sems, recv_sems = refs[ns + n + len(extra)], refs[ns + n + len(extra) + 1]
        token = refs[-1]
        me, peers = _mesh_place()
        ids = [4 * p[0] + 2 * p[1] + p[2] for p in peers]
        for j in range(n):
            for k in range(N_DEV - 1):
                src = lands[j].at[me] if gather else srcs[j].at[ids[k]]
                pltpu.make_async_remote_copy(src_ref=src, dst_ref=lands[j].at[me],
                                             send_sem=send_sems.at[j * (N_DEV - 1) + k],
                                             recv_sem=recv_sems.at[j * (N_DEV - 1) + k], device_id=peers[k],
                                             device_id_type=pl.DeviceIdType.MESH).start()
        token[...] = jnp.zeros_like(token)

    nsem = n * (N_DEV - 1)
    res = pl.pallas_call(
        body, name=name,
        in_specs=[_HBM] * (ns + n) + [_ANY] * len(extra),
        out_specs=(_SEMS, _SEMS, *[_HBM] * (ns + n), pl.BlockSpec(memory_space=pltpu.VMEM)),
        out_shape=(pltpu.SemaphoreType.DMA((nsem,)), pltpu.SemaphoreType.DMA((nsem,)),
                   *[pltpu.HBM(a.shape, a.dtype) for a in ops], jax.ShapeDtypeStruct((8, BLK), F32)),
        input_output_aliases={i: 2 + i for i in range(ns + n)},
        compiler_params=pltpu.CompilerParams(has_side_effects=_EFFECT),
    )(*[pltpu.with_memory_space_constraint(a, pltpu.HBM) for a in ops], *extra)
    return dict(gather=gather, send=res[0], recv=res[1], srcs=list(res[2:2 + ns]), lands=list(res[2 + ns:2 + ns + n]),
                token=res[-1])


def exchange_wait(hd, idxs, name, after):
    gather = hd["gather"]
    n = len(idxs)
    ns = 0 if gather else n
    ops = ([] if gather else [hd["srcs"][j] for j in idxs]) + [hd["lands"][j] for j in idxs]

    def body(*refs):
        srcs, lands = refs[:ns], refs[ns:ns + n]
        send_sems, recv_sems = refs[ns + n], refs[ns + n + 1]
        me, peers = _mesh_place()
        ids = [4 * p[0] + 2 * p[1] + p[2] for p in peers]
        for p, j in enumerate(idxs):
            for k in range(N_DEV - 1):
                src = lands[p].at[me] if gather else srcs[p].at[ids[k]]
                cp = pltpu.make_async_remote_copy(src_ref=src, dst_ref=lands[p].at[ids[k]],
                                                  send_sem=send_sems.at[j * (N_DEV - 1) + k],
                                                  recv_sem=recv_sems.at[j * (N_DEV - 1) + k], device_id=peers[k],
                                                  device_id_type=pl.DeviceIdType.MESH)
                cp.wait_send()
                cp.wait_recv()

    res = pl.pallas_call(
        body, name=name,
        in_specs=[_HBM] * (ns + n) + [_SEMS, _SEMS, _ANY],
        out_specs=[_HBM] * (ns + n),
        out_shape=[pltpu.HBM(a.shape, a.dtype) for a in ops],
        input_output_aliases={i: i for i in range(ns + n)},
        compiler_params=pltpu.CompilerParams(has_side_effects=_EFFECT),
    )(*ops, hd["send"], hd["recv"], after)
    return list(res[:ns]), list(res[ns:])


def _chip_place():
    x, y, c = lax.axis_index("x"), lax.axis_index("y"), lax.axis_index("c")
    chips = [((x + 1) % 2, y), (x, (y + 1) % 2), ((x + 1) % 2, (y + 1) % 2)]
    ident = lambda p: 4 * p[0] + 2 * p[1] + p[2]
    return dict(me=4 * x + 2 * y + c, sib=(x, y, 1 - c), sib_id=4 * x + 2 * y + 1 - c,
                same=[(cx, cy, c) for cx, cy in chips], same_ids=[ident((cx, cy, c)) for cx, cy in chips],
                other_ids=[ident((cx, cy, 1 - c)) for cx, cy in chips])


def _remote(src, dst, send_sem, recv_sem, dev):
    return pltpu.make_async_remote_copy(src_ref=src, dst_ref=dst, send_sem=send_sem, recv_sem=recv_sem, device_id=dev,
                                        device_id_type=pl.DeviceIdType.MESH)


def gather_start(zones, name):
    n = len(zones)

    def body(*refs):
        lands, send_sems, recv_sems, token = refs[:n], refs[n], refs[n + 1], refs[-1]
        pc = _chip_place()
        for j in range(n):
            own = lands[j].at[pc["me"]]
            for k, dev in enumerate([pc["sib"]] + pc["same"]):
                _remote(own, own, send_sems.at[4 * j + k], recv_sems.at[4 * j + k], dev).start()
        token[...] = jnp.zeros_like(token)

    res = pl.pallas_call(
        body, name=name,
        in_specs=[_HBM] * n,
        out_specs=(_SEMS, _SEMS, *[_HBM] * n, pl.BlockSpec(memory_space=pltpu.VMEM)),
        out_shape=(pltpu.SemaphoreType.DMA((4 * n,)), pltpu.SemaphoreType.DMA((4 * n,)),
                   *[pltpu.HBM(a.shape, a.dtype) for a in zones], jax.ShapeDtypeStruct((8, BLK), F32)),
        input_output_aliases={i: 2 + i for i in range(n)},
        compiler_params=pltpu.CompilerParams(has_side_effects=_EFFECT),
    )(*[pltpu.with_memory_space_constraint(a, pltpu.HBM) for a in zones])
    return dict(send=res[0], recv=res[1], lands=list(res[2:2 + n]), token=res[-1])


def gather_relay(hd, idxs, name, after):
    n = len(idxs)

    def body(*refs):
        lands, send_sems, recv_sems = refs[:n], refs[n], refs[n + 1]
        fsend, frecv, token = refs[n + 3 + n], refs[n + 4 + n], refs[-1]
        pc = _chip_place()
        for p, j in enumerate(idxs):
            for k in range(3):
                _remote(lands[p].at[pc["me"]], lands[p].at[pc["same_ids"][k]], send_sems.at[4 * j + 1 + k],
                        recv_sems.at[4 * j + 1 + k], pc["same"][k]).wait_recv()
        for p in range(n):
            for k in range(3):
                got = lands[p].at[pc["same_ids"][k]]
                _remote(got, got, fsend.at[3 * p + k], frecv.at[3 * p + k], pc["sib"]).start()
        token[...] = jnp.zeros_like(token)

    ops = [hd["lands"][j] for j in idxs]
    res = pl.pallas_call(
        body, name=name,
        in_specs=[_HBM] * n + [_SEMS, _SEMS, _ANY],
        out_specs=(*[_HBM] * n, _SEMS, _SEMS, pl.BlockSpec(memory_space=pltpu.VMEM)),
        out_shape=(*[pltpu.HBM(a.shape, a.dtype) for a in ops], pltpu.SemaphoreType.DMA((3 * n,)),
                   pltpu.SemaphoreType.DMA((3 * n,)), jax.ShapeDtypeStruct((8, BLK), F32)),
        input_output_aliases={i: i for i in range(n)},
        compiler_params=pltpu.CompilerParams(has_side_effects=_EFFECT),
    )(*ops, hd["send"], hd["recv"], after)
    return dict(lands=list(res[:n]), fsend=res[n], frecv=res[n + 1], token=res[-1])


def gather_wait(hd, rl, idxs, name, after):
    n = len(idxs)

    def body(*refs):
        lands, send_sems, recv_sems, fsend, frecv = refs[:n], refs[n], refs[n + 1], refs[n + 2], refs[n + 3]
        pc = _chip_place()
        for p, j in enumerate(idxs):
            own = lands[p].at[pc["me"]]
            for k, dev in enumerate([pc["sib"]] + pc["same"]):
                _remote(own, own, send_sems.at[4 * j + k], recv_sems.at[4 * j + k], dev).wait_send()
            _remote(own, lands[p].at[pc["sib_id"]], send_sems.at[4 * j], recv_sems.at[4 * j], pc["sib"]).wait_recv()
            for k in range(3):
                cp = _remote(lands[p].at[pc["same_ids"][k]], lands[p].at[pc["other_ids"][k]], fsend.at[3 * p + k],
                             frecv.at[3 * p + k], pc["sib"])
                cp.wait_send()
                cp.wait_recv()

    res = pl.pallas_call(
        body, name=name,
        in_specs=[_HBM] * n + [_SEMS, _SEMS, _SEMS, _SEMS, _ANY],
        out_specs=[_HBM] * n,
        out_shape=[pltpu.HBM(a.shape, a.dtype) for a in rl["lands"]],
        input_output_aliases={i: i for i in range(n)},
        compiler_params=pltpu.CompilerParams(has_side_effects=_EFFECT),
    )(*rl["lands"], hd["send"], hd["recv"], rl["fsend"], rl["frecv"], after)
    return list(res)


def _pad_cols(a, n):
    return jnp.pad(a, ((0, 0),) * (a.ndim - 1) + ((0, n - a.shape[-1]),))


def w_in_to_padded(w):
    z = lambda n: jnp.zeros(w.shape[:-1] + (n,), w.dtype)
    return jnp.concatenate([
        w[..., 0:1280], w[..., 1288:2056], w[..., 2060:2316], w[..., 2316:2444],
        w[..., 1280:1288], w[..., 2056:2060], z(SM_KR - SM_F - FOX_H), w[..., 2444:2476], z(BLK - SM_KR - MLA_ROPE)], axis=-1)


def w_in_from_padded(g):
    s = C_SM
    return jnp.concatenate([
        g[..., 0:1280], g[..., s + SM_DT:s + SM_DT + 8], g[..., 1280:2048], g[..., s + SM_F:s + SM_F + 4],
        g[..., 2048:2304], g[..., 2304:2432], g[..., s + SM_KR:s + SM_KR + MLA_ROPE]], axis=-1)


def _unshard_cols(gth):
    n, r, c = gth.shape
    return jnp.transpose(gth, (1, 0, 2)).reshape(r, n * c)


def _shard_cols(full):
    r, nc = full.shape
    return jnp.transpose(full.reshape(r, N_DEV, nc // N_DEV), (1, 0, 2))


def mla_weights(uq_g, ukv_g):
    uq = _unshard_cols(uq_g)
    dqh = MLA_NOPE + MLA_ROPE
    wq = jnp.concatenate([_pad_cols(uq[:, dqh * h:dqh * (h + 1)], BLK) for h in range(MLA_H)], axis=1)
    wk = jnp.concatenate([_pad_cols(ukv_g[2 * h], BLK) for h in range(MLA_H)], axis=1)
    wv = jnp.concatenate([ukv_g[2 * h + 1] for h in range(MLA_H)], axis=1)
    return wq, wk, wv


def mla_weight_grads(dwq, dwk, dwv):
    dqh = MLA_NOPE + MLA_ROPE
    duq = _shard_cols(jnp.concatenate([dwq[:, BLK * h:BLK * h + dqh] for h in range(MLA_H)], axis=1))
    parts = []
    for h in range(MLA_H):
        parts += [dwk[:, BLK * h:BLK * h + MLA_NOPE], dwv[:, MLA_V * h:MLA_V * (h + 1)]]
    return duq, jnp.stack(parts, axis=0)


def rope_tables(t):
    pos = (jnp.arange(t, dtype=jnp.int32) - PAD).astype(F32)
    inv_freq = 1.0 / (10000.0 ** (jnp.arange(0, MLA_ROPE, 2, dtype=F32) / MLA_ROPE))
    ang = pos[:, None] * inv_freq[None, :]
    cos, sin = jnp.cos(ang), jnp.sin(ang)
    one, zero = jnp.ones((t, SM_KR), F32), jnp.zeros((t, SM_KR), F32)
    tail = BLK - SM_KR - MLA_ROPE
    cosq = jnp.concatenate([one, cos, cos, jnp.ones((t, tail), F32)], axis=1)
    sinq = jnp.concatenate([zero, -sin, sin, jnp.zeros((t, tail), F32)], axis=1)
    return cosq, sinq


def _lanes(v, off=0):
    return jnp.pad(v.astype(F32), (off, BLK - off - v.shape[0]))[None, :]


def layer_fwd(x, ln, hb, getw, tabs, ahead):
    sv = {"h0b": hb}
    def behind(vec, tok):
        return vec if tok is None else vec + 0.0 * tok[0:1, 0:1]

    W = dict(getw("ffn1", hb))
    ln1 = (behind(W["ln1_g"], ahead(0, "mix", hb, 1)), W["ln1_b"])
    u, v, r1, h1b = ffn_fwd_seq(x, ln, W["g1"], W["u1"], W["d1"], ln1)
    sv.update(u1=u, v1=v, r1=r1, h1b=h1b)
    W.update(getw("mix", h1b))
    ln2 = (W["ln2_g"], W["ln2_b"])
    proj = mm_nn(h1b, W["w_in"])
    xa = conv_fwd(proj, W["conv_w"], W["conv_b"])
    y_ssd, sprev = ssd_fwd(xa, proj, W["dtb"], W["alog"], W["dskip"], W["normg"])
    c_col, c_row = fox_pre(proj, W["fb"])
    y_fox, lse_f = attn_fwd(proj, proj, proj, C_FQ // 256, C_FK // 256, C_FV // 256, FOX_H, FOX_DH, FOX_DH,
                            FOX_DH ** -0.5, c_col, c_row, SM_F)
    q, k, vv, cqn, ckvn = mla_pre(proj, behind(W["qg"], ahead(0, "ffn2", y_fox)), W["kvg"], W["wq"], W["wk"], W["wv"], *tabs)
    y_mla, lse_m = attn_fwd(q, k, vv, 0, 0, 0, MLA_H, BLK, MLA_V, (MLA_NOPE + MLA_ROPE) ** -0.5)
    mixcat = jnp.concatenate([y_ssd, y_fox, y_mla], axis=1)
    r2, h2b = mm_res_ln(mixcat, W["w_out"], r1, ln1, ln2)
    sv.update(proj=proj, xa=xa, sprev=sprev, c_col=c_col, c_row=c_row, lse_f=lse_f, q=q, k=k, v=vv, cqn=cqn, ckvn=ckvn,
              lse_m=lse_m, mixcat=mixcat, r2=r2, h2b=h2b)
    W.update(getw("ffn2", h2b))
    ln3 = (behind(W["ln3_g"], ahead(1, "ffn1", h2b)), W["ln3_b"])
    u, v, r3, h3b = ffn_fwd_seq(r2, ln2, W["g2"], W["u2"], W["d2"], ln3)
    sv.update(u2=u, v2=v, r3=r3, W=W)
    return r3, ln3, h3b, sv


def ffn_bwd(parts, r, gamma, hb_in, u, v, wg, wu, wd, after=None):
    dh, dwg, dwu, dwd, dg, db = ffn_bwd_seq(parts, r, gamma, hb_in, u, v, wg, wu, wd, after)
    return dh, dict(d=dwd, g=dwg, u=dwu, ln_g=dg, ln_b=db)


def layer_bwd(parts, sv, emit, tabs, after):
    G = {}
    W = sv["W"]
    dh2, g2 = ffn_bwd(parts, sv["r3"], W["ln3_g"], sv["h2b"], sv["u2"], sv["v2"], W["g2"], W["u2"], W["d2"], after)
    G.update(g2=g2["g"], u2=g2["u"], d2=g2["d"], ln3_g=g2["ln_g"], ln3_b=g2["ln_b"])
    tok = emit("ffn2", G)
    dr2, dmc, G["w_out"], G["ln2_g"], G["ln2_b"] = oproj_bwd(dh2, sv["r2"], W["ln2_g"], sv["mixcat"], W["w_out"], tok)
    proj = sv["proj"]
    dxa, dz, dsm, G["normg"], G["dskip"], G["alog"], G["dtb"] = ssd_bwd(
        dmc, sv["xa"], proj, sv["sprev"], W["dtb"], W["alog"], W["dskip"], W["normg"])
    dxbc, G["conv_w"], G["conv_b"] = conv_bwd(dxa, proj, W["conv_w"], W["conv_b"])
    dfq, dfk, dfv, dcq, dck = attn_bwd(proj, proj, proj, dmc, sv["lse_f"], sv["mixcat"], C_FQ // 256, C_FK // 256,
                                       C_FV // 256, 2, 2, FOX_H, FOX_DH, FOX_DH, FOX_DH ** -0.5, sv["c_col"], sv["c_row"], SM_F)
    dsm, G["fb"] = fox_pre_bwd(dcq, dck, proj, W["fb"], dsm)
    dq, dk, dv = attn_bwd(sv["q"], sv["k"], sv["v"], dmc, sv["lse_m"], sv["mixcat"], 0, 0, 0, 3, 3, MLA_H, BLK, MLA_V,
                          (MLA_NOPE + MLA_ROPE) ** -0.5)
    dcql, dckv, dsm, G["wq"], G["wk"], G["wv"], G["qg"], G["kvg"] = mla_pre_bwd(
        dq, dk, dv, proj, sv["cqn"], sv["ckvn"], W["qg"], W["kvg"], W["wq"], W["wk"], W["wv"], *tabs, dsm)
    dproj = jnp.concatenate([dz, dxbc, dfq, dfk, dfv, dcql, dckv, dsm], axis=1).astype(BF16)
    dh1p, G["w_in"] = proj_bwd(dproj, sv["h1b"], W["w_in"])
    tok = emit("mix", G)
    dh0, g1 = ffn_bwd([(dr2, ALPHA), (dh1p, 1.0)], sv["r1"], W["ln1_g"], sv["h0b"], sv["u1"], sv["v1"],
                      W["g1"], W["u1"], W["d1"], tok)
    G.update(g1=g1["g"], u1=g1["u"], d1=g1["d"], ln1_g=g1["ln_g"], ln1_b=g1["ln_b"])
    tok = emit("ffn1", G)
    return [(dh0, 1.0)], G, tok


def local_step(x, target, meta_full, getw, emit, ahead=lambda l, stage, after, min_layer=0: None):
    t = x.shape[0] + BLK
    tabs = rope_tables(t)
    xr, hb = build_h0(meta_full, x)
    ln = None
    saved = []
    for l in range(NL):
        xr, ln, hb, sv = layer_fwd(xr, ln, hb, functools.partial(getw, l), tabs,
                                   lambda dl, stage, after, min_layer=0, l=l: ahead(l + dl, stage, after, min_layer))
        saved.append(sv)
    dy, loss = loss_head(xr, ln, target)
    parts = [(dy, 1.0)]
    grads = [None] * NL
    tok = None
    for l in range(NL - 1, -1, -1):
        parts, grads[l], tok = layer_bwd(parts, saved[l], functools.partial(emit, l), tabs, tok)
    gx, gmeta = split_dh0(parts[0][0], tok)
    return loss, gx, gmeta, grads


_SMALL = ["ln1_g", "ln1_b", "ln2_g", "ln2_b", "ln3_g", "ln3_b", "conv_b", "ssd_norm_g", "mla_q_norm_g",
          "mla_kv_norm_g", "dt_bias", "a_log", "d_skip", "fox_f_b"]
_SMALL_ROWS = 8
_BIG = ["ffn1_w_gate", "ffn1_w_up", "ffn1_w_down", "w_in", "conv_w", "mla_w_uq", "mla_w_ukv", "w_out",
        "ffn2_w_gate", "ffn2_w_up", "ffn2_w_down"]
_NAMES = ["meta", "ffn1_w_gate", "ffn1_w_up", "ffn1_w_down", "ln1_g", "ln1_b", "w_in", "conv_w", "conv_b", "dt_bias",
          "a_log", "d_skip", "ssd_norm_g", "fox_f_b", "mla_q_norm_g", "mla_w_uq", "mla_kv_norm_g", "mla_w_ukv", "w_out",
          "ln2_g", "ln2_b", "ffn2_w_gate", "ffn2_w_up", "ffn2_w_down", "ln3_g", "ln3_b"]


def pack_small(p):
    flat = jnp.concatenate([p[n].astype(F32) for n in _SMALL], axis=1)
    return _pad_cols(flat, _SMALL_ROWS * D).reshape(NL * _SMALL_ROWS, D)


def unpack_small(a, like):
    flat = a.reshape(NL, _SMALL_ROWS * D)
    out, at = {}, 0
    for n in _SMALL:
        out[n] = flat[:, at:at + like[n].shape[1]]
        at += like[n].shape[1]
    return out


_STAGES = {"ffn1": ["ffn1_w_gate", "ffn1_w_up", "ffn1_w_down"],
           "mix": ["w_in", "conv_w", "mla_w_uq", "mla_w_ukv", "w_out"],
           "ffn2": ["ffn2_w_gate", "ffn2_w_up", "ffn2_w_down"]}


_FFN_T = ("ffn1_w_gate", "ffn1_w_up", "ffn2_w_gate", "ffn2_w_up")


def stage_weights(l, stage, g, rep):
    if stage != "mix":
        i = stage[3]
        return {"g" + i: g[f"ffn{i}_w_gate"].reshape(D_FF, D), "u" + i: g[f"ffn{i}_w_up"].reshape(D_FF, D),
                "d" + i: g[f"ffn{i}_w_down"].reshape(D_FF, D),
                "ln1_g" if i == "1" else "ln3_g": rep["ln1_g" if i == "1" else "ln3_g"][l][None, :],
                "ln1_b" if i == "1" else "ln3_b": rep["ln1_b" if i == "1" else "ln3_b"][l][None, :]}
    W = {}
    W["w_in"] = g["w_in"].reshape(D, N_INP)
    W["w_out"] = g["w_out"].reshape(D, D)
    W["wq"], W["wk"], W["wv"] = mla_weights(g["mla_w_uq"], g["mla_w_ukv"])
    W["conv_w"] = _unshard_cols(g["conv_w"])
    for k in ("ln2_g", "ln2_b", "conv_b"):
        W[k] = rep[k][l][None, :]
    W["normg"] = rep["ssd_norm_g"][l][None, :]
    W["qg"] = rep["mla_q_norm_g"][l][None, :]
    W["kvg"] = rep["mla_kv_norm_g"][l][None, :]
    W["dtb"] = _lanes(rep["dt_bias"][l], SM_DT)
    W["alog"] = _lanes(rep["a_log"][l], SM_DT)
    W["dskip"] = _lanes(rep["d_skip"][l], SM_DT)
    W["fb"] = _lanes(rep["fox_f_b"][l], SM_F)
    return W


def small_grads(G):
    return {"ln1_g": G["ln1_g"][0], "ln1_b": G["ln1_b"][0], "ln2_g": G["ln2_g"][0], "ln2_b": G["ln2_b"][0],
            "ln3_g": G["ln3_g"][0], "ln3_b": G["ln3_b"][0], "conv_b": G["conv_b"][0], "ssd_norm_g": G["normg"][0],
            "mla_q_norm_g": G["qg"][0], "mla_kv_norm_g": G["kvg"][0], "dt_bias": G["dtb"][0, :SSD_H],
            "a_log": G["alog"][0, :SSD_H], "d_skip": G["dskip"][0, :SSD_H], "fox_f_b": G["fb"][0, SM_F:SM_F + FOX_H]}


def big_grads(G, stage):
    if stage != "mix":
        i = stage[-1]
        return {f"ffn{i}_w_{k}": G[k[0] + i].reshape(N_DEV, HS, D) for k in ("gate", "up", "down")}
    duq, dukv = mla_weight_grads(G["wq"], G["wk"], G["wv"])
    return {"w_in": G["w_in"].reshape(N_DEV, D // N_DEV, N_INP), "w_out": G["w_out"].reshape(N_DEV, D // N_DEV, D),
            "mla_w_uq": duq, "mla_w_ukv": dukv, "conv_w": _shard_cols(G["conv_w"])}


def kernel(x, meta, ffn1_w_gate, ffn1_w_up, ffn1_w_down, ln1_g, ln1_b, w_in, conv_w, conv_b, dt_bias, a_log, d_skip, ssd_norm_g, fox_f_b, mla_q_norm_g, mla_w_uq, mla_kv_norm_g, mla_w_ukv, w_out, ln2_g, ln2_b, ffn2_w_gate, ffn2_w_up, ffn2_w_down, ln3_g, ln3_b, loss_target, m_meta, m_ffn1_w_gate, m_ffn1_w_up, m_ffn1_w_down, m_ln1_g, m_ln1_b, m_w_in, m_conv_w, m_conv_b, m_dt_bias, m_a_log, m_d_skip, m_ssd_norm_g, m_fox_f_b, m_mla_q_norm_g, m_mla_w_uq, m_mla_kv_norm_g, m_mla_w_ukv, m_w_out, m_ln2_g, m_ln2_b, m_ffn2_w_gate, m_ffn2_w_up, m_ffn2_w_down, m_ln3_g, m_ln3_b, v_meta, v_ffn1_w_gate, v_ffn1_w_up, v_ffn1_w_down, v_ln1_g, v_ln1_b, v_w_in, v_conv_w, v_conv_b, v_dt_bias, v_a_log, v_d_skip, v_ssd_norm_g, v_fox_f_b, v_mla_q_norm_g, v_mla_w_uq, v_mla_kv_norm_g, v_mla_w_ukv, v_w_out, v_ln2_g, v_ln2_b, v_ffn2_w_gate, v_ffn2_w_up, v_ffn2_w_down, v_ln3_g, v_ln3_b):
    vals = (meta, ffn1_w_gate, ffn1_w_up, ffn1_w_down, ln1_g, ln1_b, w_in, conv_w, conv_b, dt_bias, a_log, d_skip, ssd_norm_g, fox_f_b, mla_q_norm_g, mla_w_uq, mla_kv_norm_g, mla_w_ukv, w_out, ln2_g, ln2_b, ffn2_w_gate, ffn2_w_up, ffn2_w_down, ln3_g, ln3_b)
    moms = (m_meta, m_ffn1_w_gate, m_ffn1_w_up, m_ffn1_w_down, m_ln1_g, m_ln1_b, m_w_in, m_conv_w, m_conv_b, m_dt_bias, m_a_log, m_d_skip, m_ssd_norm_g, m_fox_f_b, m_mla_q_norm_g, m_mla_w_uq, m_mla_kv_norm_g, m_mla_w_ukv, m_w_out, m_ln2_g, m_ln2_b, m_ffn2_w_gate, m_ffn2_w_up, m_ffn2_w_down, m_ln3_g, m_ln3_b)
    vars_ = (v_meta, v_ffn1_w_gate, v_ffn1_w_up, v_ffn1_w_down, v_ln1_g, v_ln1_b, v_w_in, v_conv_w, v_conv_b, v_dt_bias, v_a_log, v_d_skip, v_ssd_norm_g, v_fox_f_b, v_mla_q_norm_g, v_mla_w_uq, v_mla_kv_norm_g, v_mla_w_ukv, v_w_out, v_ln2_g, v_ln2_b, v_ffn2_w_gate, v_ffn2_w_up, v_ffn2_w_down, v_ln3_g, v_ln3_b)
    P = dict(zip(_NAMES, vals))
    M = dict(zip(_NAMES, moms))
    V = dict(zip(_NAMES, vars_))
    me = 4 * lax.axis_index("x") + 2 * lax.axis_index("y") + lax.axis_index("c")

    me_arr = me.astype(jnp.int32).reshape(1)
    for n in _FFN_T:
        P[n], M[n], V[n] = (jnp.swapaxes(a[n], 1, 2) for a in (P, M, V))
    src = dict(P)
    src["w_in"] = w_in_to_padded(P["w_in"])
    order = [("meta", 0)] + [(n, l) for l in range(NL) for names in _STAGES.values() for n in names]
    nfirst = 1 + len(_STAGES["ffn1"])

    def place(n, l):
        return place_own(P["meta"][None] if n == "meta" else src[n], l, F32 if n in ("meta", "conv_w") else BF16, me_arr)

    hg_first = gather_start([place(n, l) for n, l in order[:nfirst]], "gather_start_first")
    hg_rest = gather_start([place(n, l) for n, l in order[nfirst:]], "gather_start_rest")
    zone_of = {nl_: ((hg_first, i) if i < nfirst else (hg_rest, i - nfirst)) for i, nl_ in enumerate(order)}
    relays = {}

    def ahead(l, stage, after, min_layer=0):
        if not min_layer <= l < NL:
            return None
        if (l, stage) not in relays:
            zs = [zone_of[("meta", 0)]] if stage == "meta" else [zone_of[(n, l)] for n in _STAGES[stage]]
            hg, idxs = zs[0][0], [i for _, i in zs]
            relays[(l, stage)] = (hg, idxs, gather_relay(hg, idxs, f"gather_relay_{l}_{stage}", after))
        return relays[(l, stage)][2]["token"]

    def arrived(l, stage, after):
        ahead(l, stage, after)
        hg, idxs, rl = relays[(l, stage)]
        return gather_wait(hg, rl, idxs, f"gather_wait_{l}_{stage}", after)

    meta_full = _unshard_cols(arrived(0, "meta", hg_rest["token"])[0])

    def getw(l, stage, after):
        return stage_weights(l, stage, dict(zip(_STAGES[stage], arrived(l, stage, after))), P)

    sent = {}

    def emit(l, stage, G):
        bg = big_grads(G, stage)
        sent[(l, stage)] = exchange_start("scatter", [bg[n] for n in _STAGES[stage]], f"scatter_start_{l}_{stage}")
        return sent[(l, stage)]["token"]

    loss, gx, gmeta, grads = local_step(x[0], loss_target[0], meta_full, getw, emit, ahead)

    small = jnp.concatenate([pack_small({n: jnp.stack([small_grads(g)[n] for g in grads]) for n in _SMALL}), gmeta,
                             jnp.pad(loss, ((0, 7), (0, D - 1)))], axis=0)
    hs = exchange_start("gather", [place_own(small[None], 0, F32, me_arr)], "small_start")

    out = {}
    after = hs["token"]
    for stage in ("ffn2", "mix", "ffn1"):
        names = _STAGES[stage]
        whole = [l for l in range(NL - 1, -1, -1) if (l, stage) != (0, "ffn1")]
        got = {l: exchange_wait(sent[(l, stage)], list(range(len(names))), f"scatter_wait_{l}_{stage}", after) for l in whole}
        for i, n in enumerate(names):
            one = {l: (got[l][0][i], got[l][1][i]) for l in whole}
            for l in set(range(NL)) - set(whole):
                s_, r_ = exchange_wait(sent[(l, stage)], [i], f"scatter_wait_{l}_{stage}_{i}", after)
                one[l] = (s_[0], r_[0])
            own = [one[l][0] for l in range(NL)]
            recv = [one[l][1] for l in range(NL)]
            if n == "w_in":
                g = jnp.stack([w_in_from_padded(sum_slots(recv[l], own[l], me_arr)) for l in range(NL)])
                out[n] = (g,) + adamw(P[n], M[n], V[n], g=g)
            else:
                out[n] = adamw(P[n], M[n], V[n], recv=recv, own=own, me_arr=me_arr)
                if n in _FFN_T:
                    out[n] = tuple(jnp.swapaxes(a, 1, 2) for a in out[n])
            after = out[n][1]
    gsmall = sum_slots(exchange_wait(hs, [0], "small_wait", after)[1][0])
    gm = lax.dynamic_slice(gsmall[NL * _SMALL_ROWS:], (0, me * (D // N_DEV)), (N_META, D // N_DEV))
    out["meta"] = (gm,) + adamw(P["meta"], M["meta"], V["meta"], g=gm)
    gs = gsmall[:NL * _SMALL_ROWS]
    sd, sm_, sv_ = adamw(pack_small(P), pack_small(M), pack_small(V), g=gs)
    ups = [unpack_small(a, P) for a in (gs, sd, sm_, sv_)]
    for n in _SMALL:
        out[n] = tuple(u[n] for u in ups)

    loss_all = gsmall[NL * _SMALL_ROWS + N_META, 0]
    flat = [loss_all, gx[None]]
    for k in range(4):
        flat += [out[n][k] for n in _NAMES]
    return tuple(flat)
```

```python
import functools

import jax
import jax.numpy as jnp
from jax import lax
from jax.experimental import pallas as pl
from jax.experimental.pallas import tpu as pltpu

F32, BF16 = jnp.float32, jnp.bfloat16
HI = lax.Precision.HIGHEST

N_DEV = 8
D = 1024
NL = 2
N_META = 16
BLK = 128
PAD = BLK - N_META
D_FF = 2816
HS = D_FF // N_DEV
SSD_H, SSD_P, SSD_N, SSD_G = 8, 64, 64, 2
SSD_D = SSD_H * SSD_P
CONV_K = 4
CONV_D = SSD_D + 2 * SSD_G * SSD_N
FOX_H, FOX_DH = 4, 64
MLA_H, MLA_QL, MLA_KVL, MLA_NOPE, MLA_ROPE, MLA_V = 4, 256, 128, 64, 32, 64
N_IN = 2476
C_Z, C_XBC, C_FQ, C_FK, C_FV, C_CQ, C_CKV, C_SM, N_INP = 0, 512, 1280, 1536, 1792, 2048, 2304, 2432, 2560
SM_DT, SM_F, SM_KR = 0, 8, 64
ALPHA = (2 * NL) ** 0.25
EPS = 1e-5
NEG = -1e30
LR, B1, B2, AEPS, WD, STEP = 0.001, 0.9, 0.999, 1e-08, 0.01, 10
VMEM_MB = 56


def _cp(*sem):
    return pltpu.CompilerParams(dimension_semantics=sem, vmem_limit_bytes=VMEM_MB << 20)


def _nn(a, b):
    return lax.dot_general(a, b, (((1,), (0,)), ((), ())), preferred_element_type=F32)


def _nt(a, b):
    return lax.dot_general(a, b, (((1,), (1,)), ((), ())), preferred_element_type=F32)


def _tn(a, b):
    return lax.dot_general(a, b, (((0,), (0,)), ((), ())), preferred_element_type=F32)


def _nn_hi(a, b):
    return lax.dot_general(a, b, (((1,), (0,)), ((), ())), precision=HI, preferred_element_type=F32)


def _row_tile(t):
    for d in range(640, 15, -16):
        if t % d == 0:
            return d
    raise ValueError(t)


def _sig(x):
    return 1.0 / (1.0 + jnp.exp(-x))


def _tri(lower=True):
    r = lax.broadcasted_iota(jnp.int32, (BLK, BLK), 0)
    c = lax.broadcasted_iota(jnp.int32, (BLK, BLK), 1)
    return (r >= c) if lower else (r <= c)


def build_h0(meta_full, x):
    s = x.shape[0]
    nb = s // BLK + 1

    def body(m_ref, x_ref, h_ref, hb_ref):
        i = pl.program_id(0)

        @pl.when(i == 0)
        def _():
            h = jnp.concatenate([jnp.zeros((PAD, D), F32), m_ref[...]], axis=0)
            h_ref[...] = h
            hb_ref[...] = h.astype(BF16)

        @pl.when(i > 0)
        def _():
            h_ref[...] = x_ref[...]
            hb_ref[...] = x_ref[...].astype(BF16)

    return pl.pallas_call(
        body, name="build_h0", grid=(nb,),
        in_specs=[pl.BlockSpec((N_META, D), lambda i: (0, 0)),
                  pl.BlockSpec((BLK, D), lambda i: (jnp.maximum(i - 1, 0), 0))],
        out_specs=[pl.BlockSpec((BLK, D), lambda i: (i, 0))] * 2,
        out_shape=[jax.ShapeDtypeStruct((nb * BLK, D), F32), jax.ShapeDtypeStruct((nb * BLK, D), BF16)],
        compiler_params=_cp("arbitrary"),
    )(meta_full, x)


FT = 256


def _layer_norm(r, gamma, beta):
    mu = jnp.mean(r, axis=1, keepdims=True)
    xc = r - mu
    var = jnp.mean(xc * xc, axis=1, keepdims=True)
    return xc * lax.rsqrt(var + EPS) * gamma + beta


def ffn_fwd_seq(x, ln_in, wg, wu, wd, ln_out):
    t = x.shape[0]
    f = wg.shape[0]
    nj, nr = f // FT, t // _row_tile(t)
    rc = t // nr
    plain = ln_in is None
    gi, bi = ln_out if plain else ln_in

    def body(x_hbm, gi_ref, bi_ref, go_ref, bo_ref, wg_ref, wu_ref, wd_ref, u_ref, v_ref, r_hbm, yb_hbm,
             acc, hbs, xbuf, sem_in, sem_out):
        j = pl.program_id(0)

        @pl.when(j == 0)
        def _():
            def fetch(k):
                return pltpu.make_async_copy(x_hbm.at[pl.ds(k * rc, rc)], xbuf.at[k % 2], sem_in.at[k % 2])

            fetch(0).start()
            for k in range(nr):
                if k + 1 < nr:
                    fetch(k + 1).start()
                fetch(k).wait()
                h = xbuf[k % 2]
                if not plain:
                    h = _layer_norm(h, gi_ref[...], bi_ref[...])
                acc[k * rc:(k + 1) * rc, :] = ALPHA * h
                hbs[k * rc:(k + 1) * rc, :] = h.astype(BF16)

        def chunk(k, last):
            sl = slice(k * rc, (k + 1) * rc)
            h = hbs[sl, :]
            u = _nt(h, wg_ref[...])
            v = _nt(h, wu_ref[...])
            u_ref[sl, :] = u.astype(BF16)
            v_ref[sl, :] = v.astype(BF16)
            acc[sl, :] += _nn((0.5 * u * _sig(u) * v).astype(BF16), wd_ref[...])
            if not last:
                return []
            rows = pl.ds(k * rc, rc)
            cps = [pltpu.make_async_copy(acc.at[rows], r_hbm.at[rows], sem_out.at[2 * k])]
            cps[0].start()
            hbs[sl, :] = _layer_norm(acc[sl, :], go_ref[...], bo_ref[...]).astype(BF16)
            cps.append(pltpu.make_async_copy(hbs.at[rows], yb_hbm.at[rows], sem_out.at[2 * k + 1]))
            cps[1].start()
            return cps

        @pl.when(j < nj - 1)
        def _():
            for k in range(nr):
                chunk(k, False)

        @pl.when(j == nj - 1)
        def _():
            cps = []
            for k in range(nr):
                cps += chunk(k, True)
            for cp in cps:
                cp.wait()

    vec = pl.BlockSpec((1, D), lambda j: (0, 0))
    wsp = pl.BlockSpec((FT, D), lambda j: (j, 0))
    act = pl.BlockSpec((None, t, FT), lambda j: (j, 0, 0))
    return pl.pallas_call(
        body, name="ffn_fwd_seq", grid=(nj,),
        in_specs=[_ANY, vec, vec, vec, vec, wsp, wsp, wsp],
        out_specs=[act, act, _ANY, _ANY],
        out_shape=[jax.ShapeDtypeStruct((nj, t, FT), BF16), jax.ShapeDtypeStruct((nj, t, FT), BF16),
                   jax.ShapeDtypeStruct((t, D), F32), jax.ShapeDtypeStruct((t, D), BF16)],
        scratch_shapes=[pltpu.VMEM((t, D), F32), pltpu.VMEM((t, D), BF16), pltpu.VMEM((2, rc, D), F32),
                        pltpu.SemaphoreType.DMA((2,)), pltpu.SemaphoreType.DMA((2 * nr,))],
        compiler_params=_cp("arbitrary"),
    )(x, gi, bi, ln_out[0], ln_out[1], wg, wu, wd)


def ffn_bwd_seq(parts, r, gamma, hb, u, v, wg, wu, wd, after=None):
    nj, t, _ = u.shape
    f = nj * FT
    nr = t // _row_tile(t)
    rc = t // nr
    nc = t // BLK
    scales = [s for _, s in parts]
    npart = len(parts)
    extra = [] if after is None else [after]

    def body(*refs):
        refs = refs[len(extra):]
        p_hbm, refs = refs[:npart], refs[npart:]
        (r_hbm, g_ref, hb_hbm, u_ref, v_ref, wg_ref, wu_ref, wd_ref, dh_hbm, dwg_ref, dwu_ref, dwd_ref, dg_ref, db_ref,
         dfs, hbt, dft, dhacc, dus, dvs, acs, pbuf, rbuf, hbuf, sems, sem_out) = refs
        j = pl.program_id(0)

        @pl.when(j == 0)
        def _():
            def fetch(c):
                rows = pl.ds(c * BLK, BLK)
                cps = [pltpu.make_async_copy(p_hbm[p].at[rows], pbuf.at[c % 2, p], sems.at[c % 2, p]) for p in range(npart)]
                cps.append(pltpu.make_async_copy(r_hbm.at[rows], rbuf.at[c % 2], sems.at[c % 2, npart]))
                cps.append(pltpu.make_async_copy(hb_hbm.at[rows], hbuf.at[c % 2], sems.at[c % 2, npart + 1]))
                return cps

            for cp in fetch(0):
                cp.start()
            dg = jnp.zeros((1, D), F32)
            db = jnp.zeros((1, D), F32)
            for c in range(nc):
                if c + 1 < nc:
                    for cp in fetch(c + 1):
                        cp.start()
                for cp in fetch(c):
                    cp.wait()
                sl = slice(c * BLK, (c + 1) * BLK)
                dy = scales[0] * pbuf[c % 2, 0]
                for p in range(1, npart):
                    dy += scales[p] * pbuf[c % 2, p]
                rr = rbuf[c % 2]
                xc = rr - jnp.mean(rr, axis=1, keepdims=True)
                rstd = lax.rsqrt(jnp.mean(xc * xc, axis=1, keepdims=True) + EPS)
                xh = xc * rstd
                dxh = dy * g_ref[...]
                dr = rstd * (dxh - jnp.mean(dxh, axis=1, keepdims=True) - xh * jnp.mean(dxh * xh, axis=1, keepdims=True))
                dg += jnp.sum(dy * xh, axis=0, keepdims=True)
                db += jnp.sum(dy, axis=0, keepdims=True)
                dhacc[sl, :] = ALPHA * dr
                dfc = (0.5 * dr).astype(BF16)
                dfs[sl, :] = dfc
                dft[:, sl] = dfc.T
                hbt[:, sl] = hbuf[c % 2].T
            dg_ref[...] = dg
            db_ref[...] = db

        for k in range(nr):
            sl = slice(k * rc, (k + 1) * rc)
            da = _nt(dfs[sl, :], wd_ref[...])
            uu = u_ref[sl, :].astype(F32)
            vv = v_ref[sl, :].astype(F32)
            sg = _sig(uu)
            du = (da * vv * (sg * (1.0 + uu * (1.0 - sg)))).astype(BF16)
            dv = (da * uu * sg).astype(BF16)
            dus[sl, :] = du
            dvs[sl, :] = dv
            acs[sl, :] = (uu * sg * vv).astype(BF16)
            dhacc[sl, :] += _nn(du, wg_ref[...]) + _nn(dv, wu_ref[...])
        @pl.when(j == nj - 1)
        def _():
            pltpu.make_async_copy(dhacc, dh_hbm, sem_out.at[0]).start()

        dwg_ref[...] = _nn(hbt[...], dus[...]).astype(BF16).T
        dwu_ref[...] = _nn(hbt[...], dvs[...]).astype(BF16).T
        dwd_ref[...] = _nn(dft[...], acs[...]).astype(BF16).T

        @pl.when(j == nj - 1)
        def _():
            pltpu.make_async_copy(dhacc, dh_hbm, sem_out.at[0]).wait()

    vec = pl.BlockSpec((1, D), lambda j: (0, 0))
    wsp = pl.BlockSpec((FT, D), lambda j: (j, 0))
    act = pl.BlockSpec((None, t, FT), lambda j: (j, 0, 0))
    return pl.pallas_call(
        body, name="ffn_bwd_seq", grid=(nj,),
        in_specs=[_ANY] * (len(extra) + npart + 1) + [vec, _ANY, act, act, wsp, wsp, wsp],
        out_specs=[_ANY, wsp, wsp, wsp, vec, vec],
        out_shape=[jax.ShapeDtypeStruct((t, D), F32)] + [jax.ShapeDtypeStruct((f, D), BF16)] * 3
        + [jax.ShapeDtypeStruct((1, D), F32)] * 2,
        scratch_shapes=[pltpu.VMEM((t, D), BF16), pltpu.VMEM((D, t), BF16), pltpu.VMEM((D, t), BF16),
                        pltpu.VMEM((t, D), F32), pltpu.VMEM((t, FT), BF16), pltpu.VMEM((t, FT), BF16),
                        pltpu.VMEM((t, FT), BF16), pltpu.VMEM((2, npart, BLK, D), F32), pltpu.VMEM((2, BLK, D), F32),
                        pltpu.VMEM((2, BLK, D), BF16), pltpu.SemaphoreType.DMA((2, npart + 2)),
                        pltpu.SemaphoreType.DMA((1,))],
        compiler_params=_cp("arbitrary"),
    )(*extra, *[p for p, _ in parts], r, gamma, hb, u, v, wg, wu, wd)


def mm_res_ln(a, b, x, ln_in, ln_out):
    t, k = a.shape
    tm = _row_tile(t)

    def body(a_ref, b_ref, x_ref, gi_ref, bi_ref, go_ref, bo_ref, r_ref, yb_ref):
        r = ALPHA * _layer_norm(x_ref[...], gi_ref[...], bi_ref[...]) + _nn(a_ref[...], b_ref[...])
        r_ref[...] = r
        yb_ref[...] = _layer_norm(r, go_ref[...], bo_ref[...]).astype(BF16)

    row = pl.BlockSpec((tm, D), lambda i: (i, 0))
    vec = pl.BlockSpec((1, D), lambda i: (0, 0))
    return pl.pallas_call(
        body, name="mm_res_ln", grid=(t // tm,),
        in_specs=[pl.BlockSpec((tm, k), lambda i: (i, 0)), pl.BlockSpec((k, D), lambda i: (0, 0)), row, vec, vec, vec, vec],
        out_specs=[row, row],
        out_shape=[jax.ShapeDtypeStruct((t, D), F32), jax.ShapeDtypeStruct((t, D), BF16)],
        compiler_params=_cp("arbitrary"),
    )(a, b, x, ln_in[0], ln_in[1], ln_out[0], ln_out[1])


def mm_nn(a, b):
    t, k = a.shape
    n = tn = b.shape[1]
    tm = _row_tile(t)

    def body(a_ref, b_ref, o_ref):
        o_ref[...] = _nn(a_ref[...], b_ref[...])

    return pl.pallas_call(
        body, name="mm_nn", grid=(t // tm, n // tn),
        in_specs=[pl.BlockSpec((tm, k), lambda i, j: (i, 0)), pl.BlockSpec((k, tn), lambda i, j: (0, j))],
        out_specs=pl.BlockSpec((tm, tn), lambda i, j: (i, j)),
        out_shape=jax.ShapeDtypeStruct((t, n), F32),
        compiler_params=_cp("arbitrary", "arbitrary"),
    )(a, b)


def oproj_bwd(dy, r, gamma, mixcat, w_out, after=None):
    t = r.shape[0]
    tm = _row_tile(t)
    nt = t // tm
    extra = [] if after is None else [after]

    def body(*refs):
        dy_ref, r_ref, g_ref, m_ref, w_ref, dr_ref, dm_ref, dw_ref, dg_ref, db_ref, acc = refs[len(extra):]
        i = pl.program_id(0)
        dy = dy_ref[...]
        rr = r_ref[...]
        xc = rr - jnp.mean(rr, axis=1, keepdims=True)
        rstd = lax.rsqrt(jnp.mean(xc * xc, axis=1, keepdims=True) + EPS)
        xh = xc * rstd
        dxh = dy * g_ref[...]
        dr = rstd * (dxh - jnp.mean(dxh, axis=1, keepdims=True) - xh * jnp.mean(dxh * xh, axis=1, keepdims=True))
        dr_ref[...] = dr
        drb = dr.astype(BF16)
        dm_ref[...] = _nt(drb, w_ref[...])
        dw = _tn(m_ref[...], drb)
        dg = jnp.sum(dy * xh, axis=0, keepdims=True)
        db = jnp.sum(dy, axis=0, keepdims=True)

        @pl.when(i == 0)
        def _():
            acc[...] = dw
            dg_ref[...] = dg
            db_ref[...] = db

        @pl.when(i > 0)
        def _():
            acc[...] += dw
            dg_ref[...] += dg
            db_ref[...] += db

        @pl.when(i == nt - 1)
        def _():
            dw_ref[...] = acc[...].astype(BF16)

    row = pl.BlockSpec((tm, D), lambda i: (i, 0))
    vec = pl.BlockSpec((1, D), lambda i: (0, 0))
    mat = pl.BlockSpec((D, D), lambda i: (0, 0))
    return pl.pallas_call(
        body, name="oproj_bwd", grid=(nt,),
        in_specs=[_ANY] * len(extra) + [row, row, vec, row, mat],
        out_specs=[row, row, mat, vec, vec],
        out_shape=[jax.ShapeDtypeStruct((t, D), F32), jax.ShapeDtypeStruct((t, D), F32), jax.ShapeDtypeStruct((D, D), BF16),
                   jax.ShapeDtypeStruct((1, D), F32), jax.ShapeDtypeStruct((1, D), F32)],
        scratch_shapes=[pltpu.VMEM((D, D), F32)],
        compiler_params=_cp("arbitrary"),
    )(*extra, dy, r, gamma, mixcat, w_out)


def proj_bwd(dproj, hb, w_in):
    t, n = dproj.shape
    tm = _row_tile(t)
    nt = t // tm

    def body(dp_ref, h_ref, w_ref, dh_ref, dw_ref, acc):
        i = pl.program_id(0)
        dp = dp_ref[...]
        dh_ref[...] = _nt(dp, w_ref[...])
        dw = _tn(h_ref[...], dp)

        @pl.when(i == 0)
        def _():
            acc[...] = dw

        @pl.when(i > 0)
        def _():
            acc[...] += dw

        @pl.when(i == nt - 1)
        def _():
            dw_ref[...] = acc[...].astype(BF16)

    mat = pl.BlockSpec((D, n), lambda i: (0, 0))
    return pl.pallas_call(
        body, name="proj_bwd", grid=(nt,),
        in_specs=[pl.BlockSpec((tm, n), lambda i: (i, 0)), pl.BlockSpec((tm, D), lambda i: (i, 0)), mat],
        out_specs=[pl.BlockSpec((tm, D), lambda i: (i, 0)), mat],
        out_shape=[jax.ShapeDtypeStruct((t, D), F32), jax.ShapeDtypeStruct((D, n), BF16)],
        scratch_shapes=[pltpu.VMEM((D, n), F32)],
        compiler_params=_cp("arbitrary"),
    )(dproj, hb, w_in)


def loss_head(r, ln, target):
    t = r.shape[0]
    nb = t // BLK

    def body(r_ref, g_ref, b_ref, t_ref, dy_ref, l_ref):
        i = pl.program_id(0)

        @pl.when(i == 0)
        def _():
            dy_ref[...] = jnp.zeros_like(dy_ref)
            l_ref[...] = jnp.zeros_like(l_ref)

        @pl.when(i > 0)
        def _():
            err = _layer_norm(r_ref[...], g_ref[...], b_ref[...]) - t_ref[...]
            dy_ref[...] = err * (1.0 / D)
            l_ref[...] += (0.5 / D) * jnp.sum(err * err, keepdims=True)

    vec = pl.BlockSpec((1, D), lambda i: (0, 0))
    return pl.pallas_call(
        body, name="loss_head", grid=(nb,),
        in_specs=[pl.BlockSpec((BLK, D), lambda i: (i, 0)), vec, vec,
                  pl.BlockSpec((BLK, D), lambda i: (jnp.maximum(i - 1, 0), 0))],
        out_specs=[pl.BlockSpec((BLK, D), lambda i: (i, 0)), pl.BlockSpec((1, 1), lambda i: (0, 0))],
        out_shape=[jax.ShapeDtypeStruct((t, D), F32), jax.ShapeDtypeStruct((1, 1), F32)],
        compiler_params=_cp("arbitrary"),
    )(r, ln[0], ln[1], target)


def split_dh0(dh0, after=None):
    t = dh0.shape[0]
    nb = t // BLK
    extra = [] if after is None else [after]

    def body(*refs):
        a_ref, gx_ref, gm_ref = refs[len(extra):]
        i = pl.program_id(0)
        tot = a_ref[...]

        @pl.when(i == 0)
        def _():
            gm_ref[...] = tot[PAD:, :]

        @pl.when(i > 0)
        def _():
            gx_ref[...] = tot

    blk = pl.BlockSpec((BLK, D), lambda i: (i, 0))
    return pl.pallas_call(
        body, name="split_dh0", grid=(nb,),
        in_specs=[_ANY] * len(extra) + [blk],
        out_specs=[pl.BlockSpec((BLK, D), lambda i: (jnp.maximum(i - 1, 0), 0)),
                   pl.BlockSpec((N_META, D), lambda i: (0, 0))],
        out_shape=[jax.ShapeDtypeStruct((t - BLK, D), F32), jax.ShapeDtypeStruct((N_META, D), F32)],
        compiler_params=_cp("arbitrary"),
    )(*extra, dh0)


def _valid_rows(nrows, first_row):
    return (first_row + lax.broadcasted_iota(jnp.int32, (nrows, 1), 0)) >= PAD


def conv_fwd(proj, conv_w, conv_b):
    t = proj.shape[0]
    c0 = C_XBC // BLK

    def body(x_ref, w_ref, b_ref, o_ref):
        ok = _valid_rows(t, 0)
        x = jnp.where(ok, x_ref[...], 0.0)
        w = w_ref[...]
        acc = b_ref[...] + w[CONV_K - 1:CONV_K, :] * x
        for s in range(1, CONV_K):
            acc += w[CONV_K - 1 - s:CONV_K - s, :] * pltpu.roll(x, s, 0)
        o_ref[...] = jnp.where(ok, acc * _sig(acc), 0.0)

    return pl.pallas_call(
        body, name="conv_fwd", grid=(CONV_D // BLK,),
        in_specs=[pl.BlockSpec((t, BLK), lambda j: (0, c0 + j)),
                  pl.BlockSpec((CONV_K, BLK), lambda j: (0, j)), pl.BlockSpec((1, BLK), lambda j: (0, j))],
        out_specs=pl.BlockSpec((t, BLK), lambda j: (0, j)),
        out_shape=jax.ShapeDtypeStruct((t, CONV_D), F32),
        compiler_params=_cp("arbitrary"),
    )(proj, conv_w, conv_b)


def conv_bwd(dxa, proj, conv_w, conv_b):
    t = proj.shape[0]
    c0 = C_XBC // BLK

    def body(d_ref, x_ref, w_ref, b_ref, dx_ref, dw_ref, db_ref):
        ok = _valid_rows(t, 0)
        x = jnp.where(ok, x_ref[...], 0.0)
        w = w_ref[...]
        xs = [x] + [pltpu.roll(x, s, 0) for s in range(1, CONV_K)]
        acc = b_ref[...] + w[CONV_K - 1:CONV_K, :] * x
        for s in range(1, CONV_K):
            acc += w[CONV_K - 1 - s:CONV_K - s, :] * xs[s]
        sg = _sig(acc)
        dxc = jnp.where(ok, d_ref[...] * (sg * (1.0 + acc * (1.0 - sg))), 0.0)
        db_ref[...] = jnp.sum(dxc, axis=0, keepdims=True)
        dw_ref[...] = jnp.concatenate(
            [jnp.sum(dxc * xs[CONV_K - 1 - k], axis=0, keepdims=True) for k in range(CONV_K)], axis=0)
        dx = w[CONV_K - 1:CONV_K, :] * dxc
        for s in range(1, CONV_K):
            dx += w[CONV_K - 1 - s:CONV_K - s, :] * pltpu.roll(dxc, t - s, 0)
        dx_ref[...] = jnp.where(ok, dx, 0.0)

    col = pl.BlockSpec((t, BLK), lambda j: (0, j))
    return pl.pallas_call(
        body, name="conv_bwd", grid=(CONV_D // BLK,),
        in_specs=[col, pl.BlockSpec((t, BLK), lambda j: (0, c0 + j)),
                  pl.BlockSpec((CONV_K, BLK), lambda j: (0, j)), pl.BlockSpec((1, BLK), lambda j: (0, j))],
        out_specs=[col, pl.BlockSpec((CONV_K, BLK), lambda j: (0, j)), pl.BlockSpec((1, BLK), lambda j: (0, j))],
        out_shape=[jax.ShapeDtypeStruct((t, CONV_D), F32), jax.ShapeDtypeStruct((CONV_K, CONV_D), F32),
                   jax.ShapeDtypeStruct((1, CONV_D), F32)],
        compiler_params=_cp("arbitrary"),
    )(dxa, proj, conv_w, conv_b)


def _softplus(x):
    return jnp.maximum(x, 0.0) + jnp.log(1.0 + jnp.exp(-jnp.abs(x)))


GW = SSD_D // SSD_G
HPG = SSD_H // SSD_G


def _head_expand():
    r = lax.broadcasted_iota(jnp.int32, (BLK, SSD_D), 0)
    c = lax.broadcasted_iota(jnp.int32, (BLK, SSD_D), 1)
    rt = lax.broadcasted_iota(jnp.int32, (SSD_D, BLK), 0)
    ct = lax.broadcasted_iota(jnp.int32, (SSD_D, BLK), 1)
    return (c // SSD_P == r).astype(F32), (rt // SSD_P == ct).astype(F32)


def _ssd_chunk(xa, sm, dtb, alog, dskip, ok, sp):
    e, et = _head_expand()
    dt = jnp.where(ok, _softplus(sm + dtb), 0.0)
    amat = -jnp.exp(alog)
    tri = _tri()
    ac = _nn_hi(tri.astype(F32), dt * amat)
    act = ac.T
    ace, dte, dse = _nn_hi(ac, e), _nn_hi(dt, e), _nn_hi(dskip, e)
    laste = ace[BLK - 1:BLK, :]
    ee, dece, gle = jnp.exp(ace), jnp.exp(laste - ace), jnp.exp(laste)
    xs = xa[:, :SSD_D]
    xdt = xs * dte
    decx = dece * xdt
    xdtb = xdt.astype(BF16)
    d = dict(e=e, et=et, dt=dt, amat=amat, tri=tri, ac=ac, act=act, dte=dte, dse=dse, ee=ee, dece=dece, gle=gle, xs=xs,
             xdt=xdt, xdtb=xdtb, decx=decx, bg=[], cg=[], cb=[], yo=[], seg=[], m=[], new_s=[])
    ys = []
    for g in range(SSD_G):
        cols = slice(GW * g, GW * (g + 1))
        bg = xa[:, SSD_D + SSD_N * g:SSD_D + SSD_N * (g + 1)].astype(BF16)
        cg = xa[:, SSD_D + SSD_G * SSD_N + SSD_N * g:SSD_D + SSD_G * SSD_N + SSD_N * (g + 1)].astype(BF16)
        spg = sp[:, cols]
        sloc = _tn(bg, decx[:, cols].astype(BF16))
        yo = _nn(cg, spg.astype(BF16)) * ee[:, cols]
        cb = _nt(cg, bg)
        d["new_s"].append(gle[:, cols] * spg + sloc)
        yds = []
        for h in range(HPG * g, HPG * (g + 1)):
            seg = jnp.where(tri, jnp.exp(jnp.minimum(ac[:, h:h + 1] - act[h:h + 1, :], 0.0)), 0.0)
            m = cb * seg
            yds.append(_nn(m.astype(BF16), xdtb[:, SSD_P * h:SSD_P * (h + 1)]))
            d["seg"].append(seg)
            d["m"].append(m)
        ys.append(jnp.concatenate(yds, axis=1) + yo)
        for k, val in (("bg", bg), ("cg", cg), ("cb", cb), ("yo", yo)):
            d[k].append(val)
    d["y"] = jnp.concatenate(ys, axis=1) + dse * xs
    return d


def ssd_fwd(xa, proj, dtb, alog, dskip, normg):
    t = xa.shape[0]
    nb = t // BLK
    gw = SSD_D // SSD_G

    def body(xa_ref, z_ref, sm_ref, dtb_ref, al_ref, ds_ref, ng_ref, y_ref, sp_ref, st):
        c = pl.program_id(0)

        @pl.when(c == 0)
        def _():
            st[...] = jnp.zeros_like(st)

        ok = _valid_rows(BLK, c * BLK)
        sp = st[...]
        sp_ref[...] = sp
        d = _ssd_chunk(xa_ref[...], sm_ref[...], dtb_ref[...], al_ref[...], ds_ref[...], ok, sp)
        st[...] = jnp.concatenate(d["new_s"], axis=1)
        y = d["y"]
        z = z_ref[...]
        yg = y * (z * _sig(z))
        outs = []
        for g in range(SSD_G):
            v = yg[:, gw * g:gw * (g + 1)]
            outs.append(v * lax.rsqrt(jnp.mean(v * v, axis=1, keepdims=True) + EPS))
        y_ref[...] = (jnp.concatenate(outs, axis=1) * ng_ref[...]).astype(BF16)

    vec = pl.BlockSpec((1, BLK), lambda c: (0, 0))
    return pl.pallas_call(
        body, name="ssd_fwd", grid=(nb,),
        in_specs=[pl.BlockSpec((BLK, CONV_D), lambda c: (c, 0)),
                  pl.BlockSpec((BLK, SSD_D), lambda c: (c, C_Z // SSD_D)),
                  pl.BlockSpec((BLK, BLK), lambda c: (c, C_SM // BLK)),
                  vec, vec, vec, pl.BlockSpec((1, SSD_D), lambda c: (0, 0))],
        out_specs=[pl.BlockSpec((BLK, SSD_D), lambda c: (c, 0)),
                   pl.BlockSpec((None, SSD_N, SSD_D), lambda c: (c, 0, 0))],
        out_shape=[jax.ShapeDtypeStruct((t, SSD_D), BF16), jax.ShapeDtypeStruct((nb, SSD_N, SSD_D), F32)],
        scratch_shapes=[pltpu.VMEM((SSD_N, SSD_D), F32)],
        compiler_params=_cp("arbitrary"),
    )(xa, proj, proj, dtb, alog, dskip, normg)


def _lane_put(col, lane):
    li = lax.broadcasted_iota(jnp.int32, (col.shape[0], BLK), 1)
    return jnp.where(li == lane, col, 0.0)


def ssd_bwd(dmix, xa, proj, sprev, dtb, alog, dskip, normg):
    t = xa.shape[0]
    nb = t // BLK
    gw = SSD_D // SSD_G
    rev = lambda c: nb - 1 - c

    def body(dy_ref, xa_ref, z_ref, sm_ref, sp_ref, dtb_ref, al_ref, ds_ref, ng_ref,
             dxa_ref, dz_ref, dsm_ref, dng_ref, dds_ref, dal_ref, ddtb_ref, dst):
        c = pl.program_id(0)

        @pl.when(c == 0)
        def _():
            dst[...] = jnp.zeros_like(dst)
            dng_ref[...] = jnp.zeros_like(dng_ref)
            dds_ref[...] = jnp.zeros_like(dds_ref)
            dal_ref[...] = jnp.zeros_like(dal_ref)
            ddtb_ref[...] = jnp.zeros_like(ddtb_ref)

        ok = _valid_rows(BLK, rev(c) * BLK)
        sm = sm_ref[...]
        sp = sp_ref[...]
        d = _ssd_chunk(xa_ref[...], sm, dtb_ref[...], al_ref[...], ds_ref[...], ok, sp)
        dt, amat, ac, act, tri, et, xs, xdt = (d[k] for k in ("dt", "amat", "ac", "act", "tri", "et", "xs", "xdt"))
        rowi = lax.broadcasted_iota(jnp.int32, (BLK, 1), 0)
        y = d["y"]
        z = z_ref[...]
        sgz = _sig(z)
        siluz = z * sgz
        yg = y * siluz
        dout = dy_ref[...]
        ng = ng_ref[...]
        dygs, xhs = [], []
        for g in range(SSD_G):
            v = yg[:, gw * g:gw * (g + 1)]
            rr = lax.rsqrt(jnp.mean(v * v, axis=1, keepdims=True) + EPS)
            xh = v * rr
            dxh = dout[:, gw * g:gw * (g + 1)] * ng[:, gw * g:gw * (g + 1)]
            dygs.append(rr * (dxh - xh * jnp.mean(dxh * xh, axis=1, keepdims=True)))
            xhs.append(xh)
        dyg = jnp.concatenate(dygs, axis=1)
        dng_ref[...] += jnp.sum(dout * jnp.concatenate(xhs, axis=1), axis=0, keepdims=True)
        dy = dyg * siluz
        dz_ref[...] = dyg * y * (sgz * (1.0 + z * (1.0 - sgz)))

        triu = _tri(lower=False)
        dyb = dy.astype(BF16)
        dsn = dst[...]
        dds_ref[...] += _nn_hi(jnp.sum(dy * xs, axis=0, keepdims=True), et)
        dac_all = _nn_hi(dy * jnp.concatenate(d["yo"], axis=1), et)
        dyo = (dy * d["ee"]).astype(BF16)
        gl = jnp.exp(ac[BLK - 1:BLK, :])
        dlast = _nn_hi(jnp.sum(dsn * sp, axis=0, keepdims=True), et) * gl
        bds, db_g, dc_g, dxdt_i, new_dst = [], [], [], [], []
        for g in range(SSD_G):
            cols = slice(GW * g, GW * (g + 1))
            bg, cg = d["bg"][g], d["cg"][g]
            dsng = dsn[:, cols].astype(BF16)
            dc = _nt(dyo[:, cols], sp[:, cols].astype(BF16))
            new_dst.append(_tn(cg, dyo[:, cols]) + d["gle"][:, cols] * dsn[:, cols])
            bds.append(_nn(bg, dsng))
            db = _nt(d["decx"][:, cols].astype(BF16), dsng)
            cbt = _nt(bg, cg)
            dcb = jnp.zeros((BLK, BLK), F32)
            for h in range(HPG * g, HPG * (g + 1)):
                hc = slice(SSD_P * h, SSD_P * (h + 1))
                dm = lax.dot_general(dy[:, hc], xdt[:, hc], (((1,), (1,)), ((), ())), precision=HI, preferred_element_type=F32)
                dcb += dm * d["seg"][h]
                w = dm * d["m"][h]
                dac_all += _lane_put(jnp.sum(w, axis=1, keepdims=True) - jnp.sum(w.T, axis=1, keepdims=True), h)
                segt = jnp.where(triu, jnp.exp(jnp.minimum(act[h:h + 1, :] - ac[:, h:h + 1], 0.0)), 0.0)
                dxdt_i.append(_nn((cbt * segt).astype(BF16), dyb[:, hc]))
            dcbb = dcb.astype(BF16)
            dc_g.append(dc + _nn(dcbb, bg))
            db_g.append(db + _tn(dcbb, cg))
        dst[...] = jnp.concatenate(new_dst, axis=1)
        bds = jnp.concatenate(bds, axis=1)
        tdec = jnp.exp(ac[BLK - 1:BLK, :] - ac) * _nn_hi(xdt * bds, et)
        dlast += jnp.sum(tdec, axis=0, keepdims=True)
        dac_all += jnp.where(rowi == BLK - 1, dlast, 0.0) - tdec
        dxdt = d["dece"] * bds + jnp.concatenate(dxdt_i, axis=1)
        da = _nn_hi(triu.astype(F32), dac_all)
        ddt = _nn_hi(dxdt * xs, et) + da * amat
        dal_ref[...] += jnp.sum(da * dt, axis=0, keepdims=True) * amat
        ddtr = jnp.where(ok, ddt * _sig(sm + dtb_ref[...]), 0.0)
        ddtb_ref[...] += jnp.sum(ddtr, axis=0, keepdims=True)
        dsm_ref[...] = ddtr
        dxs = d["dse"] * dy + dxdt * d["dte"]
        dxa_ref[...] = jnp.where(ok, jnp.concatenate([dxs] + db_g + dc_g, axis=1), 0.0)

    vec = pl.BlockSpec((1, BLK), lambda c: (0, 0))
    nvec = pl.BlockSpec((1, SSD_D), lambda c: (0, 0))
    return pl.pallas_call(
        body, name="ssd_bwd", grid=(nb,),
        in_specs=[pl.BlockSpec((BLK, SSD_D), lambda c: (rev(c), 0)),
                  pl.BlockSpec((BLK, CONV_D), lambda c: (rev(c), 0)),
                  pl.BlockSpec((BLK, SSD_D), lambda c: (rev(c), C_Z // SSD_D)),
                  pl.BlockSpec((BLK, BLK), lambda c: (rev(c), C_SM // BLK)),
                  pl.BlockSpec((None, SSD_N, SSD_D), lambda c: (rev(c), 0, 0)),
                  vec, vec, vec, nvec],
        out_specs=[pl.BlockSpec((BLK, CONV_D), lambda c: (rev(c), 0)),
                   pl.BlockSpec((BLK, SSD_D), lambda c: (rev(c), 0)),
                   pl.BlockSpec((BLK, BLK), lambda c: (rev(c), 0)),
                   nvec, vec, vec, vec],
        out_shape=[jax.ShapeDtypeStruct((t, CONV_D), F32), jax.ShapeDtypeStruct((t, SSD_D), F32),
                   jax.ShapeDtypeStruct((t, BLK), F32), jax.ShapeDtypeStruct((1, SSD_D), F32),
                   jax.ShapeDtypeStruct((1, BLK), F32), jax.ShapeDtypeStruct((1, BLK), F32),
                   jax.ShapeDtypeStruct((1, BLK), F32)],
        scratch_shapes=[pltpu.VMEM((SSD_N, SSD_D), F32)],
        compiler_params=_cp("arbitrary"),
    )(dmix, xa, proj, proj, sprev, dtb, alog, dskip, normg)


def _segments(nb, fine):
    if fine:
        cuts = list(range(0, nb, 2)) + [nb]
    else:
        cuts = sorted({0, nb} | {max(1, round(nb * f)) for f in (0.3, 0.53, 0.77)})
    return list(zip(cuts[:-1], cuts[1:]))


def attn_fwd(q, k, v, qcol, kcol, vcol, nh, dq, dv, scale, c_col=None, c_row=None, lane0=0):
    t = q.shape[0]
    tq = BLK
    use_bias = c_col is not None

    def body(*refs):
        if use_bias:
            q_ref, k_ref, v_ref, cc_ref, cr_ref, o_ref, l_ref = refs
        else:
            q_ref, k_ref, v_ref, o_ref, l_ref = refs
        i = pl.program_id(0)
        rowg = i * tq + lax.broadcasted_iota(jnp.int32, (tq, 1), 0)

        def tile(tk):
            col = lax.broadcasted_iota(jnp.int32, (1, tk), 1)
            mask = (col <= rowg) & (col >= PAD)
            outs = []
            lse = jnp.zeros((tq, BLK), F32)
            for h in range(nh):
                s = _nt(q_ref[:, dq * h:dq * (h + 1)].astype(BF16), k_ref[0:tk, dq * h:dq * (h + 1)].astype(BF16)) * scale
                if use_bias:
                    s = s + (cc_ref[:, lane0 + h:lane0 + h + 1] - cr_ref[h:h + 1, 0:tk])
                s = jnp.where(mask, s, NEG)
                m = jnp.max(s, axis=1, keepdims=True)
                p = jnp.exp(s - m)
                l = jnp.sum(p, axis=1, keepdims=True)
                outs.append(_nn(p.astype(BF16), v_ref[0:tk, dv * h:dv * (h + 1)].astype(BF16)) / l)
                lse += _lane_put(m + jnp.log(l), h)
            o_ref[...] = jnp.concatenate(outs, axis=1).astype(BF16)
            l_ref[...] = lse.T[0:8, :]

        for t0, t1 in _segments(t // tq, True):
            pl.when((i >= t0) & (i < t1))(functools.partial(tile, t1 * BLK))

    in_specs = [pl.BlockSpec((tq, nh * dq), lambda i: (i, qcol)),
                pl.BlockSpec((t, nh * dq), lambda i: (0, kcol)),
                pl.BlockSpec((t, nh * dv), lambda i: (0, vcol))]
    args = [q, k, v]
    if use_bias:
        in_specs += [pl.BlockSpec((tq, BLK), lambda i: (i, 0)), pl.BlockSpec((8, t), lambda i: (0, 0))]
        args += [c_col, c_row]
    return pl.pallas_call(
        body, name="attn_fwd", grid=(t // tq,),
        in_specs=in_specs,
        out_specs=[pl.BlockSpec((tq, nh * dv), lambda i: (i, 0)), pl.BlockSpec((8, tq), lambda i: (0, i))],
        out_shape=[jax.ShapeDtypeStruct((t, nh * dv), BF16), jax.ShapeDtypeStruct((8, t), F32)],
        compiler_params=_cp("arbitrary"),
    )(*args)


def attn_bwd(q, k, v, do, lse_row, o, qcol, kcol, vcol, docol, ocol, nh, dq, dv, scale, c_col=None, c_row=None, lane0=0):
    t = q.shape[0]
    tq = BLK
    use_bias = c_col is not None
    nq = t // tq

    def body(*refs):
        if use_bias:
            (q_ref, k_ref, v_ref, do_ref, l_ref, o_ref, cc_ref, cr_ref, dq_ref, dk_ref, dv_ref, dcq_ref, dck_ref,
             kt, ckb, dacc) = refs
        else:
            q_ref, k_ref, v_ref, do_ref, l_ref, o_ref, dq_ref, dk_ref, dv_ref, kt = refs
        i = pl.program_id(0)

        @pl.when(i == 0)
        def _():
            kt[...] = k_ref[...].astype(BF16).T
            dk_ref[...] = jnp.zeros_like(dk_ref)
            dv_ref[...] = jnp.zeros_like(dv_ref)
            if use_bias:
                dacc[...] = jnp.zeros_like(dacc)
                for h in range(nh):
                    ckb[h] = jnp.broadcast_to(cc_ref[:, lane0 + h:lane0 + h + 1], (t, BLK))

        qry = i * tq + lax.broadcasted_iota(jnp.int32, (1, tq), 1)
        dot = (do_ref[...].astype(F32) * o_ref[...].astype(F32)).T

        def tile(tk):
            key = lax.broadcasted_iota(jnp.int32, (tk, 1), 0)
            mask = (key <= qry) & (key >= PAD)
            dqts, dcqs = [], []
            for h in range(nh):
                qh = q_ref[:, dq * h:dq * (h + 1)].astype(BF16)
                kh = k_ref[0:tk, dq * h:dq * (h + 1)].astype(BF16)
                vh = v_ref[0:tk, dv * h:dv * (h + 1)].astype(BF16)
                doh = do_ref[:, dv * h:dv * (h + 1)].astype(BF16)
                delta = jnp.sum(dot[dv * h:dv * (h + 1), :], axis=0, keepdims=True)
                st = _nt(kh, qh) * scale
                if use_bias:
                    st = st + (cr_ref[h:h + 1, :] - ckb[h, 0:tk, :])
                pt = jnp.exp(jnp.where(mask, st, NEG) - l_ref[h:h + 1, :])
                dst = pt * (_nt(vh, doh) - delta)
                dsb = dst.astype(BF16)
                dk_ref[0:tk, dq * h:dq * (h + 1)] += _nn(dsb, qh) * scale
                dv_ref[0:tk, dv * h:dv * (h + 1)] += _nn(pt.astype(BF16), doh)
                dqts.append(_nn(kt[dq * h:dq * (h + 1), 0:tk], dsb))
                if use_bias:
                    dcqs.append(jnp.sum(dst, axis=0, keepdims=True))
                    dacc[h, 0:tk, :] += dst
            dq_ref[...] = jnp.concatenate(dqts, axis=0).T * scale
            if use_bias:
                dcq_ref[...] = jnp.concatenate(dcqs + [jnp.zeros((8 - nh, tq), F32)], axis=0)

        for t0, t1 in _segments(nq, not use_bias):
            pl.when((i >= t0) & (i < t1))(functools.partial(tile, t1 * BLK))

        if use_bias:
            @pl.when(i == nq - 1)
            def _():
                lane = lax.broadcasted_iota(jnp.int32, (1, BLK), 1)
                tot = jnp.zeros((t, BLK), F32)
                for h in range(nh):
                    tot += jnp.where(lane == lane0 + h, jnp.sum(dacc[h], axis=1, keepdims=True), 0.0)
                dck_ref[...] = tot

    keys_q = pl.BlockSpec((t, nh * dq), lambda i: (0, 0))
    keys_v = pl.BlockSpec((t, nh * dv), lambda i: (0, 0))
    keys_c = pl.BlockSpec((t, BLK), lambda i: (0, 0))
    qrow = pl.BlockSpec((8, tq), lambda i: (0, i))
    in_specs = [pl.BlockSpec((tq, nh * dq), lambda i: (i, qcol)),
                pl.BlockSpec((t, nh * dq), lambda i: (0, kcol)),
                pl.BlockSpec((t, nh * dv), lambda i: (0, vcol)),
                pl.BlockSpec((tq, nh * dv), lambda i: (i, docol)),
                qrow,
                pl.BlockSpec((tq, nh * dv), lambda i: (i, ocol))]
    args = [q, k, v, do, lse_row, o]
    out_specs = [pl.BlockSpec((tq, nh * dq), lambda i: (i, 0)), keys_q, keys_v]
    out_shape = [jax.ShapeDtypeStruct((t, nh * dq), F32), jax.ShapeDtypeStruct((t, nh * dq), F32),
                 jax.ShapeDtypeStruct((t, nh * dv), F32)]
    scratch = [pltpu.VMEM((nh * dq, t), BF16)]
    if use_bias:
        in_specs += [keys_c, qrow]
        args += [c_col, c_row]
        out_specs += [qrow, keys_c]
        out_shape += [jax.ShapeDtypeStruct((8, t), F32), jax.ShapeDtypeStruct((t, BLK), F32)]
        scratch += [pltpu.VMEM((nh, t, BLK), F32), pltpu.VMEM((nh, t, BLK), F32)]
    return pl.pallas_call(
        body, name="attn_bwd", grid=(nq,),
        in_specs=in_specs, out_specs=out_specs, out_shape=out_shape, scratch_shapes=scratch,
        compiler_params=_cp("arbitrary"),
    )(*args)


def fox_pre(proj, fb):
    t = proj.shape[0]
    nb = t // BLK

    def body(sm_ref, fb_ref, c_ref, cr_ref):
        x = sm_ref[...] + fb_ref[...]
        lane = lax.broadcasted_iota(jnp.int32, (1, BLK), 1)
        keep = _valid_rows(t, 0) & (lane >= SM_F) & (lane < SM_F + FOX_H)
        logf = jnp.where(keep, jnp.minimum(x, 0.0) - jnp.log(1.0 + jnp.exp(-jnp.abs(x))), 0.0)
        tri = _tri().astype(F32)
        carry = jnp.zeros((1, BLK), F32)
        for b in range(nb):
            cb = _nn_hi(tri, logf[b * BLK:(b + 1) * BLK, :]) + carry
            c_ref[b * BLK:(b + 1) * BLK, :] = cb
            carry = cb[BLK - 1:BLK, :]
        cr_ref[...] = c_ref[...].T[SM_F:SM_F + 8, :]

    return pl.pallas_call(
        body, name="fox_pre", grid=(1,),
        in_specs=[pl.BlockSpec((t, BLK), lambda i: (0, C_SM // BLK)), pl.BlockSpec((1, BLK), lambda i: (0, 0))],
        out_specs=[pl.BlockSpec((t, BLK), lambda i: (0, 0)), pl.BlockSpec((8, t), lambda i: (0, 0))],
        out_shape=[jax.ShapeDtypeStruct((t, BLK), F32), jax.ShapeDtypeStruct((8, t), F32)],
        compiler_params=_cp("arbitrary"),
    )(proj, fb)


def fox_pre_bwd(dcq, dck, proj, fb, dsm_in):
    t = proj.shape[0]
    nb = t // BLK

    def body(dcq_ref, dck_ref, sm_ref, fb_ref, din_ref, dsm_ref, dfb_ref, scr):
        triu = _tri(lower=False).astype(F32)
        carry = jnp.zeros((1, BLK), F32)
        scr[...] = jnp.concatenate([jnp.zeros((SM_F, t), F32), dcq_ref[...], jnp.zeros((BLK - SM_F - 8, t), F32)], axis=0).T
        for b in range(nb - 1, -1, -1):
            blk = scr[b * BLK:(b + 1) * BLK, :] - dck_ref[b * BLK:(b + 1) * BLK, :]
            cb = _nn_hi(triu, blk) + carry
            scr[b * BLK:(b + 1) * BLK, :] = cb
            carry = cb[0:1, :]
        x = sm_ref[...] + fb_ref[...]
        lane = lax.broadcasted_iota(jnp.int32, (1, BLK), 1)
        keep = _valid_rows(t, 0) & (lane >= SM_F) & (lane < SM_F + FOX_H)
        df = jnp.where(keep, scr[...] * _sig(-x), 0.0)
        dfb_ref[...] = jnp.sum(df, axis=0, keepdims=True)
        dsm_ref[...] = din_ref[...] + df

    full = pl.BlockSpec((t, BLK), lambda i: (0, 0))
    return pl.pallas_call(
        body, name="fox_pre_bwd", grid=(1,),
        in_specs=[pl.BlockSpec((8, t), lambda i: (0, 0)), full,
                  pl.BlockSpec((t, BLK), lambda i: (0, C_SM // BLK)), pl.BlockSpec((1, BLK), lambda i: (0, 0)), full],
        out_specs=[full, pl.BlockSpec((1, BLK), lambda i: (0, 0))],
        out_shape=[jax.ShapeDtypeStruct((t, BLK), F32), jax.ShapeDtypeStruct((1, BLK), F32)],
        scratch_shapes=[pltpu.VMEM((t, BLK), F32)],
        compiler_params=_cp("arbitrary"),
    )(dcq, dck, proj, fb, dsm_in)


def _swap_rope(x):
    lane = lax.broadcasted_iota(jnp.int32, (1, BLK), 1)
    return jnp.where((lane >= SM_KR) & (lane < SM_KR + 16), pltpu.roll(x, BLK - 16, 1),
                     jnp.where((lane >= SM_KR + 16) & (lane < SM_KR + 32), pltpu.roll(x, 16, 1), 0.0))


def _rms(x, g):
    r = lax.rsqrt(jnp.mean(x * x, axis=1, keepdims=True) + EPS)
    return r, x * r


def mla_pre(proj, qg, kvg, wq, wk, wv, cosq, sinq):
    t = proj.shape[0]
    tm = _row_tile(t)

    def body(cq_ref, ckv_ref, sm_ref, qg_ref, kvg_ref, wq_ref, wk_ref, wv_ref, cos_ref, sin_ref,
             q_ref, k_ref, v_ref, cqn_ref, ckvn_ref):
        cs, sn = cos_ref[...], sin_ref[...]
        _, xh = _rms(cq_ref[...], None)
        cqn = (xh * qg_ref[...]).astype(BF16)
        cqn_ref[...] = cqn
        qraw = _nn(cqn, wq_ref[...])
        qs = []
        for h in range(MLA_H):
            hb = qraw[:, BLK * h:BLK * (h + 1)]
            qs.append(hb * cs + _swap_rope(hb) * sn)
        q_ref[...] = jnp.concatenate(qs, axis=1).astype(BF16)
        _, kh = _rms(ckv_ref[...], None)
        ckvn = (kh * kvg_ref[...]).astype(BF16)
        ckvn_ref[...] = ckvn
        kraw = _nn(ckvn, wk_ref[...])
        v_ref[...] = _nn(ckvn, wv_ref[...]).astype(BF16)
        lane = lax.broadcasted_iota(jnp.int32, (1, BLK), 1)
        kr = sm_ref[...]
        krr = jnp.where((lane >= SM_KR) & (lane < SM_KR + MLA_ROPE), kr * cs + _swap_rope(kr) * sn, 0.0)
        k_ref[...] = jnp.concatenate([kraw[:, BLK * h:BLK * (h + 1)] + krr for h in range(MLA_H)], axis=1).astype(BF16)

    def rows(w, cb):
        return pl.BlockSpec((tm, w), lambda i: (i, cb))

    def whole(a):
        return pl.BlockSpec(a.shape, lambda i: (0, 0))

    return pl.pallas_call(
        body, name="mla_pre", grid=(t // tm,),
        in_specs=[rows(MLA_QL, C_CQ // MLA_QL), rows(MLA_KVL, C_CKV // MLA_KVL), rows(BLK, C_SM // BLK),
                  whole(qg), whole(kvg), whole(wq), whole(wk), whole(wv), rows(BLK, 0), rows(BLK, 0)],
        out_specs=[rows(512, 0), rows(512, 0), rows(256, 0), rows(MLA_QL, 0), rows(MLA_KVL, 0)],
        out_shape=[jax.ShapeDtypeStruct((t, 512), BF16), jax.ShapeDtypeStruct((t, 512), BF16),
                   jax.ShapeDtypeStruct((t, 256), BF16), jax.ShapeDtypeStruct((t, MLA_QL), BF16),
                   jax.ShapeDtypeStruct((t, MLA_KVL), BF16)],
        compiler_params=_cp("arbitrary"),
    )(proj, proj, proj, qg, kvg, wq, wk, wv, cosq, sinq)


def mla_pre_bwd(dq, dk, dv, proj, cqn, ckvn, qg, kvg, wq, wk, wv, cosq, sinq, dsm_in):
    t = proj.shape[0]
    tm = _row_tile(t)

    def body(dq_ref, dk_ref, dv_ref, cq_ref, ckv_ref, cqn_ref, ckvn_ref, qg_ref, kvg_ref, wq_ref, wk_ref, wv_ref,
             cos_ref, sin_ref, din_ref, dcq_ref, dckv_ref, dsm_ref, dwq_ref, dwk_ref, dwv_ref, dqg_ref, dkvg_ref):
        i = pl.program_id(0)

        @pl.when(i == 0)
        def _():
            for r in (dwq_ref, dwk_ref, dwv_ref, dqg_ref, dkvg_ref):
                r[...] = jnp.zeros_like(r)

        cs, sn = cos_ref[...], sin_ref[...]
        lane = lax.broadcasted_iota(jnp.int32, (1, BLK), 1)

        def unrope(dy):
            return dy * cs + _swap_rope(dy * sn)

        dqp = jnp.concatenate([unrope(dq_ref[:, BLK * h:BLK * (h + 1)]) for h in range(MLA_H)], axis=1).astype(BF16)
        dwq_ref[...] += _tn(cqn_ref[...], dqp)
        dcqn = _nt(dqp, wq_ref[...])
        r, xh = _rms(cq_ref[...], None)
        dqg_ref[...] += jnp.sum(dcqn * xh, axis=0, keepdims=True)
        dxh = dcqn * qg_ref[...]
        dcq_ref[...] = r * (dxh - xh * jnp.mean(dxh * xh, axis=1, keepdims=True))

        dkn, dkr = [], jnp.zeros((tm, BLK), F32)
        for h in range(MLA_H):
            blk = dk_ref[:, BLK * h:BLK * (h + 1)]
            dkn.append(jnp.where(lane < MLA_NOPE, blk, 0.0))
            dkr += jnp.where((lane >= SM_KR) & (lane < SM_KR + MLA_ROPE), blk, 0.0)
        dknb = jnp.concatenate(dkn, axis=1).astype(BF16)
        dvb = dv_ref[...].astype(BF16)
        ckvn = ckvn_ref[...]
        dwk_ref[...] += _tn(ckvn, dknb)
        dwv_ref[...] += _tn(ckvn, dvb)
        dckvn = _nt(dknb, wk_ref[...]) + _nt(dvb, wv_ref[...])
        r2, kh = _rms(ckv_ref[...], None)
        dkvg_ref[...] += jnp.sum(dckvn * kh, axis=0, keepdims=True)
        dkh = dckvn * kvg_ref[...]
        dckv_ref[...] = r2 * (dkh - kh * jnp.mean(dkh * kh, axis=1, keepdims=True))
        dsm_ref[...] = din_ref[...] + jnp.where((lane >= SM_KR) & (lane < SM_KR + MLA_ROPE), unrope(dkr), 0.0)

    def rows(w, cb):
        return pl.BlockSpec((tm, w), lambda i: (i, cb))

    def whole(a):
        return pl.BlockSpec(a.shape, lambda i: (0, 0))

    def wshape(a):
        return jax.ShapeDtypeStruct(a.shape, F32)

    return pl.pallas_call(
        body, name="mla_pre_bwd", grid=(t // tm,),
        in_specs=[rows(512, 0), rows(512, 0), rows(256, 0), rows(MLA_QL, C_CQ // MLA_QL), rows(MLA_KVL, C_CKV // MLA_KVL),
                  rows(MLA_QL, 0), rows(MLA_KVL, 0), whole(qg), whole(kvg), whole(wq), whole(wk), whole(wv),
                  rows(BLK, 0), rows(BLK, 0), rows(BLK, 0)],
        out_specs=[rows(MLA_QL, 0), rows(MLA_KVL, 0), rows(BLK, 0), whole(wq), whole(wk), whole(wv), whole(qg), whole(kvg)],
        out_shape=[jax.ShapeDtypeStruct((t, MLA_QL), F32), jax.ShapeDtypeStruct((t, MLA_KVL), F32),
                   jax.ShapeDtypeStruct((t, BLK), F32), wshape(wq), wshape(wk), wshape(wv), wshape(qg), wshape(kvg)],
        compiler_params=_cp("arbitrary"),
    )(dq, dk, dv, proj, proj, cqn, ckvn, qg, kvg, wq, wk, wv, cosq, sinq, dsm_in)


def _slot_sum(me, own, recv_ref):
    gg = own.astype(F32)
    for s in range(N_DEV):
        gg = gg + jnp.where(me == s, 0.0, recv_ref[s].astype(F32))
    return gg


def adamw(w, m, v, g=None, recv=None, own=None, me_arr=None):
    shape = w.shape
    c = shape[-1]
    from_recv = recv is not None
    if not from_recv:
        me_arr = jnp.zeros((1,), jnp.int32)
    nl = len(recv) if from_recv else 1
    rws = w.size // c // nl
    tr = rws
    for d in (1024, 512, 352, 256, 128, 64, 32, 16, 8):
        if rws % d == 0 and d * c * 4 <= (2 << 20):
            tr = d
            break
    nt = rws // tr
    w2, m2, v2 = (a.reshape(nl, rws, c) for a in (w, m, v))
    if from_recv:
        gin = [a.reshape(N_DEV, rws, c) for a in list(recv) + list(own)]
    else:
        gin = [g.reshape(1, rws, c)]

    def body(me_ref, w_ref, m_ref, v_ref, *rest):
        g_refs, outs = rest[:len(gin)], rest[len(gin):]
        if from_recv:
            g_out, outs = outs[0], outs[1:]
            for li in range(nl):
                @pl.when(pl.program_id(0) == li)
                def _(li=li):
                    g_out[...] = _slot_sum(me_ref[0], g_refs[nl + li][...], g_refs[li])
            gg = g_out[...]
        else:
            gg = g_refs[0][...]
        d_ref, nm_ref, nv_ref = outs
        nm = B1 * m_ref[...] + (1.0 - B1) * gg
        nv = B2 * v_ref[...] + (1.0 - B2) * (gg * gg)
        mh = nm / (1.0 - B1 ** STEP)
        vh = nv / (1.0 - B2 ** STEP)
        d_ref[...] = -LR * (mh / (jnp.sqrt(vh) + AEPS) + WD * w_ref[...])
        nm_ref[...] = nm
        nv_ref[...] = nv

    row = pl.BlockSpec((None, tr, c), lambda l, i, me: (l, i, 0))
    if from_recv:
        gspecs = [pl.BlockSpec((N_DEV, tr, c), lambda l, i, me, li=li: (0, jnp.where(l == li, i, 0), 0))
                  for li in range(nl)]
        gspecs += [pl.BlockSpec((None, tr, c), lambda l, i, me, li=li: (me[0], jnp.where(l == li, i, 0), 0))
                   for li in range(nl)]
    else:
        gspecs = [row]
    nout = 4 if from_recv else 3
    outs = pl.pallas_call(
        body, name="adamw",
        grid_spec=pltpu.PrefetchScalarGridSpec(num_scalar_prefetch=1, grid=(nl, nt), in_specs=[row, row, row] + gspecs,
                                               out_specs=[row] * nout),
        out_shape=[jax.ShapeDtypeStruct((nl, rws, c), F32)] * nout,
        compiler_params=_cp("arbitrary", "arbitrary"),
    )(me_arr, w2, m2, v2, *gin)
    return tuple(o.reshape(shape) for o in outs)


def sum_slots(recv, own=None, me_arr=None):
    _, r, c = recv.shape
    if own is None:
        own, me_arr = recv, jnp.zeros((1,), jnp.int32)
        plain = True
    else:
        plain = False

    def body(me_ref, r_ref, own_ref, o_ref):
        if plain:
            gg = r_ref[0].astype(F32)
            for s in range(1, N_DEV):
                gg = gg + r_ref[s].astype(F32)
            o_ref[...] = gg
        else:
            o_ref[...] = _slot_sum(me_ref[0], own_ref[...], r_ref)

    return pl.pallas_call(
        body, name="sum_slots",
        grid_spec=pltpu.PrefetchScalarGridSpec(
            num_scalar_prefetch=1, grid=(1,),
            in_specs=[pl.BlockSpec((N_DEV, r, c), lambda i, me: (0, 0, 0)),
                      pl.BlockSpec((None, r, c), lambda i, me: (me[0], 0, 0))],
            out_specs=pl.BlockSpec((r, c), lambda i, me: (0, 0))),
        out_shape=jax.ShapeDtypeStruct((r, c), F32),
        compiler_params=_cp("arbitrary"),
    )(me_arr, recv, own)


_FLIPS = [(0, 0, 1), (0, 1, 0), (0, 1, 1), (1, 0, 0), (1, 0, 1), (1, 1, 0), (1, 1, 1)]
_ANY = pl.BlockSpec(memory_space=pl.ANY)


def _mesh_place():
    x, y, c = lax.axis_index("x"), lax.axis_index("y"), lax.axis_index("c")
    me = 4 * x + 2 * y + c
    peers = [((x + fx) % 2, (y + fy) % 2, (c + fc) % 2) for fx, fy, fc in _FLIPS]
    return me, peers


def place_own(src, l, dtype, me_arr):
    _, r, c = src.shape
    tr = r
    for d in (512, 352, 256, 128, 64, 32, 16, 8):
        if r % d == 0 and d * c * 4 <= (2 << 20):
            tr = d
            break

    def body(me_ref, s_ref, o_ref):
        o_ref[...] = s_ref[...].astype(dtype)

    return pl.pallas_call(
        body, name="place_own",
        grid_spec=pltpu.PrefetchScalarGridSpec(
            num_scalar_prefetch=1, grid=(r // tr,),
            in_specs=[pl.BlockSpec((None, tr, c), lambda i, me: (l, i, 0))],
            out_specs=pl.BlockSpec((None, tr, c), lambda i, me: (me[0], i, 0))),
        out_shape=jax.ShapeDtypeStruct((N_DEV, r, c), dtype),
        compiler_params=_cp("arbitrary"),
    )(me_arr, src)


_HBM = pl.BlockSpec(memory_space=pltpu.HBM)
_SEMS = pl.BlockSpec(memory_space=pltpu.SEMAPHORE)
_EFFECT = pltpu.SideEffectType.DATAFLOW_SIDE_EFFECTING


def exchange_start(mode, arrays, name, after=None):
    n = len(arrays)
    gather = mode == "gather"
    ns = 0 if gather else n
    zones = list(arrays) if gather else [lax.empty(a.shape, a.dtype) for a in arrays]
    ops = ([] if gather else list(arrays)) + zones
    extra = [] if after is None else [after]

    def body(*refs):
        srcs, lands = refs[:ns], refs[ns:ns + n]
        send_sems, recv_sems = refs[ns + n + len(extra)], refs[ns + n + len(extra) + 1]
        token = refs[-1]
        me, peers = _mesh_place()
        ids = [4 * p[0] + 2 * p[1] + p[2] for p in peers]
        for j in range(n):
            for k in range(N_DEV - 1):
                src = lands[j].at[me] if gather else srcs[j].at[ids[k]]
                pltpu.make_async_remote_copy(src_ref=src, dst_ref=lands[j].at[me],
                                             send_sem=send_sems.at[j * (N_DEV - 1) + k],
                                             recv_sem=recv_sems.at[j * (N_DEV - 1) + k], device_id=peers[k],
                                             device_id_type=pl.DeviceIdType.MESH).start()
        token[...] = jnp.zeros_like(token)

    nsem = n * (N_DEV - 1)
    res = pl.pallas_call(
        body, name=name,
        in_specs=[_HBM] * (ns + n) + [_ANY] * len(extra),
        out_specs=(_SEMS, _SEMS, *[_HBM] * (ns + n), pl.BlockSpec(memory_space=pltpu.VMEM)),
        out_shape=(pltpu.SemaphoreType.DMA((nsem,)), pltpu.SemaphoreType.DMA((nsem,)),
                   *[pltpu.HBM(a.shape, a.dtype) for a in ops], jax.ShapeDtypeStruct((8, BLK), F32)),
        input_output_aliases={i: 2 + i for i in range(ns + n)},
        compiler_params=pltpu.CompilerParams(has_side_effects=_EFFECT),
    )(*[pltpu.with_memory_space_constraint(a, pltpu.HBM) for a in ops], *extra)
    return dict(gather=gather, send=res[0], recv=res[1], srcs=list(res[2:2 + ns]), lands=list(res[2 + ns:2 + ns + n]),
                token=res[-1])


def exchange_wait(hd, idxs, name, after):
    gather = hd["gather"]
    n = len(idxs)
    ns = 0 if gather else n
    ops = ([] if gather else [hd["srcs"][j] for j in idxs]) + [hd["lands"][j] for j in idxs]

    def body(*refs):
        srcs, lands = refs[:ns], refs[ns:ns + n]
        send_sems, recv_sems = refs[ns + n], refs[ns + n + 1]
        me, peers = _mesh_place()
        ids = [4 * p[0] + 2 * p[1] + p[2] for p in peers]
        for p, j in enumerate(idxs):
            for k in range(N_DEV - 1):
                src = lands[p].at[me] if gather else srcs[p].at[ids[k]]
                cp = pltpu.make_async_remote_copy(src_ref=src, dst_ref=lands[p].at[ids[k]],
                                                  send_sem=send_sems.at[j * (N_DEV - 1) + k],
                                                  recv_sem=recv_sems.at[j * (N_DEV - 1) + k], device_id=peers[k],
                                                  device_id_type=pl.DeviceIdType.MESH)
                cp.wait_send()
                cp.wait_recv()

    res = pl.pallas_call(
        body, name=name,
        in_specs=[_HBM] * (ns + n) + [_SEMS, _SEMS, _ANY],
        out_specs=[_HBM] * (ns + n),
        out_shape=[pltpu.HBM(a.shape, a.dtype) for a in ops],
        input_output_aliases={i: i for i in range(ns + n)},
        compiler_params=pltpu.CompilerParams(has_side_effects=_EFFECT),
    )(*ops, hd["send"], hd["recv"], after)
    return list(res[:ns]), list(res[ns:])


def _chip_place():
    x, y, c = lax.axis_index("x"), lax.axis_index("y"), lax.axis_index("c")
    chips = [((x + 1) % 2, y), (x, (y + 1) % 2), ((x + 1) % 2, (y + 1) % 2)]
    ident = lambda p: 4 * p[0] + 2 * p[1] + p[2]
    return dict(me=4 * x + 2 * y + c, sib=(x, y, 1 - c), sib_id=4 * x + 2 * y + 1 - c,
                same=[(cx, cy, c) for cx, cy in chips], same_ids=[ident((cx, cy, c)) for cx, cy in chips],
                other_ids=[ident((cx, cy, 1 - c)) for cx, cy in chips])


def _remote(src, dst, send_sem, recv_sem, dev):
    return pltpu.make_async_remote_copy(src_ref=src, dst_ref=dst, send_sem=send_sem, recv_sem=recv_sem, device_id=dev,
                                        device_id_type=pl.DeviceIdType.MESH)


def gather_start(zones, name):
    n = len(zones)

    def body(*refs):
        lands, send_sems, recv_sems, token = refs[:n], refs[n], refs[n + 1], refs[-1]
        pc = _chip_place()
        for j in range(n):
            own = lands[j].at[pc["me"]]
            for k, dev in enumerate([pc["sib"]] + pc["same"]):
                _remote(own, own, send_sems.at[4 * j + k], recv_sems.at[4 * j + k], dev).start()
        token[...] = jnp.zeros_like(token)

    res = pl.pallas_call(
        body, name=name,
        in_specs=[_HBM] * n,
        out_specs=(_SEMS, _SEMS, *[_HBM] * n, pl.BlockSpec(memory_space=pltpu.VMEM)),
        out_shape=(pltpu.SemaphoreType.DMA((4 * n,)), pltpu.SemaphoreType.DMA((4 * n,)),
                   *[pltpu.HBM(a.shape, a.dtype) for a in zones], jax.ShapeDtypeStruct((8, BLK), F32)),
        input_output_aliases={i: 2 + i for i in range(n)},
        compiler_params=pltpu.CompilerParams(has_side_effects=_EFFECT),
    )(*[pltpu.with_memory_space_constraint(a, pltpu.HBM) for a in zones])
    return dict(send=res[0], recv=res[1], lands=list(res[2:2 + n]), token=res[-1])


def gather_relay(hd, idxs, name, after):
    n = len(idxs)

    def body(*refs):
        lands, send_sems, recv_sems = refs[:n], refs[n], refs[n + 1]
        fsend, frecv, token = refs[n + 3 + n], refs[n + 4 + n], refs[-1]
        pc = _chip_place()
        for p, j in enumerate(idxs):
            for k in range(3):
                _remote(lands[p].at[pc["me"]], lands[p].at[pc["same_ids"][k]], send_sems.at[4 * j + 1 + k],
                        recv_sems.at[4 * j + 1 + k], pc["same"][k]).wait_recv()
        for p in range(n):
            for k in range(3):
                got = lands[p].at[pc["same_ids"][k]]
                _remote(got, got, fsend.at[3 * p + k], frecv.at[3 * p + k], pc["sib"]).start()
        token[...] = jnp.zeros_like(token)

    ops = [hd["lands"][j] for j in idxs]
    res = pl.pallas_call(
        body, name=name,
        in_specs=[_HBM] * n + [_SEMS, _SEMS, _ANY],
        out_specs=(*[_HBM] * n, _SEMS, _SEMS, pl.BlockSpec(memory_space=pltpu.VMEM)),
        out_shape=(*[pltpu.HBM(a.shape, a.dtype) for a in ops], pltpu.SemaphoreType.DMA((3 * n,)),
                   pltpu.SemaphoreType.DMA((3 * n,)), jax.ShapeDtypeStruct((8, BLK), F32)),
        input_output_aliases={i: i for i in range(n)},
        compiler_params=pltpu.CompilerParams(has_side_effects=_EFFECT),
    )(*ops, hd["send"], hd["recv"], after)
    return dict(lands=list(res[:n]), fsend=res[n], frecv=res[n + 1], token=res[-1])


def gather_wait(hd, rl, idxs, name, after):
    n = len(idxs)

    def body(*refs):
        lands, send_sems, recv_sems, fsend, frecv = refs[:n], refs[n], refs[n + 1], refs[n + 2], refs[n + 3]
        pc = _chip_place()
        for p, j in enumerate(idxs):
            own = lands[p].at[pc["me"]]
            for k, dev in enumerate([pc["sib"]] + pc["same"]):
                _remote(own, own, send_sems.at[4 * j + k], recv_sems.at[4 * j + k], dev).wait_send()
            _remote(own, lands[p].at[pc["sib_id"]], send_sems.at[4 * j], recv_sems.at[4 * j], pc["sib"]).wait_recv()
            for k in range(3):
                cp = _remote(lands[p].at[pc["same_ids"][k]], lands[p].at[pc["other_ids"][k]], fsend.at[3 * p + k],
                             frecv.at[3 * p + k], pc["sib"])
                cp.wait_send()
                cp.wait_recv()

    res = pl.pallas_call(
        body, name=name,
        in_specs=[_HBM] * n + [_SEMS, _SEMS, _SEMS, _SEMS, _ANY],
        out_specs=[_HBM] * n,
        out_shape=[pltpu.HBM(a.shape, a.dtype) for a in rl["lands"]],
        input_output_aliases={i: i for i in range(n)},
        compiler_params=pltpu.CompilerParams(has_side_effects=_EFFECT),
    )(*rl["lands"], hd["send"], hd["recv"], rl["fsend"], rl["frecv"], after)
    return list(res)


def _pad_cols(a, n):
    return jnp.pad(a, ((0, 0),) * (a.ndim - 1) + ((0, n - a.shape[-1]),))


def w_in_to_padded(w):
    z = lambda n: jnp.zeros(w.shape[:-1] + (n,), w.dtype)
    return jnp.concatenate([
        w[..., 0:1280], w[..., 1288:2056], w[..., 2060:2316], w[..., 2316:2444],
        w[..., 1280:1288], w[..., 2056:2060], z(SM_KR - SM_F - FOX_H), w[..., 2444:2476], z(BLK - SM_KR - MLA_ROPE)], axis=-1)


def w_in_from_padded(g):
    s = C_SM
    return jnp.concatenate([
        g[..., 0:1280], g[..., s + SM_DT:s + SM_DT + 8], g[..., 1280:2048], g[..., s + SM_F:s + SM_F + 4],
        g[..., 2048:2304], g[..., 2304:2432], g[..., s + SM_KR:s + SM_KR + MLA_ROPE]], axis=-1)


def _unshard_cols(gth):
    n, r, c = gth.shape
    return jnp.transpose(gth, (1, 0, 2)).reshape(r, n * c)


def _shard_cols(full):
    r, nc = full.shape
    return jnp.transpose(full.reshape(r, N_DEV, nc // N_DEV), (1, 0, 2))


def mla_weights(uq_g, ukv_g):
    uq = _unshard_cols(uq_g)
    dqh = MLA_NOPE + MLA_ROPE
    wq = jnp.concatenate([_pad_cols(uq[:, dqh * h:dqh * (h + 1)], BLK) for h in range(MLA_H)], axis=1)
    wk = jnp.concatenate([_pad_cols(ukv_g[2 * h], BLK) for h in range(MLA_H)], axis=1)
    wv = jnp.concatenate([ukv_g[2 * h + 1] for h in range(MLA_H)], axis=1)
    return wq, wk, wv


def mla_weight_grads(dwq, dwk, dwv):
    dqh = MLA_NOPE + MLA_ROPE
    duq = _shard_cols(jnp.concatenate([dwq[:, BLK * h:BLK * h + dqh] for h in range(MLA_H)], axis=1))
    parts = []
    for h in range(MLA_H):
        parts += [dwk[:, BLK * h:BLK * h + MLA_NOPE], dwv[:, MLA_V * h:MLA_V * (h + 1)]]
    return duq, jnp.stack(parts, axis=0)


def rope_tables(t):
    pos = (jnp.arange(t, dtype=jnp.int32) - PAD).astype(F32)
    inv_freq = 1.0 / (10000.0 ** (jnp.arange(0, MLA_ROPE, 2, dtype=F32) / MLA_ROPE))
    ang = pos[:, None] * inv_freq[None, :]
    cos, sin = jnp.cos(ang), jnp.sin(ang)
    one, zero = jnp.ones((t, SM_KR), F32), jnp.zeros((t, SM_KR), F32)
    tail = BLK - SM_KR - MLA_ROPE
    cosq = jnp.concatenate([one, cos, cos, jnp.ones((t, tail), F32)], axis=1)
    sinq = jnp.concatenate([zero, -sin, sin, jnp.zeros((t, tail), F32)], axis=1)
    return cosq, sinq


def _lanes(v, off=0):
    return jnp.pad(v.astype(F32), (off, BLK - off - v.shape[0]))[None, :]


def layer_fwd(x, ln, hb, getw, tabs, ahead):
    sv = {"h0b": hb}
    def behind(vec, tok):
        return vec if tok is None else vec + 0.0 * tok[0:1, 0:1]

    W = dict(getw("ffn1", hb))
    ln1 = (behind(W["ln1_g"], ahead(0, "mix", hb, 1)), W["ln1_b"])
    u, v, r1, h1b = ffn_fwd_seq(x, ln, W["g1"], W["u1"], W["d1"], ln1)
    sv.update(u1=u, v1=v, r1=r1, h1b=h1b)
    W.update(getw("mix", h1b))
    ln2 = (W["ln2_g"], W["ln2_b"])
    proj = mm_nn(h1b, W["w_in"])
    xa = conv_fwd(proj, W["conv_w"], W["conv_b"])
    y_ssd, sprev = ssd_fwd(xa, proj, W["dtb"], W["alog"], W["dskip"], W["normg"])
    c_col, c_row = fox_pre(proj, W["fb"])
    y_fox, lse_f = attn_fwd(proj, proj, proj, C_FQ // 256, C_FK // 256, C_FV // 256, FOX_H, FOX_DH, FOX_DH,
                            FOX_DH ** -0.5, c_col, c_row, SM_F)
    q, k, vv, cqn, ckvn = mla_pre(proj, behind(W["qg"], ahead(0, "ffn2", y_fox)), W["kvg"], W["wq"], W["wk"], W["wv"], *tabs)
    y_mla, lse_m = attn_fwd(q, k, vv, 0, 0, 0, MLA_H, BLK, MLA_V, (MLA_NOPE + MLA_ROPE) ** -0.5)
    mixcat = jnp.concatenate([y_ssd, y_fox, y_mla], axis=1)
    r2, h2b = mm_res_ln(mixcat, W["w_out"], r1, ln1, ln2)
    sv.update(proj=proj, xa=xa, sprev=sprev, c_col=c_col, c_row=c_row, lse_f=lse_f, q=q, k=k, v=vv, cqn=cqn, ckvn=ckvn,
              lse_m=lse_m, mixcat=mixcat, r2=r2, h2b=h2b)
    W.update(getw("ffn2", h2b))
    ln3 = (behind(W["ln3_g"], ahead(1, "ffn1", h2b)), W["ln3_b"])
    u, v, r3, h3b = ffn_fwd_seq(r2, ln2, W["g2"], W["u2"], W["d2"], ln3)
    sv.update(u2=u, v2=v, r3=r3, W=W)
    return r3, ln3, h3b, sv


def ffn_bwd(parts, r, gamma, hb_in, u, v, wg, wu, wd, after=None):
    dh, dwg, dwu, dwd, dg, db = ffn_bwd_seq(parts, r, gamma, hb_in, u, v, wg, wu, wd, after)
    return dh, dict(d=dwd, g=dwg, u=dwu, ln_g=dg, ln_b=db)


def layer_bwd(parts, sv, emit, tabs, after):
    G = {}
    W = sv["W"]
    dh2, g2 = ffn_bwd(parts, sv["r3"], W["ln3_g"], sv["h2b"], sv["u2"], sv["v2"], W["g2"], W["u2"], W["d2"], after)
    G.update(g2=g2["g"], u2=g2["u"], d2=g2["d"], ln3_g=g2["ln_g"], ln3_b=g2["ln_b"])
    tok = emit("ffn2", G)
    dr2, dmc, G["w_out"], G["ln2_g"], G["ln2_b"] = oproj_bwd(dh2, sv["r2"], W["ln2_g"], sv["mixcat"], W["w_out"], tok)
    proj = sv["proj"]
    dxa, dz, dsm, G["normg"], G["dskip"], G["alog"], G["dtb"] = ssd_bwd(
        dmc, sv["xa"], proj, sv["sprev"], W["dtb"], W["alog"], W["dskip"], W["normg"])
    dxbc, G["conv_w"], G["conv_b"] = conv_bwd(dxa, proj, W["conv_w"], W["conv_b"])
    dfq, dfk, dfv, dcq, dck = attn_bwd(proj, proj, proj, dmc, sv["lse_f"], sv["mixcat"], C_FQ // 256, C_FK // 256,
                                       C_FV // 256, 2, 2, FOX_H, FOX_DH, FOX_DH, FOX_DH ** -0.5, sv["c_col"], sv["c_row"], SM_F)
    dsm, G["fb"] = fox_pre_bwd(dcq, dck, proj, W["fb"], dsm)
    dq, dk, dv = attn_bwd(sv["q"], sv["k"], sv["v"], dmc, sv["lse_m"], sv["mixcat"], 0, 0, 0, 3, 3, MLA_H, BLK, MLA_V,
                          (MLA_NOPE + MLA_ROPE) ** -0.5)
    dcql, dckv, dsm, G["wq"], G["wk"], G["wv"], G["qg"], G["kvg"] = mla_pre_bwd(
        dq, dk, dv, proj, sv["cqn"], sv["ckvn"], W["qg"], W["kvg"], W["wq"], W["wk"], W["wv"], *tabs, dsm)
    dproj = jnp.concatenate([dz, dxbc, dfq, dfk, dfv, dcql, dckv, dsm], axis=1).astype(BF16)
    dh1p, G["w_in"] = proj_bwd(dproj, sv["h1b"], W["w_in"])
    tok = emit("mix", G)
    dh0, g1 = ffn_bwd([(dr2, ALPHA), (dh1p, 1.0)], sv["r1"], W["ln1_g"], sv["h0b"], sv["u1"], sv["v1"],
                      W["g1"], W["u1"], W["d1"], tok)
    G.update(g1=g1["g"], u1=g1["u"], d1=g1["d"], ln1_g=g1["ln_g"], ln1_b=g1["ln_b"])
    tok = emit("ffn1", G)
    return [(dh0, 1.0)], G, tok


def local_step(x, target, meta_full, getw, emit, ahead=lambda l, stage, after, min_layer=0: None):
    t = x.shape[0] + BLK
    tabs = rope_tables(t)
    xr, hb = build_h0(meta_full, x)
    ln = None
    saved = []
    for l in range(NL):
        xr, ln, hb, sv = layer_fwd(xr, ln, hb, functools.partial(getw, l), tabs,
                                   lambda dl, stage, after, min_layer=0, l=l: ahead(l + dl, stage, after, min_layer))
        saved.append(sv)
    dy, loss = loss_head(xr, ln, target)
    parts = [(dy, 1.0)]
    grads = [None] * NL
    tok = None
    for l in range(NL - 1, -1, -1):
        parts, grads[l], tok = layer_bwd(parts, saved[l], functools.partial(emit, l), tabs, tok)
    gx, gmeta = split_dh0(parts[0][0], tok)
    return loss, gx, gmeta, grads


_SMALL = ["ln1_g", "ln1_b", "ln2_g", "ln2_b", "ln3_g", "ln3_b", "conv_b", "ssd_norm_g", "mla_q_norm_g",
          "mla_kv_norm_g", "dt_bias", "a_log", "d_skip", "fox_f_b"]
_SMALL_ROWS = 8
_BIG = ["ffn1_w_gate", "ffn1_w_up", "ffn1_w_down", "w_in", "conv_w", "mla_w_uq", "mla_w_ukv", "w_out",
        "ffn2_w_gate", "ffn2_w_up", "ffn2_w_down"]
_NAMES = ["meta", "ffn1_w_gate", "ffn1_w_up", "ffn1_w_down", "ln1_g", "ln1_b", "w_in", "conv_w", "conv_b", "dt_bias",
          "a_log", "d_skip", "ssd_norm_g", "fox_f_b", "mla_q_norm_g", "mla_w_uq", "mla_kv_norm_g", "mla_w_ukv", "w_out",
          "ln2_g", "ln2_b", "ffn2_w_gate", "ffn2_w_up", "ffn2_w_down", "ln3_g", "ln3_b"]


def pack_small(p):
    flat = jnp.concatenate([p[n].astype(F32) for n in _SMALL], axis=1)
    return _pad_cols(flat, _SMALL_ROWS * D).reshape(NL * _SMALL_ROWS, D)


def unpack_small(a, like):
    flat = a.reshape(NL, _SMALL_ROWS * D)
    out, at = {}, 0
    for n in _SMALL:
        out[n] = flat[:, at:at + like[n].shape[1]]
        at += like[n].shape[1]
    return out


_STAGES = {"ffn1": ["ffn1_w_gate", "ffn1_w_up", "ffn1_w_down"],
           "mix": ["w_in", "conv_w", "mla_w_uq", "mla_w_ukv", "w_out"],
           "ffn2": ["ffn2_w_gate", "ffn2_w_up", "ffn2_w_down"]}


_FFN_T = ("ffn1_w_gate", "ffn1_w_up", "ffn2_w_gate", "ffn2_w_up")


def stage_weights(l, stage, g, rep):
    if stage != "mix":
        i = stage[3]
        return {"g" + i: g[f"ffn{i}_w_gate"].reshape(D_FF, D), "u" + i: g[f"ffn{i}_w_up"].reshape(D_FF, D),
                "d" + i: g[f"ffn{i}_w_down"].reshape(D_FF, D),
                "ln1_g" if i == "1" else "ln3_g": rep["ln1_g" if i == "1" else "ln3_g"][l][None, :],
                "ln1_b" if i == "1" else "ln3_b": rep["ln1_b" if i == "1" else "ln3_b"][l][None, :]}
    W = {}
    W["w_in"] = g["w_in"].reshape(D, N_INP)
    W["w_out"] = g["w_out"].reshape(D, D)
    W["wq"], W["wk"], W["wv"] = mla_weights(g["mla_w_uq"], g["mla_w_ukv"])
    W["conv_w"] = _unshard_cols(g["conv_w"])
    for k in ("ln2_g", "ln2_b", "conv_b"):
        W[k] = rep[k][l][None, :]
    W["normg"] = rep["ssd_norm_g"][l][None, :]
    W["qg"] = rep["mla_q_norm_g"][l][None, :]
    W["kvg"] = rep["mla_kv_norm_g"][l][None, :]
    W["dtb"] = _lanes(rep["dt_bias"][l], SM_DT)
    W["alog"] = _lanes(rep["a_log"][l], SM_DT)
    W["dskip"] = _lanes(rep["d_skip"][l], SM_DT)
    W["fb"] = _lanes(rep["fox_f_b"][l], SM_F)
    return W


def small_grads(G):
    return {"ln1_g": G["ln1_g"][0], "ln1_b": G["ln1_b"][0], "ln2_g": G["ln2_g"][0], "ln2_b": G["ln2_b"][0],
            "ln3_g": G["ln3_g"][0], "ln3_b": G["ln3_b"][0], "conv_b": G["conv_b"][0], "ssd_norm_g": G["normg"][0],
            "mla_q_norm_g": G["qg"][0], "mla_kv_norm_g": G["kvg"][0], "dt_bias": G["dtb"][0, :SSD_H],
            "a_log": G["alog"][0, :SSD_H], "d_skip": G["dskip"][0, :SSD_H], "fox_f_b": G["fb"][0, SM_F:SM_F + FOX_H]}


def big_grads(G, stage):
    if stage != "mix":
        i = stage[-1]
        return {f"ffn{i}_w_{k}": G[k[0] + i].reshape(N_DEV, HS, D) for k in ("gate", "up", "down")}
    duq, dukv = mla_weight_grads(G["wq"], G["wk"], G["wv"])
    return {"w_in": G["w_in"].reshape(N_DEV, D // N_DEV, N_INP), "w_out": G["w_out"].reshape(N_DEV, D // N_DEV, D),
            "mla_w_uq": duq, "mla_w_ukv": dukv, "conv_w": _shard_cols(G["conv_w"])}


def kernel(x, meta, ffn1_w_gate, ffn1_w_up, ffn1_w_down, ln1_g, ln1_b, w_in, conv_w, conv_b, dt_bias, a_log, d_skip, ssd_norm_g, fox_f_b, mla_q_norm_g, mla_w_uq, mla_kv_norm_g, mla_w_ukv, w_out, ln2_g, ln2_b, ffn2_w_gate, ffn2_w_up, ffn2_w_down, ln3_g, ln3_b, loss_target, m_meta, m_ffn1_w_gate, m_ffn1_w_up, m_ffn1_w_down, m_ln1_g, m_ln1_b, m_w_in, m_conv_w, m_conv_b, m_dt_bias, m_a_log, m_d_skip, m_ssd_norm_g, m_fox_f_b, m_mla_q_norm_g, m_mla_w_uq, m_mla_kv_norm_g, m_mla_w_ukv, m_w_out, m_ln2_g, m_ln2_b, m_ffn2_w_gate, m_ffn2_w_up, m_ffn2_w_down, m_ln3_g, m_ln3_b, v_meta, v_ffn1_w_gate, v_ffn1_w_up, v_ffn1_w_down, v_ln1_g, v_ln1_b, v_w_in, v_conv_w, v_conv_b, v_dt_bias, v_a_log, v_d_skip, v_ssd_norm_g, v_fox_f_b, v_mla_q_norm_g, v_mla_w_uq, v_mla_kv_norm_g, v_mla_w_ukv, v_w_out, v_ln2_g, v_ln2_b, v_ffn2_w_gate, v_ffn2_w_up, v_ffn2_w_down, v_ln3_g, v_ln3_b):
    vals = (meta, ffn1_w_gate, ffn1_w_up, ffn1_w_down, ln1_g, ln1_b, w_in, conv_w, conv_b, dt_bias, a_log, d_skip, ssd_norm_g, fox_f_b, mla_q_norm_g, mla_w_uq, mla_kv_norm_g, mla_w_ukv, w_out, ln2_g, ln2_b, ffn2_w_gate, ffn2_w_up, ffn2_w_down, ln3_g, ln3_b)
    moms = (m_meta, m_ffn1_w_gate, m_ffn1_w_up, m_ffn1_w_down, m_ln1_g, m_ln1_b, m_w_in, m_conv_w, m_conv_b, m_dt_bias, m_a_log, m_d_skip, m_ssd_norm_g, m_fox_f_b, m_mla_q_norm_g, m_mla_w_uq, m_mla_kv_norm_g, m_mla_w_ukv, m_w_out, m_ln2_g, m_ln2_b, m_ffn2_w_gate, m_ffn2_w_up, m_ffn2_w_down, m_ln3_g, m_ln3_b)
    vars_ = (v_meta, v_ffn1_w_gate, v_ffn1_w_up, v_ffn1_w_down, v_ln1_g, v_ln1_b, v_w_in, v_conv_w, v_conv_b, v_dt_bias, v_a_log, v_d_skip, v_ssd_norm_g, v_fox_f_b, v_mla_q_norm_g, v_mla_w_uq, v_mla_kv_norm_g, v_mla_w_ukv, v_w_out, v_ln2_g, v_ln2_b, v_ffn2_w_gate, v_ffn2_w_up, v_ffn2_w_down, v_ln3_g, v_ln3_b)
    P = dict(zip(_NAMES, vals))
    M = dict(zip(_NAMES, moms))
    V = dict(zip(_NAMES, vars_))
    me = 4 * lax.axis_index("x") + 2 * lax.axis_index("y") + lax.axis_index("c")

    me_arr = me.astype(jnp.int32).reshape(1)
    for n in _FFN_T:
        P[n], M[n], V[n] = (jnp.swapaxes(a[n], 1, 2) for a in (P, M, V))
    src = dict(P)
    src["w_in"] = w_in_to_padded(P["w_in"])
    order = [("meta", 0)] + [(n, l) for l in range(NL) for names in _STAGES.values() for n in names]
    nfirst = 1 + len(_STAGES["ffn1"])

    def place(n, l):
        return place_own(P["meta"][None] if n == "meta" else src[n], l, F32 if n in ("meta", "conv_w") else BF16, me_arr)

    hg_first = gather_start([place(n, l) for n, l in order[:nfirst]], "gather_start_first")
    hg_rest = gather_start([place(n, l) for n, l in order[nfirst:]], "gather_start_rest")
    zone_of = {nl_: ((hg_first, i) if i < nfirst else (hg_rest, i - nfirst)) for i, nl_ in enumerate(order)}
    relays = {}

    def ahead(l, stage, after, min_layer=0):
        if not min_layer <= l < NL:
            return None
        if (l, stage) not in relays:
            zs = [zone_of[("meta", 0)]] if stage == "meta" else [zone_of[(n, l)] for n in _STAGES[stage]]
            hg, idxs = zs[0][0], [i for _, i in zs]
            relays[(l, stage)] = (hg, idxs, gather_relay(hg, idxs, f"gather_relay_{l}_{stage}", after))
        return relays[(l, stage)][2]["token"]

    def arrived(l, stage, after):
        ahead(l, stage, after)
        hg, idxs, rl = relays[(l, stage)]
        return gather_wait(hg, rl, idxs, f"gather_wait_{l}_{stage}", after)

    meta_full = _unshard_cols(arrived(0, "meta", hg_rest["token"])[0])

    def getw(l, stage, after):
        return stage_weights(l, stage, dict(zip(_STAGES[stage], arrived(l, stage, after))), P)

    sent = {}

    def emit(l, stage, G):
        bg = big_grads(G, stage)
        sent[(l, stage)] = exchange_start("scatter", [bg[n] for n in _STAGES[stage]], f"scatter_start_{l}_{stage}")
        return sent[(l, stage)]["token"]

    loss, gx, gmeta, grads = local_step(x[0], loss_target[0], meta_full, getw, emit, ahead)

    small = jnp.concatenate([pack_small({n: jnp.stack([small_grads(g)[n] for g in grads]) for n in _SMALL}), gmeta,
                             jnp.pad(loss, ((0, 7), (0, D - 1)))], axis=0)
    hs = exchange_start("gather", [place_own(small[None], 0, F32, me_arr)], "small_start")

    out = {}
    after = hs["token"]
    for stage in ("ffn2", "mix", "ffn1"):
        names = _STAGES[stage]
        whole = [l for l in range(NL - 1, -1, -1) if (l, stage) != (0, "ffn1")]
        got = {l: exchange_wait(sent[(l, stage)], list(range(len(names))), f"scatter_wait_{l}_{stage}", after) for l in whole}
        for i, n in enumerate(names):
            one = {l: (got[l][0][i], got[l][1][i]) for l in whole}
            for l in set(range(NL)) - set(whole):
                s_, r_ = exchange_wait(sent[(l, stage)], [i], f"scatter_wait_{l}_{stage}_{i}", after)
                one[l] = (s_[0], r_[0])
            own = [one[l][0] for l in range(NL)]
            recv = [one[l][1] for l in range(NL)]
            if n == "w_in":
                g = jnp.stack([w_in_from_padded(sum_slots(recv[l], own[l], me_arr)) for l in range(NL)])
                out[n] = (g,) + adamw(P[n], M[n], V[n], g=g)
            else:
                out[n] = adamw(P[n], M[n], V[n], recv=recv, own=own, me_arr=me_arr)
                if n in _FFN_T:
                    out[n] = tuple(jnp.swapaxes(a, 1, 2) for a in out[n])
            after = out[n][1]
    gsmall = sum_slots(exchange_wait(hs, [0], "small_wait", after)[1][0])
    gm = lax.dynamic_slice(gsmall[NL * _SMALL_ROWS:], (0, me * (D // N_DEV)), (N_META, D // N_DEV))
    out["meta"] = (gm,) + adamw(P["meta"], M["meta"], V["meta"], g=gm)
    gs = gsmall[:NL * _SMALL_ROWS]
    sd, sm_, sv_ = adamw(pack_small(P), pack_small(M), pack_small(V), g=gs)
    ups = [unpack_small(a, P) for a in (gs, sd, sm_, sv_)]
    for n in _SMALL:
        out[n] = tuple(u[n] for u in ups)

    loss_all = gsmall[NL * _SMALL_ROWS + N_META, 0]
    flat = [loss_all, gx[None]]
    for k in range(4):
        flat += [out[n][k] for n in _NAMES]
    return tuple(flat)
```

```python
import functools

import jax
import jax.numpy as jnp
from jax import lax
from jax.experimental import pallas as pl
from jax.experimental.pallas import tpu as pltpu

F32, BF16 = jnp.float32, jnp.bfloat16
HI = lax.Precision.HIGHEST

N_DEV = 8
D = 1024
NL = 2
N_META = 16
BLK = 128
PAD = BLK - N_META
D_FF = 2816
HS = D_FF // N_DEV
SSD_H, SSD_P, SSD_N, SSD_G = 8, 64, 64, 2
SSD_D = SSD_H * SSD_P
CONV_K = 4
CONV_D = SSD_D + 2 * SSD_G * SSD_N
FOX_H, FOX_DH = 4, 64
MLA_H, MLA_QL, MLA_KVL, MLA_NOPE, MLA_ROPE, MLA_V = 4, 256, 128, 64, 32, 64
N_IN = 2476
C_Z, C_XBC, C_FQ, C_FK, C_FV, C_CQ, C_CKV, C_SM, N_INP = 0, 512, 1280, 1536, 1792, 2048, 2304, 2432, 2560
SM_DT, SM_F, SM_KR = 0, 8, 64
ALPHA = (2 * NL) ** 0.25
EPS = 1e-5
NEG = -1e30
LR, B1, B2, AEPS, WD, STEP = 0.001, 0.9, 0.999, 1e-08, 0.01, 10
VMEM_MB = 56


def _cp(*sem):
    return pltpu.CompilerParams(dimension_semantics=sem, vmem_limit_bytes=VMEM_MB << 20)


def _nn(a, b):
    return lax.dot_general(a, b, (((1,), (0,)), ((), ())), preferred_element_type=F32)


def _nt(a, b):
    return lax.dot_general(a, b, (((1,), (1,)), ((), ())), preferred_element_type=F32)


def _tn(a, b):
    return lax.dot_general(a, b, (((0,), (0,)), ((), ())), preferred_element_type=F32)


def _nn_hi(a, b):
    return lax.dot_general(a, b, (((1,), (0,)), ((), ())), precision=HI, preferred_element_type=F32)


def _row_tile(t):
    for d in range(640, 15, -16):
        if t % d == 0:
            return d
    raise ValueError(t)


def _sig(x):
    return 1.0 / (1.0 + jnp.exp(-x))


def _tri(lower=True):
    r = lax.broadcasted_iota(jnp.int32, (BLK, BLK), 0)
    c = lax.broadcasted_iota(jnp.int32, (BLK, BLK), 1)
    return (r >= c) if lower else (r <= c)


def build_h0(meta_full, x):
    s = x.shape[0]
    nb = s // BLK + 1

    def body(m_ref, x_ref, h_ref, hb_ref):
        i = pl.program_id(0)

        @pl.when(i == 0)
        def _():
            h = jnp.concatenate([jnp.zeros((PAD, D), F32), m_ref[...]], axis=0)
            h_ref[...] = h
            hb_ref[...] = h.astype(BF16)

        @pl.when(i > 0)
        def _():
            h_ref[...] = x_ref[...]
            hb_ref[...] = x_ref[...].astype(BF16)

    return pl.pallas_call(
        body, name="build_h0", grid=(nb,),
        in_specs=[pl.BlockSpec((N_META, D), lambda i: (0, 0)),
                  pl.BlockSpec((BLK, D), lambda i: (jnp.maximum(i - 1, 0), 0))],
        out_specs=[pl.BlockSpec((BLK, D), lambda i: (i, 0))] * 2,
        out_shape=[jax.ShapeDtypeStruct((nb * BLK, D), F32), jax.ShapeDtypeStruct((nb * BLK, D), BF16)],
        compiler_params=_cp("arbitrary"),
    )(meta_full, x)


FT = 256


def _layer_norm(r, gamma, beta):
    mu = jnp.mean(r, axis=1, keepdims=True)
    xc = r - mu
    var = jnp.mean(xc * xc, axis=1, keepdims=True)
    return xc * lax.rsqrt(var + EPS) * gamma + beta


def ffn_fwd_seq(x, ln_in, wg, wu, wd, ln_out):
    t = x.shape[0]
    f = wg.shape[0]
    nj, nr = f // FT, t // _row_tile(t)
    rc = t // nr
    plain = ln_in is None
    gi, bi = ln_out if plain else ln_in

    def body(x_hbm, gi_ref, bi_ref, go_ref, bo_ref, wg_ref, wu_ref, wd_ref, u_ref, v_ref, r_hbm, yb_hbm,
             acc, hbs, xbuf, sem_in, sem_out):
        j = pl.program_id(0)

        @pl.when(j == 0)
        def _():
            def fetch(k):
                return pltpu.make_async_copy(x_hbm.at[pl.ds(k * rc, rc)], xbuf.at[k % 2], sem_in.at[k % 2])

            fetch(0).start()
            for k in range(nr):
                if k + 1 < nr:
                    fetch(k + 1).start()
                fetch(k).wait()
                h = xbuf[k % 2]
                if not plain:
                    h = _layer_norm(h, gi_ref[...], bi_ref[...])
                acc[k * rc:(k + 1) * rc, :] = ALPHA * h
                hbs[k * rc:(k + 1) * rc, :] = h.astype(BF16)

        def chunk(k, last):
            sl = slice(k * rc, (k + 1) * rc)
            h = hbs[sl, :]
            u = _nt(h, wg_ref[...])
            v = _nt(h, wu_ref[...])
            u_ref[sl, :] = u.astype(BF16)
            v_ref[sl, :] = v.astype(BF16)
            acc[sl, :] += _nn((0.5 * u * _sig(u) * v).astype(BF16), wd_ref[...])
            if not last:
                return []
            rows = pl.ds(k * rc, rc)
            cps = [pltpu.make_async_copy(acc.at[rows], r_hbm.at[rows], sem_out.at[2 * k])]
            cps[0].start()
            hbs[sl, :] = _layer_norm(acc[sl, :], go_ref[...], bo_ref[...]).astype(BF16)
            cps.append(pltpu.make_async_copy(hbs.at[rows], yb_hbm.at[rows], sem_out.at[2 * k + 1]))
            cps[1].start()
            return cps

        @pl.when(j < nj - 1)
        def _():
            for k in range(nr):
                chunk(k, False)

        @pl.when(j == nj - 1)
        def _():
            cps = []
            for k in range(nr):
                cps += chunk(k, True)
            for cp in cps:
                cp.wait()

    vec = pl.BlockSpec((1, D), lambda j: (0, 0))
    wsp = pl.BlockSpec((FT, D), lambda j: (j, 0))
    act = pl.BlockSpec((None, t, FT), lambda j: (j, 0, 0))
    return pl.pallas_call(
        body, name="ffn_fwd_seq", grid=(nj,),
        in_specs=[_ANY, vec, vec, vec, vec, wsp, wsp, wsp],
        out_specs=[act, act, _ANY, _ANY],
        out_shape=[jax.ShapeDtypeStruct((nj, t, FT), BF16), jax.ShapeDtypeStruct((nj, t, FT), BF16),
                   jax.ShapeDtypeStruct((t, D), F32), jax.ShapeDtypeStruct((t, D), BF16)],
        scratch_shapes=[pltpu.VMEM((t, D), F32), pltpu.VMEM((t, D), BF16), pltpu.VMEM((2, rc, D), F32),
                        pltpu.SemaphoreType.DMA((2,)), pltpu.SemaphoreType.DMA((2 * nr,))],
        compiler_params=_cp("arbitrary"),
    )(x, gi, bi, ln_out[0], ln_out[1], wg, wu, wd)


def ffn_bwd_seq(parts, r, gamma, hb, u, v, wg, wu, wd, after=None):
    nj, t, _ = u.shape
    f = nj * FT
    nr = t // _row_tile(t)
    rc = t // nr
    nc = t // BLK
    scales = [s for _, s in parts]
    npart = len(parts)
    extra = [] if after is None else [after]

    def body(*refs):
        refs = refs[len(extra):]
        p_hbm, refs = refs[:npart], refs[npart:]
        (r_hbm, g_ref, hb_hbm, u_ref, v_ref, wg_ref, wu_ref, wd_ref, dh_hbm, dwg_ref, dwu_ref, dwd_ref, dg_ref, db_ref,
         dfs, hbt, dft, dhacc, dus, dvs, acs, pbuf, rbuf, hbuf, sems, sem_out) = refs
        j = pl.program_id(0)

        @pl.when(j == 0)
        def _():
            def fetch(c):
                rows = pl.ds(c * BLK, BLK)
                cps = [pltpu.make_async_copy(p_hbm[p].at[rows], pbuf.at[c % 2, p], sems.at[c % 2, p]) for p in range(npart)]
                cps.append(pltpu.make_async_copy(r_hbm.at[rows], rbuf.at[c % 2], sems.at[c % 2, npart]))
                cps.append(pltpu.make_async_copy(hb_hbm.at[rows], hbuf.at[c % 2], sems.at[c % 2, npart + 1]))
                return cps

            for cp in fetch(0):
                cp.start()
            dg = jnp.zeros((1, D), F32)
            db = jnp.zeros((1, D), F32)
            for c in range(nc):
                if c + 1 < nc:
                    for cp in fetch(c + 1):
                        cp.start()
                for cp in fetch(c):
                    cp.wait()
                sl = slice(c * BLK, (c + 1) * BLK)
                dy = scales[0] * pbuf[c % 2, 0]
                for p in range(1, npart):
                    dy += scales[p] * pbuf[c % 2, p]
                rr = rbuf[c % 2]
                xc = rr - jnp.mean(rr, axis=1, keepdims=True)
                rstd = lax.rsqrt(jnp.mean(xc * xc, axis=1, keepdims=True) + EPS)
                xh = xc * rstd
                dxh = dy * g_ref[...]
                dr = rstd * (dxh - jnp.mean(dxh, axis=1, keepdims=True) - xh * jnp.mean(dxh * xh, axis=1, keepdims=True))
                dg += jnp.sum(dy * xh, axis=0, keepdims=True)
                db += jnp.sum(dy, axis=0, keepdims=True)
                dhacc[sl, :] = ALPHA * dr
                dfc = (0.5 * dr).astype(BF16)
                dfs[sl, :] = dfc
                dft[:, sl] = dfc.T
                hbt[:, sl] = hbuf[c % 2].T
            dg_ref[...] = dg
            db_ref[...] = db

        for k in range(nr):
            sl = slice(k * rc, (k + 1) * rc)
            da = _nt(dfs[sl, :], wd_ref[...])
            uu = u_ref[sl, :].astype(F32)
            vv = v_ref[sl, :].astype(F32)
            sg = _sig(uu)
            du = (da * vv * (sg * (1.0 + uu * (1.0 - sg)))).astype(BF16)
            dv = (da * uu * sg).astype(BF16)
            dus[sl, :] = du
            dvs[sl, :] = dv
            acs[sl, :] = (uu * sg * vv).astype(BF16)
            dhacc[sl, :] += _nn(du, wg_ref[...]) + _nn(dv, wu_ref[...])
        @pl.when(j == nj - 1)
        def _():
            pltpu.make_async_copy(dhacc, dh_hbm, sem_out.at[0]).start()

        dwg_ref[...] = _nn(hbt[...], dus[...]).astype(BF16).T
        dwu_ref[...] = _nn(hbt[...], dvs[...]).astype(BF16).T
        dwd_ref[...] = _nn(dft[...], acs[...]).astype(BF16).T

        @pl.when(j == nj - 1)
        def _():
            pltpu.make_async_copy(dhacc, dh_hbm, sem_out.at[0]).wait()

    vec = pl.BlockSpec((1, D), lambda j: (0, 0))
    wsp = pl.BlockSpec((FT, D), lambda j: (j, 0))
    act = pl.BlockSpec((None, t, FT), lambda j: (j, 0, 0))
    return pl.pallas_call(
        body, name="ffn_bwd_seq", grid=(nj,),
        in_specs=[_ANY] * (len(extra) + npart + 1) + [vec, _ANY, act, act, wsp, wsp, wsp],
        out_specs=[_ANY, wsp, wsp, wsp, vec, vec],
        out_shape=[jax.ShapeDtypeStruct((t, D), F32)] + [jax.ShapeDtypeStruct((f, D), BF16)] * 3
        + [jax.ShapeDtypeStruct((1, D), F32)] * 2,
        scratch_shapes=[pltpu.VMEM((t, D), BF16), pltpu.VMEM((D, t), BF16), pltpu.VMEM((D, t), BF16),
                        pltpu.VMEM((t, D), F32), pltpu.VMEM((t, FT), BF16), pltpu.VMEM((t, FT), BF16),
                        pltpu.VMEM((t, FT), BF16), pltpu.VMEM((2, npart, BLK, D), F32), pltpu.VMEM((2, BLK, D), F32),
                        pltpu.VMEM((2, BLK, D), BF16), pltpu.SemaphoreType.DMA((2, npart + 2)),
                        pltpu.SemaphoreType.DMA((1,))],
        compiler_params=_cp("arbitrary"),
    )(*extra, *[p for p, _ in parts], r, gamma, hb, u, v, wg, wu, wd)


def mm_res_ln(a, b, x, ln_in, ln_out):
    t, k = a.shape
    tm = _row_tile(t)

    def body(a_ref, b_ref, x_ref, gi_ref, bi_ref, go_ref, bo_ref, r_ref, yb_ref):
        r = ALPHA * _layer_norm(x_ref[...], gi_ref[...], bi_ref[...]) + _nn(a_ref[...], b_ref[...])
        r_ref[...] = r
        yb_ref[...] = _layer_norm(r, go_ref[...], bo_ref[...]).astype(BF16)

    row = pl.BlockSpec((tm, D), lambda i: (i, 0))
    vec = pl.BlockSpec((1, D), lambda i: (0, 0))
    return pl.pallas_call(
        body, name="mm_res_ln", grid=(t // tm,),
        in_specs=[pl.BlockSpec((tm, k), lambda i: (i, 0)), pl.BlockSpec((k, D), lambda i: (0, 0)), row, vec, vec, vec, vec],
        out_specs=[row, row],
        out_shape=[jax.ShapeDtypeStruct((t, D), F32), jax.ShapeDtypeStruct((t, D), BF16)],
        compiler_params=_cp("arbitrary"),
    )(a, b, x, ln_in[0], ln_in[1], ln_out[0], ln_out[1])


def mm_nn(a, b):
    t, k = a.shape
    n = tn = b.shape[1]
    tm = _row_tile(t)

    def body(a_ref, b_ref, o_ref):
        o_ref[...] = _nn(a_ref[...], b_ref[...])

    return pl.pallas_call(
        body, name="mm_nn", grid=(t // tm, n // tn),
        in_specs=[pl.BlockSpec((tm, k), lambda i, j: (i, 0)), pl.BlockSpec((k, tn), lambda i, j: (0, j))],
        out_specs=pl.BlockSpec((tm, tn), lambda i, j: (i, j)),
        out_shape=jax.ShapeDtypeStruct((t, n), F32),
        compiler_params=_cp("arbitrary", "arbitrary"),
    )(a, b)


def oproj_bwd(dy, r, gamma, mixcat, w_out, after=None):
    t = r.shape[0]
    tm = _row_tile(t)
    nt = t // tm
    extra = [] if after is None else [after]

    def body(*refs):
        dy_ref, r_ref, g_ref, m_ref, w_ref, dr_ref, dm_ref, dw_ref, dg_ref, db_ref, acc = refs[len(extra):]
        i = pl.program_id(0)
        dy = dy_ref[...]
        rr = r_ref[...]
        xc = rr - jnp.mean(rr, axis=1, keepdims=True)
        rstd = lax.rsqrt(jnp.mean(xc * xc, axis=1, keepdims=True) + EPS)
        xh = xc * rstd
        dxh = dy * g_ref[...]
        dr = rstd * (dxh - jnp.mean(dxh, axis=1, keepdims=True) - xh * jnp.mean(dxh * xh, axis=1, keepdims=True))
        dr_ref[...] = dr
        drb = dr.astype(BF16)
        dm_ref[...] = _nt(drb, w_ref[...])
        dw = _tn(m_ref[...], drb)
        dg = jnp.sum(dy * xh, axis=0, keepdims=True)
        db = jnp.sum(dy, axis=0, keepdims=True)

        @pl.when(i == 0)
        def _():
            acc[...] = dw
            dg_ref[...] = dg
            db_ref[...] = db

        @pl.when(i > 0)
        def _():
            acc[...] += dw
            dg_ref[...] += dg
            db_ref[...] += db

        @pl.when(i == nt - 1)
        def _():
            dw_ref[...] = acc[...].astype(BF16)

    row = pl.BlockSpec((tm, D), lambda i: (i, 0))
    vec = pl.BlockSpec((1, D), lambda i: (0, 0))
    mat = pl.BlockSpec((D, D), lambda i: (0, 0))
    return pl.pallas_call(
        body, name="oproj_bwd", grid=(nt,),
        in_specs=[_ANY] * len(extra) + [row, row, vec, row, mat],
        out_specs=[row, row, mat, vec, vec],
        out_shape=[jax.ShapeDtypeStruct((t, D), F32), jax.ShapeDtypeStruct((t, D), F32), jax.ShapeDtypeStruct((D, D), BF16),
                   jax.ShapeDtypeStruct((1, D), F32), jax.ShapeDtypeStruct((1, D), F32)],
        scratch_shapes=[pltpu.VMEM((D, D), F32)],
        compiler_params=_cp("arbitrary"),
    )(*extra, dy, r, gamma, mixcat, w_out)


def proj_bwd(pieces, hb, w_in):
    t = hb.shape[0]
    n = w_in.shape[1]
    tm = _row_tile(t)
    nt = t // tm
    widths = [p.shape[1] for p in pieces]
    starts = [sum(widths[:k]) for k in range(len(pieces))]
    assert sum(widths) == n

    def body(*refs):
        p_refs = refs[:len(pieces)]
        h_ref, w_ref, dh_ref, dw_ref, acc = refs[len(pieces):]
        i = pl.program_id(0)
        h = h_ref[...]
        dh = jnp.zeros((tm, D), F32)
        dws = []
        for p_ref, c0, w in zip(p_refs, starts, widths):
            pb = p_ref[...].astype(BF16)
            dh += _nt(pb, w_ref[:, c0:c0 + w])
            dws.append(_tn(h, pb))
        dh_ref[...] = dh

        @pl.when(i == 0)
        def _():
            for dw, c0, w in zip(dws, starts, widths):
                acc[:, c0:c0 + w] = dw

        @pl.when(i > 0)
        def _():
            for dw, c0, w in zip(dws, starts, widths):
                acc[:, c0:c0 + w] += dw

        @pl.when(i == nt - 1)
        def _():
            dw_ref[...] = acc[...].astype(BF16)

    mat = pl.BlockSpec((D, n), lambda i: (0, 0))
    return pl.pallas_call(
        body, name="proj_bwd", grid=(nt,),
        in_specs=[pl.BlockSpec((tm, w), lambda i: (i, 0)) for w in widths] + [pl.BlockSpec((tm, D), lambda i: (i, 0)), mat],
        out_specs=[pl.BlockSpec((tm, D), lambda i: (i, 0)), mat],
        out_shape=[jax.ShapeDtypeStruct((t, D), F32), jax.ShapeDtypeStruct((D, n), BF16)],
        scratch_shapes=[pltpu.VMEM((D, n), F32)],
        compiler_params=_cp("arbitrary"),
    )(*pieces, hb, w_in)


def loss_head(r, ln, target):
    t = r.shape[0]
    nb = t // BLK

    def body(r_ref, g_ref, b_ref, t_ref, dy_ref, l_ref):
        i = pl.program_id(0)

        @pl.when(i == 0)
        def _():
            dy_ref[...] = jnp.zeros_like(dy_ref)
            l_ref[...] = jnp.zeros_like(l_ref)

        @pl.when(i > 0)
        def _():
            err = _layer_norm(r_ref[...], g_ref[...], b_ref[...]) - t_ref[...]
            dy_ref[...] = err * (1.0 / D)
            l_ref[...] += (0.5 / D) * jnp.sum(err * err, keepdims=True)

    vec = pl.BlockSpec((1, D), lambda i: (0, 0))
    return pl.pallas_call(
        body, name="loss_head", grid=(nb,),
        in_specs=[pl.BlockSpec((BLK, D), lambda i: (i, 0)), vec, vec,
                  pl.BlockSpec((BLK, D), lambda i: (jnp.maximum(i - 1, 0), 0))],
        out_specs=[pl.BlockSpec((BLK, D), lambda i: (i, 0)), pl.BlockSpec((1, 1), lambda i: (0, 0))],
        out_shape=[jax.ShapeDtypeStruct((t, D), F32), jax.ShapeDtypeStruct((1, 1), F32)],
        compiler_params=_cp("arbitrary"),
    )(r, ln[0], ln[1], target)


def split_dh0(dh0, after=None):
    t = dh0.shape[0]
    nb = t // BLK
    extra = [] if after is None else [after]

    def body(*refs):
        a_ref, gx_ref, gm_ref = refs[len(extra):]
        i = pl.program_id(0)
        tot = a_ref[...]

        @pl.when(i == 0)
        def _():
            gm_ref[...] = tot[PAD:, :]

        @pl.when(i > 0)
        def _():
            gx_ref[...] = tot

    blk = pl.BlockSpec((BLK, D), lambda i: (i, 0))
    return pl.pallas_call(
        body, name="split_dh0", grid=(nb,),
        in_specs=[_ANY] * len(extra) + [blk],
        out_specs=[pl.BlockSpec((BLK, D), lambda i: (jnp.maximum(i - 1, 0), 0)),
                   pl.BlockSpec((N_META, D), lambda i: (0, 0))],
        out_shape=[jax.ShapeDtypeStruct((t - BLK, D), F32), jax.ShapeDtypeStruct((N_META, D), F32)],
        compiler_params=_cp("arbitrary"),
    )(*extra, dh0)


def _valid_rows(nrows, first_row):
    return (first_row + lax.broadcasted_iota(jnp.int32, (nrows, 1), 0)) >= PAD


def conv_fwd(proj, conv_w, conv_b):
    t = proj.shape[0]
    c0 = C_XBC // BLK

    def body(x_ref, w_ref, b_ref, o_ref):
        ok = _valid_rows(t, 0)
        x = jnp.where(ok, x_ref[...], 0.0)
        w = w_ref[...]
        acc = b_ref[...] + w[CONV_K - 1:CONV_K, :] * x
        for s in range(1, CONV_K):
            acc += w[CONV_K - 1 - s:CONV_K - s, :] * pltpu.roll(x, s, 0)
        o_ref[...] = jnp.where(ok, acc * _sig(acc), 0.0)

    return pl.pallas_call(
        body, name="conv_fwd", grid=(CONV_D // BLK,),
        in_specs=[pl.BlockSpec((t, BLK), lambda j: (0, c0 + j)),
                  pl.BlockSpec((CONV_K, BLK), lambda j: (0, j)), pl.BlockSpec((1, BLK), lambda j: (0, j))],
        out_specs=pl.BlockSpec((t, BLK), lambda j: (0, j)),
        out_shape=jax.ShapeDtypeStruct((t, CONV_D), F32),
        compiler_params=_cp("arbitrary"),
    )(proj, conv_w, conv_b)


def conv_bwd(dxa, proj, conv_w, conv_b):
    t = proj.shape[0]
    c0 = C_XBC // BLK

    def body(d_ref, x_ref, w_ref, b_ref, dx_ref, dw_ref, db_ref):
        ok = _valid_rows(t, 0)
        x = jnp.where(ok, x_ref[...], 0.0)
        w = w_ref[...]
        xs = [x] + [pltpu.roll(x, s, 0) for s in range(1, CONV_K)]
        acc = b_ref[...] + w[CONV_K - 1:CONV_K, :] * x
        for s in range(1, CONV_K):
            acc += w[CONV_K - 1 - s:CONV_K - s, :] * xs[s]
        sg = _sig(acc)
        dxc = jnp.where(ok, d_ref[...] * (sg * (1.0 + acc * (1.0 - sg))), 0.0)
        db_ref[...] = jnp.sum(dxc, axis=0, keepdims=True)
        dw_ref[...] = jnp.concatenate(
            [jnp.sum(dxc * xs[CONV_K - 1 - k], axis=0, keepdims=True) for k in range(CONV_K)], axis=0)
        dx = w[CONV_K - 1:CONV_K, :] * dxc
        for s in range(1, CONV_K):
            dx += w[CONV_K - 1 - s:CONV_K - s, :] * pltpu.roll(dxc, t - s, 0)
        dx_ref[...] = jnp.where(ok, dx, 0.0)

    col = pl.BlockSpec((t, BLK), lambda j: (0, j))
    return pl.pallas_call(
        body, name="conv_bwd", grid=(CONV_D // BLK,),
        in_specs=[col, pl.BlockSpec((t, BLK), lambda j: (0, c0 + j)),
                  pl.BlockSpec((CONV_K, BLK), lambda j: (0, j)), pl.BlockSpec((1, BLK), lambda j: (0, j))],
        out_specs=[col, pl.BlockSpec((CONV_K, BLK), lambda j: (0, j)), pl.BlockSpec((1, BLK), lambda j: (0, j))],
        out_shape=[jax.ShapeDtypeStruct((t, CONV_D), F32), jax.ShapeDtypeStruct((CONV_K, CONV_D), F32),
                   jax.ShapeDtypeStruct((1, CONV_D), F32)],
        compiler_params=_cp("arbitrary"),
    )(dxa, proj, conv_w, conv_b)


def _softplus(x):
    return jnp.maximum(x, 0.0) + jnp.log(1.0 + jnp.exp(-jnp.abs(x)))


GW = SSD_D // SSD_G
HPG = SSD_H // SSD_G


def _head_expand():
    r = lax.broadcasted_iota(jnp.int32, (BLK, SSD_D), 0)
    c = lax.broadcasted_iota(jnp.int32, (BLK, SSD_D), 1)
    rt = lax.broadcasted_iota(jnp.int32, (SSD_D, BLK), 0)
    ct = lax.broadcasted_iota(jnp.int32, (SSD_D, BLK), 1)
    return (c // SSD_P == r).astype(F32), (rt // SSD_P == ct).astype(F32)


def _ssd_chunk(xa, sm, dtb, alog, dskip, ok, sp):
    e, et = _head_expand()
    dt = jnp.where(ok, _softplus(sm + dtb), 0.0)
    amat = -jnp.exp(alog)
    tri = _tri()
    ac = _nn_hi(tri.astype(F32), dt * amat)
    act = ac.T
    ace, dte, dse = _nn_hi(ac, e), _nn_hi(dt, e), _nn_hi(dskip, e)
    laste = ace[BLK - 1:BLK, :]
    ee, dece, gle = jnp.exp(ace), jnp.exp(laste - ace), jnp.exp(laste)
    xs = xa[:, :SSD_D]
    xdt = xs * dte
    decx = dece * xdt
    xdtb = xdt.astype(BF16)
    d = dict(e=e, et=et, dt=dt, amat=amat, tri=tri, ac=ac, act=act, dte=dte, dse=dse, ee=ee, dece=dece, gle=gle, xs=xs,
             xdt=xdt, xdtb=xdtb, decx=decx, bg=[], cg=[], cb=[], yo=[], seg=[], m=[], new_s=[])
    ys = []
    for g in range(SSD_G):
        cols = slice(GW * g, GW * (g + 1))
        bg = xa[:, SSD_D + SSD_N * g:SSD_D + SSD_N * (g + 1)].astype(BF16)
        cg = xa[:, SSD_D + SSD_G * SSD_N + SSD_N * g:SSD_D + SSD_G * SSD_N + SSD_N * (g + 1)].astype(BF16)
        spg = sp[:, cols]
        sloc = _tn(bg, decx[:, cols].astype(BF16))
        yo = _nn(cg, spg.astype(BF16)) * ee[:, cols]
        cb = _nt(cg, bg)
        d["new_s"].append(gle[:, cols] * spg + sloc)
        yds = []
        for h in range(HPG * g, HPG * (g + 1)):
            seg = jnp.where(tri, jnp.exp(jnp.minimum(ac[:, h:h + 1] - act[h:h + 1, :], 0.0)), 0.0)
            m = cb * seg
            yds.append(_nn(m.astype(BF16), xdtb[:, SSD_P * h:SSD_P * (h + 1)]))
            d["seg"].append(seg)
            d["m"].append(m)
        ys.append(jnp.concatenate(yds, axis=1) + yo)
        for k, val in (("bg", bg), ("cg", cg), ("cb", cb), ("yo", yo)):
            d[k].append(val)
    d["y"] = jnp.concatenate(ys, axis=1) + dse * xs
    return d


def ssd_fwd(xa, proj, dtb, alog, dskip, normg):
    t = xa.shape[0]
    nb = t // BLK
    gw = SSD_D // SSD_G

    def body(xa_ref, z_ref, sm_ref, dtb_ref, al_ref, ds_ref, ng_ref, y_ref, sp_ref, st):
        c = pl.program_id(0)

        @pl.when(c == 0)
        def _():
            st[...] = jnp.zeros_like(st)

        ok = _valid_rows(BLK, c * BLK)
        sp = st[...]
        sp_ref[...] = sp
        d = _ssd_chunk(xa_ref[...], sm_ref[...], dtb_ref[...], al_ref[...], ds_ref[...], ok, sp)
        st[...] = jnp.concatenate(d["new_s"], axis=1)
        y = d["y"]
        z = z_ref[...]
        yg = y * (z * _sig(z))
        outs = []
        for g in range(SSD_G):
            v = yg[:, gw * g:gw * (g + 1)]
            outs.append(v * lax.rsqrt(jnp.mean(v * v, axis=1, keepdims=True) + EPS))
        y_ref[...] = (jnp.concatenate(outs, axis=1) * ng_ref[...]).astype(BF16)

    vec = pl.BlockSpec((1, BLK), lambda c: (0, 0))
    return pl.pallas_call(
        body, name="ssd_fwd", grid=(nb,),
        in_specs=[pl.BlockSpec((BLK, CONV_D), lambda c: (c, 0)),
                  pl.BlockSpec((BLK, SSD_D), lambda c: (c, C_Z // SSD_D)),
                  pl.BlockSpec((BLK, BLK), lambda c: (c, C_SM // BLK)),
                  vec, vec, vec, pl.BlockSpec((1, SSD_D), lambda c: (0, 0))],
        out_specs=[pl.BlockSpec((BLK, SSD_D), lambda c: (c, 0)),
                   pl.BlockSpec((None, SSD_N, SSD_D), lambda c: (c, 0, 0))],
        out_shape=[jax.ShapeDtypeStruct((t, SSD_D), BF16), jax.ShapeDtypeStruct((nb, SSD_N, SSD_D), F32)],
        scratch_shapes=[pltpu.VMEM((SSD_N, SSD_D), F32)],
        compiler_params=_cp("arbitrary"),
    )(xa, proj, proj, dtb, alog, dskip, normg)


def _lane_put(col, lane):
    li = lax.broadcasted_iota(jnp.int32, (col.shape[0], BLK), 1)
    return jnp.where(li == lane, col, 0.0)


def ssd_bwd(dmix, xa, proj, sprev, dtb, alog, dskip, normg):
    t = xa.shape[0]
    nb = t // BLK
    gw = SSD_D // SSD_G
    rev = lambda c: nb - 1 - c

    def body(dy_ref, xa_ref, z_ref, sm_ref, sp_ref, dtb_ref, al_ref, ds_ref, ng_ref,
             dxa_ref, dz_ref, dsm_ref, dng_ref, dds_ref, dal_ref, ddtb_ref, dst):
        c = pl.program_id(0)

        @pl.when(c == 0)
        def _():
            dst[...] = jnp.zeros_like(dst)
            dng_ref[...] = jnp.zeros_like(dng_ref)
            dds_ref[...] = jnp.zeros_like(dds_ref)
            dal_ref[...] = jnp.zeros_like(dal_ref)
            ddtb_ref[...] = jnp.zeros_like(ddtb_ref)

        ok = _valid_rows(BLK, rev(c) * BLK)
        sm = sm_ref[...]
        sp = sp_ref[...]
        d = _ssd_chunk(xa_ref[...], sm, dtb_ref[...], al_ref[...], ds_ref[...], ok, sp)
        dt, amat, ac, act, tri, et, xs, xdt = (d[k] for k in ("dt", "amat", "ac", "act", "tri", "et", "xs", "xdt"))
        rowi = lax.broadcasted_iota(jnp.int32, (BLK, 1), 0)
        y = d["y"]
        z = z_ref[...]
        sgz = _sig(z)
        siluz = z * sgz
        yg = y * siluz
        dout = dy_ref[...]
        ng = ng_ref[...]
        dygs, xhs = [], []
        for g in range(SSD_G):
            v = yg[:, gw * g:gw * (g + 1)]
            rr = lax.rsqrt(jnp.mean(v * v, axis=1, keepdims=True) + EPS)
            xh = v * rr
            dxh = dout[:, gw * g:gw * (g + 1)] * ng[:, gw * g:gw * (g + 1)]
            dygs.append(rr * (dxh - xh * jnp.mean(dxh * xh, axis=1, keepdims=True)))
            xhs.append(xh)
        dyg = jnp.concatenate(dygs, axis=1)
        dng_ref[...] += jnp.sum(dout * jnp.concatenate(xhs, axis=1), axis=0, keepdims=True)
        dy = dyg * siluz
        dz_ref[...] = dyg * y * (sgz * (1.0 + z * (1.0 - sgz)))

        triu = _tri(lower=False)
        dyb = dy.astype(BF16)
        dsn = dst[...]
        dds_ref[...] += _nn_hi(jnp.sum(dy * xs, axis=0, keepdims=True), et)
        dac_all = _nn_hi(dy * jnp.concatenate(d["yo"], axis=1), et)
        dyo = (dy * d["ee"]).astype(BF16)
        gl = jnp.exp(ac[BLK - 1:BLK, :])
        dlast = _nn_hi(jnp.sum(dsn * sp, axis=0, keepdims=True), et) * gl
        bds, db_g, dc_g, dxdt_i, new_dst = [], [], [], [], []
        for g in range(SSD_G):
            cols = slice(GW * g, GW * (g + 1))
            bg, cg = d["bg"][g], d["cg"][g]
            dsng = dsn[:, cols].astype(BF16)
            dc = _nt(dyo[:, cols], sp[:, cols].astype(BF16))
            new_dst.append(_tn(cg, dyo[:, cols]) + d["gle"][:, cols] * dsn[:, cols])
            bds.append(_nn(bg, dsng))
            db = _nt(d["decx"][:, cols].astype(BF16), dsng)
            cbt = _nt(bg, cg)
            dcb = jnp.zeros((BLK, BLK), F32)
            for h in range(HPG * g, HPG * (g + 1)):
                hc = slice(SSD_P * h, SSD_P * (h + 1))
                dm = _nt(dyb[:, hc], d["xdtb"][:, hc])
                dcb += dm * d["seg"][h]
                w = dm * d["m"][h]
                dac_all += _lane_put(jnp.sum(w, axis=1, keepdims=True) - jnp.sum(w.T, axis=1, keepdims=True), h)
                segt = jnp.where(triu, jnp.exp(jnp.minimum(act[h:h + 1, :] - ac[:, h:h + 1], 0.0)), 0.0)
                dxdt_i.append(_nn((cbt * segt).astype(BF16), dyb[:, hc]))
            dcbb = dcb.astype(BF16)
            dc_g.append(dc + _nn(dcbb, bg))
            db_g.append(db + _tn(dcbb, cg))
        dst[...] = jnp.concatenate(new_dst, axis=1)
        bds = jnp.concatenate(bds, axis=1)
        tdec = jnp.exp(ac[BLK - 1:BLK, :] - ac) * _nn_hi(xdt * bds, et)
        dlast += jnp.sum(tdec, axis=0, keepdims=True)
        dac_all += jnp.where(rowi == BLK - 1, dlast, 0.0) - tdec
        dxdt = d["dece"] * bds + jnp.concatenate(dxdt_i, axis=1)
        da = _nn_hi(triu.astype(F32), dac_all)
        ddt = _nn_hi(dxdt * xs, et) + da * amat
        dal_ref[...] += jnp.sum(da * dt, axis=0, keepdims=True) * amat
        ddtr = jnp.where(ok, ddt * _sig(sm + dtb_ref[...]), 0.0)
        ddtb_ref[...] += jnp.sum(ddtr, axis=0, keepdims=True)
        dsm_ref[...] = ddtr
        dxs = d["dse"] * dy + dxdt * d["dte"]
        dxa_ref[...] = jnp.where(ok, jnp.concatenate([dxs] + db_g + dc_g, axis=1), 0.0)

    vec = pl.BlockSpec((1, BLK), lambda c: (0, 0))
    nvec = pl.BlockSpec((1, SSD_D), lambda c: (0, 0))
    return pl.pallas_call(
        body, name="ssd_bwd", grid=(nb,),
        in_specs=[pl.BlockSpec((BLK, SSD_D), lambda c: (rev(c), 0)),
                  pl.BlockSpec((BLK, CONV_D), lambda c: (rev(c), 0)),
                  pl.BlockSpec((BLK, SSD_D), lambda c: (rev(c), C_Z // SSD_D)),
                  pl.BlockSpec((BLK, BLK), lambda c: (rev(c), C_SM // BLK)),
                  pl.BlockSpec((None, SSD_N, SSD_D), lambda c: (rev(c), 0, 0)),
                  vec, vec, vec, nvec],
        out_specs=[pl.BlockSpec((BLK, CONV_D), lambda c: (rev(c), 0)),
                   pl.BlockSpec((BLK, SSD_D), lambda c: (rev(c), 0)),
                   pl.BlockSpec((BLK, BLK), lambda c: (rev(c), 0)),
                   nvec, vec, vec, vec],
        out_shape=[jax.ShapeDtypeStruct((t, CONV_D), F32), jax.ShapeDtypeStruct((t, SSD_D), F32),
                   jax.ShapeDtypeStruct((t, BLK), F32), jax.ShapeDtypeStruct((1, SSD_D), F32),
                   jax.ShapeDtypeStruct((1, BLK), F32), jax.ShapeDtypeStruct((1, BLK), F32),
                   jax.ShapeDtypeStruct((1, BLK), F32)],
        scratch_shapes=[pltpu.VMEM((SSD_N, SSD_D), F32)],
        compiler_params=_cp("arbitrary"),
    )(dmix, xa, proj, proj, sprev, dtb, alog, dskip, normg)


def _segments(nb, fine):
    if fine:
        cuts = list(range(0, nb, 2)) + [nb]
    else:
        cuts = sorted({0, nb} | {max(1, round(nb * f)) for f in (0.3, 0.53, 0.77)})
    return list(zip(cuts[:-1], cuts[1:]))


def attn_fwd(q, k, v, qcol, kcol, vcol, nh, dq, dv, scale, c_col=None, c_row=None, lane0=0):
    t = q.shape[0]
    tq = BLK
    use_bias = c_col is not None

    def body(*refs):
        if use_bias:
            q_ref, k_ref, v_ref, cc_ref, cr_ref, o_ref, l_ref = refs
        else:
            q_ref, k_ref, v_ref, o_ref, l_ref = refs
        i = pl.program_id(0)
        rowg = i * tq + lax.broadcasted_iota(jnp.int32, (tq, 1), 0)

        def tile(tk):
            col = lax.broadcasted_iota(jnp.int32, (1, tk), 1)
            mask = (col <= rowg) & (col >= PAD)
            outs = []
            lse = jnp.zeros((tq, BLK), F32)
            for h in range(nh):
                s = _nt(q_ref[:, dq * h:dq * (h + 1)].astype(BF16), k_ref[0:tk, dq * h:dq * (h + 1)].astype(BF16)) * scale
                if use_bias:
                    s = s + (cc_ref[:, lane0 + h:lane0 + h + 1] - cr_ref[h:h + 1, 0:tk])
                s = jnp.where(mask, s, NEG)
                m = jnp.max(s, axis=1, keepdims=True)
                p = jnp.exp(s - m)
                l = jnp.sum(p, axis=1, keepdims=True)
                outs.append(_nn(p.astype(BF16), v_ref[0:tk, dv * h:dv * (h + 1)].astype(BF16)) / l)
                lse += _lane_put(m + jnp.log(l), h)
            o_ref[...] = jnp.concatenate(outs, axis=1).astype(BF16)
            l_ref[...] = lse.T[0:8, :]

        for t0, t1 in _segments(t // tq, True):
            pl.when((i >= t0) & (i < t1))(functools.partial(tile, t1 * BLK))

    in_specs = [pl.BlockSpec((tq, nh * dq), lambda i: (i, qcol)),
                pl.BlockSpec((t, nh * dq), lambda i: (0, kcol)),
                pl.BlockSpec((t, nh * dv), lambda i: (0, vcol))]
    args = [q, k, v]
    if use_bias:
        in_specs += [pl.BlockSpec((tq, BLK), lambda i: (i, 0)), pl.BlockSpec((8, t), lambda i: (0, 0))]
        args += [c_col, c_row]
    return pl.pallas_call(
        body, name="attn_fwd", grid=(t // tq,),
        in_specs=in_specs,
        out_specs=[pl.BlockSpec((tq, nh * dv), lambda i: (i, 0)), pl.BlockSpec((8, tq), lambda i: (0, i))],
        out_shape=[jax.ShapeDtypeStruct((t, nh * dv), BF16), jax.ShapeDtypeStruct((8, t), F32)],
        compiler_params=_cp("arbitrary"),
    )(*args)


def attn_bwd(q, k, v, do, lse_row, o, qcol, kcol, vcol, docol, ocol, nh, dq, dv, scale, c_col=None, c_row=None, lane0=0):
    t = q.shape[0]
    tq = BLK
    use_bias = c_col is not None
    nq = t // tq

    def body(*refs):
        if use_bias:
            (q_ref, k_ref, v_ref, do_ref, l_ref, o_ref, cc_ref, cr_ref, dq_ref, dk_ref, dv_ref, dcq_ref, dck_ref,
             kt, ckb, dacc) = refs
        else:
            q_ref, k_ref, v_ref, do_ref, l_ref, o_ref, dq_ref, dk_ref, dv_ref, kt = refs
        i = pl.program_id(0)

        @pl.when(i == 0)
        def _():
            kt[...] = k_ref[...].astype(BF16).T
            dk_ref[...] = jnp.zeros_like(dk_ref)
            dv_ref[...] = jnp.zeros_like(dv_ref)
            if use_bias:
                dacc[...] = jnp.zeros_like(dacc)
                for h in range(nh):
                    ckb[h] = jnp.broadcast_to(cc_ref[:, lane0 + h:lane0 + h + 1], (t, BLK))

        qry = i * tq + lax.broadcasted_iota(jnp.int32, (1, tq), 1)
        dot = (do_ref[...].astype(F32) * o_ref[...].astype(F32)).T

        def tile(tk):
            key = lax.broadcasted_iota(jnp.int32, (tk, 1), 0)
            mask = (key <= qry) & (key >= PAD)
            dqts, dcqs = [], []
            for h in range(nh):
                qh = q_ref[:, dq * h:dq * (h + 1)].astype(BF16)
                kh = k_ref[0:tk, dq * h:dq * (h + 1)].astype(BF16)
                vh = v_ref[0:tk, dv * h:dv * (h + 1)].astype(BF16)
                doh = do_ref[:, dv * h:dv * (h + 1)].astype(BF16)
                delta = jnp.sum(dot[dv * h:dv * (h + 1), :], axis=0, keepdims=True)
                st = _nt(kh, qh) * scale
                if use_bias:
                    st = st + (cr_ref[h:h + 1, :] - ckb[h, 0:tk, :])
                pt = jnp.exp(jnp.where(mask, st, NEG) - l_ref[h:h + 1, :])
                dst = pt * (_nt(vh, doh) - delta)
                dsb = dst.astype(BF16)
                dk_ref[0:tk, dq * h:dq * (h + 1)] += _nn(dsb, qh) * scale
                dv_ref[0:tk, dv * h:dv * (h + 1)] += _nn(pt.astype(BF16), doh)
                dqts.append(_nn(kt[dq * h:dq * (h + 1), 0:tk], dsb))
                if use_bias:
                    dcqs.append(jnp.sum(dst, axis=0, keepdims=True))
                    dacc[h, 0:tk, :] += dst
            dq_ref[...] = jnp.concatenate(dqts, axis=0).T * scale
            if use_bias:
                dcq_ref[...] = jnp.concatenate(dcqs + [jnp.zeros((8 - nh, tq), F32)], axis=0)

        for t0, t1 in _segments(nq, not use_bias):
            pl.when((i >= t0) & (i < t1))(functools.partial(tile, t1 * BLK))

        if use_bias:
            @pl.when(i == nq - 1)
            def _():
                lane = lax.broadcasted_iota(jnp.int32, (1, BLK), 1)
                tot = jnp.zeros((t, BLK), F32)
                for h in range(nh):
                    tot += jnp.where(lane == lane0 + h, jnp.sum(dacc[h], axis=1, keepdims=True), 0.0)
                dck_ref[...] = tot

    keys_q = pl.BlockSpec((t, nh * dq), lambda i: (0, 0))
    keys_v = pl.BlockSpec((t, nh * dv), lambda i: (0, 0))
    keys_c = pl.BlockSpec((t, BLK), lambda i: (0, 0))
    qrow = pl.BlockSpec((8, tq), lambda i: (0, i))
    in_specs = [pl.BlockSpec((tq, nh * dq), lambda i: (i, qcol)),
                pl.BlockSpec((t, nh * dq), lambda i: (0, kcol)),
                pl.BlockSpec((t, nh * dv), lambda i: (0, vcol)),
                pl.BlockSpec((tq, nh * dv), lambda i: (i, docol)),
                qrow,
                pl.BlockSpec((tq, nh * dv), lambda i: (i, ocol))]
    args = [q, k, v, do, lse_row, o]
    out_specs = [pl.BlockSpec((tq, nh * dq), lambda i: (i, 0)), keys_q, keys_v]
    out_shape = [jax.ShapeDtypeStruct((t, nh * dq), F32), jax.ShapeDtypeStruct((t, nh * dq), F32),
                 jax.ShapeDtypeStruct((t, nh * dv), F32)]
    scratch = [pltpu.VMEM((nh * dq, t), BF16)]
    if use_bias:
        in_specs += [keys_c, qrow]
        args += [c_col, c_row]
        out_specs += [qrow, keys_c]
        out_shape += [jax.ShapeDtypeStruct((8, t), F32), jax.ShapeDtypeStruct((t, BLK), F32)]
        scratch += [pltpu.VMEM((nh, t, BLK), F32), pltpu.VMEM((nh, t, BLK), F32)]
    return pl.pallas_call(
        body, name="attn_bwd", grid=(nq,),
        in_specs=in_specs, out_specs=out_specs, out_shape=out_shape, scratch_shapes=scratch,
        compiler_params=_cp("arbitrary"),
    )(*args)


def fox_pre(proj, fb):
    t = proj.shape[0]
    nb = t // BLK

    def body(sm_ref, fb_ref, c_ref, cr_ref):
        x = sm_ref[...] + fb_ref[...]
        lane = lax.broadcasted_iota(jnp.int32, (1, BLK), 1)
        keep = _valid_rows(t, 0) & (lane >= SM_F) & (lane < SM_F + FOX_H)
        logf = jnp.where(keep, jnp.minimum(x, 0.0) - jnp.log(1.0 + jnp.exp(-jnp.abs(x))), 0.0)
        tri = _tri().astype(F32)
        carry = jnp.zeros((1, BLK), F32)
        for b in range(nb):
            cb = _nn_hi(tri, logf[b * BLK:(b + 1) * BLK, :]) + carry
            c_ref[b * BLK:(b + 1) * BLK, :] = cb
            carry = cb[BLK - 1:BLK, :]
        cr_ref[...] = c_ref[...].T[SM_F:SM_F + 8, :]

    return pl.pallas_call(
        body, name="fox_pre", grid=(1,),
        in_specs=[pl.BlockSpec((t, BLK), lambda i: (0, C_SM // BLK)), pl.BlockSpec((1, BLK), lambda i: (0, 0))],
        out_specs=[pl.BlockSpec((t, BLK), lambda i: (0, 0)), pl.BlockSpec((8, t), lambda i: (0, 0))],
        out_shape=[jax.ShapeDtypeStruct((t, BLK), F32), jax.ShapeDtypeStruct((8, t), F32)],
        compiler_params=_cp("arbitrary"),
    )(proj, fb)


def fox_pre_bwd(dcq, dck, proj, fb, dsm_in):
    t = proj.shape[0]
    nb = t // BLK

    def body(dcq_ref, dck_ref, sm_ref, fb_ref, din_ref, dsm_ref, dfb_ref, scr):
        triu = _tri(lower=False).astype(F32)
        carry = jnp.zeros((1, BLK), F32)
        scr[...] = jnp.concatenate([jnp.zeros((SM_F, t), F32), dcq_ref[...], jnp.zeros((BLK - SM_F - 8, t), F32)], axis=0).T
        for b in range(nb - 1, -1, -1):
            blk = scr[b * BLK:(b + 1) * BLK, :] - dck_ref[b * BLK:(b + 1) * BLK, :]
            cb = _nn_hi(triu, blk) + carry
            scr[b * BLK:(b + 1) * BLK, :] = cb
            carry = cb[0:1, :]
        x = sm_ref[...] + fb_ref[...]
        lane = lax.broadcasted_iota(jnp.int32, (1, BLK), 1)
        keep = _valid_rows(t, 0) & (lane >= SM_F) & (lane < SM_F + FOX_H)
        df = jnp.where(keep, scr[...] * _sig(-x), 0.0)
        dfb_ref[...] = jnp.sum(df, axis=0, keepdims=True)
        dsm_ref[...] = din_ref[...] + df

    full = pl.BlockSpec((t, BLK), lambda i: (0, 0))
    return pl.pallas_call(
        body, name="fox_pre_bwd", grid=(1,),
        in_specs=[pl.BlockSpec((8, t), lambda i: (0, 0)), full,
                  pl.BlockSpec((t, BLK), lambda i: (0, C_SM // BLK)), pl.BlockSpec((1, BLK), lambda i: (0, 0)), full],
        out_specs=[full, pl.BlockSpec((1, BLK), lambda i: (0, 0))],
        out_shape=[jax.ShapeDtypeStruct((t, BLK), F32), jax.ShapeDtypeStruct((1, BLK), F32)],
        scratch_shapes=[pltpu.VMEM((t, BLK), F32)],
        compiler_params=_cp("arbitrary"),
    )(dcq, dck, proj, fb, dsm_in)


def _swap_rope(x):
    lane = lax.broadcasted_iota(jnp.int32, (1, BLK), 1)
    return jnp.where((lane >= SM_KR) & (lane < SM_KR + 16), pltpu.roll(x, BLK - 16, 1),
                     jnp.where((lane >= SM_KR + 16) & (lane < SM_KR + 32), pltpu.roll(x, 16, 1), 0.0))


def _rms(x, g):
    r = lax.rsqrt(jnp.mean(x * x, axis=1, keepdims=True) + EPS)
    return r, x * r


def mla_pre(proj, qg, kvg, wq, wk, wv, cosq, sinq):
    t = proj.shape[0]
    tm = _row_tile(t)

    def body(cq_ref, ckv_ref, sm_ref, qg_ref, kvg_ref, wq_ref, wk_ref, wv_ref, cos_ref, sin_ref,
             q_ref, k_ref, v_ref, cqn_ref, ckvn_ref):
        cs, sn = cos_ref[...], sin_ref[...]
        _, xh = _rms(cq_ref[...], None)
        cqn = (xh * qg_ref[...]).astype(BF16)
        cqn_ref[...] = cqn
        qraw = _nn(cqn, wq_ref[...])
        qs = []
        for h in range(MLA_H):
            hb = qraw[:, BLK * h:BLK * (h + 1)]
            qs.append(hb * cs + _swap_rope(hb) * sn)
        q_ref[...] = jnp.concatenate(qs, axis=1).astype(BF16)
        _, kh = _rms(ckv_ref[...], None)
        ckvn = (kh * kvg_ref[...]).astype(BF16)
        ckvn_ref[...] = ckvn
        kraw = _nn(ckvn, wk_ref[...])
        v_ref[...] = _nn(ckvn, wv_ref[...]).astype(BF16)
        lane = lax.broadcasted_iota(jnp.int32, (1, BLK), 1)
        kr = sm_ref[...]
        krr = jnp.where((lane >= SM_KR) & (lane < SM_KR + MLA_ROPE), kr * cs + _swap_rope(kr) * sn, 0.0)
        k_ref[...] = jnp.concatenate([kraw[:, BLK * h:BLK * (h + 1)] + krr for h in range(MLA_H)], axis=1).astype(BF16)

    def rows(w, cb):
        return pl.BlockSpec((tm, w), lambda i: (i, cb))

    def whole(a):
        return pl.BlockSpec(a.shape, lambda i: (0, 0))

    return pl.pallas_call(
        body, name="mla_pre", grid=(t // tm,),
        in_specs=[rows(MLA_QL, C_CQ // MLA_QL), rows(MLA_KVL, C_CKV // MLA_KVL), rows(BLK, C_SM // BLK),
                  whole(qg), whole(kvg), whole(wq), whole(wk), whole(wv), rows(BLK, 0), rows(BLK, 0)],
        out_specs=[rows(512, 0), rows(512, 0), rows(256, 0), rows(MLA_QL, 0), rows(MLA_KVL, 0)],
        out_shape=[jax.ShapeDtypeStruct((t, 512), BF16), jax.ShapeDtypeStruct((t, 512), BF16),
                   jax.ShapeDtypeStruct((t, 256), BF16), jax.ShapeDtypeStruct((t, MLA_QL), BF16),
                   jax.ShapeDtypeStruct((t, MLA_KVL), BF16)],
        compiler_params=_cp("arbitrary"),
    )(proj, proj, proj, qg, kvg, wq, wk, wv, cosq, sinq)


def mla_pre_bwd(dq, dk, dv, proj, cqn, ckvn, qg, kvg, wq, wk, wv, cosq, sinq, dsm_in):
    t = proj.shape[0]
    tm = _row_tile(t)

    def body(dq_ref, dk_ref, dv_ref, cq_ref, ckv_ref, cqn_ref, ckvn_ref, qg_ref, kvg_ref, wq_ref, wk_ref, wv_ref,
             cos_ref, sin_ref, din_ref, dcq_ref, dckv_ref, dsm_ref, dwq_ref, dwk_ref, dwv_ref, dqg_ref, dkvg_ref):
        i = pl.program_id(0)

        @pl.when(i == 0)
        def _():
            for r in (dwq_ref, dwk_ref, dwv_ref, dqg_ref, dkvg_ref):
                r[...] = jnp.zeros_like(r)

        cs, sn = cos_ref[...], sin_ref[...]
        lane = lax.broadcasted_iota(jnp.int32, (1, BLK), 1)

        def unrope(dy):
            return dy * cs + _swap_rope(dy * sn)

        dqp = jnp.concatenate([unrope(dq_ref[:, BLK * h:BLK * (h + 1)]) for h in range(MLA_H)], axis=1).astype(BF16)
        dwq_ref[...] += _tn(cqn_ref[...], dqp)
        dcqn = _nt(dqp, wq_ref[...])
        r, xh = _rms(cq_ref[...], None)
        dqg_ref[...] += jnp.sum(dcqn * xh, axis=0, keepdims=True)
        dxh = dcqn * qg_ref[...]
        dcq_ref[...] = r * (dxh - xh * jnp.mean(dxh * xh, axis=1, keepdims=True))

        dkn, dkr = [], jnp.zeros((tm, BLK), F32)
        for h in range(MLA_H):
            blk = dk_ref[:, BLK * h:BLK * (h + 1)]
            dkn.append(jnp.where(lane < MLA_NOPE, blk, 0.0))
            dkr += jnp.where((lane >= SM_KR) & (lane < SM_KR + MLA_ROPE), blk, 0.0)
        dknb = jnp.concatenate(dkn, axis=1).astype(BF16)
        dvb = dv_ref[...].astype(BF16)
        ckvn = ckvn_ref[...]
        dwk_ref[...] += _tn(ckvn, dknb)
        dwv_ref[...] += _tn(ckvn, dvb)
        dckvn = _nt(dknb, wk_ref[...]) + _nt(dvb, wv_ref[...])
        r2, kh = _rms(ckv_ref[...], None)
        dkvg_ref[...] += jnp.sum(dckvn * kh, axis=0, keepdims=True)
        dkh = dckvn * kvg_ref[...]
        dckv_ref[...] = r2 * (dkh - kh * jnp.mean(dkh * kh, axis=1, keepdims=True))
        dsm_ref[...] = din_ref[...] + jnp.where((lane >= SM_KR) & (lane < SM_KR + MLA_ROPE), unrope(dkr), 0.0)

    def rows(w, cb):
        return pl.BlockSpec((tm, w), lambda i: (i, cb))

    def whole(a):
        return pl.BlockSpec(a.shape, lambda i: (0, 0))

    def wshape(a):
        return jax.ShapeDtypeStruct(a.shape, F32)

    return pl.pallas_call(
        body, name="mla_pre_bwd", grid=(t // tm,),
        in_specs=[rows(512, 0), rows(512, 0), rows(256, 0), rows(MLA_QL, C_CQ // MLA_QL), rows(MLA_KVL, C_CKV // MLA_KVL),
                  rows(MLA_QL, 0), rows(MLA_KVL, 0), whole(qg), whole(kvg), whole(wq), whole(wk), whole(wv),
                  rows(BLK, 0), rows(BLK, 0), rows(BLK, 0)],
        out_specs=[rows(MLA_QL, 0), rows(MLA_KVL, 0), rows(BLK, 0), whole(wq), whole(wk), whole(wv), whole(qg), whole(kvg)],
        out_shape=[jax.ShapeDtypeStruct((t, MLA_QL), F32), jax.ShapeDtypeStruct((t, MLA_KVL), F32),
                   jax.ShapeDtypeStruct((t, BLK), F32), wshape(wq), wshape(wk), wshape(wv), wshape(qg), wshape(kvg)],
        compiler_params=_cp("arbitrary"),
    )(dq, dk, dv, proj, proj, cqn, ckvn, qg, kvg, wq, wk, wv, cosq, sinq, dsm_in)


def _slot_sum(me, own, recv_ref):
    gg = own.astype(F32)
    for s in range(N_DEV):
        gg = gg + jnp.where(me == s, 0.0, recv_ref[s].astype(F32))
    return gg


def adamw(w, m, v, g=None, recv=None, own=None, me_arr=None):
    shape = w.shape
    c = shape[-1]
    from_recv = recv is not None
    if not from_recv:
        me_arr = jnp.zeros((1,), jnp.int32)
    nl = len(recv) if from_recv else 1
    rws = w.size // c // nl
    tr = rws
    for d in (1024, 512, 352, 256, 128, 64, 32, 16, 8):
        if rws % d == 0 and d * c * 4 <= (2 << 20):
            tr = d
            break
    nt = rws // tr
    w2, m2, v2 = (a.reshape(nl, rws, c) for a in (w, m, v))
    if from_recv:
        gin = [a.reshape(N_DEV, rws, c) for a in list(recv) + list(own)]
    else:
        gin = [g.reshape(1, rws, c)]

    def body(me_ref, w_ref, m_ref, v_ref, *rest):
        g_refs, outs = rest[:len(gin)], rest[len(gin):]
        if from_recv:
            g_out, outs = outs[0], outs[1:]
            for li in range(nl):
                @pl.when(pl.program_id(0) == li)
                def _(li=li):
                    g_out[...] = _slot_sum(me_ref[0], g_refs[nl + li][...], g_refs[li])
            gg = g_out[...]
        else:
            gg = g_refs[0][...]
        d_ref, nm_ref, nv_ref = outs
        nm = B1 * m_ref[...] + (1.0 - B1) * gg
        nv = B2 * v_ref[...] + (1.0 - B2) * (gg * gg)
        mh = nm / (1.0 - B1 ** STEP)
        vh = nv / (1.0 - B2 ** STEP)
        d_ref[...] = -LR * (mh / (jnp.sqrt(vh) + AEPS) + WD * w_ref[...])
        nm_ref[...] = nm
        nv_ref[...] = nv

    row = pl.BlockSpec((None, tr, c), lambda l, i, me: (l, i, 0))
    if from_recv:
        gspecs = [pl.BlockSpec((N_DEV, tr, c), lambda l, i, me, li=li: (0, jnp.where(l == li, i, 0), 0))
                  for li in range(nl)]
        gspecs += [pl.BlockSpec((None, tr, c), lambda l, i, me, li=li: (me[0], jnp.where(l == li, i, 0), 0))
                   for li in range(nl)]
    else:
        gspecs = [row]
    nout = 4 if from_recv else 3
    outs = pl.pallas_call(
        body, name="adamw",
        grid_spec=pltpu.PrefetchScalarGridSpec(num_scalar_prefetch=1, grid=(nl, nt), in_specs=[row, row, row] + gspecs,
                                               out_specs=[row] * nout),
        out_shape=[jax.ShapeDtypeStruct((nl, rws, c), F32)] * nout,
        compiler_params=_cp("arbitrary", "arbitrary"),
    )(me_arr, w2, m2, v2, *gin)
    return tuple(o.reshape(shape) for o in outs)


def sum_slots(recv, own=None, me_arr=None):
    _, r, c = recv.shape
    if own is None:
        own, me_arr = recv, jnp.zeros((1,), jnp.int32)
        plain = True
    else:
        plain = False

    def body(me_ref, r_ref, own_ref, o_ref):
        if plain:
            gg = r_ref[0].astype(F32)
            for s in range(1, N_DEV):
                gg = gg + r_ref[s].astype(F32)
            o_ref[...] = gg
        else:
            o_ref[...] = _slot_sum(me_ref[0], own_ref[...], r_ref)

    return pl.pallas_call(
        body, name="sum_slots",
        grid_spec=pltpu.PrefetchScalarGridSpec(
            num_scalar_prefetch=1, grid=(1,),
            in_specs=[pl.BlockSpec((N_DEV, r, c), lambda i, me: (0, 0, 0)),
                      pl.BlockSpec((None, r, c), lambda i, me: (me[0], 0, 0))],
            out_specs=pl.BlockSpec((r, c), lambda i, me: (0, 0))),
        out_shape=jax.ShapeDtypeStruct((r, c), F32),
        compiler_params=_cp("arbitrary"),
    )(me_arr, recv, own)


_FLIPS = [(0, 0, 1), (0, 1, 0), (0, 1, 1), (1, 0, 0), (1, 0, 1), (1, 1, 0), (1, 1, 1)]
_ANY = pl.BlockSpec(memory_space=pl.ANY)


def _mesh_place():
    x, y, c = lax.axis_index("x"), lax.axis_index("y"), lax.axis_index("c")
    me = 4 * x + 2 * y + c
    peers = [((x + fx) % 2, (y + fy) % 2, (c + fc) % 2) for fx, fy, fc in _FLIPS]
    return me, peers


def place_own(src, l, dtype, me_arr):
    _, r, c = src.shape
    tr = r
    for d in (512, 352, 256, 128, 64, 32, 16, 8):
        if r % d == 0 and d * c * 4 <= (2 << 20):
            tr = d
            break

    def body(me_ref, s_ref, o_ref):
        o_ref[...] = s_ref[...].astype(dtype)

    return pl.pallas_call(
        body, name="place_own",
        grid_spec=pltpu.PrefetchScalarGridSpec(
            num_scalar_prefetch=1, grid=(r // tr,),
            in_specs=[pl.BlockSpec((None, tr, c), lambda i, me: (l, i, 0))],
            out_specs=pl.BlockSpec((None, tr, c), lambda i, me: (me[0], i, 0))),
        out_shape=jax.ShapeDtypeStruct((N_DEV, r, c), dtype),
        compiler_params=_cp("arbitrary"),
    )(me_arr, src)


_HBM = pl.BlockSpec(memory_space=pltpu.HBM)
_SEMS = pl.BlockSpec(memory_space=pltpu.SEMAPHORE)
_EFFECT = pltpu.SideEffectType.DATAFLOW_SIDE_EFFECTING


def exchange_start(mode, arrays, name, after=None):
    n = len(arrays)
    gather = mode == "gather"
    ns = 0 if gather else n
    zones = list(arrays) if gather else [lax.empty(a.shape, a.dtype) for a in arrays]
    ops = ([] if gather else list(arrays)) + zones
    extra = [] if after is None else [after]

    def body(*refs):
        srcs, lands = refs[:ns], refs[ns:ns + n]
        send_sems, recv_sems = refs[ns + n + len(extra)], refs[ns + n + len(extra) + 1]
        token = refs[-1]
        me, peers = _mesh_place()
        ids = [4 * p[0] + 2 * p[1] + p[2] for p in peers]
        for j in range(n):
            for k in range(N_DEV - 1):
                src = lands[j].at[me] if gather else srcs[j].at[ids[k]]
                pltpu.make_async_remote_copy(src_ref=src, dst_ref=lands[j].at[me],
                                             send_sem=send_sems.at[j * (N_DEV - 1) + k],
                                             recv_sem=recv_sems.at[j * (N_DEV - 1) + k], device_id=peers[k],
                                             device_id_type=pl.DeviceIdType.MESH).start()
        token[...] = jnp.zeros_like(token)

    nsem = n * (N_DEV - 1)
    res = pl.pallas_call(
        body, name=name,
        in_specs=[_HBM] * (ns + n) + [_ANY] * len(extra),
        out_specs=(_SEMS, _SEMS, *[_HBM] * (ns + n), pl.BlockSpec(memory_space=pltpu.VMEM)),
        out_shape=(pltpu.SemaphoreType.DMA((nsem,)), pltpu.SemaphoreType.DMA((nsem,)),
                   *[pltpu.HBM(a.shape, a.dtype) for a in ops], jax.ShapeDtypeStruct((8, BLK), F32)),
        input_output_aliases={i: 2 + i for i in range(ns + n)},
        compiler_params=pltpu.CompilerParams(has_side_effects=_EFFECT),
    )(*[pltpu.with_memory_space_constraint(a, pltpu.HBM) for a in ops], *extra)
    return dict(gather=gather, send=res[0], recv=res[1], srcs=list(res[2:2 + ns]), lands=list(res[2 + ns:2 + ns + n]),
                token=res[-1])


def exchange_wait(hd, idxs, name, after):
    gather = hd["gather"]
    n = len(idxs)
    ns = 0 if gather else n
    ops = ([] if gather else [hd["srcs"][j] for j in idxs]) + [hd["lands"][j] for j in idxs]

    def body(*refs):
        srcs, lands = refs[:ns], refs[ns:ns + n]
        send_sems, recv_sems = refs[ns + n], refs[ns + n + 1]
        me, peers = _mesh_place()
        ids = [4 * p[0] + 2 * p[1] + p[2] for p in peers]
        for p, j in enumerate(idxs):
            for k in range(N_DEV - 1):
                src = lands[p].at[me] if gather else srcs[p].at[ids[k]]
                cp = pltpu.make_async_remote_copy(src_ref=src, dst_ref=lands[p].at[ids[k]],
                                                  send_sem=send_sems.at[j * (N_DEV - 1) + k],
                                                  recv_sem=recv_sems.at[j * (N_DEV - 1) + k], device_id=peers[k],
                                                  device_id_type=pl.DeviceIdType.MESH)
                cp.wait_send()
                cp.wait_recv()

    res = pl.pallas_call(
        body, name=name,
        in_specs=[_HBM] * (ns + n) + [_SEMS, _SEMS, _ANY],
        out_specs=[_HBM] * (ns + n),
        out_shape=[pltpu.HBM(a.shape, a.dtype) for a in ops],
        input_output_aliases={i: i for i in range(ns + n)},
        compiler_params=pltpu.CompilerParams(has_side_effects=_EFFECT),
    )(*ops, hd["send"], hd["recv"], after)
    return list(res[:ns]), list(res[ns:])


def _chip_place():
    x, y, c = lax.axis_index("x"), lax.axis_index("y"), lax.axis_index("c")
    chips = [((x + 1) % 2, y), (x, (y + 1) % 2), ((x + 1) % 2, (y + 1) % 2)]
    ident = lambda p: 4 * p[0] + 2 * p[1] + p[2]
    return dict(me=4 * x + 2 * y + c, sib=(x, y, 1 - c), sib_id=4 * x + 2 * y + 1 - c,
                same=[(cx, cy, c) for cx, cy in chips], same_ids=[ident((cx, cy, c)) for cx, cy in chips],
                other_ids=[ident((cx, cy, 1 - c)) for cx, cy in chips])


def _remote(src, dst, send_sem, recv_sem, dev):
    return pltpu.make_async_remote_copy(src_ref=src, dst_ref=dst, send_sem=send_sem, recv_sem=recv_sem, device_id=dev,
                                        device_id_type=pl.DeviceIdType.MESH)


def gather_start(zones, name):
    n = len(zones)

    def body(*refs):
        lands, send_sems, recv_sems, token = refs[:n], refs[n], refs[n + 1], refs[-1]
        pc = _chip_place()
        for j in range(n):
            own = lands[j].at[pc["me"]]
            for k, dev in enumerate([pc["sib"]] + pc["same"]):
                _remote(own, own, send_sems.at[4 * j + k], recv_sems.at[4 * j + k], dev).start()
        token[...] = jnp.zeros_like(token)

    res = pl.pallas_call(
        body, name=name,
        in_specs=[_HBM] * n,
        out_specs=(_SEMS, _SEMS, *[_HBM] * n, pl.BlockSpec(memory_space=pltpu.VMEM)),
        out_shape=(pltpu.SemaphoreType.DMA((4 * n,)), pltpu.SemaphoreType.DMA((4 * n,)),
                   *[pltpu.HBM(a.shape, a.dtype) for a in zones], jax.ShapeDtypeStruct((8, BLK), F32)),
        input_output_aliases={i: 2 + i for i in range(n)},
        compiler_params=pltpu.CompilerParams(has_side_effects=_EFFECT),
    )(*[pltpu.with_memory_space_constraint(a, pltpu.HBM) for a in zones])
    return dict(send=res[0], recv=res[1], lands=list(res[2:2 + n]), token=res[-1])


def gather_relay(hd, idxs, name, after):
    n = len(idxs)

    def body(*refs):
        lands, send_sems, recv_sems = refs[:n], refs[n], refs[n + 1]
        fsend, frecv, token = refs[n + 3 + n], refs[n + 4 + n], refs[-1]
        pc = _chip_place()
        for p, j in enumerate(idxs):
            for k in range(3):
                _remote(lands[p].at[pc["me"]], lands[p].at[pc["same_ids"][k]], send_sems.at[4 * j + 1 + k],
                        recv_sems.at[4 * j + 1 + k], pc["same"][k]).wait_recv()
        for p in range(n):
            for k in range(3):
                got = lands[p].at[pc["same_ids"][k]]
                _remote(got, got, fsend.at[3 * p + k], frecv.at[3 * p + k], pc["sib"]).start()
        token[...] = jnp.zeros_like(token)

    ops = [hd["lands"][j] for j in idxs]
    res = pl.pallas_call(
        body, name=name,
        in_specs=[_HBM] * n + [_SEMS, _SEMS, _ANY],
        out_specs=(*[_HBM] * n, _SEMS, _SEMS, pl.BlockSpec(memory_space=pltpu.VMEM)),
        out_shape=(*[pltpu.HBM(a.shape, a.dtype) for a in ops], pltpu.SemaphoreType.DMA((3 * n,)),
                   pltpu.SemaphoreType.DMA((3 * n,)), jax.ShapeDtypeStruct((8, BLK), F32)),
        input_output_aliases={i: i for i in range(n)},
        compiler_params=pltpu.CompilerParams(has_side_effects=_EFFECT),
    )(*ops, hd["send"], hd["recv"], after)
    return dict(lands=list(res[:n]), fsend=res[n], frecv=res[n + 1], token=res[-1])


def gather_wait(hd, rl, idxs, name, after):
    n = len(idxs)

    def body(*refs):
        lands, send_sems, recv_sems, fsend, frecv = refs[:n], refs[n], refs[n + 1], refs[n + 2], refs[n + 3]
        pc = _chip_place()
        for p, j in enumerate(idxs):
            own = lands[p].at[pc["me"]]
            for k, dev in enumerate([pc["sib"]] + pc["same"]):
                _remote(own, own, send_sems.at[4 * j + k], recv_sems.at[4 * j + k], dev).wait_send()
            _remote(own, lands[p].at[pc["sib_id"]], send_sems.at[4 * j], recv_sems.at[4 * j], pc["sib"]).wait_recv()
            for k in range(3):
                cp = _remote(lands[p].at[pc["same_ids"][k]], lands[p].at[pc["other_ids"][k]], fsend.at[3 * p + k],
                             frecv.at[3 * p + k], pc["sib"])
                cp.wait_send()
                cp.wait_recv()

    res = pl.pallas_call(
        body, name=name,
        in_specs=[_HBM] * n + [_SEMS, _SEMS, _SEMS, _SEMS, _ANY],
        out_specs=[_HBM] * n,
        out_shape=[pltpu.HBM(a.shape, a.dtype) for a in rl["lands"]],
        input_output_aliases={i: i for i in range(n)},
        compiler_params=pltpu.CompilerParams(has_side_effects=_EFFECT),
    )(*rl["lands"], hd["send"], hd["recv"], rl["fsend"], rl["frecv"], after)
    return list(res)


def _pad_cols(a, n):
    return jnp.pad(a, ((0, 0),) * (a.ndim - 1) + ((0, n - a.shape[-1]),))


def w_in_to_padded(w):
    z = lambda n: jnp.zeros(w.shape[:-1] + (n,), w.dtype)
    return jnp.concatenate([
        w[..., 0:1280], w[..., 1288:2056], w[..., 2060:2316], w[..., 2316:2444],
        w[..., 1280:1288], w[..., 2056:2060], z(SM_KR - SM_F - FOX_H), w[..., 2444:2476], z(BLK - SM_KR - MLA_ROPE)], axis=-1)


def w_in_from_padded(g):
    s = C_SM
    return jnp.concatenate([
        g[..., 0:1280], g[..., s + SM_DT:s + SM_DT + 8], g[..., 1280:2048], g[..., s + SM_F:s + SM_F + 4],
        g[..., 2048:2304], g[..., 2304:2432], g[..., s + SM_KR:s + SM_KR + MLA_ROPE]], axis=-1)


def _unshard_cols(gth):
    n, r, c = gth.shape
    return jnp.transpose(gth, (1, 0, 2)).reshape(r, n * c)


def _shard_cols(full):
    r, nc = full.shape
    return jnp.transpose(full.reshape(r, N_DEV, nc // N_DEV), (1, 0, 2))


def mla_weights(uq_g, ukv_g):
    uq = _unshard_cols(uq_g)
    dqh = MLA_NOPE + MLA_ROPE
    wq = jnp.concatenate([_pad_cols(uq[:, dqh * h:dqh * (h + 1)], BLK) for h in range(MLA_H)], axis=1)
    wk = jnp.concatenate([_pad_cols(ukv_g[2 * h], BLK) for h in range(MLA_H)], axis=1)
    wv = jnp.concatenate([ukv_g[2 * h + 1] for h in range(MLA_H)], axis=1)
    return wq, wk, wv


def mla_weight_grads(dwq, dwk, dwv):
    dqh = MLA_NOPE + MLA_ROPE
    duq = _shard_cols(jnp.concatenate([dwq[:, BLK * h:BLK * h + dqh] for h in range(MLA_H)], axis=1))
    parts = []
    for h in range(MLA_H):
        parts += [dwk[:, BLK * h:BLK * h + MLA_NOPE], dwv[:, MLA_V * h:MLA_V * (h + 1)]]
    return duq, jnp.stack(parts, axis=0)


def rope_tables(t):
    pos = (jnp.arange(t, dtype=jnp.int32) - PAD).astype(F32)
    inv_freq = 1.0 / (10000.0 ** (jnp.arange(0, MLA_ROPE, 2, dtype=F32) / MLA_ROPE))
    ang = pos[:, None] * inv_freq[None, :]
    cos, sin = jnp.cos(ang), jnp.sin(ang)
    one, zero = jnp.ones((t, SM_KR), F32), jnp.zeros((t, SM_KR), F32)
    tail = BLK - SM_KR - MLA_ROPE
    cosq = jnp.concatenate([one, cos, cos, jnp.ones((t, tail), F32)], axis=1)
    sinq = jnp.concatenate([zero, -sin, sin, jnp.zeros((t, tail), F32)], axis=1)
    return cosq, sinq


def _lanes(v, off=0):
    return jnp.pad(v.astype(F32), (off, BLK - off - v.shape[0]))[None, :]


def layer_fwd(x, ln, hb, getw, tabs, ahead):
    sv = {"h0b": hb}
    def behind(vec, tok):
        return vec if tok is None else vec + 0.0 * tok[0:1, 0:1]

    W = dict(getw("ffn1", hb))
    ln1 = (behind(W["ln1_g"], ahead(0, "mix", hb, 1)), W["ln1_b"])
    u, v, r1, h1b = ffn_fwd_seq(x, ln, W["g1"], W["u1"], W["d1"], ln1)
    sv.update(u1=u, v1=v, r1=r1, h1b=h1b)
    W.update(getw("mix", h1b))
    ln2 = (W["ln2_g"], W["ln2_b"])
    proj = mm_nn(h1b, W["w_in"])
    xa = conv_fwd(proj, W["conv_w"], W["conv_b"])
    y_ssd, sprev = ssd_fwd(xa, proj, W["dtb"], W["alog"], W["dskip"], W["normg"])
    c_col, c_row = fox_pre(proj, W["fb"])
    y_fox, lse_f = attn_fwd(proj, proj, proj, C_FQ // 256, C_FK // 256, C_FV // 256, FOX_H, FOX_DH, FOX_DH,
                            FOX_DH ** -0.5, c_col, c_row, SM_F)
    q, k, vv, cqn, ckvn = mla_pre(proj, behind(W["qg"], ahead(0, "ffn2", y_fox)), W["kvg"], W["wq"], W["wk"], W["wv"], *tabs)
    y_mla, lse_m = attn_fwd(q, k, vv, 0, 0, 0, MLA_H, BLK, MLA_V, (MLA_NOPE + MLA_ROPE) ** -0.5)
    mixcat = jnp.concatenate([y_ssd, y_fox, y_mla], axis=1)
    r2, h2b = mm_res_ln(mixcat, W["w_out"], r1, ln1, ln2)
    sv.update(proj=proj, xa=xa, sprev=sprev, c_col=c_col, c_row=c_row, lse_f=lse_f, q=q, k=k, v=vv, cqn=cqn, ckvn=ckvn,
              lse_m=lse_m, mixcat=mixcat, r2=r2, h2b=h2b)
    W.update(getw("ffn2", h2b))
    ln3 = (behind(W["ln3_g"], ahead(1, "ffn1", h2b)), W["ln3_b"])
    u, v, r3, h3b = ffn_fwd_seq(r2, ln2, W["g2"], W["u2"], W["d2"], ln3)
    sv.update(u2=u, v2=v, r3=r3, W=W)
    return r3, ln3, h3b, sv


def ffn_bwd(parts, r, gamma, hb_in, u, v, wg, wu, wd, after=None):
    dh, dwg, dwu, dwd, dg, db = ffn_bwd_seq(parts, r, gamma, hb_in, u, v, wg, wu, wd, after)
    return dh, dict(d=dwd, g=dwg, u=dwu, ln_g=dg, ln_b=db)


def layer_bwd(parts, sv, emit, tabs, after):
    G = {}
    W = sv["W"]
    dh2, g2 = ffn_bwd(parts, sv["r3"], W["ln3_g"], sv["h2b"], sv["u2"], sv["v2"], W["g2"], W["u2"], W["d2"], after)
    G.update(g2=g2["g"], u2=g2["u"], d2=g2["d"], ln3_g=g2["ln_g"], ln3_b=g2["ln_b"])
    tok = emit("ffn2", G)
    dr2, dmc, G["w_out"], G["ln2_g"], G["ln2_b"] = oproj_bwd(dh2, sv["r2"], W["ln2_g"], sv["mixcat"], W["w_out"], tok)
    proj = sv["proj"]
    dxa, dz, dsm, G["normg"], G["dskip"], G["alog"], G["dtb"] = ssd_bwd(
        dmc, sv["xa"], proj, sv["sprev"], W["dtb"], W["alog"], W["dskip"], W["normg"])
    dxbc, G["conv_w"], G["conv_b"] = conv_bwd(dxa, proj, W["conv_w"], W["conv_b"])
    dfq, dfk, dfv, dcq, dck = attn_bwd(proj, proj, proj, dmc, sv["lse_f"], sv["mixcat"], C_FQ // 256, C_FK // 256,
                                       C_FV // 256, 2, 2, FOX_H, FOX_DH, FOX_DH, FOX_DH ** -0.5, sv["c_col"], sv["c_row"], SM_F)
    dsm, G["fb"] = fox_pre_bwd(dcq, dck, proj, W["fb"], dsm)
    dq, dk, dv = attn_bwd(sv["q"], sv["k"], sv["v"], dmc, sv["lse_m"], sv["mixcat"], 0, 0, 0, 3, 3, MLA_H, BLK, MLA_V,
                          (MLA_NOPE + MLA_ROPE) ** -0.5)
    dcql, dckv, dsm, G["wq"], G["wk"], G["wv"], G["qg"], G["kvg"] = mla_pre_bwd(
        dq, dk, dv, proj, sv["cqn"], sv["ckvn"], W["qg"], W["kvg"], W["wq"], W["wk"], W["wv"], *tabs, dsm)
    dh1p, G["w_in"] = proj_bwd([dz, dxbc, dfq, dfk, dfv, dcql, dckv, dsm], sv["h1b"], W["w_in"])
    tok = emit("mix", G)
    dh0, g1 = ffn_bwd([(dr2, ALPHA), (dh1p, 1.0)], sv["r1"], W["ln1_g"], sv["h0b"], sv["u1"], sv["v1"],
                      W["g1"], W["u1"], W["d1"], tok)
    G.update(g1=g1["g"], u1=g1["u"], d1=g1["d"], ln1_g=g1["ln_g"], ln1_b=g1["ln_b"])
    tok = emit("ffn1", G)
    return [(dh0, 1.0)], G, tok


def local_step(x, target, meta_full, getw, emit, ahead=lambda l, stage, after, min_layer=0: None):
    t = x.shape[0] + BLK
    tabs = rope_tables(t)
    xr, hb = build_h0(meta_full, x)
    ln = None
    saved = []
    for l in range(NL):
        xr, ln, hb, sv = layer_fwd(xr, ln, hb, functools.partial(getw, l), tabs,
                                   lambda dl, stage, after, min_layer=0, l=l: ahead(l + dl, stage, after, min_layer))
        saved.append(sv)
    dy, loss = loss_head(xr, ln, target)
    parts = [(dy, 1.0)]
    grads = [None] * NL
    tok = None
    for l in range(NL - 1, -1, -1):
        parts, grads[l], tok = layer_bwd(parts, saved[l], functools.partial(emit, l), tabs, tok)
    gx, gmeta = split_dh0(parts[0][0], tok)
    return loss, gx, gmeta, grads


_SMALL = ["ln1_g", "ln1_b", "ln2_g", "ln2_b", "ln3_g", "ln3_b", "conv_b", "ssd_norm_g", "mla_q_norm_g",
          "mla_kv_norm_g", "dt_bias", "a_log", "d_skip", "fox_f_b"]
_SMALL_ROWS = 8
_BIG = ["ffn1_w_gate", "ffn1_w_up", "ffn1_w_down", "w_in", "conv_w", "mla_w_uq", "mla_w_ukv", "w_out",
        "ffn2_w_gate", "ffn2_w_up", "ffn2_w_down"]
_NAMES = ["meta", "ffn1_w_gate", "ffn1_w_up", "ffn1_w_down", "ln1_g", "ln1_b", "w_in", "conv_w", "conv_b", "dt_bias",
          "a_log", "d_skip", "ssd_norm_g", "fox_f_b", "mla_q_norm_g", "mla_w_uq", "mla_kv_norm_g", "mla_w_ukv", "w_out",
          "ln2_g", "ln2_b", "ffn2_w_gate", "ffn2_w_up", "ffn2_w_down", "ln3_g", "ln3_b"]


def pack_small(p):
    flat = jnp.concatenate([p[n].astype(F32) for n in _SMALL], axis=1)
    return _pad_cols(flat, _SMALL_ROWS * D).reshape(NL * _SMALL_ROWS, D)


def unpack_small(a, like):
    flat = a.reshape(NL, _SMALL_ROWS * D)
    out, at = {}, 0
    for n in _SMALL:
        out[n] = flat[:, at:at + like[n].shape[1]]
        at += like[n].shape[1]
    return out


_STAGES = {"ffn1": ["ffn1_w_gate", "ffn1_w_up", "ffn1_w_down"],
           "mix": ["w_in", "conv_w", "mla_w_uq", "mla_w_ukv", "w_out"],
           "ffn2": ["ffn2_w_gate", "ffn2_w_up", "ffn2_w_down"]}


_FFN_T = ("ffn1_w_gate", "ffn1_w_up", "ffn2_w_gate", "ffn2_w_up")


def stage_weights(l, stage, g, rep):
    if stage != "mix":
        i = stage[3]
        return {"g" + i: g[f"ffn{i}_w_gate"].reshape(D_FF, D), "u" + i: g[f"ffn{i}_w_up"].reshape(D_FF, D),
                "d" + i: g[f"ffn{i}_w_down"].reshape(D_FF, D),
                "ln1_g" if i == "1" else "ln3_g": rep["ln1_g" if i == "1" else "ln3_g"][l][None, :],
                "ln1_b" if i == "1" else "ln3_b": rep["ln1_b" if i == "1" else "ln3_b"][l][None, :]}
    W = {}
    W["w_in"] = g["w_in"].reshape(D, N_INP)
    W["w_out"] = g["w_out"].reshape(D, D)
    W["wq"], W["wk"], W["wv"] = mla_weights(g["mla_w_uq"], g["mla_w_ukv"])
    W["conv_w"] = _unshard_cols(g["conv_w"])
    for k in ("ln2_g", "ln2_b", "conv_b"):
        W[k] = rep[k][l][None, :]
    W["normg"] = rep["ssd_norm_g"][l][None, :]
    W["qg"] = rep["mla_q_norm_g"][l][None, :]
    W["kvg"] = rep["mla_kv_norm_g"][l][None, :]
    W["dtb"] = _lanes(rep["dt_bias"][l], SM_DT)
    W["alog"] = _lanes(rep["a_log"][l], SM_DT)
    W["dskip"] = _lanes(rep["d_skip"][l], SM_DT)
    W["fb"] = _lanes(rep["fox_f_b"][l], SM_F)
    return W


def small_grads(G):
    return {"ln1_g": G["ln1_g"][0], "ln1_b": G["ln1_b"][0], "ln2_g": G["ln2_g"][0], "ln2_b": G["ln2_b"][0],
            "ln3_g": G["ln3_g"][0], "ln3_b": G["ln3_b"][0], "conv_b": G["conv_b"][0], "ssd_norm_g": G["normg"][0],
            "mla_q_norm_g": G["qg"][0], "mla_kv_norm_g": G["kvg"][0], "dt_bias": G["dtb"][0, :SSD_H],
            "a_log": G["alog"][0, :SSD_H], "d_skip": G["dskip"][0, :SSD_H], "fox_f_b": G["fb"][0, SM_F:SM_F + FOX_H]}


def big_grads(G, stage):
    if stage != "mix":
        i = stage[-1]
        return {f"ffn{i}_w_{k}": G[k[0] + i].reshape(N_DEV, HS, D) for k in ("gate", "up", "down")}
    duq, dukv = mla_weight_grads(G["wq"], G["wk"], G["wv"])
    return {"w_in": G["w_in"].reshape(N_DEV, D // N_DEV, N_INP), "w_out": G["w_out"].reshape(N_DEV, D // N_DEV, D),
            "mla_w_uq": duq, "mla_w_ukv": dukv, "conv_w": _shard_cols(G["conv_w"])}


def kernel(x, meta, ffn1_w_gate, ffn1_w_up, ffn1_w_down, ln1_g, ln1_b, w_in, conv_w, conv_b, dt_bias, a_log, d_skip, ssd_norm_g, fox_f_b, mla_q_norm_g, mla_w_uq, mla_kv_norm_g, mla_w_ukv, w_out, ln2_g, ln2_b, ffn2_w_gate, ffn2_w_up, ffn2_w_down, ln3_g, ln3_b, loss_target, m_meta, m_ffn1_w_gate, m_ffn1_w_up, m_ffn1_w_down, m_ln1_g, m_ln1_b, m_w_in, m_conv_w, m_conv_b, m_dt_bias, m_a_log, m_d_skip, m_ssd_norm_g, m_fox_f_b, m_mla_q_norm_g, m_mla_w_uq, m_mla_kv_norm_g, m_mla_w_ukv, m_w_out, m_ln2_g, m_ln2_b, m_ffn2_w_gate, m_ffn2_w_up, m_ffn2_w_down, m_ln3_g, m_ln3_b, v_meta, v_ffn1_w_gate, v_ffn1_w_up, v_ffn1_w_down, v_ln1_g, v_ln1_b, v_w_in, v_conv_w, v_conv_b, v_dt_bias, v_a_log, v_d_skip, v_ssd_norm_g, v_fox_f_b, v_mla_q_norm_g, v_mla_w_uq, v_mla_kv_norm_g, v_mla_w_ukv, v_w_out, v_ln2_g, v_ln2_b, v_ffn2_w_gate, v_ffn2_w_up, v_ffn2_w_down, v_ln3_g, v_ln3_b):
    vals = (meta, ffn1_w_gate, ffn1_w_up, ffn1_w_down, ln1_g, ln1_b, w_in, conv_w, conv_b, dt_bias, a_log, d_skip, ssd_norm_g, fox_f_b, mla_q_norm_g, mla_w_uq, mla_kv_norm_g, mla_w_ukv, w_out, ln2_g, ln2_b, ffn2_w_gate, ffn2_w_up, ffn2_w_down, ln3_g, ln3_b)
    moms = (m_meta, m_ffn1_w_gate, m_ffn1_w_up, m_ffn1_w_down, m_ln1_g, m_ln1_b, m_w_in, m_conv_w, m_conv_b, m_dt_bias, m_a_log, m_d_skip, m_ssd_norm_g, m_fox_f_b, m_mla_q_norm_g, m_mla_w_uq, m_mla_kv_norm_g, m_mla_w_ukv, m_w_out, m_ln2_g, m_ln2_b, m_ffn2_w_gate, m_ffn2_w_up, m_ffn2_w_down, m_ln3_g, m_ln3_b)
    vars_ = (v_meta, v_ffn1_w_gate, v_ffn1_w_up, v_ffn1_w_down, v_ln1_g, v_ln1_b, v_w_in, v_conv_w, v_conv_b, v_dt_bias, v_a_log, v_d_skip, v_ssd_norm_g, v_fox_f_b, v_mla_q_norm_g, v_mla_w_uq, v_mla_kv_norm_g, v_mla_w_ukv, v_w_out, v_ln2_g, v_ln2_b, v_ffn2_w_gate, v_ffn2_w_up, v_ffn2_w_down, v_ln3_g, v_ln3_b)
    P = dict(zip(_NAMES, vals))
    M = dict(zip(_NAMES, moms))
    V = dict(zip(_NAMES, vars_))
    me = 4 * lax.axis_index("x") + 2 * lax.axis_index("y") + lax.axis_index("c")

    me_arr = me.astype(jnp.int32).reshape(1)
    for n in _FFN_T:
        P[n], M[n], V[n] = (jnp.swapaxes(a[n], 1, 2) for a in (P, M, V))
    src = dict(P)
    src["w_in"] = w_in_to_padded(P["w_in"])
    order = [("meta", 0)] + [(n, l) for l in range(NL) for names in _STAGES.values() for n in names]
    nfirst = 1 + len(_STAGES["ffn1"])

    def place(n, l):
        return place_own(P["meta"][None] if n == "meta" else src[n], l, F32 if n in ("meta", "conv_w") else BF16, me_arr)

    hg_first = gather_start([place(n, l) for n, l in order[:nfirst]], "gather_start_first")
    hg_rest = gather_start([place(n, l) for n, l in order[nfirst:]], "gather_start_rest")
    zone_of = {nl_: ((hg_first, i) if i < nfirst else (hg_rest, i - nfirst)) for i, nl_ in enumerate(order)}
    relays = {}

    def ahead(l, stage, after, min_layer=0):
        if not min_layer <= l < NL:
            return None
        if (l, stage) not in relays:
            zs = [zone_of[("meta", 0)]] if stage == "meta" else [zone_of[(n, l)] for n in _STAGES[stage]]
            hg, idxs = zs[0][0], [i for _, i in zs]
            relays[(l, stage)] = (hg, idxs, gather_relay(hg, idxs, f"gather_relay_{l}_{stage}", after))
        return relays[(l, stage)][2]["token"]

    def arrived(l, stage, after):
        ahead(l, stage, after)
        hg, idxs, rl = relays[(l, stage)]
        return gather_wait(hg, rl, idxs, f"gather_wait_{l}_{stage}", after)

    meta_full = _unshard_cols(arrived(0, "meta", hg_rest["token"])[0])

    def getw(l, stage, after):
        return stage_weights(l, stage, dict(zip(_STAGES[stage], arrived(l, stage, after))), P)

    sent = {}

    def emit(l, stage, G):
        bg = big_grads(G, stage)
        sent[(l, stage)] = exchange_start("scatter", [bg[n] for n in _STAGES[stage]], f"scatter_start_{l}_{stage}")
        return sent[(l, stage)]["token"]

    loss, gx, gmeta, grads = local_step(x[0], loss_target[0], meta_full, getw, emit, ahead)

    small = jnp.concatenate([pack_small({n: jnp.stack([small_grads(g)[n] for g in grads]) for n in _SMALL}), gmeta,
                             jnp.pad(loss, ((0, 7), (0, D - 1)))], axis=0)
    hs = exchange_start("gather", [place_own(small[None], 0, F32, me_arr)], "small_start")

    out = {}
    after = hs["token"]
    for stage in ("ffn2", "mix", "ffn1"):
        names = _STAGES[stage]
        whole = [l for l in range(NL - 1, -1, -1) if (l, stage) != (0, "ffn1")]
        got = {l: exchange_wait(sent[(l, stage)], list(range(len(names))), f"scatter_wait_{l}_{stage}", after) for l in whole}
        for i, n in enumerate(names):
            one = {l: (got[l][0][i], got[l][1][i]) for l in whole}
            for l in set(range(NL)) - set(whole):
                s_, r_ = exchange_wait(sent[(l, stage)], [i], f"scatter_wait_{l}_{stage}_{i}", after)
                one[l] = (s_[0], r_[0])
            own = [one[l][0] for l in range(NL)]
            recv = [one[l][1] for l in range(NL)]
            if n == "w_in":
                g = jnp.stack([w_in_from_padded(sum_slots(recv[l], own[l], me_arr)) for l in range(NL)])
                out[n] = (g,) + adamw(P[n], M[n], V[n], g=g)
            else:
                out[n] = adamw(P[n], M[n], V[n], recv=recv, own=own, me_arr=me_arr)
                if n in _FFN_T:
                    out[n] = tuple(jnp.swapaxes(a, 1, 2) for a in out[n])
            after = out[n][1]
    gsmall = sum_slots(exchange_wait(hs, [0], "small_wait", after)[1][0])
    gm = lax.dynamic_slice(gsmall[NL * _SMALL_ROWS:], (0, me * (D // N_DEV)), (N_META, D // N_DEV))
    out["meta"] = (gm,) + adamw(P["meta"], M["meta"], V["meta"], g=gm)
    gs = gsmall[:NL * _SMALL_ROWS]
    sd, sm_, sv_ = adamw(pack_small(P), pack_small(M), pack_small(V), g=gs)
    ups = [unpack_small(a, P) for a in (gs, sd, sm_, sv_)]
    for n in _SMALL:
        out[n] = tuple(u[n] for u in ups)

    loss_all = gsmall[NL * _SMALL_ROWS + N_META, 0]
    flat = [loss_all, gx[None]]
    for k in range(4):
        flat += [out[n][k] for n in _NAMES]
    return tuple(flat)
```

```python
import functools

import jax
import jax.numpy as jnp
from jax import lax
from jax.experimental import pallas as pl
from jax.experimental.pallas import tpu as pltpu

F32, BF16 = jnp.float32, jnp.bfloat16
HI = lax.Precision.HIGHEST

N_DEV = 8
D = 1024
NL = 2
N_META = 16
BLK = 128
PAD = BLK - N_META
D_FF = 2816
HS = D_FF // N_DEV
SSD_H, SSD_P, SSD_N, SSD_G = 8, 64, 64, 2
SSD_D = SSD_H * SSD_P
CONV_K = 4
CONV_D = SSD_D + 2 * SSD_G * SSD_N
FOX_H, FOX_DH = 4, 64
MLA_H, MLA_QL, MLA_KVL, MLA_NOPE, MLA_ROPE, MLA_V = 4, 256, 128, 64, 32, 64
N_IN = 2476
C_Z, C_XBC, C_FQ, C_FK, C_FV, C_CQ, C_CKV, C_SM, N_INP = 0, 512, 1280, 1536, 1792, 2048, 2304, 2432, 2560
SM_DT, SM_F, SM_KR = 0, 8, 64
ALPHA = (2 * NL) ** 0.25
EPS = 1e-5
NEG = -1e30
LR, B1, B2, AEPS, WD, STEP = 0.001, 0.9, 0.999, 1e-08, 0.01, 10
VMEM_MB = 56


def _cp(*sem):
    return pltpu.CompilerParams(dimension_semantics=sem, vmem_limit_bytes=VMEM_MB << 20)


def _nn(a, b):
    return lax.dot_general(a, b, (((1,), (0,)), ((), ())), preferred_element_type=F32)


def _nt(a, b):
    return lax.dot_general(a, b, (((1,), (1,)), ((), ())), preferred_element_type=F32)


def _tn(a, b):
    return lax.dot_general(a, b, (((0,), (0,)), ((), ())), preferred_element_type=F32)


def _nn_hi(a, b):
    return lax.dot_general(a, b, (((1,), (0,)), ((), ())), precision=HI, preferred_element_type=F32)


def _row_tile(t):
    for d in range(640, 15, -16):
        if t % d == 0:
            return d
    raise ValueError(t)


def _sig(x):
    return 1.0 / (1.0 + jnp.exp(-x))


def _tri(lower=True):
    r = lax.broadcasted_iota(jnp.int32, (BLK, BLK), 0)
    c = lax.broadcasted_iota(jnp.int32, (BLK, BLK), 1)
    return (r >= c) if lower else (r <= c)


def build_h0(meta_full, x):
    s = x.shape[0]
    nb = s // BLK + 1

    def body(m_ref, x_ref, h_ref, hb_ref):
        i = pl.program_id(0)

        @pl.when(i == 0)
        def _():
            h = jnp.concatenate([jnp.zeros((PAD, D), F32), m_ref[...]], axis=0)
            h_ref[...] = h
            hb_ref[...] = h.astype(BF16)

        @pl.when(i > 0)
        def _():
            h_ref[...] = x_ref[...]
            hb_ref[...] = x_ref[...].astype(BF16)

    return pl.pallas_call(
        body, name="build_h0", grid=(nb,),
        in_specs=[pl.BlockSpec((N_META, D), lambda i: (0, 0)),
                  pl.BlockSpec((BLK, D), lambda i: (jnp.maximum(i - 1, 0), 0))],
        out_specs=[pl.BlockSpec((BLK, D), lambda i: (i, 0))] * 2,
        out_shape=[jax.ShapeDtypeStruct((nb * BLK, D), F32), jax.ShapeDtypeStruct((nb * BLK, D), BF16)],
        compiler_params=_cp("arbitrary"),
    )(meta_full, x)


FT = 256


def _layer_norm(r, gamma, beta):
    mu = jnp.mean(r, axis=1, keepdims=True)
    xc = r - mu
    var = jnp.mean(xc * xc, axis=1, keepdims=True)
    return xc * lax.rsqrt(var + EPS) * gamma + beta


def ffn_fwd_seq(x, ln_in, wg, wu, wd, ln_out):
    t = x.shape[0]
    f = wg.shape[0]
    nj, nr = f // FT, t // _row_tile(t)
    rc = t // nr
    plain = ln_in is None
    gi, bi = ln_out if plain else ln_in

    def body(x_hbm, gi_ref, bi_ref, go_ref, bo_ref, wg_ref, wu_ref, wd_ref, u_ref, v_ref, r_hbm, yb_hbm,
             acc, hbs, xbuf, sem_in, sem_out):
        j = pl.program_id(0)

        @pl.when(j == 0)
        def _():
            def fetch(k):
                return pltpu.make_async_copy(x_hbm.at[pl.ds(k * rc, rc)], xbuf.at[k % 2], sem_in.at[k % 2])

            fetch(0).start()
            for k in range(nr):
                if k + 1 < nr:
                    fetch(k + 1).start()
                fetch(k).wait()
                h = xbuf[k % 2]
                if not plain:
                    h = _layer_norm(h, gi_ref[...], bi_ref[...])
                acc[k * rc:(k + 1) * rc, :] = ALPHA * h
                hbs[k * rc:(k + 1) * rc, :] = h.astype(BF16)

        def chunk(k, last):
            sl = slice(k * rc, (k + 1) * rc)
            h = hbs[sl, :]
            u = _nt(h, wg_ref[...])
            v = _nt(h, wu_ref[...])
            u_ref[sl, :] = u.astype(BF16)
            v_ref[sl, :] = v.astype(BF16)
            acc[sl, :] += _nn((0.5 * u * _sig(u) * v).astype(BF16), wd_ref[...])
            if not last:
                return []
            rows = pl.ds(k * rc, rc)
            cps = [pltpu.make_async_copy(acc.at[rows], r_hbm.at[rows], sem_out.at[2 * k])]
            cps[0].start()
            hbs[sl, :] = _layer_norm(acc[sl, :], go_ref[...], bo_ref[...]).astype(BF16)
            cps.append(pltpu.make_async_copy(hbs.at[rows], yb_hbm.at[rows], sem_out.at[2 * k + 1]))
            cps[1].start()
            return cps

        @pl.when(j < nj - 1)
        def _():
            for k in range(nr):
                chunk(k, False)

        @pl.when(j == nj - 1)
        def _():
            cps = []
            for k in range(nr):
                cps += chunk(k, True)
            for cp in cps:
                cp.wait()

    vec = pl.BlockSpec((1, D), lambda j: (0, 0))
    wsp = pl.BlockSpec((FT, D), lambda j: (j, 0))
    act = pl.BlockSpec((None, t, FT), lambda j: (j, 0, 0))
    return pl.pallas_call(
        body, name="ffn_fwd_seq", grid=(nj,),
        in_specs=[_ANY, vec, vec, vec, vec, wsp, wsp, wsp],
        out_specs=[act, act, _ANY, _ANY],
        out_shape=[jax.ShapeDtypeStruct((nj, t, FT), BF16), jax.ShapeDtypeStruct((nj, t, FT), BF16),
                   jax.ShapeDtypeStruct((t, D), F32), jax.ShapeDtypeStruct((t, D), BF16)],
        scratch_shapes=[pltpu.VMEM((t, D), F32), pltpu.VMEM((t, D), BF16), pltpu.VMEM((2, rc, D), F32),
                        pltpu.SemaphoreType.DMA((2,)), pltpu.SemaphoreType.DMA((2 * nr,))],
        compiler_params=_cp("arbitrary"),
    )(x, gi, bi, ln_out[0], ln_out[1], wg, wu, wd)


def ffn_bwd_seq(parts, r, gamma, hb, u, v, wg, wu, wd, after=None):
    nj, t, _ = u.shape
    f = nj * FT
    nr = 2 * (t // _row_tile(t))
    rc = t // nr
    nc = t // BLK
    scales = [s for _, s in parts]
    npart = len(parts)
    extra = [] if after is None else [after]

    def body(*refs):
        refs = refs[len(extra):]
        p_hbm, refs = refs[:npart], refs[npart:]
        (r_hbm, g_ref, hb_hbm, u_ref, v_ref, wg_ref, wu_ref, wd_ref, dh_hbm, dwg_ref, dwu_ref, dwd_ref, dg_ref, db_ref,
         dfs, hbt, dft, dhacc, dus, dvs, acs, pbuf, rbuf, hbuf, sems, sem_out) = refs
        j = pl.program_id(0)

        @pl.when(j == 0)
        def _():
            def fetch(c):
                rows = pl.ds(c * BLK, BLK)
                cps = [pltpu.make_async_copy(p_hbm[p].at[rows], pbuf.at[c % 2, p], sems.at[c % 2, p]) for p in range(npart)]
                cps.append(pltpu.make_async_copy(r_hbm.at[rows], rbuf.at[c % 2], sems.at[c % 2, npart]))
                cps.append(pltpu.make_async_copy(hb_hbm.at[rows], hbuf.at[c % 2], sems.at[c % 2, npart + 1]))
                return cps

            for cp in fetch(0):
                cp.start()
            dg = jnp.zeros((1, D), F32)
            db = jnp.zeros((1, D), F32)
            for c in range(nc):
                if c + 1 < nc:
                    for cp in fetch(c + 1):
                        cp.start()
                for cp in fetch(c):
                    cp.wait()
                sl = slice(c * BLK, (c + 1) * BLK)
                dy = scales[0] * pbuf[c % 2, 0]
                for p in range(1, npart):
                    dy += scales[p] * pbuf[c % 2, p]
                rr = rbuf[c % 2]
                xc = rr - jnp.mean(rr, axis=1, keepdims=True)
                rstd = lax.rsqrt(jnp.mean(xc * xc, axis=1, keepdims=True) + EPS)
                xh = xc * rstd
                dxh = dy * g_ref[...]
                dr = rstd * (dxh - jnp.mean(dxh, axis=1, keepdims=True) - xh * jnp.mean(dxh * xh, axis=1, keepdims=True))
                dg += jnp.sum(dy * xh, axis=0, keepdims=True)
                db += jnp.sum(dy, axis=0, keepdims=True)
                dhacc[sl, :] = ALPHA * dr
                dfc = (0.5 * dr).astype(BF16)
                dfs[sl, :] = dfc
                dft[:, sl] = dfc.T
                hbt[:, sl] = hbuf[c % 2].T
            dg_ref[...] = dg
            db_ref[...] = db

        for k in range(nr):
            sl = slice(k * rc, (k + 1) * rc)
            da = _nt(dfs[sl, :], wd_ref[...])
            uu = u_ref[sl, :].astype(F32)
            vv = v_ref[sl, :].astype(F32)
            sg = _sig(uu)
            du = (da * vv * (sg * (1.0 + uu * (1.0 - sg)))).astype(BF16)
            dv = (da * uu * sg).astype(BF16)
            dus[sl, :] = du
            dvs[sl, :] = dv
            acs[sl, :] = (uu * sg * vv).astype(BF16)
            dhacc[sl, :] += _nn(du, wg_ref[...]) + _nn(dv, wu_ref[...])
        @pl.when(j == nj - 1)
        def _():
            pltpu.make_async_copy(dhacc, dh_hbm, sem_out.at[0]).start()

        dwg_ref[...] = _nn(hbt[...], dus[...]).astype(BF16).T
        dwu_ref[...] = _nn(hbt[...], dvs[...]).astype(BF16).T
        dwd_ref[...] = _nn(dft[...], acs[...]).astype(BF16).T

        @pl.when(j == nj - 1)
        def _():
            pltpu.make_async_copy(dhacc, dh_hbm, sem_out.at[0]).wait()

    vec = pl.BlockSpec((1, D), lambda j: (0, 0))
    wsp = pl.BlockSpec((FT, D), lambda j: (j, 0))
    act = pl.BlockSpec((None, t, FT), lambda j: (j, 0, 0))
    return pl.pallas_call(
        body, name="ffn_bwd_seq", grid=(nj,),
        in_specs=[_ANY] * (len(extra) + npart + 1) + [vec, _ANY, act, act, wsp, wsp, wsp],
        out_specs=[_ANY, wsp, wsp, wsp, vec, vec],
        out_shape=[jax.ShapeDtypeStruct((t, D), F32)] + [jax.ShapeDtypeStruct((f, D), BF16)] * 3
        + [jax.ShapeDtypeStruct((1, D), F32)] * 2,
        scratch_shapes=[pltpu.VMEM((t, D), BF16), pltpu.VMEM((D, t), BF16), pltpu.VMEM((D, t), BF16),
                        pltpu.VMEM((t, D), F32), pltpu.VMEM((t, FT), BF16), pltpu.VMEM((t, FT), BF16),
                        pltpu.VMEM((t, FT), BF16), pltpu.VMEM((2, npart, BLK, D), F32), pltpu.VMEM((2, BLK, D), F32),
                        pltpu.VMEM((2, BLK, D), BF16), pltpu.SemaphoreType.DMA((2, npart + 2)),
                        pltpu.SemaphoreType.DMA((1,))],
        compiler_params=_cp("arbitrary"),
    )(*extra, *[p for p, _ in parts], r, gamma, hb, u, v, wg, wu, wd)


def mm_res_ln(a, b, x, ln_in, ln_out):
    t, k = a.shape
    tm = _row_tile(t)

    def body(a_ref, b_ref, x_ref, gi_ref, bi_ref, go_ref, bo_ref, r_ref, yb_ref):
        r = ALPHA * _layer_norm(x_ref[...], gi_ref[...], bi_ref[...]) + _nn(a_ref[...], b_ref[...])
        r_ref[...] = r
        yb_ref[...] = _layer_norm(r, go_ref[...], bo_ref[...]).astype(BF16)

    row = pl.BlockSpec((tm, D), lambda i: (i, 0))
    vec = pl.BlockSpec((1, D), lambda i: (0, 0))
    return pl.pallas_call(
        body, name="mm_res_ln", grid=(t // tm,),
        in_specs=[pl.BlockSpec((tm, k), lambda i: (i, 0)), pl.BlockSpec((k, D), lambda i: (0, 0)), row, vec, vec, vec, vec],
        out_specs=[row, row],
        out_shape=[jax.ShapeDtypeStruct((t, D), F32), jax.ShapeDtypeStruct((t, D), BF16)],
        compiler_params=_cp("arbitrary"),
    )(a, b, x, ln_in[0], ln_in[1], ln_out[0], ln_out[1])


def mm_nn(a, b):
    t, k = a.shape
    n = tn = b.shape[1]
    tm = _row_tile(t)

    def body(a_ref, b_ref, o_ref):
        o_ref[...] = _nn(a_ref[...], b_ref[...])

    return pl.pallas_call(
        body, name="mm_nn", grid=(t // tm, n // tn),
        in_specs=[pl.BlockSpec((tm, k), lambda i, j: (i, 0)), pl.BlockSpec((k, tn), lambda i, j: (0, j))],
        out_specs=pl.BlockSpec((tm, tn), lambda i, j: (i, j)),
        out_shape=jax.ShapeDtypeStruct((t, n), F32),
        compiler_params=_cp("arbitrary", "arbitrary"),
    )(a, b)


def oproj_bwd(dy, r, gamma, mixcat, w_out, after=None):
    t = r.shape[0]
    tm = _row_tile(t)
    nt = t // tm
    extra = [] if after is None else [after]

    def body(*refs):
        dy_ref, r_ref, g_ref, m_ref, w_ref, dr_ref, dm_ref, dw_ref, dg_ref, db_ref, acc = refs[len(extra):]
        i = pl.program_id(0)
        dy = dy_ref[...]
        rr = r_ref[...]
        xc = rr - jnp.mean(rr, axis=1, keepdims=True)
        rstd = lax.rsqrt(jnp.mean(xc * xc, axis=1, keepdims=True) + EPS)
        xh = xc * rstd
        dxh = dy * g_ref[...]
        dr = rstd * (dxh - jnp.mean(dxh, axis=1, keepdims=True) - xh * jnp.mean(dxh * xh, axis=1, keepdims=True))
        dr_ref[...] = dr
        drb = dr.astype(BF16)
        dm_ref[...] = _nt(drb, w_ref[...])
        dw = _tn(m_ref[...], drb)
        dg = jnp.sum(dy * xh, axis=0, keepdims=True)
        db = jnp.sum(dy, axis=0, keepdims=True)

        @pl.when(i == 0)
        def _():
            acc[...] = dw
            dg_ref[...] = dg
            db_ref[...] = db

        @pl.when(i > 0)
        def _():
            acc[...] += dw
            dg_ref[...] += dg
            db_ref[...] += db

        @pl.when(i == nt - 1)
        def _():
            dw_ref[...] = acc[...].astype(BF16)

    row = pl.BlockSpec((tm, D), lambda i: (i, 0))
    vec = pl.BlockSpec((1, D), lambda i: (0, 0))
    mat = pl.BlockSpec((D, D), lambda i: (0, 0))
    return pl.pallas_call(
        body, name="oproj_bwd", grid=(nt,),
        in_specs=[_ANY] * len(extra) + [row, row, vec, row, mat],
        out_specs=[row, row, mat, vec, vec],
        out_shape=[jax.ShapeDtypeStruct((t, D), F32), jax.ShapeDtypeStruct((t, D), F32), jax.ShapeDtypeStruct((D, D), BF16),
                   jax.ShapeDtypeStruct((1, D), F32), jax.ShapeDtypeStruct((1, D), F32)],
        scratch_shapes=[pltpu.VMEM((D, D), F32)],
        compiler_params=_cp("arbitrary"),
    )(*extra, dy, r, gamma, mixcat, w_out)


def proj_bwd(pieces, hb, w_in):
    t = hb.shape[0]
    n = w_in.shape[1]
    tm = _row_tile(t)
    nt = t // tm
    widths = [p.shape[1] for p in pieces]
    starts = [sum(widths[:k]) for k in range(len(pieces))]
    assert sum(widths) == n

    def body(*refs):
        p_refs = refs[:len(pieces)]
        h_ref, w_ref, dh_ref, dw_ref, acc = refs[len(pieces):]
        i = pl.program_id(0)
        h = h_ref[...]
        dh = jnp.zeros((tm, D), F32)
        dws = []
        for p_ref, c0, w in zip(p_refs, starts, widths):
            pb = p_ref[...].astype(BF16)
            dh += _nt(pb, w_ref[:, c0:c0 + w])
            dws.append(_tn(h, pb))
        dh_ref[...] = dh

        @pl.when(i == 0)
        def _():
            for dw, c0, w in zip(dws, starts, widths):
                acc[:, c0:c0 + w] = dw

        @pl.when(i > 0)
        def _():
            for dw, c0, w in zip(dws, starts, widths):
                acc[:, c0:c0 + w] += dw

        @pl.when(i == nt - 1)
        def _():
            dw_ref[...] = acc[...].astype(BF16)

    mat = pl.BlockSpec((D, n), lambda i: (0, 0))
    return pl.pallas_call(
        body, name="proj_bwd", grid=(nt,),
        in_specs=[pl.BlockSpec((tm, w), lambda i: (i, 0)) for w in widths] + [pl.BlockSpec((tm, D), lambda i: (i, 0)), mat],
        out_specs=[pl.BlockSpec((tm, D), lambda i: (i, 0)), mat],
        out_shape=[jax.ShapeDtypeStruct((t, D), F32), jax.ShapeDtypeStruct((D, n), BF16)],
        scratch_shapes=[pltpu.VMEM((D, n), F32)],
        compiler_params=_cp("arbitrary"),
    )(*pieces, hb, w_in)


def loss_head(r, ln, target):
    t = r.shape[0]
    nb = t // BLK

    def body(r_ref, g_ref, b_ref, t_ref, dy_ref, l_ref):
        i = pl.program_id(0)

        @pl.when(i == 0)
        def _():
            dy_ref[...] = jnp.zeros_like(dy_ref)
            l_ref[...] = jnp.zeros_like(l_ref)

        @pl.when(i > 0)
        def _():
            err = _layer_norm(r_ref[...], g_ref[...], b_ref[...]) - t_ref[...]
            dy_ref[...] = err * (1.0 / D)
            l_ref[...] += (0.5 / D) * jnp.sum(err * err, keepdims=True)

    vec = pl.BlockSpec((1, D), lambda i: (0, 0))
    return pl.pallas_call(
        body, name="loss_head", grid=(nb,),
        in_specs=[pl.BlockSpec((BLK, D), lambda i: (i, 0)), vec, vec,
                  pl.BlockSpec((BLK, D), lambda i: (jnp.maximum(i - 1, 0), 0))],
        out_specs=[pl.BlockSpec((BLK, D), lambda i: (i, 0)), pl.BlockSpec((1, 1), lambda i: (0, 0))],
        out_shape=[jax.ShapeDtypeStruct((t, D), F32), jax.ShapeDtypeStruct((1, 1), F32)],
        compiler_params=_cp("arbitrary"),
    )(r, ln[0], ln[1], target)


def split_dh0(dh0, after=None):
    t = dh0.shape[0]
    nb = t // BLK
    extra = [] if after is None else [after]

    def body(*refs):
        a_ref, gx_ref, gm_ref = refs[len(extra):]
        i = pl.program_id(0)
        tot = a_ref[...]

        @pl.when(i == 0)
        def _():
            gm_ref[...] = tot[PAD:, :]

        @pl.when(i > 0)
        def _():
            gx_ref[...] = tot

    blk = pl.BlockSpec((BLK, D), lambda i: (i, 0))
    return pl.pallas_call(
        body, name="split_dh0", grid=(nb,),
        in_specs=[_ANY] * len(extra) + [blk],
        out_specs=[pl.BlockSpec((BLK, D), lambda i: (jnp.maximum(i - 1, 0), 0)),
                   pl.BlockSpec((N_META, D), lambda i: (0, 0))],
        out_shape=[jax.ShapeDtypeStruct((t - BLK, D), F32), jax.ShapeDtypeStruct((N_META, D), F32)],
        compiler_params=_cp("arbitrary"),
    )(*extra, dh0)


def _valid_rows(nrows, first_row):
    return (first_row + lax.broadcasted_iota(jnp.int32, (nrows, 1), 0)) >= PAD


def conv_fwd(proj, conv_w, conv_b):
    t = proj.shape[0]
    c0 = C_XBC // BLK

    def body(x_ref, w_ref, b_ref, o_ref):
        ok = _valid_rows(t, 0)
        x = jnp.where(ok, x_ref[...], 0.0)
        w = w_ref[...]
        acc = b_ref[...] + w[CONV_K - 1:CONV_K, :] * x
        for s in range(1, CONV_K):
            acc += w[CONV_K - 1 - s:CONV_K - s, :] * pltpu.roll(x, s, 0)
        o_ref[...] = jnp.where(ok, acc * _sig(acc), 0.0)

    return pl.pallas_call(
        body, name="conv_fwd", grid=(CONV_D // BLK,),
        in_specs=[pl.BlockSpec((t, BLK), lambda j: (0, c0 + j)),
                  pl.BlockSpec((CONV_K, BLK), lambda j: (0, j)), pl.BlockSpec((1, BLK), lambda j: (0, j))],
        out_specs=pl.BlockSpec((t, BLK), lambda j: (0, j)),
        out_shape=jax.ShapeDtypeStruct((t, CONV_D), F32),
        compiler_params=_cp("arbitrary"),
    )(proj, conv_w, conv_b)


def conv_bwd(dxa, proj, conv_w, conv_b):
    t = proj.shape[0]
    c0 = C_XBC // BLK

    def body(d_ref, x_ref, w_ref, b_ref, dx_ref, dw_ref, db_ref):
        ok = _valid_rows(t, 0)
        x = jnp.where(ok, x_ref[...], 0.0)
        w = w_ref[...]
        xs = [x] + [pltpu.roll(x, s, 0) for s in range(1, CONV_K)]
        acc = b_ref[...] + w[CONV_K - 1:CONV_K, :] * x
        for s in range(1, CONV_K):
            acc += w[CONV_K - 1 - s:CONV_K - s, :] * xs[s]
        sg = _sig(acc)
        dxc = jnp.where(ok, d_ref[...] * (sg * (1.0 + acc * (1.0 - sg))), 0.0)
        db_ref[...] = jnp.sum(dxc, axis=0, keepdims=True)
        dw_ref[...] = jnp.concatenate(
            [jnp.sum(dxc * xs[CONV_K - 1 - k], axis=0, keepdims=True) for k in range(CONV_K)], axis=0)
        dx = w[CONV_K - 1:CONV_K, :] * dxc
        for s in range(1, CONV_K):
            dx += w[CONV_K - 1 - s:CONV_K - s, :] * pltpu.roll(dxc, t - s, 0)
        dx_ref[...] = jnp.where(ok, dx, 0.0)

    col = pl.BlockSpec((t, BLK), lambda j: (0, j))
    return pl.pallas_call(
        body, name="conv_bwd", grid=(CONV_D // BLK,),
        in_specs=[col, pl.BlockSpec((t, BLK), lambda j: (0, c0 + j)),
                  pl.BlockSpec((CONV_K, BLK), lambda j: (0, j)), pl.BlockSpec((1, BLK), lambda j: (0, j))],
        out_specs=[col, pl.BlockSpec((CONV_K, BLK), lambda j: (0, j)), pl.BlockSpec((1, BLK), lambda j: (0, j))],
        out_shape=[jax.ShapeDtypeStruct((t, CONV_D), F32), jax.ShapeDtypeStruct((CONV_K, CONV_D), F32),
                   jax.ShapeDtypeStruct((1, CONV_D), F32)],
        compiler_params=_cp("arbitrary"),
    )(dxa, proj, conv_w, conv_b)


def _softplus(x):
    return jnp.maximum(x, 0.0) + jnp.log(1.0 + jnp.exp(-jnp.abs(x)))


GW = SSD_D // SSD_G
HPG = SSD_H // SSD_G


def _head_expand():
    r = lax.broadcasted_iota(jnp.int32, (BLK, SSD_D), 0)
    c = lax.broadcasted_iota(jnp.int32, (BLK, SSD_D), 1)
    rt = lax.broadcasted_iota(jnp.int32, (SSD_D, BLK), 0)
    ct = lax.broadcasted_iota(jnp.int32, (SSD_D, BLK), 1)
    return (c // SSD_P == r).astype(F32), (rt // SSD_P == ct).astype(F32)


def _ssd_chunk(xa, sm, dtb, alog, dskip, ok, sp):
    e, et = _head_expand()
    dt = jnp.where(ok, _softplus(sm + dtb), 0.0)
    amat = -jnp.exp(alog)
    tri = _tri()
    ac = _nn_hi(tri.astype(F32), dt * amat)
    act = ac.T
    ace, dte, dse = _nn_hi(ac, e), _nn_hi(dt, e), _nn_hi(dskip, e)
    laste = ace[BLK - 1:BLK, :]
    ee, dece, gle = jnp.exp(ace), jnp.exp(laste - ace), jnp.exp(laste)
    xs = xa[:, :SSD_D]
    xdt = xs * dte
    decx = dece * xdt
    xdtb = xdt.astype(BF16)
    d = dict(e=e, et=et, dt=dt, amat=amat, tri=tri, ac=ac, act=act, dte=dte, dse=dse, ee=ee, dece=dece, gle=gle, xs=xs,
             xdt=xdt, xdtb=xdtb, decx=decx, bg=[], cg=[], cb=[], yo=[], seg=[], m=[], new_s=[])
    ys = []
    for g in range(SSD_G):
        cols = slice(GW * g, GW * (g + 1))
        bg = xa[:, SSD_D + SSD_N * g:SSD_D + SSD_N * (g + 1)].astype(BF16)
        cg = xa[:, SSD_D + SSD_G * SSD_N + SSD_N * g:SSD_D + SSD_G * SSD_N + SSD_N * (g + 1)].astype(BF16)
        spg = sp[:, cols]
        sloc = _tn(bg, decx[:, cols].astype(BF16))
        yo = _nn(cg, spg.astype(BF16)) * ee[:, cols]
        cb = _nt(cg, bg)
        d["new_s"].append(gle[:, cols] * spg + sloc)
        yds = []
        for h in range(HPG * g, HPG * (g + 1)):
            seg = jnp.where(tri, jnp.exp(jnp.minimum(ac[:, h:h + 1] - act[h:h + 1, :], 0.0)), 0.0)
            m = cb * seg
            yds.append(_nn(m.astype(BF16), xdtb[:, SSD_P * h:SSD_P * (h + 1)]))
            d["seg"].append(seg)
            d["m"].append(m)
        ys.append(jnp.concatenate(yds, axis=1) + yo)
        for k, val in (("bg", bg), ("cg", cg), ("cb", cb), ("yo", yo)):
            d[k].append(val)
    d["y"] = jnp.concatenate(ys, axis=1) + dse * xs
    return d


def ssd_fwd(xa, proj, dtb, alog, dskip, normg):
    t = xa.shape[0]
    nb = t // BLK
    gw = SSD_D // SSD_G

    def body(xa_ref, z_ref, sm_ref, dtb_ref, al_ref, ds_ref, ng_ref, y_ref, sp_ref, st):
        c = pl.program_id(0)

        @pl.when(c == 0)
        def _():
            st[...] = jnp.zeros_like(st)

        ok = _valid_rows(BLK, c * BLK)
        sp = st[...]
        sp_ref[...] = sp
        d = _ssd_chunk(xa_ref[...], sm_ref[...], dtb_ref[...], al_ref[...], ds_ref[...], ok, sp)
        st[...] = jnp.concatenate(d["new_s"], axis=1)
        y = d["y"]
        z = z_ref[...]
        yg = y * (z * _sig(z))
        outs = []
        for g in range(SSD_G):
            v = yg[:, gw * g:gw * (g + 1)]
            outs.append(v * lax.rsqrt(jnp.mean(v * v, axis=1, keepdims=True) + EPS))
        y_ref[...] = (jnp.concatenate(outs, axis=1) * ng_ref[...]).astype(BF16)

    vec = pl.BlockSpec((1, BLK), lambda c: (0, 0))
    return pl.pallas_call(
        body, name="ssd_fwd", grid=(nb,),
        in_specs=[pl.BlockSpec((BLK, CONV_D), lambda c: (c, 0)),
                  pl.BlockSpec((BLK, SSD_D), lambda c: (c, C_Z // SSD_D)),
                  pl.BlockSpec((BLK, BLK), lambda c: (c, C_SM // BLK)),
                  vec, vec, vec, pl.BlockSpec((1, SSD_D), lambda c: (0, 0))],
        out_specs=[pl.BlockSpec((BLK, SSD_D), lambda c: (c, 0)),
                   pl.BlockSpec((None, SSD_N, SSD_D), lambda c: (c, 0, 0))],
        out_shape=[jax.ShapeDtypeStruct((t, SSD_D), BF16), jax.ShapeDtypeStruct((nb, SSD_N, SSD_D), F32)],
        scratch_shapes=[pltpu.VMEM((SSD_N, SSD_D), F32)],
        compiler_params=_cp("arbitrary"),
    )(xa, proj, proj, dtb, alog, dskip, normg)


def _lane_put(col, lane):
    li = lax.broadcasted_iota(jnp.int32, (col.shape[0], BLK), 1)
    return jnp.where(li == lane, col, 0.0)


def ssd_bwd(dmix, xa, proj, sprev, dtb, alog, dskip, normg):
    t = xa.shape[0]
    nb = t // BLK
    gw = SSD_D // SSD_G
    rev = lambda c: nb - 1 - c

    def body(dy_ref, xa_ref, z_ref, sm_ref, sp_ref, dtb_ref, al_ref, ds_ref, ng_ref,
             dxa_ref, dz_ref, dsm_ref, dng_ref, dds_ref, dal_ref, ddtb_ref, dst):
        c = pl.program_id(0)

        @pl.when(c == 0)
        def _():
            dst[...] = jnp.zeros_like(dst)
            dng_ref[...] = jnp.zeros_like(dng_ref)
            dds_ref[...] = jnp.zeros_like(dds_ref)
            dal_ref[...] = jnp.zeros_like(dal_ref)
            ddtb_ref[...] = jnp.zeros_like(ddtb_ref)

        ok = _valid_rows(BLK, rev(c) * BLK)
        sm = sm_ref[...]
        sp = sp_ref[...]
        d = _ssd_chunk(xa_ref[...], sm, dtb_ref[...], al_ref[...], ds_ref[...], ok, sp)
        dt, amat, ac, act, tri, et, xs, xdt = (d[k] for k in ("dt", "amat", "ac", "act", "tri", "et", "xs", "xdt"))
        rowi = lax.broadcasted_iota(jnp.int32, (BLK, 1), 0)
        y = d["y"]
        z = z_ref[...]
        sgz = _sig(z)
        siluz = z * sgz
        yg = y * siluz
        dout = dy_ref[...]
        ng = ng_ref[...]
        dygs, xhs = [], []
        for g in range(SSD_G):
            v = yg[:, gw * g:gw * (g + 1)]
            rr = lax.rsqrt(jnp.mean(v * v, axis=1, keepdims=True) + EPS)
            xh = v * rr
            dxh = dout[:, gw * g:gw * (g + 1)] * ng[:, gw * g:gw * (g + 1)]
            dygs.append(rr * (dxh - xh * jnp.mean(dxh * xh, axis=1, keepdims=True)))
            xhs.append(xh)
        dyg = jnp.concatenate(dygs, axis=1)
        dng_ref[...] += jnp.sum(dout * jnp.concatenate(xhs, axis=1), axis=0, keepdims=True)
        dy = dyg * siluz
        dz_ref[...] = dyg * y * (sgz * (1.0 + z * (1.0 - sgz)))

        triu = _tri(lower=False)
        dyb = dy.astype(BF16)
        dsn = dst[...]
        dds_ref[...] += _nn_hi(jnp.sum(dy * xs, axis=0, keepdims=True), et)
        dac_all = _nn_hi(dy * jnp.concatenate(d["yo"], axis=1), et)
        dyo = (dy * d["ee"]).astype(BF16)
        gl = jnp.exp(ac[BLK - 1:BLK, :])
        dlast = _nn_hi(jnp.sum(dsn * sp, axis=0, keepdims=True), et) * gl
        bds, db_g, dc_g, dxdt_i, new_dst = [], [], [], [], []
        for g in range(SSD_G):
            cols = slice(GW * g, GW * (g + 1))
            bg, cg = d["bg"][g], d["cg"][g]
            dsng = dsn[:, cols].astype(BF16)
            dc = _nt(dyo[:, cols], sp[:, cols].astype(BF16))
            new_dst.append(_tn(cg, dyo[:, cols]) + d["gle"][:, cols] * dsn[:, cols])
            bds.append(_nn(bg, dsng))
            db = _nt(d["decx"][:, cols].astype(BF16), dsng)
            cbt = _nt(bg, cg)
            dcb = jnp.zeros((BLK, BLK), F32)
            for h in range(HPG * g, HPG * (g + 1)):
                hc = slice(SSD_P * h, SSD_P * (h + 1))
                dm = _nt(dyb[:, hc], d["xdtb"][:, hc])
                dcb += dm * d["seg"][h]
                w = dm * d["m"][h]
                dac_all += _lane_put(jnp.sum(w, axis=1, keepdims=True) - jnp.sum(w.T, axis=1, keepdims=True), h)
                segt = jnp.where(triu, jnp.exp(jnp.minimum(act[h:h + 1, :] - ac[:, h:h + 1], 0.0)), 0.0)
                dxdt_i.append(_nn((cbt * segt).astype(BF16), dyb[:, hc]))
            dcbb = dcb.astype(BF16)
            dc_g.append(dc + _nn(dcbb, bg))
            db_g.append(db + _tn(dcbb, cg))
        dst[...] = jnp.concatenate(new_dst, axis=1)
        bds = jnp.concatenate(bds, axis=1)
        tdec = jnp.exp(ac[BLK - 1:BLK, :] - ac) * _nn_hi(xdt * bds, et)
        dlast += jnp.sum(tdec, axis=0, keepdims=True)
        dac_all += jnp.where(rowi == BLK - 1, dlast, 0.0) - tdec
        dxdt = d["dece"] * bds + jnp.concatenate(dxdt_i, axis=1)
        da = _nn_hi(triu.astype(F32), dac_all)
        ddt = _nn_hi(dxdt * xs, et) + da * amat
        dal_ref[...] += jnp.sum(da * dt, axis=0, keepdims=True) * amat
        ddtr = jnp.where(ok, ddt * _sig(sm + dtb_ref[...]), 0.0)
        ddtb_ref[...] += jnp.sum(ddtr, axis=0, keepdims=True)
        dsm_ref[...] = ddtr
        dxs = d["dse"] * dy + dxdt * d["dte"]
        dxa_ref[...] = jnp.where(ok, jnp.concatenate([dxs] + db_g + dc_g, axis=1), 0.0)

    vec = pl.BlockSpec((1, BLK), lambda c: (0, 0))
    nvec = pl.BlockSpec((1, SSD_D), lambda c: (0, 0))
    return pl.pallas_call(
        body, name="ssd_bwd", grid=(nb,),
        in_specs=[pl.BlockSpec((BLK, SSD_D), lambda c: (rev(c), 0)),
                  pl.BlockSpec((BLK, CONV_D), lambda c: (rev(c), 0)),
                  pl.BlockSpec((BLK, SSD_D), lambda c: (rev(c), C_Z // SSD_D)),
                  pl.BlockSpec((BLK, BLK), lambda c: (rev(c), C_SM // BLK)),
                  pl.BlockSpec((None, SSD_N, SSD_D), lambda c: (rev(c), 0, 0)),
                  vec, vec, vec, nvec],
        out_specs=[pl.BlockSpec((BLK, CONV_D), lambda c: (rev(c), 0)),
                   pl.BlockSpec((BLK, SSD_D), lambda c: (rev(c), 0)),
                   pl.BlockSpec((BLK, BLK), lambda c: (rev(c), 0)),
                   nvec, vec, vec, vec],
        out_shape=[jax.ShapeDtypeStruct((t, CONV_D), F32), jax.ShapeDtypeStruct((t, SSD_D), F32),
                   jax.ShapeDtypeStruct((t, BLK), F32), jax.ShapeDtypeStruct((1, SSD_D), F32),
                   jax.ShapeDtypeStruct((1, BLK), F32), jax.ShapeDtypeStruct((1, BLK), F32),
                   jax.ShapeDtypeStruct((1, BLK), F32)],
        scratch_shapes=[pltpu.VMEM((SSD_N, SSD_D), F32)],
        compiler_params=_cp("arbitrary"),
    )(dmix, xa, proj, proj, sprev, dtb, alog, dskip, normg)


def _segments(nb, fine):
    if fine:
        cuts = list(range(0, nb, 2)) + [nb]
    else:
        cuts = sorted({0, nb} | {max(1, round(nb * f)) for f in (0.3, 0.53, 0.77)})
    return list(zip(cuts[:-1], cuts[1:]))


def attn_fwd(q, k, v, qcol, kcol, vcol, nh, dq, dv, scale, c_col=None, c_row=None, lane0=0):
    t = q.shape[0]
    tq = BLK
    use_bias = c_col is not None

    def body(*refs):
        if use_bias:
            q_ref, k_ref, v_ref, cc_ref, cr_ref, o_ref, l_ref = refs
        else:
            q_ref, k_ref, v_ref, o_ref, l_ref = refs
        i = pl.program_id(0)
        rowg = i * tq + lax.broadcasted_iota(jnp.int32, (tq, 1), 0)

        def tile(tk):
            col = lax.broadcasted_iota(jnp.int32, (1, tk), 1)
            mask = (col <= rowg) & (col >= PAD)
            outs = []
            lse = jnp.zeros((tq, BLK), F32)
            for h in range(nh):
                s = _nt(q_ref[:, dq * h:dq * (h + 1)].astype(BF16), k_ref[0:tk, dq * h:dq * (h + 1)].astype(BF16)) * scale
                if use_bias:
                    s = s + (cc_ref[:, lane0 + h:lane0 + h + 1] - cr_ref[h:h + 1, 0:tk])
                s = jnp.where(mask, s, NEG)
                m = jnp.max(s, axis=1, keepdims=True)
                p = jnp.exp(s - m)
                l = jnp.sum(p, axis=1, keepdims=True)
                outs.append(_nn(p.astype(BF16), v_ref[0:tk, dv * h:dv * (h + 1)].astype(BF16)) / l)
                lse += _lane_put(m + jnp.log(l), h)
            o_ref[...] = jnp.concatenate(outs, axis=1).astype(BF16)
            l_ref[...] = lse.T[0:8, :]

        for t0, t1 in _segments(t // tq, True):
            pl.when((i >= t0) & (i < t1))(functools.partial(tile, t1 * BLK))

    in_specs = [pl.BlockSpec((tq, nh * dq), lambda i: (i, qcol)),
                pl.BlockSpec((t, nh * dq), lambda i: (0, kcol)),
                pl.BlockSpec((t, nh * dv), lambda i: (0, vcol))]
    args = [q, k, v]
    if use_bias:
        in_specs += [pl.BlockSpec((tq, BLK), lambda i: (i, 0)), pl.BlockSpec((8, t), lambda i: (0, 0))]
        args += [c_col, c_row]
    return pl.pallas_call(
        body, name="attn_fwd", grid=(t // tq,),
        in_specs=in_specs,
        out_specs=[pl.BlockSpec((tq, nh * dv), lambda i: (i, 0)), pl.BlockSpec((8, tq), lambda i: (0, i))],
        out_shape=[jax.ShapeDtypeStruct((t, nh * dv), BF16), jax.ShapeDtypeStruct((8, t), F32)],
        compiler_params=_cp("arbitrary"),
    )(*args)


def attn_bwd(q, k, v, do, lse_row, o, qcol, kcol, vcol, docol, ocol, nh, dq, dv, scale, c_col=None, c_row=None, lane0=0):
    t = q.shape[0]
    tq = BLK
    use_bias = c_col is not None
    nq = t // tq

    def body(*refs):
        if use_bias:
            (q_ref, k_ref, v_ref, do_ref, l_ref, o_ref, cc_ref, cr_ref, dq_ref, dk_ref, dv_ref, dcq_ref, dck_ref,
             kt, ckb, dacc) = refs
        else:
            q_ref, k_ref, v_ref, do_ref, l_ref, o_ref, dq_ref, dk_ref, dv_ref, kt = refs
        i = pl.program_id(0)

        @pl.when(i == 0)
        def _():
            kt[...] = k_ref[...].astype(BF16).T
            dk_ref[...] = jnp.zeros_like(dk_ref)
            dv_ref[...] = jnp.zeros_like(dv_ref)
            if use_bias:
                dacc[...] = jnp.zeros_like(dacc)
                for h in range(nh):
                    ckb[h] = jnp.broadcast_to(cc_ref[:, lane0 + h:lane0 + h + 1], (t, BLK))

        qry = i * tq + lax.broadcasted_iota(jnp.int32, (1, tq), 1)
        dot = (do_ref[...].astype(F32) * o_ref[...].astype(F32)).T

        def tile(tk):
            key = lax.broadcasted_iota(jnp.int32, (tk, 1), 0)
            mask = (key <= qry) & (key >= PAD)
            dqts, dcqs = [], []
            for h in range(nh):
                qh = q_ref[:, dq * h:dq * (h + 1)].astype(BF16)
                kh = k_ref[0:tk, dq * h:dq * (h + 1)].astype(BF16)
                vh = v_ref[0:tk, dv * h:dv * (h + 1)].astype(BF16)
                doh = do_ref[:, dv * h:dv * (h + 1)].astype(BF16)
                delta = jnp.sum(dot[dv * h:dv * (h + 1), :], axis=0, keepdims=True)
                st = _nt(kh, qh) * scale
                if use_bias:
                    st = st + (cr_ref[h:h + 1, :] - ckb[h, 0:tk, :])
                pt = jnp.exp(jnp.where(mask, st, NEG) - l_ref[h:h + 1, :])
                dst = pt * (_nt(vh, doh) - delta)
                dsb = dst.astype(BF16)
                dk_ref[0:tk, dq * h:dq * (h + 1)] += _nn(dsb, qh) * scale
                dv_ref[0:tk, dv * h:dv * (h + 1)] += _nn(pt.astype(BF16), doh)
                dqts.append(_nn(kt[dq * h:dq * (h + 1), 0:tk], dsb))
                if use_bias:
                    dcqs.append(jnp.sum(dst, axis=0, keepdims=True))
                    dacc[h, 0:tk, :] += dst
            dq_ref[...] = jnp.concatenate(dqts, axis=0).T * scale
            if use_bias:
                dcq_ref[...] = jnp.concatenate(dcqs + [jnp.zeros((8 - nh, tq), F32)], axis=0)

        for t0, t1 in _segments(nq, not use_bias):
            pl.when((i >= t0) & (i < t1))(functools.partial(tile, t1 * BLK))

        if use_bias:
            @pl.when(i == nq - 1)
            def _():
                lane = lax.broadcasted_iota(jnp.int32, (1, BLK), 1)
                tot = jnp.zeros((t, BLK), F32)
                for h in range(nh):
                    tot += jnp.where(lane == lane0 + h, jnp.sum(dacc[h], axis=1, keepdims=True), 0.0)
                dck_ref[...] = tot

    keys_q = pl.BlockSpec((t, nh * dq), lambda i: (0, 0))
    keys_v = pl.BlockSpec((t, nh * dv), lambda i: (0, 0))
    keys_c = pl.BlockSpec((t, BLK), lambda i: (0, 0))
    qrow = pl.BlockSpec((8, tq), lambda i: (0, i))
    in_specs = [pl.BlockSpec((tq, nh * dq), lambda i: (i, qcol)),
                pl.BlockSpec((t, nh * dq), lambda i: (0, kcol)),
                pl.BlockSpec((t, nh * dv), lambda i: (0, vcol)),
                pl.BlockSpec((tq, nh * dv), lambda i: (i, docol)),
                qrow,
                pl.BlockSpec((tq, nh * dv), lambda i: (i, ocol))]
    args = [q, k, v, do, lse_row, o]
    out_specs = [pl.BlockSpec((tq, nh * dq), lambda i: (i, 0)), keys_q, keys_v]
    out_shape = [jax.ShapeDtypeStruct((t, nh * dq), F32), jax.ShapeDtypeStruct((t, nh * dq), F32),
                 jax.ShapeDtypeStruct((t, nh * dv), F32)]
    scratch = [pltpu.VMEM((nh * dq, t), BF16)]
    if use_bias:
        in_specs += [keys_c, qrow]
        args += [c_col, c_row]
        out_specs += [qrow, keys_c]
        out_shape += [jax.ShapeDtypeStruct((8, t), F32), jax.ShapeDtypeStruct((t, BLK), F32)]
        scratch += [pltpu.VMEM((nh, t, BLK), F32), pltpu.VMEM((nh, t, BLK), F32)]
    return pl.pallas_call(
        body, name="attn_bwd", grid=(nq,),
        in_specs=in_specs, out_specs=out_specs, out_shape=out_shape, scratch_shapes=scratch,
        compiler_params=_cp("arbitrary"),
    )(*args)


def fox_pre(proj, fb):
    t = proj.shape[0]
    nb = t // BLK

    def body(sm_ref, fb_ref, c_ref, cr_ref):
        x = sm_ref[...] + fb_ref[...]
        lane = lax.broadcasted_iota(jnp.int32, (1, BLK), 1)
        keep = _valid_rows(t, 0) & (lane >= SM_F) & (lane < SM_F + FOX_H)
        logf = jnp.where(keep, jnp.minimum(x, 0.0) - jnp.log(1.0 + jnp.exp(-jnp.abs(x))), 0.0)
        tri = _tri().astype(F32)
        carry = jnp.zeros((1, BLK), F32)
        for b in range(nb):
            cb = _nn_hi(tri, logf[b * BLK:(b + 1) * BLK, :]) + carry
            c_ref[b * BLK:(b + 1) * BLK, :] = cb
            carry = cb[BLK - 1:BLK, :]
        cr_ref[...] = c_ref[...].T[SM_F:SM_F + 8, :]

    return pl.pallas_call(
        body, name="fox_pre", grid=(1,),
        in_specs=[pl.BlockSpec((t, BLK), lambda i: (0, C_SM // BLK)), pl.BlockSpec((1, BLK), lambda i: (0, 0))],
        out_specs=[pl.BlockSpec((t, BLK), lambda i: (0, 0)), pl.BlockSpec((8, t), lambda i: (0, 0))],
        out_shape=[jax.ShapeDtypeStruct((t, BLK), F32), jax.ShapeDtypeStruct((8, t), F32)],
        compiler_params=_cp("arbitrary"),
    )(proj, fb)


def fox_pre_bwd(dcq, dck, proj, fb, dsm_in):
    t = proj.shape[0]
    nb = t // BLK

    def body(dcq_ref, dck_ref, sm_ref, fb_ref, din_ref, dsm_ref, dfb_ref, scr):
        triu = _tri(lower=False).astype(F32)
        carry = jnp.zeros((1, BLK), F32)
        scr[...] = jnp.concatenate([jnp.zeros((SM_F, t), F32), dcq_ref[...], jnp.zeros((BLK - SM_F - 8, t), F32)], axis=0).T
        for b in range(nb - 1, -1, -1):
            blk = scr[b * BLK:(b + 1) * BLK, :] - dck_ref[b * BLK:(b + 1) * BLK, :]
            cb = _nn_hi(triu, blk) + carry
            scr[b * BLK:(b + 1) * BLK, :] = cb
            carry = cb[0:1, :]
        x = sm_ref[...] + fb_ref[...]
        lane = lax.broadcasted_iota(jnp.int32, (1, BLK), 1)
        keep = _valid_rows(t, 0) & (lane >= SM_F) & (lane < SM_F + FOX_H)
        df = jnp.where(keep, scr[...] * _sig(-x), 0.0)
        dfb_ref[...] = jnp.sum(df, axis=0, keepdims=True)
        dsm_ref[...] = din_ref[...] + df

    full = pl.BlockSpec((t, BLK), lambda i: (0, 0))
    return pl.pallas_call(
        body, name="fox_pre_bwd", grid=(1,),
        in_specs=[pl.BlockSpec((8, t), lambda i: (0, 0)), full,
                  pl.BlockSpec((t, BLK), lambda i: (0, C_SM // BLK)), pl.BlockSpec((1, BLK), lambda i: (0, 0)), full],
        out_specs=[full, pl.BlockSpec((1, BLK), lambda i: (0, 0))],
        out_shape=[jax.ShapeDtypeStruct((t, BLK), F32), jax.ShapeDtypeStruct((1, BLK), F32)],
        scratch_shapes=[pltpu.VMEM((t, BLK), F32)],
        compiler_params=_cp("arbitrary"),
    )(dcq, dck, proj, fb, dsm_in)


def _swap_rope(x):
    lane = lax.broadcasted_iota(jnp.int32, (1, BLK), 1)
    return jnp.where((lane >= SM_KR) & (lane < SM_KR + 16), pltpu.roll(x, BLK - 16, 1),
                     jnp.where((lane >= SM_KR + 16) & (lane < SM_KR + 32), pltpu.roll(x, 16, 1), 0.0))


def _rms(x, g):
    r = lax.rsqrt(jnp.mean(x * x, axis=1, keepdims=True) + EPS)
    return r, x * r


def mla_pre(proj, qg, kvg, wq, wk, wv, cosq, sinq):
    t = proj.shape[0]
    tm = _row_tile(t)

    def body(cq_ref, ckv_ref, sm_ref, qg_ref, kvg_ref, wq_ref, wk_ref, wv_ref, cos_ref, sin_ref,
             q_ref, k_ref, v_ref, cqn_ref, ckvn_ref):
        cs, sn = cos_ref[...], sin_ref[...]
        _, xh = _rms(cq_ref[...], None)
        cqn = (xh * qg_ref[...]).astype(BF16)
        cqn_ref[...] = cqn
        qraw = _nn(cqn, wq_ref[...])
        qs = []
        for h in range(MLA_H):
            hb = qraw[:, BLK * h:BLK * (h + 1)]
            qs.append(hb * cs + _swap_rope(hb) * sn)
        q_ref[...] = jnp.concatenate(qs, axis=1).astype(BF16)
        _, kh = _rms(ckv_ref[...], None)
        ckvn = (kh * kvg_ref[...]).astype(BF16)
        ckvn_ref[...] = ckvn
        kraw = _nn(ckvn, wk_ref[...])
        v_ref[...] = _nn(ckvn, wv_ref[...]).astype(BF16)
        lane = lax.broadcasted_iota(jnp.int32, (1, BLK), 1)
        kr = sm_ref[...]
        krr = jnp.where((lane >= SM_KR) & (lane < SM_KR + MLA_ROPE), kr * cs + _swap_rope(kr) * sn, 0.0)
        k_ref[...] = jnp.concatenate([kraw[:, BLK * h:BLK * (h + 1)] + krr for h in range(MLA_H)], axis=1).astype(BF16)

    def rows(w, cb):
        return pl.BlockSpec((tm, w), lambda i: (i, cb))

    def whole(a):
        return pl.BlockSpec(a.shape, lambda i: (0, 0))

    return pl.pallas_call(
        body, name="mla_pre", grid=(t // tm,),
        in_specs=[rows(MLA_QL, C_CQ // MLA_QL), rows(MLA_KVL, C_CKV // MLA_KVL), rows(BLK, C_SM // BLK),
                  whole(qg), whole(kvg), whole(wq), whole(wk), whole(wv), rows(BLK, 0), rows(BLK, 0)],
        out_specs=[rows(512, 0), rows(512, 0), rows(256, 0), rows(MLA_QL, 0), rows(MLA_KVL, 0)],
        out_shape=[jax.ShapeDtypeStruct((t, 512), BF16), jax.ShapeDtypeStruct((t, 512), BF16),
                   jax.ShapeDtypeStruct((t, 256), BF16), jax.ShapeDtypeStruct((t, MLA_QL), BF16),
                   jax.ShapeDtypeStruct((t, MLA_KVL), BF16)],
        compiler_params=_cp("arbitrary"),
    )(proj, proj, proj, qg, kvg, wq, wk, wv, cosq, sinq)


def mla_pre_bwd(dq, dk, dv, proj, cqn, ckvn, qg, kvg, wq, wk, wv, cosq, sinq, dsm_in):
    t = proj.shape[0]
    tm = _row_tile(t)

    def body(dq_ref, dk_ref, dv_ref, cq_ref, ckv_ref, cqn_ref, ckvn_ref, qg_ref, kvg_ref, wq_ref, wk_ref, wv_ref,
             cos_ref, sin_ref, din_ref, dcq_ref, dckv_ref, dsm_ref, dwq_ref, dwk_ref, dwv_ref, dqg_ref, dkvg_ref):
        i = pl.program_id(0)

        @pl.when(i == 0)
        def _():
            for r in (dwq_ref, dwk_ref, dwv_ref, dqg_ref, dkvg_ref):
                r[...] = jnp.zeros_like(r)

        cs, sn = cos_ref[...], sin_ref[...]
        lane = lax.broadcasted_iota(jnp.int32, (1, BLK), 1)

        def unrope(dy):
            return dy * cs + _swap_rope(dy * sn)

        dqp = jnp.concatenate([unrope(dq_ref[:, BLK * h:BLK * (h + 1)]) for h in range(MLA_H)], axis=1).astype(BF16)
        dwq_ref[...] += _tn(cqn_ref[...], dqp)
        dcqn = _nt(dqp, wq_ref[...])
        r, xh = _rms(cq_ref[...], None)
        dqg_ref[...] += jnp.sum(dcqn * xh, axis=0, keepdims=True)
        dxh = dcqn * qg_ref[...]
        dcq_ref[...] = r * (dxh - xh * jnp.mean(dxh * xh, axis=1, keepdims=True))

        dkn, dkr = [], jnp.zeros((tm, BLK), F32)
        for h in range(MLA_H):
            blk = dk_ref[:, BLK * h:BLK * (h + 1)]
            dkn.append(jnp.where(lane < MLA_NOPE, blk, 0.0))
            dkr += jnp.where((lane >= SM_KR) & (lane < SM_KR + MLA_ROPE), blk, 0.0)
        dknb = jnp.concatenate(dkn, axis=1).astype(BF16)
        dvb = dv_ref[...].astype(BF16)
        ckvn = ckvn_ref[...]
        dwk_ref[...] += _tn(ckvn, dknb)
        dwv_ref[...] += _tn(ckvn, dvb)
        dckvn = _nt(dknb, wk_ref[...]) + _nt(dvb, wv_ref[...])
        r2, kh = _rms(ckv_ref[...], None)
        dkvg_ref[...] += jnp.sum(dckvn * kh, axis=0, keepdims=True)
        dkh = dckvn * kvg_ref[...]
        dckv_ref[...] = r2 * (dkh - kh * jnp.mean(dkh * kh, axis=1, keepdims=True))
        dsm_ref[...] = din_ref[...] + jnp.where((lane >= SM_KR) & (lane < SM_KR + MLA_ROPE), unrope(dkr), 0.0)

    def rows(w, cb):
        return pl.BlockSpec((tm, w), lambda i: (i, cb))

    def whole(a):
        return pl.BlockSpec(a.shape, lambda i: (0, 0))

    def wshape(a):
        return jax.ShapeDtypeStruct(a.shape, F32)

    return pl.pallas_call(
        body, name="mla_pre_bwd", grid=(t // tm,),
        in_specs=[rows(512, 0), rows(512, 0), rows(256, 0), rows(MLA_QL, C_CQ // MLA_QL), rows(MLA_KVL, C_CKV // MLA_KVL),
                  rows(MLA_QL, 0), rows(MLA_KVL, 0), whole(qg), whole(kvg), whole(wq), whole(wk), whole(wv),
                  rows(BLK, 0), rows(BLK, 0), rows(BLK, 0)],
        out_specs=[rows(MLA_QL, 0), rows(MLA_KVL, 0), rows(BLK, 0), whole(wq), whole(wk), whole(wv), whole(qg), whole(kvg)],
        out_shape=[jax.ShapeDtypeStruct((t, MLA_QL), F32), jax.ShapeDtypeStruct((t, MLA_KVL), F32),
                   jax.ShapeDtypeStruct((t, BLK), F32), wshape(wq), wshape(wk), wshape(wv), wshape(qg), wshape(kvg)],
        compiler_params=_cp("arbitrary"),
    )(dq, dk, dv, proj, proj, cqn, ckvn, qg, kvg, wq, wk, wv, cosq, sinq, dsm_in)


def _slot_sum(me, own, recv_ref):
    gg = own.astype(F32)
    for s in range(N_DEV):
        gg = gg + jnp.where(me == s, 0.0, recv_ref[s].astype(F32))
    return gg


def adamw(w, m, v, g=None, recv=None, own=None, me_arr=None):
    shape = w.shape
    c = shape[-1]
    from_recv = recv is not None
    if not from_recv:
        me_arr = jnp.zeros((1,), jnp.int32)
    nl = len(recv) if from_recv else 1
    rws = w.size // c // nl
    tr = rws
    for d in (1024, 512, 352, 256, 128, 64, 32, 16, 8):
        if rws % d == 0 and d * c * 4 <= (2 << 20):
            tr = d
            break
    nt = rws // tr
    w2, m2, v2 = (a.reshape(nl, rws, c) for a in (w, m, v))
    if from_recv:
        gin = [a.reshape(N_DEV, rws, c) for a in list(recv) + list(own)]
    else:
        gin = [g.reshape(1, rws, c)]

    def body(me_ref, w_ref, m_ref, v_ref, *rest):
        g_refs, outs = rest[:len(gin)], rest[len(gin):]
        if from_recv:
            g_out, outs = outs[0], outs[1:]
            for li in range(nl):
                @pl.when(pl.program_id(0) == li)
                def _(li=li):
                    g_out[...] = _slot_sum(me_ref[0], g_refs[nl + li][...], g_refs[li])
            gg = g_out[...]
        else:
            gg = g_refs[0][...]
        d_ref, nm_ref, nv_ref = outs
        nm = B1 * m_ref[...] + (1.0 - B1) * gg
        nv = B2 * v_ref[...] + (1.0 - B2) * (gg * gg)
        mh = nm / (1.0 - B1 ** STEP)
        vh = nv / (1.0 - B2 ** STEP)
        d_ref[...] = -LR * (mh / (jnp.sqrt(vh) + AEPS) + WD * w_ref[...])
        nm_ref[...] = nm
        nv_ref[...] = nv

    row = pl.BlockSpec((None, tr, c), lambda l, i, me: (l, i, 0))
    if from_recv:
        gspecs = [pl.BlockSpec((N_DEV, tr, c), lambda l, i, me, li=li: (0, jnp.where(l == li, i, 0), 0))
                  for li in range(nl)]
        gspecs += [pl.BlockSpec((None, tr, c), lambda l, i, me, li=li: (me[0], jnp.where(l == li, i, 0), 0))
                   for li in range(nl)]
    else:
        gspecs = [row]
    nout = 4 if from_recv else 3
    outs = pl.pallas_call(
        body, name="adamw",
        grid_spec=pltpu.PrefetchScalarGridSpec(num_scalar_prefetch=1, grid=(nl, nt), in_specs=[row, row, row] + gspecs,
                                               out_specs=[row] * nout),
        out_shape=[jax.ShapeDtypeStruct((nl, rws, c), F32)] * nout,
        compiler_params=_cp("arbitrary", "arbitrary"),
    )(me_arr, w2, m2, v2, *gin)
    return tuple(o.reshape(shape) for o in outs)


def sum_slots(recv, own=None, me_arr=None):
    _, r, c = recv.shape
    if own is None:
        own, me_arr = recv, jnp.zeros((1,), jnp.int32)
        plain = True
    else:
        plain = False

    def body(me_ref, r_ref, own_ref, o_ref):
        if plain:
            gg = r_ref[0].astype(F32)
            for s in range(1, N_DEV):
                gg = gg + r_ref[s].astype(F32)
            o_ref[...] = gg
        else:
            o_ref[...] = _slot_sum(me_ref[0], own_ref[...], r_ref)

    return pl.pallas_call(
        body, name="sum_slots",
        grid_spec=pltpu.PrefetchScalarGridSpec(
            num_scalar_prefetch=1, grid=(1,),
            in_specs=[pl.BlockSpec((N_DEV, r, c), lambda i, me: (0, 0, 0)),
                      pl.BlockSpec((None, r, c), lambda i, me: (me[0], 0, 0))],
            out_specs=pl.BlockSpec((r, c), lambda i, me: (0, 0))),
        out_shape=jax.ShapeDtypeStruct((r, c), F32),
        compiler_params=_cp("arbitrary"),
    )(me_arr, recv, own)


_FLIPS = [(0, 0, 1), (0, 1, 0), (0, 1, 1), (1, 0, 0), (1, 0, 1), (1, 1, 0), (1, 1, 1)]
_ANY = pl.BlockSpec(memory_space=pl.ANY)


def _mesh_place():
    x, y, c = lax.axis_index("x"), lax.axis_index("y"), lax.axis_index("c")
    me = 4 * x + 2 * y + c
    peers = [((x + fx) % 2, (y + fy) % 2, (c + fc) % 2) for fx, fy, fc in _FLIPS]
    return me, peers


def place_own(src, l, dtype, me_arr):
    _, r, c = src.shape
    tr = r
    for d in (512, 352, 256, 128, 64, 32, 16, 8):
        if r % d == 0 and d * c * 4 <= (2 << 20):
            tr = d
            break

    def body(me_ref, s_ref, o_ref):
        o_ref[...] = s_ref[...].astype(dtype)

    return pl.pallas_call(
        body, name="place_own",
        grid_spec=pltpu.PrefetchScalarGridSpec(
            num_scalar_prefetch=1, grid=(r // tr,),
            in_specs=[pl.BlockSpec((None, tr, c), lambda i, me: (l, i, 0))],
            out_specs=pl.BlockSpec((None, tr, c), lambda i, me: (me[0], i, 0))),
        out_shape=jax.ShapeDtypeStruct((N_DEV, r, c), dtype),
        compiler_params=_cp("arbitrary"),
    )(me_arr, src)


_HBM = pl.BlockSpec(memory_space=pltpu.HBM)
_SEMS = pl.BlockSpec(memory_space=pltpu.SEMAPHORE)
_EFFECT = pltpu.SideEffectType.DATAFLOW_SIDE_EFFECTING


def exchange_start(mode, arrays, name, after=None):
    n = len(arrays)
    gather = mode == "gather"
    ns = 0 if gather else n
    zones = list(arrays) if gather else [lax.empty(a.shape, a.dtype) for a in arrays]
    ops = ([] if gather else list(arrays)) + zones
    extra = [] if after is None else [after]

    def body(*refs):
        srcs, lands = refs[:ns], refs[ns:ns + n]
        send_sems, recv_sems = refs[ns + n + len(extra)], refs[ns + n + len(extra) + 1]
        token = refs[-1]
        me, peers = _mesh_place()
        ids = [4 * p[0] + 2 * p[1] + p[2] for p in peers]
        for j in range(n):
            for k in range(N_DEV - 1):
                src = lands[j].at[me] if gather else srcs[j].at[ids[k]]
                pltpu.make_async_remote_copy(src_ref=src, dst_ref=lands[j].at[me],
                                             send_sem=send_sems.at[j * (N_DEV - 1) + k],
                                             recv_sem=recv_sems.at[j * (N_DEV - 1) + k], device_id=peers[k],
                                             device_id_type=pl.DeviceIdType.MESH).start()
        token[...] = jnp.zeros_like(token)

    nsem = n * (N_DEV - 1)
    res = pl.pallas_call(
        body, name=name,
        in_specs=[_HBM] * (ns + n) + [_ANY] * len(extra),
        out_specs=(_SEMS, _SEMS, *[_HBM] * (ns + n), pl.BlockSpec(memory_space=pltpu.VMEM)),
        out_shape=(pltpu.SemaphoreType.DMA((nsem,)), pltpu.SemaphoreType.DMA((nsem,)),
                   *[pltpu.HBM(a.shape, a.dtype) for a in ops], jax.ShapeDtypeStruct((8, BLK), F32)),
        input_output_aliases={i: 2 + i for i in range(ns + n)},
        compiler_params=pltpu.CompilerParams(has_side_effects=_EFFECT),
    )(*[pltpu.with_memory_space_constraint(a, pltpu.HBM) for a in ops], *extra)
    return dict(gather=gather, send=res[0], recv=res[1], srcs=list(res[2:2 + ns]), lands=list(res[2 + ns:2 + ns + n]),
                token=res[-1])


def exchange_wait(hd, idxs, name, after):
    gather = hd["gather"]
    n = len(idxs)
    ns = 0 if gather else n
    ops = ([] if gather else [hd["srcs"][j] for j in idxs]) + [hd["lands"][j] for j in idxs]

    def body(*refs):
        srcs, lands = refs[:ns], refs[ns:ns + n]
        send_sems, recv_sems = refs[ns + n], refs[ns + n + 1]
        me, peers = _mesh_place()
        ids = [4 * p[0] + 2 * p[1] + p[2] for p in peers]
        for p, j in enumerate(idxs):
            for k in range(N_DEV - 1):
                src = lands[p].at[me] if gather else srcs[p].at[ids[k]]
                cp = pltpu.make_async_remote_copy(src_ref=src, dst_ref=lands[p].at[ids[k]],
                                                  send_sem=send_sems.at[j * (N_DEV - 1) + k],
                                                  recv_sem=recv_sems.at[j * (N_DEV - 1) + k], device_id=peers[k],
                                                  device_id_type=pl.DeviceIdType.MESH)
                cp.wait_send()
                cp.wait_recv()

    res = pl.pallas_call(
        body, name=name,
        in_specs=[_HBM] * (ns + n) + [_SEMS, _SEMS, _ANY],
        out_specs=[_HBM] * (ns + n),
        out_shape=[pltpu.HBM(a.shape, a.dtype) for a in ops],
        input_output_aliases={i: i for i in range(ns + n)},
        compiler_params=pltpu.CompilerParams(has_side_effects=_EFFECT),
    )(*ops, hd["send"], hd["recv"], after)
    return list(res[:ns]), list(res[ns:])


def _chip_place():
    x, y, c = lax.axis_index("x"), lax.axis_index("y"), lax.axis_index("c")
    chips = [((x + 1) % 2, y), (x, (y + 1) % 2), ((x + 1) % 2, (y + 1) % 2)]
    ident = lambda p: 4 * p[0] + 2 * p[1] + p[2]
    return dict(me=4 * x + 2 * y + c, sib=(x, y, 1 - c), sib_id=4 * x + 2 * y + 1 - c,
                same=[(cx, cy, c) for cx, cy in chips], same_ids=[ident((cx, cy, c)) for cx, cy in chips],
                other_ids=[ident((cx, cy, 1 - c)) for cx, cy in chips])


def _remote(src, dst, send_sem, recv_sem, dev):
    return pltpu.make_async_remote_copy(src_ref=src, dst_ref=dst, send_sem=send_sem, recv_sem=recv_sem, device_id=dev,
                                        device_id_type=pl.DeviceIdType.MESH)


def gather_start(zones, name):
    n = len(zones)

    def body(*refs):
        lands, send_sems, recv_sems, token = refs[:n], refs[n], refs[n + 1], refs[-1]
        pc = _chip_place()
        for j in range(n):
            own = lands[j].at[pc["me"]]
            for k, dev in enumerate([pc["sib"]] + pc["same"]):
                _remote(own, own, send_sems.at[4 * j + k], recv_sems.at[4 * j + k], dev).start()
        token[...] = jnp.zeros_like(token)

    res = pl.pallas_call(
        body, name=name,
        in_specs=[_HBM] * n,
        out_specs=(_SEMS, _SEMS, *[_HBM] * n, pl.BlockSpec(memory_space=pltpu.VMEM)),
        out_shape=(pltpu.SemaphoreType.DMA((4 * n,)), pltpu.SemaphoreType.DMA((4 * n,)),
                   *[pltpu.HBM(a.shape, a.dtype) for a in zones], jax.ShapeDtypeStruct((8, BLK), F32)),
        input_output_aliases={i: 2 + i for i in range(n)},
        compiler_params=pltpu.CompilerParams(has_side_effects=_EFFECT),
    )(*[pltpu.with_memory_space_constraint(a, pltpu.HBM) for a in zones])
    return dict(send=res[0], recv=res[1], lands=list(res[2:2 + n]), token=res[-1])


def gather_relay(hd, idxs, name, after):
    n = len(idxs)

    def body(*refs):
        lands, send_sems, recv_sems = refs[:n], refs[n], refs[n + 1]
        fsend, frecv, token = refs[n + 3 + n], refs[n + 4 + n], refs[-1]
        pc = _chip_place()
        for p, j in enumerate(idxs):
            for k in range(3):
                _remote(lands[p].at[pc["me"]], lands[p].at[pc["same_ids"][k]], send_sems.at[4 * j + 1 + k],
                        recv_sems.at[4 * j + 1 + k], pc["same"][k]).wait_recv()
        for p in range(n):
            for k in range(3):
                got = lands[p].at[pc["same_ids"][k]]
                _remote(got, got, fsend.at[3 * p + k], frecv.at[3 * p + k], pc["sib"]).start()
        token[...] = jnp.zeros_like(token)

    ops = [hd["lands"][j] for j in idxs]
    res = pl.pallas_call(
        body, name=name,
        in_specs=[_HBM] * n + [_SEMS, _SEMS, _ANY],
        out_specs=(*[_HBM] * n, _SEMS, _SEMS, pl.BlockSpec(memory_space=pltpu.VMEM)),
        out_shape=(*[pltpu.HBM(a.shape, a.dtype) for a in ops], pltpu.SemaphoreType.DMA((3 * n,)),
                   pltpu.SemaphoreType.DMA((3 * n,)), jax.ShapeDtypeStruct((8, BLK), F32)),
        input_output_aliases={i: i for i in range(n)},
        compiler_params=pltpu.CompilerParams(has_side_effects=_EFFECT),
    )(*ops, hd["send"], hd["recv"], after)
    return dict(lands=list(res[:n]), fsend=res[n], frecv=res[n + 1], token=res[-1])


def gather_wait(hd, rl, idxs, name, after):
    n = len(idxs)

    def body(*refs):
        lands, send_sems, recv_sems, fsend, frecv = refs[:n], refs[n], refs[n + 1], refs[n + 2], refs[n + 3]
        pc = _chip_place()
        for p, j in enumerate(idxs):
            own = lands[p].at[pc["me"]]
            for k, dev in enumerate([pc["sib"]] + pc["same"]):
                _remote(own, own, send_sems.at[4 * j + k], recv_sems.at[4 * j + k], dev).wait_send()
            _remote(own, lands[p].at[pc["sib_id"]], send_sems.at[4 * j], recv_sems.at[4 * j], pc["sib"]).wait_recv()
            for k in range(3):
                cp = _remote(lands[p].at[pc["same_ids"][k]], lands[p].at[pc["other_ids"][k]], fsend.at[3 * p + k],
                             frecv.at[3 * p + k], pc["sib"])
                cp.wait_send()
                cp.wait_recv()

    res = pl.pallas_call(
        body, name=name,
        in_specs=[_HBM] * n + [_SEMS, _SEMS, _SEMS, _SEMS, _ANY],
        out_specs=[_HBM] * n,
        out_shape=[pltpu.HBM(a.shape, a.dtype) for a in rl["lands"]],
        input_output_aliases={i: i for i in range(n)},
        compiler_params=pltpu.CompilerParams(has_side_effects=_EFFECT),
    )(*rl["lands"], hd["send"], hd["recv"], rl["fsend"], rl["frecv"], after)
    return list(res)


def _pad_cols(a, n):
    return jnp.pad(a, ((0, 0),) * (a.ndim - 1) + ((0, n - a.shape[-1]),))


def w_in_to_padded(w):
    z = lambda n: jnp.zeros(w.shape[:-1] + (n,), w.dtype)
    return jnp.concatenate([
        w[..., 0:1280], w[..., 1288:2056], w[..., 2060:2316], w[..., 2316:2444],
        w[..., 1280:1288], w[..., 2056:2060], z(SM_KR - SM_F - FOX_H), w[..., 2444:2476], z(BLK - SM_KR - MLA_ROPE)], axis=-1)


def w_in_from_padded(g):
    s = C_SM
    return jnp.concatenate([
        g[..., 0:1280], g[..., s + SM_DT:s + SM_DT + 8], g[..., 1280:2048], g[..., s + SM_F:s + SM_F + 4],
        g[..., 2048:2304], g[..., 2304:2432], g[..., s + SM_KR:s + SM_KR + MLA_ROPE]], axis=-1)


def _unshard_cols(gth):
    n, r, c = gth.shape
    return jnp.transpose(gth, (1, 0, 2)).reshape(r, n * c)


def _shard_cols(full):
    r, nc = full.shape
    return jnp.transpose(full.reshape(r, N_DEV, nc // N_DEV), (1, 0, 2))


def mla_weights(uq_g, ukv_g):
    uq = _unshard_cols(uq_g)
    dqh = MLA_NOPE + MLA_ROPE
    wq = jnp.concatenate([_pad_cols(uq[:, dqh * h:dqh * (h + 1)], BLK) for h in range(MLA_H)], axis=1)
    wk = jnp.concatenate([_pad_cols(ukv_g[2 * h], BLK) for h in range(MLA_H)], axis=1)
    wv = jnp.concatenate([ukv_g[2 * h + 1] for h in range(MLA_H)], axis=1)
    return wq, wk, wv


def mla_weight_grads(dwq, dwk, dwv):
    dqh = MLA_NOPE + MLA_ROPE
    duq = _shard_cols(jnp.concatenate([dwq[:, BLK * h:BLK * h + dqh] for h in range(MLA_H)], axis=1))
    parts = []
    for h in range(MLA_H):
        parts += [dwk[:, BLK * h:BLK * h + MLA_NOPE], dwv[:, MLA_V * h:MLA_V * (h + 1)]]
    return duq, jnp.stack(parts, axis=0)


def rope_tables(t):
    pos = (jnp.arange(t, dtype=jnp.int32) - PAD).astype(F32)
    inv_freq = 1.0 / (10000.0 ** (jnp.arange(0, MLA_ROPE, 2, dtype=F32) / MLA_ROPE))
    ang = pos[:, None] * inv_freq[None, :]
    cos, sin = jnp.cos(ang), jnp.sin(ang)
    one, zero = jnp.ones((t, SM_KR), F32), jnp.zeros((t, SM_KR), F32)
    tail = BLK - SM_KR - MLA_ROPE
    cosq = jnp.concatenate([one, cos, cos, jnp.ones((t, tail), F32)], axis=1)
    sinq = jnp.concatenate([zero, -sin, sin, jnp.zeros((t, tail), F32)], axis=1)
    return cosq, sinq


def _lanes(v, off=0):
    return jnp.pad(v.astype(F32), (off, BLK - off - v.shape[0]))[None, :]


def layer_fwd(x, ln, hb, getw, tabs, ahead):
    sv = {"h0b": hb}
    def behind(vec, tok):
        return vec if tok is None else vec + 0.0 * tok[0:1, 0:1]

    W = dict(getw("ffn1", hb))
    ln1 = (behind(W["ln1_g"], ahead(0, "mix", hb, 1)), W["ln1_b"])
    u, v, r1, h1b = ffn_fwd_seq(x, ln, W["g1"], W["u1"], W["d1"], ln1)
    sv.update(u1=u, v1=v, r1=r1, h1b=h1b)
    W.update(getw("mix", h1b))
    ln2 = (W["ln2_g"], W["ln2_b"])
    proj = mm_nn(h1b, W["w_in"])
    xa = conv_fwd(proj, W["conv_w"], W["conv_b"])
    y_ssd, sprev = ssd_fwd(xa, proj, W["dtb"], W["alog"], W["dskip"], W["normg"])
    c_col, c_row = fox_pre(proj, W["fb"])
    y_fox, lse_f = attn_fwd(proj, proj, proj, C_FQ // 256, C_FK // 256, C_FV // 256, FOX_H, FOX_DH, FOX_DH,
                            FOX_DH ** -0.5, c_col, c_row, SM_F)
    q, k, vv, cqn, ckvn = mla_pre(proj, behind(W["qg"], ahead(0, "ffn2", y_fox)), W["kvg"], W["wq"], W["wk"], W["wv"], *tabs)
    y_mla, lse_m = attn_fwd(q, k, vv, 0, 0, 0, MLA_H, BLK, MLA_V, (MLA_NOPE + MLA_ROPE) ** -0.5)
    mixcat = jnp.concatenate([y_ssd, y_fox, y_mla], axis=1)
    r2, h2b = mm_res_ln(mixcat, W["w_out"], r1, ln1, ln2)
    sv.update(proj=proj, xa=xa, sprev=sprev, c_col=c_col, c_row=c_row, lse_f=lse_f, q=q, k=k, v=vv, cqn=cqn, ckvn=ckvn,
              lse_m=lse_m, mixcat=mixcat, r2=r2, h2b=h2b)
    W.update(getw("ffn2", h2b))
    ln3 = (behind(W["ln3_g"], ahead(1, "ffn1", h2b)), W["ln3_b"])
    u, v, r3, h3b = ffn_fwd_seq(r2, ln2, W["g2"], W["u2"], W["d2"], ln3)
    sv.update(u2=u, v2=v, r3=r3, W=W)
    return r3, ln3, h3b, sv


def ffn_bwd(parts, r, gamma, hb_in, u, v, wg, wu, wd, after=None):
    dh, dwg, dwu, dwd, dg, db = ffn_bwd_seq(parts, r, gamma, hb_in, u, v, wg, wu, wd, after)
    return dh, dict(d=dwd, g=dwg, u=dwu, ln_g=dg, ln_b=db)


def layer_bwd(parts, sv, emit, tabs, after):
    G = {}
    W = sv["W"]
    dh2, g2 = ffn_bwd(parts, sv["r3"], W["ln3_g"], sv["h2b"], sv["u2"], sv["v2"], W["g2"], W["u2"], W["d2"], after)
    G.update(g2=g2["g"], u2=g2["u"], d2=g2["d"], ln3_g=g2["ln_g"], ln3_b=g2["ln_b"])
    tok = emit("ffn2", G)
    dr2, dmc, G["w_out"], G["ln2_g"], G["ln2_b"] = oproj_bwd(dh2, sv["r2"], W["ln2_g"], sv["mixcat"], W["w_out"], tok)
    proj = sv["proj"]
    dxa, dz, dsm, G["normg"], G["dskip"], G["alog"], G["dtb"] = ssd_bwd(
        dmc, sv["xa"], proj, sv["sprev"], W["dtb"], W["alog"], W["dskip"], W["normg"])
    dxbc, G["conv_w"], G["conv_b"] = conv_bwd(dxa, proj, W["conv_w"], W["conv_b"])
    dfq, dfk, dfv, dcq, dck = attn_bwd(proj, proj, proj, dmc, sv["lse_f"], sv["mixcat"], C_FQ // 256, C_FK // 256,
                                       C_FV // 256, 2, 2, FOX_H, FOX_DH, FOX_DH, FOX_DH ** -0.5, sv["c_col"], sv["c_row"], SM_F)
    dsm, G["fb"] = fox_pre_bwd(dcq, dck, proj, W["fb"], dsm)
    dq, dk, dv = attn_bwd(sv["q"], sv["k"], sv["v"], dmc, sv["lse_m"], sv["mixcat"], 0, 0, 0, 3, 3, MLA_H, BLK, MLA_V,
                          (MLA_NOPE + MLA_ROPE) ** -0.5)
    dcql, dckv, dsm, G["wq"], G["wk"], G["wv"], G["qg"], G["kvg"] = mla_pre_bwd(
        dq, dk, dv, proj, sv["cqn"], sv["ckvn"], W["qg"], W["kvg"], W["wq"], W["wk"], W["wv"], *tabs, dsm)
    dh1p, G["w_in"] = proj_bwd([dz, dxbc, dfq, dfk, dfv, dcql, dckv, dsm], sv["h1b"], W["w_in"])
    tok = emit("mix", G)
    dh0, g1 = ffn_bwd([(dr2, ALPHA), (dh1p, 1.0)], sv["r1"], W["ln1_g"], sv["h0b"], sv["u1"], sv["v1"],
                      W["g1"], W["u1"], W["d1"], tok)
    G.update(g1=g1["g"], u1=g1["u"], d1=g1["d"], ln1_g=g1["ln_g"], ln1_b=g1["ln_b"])
    tok = emit("ffn1", G)
    return [(dh0, 1.0)], G, tok


def local_step(x, target, meta_full, getw, emit, ahead=lambda l, stage, after, min_layer=0: None):
    t = x.shape[0] + BLK
    tabs = rope_tables(t)
    xr, hb = build_h0(meta_full, x)
    ln = None
    saved = []
    for l in range(NL):
        xr, ln, hb, sv = layer_fwd(xr, ln, hb, functools.partial(getw, l), tabs,
                                   lambda dl, stage, after, min_layer=0, l=l: ahead(l + dl, stage, after, min_layer))
        saved.append(sv)
    dy, loss = loss_head(xr, ln, target)
    parts = [(dy, 1.0)]
    grads = [None] * NL
    tok = None
    for l in range(NL - 1, -1, -1):
        parts, grads[l], tok = layer_bwd(parts, saved[l], functools.partial(emit, l), tabs, tok)
    gx, gmeta = split_dh0(parts[0][0], tok)
    return loss, gx, gmeta, grads


_SMALL = ["ln1_g", "ln1_b", "ln2_g", "ln2_b", "ln3_g", "ln3_b", "conv_b", "ssd_norm_g", "mla_q_norm_g",
          "mla_kv_norm_g", "dt_bias", "a_log", "d_skip", "fox_f_b"]
_SMALL_ROWS = 8
_BIG = ["ffn1_w_gate", "ffn1_w_up", "ffn1_w_down", "w_in", "conv_w", "mla_w_uq", "mla_w_ukv", "w_out",
        "ffn2_w_gate", "ffn2_w_up", "ffn2_w_down"]
_NAMES = ["meta", "ffn1_w_gate", "ffn1_w_up", "ffn1_w_down", "ln1_g", "ln1_b", "w_in", "conv_w", "conv_b", "dt_bias",
          "a_log", "d_skip", "ssd_norm_g", "fox_f_b", "mla_q_norm_g", "mla_w_uq", "mla_kv_norm_g", "mla_w_ukv", "w_out",
          "ln2_g", "ln2_b", "ffn2_w_gate", "ffn2_w_up", "ffn2_w_down", "ln3_g", "ln3_b"]


def pack_small(p):
    flat = jnp.concatenate([p[n].astype(F32) for n in _SMALL], axis=1)
    return _pad_cols(flat, _SMALL_ROWS * D).reshape(NL * _SMALL_ROWS, D)


def unpack_small(a, like):
    flat = a.reshape(NL, _SMALL_ROWS * D)
    out, at = {}, 0
    for n in _SMALL:
        out[n] = flat[:, at:at + like[n].shape[1]]
        at += like[n].shape[1]
    return out


_STAGES = {"ffn1": ["ffn1_w_gate", "ffn1_w_up", "ffn1_w_down"],
           "mix": ["w_in", "conv_w", "mla_w_uq", "mla_w_ukv", "w_out"],
           "ffn2": ["ffn2_w_gate", "ffn2_w_up", "ffn2_w_down"]}


_FFN_T = ("ffn1_w_gate", "ffn1_w_up", "ffn2_w_gate", "ffn2_w_up")


def stage_weights(l, stage, g, rep):
    if stage != "mix":
        i = stage[3]
        return {"g" + i: g[f"ffn{i}_w_gate"].reshape(D_FF, D), "u" + i: g[f"ffn{i}_w_up"].reshape(D_FF, D),
                "d" + i: g[f"ffn{i}_w_down"].reshape(D_FF, D),
                "ln1_g" if i == "1" else "ln3_g": rep["ln1_g" if i == "1" else "ln3_g"][l][None, :],
                "ln1_b" if i == "1" else "ln3_b": rep["ln1_b" if i == "1" else "ln3_b"][l][None, :]}
    W = {}
    W["w_in"] = g["w_in"].reshape(D, N_INP)
    W["w_out"] = g["w_out"].reshape(D, D)
    W["wq"], W["wk"], W["wv"] = mla_weights(g["mla_w_uq"], g["mla_w_ukv"])
    W["conv_w"] = _unshard_cols(g["conv_w"])
    for k in ("ln2_g", "ln2_b", "conv_b"):
        W[k] = rep[k][l][None, :]
    W["normg"] = rep["ssd_norm_g"][l][None, :]
    W["qg"] = rep["mla_q_norm_g"][l][None, :]
    W["kvg"] = rep["mla_kv_norm_g"][l][None, :]
    W["dtb"] = _lanes(rep["dt_bias"][l], SM_DT)
    W["alog"] = _lanes(rep["a_log"][l], SM_DT)
    W["dskip"] = _lanes(rep["d_skip"][l], SM_DT)
    W["fb"] = _lanes(rep["fox_f_b"][l], SM_F)
    return W


def small_grads(G):
    return {"ln1_g": G["ln1_g"][0], "ln1_b": G["ln1_b"][0], "ln2_g": G["ln2_g"][0], "ln2_b": G["ln2_b"][0],
            "ln3_g": G["ln3_g"][0], "ln3_b": G["ln3_b"][0], "conv_b": G["conv_b"][0], "ssd_norm_g": G["normg"][0],
            "mla_q_norm_g": G["qg"][0], "mla_kv_norm_g": G["kvg"][0], "dt_bias": G["dtb"][0, :SSD_H],
            "a_log": G["alog"][0, :SSD_H], "d_skip": G["dskip"][0, :SSD_H], "fox_f_b": G["fb"][0, SM_F:SM_F + FOX_H]}


def big_grads(G, stage):
    if stage != "mix":
        i = stage[-1]
        return {f"ffn{i}_w_{k}": G[k[0] + i].reshape(N_DEV, HS, D) for k in ("gate", "up", "down")}
    duq, dukv = mla_weight_grads(G["wq"], G["wk"], G["wv"])
    return {"w_in": G["w_in"].reshape(N_DEV, D // N_DEV, N_INP), "w_out": G["w_out"].reshape(N_DEV, D // N_DEV, D),
            "mla_w_uq": duq, "mla_w_ukv": dukv, "conv_w": _shard_cols(G["conv_w"])}


def kernel(x, meta, ffn1_w_gate, ffn1_w_up, ffn1_w_down, ln1_g, ln1_b, w_in, conv_w, conv_b, dt_bias, a_log, d_skip, ssd_norm_g, fox_f_b, mla_q_norm_g, mla_w_uq, mla_kv_norm_g, mla_w_ukv, w_out, ln2_g, ln2_b, ffn2_w_gate, ffn2_w_up, ffn2_w_down, ln3_g, ln3_b, loss_target, m_meta, m_ffn1_w_gate, m_ffn1_w_up, m_ffn1_w_down, m_ln1_g, m_ln1_b, m_w_in, m_conv_w, m_conv_b, m_dt_bias, m_a_log, m_d_skip, m_ssd_norm_g, m_fox_f_b, m_mla_q_norm_g, m_mla_w_uq, m_mla_kv_norm_g, m_mla_w_ukv, m_w_out, m_ln2_g, m_ln2_b, m_ffn2_w_gate, m_ffn2_w_up, m_ffn2_w_down, m_ln3_g, m_ln3_b, v_meta, v_ffn1_w_gate, v_ffn1_w_up, v_ffn1_w_down, v_ln1_g, v_ln1_b, v_w_in, v_conv_w, v_conv_b, v_dt_bias, v_a_log, v_d_skip, v_ssd_norm_g, v_fox_f_b, v_mla_q_norm_g, v_mla_w_uq, v_mla_kv_norm_g, v_mla_w_ukv, v_w_out, v_ln2_g, v_ln2_b, v_ffn2_w_gate, v_ffn2_w_up, v_ffn2_w_down, v_ln3_g, v_ln3_b):
    vals = (meta, ffn1_w_gate, ffn1_w_up, ffn1_w_down, ln1_g, ln1_b, w_in, conv_w, conv_b, dt_bias, a_log, d_skip, ssd_norm_g, fox_f_b, mla_q_norm_g, mla_w_uq, mla_kv_norm_g, mla_w_ukv, w_out, ln2_g, ln2_b, ffn2_w_gate, ffn2_w_up, ffn2_w_down, ln3_g, ln3_b)
    moms = (m_meta, m_ffn1_w_gate, m_ffn1_w_up, m_ffn1_w_down, m_ln1_g, m_ln1_b, m_w_in, m_conv_w, m_conv_b, m_dt_bias, m_a_log, m_d_skip, m_ssd_norm_g, m_fox_f_b, m_mla_q_norm_g, m_mla_w_uq, m_mla_kv_norm_g, m_mla_w_ukv, m_w_out, m_ln2_g, m_ln2_b, m_ffn2_w_gate, m_ffn2_w_up, m_ffn2_w_down, m_ln3_g, m_ln3_b)
    vars_ = (v_meta, v_ffn1_w_gate, v_ffn1_w_up, v_ffn1_w_down, v_ln1_g, v_ln1_b, v_w_in, v_conv_w, v_conv_b, v_dt_bias, v_a_log, v_d_skip, v_ssd_norm_g, v_fox_f_b, v_mla_q_norm_g, v_mla_w_uq, v_mla_kv_norm_g, v_mla_w_ukv, v_w_out, v_ln2_g, v_ln2_b, v_ffn2_w_gate, v_ffn2_w_up, v_ffn2_w_down, v_ln3_g, v_ln3_b)
    P = dict(zip(_NAMES, vals))
    M = dict(zip(_NAMES, moms))
    V = dict(zip(_NAMES, vars_))
    me = 4 * lax.axis_index("x") + 2 * lax.axis_index("y") + lax.axis_index("c")

    me_arr = me.astype(jnp.int32).reshape(1)
    for n in _FFN_T:
        P[n], M[n], V[n] = (jnp.swapaxes(a[n], 1, 2) for a in (P, M, V))
    src = dict(P)
    src["w_in"] = w_in_to_padded(P["w_in"])
    order = [("meta", 0)] + [(n, l) for l in range(NL) for names in _STAGES.values() for n in names]
    nfirst = 1 + len(_STAGES["ffn1"])

    def place(n, l):
        return place_own(P["meta"][None] if n == "meta" else src[n], l, F32 if n in ("meta", "conv_w") else BF16, me_arr)

    hg_first = gather_start([place(n, l) for n, l in order[:nfirst]], "gather_start_first")
    hg_rest = gather_start([place(n, l) for n, l in order[nfirst:]], "gather_start_rest")
    zone_of = {nl_: ((hg_first, i) if i < nfirst else (hg_rest, i - nfirst)) for i, nl_ in enumerate(order)}
    relays = {}

    def ahead(l, stage, after, min_layer=0):
        if not min_layer <= l < NL:
            return None
        if (l, stage) not in relays:
            zs = [zone_of[("meta", 0)]] if stage == "meta" else [zone_of[(n, l)] for n in _STAGES[stage]]
            hg, idxs = zs[0][0], [i for _, i in zs]
            relays[(l, stage)] = (hg, idxs, gather_relay(hg, idxs, f"gather_relay_{l}_{stage}", after))
        return relays[(l, stage)][2]["token"]

    def arrived(l, stage, after):
        ahead(l, stage, after)
        hg, idxs, rl = relays[(l, stage)]
        return gather_wait(hg, rl, idxs, f"gather_wait_{l}_{stage}", after)

    meta_full = _unshard_cols(arrived(0, "meta", hg_rest["token"])[0])

    def getw(l, stage, after):
        return stage_weights(l, stage, dict(zip(_STAGES[stage], arrived(l, stage, after))), P)

    sent = {}

    def emit(l, stage, G):
        bg = big_grads(G, stage)
        sent[(l, stage)] = exchange_start("scatter", [bg[n] for n in _STAGES[stage]], f"scatter_start_{l}_{stage}")
        return sent[(l, stage)]["token"]

    loss, gx, gmeta, grads = local_step(x[0], loss_target[0], meta_full, getw, emit, ahead)

    small = jnp.concatenate([pack_small({n: jnp.stack([small_grads(g)[n] for g in grads]) for n in _SMALL}), gmeta,
                             jnp.pad(loss, ((0, 7), (0, D - 1)))], axis=0)
    hs = exchange_start("gather", [place_own(small[None], 0, F32, me_arr)], "small_start")

    out = {}
    after = hs["token"]
    for stage in ("ffn2", "mix", "ffn1"):
        names = _STAGES[stage]
        whole = [l for l in range(NL - 1, -1, -1) if (l, stage) != (0, "ffn1")]
        got = {l: exchange_wait(sent[(l, stage)], list(range(len(names))), f"scatter_wait_{l}_{stage}", after) for l in whole}
        for i, n in enumerate(names):
            one = {l: (got[l][0][i], got[l][1][i]) for l in whole}
            for l in set(range(NL)) - set(whole):
                s_, r_ = exchange_wait(sent[(l, stage)], [i], f"scatter_wait_{l}_{stage}_{i}", after)
                one[l] = (s_[0], r_[0])
            own = [one[l][0] for l in range(NL)]
            recv = [one[l][1] for l in range(NL)]
            if n == "w_in":
                g = jnp.stack([w_in_from_padded(sum_slots(recv[l], own[l], me_arr)) for l in range(NL)])
                out[n] = (g,) + adamw(P[n], M[n], V[n], g=g)
            else:
                out[n] = adamw(P[n], M[n], V[n], recv=recv, own=own, me_arr=me_arr)
                if n in _FFN_T:
                    out[n] = tuple(jnp.swapaxes(a, 1, 2) for a in out[n])
            after = out[n][1]
    gsmall = sum_slots(exchange_wait(hs, [0], "small_wait", after)[1][0])
    gm = lax.dynamic_slice(gsmall[NL * _SMALL_ROWS:], (0, me * (D // N_DEV)), (N_META, D // N_DEV))
    out["meta"] = (gm,) + adamw(P["meta"], M["meta"], V["meta"], g=gm)
    gs = gsmall[:NL * _SMALL_ROWS]
    sd, sm_, sv_ = adamw(pack_small(P), pack_small(M), pack_small(V), g=gs)
    ups = [unpack_small(a, P) for a in (gs, sd, sm_, sv_)]
    for n in _SMALL:
        out[n] = tuple(u[n] for u in ups)

    loss_all = gsmall[NL * _SMALL_ROWS + N_META, 0]
    flat = [loss_all, gx[None]]
    for k in range(4):
        flat += [out[n][k] for n in _NAMES]
    return tuple(flat)
```

```python
import functools

import jax
import jax.numpy as jnp
from jax import lax
from jax.experimental import pallas as pl
from jax.experimental.pallas import tpu as pltpu

F32, BF16 = jnp.float32, jnp.bfloat16
HI = lax.Precision.HIGHEST

N_DEV = 8
D = 1024
NL = 2
N_META = 16
BLK = 128
PAD = BLK - N_META
D_FF = 2816
HS = D_FF // N_DEV
SSD_H, SSD_P, SSD_N, SSD_G = 8, 64, 64, 2
SSD_D = SSD_H * SSD_P
CONV_K = 4
CONV_D = SSD_D + 2 * SSD_G * SSD_N
FOX_H, FOX_DH = 4, 64
MLA_H, MLA_QL, MLA_KVL, MLA_NOPE, MLA_ROPE, MLA_V = 4, 256, 128, 64, 32, 64
N_IN = 2476
C_Z, C_XBC, C_FQ, C_FK, C_FV, C_CQ, C_CKV, C_SM, N_INP = 0, 512, 1280, 1536, 1792, 2048, 2304, 2432, 2560
SM_DT, SM_F, SM_KR = 0, 8, 64
ALPHA = (2 * NL) ** 0.25
EPS = 1e-5
NEG = -1e30
LR, B1, B2, AEPS, WD, STEP = 0.001, 0.9, 0.999, 1e-08, 0.01, 10
VMEM_MB = 56


def _cp(*sem):
    return pltpu.CompilerParams(dimension_semantics=sem, vmem_limit_bytes=VMEM_MB << 20)


def _nn(a, b):
    return lax.dot_general(a, b, (((1,), (0,)), ((), ())), preferred_element_type=F32)


def _nt(a, b):
    return lax.dot_general(a, b, (((1,), (1,)), ((), ())), preferred_element_type=F32)


def _tn(a, b):
    return lax.dot_general(a, b, (((0,), (0,)), ((), ())), preferred_element_type=F32)


def _nn_hi(a, b):
    return lax.dot_general(a, b, (((1,), (0,)), ((), ())), precision=HI, preferred_element_type=F32)


def _row_tile(t):
    for d in range(640, 15, -16):
        if t % d == 0:
            return d
    raise ValueError(t)


def _sig(x):
    return 1.0 / (1.0 + jnp.exp(-x))


def _tri(lower=True):
    r = lax.broadcasted_iota(jnp.int32, (BLK, BLK), 0)
    c = lax.broadcasted_iota(jnp.int32, (BLK, BLK), 1)
    return (r >= c) if lower else (r <= c)


def build_h0(meta_full, x):
    s = x.shape[0]
    nb = s // BLK + 1

    def body(m_ref, x_ref, h_ref, hb_ref):
        i = pl.program_id(0)

        @pl.when(i == 0)
        def _():
            h = jnp.concatenate([jnp.zeros((PAD, D), F32), m_ref[...]], axis=0)
            h_ref[...] = h
            hb_ref[...] = h.astype(BF16)

        @pl.when(i > 0)
        def _():
            h_ref[...] = x_ref[...]
            hb_ref[...] = x_ref[...].astype(BF16)

    return pl.pallas_call(
        body, name="build_h0", grid=(nb,),
        in_specs=[pl.BlockSpec((N_META, D), lambda i: (0, 0)),
                  pl.BlockSpec((BLK, D), lambda i: (jnp.maximum(i - 1, 0), 0))],
        out_specs=[pl.BlockSpec((BLK, D), lambda i: (i, 0))] * 2,
        out_shape=[jax.ShapeDtypeStruct((nb * BLK, D), F32), jax.ShapeDtypeStruct((nb * BLK, D), BF16)],
        compiler_params=_cp("arbitrary"),
    )(meta_full, x)


FT = 256


def _layer_norm(r, gamma, beta):
    mu = jnp.mean(r, axis=1, keepdims=True)
    xc = r - mu
    var = jnp.mean(xc * xc, axis=1, keepdims=True)
    return xc * lax.rsqrt(var + EPS) * gamma + beta


def ffn_fwd_seq(x, ln_in, wg, wu, wd, ln_out):
    t = x.shape[0]
    f = wg.shape[0]
    nj, nr = f // FT, t // _row_tile(t)
    rc = t // nr
    plain = ln_in is None
    gi, bi = ln_out if plain else ln_in

    def body(x_hbm, gi_ref, bi_ref, go_ref, bo_ref, wg_ref, wu_ref, wd_ref, u_ref, v_ref, r_hbm, yb_hbm,
             acc, hbs, xbuf, sem_in, sem_out):
        j = pl.program_id(0)

        @pl.when(j == 0)
        def _():
            def fetch(k):
                return pltpu.make_async_copy(x_hbm.at[pl.ds(k * rc, rc)], xbuf.at[k % 2], sem_in.at[k % 2])

            fetch(0).start()
            for k in range(nr):
                if k + 1 < nr:
                    fetch(k + 1).start()
                fetch(k).wait()
                h = xbuf[k % 2]
                if not plain:
                    h = _layer_norm(h, gi_ref[...], bi_ref[...])
                acc[k * rc:(k + 1) * rc, :] = ALPHA * h
                hbs[k * rc:(k + 1) * rc, :] = h.astype(BF16)

        def chunk(k, last):
            sl = slice(k * rc, (k + 1) * rc)
            h = hbs[sl, :]
            u = _nt(h, wg_ref[...])
            v = _nt(h, wu_ref[...])
            u_ref[sl, :] = u.astype(BF16)
            v_ref[sl, :] = v.astype(BF16)
            acc[sl, :] += _nn((0.5 * u * _sig(u) * v).astype(BF16), wd_ref[...])
            if not last:
                return []
            rows = pl.ds(k * rc, rc)
            cps = [pltpu.make_async_copy(acc.at[rows], r_hbm.at[rows], sem_out.at[2 * k])]
            cps[0].start()
            hbs[sl, :] = _layer_norm(acc[sl, :], go_ref[...], bo_ref[...]).astype(BF16)
            cps.append(pltpu.make_async_copy(hbs.at[rows], yb_hbm.at[rows], sem_out.at[2 * k + 1]))
            cps[1].start()
            return cps

        @pl.when(j < nj - 1)
        def _():
            for k in range(nr):
                chunk(k, False)

        @pl.when(j == nj - 1)
        def _():
            cps = []
            for k in range(nr):
                cps += chunk(k, True)
            for cp in cps:
                cp.wait()

    vec = pl.BlockSpec((1, D), lambda j: (0, 0))
    wsp = pl.BlockSpec((FT, D), lambda j: (j, 0))
    act = pl.BlockSpec((None, t, FT), lambda j: (j, 0, 0))
    return pl.pallas_call(
        body, name="ffn_fwd_seq", grid=(nj,),
        in_specs=[_ANY, vec, vec, vec, vec, wsp, wsp, wsp],
        out_specs=[act, act, _ANY, _ANY],
        out_shape=[jax.ShapeDtypeStruct((nj, t, FT), BF16), jax.ShapeDtypeStruct((nj, t, FT), BF16),
                   jax.ShapeDtypeStruct((t, D), F32), jax.ShapeDtypeStruct((t, D), BF16)],
        scratch_shapes=[pltpu.VMEM((t, D), F32), pltpu.VMEM((t, D), BF16), pltpu.VMEM((2, rc, D), F32),
                        pltpu.SemaphoreType.DMA((2,)), pltpu.SemaphoreType.DMA((2 * nr,))],
        compiler_params=_cp("arbitrary"),
    )(x, gi, bi, ln_out[0], ln_out[1], wg, wu, wd)


def ffn_bwd_seq(parts, r, gamma, hb, u, v, wg, wu, wd, after=None):
    nj, t, _ = u.shape
    f = nj * FT
    nr = 2 * (t // _row_tile(t))
    rc = t // nr
    nc = t // BLK
    scales = [s for _, s in parts]
    npart = len(parts)
    extra = [] if after is None else [after]

    def body(*refs):
        refs = refs[len(extra):]
        p_hbm, refs = refs[:npart], refs[npart:]
        (r_hbm, g_ref, hb_hbm, u_ref, v_ref, wg_ref, wu_ref, wd_ref, dh_hbm, dwg_ref, dwu_ref, dwd_ref, dg_ref, db_ref,
         dfs, hbt, dft, dhacc, dus, dvs, acs, pbuf, rbuf, hbuf, sems, sem_out) = refs
        j = pl.program_id(0)

        @pl.when(j == 0)
        def _():
            def fetch(c):
                rows = pl.ds(c * BLK, BLK)
                cps = [pltpu.make_async_copy(p_hbm[p].at[rows], pbuf.at[c % 2, p], sems.at[c % 2, p]) for p in range(npart)]
                cps.append(pltpu.make_async_copy(r_hbm.at[rows], rbuf.at[c % 2], sems.at[c % 2, npart]))
                cps.append(pltpu.make_async_copy(hb_hbm.at[rows], hbuf.at[c % 2], sems.at[c % 2, npart + 1]))
                return cps

            for cp in fetch(0):
                cp.start()
            dg = jnp.zeros((1, D), F32)
            db = jnp.zeros((1, D), F32)
            for c in range(nc):
                if c + 1 < nc:
                    for cp in fetch(c + 1):
                        cp.start()
                for cp in fetch(c):
                    cp.wait()
                sl = slice(c * BLK, (c + 1) * BLK)
                dy = scales[0] * pbuf[c % 2, 0]
                for p in range(1, npart):
                    dy += scales[p] * pbuf[c % 2, p]
                rr = rbuf[c % 2]
                xc = rr - jnp.mean(rr, axis=1, keepdims=True)
                rstd = lax.rsqrt(jnp.mean(xc * xc, axis=1, keepdims=True) + EPS)
                xh = xc * rstd
                dxh = dy * g_ref[...]
                dr = rstd * (dxh - jnp.mean(dxh, axis=1, keepdims=True) - xh * jnp.mean(dxh * xh, axis=1, keepdims=True))
                dg += jnp.sum(dy * xh, axis=0, keepdims=True)
                db += jnp.sum(dy, axis=0, keepdims=True)
                dhacc[sl, :] = ALPHA * dr
                dfc = (0.5 * dr).astype(BF16)
                dfs[sl, :] = dfc
                dft[:, sl] = dfc.T
                hbt[:, sl] = hbuf[c % 2].T
            dg_ref[...] = dg
            db_ref[...] = db

        for k in range(nr):
            sl = slice(k * rc, (k + 1) * rc)
            da = _nt(dfs[sl, :], wd_ref[...])
            uu = u_ref[sl, :].astype(F32)
            vv = v_ref[sl, :].astype(F32)
            sg = _sig(uu)
            du = (da * vv * (sg * (1.0 + uu * (1.0 - sg)))).astype(BF16)
            dv = (da * uu * sg).astype(BF16)
            dus[sl, :] = du
            dvs[sl, :] = dv
            acs[sl, :] = (uu * sg * vv).astype(BF16)
            dhacc[sl, :] += _nn(du, wg_ref[...]) + _nn(dv, wu_ref[...])
        @pl.when(j == nj - 1)
        def _():
            pltpu.make_async_copy(dhacc, dh_hbm, sem_out.at[0]).start()

        dwg_ref[...] = _nn(hbt[...], dus[...]).astype(BF16).T
        dwu_ref[...] = _nn(hbt[...], dvs[...]).astype(BF16).T
        dwd_ref[...] = _nn(dft[...], acs[...]).astype(BF16).T

        @pl.when(j == nj - 1)
        def _():
            pltpu.make_async_copy(dhacc, dh_hbm, sem_out.at[0]).wait()

    vec = pl.BlockSpec((1, D), lambda j: (0, 0))
    wsp = pl.BlockSpec((FT, D), lambda j: (j, 0))
    act = pl.BlockSpec((None, t, FT), lambda j: (j, 0, 0))
    return pl.pallas_call(
        body, name="ffn_bwd_seq", grid=(nj,),
        in_specs=[_ANY] * (len(extra) + npart + 1) + [vec, _ANY, act, act, wsp, wsp, wsp],
        out_specs=[_ANY, wsp, wsp, wsp, vec, vec],
        out_shape=[jax.ShapeDtypeStruct((t, D), F32)] + [jax.ShapeDtypeStruct((f, D), BF16)] * 3
        + [jax.ShapeDtypeStruct((1, D), F32)] * 2,
        scratch_shapes=[pltpu.VMEM((t, D), BF16), pltpu.VMEM((D, t), BF16), pltpu.VMEM((D, t), BF16),
                        pltpu.VMEM((t, D), F32), pltpu.VMEM((t, FT), BF16), pltpu.VMEM((t, FT), BF16),
                        pltpu.VMEM((t, FT), BF16), pltpu.VMEM((2, npart, BLK, D), F32), pltpu.VMEM((2, BLK, D), F32),
                        pltpu.VMEM((2, BLK, D), BF16), pltpu.SemaphoreType.DMA((2, npart + 2)),
                        pltpu.SemaphoreType.DMA((1,))],
        compiler_params=_cp("arbitrary"),
    )(*extra, *[p for p, _ in parts], r, gamma, hb, u, v, wg, wu, wd)


def mm_res_ln(pieces, b, x, ln_in, ln_out):
    t = x.shape[0]
    k = b.shape[0]
    tm = _row_tile(t)
    na = len(pieces)

    def body(*refs):
        b_ref, x_ref, gi_ref, bi_ref, go_ref, bo_ref, r_ref, yb_ref = refs[na:]
        a = jnp.concatenate([ref[...] for ref in refs[:na]], axis=1)
        r = ALPHA * _layer_norm(x_ref[...], gi_ref[...], bi_ref[...]) + _nn(a, b_ref[...])
        r_ref[...] = r
        yb_ref[...] = _layer_norm(r, go_ref[...], bo_ref[...]).astype(BF16)

    row = pl.BlockSpec((tm, D), lambda i: (i, 0))
    vec = pl.BlockSpec((1, D), lambda i: (0, 0))
    return pl.pallas_call(
        body, name="mm_res_ln", grid=(t // tm,),
        in_specs=[pl.BlockSpec((tm, p.shape[1]), lambda i: (i, 0)) for p in pieces]
        + [pl.BlockSpec((k, D), lambda i: (0, 0)), row, vec, vec, vec, vec],
        out_specs=[row, row],
        out_shape=[jax.ShapeDtypeStruct((t, D), F32), jax.ShapeDtypeStruct((t, D), BF16)],
        compiler_params=_cp("arbitrary"),
    )(*pieces, b, x, ln_in[0], ln_in[1], ln_out[0], ln_out[1])


def mm_nn(a, b):
    t, k = a.shape
    n = tn = b.shape[1]
    tm = _row_tile(t)

    def body(a_ref, b_ref, o_ref):
        o_ref[...] = _nn(a_ref[...], b_ref[...])

    return pl.pallas_call(
        body, name="mm_nn", grid=(t // tm, n // tn),
        in_specs=[pl.BlockSpec((tm, k), lambda i, j: (i, 0)), pl.BlockSpec((k, tn), lambda i, j: (0, j))],
        out_specs=pl.BlockSpec((tm, tn), lambda i, j: (i, j)),
        out_shape=jax.ShapeDtypeStruct((t, n), F32),
        compiler_params=_cp("arbitrary", "arbitrary"),
    )(a, b)


def oproj_bwd(dy, r, gamma, pieces, w_out, after=None):
    t = r.shape[0]
    tm = _row_tile(t)
    nt = t // tm
    extra = [] if after is None else [after]
    na = len(pieces)

    def body(*refs):
        refs = refs[len(extra):]
        dy_ref, r_ref, g_ref = refs[:3]
        w_ref, dr_ref, dm_ref, dw_ref, dg_ref, db_ref, acc = refs[3 + na:]
        mix = jnp.concatenate([ref[...] for ref in refs[3:3 + na]], axis=1)
        i = pl.program_id(0)
        dy = dy_ref[...]
        rr = r_ref[...]
        xc = rr - jnp.mean(rr, axis=1, keepdims=True)
        rstd = lax.rsqrt(jnp.mean(xc * xc, axis=1, keepdims=True) + EPS)
        xh = xc * rstd
        dxh = dy * g_ref[...]
        dr = rstd * (dxh - jnp.mean(dxh, axis=1, keepdims=True) - xh * jnp.mean(dxh * xh, axis=1, keepdims=True))
        dr_ref[...] = dr
        drb = dr.astype(BF16)
        dm_ref[...] = _nt(drb, w_ref[...])
        dw = _tn(mix, drb)
        dg = jnp.sum(dy * xh, axis=0, keepdims=True)
        db = jnp.sum(dy, axis=0, keepdims=True)

        @pl.when(i == 0)
        def _():
            acc[...] = dw
            dg_ref[...] = dg
            db_ref[...] = db

        @pl.when(i > 0)
        def _():
            acc[...] += dw
            dg_ref[...] += dg
            db_ref[...] += db

        @pl.when(i == nt - 1)
        def _():
            dw_ref[...] = acc[...].astype(BF16)

    row = pl.BlockSpec((tm, D), lambda i: (i, 0))
    vec = pl.BlockSpec((1, D), lambda i: (0, 0))
    mat = pl.BlockSpec((D, D), lambda i: (0, 0))
    return pl.pallas_call(
        body, name="oproj_bwd", grid=(nt,),
        in_specs=[_ANY] * len(extra) + [row, row, vec] + [pl.BlockSpec((tm, p.shape[1]), lambda i: (i, 0)) for p in pieces]
        + [mat],
        out_specs=[row, row, mat, vec, vec],
        out_shape=[jax.ShapeDtypeStruct((t, D), F32), jax.ShapeDtypeStruct((t, D), F32), jax.ShapeDtypeStruct((D, D), BF16),
                   jax.ShapeDtypeStruct((1, D), F32), jax.ShapeDtypeStruct((1, D), F32)],
        scratch_shapes=[pltpu.VMEM((D, D), F32)],
        compiler_params=_cp("arbitrary"),
    )(*extra, dy, r, gamma, *pieces, w_out)


def proj_bwd(pieces, hb, w_in):
    t = hb.shape[0]
    n = w_in.shape[1]
    tm = _row_tile(t)
    nt = t // tm
    widths = [p.shape[1] for p in pieces]
    starts = [sum(widths[:k]) for k in range(len(pieces))]
    assert sum(widths) == n

    def body(*refs):
        p_refs = refs[:len(pieces)]
        h_ref, w_ref, dh_ref, dw_ref, acc = refs[len(pieces):]
        i = pl.program_id(0)
        h = h_ref[...]
        dh = jnp.zeros((tm, D), F32)
        dws = []
        for p_ref, c0, w in zip(p_refs, starts, widths):
            pb = p_ref[...].astype(BF16)
            dh += _nt(pb, w_ref[:, c0:c0 + w])
            dws.append(_tn(h, pb))
        dh_ref[...] = dh

        @pl.when(i == 0)
        def _():
            for dw, c0, w in zip(dws, starts, widths):
                acc[:, c0:c0 + w] = dw

        @pl.when(i > 0)
        def _():
            for dw, c0, w in zip(dws, starts, widths):
                acc[:, c0:c0 + w] += dw

        @pl.when(i == nt - 1)
        def _():
            dw_ref[...] = acc[...].astype(BF16)

    mat = pl.BlockSpec((D, n), lambda i: (0, 0))
    return pl.pallas_call(
        body, name="proj_bwd", grid=(nt,),
        in_specs=[pl.BlockSpec((tm, w), lambda i: (i, 0)) for w in widths] + [pl.BlockSpec((tm, D), lambda i: (i, 0)), mat],
        out_specs=[pl.BlockSpec((tm, D), lambda i: (i, 0)), mat],
        out_shape=[jax.ShapeDtypeStruct((t, D), F32), jax.ShapeDtypeStruct((D, n), BF16)],
        scratch_shapes=[pltpu.VMEM((D, n), F32)],
        compiler_params=_cp("arbitrary"),
    )(*pieces, hb, w_in)


def loss_head(r, ln, target):
    t = r.shape[0]
    nb = t // BLK

    def body(r_ref, g_ref, b_ref, t_ref, dy_ref, l_ref):
        i = pl.program_id(0)

        @pl.when(i == 0)
        def _():
            dy_ref[...] = jnp.zeros_like(dy_ref)
            l_ref[...] = jnp.zeros_like(l_ref)

        @pl.when(i > 0)
        def _():
            err = _layer_norm(r_ref[...], g_ref[...], b_ref[...]) - t_ref[...]
            dy_ref[...] = err * (1.0 / D)
            l_ref[...] += (0.5 / D) * jnp.sum(err * err, keepdims=True)

    vec = pl.BlockSpec((1, D), lambda i: (0, 0))
    return pl.pallas_call(
        body, name="loss_head", grid=(nb,),
        in_specs=[pl.BlockSpec((BLK, D), lambda i: (i, 0)), vec, vec,
                  pl.BlockSpec((BLK, D), lambda i: (jnp.maximum(i - 1, 0), 0))],
        out_specs=[pl.BlockSpec((BLK, D), lambda i: (i, 0)), pl.BlockSpec((1, 1), lambda i: (0, 0))],
        out_shape=[jax.ShapeDtypeStruct((t, D), F32), jax.ShapeDtypeStruct((1, 1), F32)],
        compiler_params=_cp("arbitrary"),
    )(r, ln[0], ln[1], target)


def split_dh0(dh0, after=None):
    t = dh0.shape[0]
    nb = t // BLK
    extra = [] if after is None else [after]

    def body(*refs):
        a_ref, gx_ref, gm_ref = refs[len(extra):]
        i = pl.program_id(0)
        tot = a_ref[...]

        @pl.when(i == 0)
        def _():
            gm_ref[...] = tot[PAD:, :]

        @pl.when(i > 0)
        def _():
            gx_ref[...] = tot

    blk = pl.BlockSpec((BLK, D), lambda i: (i, 0))
    return pl.pallas_call(
        body, name="split_dh0", grid=(nb,),
        in_specs=[_ANY] * len(extra) + [blk],
        out_specs=[pl.BlockSpec((BLK, D), lambda i: (jnp.maximum(i - 1, 0), 0)),
                   pl.BlockSpec((N_META, D), lambda i: (0, 0))],
        out_shape=[jax.ShapeDtypeStruct((t - BLK, D), F32), jax.ShapeDtypeStruct((N_META, D), F32)],
        compiler_params=_cp("arbitrary"),
    )(*extra, dh0)


def _valid_rows(nrows, first_row):
    return (first_row + lax.broadcasted_iota(jnp.int32, (nrows, 1), 0)) >= PAD


def conv_fwd(proj, conv_w, conv_b):
    t = proj.shape[0]
    c0 = C_XBC // BLK

    def body(x_ref, w_ref, b_ref, o_ref):
        ok = _valid_rows(t, 0)
        x = jnp.where(ok, x_ref[...], 0.0)
        w = w_ref[...]
        acc = b_ref[...] + w[CONV_K - 1:CONV_K, :] * x
        for s in range(1, CONV_K):
            acc += w[CONV_K - 1 - s:CONV_K - s, :] * pltpu.roll(x, s, 0)
        o_ref[...] = jnp.where(ok, acc * _sig(acc), 0.0)

    return pl.pallas_call(
        body, name="conv_fwd", grid=(CONV_D // BLK,),
        in_specs=[pl.BlockSpec((t, BLK), lambda j: (0, c0 + j)),
                  pl.BlockSpec((CONV_K, BLK), lambda j: (0, j)), pl.BlockSpec((1, BLK), lambda j: (0, j))],
        out_specs=pl.BlockSpec((t, BLK), lambda j: (0, j)),
        out_shape=jax.ShapeDtypeStruct((t, CONV_D), F32),
        compiler_params=_cp("arbitrary"),
    )(proj, conv_w, conv_b)


def conv_bwd(dxa, proj, conv_w, conv_b):
    t = proj.shape[0]
    c0 = C_XBC // BLK

    def body(d_ref, x_ref, w_ref, b_ref, dx_ref, dw_ref, db_ref):
        ok = _valid_rows(t, 0)
        x = jnp.where(ok, x_ref[...], 0.0)
        w = w_ref[...]
        xs = [x] + [pltpu.roll(x, s, 0) for s in range(1, CONV_K)]
        acc = b_ref[...] + w[CONV_K - 1:CONV_K, :] * x
        for s in range(1, CONV_K):
            acc += w[CONV_K - 1 - s:CONV_K - s, :] * xs[s]
        sg = _sig(acc)
        dxc = jnp.where(ok, d_ref[...] * (sg * (1.0 + acc * (1.0 - sg))), 0.0)
        db_ref[...] = jnp.sum(dxc, axis=0, keepdims=True)
        dw_ref[...] = jnp.concatenate(
            [jnp.sum(dxc * xs[CONV_K - 1 - k], axis=0, keepdims=True) for k in range(CONV_K)], axis=0)
        dx = w[CONV_K - 1:CONV_K, :] * dxc
        for s in range(1, CONV_K):
            dx += w[CONV_K - 1 - s:CONV_K - s, :] * pltpu.roll(dxc, t - s, 0)
        dx_ref[...] = jnp.where(ok, dx, 0.0)

    col = pl.BlockSpec((t, BLK), lambda j: (0, j))
    return pl.pallas_call(
        body, name="conv_bwd", grid=(CONV_D // BLK,),
        in_specs=[col, pl.BlockSpec((t, BLK), lambda j: (0, c0 + j)),
                  pl.BlockSpec((CONV_K, BLK), lambda j: (0, j)), pl.BlockSpec((1, BLK), lambda j: (0, j))],
        out_specs=[col, pl.BlockSpec((CONV_K, BLK), lambda j: (0, j)), pl.BlockSpec((1, BLK), lambda j: (0, j))],
        out_shape=[jax.ShapeDtypeStruct((t, CONV_D), F32), jax.ShapeDtypeStruct((CONV_K, CONV_D), F32),
                   jax.ShapeDtypeStruct((1, CONV_D), F32)],
        compiler_params=_cp("arbitrary"),
    )(dxa, proj, conv_w, conv_b)


def _softplus(x):
    return jnp.maximum(x, 0.0) + jnp.log(1.0 + jnp.exp(-jnp.abs(x)))


GW = SSD_D // SSD_G
HPG = SSD_H // SSD_G


def _head_expand():
    r = lax.broadcasted_iota(jnp.int32, (BLK, SSD_D), 0)
    c = lax.broadcasted_iota(jnp.int32, (BLK, SSD_D), 1)
    rt = lax.broadcasted_iota(jnp.int32, (SSD_D, BLK), 0)
    ct = lax.broadcasted_iota(jnp.int32, (SSD_D, BLK), 1)
    return (c // SSD_P == r).astype(F32), (rt // SSD_P == ct).astype(F32)


def _ssd_chunk(xa, sm, dtb, alog, dskip, ok, sp):
    e, et = _head_expand()
    dt = jnp.where(ok, _softplus(sm + dtb), 0.0)
    amat = -jnp.exp(alog)
    tri = _tri()
    ac = _nn_hi(tri.astype(F32), dt * amat)
    act = ac.T
    ace, dte, dse = _nn_hi(ac, e), _nn_hi(dt, e), _nn_hi(dskip, e)
    laste = ace[BLK - 1:BLK, :]
    ee, dece, gle = jnp.exp(ace), jnp.exp(laste - ace), jnp.exp(laste)
    xs = xa[:, :SSD_D]
    xdt = xs * dte
    decx = dece * xdt
    xdtb = xdt.astype(BF16)
    d = dict(e=e, et=et, dt=dt, amat=amat, tri=tri, ac=ac, act=act, dte=dte, dse=dse, ee=ee, dece=dece, gle=gle, xs=xs,
             xdt=xdt, xdtb=xdtb, decx=decx, bg=[], cg=[], cb=[], yo=[], seg=[], m=[], new_s=[])
    ys = []
    for g in range(SSD_G):
        cols = slice(GW * g, GW * (g + 1))
        bg = xa[:, SSD_D + SSD_N * g:SSD_D + SSD_N * (g + 1)].astype(BF16)
        cg = xa[:, SSD_D + SSD_G * SSD_N + SSD_N * g:SSD_D + SSD_G * SSD_N + SSD_N * (g + 1)].astype(BF16)
        spg = sp[:, cols]
        sloc = _tn(bg, decx[:, cols].astype(BF16))
        yo = _nn(cg, spg.astype(BF16)) * ee[:, cols]
        cb = _nt(cg, bg)
        d["new_s"].append(gle[:, cols] * spg + sloc)
        yds = []
        for h in range(HPG * g, HPG * (g + 1)):
            seg = jnp.where(tri, jnp.exp(jnp.minimum(ac[:, h:h + 1] - act[h:h + 1, :], 0.0)), 0.0)
            m = cb * seg
            yds.append(_nn(m.astype(BF16), xdtb[:, SSD_P * h:SSD_P * (h + 1)]))
            d["seg"].append(seg)
            d["m"].append(m)
        ys.append(jnp.concatenate(yds, axis=1) + yo)
        for k, val in (("bg", bg), ("cg", cg), ("cb", cb), ("yo", yo)):
            d[k].append(val)
    d["y"] = jnp.concatenate(ys, axis=1) + dse * xs
    return d


def ssd_fwd(xa, proj, dtb, alog, dskip, normg):
    t = xa.shape[0]
    nb = t // BLK
    gw = SSD_D // SSD_G

    def body(xa_ref, z_ref, sm_ref, dtb_ref, al_ref, ds_ref, ng_ref, y_ref, sp_ref, st):
        c = pl.program_id(0)

        @pl.when(c == 0)
        def _():
            st[...] = jnp.zeros_like(st)

        ok = _valid_rows(BLK, c * BLK)
        sp = st[...]
        sp_ref[...] = sp
        d = _ssd_chunk(xa_ref[...], sm_ref[...], dtb_ref[...], al_ref[...], ds_ref[...], ok, sp)
        st[...] = jnp.concatenate(d["new_s"], axis=1)
        y = d["y"]
        z = z_ref[...]
        yg = y * (z * _sig(z))
        outs = []
        for g in range(SSD_G):
            v = yg[:, gw * g:gw * (g + 1)]
            outs.append(v * lax.rsqrt(jnp.mean(v * v, axis=1, keepdims=True) + EPS))
        y_ref[...] = (jnp.concatenate(outs, axis=1) * ng_ref[...]).astype(BF16)

    vec = pl.BlockSpec((1, BLK), lambda c: (0, 0))
    return pl.pallas_call(
        body, name="ssd_fwd", grid=(nb,),
        in_specs=[pl.BlockSpec((BLK, CONV_D), lambda c: (c, 0)),
                  pl.BlockSpec((BLK, SSD_D), lambda c: (c, C_Z // SSD_D)),
                  pl.BlockSpec((BLK, BLK), lambda c: (c, C_SM // BLK)),
                  vec, vec, vec, pl.BlockSpec((1, SSD_D), lambda c: (0, 0))],
        out_specs=[pl.BlockSpec((BLK, SSD_D), lambda c: (c, 0)),
                   pl.BlockSpec((None, SSD_N, SSD_D), lambda c: (c, 0, 0))],
        out_shape=[jax.ShapeDtypeStruct((t, SSD_D), BF16), jax.ShapeDtypeStruct((nb, SSD_N, SSD_D), F32)],
        scratch_shapes=[pltpu.VMEM((SSD_N, SSD_D), F32)],
        compiler_params=_cp("arbitrary"),
    )(xa, proj, proj, dtb, alog, dskip, normg)


def _lane_put(col, lane):
    li = lax.broadcasted_iota(jnp.int32, (col.shape[0], BLK), 1)
    return jnp.where(li == lane, col, 0.0)


def ssd_bwd(dmix, xa, proj, sprev, dtb, alog, dskip, normg):
    t = xa.shape[0]
    nb = t // BLK
    gw = SSD_D // SSD_G
    rev = lambda c: nb - 1 - c

    def body(dy_ref, xa_ref, z_ref, sm_ref, sp_ref, dtb_ref, al_ref, ds_ref, ng_ref,
             dxa_ref, dz_ref, dsm_ref, dng_ref, dds_ref, dal_ref, ddtb_ref, dst):
        c = pl.program_id(0)

        @pl.when(c == 0)
        def _():
            dst[...] = jnp.zeros_like(dst)
            dng_ref[...] = jnp.zeros_like(dng_ref)
            dds_ref[...] = jnp.zeros_like(dds_ref)
            dal_ref[...] = jnp.zeros_like(dal_ref)
            ddtb_ref[...] = jnp.zeros_like(ddtb_ref)

        ok = _valid_rows(BLK, rev(c) * BLK)
        sm = sm_ref[...]
        sp = sp_ref[...]
        d = _ssd_chunk(xa_ref[...], sm, dtb_ref[...], al_ref[...], ds_ref[...], ok, sp)
        dt, amat, ac, act, tri, et, xs, xdt = (d[k] for k in ("dt", "amat", "ac", "act", "tri", "et", "xs", "xdt"))
        rowi = lax.broadcasted_iota(jnp.int32, (BLK, 1), 0)
        y = d["y"]
        z = z_ref[...]
        sgz = _sig(z)
        siluz = z * sgz
        yg = y * siluz
        dout = dy_ref[...]
        ng = ng_ref[...]
        dygs, xhs = [], []
        for g in range(SSD_G):
            v = yg[:, gw * g:gw * (g + 1)]
            rr = lax.rsqrt(jnp.mean(v * v, axis=1, keepdims=True) + EPS)
            xh = v * rr
            dxh = dout[:, gw * g:gw * (g + 1)] * ng[:, gw * g:gw * (g + 1)]
            dygs.append(rr * (dxh - xh * jnp.mean(dxh * xh, axis=1, keepdims=True)))
            xhs.append(xh)
        dyg = jnp.concatenate(dygs, axis=1)
        dng_ref[...] += jnp.sum(dout * jnp.concatenate(xhs, axis=1), axis=0, keepdims=True)
        dy = dyg * siluz
        dz_ref[...] = dyg * y * (sgz * (1.0 + z * (1.0 - sgz)))

        triu = _tri(lower=False)
        dyb = dy.astype(BF16)
        dsn = dst[...]
        dds_ref[...] += _nn_hi(jnp.sum(dy * xs, axis=0, keepdims=True), et)
        dac_all = _nn_hi(dy * jnp.concatenate(d["yo"], axis=1), et)
        dyo = (dy * d["ee"]).astype(BF16)
        gl = jnp.exp(ac[BLK - 1:BLK, :])
        dlast = _nn_hi(jnp.sum(dsn * sp, axis=0, keepdims=True), et) * gl
        bds, db_g, dc_g, dxdt_i, new_dst = [], [], [], [], []
        for g in range(SSD_G):
            cols = slice(GW * g, GW * (g + 1))
            bg, cg = d["bg"][g], d["cg"][g]
            dsng = dsn[:, cols].astype(BF16)
            dc = _nt(dyo[:, cols], sp[:, cols].astype(BF16))
            new_dst.append(_tn(cg, dyo[:, cols]) + d["gle"][:, cols] * dsn[:, cols])
            bds.append(_nn(bg, dsng))
            db = _nt(d["decx"][:, cols].astype(BF16), dsng)
            cbt = _nt(bg, cg)
            dcb = jnp.zeros((BLK, BLK), F32)
            for h in range(HPG * g, HPG * (g + 1)):
                hc = slice(SSD_P * h, SSD_P * (h + 1))
                dm = _nt(dyb[:, hc], d["xdtb"][:, hc])
                dcb += dm * d["seg"][h]
                w = dm * d["m"][h]
                dac_all += _lane_put(jnp.sum(w, axis=1, keepdims=True) - jnp.sum(w.T, axis=1, keepdims=True), h)
                segt = jnp.where(triu, jnp.exp(jnp.minimum(act[h:h + 1, :] - ac[:, h:h + 1], 0.0)), 0.0)
                dxdt_i.append(_nn((cbt * segt).astype(BF16), dyb[:, hc]))
            dcbb = dcb.astype(BF16)
            dc_g.append(dc + _nn(dcbb, bg))
            db_g.append(db + _tn(dcbb, cg))
        dst[...] = jnp.concatenate(new_dst, axis=1)
        bds = jnp.concatenate(bds, axis=1)
        tdec = jnp.exp(ac[BLK - 1:BLK, :] - ac) * _nn_hi(xdt * bds, et)
        dlast += jnp.sum(tdec, axis=0, keepdims=True)
        dac_all += jnp.where(rowi == BLK - 1, dlast, 0.0) - tdec
        dxdt = d["dece"] * bds + jnp.concatenate(dxdt_i, axis=1)
        da = _nn_hi(triu.astype(F32), dac_all)
        ddt = _nn_hi(dxdt * xs, et) + da * amat
        dal_ref[...] += jnp.sum(da * dt, axis=0, keepdims=True) * amat
        ddtr = jnp.where(ok, ddt * _sig(sm + dtb_ref[...]), 0.0)
        ddtb_ref[...] += jnp.sum(ddtr, axis=0, keepdims=True)
        dsm_ref[...] = ddtr
        dxs = d["dse"] * dy + dxdt * d["dte"]
        dxa_ref[...] = jnp.where(ok, jnp.concatenate([dxs] + db_g + dc_g, axis=1), 0.0)

    vec = pl.BlockSpec((1, BLK), lambda c: (0, 0))
    nvec = pl.BlockSpec((1, SSD_D), lambda c: (0, 0))
    return pl.pallas_call(
        body, name="ssd_bwd", grid=(nb,),
        in_specs=[pl.BlockSpec((BLK, SSD_D), lambda c: (rev(c), 0)),
                  pl.BlockSpec((BLK, CONV_D), lambda c: (rev(c), 0)),
                  pl.BlockSpec((BLK, SSD_D), lambda c: (rev(c), C_Z // SSD_D)),
                  pl.BlockSpec((BLK, BLK), lambda c: (rev(c), C_SM // BLK)),
                  pl.BlockSpec((None, SSD_N, SSD_D), lambda c: (rev(c), 0, 0)),
                  vec, vec, vec, nvec],
        out_specs=[pl.BlockSpec((BLK, CONV_D), lambda c: (rev(c), 0)),
                   pl.BlockSpec((BLK, SSD_D), lambda c: (rev(c), 0)),
                   pl.BlockSpec((BLK, BLK), lambda c: (rev(c), 0)),
                   nvec, vec, vec, vec],
        out_shape=[jax.ShapeDtypeStruct((t, CONV_D), F32), jax.ShapeDtypeStruct((t, SSD_D), F32),
                   jax.ShapeDtypeStruct((t, BLK), F32), jax.ShapeDtypeStruct((1, SSD_D), F32),
                   jax.ShapeDtypeStruct((1, BLK), F32), jax.ShapeDtypeStruct((1, BLK), F32),
                   jax.ShapeDtypeStruct((1, BLK), F32)],
        scratch_shapes=[pltpu.VMEM((SSD_N, SSD_D), F32)],
        compiler_params=_cp("arbitrary"),
    )(dmix, xa, proj, proj, sprev, dtb, alog, dskip, normg)


def _segments(nb, fine):
    if fine:
        cuts = list(range(0, nb, 2)) + [nb]
    else:
        cuts = sorted({0, nb} | {max(1, round(nb * f)) for f in (0.3, 0.53, 0.77)})
    return list(zip(cuts[:-1], cuts[1:]))


def attn_fwd(q, k, v, qcol, kcol, vcol, nh, dq, dv, scale, c_col=None, c_row=None, lane0=0):
    t = q.shape[0]
    tq = BLK
    use_bias = c_col is not None

    def body(*refs):
        if use_bias:
            q_ref, k_ref, v_ref, cc_ref, cr_ref, o_ref, l_ref = refs
        else:
            q_ref, k_ref, v_ref, o_ref, l_ref = refs
        i = pl.program_id(0)
        rowg = i * tq + lax.broadcasted_iota(jnp.int32, (tq, 1), 0)

        def tile(tk):
            col = lax.broadcasted_iota(jnp.int32, (1, tk), 1)
            mask = (col <= rowg) & (col >= PAD)
            outs = []
            lse = jnp.zeros((tq, BLK), F32)
            for h in range(nh):
                s = _nt(q_ref[:, dq * h:dq * (h + 1)].astype(BF16), k_ref[0:tk, dq * h:dq * (h + 1)].astype(BF16)) * scale
                if use_bias:
                    s = s + (cc_ref[:, lane0 + h:lane0 + h + 1] - cr_ref[h:h + 1, 0:tk])
                s = jnp.where(mask, s, NEG)
                m = jnp.max(s, axis=1, keepdims=True)
                p = jnp.exp(s - m)
                l = jnp.sum(p, axis=1, keepdims=True)
                outs.append(_nn(p.astype(BF16), v_ref[0:tk, dv * h:dv * (h + 1)].astype(BF16)) / l)
                lse += _lane_put(m + jnp.log(l), h)
            o_ref[...] = jnp.concatenate(outs, axis=1).astype(BF16)
            l_ref[...] = lse.T[0:8, :]

        for t0, t1 in _segments(t // tq, True):
            pl.when((i >= t0) & (i < t1))(functools.partial(tile, t1 * BLK))

    in_specs = [pl.BlockSpec((tq, nh * dq), lambda i: (i, qcol)),
                pl.BlockSpec((t, nh * dq), lambda i: (0, kcol)),
                pl.BlockSpec((t, nh * dv), lambda i: (0, vcol))]
    args = [q, k, v]
    if use_bias:
        in_specs += [pl.BlockSpec((tq, BLK), lambda i: (i, 0)), pl.BlockSpec((8, t), lambda i: (0, 0))]
        args += [c_col, c_row]
    return pl.pallas_call(
        body, name="attn_fwd", grid=(t // tq,),
        in_specs=in_specs,
        out_specs=[pl.BlockSpec((tq, nh * dv), lambda i: (i, 0)), pl.BlockSpec((8, tq), lambda i: (0, i))],
        out_shape=[jax.ShapeDtypeStruct((t, nh * dv), BF16), jax.ShapeDtypeStruct((8, t), F32)],
        compiler_params=_cp("arbitrary"),
    )(*args)


def attn_bwd(q, k, v, do, lse_row, o, qcol, kcol, vcol, docol, ocol, nh, dq, dv, scale, c_col=None, c_row=None, lane0=0):
    t = q.shape[0]
    tq = BLK
    use_bias = c_col is not None
    nq = t // tq

    def body(*refs):
        if use_bias:
            (q_ref, k_ref, v_ref, do_ref, l_ref, o_ref, cc_ref, cr_ref, dq_ref, dk_ref, dv_ref, dcq_ref, dck_ref,
             kt, ckb, dacc) = refs
        else:
            q_ref, k_ref, v_ref, do_ref, l_ref, o_ref, dq_ref, dk_ref, dv_ref, kt = refs
        i = pl.program_id(0)

        @pl.when(i == 0)
        def _():
            kt[...] = k_ref[...].astype(BF16).T
            dk_ref[...] = jnp.zeros_like(dk_ref)
            dv_ref[...] = jnp.zeros_like(dv_ref)
            if use_bias:
                dacc[...] = jnp.zeros_like(dacc)
                for h in range(nh):
                    ckb[h] = jnp.broadcast_to(cc_ref[:, lane0 + h:lane0 + h + 1], (t, BLK))

        qry = i * tq + lax.broadcasted_iota(jnp.int32, (1, tq), 1)
        dot = (do_ref[...].astype(F32) * o_ref[...].astype(F32)).T

        def tile(tk):
            key = lax.broadcasted_iota(jnp.int32, (tk, 1), 0)
            mask = (key <= qry) & (key >= PAD)
            dqts, dcqs = [], []
            for h in range(nh):
                qh = q_ref[:, dq * h:dq * (h + 1)].astype(BF16)
                kh = k_ref[0:tk, dq * h:dq * (h + 1)].astype(BF16)
                vh = v_ref[0:tk, dv * h:dv * (h + 1)].astype(BF16)
                doh = do_ref[:, dv * h:dv * (h + 1)].astype(BF16)
                delta = jnp.sum(dot[dv * h:dv * (h + 1), :], axis=0, keepdims=True)
                st = _nt(kh, qh) * scale
                if use_bias:
                    st = st + (cr_ref[h:h + 1, :] - ckb[h, 0:tk, :])
                pt = jnp.exp(jnp.where(mask, st, NEG) - l_ref[h:h + 1, :])
                dst = pt * (_nt(vh, doh) - delta)
                dsb = dst.astype(BF16)
                dk_ref[0:tk, dq * h:dq * (h + 1)] += _nn(dsb, qh) * scale
                dv_ref[0:tk, dv * h:dv * (h + 1)] += _nn(pt.astype(BF16), doh)
                dqts.append(_nn(kt[dq * h:dq * (h + 1), 0:tk], dsb))
                if use_bias:
                    dcqs.append(jnp.sum(dst, axis=0, keepdims=True))
                    dacc[h, 0:tk, :] += dst
            dq_ref[...] = jnp.concatenate(dqts, axis=0).T * scale
            if use_bias:
                dcq_ref[...] = jnp.concatenate(dcqs + [jnp.zeros((8 - nh, tq), F32)], axis=0)

        for t0, t1 in _segments(nq, not use_bias):
            pl.when((i >= t0) & (i < t1))(functools.partial(tile, t1 * BLK))

        if use_bias:
            @pl.when(i == nq - 1)
            def _():
                lane = lax.broadcasted_iota(jnp.int32, (1, BLK), 1)
                tot = jnp.zeros((t, BLK), F32)
                for h in range(nh):
                    tot += jnp.where(lane == lane0 + h, jnp.sum(dacc[h], axis=1, keepdims=True), 0.0)
                dck_ref[...] = tot

    keys_q = pl.BlockSpec((t, nh * dq), lambda i: (0, 0))
    keys_v = pl.BlockSpec((t, nh * dv), lambda i: (0, 0))
    keys_c = pl.BlockSpec((t, BLK), lambda i: (0, 0))
    qrow = pl.BlockSpec((8, tq), lambda i: (0, i))
    in_specs = [pl.BlockSpec((tq, nh * dq), lambda i: (i, qcol)),
                pl.BlockSpec((t, nh * dq), lambda i: (0, kcol)),
                pl.BlockSpec((t, nh * dv), lambda i: (0, vcol)),
                pl.BlockSpec((tq, nh * dv), lambda i: (i, docol)),
                qrow,
                pl.BlockSpec((tq, nh * dv), lambda i: (i, ocol))]
    args = [q, k, v, do, lse_row, o]
    out_specs = [pl.BlockSpec((tq, nh * dq), lambda i: (i, 0)), keys_q, keys_v]
    out_shape = [jax.ShapeDtypeStruct((t, nh * dq), F32), jax.ShapeDtypeStruct((t, nh * dq), F32),
                 jax.ShapeDtypeStruct((t, nh * dv), F32)]
    scratch = [pltpu.VMEM((nh * dq, t), BF16)]
    if use_bias:
        in_specs += [keys_c, qrow]
        args += [c_col, c_row]
        out_specs += [qrow, keys_c]
        out_shape += [jax.ShapeDtypeStruct((8, t), F32), jax.ShapeDtypeStruct((t, BLK), F32)]
        scratch += [pltpu.VMEM((nh, t, BLK), F32), pltpu.VMEM((nh, t, BLK), F32)]
    return pl.pallas_call(
        body, name="attn_bwd", grid=(nq,),
        in_specs=in_specs, out_specs=out_specs, out_shape=out_shape, scratch_shapes=scratch,
        compiler_params=_cp("arbitrary"),
    )(*args)


def fox_pre(proj, fb):
    t = proj.shape[0]
    nb = t // BLK

    def body(sm_ref, fb_ref, c_ref, cr_ref):
        x = sm_ref[...] + fb_ref[...]
        lane = lax.broadcasted_iota(jnp.int32, (1, BLK), 1)
        keep = _valid_rows(t, 0) & (lane >= SM_F) & (lane < SM_F + FOX_H)
        logf = jnp.where(keep, jnp.minimum(x, 0.0) - jnp.log(1.0 + jnp.exp(-jnp.abs(x))), 0.0)
        tri = _tri().astype(F32)
        carry = jnp.zeros((1, BLK), F32)
        for b in range(nb):
            cb = _nn_hi(tri, logf[b * BLK:(b + 1) * BLK, :]) + carry
            c_ref[b * BLK:(b + 1) * BLK, :] = cb
            carry = cb[BLK - 1:BLK, :]
        cr_ref[...] = c_ref[...].T[SM_F:SM_F + 8, :]

    return pl.pallas_call(
        body, name="fox_pre", grid=(1,),
        in_specs=[pl.BlockSpec((t, BLK), lambda i: (0, C_SM // BLK)), pl.BlockSpec((1, BLK), lambda i: (0, 0))],
        out_specs=[pl.BlockSpec((t, BLK), lambda i: (0, 0)), pl.BlockSpec((8, t), lambda i: (0, 0))],
        out_shape=[jax.ShapeDtypeStruct((t, BLK), F32), jax.ShapeDtypeStruct((8, t), F32)],
        compiler_params=_cp("arbitrary"),
    )(proj, fb)


def fox_pre_bwd(dcq, dck, proj, fb, dsm_in):
    t = proj.shape[0]
    nb = t // BLK

    def body(dcq_ref, dck_ref, sm_ref, fb_ref, din_ref, dsm_ref, dfb_ref, scr):
        triu = _tri(lower=False).astype(F32)
        carry = jnp.zeros((1, BLK), F32)
        scr[...] = jnp.concatenate([jnp.zeros((SM_F, t), F32), dcq_ref[...], jnp.zeros((BLK - SM_F - 8, t), F32)], axis=0).T
        for b in range(nb - 1, -1, -1):
            blk = scr[b * BLK:(b + 1) * BLK, :] - dck_ref[b * BLK:(b + 1) * BLK, :]
            cb = _nn_hi(triu, blk) + carry
            scr[b * BLK:(b + 1) * BLK, :] = cb
            carry = cb[0:1, :]
        x = sm_ref[...] + fb_ref[...]
        lane = lax.broadcasted_iota(jnp.int32, (1, BLK), 1)
        keep = _valid_rows(t, 0) & (lane >= SM_F) & (lane < SM_F + FOX_H)
        df = jnp.where(keep, scr[...] * _sig(-x), 0.0)
        dfb_ref[...] = jnp.sum(df, axis=0, keepdims=True)
        dsm_ref[...] = din_ref[...] + df

    full = pl.BlockSpec((t, BLK), lambda i: (0, 0))
    return pl.pallas_call(
        body, name="fox_pre_bwd", grid=(1,),
        in_specs=[pl.BlockSpec((8, t), lambda i: (0, 0)), full,
                  pl.BlockSpec((t, BLK), lambda i: (0, C_SM // BLK)), pl.BlockSpec((1, BLK), lambda i: (0, 0)), full],
        out_specs=[full, pl.BlockSpec((1, BLK), lambda i: (0, 0))],
        out_shape=[jax.ShapeDtypeStruct((t, BLK), F32), jax.ShapeDtypeStruct((1, BLK), F32)],
        scratch_shapes=[pltpu.VMEM((t, BLK), F32)],
        compiler_params=_cp("arbitrary"),
    )(dcq, dck, proj, fb, dsm_in)


def _swap_rope(x):
    lane = lax.broadcasted_iota(jnp.int32, (1, BLK), 1)
    return jnp.where((lane >= SM_KR) & (lane < SM_KR + 16), pltpu.roll(x, BLK - 16, 1),
                     jnp.where((lane >= SM_KR + 16) & (lane < SM_KR + 32), pltpu.roll(x, 16, 1), 0.0))


def _rms(x, g):
    r = lax.rsqrt(jnp.mean(x * x, axis=1, keepdims=True) + EPS)
    return r, x * r


def mla_pre(proj, qg, kvg, wq, wk, wv, cosq, sinq):
    t = proj.shape[0]
    tm = _row_tile(t)

    def body(cq_ref, ckv_ref, sm_ref, qg_ref, kvg_ref, wq_ref, wk_ref, wv_ref, cos_ref, sin_ref,
             q_ref, k_ref, v_ref, cqn_ref, ckvn_ref):
        cs, sn = cos_ref[...], sin_ref[...]
        _, xh = _rms(cq_ref[...], None)
        cqn = (xh * qg_ref[...]).astype(BF16)
        cqn_ref[...] = cqn
        qraw = _nn(cqn, wq_ref[...])
        qs = []
        for h in range(MLA_H):
            hb = qraw[:, BLK * h:BLK * (h + 1)]
            qs.append(hb * cs + _swap_rope(hb) * sn)
        q_ref[...] = jnp.concatenate(qs, axis=1).astype(BF16)
        _, kh = _rms(ckv_ref[...], None)
        ckvn = (kh * kvg_ref[...]).astype(BF16)
        ckvn_ref[...] = ckvn
        kraw = _nn(ckvn, wk_ref[...])
        v_ref[...] = _nn(ckvn, wv_ref[...]).astype(BF16)
        lane = lax.broadcasted_iota(jnp.int32, (1, BLK), 1)
        kr = sm_ref[...]
        krr = jnp.where((lane >= SM_KR) & (lane < SM_KR + MLA_ROPE), kr * cs + _swap_rope(kr) * sn, 0.0)
        k_ref[...] = jnp.concatenate([kraw[:, BLK * h:BLK * (h + 1)] + krr for h in range(MLA_H)], axis=1).astype(BF16)

    def rows(w, cb):
        return pl.BlockSpec((tm, w), lambda i: (i, cb))

    def whole(a):
        return pl.BlockSpec(a.shape, lambda i: (0, 0))

    return pl.pallas_call(
        body, name="mla_pre", grid=(t // tm,),
        in_specs=[rows(MLA_QL, C_CQ // MLA_QL), rows(MLA_KVL, C_CKV // MLA_KVL), rows(BLK, C_SM // BLK),
                  whole(qg), whole(kvg), whole(wq), whole(wk), whole(wv), rows(BLK, 0), rows(BLK, 0)],
        out_specs=[rows(512, 0), rows(512, 0), rows(256, 0), rows(MLA_QL, 0), rows(MLA_KVL, 0)],
        out_shape=[jax.ShapeDtypeStruct((t, 512), BF16), jax.ShapeDtypeStruct((t, 512), BF16),
                   jax.ShapeDtypeStruct((t, 256), BF16), jax.ShapeDtypeStruct((t, MLA_QL), BF16),
                   jax.ShapeDtypeStruct((t, MLA_KVL), BF16)],
        compiler_params=_cp("arbitrary"),
    )(proj, proj, proj, qg, kvg, wq, wk, wv, cosq, sinq)


def mla_pre_bwd(dq, dk, dv, proj, cqn, ckvn, qg, kvg, wq, wk, wv, cosq, sinq, dsm_in):
    t = proj.shape[0]
    tm = _row_tile(t)

    def body(dq_ref, dk_ref, dv_ref, cq_ref, ckv_ref, cqn_ref, ckvn_ref, qg_ref, kvg_ref, wq_ref, wk_ref, wv_ref,
             cos_ref, sin_ref, din_ref, dcq_ref, dckv_ref, dsm_ref, dwq_ref, dwk_ref, dwv_ref, dqg_ref, dkvg_ref):
        i = pl.program_id(0)

        @pl.when(i == 0)
        def _():
            for r in (dwq_ref, dwk_ref, dwv_ref, dqg_ref, dkvg_ref):
                r[...] = jnp.zeros_like(r)

        cs, sn = cos_ref[...], sin_ref[...]
        lane = lax.broadcasted_iota(jnp.int32, (1, BLK), 1)

        def unrope(dy):
            return dy * cs + _swap_rope(dy * sn)

        dqp = jnp.concatenate([unrope(dq_ref[:, BLK * h:BLK * (h + 1)]) for h in range(MLA_H)], axis=1).astype(BF16)
        dwq_ref[...] += _tn(cqn_ref[...], dqp)
        dcqn = _nt(dqp, wq_ref[...])
        r, xh = _rms(cq_ref[...], None)
        dqg_ref[...] += jnp.sum(dcqn * xh, axis=0, keepdims=True)
        dxh = dcqn * qg_ref[...]
        dcq_ref[...] = r * (dxh - xh * jnp.mean(dxh * xh, axis=1, keepdims=True))

        dkn, dkr = [], jnp.zeros((tm, BLK), F32)
        for h in range(MLA_H):
            blk = dk_ref[:, BLK * h:BLK * (h + 1)]
            dkn.append(jnp.where(lane < MLA_NOPE, blk, 0.0))
            dkr += jnp.where((lane >= SM_KR) & (lane < SM_KR + MLA_ROPE), blk, 0.0)
        dknb = jnp.concatenate(dkn, axis=1).astype(BF16)
        dvb = dv_ref[...].astype(BF16)
        ckvn = ckvn_ref[...]
        dwk_ref[...] += _tn(ckvn, dknb)
        dwv_ref[...] += _tn(ckvn, dvb)
        dckvn = _nt(dknb, wk_ref[...]) + _nt(dvb, wv_ref[...])
        r2, kh = _rms(ckv_ref[...], None)
        dkvg_ref[...] += jnp.sum(dckvn * kh, axis=0, keepdims=True)
        dkh = dckvn * kvg_ref[...]
        dckv_ref[...] = r2 * (dkh - kh * jnp.mean(dkh * kh, axis=1, keepdims=True))
        dsm_ref[...] = din_ref[...] + jnp.where((lane >= SM_KR) & (lane < SM_KR + MLA_ROPE), unrope(dkr), 0.0)

    def rows(w, cb):
        return pl.BlockSpec((tm, w), lambda i: (i, cb))

    def whole(a):
        return pl.BlockSpec(a.shape, lambda i: (0, 0))

    def wshape(a):
        return jax.ShapeDtypeStruct(a.shape, F32)

    return pl.pallas_call(
        body, name="mla_pre_bwd", grid=(t // tm,),
        in_specs=[rows(512, 0), rows(512, 0), rows(256, 0), rows(MLA_QL, C_CQ // MLA_QL), rows(MLA_KVL, C_CKV // MLA_KVL),
                  rows(MLA_QL, 0), rows(MLA_KVL, 0), whole(qg), whole(kvg), whole(wq), whole(wk), whole(wv),
                  rows(BLK, 0), rows(BLK, 0), rows(BLK, 0)],
        out_specs=[rows(MLA_QL, 0), rows(MLA_KVL, 0), rows(BLK, 0), whole(wq), whole(wk), whole(wv), whole(qg), whole(kvg)],
        out_shape=[jax.ShapeDtypeStruct((t, MLA_QL), F32), jax.ShapeDtypeStruct((t, MLA_KVL), F32),
                   jax.ShapeDtypeStruct((t, BLK), F32), wshape(wq), wshape(wk), wshape(wv), wshape(qg), wshape(kvg)],
        compiler_params=_cp("arbitrary"),
    )(dq, dk, dv, proj, proj, cqn, ckvn, qg, kvg, wq, wk, wv, cosq, sinq, dsm_in)


def _slot_sum(me, own, recv_ref):
    gg = own.astype(F32)
    for s in range(N_DEV):
        gg = gg + jnp.where(me == s, 0.0, recv_ref[s].astype(F32))
    return gg


def adamw(w, m, v, g=None, recv=None, own=None, me_arr=None):
    shape = w.shape
    c = shape[-1]
    from_recv = recv is not None
    if not from_recv:
        me_arr = jnp.zeros((1,), jnp.int32)
    nl = len(recv) if from_recv else 1
    rws = w.size // c // nl
    tr = rws
    for d in (1024, 512, 352, 256, 128, 64, 32, 16, 8):
        if rws % d == 0 and d * c * 4 <= (2 << 20):
            tr = d
            break
    nt = rws // tr
    w2, m2, v2 = (a.reshape(nl, rws, c) for a in (w, m, v))
    if from_recv:
        gin = [a.reshape(N_DEV, rws, c) for a in list(recv) + list(own)]
    else:
        gin = [g.reshape(1, rws, c)]

    def body(me_ref, w_ref, m_ref, v_ref, *rest):
        g_refs, outs = rest[:len(gin)], rest[len(gin):]
        if from_recv:
            g_out, outs = outs[0], outs[1:]
            for li in range(nl):
                @pl.when(pl.program_id(0) == li)
                def _(li=li):
                    g_out[...] = _slot_sum(me_ref[0], g_refs[nl + li][...], g_refs[li])
            gg = g_out[...]
        else:
            gg = g_refs[0][...]
        d_ref, nm_ref, nv_ref = outs
        nm = B1 * m_ref[...] + (1.0 - B1) * gg
        nv = B2 * v_ref[...] + (1.0 - B2) * (gg * gg)
        mh = nm / (1.0 - B1 ** STEP)
        vh = nv / (1.0 - B2 ** STEP)
        d_ref[...] = -LR * (mh / (jnp.sqrt(vh) + AEPS) + WD * w_ref[...])
        nm_ref[...] = nm
        nv_ref[...] = nv

    row = pl.BlockSpec((None, tr, c), lambda l, i, me: (l, i, 0))
    if from_recv:
        gspecs = [pl.BlockSpec((N_DEV, tr, c), lambda l, i, me, li=li: (0, jnp.where(l == li, i, 0), 0))
                  for li in range(nl)]
        gspecs += [pl.BlockSpec((None, tr, c), lambda l, i, me, li=li: (me[0], jnp.where(l == li, i, 0), 0))
                   for li in range(nl)]
    else:
        gspecs = [row]
    nout = 4 if from_recv else 3
    outs = pl.pallas_call(
        body, name="adamw",
        grid_spec=pltpu.PrefetchScalarGridSpec(num_scalar_prefetch=1, grid=(nl, nt), in_specs=[row, row, row] + gspecs,
                                               out_specs=[row] * nout),
        out_shape=[jax.ShapeDtypeStruct((nl, rws, c), F32)] * nout,
        compiler_params=_cp("arbitrary", "arbitrary"),
    )(me_arr, w2, m2, v2, *gin)
    return tuple(o.reshape(shape) for o in outs)


def sum_slots(recv, own=None, me_arr=None):
    _, r, c = recv.shape
    if own is None:
        own, me_arr = recv, jnp.zeros((1,), jnp.int32)
        plain = True
    else:
        plain = False

    def body(me_ref, r_ref, own_ref, o_ref):
        if plain:
            gg = r_ref[0].astype(F32)
            for s in range(1, N_DEV):
                gg = gg + r_ref[s].astype(F32)
            o_ref[...] = gg
        else:
            o_ref[...] = _slot_sum(me_ref[0], own_ref[...], r_ref)

    return pl.pallas_call(
        body, name="sum_slots",
        grid_spec=pltpu.PrefetchScalarGridSpec(
            num_scalar_prefetch=1, grid=(1,),
            in_specs=[pl.BlockSpec((N_DEV, r, c), lambda i, me: (0, 0, 0)),
                      pl.BlockSpec((None, r, c), lambda i, me: (me[0], 0, 0))],
            out_specs=pl.BlockSpec((r, c), lambda i, me: (0, 0))),
        out_shape=jax.ShapeDtypeStruct((r, c), F32),
        compiler_params=_cp("arbitrary"),
    )(me_arr, recv, own)


_FLIPS = [(0, 0, 1), (0, 1, 0), (0, 1, 1), (1, 0, 0), (1, 0, 1), (1, 1, 0), (1, 1, 1)]
_ANY = pl.BlockSpec(memory_space=pl.ANY)


def _mesh_place():
    x, y, c = lax.axis_index("x"), lax.axis_index("y"), lax.axis_index("c")
    me = 4 * x + 2 * y + c
    peers = [((x + fx) % 2, (y + fy) % 2, (c + fc) % 2) for fx, fy, fc in _FLIPS]
    return me, peers


def place_own(src, l, dtype, me_arr):
    _, r, c = src.shape
    tr = r
    for d in (512, 352, 256, 128, 64, 32, 16, 8):
        if r % d == 0 and d * c * 4 <= (2 << 20):
            tr = d
            break

    def body(me_ref, s_ref, o_ref):
        o_ref[...] = s_ref[...].astype(dtype)

    return pl.pallas_call(
        body, name="place_own",
        grid_spec=pltpu.PrefetchScalarGridSpec(
            num_scalar_prefetch=1, grid=(r // tr,),
            in_specs=[pl.BlockSpec((None, tr, c), lambda i, me: (l, i, 0))],
            out_specs=pl.BlockSpec((None, tr, c), lambda i, me: (me[0], i, 0))),
        out_shape=jax.ShapeDtypeStruct((N_DEV, r, c), dtype),
        compiler_params=_cp("arbitrary"),
    )(me_arr, src)


_HBM = pl.BlockSpec(memory_space=pltpu.HBM)
_SEMS = pl.BlockSpec(memory_space=pltpu.SEMAPHORE)
_EFFECT = pltpu.SideEffectType.DATAFLOW_SIDE_EFFECTING


def exchange_start(mode, arrays, name, after=None):
    n = len(arrays)
    gather = mode == "gather"
    ns = 0 if gather else n
    zones = list(arrays) if gather else [lax.empty(a.shape, a.dtype) for a in arrays]
    ops = ([] if gather else list(arrays)) + zones
    extra = [] if after is None else [after]

    def body(*refs):
        srcs, lands = refs[:ns], refs[ns:ns + n]
        send_sems, recv_sems = refs[ns + n + len(extra)], refs[ns + n + len(extra) + 1]
        token = refs[-1]
        me, peers = _mesh_place()
        ids = [4 * p[0] + 2 * p[1] + p[2] for p in peers]
        for j in range(n):
            for k in range(N_DEV - 1):
                src = lands[j].at[me] if gather else srcs[j].at[ids[k]]
                pltpu.make_async_remote_copy(src_ref=src, dst_ref=lands[j].at[me],
                                             send_sem=send_sems.at[j * (N_DEV - 1) + k],
                                             recv_sem=recv_sems.at[j * (N_DEV - 1) + k], device_id=peers[k],
                                             device_id_type=pl.DeviceIdType.MESH).start()
        token[...] = jnp.zeros_like(token)

    nsem = n * (N_DEV - 1)
    res = pl.pallas_call(
        body, name=name,
        in_specs=[_HBM] * (ns + n) + [_ANY] * len(extra),
        out_specs=(_SEMS, _SEMS, *[_HBM] * (ns + n), pl.BlockSpec(memory_space=pltpu.VMEM)),
        out_shape=(pltpu.SemaphoreType.DMA((nsem,)), pltpu.SemaphoreType.DMA((nsem,)),
                   *[pltpu.HBM(a.shape, a.dtype) for a in ops], jax.ShapeDtypeStruct((8, BLK), F32)),
        input_output_aliases={i: 2 + i for i in range(ns + n)},
        compiler_params=pltpu.CompilerParams(has_side_effects=_EFFECT),
    )(*[pltpu.with_memory_space_constraint(a, pltpu.HBM) for a in ops], *extra)
    return dict(gather=gather, send=res[0], recv=res[1], srcs=list(res[2:2 + ns]), lands=list(res[2 + ns:2 + ns + n]),
                token=res[-1])


def exchange_wait(hd, idxs, name, after):
    gather = hd["gather"]
    n = len(idxs)
    ns = 0 if gather else n
    ops = ([] if gather else [hd["srcs"][j] for j in idxs]) + [hd["lands"][j] for j in idxs]

    def body(*refs):
        srcs, lands = refs[:ns], refs[ns:ns + n]
        send_sems, recv_sems = refs[ns + n], refs[ns + n + 1]
        me, peers = _mesh_place()
        ids = [4 * p[0] + 2 * p[1] + p[2] for p in peers]
        for p, j in enumerate(idxs):
            for k in range(N_DEV - 1):
                src = lands[p].at[me] if gather else srcs[p].at[ids[k]]
                cp = pltpu.make_async_remote_copy(src_ref=src, dst_ref=lands[p].at[ids[k]],
                                                  send_sem=send_sems.at[j * (N_DEV - 1) + k],
                                                  recv_sem=recv_sems.at[j * (N_DEV - 1) + k], device_id=peers[k],
                                                  device_id_type=pl.DeviceIdType.MESH)
                cp.wait_send()
                cp.wait_recv()

    res = pl.pallas_call(
        body, name=name,
        in_specs=[_HBM] * (ns + n) + [_SEMS, _SEMS, _ANY],
        out_specs=[_HBM] * (ns + n),
        out_shape=[pltpu.HBM(a.shape, a.dtype) for a in ops],
        input_output_aliases={i: i for i in range(ns + n)},
        compiler_params=pltpu.CompilerParams(has_side_effects=_EFFECT),
    )(*ops, hd["send"], hd["recv"], after)
    return list(res[:ns]), list(res[ns:])


def _chip_place():
    x, y, c = lax.axis_index("x"), lax.axis_index("y"), lax.axis_index("c")
    chips = [((x + 1) % 2, y), (x, (y + 1) % 2), ((x + 1) % 2, (y + 1) % 2)]
    ident = lambda p: 4 * p[0] + 2 * p[1] + p[2]
    return dict(me=4 * x + 2 * y + c, sib=(x, y, 1 - c), sib_id=4 * x + 2 * y + 1 - c,
                same=[(cx, cy, c) for cx, cy in chips], same_ids=[ident((cx, cy, c)) for cx, cy in chips],
                other_ids=[ident((cx, cy, 1 - c)) for cx, cy in chips])


def _remote(src, dst, send_sem, recv_sem, dev):
    return pltpu.make_async_remote_copy(src_ref=src, dst_ref=dst, send_sem=send_sem, recv_sem=recv_sem, device_id=dev,
                                        device_id_type=pl.DeviceIdType.MESH)


def gather_start(zones, name):
    n = len(zones)

    def body(*refs):
        lands, send_sems, recv_sems, token = refs[:n], refs[n], refs[n + 1], refs[-1]
        pc = _chip_place()
        for j in range(n):
            own = lands[j].at[pc["me"]]
            for k, dev in enumerate([pc["sib"]] + pc["same"]):
                _remote(own, own, send_sems.at[4 * j + k], recv_sems.at[4 * j + k], dev).start()
        token[...] = jnp.zeros_like(token)

    res = pl.pallas_call(
        body, name=name,
        in_specs=[_HBM] * n,
        out_specs=(_SEMS, _SEMS, *[_HBM] * n, pl.BlockSpec(memory_space=pltpu.VMEM)),
        out_shape=(pltpu.SemaphoreType.DMA((4 * n,)), pltpu.SemaphoreType.DMA((4 * n,)),
                   *[pltpu.HBM(a.shape, a.dtype) for a in zones], jax.ShapeDtypeStruct((8, BLK), F32)),
        input_output_aliases={i: 2 + i for i in range(n)},
        compiler_params=pltpu.CompilerParams(has_side_effects=_EFFECT),
    )(*[pltpu.with_memory_space_constraint(a, pltpu.HBM) for a in zones])
    return dict(send=res[0], recv=res[1], lands=list(res[2:2 + n]), token=res[-1])


def gather_relay(hd, idxs, name, after):
    n = len(idxs)

    def body(*refs):
        lands, send_sems, recv_sems = refs[:n], refs[n], refs[n + 1]
        fsend, frecv, token = refs[n + 3 + n], refs[n + 4 + n], refs[-1]
        pc = _chip_place()
        for p, j in enumerate(idxs):
            for k in range(3):
                _remote(lands[p].at[pc["me"]], lands[p].at[pc["same_ids"][k]], send_sems.at[4 * j + 1 + k],
                        recv_sems.at[4 * j + 1 + k], pc["same"][k]).wait_recv()
        for p in range(n):
            for k in range(3):
                got = lands[p].at[pc["same_ids"][k]]
                _remote(got, got, fsend.at[3 * p + k], frecv.at[3 * p + k], pc["sib"]).start()
        token[...] = jnp.zeros_like(token)

    ops = [hd["lands"][j] for j in idxs]
    res = pl.pallas_call(
        body, name=name,
        in_specs=[_HBM] * n + [_SEMS, _SEMS, _ANY],
        out_specs=(*[_HBM] * n, _SEMS, _SEMS, pl.BlockSpec(memory_space=pltpu.VMEM)),
        out_shape=(*[pltpu.HBM(a.shape, a.dtype) for a in ops], pltpu.SemaphoreType.DMA((3 * n,)),
                   pltpu.SemaphoreType.DMA((3 * n,)), jax.ShapeDtypeStruct((8, BLK), F32)),
        input_output_aliases={i: i for i in range(n)},
        compiler_params=pltpu.CompilerParams(has_side_effects=_EFFECT),
    )(*ops, hd["send"], hd["recv"], after)
    return dict(lands=list(res[:n]), fsend=res[n], frecv=res[n + 1], token=res[-1])


def gather_wait(hd, rl, idxs, name, after):
    n = len(idxs)

    def body(*refs):
        lands, send_sems, recv_sems, fsend, frecv = refs[:n], refs[n], refs[n + 1], refs[n + 2], refs[n + 3]
        pc = _chip_place()
        for p, j in enumerate(idxs):
            own = lands[p].at[pc["me"]]
            for k, dev in enumerate([pc["sib"]] + pc["same"]):
                _remote(own, own, send_sems.at[4 * j + k], recv_sems.at[4 * j + k], dev).wait_send()
            _remote(own, lands[p].at[pc["sib_id"]], send_sems.at[4 * j], recv_sems.at[4 * j], pc["sib"]).wait_recv()
            for k in range(3):
                cp = _remote(lands[p].at[pc["same_ids"][k]], lands[p].at[pc["other_ids"][k]], fsend.at[3 * p + k],
                             frecv.at[3 * p + k], pc["sib"])
                cp.wait_send()
                cp.wait_recv()

    res = pl.pallas_call(
        body, name=name,
        in_specs=[_HBM] * n + [_SEMS, _SEMS, _SEMS, _SEMS, _ANY],
        out_specs=[_HBM] * n,
        out_shape=[pltpu.HBM(a.shape, a.dtype) for a in rl["lands"]],
        input_output_aliases={i: i for i in range(n)},
        compiler_params=pltpu.CompilerParams(has_side_effects=_EFFECT),
    )(*rl["lands"], hd["send"], hd["recv"], rl["fsend"], rl["frecv"], after)
    return list(res)


def _pad_cols(a, n):
    return jnp.pad(a, ((0, 0),) * (a.ndim - 1) + ((0, n - a.shape[-1]),))


def w_in_to_padded(w):
    z = lambda n: jnp.zeros(w.shape[:-1] + (n,), w.dtype)
    return jnp.concatenate([
        w[..., 0:1280], w[..., 1288:2056], w[..., 2060:2316], w[..., 2316:2444],
        w[..., 1280:1288], w[..., 2056:2060], z(SM_KR - SM_F - FOX_H), w[..., 2444:2476], z(BLK - SM_KR - MLA_ROPE)], axis=-1)


def w_in_from_padded(g):
    s = C_SM
    return jnp.concatenate([
        g[..., 0:1280], g[..., s + SM_DT:s + SM_DT + 8], g[..., 1280:2048], g[..., s + SM_F:s + SM_F + 4],
        g[..., 2048:2304], g[..., 2304:2432], g[..., s + SM_KR:s + SM_KR + MLA_ROPE]], axis=-1)


def _unshard_cols(gth):
    n, r, c = gth.shape
    return jnp.transpose(gth, (1, 0, 2)).reshape(r, n * c)


def _shard_cols(full):
    r, nc = full.shape
    return jnp.transpose(full.reshape(r, N_DEV, nc // N_DEV), (1, 0, 2))


def mla_weights(uq_g, ukv_g):
    uq = _unshard_cols(uq_g)
    dqh = MLA_NOPE + MLA_ROPE
    wq = jnp.concatenate([_pad_cols(uq[:, dqh * h:dqh * (h + 1)], BLK) for h in range(MLA_H)], axis=1)
    wk = jnp.concatenate([_pad_cols(ukv_g[2 * h], BLK) for h in range(MLA_H)], axis=1)
    wv = jnp.concatenate([ukv_g[2 * h + 1] for h in range(MLA_H)], axis=1)
    return wq, wk, wv


def mla_weight_grads(dwq, dwk, dwv):
    dqh = MLA_NOPE + MLA_ROPE
    duq = _shard_cols(jnp.concatenate([dwq[:, BLK * h:BLK * h + dqh] for h in range(MLA_H)], axis=1))
    parts = []
    for h in range(MLA_H):
        parts += [dwk[:, BLK * h:BLK * h + MLA_NOPE], dwv[:, MLA_V * h:MLA_V * (h + 1)]]
    return duq, jnp.stack(parts, axis=0)


def rope_tables(t):
    pos = (jnp.arange(t, dtype=jnp.int32) - PAD).astype(F32)
    inv_freq = 1.0 / (10000.0 ** (jnp.arange(0, MLA_ROPE, 2, dtype=F32) / MLA_ROPE))
    ang = pos[:, None] * inv_freq[None, :]
    cos, sin = jnp.cos(ang), jnp.sin(ang)
    one, zero = jnp.ones((t, SM_KR), F32), jnp.zeros((t, SM_KR), F32)
    tail = BLK - SM_KR - MLA_ROPE
    cosq = jnp.concatenate([one, cos, cos, jnp.ones((t, tail), F32)], axis=1)
    sinq = jnp.concatenate([zero, -sin, sin, jnp.zeros((t, tail), F32)], axis=1)
    return cosq, sinq


def _lanes(v, off=0):
    return jnp.pad(v.astype(F32), (off, BLK - off - v.shape[0]))[None, :]


def layer_fwd(x, ln, hb, getw, tabs, ahead):
    sv = {"h0b": hb}
    def behind(vec, tok):
        return vec if tok is None else vec + 0.0 * tok[0:1, 0:1]

    W = dict(getw("ffn1", hb))
    ln1 = (behind(W["ln1_g"], ahead(0, "mix", hb, 1)), W["ln1_b"])
    u, v, r1, h1b = ffn_fwd_seq(x, ln, W["g1"], W["u1"], W["d1"], ln1)
    sv.update(u1=u, v1=v, r1=r1, h1b=h1b)
    W.update(getw("mix", h1b))
    ln2 = (W["ln2_g"], W["ln2_b"])
    proj = mm_nn(h1b, W["w_in"])
    xa = conv_fwd(proj, W["conv_w"], W["conv_b"])
    y_ssd, sprev = ssd_fwd(xa, proj, W["dtb"], W["alog"], W["dskip"], W["normg"])
    c_col, c_row = fox_pre(proj, W["fb"])
    y_fox, lse_f = attn_fwd(proj, proj, proj, C_FQ // 256, C_FK // 256, C_FV // 256, FOX_H, FOX_DH, FOX_DH,
                            FOX_DH ** -0.5, c_col, c_row, SM_F)
    q, k, vv, cqn, ckvn = mla_pre(proj, behind(W["qg"], ahead(0, "ffn2", y_fox)), W["kvg"], W["wq"], W["wk"], W["wv"], *tabs)
    y_mla, lse_m = attn_fwd(q, k, vv, 0, 0, 0, MLA_H, BLK, MLA_V, (MLA_NOPE + MLA_ROPE) ** -0.5)
    r2, h2b = mm_res_ln([y_ssd, y_fox, y_mla], W["w_out"], r1, ln1, ln2)
    sv.update(proj=proj, xa=xa, sprev=sprev, c_col=c_col, c_row=c_row, lse_f=lse_f, q=q, k=k, v=vv, cqn=cqn, ckvn=ckvn,
              lse_m=lse_m, y_ssd=y_ssd, y_fox=y_fox, y_mla=y_mla, r2=r2, h2b=h2b)
    W.update(getw("ffn2", h2b))
    ln3 = (behind(W["ln3_g"], ahead(1, "ffn1", h2b)), W["ln3_b"])
    u, v, r3, h3b = ffn_fwd_seq(r2, ln2, W["g2"], W["u2"], W["d2"], ln3)
    sv.update(u2=u, v2=v, r3=r3, W=W)
    return r3, ln3, h3b, sv


def ffn_bwd(parts, r, gamma, hb_in, u, v, wg, wu, wd, after=None):
    dh, dwg, dwu, dwd, dg, db = ffn_bwd_seq(parts, r, gamma, hb_in, u, v, wg, wu, wd, after)
    return dh, dict(d=dwd, g=dwg, u=dwu, ln_g=dg, ln_b=db)


def layer_bwd(parts, sv, emit, tabs, after):
    G = {}
    W = sv["W"]
    dh2, g2 = ffn_bwd(parts, sv["r3"], W["ln3_g"], sv["h2b"], sv["u2"], sv["v2"], W["g2"], W["u2"], W["d2"], after)
    G.update(g2=g2["g"], u2=g2["u"], d2=g2["d"], ln3_g=g2["ln_g"], ln3_b=g2["ln_b"])
    tok = emit("ffn2", G)
    dr2, dmc, G["w_out"], G["ln2_g"], G["ln2_b"] = oproj_bwd(dh2, sv["r2"], W["ln2_g"], [sv["y_ssd"], sv["y_fox"], sv["y_mla"]], W["w_out"], tok)
    proj = sv["proj"]
    dxa, dz, dsm, G["normg"], G["dskip"], G["alog"], G["dtb"] = ssd_bwd(
        dmc, sv["xa"], proj, sv["sprev"], W["dtb"], W["alog"], W["dskip"], W["normg"])
    dxbc, G["conv_w"], G["conv_b"] = conv_bwd(dxa, proj, W["conv_w"], W["conv_b"])
    dfq, dfk, dfv, dcq, dck = attn_bwd(proj, proj, proj, dmc, sv["lse_f"], sv["y_fox"], C_FQ // 256, C_FK // 256,
                                       C_FV // 256, 2, 0, FOX_H, FOX_DH, FOX_DH, FOX_DH ** -0.5, sv["c_col"], sv["c_row"], SM_F)
    dsm, G["fb"] = fox_pre_bwd(dcq, dck, proj, W["fb"], dsm)
    dq, dk, dv = attn_bwd(sv["q"], sv["k"], sv["v"], dmc, sv["lse_m"], sv["y_mla"], 0, 0, 0, 3, 0, MLA_H, BLK, MLA_V,
                          (MLA_NOPE + MLA_ROPE) ** -0.5)
    dcql, dckv, dsm, G["wq"], G["wk"], G["wv"], G["qg"], G["kvg"] = mla_pre_bwd(
        dq, dk, dv, proj, sv["cqn"], sv["ckvn"], W["qg"], W["kvg"], W["wq"], W["wk"], W["wv"], *tabs, dsm)
    dh1p, G["w_in"] = proj_bwd([dz, dxbc, dfq, dfk, dfv, dcql, dckv, dsm], sv["h1b"], W["w_in"])
    tok = emit("mix", G)
    dh0, g1 = ffn_bwd([(dr2, ALPHA), (dh1p, 1.0)], sv["r1"], W["ln1_g"], sv["h0b"], sv["u1"], sv["v1"],
                      W["g1"], W["u1"], W["d1"], tok)
    G.update(g1=g1["g"], u1=g1["u"], d1=g1["d"], ln1_g=g1["ln_g"], ln1_b=g1["ln_b"])
    tok = emit("ffn1", G)
    return [(dh0, 1.0)], G, tok


def local_step(x, target, meta_full, getw, emit, ahead=lambda l, stage, after, min_layer=0: None):
    t = x.shape[0] + BLK
    tabs = rope_tables(t)
    xr, hb = build_h0(meta_full, x)
    ln = None
    saved = []
    for l in range(NL):
        xr, ln, hb, sv = layer_fwd(xr, ln, hb, functools.partial(getw, l), tabs,
                                   lambda dl, stage, after, min_layer=0, l=l: ahead(l + dl, stage, after, min_layer))
        saved.append(sv)
    dy, loss = loss_head(xr, ln, target)
    parts = [(dy, 1.0)]
    grads = [None] * NL
    tok = None
    for l in range(NL - 1, -1, -1):
        parts, grads[l], tok = layer_bwd(parts, saved[l], functools.partial(emit, l), tabs, tok)
    gx, gmeta = split_dh0(parts[0][0], tok)
    return loss, gx, gmeta, grads


_SMALL = ["ln1_g", "ln1_b", "ln2_g", "ln2_b", "ln3_g", "ln3_b", "conv_b", "ssd_norm_g", "mla_q_norm_g",
          "mla_kv_norm_g", "dt_bias", "a_log", "d_skip", "fox_f_b"]
_SMALL_ROWS = 8
_BIG = ["ffn1_w_gate", "ffn1_w_up", "ffn1_w_down", "w_in", "conv_w", "mla_w_uq", "mla_w_ukv", "w_out",
        "ffn2_w_gate", "ffn2_w_up", "ffn2_w_down"]
_NAMES = ["meta", "ffn1_w_gate", "ffn1_w_up", "ffn1_w_down", "ln1_g", "ln1_b", "w_in", "conv_w", "conv_b", "dt_bias",
          "a_log", "d_skip", "ssd_norm_g", "fox_f_b", "mla_q_norm_g", "mla_w_uq", "mla_kv_norm_g", "mla_w_ukv", "w_out",
          "ln2_g", "ln2_b", "ffn2_w_gate", "ffn2_w_up", "ffn2_w_down", "ln3_g", "ln3_b"]


def pack_small(p):
    flat = jnp.concatenate([p[n].astype(F32) for n in _SMALL], axis=1)
    return _pad_cols(flat, _SMALL_ROWS * D).reshape(NL * _SMALL_ROWS, D)


def unpack_small(a, like):
    flat = a.reshape(NL, _SMALL_ROWS * D)
    out, at = {}, 0
    for n in _SMALL:
        out[n] = flat[:, at:at + like[n].shape[1]]
        at += like[n].shape[1]
    return out


_STAGES = {"ffn1": ["ffn1_w_gate", "ffn1_w_up", "ffn1_w_down"],
           "mix": ["w_in", "conv_w", "mla_w_uq", "mla_w_ukv", "w_out"],
           "ffn2": ["ffn2_w_gate", "ffn2_w_up", "ffn2_w_down"]}


_FFN_T = ("ffn1_w_gate", "ffn1_w_up", "ffn2_w_gate", "ffn2_w_up")


def stage_weights(l, stage, g, rep):
    if stage != "mix":
        i = stage[3]
        return {"g" + i: g[f"ffn{i}_w_gate"].reshape(D_FF, D), "u" + i: g[f"ffn{i}_w_up"].reshape(D_FF, D),
                "d" + i: g[f"ffn{i}_w_down"].reshape(D_FF, D),
                "ln1_g" if i == "1" else "ln3_g": rep["ln1_g" if i == "1" else "ln3_g"][l][None, :],
                "ln1_b" if i == "1" else "ln3_b": rep["ln1_b" if i == "1" else "ln3_b"][l][None, :]}
    W = {}
    W["w_in"] = g["w_in"].reshape(D, N_INP)
    W["w_out"] = g["w_out"].reshape(D, D)
    W["wq"], W["wk"], W["wv"] = mla_weights(g["mla_w_uq"], g["mla_w_ukv"])
    W["conv_w"] = _unshard_cols(g["conv_w"])
    for k in ("ln2_g", "ln2_b", "conv_b"):
        W[k] = rep[k][l][None, :]
    W["normg"] = rep["ssd_norm_g"][l][None, :]
    W["qg"] = rep["mla_q_norm_g"][l][None, :]
    W["kvg"] = rep["mla_kv_norm_g"][l][None, :]
    W["dtb"] = _lanes(rep["dt_bias"][l], SM_DT)
    W["alog"] = _lanes(rep["a_log"][l], SM_DT)
    W["dskip"] = _lanes(rep["d_skip"][l], SM_DT)
    W["fb"] = _lanes(rep["fox_f_b"][l], SM_F)
    return W


def small_grads(G):
    return {"ln1_g": G["ln1_g"][0], "ln1_b": G["ln1_b"][0], "ln2_g": G["ln2_g"][0], "ln2_b": G["ln2_b"][0],
            "ln3_g": G["ln3_g"][0], "ln3_b": G["ln3_b"][0], "conv_b": G["conv_b"][0], "ssd_norm_g": G["normg"][0],
            "mla_q_norm_g": G["qg"][0], "mla_kv_norm_g": G["kvg"][0], "dt_bias": G["dtb"][0, :SSD_H],
            "a_log": G["alog"][0, :SSD_H], "d_skip": G["dskip"][0, :SSD_H], "fox_f_b": G["fb"][0, SM_F:SM_F + FOX_H]}


def big_grads(G, stage):
    if stage != "mix":
        i = stage[-1]
        return {f"ffn{i}_w_{k}": G[k[0] + i].reshape(N_DEV, HS, D) for k in ("gate", "up", "down")}
    duq, dukv = mla_weight_grads(G["wq"], G["wk"], G["wv"])
    return {"w_in": G["w_in"].reshape(N_DEV, D // N_DEV, N_INP), "w_out": G["w_out"].reshape(N_DEV, D // N_DEV, D),
            "mla_w_uq": duq, "mla_w_ukv": dukv, "conv_w": _shard_cols(G["conv_w"])}


def kernel(x, meta, ffn1_w_gate, ffn1_w_up, ffn1_w_down, ln1_g, ln1_b, w_in, conv_w, conv_b, dt_bias, a_log, d_skip, ssd_norm_g, fox_f_b, mla_q_norm_g, mla_w_uq, mla_kv_norm_g, mla_w_ukv, w_out, ln2_g, ln2_b, ffn2_w_gate, ffn2_w_up, ffn2_w_down, ln3_g, ln3_b, loss_target, m_meta, m_ffn1_w_gate, m_ffn1_w_up, m_ffn1_w_down, m_ln1_g, m_ln1_b, m_w_in, m_conv_w, m_conv_b, m_dt_bias, m_a_log, m_d_skip, m_ssd_norm_g, m_fox_f_b, m_mla_q_norm_g, m_mla_w_uq, m_mla_kv_norm_g, m_mla_w_ukv, m_w_out, m_ln2_g, m_ln2_b, m_ffn2_w_gate, m_ffn2_w_up, m_ffn2_w_down, m_ln3_g, m_ln3_b, v_meta, v_ffn1_w_gate, v_ffn1_w_up, v_ffn1_w_down, v_ln1_g, v_ln1_b, v_w_in, v_conv_w, v_conv_b, v_dt_bias, v_a_log, v_d_skip, v_ssd_norm_g, v_fox_f_b, v_mla_q_norm_g, v_mla_w_uq, v_mla_kv_norm_g, v_mla_w_ukv, v_w_out, v_ln2_g, v_ln2_b, v_ffn2_w_gate, v_ffn2_w_up, v_ffn2_w_down, v_ln3_g, v_ln3_b):
    vals = (meta, ffn1_w_gate, ffn1_w_up, ffn1_w_down, ln1_g, ln1_b, w_in, conv_w, conv_b, dt_bias, a_log, d_skip, ssd_norm_g, fox_f_b, mla_q_norm_g, mla_w_uq, mla_kv_norm_g, mla_w_ukv, w_out, ln2_g, ln2_b, ffn2_w_gate, ffn2_w_up, ffn2_w_down, ln3_g, ln3_b)
    moms = (m_meta, m_ffn1_w_gate, m_ffn1_w_up, m_ffn1_w_down, m_ln1_g, m_ln1_b, m_w_in, m_conv_w, m_conv_b, m_dt_bias, m_a_log, m_d_skip, m_ssd_norm_g, m_fox_f_b, m_mla_q_norm_g, m_mla_w_uq, m_mla_kv_norm_g, m_mla_w_ukv, m_w_out, m_ln2_g, m_ln2_b, m_ffn2_w_gate, m_ffn2_w_up, m_ffn2_w_down, m_ln3_g, m_ln3_b)
    vars_ = (v_meta, v_ffn1_w_gate, v_ffn1_w_up, v_ffn1_w_down, v_ln1_g, v_ln1_b, v_w_in, v_conv_w, v_conv_b, v_dt_bias, v_a_log, v_d_skip, v_ssd_norm_g, v_fox_f_b, v_mla_q_norm_g, v_mla_w_uq, v_mla_kv_norm_g, v_mla_w_ukv, v_w_out, v_ln2_g, v_ln2_b, v_ffn2_w_gate, v_ffn2_w_up, v_ffn2_w_down, v_ln3_g, v_ln3_b)
    P = dict(zip(_NAMES, vals))
    M = dict(zip(_NAMES, moms))
    V = dict(zip(_NAMES, vars_))
    me = 4 * lax.axis_index("x") + 2 * lax.axis_index("y") + lax.axis_index("c")

    me_arr = me.astype(jnp.int32).reshape(1)
    for n in _FFN_T:
        P[n], M[n], V[n] = (jnp.swapaxes(a[n], 1, 2) for a in (P, M, V))
    src = dict(P)
    src["w_in"] = w_in_to_padded(P["w_in"])
    order = [("meta", 0)] + [(n, l) for l in range(NL) for names in _STAGES.values() for n in names]
    nfirst = 1 + len(_STAGES["ffn1"])

    def place(n, l):
        return place_own(P["meta"][None] if n == "meta" else src[n], l, F32 if n in ("meta", "conv_w") else BF16, me_arr)

    hg_first = gather_start([place(n, l) for n, l in order[:nfirst]], "gather_start_first")
    hg_rest = gather_start([place(n, l) for n, l in order[nfirst:]], "gather_start_rest")
    zone_of = {nl_: ((hg_first, i) if i < nfirst else (hg_rest, i - nfirst)) for i, nl_ in enumerate(order)}
    relays = {}

    def ahead(l, stage, after, min_layer=0):
        if not min_layer <= l < NL:
            return None
        if (l, stage) not in relays:
            zs = [zone_of[("meta", 0)]] if stage == "meta" else [zone_of[(n, l)] for n in _STAGES[stage]]
            hg, idxs = zs[0][0], [i for _, i in zs]
            relays[(l, stage)] = (hg, idxs, gather_relay(hg, idxs, f"gather_relay_{l}_{stage}", after))
        return relays[(l, stage)][2]["token"]

    def arrived(l, stage, after):
        ahead(l, stage, after)
        hg, idxs, rl = relays[(l, stage)]
        return gather_wait(hg, rl, idxs, f"gather_wait_{l}_{stage}", after)

    meta_full = _unshard_cols(arrived(0, "meta", hg_rest["token"])[0])

    def getw(l, stage, after):
        return stage_weights(l, stage, dict(zip(_STAGES[stage], arrived(l, stage, after))), P)

    sent = {}

    def emit(l, stage, G):
        bg = big_grads(G, stage)
        sent[(l, stage)] = exchange_start("scatter", [bg[n] for n in _STAGES[stage]], f"scatter_start_{l}_{stage}")
        return sent[(l, stage)]["token"]

    loss, gx, gmeta, grads = local_step(x[0], loss_target[0], meta_full, getw, emit, ahead)

    small = jnp.concatenate([pack_small({n: jnp.stack([small_grads(g)[n] for g in grads]) for n in _SMALL}), gmeta,
                             jnp.pad(loss, ((0, 7), (0, D - 1)))], axis=0)
    hs = exchange_start("gather", [place_own(small[None], 0, F32, me_arr)], "small_start")

    out = {}
    after = hs["token"]
    for stage in ("ffn2", "mix", "ffn1"):
        names = _STAGES[stage]
        whole = [l for l in range(NL - 1, -1, -1) if (l, stage) != (0, "ffn1")]
        got = {l: exchange_wait(sent[(l, stage)], list(range(len(names))), f"scatter_wait_{l}_{stage}", after) for l in whole}
        for i, n in enumerate(names):
            one = {l: (got[l][0][i], got[l][1][i]) for l in whole}
            for l in set(range(NL)) - set(whole):
                s_, r_ = exchange_wait(sent[(l, stage)], [i], f"scatter_wait_{l}_{stage}_{i}", after)
                one[l] = (s_[0], r_[0])
            own = [one[l][0] for l in range(NL)]
            recv = [one[l][1] for l in range(NL)]
            if n == "w_in":
                g = jnp.stack([w_in_from_padded(sum_slots(recv[l], own[l], me_arr)) for l in range(NL)])
                out[n] = (g,) + adamw(P[n], M[n], V[n], g=g)
            else:
                out[n] = adamw(P[n], M[n], V[n], recv=recv, own=own, me_arr=me_arr)
                if n in _FFN_T:
                    out[n] = tuple(jnp.swapaxes(a, 1, 2) for a in out[n])
            after = out[n][1]
    gsmall = sum_slots(exchange_wait(hs, [0], "small_wait", after)[1][0])
    gm = lax.dynamic_slice(gsmall[NL * _SMALL_ROWS:], (0, me * (D // N_DEV)), (N_META, D // N_DEV))
    out["meta"] = (gm,) + adamw(P["meta"], M["meta"], V["meta"], g=gm)
    gs = gsmall[:NL * _SMALL_ROWS]
    sd, sm_, sv_ = adamw(pack_small(P), pack_small(M), pack_small(V), g=gs)
    ups = [unpack_small(a, P) for a in (gs, sd, sm_, sv_)]
    for n in _SMALL:
        out[n] = tuple(u[n] for u in ups)

    loss_all = gsmall[NL * _SMALL_ROWS + N_META, 0]
    flat = [loss_all, gx[None]]
    for k in range(4):
        flat += [out[n][k] for n in _NAMES]
    return tuple(flat)
```

```python
import functools

import jax
import jax.numpy as jnp
from jax import lax
from jax.experimental import pallas as pl
from jax.experimental.pallas import tpu as pltpu

F32, BF16 = jnp.float32, jnp.bfloat16
HI = lax.Precision.HIGHEST

N_DEV = 8
D = 1024
NL = 2
N_META = 16
BLK = 128
PAD = BLK - N_META
D_FF = 2816
HS = D_FF // N_DEV
SSD_H, SSD_P, SSD_N, SSD_G = 8, 64, 64, 2
SSD_D = SSD_H * SSD_P
CONV_K = 4
CONV_D = SSD_D + 2 * SSD_G * SSD_N
FOX_H, FOX_DH = 4, 64
MLA_H, MLA_QL, MLA_KVL, MLA_NOPE, MLA_ROPE, MLA_V = 4, 256, 128, 64, 32, 64
N_IN = 2476
C_Z, C_XBC, C_FQ, C_FK, C_FV, C_CQ, C_CKV, C_SM, N_INP = 0, 512, 1280, 1536, 1792, 2048, 2304, 2432, 2560
SM_DT, SM_F, SM_KR = 0, 8, 64
ALPHA = (2 * NL) ** 0.25
EPS = 1e-5
NEG = -1e30
LR, B1, B2, AEPS, WD, STEP = 0.001, 0.9, 0.999, 1e-08, 0.01, 10
VMEM_MB = 56


def _cp(*sem):
    return pltpu.CompilerParams(dimension_semantics=sem, vmem_limit_bytes=VMEM_MB << 20)


def _nn(a, b):
    return lax.dot_general(a, b, (((1,), (0,)), ((), ())), preferred_element_type=F32)


def _nt(a, b):
    return lax.dot_general(a, b, (((1,), (1,)), ((), ())), preferred_element_type=F32)


def _tn(a, b):
    return lax.dot_general(a, b, (((0,), (0,)), ((), ())), preferred_element_type=F32)


def _nn_hi(a, b):
    return lax.dot_general(a, b, (((1,), (0,)), ((), ())), precision=HI, preferred_element_type=F32)


def _row_tile(t):
    for d in range(640, 15, -16):
        if t % d == 0:
            return d
    raise ValueError(t)


def _sig(x):
    return 1.0 / (1.0 + jnp.exp(-x))


def _tri(lower=True):
    r = lax.broadcasted_iota(jnp.int32, (BLK, BLK), 0)
    c = lax.broadcasted_iota(jnp.int32, (BLK, BLK), 1)
    return (r >= c) if lower else (r <= c)


def build_h0(meta_full, x):
    s = x.shape[0]
    nb = s // BLK + 1

    def body(m_ref, x_ref, h_ref, hb_ref):
        i = pl.program_id(0)

        @pl.when(i == 0)
        def _():
            h = jnp.concatenate([jnp.zeros((PAD, D), F32), m_ref[...]], axis=0)
            h_ref[...] = h
            hb_ref[...] = h.astype(BF16)

        @pl.when(i > 0)
        def _():
            h_ref[...] = x_ref[...]
            hb_ref[...] = x_ref[...].astype(BF16)

    return pl.pallas_call(
        body, name="build_h0", grid=(nb,),
        in_specs=[pl.BlockSpec((N_META, D), lambda i: (0, 0)),
                  pl.BlockSpec((BLK, D), lambda i: (jnp.maximum(i - 1, 0), 0))],
        out_specs=[pl.BlockSpec((BLK, D), lambda i: (i, 0))] * 2,
        out_shape=[jax.ShapeDtypeStruct((nb * BLK, D), F32), jax.ShapeDtypeStruct((nb * BLK, D), BF16)],
        compiler_params=_cp("arbitrary"),
    )(meta_full, x)


FT = 256


def _layer_norm(r, gamma, beta):
    mu = jnp.mean(r, axis=1, keepdims=True)
    xc = r - mu
    var = jnp.mean(xc * xc, axis=1, keepdims=True)
    return xc * lax.rsqrt(var + EPS) * gamma + beta


def ffn_fwd_seq(x, ln_in, wg, wu, wd, ln_out):
    t = x.shape[0]
    f = wg.shape[0]
    nj, nr = f // FT, t // _row_tile(t)
    rc = t // nr
    plain = ln_in is None
    gi, bi = ln_out if plain else ln_in

    def body(x_hbm, gi_ref, bi_ref, go_ref, bo_ref, wg_ref, wu_ref, wd_ref, u_ref, v_ref, r_hbm, yb_hbm,
             acc, hbs, xbuf, sem_in, sem_out):
        j = pl.program_id(0)

        @pl.when(j == 0)
        def _():
            def fetch(k):
                return pltpu.make_async_copy(x_hbm.at[pl.ds(k * rc, rc)], xbuf.at[k % 2], sem_in.at[k % 2])

            fetch(0).start()
            for k in range(nr):
                if k + 1 < nr:
                    fetch(k + 1).start()
                fetch(k).wait()
                h = xbuf[k % 2]
                if not plain:
                    h = _layer_norm(h, gi_ref[...], bi_ref[...])
                acc[k * rc:(k + 1) * rc, :] = ALPHA * h
                hbs[k * rc:(k + 1) * rc, :] = h.astype(BF16)

        def chunk(k, last):
            sl = slice(k * rc, (k + 1) * rc)
            h = hbs[sl, :]
            u = _nt(h, wg_ref[...])
            v = _nt(h, wu_ref[...])
            u_ref[sl, :] = u.astype(BF16)
            v_ref[sl, :] = v.astype(BF16)
            acc[sl, :] += _nn((0.5 * u * _sig(u) * v).astype(BF16), wd_ref[...])
            if not last:
                return []
            rows = pl.ds(k * rc, rc)
            cps = [pltpu.make_async_copy(acc.at[rows], r_hbm.at[rows], sem_out.at[2 * k])]
            cps[0].start()
            hbs[sl, :] = _layer_norm(acc[sl, :], go_ref[...], bo_ref[...]).astype(BF16)
            cps.append(pltpu.make_async_copy(hbs.at[rows], yb_hbm.at[rows], sem_out.at[2 * k + 1]))
            cps[1].start()
            return cps

        @pl.when(j < nj - 1)
        def _():
            for k in range(nr):
                chunk(k, False)

        @pl.when(j == nj - 1)
        def _():
            cps = []
            for k in range(nr):
                cps += chunk(k, True)
            for cp in cps:
                cp.wait()

    vec = pl.BlockSpec((1, D), lambda j: (0, 0))
    wsp = pl.BlockSpec((FT, D), lambda j: (j, 0))
    act = pl.BlockSpec((None, t, FT), lambda j: (j, 0, 0))
    return pl.pallas_call(
        body, name="ffn_fwd_seq", grid=(nj,),
        in_specs=[_ANY, vec, vec, vec, vec, wsp, wsp, wsp],
        out_specs=[act, act, _ANY, _ANY],
        out_shape=[jax.ShapeDtypeStruct((nj, t, FT), BF16), jax.ShapeDtypeStruct((nj, t, FT), BF16),
                   jax.ShapeDtypeStruct((t, D), F32), jax.ShapeDtypeStruct((t, D), BF16)],
        scratch_shapes=[pltpu.VMEM((t, D), F32), pltpu.VMEM((t, D), BF16), pltpu.VMEM((2, rc, D), F32),
                        pltpu.SemaphoreType.DMA((2,)), pltpu.SemaphoreType.DMA((2 * nr,))],
        compiler_params=_cp("arbitrary"),
    )(x, gi, bi, ln_out[0], ln_out[1], wg, wu, wd)


def ffn_bwd_seq(parts, r, gamma, hb, u, v, wg, wu, wd, after=None):
    nj, t, _ = u.shape
    f = nj * FT
    nr = 2 * (t // _row_tile(t))
    rc = t // nr
    nc = t // BLK
    scales = [s for _, s in parts]
    npart = len(parts)
    extra = [] if after is None else [after]

    def body(*refs):
        refs = refs[len(extra):]
        p_hbm, refs = refs[:npart], refs[npart:]
        (r_hbm, g_ref, hb_hbm, u_ref, v_ref, wg_ref, wu_ref, wd_ref, dh_hbm, dwg_ref, dwu_ref, dwd_ref, dg_ref, db_ref,
         dfs, hbt, dft, dhacc, dus, dvs, acs, pbuf, rbuf, hbuf, sems, sem_out) = refs
        j = pl.program_id(0)

        @pl.when(j == 0)
        def _():
            def fetch(c):
                rows = pl.ds(c * BLK, BLK)
                cps = [pltpu.make_async_copy(p_hbm[p].at[rows], pbuf.at[c % 2, p], sems.at[c % 2, p]) for p in range(npart)]
                cps.append(pltpu.make_async_copy(r_hbm.at[rows], rbuf.at[c % 2], sems.at[c % 2, npart]))
                cps.append(pltpu.make_async_copy(hb_hbm.at[rows], hbuf.at[c % 2], sems.at[c % 2, npart + 1]))
                return cps

            for cp in fetch(0):
                cp.start()
            dg = jnp.zeros((1, D), F32)
            db = jnp.zeros((1, D), F32)
            for c in range(nc):
                if c + 1 < nc:
                    for cp in fetch(c + 1):
                        cp.start()
                for cp in fetch(c):
                    cp.wait()
                sl = slice(c * BLK, (c + 1) * BLK)
                dy = scales[0] * pbuf[c % 2, 0]
                for p in range(1, npart):
                    dy += scales[p] * pbuf[c % 2, p]
                rr = rbuf[c % 2]
                xc = rr - jnp.mean(rr, axis=1, keepdims=True)
                rstd = lax.rsqrt(jnp.mean(xc * xc, axis=1, keepdims=True) + EPS)
                xh = xc * rstd
                dxh = dy * g_ref[...]
                dr = rstd * (dxh - jnp.mean(dxh, axis=1, keepdims=True) - xh * jnp.mean(dxh * xh, axis=1, keepdims=True))
                dg += jnp.sum(dy * xh, axis=0, keepdims=True)
                db += jnp.sum(dy, axis=0, keepdims=True)
                dhacc[sl, :] = ALPHA * dr
                dfc = (0.5 * dr).astype(BF16)
                dfs[sl, :] = dfc
                dft[:, sl] = dfc.T
                hbt[:, sl] = hbuf[c % 2].T
            dg_ref[...] = dg
            db_ref[...] = db

        for k in range(nr):
            sl = slice(k * rc, (k + 1) * rc)
            da = _nt(dfs[sl, :], wd_ref[...])
            uu = u_ref[sl, :].astype(F32)
            vv = v_ref[sl, :].astype(F32)
            sg = _sig(uu)
            du = (da * vv * (sg * (1.0 + uu * (1.0 - sg)))).astype(BF16)
            dv = (da * uu * sg).astype(BF16)
            dus[sl, :] = du
            dvs[sl, :] = dv
            acs[sl, :] = (uu * sg * vv).astype(BF16)
            dhacc[sl, :] += _nn(du, wg_ref[...]) + _nn(dv, wu_ref[...])
        @pl.when(j == nj - 1)
        def _():
            pltpu.make_async_copy(dhacc, dh_hbm, sem_out.at[0]).start()

        dwg_ref[...] = _nn(hbt[...], dus[...]).astype(BF16).T
        dwu_ref[...] = _nn(hbt[...], dvs[...]).astype(BF16).T
        dwd_ref[...] = _nn(dft[...], acs[...]).astype(BF16).T

        @pl.when(j == nj - 1)
        def _():
            pltpu.make_async_copy(dhacc, dh_hbm, sem_out.at[0]).wait()

    vec = pl.BlockSpec((1, D), lambda j: (0, 0))
    wsp = pl.BlockSpec((FT, D), lambda j: (j, 0))
    act = pl.BlockSpec((None, t, FT), lambda j: (j, 0, 0))
    return pl.pallas_call(
        body, name="ffn_bwd_seq", grid=(nj,),
        in_specs=[_ANY] * (len(extra) + npart + 1) + [vec, _ANY, act, act, wsp, wsp, wsp],
        out_specs=[_ANY, wsp, wsp, wsp, vec, vec],
        out_shape=[jax.ShapeDtypeStruct((t, D), F32)] + [jax.ShapeDtypeStruct((f, D), BF16)] * 3
        + [jax.ShapeDtypeStruct((1, D), F32)] * 2,
        scratch_shapes=[pltpu.VMEM((t, D), BF16), pltpu.VMEM((D, t), BF16), pltpu.VMEM((D, t), BF16),
                        pltpu.VMEM((t, D), F32), pltpu.VMEM((t, FT), BF16), pltpu.VMEM((t, FT), BF16),
                        pltpu.VMEM((t, FT), BF16), pltpu.VMEM((2, npart, BLK, D), F32), pltpu.VMEM((2, BLK, D), F32),
                        pltpu.VMEM((2, BLK, D), BF16), pltpu.SemaphoreType.DMA((2, npart + 2)),
                        pltpu.SemaphoreType.DMA((1,))],
        compiler_params=_cp("arbitrary"),
    )(*extra, *[p for p, _ in parts], r, gamma, hb, u, v, wg, wu, wd)


def mm_res_ln(pieces, b, x, ln_in, ln_out):
    t = x.shape[0]
    k = b.shape[0]
    tm = _row_tile(t)
    na = len(pieces)

    def body(*refs):
        b_ref, x_ref, gi_ref, bi_ref, go_ref, bo_ref, r_ref, yb_ref = refs[na:]
        a = jnp.concatenate([ref[...] for ref in refs[:na]], axis=1)
        r = ALPHA * _layer_norm(x_ref[...], gi_ref[...], bi_ref[...]) + _nn(a, b_ref[...])
        r_ref[...] = r
        yb_ref[...] = _layer_norm(r, go_ref[...], bo_ref[...]).astype(BF16)

    row = pl.BlockSpec((tm, D), lambda i: (i, 0))
    vec = pl.BlockSpec((1, D), lambda i: (0, 0))
    return pl.pallas_call(
        body, name="mm_res_ln", grid=(t // tm,),
        in_specs=[pl.BlockSpec((tm, p.shape[1]), lambda i: (i, 0)) for p in pieces]
        + [pl.BlockSpec((k, D), lambda i: (0, 0)), row, vec, vec, vec, vec],
        out_specs=[row, row],
        out_shape=[jax.ShapeDtypeStruct((t, D), F32), jax.ShapeDtypeStruct((t, D), BF16)],
        compiler_params=_cp("arbitrary"),
    )(*pieces, b, x, ln_in[0], ln_in[1], ln_out[0], ln_out[1])


def mm_nn(a, b):
    t, k = a.shape
    n = tn = b.shape[1]
    tm = _row_tile(t)

    def body(a_ref, b_ref, o_ref):
        o_ref[...] = _nn(a_ref[...], b_ref[...])

    return pl.pallas_call(
        body, name="mm_nn", grid=(t // tm, n // tn),
        in_specs=[pl.BlockSpec((tm, k), lambda i, j: (i, 0)), pl.BlockSpec((k, tn), lambda i, j: (0, j))],
        out_specs=pl.BlockSpec((tm, tn), lambda i, j: (i, j)),
        out_shape=jax.ShapeDtypeStruct((t, n), F32),
        compiler_params=_cp("arbitrary", "arbitrary"),
    )(a, b)


def oproj_bwd(dy, r, gamma, pieces, w_out, after=None):
    t = r.shape[0]
    tm = _row_tile(t)
    nt = t // tm
    extra = [] if after is None else [after]
    na = len(pieces)

    def body(*refs):
        refs = refs[len(extra):]
        dy_ref, r_ref, g_ref = refs[:3]
        w_ref, dr_ref, dm_ref, dw_ref, dg_ref, db_ref, acc = refs[3 + na:]
        mix = jnp.concatenate([ref[...] for ref in refs[3:3 + na]], axis=1)
        i = pl.program_id(0)
        dy = dy_ref[...]
        rr = r_ref[...]
        xc = rr - jnp.mean(rr, axis=1, keepdims=True)
        rstd = lax.rsqrt(jnp.mean(xc * xc, axis=1, keepdims=True) + EPS)
        xh = xc * rstd
        dxh = dy * g_ref[...]
        dr = rstd * (dxh - jnp.mean(dxh, axis=1, keepdims=True) - xh * jnp.mean(dxh * xh, axis=1, keepdims=True))
        dr_ref[...] = dr
        drb = dr.astype(BF16)
        dm_ref[...] = _nt(drb, w_ref[...])
        dw = _tn(mix, drb)
        dg = jnp.sum(dy * xh, axis=0, keepdims=True)
        db = jnp.sum(dy, axis=0, keepdims=True)

        @pl.when(i == 0)
        def _():
            acc[...] = dw
            dg_ref[...] = dg
            db_ref[...] = db

        @pl.when(i > 0)
        def _():
            acc[...] += dw
            dg_ref[...] += dg
            db_ref[...] += db

        @pl.when(i == nt - 1)
        def _():
            dw_ref[...] = acc[...].astype(BF16)

    row = pl.BlockSpec((tm, D), lambda i: (i, 0))
    vec = pl.BlockSpec((1, D), lambda i: (0, 0))
    mat = pl.BlockSpec((D, D), lambda i: (0, 0))
    return pl.pallas_call(
        body, name="oproj_bwd", grid=(nt,),
        in_specs=[_ANY] * len(extra) + [row, row, vec] + [pl.BlockSpec((tm, p.shape[1]), lambda i: (i, 0)) for p in pieces]
        + [mat],
        out_specs=[row, row, mat, vec, vec],
        out_shape=[jax.ShapeDtypeStruct((t, D), F32), jax.ShapeDtypeStruct((t, D), F32), jax.ShapeDtypeStruct((D, D), BF16),
                   jax.ShapeDtypeStruct((1, D), F32), jax.ShapeDtypeStruct((1, D), F32)],
        scratch_shapes=[pltpu.VMEM((D, D), F32)],
        compiler_params=_cp("arbitrary"),
    )(*extra, dy, r, gamma, *pieces, w_out)


def proj_bwd(pieces, hb, w_in):
    t = hb.shape[0]
    n = w_in.shape[1]
    tm = _row_tile(t)
    nt = t // tm
    widths = [p.shape[1] for p in pieces]
    starts = [sum(widths[:k]) for k in range(len(pieces))]
    assert sum(widths) == n

    def body(*refs):
        p_refs = refs[:len(pieces)]
        h_ref, w_ref, dh_ref, dw_ref, acc = refs[len(pieces):]
        i = pl.program_id(0)
        h = h_ref[...]
        dh = jnp.zeros((tm, D), F32)
        dws = []
        for p_ref, c0, w in zip(p_refs, starts, widths):
            pb = p_ref[...].astype(BF16)
            dh += _nt(pb, w_ref[:, c0:c0 + w])
            dws.append(_tn(h, pb))
        dh_ref[...] = dh

        @pl.when(i == 0)
        def _():
            for dw, c0, w in zip(dws, starts, widths):
                acc[:, c0:c0 + w] = dw

        @pl.when(i > 0)
        def _():
            for dw, c0, w in zip(dws, starts, widths):
                acc[:, c0:c0 + w] += dw

        @pl.when(i == nt - 1)
        def _():
            dw_ref[...] = acc[...].astype(BF16)

    mat = pl.BlockSpec((D, n), lambda i: (0, 0))
    return pl.pallas_call(
        body, name="proj_bwd", grid=(nt,),
        in_specs=[pl.BlockSpec((tm, w), lambda i: (i, 0)) for w in widths] + [pl.BlockSpec((tm, D), lambda i: (i, 0)), mat],
        out_specs=[pl.BlockSpec((tm, D), lambda i: (i, 0)), mat],
        out_shape=[jax.ShapeDtypeStruct((t, D), F32), jax.ShapeDtypeStruct((D, n), BF16)],
        scratch_shapes=[pltpu.VMEM((D, n), F32)],
        compiler_params=_cp("arbitrary"),
    )(*pieces, hb, w_in)


def loss_head(r, ln, target):
    t = r.shape[0]
    nb = t // BLK

    def body(r_ref, g_ref, b_ref, t_ref, dy_ref, l_ref):
        i = pl.program_id(0)

        @pl.when(i == 0)
        def _():
            dy_ref[...] = jnp.zeros_like(dy_ref)
            l_ref[...] = jnp.zeros_like(l_ref)

        @pl.when(i > 0)
        def _():
            err = _layer_norm(r_ref[...], g_ref[...], b_ref[...]) - t_ref[...]
            dy_ref[...] = err * (1.0 / D)
            l_ref[...] += (0.5 / D) * jnp.sum(err * err, keepdims=True)

    vec = pl.BlockSpec((1, D), lambda i: (0, 0))
    return pl.pallas_call(
        body, name="loss_head", grid=(nb,),
        in_specs=[pl.BlockSpec((BLK, D), lambda i: (i, 0)), vec, vec,
                  pl.BlockSpec((BLK, D), lambda i: (jnp.maximum(i - 1, 0), 0))],
        out_specs=[pl.BlockSpec((BLK, D), lambda i: (i, 0)), pl.BlockSpec((1, 1), lambda i: (0, 0))],
        out_shape=[jax.ShapeDtypeStruct((t, D), F32), jax.ShapeDtypeStruct((1, 1), F32)],
        compiler_params=_cp("arbitrary"),
    )(r, ln[0], ln[1], target)


def split_dh0(dh0, after=None):
    t = dh0.shape[0]
    nb = t // BLK
    extra = [] if after is None else [after]

    def body(*refs):
        a_ref, gx_ref, gm_ref = refs[len(extra):]
        i = pl.program_id(0)
        tot = a_ref[...]

        @pl.when(i == 0)
        def _():
            gm_ref[...] = tot[PAD:, :]

        @pl.when(i > 0)
        def _():
            gx_ref[...] = tot

    blk = pl.BlockSpec((BLK, D), lambda i: (i, 0))
    return pl.pallas_call(
        body, name="split_dh0", grid=(nb,),
        in_specs=[_ANY] * len(extra) + [blk],
        out_specs=[pl.BlockSpec((BLK, D), lambda i: (jnp.maximum(i - 1, 0), 0)),
                   pl.BlockSpec((N_META, D), lambda i: (0, 0))],
        out_shape=[jax.ShapeDtypeStruct((t - BLK, D), F32), jax.ShapeDtypeStruct((N_META, D), F32)],
        compiler_params=_cp("arbitrary"),
    )(*extra, dh0)


def _valid_rows(nrows, first_row):
    return (first_row + lax.broadcasted_iota(jnp.int32, (nrows, 1), 0)) >= PAD


def conv_fwd(proj, conv_w, conv_b):
    t = proj.shape[0]
    c0 = C_XBC // BLK

    def body(x_ref, w_ref, b_ref, o_ref):
        ok = _valid_rows(t, 0)
        x = jnp.where(ok, x_ref[...], 0.0)
        w = w_ref[...]
        acc = b_ref[...] + w[CONV_K - 1:CONV_K, :] * x
        for s in range(1, CONV_K):
            acc += w[CONV_K - 1 - s:CONV_K - s, :] * pltpu.roll(x, s, 0)
        o_ref[...] = jnp.where(ok, acc * _sig(acc), 0.0)

    return pl.pallas_call(
        body, name="conv_fwd", grid=(CONV_D // BLK,),
        in_specs=[pl.BlockSpec((t, BLK), lambda j: (0, c0 + j)),
                  pl.BlockSpec((CONV_K, BLK), lambda j: (0, j)), pl.BlockSpec((1, BLK), lambda j: (0, j))],
        out_specs=pl.BlockSpec((t, BLK), lambda j: (0, j)),
        out_shape=jax.ShapeDtypeStruct((t, CONV_D), F32),
        compiler_params=_cp("arbitrary"),
    )(proj, conv_w, conv_b)


def conv_bwd(dxa, proj, conv_w, conv_b):
    t = proj.shape[0]
    c0 = C_XBC // BLK

    def body(d_ref, x_ref, w_ref, b_ref, dx_ref, dw_ref, db_ref):
        ok = _valid_rows(t, 0)
        x = jnp.where(ok, x_ref[...], 0.0)
        w = w_ref[...]
        xs = [x] + [pltpu.roll(x, s, 0) for s in range(1, CONV_K)]
        acc = b_ref[...] + w[CONV_K - 1:CONV_K, :] * x
        for s in range(1, CONV_K):
            acc += w[CONV_K - 1 - s:CONV_K - s, :] * xs[s]
        sg = _sig(acc)
        dxc = jnp.where(ok, d_ref[...] * (sg * (1.0 + acc * (1.0 - sg))), 0.0)
        db_ref[...] = jnp.sum(dxc, axis=0, keepdims=True)
        dw_ref[...] = jnp.concatenate(
            [jnp.sum(dxc * xs[CONV_K - 1 - k], axis=0, keepdims=True) for k in range(CONV_K)], axis=0)
        dx = w[CONV_K - 1:CONV_K, :] * dxc
        for s in range(1, CONV_K):
            dx += w[CONV_K - 1 - s:CONV_K - s, :] * pltpu.roll(dxc, t - s, 0)
        dx_ref[...] = jnp.where(ok, dx, 0.0)

    col = pl.BlockSpec((t, BLK), lambda j: (0, j))
    return pl.pallas_call(
        body, name="conv_bwd", grid=(CONV_D // BLK,),
        in_specs=[col, pl.BlockSpec((t, BLK), lambda j: (0, c0 + j)),
                  pl.BlockSpec((CONV_K, BLK), lambda j: (0, j)), pl.BlockSpec((1, BLK), lambda j: (0, j))],
        out_specs=[col, pl.BlockSpec((CONV_K, BLK), lambda j: (0, j)), pl.BlockSpec((1, BLK), lambda j: (0, j))],
        out_shape=[jax.ShapeDtypeStruct((t, CONV_D), F32), jax.ShapeDtypeStruct((CONV_K, CONV_D), F32),
                   jax.ShapeDtypeStruct((1, CONV_D), F32)],
        compiler_params=_cp("arbitrary"),
    )(dxa, proj, conv_w, conv_b)


def _softplus(x):
    return jnp.maximum(x, 0.0) + jnp.log(1.0 + jnp.exp(-jnp.abs(x)))


GW = SSD_D // SSD_G
HPG = SSD_H // SSD_G


def _head_expand():
    r = lax.broadcasted_iota(jnp.int32, (BLK, SSD_D), 0)
    c = lax.broadcasted_iota(jnp.int32, (BLK, SSD_D), 1)
    rt = lax.broadcasted_iota(jnp.int32, (SSD_D, BLK), 0)
    ct = lax.broadcasted_iota(jnp.int32, (SSD_D, BLK), 1)
    return (c // SSD_P == r).astype(F32), (rt // SSD_P == ct).astype(F32)


def _ssd_chunk(xa, sm, dtb, alog, dskip, ok, sp):
    e, et = _head_expand()
    dt = jnp.where(ok, _softplus(sm + dtb), 0.0)
    amat = -jnp.exp(alog)
    tri = _tri()
    ac = _nn_hi(tri.astype(F32), dt * amat)
    act = ac.T
    ace, dte, dse = _nn_hi(ac, e), _nn_hi(dt, e), _nn_hi(dskip, e)
    laste = ace[BLK - 1:BLK, :]
    ee, dece, gle = jnp.exp(ace), jnp.exp(laste - ace), jnp.exp(laste)
    xs = xa[:, :SSD_D]
    xdt = xs * dte
    decx = dece * xdt
    xdtb = xdt.astype(BF16)
    d = dict(e=e, et=et, dt=dt, amat=amat, tri=tri, ac=ac, act=act, dte=dte, dse=dse, ee=ee, dece=dece, gle=gle, xs=xs,
             xdt=xdt, xdtb=xdtb, decx=decx, bg=[], cg=[], cb=[], yo=[], seg=[], m=[], new_s=[])
    ys = []
    for g in range(SSD_G):
        cols = slice(GW * g, GW * (g + 1))
        bg = xa[:, SSD_D + SSD_N * g:SSD_D + SSD_N * (g + 1)].astype(BF16)
        cg = xa[:, SSD_D + SSD_G * SSD_N + SSD_N * g:SSD_D + SSD_G * SSD_N + SSD_N * (g + 1)].astype(BF16)
        spg = sp[:, cols]
        sloc = _tn(bg, decx[:, cols].astype(BF16))
        yo = _nn(cg, spg.astype(BF16)) * ee[:, cols]
        cb = _nt(cg, bg)
        d["new_s"].append(gle[:, cols] * spg + sloc)
        yds = []
        for h in range(HPG * g, HPG * (g + 1)):
            seg = jnp.where(tri, jnp.exp(jnp.minimum(ac[:, h:h + 1] - act[h:h + 1, :], 0.0)), 0.0)
            m = cb * seg
            yds.append(_nn(m.astype(BF16), xdtb[:, SSD_P * h:SSD_P * (h + 1)]))
            d["seg"].append(seg)
            d["m"].append(m)
        ys.append(jnp.concatenate(yds, axis=1) + yo)
        for k, val in (("bg", bg), ("cg", cg), ("cb", cb), ("yo", yo)):
            d[k].append(val)
    d["y"] = jnp.concatenate(ys, axis=1) + dse * xs
    return d


def ssd_fwd(xa, proj, dtb, alog, dskip, normg):
    t = xa.shape[0]
    nb = t // BLK
    gw = SSD_D // SSD_G

    def body(xa_ref, z_ref, sm_ref, dtb_ref, al_ref, ds_ref, ng_ref, y_ref, sp_ref, st):
        c = pl.program_id(0)

        @pl.when(c == 0)
        def _():
            st[...] = jnp.zeros_like(st)

        ok = _valid_rows(BLK, c * BLK)
        sp = st[...]
        sp_ref[...] = sp
        d = _ssd_chunk(xa_ref[...], sm_ref[...], dtb_ref[...], al_ref[...], ds_ref[...], ok, sp)
        st[...] = jnp.concatenate(d["new_s"], axis=1)
        y = d["y"]
        z = z_ref[...]
        yg = y * (z * _sig(z))
        outs = []
        for g in range(SSD_G):
            v = yg[:, gw * g:gw * (g + 1)]
            outs.append(v * lax.rsqrt(jnp.mean(v * v, axis=1, keepdims=True) + EPS))
        y_ref[...] = (jnp.concatenate(outs, axis=1) * ng_ref[...]).astype(BF16)

    vec = pl.BlockSpec((1, BLK), lambda c: (0, 0))
    return pl.pallas_call(
        body, name="ssd_fwd", grid=(nb,),
        in_specs=[pl.BlockSpec((BLK, CONV_D), lambda c: (c, 0)),
                  pl.BlockSpec((BLK, SSD_D), lambda c: (c, C_Z // SSD_D)),
                  pl.BlockSpec((BLK, BLK), lambda c: (c, C_SM // BLK)),
                  vec, vec, vec, pl.BlockSpec((1, SSD_D), lambda c: (0, 0))],
        out_specs=[pl.BlockSpec((BLK, SSD_D), lambda c: (c, 0)),
                   pl.BlockSpec((None, SSD_N, SSD_D), lambda c: (c, 0, 0))],
        out_shape=[jax.ShapeDtypeStruct((t, SSD_D), BF16), jax.ShapeDtypeStruct((nb, SSD_N, SSD_D), F32)],
        scratch_shapes=[pltpu.VMEM((SSD_N, SSD_D), F32)],
        compiler_params=_cp("arbitrary"),
    )(xa, proj, proj, dtb, alog, dskip, normg)


def _lane_put(col, lane):
    li = lax.broadcasted_iota(jnp.int32, (col.shape[0], BLK), 1)
    return jnp.where(li == lane, col, 0.0)


def ssd_bwd(dmix, xa, proj, sprev, dtb, alog, dskip, normg):
    t = xa.shape[0]
    nb = t // BLK
    gw = SSD_D // SSD_G
    rev = lambda c: nb - 1 - c

    def body(dy_ref, xa_ref, z_ref, sm_ref, sp_ref, dtb_ref, al_ref, ds_ref, ng_ref,
             dxa_ref, dz_ref, dsm_ref, dng_ref, dds_ref, dal_ref, ddtb_ref, dst):
        c = pl.program_id(0)

        @pl.when(c == 0)
        def _():
            dst[...] = jnp.zeros_like(dst)
            dng_ref[...] = jnp.zeros_like(dng_ref)
            dds_ref[...] = jnp.zeros_like(dds_ref)
            dal_ref[...] = jnp.zeros_like(dal_ref)
            ddtb_ref[...] = jnp.zeros_like(ddtb_ref)

        ok = _valid_rows(BLK, rev(c) * BLK)
        sm = sm_ref[...]
        sp = sp_ref[...]
        d = _ssd_chunk(xa_ref[...], sm, dtb_ref[...], al_ref[...], ds_ref[...], ok, sp)
        dt, amat, ac, act, tri, et, xs, xdt = (d[k] for k in ("dt", "amat", "ac", "act", "tri", "et", "xs", "xdt"))
        rowi = lax.broadcasted_iota(jnp.int32, (BLK, 1), 0)
        y = d["y"]
        z = z_ref[...]
        sgz = _sig(z)
        siluz = z * sgz
        yg = y * siluz
        dout = dy_ref[...]
        ng = ng_ref[...]
        dygs, xhs = [], []
        for g in range(SSD_G):
            v = yg[:, gw * g:gw * (g + 1)]
            rr = lax.rsqrt(jnp.mean(v * v, axis=1, keepdims=True) + EPS)
            xh = v * rr
            dxh = dout[:, gw * g:gw * (g + 1)] * ng[:, gw * g:gw * (g + 1)]
            dygs.append(rr * (dxh - xh * jnp.mean(dxh * xh, axis=1, keepdims=True)))
            xhs.append(xh)
        dyg = jnp.concatenate(dygs, axis=1)
        dng_ref[...] += jnp.sum(dout * jnp.concatenate(xhs, axis=1), axis=0, keepdims=True)
        dy = dyg * siluz
        dz_ref[...] = dyg * y * (sgz * (1.0 + z * (1.0 - sgz)))

        triu = _tri(lower=False)
        dyb = dy.astype(BF16)
        dsn = dst[...]
        dds_ref[...] += _nn_hi(jnp.sum(dy * xs, axis=0, keepdims=True), et)
        dac_all = _nn_hi(dy * jnp.concatenate(d["yo"], axis=1), et)
        dyo = (dy * d["ee"]).astype(BF16)
        gl = jnp.exp(ac[BLK - 1:BLK, :])
        dlast = _nn_hi(jnp.sum(dsn * sp, axis=0, keepdims=True), et) * gl
        bds, db_g, dc_g, dxdt_i, new_dst = [], [], [], [], []
        for g in range(SSD_G):
            cols = slice(GW * g, GW * (g + 1))
            bg, cg = d["bg"][g], d["cg"][g]
            dsng = dsn[:, cols].astype(BF16)
            dc = _nt(dyo[:, cols], sp[:, cols].astype(BF16))
            new_dst.append(_tn(cg, dyo[:, cols]) + d["gle"][:, cols] * dsn[:, cols])
            bds.append(_nn(bg, dsng))
            db = _nt(d["decx"][:, cols].astype(BF16), dsng)
            cbt = _nt(bg, cg)
            dcb = jnp.zeros((BLK, BLK), F32)
            for h in range(HPG * g, HPG * (g + 1)):
                hc = slice(SSD_P * h, SSD_P * (h + 1))
                dm = _nt(dyb[:, hc], d["xdtb"][:, hc])
                dcb += dm * d["seg"][h]
                w = dm * d["m"][h]
                dac_all += _lane_put(jnp.sum(w, axis=1, keepdims=True) - jnp.sum(w.T, axis=1, keepdims=True), h)
                segt = jnp.where(triu, jnp.exp(jnp.minimum(act[h:h + 1, :] - ac[:, h:h + 1], 0.0)), 0.0)
                dxdt_i.append(_nn((cbt * segt).astype(BF16), dyb[:, hc]))
            dcbb = dcb.astype(BF16)
            dc_g.append(dc + _nn(dcbb, bg))
            db_g.append(db + _tn(dcbb, cg))
        dst[...] = jnp.concatenate(new_dst, axis=1)
        bds = jnp.concatenate(bds, axis=1)
        tdec = jnp.exp(ac[BLK - 1:BLK, :] - ac) * _nn_hi(xdt * bds, et)
        dlast += jnp.sum(tdec, axis=0, keepdims=True)
        dac_all += jnp.where(rowi == BLK - 1, dlast, 0.0) - tdec
        dxdt = d["dece"] * bds + jnp.concatenate(dxdt_i, axis=1)
        da = _nn_hi(triu.astype(F32), dac_all)
        ddt = _nn_hi(dxdt * xs, et) + da * amat
        dal_ref[...] += jnp.sum(da * dt, axis=0, keepdims=True) * amat
        ddtr = jnp.where(ok, ddt * _sig(sm + dtb_ref[...]), 0.0)
        ddtb_ref[...] += jnp.sum(ddtr, axis=0, keepdims=True)
        dsm_ref[...] = ddtr
        dxs = d["dse"] * dy + dxdt * d["dte"]
        dxa_ref[...] = jnp.where(ok, jnp.concatenate([dxs] + db_g + dc_g, axis=1), 0.0)

    vec = pl.BlockSpec((1, BLK), lambda c: (0, 0))
    nvec = pl.BlockSpec((1, SSD_D), lambda c: (0, 0))
    return pl.pallas_call(
        body, name="ssd_bwd", grid=(nb,),
        in_specs=[pl.BlockSpec((BLK, SSD_D), lambda c: (rev(c), 0)),
                  pl.BlockSpec((BLK, CONV_D), lambda c: (rev(c), 0)),
                  pl.BlockSpec((BLK, SSD_D), lambda c: (rev(c), C_Z // SSD_D)),
                  pl.BlockSpec((BLK, BLK), lambda c: (rev(c), C_SM // BLK)),
                  pl.BlockSpec((None, SSD_N, SSD_D), lambda c: (rev(c), 0, 0)),
                  vec, vec, vec, nvec],
        out_specs=[pl.BlockSpec((BLK, CONV_D), lambda c: (rev(c), 0)),
                   pl.BlockSpec((BLK, SSD_D), lambda c: (rev(c), 0)),
                   pl.BlockSpec((BLK, BLK), lambda c: (rev(c), 0)),
                   nvec, vec, vec, vec],
        out_shape=[jax.ShapeDtypeStruct((t, CONV_D), F32), jax.ShapeDtypeStruct((t, SSD_D), F32),
                   jax.ShapeDtypeStruct((t, BLK), F32), jax.ShapeDtypeStruct((1, SSD_D), F32),
                   jax.ShapeDtypeStruct((1, BLK), F32), jax.ShapeDtypeStruct((1, BLK), F32),
                   jax.ShapeDtypeStruct((1, BLK), F32)],
        scratch_shapes=[pltpu.VMEM((SSD_N, SSD_D), F32)],
        compiler_params=_cp("arbitrary"),
    )(dmix, xa, proj, proj, sprev, dtb, alog, dskip, normg)


def _segments(nb, fine):
    if fine:
        cuts = list(range(0, nb, 2)) + [nb]
    else:
        cuts = sorted({0, nb} | {max(1, round(nb * f)) for f in (0.3, 0.53, 0.77)})
    return list(zip(cuts[:-1], cuts[1:]))


def attn_fwd(q, k, v, qcol, kcol, vcol, nh, dq, dv, scale, c_col=None, c_row=None, lane0=0):
    t = q.shape[0]
    tq = BLK
    use_bias = c_col is not None

    def body(*refs):
        if use_bias:
            q_ref, k_ref, v_ref, cc_ref, cr_ref, o_ref, l_ref = refs
        else:
            q_ref, k_ref, v_ref, o_ref, l_ref = refs
        i = pl.program_id(0)
        rowg = i * tq + lax.broadcasted_iota(jnp.int32, (tq, 1), 0)

        def tile(tk):
            col = lax.broadcasted_iota(jnp.int32, (1, tk), 1)
            mask = (col <= rowg) & (col >= PAD)
            outs = []
            lse = jnp.zeros((tq, BLK), F32)
            for h in range(nh):
                s = _nt(q_ref[:, dq * h:dq * (h + 1)].astype(BF16), k_ref[0:tk, dq * h:dq * (h + 1)].astype(BF16)) * scale
                if use_bias:
                    s = s + (cc_ref[:, lane0 + h:lane0 + h + 1] - cr_ref[h:h + 1, 0:tk])
                s = jnp.where(mask, s, NEG)
                m = jnp.max(s, axis=1, keepdims=True)
                p = jnp.exp(s - m)
                l = jnp.sum(p, axis=1, keepdims=True)
                outs.append(_nn(p.astype(BF16), v_ref[0:tk, dv * h:dv * (h + 1)].astype(BF16)) / l)
                lse += _lane_put(m + jnp.log(l), h)
            o_ref[...] = jnp.concatenate(outs, axis=1).astype(BF16)
            l_ref[...] = lse.T[0:8, :]

        for t0, t1 in _segments(t // tq, True):
            pl.when((i >= t0) & (i < t1))(functools.partial(tile, t1 * BLK))

    in_specs = [pl.BlockSpec((tq, nh * dq), lambda i: (i, qcol)),
                pl.BlockSpec((t, nh * dq), lambda i: (0, kcol)),
                pl.BlockSpec((t, nh * dv), lambda i: (0, vcol))]
    args = [q, k, v]
    if use_bias:
        in_specs += [pl.BlockSpec((tq, BLK), lambda i: (i, 0)), pl.BlockSpec((8, t), lambda i: (0, 0))]
        args += [c_col, c_row]
    return pl.pallas_call(
        body, name="attn_fwd", grid=(t // tq,),
        in_specs=in_specs,
        out_specs=[pl.BlockSpec((tq, nh * dv), lambda i: (i, 0)), pl.BlockSpec((8, tq), lambda i: (0, i))],
        out_shape=[jax.ShapeDtypeStruct((t, nh * dv), BF16), jax.ShapeDtypeStruct((8, t), F32)],
        compiler_params=_cp("arbitrary"),
    )(*args)


def attn_bwd(q, k, v, do, lse_row, o, qcol, kcol, vcol, docol, ocol, nh, dq, dv, scale, c_col=None, c_row=None, lane0=0):
    t = q.shape[0]
    tq = BLK
    use_bias = c_col is not None
    nq = t // tq

    def body(*refs):
        if use_bias:
            (q_ref, k_ref, v_ref, do_ref, l_ref, o_ref, cc_ref, cr_ref, dq_ref, dk_ref, dv_ref, dcq_ref, dck_ref,
             kt, ckb, dacc) = refs
        else:
            q_ref, k_ref, v_ref, do_ref, l_ref, o_ref, dq_ref, dk_ref, dv_ref, kt = refs
        i = pl.program_id(0)

        @pl.when(i == 0)
        def _():
            kt[...] = k_ref[...].astype(BF16).T
            dk_ref[...] = jnp.zeros_like(dk_ref)
            dv_ref[...] = jnp.zeros_like(dv_ref)
            if use_bias:
                dacc[...] = jnp.zeros_like(dacc)
                for h in range(nh):
                    ckb[h] = jnp.broadcast_to(cc_ref[:, lane0 + h:lane0 + h + 1], (t, BLK))

        qry = i * tq + lax.broadcasted_iota(jnp.int32, (1, tq), 1)
        dot = (do_ref[...].astype(F32) * o_ref[...].astype(F32)).T

        def tile(tk):
            key = lax.broadcasted_iota(jnp.int32, (tk, 1), 0)
            mask = (key <= qry) & (key >= PAD)
            dqts, dcqs = [], []
            for h in range(nh):
                qh = q_ref[:, dq * h:dq * (h + 1)].astype(BF16)
                kh = k_ref[0:tk, dq * h:dq * (h + 1)].astype(BF16)
                vh = v_ref[0:tk, dv * h:dv * (h + 1)].astype(BF16)
                doh = do_ref[:, dv * h:dv * (h + 1)].astype(BF16)
                delta = jnp.sum(dot[dv * h:dv * (h + 1), :], axis=0, keepdims=True)
                st = _nt(kh, qh) * scale
                if use_bias:
                    st = st + (cr_ref[h:h + 1, :] - ckb[h, 0:tk, :])
                pt = jnp.exp(jnp.where(mask, st, NEG) - l_ref[h:h + 1, :])
                dst = pt * (_nt(vh, doh) - delta)
                dsb = dst.astype(BF16)
                dk_ref[0:tk, dq * h:dq * (h + 1)] += _nn(dsb, qh) * scale
                dv_ref[0:tk, dv * h:dv * (h + 1)] += _nn(pt.astype(BF16), doh)
                dqts.append(_nn(kt[dq * h:dq * (h + 1), 0:tk], dsb))
                if use_bias:
                    dcqs.append(jnp.sum(dst, axis=0, keepdims=True))
                    dacc[h, 0:tk, :] += dst
            dq_ref[...] = jnp.concatenate(dqts, axis=0).T * scale
            if use_bias:
                dcq_ref[...] = jnp.concatenate(dcqs + [jnp.zeros((8 - nh, tq), F32)], axis=0)

        for t0, t1 in _segments(nq, not use_bias):
            pl.when((i >= t0) & (i < t1))(functools.partial(tile, t1 * BLK))

        if use_bias:
            @pl.when(i == nq - 1)
            def _():
                lane = lax.broadcasted_iota(jnp.int32, (1, BLK), 1)
                tot = jnp.zeros((t, BLK), F32)
                for h in range(nh):
                    tot += jnp.where(lane == lane0 + h, jnp.sum(dacc[h], axis=1, keepdims=True), 0.0)
                dck_ref[...] = tot

    keys_q = pl.BlockSpec((t, nh * dq), lambda i: (0, 0))
    keys_v = pl.BlockSpec((t, nh * dv), lambda i: (0, 0))
    keys_c = pl.BlockSpec((t, BLK), lambda i: (0, 0))
    qrow = pl.BlockSpec((8, tq), lambda i: (0, i))
    in_specs = [pl.BlockSpec((tq, nh * dq), lambda i: (i, qcol)),
                pl.BlockSpec((t, nh * dq), lambda i: (0, kcol)),
                pl.BlockSpec((t, nh * dv), lambda i: (0, vcol)),
                pl.BlockSpec((tq, nh * dv), lambda i: (i, docol)),
                qrow,
                pl.BlockSpec((tq, nh * dv), lambda i: (i, ocol))]
    args = [q, k, v, do, lse_row, o]
    out_specs = [pl.BlockSpec((tq, nh * dq), lambda i: (i, 0)), keys_q, keys_v]
    out_shape = [jax.ShapeDtypeStruct((t, nh * dq), F32), jax.ShapeDtypeStruct((t, nh * dq), F32),
                 jax.ShapeDtypeStruct((t, nh * dv), F32)]
    scratch = [pltpu.VMEM((nh * dq, t), BF16)]
    if use_bias:
        in_specs += [keys_c, qrow]
        args += [c_col, c_row]
        out_specs += [qrow, keys_c]
        out_shape += [jax.ShapeDtypeStruct((8, t), F32), jax.ShapeDtypeStruct((t, BLK), F32)]
        scratch += [pltpu.VMEM((nh, t, BLK), F32), pltpu.VMEM((nh, t, BLK), F32)]
    return pl.pallas_call(
        body, name="attn_bwd", grid=(nq,),
        in_specs=in_specs, out_specs=out_specs, out_shape=out_shape, scratch_shapes=scratch,
        compiler_params=_cp("arbitrary"),
    )(*args)


def fox_pre(proj, fb):
    t = proj.shape[0]
    nb = t // BLK

    def body(sm_ref, fb_ref, c_ref, cr_ref):
        x = sm_ref[...] + fb_ref[...]
        lane = lax.broadcasted_iota(jnp.int32, (1, BLK), 1)
        keep = _valid_rows(t, 0) & (lane >= SM_F) & (lane < SM_F + FOX_H)
        logf = jnp.where(keep, jnp.minimum(x, 0.0) - jnp.log(1.0 + jnp.exp(-jnp.abs(x))), 0.0)
        tri = _tri().astype(F32)
        carry = jnp.zeros((1, BLK), F32)
        for b in range(nb):
            cb = _nn_hi(tri, logf[b * BLK:(b + 1) * BLK, :]) + carry
            c_ref[b * BLK:(b + 1) * BLK, :] = cb
            carry = cb[BLK - 1:BLK, :]
        cr_ref[...] = c_ref[...].T[SM_F:SM_F + 8, :]

    return pl.pallas_call(
        body, name="fox_pre", grid=(1,),
        in_specs=[pl.BlockSpec((t, BLK), lambda i: (0, C_SM // BLK)), pl.BlockSpec((1, BLK), lambda i: (0, 0))],
        out_specs=[pl.BlockSpec((t, BLK), lambda i: (0, 0)), pl.BlockSpec((8, t), lambda i: (0, 0))],
        out_shape=[jax.ShapeDtypeStruct((t, BLK), F32), jax.ShapeDtypeStruct((8, t), F32)],
        compiler_params=_cp("arbitrary"),
    )(proj, fb)


def fox_pre_bwd(dcq, dck, proj, fb, dsm_in):
    t = proj.shape[0]
    nb = t // BLK

    def body(dcq_ref, dck_ref, sm_ref, fb_ref, din_ref, dsm_ref, dfb_ref, scr):
        triu = _tri(lower=False).astype(F32)
        carry = jnp.zeros((1, BLK), F32)
        scr[...] = jnp.concatenate([jnp.zeros((SM_F, t), F32), dcq_ref[...], jnp.zeros((BLK - SM_F - 8, t), F32)], axis=0).T
        for b in range(nb - 1, -1, -1):
            blk = scr[b * BLK:(b + 1) * BLK, :] - dck_ref[b * BLK:(b + 1) * BLK, :]
            cb = _nn_hi(triu, blk) + carry
            scr[b * BLK:(b + 1) * BLK, :] = cb
            carry = cb[0:1, :]
        x = sm_ref[...] + fb_ref[...]
        lane = lax.broadcasted_iota(jnp.int32, (1, BLK), 1)
        keep = _valid_rows(t, 0) & (lane >= SM_F) & (lane < SM_F + FOX_H)
        df = jnp.where(keep, scr[...] * _sig(-x), 0.0)
        dfb_ref[...] = jnp.sum(df, axis=0, keepdims=True)
        dsm_ref[...] = din_ref[...] + df

    full = pl.BlockSpec((t, BLK), lambda i: (0, 0))
    return pl.pallas_call(
        body, name="fox_pre_bwd", grid=(1,),
        in_specs=[pl.BlockSpec((8, t), lambda i: (0, 0)), full,
                  pl.BlockSpec((t, BLK), lambda i: (0, C_SM // BLK)), pl.BlockSpec((1, BLK), lambda i: (0, 0)), full],
        out_specs=[full, pl.BlockSpec((1, BLK), lambda i: (0, 0))],
        out_shape=[jax.ShapeDtypeStruct((t, BLK), F32), jax.ShapeDtypeStruct((1, BLK), F32)],
        scratch_shapes=[pltpu.VMEM((t, BLK), F32)],
        compiler_params=_cp("arbitrary"),
    )(dcq, dck, proj, fb, dsm_in)


def _swap_rope(x):
    lane = lax.broadcasted_iota(jnp.int32, (1, BLK), 1)
    return jnp.where((lane >= SM_KR) & (lane < SM_KR + 16), pltpu.roll(x, BLK - 16, 1),
                     jnp.where((lane >= SM_KR + 16) & (lane < SM_KR + 32), pltpu.roll(x, 16, 1), 0.0))


def _rms(x, g):
    r = lax.rsqrt(jnp.mean(x * x, axis=1, keepdims=True) + EPS)
    return r, x * r


def mla_pre(proj, qg, kvg, wq, wk, wv, cosq, sinq):
    t = proj.shape[0]
    tm = _row_tile(t)

    def body(cq_ref, ckv_ref, sm_ref, qg_ref, kvg_ref, wq_ref, wk_ref, wv_ref, cos_ref, sin_ref,
             q_ref, k_ref, v_ref, cqn_ref, ckvn_ref):
        cs, sn = cos_ref[...], sin_ref[...]
        _, xh = _rms(cq_ref[...], None)
        cqn = (xh * qg_ref[...]).astype(BF16)
        cqn_ref[...] = cqn
        qraw = _nn(cqn, wq_ref[...])
        qs = []
        for h in range(MLA_H):
            hb = qraw[:, BLK * h:BLK * (h + 1)]
            qs.append(hb * cs + _swap_rope(hb) * sn)
        q_ref[...] = jnp.concatenate(qs, axis=1).astype(BF16)
        _, kh = _rms(ckv_ref[...], None)
        ckvn = (kh * kvg_ref[...]).astype(BF16)
        ckvn_ref[...] = ckvn
        kraw = _nn(ckvn, wk_ref[...])
        v_ref[...] = _nn(ckvn, wv_ref[...]).astype(BF16)
        lane = lax.broadcasted_iota(jnp.int32, (1, BLK), 1)
        kr = sm_ref[...]
        krr = jnp.where((lane >= SM_KR) & (lane < SM_KR + MLA_ROPE), kr * cs + _swap_rope(kr) * sn, 0.0)
        k_ref[...] = jnp.concatenate([kraw[:, BLK * h:BLK * (h + 1)] + krr for h in range(MLA_H)], axis=1).astype(BF16)

    def rows(w, cb):
        return pl.BlockSpec((tm, w), lambda i: (i, cb))

    def whole(a):
        return pl.BlockSpec(a.shape, lambda i: (0, 0))

    return pl.pallas_call(
        body, name="mla_pre", grid=(t // tm,),
        in_specs=[rows(MLA_QL, C_CQ // MLA_QL), rows(MLA_KVL, C_CKV // MLA_KVL), rows(BLK, C_SM // BLK),
                  whole(qg), whole(kvg), whole(wq), whole(wk), whole(wv), rows(BLK, 0), rows(BLK, 0)],
        out_specs=[rows(512, 0), rows(512, 0), rows(256, 0), rows(MLA_QL, 0), rows(MLA_KVL, 0)],
        out_shape=[jax.ShapeDtypeStruct((t, 512), BF16), jax.ShapeDtypeStruct((t, 512), BF16),
                   jax.ShapeDtypeStruct((t, 256), BF16), jax.ShapeDtypeStruct((t, MLA_QL), BF16),
                   jax.ShapeDtypeStruct((t, MLA_KVL), BF16)],
        compiler_params=_cp("arbitrary"),
    )(proj, proj, proj, qg, kvg, wq, wk, wv, cosq, sinq)


def mla_pre_bwd(dq, dk, dv, proj, cqn, ckvn, qg, kvg, wq, wk, wv, cosq, sinq, dsm_in):
    t = proj.shape[0]
    tm = _row_tile(t)

    def body(dq_ref, dk_ref, dv_ref, cq_ref, ckv_ref, cqn_ref, ckvn_ref, qg_ref, kvg_ref, wq_ref, wk_ref, wv_ref,
             cos_ref, sin_ref, din_ref, dcq_ref, dckv_ref, dsm_ref, dwq_ref, dwk_ref, dwv_ref, dqg_ref, dkvg_ref):
        i = pl.program_id(0)

        @pl.when(i == 0)
        def _():
            for r in (dwq_ref, dwk_ref, dwv_ref, dqg_ref, dkvg_ref):
                r[...] = jnp.zeros_like(r)

        cs, sn = cos_ref[...], sin_ref[...]
        lane = lax.broadcasted_iota(jnp.int32, (1, BLK), 1)

        def unrope(dy):
            return dy * cs + _swap_rope(dy * sn)

        dqp = jnp.concatenate([unrope(dq_ref[:, BLK * h:BLK * (h + 1)]) for h in range(MLA_H)], axis=1).astype(BF16)
        dwq_ref[...] += _tn(cqn_ref[...], dqp)
        dcqn = _nt(dqp, wq_ref[...])
        r, xh = _rms(cq_ref[...], None)
        dqg_ref[...] += jnp.sum(dcqn * xh, axis=0, keepdims=True)
        dxh = dcqn * qg_ref[...]
        dcq_ref[...] = r * (dxh - xh * jnp.mean(dxh * xh, axis=1, keepdims=True))

        dkn, dkr = [], jnp.zeros((tm, BLK), F32)
        for h in range(MLA_H):
            blk = dk_ref[:, BLK * h:BLK * (h + 1)]
            dkn.append(jnp.where(lane < MLA_NOPE, blk, 0.0))
            dkr += jnp.where((lane >= SM_KR) & (lane < SM_KR + MLA_ROPE), blk, 0.0)
        dknb = jnp.concatenate(dkn, axis=1).astype(BF16)
        dvb = dv_ref[...].astype(BF16)
        ckvn = ckvn_ref[...]
        dwk_ref[...] += _tn(ckvn, dknb)
        dwv_ref[...] += _tn(ckvn, dvb)
        dckvn = _nt(dknb, wk_ref[...]) + _nt(dvb, wv_ref[...])
        r2, kh = _rms(ckv_ref[...], None)
        dkvg_ref[...] += jnp.sum(dckvn * kh, axis=0, keepdims=True)
        dkh = dckvn * kvg_ref[...]
        dckv_ref[...] = r2 * (dkh - kh * jnp.mean(dkh * kh, axis=1, keepdims=True))
        dsm_ref[...] = din_ref[...] + jnp.where((lane >= SM_KR) & (lane < SM_KR + MLA_ROPE), unrope(dkr), 0.0)

    def rows(w, cb):
        return pl.BlockSpec((tm, w), lambda i: (i, cb))

    def whole(a):
        return pl.BlockSpec(a.shape, lambda i: (0, 0))

    def wshape(a):
        return jax.ShapeDtypeStruct(a.shape, F32)

    return pl.pallas_call(
        body, name="mla_pre_bwd", grid=(t // tm,),
        in_specs=[rows(512, 0), rows(512, 0), rows(256, 0), rows(MLA_QL, C_CQ // MLA_QL), rows(MLA_KVL, C_CKV // MLA_KVL),
                  rows(MLA_QL, 0), rows(MLA_KVL, 0), whole(qg), whole(kvg), whole(wq), whole(wk), whole(wv),
                  rows(BLK, 0), rows(BLK, 0), rows(BLK, 0)],
        out_specs=[rows(MLA_QL, 0), rows(MLA_KVL, 0), rows(BLK, 0), whole(wq), whole(wk), whole(wv), whole(qg), whole(kvg)],
        out_shape=[jax.ShapeDtypeStruct((t, MLA_QL), F32), jax.ShapeDtypeStruct((t, MLA_KVL), F32),
                   jax.ShapeDtypeStruct((t, BLK), F32), wshape(wq), wshape(wk), wshape(wv), wshape(qg), wshape(kvg)],
        compiler_params=_cp("arbitrary"),
    )(dq, dk, dv, proj, proj, cqn, ckvn, qg, kvg, wq, wk, wv, cosq, sinq, dsm_in)


def _slot_sum(me, own, recv_ref):
    gg = own.astype(F32)
    for s in range(N_DEV):
        gg = gg + jnp.where(me == s, 0.0, recv_ref[s].astype(F32))
    return gg


def adamw(w, m, v, g=None, recv=None, own=None, me_arr=None):
    shape = w.shape
    c = shape[-1]
    from_recv = recv is not None
    if not from_recv:
        me_arr = jnp.zeros((1,), jnp.int32)
    nl = len(recv) if from_recv else 1
    rws = w.size // c // nl
    tr = rws
    for d in (1024, 512, 352, 256, 128, 64, 32, 16, 8):
        if rws % d == 0 and d * c * 4 <= (2 << 20):
            tr = d
            break
    nt = rws // tr
    w2, m2, v2 = (a.reshape(nl, rws, c) for a in (w, m, v))
    if from_recv:
        gin = [a.reshape(N_DEV, rws, c) for a in list(recv) + list(own)]
    else:
        gin = [g.reshape(1, rws, c)]

    def body(me_ref, w_ref, m_ref, v_ref, *rest):
        g_refs, outs = rest[:len(gin)], rest[len(gin):]
        if from_recv:
            g_out, outs = outs[0], outs[1:]
            for li in range(nl):
                @pl.when(pl.program_id(0) == li)
                def _(li=li):
                    g_out[...] = _slot_sum(me_ref[0], g_refs[nl + li][...], g_refs[li])
            gg = g_out[...]
        else:
            gg = g_refs[0][...]
        d_ref, nm_ref, nv_ref = outs
        nm = B1 * m_ref[...] + (1.0 - B1) * gg
        nv = B2 * v_ref[...] + (1.0 - B2) * (gg * gg)
        mh = nm / (1.0 - B1 ** STEP)
        vh = nv / (1.0 - B2 ** STEP)
        d_ref[...] = -LR * (mh / (jnp.sqrt(vh) + AEPS) + WD * w_ref[...])
        nm_ref[...] = nm
        nv_ref[...] = nv

    row = pl.BlockSpec((None, tr, c), lambda l, i, me: (l, i, 0))
    if from_recv:
        gspecs = [pl.BlockSpec((N_DEV, tr, c), lambda l, i, me, li=li: (0, jnp.where(l == li, i, 0), 0))
                  for li in range(nl)]
        gspecs += [pl.BlockSpec((None, tr, c), lambda l, i, me, li=li: (me[0], jnp.where(l == li, i, 0), 0))
                   for li in range(nl)]
    else:
        gspecs = [row]
    nout = 4 if from_recv else 3
    outs = pl.pallas_call(
        body, name="adamw",
        grid_spec=pltpu.PrefetchScalarGridSpec(num_scalar_prefetch=1, grid=(nl, nt), in_specs=[row, row, row] + gspecs,
                                               out_specs=[row] * nout),
        out_shape=[jax.ShapeDtypeStruct((nl, rws, c), F32)] * nout,
        compiler_params=_cp("arbitrary", "arbitrary"),
    )(me_arr, w2, m2, v2, *gin)
    return tuple(o.reshape(shape) for o in outs)


def sum_slots(recv, own=None, me_arr=None):
    _, r, c = recv.shape
    if own is None:
        own, me_arr = recv, jnp.zeros((1,), jnp.int32)
        plain = True
    else:
        plain = False

    def body(me_ref, r_ref, own_ref, o_ref):
        if plain:
            gg = r_ref[0].astype(F32)
            for s in range(1, N_DEV):
                gg = gg + r_ref[s].astype(F32)
            o_ref[...] = gg
        else:
            o_ref[...] = _slot_sum(me_ref[0], own_ref[...], r_ref)

    return pl.pallas_call(
        body, name="sum_slots",
        grid_spec=pltpu.PrefetchScalarGridSpec(
            num_scalar_prefetch=1, grid=(1,),
            in_specs=[pl.BlockSpec((N_DEV, r, c), lambda i, me: (0, 0, 0)),
                      pl.BlockSpec((None, r, c), lambda i, me: (me[0], 0, 0))],
            out_specs=pl.BlockSpec((r, c), lambda i, me: (0, 0))),
        out_shape=jax.ShapeDtypeStruct((r, c), F32),
        compiler_params=_cp("arbitrary"),
    )(me_arr, recv, own)


_FLIPS = [(0, 0, 1), (0, 1, 0), (0, 1, 1), (1, 0, 0), (1, 0, 1), (1, 1, 0), (1, 1, 1)]
_ANY = pl.BlockSpec(memory_space=pl.ANY)


def _mesh_place():
    x, y, c = lax.axis_index("x"), lax.axis_index("y"), lax.axis_index("c")
    me = 4 * x + 2 * y + c
    peers = [((x + fx) % 2, (y + fy) % 2, (c + fc) % 2) for fx, fy, fc in _FLIPS]
    return me, peers


def place_own(src, l, dtype, me_arr):
    _, r, c = src.shape
    tr = r
    for d in (512, 352, 256, 128, 64, 32, 16, 8):
        if r % d == 0 and d * c * 4 <= (2 << 20):
            tr = d
            break

    def body(me_ref, s_ref, o_ref):
        o_ref[...] = s_ref[...].astype(dtype)

    return pl.pallas_call(
        body, name="place_own",
        grid_spec=pltpu.PrefetchScalarGridSpec(
            num_scalar_prefetch=1, grid=(r // tr,),
            in_specs=[pl.BlockSpec((None, tr, c), lambda i, me: (l, i, 0))],
            out_specs=pl.BlockSpec((None, tr, c), lambda i, me: (me[0], i, 0))),
        out_shape=jax.ShapeDtypeStruct((N_DEV, r, c), dtype),
        compiler_params=_cp("arbitrary"),
    )(me_arr, src)


_HBM = pl.BlockSpec(memory_space=pltpu.HBM)
_SEMS = pl.BlockSpec(memory_space=pltpu.SEMAPHORE)
_EFFECT = pltpu.SideEffectType.DATAFLOW_SIDE_EFFECTING


def exchange_start(mode, arrays, name, after=None):
    n = len(arrays)
    gather = mode == "gather"
    ns = 0 if gather else n
    zones = list(arrays) if gather else [lax.empty(a.shape, a.dtype) for a in arrays]
    ops = ([] if gather else list(arrays)) + zones
    extra = [] if after is None else [after]

    def body(*refs):
        srcs, lands = refs[:ns], refs[ns:ns + n]
        send_sems, recv_sems = refs[ns + n + len(extra)], refs[ns + n + len(extra) + 1]
        token = refs[-1]
        me, peers = _mesh_place()
        ids = [4 * p[0] + 2 * p[1] + p[2] for p in peers]
        for j in range(n):
            for k in range(N_DEV - 1):
                src = lands[j].at[me] if gather else srcs[j].at[ids[k]]
                pltpu.make_async_remote_copy(src_ref=src, dst_ref=lands[j].at[me],
                                             send_sem=send_sems.at[j * (N_DEV - 1) + k],
                                             recv_sem=recv_sems.at[j * (N_DEV - 1) + k], device_id=peers[k],
                                             device_id_type=pl.DeviceIdType.MESH).start()
        token[...] = jnp.zeros_like(token)

    nsem = n * (N_DEV - 1)
    res = pl.pallas_call(
        body, name=name,
        in_specs=[_HBM] * (ns + n) + [_ANY] * len(extra),
        out_specs=(_SEMS, _SEMS, *[_HBM] * (ns + n), pl.BlockSpec(memory_space=pltpu.VMEM)),
        out_shape=(pltpu.SemaphoreType.DMA((nsem,)), pltpu.SemaphoreType.DMA((nsem,)),
                   *[pltpu.HBM(a.shape, a.dtype) for a in ops], jax.ShapeDtypeStruct((8, BLK), F32)),
        input_output_aliases={i: 2 + i for i in range(ns + n)},
        compiler_params=pltpu.CompilerParams(has_side_effects=_EFFECT),
    )(*[pltpu.with_memory_space_constraint(a, pltpu.HBM) for a in ops], *extra)
    return dict(gather=gather, send=res[0], recv=res[1], srcs=list(res[2:2 + ns]), lands=list(res[2 + ns:2 + ns + n]),
                token=res[-1])


def exchange_wait(hd, idxs, name, after):
    gather = hd["gather"]
    n = len(idxs)
    ns = 0 if gather else n
    ops = ([] if gather else [hd["srcs"][j] for j in idxs]) + [hd["lands"][j] for j in idxs]

    def body(*refs):
        srcs, lands = refs[:ns], refs[ns:ns + n]
        send_sems, recv_sems = refs[ns + n], refs[ns + n + 1]
        me, peers = _mesh_place()
        ids = [4 * p[0] + 2 * p[1] + p[2] for p in peers]
        for p, j in enumerate(idxs):
            for k in range(N_DEV - 1):
                src = lands[p].at[me] if gather else srcs[p].at[ids[k]]
                cp = pltpu.make_async_remote_copy(src_ref=src, dst_ref=lands[p].at[ids[k]],
                                                  send_sem=send_sems.at[j * (N_DEV - 1) + k],
                                                  recv_sem=recv_sems.at[j * (N_DEV - 1) + k], device_id=peers[k],
                                                  device_id_type=pl.DeviceIdType.MESH)
                cp.wait_send()
                cp.wait_recv()

    res = pl.pallas_call(
        body, name=name,
        in_specs=[_HBM] * (ns + n) + [_SEMS, _SEMS, _ANY],
        out_specs=[_HBM] * (ns + n),
        out_shape=[pltpu.HBM(a.shape, a.dtype) for a in ops],
        input_output_aliases={i: i for i in range(ns + n)},
        compiler_params=pltpu.CompilerParams(has_side_effects=_EFFECT),
    )(*ops, hd["send"], hd["recv"], after)
    return list(res[:ns]), list(res[ns:])


def _chip_place():
    x, y, c = lax.axis_index("x"), lax.axis_index("y"), lax.axis_index("c")
    chips = [((x + 1) % 2, y), (x, (y + 1) % 2), ((x + 1) % 2, (y + 1) % 2)]
    ident = lambda p: 4 * p[0] + 2 * p[1] + p[2]
    return dict(me=4 * x + 2 * y + c, sib=(x, y, 1 - c), sib_id=4 * x + 2 * y + 1 - c,
                same=[(cx, cy, c) for cx, cy in chips], same_ids=[ident((cx, cy, c)) for cx, cy in chips],
                other_ids=[ident((cx, cy, 1 - c)) for cx, cy in chips])


def _remote(src, dst, send_sem, recv_sem, dev):
    return pltpu.make_async_remote_copy(src_ref=src, dst_ref=dst, send_sem=send_sem, recv_sem=recv_sem, device_id=dev,
                                        device_id_type=pl.DeviceIdType.MESH)


def gather_start(zones, name):
    n = len(zones)

    def body(*refs):
        lands, send_sems, recv_sems, token = refs[:n], refs[n], refs[n + 1], refs[-1]
        pc = _chip_place()
        for j in range(n):
            own = lands[j].at[pc["me"]]
            for k, dev in enumerate([pc["sib"]] + pc["same"]):
                _remote(own, own, send_sems.at[4 * j + k], recv_sems.at[4 * j + k], dev).start()
        token[...] = jnp.zeros_like(token)

    res = pl.pallas_call(
        body, name=name,
        in_specs=[_HBM] * n,
        out_specs=(_SEMS, _SEMS, *[_HBM] * n, pl.BlockSpec(memory_space=pltpu.VMEM)),
        out_shape=(pltpu.SemaphoreType.DMA((4 * n,)), pltpu.SemaphoreType.DMA((4 * n,)),
                   *[pltpu.HBM(a.shape, a.dtype) for a in zones], jax.ShapeDtypeStruct((8, BLK), F32)),
        input_output_aliases={i: 2 + i for i in range(n)},
        compiler_params=pltpu.CompilerParams(has_side_effects=_EFFECT),
    )(*[pltpu.with_memory_space_constraint(a, pltpu.HBM) for a in zones])
    return dict(send=res[0], recv=res[1], lands=list(res[2:2 + n]), token=res[-1])


def gather_relay(hd, idxs, name, after):
    n = len(idxs)

    def body(*refs):
        lands, send_sems, recv_sems = refs[:n], refs[n], refs[n + 1]
        fsend, frecv, token = refs[n + 3 + n], refs[n + 4 + n], refs[-1]
        pc = _chip_place()
        for p, j in enumerate(idxs):
            for k in range(3):
                _remote(lands[p].at[pc["me"]], lands[p].at[pc["same_ids"][k]], send_sems.at[4 * j + 1 + k],
                        recv_sems.at[4 * j + 1 + k], pc["same"][k]).wait_recv()
        for p in range(n):
            for k in range(3):
                got = lands[p].at[pc["same_ids"][k]]
                _remote(got, got, fsend.at[3 * p + k], frecv.at[3 * p + k], pc["sib"]).start()
        token[...] = jnp.zeros_like(token)

    ops = [hd["lands"][j] for j in idxs]
    res = pl.pallas_call(
        body, name=name,
        in_specs=[_HBM] * n + [_SEMS, _SEMS, _ANY],
        out_specs=(*[_HBM] * n, _SEMS, _SEMS, pl.BlockSpec(memory_space=pltpu.VMEM)),
        out_shape=(*[pltpu.HBM(a.shape, a.dtype) for a in ops], pltpu.SemaphoreType.DMA((3 * n,)),
                   pltpu.SemaphoreType.DMA((3 * n,)), jax.ShapeDtypeStruct((8, BLK), F32)),
        input_output_aliases={i: i for i in range(n)},
        compiler_params=pltpu.CompilerParams(has_side_effects=_EFFECT),
    )(*ops, hd["send"], hd["recv"], after)
    return dict(lands=list(res[:n]), fsend=res[n], frecv=res[n + 1], token=res[-1])


def gather_wait(hd, rl, idxs, name, after):
    n = len(idxs)

    def body(*refs):
        lands, send_sems, recv_sems, fsend, frecv = refs[:n], refs[n], refs[n + 1], refs[n + 2], refs[n + 3]
        pc = _chip_place()
        for p, j in enumerate(idxs):
            own = lands[p].at[pc["me"]]
            for k, dev in enumerate([pc["sib"]] + pc["same"]):
                _remote(own, own, send_sems.at[4 * j + k], recv_sems.at[4 * j + k], dev).wait_send()
            _remote(own, lands[p].at[pc["sib_id"]], send_sems.at[4 * j], recv_sems.at[4 * j], pc["sib"]).wait_recv()
            for k in range(3):
                cp = _remote(lands[p].at[pc["same_ids"][k]], lands[p].at[pc["other_ids"][k]], fsend.at[3 * p + k],
                             frecv.at[3 * p + k], pc["sib"])
                cp.wait_send()
                cp.wait_recv()

    res = pl.pallas_call(
        body, name=name,
        in_specs=[_HBM] * n + [_SEMS, _SEMS, _SEMS, _SEMS, _ANY],
        out_specs=[_HBM] * n,
        out_shape=[pltpu.HBM(a.shape, a.dtype) for a in rl["lands"]],
        input_output_aliases={i: i for i in range(n)},
        compiler_params=pltpu.CompilerParams(has_side_effects=_EFFECT),
    )(*rl["lands"], hd["send"], hd["recv"], rl["fsend"], rl["frecv"], after)
    return list(res)


def _pad_cols(a, n):
    return jnp.pad(a, ((0, 0),) * (a.ndim - 1) + ((0, n - a.shape[-1]),))


def w_in_to_padded(w):
    z = lambda n: jnp.zeros(w.shape[:-1] + (n,), w.dtype)
    return jnp.concatenate([
        w[..., 0:1280], w[..., 1288:2056], w[..., 2060:2316], w[..., 2316:2444],
        w[..., 1280:1288], w[..., 2056:2060], z(SM_KR - SM_F - FOX_H), w[..., 2444:2476], z(BLK - SM_KR - MLA_ROPE)], axis=-1)


def w_in_from_padded(g):
    s = C_SM
    return jnp.concatenate([
        g[..., 0:1280], g[..., s + SM_DT:s + SM_DT + 8], g[..., 1280:2048], g[..., s + SM_F:s + SM_F + 4],
        g[..., 2048:2304], g[..., 2304:2432], g[..., s + SM_KR:s + SM_KR + MLA_ROPE]], axis=-1)


def _unshard_cols(gth):
    n, r, c = gth.shape
    return jnp.transpose(gth, (1, 0, 2)).reshape(r, n * c)


def _shard_cols(full):
    r, nc = full.shape
    return jnp.transpose(full.reshape(r, N_DEV, nc // N_DEV), (1, 0, 2))


def mla_weights(uq_g, ukv_g):
    uq = _unshard_cols(uq_g)
    dqh = MLA_NOPE + MLA_ROPE
    wq = jnp.concatenate([_pad_cols(uq[:, dqh * h:dqh * (h + 1)], BLK) for h in range(MLA_H)], axis=1)
    wk = jnp.concatenate([_pad_cols(ukv_g[2 * h], BLK) for h in range(MLA_H)], axis=1)
    wv = jnp.concatenate([ukv_g[2 * h + 1] for h in range(MLA_H)], axis=1)
    return wq, wk, wv


def mla_weight_grads(dwq, dwk, dwv):
    dqh = MLA_NOPE + MLA_ROPE
    duq = _shard_cols(jnp.concatenate([dwq[:, BLK * h:BLK * h + dqh] for h in range(MLA_H)], axis=1))
    parts = []
    for h in range(MLA_H):
        parts += [dwk[:, BLK * h:BLK * h + MLA_NOPE], dwv[:, MLA_V * h:MLA_V * (h + 1)]]
    return duq, jnp.stack(parts, axis=0)


def rope_tables(t):
    pos = (jnp.arange(t, dtype=jnp.int32) - PAD).astype(F32)
    inv_freq = 1.0 / (10000.0 ** (jnp.arange(0, MLA_ROPE, 2, dtype=F32) / MLA_ROPE))
    ang = pos[:, None] * inv_freq[None, :]
    cos, sin = jnp.cos(ang), jnp.sin(ang)
    one, zero = jnp.ones((t, SM_KR), F32), jnp.zeros((t, SM_KR), F32)
    tail = BLK - SM_KR - MLA_ROPE
    cosq = jnp.concatenate([one, cos, cos, jnp.ones((t, tail), F32)], axis=1)
    sinq = jnp.concatenate([zero, -sin, sin, jnp.zeros((t, tail), F32)], axis=1)
    return cosq, sinq


def _lanes(v, off=0):
    return jnp.pad(v.astype(F32), (off, BLK - off - v.shape[0]))[None, :]


def layer_fwd(x, ln, hb, getw, tabs, ahead):
    sv = {"h0b": hb}
    def behind(vec, tok):
        return vec if tok is None else vec + 0.0 * tok[0:1, 0:1]

    W = dict(getw("ffn1", hb))
    ln1 = (behind(W["ln1_g"], ahead(0, "mix", hb, 1)), W["ln1_b"])
    u, v, r1, h1b = ffn_fwd_seq(x, ln, W["g1"], W["u1"], W["d1"], ln1)
    sv.update(u1=u, v1=v, r1=r1, h1b=h1b)
    W.update(getw("mix", h1b))
    ln2 = (W["ln2_g"], W["ln2_b"])
    proj = mm_nn(h1b, W["w_in"])
    xa = conv_fwd(proj, W["conv_w"], W["conv_b"])
    y_ssd, sprev = ssd_fwd(xa, proj, W["dtb"], W["alog"], W["dskip"], W["normg"])
    c_col, c_row = fox_pre(proj, W["fb"])
    y_fox, lse_f = attn_fwd(proj, proj, proj, C_FQ // 256, C_FK // 256, C_FV // 256, FOX_H, FOX_DH, FOX_DH,
                            FOX_DH ** -0.5, c_col, c_row, SM_F)
    q, k, vv, cqn, ckvn = mla_pre(proj, behind(W["qg"], ahead(0, "ffn2", y_fox)), W["kvg"], W["wq"], W["wk"], W["wv"], *tabs)
    y_mla, lse_m = attn_fwd(q, k, vv, 0, 0, 0, MLA_H, BLK, MLA_V, (MLA_NOPE + MLA_ROPE) ** -0.5)
    r2, h2b = mm_res_ln([y_ssd, y_fox, y_mla], W["w_out"], r1, ln1, ln2)
    sv.update(proj=proj, xa=xa, sprev=sprev, c_col=c_col, c_row=c_row, lse_f=lse_f, q=q, k=k, v=vv, cqn=cqn, ckvn=ckvn,
              lse_m=lse_m, y_ssd=y_ssd, y_fox=y_fox, y_mla=y_mla, r2=r2, h2b=h2b)
    W.update(getw("ffn2", h2b))
    ln3 = (behind(W["ln3_g"], ahead(1, "ffn1", h2b)), W["ln3_b"])
    u, v, r3, h3b = ffn_fwd_seq(r2, ln2, W["g2"], W["u2"], W["d2"], ln3)
    sv.update(u2=u, v2=v, r3=r3, W=W)
    return r3, ln3, h3b, sv


def ffn_bwd(parts, r, gamma, hb_in, u, v, wg, wu, wd, after=None):
    dh, dwg, dwu, dwd, dg, db = ffn_bwd_seq(parts, r, gamma, hb_in, u, v, wg, wu, wd, after)
    return dh, dict(d=dwd, g=dwg, u=dwu, ln_g=dg, ln_b=db)


def layer_bwd(parts, sv, emit, tabs, after):
    G = {}
    W = sv["W"]
    dh2, g2 = ffn_bwd(parts, sv["r3"], W["ln3_g"], sv["h2b"], sv["u2"], sv["v2"], W["g2"], W["u2"], W["d2"], after)
    G.update(g2=g2["g"], u2=g2["u"], d2=g2["d"], ln3_g=g2["ln_g"], ln3_b=g2["ln_b"])
    tok = emit("ffn2", G)
    dr2, dmc, G["w_out"], G["ln2_g"], G["ln2_b"] = oproj_bwd(dh2, sv["r2"], W["ln2_g"], [sv["y_ssd"], sv["y_fox"], sv["y_mla"]], W["w_out"], tok)
    proj = sv["proj"]
    dxa, dz, dsm, G["normg"], G["dskip"], G["alog"], G["dtb"] = ssd_bwd(
        dmc, sv["xa"], proj, sv["sprev"], W["dtb"], W["alog"], W["dskip"], W["normg"])
    dxbc, G["conv_w"], G["conv_b"] = conv_bwd(dxa, proj, W["conv_w"], W["conv_b"])
    dfq, dfk, dfv, dcq, dck = attn_bwd(proj, proj, proj, dmc, sv["lse_f"], sv["y_fox"], C_FQ // 256, C_FK // 256,
                                       C_FV // 256, 2, 0, FOX_H, FOX_DH, FOX_DH, FOX_DH ** -0.5, sv["c_col"], sv["c_row"], SM_F)
    dsm, G["fb"] = fox_pre_bwd(dcq, dck, proj, W["fb"], dsm)
    dq, dk, dv = attn_bwd(sv["q"], sv["k"], sv["v"], dmc, sv["lse_m"], sv["y_mla"], 0, 0, 0, 3, 0, MLA_H, BLK, MLA_V,
                          (MLA_NOPE + MLA_ROPE) ** -0.5)
    dcql, dckv, dsm, G["wq"], G["wk"], G["wv"], G["qg"], G["kvg"] = mla_pre_bwd(
        dq, dk, dv, proj, sv["cqn"], sv["ckvn"], W["qg"], W["kvg"], W["wq"], W["wk"], W["wv"], *tabs, dsm)
    dh1p, G["w_in"] = proj_bwd([dz, dxbc, dfq, dfk, dfv, dcql, dckv, dsm], sv["h1b"], W["w_in"])
    tok = emit("mix", G)
    dh0, g1 = ffn_bwd([(dr2, ALPHA), (dh1p, 1.0)], sv["r1"], W["ln1_g"], sv["h0b"], sv["u1"], sv["v1"],
                      W["g1"], W["u1"], W["d1"], tok)
    G.update(g1=g1["g"], u1=g1["u"], d1=g1["d"], ln1_g=g1["ln_g"], ln1_b=g1["ln_b"])
    tok = emit("ffn1", G)
    return [(dh0, 1.0)], G, tok


def local_step(x, target, meta_full, getw, emit, ahead=lambda l, stage, after, min_layer=0: None):
    t = x.shape[0] + BLK
    tabs = rope_tables(t)
    xr, hb = build_h0(meta_full, x)
    ln = None
    saved = []
    for l in range(NL):
        xr, ln, hb, sv = layer_fwd(xr, ln, hb, functools.partial(getw, l), tabs,
                                   lambda dl, stage, after, min_layer=0, l=l: ahead(l + dl, stage, after, min_layer))
        saved.append(sv)
    dy, loss = loss_head(xr, ln, target)
    parts = [(dy, 1.0)]
    grads = [None] * NL
    tok = None
    for l in range(NL - 1, -1, -1):
        parts, grads[l], tok = layer_bwd(parts, saved[l], functools.partial(emit, l), tabs, tok)
    gx, gmeta = split_dh0(parts[0][0], tok)
    return loss, gx, gmeta, grads


_SMALL = ["ln1_g", "ln1_b", "ln2_g", "ln2_b", "ln3_g", "ln3_b", "conv_b", "ssd_norm_g", "mla_q_norm_g",
          "mla_kv_norm_g", "dt_bias", "a_log", "d_skip", "fox_f_b"]
_SMALL_ROWS = 8
_NAMES = ["meta", "ffn1_w_gate", "ffn1_w_up", "ffn1_w_down", "ln1_g", "ln1_b", "w_in", "conv_w", "conv_b", "dt_bias",
          "a_log", "d_skip", "ssd_norm_g", "fox_f_b", "mla_q_norm_g", "mla_w_uq", "mla_kv_norm_g", "mla_w_ukv", "w_out",
          "ln2_g", "ln2_b", "ffn2_w_gate", "ffn2_w_up", "ffn2_w_down", "ln3_g", "ln3_b"]


def pack_small(p):
    flat = jnp.concatenate([p[n].astype(F32) for n in _SMALL], axis=1)
    return _pad_cols(flat, _SMALL_ROWS * D).reshape(NL * _SMALL_ROWS, D)


def unpack_small(a, like):
    flat = a.reshape(NL, _SMALL_ROWS * D)
    out, at = {}, 0
    for n in _SMALL:
        out[n] = flat[:, at:at + like[n].shape[1]]
        at += like[n].shape[1]
    return out


_STAGES = {"ffn1": ["ffn1_w_gate", "ffn1_w_up", "ffn1_w_down"],
           "mix": ["w_in", "conv_w", "mla_w_uq", "mla_w_ukv", "w_out"],
           "ffn2": ["ffn2_w_gate", "ffn2_w_up", "ffn2_w_down"]}


_FFN_T = ("ffn1_w_gate", "ffn1_w_up", "ffn2_w_gate", "ffn2_w_up")


def stage_weights(l, stage, g, rep):
    if stage != "mix":
        i = stage[3]
        return {"g" + i: g[f"ffn{i}_w_gate"].reshape(D_FF, D), "u" + i: g[f"ffn{i}_w_up"].reshape(D_FF, D),
                "d" + i: g[f"ffn{i}_w_down"].reshape(D_FF, D),
                "ln1_g" if i == "1" else "ln3_g": rep["ln1_g" if i == "1" else "ln3_g"][l][None, :],
                "ln1_b" if i == "1" else "ln3_b": rep["ln1_b" if i == "1" else "ln3_b"][l][None, :]}
    W = {}
    W["w_in"] = g["w_in"].reshape(D, N_INP)
    W["w_out"] = g["w_out"].reshape(D, D)
    W["wq"], W["wk"], W["wv"] = mla_weights(g["mla_w_uq"], g["mla_w_ukv"])
    W["conv_w"] = _unshard_cols(g["conv_w"])
    for k in ("ln2_g", "ln2_b", "conv_b"):
        W[k] = rep[k][l][None, :]
    W["normg"] = rep["ssd_norm_g"][l][None, :]
    W["qg"] = rep["mla_q_norm_g"][l][None, :]
    W["kvg"] = rep["mla_kv_norm_g"][l][None, :]
    W["dtb"] = _lanes(rep["dt_bias"][l], SM_DT)
    W["alog"] = _lanes(rep["a_log"][l], SM_DT)
    W["dskip"] = _lanes(rep["d_skip"][l], SM_DT)
    W["fb"] = _lanes(rep["fox_f_b"][l], SM_F)
    return W


def small_grads(G):
    return {"ln1_g": G["ln1_g"][0], "ln1_b": G["ln1_b"][0], "ln2_g": G["ln2_g"][0], "ln2_b": G["ln2_b"][0],
            "ln3_g": G["ln3_g"][0], "ln3_b": G["ln3_b"][0], "conv_b": G["conv_b"][0], "ssd_norm_g": G["normg"][0],
            "mla_q_norm_g": G["qg"][0], "mla_kv_norm_g": G["kvg"][0], "dt_bias": G["dtb"][0, :SSD_H],
            "a_log": G["alog"][0, :SSD_H], "d_skip": G["dskip"][0, :SSD_H], "fox_f_b": G["fb"][0, SM_F:SM_F + FOX_H]}


def big_grads(G, stage):
    if stage != "mix":
        i = stage[-1]
        return {f"ffn{i}_w_{k}": G[k[0] + i].reshape(N_DEV, HS, D) for k in ("gate", "up", "down")}
    duq, dukv = mla_weight_grads(G["wq"], G["wk"], G["wv"])
    return {"w_in": G["w_in"].reshape(N_DEV, D // N_DEV, N_INP), "w_out": G["w_out"].reshape(N_DEV, D // N_DEV, D),
            "mla_w_uq": duq, "mla_w_ukv": dukv, "conv_w": _shard_cols(G["conv_w"])}


def kernel(x, meta, ffn1_w_gate, ffn1_w_up, ffn1_w_down, ln1_g, ln1_b, w_in, conv_w, conv_b, dt_bias, a_log, d_skip, ssd_norm_g, fox_f_b, mla_q_norm_g, mla_w_uq, mla_kv_norm_g, mla_w_ukv, w_out, ln2_g, ln2_b, ffn2_w_gate, ffn2_w_up, ffn2_w_down, ln3_g, ln3_b, loss_target, m_meta, m_ffn1_w_gate, m_ffn1_w_up, m_ffn1_w_down, m_ln1_g, m_ln1_b, m_w_in, m_conv_w, m_conv_b, m_dt_bias, m_a_log, m_d_skip, m_ssd_norm_g, m_fox_f_b, m_mla_q_norm_g, m_mla_w_uq, m_mla_kv_norm_g, m_mla_w_ukv, m_w_out, m_ln2_g, m_ln2_b, m_ffn2_w_gate, m_ffn2_w_up, m_ffn2_w_down, m_ln3_g, m_ln3_b, v_meta, v_ffn1_w_gate, v_ffn1_w_up, v_ffn1_w_down, v_ln1_g, v_ln1_b, v_w_in, v_conv_w, v_conv_b, v_dt_bias, v_a_log, v_d_skip, v_ssd_norm_g, v_fox_f_b, v_mla_q_norm_g, v_mla_w_uq, v_mla_kv_norm_g, v_mla_w_ukv, v_w_out, v_ln2_g, v_ln2_b, v_ffn2_w_gate, v_ffn2_w_up, v_ffn2_w_down, v_ln3_g, v_ln3_b):
    vals = (meta, ffn1_w_gate, ffn1_w_up, ffn1_w_down, ln1_g, ln1_b, w_in, conv_w, conv_b, dt_bias, a_log, d_skip, ssd_norm_g, fox_f_b, mla_q_norm_g, mla_w_uq, mla_kv_norm_g, mla_w_ukv, w_out, ln2_g, ln2_b, ffn2_w_gate, ffn2_w_up, ffn2_w_down, ln3_g, ln3_b)
    moms = (m_meta, m_ffn1_w_gate, m_ffn1_w_up, m_ffn1_w_down, m_ln1_g, m_ln1_b, m_w_in, m_conv_w, m_conv_b, m_dt_bias, m_a_log, m_d_skip, m_ssd_norm_g, m_fox_f_b, m_mla_q_norm_g, m_mla_w_uq, m_mla_kv_norm_g, m_mla_w_ukv, m_w_out, m_ln2_g, m_ln2_b, m_ffn2_w_gate, m_ffn2_w_up, m_ffn2_w_down, m_ln3_g, m_ln3_b)
    vars_ = (v_meta, v_ffn1_w_gate, v_ffn1_w_up, v_ffn1_w_down, v_ln1_g, v_ln1_b, v_w_in, v_conv_w, v_conv_b, v_dt_bias, v_a_log, v_d_skip, v_ssd_norm_g, v_fox_f_b, v_mla_q_norm_g, v_mla_w_uq, v_mla_kv_norm_g, v_mla_w_ukv, v_w_out, v_ln2_g, v_ln2_b, v_ffn2_w_gate, v_ffn2_w_up, v_ffn2_w_down, v_ln3_g, v_ln3_b)
    P = dict(zip(_NAMES, vals))
    M = dict(zip(_NAMES, moms))
    V = dict(zip(_NAMES, vars_))
    me = 4 * lax.axis_index("x") + 2 * lax.axis_index("y") + lax.axis_index("c")

    me_arr = me.astype(jnp.int32).reshape(1)
    for n in _FFN_T:
        P[n], M[n], V[n] = (jnp.swapaxes(a[n], 1, 2) for a in (P, M, V))
    src = dict(P)
    src["w_in"] = w_in_to_padded(P["w_in"])
    order = [("meta", 0)] + [(n, l) for l in range(NL) for names in _STAGES.values() for n in names]
    nfirst = 1 + len(_STAGES["ffn1"]) + len(_STAGES["mix"])

    def place(n, l):
        return place_own(P["meta"][None] if n == "meta" else src[n], l, F32 if n in ("meta", "conv_w") else BF16, me_arr)

    hg_first = gather_start([place(n, l) for n, l in order[:nfirst]], "gather_start_first")
    hg_rest = gather_start([place(n, l) for n, l in order[nfirst:]], "gather_start_rest")
    zone_of = {nl_: ((hg_first, i) if i < nfirst else (hg_rest, i - nfirst)) for i, nl_ in enumerate(order)}
    relays = {}

    def ahead(l, stage, after, min_layer=0):
        if not min_layer <= l < NL:
            return None
        if (l, stage) not in relays:
            zs = [zone_of[("meta", 0)]] if stage == "meta" else [zone_of[(n, l)] for n in _STAGES[stage]]
            hg, idxs = zs[0][0], [i for _, i in zs]
            relays[(l, stage)] = (hg, idxs, gather_relay(hg, idxs, f"gather_relay_{l}_{stage}", after))
        return relays[(l, stage)][2]["token"]

    def arrived(l, stage, after):
        ahead(l, stage, after)
        hg, idxs, rl = relays[(l, stage)]
        return gather_wait(hg, rl, idxs, f"gather_wait_{l}_{stage}", after)

    meta_full = _unshard_cols(arrived(0, "meta", hg_rest["token"])[0])

    def getw(l, stage, after):
        return stage_weights(l, stage, dict(zip(_STAGES[stage], arrived(l, stage, after))), P)

    sent = {}

    def emit(l, stage, G):
        bg = big_grads(G, stage)
        sent[(l, stage)] = exchange_start("scatter", [bg[n] for n in _STAGES[stage]], f"scatter_start_{l}_{stage}")
        return sent[(l, stage)]["token"]

    loss, gx, gmeta, grads = local_step(x[0], loss_target[0], meta_full, getw, emit, ahead)

    small = jnp.concatenate([pack_small({n: jnp.stack([small_grads(g)[n] for g in grads]) for n in _SMALL}), gmeta,
                             jnp.pad(loss, ((0, 7), (0, D - 1)))], axis=0)
    hs = exchange_start("gather", [place_own(small[None], 0, F32, me_arr)], "small_start")

    out = {}
    after = hs["token"]
    for stage in ("ffn2", "mix", "ffn1"):
        names = _STAGES[stage]
        whole = [l for l in range(NL - 1, -1, -1) if (l, stage) != (0, "ffn1")]
        got = {l: exchange_wait(sent[(l, stage)], list(range(len(names))), f"scatter_wait_{l}_{stage}", after) for l in whole}
        for i, n in enumerate(names):
            one = {l: (got[l][0][i], got[l][1][i]) for l in whole}
            for l in set(range(NL)) - set(whole):
                s_, r_ = exchange_wait(sent[(l, stage)], [i], f"scatter_wait_{l}_{stage}_{i}", after)
                one[l] = (s_[0], r_[0])
            own = [one[l][0] for l in range(NL)]
            recv = [one[l][1] for l in range(NL)]
            if n == "w_in":
                g = jnp.stack([w_in_from_padded(sum_slots(recv[l], own[l], me_arr)) for l in range(NL)])
                out[n] = (g,) + adamw(P[n], M[n], V[n], g=g)
            else:
                out[n] = adamw(P[n], M[n], V[n], recv=recv, own=own, me_arr=me_arr)
                if n in _FFN_T:
                    out[n] = tuple(jnp.swapaxes(a, 1, 2) for a in out[n])
            after = out[n][1]
    gsmall = sum_slots(exchange_wait(hs, [0], "small_wait", after)[1][0])
    gm = lax.dynamic_slice(gsmall[NL * _SMALL_ROWS:], (0, me * (D // N_DEV)), (N_META, D // N_DEV))
    out["meta"] = (gm,) + adamw(P["meta"], M["meta"], V["meta"], g=gm)
    gs = gsmall[:NL * _SMALL_ROWS]
    sd, sm_, sv_ = adamw(pack_small(P), pack_small(M), pack_small(V), g=gs)
    ups = [unpack_small(a, P) for a in (gs, sd, sm_, sv_)]
    for n in _SMALL:
        out[n] = tuple(u[n] for u in ups)

    loss_all = gsmall[NL * _SMALL_ROWS + N_META, 0]
    flat = [loss_all, gx[None]]
    for k in range(4):
        flat += [out[n][k] for n in _NAMES]
    return tuple(flat)
```

```python
import functools

import jax
import jax.numpy as jnp
from jax import lax
from jax.experimental import pallas as pl
from jax.experimental.pallas import tpu as pltpu

F32, BF16 = jnp.float32, jnp.bfloat16
HI = lax.Precision.HIGHEST

N_DEV = 8
D = 1024
NL = 2
N_META = 16
BLK = 128
PAD = BLK - N_META
D_FF = 2816
HS = D_FF // N_DEV
SSD_H, SSD_P, SSD_N, SSD_G = 8, 64, 64, 2
SSD_D = SSD_H * SSD_P
CONV_K = 4
CONV_D = SSD_D + 2 * SSD_G * SSD_N
FOX_H, FOX_DH = 4, 64
MLA_H, MLA_QL, MLA_KVL, MLA_NOPE, MLA_ROPE, MLA_V = 4, 256, 128, 64, 32, 64
N_IN = 2476
C_Z, C_XBC, C_FQ, C_FK, C_FV, C_CQ, C_CKV, C_SM, N_INP = 0, 512, 1280, 1536, 1792, 2048, 2304, 2432, 2560
SM_DT, SM_F, SM_KR = 0, 8, 64
ALPHA = (2 * NL) ** 0.25
EPS = 1e-5
NEG = -1e30
LR, B1, B2, AEPS, WD, STEP = 0.001, 0.9, 0.999, 1e-08, 0.01, 10
VMEM_MB = 56


def _cp(*sem):
    return pltpu.CompilerParams(dimension_semantics=sem, vmem_limit_bytes=VMEM_MB << 20)


def _nn(a, b):
    return lax.dot_general(a, b, (((1,), (0,)), ((), ())), preferred_element_type=F32)


def _nt(a, b):
    return lax.dot_general(a, b, (((1,), (1,)), ((), ())), preferred_element_type=F32)


def _tn(a, b):
    return lax.dot_general(a, b, (((0,), (0,)), ((), ())), preferred_element_type=F32)


def _nn_hi(a, b):
    return lax.dot_general(a, b, (((1,), (0,)), ((), ())), precision=HI, preferred_element_type=F32)


def _row_tile(t):
    for d in range(640, 15, -16):
        if t % d == 0:
            return d
    raise ValueError(t)


def _sig(x):
    return 1.0 / (1.0 + jnp.exp(-x))


def _tri(lower=True):
    r = lax.broadcasted_iota(jnp.int32, (BLK, BLK), 0)
    c = lax.broadcasted_iota(jnp.int32, (BLK, BLK), 1)
    return (r >= c) if lower else (r <= c)


def build_h0(meta_full, x):
    s = x.shape[0]
    nb = s // BLK + 1

    def body(m_ref, x_ref, h_ref, hb_ref):
        i = pl.program_id(0)

        @pl.when(i == 0)
        def _():
            h = jnp.concatenate([jnp.zeros((PAD, D), F32), m_ref[...]], axis=0)
            h_ref[...] = h
            hb_ref[...] = h.astype(BF16)

        @pl.when(i > 0)
        def _():
            h_ref[...] = x_ref[...]
            hb_ref[...] = x_ref[...].astype(BF16)

    return pl.pallas_call(
        body, name="build_h0", grid=(nb,),
        in_specs=[pl.BlockSpec((N_META, D), lambda i: (0, 0)),
                  pl.BlockSpec((BLK, D), lambda i: (jnp.maximum(i - 1, 0), 0))],
        out_specs=[pl.BlockSpec((BLK, D), lambda i: (i, 0))] * 2,
        out_shape=[jax.ShapeDtypeStruct((nb * BLK, D), F32), jax.ShapeDtypeStruct((nb * BLK, D), BF16)],
        compiler_params=_cp("arbitrary"),
    )(meta_full, x)


FT = 256


def _layer_norm(r, gamma, beta):
    mu = jnp.mean(r, axis=1, keepdims=True)
    xc = r - mu
    var = jnp.mean(xc * xc, axis=1, keepdims=True)
    return xc * lax.rsqrt(var + EPS) * gamma + beta


def ffn_fwd_seq(x, ln_in, wg, wu, wd, ln_out):
    t = x.shape[0]
    f = wg.shape[0]
    nj, nr = f // FT, t // _row_tile(t)
    rc = t // nr
    plain = ln_in is None
    gi, bi = ln_out if plain else ln_in

    def body(x_hbm, gi_ref, bi_ref, go_ref, bo_ref, wg_ref, wu_ref, wd_ref, u_ref, v_ref, r_hbm, yb_hbm,
             acc, hbs, xbuf, sem_in, sem_out):
        j = pl.program_id(0)

        @pl.when(j == 0)
        def _():
            def fetch(k):
                return pltpu.make_async_copy(x_hbm.at[pl.ds(k * rc, rc)], xbuf.at[k % 2], sem_in.at[k % 2])

            fetch(0).start()
            for k in range(nr):
                if k + 1 < nr:
                    fetch(k + 1).start()
                fetch(k).wait()
                h = xbuf[k % 2]
                if not plain:
                    h = _layer_norm(h, gi_ref[...], bi_ref[...])
                acc[k * rc:(k + 1) * rc, :] = ALPHA * h
                hbs[k * rc:(k + 1) * rc, :] = h.astype(BF16)

        def chunk(k, last):
            sl = slice(k * rc, (k + 1) * rc)
            h = hbs[sl, :]
            u = _nt(h, wg_ref[...])
            v = _nt(h, wu_ref[...])
            u_ref[sl, :] = u.astype(BF16)
            v_ref[sl, :] = v.astype(BF16)
            acc[sl, :] += _nn((0.5 * u * _sig(u) * v).astype(BF16), wd_ref[...])
            if not last:
                return []
            rows = pl.ds(k * rc, rc)
            cps = [pltpu.make_async_copy(acc.at[rows], r_hbm.at[rows], sem_out.at[2 * k])]
            cps[0].start()
            hbs[sl, :] = _layer_norm(acc[sl, :], go_ref[...], bo_ref[...]).astype(BF16)
            cps.append(pltpu.make_async_copy(hbs.at[rows], yb_hbm.at[rows], sem_out.at[2 * k + 1]))
            cps[1].start()
            return cps

        @pl.when(j < nj - 1)
        def _():
            for k in range(nr):
                chunk(k, False)

        @pl.when(j == nj - 1)
        def _():
            cps = []
            for k in range(nr):
                cps += chunk(k, True)
            for cp in cps:
                cp.wait()

    vec = pl.BlockSpec((1, D), lambda j: (0, 0))
    wsp = pl.BlockSpec((FT, D), lambda j: (j, 0))
    act = pl.BlockSpec((None, t, FT), lambda j: (j, 0, 0))
    return pl.pallas_call(
        body, name="ffn_fwd_seq", grid=(nj,),
        in_specs=[_ANY, vec, vec, vec, vec, wsp, wsp, wsp],
        out_specs=[act, act, _ANY, _ANY],
        out_shape=[jax.ShapeDtypeStruct((nj, t, FT), BF16), jax.ShapeDtypeStruct((nj, t, FT), BF16),
                   jax.ShapeDtypeStruct((t, D), F32), jax.ShapeDtypeStruct((t, D), BF16)],
        scratch_shapes=[pltpu.VMEM((t, D), F32), pltpu.VMEM((t, D), BF16), pltpu.VMEM((2, rc, D), F32),
                        pltpu.SemaphoreType.DMA((2,)), pltpu.SemaphoreType.DMA((2 * nr,))],
        compiler_params=_cp("arbitrary"),
    )(x, gi, bi, ln_out[0], ln_out[1], wg, wu, wd)


def ffn_bwd_seq(parts, r, gamma, hb, u, v, wg, wu, wd, after=None):
    nj, t, _ = u.shape
    f = nj * FT
    nr = 2 * (t // _row_tile(t))
    rc = t // nr
    nc = t // BLK
    scales = [s for _, s in parts]
    npart = len(parts)
    extra = [] if after is None else [after]

    def body(*refs):
        refs = refs[len(extra):]
        p_hbm, refs = refs[:npart], refs[npart:]
        (r_hbm, g_ref, hb_hbm, u_ref, v_ref, wg_ref, wu_ref, wd_ref, dh_hbm, dwg_ref, dwu_ref, dwd_ref, dg_ref, db_ref,
         dfs, hbt, dft, dhacc, dus, dvs, acs, pbuf, rbuf, hbuf, sems, sem_out) = refs
        j = pl.program_id(0)

        @pl.when(j == 0)
        def _():
            def fetch(c):
                rows = pl.ds(c * BLK, BLK)
                cps = [pltpu.make_async_copy(p_hbm[p].at[rows], pbuf.at[c % 2, p], sems.at[c % 2, p]) for p in range(npart)]
                cps.append(pltpu.make_async_copy(r_hbm.at[rows], rbuf.at[c % 2], sems.at[c % 2, npart]))
                cps.append(pltpu.make_async_copy(hb_hbm.at[rows], hbuf.at[c % 2], sems.at[c % 2, npart + 1]))
                return cps

            for cp in fetch(0):
                cp.start()
            dg = jnp.zeros((1, D), F32)
            db = jnp.zeros((1, D), F32)
            for c in range(nc):
                if c + 1 < nc:
                    for cp in fetch(c + 1):
                        cp.start()
                for cp in fetch(c):
                    cp.wait()
                sl = slice(c * BLK, (c + 1) * BLK)
                dy = scales[0] * pbuf[c % 2, 0]
                for p in range(1, npart):
                    dy += scales[p] * pbuf[c % 2, p]
                rr = rbuf[c % 2]
                xc = rr - jnp.mean(rr, axis=1, keepdims=True)
                rstd = lax.rsqrt(jnp.mean(xc * xc, axis=1, keepdims=True) + EPS)
                xh = xc * rstd
                dxh = dy * g_ref[...]
                dr = rstd * (dxh - jnp.mean(dxh, axis=1, keepdims=True) - xh * jnp.mean(dxh * xh, axis=1, keepdims=True))
                dg += jnp.sum(dy * xh, axis=0, keepdims=True)
                db += jnp.sum(dy, axis=0, keepdims=True)
                dhacc[sl, :] = ALPHA * dr
                dfc = (0.5 * dr).astype(BF16)
                dfs[sl, :] = dfc
                dft[:, sl] = dfc.T
                hbt[:, sl] = hbuf[c % 2].T
            dg_ref[...] = dg
            db_ref[...] = db

        for k in range(nr):
            sl = slice(k * rc, (k + 1) * rc)
            da = _nt(dfs[sl, :], wd_ref[...])
            uu = u_ref[sl, :].astype(F32)
            vv = v_ref[sl, :].astype(F32)
            sg = _sig(uu)
            du = (da * vv * (sg * (1.0 + uu * (1.0 - sg)))).astype(BF16)
            dv = (da * uu * sg).astype(BF16)
            dus[sl, :] = du
            dvs[sl, :] = dv
            acs[sl, :] = (uu * sg * vv).astype(BF16)
            dhacc[sl, :] += _nn(du, wg_ref[...]) + _nn(dv, wu_ref[...])
        @pl.when(j == nj - 1)
        def _():
            pltpu.make_async_copy(dhacc, dh_hbm, sem_out.at[0]).start()

        dwg_ref[...] = _nn(hbt[...], dus[...]).astype(BF16).T
        dwu_ref[...] = _nn(hbt[...], dvs[...]).astype(BF16).T
        dwd_ref[...] = _nn(dft[...], acs[...]).astype(BF16).T

        @pl.when(j == nj - 1)
        def _():
            pltpu.make_async_copy(dhacc, dh_hbm, sem_out.at[0]).wait()

    vec = pl.BlockSpec((1, D), lambda j: (0, 0))
    wsp = pl.BlockSpec((FT, D), lambda j: (j, 0))
    act = pl.BlockSpec((None, t, FT), lambda j: (j, 0, 0))
    return pl.pallas_call(
        body, name="ffn_bwd_seq", grid=(nj,),
        in_specs=[_ANY] * (len(extra) + npart + 1) + [vec, _ANY, act, act, wsp, wsp, wsp],
        out_specs=[_ANY, wsp, wsp, wsp, vec, vec],
        out_shape=[jax.ShapeDtypeStruct((t, D), F32)] + [jax.ShapeDtypeStruct((f, D), BF16)] * 3
        + [jax.ShapeDtypeStruct((1, D), F32)] * 2,
        scratch_shapes=[pltpu.VMEM((t, D), BF16), pltpu.VMEM((D, t), BF16), pltpu.VMEM((D, t), BF16),
                        pltpu.VMEM((t, D), F32), pltpu.VMEM((t, FT), BF16), pltpu.VMEM((t, FT), BF16),
                        pltpu.VMEM((t, FT), BF16), pltpu.VMEM((2, npart, BLK, D), F32), pltpu.VMEM((2, BLK, D), F32),
                        pltpu.VMEM((2, BLK, D), BF16), pltpu.SemaphoreType.DMA((2, npart + 2)),
                        pltpu.SemaphoreType.DMA((1,))],
        compiler_params=_cp("arbitrary"),
    )(*extra, *[p for p, _ in parts], r, gamma, hb, u, v, wg, wu, wd)


def mm_res_ln(pieces, b, x, ln_in, ln_out):
    t = x.shape[0]
    k = b.shape[0]
    tm = _row_tile(t)
    na = len(pieces)

    def body(*refs):
        b_ref, x_ref, gi_ref, bi_ref, go_ref, bo_ref, r_ref, yb_ref = refs[na:]
        a = jnp.concatenate([ref[...] for ref in refs[:na]], axis=1)
        r = ALPHA * _layer_norm(x_ref[...], gi_ref[...], bi_ref[...]) + _nn(a, b_ref[...])
        r_ref[...] = r
        yb_ref[...] = _layer_norm(r, go_ref[...], bo_ref[...]).astype(BF16)

    row = pl.BlockSpec((tm, D), lambda i: (i, 0))
    vec = pl.BlockSpec((1, D), lambda i: (0, 0))
    return pl.pallas_call(
        body, name="mm_res_ln", grid=(t // tm,),
        in_specs=[pl.BlockSpec((tm, p.shape[1]), lambda i: (i, 0)) for p in pieces]
        + [pl.BlockSpec((k, D), lambda i: (0, 0)), row, vec, vec, vec, vec],
        out_specs=[row, row],
        out_shape=[jax.ShapeDtypeStruct((t, D), F32), jax.ShapeDtypeStruct((t, D), BF16)],
        compiler_params=_cp("arbitrary"),
    )(*pieces, b, x, ln_in[0], ln_in[1], ln_out[0], ln_out[1])


def mm_nn(a, b):
    t, k = a.shape
    n = tn = b.shape[1]
    tm = _row_tile(t)

    def body(a_ref, b_ref, o_ref):
        o_ref[...] = _nn(a_ref[...], b_ref[...])

    return pl.pallas_call(
        body, name="mm_nn", grid=(t // tm, n // tn),
        in_specs=[pl.BlockSpec((tm, k), lambda i, j: (i, 0)), pl.BlockSpec((k, tn), lambda i, j: (0, j))],
        out_specs=pl.BlockSpec((tm, tn), lambda i, j: (i, j)),
        out_shape=jax.ShapeDtypeStruct((t, n), F32),
        compiler_params=_cp("arbitrary", "arbitrary"),
    )(a, b)


def oproj_bwd(dy, r, gamma, pieces, w_out, after=None):
    t = r.shape[0]
    tm = _row_tile(t)
    nt = t // tm
    extra = [] if after is None else [after]
    na = len(pieces)

    def body(*refs):
        refs = refs[len(extra):]
        dy_ref, r_ref, g_ref = refs[:3]
        w_ref, dr_ref, dm_ref, dw_ref, dg_ref, db_ref, acc = refs[3 + na:]
        mix = jnp.concatenate([ref[...] for ref in refs[3:3 + na]], axis=1)
        i = pl.program_id(0)
        dy = dy_ref[...]
        rr = r_ref[...]
        xc = rr - jnp.mean(rr, axis=1, keepdims=True)
        rstd = lax.rsqrt(jnp.mean(xc * xc, axis=1, keepdims=True) + EPS)
        xh = xc * rstd
        dxh = dy * g_ref[...]
        dr = rstd * (dxh - jnp.mean(dxh, axis=1, keepdims=True) - xh * jnp.mean(dxh * xh, axis=1, keepdims=True))
        dr_ref[...] = dr
        drb = dr.astype(BF16)
        dm_ref[...] = _nt(drb, w_ref[...])
        dw = _tn(mix, drb)
        dg = jnp.sum(dy * xh, axis=0, keepdims=True)
        db = jnp.sum(dy, axis=0, keepdims=True)

        @pl.when(i == 0)
        def _():
            acc[...] = dw
            dg_ref[...] = dg
            db_ref[...] = db

        @pl.when(i > 0)
        def _():
            acc[...] += dw
            dg_ref[...] += dg
            db_ref[...] += db

        @pl.when(i == nt - 1)
        def _():
            dw_ref[...] = acc[...].astype(BF16)

    row = pl.BlockSpec((tm, D), lambda i: (i, 0))
    vec = pl.BlockSpec((1, D), lambda i: (0, 0))
    mat = pl.BlockSpec((D, D), lambda i: (0, 0))
    return pl.pallas_call(
        body, name="oproj_bwd", grid=(nt,),
        in_specs=[_ANY] * len(extra) + [row, row, vec] + [pl.BlockSpec((tm, p.shape[1]), lambda i: (i, 0)) for p in pieces]
        + [mat],
        out_specs=[row, row, mat, vec, vec],
        out_shape=[jax.ShapeDtypeStruct((t, D), F32), jax.ShapeDtypeStruct((t, D), F32), jax.ShapeDtypeStruct((D, D), BF16),
                   jax.ShapeDtypeStruct((1, D), F32), jax.ShapeDtypeStruct((1, D), F32)],
        scratch_shapes=[pltpu.VMEM((D, D), F32)],
        compiler_params=_cp("arbitrary"),
    )(*extra, dy, r, gamma, *pieces, w_out)


def proj_bwd(pieces, hb, w_in):
    t = hb.shape[0]
    n = w_in.shape[1]
    tm = _row_tile(t)
    nt = t // tm
    widths = [p.shape[1] for p in pieces]
    starts = [sum(widths[:k]) for k in range(len(pieces))]
    assert sum(widths) == n

    def body(*refs):
        p_refs = refs[:len(pieces)]
        h_ref, w_ref, dh_ref, dw_ref, acc = refs[len(pieces):]
        i = pl.program_id(0)
        h = h_ref[...]
        dh = jnp.zeros((tm, D), F32)
        dws = []
        for p_ref, c0, w in zip(p_refs, starts, widths):
            pb = p_ref[...].astype(BF16)
            dh += _nt(pb, w_ref[:, c0:c0 + w])
            dws.append(_tn(h, pb))
        dh_ref[...] = dh

        @pl.when(i == 0)
        def _():
            for dw, c0, w in zip(dws, starts, widths):
                acc[:, c0:c0 + w] = dw

        @pl.when(i > 0)
        def _():
            for dw, c0, w in zip(dws, starts, widths):
                acc[:, c0:c0 + w] += dw

        @pl.when(i == nt - 1)
        def _():
            dw_ref[...] = acc[...].astype(BF16)

    mat = pl.BlockSpec((D, n), lambda i: (0, 0))
    return pl.pallas_call(
        body, name="proj_bwd", grid=(nt,),
        in_specs=[pl.BlockSpec((tm, w), lambda i: (i, 0)) for w in widths] + [pl.BlockSpec((tm, D), lambda i: (i, 0)), mat],
        out_specs=[pl.BlockSpec((tm, D), lambda i: (i, 0)), mat],
        out_shape=[jax.ShapeDtypeStruct((t, D), F32), jax.ShapeDtypeStruct((D, n), BF16)],
        scratch_shapes=[pltpu.VMEM((D, n), F32)],
        compiler_params=_cp("arbitrary"),
    )(*pieces, hb, w_in)


def loss_head(r, ln, target):
    t = r.shape[0]
    nb = t // BLK

    def body(r_ref, g_ref, b_ref, t_ref, dy_ref, l_ref):
        i = pl.program_id(0)

        @pl.when(i == 0)
        def _():
            dy_ref[...] = jnp.zeros_like(dy_ref)
            l_ref[...] = jnp.zeros_like(l_ref)

        @pl.when(i > 0)
        def _():
            err = _layer_norm(r_ref[...], g_ref[...], b_ref[...]) - t_ref[...]
            dy_ref[...] = err * (1.0 / D)
            l_ref[...] += (0.5 / D) * jnp.sum(err * err, keepdims=True)

    vec = pl.BlockSpec((1, D), lambda i: (0, 0))
    return pl.pallas_call(
        body, name="loss_head", grid=(nb,),
        in_specs=[pl.BlockSpec((BLK, D), lambda i: (i, 0)), vec, vec,
                  pl.BlockSpec((BLK, D), lambda i: (jnp.maximum(i - 1, 0), 0))],
        out_specs=[pl.BlockSpec((BLK, D), lambda i: (i, 0)), pl.BlockSpec((1, 1), lambda i: (0, 0))],
        out_shape=[jax.ShapeDtypeStruct((t, D), F32), jax.ShapeDtypeStruct((1, 1), F32)],
        compiler_params=_cp("arbitrary"),
    )(r, ln[0], ln[1], target)


def split_dh0(dh0, after=None):
    t = dh0.shape[0]
    nb = t // BLK
    extra = [] if after is None else [after]

    def body(*refs):
        a_ref, gx_ref, gm_ref = refs[len(extra):]
        i = pl.program_id(0)
        tot = a_ref[...]

        @pl.when(i == 0)
        def _():
            gm_ref[...] = tot[PAD:, :]

        @pl.when(i > 0)
        def _():
            gx_ref[...] = tot

    blk = pl.BlockSpec((BLK, D), lambda i: (i, 0))
    return pl.pallas_call(
        body, name="split_dh0", grid=(nb,),
        in_specs=[_ANY] * len(extra) + [blk],
        out_specs=[pl.BlockSpec((BLK, D), lambda i: (jnp.maximum(i - 1, 0), 0)),
                   pl.BlockSpec((N_META, D), lambda i: (0, 0))],
        out_shape=[jax.ShapeDtypeStruct((t - BLK, D), F32), jax.ShapeDtypeStruct((N_META, D), F32)],
        compiler_params=_cp("arbitrary"),
    )(*extra, dh0)


def _valid_rows(nrows, first_row):
    return (first_row + lax.broadcasted_iota(jnp.int32, (nrows, 1), 0)) >= PAD


def conv_fwd(proj, conv_w, conv_b):
    t = proj.shape[0]
    c0 = C_XBC // BLK

    def body(x_ref, w_ref, b_ref, o_ref):
        ok = _valid_rows(t, 0)
        x = jnp.where(ok, x_ref[...], 0.0)
        w = w_ref[...]
        acc = b_ref[...] + w[CONV_K - 1:CONV_K, :] * x
        for s in range(1, CONV_K):
            acc += w[CONV_K - 1 - s:CONV_K - s, :] * pltpu.roll(x, s, 0)
        o_ref[...] = jnp.where(ok, acc * _sig(acc), 0.0)

    return pl.pallas_call(
        body, name="conv_fwd", grid=(CONV_D // BLK,),
        in_specs=[pl.BlockSpec((t, BLK), lambda j: (0, c0 + j)),
                  pl.BlockSpec((CONV_K, BLK), lambda j: (0, j)), pl.BlockSpec((1, BLK), lambda j: (0, j))],
        out_specs=pl.BlockSpec((t, BLK), lambda j: (0, j)),
        out_shape=jax.ShapeDtypeStruct((t, CONV_D), F32),
        compiler_params=_cp("arbitrary"),
    )(proj, conv_w, conv_b)


def conv_bwd(dxa, proj, conv_w, conv_b):
    t = proj.shape[0]
    c0 = C_XBC // BLK

    def body(d_ref, x_ref, w_ref, b_ref, dx_ref, dw_ref, db_ref):
        ok = _valid_rows(t, 0)
        x = jnp.where(ok, x_ref[...], 0.0)
        w = w_ref[...]
        xs = [x] + [pltpu.roll(x, s, 0) for s in range(1, CONV_K)]
        acc = b_ref[...] + w[CONV_K - 1:CONV_K, :] * x
        for s in range(1, CONV_K):
            acc += w[CONV_K - 1 - s:CONV_K - s, :] * xs[s]
        sg = _sig(acc)
        dxc = jnp.where(ok, d_ref[...] * (sg * (1.0 + acc * (1.0 - sg))), 0.0)
        db_ref[...] = jnp.sum(dxc, axis=0, keepdims=True)
        dw_ref[...] = jnp.concatenate(
            [jnp.sum(dxc * xs[CONV_K - 1 - k], axis=0, keepdims=True) for k in range(CONV_K)], axis=0)
        dx = w[CONV_K - 1:CONV_K, :] * dxc
        for s in range(1, CONV_K):
            dx += w[CONV_K - 1 - s:CONV_K - s, :] * pltpu.roll(dxc, t - s, 0)
        dx_ref[...] = jnp.where(ok, dx, 0.0)

    col = pl.BlockSpec((t, BLK), lambda j: (0, j))
    return pl.pallas_call(
        body, name="conv_bwd", grid=(CONV_D // BLK,),
        in_specs=[col, pl.BlockSpec((t, BLK), lambda j: (0, c0 + j)),
                  pl.BlockSpec((CONV_K, BLK), lambda j: (0, j)), pl.BlockSpec((1, BLK), lambda j: (0, j))],
        out_specs=[col, pl.BlockSpec((CONV_K, BLK), lambda j: (0, j)), pl.BlockSpec((1, BLK), lambda j: (0, j))],
        out_shape=[jax.ShapeDtypeStruct((t, CONV_D), F32), jax.ShapeDtypeStruct((CONV_K, CONV_D), F32),
                   jax.ShapeDtypeStruct((1, CONV_D), F32)],
        compiler_params=_cp("arbitrary"),
    )(dxa, proj, conv_w, conv_b)


def _softplus(x):
    return jnp.maximum(x, 0.0) + jnp.log(1.0 + jnp.exp(-jnp.abs(x)))


GW = SSD_D // SSD_G
HPG = SSD_H // SSD_G


def _head_expand():
    r = lax.broadcasted_iota(jnp.int32, (BLK, SSD_D), 0)
    c = lax.broadcasted_iota(jnp.int32, (BLK, SSD_D), 1)
    rt = lax.broadcasted_iota(jnp.int32, (SSD_D, BLK), 0)
    ct = lax.broadcasted_iota(jnp.int32, (SSD_D, BLK), 1)
    return (c // SSD_P == r).astype(F32), (rt // SSD_P == ct).astype(F32)


def _ssd_chunk(xa, sm, dtb, alog, dskip, ok, sp):
    e, et = _head_expand()
    dt = jnp.where(ok, _softplus(sm + dtb), 0.0)
    amat = -jnp.exp(alog)
    tri = _tri()
    ac = _nn_hi(tri.astype(F32), dt * amat)
    act = ac.T
    ace, dte, dse = _nn_hi(ac, e), _nn_hi(dt, e), _nn_hi(dskip, e)
    laste = ace[BLK - 1:BLK, :]
    ee, dece, gle = jnp.exp(ace), jnp.exp(laste - ace), jnp.exp(laste)
    xs = xa[:, :SSD_D]
    xdt = xs * dte
    decx = dece * xdt
    xdtb = xdt.astype(BF16)
    d = dict(e=e, et=et, dt=dt, amat=amat, tri=tri, ac=ac, act=act, dte=dte, dse=dse, ee=ee, dece=dece, gle=gle, xs=xs,
             xdt=xdt, xdtb=xdtb, decx=decx, bg=[], cg=[], cb=[], yo=[], seg=[], m=[], new_s=[])
    ys = []
    for g in range(SSD_G):
        cols = slice(GW * g, GW * (g + 1))
        bg = xa[:, SSD_D + SSD_N * g:SSD_D + SSD_N * (g + 1)].astype(BF16)
        cg = xa[:, SSD_D + SSD_G * SSD_N + SSD_N * g:SSD_D + SSD_G * SSD_N + SSD_N * (g + 1)].astype(BF16)
        spg = sp[:, cols]
        sloc = _tn(bg, decx[:, cols].astype(BF16))
        yo = _nn(cg, spg.astype(BF16)) * ee[:, cols]
        cb = _nt(cg, bg)
        d["new_s"].append(gle[:, cols] * spg + sloc)
        yds = []
        for h in range(HPG * g, HPG * (g + 1)):
            seg = jnp.where(tri, jnp.exp(jnp.minimum(ac[:, h:h + 1] - act[h:h + 1, :], 0.0)), 0.0)
            m = cb * seg
            yds.append(_nn(m.astype(BF16), xdtb[:, SSD_P * h:SSD_P * (h + 1)]))
            d["seg"].append(seg)
            d["m"].append(m)
        ys.append(jnp.concatenate(yds, axis=1) + yo)
        for k, val in (("bg", bg), ("cg", cg), ("cb", cb), ("yo", yo)):
            d[k].append(val)
    d["y"] = jnp.concatenate(ys, axis=1) + dse * xs
    return d


def ssd_fwd(xa, proj, dtb, alog, dskip, normg):
    t = xa.shape[0]
    nb = t // BLK
    gw = SSD_D // SSD_G

    def body(xa_ref, z_ref, sm_ref, dtb_ref, al_ref, ds_ref, ng_ref, y_ref, sp_ref, st):
        c = pl.program_id(0)

        @pl.when(c == 0)
        def _():
            st[...] = jnp.zeros_like(st)

        ok = _valid_rows(BLK, c * BLK)
        sp = st[...]
        sp_ref[...] = sp
        d = _ssd_chunk(xa_ref[...], sm_ref[...], dtb_ref[...], al_ref[...], ds_ref[...], ok, sp)
        st[...] = jnp.concatenate(d["new_s"], axis=1)
        y = d["y"]
        z = z_ref[...]
        yg = y * (z * _sig(z))
        outs = []
        for g in range(SSD_G):
            v = yg[:, gw * g:gw * (g + 1)]
            outs.append(v * lax.rsqrt(jnp.mean(v * v, axis=1, keepdims=True) + EPS))
        y_ref[...] = (jnp.concatenate(outs, axis=1) * ng_ref[...]).astype(BF16)

    vec = pl.BlockSpec((1, BLK), lambda c: (0, 0))
    return pl.pallas_call(
        body, name="ssd_fwd", grid=(nb,),
        in_specs=[pl.BlockSpec((BLK, CONV_D), lambda c: (c, 0)),
                  pl.BlockSpec((BLK, SSD_D), lambda c: (c, C_Z // SSD_D)),
                  pl.BlockSpec((BLK, BLK), lambda c: (c, C_SM // BLK)),
                  vec, vec, vec, pl.BlockSpec((1, SSD_D), lambda c: (0, 0))],
        out_specs=[pl.BlockSpec((BLK, SSD_D), lambda c: (c, 0)),
                   pl.BlockSpec((None, SSD_N, SSD_D), lambda c: (c, 0, 0))],
        out_shape=[jax.ShapeDtypeStruct((t, SSD_D), BF16), jax.ShapeDtypeStruct((nb, SSD_N, SSD_D), F32)],
        scratch_shapes=[pltpu.VMEM((SSD_N, SSD_D), F32)],
        compiler_params=_cp("arbitrary"),
    )(xa, proj, proj, dtb, alog, dskip, normg)


def _lane_put(col, lane):
    li = lax.broadcasted_iota(jnp.int32, (col.shape[0], BLK), 1)
    return jnp.where(li == lane, col, 0.0)


def ssd_bwd(dmix, xa, proj, sprev, dtb, alog, dskip, normg):
    t = xa.shape[0]
    nb = t // BLK
    gw = SSD_D // SSD_G
    rev = lambda c: nb - 1 - c

    def body(dy_ref, xa_ref, z_ref, sm_ref, sp_ref, dtb_ref, al_ref, ds_ref, ng_ref,
             dxa_ref, dz_ref, dsm_ref, dng_ref, dds_ref, dal_ref, ddtb_ref, dst):
        c = pl.program_id(0)

        @pl.when(c == 0)
        def _():
            dst[...] = jnp.zeros_like(dst)
            dng_ref[...] = jnp.zeros_like(dng_ref)
            dds_ref[...] = jnp.zeros_like(dds_ref)
            dal_ref[...] = jnp.zeros_like(dal_ref)
            ddtb_ref[...] = jnp.zeros_like(ddtb_ref)

        ok = _valid_rows(BLK, rev(c) * BLK)
        sm = sm_ref[...]
        sp = sp_ref[...]
        d = _ssd_chunk(xa_ref[...], sm, dtb_ref[...], al_ref[...], ds_ref[...], ok, sp)
        dt, amat, ac, act, tri, et, xs, xdt = (d[k] for k in ("dt", "amat", "ac", "act", "tri", "et", "xs", "xdt"))
        rowi = lax.broadcasted_iota(jnp.int32, (BLK, 1), 0)
        y = d["y"]
        z = z_ref[...]
        sgz = _sig(z)
        siluz = z * sgz
        yg = y * siluz
        dout = dy_ref[...]
        ng = ng_ref[...]
        dygs, xhs = [], []
        for g in range(SSD_G):
            v = yg[:, gw * g:gw * (g + 1)]
            rr = lax.rsqrt(jnp.mean(v * v, axis=1, keepdims=True) + EPS)
            xh = v * rr
            dxh = dout[:, gw * g:gw * (g + 1)] * ng[:, gw * g:gw * (g + 1)]
            dygs.append(rr * (dxh - xh * jnp.mean(dxh * xh, axis=1, keepdims=True)))
            xhs.append(xh)
        dyg = jnp.concatenate(dygs, axis=1)
        dng_ref[...] += jnp.sum(dout * jnp.concatenate(xhs, axis=1), axis=0, keepdims=True)
        dy = dyg * siluz
        dz_ref[...] = dyg * y * (sgz * (1.0 + z * (1.0 - sgz)))

        triu = _tri(lower=False)
        dyb = dy.astype(BF16)
        dsn = dst[...]
        dds_ref[...] += _nn_hi(jnp.sum(dy * xs, axis=0, keepdims=True), et)
        dac_all = _nn_hi(dy * jnp.concatenate(d["yo"], axis=1), et)
        dyo = (dy * d["ee"]).astype(BF16)
        gl = jnp.exp(ac[BLK - 1:BLK, :])
        dlast = _nn_hi(jnp.sum(dsn * sp, axis=0, keepdims=True), et) * gl
        bds, db_g, dc_g, dxdt_i, new_dst = [], [], [], [], []
        for g in range(SSD_G):
            cols = slice(GW * g, GW * (g + 1))
            bg, cg = d["bg"][g], d["cg"][g]
            dsng = dsn[:, cols].astype(BF16)
            dc = _nt(dyo[:, cols], sp[:, cols].astype(BF16))
            new_dst.append(_tn(cg, dyo[:, cols]) + d["gle"][:, cols] * dsn[:, cols])
            bds.append(_nn(bg, dsng))
            db = _nt(d["decx"][:, cols].astype(BF16), dsng)
            cbt = _nt(bg, cg)
            dcb = jnp.zeros((BLK, BLK), F32)
            for h in range(HPG * g, HPG * (g + 1)):
                hc = slice(SSD_P * h, SSD_P * (h + 1))
                dm = _nt(dyb[:, hc], d["xdtb"][:, hc])
                dcb += dm * d["seg"][h]
                w = dm * d["m"][h]
                dac_all += _lane_put(jnp.sum(w, axis=1, keepdims=True) - jnp.sum(w.T, axis=1, keepdims=True), h)
                segt = jnp.where(triu, jnp.exp(jnp.minimum(act[h:h + 1, :] - ac[:, h:h + 1], 0.0)), 0.0)
                dxdt_i.append(_nn((cbt * segt).astype(BF16), dyb[:, hc]))
            dcbb = dcb.astype(BF16)
            dc_g.append(dc + _nn(dcbb, bg))
            db_g.append(db + _tn(dcbb, cg))
        dst[...] = jnp.concatenate(new_dst, axis=1)
        bds = jnp.concatenate(bds, axis=1)
        tdec = jnp.exp(ac[BLK - 1:BLK, :] - ac) * _nn_hi(xdt * bds, et)
        dlast += jnp.sum(tdec, axis=0, keepdims=True)
        dac_all += jnp.where(rowi == BLK - 1, dlast, 0.0) - tdec
        dxdt = d["dece"] * bds + jnp.concatenate(dxdt_i, axis=1)
        da = _nn_hi(triu.astype(F32), dac_all)
        ddt = _nn_hi(dxdt * xs, et) + da * amat
        dal_ref[...] += jnp.sum(da * dt, axis=0, keepdims=True) * amat
        ddtr = jnp.where(ok, ddt * _sig(sm + dtb_ref[...]), 0.0)
        ddtb_ref[...] += jnp.sum(ddtr, axis=0, keepdims=True)
        dsm_ref[...] = ddtr
        dxs = d["dse"] * dy + dxdt * d["dte"]
        dxa_ref[...] = jnp.where(ok, jnp.concatenate([dxs] + db_g + dc_g, axis=1), 0.0)

    vec = pl.BlockSpec((1, BLK), lambda c: (0, 0))
    nvec = pl.BlockSpec((1, SSD_D), lambda c: (0, 0))
    return pl.pallas_call(
        body, name="ssd_bwd", grid=(nb,),
        in_specs=[pl.BlockSpec((BLK, SSD_D), lambda c: (rev(c), 0)),
                  pl.BlockSpec((BLK, CONV_D), lambda c: (rev(c), 0)),
                  pl.BlockSpec((BLK, SSD_D), lambda c: (rev(c), C_Z // SSD_D)),
                  pl.BlockSpec((BLK, BLK), lambda c: (rev(c), C_SM // BLK)),
                  pl.BlockSpec((None, SSD_N, SSD_D), lambda c: (rev(c), 0, 0)),
                  vec, vec, vec, nvec],
        out_specs=[pl.BlockSpec((BLK, CONV_D), lambda c: (rev(c), 0)),
                   pl.BlockSpec((BLK, SSD_D), lambda c: (rev(c), 0)),
                   pl.BlockSpec((BLK, BLK), lambda c: (rev(c), 0)),
                   nvec, vec, vec, vec],
        out_shape=[jax.ShapeDtypeStruct((t, CONV_D), F32), jax.ShapeDtypeStruct((t, SSD_D), F32),
                   jax.ShapeDtypeStruct((t, BLK), F32), jax.ShapeDtypeStruct((1, SSD_D), F32),
                   jax.ShapeDtypeStruct((1, BLK), F32), jax.ShapeDtypeStruct((1, BLK), F32),
                   jax.ShapeDtypeStruct((1, BLK), F32)],
        scratch_shapes=[pltpu.VMEM((SSD_N, SSD_D), F32)],
        compiler_params=_cp("arbitrary"),
    )(dmix, xa, proj, proj, sprev, dtb, alog, dskip, normg)


def _segments(nb, fine):
    cuts = list(range(0, nb, 2 if fine else 3)) + [nb]
    return list(zip(cuts[:-1], cuts[1:]))


def attn_fwd(q, k, v, qcol, kcol, vcol, nh, dq, dv, scale, c_col=None, c_row=None, lane0=0):
    t = q.shape[0]
    tq = BLK
    use_bias = c_col is not None

    def body(*refs):
        if use_bias:
            q_ref, k_ref, v_ref, cc_ref, cr_ref, o_ref, l_ref = refs
        else:
            q_ref, k_ref, v_ref, o_ref, l_ref = refs
        i = pl.program_id(0)
        rowg = i * tq + lax.broadcasted_iota(jnp.int32, (tq, 1), 0)

        def tile(tk):
            col = lax.broadcasted_iota(jnp.int32, (1, tk), 1)
            mask = (col <= rowg) & (col >= PAD)
            outs = []
            lse = jnp.zeros((tq, BLK), F32)
            for h in range(nh):
                s = _nt(q_ref[:, dq * h:dq * (h + 1)].astype(BF16), k_ref[0:tk, dq * h:dq * (h + 1)].astype(BF16)) * scale
                if use_bias:
                    s = s + (cc_ref[:, lane0 + h:lane0 + h + 1] - cr_ref[h:h + 1, 0:tk])
                s = jnp.where(mask, s, NEG)
                m = jnp.max(s, axis=1, keepdims=True)
                p = jnp.exp(s - m)
                l = jnp.sum(p, axis=1, keepdims=True)
                outs.append(_nn(p.astype(BF16), v_ref[0:tk, dv * h:dv * (h + 1)].astype(BF16)) / l)
                lse += _lane_put(m + jnp.log(l), h)
            o_ref[...] = jnp.concatenate(outs, axis=1).astype(BF16)
            l_ref[...] = lse.T[0:8, :]

        for t0, t1 in _segments(t // tq, True):
            pl.when((i >= t0) & (i < t1))(functools.partial(tile, t1 * BLK))

    in_specs = [pl.BlockSpec((tq, nh * dq), lambda i: (i, qcol)),
                pl.BlockSpec((t, nh * dq), lambda i: (0, kcol)),
                pl.BlockSpec((t, nh * dv), lambda i: (0, vcol))]
    args = [q, k, v]
    if use_bias:
        in_specs += [pl.BlockSpec((tq, BLK), lambda i: (i, 0)), pl.BlockSpec((8, t), lambda i: (0, 0))]
        args += [c_col, c_row]
    return pl.pallas_call(
        body, name="attn_fwd", grid=(t // tq,),
        in_specs=in_specs,
        out_specs=[pl.BlockSpec((tq, nh * dv), lambda i: (i, 0)), pl.BlockSpec((8, tq), lambda i: (0, i))],
        out_shape=[jax.ShapeDtypeStruct((t, nh * dv), BF16), jax.ShapeDtypeStruct((8, t), F32)],
        compiler_params=_cp("arbitrary"),
    )(*args)


def attn_bwd(q, k, v, do, lse_row, o, qcol, kcol, vcol, docol, ocol, nh, dq, dv, scale, c_col=None, c_row=None, lane0=0):
    t = q.shape[0]
    tq = BLK
    use_bias = c_col is not None
    nq = t // tq

    def body(*refs):
        if use_bias:
            (q_ref, k_ref, v_ref, do_ref, l_ref, o_ref, cc_ref, cr_ref, dq_ref, dk_ref, dv_ref, dcq_ref, dck_ref,
             kt, ckb, dacc) = refs
        else:
            q_ref, k_ref, v_ref, do_ref, l_ref, o_ref, dq_ref, dk_ref, dv_ref, kt = refs
        i = pl.program_id(0)

        @pl.when(i == 0)
        def _():
            kt[...] = k_ref[...].astype(BF16).T
            dk_ref[...] = jnp.zeros_like(dk_ref)
            dv_ref[...] = jnp.zeros_like(dv_ref)
            if use_bias:
                dacc[...] = jnp.zeros_like(dacc)
                for h in range(nh):
                    ckb[h] = jnp.broadcast_to(cc_ref[:, lane0 + h:lane0 + h + 1], (t, BLK))

        qry = i * tq + lax.broadcasted_iota(jnp.int32, (1, tq), 1)
        dot = (do_ref[...].astype(F32) * o_ref[...].astype(F32)).T

        def tile(tk):
            key = lax.broadcasted_iota(jnp.int32, (tk, 1), 0)
            mask = (key <= qry) & (key >= PAD)
            dqts, dcqs = [], []
            for h in range(nh):
                qh = q_ref[:, dq * h:dq * (h + 1)].astype(BF16)
                kh = k_ref[0:tk, dq * h:dq * (h + 1)].astype(BF16)
                vh = v_ref[0:tk, dv * h:dv * (h + 1)].astype(BF16)
                doh = do_ref[:, dv * h:dv * (h + 1)].astype(BF16)
                delta = jnp.sum(dot[dv * h:dv * (h + 1), :], axis=0, keepdims=True)
                st = _nt(kh, qh) * scale
                if use_bias:
                    st = st + (cr_ref[h:h + 1, :] - ckb[h, 0:tk, :])
                pt = jnp.exp(jnp.where(mask, st, NEG) - l_ref[h:h + 1, :])
                dst = pt * (_nt(vh, doh) - delta)
                dsb = dst.astype(BF16)
                dk_ref[0:tk, dq * h:dq * (h + 1)] += _nn(dsb, qh) * scale
                dv_ref[0:tk, dv * h:dv * (h + 1)] += _nn(pt.astype(BF16), doh)
                dqts.append(_nn(kt[dq * h:dq * (h + 1), 0:tk], dsb))
                if use_bias:
                    dcqs.append(jnp.sum(dst, axis=0, keepdims=True))
                    dacc[h, 0:tk, :] += dst
            dq_ref[...] = jnp.concatenate(dqts, axis=0).T * scale
            if use_bias:
                dcq_ref[...] = jnp.concatenate(dcqs + [jnp.zeros((8 - nh, tq), F32)], axis=0)

        for t0, t1 in _segments(nq, not use_bias):
            pl.when((i >= t0) & (i < t1))(functools.partial(tile, t1 * BLK))

        if use_bias:
            @pl.when(i == nq - 1)
            def _():
                lane = lax.broadcasted_iota(jnp.int32, (1, BLK), 1)
                tot = jnp.zeros((t, BLK), F32)
                for h in range(nh):
                    tot += jnp.where(lane == lane0 + h, jnp.sum(dacc[h], axis=1, keepdims=True), 0.0)
                dck_ref[...] = tot

    keys_q = pl.BlockSpec((t, nh * dq), lambda i: (0, 0))
    keys_v = pl.BlockSpec((t, nh * dv), lambda i: (0, 0))
    keys_c = pl.BlockSpec((t, BLK), lambda i: (0, 0))
    qrow = pl.BlockSpec((8, tq), lambda i: (0, i))
    in_specs = [pl.BlockSpec((tq, nh * dq), lambda i: (i, qcol)),
                pl.BlockSpec((t, nh * dq), lambda i: (0, kcol)),
                pl.BlockSpec((t, nh * dv), lambda i: (0, vcol)),
                pl.BlockSpec((tq, nh * dv), lambda i: (i, docol)),
                qrow,
                pl.BlockSpec((tq, nh * dv), lambda i: (i, ocol))]
    args = [q, k, v, do, lse_row, o]
    out_specs = [pl.BlockSpec((tq, nh * dq), lambda i: (i, 0)), keys_q, keys_v]
    out_shape = [jax.ShapeDtypeStruct((t, nh * dq), F32), jax.ShapeDtypeStruct((t, nh * dq), F32),
                 jax.ShapeDtypeStruct((t, nh * dv), F32)]
    scratch = [pltpu.VMEM((nh * dq, t), BF16)]
    if use_bias:
        in_specs += [keys_c, qrow]
        args += [c_col, c_row]
        out_specs += [qrow, keys_c]
        out_shape += [jax.ShapeDtypeStruct((8, t), F32), jax.ShapeDtypeStruct((t, BLK), F32)]
        scratch += [pltpu.VMEM((nh, t, BLK), F32), pltpu.VMEM((nh, t, BLK), F32)]
    return pl.pallas_call(
        body, name="attn_bwd", grid=(nq,),
        in_specs=in_specs, out_specs=out_specs, out_shape=out_shape, scratch_shapes=scratch,
        compiler_params=_cp("arbitrary"),
    )(*args)


def fox_pre(proj, fb):
    t = proj.shape[0]
    nb = t // BLK

    def body(sm_ref, fb_ref, c_ref, cr_ref):
        x = sm_ref[...] + fb_ref[...]
        lane = lax.broadcasted_iota(jnp.int32, (1, BLK), 1)
        keep = _valid_rows(t, 0) & (lane >= SM_F) & (lane < SM_F + FOX_H)
        logf = jnp.where(keep, jnp.minimum(x, 0.0) - jnp.log(1.0 + jnp.exp(-jnp.abs(x))), 0.0)
        tri = _tri().astype(F32)
        carry = jnp.zeros((1, BLK), F32)
        for b in range(nb):
            cb = _nn_hi(tri, logf[b * BLK:(b + 1) * BLK, :]) + carry
            c_ref[b * BLK:(b + 1) * BLK, :] = cb
            carry = cb[BLK - 1:BLK, :]
        cr_ref[...] = c_ref[...].T[SM_F:SM_F + 8, :]

    return pl.pallas_call(
        body, name="fox_pre", grid=(1,),
        in_specs=[pl.BlockSpec((t, BLK), lambda i: (0, C_SM // BLK)), pl.BlockSpec((1, BLK), lambda i: (0, 0))],
        out_specs=[pl.BlockSpec((t, BLK), lambda i: (0, 0)), pl.BlockSpec((8, t), lambda i: (0, 0))],
        out_shape=[jax.ShapeDtypeStruct((t, BLK), F32), jax.ShapeDtypeStruct((8, t), F32)],
        compiler_params=_cp("arbitrary"),
    )(proj, fb)


def fox_pre_bwd(dcq, dck, proj, fb, dsm_in):
    t = proj.shape[0]
    nb = t // BLK

    def body(dcq_ref, dck_ref, sm_ref, fb_ref, din_ref, dsm_ref, dfb_ref, scr):
        triu = _tri(lower=False).astype(F32)
        carry = jnp.zeros((1, BLK), F32)
        scr[...] = jnp.concatenate([jnp.zeros((SM_F, t), F32), dcq_ref[...], jnp.zeros((BLK - SM_F - 8, t), F32)], axis=0).T
        for b in range(nb - 1, -1, -1):
            blk = scr[b * BLK:(b + 1) * BLK, :] - dck_ref[b * BLK:(b + 1) * BLK, :]
            cb = _nn_hi(triu, blk) + carry
            scr[b * BLK:(b + 1) * BLK, :] = cb
            carry = cb[0:1, :]
        x = sm_ref[...] + fb_ref[...]
        lane = lax.broadcasted_iota(jnp.int32, (1, BLK), 1)
        keep = _valid_rows(t, 0) & (lane >= SM_F) & (lane < SM_F + FOX_H)
        df = jnp.where(keep, scr[...] * _sig(-x), 0.0)
        dfb_ref[...] = jnp.sum(df, axis=0, keepdims=True)
        dsm_ref[...] = din_ref[...] + df

    full = pl.BlockSpec((t, BLK), lambda i: (0, 0))
    return pl.pallas_call(
        body, name="fox_pre_bwd", grid=(1,),
        in_specs=[pl.BlockSpec((8, t), lambda i: (0, 0)), full,
                  pl.BlockSpec((t, BLK), lambda i: (0, C_SM // BLK)), pl.BlockSpec((1, BLK), lambda i: (0, 0)), full],
        out_specs=[full, pl.BlockSpec((1, BLK), lambda i: (0, 0))],
        out_shape=[jax.ShapeDtypeStruct((t, BLK), F32), jax.ShapeDtypeStruct((1, BLK), F32)],
        scratch_shapes=[pltpu.VMEM((t, BLK), F32)],
        compiler_params=_cp("arbitrary"),
    )(dcq, dck, proj, fb, dsm_in)


def _swap_rope(x):
    lane = lax.broadcasted_iota(jnp.int32, (1, BLK), 1)
    return jnp.where((lane >= SM_KR) & (lane < SM_KR + 16), pltpu.roll(x, BLK - 16, 1),
                     jnp.where((lane >= SM_KR + 16) & (lane < SM_KR + 32), pltpu.roll(x, 16, 1), 0.0))


def _rms(x, g):
    r = lax.rsqrt(jnp.mean(x * x, axis=1, keepdims=True) + EPS)
    return r, x * r


def mla_pre(proj, qg, kvg, wq, wk, wv, cosq, sinq):
    t = proj.shape[0]
    tm = _row_tile(t)

    def body(cq_ref, ckv_ref, sm_ref, qg_ref, kvg_ref, wq_ref, wk_ref, wv_ref, cos_ref, sin_ref,
             q_ref, k_ref, v_ref, cqn_ref, ckvn_ref):
        cs, sn = cos_ref[...], sin_ref[...]
        _, xh = _rms(cq_ref[...], None)
        cqn = (xh * qg_ref[...]).astype(BF16)
        cqn_ref[...] = cqn
        qraw = _nn(cqn, wq_ref[...])
        qs = []
        for h in range(MLA_H):
            hb = qraw[:, BLK * h:BLK * (h + 1)]
            qs.append(hb * cs + _swap_rope(hb) * sn)
        q_ref[...] = jnp.concatenate(qs, axis=1).astype(BF16)
        _, kh = _rms(ckv_ref[...], None)
        ckvn = (kh * kvg_ref[...]).astype(BF16)
        ckvn_ref[...] = ckvn
        kraw = _nn(ckvn, wk_ref[...])
        v_ref[...] = _nn(ckvn, wv_ref[...]).astype(BF16)
        lane = lax.broadcasted_iota(jnp.int32, (1, BLK), 1)
        kr = sm_ref[...]
        krr = jnp.where((lane >= SM_KR) & (lane < SM_KR + MLA_ROPE), kr * cs + _swap_rope(kr) * sn, 0.0)
        k_ref[...] = jnp.concatenate([kraw[:, BLK * h:BLK * (h + 1)] + krr for h in range(MLA_H)], axis=1).astype(BF16)

    def rows(w, cb):
        return pl.BlockSpec((tm, w), lambda i: (i, cb))

    def whole(a):
        return pl.BlockSpec(a.shape, lambda i: (0, 0))

    return pl.pallas_call(
        body, name="mla_pre", grid=(t // tm,),
        in_specs=[rows(MLA_QL, C_CQ // MLA_QL), rows(MLA_KVL, C_CKV // MLA_KVL), rows(BLK, C_SM // BLK),
                  whole(qg), whole(kvg), whole(wq), whole(wk), whole(wv), rows(BLK, 0), rows(BLK, 0)],
        out_specs=[rows(512, 0), rows(512, 0), rows(256, 0), rows(MLA_QL, 0), rows(MLA_KVL, 0)],
        out_shape=[jax.ShapeDtypeStruct((t, 512), BF16), jax.ShapeDtypeStruct((t, 512), BF16),
                   jax.ShapeDtypeStruct((t, 256), BF16), jax.ShapeDtypeStruct((t, MLA_QL), BF16),
                   jax.ShapeDtypeStruct((t, MLA_KVL), BF16)],
        compiler_params=_cp("arbitrary"),
    )(proj, proj, proj, qg, kvg, wq, wk, wv, cosq, sinq)


def mla_pre_bwd(dq, dk, dv, proj, cqn, ckvn, qg, kvg, wq, wk, wv, cosq, sinq, dsm_in):
    t = proj.shape[0]
    tm = _row_tile(t)

    def body(dq_ref, dk_ref, dv_ref, cq_ref, ckv_ref, cqn_ref, ckvn_ref, qg_ref, kvg_ref, wq_ref, wk_ref, wv_ref,
             cos_ref, sin_ref, din_ref, dcq_ref, dckv_ref, dsm_ref, dwq_ref, dwk_ref, dwv_ref, dqg_ref, dkvg_ref):
        i = pl.program_id(0)

        @pl.when(i == 0)
        def _():
            for r in (dwq_ref, dwk_ref, dwv_ref, dqg_ref, dkvg_ref):
                r[...] = jnp.zeros_like(r)

        cs, sn = cos_ref[...], sin_ref[...]
        lane = lax.broadcasted_iota(jnp.int32, (1, BLK), 1)

        def unrope(dy):
            return dy * cs + _swap_rope(dy * sn)

        dqp = jnp.concatenate([unrope(dq_ref[:, BLK * h:BLK * (h + 1)]) for h in range(MLA_H)], axis=1).astype(BF16)
        dwq_ref[...] += _tn(cqn_ref[...], dqp)
        dcqn = _nt(dqp, wq_ref[...])
        r, xh = _rms(cq_ref[...], None)
        dqg_ref[...] += jnp.sum(dcqn * xh, axis=0, keepdims=True)
        dxh = dcqn * qg_ref[...]
        dcq_ref[...] = r * (dxh - xh * jnp.mean(dxh * xh, axis=1, keepdims=True))

        dkn, dkr = [], jnp.zeros((tm, BLK), F32)
        for h in range(MLA_H):
            blk = dk_ref[:, BLK * h:BLK * (h + 1)]
            dkn.append(jnp.where(lane < MLA_NOPE, blk, 0.0))
            dkr += jnp.where((lane >= SM_KR) & (lane < SM_KR + MLA_ROPE), blk, 0.0)
        dknb = jnp.concatenate(dkn, axis=1).astype(BF16)
        dvb = dv_ref[...].astype(BF16)
        ckvn = ckvn_ref[...]
        dwk_ref[...] += _tn(ckvn, dknb)
        dwv_ref[...] += _tn(ckvn, dvb)
        dckvn = _nt(dknb, wk_ref[...]) + _nt(dvb, wv_ref[...])
        r2, kh = _rms(ckv_ref[...], None)
        dkvg_ref[...] += jnp.sum(dckvn * kh, axis=0, keepdims=True)
        dkh = dckvn * kvg_ref[...]
        dckv_ref[...] = r2 * (dkh - kh * jnp.mean(dkh * kh, axis=1, keepdims=True))
        dsm_ref[...] = din_ref[...] + jnp.where((lane >= SM_KR) & (lane < SM_KR + MLA_ROPE), unrope(dkr), 0.0)

    def rows(w, cb):
        return pl.BlockSpec((tm, w), lambda i: (i, cb))

    def whole(a):
        return pl.BlockSpec(a.shape, lambda i: (0, 0))

    def wshape(a):
        return jax.ShapeDtypeStruct(a.shape, F32)

    return pl.pallas_call(
        body, name="mla_pre_bwd", grid=(t // tm,),
        in_specs=[rows(512, 0), rows(512, 0), rows(256, 0), rows(MLA_QL, C_CQ // MLA_QL), rows(MLA_KVL, C_CKV // MLA_KVL),
                  rows(MLA_QL, 0), rows(MLA_KVL, 0), whole(qg), whole(kvg), whole(wq), whole(wk), whole(wv),
                  rows(BLK, 0), rows(BLK, 0), rows(BLK, 0)],
        out_specs=[rows(MLA_QL, 0), rows(MLA_KVL, 0), rows(BLK, 0), whole(wq), whole(wk), whole(wv), whole(qg), whole(kvg)],
        out_shape=[jax.ShapeDtypeStruct((t, MLA_QL), F32), jax.ShapeDtypeStruct((t, MLA_KVL), F32),
                   jax.ShapeDtypeStruct((t, BLK), F32), wshape(wq), wshape(wk), wshape(wv), wshape(qg), wshape(kvg)],
        compiler_params=_cp("arbitrary"),
    )(dq, dk, dv, proj, proj, cqn, ckvn, qg, kvg, wq, wk, wv, cosq, sinq, dsm_in)


def _slot_sum(me, own, recv_ref):
    gg = own.astype(F32)
    for s in range(N_DEV):
        gg = gg + jnp.where(me == s, 0.0, recv_ref[s].astype(F32))
    return gg


def adamw(w, m, v, g=None, recv=None, own=None, me_arr=None):
    shape = w.shape
    c = shape[-1]
    from_recv = recv is not None
    if not from_recv:
        me_arr = jnp.zeros((1,), jnp.int32)
    nl = len(recv) if from_recv else 1
    rws = w.size // c // nl
    tr = rws
    for d in (1024, 512, 352, 256, 128, 64, 32, 16, 8):
        if rws % d == 0 and d * c * 4 <= (2 << 20):
            tr = d
            break
    nt = rws // tr
    w2, m2, v2 = (a.reshape(nl, rws, c) for a in (w, m, v))
    if from_recv:
        gin = [a.reshape(N_DEV, rws, c) for a in list(recv) + list(own)]
    else:
        gin = [g.reshape(1, rws, c)]

    def body(me_ref, w_ref, m_ref, v_ref, *rest):
        g_refs, outs = rest[:len(gin)], rest[len(gin):]
        if from_recv:
            g_out, outs = outs[0], outs[1:]
            for li in range(nl):
                @pl.when(pl.program_id(0) == li)
                def _(li=li):
                    g_out[...] = _slot_sum(me_ref[0], g_refs[nl + li][...], g_refs[li])
            gg = g_out[...]
        else:
            gg = g_refs[0][...]
        d_ref, nm_ref, nv_ref = outs
        nm = B1 * m_ref[...] + (1.0 - B1) * gg
        nv = B2 * v_ref[...] + (1.0 - B2) * (gg * gg)
        mh = nm / (1.0 - B1 ** STEP)
        vh = nv / (1.0 - B2 ** STEP)
        d_ref[...] = -LR * (mh / (jnp.sqrt(vh) + AEPS) + WD * w_ref[...])
        nm_ref[...] = nm
        nv_ref[...] = nv

    row = pl.BlockSpec((None, tr, c), lambda l, i, me: (l, i, 0))
    if from_recv:
        gspecs = [pl.BlockSpec((N_DEV, tr, c), lambda l, i, me, li=li: (0, jnp.where(l == li, i, 0), 0))
                  for li in range(nl)]
        gspecs += [pl.BlockSpec((None, tr, c), lambda l, i, me, li=li: (me[0], jnp.where(l == li, i, 0), 0))
                   for li in range(nl)]
    else:
        gspecs = [row]
    nout = 4 if from_recv else 3
    outs = pl.pallas_call(
        body, name="adamw",
        grid_spec=pltpu.PrefetchScalarGridSpec(num_scalar_prefetch=1, grid=(nl, nt), in_specs=[row, row, row] + gspecs,
                                               out_specs=[row] * nout),
        out_shape=[jax.ShapeDtypeStruct((nl, rws, c), F32)] * nout,
        compiler_params=_cp("arbitrary", "arbitrary"),
    )(me_arr, w2, m2, v2, *gin)
    return tuple(o.reshape(shape) for o in outs)


def sum_slots(recv, own=None, me_arr=None):
    _, r, c = recv.shape
    if own is None:
        own, me_arr = recv, jnp.zeros((1,), jnp.int32)
        plain = True
    else:
        plain = False

    def body(me_ref, r_ref, own_ref, o_ref):
        if plain:
            gg = r_ref[0].astype(F32)
            for s in range(1, N_DEV):
                gg = gg + r_ref[s].astype(F32)
            o_ref[...] = gg
        else:
            o_ref[...] = _slot_sum(me_ref[0], own_ref[...], r_ref)

    return pl.pallas_call(
        body, name="sum_slots",
        grid_spec=pltpu.PrefetchScalarGridSpec(
            num_scalar_prefetch=1, grid=(1,),
            in_specs=[pl.BlockSpec((N_DEV, r, c), lambda i, me: (0, 0, 0)),
                      pl.BlockSpec((None, r, c), lambda i, me: (me[0], 0, 0))],
            out_specs=pl.BlockSpec((r, c), lambda i, me: (0, 0))),
        out_shape=jax.ShapeDtypeStruct((r, c), F32),
        compiler_params=_cp("arbitrary"),
    )(me_arr, recv, own)


_FLIPS = [(0, 0, 1), (0, 1, 0), (0, 1, 1), (1, 0, 0), (1, 0, 1), (1, 1, 0), (1, 1, 1)]
_ANY = pl.BlockSpec(memory_space=pl.ANY)


def _mesh_place():
    x, y, c = lax.axis_index("x"), lax.axis_index("y"), lax.axis_index("c")
    me = 4 * x + 2 * y + c
    peers = [((x + fx) % 2, (y + fy) % 2, (c + fc) % 2) for fx, fy, fc in _FLIPS]
    return me, peers


def place_own(src, l, dtype, me_arr):
    _, r, c = src.shape
    tr = r
    for d in (512, 352, 256, 128, 64, 32, 16, 8):
        if r % d == 0 and d * c * 4 <= (2 << 20):
            tr = d
            break

    def body(me_ref, s_ref, o_ref):
        o_ref[...] = s_ref[...].astype(dtype)

    return pl.pallas_call(
        body, name="place_own",
        grid_spec=pltpu.PrefetchScalarGridSpec(
            num_scalar_prefetch=1, grid=(r // tr,),
            in_specs=[pl.BlockSpec((None, tr, c), lambda i, me: (l, i, 0))],
            out_specs=pl.BlockSpec((None, tr, c), lambda i, me: (me[0], i, 0))),
        out_shape=jax.ShapeDtypeStruct((N_DEV, r, c), dtype),
        compiler_params=_cp("arbitrary"),
    )(me_arr, src)


_HBM = pl.BlockSpec(memory_space=pltpu.HBM)
_SEMS = pl.BlockSpec(memory_space=pltpu.SEMAPHORE)
_EFFECT = pltpu.SideEffectType.DATAFLOW_SIDE_EFFECTING


def exchange_start(mode, arrays, name, after=None):
    n = len(arrays)
    gather = mode == "gather"
    ns = 0 if gather else n
    zones = list(arrays) if gather else [lax.empty(a.shape, a.dtype) for a in arrays]
    ops = ([] if gather else list(arrays)) + zones
    extra = [] if after is None else [after]

    def body(*refs):
        srcs, lands = refs[:ns], refs[ns:ns + n]
        send_sems, recv_sems = refs[ns + n + len(extra)], refs[ns + n + len(extra) + 1]
        token = refs[-1]
        me, peers = _mesh_place()
        ids = [4 * p[0] + 2 * p[1] + p[2] for p in peers]
        for j in range(n):
            for k in range(N_DEV - 1):
                src = lands[j].at[me] if gather else srcs[j].at[ids[k]]
                pltpu.make_async_remote_copy(src_ref=src, dst_ref=lands[j].at[me],
                                             send_sem=send_sems.at[j * (N_DEV - 1) + k],
                                             recv_sem=recv_sems.at[j * (N_DEV - 1) + k], device_id=peers[k],
                                             device_id_type=pl.DeviceIdType.MESH).start()
        token[...] = jnp.zeros_like(token)

    nsem = n * (N_DEV - 1)
    res = pl.pallas_call(
        body, name=name,
        in_specs=[_HBM] * (ns + n) + [_ANY] * len(extra),
        out_specs=(_SEMS, _SEMS, *[_HBM] * (ns + n), pl.BlockSpec(memory_space=pltpu.VMEM)),
        out_shape=(pltpu.SemaphoreType.DMA((nsem,)), pltpu.SemaphoreType.DMA((nsem,)),
                   *[pltpu.HBM(a.shape, a.dtype) for a in ops], jax.ShapeDtypeStruct((8, BLK), F32)),
        input_output_aliases={i: 2 + i for i in range(ns + n)},
        compiler_params=pltpu.CompilerParams(has_side_effects=_EFFECT),
    )(*[pltpu.with_memory_space_constraint(a, pltpu.HBM) for a in ops], *extra)
    return dict(gather=gather, send=res[0], recv=res[1], srcs=list(res[2:2 + ns]), lands=list(res[2 + ns:2 + ns + n]),
                token=res[-1])


def exchange_wait(hd, idxs, name, after):
    gather = hd["gather"]
    n = len(idxs)
    ns = 0 if gather else n
    ops = ([] if gather else [hd["srcs"][j] for j in idxs]) + [hd["lands"][j] for j in idxs]

    def body(*refs):
        srcs, lands = refs[:ns], refs[ns:ns + n]
        send_sems, recv_sems = refs[ns + n], refs[ns + n + 1]
        me, peers = _mesh_place()
        ids = [4 * p[0] + 2 * p[1] + p[2] for p in peers]
        for p, j in enumerate(idxs):
            for k in range(N_DEV - 1):
                src = lands[p].at[me] if gather else srcs[p].at[ids[k]]
                cp = pltpu.make_async_remote_copy(src_ref=src, dst_ref=lands[p].at[ids[k]],
                                                  send_sem=send_sems.at[j * (N_DEV - 1) + k],
                                                  recv_sem=recv_sems.at[j * (N_DEV - 1) + k], device_id=peers[k],
                                                  device_id_type=pl.DeviceIdType.MESH)
                cp.wait_send()
                cp.wait_recv()

    res = pl.pallas_call(
        body, name=name,
        in_specs=[_HBM] * (ns + n) + [_SEMS, _SEMS, _ANY],
        out_specs=[_HBM] * (ns + n),
        out_shape=[pltpu.HBM(a.shape, a.dtype) for a in ops],
        input_output_aliases={i: i for i in range(ns + n)},
        compiler_params=pltpu.CompilerParams(has_side_effects=_EFFECT),
    )(*ops, hd["send"], hd["recv"], after)
    return list(res[:ns]), list(res[ns:])


def _chip_place():
    x, y, c = lax.axis_index("x"), lax.axis_index("y"), lax.axis_index("c")
    chips = [((x + 1) % 2, y), (x, (y + 1) % 2), ((x + 1) % 2, (y + 1) % 2)]
    ident = lambda p: 4 * p[0] + 2 * p[1] + p[2]
    return dict(me=4 * x + 2 * y + c, sib=(x, y, 1 - c), sib_id=4 * x + 2 * y + 1 - c,
                same=[(cx, cy, c) for cx, cy in chips], same_ids=[ident((cx, cy, c)) for cx, cy in chips],
                other_ids=[ident((cx, cy, 1 - c)) for cx, cy in chips])


def _remote(src, dst, send_sem, recv_sem, dev):
    return pltpu.make_async_remote_copy(src_ref=src, dst_ref=dst, send_sem=send_sem, recv_sem=recv_sem, device_id=dev,
                                        device_id_type=pl.DeviceIdType.MESH)


def gather_start(zones, name):
    n = len(zones)

    def body(*refs):
        lands, send_sems, recv_sems, token = refs[:n], refs[n], refs[n + 1], refs[-1]
        pc = _chip_place()
        for j in range(n):
            own = lands[j].at[pc["me"]]
            for k, dev in enumerate([pc["sib"]] + pc["same"]):
                _remote(own, own, send_sems.at[4 * j + k], recv_sems.at[4 * j + k], dev).start()
        token[...] = jnp.zeros_like(token)

    res = pl.pallas_call(
        body, name=name,
        in_specs=[_HBM] * n,
        out_specs=(_SEMS, _SEMS, *[_HBM] * n, pl.BlockSpec(memory_space=pltpu.VMEM)),
        out_shape=(pltpu.SemaphoreType.DMA((4 * n,)), pltpu.SemaphoreType.DMA((4 * n,)),
                   *[pltpu.HBM(a.shape, a.dtype) for a in zones], jax.ShapeDtypeStruct((8, BLK), F32)),
        input_output_aliases={i: 2 + i for i in range(n)},
        compiler_params=pltpu.CompilerParams(has_side_effects=_EFFECT),
    )(*[pltpu.with_memory_space_constraint(a, pltpu.HBM) for a in zones])
    return dict(send=res[0], recv=res[1], lands=list(res[2:2 + n]), token=res[-1])


def gather_relay(hd, idxs, name, after):
    n = len(idxs)

    def body(*refs):
        lands, send_sems, recv_sems = refs[:n], refs[n], refs[n + 1]
        fsend, frecv, token = refs[n + 3 + n], refs[n + 4 + n], refs[-1]
        pc = _chip_place()
        for p, j in enumerate(idxs):
            for k in range(3):
                _remote(lands[p].at[pc["me"]], lands[p].at[pc["same_ids"][k]], send_sems.at[4 * j + 1 + k],
                        recv_sems.at[4 * j + 1 + k], pc["same"][k]).wait_recv()
        for p in range(n):
            for k in range(3):
                got = lands[p].at[pc["same_ids"][k]]
                _remote(got, got, fsend.at[3 * p + k], frecv.at[3 * p + k], pc["sib"]).start()
        token[...] = jnp.zeros_like(token)

    ops = [hd["lands"][j] for j in idxs]
    res = pl.pallas_call(
        body, name=name,
        in_specs=[_HBM] * n + [_SEMS, _SEMS, _ANY],
        out_specs=(*[_HBM] * n, _SEMS, _SEMS, pl.BlockSpec(memory_space=pltpu.VMEM)),
        out_shape=(*[pltpu.HBM(a.shape, a.dtype) for a in ops], pltpu.SemaphoreType.DMA((3 * n,)),
                   pltpu.SemaphoreType.DMA((3 * n,)), jax.ShapeDtypeStruct((8, BLK), F32)),
        input_output_aliases={i: i for i in range(n)},
        compiler_params=pltpu.CompilerParams(has_side_effects=_EFFECT),
    )(*ops, hd["send"], hd["recv"], after)
    return dict(lands=list(res[:n]), fsend=res[n], frecv=res[n + 1], token=res[-1])


def gather_wait(hd, rl, idxs, name, after):
    n = len(idxs)

    def body(*refs):
        lands, send_sems, recv_sems, fsend, frecv = refs[:n], refs[n], refs[n + 1], refs[n + 2], refs[n + 3]
        pc = _chip_place()
        for p, j in enumerate(idxs):
            own = lands[p].at[pc["me"]]
            for k, dev in enumerate([pc["sib"]] + pc["same"]):
                _remote(own, own, send_sems.at[4 * j + k], recv_sems.at[4 * j + k], dev).wait_send()
            _remote(own, lands[p].at[pc["sib_id"]], send_sems.at[4 * j], recv_sems.at[4 * j], pc["sib"]).wait_recv()
            for k in range(3):
                cp = _remote(lands[p].at[pc["same_ids"][k]], lands[p].at[pc["other_ids"][k]], fsend.at[3 * p + k],
                             frecv.at[3 * p + k], pc["sib"])
                cp.wait_send()
                cp.wait_recv()

    res = pl.pallas_call(
        body, name=name,
        in_specs=[_HBM] * n + [_SEMS, _SEMS, _SEMS, _SEMS, _ANY],
        out_specs=[_HBM] * n,
        out_shape=[pltpu.HBM(a.shape, a.dtype) for a in rl["lands"]],
        input_output_aliases={i: i for i in range(n)},
        compiler_params=pltpu.CompilerParams(has_side_effects=_EFFECT),
    )(*rl["lands"], hd["send"], hd["recv"], rl["fsend"], rl["frecv"], after)
    return list(res)


def _pad_cols(a, n):
    return jnp.pad(a, ((0, 0),) * (a.ndim - 1) + ((0, n - a.shape[-1]),))


def w_in_to_padded(w):
    z = lambda n: jnp.zeros(w.shape[:-1] + (n,), w.dtype)
    return jnp.concatenate([
        w[..., 0:1280], w[..., 1288:2056], w[..., 2060:2316], w[..., 2316:2444],
        w[..., 1280:1288], w[..., 2056:2060], z(SM_KR - SM_F - FOX_H), w[..., 2444:2476], z(BLK - SM_KR - MLA_ROPE)], axis=-1)


def w_in_from_padded(g):
    s = C_SM
    return jnp.concatenate([
        g[..., 0:1280], g[..., s + SM_DT:s + SM_DT + 8], g[..., 1280:2048], g[..., s + SM_F:s + SM_F + 4],
        g[..., 2048:2304], g[..., 2304:2432], g[..., s + SM_KR:s + SM_KR + MLA_ROPE]], axis=-1)


def _unshard_cols(gth):
    n, r, c = gth.shape
    return jnp.transpose(gth, (1, 0, 2)).reshape(r, n * c)


def _shard_cols(full):
    r, nc = full.shape
    return jnp.transpose(full.reshape(r, N_DEV, nc // N_DEV), (1, 0, 2))


def mla_weights(uq_g, ukv_g):
    uq = _unshard_cols(uq_g)
    dqh = MLA_NOPE + MLA_ROPE
    wq = jnp.concatenate([_pad_cols(uq[:, dqh * h:dqh * (h + 1)], BLK) for h in range(MLA_H)], axis=1)
    wk = jnp.concatenate([_pad_cols(ukv_g[2 * h], BLK) for h in range(MLA_H)], axis=1)
    wv = jnp.concatenate([ukv_g[2 * h + 1] for h in range(MLA_H)], axis=1)
    return wq, wk, wv


def mla_weight_grads(dwq, dwk, dwv):
    dqh = MLA_NOPE + MLA_ROPE
    duq = _shard_cols(jnp.concatenate([dwq[:, BLK * h:BLK * h + dqh] for h in range(MLA_H)], axis=1))
    parts = []
    for h in range(MLA_H):
        parts += [dwk[:, BLK * h:BLK * h + MLA_NOPE], dwv[:, MLA_V * h:MLA_V * (h + 1)]]
    return duq, jnp.stack(parts, axis=0)


def rope_tables(t):
    pos = (jnp.arange(t, dtype=jnp.int32) - PAD).astype(F32)
    inv_freq = 1.0 / (10000.0 ** (jnp.arange(0, MLA_ROPE, 2, dtype=F32) / MLA_ROPE))
    ang = pos[:, None] * inv_freq[None, :]
    cos, sin = jnp.cos(ang), jnp.sin(ang)
    one, zero = jnp.ones((t, SM_KR), F32), jnp.zeros((t, SM_KR), F32)
    tail = BLK - SM_KR - MLA_ROPE
    cosq = jnp.concatenate([one, cos, cos, jnp.ones((t, tail), F32)], axis=1)
    sinq = jnp.concatenate([zero, -sin, sin, jnp.zeros((t, tail), F32)], axis=1)
    return cosq, sinq


def _lanes(v, off=0):
    return jnp.pad(v.astype(F32), (off, BLK - off - v.shape[0]))[None, :]


def layer_fwd(x, ln, hb, getw, tabs, ahead):
    sv = {"h0b": hb}
    def behind(vec, tok):
        return vec if tok is None else vec + 0.0 * tok[0:1, 0:1]

    W = dict(getw("ffn1", hb))
    ln1 = (behind(W["ln1_g"], ahead(0, "mix", hb, 1)), W["ln1_b"])
    u, v, r1, h1b = ffn_fwd_seq(x, ln, W["g1"], W["u1"], W["d1"], ln1)
    sv.update(u1=u, v1=v, r1=r1, h1b=h1b)
    W.update(getw("mix", h1b))
    ln2 = (W["ln2_g"], W["ln2_b"])
    proj = mm_nn(h1b, W["w_in"])
    xa = conv_fwd(proj, W["conv_w"], W["conv_b"])
    y_ssd, sprev = ssd_fwd(xa, proj, W["dtb"], W["alog"], W["dskip"], W["normg"])
    c_col, c_row = fox_pre(proj, W["fb"])
    y_fox, lse_f = attn_fwd(proj, proj, proj, C_FQ // 256, C_FK // 256, C_FV // 256, FOX_H, FOX_DH, FOX_DH,
                            FOX_DH ** -0.5, c_col, c_row, SM_F)
    q, k, vv, cqn, ckvn = mla_pre(proj, behind(W["qg"], ahead(0, "ffn2", y_fox)), W["kvg"], W["wq"], W["wk"], W["wv"], *tabs)
    y_mla, lse_m = attn_fwd(q, k, vv, 0, 0, 0, MLA_H, BLK, MLA_V, (MLA_NOPE + MLA_ROPE) ** -0.5)
    r2, h2b = mm_res_ln([y_ssd, y_fox, y_mla], W["w_out"], r1, ln1, ln2)
    sv.update(proj=proj, xa=xa, sprev=sprev, c_col=c_col, c_row=c_row, lse_f=lse_f, q=q, k=k, v=vv, cqn=cqn, ckvn=ckvn,
              lse_m=lse_m, y_ssd=y_ssd, y_fox=y_fox, y_mla=y_mla, r2=r2, h2b=h2b)
    W.update(getw("ffn2", h2b))
    ln3 = (behind(W["ln3_g"], ahead(1, "ffn1", h2b)), W["ln3_b"])
    u, v, r3, h3b = ffn_fwd_seq(r2, ln2, W["g2"], W["u2"], W["d2"], ln3)
    sv.update(u2=u, v2=v, r3=r3, W=W)
    return r3, ln3, h3b, sv


def ffn_bwd(parts, r, gamma, hb_in, u, v, wg, wu, wd, after=None):
    dh, dwg, dwu, dwd, dg, db = ffn_bwd_seq(parts, r, gamma, hb_in, u, v, wg, wu, wd, after)
    return dh, dict(d=dwd, g=dwg, u=dwu, ln_g=dg, ln_b=db)


def layer_bwd(parts, sv, emit, tabs, after):
    G = {}
    W = sv["W"]
    dh2, g2 = ffn_bwd(parts, sv["r3"], W["ln3_g"], sv["h2b"], sv["u2"], sv["v2"], W["g2"], W["u2"], W["d2"], after)
    G.update(g2=g2["g"], u2=g2["u"], d2=g2["d"], ln3_g=g2["ln_g"], ln3_b=g2["ln_b"])
    tok = emit("ffn2", G)
    dr2, dmc, G["w_out"], G["ln2_g"], G["ln2_b"] = oproj_bwd(dh2, sv["r2"], W["ln2_g"], [sv["y_ssd"], sv["y_fox"], sv["y_mla"]], W["w_out"], tok)
    proj = sv["proj"]
    dxa, dz, dsm, G["normg"], G["dskip"], G["alog"], G["dtb"] = ssd_bwd(
        dmc, sv["xa"], proj, sv["sprev"], W["dtb"], W["alog"], W["dskip"], W["normg"])
    dxbc, G["conv_w"], G["conv_b"] = conv_bwd(dxa, proj, W["conv_w"], W["conv_b"])
    dfq, dfk, dfv, dcq, dck = attn_bwd(proj, proj, proj, dmc, sv["lse_f"], sv["y_fox"], C_FQ // 256, C_FK // 256,
                                       C_FV // 256, 2, 0, FOX_H, FOX_DH, FOX_DH, FOX_DH ** -0.5, sv["c_col"], sv["c_row"], SM_F)
    dsm, G["fb"] = fox_pre_bwd(dcq, dck, proj, W["fb"], dsm)
    dq, dk, dv = attn_bwd(sv["q"], sv["k"], sv["v"], dmc, sv["lse_m"], sv["y_mla"], 0, 0, 0, 3, 0, MLA_H, BLK, MLA_V,
                          (MLA_NOPE + MLA_ROPE) ** -0.5)
    dcql, dckv, dsm, G["wq"], G["wk"], G["wv"], G["qg"], G["kvg"] = mla_pre_bwd(
        dq, dk, dv, proj, sv["cqn"], sv["ckvn"], W["qg"], W["kvg"], W["wq"], W["wk"], W["wv"], *tabs, dsm)
    dh1p, G["w_in"] = proj_bwd([dz, dxbc, dfq, dfk, dfv, dcql, dckv, dsm], sv["h1b"], W["w_in"])
    tok = emit("mix", G)
    dh0, g1 = ffn_bwd([(dr2, ALPHA), (dh1p, 1.0)], sv["r1"], W["ln1_g"], sv["h0b"], sv["u1"], sv["v1"],
                      W["g1"], W["u1"], W["d1"], tok)
    G.update(g1=g1["g"], u1=g1["u"], d1=g1["d"], ln1_g=g1["ln_g"], ln1_b=g1["ln_b"])
    tok = emit("ffn1", G)
    return [(dh0, 1.0)], G, tok


def local_step(x, target, meta_full, getw, emit, ahead=lambda l, stage, after, min_layer=0: None):
    t = x.shape[0] + BLK
    tabs = rope_tables(t)
    xr, hb = build_h0(meta_full, x)
    ln = None
    saved = []
    for l in range(NL):
        xr, ln, hb, sv = layer_fwd(xr, ln, hb, functools.partial(getw, l), tabs,
                                   lambda dl, stage, after, min_layer=0, l=l: ahead(l + dl, stage, after, min_layer))
        saved.append(sv)
    dy, loss = loss_head(xr, ln, target)
    parts = [(dy, 1.0)]
    grads = [None] * NL
    tok = None
    for l in range(NL - 1, -1, -1):
        parts, grads[l], tok = layer_bwd(parts, saved[l], functools.partial(emit, l), tabs, tok)
    gx, gmeta = split_dh0(parts[0][0], tok)
    return loss, gx, gmeta, grads


_SMALL = ["ln1_g", "ln1_b", "ln2_g", "ln2_b", "ln3_g", "ln3_b", "conv_b", "ssd_norm_g", "mla_q_norm_g",
          "mla_kv_norm_g", "dt_bias", "a_log", "d_skip", "fox_f_b"]
_SMALL_ROWS = 8
_NAMES = ["meta", "ffn1_w_gate", "ffn1_w_up", "ffn1_w_down", "ln1_g", "ln1_b", "w_in", "conv_w", "conv_b", "dt_bias",
          "a_log", "d_skip", "ssd_norm_g", "fox_f_b", "mla_q_norm_g", "mla_w_uq", "mla_kv_norm_g", "mla_w_ukv", "w_out",
          "ln2_g", "ln2_b", "ffn2_w_gate", "ffn2_w_up", "ffn2_w_down", "ln3_g", "ln3_b"]


def pack_small(p):
    flat = jnp.concatenate([p[n].astype(F32) for n in _SMALL], axis=1)
    return _pad_cols(flat, _SMALL_ROWS * D).reshape(NL * _SMALL_ROWS, D)


def unpack_small(a, like):
    flat = a.reshape(NL, _SMALL_ROWS * D)
    out, at = {}, 0
    for n in _SMALL:
        out[n] = flat[:, at:at + like[n].shape[1]]
        at += like[n].shape[1]
    return out


_STAGES = {"ffn1": ["ffn1_w_gate", "ffn1_w_up", "ffn1_w_down"],
           "mix": ["w_in", "conv_w", "mla_w_uq", "mla_w_ukv", "w_out"],
           "ffn2": ["ffn2_w_gate", "ffn2_w_up", "ffn2_w_down"]}


_FFN_T = ("ffn1_w_gate", "ffn1_w_up", "ffn2_w_gate", "ffn2_w_up")


def stage_weights(l, stage, g, rep):
    if stage != "mix":
        i = stage[3]
        return {"g" + i: g[f"ffn{i}_w_gate"].reshape(D_FF, D), "u" + i: g[f"ffn{i}_w_up"].reshape(D_FF, D),
                "d" + i: g[f"ffn{i}_w_down"].reshape(D_FF, D),
                "ln1_g" if i == "1" else "ln3_g": rep["ln1_g" if i == "1" else "ln3_g"][l][None, :],
                "ln1_b" if i == "1" else "ln3_b": rep["ln1_b" if i == "1" else "ln3_b"][l][None, :]}
    W = {}
    W["w_in"] = g["w_in"].reshape(D, N_INP)
    W["w_out"] = g["w_out"].reshape(D, D)
    W["wq"], W["wk"], W["wv"] = mla_weights(g["mla_w_uq"], g["mla_w_ukv"])
    W["conv_w"] = _unshard_cols(g["conv_w"])
    for k in ("ln2_g", "ln2_b", "conv_b"):
        W[k] = rep[k][l][None, :]
    W["normg"] = rep["ssd_norm_g"][l][None, :]
    W["qg"] = rep["mla_q_norm_g"][l][None, :]
    W["kvg"] = rep["mla_kv_norm_g"][l][None, :]
    W["dtb"] = _lanes(rep["dt_bias"][l], SM_DT)
    W["alog"] = _lanes(rep["a_log"][l], SM_DT)
    W["dskip"] = _lanes(rep["d_skip"][l], SM_DT)
    W["fb"] = _lanes(rep["fox_f_b"][l], SM_F)
    return W


def small_grads(G):
    return {"ln1_g": G["ln1_g"][0], "ln1_b": G["ln1_b"][0], "ln2_g": G["ln2_g"][0], "ln2_b": G["ln2_b"][0],
            "ln3_g": G["ln3_g"][0], "ln3_b": G["ln3_b"][0], "conv_b": G["conv_b"][0], "ssd_norm_g": G["normg"][0],
            "mla_q_norm_g": G["qg"][0], "mla_kv_norm_g": G["kvg"][0], "dt_bias": G["dtb"][0, :SSD_H],
            "a_log": G["alog"][0, :SSD_H], "d_skip": G["dskip"][0, :SSD_H], "fox_f_b": G["fb"][0, SM_F:SM_F + FOX_H]}


def big_grads(G, stage):
    if stage != "mix":
        i = stage[-1]
        return {f"ffn{i}_w_{k}": G[k[0] + i].reshape(N_DEV, HS, D) for k in ("gate", "up", "down")}
    duq, dukv = mla_weight_grads(G["wq"], G["wk"], G["wv"])
    return {"w_in": G["w_in"].reshape(N_DEV, D // N_DEV, N_INP), "w_out": G["w_out"].reshape(N_DEV, D // N_DEV, D),
            "mla_w_uq": duq, "mla_w_ukv": dukv, "conv_w": _shard_cols(G["conv_w"])}


def kernel(x, meta, ffn1_w_gate, ffn1_w_up, ffn1_w_down, ln1_g, ln1_b, w_in, conv_w, conv_b, dt_bias, a_log, d_skip, ssd_norm_g, fox_f_b, mla_q_norm_g, mla_w_uq, mla_kv_norm_g, mla_w_ukv, w_out, ln2_g, ln2_b, ffn2_w_gate, ffn2_w_up, ffn2_w_down, ln3_g, ln3_b, loss_target, m_meta, m_ffn1_w_gate, m_ffn1_w_up, m_ffn1_w_down, m_ln1_g, m_ln1_b, m_w_in, m_conv_w, m_conv_b, m_dt_bias, m_a_log, m_d_skip, m_ssd_norm_g, m_fox_f_b, m_mla_q_norm_g, m_mla_w_uq, m_mla_kv_norm_g, m_mla_w_ukv, m_w_out, m_ln2_g, m_ln2_b, m_ffn2_w_gate, m_ffn2_w_up, m_ffn2_w_down, m_ln3_g, m_ln3_b, v_meta, v_ffn1_w_gate, v_ffn1_w_up, v_ffn1_w_down, v_ln1_g, v_ln1_b, v_w_in, v_conv_w, v_conv_b, v_dt_bias, v_a_log, v_d_skip, v_ssd_norm_g, v_fox_f_b, v_mla_q_norm_g, v_mla_w_uq, v_mla_kv_norm_g, v_mla_w_ukv, v_w_out, v_ln2_g, v_ln2_b, v_ffn2_w_gate, v_ffn2_w_up, v_ffn2_w_down, v_ln3_g, v_ln3_b):
    vals = (meta, ffn1_w_gate, ffn1_w_up, ffn1_w_down, ln1_g, ln1_b, w_in, conv_w, conv_b, dt_bias, a_log, d_skip, ssd_norm_g, fox_f_b, mla_q_norm_g, mla_w_uq, mla_kv_norm_g, mla_w_ukv, w_out, ln2_g, ln2_b, ffn2_w_gate, ffn2_w_up, ffn2_w_down, ln3_g, ln3_b)
    moms = (m_meta, m_ffn1_w_gate, m_ffn1_w_up, m_ffn1_w_down, m_ln1_g, m_ln1_b, m_w_in, m_conv_w, m_conv_b, m_dt_bias, m_a_log, m_d_skip, m_ssd_norm_g, m_fox_f_b, m_mla_q_norm_g, m_mla_w_uq, m_mla_kv_norm_g, m_mla_w_ukv, m_w_out, m_ln2_g, m_ln2_b, m_ffn2_w_gate, m_ffn2_w_up, m_ffn2_w_down, m_ln3_g, m_ln3_b)
    vars_ = (v_meta, v_ffn1_w_gate, v_ffn1_w_up, v_ffn1_w_down, v_ln1_g, v_ln1_b, v_w_in, v_conv_w, v_conv_b, v_dt_bias, v_a_log, v_d_skip, v_ssd_norm_g, v_fox_f_b, v_mla_q_norm_g, v_mla_w_uq, v_mla_kv_norm_g, v_mla_w_ukv, v_w_out, v_ln2_g, v_ln2_b, v_ffn2_w_gate, v_ffn2_w_up, v_ffn2_w_down, v_ln3_g, v_ln3_b)
    P = dict(zip(_NAMES, vals))
    M = dict(zip(_NAMES, moms))
    V = dict(zip(_NAMES, vars_))
    me = 4 * lax.axis_index("x") + 2 * lax.axis_index("y") + lax.axis_index("c")

    me_arr = me.astype(jnp.int32).reshape(1)
    for n in _FFN_T:
        P[n], M[n], V[n] = (jnp.swapaxes(a[n], 1, 2) for a in (P, M, V))
    src = dict(P)
    src["w_in"] = w_in_to_padded(P["w_in"])
    order = [("meta", 0)] + [(n, l) for l in range(NL) for names in _STAGES.values() for n in names]
    nfirst = 1 + len(_STAGES["ffn1"])

    def place(n, l):
        return place_own(P["meta"][None] if n == "meta" else src[n], l, F32 if n in ("meta", "conv_w") else BF16, me_arr)

    hg_first = gather_start([place(n, l) for n, l in order[:nfirst]], "gather_start_first")
    hg_rest = gather_start([place(n, l) for n, l in order[nfirst:]], "gather_start_rest")
    zone_of = {nl_: ((hg_first, i) if i < nfirst else (hg_rest, i - nfirst)) for i, nl_ in enumerate(order)}
    relays = {}

    def ahead(l, stage, after, min_layer=0):
        if not min_layer <= l < NL:
            return None
        if (l, stage) not in relays:
            zs = [zone_of[("meta", 0)]] if stage == "meta" else [zone_of[(n, l)] for n in _STAGES[stage]]
            hg, idxs = zs[0][0], [i for _, i in zs]
            relays[(l, stage)] = (hg, idxs, gather_relay(hg, idxs, f"gather_relay_{l}_{stage}", after))
        return relays[(l, stage)][2]["token"]

    def arrived(l, stage, after):
        ahead(l, stage, after)
        hg, idxs, rl = relays[(l, stage)]
        return gather_wait(hg, rl, idxs, f"gather_wait_{l}_{stage}", after)

    meta_full = _unshard_cols(arrived(0, "meta", hg_rest["token"])[0])

    def getw(l, stage, after):
        return stage_weights(l, stage, dict(zip(_STAGES[stage], arrived(l, stage, after))), P)

    sent = {}

    def emit(l, stage, G):
        bg = big_grads(G, stage)
        sent[(l, stage)] = exchange_start("scatter", [bg[n] for n in _STAGES[stage]], f"scatter_start_{l}_{stage}")
        return sent[(l, stage)]["token"]

    loss, gx, gmeta, grads = local_step(x[0], loss_target[0], meta_full, getw, emit, ahead)

    small = jnp.concatenate([pack_small({n: jnp.stack([small_grads(g)[n] for g in grads]) for n in _SMALL}), gmeta,
                             jnp.pad(loss, ((0, 7), (0, D - 1)))], axis=0)
    hs = exchange_start("gather", [place_own(small[None], 0, F32, me_arr)], "small_start")

    out = {}
    after = hs["token"]
    for stage in ("ffn2", "mix", "ffn1"):
        names = _STAGES[stage]
        whole = [l for l in range(NL - 1, -1, -1) if (l, stage) != (0, "ffn1")]
        got = {l: exchange_wait(sent[(l, stage)], list(range(len(names))), f"scatter_wait_{l}_{stage}", after) for l in whole}
        for i, n in enumerate(names):
            one = {l: (got[l][0][i], got[l][1][i]) for l in whole}
            for l in set(range(NL)) - set(whole):
                s_, r_ = exchange_wait(sent[(l, stage)], [i], f"scatter_wait_{l}_{stage}_{i}", after)
                one[l] = (s_[0], r_[0])
            own = [one[l][0] for l in range(NL)]
            recv = [one[l][1] for l in range(NL)]
            if n == "w_in":
                g = jnp.stack([w_in_from_padded(sum_slots(recv[l], own[l], me_arr)) for l in range(NL)])
                out[n] = (g,) + adamw(P[n], M[n], V[n], g=g)
            else:
                out[n] = adamw(P[n], M[n], V[n], recv=recv, own=own, me_arr=me_arr)
                if n in _FFN_T:
                    out[n] = tuple(jnp.swapaxes(a, 1, 2) for a in out[n])
            after = out[n][1]
    gsmall = sum_slots(exchange_wait(hs, [0], "small_wait", after)[1][0])
    gm = lax.dynamic_slice(gsmall[NL * _SMALL_ROWS:], (0, me * (D // N_DEV)), (N_META, D // N_DEV))
    out["meta"] = (gm,) + adamw(P["meta"], M["meta"], V["meta"], g=gm)
    gs = gsmall[:NL * _SMALL_ROWS]
    sd, sm_, sv_ = adamw(pack_small(P), pack_small(M), pack_small(V), g=gs)
    ups = [unpack_small(a, P) for a in (gs, sd, sm_, sv_)]
    for n in _SMALL:
        out[n] = tuple(u[n] for u in ups)

    loss_all = gsmall[NL * _SMALL_ROWS + N_META, 0]
    flat = [loss_all, gx[None]]
    for k in range(4):
        flat += [out[n][k] for n in _NAMES]
    return tuple(flat)
```

```python
import functools
import math

import jax
import jax.numpy as jnp
from jax import lax
from jax.experimental import pallas as pl
from jax.experimental.pallas import tpu as pltpu

F32, BF16 = jnp.float32, jnp.bfloat16
HI = lax.Precision.HIGHEST

N_DEV = 8
D = 1024
NL = 2
N_META = 16
BLK = 128
PAD = BLK - N_META
D_FF = 2816
HS = D_FF // N_DEV
SSD_H, SSD_P, SSD_N, SSD_G = 8, 64, 64, 2
SSD_D = SSD_H * SSD_P
CONV_K = 4
CONV_D = SSD_D + 2 * SSD_G * SSD_N
FOX_H, FOX_DH = 4, 64
MLA_H, MLA_QL, MLA_KVL, MLA_NOPE, MLA_ROPE, MLA_V = 4, 256, 128, 64, 32, 64
N_IN = 2476
C_Z, C_XBC, C_FQ, C_FK, C_FV, C_CQ, C_CKV, C_SM, N_INP = 0, 512, 1280, 1536, 1792, 2048, 2304, 2432, 2560
SM_DT, SM_F, SM_KR = 0, 8, 64
ALPHA = (2 * NL) ** 0.25
EPS = 1e-5
NEG = -1e30
LR, B1, B2, AEPS, WD, STEP = 0.001, 0.9, 0.999, 1e-08, 0.01, 10
VMEM_MB = 56


def _cp(*sem):
    return pltpu.CompilerParams(dimension_semantics=sem, vmem_limit_bytes=VMEM_MB << 20)


def _nn(a, b):
    return lax.dot_general(a, b, (((1,), (0,)), ((), ())), preferred_element_type=F32)


def _nt(a, b):
    return lax.dot_general(a, b, (((1,), (1,)), ((), ())), preferred_element_type=F32)


def _tn(a, b):
    return lax.dot_general(a, b, (((0,), (0,)), ((), ())), preferred_element_type=F32)


def _nn_hi(a, b):
    return lax.dot_general(a, b, (((1,), (0,)), ((), ())), precision=HI, preferred_element_type=F32)


def _row_tile(t):
    for d in range(640, 15, -16):
        if t % d == 0:
            return d
    raise ValueError(t)


def _sig(x):
    return 1.0 / (1.0 + jnp.exp(-x))


def _tri(lower=True):
    r = lax.broadcasted_iota(jnp.int32, (BLK, BLK), 0)
    c = lax.broadcasted_iota(jnp.int32, (BLK, BLK), 1)
    return (r >= c) if lower else (r <= c)


def build_h0(meta_full, x):
    s = x.shape[0]
    nb = s // BLK + 1

    def body(m_ref, x_ref, h_ref, hb_ref):
        i = pl.program_id(0)

        @pl.when(i == 0)
        def _():
            h = jnp.concatenate([jnp.zeros((PAD, D), F32), m_ref[...]], axis=0)
            h_ref[...] = h
            hb_ref[...] = h.astype(BF16)

        @pl.when(i > 0)
        def _():
            h_ref[...] = x_ref[...]
            hb_ref[...] = x_ref[...].astype(BF16)

    return pl.pallas_call(
        body, name="build_h0", grid=(nb,),
        in_specs=[pl.BlockSpec((N_META, D), lambda i: (0, 0)),
                  pl.BlockSpec((BLK, D), lambda i: (jnp.maximum(i - 1, 0), 0))],
        out_specs=[pl.BlockSpec((BLK, D), lambda i: (i, 0))] * 2,
        out_shape=[jax.ShapeDtypeStruct((nb * BLK, D), F32), jax.ShapeDtypeStruct((nb * BLK, D), BF16)],
        compiler_params=_cp("arbitrary"),
    )(meta_full, x)


FT = 256


def _layer_norm(r, gamma, beta):
    mu = jnp.mean(r, axis=1, keepdims=True)
    xc = r - mu
    var = jnp.mean(xc * xc, axis=1, keepdims=True)
    return xc * lax.rsqrt(var + EPS) * gamma + beta


def ffn_fwd_seq(x, ln_in, wg, wu, wd, ln_out):
    t = x.shape[0]
    f = wg.shape[0]
    nj, nr = f // FT, t // _row_tile(t)
    rc = t // nr
    plain = ln_in is None
    gi, bi = ln_out if plain else ln_in

    def body(x_hbm, gi_ref, bi_ref, go_ref, bo_ref, wg_ref, wu_ref, wd_ref, u_ref, v_ref, r_hbm, yb_hbm,
             acc, hbs, xbuf, sem_in, sem_out):
        j = pl.program_id(0)

        @pl.when(j == 0)
        def _():
            def fetch(k):
                return pltpu.make_async_copy(x_hbm.at[pl.ds(k * rc, rc)], xbuf.at[k % 2], sem_in.at[k % 2])

            fetch(0).start()
            for k in range(nr):
                if k + 1 < nr:
                    fetch(k + 1).start()
                fetch(k).wait()
                h = xbuf[k % 2]
                if not plain:
                    h = _layer_norm(h, gi_ref[...], bi_ref[...])
                acc[k * rc:(k + 1) * rc, :] = ALPHA * h
                hbs[k * rc:(k + 1) * rc, :] = h.astype(BF16)

        def chunk(k, last):
            sl = slice(k * rc, (k + 1) * rc)
            h = hbs[sl, :]
            u = _nt(h, wg_ref[...])
            v = _nt(h, wu_ref[...])
            u_ref[sl, :] = u.astype(BF16)
            v_ref[sl, :] = v.astype(BF16)
            acc[sl, :] += _nn((0.5 * u * _sig(u) * v).astype(BF16), wd_ref[...])
            if not last:
                return []
            rows = pl.ds(k * rc, rc)
            cps = [pltpu.make_async_copy(acc.at[rows], r_hbm.at[rows], sem_out.at[2 * k])]
            cps[0].start()
            hbs[sl, :] = _layer_norm(acc[sl, :], go_ref[...], bo_ref[...]).astype(BF16)
            cps.append(pltpu.make_async_copy(hbs.at[rows], yb_hbm.at[rows], sem_out.at[2 * k + 1]))
            cps[1].start()
            return cps

        @pl.when(j < nj - 1)
        def _():
            for k in range(nr):
                chunk(k, False)

        @pl.when(j == nj - 1)
        def _():
            cps = []
            for k in range(nr):
                cps += chunk(k, True)
            for cp in cps:
                cp.wait()

    vec = pl.BlockSpec((1, D), lambda j: (0, 0))
    wsp = pl.BlockSpec((FT, D), lambda j: (j, 0))
    act = pl.BlockSpec((None, t, FT), lambda j: (j, 0, 0))
    return pl.pallas_call(
        body, name="ffn_fwd_seq", grid=(nj,),
        in_specs=[_ANY, vec, vec, vec, vec, wsp, wsp, wsp],
        out_specs=[act, act, _ANY, _ANY],
        out_shape=[jax.ShapeDtypeStruct((nj, t, FT), BF16), jax.ShapeDtypeStruct((nj, t, FT), BF16),
                   jax.ShapeDtypeStruct((t, D), F32), jax.ShapeDtypeStruct((t, D), BF16)],
        scratch_shapes=[pltpu.VMEM((t, D), F32), pltpu.VMEM((t, D), BF16), pltpu.VMEM((2, rc, D), F32),
                        pltpu.SemaphoreType.DMA((2,)), pltpu.SemaphoreType.DMA((2 * nr,))],
        compiler_params=_cp("arbitrary"),
    )(x, gi, bi, ln_out[0], ln_out[1], wg, wu, wd)


def ffn_bwd_seq(parts, r, gamma, hb, u, v, wg, wu, wd, after=None):
    nj, t, _ = u.shape
    f = nj * FT
    nr = 2 * (t // _row_tile(t))
    rc = t // nr
    nc = t // BLK
    scales = [s for _, s in parts]
    npart = len(parts)
    extra = [] if after is None else [after]

    def body(*refs):
        refs = refs[len(extra):]
        p_hbm, refs = refs[:npart], refs[npart:]
        (r_hbm, g_ref, hb_hbm, u_ref, v_ref, wg_ref, wu_ref, wd_ref, dh_hbm, dwg_ref, dwu_ref, dwd_ref, dg_ref, db_ref,
         dfs, hbt, dft, dhacc, dus, dvs, acs, pbuf, rbuf, hbuf, sems, sem_out) = refs
        j = pl.program_id(0)

        @pl.when(j == 0)
        def _():
            def fetch(c):
                rows = pl.ds(c * BLK, BLK)
                cps = [pltpu.make_async_copy(p_hbm[p].at[rows], pbuf.at[c % 2, p], sems.at[c % 2, p]) for p in range(npart)]
                cps.append(pltpu.make_async_copy(r_hbm.at[rows], rbuf.at[c % 2], sems.at[c % 2, npart]))
                cps.append(pltpu.make_async_copy(hb_hbm.at[rows], hbuf.at[c % 2], sems.at[c % 2, npart + 1]))
                return cps

            for cp in fetch(0):
                cp.start()
            dg = jnp.zeros((1, D), F32)
            db = jnp.zeros((1, D), F32)
            for c in range(nc):
                if c + 1 < nc:
                    for cp in fetch(c + 1):
                        cp.start()
                for cp in fetch(c):
                    cp.wait()
                sl = slice(c * BLK, (c + 1) * BLK)
                dy = scales[0] * pbuf[c % 2, 0]
                for p in range(1, npart):
                    dy += scales[p] * pbuf[c % 2, p]
                rr = rbuf[c % 2]
                xc = rr - jnp.mean(rr, axis=1, keepdims=True)
                rstd = lax.rsqrt(jnp.mean(xc * xc, axis=1, keepdims=True) + EPS)
                xh = xc * rstd
                dxh = dy * g_ref[...]
                dr = rstd * (dxh - jnp.mean(dxh, axis=1, keepdims=True) - xh * jnp.mean(dxh * xh, axis=1, keepdims=True))
                dg += jnp.sum(dy * xh, axis=0, keepdims=True)
                db += jnp.sum(dy, axis=0, keepdims=True)
                dhacc[sl, :] = ALPHA * dr
                dfc = (0.5 * dr).astype(BF16)
                dfs[sl, :] = dfc
                dft[:, sl] = dfc.T
                hbt[:, sl] = hbuf[c % 2].T
            dg_ref[...] = dg
            db_ref[...] = db

        for k in range(nr):
            sl = slice(k * rc, (k + 1) * rc)
            da = _nt(dfs[sl, :], wd_ref[...])
            uu = u_ref[sl, :].astype(F32)
            vv = v_ref[sl, :].astype(F32)
            sg = _sig(uu)
            du = (da * vv * (sg * (1.0 + uu * (1.0 - sg)))).astype(BF16)
            dv = (da * uu * sg).astype(BF16)
            dus[sl, :] = du
            dvs[sl, :] = dv
            acs[sl, :] = (uu * sg * vv).astype(BF16)
            dhacc[sl, :] += _nn(du, wg_ref[...]) + _nn(dv, wu_ref[...])
        @pl.when(j == nj - 1)
        def _():
            pltpu.make_async_copy(dhacc, dh_hbm, sem_out.at[0]).start()

        dwg_ref[...] = _nn(hbt[...], dus[...]).astype(BF16).T
        dwu_ref[...] = _nn(hbt[...], dvs[...]).astype(BF16).T
        dwd_ref[...] = _nn(dft[...], acs[...]).astype(BF16).T

        @pl.when(j == nj - 1)
        def _():
            pltpu.make_async_copy(dhacc, dh_hbm, sem_out.at[0]).wait()

    vec = pl.BlockSpec((1, D), lambda j: (0, 0))
    wsp = pl.BlockSpec((FT, D), lambda j: (j, 0))
    act = pl.BlockSpec((None, t, FT), lambda j: (j, 0, 0))
    return pl.pallas_call(
        body, name="ffn_bwd_seq", grid=(nj,),
        in_specs=[_ANY] * (len(extra) + npart + 1) + [vec, _ANY, act, act, wsp, wsp, wsp],
        out_specs=[_ANY, wsp, wsp, wsp, vec, vec],
        out_shape=[jax.ShapeDtypeStruct((t, D), F32)] + [jax.ShapeDtypeStruct((f, D), BF16)] * 3
        + [jax.ShapeDtypeStruct((1, D), F32)] * 2,
        scratch_shapes=[pltpu.VMEM((t, D), BF16), pltpu.VMEM((D, t), BF16), pltpu.VMEM((D, t), BF16),
                        pltpu.VMEM((t, D), F32), pltpu.VMEM((t, FT), BF16), pltpu.VMEM((t, FT), BF16),
                        pltpu.VMEM((t, FT), BF16), pltpu.VMEM((2, npart, BLK, D), F32), pltpu.VMEM((2, BLK, D), F32),
                        pltpu.VMEM((2, BLK, D), BF16), pltpu.SemaphoreType.DMA((2, npart + 2)),
                        pltpu.SemaphoreType.DMA((1,))],
        compiler_params=_cp("arbitrary"),
    )(*extra, *[p for p, _ in parts], r, gamma, hb, u, v, wg, wu, wd)


def mm_res_ln(pieces, b, x, ln_in, ln_out):
    t = x.shape[0]
    k = b.shape[0]
    tm = _row_tile(t)
    na = len(pieces)

    def body(*refs):
        b_ref, x_ref, gi_ref, bi_ref, go_ref, bo_ref, r_ref, yb_ref = refs[na:]
        a = jnp.concatenate([ref[...] for ref in refs[:na]], axis=1)
        r = ALPHA * _layer_norm(x_ref[...], gi_ref[...], bi_ref[...]) + _nn(a, b_ref[...])
        r_ref[...] = r
        yb_ref[...] = _layer_norm(r, go_ref[...], bo_ref[...]).astype(BF16)

    row = pl.BlockSpec((tm, D), lambda i: (i, 0))
    vec = pl.BlockSpec((1, D), lambda i: (0, 0))
    return pl.pallas_call(
        body, name="mm_res_ln", grid=(t // tm,),
        in_specs=[pl.BlockSpec((tm, p.shape[1]), lambda i: (i, 0)) for p in pieces]
        + [pl.BlockSpec((k, D), lambda i: (0, 0)), row, vec, vec, vec, vec],
        out_specs=[row, row],
        out_shape=[jax.ShapeDtypeStruct((t, D), F32), jax.ShapeDtypeStruct((t, D), BF16)],
        compiler_params=_cp("arbitrary"),
    )(*pieces, b, x, ln_in[0], ln_in[1], ln_out[0], ln_out[1])


def mm_nn(a, b):
    t, k = a.shape
    n = tn = b.shape[1]
    tm = _row_tile(t)

    def body(a_ref, b_ref, o_ref):
        o_ref[...] = _nn(a_ref[...], b_ref[...])

    return pl.pallas_call(
        body, name="mm_nn", grid=(t // tm, n // tn),
        in_specs=[pl.BlockSpec((tm, k), lambda i, j: (i, 0)), pl.BlockSpec((k, tn), lambda i, j: (0, j))],
        out_specs=pl.BlockSpec((tm, tn), lambda i, j: (i, j)),
        out_shape=jax.ShapeDtypeStruct((t, n), F32),
        compiler_params=_cp("arbitrary", "arbitrary"),
    )(a, b)


def oproj_bwd(dy, r, gamma, pieces, w_out, after=None):
    t = r.shape[0]
    tm = _row_tile(t)
    nt = t // tm
    extra = [] if after is None else [after]
    na = len(pieces)

    def body(*refs):
        refs = refs[len(extra):]
        dy_ref, r_ref, g_ref = refs[:3]
        w_ref, dr_ref, dm_ref, dw_ref, dg_ref, db_ref, acc = refs[3 + na:]
        mix = jnp.concatenate([ref[...] for ref in refs[3:3 + na]], axis=1)
        i = pl.program_id(0)
        dy = dy_ref[...]
        rr = r_ref[...]
        xc = rr - jnp.mean(rr, axis=1, keepdims=True)
        rstd = lax.rsqrt(jnp.mean(xc * xc, axis=1, keepdims=True) + EPS)
        xh = xc * rstd
        dxh = dy * g_ref[...]
        dr = rstd * (dxh - jnp.mean(dxh, axis=1, keepdims=True) - xh * jnp.mean(dxh * xh, axis=1, keepdims=True))
        dr_ref[...] = dr
        drb = dr.astype(BF16)
        dm_ref[...] = _nt(drb, w_ref[...])
        dw = _tn(mix, drb)
        dg = jnp.sum(dy * xh, axis=0, keepdims=True)
        db = jnp.sum(dy, axis=0, keepdims=True)

        @pl.when(i == 0)
        def _():
            acc[...] = dw
            dg_ref[...] = dg
            db_ref[...] = db

        @pl.when(i > 0)
        def _():
            acc[...] += dw
            dg_ref[...] += dg
            db_ref[...] += db

        @pl.when(i == nt - 1)
        def _():
            dw_ref[...] = acc[...].astype(BF16)

    row = pl.BlockSpec((tm, D), lambda i: (i, 0))
    vec = pl.BlockSpec((1, D), lambda i: (0, 0))
    mat = pl.BlockSpec((D, D), lambda i: (0, 0))
    return pl.pallas_call(
        body, name="oproj_bwd", grid=(nt,),
        in_specs=[_ANY] * len(extra) + [row, row, vec] + [pl.BlockSpec((tm, p.shape[1]), lambda i: (i, 0)) for p in pieces]
        + [mat],
        out_specs=[row, row, mat, vec, vec],
        out_shape=[jax.ShapeDtypeStruct((t, D), F32), jax.ShapeDtypeStruct((t, D), F32), jax.ShapeDtypeStruct((D, D), BF16),
                   jax.ShapeDtypeStruct((1, D), F32), jax.ShapeDtypeStruct((1, D), F32)],
        scratch_shapes=[pltpu.VMEM((D, D), F32)],
        compiler_params=_cp("arbitrary"),
    )(*extra, dy, r, gamma, *pieces, w_out)


def proj_bwd(pieces, hb, w_in):
    t = hb.shape[0]
    n = w_in.shape[1]
    tm = _row_tile(t)
    nt = t // tm
    widths = [p.shape[1] for p in pieces]
    starts = [sum(widths[:k]) for k in range(len(pieces))]
    assert sum(widths) == n

    def body(*refs):
        p_refs = refs[:len(pieces)]
        h_ref, w_ref, dh_ref, dw_ref, acc = refs[len(pieces):]
        i = pl.program_id(0)
        h = h_ref[...]
        dh = jnp.zeros((tm, D), F32)
        dws = []
        for p_ref, c0, w in zip(p_refs, starts, widths):
            pb = p_ref[...].astype(BF16)
            dh += _nt(pb, w_ref[:, c0:c0 + w])
            dws.append(_tn(h, pb))
        dh_ref[...] = dh

        @pl.when(i == 0)
        def _():
            for dw, c0, w in zip(dws, starts, widths):
                acc[:, c0:c0 + w] = dw

        @pl.when(i > 0)
        def _():
            for dw, c0, w in zip(dws, starts, widths):
                acc[:, c0:c0 + w] += dw

        @pl.when(i == nt - 1)
        def _():
            dw_ref[...] = acc[...].astype(BF16)

    mat = pl.BlockSpec((D, n), lambda i: (0, 0))
    return pl.pallas_call(
        body, name="proj_bwd", grid=(nt,),
        in_specs=[pl.BlockSpec((tm, w), lambda i: (i, 0)) for w in widths] + [pl.BlockSpec((tm, D), lambda i: (i, 0)), mat],
        out_specs=[pl.BlockSpec((tm, D), lambda i: (i, 0)), mat],
        out_shape=[jax.ShapeDtypeStruct((t, D), F32), jax.ShapeDtypeStruct((D, n), BF16)],
        scratch_shapes=[pltpu.VMEM((D, n), F32)],
        compiler_params=_cp("arbitrary"),
    )(*pieces, hb, w_in)


def loss_head(r, ln, target):
    t = r.shape[0]
    nb = t // BLK

    def body(r_ref, g_ref, b_ref, t_ref, dy_ref, l_ref):
        i = pl.program_id(0)

        @pl.when(i == 0)
        def _():
            dy_ref[...] = jnp.zeros_like(dy_ref)
            l_ref[...] = jnp.zeros_like(l_ref)

        @pl.when(i > 0)
        def _():
            err = _layer_norm(r_ref[...], g_ref[...], b_ref[...]) - t_ref[...]
            dy_ref[...] = err * (1.0 / D)
            l_ref[...] += (0.5 / D) * jnp.sum(err * err, keepdims=True)

    vec = pl.BlockSpec((1, D), lambda i: (0, 0))
    return pl.pallas_call(
        body, name="loss_head", grid=(nb,),
        in_specs=[pl.BlockSpec((BLK, D), lambda i: (i, 0)), vec, vec,
                  pl.BlockSpec((BLK, D), lambda i: (jnp.maximum(i - 1, 0), 0))],
        out_specs=[pl.BlockSpec((BLK, D), lambda i: (i, 0)), pl.BlockSpec((1, 1), lambda i: (0, 0))],
        out_shape=[jax.ShapeDtypeStruct((t, D), F32), jax.ShapeDtypeStruct((1, 1), F32)],
        compiler_params=_cp("arbitrary"),
    )(r, ln[0], ln[1], target)


def split_dh0(dh0, after=None):
    t = dh0.shape[0]
    nb = t // BLK
    extra = [] if after is None else [after]

    def body(*refs):
        a_ref, gx_ref, gm_ref = refs[len(extra):]
        i = pl.program_id(0)
        tot = a_ref[...]

        @pl.when(i == 0)
        def _():
            gm_ref[...] = tot[PAD:, :]

        @pl.when(i > 0)
        def _():
            gx_ref[...] = tot

    blk = pl.BlockSpec((BLK, D), lambda i: (i, 0))
    return pl.pallas_call(
        body, name="split_dh0", grid=(nb,),
        in_specs=[_ANY] * len(extra) + [blk],
        out_specs=[pl.BlockSpec((BLK, D), lambda i: (jnp.maximum(i - 1, 0), 0)),
                   pl.BlockSpec((N_META, D), lambda i: (0, 0))],
        out_shape=[jax.ShapeDtypeStruct((t - BLK, D), F32), jax.ShapeDtypeStruct((N_META, D), F32)],
        compiler_params=_cp("arbitrary"),
    )(*extra, dh0)


def _valid_rows(nrows, first_row):
    return (first_row + lax.broadcasted_iota(jnp.int32, (nrows, 1), 0)) >= PAD


def conv_fwd(proj, conv_w, conv_b):
    t = proj.shape[0]
    c0 = C_XBC // BLK

    def body(x_ref, w_ref, b_ref, o_ref):
        ok = _valid_rows(t, 0)
        x = jnp.where(ok, x_ref[...], 0.0)
        w = w_ref[...]
        acc = b_ref[...] + w[CONV_K - 1:CONV_K, :] * x
        for s in range(1, CONV_K):
            acc += w[CONV_K - 1 - s:CONV_K - s, :] * pltpu.roll(x, s, 0)
        o_ref[...] = jnp.where(ok, acc * _sig(acc), 0.0)

    return pl.pallas_call(
        body, name="conv_fwd", grid=(CONV_D // BLK,),
        in_specs=[pl.BlockSpec((t, BLK), lambda j: (0, c0 + j)),
                  pl.BlockSpec((CONV_K, BLK), lambda j: (0, j)), pl.BlockSpec((1, BLK), lambda j: (0, j))],
        out_specs=pl.BlockSpec((t, BLK), lambda j: (0, j)),
        out_shape=jax.ShapeDtypeStruct((t, CONV_D), F32),
        compiler_params=_cp("arbitrary"),
    )(proj, conv_w, conv_b)


def conv_bwd(dxa, proj, conv_w, conv_b):
    t = proj.shape[0]
    c0 = C_XBC // BLK

    def body(d_ref, x_ref, w_ref, b_ref, dx_ref, dw_ref, db_ref):
        ok = _valid_rows(t, 0)
        x = jnp.where(ok, x_ref[...], 0.0)
        w = w_ref[...]
        xs = [x] + [pltpu.roll(x, s, 0) for s in range(1, CONV_K)]
        acc = b_ref[...] + w[CONV_K - 1:CONV_K, :] * x
        for s in range(1, CONV_K):
            acc += w[CONV_K - 1 - s:CONV_K - s, :] * xs[s]
        sg = _sig(acc)
        dxc = jnp.where(ok, d_ref[...] * (sg * (1.0 + acc * (1.0 - sg))), 0.0)
        db_ref[...] = jnp.sum(dxc, axis=0, keepdims=True)
        dw_ref[...] = jnp.concatenate(
            [jnp.sum(dxc * xs[CONV_K - 1 - k], axis=0, keepdims=True) for k in range(CONV_K)], axis=0)
        dx = w[CONV_K - 1:CONV_K, :] * dxc
        for s in range(1, CONV_K):
            dx += w[CONV_K - 1 - s:CONV_K - s, :] * pltpu.roll(dxc, t - s, 0)
        dx_ref[...] = jnp.where(ok, dx, 0.0)

    col = pl.BlockSpec((t, BLK), lambda j: (0, j))
    return pl.pallas_call(
        body, name="conv_bwd", grid=(CONV_D // BLK,),
        in_specs=[col, pl.BlockSpec((t, BLK), lambda j: (0, c0 + j)),
                  pl.BlockSpec((CONV_K, BLK), lambda j: (0, j)), pl.BlockSpec((1, BLK), lambda j: (0, j))],
        out_specs=[col, pl.BlockSpec((CONV_K, BLK), lambda j: (0, j)), pl.BlockSpec((1, BLK), lambda j: (0, j))],
        out_shape=[jax.ShapeDtypeStruct((t, CONV_D), F32), jax.ShapeDtypeStruct((CONV_K, CONV_D), F32),
                   jax.ShapeDtypeStruct((1, CONV_D), F32)],
        compiler_params=_cp("arbitrary"),
    )(dxa, proj, conv_w, conv_b)


def _softplus(x):
    return jnp.maximum(x, 0.0) + jnp.log(1.0 + jnp.exp(-jnp.abs(x)))


GW = SSD_D // SSD_G
HPG = SSD_H // SSD_G


def _head_expand():
    r = lax.broadcasted_iota(jnp.int32, (BLK, SSD_D), 0)
    c = lax.broadcasted_iota(jnp.int32, (BLK, SSD_D), 1)
    rt = lax.broadcasted_iota(jnp.int32, (SSD_D, BLK), 0)
    ct = lax.broadcasted_iota(jnp.int32, (SSD_D, BLK), 1)
    return (c // SSD_P == r).astype(F32), (rt // SSD_P == ct).astype(F32)


def _ssd_chunk(xa, sm, dtb, alog, dskip, ok, sp):
    e, et = _head_expand()
    dt = jnp.where(ok, _softplus(sm + dtb), 0.0)
    amat = -jnp.exp(alog)
    tri = _tri()
    ac = _nn_hi(tri.astype(F32), dt * amat)
    act = ac.T
    ace, dte, dse = _nn_hi(ac, e), _nn_hi(dt, e), _nn_hi(dskip, e)
    laste = ace[BLK - 1:BLK, :]
    ee, dece, gle = jnp.exp(ace), jnp.exp(laste - ace), jnp.exp(laste)
    xs = xa[:, :SSD_D]
    xdt = xs * dte
    decx = dece * xdt
    xdtb = xdt.astype(BF16)
    d = dict(e=e, et=et, dt=dt, amat=amat, tri=tri, ac=ac, act=act, dte=dte, dse=dse, ee=ee, dece=dece, gle=gle, xs=xs,
             xdt=xdt, xdtb=xdtb, decx=decx, bg=[], cg=[], cb=[], yo=[], seg=[], m=[], new_s=[])
    ys = []
    for g in range(SSD_G):
        cols = slice(GW * g, GW * (g + 1))
        bg = xa[:, SSD_D + SSD_N * g:SSD_D + SSD_N * (g + 1)].astype(BF16)
        cg = xa[:, SSD_D + SSD_G * SSD_N + SSD_N * g:SSD_D + SSD_G * SSD_N + SSD_N * (g + 1)].astype(BF16)
        spg = sp[:, cols]
        sloc = _tn(bg, decx[:, cols].astype(BF16))
        yo = _nn(cg, spg.astype(BF16)) * ee[:, cols]
        cb = _nt(cg, bg)
        d["new_s"].append(gle[:, cols] * spg + sloc)
        yds = []
        for h in range(HPG * g, HPG * (g + 1)):
            seg = jnp.where(tri, jnp.exp(jnp.minimum(ac[:, h:h + 1] - act[h:h + 1, :], 0.0)), 0.0)
            m = cb * seg
            yds.append(_nn(m.astype(BF16), xdtb[:, SSD_P * h:SSD_P * (h + 1)]))
            d["seg"].append(seg)
            d["m"].append(m)
        ys.append(jnp.concatenate(yds, axis=1) + yo)
        for k, val in (("bg", bg), ("cg", cg), ("cb", cb), ("yo", yo)):
            d[k].append(val)
    d["y"] = jnp.concatenate(ys, axis=1) + dse * xs
    return d


def ssd_fwd(xa, proj, dtb, alog, dskip, normg):
    t = xa.shape[0]
    nb = t // BLK
    gw = SSD_D // SSD_G

    def body(xa_ref, z_ref, sm_ref, dtb_ref, al_ref, ds_ref, ng_ref, y_ref, sp_ref, st):
        c = pl.program_id(0)

        @pl.when(c == 0)
        def _():
            st[...] = jnp.zeros_like(st)

        ok = _valid_rows(BLK, c * BLK)
        sp = st[...]
        sp_ref[...] = sp
        d = _ssd_chunk(xa_ref[...], sm_ref[...], dtb_ref[...], al_ref[...], ds_ref[...], ok, sp)
        st[...] = jnp.concatenate(d["new_s"], axis=1)
        y = d["y"]
        z = z_ref[...]
        yg = y * (z * _sig(z))
        outs = []
        for g in range(SSD_G):
            v = yg[:, gw * g:gw * (g + 1)]
            outs.append(v * lax.rsqrt(jnp.mean(v * v, axis=1, keepdims=True) + EPS))
        y_ref[...] = (jnp.concatenate(outs, axis=1) * ng_ref[...]).astype(BF16)

    vec = pl.BlockSpec((1, BLK), lambda c: (0, 0))
    return pl.pallas_call(
        body, name="ssd_fwd", grid=(nb,),
        in_specs=[pl.BlockSpec((BLK, CONV_D), lambda c: (c, 0)),
                  pl.BlockSpec((BLK, SSD_D), lambda c: (c, C_Z // SSD_D)),
                  pl.BlockSpec((BLK, BLK), lambda c: (c, C_SM // BLK)),
                  vec, vec, vec, pl.BlockSpec((1, SSD_D), lambda c: (0, 0))],
        out_specs=[pl.BlockSpec((BLK, SSD_D), lambda c: (c, 0)),
                   pl.BlockSpec((None, SSD_N, SSD_D), lambda c: (c, 0, 0))],
        out_shape=[jax.ShapeDtypeStruct((t, SSD_D), BF16), jax.ShapeDtypeStruct((nb, SSD_N, SSD_D), F32)],
        scratch_shapes=[pltpu.VMEM((SSD_N, SSD_D), F32)],
        compiler_params=_cp("arbitrary"),
    )(xa, proj, proj, dtb, alog, dskip, normg)


def _lane_put(col, lane):
    li = lax.broadcasted_iota(jnp.int32, (col.shape[0], BLK), 1)
    return jnp.where(li == lane, col, 0.0)


def ssd_bwd(dmix, xa, proj, sprev, dtb, alog, dskip, normg):
    t = xa.shape[0]
    nb = t // BLK
    gw = SSD_D // SSD_G
    rev = lambda c: nb - 1 - c

    def body(dy_ref, xa_ref, z_ref, sm_ref, sp_ref, dtb_ref, al_ref, ds_ref, ng_ref,
             dxa_ref, dz_ref, dsm_ref, dng_ref, dds_ref, dal_ref, ddtb_ref, dst):
        c = pl.program_id(0)

        @pl.when(c == 0)
        def _():
            dst[...] = jnp.zeros_like(dst)
            dng_ref[...] = jnp.zeros_like(dng_ref)
            dds_ref[...] = jnp.zeros_like(dds_ref)
            dal_ref[...] = jnp.zeros_like(dal_ref)
            ddtb_ref[...] = jnp.zeros_like(ddtb_ref)

        ok = _valid_rows(BLK, rev(c) * BLK)
        sm = sm_ref[...]
        sp = sp_ref[...]
        d = _ssd_chunk(xa_ref[...], sm, dtb_ref[...], al_ref[...], ds_ref[...], ok, sp)
        dt, amat, ac, act, tri, et, xs, xdt = (d[k] for k in ("dt", "amat", "ac", "act", "tri", "et", "xs", "xdt"))
        rowi = lax.broadcasted_iota(jnp.int32, (BLK, 1), 0)
        y = d["y"]
        z = z_ref[...]
        sgz = _sig(z)
        siluz = z * sgz
        yg = y * siluz
        dout = dy_ref[...]
        ng = ng_ref[...]
        dygs, xhs = [], []
        for g in range(SSD_G):
            v = yg[:, gw * g:gw * (g + 1)]
            rr = lax.rsqrt(jnp.mean(v * v, axis=1, keepdims=True) + EPS)
            xh = v * rr
            dxh = dout[:, gw * g:gw * (g + 1)] * ng[:, gw * g:gw * (g + 1)]
            dygs.append(rr * (dxh - xh * jnp.mean(dxh * xh, axis=1, keepdims=True)))
            xhs.append(xh)
        dyg = jnp.concatenate(dygs, axis=1)
        dng_ref[...] += jnp.sum(dout * jnp.concatenate(xhs, axis=1), axis=0, keepdims=True)
        dy = dyg * siluz
        dz_ref[...] = dyg * y * (sgz * (1.0 + z * (1.0 - sgz)))

        triu = _tri(lower=False)
        dyb = dy.astype(BF16)
        dsn = dst[...]
        dds_ref[...] += _nn_hi(jnp.sum(dy * xs, axis=0, keepdims=True), et)
        dac_all = _nn_hi(dy * jnp.concatenate(d["yo"], axis=1), et)
        dyo = (dy * d["ee"]).astype(BF16)
        gl = jnp.exp(ac[BLK - 1:BLK, :])
        dlast = _nn_hi(jnp.sum(dsn * sp, axis=0, keepdims=True), et) * gl
        bds, db_g, dc_g, dxdt_i, new_dst = [], [], [], [], []
        for g in range(SSD_G):
            cols = slice(GW * g, GW * (g + 1))
            bg, cg = d["bg"][g], d["cg"][g]
            dsng = dsn[:, cols].astype(BF16)
            dc = _nt(dyo[:, cols], sp[:, cols].astype(BF16))
            new_dst.append(_tn(cg, dyo[:, cols]) + d["gle"][:, cols] * dsn[:, cols])
            bds.append(_nn(bg, dsng))
            db = _nt(d["decx"][:, cols].astype(BF16), dsng)
            cbt = _nt(bg, cg)
            dcb = jnp.zeros((BLK, BLK), F32)
            for h in range(HPG * g, HPG * (g + 1)):
                hc = slice(SSD_P * h, SSD_P * (h + 1))
                dm = _nt(dyb[:, hc], d["xdtb"][:, hc])
                dcb += dm * d["seg"][h]
                w = dm * d["m"][h]
                dac_all += _lane_put(jnp.sum(w, axis=1, keepdims=True) - jnp.sum(w.T, axis=1, keepdims=True), h)
                segt = jnp.where(triu, jnp.exp(jnp.minimum(act[h:h + 1, :] - ac[:, h:h + 1], 0.0)), 0.0)
                dxdt_i.append(_nn((cbt * segt).astype(BF16), dyb[:, hc]))
            dcbb = dcb.astype(BF16)
            dc_g.append(dc + _nn(dcbb, bg))
            db_g.append(db + _tn(dcbb, cg))
        dst[...] = jnp.concatenate(new_dst, axis=1)
        bds = jnp.concatenate(bds, axis=1)
        tdec = jnp.exp(ac[BLK - 1:BLK, :] - ac) * _nn_hi(xdt * bds, et)
        dlast += jnp.sum(tdec, axis=0, keepdims=True)
        dac_all += jnp.where(rowi == BLK - 1, dlast, 0.0) - tdec
        dxdt = d["dece"] * bds + jnp.concatenate(dxdt_i, axis=1)
        da = _nn_hi(triu.astype(F32), dac_all)
        ddt = _nn_hi(dxdt * xs, et) + da * amat
        dal_ref[...] += jnp.sum(da * dt, axis=0, keepdims=True) * amat
        ddtr = jnp.where(ok, ddt * _sig(sm + dtb_ref[...]), 0.0)
        ddtb_ref[...] += jnp.sum(ddtr, axis=0, keepdims=True)
        dsm_ref[...] = ddtr
        dxs = d["dse"] * dy + dxdt * d["dte"]
        dxa_ref[...] = jnp.where(ok, jnp.concatenate([dxs] + db_g + dc_g, axis=1), 0.0)

    vec = pl.BlockSpec((1, BLK), lambda c: (0, 0))
    nvec = pl.BlockSpec((1, SSD_D), lambda c: (0, 0))
    return pl.pallas_call(
        body, name="ssd_bwd", grid=(nb,),
        in_specs=[pl.BlockSpec((BLK, SSD_D), lambda c: (rev(c), 0)),
                  pl.BlockSpec((BLK, CONV_D), lambda c: (rev(c), 0)),
                  pl.BlockSpec((BLK, SSD_D), lambda c: (rev(c), C_Z // SSD_D)),
                  pl.BlockSpec((BLK, BLK), lambda c: (rev(c), C_SM // BLK)),
                  pl.BlockSpec((None, SSD_N, SSD_D), lambda c: (rev(c), 0, 0)),
                  vec, vec, vec, nvec],
        out_specs=[pl.BlockSpec((BLK, CONV_D), lambda c: (rev(c), 0)),
                   pl.BlockSpec((BLK, SSD_D), lambda c: (rev(c), 0)),
                   pl.BlockSpec((BLK, BLK), lambda c: (rev(c), 0)),
                   nvec, vec, vec, vec],
        out_shape=[jax.ShapeDtypeStruct((t, CONV_D), F32), jax.ShapeDtypeStruct((t, SSD_D), F32),
                   jax.ShapeDtypeStruct((t, BLK), F32), jax.ShapeDtypeStruct((1, SSD_D), F32),
                   jax.ShapeDtypeStruct((1, BLK), F32), jax.ShapeDtypeStruct((1, BLK), F32),
                   jax.ShapeDtypeStruct((1, BLK), F32)],
        scratch_shapes=[pltpu.VMEM((SSD_N, SSD_D), F32)],
        compiler_params=_cp("arbitrary"),
    )(dmix, xa, proj, proj, sprev, dtb, alog, dskip, normg)


def _segments(nb, fine):
    if fine:
        cuts = list(range(0, nb, 2)) + [nb]
    else:
        cuts = sorted({0, nb} | {max(1, round(nb * f)) for f in (0.3, 0.53, 0.77)})
    return list(zip(cuts[:-1], cuts[1:]))


def attn_fwd(q, k, v, qcol, kcol, vcol, nh, dq, dv, scale, c_col=None, c_row=None, lane0=0):
    t = q.shape[0]
    tq = BLK
    use_bias = c_col is not None
    pow2 = math.frexp(scale)[0] == 0.5

    def body(*refs):
        if use_bias:
            q_ref, k_ref, v_ref, cc_ref, cr_ref, o_ref, l_ref = refs
        else:
            q_ref, k_ref, v_ref, o_ref, l_ref = refs
        i = pl.program_id(0)
        rowg = i * tq + lax.broadcasted_iota(jnp.int32, (tq, 1), 0)

        def tile(tk):
            col = lax.broadcasted_iota(jnp.int32, (1, tk), 1)
            mask = (col <= rowg) & (col >= PAD)
            outs = []
            lse = jnp.zeros((tq, BLK), F32)
            for h in range(nh):
                qh = q_ref[:, dq * h:dq * (h + 1)]
                kh = k_ref[0:tk, dq * h:dq * (h + 1)].astype(BF16)
                if pow2:
                    s = _nt((qh * scale).astype(BF16), kh)
                else:
                    s = _nt(qh.astype(BF16), kh) * scale
                if use_bias:
                    s = s + (cc_ref[:, lane0 + h:lane0 + h + 1] - cr_ref[h:h + 1, 0:tk])
                s = jnp.where(mask, s, NEG)
                m = jnp.max(s, axis=1, keepdims=True)
                p = jnp.exp(s - m)
                l = jnp.sum(p, axis=1, keepdims=True)
                outs.append(_nn(p.astype(BF16), v_ref[0:tk, dv * h:dv * (h + 1)].astype(BF16)) / l)
                lse += _lane_put(m + jnp.log(l), h)
            o_ref[...] = jnp.concatenate(outs, axis=1).astype(BF16)
            l_ref[...] = lse.T[0:8, :]

        for t0, t1 in _segments(t // tq, True):
            pl.when((i >= t0) & (i < t1))(functools.partial(tile, t1 * BLK))

    in_specs = [pl.BlockSpec((tq, nh * dq), lambda i: (i, qcol)),
                pl.BlockSpec((t, nh * dq), lambda i: (0, kcol)),
                pl.BlockSpec((t, nh * dv), lambda i: (0, vcol))]
    args = [q, k, v]
    if use_bias:
        in_specs += [pl.BlockSpec((tq, BLK), lambda i: (i, 0)), pl.BlockSpec((8, t), lambda i: (0, 0))]
        args += [c_col, c_row]
    return pl.pallas_call(
        body, name="attn_fwd", grid=(t // tq,),
        in_specs=in_specs,
        out_specs=[pl.BlockSpec((tq, nh * dv), lambda i: (i, 0)), pl.BlockSpec((8, tq), lambda i: (0, i))],
        out_shape=[jax.ShapeDtypeStruct((t, nh * dv), BF16), jax.ShapeDtypeStruct((8, t), F32)],
        compiler_params=_cp("arbitrary"),
    )(*args)


def attn_bwd(q, k, v, do, lse_row, o, qcol, kcol, vcol, docol, ocol, nh, dq, dv, scale, c_col=None, c_row=None, lane0=0):
    t = q.shape[0]
    tq = BLK
    use_bias = c_col is not None
    pow2 = math.frexp(scale)[0] == 0.5
    nq = t // tq

    def body(*refs):
        if use_bias:
            (q_ref, k_ref, v_ref, do_ref, l_ref, o_ref, cc_ref, cr_ref, dq_ref, dk_ref, dv_ref, dcq_ref, dck_ref,
             kt, ckb, dacc) = refs
        else:
            q_ref, k_ref, v_ref, do_ref, l_ref, o_ref, dq_ref, dk_ref, dv_ref, kt = refs
        i = pl.program_id(0)

        @pl.when(i == 0)
        def _():
            kt[...] = k_ref[...].astype(BF16).T
            dk_ref[...] = jnp.zeros_like(dk_ref)
            dv_ref[...] = jnp.zeros_like(dv_ref)
            if use_bias:
                dacc[...] = jnp.zeros_like(dacc)
                for h in range(nh):
                    ckb[h] = jnp.broadcast_to(cc_ref[:, lane0 + h:lane0 + h + 1], (t, BLK))

        qry = i * tq + lax.broadcasted_iota(jnp.int32, (1, tq), 1)
        dot = (do_ref[...].astype(F32) * o_ref[...].astype(F32)).T

        def tile(tk):
            key = lax.broadcasted_iota(jnp.int32, (tk, 1), 0)
            mask = (key <= qry) & (key >= PAD)
            dqts, dcqs = [], []
            for h in range(nh):
                kh = k_ref[0:tk, dq * h:dq * (h + 1)].astype(BF16)
                vh = v_ref[0:tk, dv * h:dv * (h + 1)].astype(BF16)
                doh = do_ref[:, dv * h:dv * (h + 1)].astype(BF16)
                delta = jnp.sum(dot[dv * h:dv * (h + 1), :], axis=0, keepdims=True)
                if pow2:
                    qh = (q_ref[:, dq * h:dq * (h + 1)] * scale).astype(BF16)
                    st = _nt(kh, qh)
                else:
                    qh = q_ref[:, dq * h:dq * (h + 1)].astype(BF16)
                    st = _nt(kh, qh) * scale
                if use_bias:
                    st = st + (cr_ref[h:h + 1, :] - ckb[h, 0:tk, :])
                pt = jnp.exp(jnp.where(mask, st, NEG) - l_ref[h:h + 1, :])
                dst = pt * (_nt(vh, doh) - delta)
                dsb = dst.astype(BF16)
                dk_ref[0:tk, dq * h:dq * (h + 1)] += _nn(dsb, qh) if pow2 else _nn(dsb, qh) * scale
                dv_ref[0:tk, dv * h:dv * (h + 1)] += _nn(pt.astype(BF16), doh)
                dqts.append(_nn(kt[dq * h:dq * (h + 1), 0:tk], dsb))
                if use_bias:
                    dcqs.append(jnp.sum(dst, axis=0, keepdims=True))
                    dacc[h, 0:tk, :] += dst
            dq_ref[...] = jnp.concatenate(dqts, axis=0).T * scale
            if use_bias:
                dcq_ref[...] = jnp.concatenate(dcqs + [jnp.zeros((8 - nh, tq), F32)], axis=0)

        for t0, t1 in _segments(nq, not use_bias):
            pl.when((i >= t0) & (i < t1))(functools.partial(tile, t1 * BLK))

        if use_bias:
            @pl.when(i == nq - 1)
            def _():
                lane = lax.broadcasted_iota(jnp.int32, (1, BLK), 1)
                tot = jnp.zeros((t, BLK), F32)
                for h in range(nh):
                    tot += jnp.where(lane == lane0 + h, jnp.sum(dacc[h], axis=1, keepdims=True), 0.0)
                dck_ref[...] = tot

    keys_q = pl.BlockSpec((t, nh * dq), lambda i: (0, 0))
    keys_v = pl.BlockSpec((t, nh * dv), lambda i: (0, 0))
    keys_c = pl.BlockSpec((t, BLK), lambda i: (0, 0))
    qrow = pl.BlockSpec((8, tq), lambda i: (0, i))
    in_specs = [pl.BlockSpec((tq, nh * dq), lambda i: (i, qcol)),
                pl.BlockSpec((t, nh * dq), lambda i: (0, kcol)),
                pl.BlockSpec((t, nh * dv), lambda i: (0, vcol)),
                pl.BlockSpec((tq, nh * dv), lambda i: (i, docol)),
                qrow,
                pl.BlockSpec((tq, nh * dv), lambda i: (i, ocol))]
    args = [q, k, v, do, lse_row, o]
    out_specs = [pl.BlockSpec((tq, nh * dq), lambda i: (i, 0)), keys_q, keys_v]
    out_shape = [jax.ShapeDtypeStruct((t, nh * dq), F32), jax.ShapeDtypeStruct((t, nh * dq), F32),
                 jax.ShapeDtypeStruct((t, nh * dv), F32)]
    scratch = [pltpu.VMEM((nh * dq, t), BF16)]
    if use_bias:
        in_specs += [keys_c, qrow]
        args += [c_col, c_row]
        out_specs += [qrow, keys_c]
        out_shape += [jax.ShapeDtypeStruct((8, t), F32), jax.ShapeDtypeStruct((t, BLK), F32)]
        scratch += [pltpu.VMEM((nh, t, BLK), F32), pltpu.VMEM((nh, t, BLK), F32)]
    return pl.pallas_call(
        body, name="attn_bwd", grid=(nq,),
        in_specs=in_specs, out_specs=out_specs, out_shape=out_shape, scratch_shapes=scratch,
        compiler_params=_cp("arbitrary"),
    )(*args)


def fox_pre(proj, fb):
    t = proj.shape[0]
    nb = t // BLK

    def body(sm_ref, fb_ref, c_ref, cr_ref):
        x = sm_ref[...] + fb_ref[...]
        lane = lax.broadcasted_iota(jnp.int32, (1, BLK), 1)
        keep = _valid_rows(t, 0) & (lane >= SM_F) & (lane < SM_F + FOX_H)
        logf = jnp.where(keep, jnp.minimum(x, 0.0) - jnp.log(1.0 + jnp.exp(-jnp.abs(x))), 0.0)
        tri = _tri().astype(F32)
        carry = jnp.zeros((1, BLK), F32)
        for b in range(nb):
            cb = _nn_hi(tri, logf[b * BLK:(b + 1) * BLK, :]) + carry
            c_ref[b * BLK:(b + 1) * BLK, :] = cb
            carry = cb[BLK - 1:BLK, :]
        cr_ref[...] = c_ref[...].T[SM_F:SM_F + 8, :]

    return pl.pallas_call(
        body, name="fox_pre", grid=(1,),
        in_specs=[pl.BlockSpec((t, BLK), lambda i: (0, C_SM // BLK)), pl.BlockSpec((1, BLK), lambda i: (0, 0))],
        out_specs=[pl.BlockSpec((t, BLK), lambda i: (0, 0)), pl.BlockSpec((8, t), lambda i: (0, 0))],
        out_shape=[jax.ShapeDtypeStruct((t, BLK), F32), jax.ShapeDtypeStruct((8, t), F32)],
        compiler_params=_cp("arbitrary"),
    )(proj, fb)


def fox_pre_bwd(dcq, dck, proj, fb, dsm_in):
    t = proj.shape[0]
    nb = t // BLK

    def body(dcq_ref, dck_ref, sm_ref, fb_ref, din_ref, dsm_ref, dfb_ref, scr):
        triu = _tri(lower=False).astype(F32)
        carry = jnp.zeros((1, BLK), F32)
        scr[...] = jnp.concatenate([jnp.zeros((SM_F, t), F32), dcq_ref[...], jnp.zeros((BLK - SM_F - 8, t), F32)], axis=0).T
        for b in range(nb - 1, -1, -1):
            blk = scr[b * BLK:(b + 1) * BLK, :] - dck_ref[b * BLK:(b + 1) * BLK, :]
            cb = _nn_hi(triu, blk) + carry
            scr[b * BLK:(b + 1) * BLK, :] = cb
            carry = cb[0:1, :]
        x = sm_ref[...] + fb_ref[...]
        lane = lax.broadcasted_iota(jnp.int32, (1, BLK), 1)
        keep = _valid_rows(t, 0) & (lane >= SM_F) & (lane < SM_F + FOX_H)
        df = jnp.where(keep, scr[...] * _sig(-x), 0.0)
        dfb_ref[...] = jnp.sum(df, axis=0, keepdims=True)
        dsm_ref[...] = din_ref[...] + df

    full = pl.BlockSpec((t, BLK), lambda i: (0, 0))
    return pl.pallas_call(
        body, name="fox_pre_bwd", grid=(1,),
        in_specs=[pl.BlockSpec((8, t), lambda i: (0, 0)), full,
                  pl.BlockSpec((t, BLK), lambda i: (0, C_SM // BLK)), pl.BlockSpec((1, BLK), lambda i: (0, 0)), full],
        out_specs=[full, pl.BlockSpec((1, BLK), lambda i: (0, 0))],
        out_shape=[jax.ShapeDtypeStruct((t, BLK), F32), jax.ShapeDtypeStruct((1, BLK), F32)],
        scratch_shapes=[pltpu.VMEM((t, BLK), F32)],
        compiler_params=_cp("arbitrary"),
    )(dcq, dck, proj, fb, dsm_in)


def _swap_rope(x):
    lane = lax.broadcasted_iota(jnp.int32, (1, BLK), 1)
    return jnp.where((lane >= SM_KR) & (lane < SM_KR + 16), pltpu.roll(x, BLK - 16, 1),
                     jnp.where((lane >= SM_KR + 16) & (lane < SM_KR + 32), pltpu.roll(x, 16, 1), 0.0))


def _rms(x, g):
    r = lax.rsqrt(jnp.mean(x * x, axis=1, keepdims=True) + EPS)
    return r, x * r


def mla_pre(proj, qg, kvg, wq, wk, wv, cosq, sinq):
    t = proj.shape[0]
    tm = _row_tile(t)

    def body(cq_ref, ckv_ref, sm_ref, qg_ref, kvg_ref, wq_ref, wk_ref, wv_ref, cos_ref, sin_ref,
             q_ref, k_ref, v_ref, cqn_ref, ckvn_ref):
        cs, sn = cos_ref[...], sin_ref[...]
        _, xh = _rms(cq_ref[...], None)
        cqn = (xh * qg_ref[...]).astype(BF16)
        cqn_ref[...] = cqn
        qraw = _nn(cqn, wq_ref[...])
        qs = []
        for h in range(MLA_H):
            hb = qraw[:, BLK * h:BLK * (h + 1)]
            qs.append(hb * cs + _swap_rope(hb) * sn)
        q_ref[...] = jnp.concatenate(qs, axis=1).astype(BF16)
        _, kh = _rms(ckv_ref[...], None)
        ckvn = (kh * kvg_ref[...]).astype(BF16)
        ckvn_ref[...] = ckvn
        kraw = _nn(ckvn, wk_ref[...])
        v_ref[...] = _nn(ckvn, wv_ref[...]).astype(BF16)
        lane = lax.broadcasted_iota(jnp.int32, (1, BLK), 1)
        kr = sm_ref[...]
        krr = jnp.where((lane >= SM_KR) & (lane < SM_KR + MLA_ROPE), kr * cs + _swap_rope(kr) * sn, 0.0)
        k_ref[...] = jnp.concatenate([kraw[:, BLK * h:BLK * (h + 1)] + krr for h in range(MLA_H)], axis=1).astype(BF16)

    def rows(w, cb):
        return pl.BlockSpec((tm, w), lambda i: (i, cb))

    def whole(a):
        return pl.BlockSpec(a.shape, lambda i: (0, 0))

    return pl.pallas_call(
        body, name="mla_pre", grid=(t // tm,),
        in_specs=[rows(MLA_QL, C_CQ // MLA_QL), rows(MLA_KVL, C_CKV // MLA_KVL), rows(BLK, C_SM // BLK),
                  whole(qg), whole(kvg), whole(wq), whole(wk), whole(wv), rows(BLK, 0), rows(BLK, 0)],
        out_specs=[rows(512, 0), rows(512, 0), rows(256, 0), rows(MLA_QL, 0), rows(MLA_KVL, 0)],
        out_shape=[jax.ShapeDtypeStruct((t, 512), BF16), jax.ShapeDtypeStruct((t, 512), BF16),
                   jax.ShapeDtypeStruct((t, 256), BF16), jax.ShapeDtypeStruct((t, MLA_QL), BF16),
                   jax.ShapeDtypeStruct((t, MLA_KVL), BF16)],
        compiler_params=_cp("arbitrary"),
    )(proj, proj, proj, qg, kvg, wq, wk, wv, cosq, sinq)


def mla_pre_bwd(dq, dk, dv, proj, cqn, ckvn, qg, kvg, wq, wk, wv, cosq, sinq, dsm_in):
    t = proj.shape[0]
    tm = _row_tile(t)

    def body(dq_ref, dk_ref, dv_ref, cq_ref, ckv_ref, cqn_ref, ckvn_ref, qg_ref, kvg_ref, wq_ref, wk_ref, wv_ref,
             cos_ref, sin_ref, din_ref, dcq_ref, dckv_ref, dsm_ref, dwq_ref, dwk_ref, dwv_ref, dqg_ref, dkvg_ref):
        i = pl.program_id(0)

        @pl.when(i == 0)
        def _():
            for r in (dwq_ref, dwk_ref, dwv_ref, dqg_ref, dkvg_ref):
                r[...] = jnp.zeros_like(r)

        cs, sn = cos_ref[...], sin_ref[...]
        lane = lax.broadcasted_iota(jnp.int32, (1, BLK), 1)

        def unrope(dy):
            return dy * cs + _swap_rope(dy * sn)

        dqp = jnp.concatenate([unrope(dq_ref[:, BLK * h:BLK * (h + 1)]) for h in range(MLA_H)], axis=1).astype(BF16)
        dwq_ref[...] += _tn(cqn_ref[...], dqp)
        dcqn = _nt(dqp, wq_ref[...])
        r, xh = _rms(cq_ref[...], None)
        dqg_ref[...] += jnp.sum(dcqn * xh, axis=0, keepdims=True)
        dxh = dcqn * qg_ref[...]
        dcq_ref[...] = r * (dxh - xh * jnp.mean(dxh * xh, axis=1, keepdims=True))

        dkn, dkr = [], jnp.zeros((tm, BLK), F32)
        for h in range(MLA_H):
            blk = dk_ref[:, BLK * h:BLK * (h + 1)]
            dkn.append(jnp.where(lane < MLA_NOPE, blk, 0.0))
            dkr += jnp.where((lane >= SM_KR) & (lane < SM_KR + MLA_ROPE), blk, 0.0)
        dknb = jnp.concatenate(dkn, axis=1).astype(BF16)
        dvb = dv_ref[...].astype(BF16)
        ckvn = ckvn_ref[...]
        dwk_ref[...] += _tn(ckvn, dknb)
        dwv_ref[...] += _tn(ckvn, dvb)
        dckvn = _nt(dknb, wk_ref[...]) + _nt(dvb, wv_ref[...])
        r2, kh = _rms(ckv_ref[...], None)
        dkvg_ref[...] += jnp.sum(dckvn * kh, axis=0, keepdims=True)
        dkh = dckvn * kvg_ref[...]
        dckv_ref[...] = r2 * (dkh - kh * jnp.mean(dkh * kh, axis=1, keepdims=True))
        dsm_ref[...] = din_ref[...] + jnp.where((lane >= SM_KR) & (lane < SM_KR + MLA_ROPE), unrope(dkr), 0.0)

    def rows(w, cb):
        return pl.BlockSpec((tm, w), lambda i: (i, cb))

    def whole(a):
        return pl.BlockSpec(a.shape, lambda i: (0, 0))

    def wshape(a):
        return jax.ShapeDtypeStruct(a.shape, F32)

    return pl.pallas_call(
        body, name="mla_pre_bwd", grid=(t // tm,),
        in_specs=[rows(512, 0), rows(512, 0), rows(256, 0), rows(MLA_QL, C_CQ // MLA_QL), rows(MLA_KVL, C_CKV // MLA_KVL),
                  rows(MLA_QL, 0), rows(MLA_KVL, 0), whole(qg), whole(kvg), whole(wq), whole(wk), whole(wv),
                  rows(BLK, 0), rows(BLK, 0), rows(BLK, 0)],
        out_specs=[rows(MLA_QL, 0), rows(MLA_KVL, 0), rows(BLK, 0), whole(wq), whole(wk), whole(wv), whole(qg), whole(kvg)],
        out_shape=[jax.ShapeDtypeStruct((t, MLA_QL), F32), jax.ShapeDtypeStruct((t, MLA_KVL), F32),
                   jax.ShapeDtypeStruct((t, BLK), F32), wshape(wq), wshape(wk), wshape(wv), wshape(qg), wshape(kvg)],
        compiler_params=_cp("arbitrary"),
    )(dq, dk, dv, proj, proj, cqn, ckvn, qg, kvg, wq, wk, wv, cosq, sinq, dsm_in)


def _slot_sum(me, own, recv_ref):
    gg = own.astype(F32)
    for s in range(N_DEV):
        gg = gg + jnp.where(me == s, 0.0, recv_ref[s].astype(F32))
    return gg


def adamw(w, m, v, g=None, recv=None, own=None, me_arr=None):
    shape = w.shape
    c = shape[-1]
    from_recv = recv is not None
    if not from_recv:
        me_arr = jnp.zeros((1,), jnp.int32)
    nl = len(recv) if from_recv else 1
    rws = w.size // c // nl
    tr = rws
    for d in (1024, 512, 352, 256, 128, 64, 32, 16, 8):
        if rws % d == 0 and d * c * 4 <= (2 << 20):
            tr = d
            break
    nt = rws // tr
    w2, m2, v2 = (a.reshape(nl, rws, c) for a in (w, m, v))
    if from_recv:
        gin = [a.reshape(N_DEV, rws, c) for a in list(recv) + list(own)]
    else:
        gin = [g.reshape(1, rws, c)]

    def body(me_ref, w_ref, m_ref, v_ref, *rest):
        g_refs, outs = rest[:len(gin)], rest[len(gin):]
        if from_recv:
            g_out, outs = outs[0], outs[1:]
            for li in range(nl):
                @pl.when(pl.program_id(0) == li)
                def _(li=li):
                    g_out[...] = _slot_sum(me_ref[0], g_refs[nl + li][...], g_refs[li])
            gg = g_out[...]
        else:
            gg = g_refs[0][...]
        d_ref, nm_ref, nv_ref = outs
        nm = B1 * m_ref[...] + (1.0 - B1) * gg
        nv = B2 * v_ref[...] + (1.0 - B2) * (gg * gg)
        mh = nm / (1.0 - B1 ** STEP)
        vh = nv / (1.0 - B2 ** STEP)
        d_ref[...] = -LR * (mh / (jnp.sqrt(vh) + AEPS) + WD * w_ref[...])
        nm_ref[...] = nm
        nv_ref[...] = nv

    row = pl.BlockSpec((None, tr, c), lambda l, i, me: (l, i, 0))
    if from_recv:
        gspecs = [pl.BlockSpec((N_DEV, tr, c), lambda l, i, me, li=li: (0, jnp.where(l == li, i, 0), 0))
                  for li in range(nl)]
        gspecs += [pl.BlockSpec((None, tr, c), lambda l, i, me, li=li: (me[0], jnp.where(l == li, i, 0), 0))
                   for li in range(nl)]
    else:
        gspecs = [row]
    nout = 4 if from_recv else 3
    outs = pl.pallas_call(
        body, name="adamw",
        grid_spec=pltpu.PrefetchScalarGridSpec(num_scalar_prefetch=1, grid=(nl, nt), in_specs=[row, row, row] + gspecs,
                                               out_specs=[row] * nout),
        out_shape=[jax.ShapeDtypeStruct((nl, rws, c), F32)] * nout,
        compiler_params=_cp("arbitrary", "arbitrary"),
    )(me_arr, w2, m2, v2, *gin)
    return tuple(o.reshape(shape) for o in outs)


def sum_slots(recv, own=None, me_arr=None):
    _, r, c = recv.shape
    if own is None:
        own, me_arr = recv, jnp.zeros((1,), jnp.int32)
        plain = True
    else:
        plain = False

    def body(me_ref, r_ref, own_ref, o_ref):
        if plain:
            gg = r_ref[0].astype(F32)
            for s in range(1, N_DEV):
                gg = gg + r_ref[s].astype(F32)
            o_ref[...] = gg
        else:
            o_ref[...] = _slot_sum(me_ref[0], own_ref[...], r_ref)

    return pl.pallas_call(
        body, name="sum_slots",
        grid_spec=pltpu.PrefetchScalarGridSpec(
            num_scalar_prefetch=1, grid=(1,),
            in_specs=[pl.BlockSpec((N_DEV, r, c), lambda i, me: (0, 0, 0)),
                      pl.BlockSpec((None, r, c), lambda i, me: (me[0], 0, 0))],
            out_specs=pl.BlockSpec((r, c), lambda i, me: (0, 0))),
        out_shape=jax.ShapeDtypeStruct((r, c), F32),
        compiler_params=_cp("arbitrary"),
    )(me_arr, recv, own)


_FLIPS = [(0, 0, 1), (0, 1, 0), (0, 1, 1), (1, 0, 0), (1, 0, 1), (1, 1, 0), (1, 1, 1)]
_ANY = pl.BlockSpec(memory_space=pl.ANY)


def _mesh_place():
    x, y, c = lax.axis_index("x"), lax.axis_index("y"), lax.axis_index("c")
    me = 4 * x + 2 * y + c
    peers = [((x + fx) % 2, (y + fy) % 2, (c + fc) % 2) for fx, fy, fc in _FLIPS]
    return me, peers


def place_own(src, l, dtype, me_arr):
    _, r, c = src.shape
    tr = r
    for d in (512, 352, 256, 128, 64, 32, 16, 8):
        if r % d == 0 and d * c * 4 <= (2 << 20):
            tr = d
            break

    def body(me_ref, s_ref, o_ref):
        o_ref[...] = s_ref[...].astype(dtype)

    return pl.pallas_call(
        body, name="place_own",
        grid_spec=pltpu.PrefetchScalarGridSpec(
            num_scalar_prefetch=1, grid=(r // tr,),
            in_specs=[pl.BlockSpec((None, tr, c), lambda i, me: (l, i, 0))],
            out_specs=pl.BlockSpec((None, tr, c), lambda i, me: (me[0], i, 0))),
        out_shape=jax.ShapeDtypeStruct((N_DEV, r, c), dtype),
        compiler_params=_cp("arbitrary"),
    )(me_arr, src)


_HBM = pl.BlockSpec(memory_space=pltpu.HBM)
_SEMS = pl.BlockSpec(memory_space=pltpu.SEMAPHORE)
_EFFECT = pltpu.SideEffectType.DATAFLOW_SIDE_EFFECTING


def exchange_start(mode, arrays, name, after=None):
    n = len(arrays)
    gather = mode == "gather"
    ns = 0 if gather else n
    zones = list(arrays) if gather else [lax.empty(a.shape, a.dtype) for a in arrays]
    ops = ([] if gather else list(arrays)) + zones
    extra = [] if after is None else [after]

    def body(*refs):
        srcs, lands = refs[:ns], refs[ns:ns + n]
        send_sems, recv_sems = refs[ns + n + len(extra)], refs[ns + n + len(extra) + 1]
        token = refs[-1]
        me, peers = _mesh_place()
        ids = [4 * p[0] + 2 * p[1] + p[2] for p in peers]
        for j in range(n):
            for k in range(N_DEV - 1):
                src = lands[j].at[me] if gather else srcs[j].at[ids[k]]
                pltpu.make_async_remote_copy(src_ref=src, dst_ref=lands[j].at[me],
                                             send_sem=send_sems.at[j * (N_DEV - 1) + k],
                                             recv_sem=recv_sems.at[j * (N_DEV - 1) + k], device_id=peers[k],
                                             device_id_type=pl.DeviceIdType.MESH).start()
        token[...] = jnp.zeros_like(token)

    nsem = n * (N_DEV - 1)
    res = pl.pallas_call(
        body, name=name,
        in_specs=[_HBM] * (ns + n) + [_ANY] * len(extra),
        out_specs=(_SEMS, _SEMS, *[_HBM] * (ns + n), pl.BlockSpec(memory_space=pltpu.VMEM)),
        out_shape=(pltpu.SemaphoreType.DMA((nsem,)), pltpu.SemaphoreType.DMA((nsem,)),
                   *[pltpu.HBM(a.shape, a.dtype) for a in ops], jax.ShapeDtypeStruct((8, BLK), F32)),
        input_output_aliases={i: 2 + i for i in range(ns + n)},
        compiler_params=pltpu.CompilerParams(has_side_effects=_EFFECT),
    )(*[pltpu.with_memory_space_constraint(a, pltpu.HBM) for a in ops], *extra)
    return dict(gather=gather, send=res[0], recv=res[1], srcs=list(res[2:2 + ns]), lands=list(res[2 + ns:2 + ns + n]),
                token=res[-1])


def exchange_wait(hd, idxs, name, after):
    gather = hd["gather"]
    n = len(idxs)
    ns = 0 if gather else n
    ops = ([] if gather else [hd["srcs"][j] for j in idxs]) + [hd["lands"][j] for j in idxs]

    def body(*refs):
        srcs, lands = refs[:ns], refs[ns:ns + n]
        send_sems, recv_sems = refs[ns + n], refs[ns + n + 1]
        me, peers = _mesh_place()
        ids = [4 * p[0] + 2 * p[1] + p[2] for p in peers]
        for p, j in enumerate(idxs):
            for k in range(N_DEV - 1):
                src = lands[p].at[me] if gather else srcs[p].at[ids[k]]
                cp = pltpu.make_async_remote_copy(src_ref=src, dst_ref=lands[p].at[ids[k]],
                                                  send_sem=send_sems.at[j * (N_DEV - 1) + k],
                                                  recv_sem=recv_sems.at[j * (N_DEV - 1) + k], device_id=peers[k],
                                                  device_id_type=pl.DeviceIdType.MESH)
                cp.wait_send()
                cp.wait_recv()

    res = pl.pallas_call(
        body, name=name,
        in_specs=[_HBM] * (ns + n) + [_SEMS, _SEMS, _ANY],
        out_specs=[_HBM] * (ns + n),
        out_shape=[pltpu.HBM(a.shape, a.dtype) for a in ops],
        input_output_aliases={i: i for i in range(ns + n)},
        compiler_params=pltpu.CompilerParams(has_side_effects=_EFFECT),
    )(*ops, hd["send"], hd["recv"], after)
    return list(res[:ns]), list(res[ns:])


def _chip_place():
    x, y, c = lax.axis_index("x"), lax.axis_index("y"), lax.axis_index("c")
    chips = [((x + 1) % 2, y), (x, (y + 1) % 2), ((x + 1) % 2, (y + 1) % 2)]
    ident = lambda p: 4 * p[0] + 2 * p[1] + p[2]
    return dict(me=4 * x + 2 * y + c, sib=(x, y, 1 - c), sib_id=4 * x + 2 * y + 1 - c,
                same=[(cx, cy, c) for cx, cy in chips], same_ids=[ident((cx, cy, c)) for cx, cy in chips],
                other_ids=[ident((cx, cy, 1 - c)) for cx, cy in chips])


def _remote(src, dst, send_sem, recv_sem, dev):
    return pltpu.make_async_remote_copy(src_ref=src, dst_ref=dst, send_sem=send_sem, recv_sem=recv_sem, device_id=dev,
                                        device_id_type=pl.DeviceIdType.MESH)


def gather_start(zones, name):
    n = len(zones)

    def body(*refs):
        lands, send_sems, recv_sems, token = refs[:n], refs[n], refs[n + 1], refs[-1]
        pc = _chip_place()
        for j in range(n):
            own = lands[j].at[pc["me"]]
            for k, dev in enumerate([pc["sib"]] + pc["same"]):
                _remote(own, own, send_sems.at[4 * j + k], recv_sems.at[4 * j + k], dev).start()
        token[...] = jnp.zeros_like(token)

    res = pl.pallas_call(
        body, name=name,
        in_specs=[_HBM] * n,
        out_specs=(_SEMS, _SEMS, *[_HBM] * n, pl.BlockSpec(memory_space=pltpu.VMEM)),
        out_shape=(pltpu.SemaphoreType.DMA((4 * n,)), pltpu.SemaphoreType.DMA((4 * n,)),
                   *[pltpu.HBM(a.shape, a.dtype) for a in zones], jax.ShapeDtypeStruct((8, BLK), F32)),
        input_output_aliases={i: 2 + i for i in range(n)},
        compiler_params=pltpu.CompilerParams(has_side_effects=_EFFECT),
    )(*[pltpu.with_memory_space_constraint(a, pltpu.HBM) for a in zones])
    return dict(send=res[0], recv=res[1], lands=list(res[2:2 + n]), token=res[-1])


def gather_relay(hd, idxs, name, after):
    n = len(idxs)

    def body(*refs):
        lands, send_sems, recv_sems = refs[:n], refs[n], refs[n + 1]
        fsend, frecv, token = refs[n + 3 + n], refs[n + 4 + n], refs[-1]
        pc = _chip_place()
        for p, j in enumerate(idxs):
            for k in range(3):
                _remote(lands[p].at[pc["me"]], lands[p].at[pc["same_ids"][k]], send_sems.at[4 * j + 1 + k],
                        recv_sems.at[4 * j + 1 + k], pc["same"][k]).wait_recv()
        for p in range(n):
            for k in range(3):
                got = lands[p].at[pc["same_ids"][k]]
                _remote(got, got, fsend.at[3 * p + k], frecv.at[3 * p + k], pc["sib"]).start()
        token[...] = jnp.zeros_like(token)

    ops = [hd["lands"][j] for j in idxs]
    res = pl.pallas_call(
        body, name=name,
        in_specs=[_HBM] * n + [_SEMS, _SEMS, _ANY],
        out_specs=(*[_HBM] * n, _SEMS, _SEMS, pl.BlockSpec(memory_space=pltpu.VMEM)),
        out_shape=(*[pltpu.HBM(a.shape, a.dtype) for a in ops], pltpu.SemaphoreType.DMA((3 * n,)),
                   pltpu.SemaphoreType.DMA((3 * n,)), jax.ShapeDtypeStruct((8, BLK), F32)),
        input_output_aliases={i: i for i in range(n)},
        compiler_params=pltpu.CompilerParams(has_side_effects=_EFFECT),
    )(*ops, hd["send"], hd["recv"], after)
    return dict(lands=list(res[:n]), fsend=res[n], frecv=res[n + 1], token=res[-1])


def gather_wait(hd, rl, idxs, name, after):
    n = len(idxs)

    def body(*refs):
        lands, send_sems, recv_sems, fsend, frecv = refs[:n], refs[n], refs[n + 1], refs[n + 2], refs[n + 3]
        pc = _chip_place()
        for p, j in enumerate(idxs):
            own = lands[p].at[pc["me"]]
            for k, dev in enumerate([pc["sib"]] + pc["same"]):
                _remote(own, own, send_sems.at[4 * j + k], recv_sems.at[4 * j + k], dev).wait_send()
            _remote(own, lands[p].at[pc["sib_id"]], send_sems.at[4 * j], recv_sems.at[4 * j], pc["sib"]).wait_recv()
            for k in range(3):
                cp = _remote(lands[p].at[pc["same_ids"][k]], lands[p].at[pc["other_ids"][k]], fsend.at[3 * p + k],
                             frecv.at[3 * p + k], pc["sib"])
                cp.wait_send()
                cp.wait_recv()

    res = pl.pallas_call(
        body, name=name,
        in_specs=[_HBM] * n + [_SEMS, _SEMS, _SEMS, _SEMS, _ANY],
        out_specs=[_HBM] * n,
        out_shape=[pltpu.HBM(a.shape, a.dtype) for a in rl["lands"]],
        input_output_aliases={i: i for i in range(n)},
        compiler_params=pltpu.CompilerParams(has_side_effects=_EFFECT),
    )(*rl["lands"], hd["send"], hd["recv"], rl["fsend"], rl["frecv"], after)
    return list(res)


def _pad_cols(a, n):
    return jnp.pad(a, ((0, 0),) * (a.ndim - 1) + ((0, n - a.shape[-1]),))


def w_in_to_padded(w):
    z = lambda n: jnp.zeros(w.shape[:-1] + (n,), w.dtype)
    return jnp.concatenate([
        w[..., 0:1280], w[..., 1288:2056], w[..., 2060:2316], w[..., 2316:2444],
        w[..., 1280:1288], w[..., 2056:2060], z(SM_KR - SM_F - FOX_H), w[..., 2444:2476], z(BLK - SM_KR - MLA_ROPE)], axis=-1)


def w_in_from_padded(g):
    s = C_SM
    return jnp.concatenate([
        g[..., 0:1280], g[..., s + SM_DT:s + SM_DT + 8], g[..., 1280:2048], g[..., s + SM_F:s + SM_F + 4],
        g[..., 2048:2304], g[..., 2304:2432], g[..., s + SM_KR:s + SM_KR + MLA_ROPE]], axis=-1)


def _unshard_cols(gth):
    n, r, c = gth.shape
    return jnp.transpose(gth, (1, 0, 2)).reshape(r, n * c)


def _shard_cols(full):
    r, nc = full.shape
    return jnp.transpose(full.reshape(r, N_DEV, nc // N_DEV), (1, 0, 2))


def mla_weights(uq_g, ukv_g):
    uq = _unshard_cols(uq_g)
    dqh = MLA_NOPE + MLA_ROPE
    wq = jnp.concatenate([_pad_cols(uq[:, dqh * h:dqh * (h + 1)], BLK) for h in range(MLA_H)], axis=1)
    wk = jnp.concatenate([_pad_cols(ukv_g[2 * h], BLK) for h in range(MLA_H)], axis=1)
    wv = jnp.concatenate([ukv_g[2 * h + 1] for h in range(MLA_H)], axis=1)
    return wq, wk, wv


def mla_weight_grads(dwq, dwk, dwv):
    dqh = MLA_NOPE + MLA_ROPE
    duq = _shard_cols(jnp.concatenate([dwq[:, BLK * h:BLK * h + dqh] for h in range(MLA_H)], axis=1))
    parts = []
    for h in range(MLA_H):
        parts += [dwk[:, BLK * h:BLK * h + MLA_NOPE], dwv[:, MLA_V * h:MLA_V * (h + 1)]]
    return duq, jnp.stack(parts, axis=0)


def rope_tables(t):
    pos = (jnp.arange(t, dtype=jnp.int32) - PAD).astype(F32)
    inv_freq = 1.0 / (10000.0 ** (jnp.arange(0, MLA_ROPE, 2, dtype=F32) / MLA_ROPE))
    ang = pos[:, None] * inv_freq[None, :]
    cos, sin = jnp.cos(ang), jnp.sin(ang)
    one, zero = jnp.ones((t, SM_KR), F32), jnp.zeros((t, SM_KR), F32)
    tail = BLK - SM_KR - MLA_ROPE
    cosq = jnp.concatenate([one, cos, cos, jnp.ones((t, tail), F32)], axis=1)
    sinq = jnp.concatenate([zero, -sin, sin, jnp.zeros((t, tail), F32)], axis=1)
    return cosq, sinq


def _lanes(v, off=0):
    return jnp.pad(v.astype(F32), (off, BLK - off - v.shape[0]))[None, :]


def layer_fwd(x, ln, hb, getw, tabs, ahead):
    sv = {"h0b": hb}
    def behind(vec, tok):
        return vec if tok is None else vec + 0.0 * tok[0:1, 0:1]

    W = dict(getw("ffn1", hb))
    ln1 = (behind(W["ln1_g"], ahead(0, "mix", hb, 1)), W["ln1_b"])
    u, v, r1, h1b = ffn_fwd_seq(x, ln, W["g1"], W["u1"], W["d1"], ln1)
    sv.update(u1=u, v1=v, r1=r1, h1b=h1b)
    W.update(getw("mix", h1b))
    ln2 = (W["ln2_g"], W["ln2_b"])
    proj = mm_nn(h1b, W["w_in"])
    xa = conv_fwd(proj, W["conv_w"], W["conv_b"])
    y_ssd, sprev = ssd_fwd(xa, proj, W["dtb"], W["alog"], W["dskip"], W["normg"])
    c_col, c_row = fox_pre(proj, W["fb"])
    y_fox, lse_f = attn_fwd(proj, proj, proj, C_FQ // 256, C_FK // 256, C_FV // 256, FOX_H, FOX_DH, FOX_DH,
                            FOX_DH ** -0.5, c_col, c_row, SM_F)
    q, k, vv, cqn, ckvn = mla_pre(proj, behind(W["qg"], ahead(0, "ffn2", y_fox)), W["kvg"], W["wq"], W["wk"], W["wv"], *tabs)
    y_mla, lse_m = attn_fwd(q, k, vv, 0, 0, 0, MLA_H, BLK, MLA_V, (MLA_NOPE + MLA_ROPE) ** -0.5)
    r2, h2b = mm_res_ln([y_ssd, y_fox, y_mla], W["w_out"], r1, ln1, ln2)
    sv.update(proj=proj, xa=xa, sprev=sprev, c_col=c_col, c_row=c_row, lse_f=lse_f, q=q, k=k, v=vv, cqn=cqn, ckvn=ckvn,
              lse_m=lse_m, y_ssd=y_ssd, y_fox=y_fox, y_mla=y_mla, r2=r2, h2b=h2b)
    W.update(getw("ffn2", h2b))
    ln3 = (behind(W["ln3_g"], ahead(1, "ffn1", h2b)), W["ln3_b"])
    u, v, r3, h3b = ffn_fwd_seq(r2, ln2, W["g2"], W["u2"], W["d2"], ln3)
    sv.update(u2=u, v2=v, r3=r3, W=W)
    return r3, ln3, h3b, sv


def ffn_bwd(parts, r, gamma, hb_in, u, v, wg, wu, wd, after=None):
    dh, dwg, dwu, dwd, dg, db = ffn_bwd_seq(parts, r, gamma, hb_in, u, v, wg, wu, wd, after)
    return dh, dict(d=dwd, g=dwg, u=dwu, ln_g=dg, ln_b=db)


def layer_bwd(parts, sv, emit, tabs, after):
    G = {}
    W = sv["W"]
    dh2, g2 = ffn_bwd(parts, sv["r3"], W["ln3_g"], sv["h2b"], sv["u2"], sv["v2"], W["g2"], W["u2"], W["d2"], after)
    G.update(g2=g2["g"], u2=g2["u"], d2=g2["d"], ln3_g=g2["ln_g"], ln3_b=g2["ln_b"])
    tok = emit("ffn2", G)
    dr2, dmc, G["w_out"], G["ln2_g"], G["ln2_b"] = oproj_bwd(dh2, sv["r2"], W["ln2_g"], [sv["y_ssd"], sv["y_fox"], sv["y_mla"]], W["w_out"], tok)
    proj = sv["proj"]
    dxa, dz, dsm, G["normg"], G["dskip"], G["alog"], G["dtb"] = ssd_bwd(
        dmc, sv["xa"], proj, sv["sprev"], W["dtb"], W["alog"], W["dskip"], W["normg"])
    dxbc, G["conv_w"], G["conv_b"] = conv_bwd(dxa, proj, W["conv_w"], W["conv_b"])
    dfq, dfk, dfv, dcq, dck = attn_bwd(proj, proj, proj, dmc, sv["lse_f"], sv["y_fox"], C_FQ // 256, C_FK // 256,
                                       C_FV // 256, 2, 0, FOX_H, FOX_DH, FOX_DH, FOX_DH ** -0.5, sv["c_col"], sv["c_row"], SM_F)
    dsm, G["fb"] = fox_pre_bwd(dcq, dck, proj, W["fb"], dsm)
    dq, dk, dv = attn_bwd(sv["q"], sv["k"], sv["v"], dmc, sv["lse_m"], sv["y_mla"], 0, 0, 0, 3, 0, MLA_H, BLK, MLA_V,
                          (MLA_NOPE + MLA_ROPE) ** -0.5)
    dcql, dckv, dsm, G["wq"], G["wk"], G["wv"], G["qg"], G["kvg"] = mla_pre_bwd(
        dq, dk, dv, proj, sv["cqn"], sv["ckvn"], W["qg"], W["kvg"], W["wq"], W["wk"], W["wv"], *tabs, dsm)
    dh1p, G["w_in"] = proj_bwd([dz, dxbc, dfq, dfk, dfv, dcql, dckv, dsm], sv["h1b"], W["w_in"])
    tok = emit("mix", G)
    dh0, g1 = ffn_bwd([(dr2, ALPHA), (dh1p, 1.0)], sv["r1"], W["ln1_g"], sv["h0b"], sv["u1"], sv["v1"],
                      W["g1"], W["u1"], W["d1"], tok)
    G.update(g1=g1["g"], u1=g1["u"], d1=g1["d"], ln1_g=g1["ln_g"], ln1_b=g1["ln_b"])
    tok = emit("ffn1", G)
    return [(dh0, 1.0)], G, tok


def local_step(x, target, meta_full, getw, emit, ahead=lambda l, stage, after, min_layer=0: None):
    t = x.shape[0] + BLK
    tabs = rope_tables(t)
    xr, hb = build_h0(meta_full, x)
    ln = None
    saved = []
    for l in range(NL):
        xr, ln, hb, sv = layer_fwd(xr, ln, hb, functools.partial(getw, l), tabs,
                                   lambda dl, stage, after, min_layer=0, l=l: ahead(l + dl, stage, after, min_layer))
        saved.append(sv)
    dy, loss = loss_head(xr, ln, target)
    parts = [(dy, 1.0)]
    grads = [None] * NL
    tok = None
    for l in range(NL - 1, -1, -1):
        parts, grads[l], tok = layer_bwd(parts, saved[l], functools.partial(emit, l), tabs, tok)
    gx, gmeta = split_dh0(parts[0][0], tok)
    return loss, gx, gmeta, grads


_SMALL = ["ln1_g", "ln1_b", "ln2_g", "ln2_b", "ln3_g", "ln3_b", "conv_b", "ssd_norm_g", "mla_q_norm_g",
          "mla_kv_norm_g", "dt_bias", "a_log", "d_skip", "fox_f_b"]
_SMALL_ROWS = 8
_NAMES = ["meta", "ffn1_w_gate", "ffn1_w_up", "ffn1_w_down", "ln1_g", "ln1_b", "w_in", "conv_w", "conv_b", "dt_bias",
          "a_log", "d_skip", "ssd_norm_g", "fox_f_b", "mla_q_norm_g", "mla_w_uq", "mla_kv_norm_g", "mla_w_ukv", "w_out",
          "ln2_g", "ln2_b", "ffn2_w_gate", "ffn2_w_up", "ffn2_w_down", "ln3_g", "ln3_b"]


def pack_small(p):
    flat = jnp.concatenate([p[n].astype(F32) for n in _SMALL], axis=1)
    return _pad_cols(flat, _SMALL_ROWS * D).reshape(NL * _SMALL_ROWS, D)


def unpack_small(a, like):
    flat = a.reshape(NL, _SMALL_ROWS * D)
    out, at = {}, 0
    for n in _SMALL:
        out[n] = flat[:, at:at + like[n].shape[1]]
        at += like[n].shape[1]
    return out


_STAGES = {"ffn1": ["ffn1_w_gate", "ffn1_w_up", "ffn1_w_down"],
           "mix": ["w_in", "conv_w", "mla_w_uq", "mla_w_ukv", "w_out"],
           "ffn2": ["ffn2_w_gate", "ffn2_w_up", "ffn2_w_down"]}


_FFN_T = ("ffn1_w_gate", "ffn1_w_up", "ffn2_w_gate", "ffn2_w_up")


def stage_weights(l, stage, g, rep):
    if stage != "mix":
        i = stage[3]
        return {"g" + i: g[f"ffn{i}_w_gate"].reshape(D_FF, D), "u" + i: g[f"ffn{i}_w_up"].reshape(D_FF, D),
                "d" + i: g[f"ffn{i}_w_down"].reshape(D_FF, D),
                "ln1_g" if i == "1" else "ln3_g": rep["ln1_g" if i == "1" else "ln3_g"][l][None, :],
                "ln1_b" if i == "1" else "ln3_b": rep["ln1_b" if i == "1" else "ln3_b"][l][None, :]}
    W = {}
    W["w_in"] = g["w_in"].reshape(D, N_INP)
    W["w_out"] = g["w_out"].reshape(D, D)
    W["wq"], W["wk"], W["wv"] = mla_weights(g["mla_w_uq"], g["mla_w_ukv"])
    W["conv_w"] = _unshard_cols(g["conv_w"])
    for k in ("ln2_g", "ln2_b", "conv_b"):
        W[k] = rep[k][l][None, :]
    W["normg"] = rep["ssd_norm_g"][l][None, :]
    W["qg"] = rep["mla_q_norm_g"][l][None, :]
    W["kvg"] = rep["mla_kv_norm_g"][l][None, :]
    W["dtb"] = _lanes(rep["dt_bias"][l], SM_DT)
    W["alog"] = _lanes(rep["a_log"][l], SM_DT)
    W["dskip"] = _lanes(rep["d_skip"][l], SM_DT)
    W["fb"] = _lanes(rep["fox_f_b"][l], SM_F)
    return W


def small_grads(G):
    return {"ln1_g": G["ln1_g"][0], "ln1_b": G["ln1_b"][0], "ln2_g": G["ln2_g"][0], "ln2_b": G["ln2_b"][0],
            "ln3_g": G["ln3_g"][0], "ln3_b": G["ln3_b"][0], "conv_b": G["conv_b"][0], "ssd_norm_g": G["normg"][0],
            "mla_q_norm_g": G["qg"][0], "mla_kv_norm_g": G["kvg"][0], "dt_bias": G["dtb"][0, :SSD_H],
            "a_log": G["alog"][0, :SSD_H], "d_skip": G["dskip"][0, :SSD_H], "fox_f_b": G["fb"][0, SM_F:SM_F + FOX_H]}


def big_grads(G, stage):
    if stage != "mix":
        i = stage[-1]
        return {f"ffn{i}_w_{k}": G[k[0] + i].reshape(N_DEV, HS, D) for k in ("gate", "up", "down")}
    duq, dukv = mla_weight_grads(G["wq"], G["wk"], G["wv"])
    return {"w_in": G["w_in"].reshape(N_DEV, D // N_DEV, N_INP), "w_out": G["w_out"].reshape(N_DEV, D // N_DEV, D),
            "mla_w_uq": duq, "mla_w_ukv": dukv, "conv_w": _shard_cols(G["conv_w"])}


def kernel(x, meta, ffn1_w_gate, ffn1_w_up, ffn1_w_down, ln1_g, ln1_b, w_in, conv_w, conv_b, dt_bias, a_log, d_skip, ssd_norm_g, fox_f_b, mla_q_norm_g, mla_w_uq, mla_kv_norm_g, mla_w_ukv, w_out, ln2_g, ln2_b, ffn2_w_gate, ffn2_w_up, ffn2_w_down, ln3_g, ln3_b, loss_target, m_meta, m_ffn1_w_gate, m_ffn1_w_up, m_ffn1_w_down, m_ln1_g, m_ln1_b, m_w_in, m_conv_w, m_conv_b, m_dt_bias, m_a_log, m_d_skip, m_ssd_norm_g, m_fox_f_b, m_mla_q_norm_g, m_mla_w_uq, m_mla_kv_norm_g, m_mla_w_ukv, m_w_out, m_ln2_g, m_ln2_b, m_ffn2_w_gate, m_ffn2_w_up, m_ffn2_w_down, m_ln3_g, m_ln3_b, v_meta, v_ffn1_w_gate, v_ffn1_w_up, v_ffn1_w_down, v_ln1_g, v_ln1_b, v_w_in, v_conv_w, v_conv_b, v_dt_bias, v_a_log, v_d_skip, v_ssd_norm_g, v_fox_f_b, v_mla_q_norm_g, v_mla_w_uq, v_mla_kv_norm_g, v_mla_w_ukv, v_w_out, v_ln2_g, v_ln2_b, v_ffn2_w_gate, v_ffn2_w_up, v_ffn2_w_down, v_ln3_g, v_ln3_b):
    vals = (meta, ffn1_w_gate, ffn1_w_up, ffn1_w_down, ln1_g, ln1_b, w_in, conv_w, conv_b, dt_bias, a_log, d_skip, ssd_norm_g, fox_f_b, mla_q_norm_g, mla_w_uq, mla_kv_norm_g, mla_w_ukv, w_out, ln2_g, ln2_b, ffn2_w_gate, ffn2_w_up, ffn2_w_down, ln3_g, ln3_b)
    moms = (m_meta, m_ffn1_w_gate, m_ffn1_w_up, m_ffn1_w_down, m_ln1_g, m_ln1_b, m_w_in, m_conv_w, m_conv_b, m_dt_bias, m_a_log, m_d_skip, m_ssd_norm_g, m_fox_f_b, m_mla_q_norm_g, m_mla_w_uq, m_mla_kv_norm_g, m_mla_w_ukv, m_w_out, m_ln2_g, m_ln2_b, m_ffn2_w_gate, m_ffn2_w_up, m_ffn2_w_down, m_ln3_g, m_ln3_b)
    vars_ = (v_meta, v_ffn1_w_gate, v_ffn1_w_up, v_ffn1_w_down, v_ln1_g, v_ln1_b, v_w_in, v_conv_w, v_conv_b, v_dt_bias, v_a_log, v_d_skip, v_ssd_norm_g, v_fox_f_b, v_mla_q_norm_g, v_mla_w_uq, v_mla_kv_norm_g, v_mla_w_ukv, v_w_out, v_ln2_g, v_ln2_b, v_ffn2_w_gate, v_ffn2_w_up, v_ffn2_w_down, v_ln3_g, v_ln3_b)
    P = dict(zip(_NAMES, vals))
    M = dict(zip(_NAMES, moms))
    V = dict(zip(_NAMES, vars_))
    me = 4 * lax.axis_index("x") + 2 * lax.axis_index("y") + lax.axis_index("c")

    me_arr = me.astype(jnp.int32).reshape(1)
    for n in _FFN_T:
        P[n], M[n], V[n] = (jnp.swapaxes(a[n], 1, 2) for a in (P, M, V))
    src = dict(P)
    src["w_in"] = w_in_to_padded(P["w_in"])
    order = [("meta", 0)] + [(n, l) for l in range(NL) for names in _STAGES.values() for n in names]
    nfirst = 1 + len(_STAGES["ffn1"])

    def place(n, l):
        return place_own(P["meta"][None] if n == "meta" else src[n], l, F32 if n in ("meta", "conv_w") else BF16, me_arr)

    hg_first = gather_start([place(n, l) for n, l in order[:nfirst]], "gather_start_first")
    hg_rest = gather_start([place(n, l) for n, l in order[nfirst:]], "gather_start_rest")
    zone_of = {nl_: ((hg_first, i) if i < nfirst else (hg_rest, i - nfirst)) for i, nl_ in enumerate(order)}
    relays = {}

    def ahead(l, stage, after, min_layer=0):
        if not min_layer <= l < NL:
            return None
        if (l, stage) not in relays:
            zs = [zone_of[("meta", 0)]] if stage == "meta" else [zone_of[(n, l)] for n in _STAGES[stage]]
            hg, idxs = zs[0][0], [i for _, i in zs]
            relays[(l, stage)] = (hg, idxs, gather_relay(hg, idxs, f"gather_relay_{l}_{stage}", after))
        return relays[(l, stage)][2]["token"]

    def arrived(l, stage, after):
        ahead(l, stage, after)
        hg, idxs, rl = relays[(l, stage)]
        return gather_wait(hg, rl, idxs, f"gather_wait_{l}_{stage}", after)

    meta_full = _unshard_cols(arrived(0, "meta", hg_rest["token"])[0])

    def getw(l, stage, after):
        return stage_weights(l, stage, dict(zip(_STAGES[stage], arrived(l, stage, after))), P)

    sent = {}

    def emit(l, stage, G):
        bg = big_grads(G, stage)
        sent[(l, stage)] = exchange_start("scatter", [bg[n] for n in _STAGES[stage]], f"scatter_start_{l}_{stage}")
        return sent[(l, stage)]["token"]

    loss, gx, gmeta, grads = local_step(x[0], loss_target[0], meta_full, getw, emit, ahead)

    small = jnp.concatenate([pack_small({n: jnp.stack([small_grads(g)[n] for g in grads]) for n in _SMALL}), gmeta,
                             jnp.pad(loss, ((0, 7), (0, D - 1)))], axis=0)
    hs = exchange_start("gather", [place_own(small[None], 0, F32, me_arr)], "small_start")

    out = {}
    after = hs["token"]
    for stage in ("ffn2", "mix", "ffn1"):
        names = _STAGES[stage]
        whole = [l for l in range(NL - 1, -1, -1) if (l, stage) != (0, "ffn1")]
        got = {l: exchange_wait(sent[(l, stage)], list(range(len(names))), f"scatter_wait_{l}_{stage}", after) for l in whole}
        for i, n in enumerate(names):
            one = {l: (got[l][0][i], got[l][1][i]) for l in whole}
            for l in set(range(NL)) - set(whole):
                s_, r_ = exchange_wait(sent[(l, stage)], [i], f"scatter_wait_{l}_{stage}_{i}", after)
                one[l] = (s_[0], r_[0])
            own = [one[l][0] for l in range(NL)]
            recv = [one[l][1] for l in range(NL)]
            if n == "w_in":
                g = jnp.stack([w_in_from_padded(sum_slots(recv[l], own[l], me_arr)) for l in range(NL)])
                out[n] = (g,) + adamw(P[n], M[n], V[n], g=g)
            else:
                out[n] = adamw(P[n], M[n], V[n], recv=recv, own=own, me_arr=me_arr)
                if n in _FFN_T:
                    out[n] = tuple(jnp.swapaxes(a, 1, 2) for a in out[n])
            after = out[n][1]
    gsmall = sum_slots(exchange_wait(hs, [0], "small_wait", after)[1][0])
    gm = lax.dynamic_slice(gsmall[NL * _SMALL_ROWS:], (0, me * (D // N_DEV)), (N_META, D // N_DEV))
    out["meta"] = (gm,) + adamw(P["meta"], M["meta"], V["meta"], g=gm)
    gs = gsmall[:NL * _SMALL_ROWS]
    sd, sm_, sv_ = adamw(pack_small(P), pack_small(M), pack_small(V), g=gs)
    ups = [unpack_small(a, P) for a in (gs, sd, sm_, sv_)]
    for n in _SMALL:
        out[n] = tuple(u[n] for u in ups)

    loss_all = gsmall[NL * _SMALL_ROWS + N_META, 0]
    flat = [loss_all, gx[None]]
    for k in range(4):
        flat += [out[n][k] for n in _NAMES]
    return tuple(flat)
```
